```python
import jax, jax.numpy as jnp
from jax import lax
import numpy as np

D_MODEL = 1024
BATCH = 16
SEQ = 2048
DEPTH = 1

N_META = 16
D_MIX = D_MODEL
MLA_HEADS = 4
QK_NOPE_DIM = 128
QK_ROPE_DIM = 64
V_HEAD_DIM = 128
MLA_WIDTH = MLA_HEADS * V_HEAD_DIM
Q_LORA_RANK = 256
KV_LORA_RANK = 128
ROPE_THETA = 10000.0
ATTN_SCALE = (QK_NOPE_DIM + QK_ROPE_DIM) ** -0.5
Q_BLOCK = 128
NEG_INF = -1e30
CONV_WIDTH = D_MIX - MLA_WIDTH
CONV_GROUPS = 8
CONV_KSIZE = 3
IN_SPLITS = (Q_LORA_RANK, KV_LORA_RANK, QK_ROPE_DIM, MLA_WIDTH,
             CONV_WIDTH, CONV_WIDTH, CONV_WIDTH, CONV_WIDTH)
IN_PROJ_DIM = sum(IN_SPLITS)
EPS = 1e-6

kernel_name = "hymba_mla_shortconv_hybrid"


def rms_norm(x, g):
    xf = x.astype(jnp.float32)
    y = xf * lax.rsqrt(jnp.mean(xf * xf, axis=-1, keepdims=True) + EPS)
    return (y * g.astype(jnp.float32)).astype(x.dtype)


def apply_rope(x, pos):
    half = x.shape[-1] // 2
    inv_freq = 1.0 / (ROPE_THETA ** (jnp.arange(half, dtype=jnp.float32) / half))
    ang = pos.astype(jnp.float32)[:, None] * inv_freq[None, :]
    cos = jnp.cos(ang)[None, :, None, :]
    sin = jnp.sin(ang)[None, :, None, :]
    xf = x.astype(jnp.float32)
    x1, x2 = xf[..., :half], xf[..., half:]
    return jnp.concatenate([x1 * cos - x2 * sin, x2 * cos + x1 * sin], axis=-1).astype(x.dtype)


def _attend_block(q_blk, q_pos, k, v, k_pos):
    s = jnp.einsum('bqhd,bkhd->bhqk', q_blk, k, preferred_element_type=jnp.float32) * ATTN_SCALE
    mask = k_pos[None, :] <= q_pos[:, None]
    s = jnp.where(mask[None, None], s, NEG_INF)
    p = jax.nn.softmax(s, axis=-1)
    return jnp.einsum('bhqk,bkhd->bqhd', p.astype(v.dtype), v)


def causal_attention(q, k, v):
    B, T, H, _ = q.shape
    pos = jnp.arange(T)
    o_meta = _attend_block(q[:, :N_META], pos[:N_META], k[:, :N_META], v[:, :N_META], pos[:N_META])
    n_blk = (T - N_META) // Q_BLOCK
    q_real = q[:, N_META:].reshape(B, n_blk, Q_BLOCK, H, q.shape[-1]).transpose(1, 0, 2, 3, 4)
    q_pos = pos[N_META:].reshape(n_blk, Q_BLOCK)
    o_real = lax.map(lambda a: _attend_block(a[0], a[1], k, v, pos), (q_real, q_pos))
    o_real = o_real.transpose(1, 0, 2, 3, 4).reshape(B, T - N_META, H, v.shape[-1])
    return jnp.concatenate([o_meta, o_real], axis=1)


def causal_dwconv(u, w):
    C = u.shape[-1]
    return lax.conv_general_dilated(
        u, w[:, None, :].astype(u.dtype), window_strides=(1,), padding=[(CONV_KSIZE - 1, 0)],
        dimension_numbers=('NWC', 'WIO', 'NWC'), feature_group_count=C)


def hybrid_layer(h, norm_g, w_in, q_norm_g, w_q_up, kv_norm_g, w_kv_up, conv_w,
                 attn_out_g, conv_out_g, w_out):
    B, T, _ = h.shape
    pos = jnp.arange(T)
    u = rms_norm(h, norm_g)
    p = u @ w_in
    idx = np.cumsum(IN_SPLITS)[:-1].tolist()
    c_q, c_kv, k_rope, z_attn, conv_b, conv_c, conv_h, z_conv = jnp.split(p, idx, axis=-1)

    q = (rms_norm(c_q, q_norm_g) @ w_q_up).reshape(B, T, MLA_HEADS, QK_NOPE_DIM + QK_ROPE_DIM)
    q_nope, q_pe = q[..., :QK_NOPE_DIM], q[..., QK_NOPE_DIM:]
    q_pe = apply_rope(q_pe, pos)
    kv = (rms_norm(c_kv, kv_norm_g) @ w_kv_up).reshape(B, T, MLA_HEADS, QK_NOPE_DIM + V_HEAD_DIM)
    k_nope, v = kv[..., :QK_NOPE_DIM], kv[..., QK_NOPE_DIM:]
    k_pe = apply_rope(k_rope[:, :, None, :], pos)
    k_pe = jnp.broadcast_to(k_pe, (B, T, MLA_HEADS, QK_ROPE_DIM))
    q_full = jnp.concatenate([q_nope, q_pe], axis=-1)
    k_full = jnp.concatenate([k_nope, k_pe], axis=-1)
    o = causal_attention(q_full, k_full, v)
    o = rms_norm(o, attn_out_g.reshape(MLA_HEADS, V_HEAD_DIM)).reshape(B, T, MLA_WIDTH)
    y_attn = o * jax.nn.silu(z_attn)

    yc = conv_b * causal_dwconv(conv_c * conv_h, conv_w)
    yc = rms_norm(yc.reshape(B, T, CONV_GROUPS, CONV_WIDTH // CONV_GROUPS),
                  conv_out_g.reshape(CONV_GROUPS, CONV_WIDTH // CONV_GROUPS)).reshape(B, T, CONV_WIDTH)
    y_conv = yc * jax.nn.silu(z_conv)

    mix = jnp.concatenate([y_attn, y_conv], axis=-1) @ w_out
    return h + mix


def _fwd_setup_inputs(seed: int = 0) -> dict:
    key = jax.random.key(seed)
    ks = jax.random.split(key, 16)
    f32 = jnp.float32

    def w(k, shape, fan_in):
        return jax.random.normal(k, shape, f32) * fan_in ** -0.5

    def gain(k, shape):
        return 1.0 + 0.02 * jax.random.normal(k, shape, f32)

    return {
        "x": jax.random.normal(ks[0], (BATCH, SEQ, D_MODEL), f32),
        "meta_tokens": jax.random.normal(ks[1], (N_META, D_MODEL), f32),
        "norm_g": gain(ks[2], (DEPTH, D_MODEL)),
        "w_in": w(ks[3], (DEPTH, D_MODEL, IN_PROJ_DIM), D_MODEL),
        "q_norm_g": gain(ks[4], (DEPTH, Q_LORA_RANK)),
        "w_q_up": w(ks[5], (DEPTH, Q_LORA_RANK, MLA_HEADS * (QK_NOPE_DIM + QK_ROPE_DIM)), Q_LORA_RANK),
        "kv_norm_g": gain(ks[6], (DEPTH, KV_LORA_RANK)),
        "w_kv_up": w(ks[7], (DEPTH, KV_LORA_RANK, MLA_HEADS * (QK_NOPE_DIM + V_HEAD_DIM)), KV_LORA_RANK),
        "conv_w": w(ks[8], (DEPTH, CONV_KSIZE, CONV_WIDTH), CONV_KSIZE),
        "attn_out_g": gain(ks[9], (DEPTH, MLA_WIDTH)),
        "conv_out_g": gain(ks[10], (DEPTH, CONV_WIDTH)),
        "w_out": w(ks[11], (DEPTH, D_MIX, D_MODEL), D_MIX),
        "final_norm_g": gain(ks[12], (D_MODEL,)),
    }


def _fwd_reference(x, meta_tokens, norm_g, w_in, q_norm_g, w_q_up, kv_norm_g, w_kv_up, conv_w,
              attn_out_g, conv_out_g, w_out, final_norm_g):
    B = x.shape[0]
    meta = jnp.broadcast_to(meta_tokens[None].astype(x.dtype), (B, N_META, x.shape[-1]))
    h = jnp.concatenate([meta, x], axis=1)
    for l in range(DEPTH):
        h = hybrid_layer(h, norm_g[l], w_in[l], q_norm_g[l], w_q_up[l], kv_norm_g[l], w_kv_up[l],
                         conv_w[l], attn_out_g[l], conv_out_g[l], w_out[l])
    return rms_norm(h, final_norm_g)[:, N_META:]


import jax as _jax
import jax.numpy as _jnp

TWIN_FORMAT = 'train_step'
FWD_PARAMS = ['x', 'meta_tokens', 'norm_g', 'w_in', 'q_norm_g', 'w_q_up', 'kv_norm_g', 'w_kv_up', 'conv_w', 'attn_out_g', 'conv_out_g', 'w_out', 'final_norm_g']
TWIN_WEIGHTS = ['meta_tokens', 'norm_g', 'w_in', 'q_norm_g', 'w_q_up', 'kv_norm_g', 'w_kv_up', 'conv_w', 'attn_out_g', 'conv_out_g', 'w_out', 'final_norm_g']
TWIN_DIFF_INPUT = 'x'
TWIN_INPUTS = ['x', 'meta_tokens', 'norm_g', 'w_in', 'q_norm_g', 'w_q_up', 'kv_norm_g', 'w_kv_up', 'conv_w', 'attn_out_g', 'conv_out_g', 'w_out', 'final_norm_g', 'loss_target', 'm_meta_tokens', 'm_norm_g', 'm_w_in', 'm_q_norm_g', 'm_w_q_up', 'm_kv_norm_g', 'm_w_kv_up', 'm_conv_w', 'm_attn_out_g', 'm_conv_out_g', 'm_w_out', 'm_final_norm_g', 'v_meta_tokens', 'v_norm_g', 'v_w_in', 'v_q_norm_g', 'v_w_q_up', 'v_kv_norm_g', 'v_w_kv_up', 'v_conv_w', 'v_attn_out_g', 'v_conv_out_g', 'v_w_out', 'v_final_norm_g']
TWIN_OUTPUTS = ['loss', 'grad_x', 'grad_meta_tokens', 'grad_norm_g', 'grad_w_in', 'grad_q_norm_g', 'grad_w_q_up', 'grad_kv_norm_g', 'grad_w_kv_up', 'grad_conv_w', 'grad_attn_out_g', 'grad_conv_out_g', 'grad_w_out', 'grad_final_norm_g', 'delta_meta_tokens', 'delta_norm_g', 'delta_w_in', 'delta_q_norm_g', 'delta_w_q_up', 'delta_kv_norm_g', 'delta_w_kv_up', 'delta_conv_w', 'delta_attn_out_g', 'delta_conv_out_g', 'delta_w_out', 'delta_final_norm_g', 'new_m_meta_tokens', 'new_m_norm_g', 'new_m_w_in', 'new_m_q_norm_g', 'new_m_w_q_up', 'new_m_kv_norm_g', 'new_m_w_kv_up', 'new_m_conv_w', 'new_m_attn_out_g', 'new_m_conv_out_g', 'new_m_w_out', 'new_m_final_norm_g', 'new_v_meta_tokens', 'new_v_norm_g', 'new_v_w_in', 'new_v_q_norm_g', 'new_v_w_q_up', 'new_v_kv_norm_g', 'new_v_w_kv_up', 'new_v_conv_w', 'new_v_attn_out_g', 'new_v_conv_out_g', 'new_v_w_out', 'new_v_final_norm_g']
TWIN_LEAF_KINDS = {'loss': 'loss', 'grad_x': 'grad_x', 'grad_meta_tokens': 'grad_w', 'grad_norm_g': 'grad_w', 'grad_w_in': 'grad_w', 'grad_q_norm_g': 'grad_w', 'grad_w_q_up': 'grad_w', 'grad_kv_norm_g': 'grad_w', 'grad_w_kv_up': 'grad_w', 'grad_conv_w': 'grad_w', 'grad_attn_out_g': 'grad_w', 'grad_conv_out_g': 'grad_w', 'grad_w_out': 'grad_w', 'grad_final_norm_g': 'grad_w', 'delta_meta_tokens': 'delta_w', 'delta_norm_g': 'delta_w', 'delta_w_in': 'delta_w', 'delta_q_norm_g': 'delta_w', 'delta_w_q_up': 'delta_w', 'delta_kv_norm_g': 'delta_w', 'delta_w_kv_up': 'delta_w', 'delta_conv_w': 'delta_w', 'delta_attn_out_g': 'delta_w', 'delta_conv_out_g': 'delta_w', 'delta_w_out': 'delta_w', 'delta_final_norm_g': 'delta_w', 'new_m_meta_tokens': 'new_m', 'new_m_norm_g': 'new_m', 'new_m_w_in': 'new_m', 'new_m_q_norm_g': 'new_m', 'new_m_w_q_up': 'new_m', 'new_m_kv_norm_g': 'new_m', 'new_m_w_kv_up': 'new_m', 'new_m_conv_w': 'new_m', 'new_m_attn_out_g': 'new_m', 'new_m_conv_out_g': 'new_m', 'new_m_w_out': 'new_m', 'new_m_final_norm_g': 'new_m', 'new_v_meta_tokens': 'new_v', 'new_v_norm_g': 'new_v', 'new_v_w_in': 'new_v', 'new_v_q_norm_g': 'new_v', 'new_v_w_q_up': 'new_v', 'new_v_kv_norm_g': 'new_v', 'new_v_w_kv_up': 'new_v', 'new_v_conv_w': 'new_v', 'new_v_attn_out_g': 'new_v', 'new_v_conv_out_g': 'new_v', 'new_v_w_out': 'new_v', 'new_v_final_norm_g': 'new_v'}


def _forward(args):
    return _fwd_reference(*[args[k] for k in FWD_PARAMS])


def _output_shape():
    out = _jax.eval_shape(lambda: _forward(_fwd_setup_inputs(0)))
    return out.shape, out.dtype

N_MICROBATCH = 1
ADAM_LR = 0.001
ADAM_B1 = 0.9
ADAM_B2 = 0.999
ADAM_EPS = 1e-08
ADAM_WD = 0.01
ADAM_STEP = 10
PER_EXAMPLE_BATCH_AXIS = {'x': 0, 'loss_target': 0}
SHARED_INPUTS = []
_WEIGHT_DTYPES = {'meta_tokens': _jnp.float32, 'norm_g': _jnp.float32, 'w_in': _jnp.float32, 'q_norm_g': _jnp.float32, 'w_q_up': _jnp.float32, 'kv_norm_g': _jnp.float32, 'w_kv_up': _jnp.float32, 'conv_w': _jnp.float32, 'attn_out_g': _jnp.float32, 'conv_out_g': _jnp.float32, 'w_out': _jnp.float32, 'final_norm_g': _jnp.float32}
MOMENT_SCALE = {'meta_tokens': 8.626313e-03, 'norm_g': 1.961738e-01, 'w_in': 1.045862e-01, 'q_norm_g': 1.337332e-01, 'w_q_up': 7.594899e-02, 'kv_norm_g': 3.354338e-01, 'w_kv_up': 8.768146e-02, 'conv_w': 9.268744e-02, 'attn_out_g': 9.379558e-02, 'conv_out_g': 9.157394e-02, 'w_out': 9.071870e-02, 'final_norm_g': 3.203430e+01}


def _to_microbatches(a, axis):
    t = _jnp.moveaxis(a, axis, 0)
    t = t.reshape((N_MICROBATCH, t.shape[0] // N_MICROBATCH) + t.shape[1:])
    return _jnp.moveaxis(t, 1, axis + 1)


def setup_inputs(seed: int = 0) -> dict:
    inp = _fwd_setup_inputs(seed)
    key = _jax.random.fold_in(_jax.random.key(seed), 7919)
    shape, _ = _output_shape()
    out = dict(inp)
    out["loss_target"] = _jax.random.normal(_jax.random.fold_in(key, 0), shape, _jnp.float32)
    for i, name in enumerate(TWIN_WEIGHTS):
        w = inp[name].astype(_jnp.float32)
        if MOMENT_SCALE is None:
            s = _jnp.sqrt(_jnp.mean(_jnp.square(w)) + 1e-30)
        else:
            s = MOMENT_SCALE[name]
        km, kv = _jax.random.split(_jax.random.fold_in(key, i + 1))
        out[name] = w
        out["m_" + name] = s * _jax.random.normal(km, w.shape, _jnp.float32)
        out["v_" + name] = (s * s) * _jax.random.uniform(kv, w.shape, _jnp.float32, 0.5, 1.5)
    if N_MICROBATCH > 1:
        for name, axis in PER_EXAMPLE_BATCH_AXIS.items():
            out[name] = _to_microbatches(out[name], axis)
    return {'x': out['x'], 'meta_tokens': out['meta_tokens'], 'norm_g': out['norm_g'], 'w_in': out['w_in'], 'q_norm_g': out['q_norm_g'], 'w_q_up': out['w_q_up'], 'kv_norm_g': out['kv_norm_g'], 'w_kv_up': out['w_kv_up'], 'conv_w': out['conv_w'], 'attn_out_g': out['attn_out_g'], 'conv_out_g': out['conv_out_g'], 'w_out': out['w_out'], 'final_norm_g': out['final_norm_g'], 'loss_target': out['loss_target'], 'm_meta_tokens': out['m_meta_tokens'], 'm_norm_g': out['m_norm_g'], 'm_w_in': out['m_w_in'], 'm_q_norm_g': out['m_q_norm_g'], 'm_w_q_up': out['m_w_q_up'], 'm_kv_norm_g': out['m_kv_norm_g'], 'm_w_kv_up': out['m_w_kv_up'], 'm_conv_w': out['m_conv_w'], 'm_attn_out_g': out['m_attn_out_g'], 'm_conv_out_g': out['m_conv_out_g'], 'm_w_out': out['m_w_out'], 'm_final_norm_g': out['m_final_norm_g'], 'v_meta_tokens': out['v_meta_tokens'], 'v_norm_g': out['v_norm_g'], 'v_w_in': out['v_w_in'], 'v_q_norm_g': out['v_q_norm_g'], 'v_w_q_up': out['v_w_q_up'], 'v_kv_norm_g': out['v_kv_norm_g'], 'v_w_kv_up': out['v_w_kv_up'], 'v_conv_w': out['v_conv_w'], 'v_attn_out_g': out['v_attn_out_g'], 'v_conv_out_g': out['v_conv_out_g'], 'v_w_out': out['v_w_out'], 'v_final_norm_g': out['v_final_norm_g']}


def _loss(weights, diff, rest, loss_target):
    with _jax.named_scope("forward"):
        args = {**rest, TWIN_DIFF_INPUT: diff, **{k: w.astype(_WEIGHT_DTYPES[k]) for k, w in weights.items()}}
        y = _forward(args)
    with _jax.named_scope("loss_head"):
        err = _jnp.square(y.astype(_jnp.float32) - loss_target)
        return 0.5 * _jnp.sum(_jnp.mean(err, axis=-1)) if err.ndim else 0.5 * err


def _adamw(w, g, m, v):
    m = ADAM_B1 * m + (1.0 - ADAM_B1) * g
    v = ADAM_B2 * v + (1.0 - ADAM_B2) * _jnp.square(g)
    m_hat = m / (1.0 - ADAM_B1 ** ADAM_STEP)
    v_hat = v / (1.0 - ADAM_B2 ** ADAM_STEP)
    delta = -ADAM_LR * (m_hat / (_jnp.sqrt(v_hat) + ADAM_EPS) + ADAM_WD * w)
    return delta, m, v


def reference(x, meta_tokens, norm_g, w_in, q_norm_g, w_q_up, kv_norm_g, w_kv_up, conv_w, attn_out_g, conv_out_g, w_out, final_norm_g, loss_target, m_meta_tokens, m_norm_g, m_w_in, m_q_norm_g, m_w_q_up, m_kv_norm_g, m_w_kv_up, m_conv_w, m_attn_out_g, m_conv_out_g, m_w_out, m_final_norm_g, v_meta_tokens, v_norm_g, v_w_in, v_q_norm_g, v_w_q_up, v_kv_norm_g, v_w_kv_up, v_conv_w, v_attn_out_g, v_conv_out_g, v_w_out, v_final_norm_g):
    given = dict(x=x, meta_tokens=meta_tokens, norm_g=norm_g, w_in=w_in, q_norm_g=q_norm_g, w_q_up=w_q_up, kv_norm_g=kv_norm_g, w_kv_up=w_kv_up, conv_w=conv_w, attn_out_g=attn_out_g, conv_out_g=conv_out_g, w_out=w_out, final_norm_g=final_norm_g, loss_target=loss_target, m_meta_tokens=m_meta_tokens, m_norm_g=m_norm_g, m_w_in=m_w_in, m_q_norm_g=m_q_norm_g, m_w_q_up=m_w_q_up, m_kv_norm_g=m_kv_norm_g, m_w_kv_up=m_w_kv_up, m_conv_w=m_conv_w, m_attn_out_g=m_attn_out_g, m_conv_out_g=m_conv_out_g, m_w_out=m_w_out, m_final_norm_g=m_final_norm_g, v_meta_tokens=v_meta_tokens, v_norm_g=v_norm_g, v_w_in=v_w_in, v_q_norm_g=v_q_norm_g, v_w_q_up=v_w_q_up, v_kv_norm_g=v_kv_norm_g, v_w_kv_up=v_w_kv_up, v_conv_w=v_conv_w, v_attn_out_g=v_attn_out_g, v_conv_out_g=v_conv_out_g, v_w_out=v_w_out, v_final_norm_g=v_final_norm_g)
    weights = {n: given[n] for n in TWIN_WEIGHTS}
    shared = {n: given[n] for n in SHARED_INPUTS}
    per_example = {n: given[n] for n in ['x']}
    grad_fn = _jax.value_and_grad(_loss, argnums=(0, 1))

    def one_microbatch(ex, loss_target):
        ex = dict(ex)
        diff = ex.pop(TWIN_DIFF_INPUT)
        return grad_fn(weights, diff, {**shared, **ex}, loss_target)

    if N_MICROBATCH == 1:
        loss, (grad_w, grad_x) = one_microbatch(per_example, given["loss_target"])
    else:
        def body(carry, xs):
            loss_sum, grad_sum = carry
            l_k, (gw_k, gx_k) = one_microbatch(xs[0], xs[1])
            with _jax.named_scope("update"):
                return (loss_sum + l_k, _jax.tree.map(_jnp.add, grad_sum, gw_k)), gx_k

        init = (_jnp.zeros((), _jnp.float32), _jax.tree.map(_jnp.zeros_like, weights))
        (loss, grad_w), grad_x = _jax.lax.scan(body, init, (per_example, given["loss_target"]))
    with _jax.named_scope("update"):
        delta_w, new_m, new_v = {}, {}, {}
        for n in TWIN_WEIGHTS:
            delta_w[n], new_m[n], new_v[n] = _adamw(weights[n], grad_w[n], given["m_" + n], given["v_" + n])
    return (loss, grad_x, *[grad_w[n] for n in TWIN_WEIGHTS], *[delta_w[n] for n in TWIN_WEIGHTS],
            *[new_m[n] for n in TWIN_WEIGHTS], *[new_v[n] for n in TWIN_WEIGHTS])
```

```python
import functools

import jax
import jax.numpy as jnp
from jax import lax
from jax.experimental import pallas as pl
from jax.experimental.pallas import tpu as pltpu

F32 = jnp.float32
BF16 = jnp.bfloat16

D_MODEL = 1024
N_META = 16
HEADS = 4
NOPE = 128
ROPE = 64
VDIM = 128
QK_PAD = 256
Q_RANK = 256
KV_RANK = 128
CONV_W = 512
CONV_GROUP = 64
ROPE_THETA = 10000.0
EPS = 1e-6
ATTN_SCALE = (NOPE + ROPE) ** -0.5
IN_DIM = 3008
IN_PAD = 3072
BLK_ZA, BLK_CB, BLK_CC, BLK_CH, BLK_ZC = 1, 2, 3, 4, 5
NEG_INF = -1e30

ADAM_LR = 0.001
ADAM_B1 = 0.9
ADAM_B2 = 0.999
ADAM_EPS = 1e-08
ADAM_WD = 0.01
ADAM_STEP = 10

ROW_TILE = 256
VMEM_LIMIT = 56 * 1024 * 1024

NT = (((1,), (1,)), ((), ()))
TN = (((0,), (0,)), ((), ()))


def _cparams(*sem):
    return pltpu.CompilerParams(dimension_semantics=sem, vmem_limit_bytes=VMEM_LIMIT)


def _dot(a, b):
    return jnp.dot(a, b, preferred_element_type=F32)


def _dot_nt(a, b):
    return lax.dot_general(a, b, NT, preferred_element_type=F32)


def _dot_tn(a, b):
    return lax.dot_general(a, b, TN, preferred_element_type=F32)


def _rms(x, g):
    r = lax.rsqrt(jnp.mean(x * x, axis=-1, keepdims=True) + EPS)
    return x * r * g, r


def _rms_bwd(dy, x, r, g):
    xh = x * r
    dyg = dy * g
    dx = r * (dyg - xh * jnp.mean(dyg * xh, axis=-1, keepdims=True))
    return dx, dy * xh


def _sigmoid(z):
    return 1.0 / (1.0 + jnp.exp(-z))


def _rope(b, c, sa, sb):
    return b * c + pltpu.roll(b, 96, 1) * sa + pltpu.roll(b, 32, 1) * sb


def _rope_bwd(d, c, sa, sb):
    return d * c + pltpu.roll(d * sa, 32, 1) + pltpu.roll(d * sb, 96, 1)


def _group_mean(x, gmat):
    hi = x.astype(BF16)
    lo = (x - hi.astype(F32)).astype(BF16)
    return _dot(hi, gmat) + _dot(lo, gmat)


def _row_of(col, rows):
    return jnp.transpose(jnp.broadcast_to(col, (rows, 128)))[0:1, :]


def _rope_tables(n_pos):
    half = ROPE // 2
    inv_freq = 1.0 / (ROPE_THETA ** (jnp.arange(half, dtype=F32) / half))
    ang = jnp.arange(n_pos, dtype=F32)[:, None] * inv_freq[None, :]
    cos, sin, z = jnp.cos(ang), jnp.sin(ang), jnp.zeros((n_pos, half), F32)
    c = jnp.concatenate([cos, cos, z, z], axis=1)
    sa = jnp.concatenate([-sin, z, z, z], axis=1)
    sb = jnp.concatenate([z, sin, z, z], axis=1)
    return c, sa, sb


def _fwd_proj(x2d, tabs, norm_g, w_in_p, q_norm_g, wq_p, kv_norm_g, wkv_p, nb, s, tm, name):
    nt = s // tm
    c_t, sa_t, sb_t = tabs

    def body(x_ref, c_ref, sa_ref, sb_ref, g_ref, w_ref, gq_ref, wq_ref, gkv_ref, wkv_ref,
             p_ref, q_ref, k_ref, v_ref):
        u, _ = _rms(x_ref[...], g_ref[...])
        p = _dot(u.astype(BF16), w_ref[...])
        p_ref[...] = p
        c, sa, sb = c_ref[...], sa_ref[...], sb_ref[...]
        qn, _ = _rms(p[:, 0:Q_RANK], gq_ref[...])
        q = _dot(qn.astype(BF16), wq_ref[...])
        kvn, _ = _rms(p[:, Q_RANK:Q_RANK + KV_RANK], gkv_ref[...])
        kv = _dot(kvn.astype(BF16), wkv_ref[...])
        kpe = _rope(p[:, 384:512], c, sa, sb)
        for h in range(HEADS):
            pe = _rope(q[:, QK_PAD * h + NOPE:QK_PAD * (h + 1)], c, sa, sb)
            qh = jnp.concatenate([q[:, QK_PAD * h:QK_PAD * h + NOPE], pe], axis=1)
            q_ref[0, h] = (qh * ATTN_SCALE).astype(BF16)
            k_ref[0, h] = jnp.concatenate([kv[:, NOPE * h:NOPE * (h + 1)], kpe], axis=1).astype(BF16)
            v_ref[0, h] = kv[:, 512 + VDIM * h:512 + VDIM * (h + 1)].astype(BF16)

    full = lambda a: pl.BlockSpec(a.shape, lambda i: (0,) * a.ndim)
    tab = pl.BlockSpec((tm, 128), lambda i: (i % nt, 0))
    return pl.pallas_call(
        body, name=name, grid=(nb * nt,),
        in_specs=[pl.BlockSpec((tm, D_MODEL), lambda i: (i, 0)), tab, tab, tab,
                  full(norm_g), full(w_in_p), full(q_norm_g), full(wq_p), full(kv_norm_g), full(wkv_p)],
        out_specs=[pl.BlockSpec((tm, IN_PAD), lambda i: (i, 0)),
                   pl.BlockSpec((1, HEADS, tm, QK_PAD), lambda i: (i // nt, 0, i % nt, 0)),
                   pl.BlockSpec((1, HEADS, tm, QK_PAD), lambda i: (i // nt, 0, i % nt, 0)),
                   pl.BlockSpec((1, HEADS, tm, VDIM), lambda i: (i // nt, 0, i % nt, 0))],
        out_shape=[jax.ShapeDtypeStruct((nb * s, IN_PAD), F32),
                   jax.ShapeDtypeStruct((nb, HEADS, s, QK_PAD), BF16),
                   jax.ShapeDtypeStruct((nb, HEADS, s, QK_PAD), BF16),
                   jax.ShapeDtypeStruct((nb, HEADS, s, VDIM), BF16)],
        compiler_params=_cparams("parallel"),
    )(x2d, c_t, sa_t, sb_t, norm_g, w_in_p, q_norm_g, wq_p, kv_norm_g, wkv_p)


def _attn_fwd(q, k, v, km, vm, nb, s, tq):
    nq = s // tq

    def body(q_ref, k_ref, v_ref, km_ref, vm_ref, o_ref, lse_ref):
        i = pl.program_id(2)
        qt = q_ref[0, 0]
        sm = _dot_nt(qt, km_ref[0, 0])
        m = jnp.max(sm, axis=1, keepdims=True)
        pm = jnp.exp(sm - m)
        l = jnp.sum(pm, axis=1, keepdims=True)
        acc = _dot(pm.astype(BF16), vm_ref[0, 0])

        def step(j, carry, masked):
            m, l, acc = carry
            kj = k_ref[0, 0, pl.ds(pl.multiple_of(j * tq, tq), tq), :]
            vj = v_ref[0, 0, pl.ds(pl.multiple_of(j * tq, tq), tq), :]
            sc = _dot_nt(qt, kj)
            if masked:
                row = lax.broadcasted_iota(jnp.int32, (tq, tq), 0)
                col = lax.broadcasted_iota(jnp.int32, (tq, tq), 1)
                sc = jnp.where(col <= row, sc, NEG_INF)
            m_new = jnp.maximum(m, jnp.max(sc, axis=1, keepdims=True))
            a = jnp.exp(m - m_new)
            p = jnp.exp(sc - m_new)
            l = a * l + jnp.sum(p, axis=1, keepdims=True)
            acc = a * acc + _dot(p.astype(BF16), vj)
            return m_new, l, acc

        carry = lax.fori_loop(0, i, functools.partial(step, masked=False), (m, l, acc))
        m, l, acc = step(i, carry, True)
        o_ref[0, 0] = acc / l
        lse_ref[0, 0, 0] = _row_of(m + jnp.log(l), tq)

    return pl.pallas_call(
        body, name="attn_fwd", grid=(nb, HEADS, nq),
        in_specs=[pl.BlockSpec((1, 1, tq, QK_PAD), lambda b, h, i: (b, h, i, 0)),
                  pl.BlockSpec((1, 1, s, QK_PAD), lambda b, h, i: (b, h, 0, 0)),
                  pl.BlockSpec((1, 1, s, VDIM), lambda b, h, i: (b, h, 0, 0)),
                  pl.BlockSpec((1, 1, N_META, QK_PAD), lambda b, h, i: (0, h, 0, 0)),
                  pl.BlockSpec((1, 1, N_META, VDIM), lambda b, h, i: (0, h, 0, 0))],
        out_specs=[pl.BlockSpec((1, 1, tq, VDIM), lambda b, h, i: (b, h, i, 0)),
                   pl.BlockSpec((1, 1, 1, 1, tq), lambda b, h, i: (b, h, i, 0, 0))],
        out_shape=[jax.ShapeDtypeStruct((nb, HEADS, s, VDIM), F32),
                   jax.ShapeDtypeStruct((nb, HEADS, nq, 1, tq), F32)],
        compiler_params=_cparams("parallel", "parallel", "parallel"),
    )(q, k, v, km, vm)


def _shift_rows(a, prev, n_rows):
    rid = lax.broadcasted_iota(jnp.int32, a.shape, 0)
    a1 = jnp.where(rid == 0, prev[7:8, :], pltpu.roll(a, 1, 0))
    a2 = jnp.where(rid == 0, prev[6:7, :], jnp.where(rid == 1, prev[7:8, :], pltpu.roll(a, 2, 0)))
    return a1, a2


def _attn_gate(o, za, ga_h):
    on, r = _rms(o, ga_h)
    return on * (za * _sigmoid(za)), on, r


def _out_fwd_bwd(x2d, tgt2d, o, p, pm, conv_w, ga, gc, gmat, w_out, gf, nb, s, tm):
    nt = s // tm
    r = nb * s
    prev_idx = lambda i: jnp.maximum(i * (tm // 8) - 1, 0)

    def body(x_ref, t_ref, o_ref, za_ref, cb_ref, cc_ref, ch_ref, zc_ref, ccp_ref, chp_ref, mc_ref, mh_ref,
             cw_ref, ga_ref, gc_ref, gm_ref, w_ref, gf_ref,
             dh_ref, dy_ref, dw_ref, dgf_ref, loss_ref):
        i = pl.program_id(0)

        @pl.when(i == 0)
        def _():
            dw_ref[...] = jnp.zeros_like(dw_ref)
            dgf_ref[...] = jnp.zeros_like(dgf_ref)
            loss_ref[...] = jnp.zeros_like(loss_ref)

        ya = []
        for h in range(HEADS):
            y, _, _ = _attn_gate(o_ref[0, h], za_ref[:, VDIM * h:VDIM * (h + 1)],
                                 ga_ref[:, VDIM * h:VDIM * (h + 1)])
            ya.append(y)
        cc = cc_ref[...] * ch_ref[...]
        prev = jnp.where(i % nt == 0, mc_ref[8:16, :] * mh_ref[8:16, :], ccp_ref[...] * chp_ref[...])
        cc1, cc2 = _shift_rows(cc, prev, tm)
        yc = cb_ref[...] * (cw_ref[0:1, :] * cc2 + cw_ref[1:2, :] * cc1 + cw_ref[2:3, :] * cc)
        rg = lax.rsqrt(_group_mean(yc * yc, gm_ref[...]) + EPS)
        zc = zc_ref[...]
        yconv = yc * rg * gc_ref[...] * (zc * _sigmoid(zc))
        ycat = jnp.concatenate(ya + [yconv], axis=1).astype(BF16)
        h2 = x_ref[...] + _dot(ycat, w_ref[...])
        gfv = gf_ref[...]
        y, r2 = _rms(h2, gfv)
        e = y - t_ref[...]
        loss_ref[...] += 0.5 * jnp.sum(e * e) / D_MODEL
        dyv = e * (1.0 / D_MODEL)
        dh2, dgf = _rms_bwd(dyv, h2, r2, gfv)
        dgf_ref[...] += jnp.sum(dgf, axis=0, keepdims=True)
        dh_ref[...] = dh2
        dhb = dh2.astype(BF16)
        dy_ref[...] = _dot_nt(dhb, w_ref[...])
        dw_ref[...] += _dot_tn(ycat, dhb)

    row = lambda w, j: pl.BlockSpec((tm, w), lambda i: (i, j))
    pblk = lambda j: pl.BlockSpec((tm, 512), lambda i: (i, j))
    pprev = lambda j: pl.BlockSpec((8, 512), lambda i: (prev_idx(i), j))
    mblk = lambda j: pl.BlockSpec((N_META, 512), lambda i: (0, j))
    full = lambda a: pl.BlockSpec(a.shape, lambda i: (0,) * a.ndim)
    return pl.pallas_call(
        body, name="out_fwd_bwd", grid=(nb * nt,),
        in_specs=[row(D_MODEL, 0), row(D_MODEL, 0),
                  pl.BlockSpec((1, HEADS, tm, VDIM), lambda i: (i // nt, 0, i % nt, 0)),
                  pblk(BLK_ZA), pblk(BLK_CB), pblk(BLK_CC), pblk(BLK_CH), pblk(BLK_ZC),
                  pprev(BLK_CC), pprev(BLK_CH), mblk(BLK_CC), mblk(BLK_CH),
                  full(conv_w), full(ga), full(gc), full(gmat), full(w_out), full(gf)],
        out_specs=[row(D_MODEL, 0), row(D_MODEL, 0),
                   pl.BlockSpec((D_MODEL, D_MODEL), lambda i: (0, 0)),
                   pl.BlockSpec((1, D_MODEL), lambda i: (0, 0)),
                   pl.BlockSpec((1, 128), lambda i: (0, 0))],
        out_shape=[jax.ShapeDtypeStruct((r, D_MODEL), F32), jax.ShapeDtypeStruct((r, D_MODEL), F32),
                   jax.ShapeDtypeStruct((D_MODEL, D_MODEL), F32), jax.ShapeDtypeStruct((1, D_MODEL), F32),
                   jax.ShapeDtypeStruct((1, 128), F32)],
        compiler_params=_cparams("arbitrary"),
    )(x2d, tgt2d, o, p, p, p, p, p, p, p, pm, pm, conv_w, ga, gc, gmat, w_out, gf)


def _gate_bwd(dycat, o, p, pm, conv_w, ga, gc, gmat, nb, s, tm):
    nt = s // tm
    r = nb * s
    ext = tm + 8
    prev_idx = lambda i: jnp.maximum(i * (tm // 8) - 1, 0)
    next_idx = lambda i: jnp.minimum((i + 1) * (tm // 8), r // 8 - 1)

    def body(dya_ref, dyc_ref, dycn_ref, o_ref, za_ref, cb_ref, cbn_ref, cc_ref, ccp_ref, ccn_ref,
             ch_ref, chp_ref, chn_ref, zc_ref, zcn_ref, mc_ref, mh_ref, cw_ref, ga_ref, gc_ref, gm_ref,
             dpb_ref, do_ref, dl_ref, dccm_ref, dga_ref, dgc_ref, dcw_ref):
        i = pl.program_id(0)

        @pl.when(i == 0)
        def _():
            dga_ref[...] = jnp.zeros_like(dga_ref)
            dgc_ref[...] = jnp.zeros_like(dgc_ref)
            dcw_ref[...] = jnp.zeros_like(dcw_ref)

        dga = []
        for h in range(HEADS):
            hs = slice(VDIM * h, VDIM * (h + 1))
            oh, za, gah, dya = o_ref[0, h], za_ref[:, hs], ga_ref[:, hs], dya_ref[:, hs]
            sg = _sigmoid(za)
            on, ro = _rms(oh, gah)
            don = dya * (za * sg)
            dpb_ref[:, hs] = dya * on * (sg * (1.0 + za * (1.0 - sg)))
            do, dg = _rms_bwd(don, oh, ro, gah)
            dga.append(jnp.sum(dg, axis=0, keepdims=True))
            dob = do.astype(BF16)
            do_ref[0, h] = dob
            dl_ref[0, h, 0] = _row_of(jnp.sum(dob.astype(F32) * oh, axis=1, keepdims=True), tm)
        dga_ref[...] += jnp.concatenate(dga, axis=1)

        cat = lambda a, b: jnp.concatenate([a[...], b[...]], axis=0)
        cch = cat(cc_ref, ccn_ref)
        chh = cat(ch_ref, chn_ref)
        cb = cat(cb_ref, cbn_ref)
        zc = cat(zc_ref, zcn_ref)
        dy = cat(dyc_ref, dycn_ref)
        first = i % nt == 0
        last = i % nt == nt - 1
        cc = cch * chh
        prev = jnp.where(first, mc_ref[8:16, :] * mh_ref[8:16, :], ccp_ref[...] * chp_ref[...])
        cc1, cc2 = _shift_rows(cc, prev, ext)
        w0, w1, w2 = cw_ref[0:1, :], cw_ref[1:2, :], cw_ref[2:3, :]
        dw = w0 * cc2 + w1 * cc1 + w2 * cc
        yc = cb * dw
        rg = lax.rsqrt(_group_mean(yc * yc, gm_ref[...]) + EPS)
        ych = yc * rg
        gcv = gc_ref[...]
        sg = _sigmoid(zc)
        dycn = dy * (zc * sg)
        dzc = dy * (ych * gcv) * (sg * (1.0 + zc * (1.0 - sg)))
        dgc_ref[...] += jnp.sum((dycn * ych)[:tm], axis=0, keepdims=True)
        dycg = dycn * gcv
        dyc = rg * (dycg - ych * _group_mean(dycg * ych, gm_ref[...]))
        rid = lax.broadcasted_iota(jnp.int32, (ext, CONV_W), 0)
        ddw = jnp.where(jnp.logical_and(last, rid >= tm), 0.0, dyc * cb)
        dcb = dyc * dw
        dcc = w2 * ddw + w1 * pltpu.roll(ddw, ext - 1, 0) + w0 * pltpu.roll(ddw, ext - 2, 0)
        dpb_ref[:, 512:1024] = dcb[:tm]
        dpb_ref[:, 1024:1536] = (dcc * chh)[:tm]
        dpb_ref[:, 1536:2048] = (dcc * cch)[:tm]
        dpb_ref[:, 2048:2560] = dzc[:tm]
        rs = lambda a: jnp.sum(a[:tm], axis=0, keepdims=True)
        dcw_ref[0:1, :] += rs(ddw * cc2)
        dcw_ref[1:2, :] += rs(ddw * cc1)
        dcw_ref[2:3, :] += rs(ddw * cc)

        @pl.when(first)
        def _():
            d0, d1 = ddw[0:1, :], ddw[1:2, :]
            r8 = lax.broadcasted_iota(jnp.int32, (8, CONV_W), 0)
            dccm_ref[0] = jnp.where(r8 == 7, w1 * d0 + w0 * d1, jnp.where(r8 == 6, w0 * d0, 0.0))

    row = lambda j: pl.BlockSpec((tm, 512), lambda i: (i, j))
    prv = lambda j: pl.BlockSpec((8, 512), lambda i: (prev_idx(i), j))
    nxt = lambda j: pl.BlockSpec((8, 512), lambda i: (next_idx(i), j))
    mblk = lambda j: pl.BlockSpec((N_META, 512), lambda i: (0, j))
    full = lambda a: pl.BlockSpec(a.shape, lambda i: (0,) * a.ndim)
    hb = lambda w: pl.BlockSpec((1, HEADS, tm, w), lambda i: (i // nt, 0, i % nt, 0))
    acc = lambda rr: pl.BlockSpec((rr, 512), lambda i: (0, 0))
    return pl.pallas_call(
        body, name="gate_bwd", grid=(nb * nt,),
        in_specs=[row(0), row(1), nxt(1), hb(VDIM),
                  row(BLK_ZA), row(BLK_CB), nxt(BLK_CB), row(BLK_CC), prv(BLK_CC), nxt(BLK_CC),
                  row(BLK_CH), prv(BLK_CH), nxt(BLK_CH), row(BLK_ZC), nxt(BLK_ZC),
                  mblk(BLK_CC), mblk(BLK_CH), full(conv_w), full(ga), full(gc), full(gmat)],
        out_specs=[pl.BlockSpec((tm, 2560), lambda i: (i, 0)), hb(VDIM),
                   pl.BlockSpec((1, HEADS, 1, 1, tm), lambda i: (i // nt, 0, i % nt, 0, 0)),
                   pl.BlockSpec((1, 8, 512), lambda i: (i // nt, 0, 0)),
                   acc(1), acc(1), acc(8)],
        out_shape=[jax.ShapeDtypeStruct((r, 2560), F32), jax.ShapeDtypeStruct((nb, HEADS, s, VDIM), BF16),
                   jax.ShapeDtypeStruct((nb, HEADS, nt, 1, tm), F32), jax.ShapeDtypeStruct((nb, 8, 512), F32),
                   jax.ShapeDtypeStruct((1, 512), F32), jax.ShapeDtypeStruct((1, 512), F32),
                   jax.ShapeDtypeStruct((8, 512), F32)],
        compiler_params=_cparams("arbitrary"),
    )(dycat, dycat, dycat, o, p, p, p, p, p, p, p, p, p, p, p, pm, pm, conv_w, ga, gc, gmat)


def _attn_bwd(q, k, v, do, lse, delta, km, vm, nb, s, t):
    n = s // t

    def body(q_ref, k_ref, v_ref, do_ref, lse_ref, dl_ref, km_ref, vm_ref,
             dq_ref, dk_ref, dv_ref, dkm_ref, dvm_ref):
        b = pl.program_id(1)
        dq_ref[...] = jnp.zeros_like(dq_ref)
        dk_ref[...] = jnp.zeros_like(dk_ref)
        dv_ref[...] = jnp.zeros_like(dv_ref)

        @pl.when(b == 0)
        def _():
            dkm_ref[...] = jnp.zeros_like(dkm_ref)
            dvm_ref[...] = jnp.zeros_like(dvm_ref)

        def tile(kj, vj, i, masked):
            rows = pl.ds(pl.multiple_of(i * t, t), t)
            qi = q_ref[0, 0, rows, :]
            doi = do_ref[0, 0, rows, :]
            st = _dot_nt(kj, qi)
            if masked:
                kr = lax.broadcasted_iota(jnp.int32, st.shape, 0)
                qc = lax.broadcasted_iota(jnp.int32, st.shape, 1)
                st = jnp.where(kr <= qc, st, NEG_INF)
            pt = jnp.exp(st - lse_ref[0, 0, i])
            dpt = _dot_nt(vj, doi)
            dst = (pt * (dpt - dl_ref[0, 0, i])).astype(BF16)
            dq_ref[0, 0, rows, :] += _dot_tn(dst, kj)
            return _dot(dst, qi), _dot(pt.astype(BF16), doi)

        def per_key_tile(j, _):
            krows = pl.ds(pl.multiple_of(j * t, t), t)
            kj = k_ref[0, 0, krows, :]
            vj = v_ref[0, 0, krows, :]
            dk, dv = tile(kj, vj, j, True)
            dk_ref[0, 0, krows, :] += dk
            dv_ref[0, 0, krows, :] += dv

            def inner(i, _):
                dk, dv = tile(kj, vj, i, False)
                dk_ref[0, 0, krows, :] += dk
                dv_ref[0, 0, krows, :] += dv
                return 0

            lax.fori_loop(j + 1, n, inner, 0)
            return 0

        lax.fori_loop(0, n, per_key_tile, 0)

        def meta(i, _):
            dk, dv = tile(km_ref[0, 0], vm_ref[0, 0], i, False)
            dkm_ref[0] += dk
            dvm_ref[0] += dv
            return 0

        lax.fori_loop(0, n, meta, 0)

    big = lambda w: pl.BlockSpec((1, 1, s, w), lambda h, b: (b, h, 0, 0))
    rowv = pl.BlockSpec((1, 1, n, 1, t), lambda h, b: (b, h, 0, 0, 0))
    mk = lambda w: pl.BlockSpec((1, 1, N_META, w), lambda h, b: (0, h, 0, 0))
    mo = lambda w: pl.BlockSpec((1, N_META, w), lambda h, b: (h, 0, 0))
    return pl.pallas_call(
        body, name="attn_bwd", grid=(HEADS, nb),
        in_specs=[big(QK_PAD), big(QK_PAD), big(VDIM), big(VDIM), rowv, rowv, mk(QK_PAD), mk(VDIM)],
        out_specs=[big(QK_PAD), big(QK_PAD), big(VDIM), mo(QK_PAD), mo(VDIM)],
        out_shape=[jax.ShapeDtypeStruct((nb, HEADS, s, QK_PAD), F32),
                   jax.ShapeDtypeStruct((nb, HEADS, s, QK_PAD), F32),
                   jax.ShapeDtypeStruct((nb, HEADS, s, VDIM), F32),
                   jax.ShapeDtypeStruct((HEADS, N_META, QK_PAD), F32),
                   jax.ShapeDtypeStruct((HEADS, N_META, VDIM), F32)],
        compiler_params=_cparams("arbitrary", "arbitrary"),
    )(q, k, v, do, lse, delta, km, vm)


def _up_bwd(dq, dk, dv, dkm, dvm, p, pm, tabs, tabs_m, wq_p, wkv_p, gq, gkv, nb, s, tm):
    nt = s // tm
    n = nb * nt
    c_t, sa_t, sb_t = tabs
    cm_t, sam_t, sbm_t = tabs_m

    def kv_path(dkh, dvh, pa, c, sa, sb, wkv, gkvv):
        dkpe = dkh[0][:, NOPE:]
        for h in range(1, HEADS):
            dkpe = dkpe + dkh[h][:, NOPE:]
        dkr = _rope_bwd(dkpe, c, sa, sb)
        dkv = jnp.concatenate([d[:, :NOPE] for d in dkh] + list(dvh), axis=1).astype(BF16)
        ckv = pa[:, Q_RANK:Q_RANK + KV_RANK]
        kvn, rkv = _rms(ckv, gkvv)
        dckv, dg = _rms_bwd(_dot_nt(dkv, wkv), ckv, rkv, gkvv)
        return dckv, dkr, kvn.astype(BF16), dkv, jnp.sum(dg, axis=0, keepdims=True)

    def body(dq_ref, dk_ref, dv_ref, pa_ref, c_ref, sa_ref, sb_ref,
             dkm_ref, dvm_ref, pam_ref, cm_ref, sam_ref, sbm_ref,
             wq_ref, wkv_ref, gq_ref, gkv_ref,
             dpa_ref, dpam_ref, dwq_ref, dwkv_ref, dgq_ref, dgkv_ref):
        i = pl.program_id(0)

        @pl.when(i == 0)
        def _():
            dwq_ref[...] = jnp.zeros_like(dwq_ref)
            dwkv_ref[...] = jnp.zeros_like(dwkv_ref)
            dgq_ref[...] = jnp.zeros_like(dgq_ref)
            dgkv_ref[...] = jnp.zeros_like(dgkv_ref)

        @pl.when(i < n)
        def _():
            c, sa, sb = c_ref[...], sa_ref[...], sb_ref[...]
            pa = pa_ref[...]
            parts = []
            for h in range(HEADS):
                dqh = dq_ref[0, h] * ATTN_SCALE
                parts += [dqh[:, :NOPE], _rope_bwd(dqh[:, NOPE:], c, sa, sb)]
            dql = jnp.concatenate(parts, axis=1).astype(BF16)
            cq = pa[:, 0:Q_RANK]
            gqv = gq_ref[...]
            qn, rq = _rms(cq, gqv)
            dwq_ref[...] += _dot_tn(qn.astype(BF16), dql)
            dcq, dg = _rms_bwd(_dot_nt(dql, wq_ref[...]), cq, rq, gqv)
            dgq_ref[...] += jnp.sum(dg, axis=0, keepdims=True)
            dckv, dkr, kvn, dkv, dgk = kv_path([dk_ref[0, h] for h in range(HEADS)],
                                               [dv_ref[0, h] for h in range(HEADS)],
                                               pa, c, sa, sb, wkv_ref[...], gkv_ref[...])
            dwkv_ref[...] += _dot_tn(kvn, dkv)
            dgkv_ref[...] += dgk
            dpa_ref[...] = jnp.concatenate([dcq, dckv, dkr], axis=1)

        @pl.when(i == n)
        def _():
            dckv, dkr, kvn, dkv, dgk = kv_path([dkm_ref[h] for h in range(HEADS)],
                                               [dvm_ref[h] for h in range(HEADS)],
                                               pam_ref[...], cm_ref[...], sam_ref[...], sbm_ref[...],
                                               wkv_ref[...], gkv_ref[...])
            dwkv_ref[...] += _dot_tn(kvn, dkv)
            dgkv_ref[...] += dgk
            dpam_ref[...] = jnp.concatenate([jnp.zeros((N_META, Q_RANK), F32), dckv, dkr], axis=1)

    cl = lambda i: jnp.minimum(i, n - 1)
    hb = lambda w: pl.BlockSpec((1, HEADS, tm, w), lambda i: (cl(i) // nt, 0, cl(i) % nt, 0))
    tab = pl.BlockSpec((tm, 128), lambda i: (cl(i) % nt, 0))
    full = lambda a: pl.BlockSpec(a.shape, lambda i: (0,) * a.ndim)
    const = lambda shape: pl.BlockSpec(shape, lambda i: (0,) * len(shape))
    return pl.pallas_call(
        body, name="up_bwd", grid=(n + 1,),
        in_specs=[hb(QK_PAD), hb(QK_PAD), hb(VDIM), pl.BlockSpec((tm, 512), lambda i: (cl(i), 0)), tab, tab, tab,
                  full(dkm), full(dvm), pl.BlockSpec((N_META, 512), lambda i: (0, 0)),
                  full(cm_t), full(sam_t), full(sbm_t), full(wq_p), full(wkv_p), full(gq), full(gkv)],
        out_specs=[pl.BlockSpec((tm, 512), lambda i: (cl(i), 0)), const((N_META, 512)),
                   const((Q_RANK, HEADS * QK_PAD)), const((KV_RANK, 1024)), const((1, Q_RANK)), const((1, KV_RANK))],
        out_shape=[jax.ShapeDtypeStruct((nb * s, 512), F32), jax.ShapeDtypeStruct((N_META, 512), F32),
                   jax.ShapeDtypeStruct((Q_RANK, HEADS * QK_PAD), F32), jax.ShapeDtypeStruct((KV_RANK, 1024), F32),
                   jax.ShapeDtypeStruct((1, Q_RANK), F32), jax.ShapeDtypeStruct((1, KV_RANK), F32)],
        compiler_params=_cparams("arbitrary"),
    )(dq, dk, dv, p, c_t, sa_t, sb_t, dkm, dvm, pm, cm_t, sam_t, sbm_t, wq_p, wkv_p, gq, gkv)


def _in_bwd(x2d, dh2, dpa, dpb, meta, dpam, dccm, pm, w_in_p, norm_g, nb, s, tm):
    nt = s // tm
    n = nb * nt

    def body(x_ref, dh_ref, dpa_ref, dpb_ref, mt_ref, dpam_ref, dccm_ref, mc_ref, mh_ref, w_ref, g_ref,
             gx_ref, gm_ref, dw_hbm, dg_ref, acc_ref, sem):
        i = pl.program_id(0)

        @pl.when(i == 0)
        def _():
            acc_ref[...] = jnp.zeros_like(acc_ref)
            dg_ref[...] = jnp.zeros_like(dg_ref)

        def rows(x, dp, dres):
            g = g_ref[...]
            u, r1 = _rms(x, g)
            dpb16 = dp.astype(BF16)
            acc_ref[...] += _dot_tn(u.astype(BF16), dpb16)
            dx, dg = _rms_bwd(_dot_nt(dpb16, w_ref[...]), x, r1, g)
            dg_ref[...] += jnp.sum(dg, axis=0, keepdims=True)
            return dx if dres is None else dx + dres

        @pl.when(i < n)
        def _():
            dp = jnp.concatenate([dpa_ref[...], dpb_ref[...]], axis=1)
            gx_ref[...] = rows(x_ref[...], dp, dh_ref[...])

        @pl.when(i == n)
        def _():
            dcc = dccm_ref[0]
            for b in range(1, nb):
                dcc = dcc + dccm_ref[b]
            z8 = jnp.zeros((8, CONV_W), F32)
            dc = jnp.concatenate([z8, dcc * mh_ref[8:16, :]], axis=0)
            dh = jnp.concatenate([z8, dcc * mc_ref[8:16, :]], axis=0)
            z = jnp.zeros((N_META, CONV_W), F32)
            dp = jnp.concatenate([dpam_ref[...], z, z, dc, dh, z], axis=1)
            gm_ref[...] = rows(mt_ref[...], dp, None)
            cp = pltpu.make_async_copy(acc_ref, dw_hbm, sem)
            cp.start()
            cp.wait()

    cl = lambda i: jnp.minimum(i, n - 1)
    row = lambda w: pl.BlockSpec((tm, w), lambda i: (cl(i), 0))
    full = lambda a: pl.BlockSpec(a.shape, lambda i: (0,) * a.ndim)
    mblk = lambda j: pl.BlockSpec((N_META, 512), lambda i: (0, j))
    return pl.pallas_call(
        body, name="in_bwd", grid=(n + 1,),
        in_specs=[row(D_MODEL), row(D_MODEL), row(512), row(2560), full(meta), full(dpam), full(dccm),
                  mblk(BLK_CC), mblk(BLK_CH), full(w_in_p), full(norm_g)],
        out_specs=[row(D_MODEL), pl.BlockSpec((N_META, D_MODEL), lambda i: (0, 0)),
                   pl.BlockSpec(memory_space=pl.ANY), pl.BlockSpec((1, D_MODEL), lambda i: (0, 0))],
        out_shape=[jax.ShapeDtypeStruct((nb * s, D_MODEL), F32), jax.ShapeDtypeStruct((N_META, D_MODEL), F32),
                   jax.ShapeDtypeStruct((D_MODEL, IN_PAD), F32), jax.ShapeDtypeStruct((1, D_MODEL), F32)],
        scratch_shapes=[pltpu.VMEM((D_MODEL, IN_PAD), F32), pltpu.SemaphoreType.DMA],
        compiler_params=_cparams("arbitrary"),
    )(x2d, dh2, dpa, dpb, meta, dpam, dccm, pm, pm, w_in_p, norm_g)


def _gather_shards(shards):
    n = len(shards)

    def body(*refs):
        ins, outs = refs[:n], refs[n:2 * n]
        send_sems, recv_sems, loc_sems = refs[2 * n:]
        x, y, c = lax.axis_index("x"), lax.axis_index("y"), lax.axis_index("c")
        mine = 2 * x + y
        chips = [(1 - x, y), (x, 1 - y), (1 - x, 1 - y)]

        def remote(a, j, block):
            px, py = chips[j]
            return pltpu.make_async_remote_copy(
                src_ref=ins[a], dst_ref=outs[a].at[block], send_sem=send_sems.at[a, j], recv_sem=recv_sems.at[a, j],
                device_id=(px, py, c), device_id_type=pl.DeviceIdType.MESH)

        local = [pltpu.make_async_copy(ins[a], outs[a].at[mine], loc_sems.at[a]) for a in range(n)]
        sends = [remote(a, j, mine) for a in range(n) for j in range(3)]
        for cp in local + sends:
            cp.start()
        for a in range(n):
            for j, (px, py) in enumerate(chips):
                remote(a, j, 2 * px + py).wait_recv()
        for cp in sends:
            cp.wait_send()
        for cp in local:
            cp.wait()

    hbm = pl.BlockSpec(memory_space=pl.ANY)
    return pl.pallas_call(
        body, name="gather_shards",
        in_specs=[hbm] * n, out_specs=[hbm] * n,
        out_shape=[jax.ShapeDtypeStruct((4,) + a.shape, a.dtype) for a in shards],
        scratch_shapes=[pltpu.SemaphoreType.DMA((n, 3)), pltpu.SemaphoreType.DMA((n, 3)),
                        pltpu.SemaphoreType.DMA((n,))],
        compiler_params=pltpu.CompilerParams(vmem_limit_bytes=VMEM_LIMIT),
    )(*shards)


def _reduce_grads(parts, small):
    n = len(parts)
    shapes = [a.shape[1:] for a in parts]

    def body(*refs):
        pin, sm_in = refs[:n], refs[n]
        gout, sm_out = refs[n + 1:2 * n + 1], refs[2 * n + 1]
        scr = refs[2 * n + 2:]
        rbuf, own, sib = scr[:n], scr[n:2 * n], scr[2 * n:3 * n]
        sbuf, send_sems, recv_sems, loc_sems, sib_send, sib_recv, sm_send, sm_recv = scr[3 * n:]
        x, y, c = lax.axis_index("x"), lax.axis_index("y"), lax.axis_index("c")
        mine = 2 * x + y
        me = 4 * x + 2 * y + c
        chips = [(1 - x, y), (x, 1 - y), (1 - x, 1 - y)]

        def remote(a, j):
            px, py = chips[j]
            return pltpu.make_async_remote_copy(
                src_ref=pin[a].at[2 * px + py], dst_ref=rbuf[a].at[j], send_sem=send_sems.at[a, j],
                recv_sem=recv_sems.at[a, j], device_id=(px, py, c), device_id_type=pl.DeviceIdType.MESH)

        def small_copy(kk):
            peer = (x ^ (kk >> 2), y ^ ((kk >> 1) & 1), c ^ (kk & 1))
            return pltpu.make_async_remote_copy(
                src_ref=sm_in, dst_ref=sbuf.at[kk], send_sem=sm_send.at[kk - 1], recv_sem=sm_recv.at[kk - 1],
                device_id=peer, device_id_type=pl.DeviceIdType.MESH)

        local = [pltpu.make_async_copy(pin[a].at[mine], own[a], loc_sems.at[a]) for a in range(n)]
        sends = [remote(a, j) for a in range(n) for j in range(3)]
        smalls = [small_copy(kk) for kk in range(1, 8)]
        for cp in local + sends + smalls:
            cp.start()
        sbuf[0] = sm_in[...]
        swaps = []
        for a in range(n):
            local[a].wait()
            for j in range(3):
                remote(a, j).wait_recv()
            own[a][...] = own[a][...] + rbuf[a][0] + rbuf[a][1] + rbuf[a][2]
            sw = pltpu.make_async_remote_copy(
                src_ref=own[a], dst_ref=sib[a], send_sem=sib_send.at[a], recv_sem=sib_recv.at[a],
                device_id=(x, y, 1 - c), device_id_type=pl.DeviceIdType.MESH)
            sw.start()
            swaps.append(sw)
        for cp in smalls:
            cp.wait_recv()
        total = sbuf[me]
        for d in range(1, 8):
            total = total + sbuf[me ^ d]
        sm_out[...] = total
        for a in range(n):
            swaps[a].wait_recv()
            gout[a][...] = own[a][...] + sib[a][...]
        for cp in sends + smalls + swaps:
            cp.wait_send()

    hbm = pl.BlockSpec(memory_space=pl.ANY)
    vmem = pl.BlockSpec(memory_space=pltpu.VMEM)
    return pl.pallas_call(
        body, name="reduce_grads",
        in_specs=[hbm] * n + [vmem], out_specs=[vmem] * (n + 1),
        out_shape=[jax.ShapeDtypeStruct(sh, F32) for sh in shapes] + [jax.ShapeDtypeStruct(small.shape, F32)],
        scratch_shapes=([pltpu.VMEM((3,) + sh, F32) for sh in shapes] + [pltpu.VMEM(sh, F32) for sh in shapes]
                        + [pltpu.VMEM(sh, F32) for sh in shapes]
                        + [pltpu.VMEM((8,) + small.shape, F32),
                           pltpu.SemaphoreType.DMA((n, 3)), pltpu.SemaphoreType.DMA((n, 3)),
                           pltpu.SemaphoreType.DMA((n,)), pltpu.SemaphoreType.DMA((n,)),
                           pltpu.SemaphoreType.DMA((n,)), pltpu.SemaphoreType.DMA((7,)),
                           pltpu.SemaphoreType.DMA((7,))]),
        compiler_params=pltpu.CompilerParams(vmem_limit_bytes=VMEM_LIMIT),
    )(*parts, small)


def _adamw(w, g, m, v, name):
    shape = w.shape
    w2, g2, m2, v2 = (a.reshape((-1, shape[-1])) for a in (w, g, m, v))

    def body(w_ref, g_ref, m_ref, v_ref, d_ref, nm_ref, nv_ref):
        gv = g_ref[...]
        nm = ADAM_B1 * m_ref[...] + (1.0 - ADAM_B1) * gv
        nv = ADAM_B2 * v_ref[...] + (1.0 - ADAM_B2) * (gv * gv)
        m_hat = nm / (1.0 - ADAM_B1 ** ADAM_STEP)
        v_hat = nv / (1.0 - ADAM_B2 ** ADAM_STEP)
        d_ref[...] = -ADAM_LR * (m_hat / (jnp.sqrt(v_hat) + ADAM_EPS) + ADAM_WD * w_ref[...])
        nm_ref[...] = nm
        nv_ref[...] = nv

    out = pl.pallas_call(
        body, name=name,
        out_shape=[jax.ShapeDtypeStruct(w2.shape, F32)] * 3,
        compiler_params=pltpu.CompilerParams(vmem_limit_bytes=VMEM_LIMIT),
    )(w2, g2, m2, v2)
    return tuple(a.reshape(shape) for a in out)


def kernel(x, meta_tokens, norm_g, w_in, q_norm_g, w_q_up, kv_norm_g, w_kv_up, conv_w, attn_out_g, conv_out_g, w_out, final_norm_g, loss_target, m_meta_tokens, m_norm_g, m_w_in, m_q_norm_g, m_w_q_up, m_kv_norm_g, m_w_kv_up, m_conv_w, m_attn_out_g, m_conv_out_g, m_w_out, m_final_norm_g, v_meta_tokens, v_norm_g, v_w_in, v_q_norm_g, v_w_q_up, v_kv_norm_g, v_w_kv_up, v_conv_w, v_attn_out_g, v_conv_out_g, v_w_out, v_final_norm_g):
    nb, s, _ = x.shape
    tm = ROW_TILE
    assert s % tm == 0 and tm % 8 == 0
    r = nb * s

    g_in, g_q, g_kv, g_out, g_cw, g_meta = _gather_shards([
        w_in[0].astype(BF16), w_q_up[0].astype(BF16), w_kv_up[0].astype(BF16), w_out[0].astype(BF16),
        conv_w[0], meta_tokens])
    w_in_f = jnp.transpose(g_in, (1, 0, 2)).reshape(D_MODEL, IN_DIM)
    w_in_p = jnp.concatenate([w_in_f[:, :448], jnp.zeros((D_MODEL, 64), BF16), w_in_f[:, 448:]], axis=1)
    wq_f = jnp.transpose(g_q, (1, 0, 2)).reshape(Q_RANK, HEADS, NOPE + ROPE)
    wq_p = jnp.pad(wq_f, ((0, 0), (0, 0), (0, QK_PAD - NOPE - ROPE))).reshape(Q_RANK, HEADS * QK_PAD)
    wkv_f = jnp.transpose(g_kv, (1, 0, 2)).reshape(KV_RANK, HEADS, 2, NOPE)
    wkv_p = jnp.transpose(wkv_f, (0, 2, 1, 3)).reshape(KV_RANK, 2 * HEADS * NOPE)
    w_out_f = g_out.reshape(D_MODEL, D_MODEL)
    conv_f = jnp.transpose(g_cw, (1, 0, 2)).reshape(3, CONV_W)
    meta_f = jnp.transpose(g_meta, (1, 0, 2)).reshape(N_META, D_MODEL)

    c_all, sa_all, sb_all = _rope_tables(N_META + s)
    tabs_m = (c_all[:N_META], sa_all[:N_META], sb_all[:N_META])
    tabs = (c_all[N_META:], sa_all[N_META:], sb_all[N_META:])
    gid = jnp.arange(CONV_W) // CONV_GROUP
    gmat = jnp.where(gid[:, None] == gid[None, :], 1.0 / CONV_GROUP, 0.0).astype(BF16)
    ga, gc = attn_out_g, conv_out_g
    gf = final_norm_g.reshape(1, D_MODEL)

    x2d = x.reshape(r, D_MODEL)
    tgt2d = loss_target.reshape(r, D_MODEL)

    pm, _, km, vm = _fwd_proj(meta_f, tabs_m, norm_g, w_in_p, q_norm_g, wq_p, kv_norm_g, wkv_p,
                              1, N_META, N_META, "fwd_proj_meta")
    p, q, k, v = _fwd_proj(x2d, tabs, norm_g, w_in_p, q_norm_g, wq_p, kv_norm_g, wkv_p, nb, s, tm, "fwd_proj")
    o, lse = _attn_fwd(q, k, v, km, vm, nb, s, tm)
    dh2, dycat, dw_out, dgf, loss_acc = _out_fwd_bwd(x2d, tgt2d, o, p, pm, conv_f, ga, gc, gmat, w_out_f, gf,
                                                     nb, s, tm)
    dpb, do, delta, dccm, dga, dgc, dcw = _gate_bwd(dycat, o, p, pm, conv_f, ga, gc, gmat, nb, s, tm)
    dq, dk, dv, dkm, dvm = _attn_bwd(q, k, v, do, lse, delta, km, vm, nb, s, tm)
    dpa, dpam, dwq_p, dwkv_p, dgq, dgkv = _up_bwd(dq, dk, dv, dkm, dvm, p, pm, tabs, tabs_m, wq_p, wkv_p,
                                                  q_norm_g, kv_norm_g, nb, s, tm)
    gx, gmeta, dw_in_p, dng = _in_bwd(x2d, dh2, dpa, dpb, meta_f, dpam, dccm, pm, w_in_p, norm_g, nb, s, tm)

    dw_in_f = jnp.concatenate([dw_in_p[:, :448], dw_in_p[:, 512:]], axis=1)
    p_in = jnp.transpose(dw_in_f.reshape(D_MODEL, 4, IN_DIM // 4), (1, 0, 2))
    p_q = jnp.transpose(dwq_p.reshape(Q_RANK, HEADS, QK_PAD)[:, :, :NOPE + ROPE], (1, 0, 2))
    dwkv_f = jnp.transpose(dwkv_p.reshape(KV_RANK, 2, HEADS, NOPE), (0, 2, 1, 3)).reshape(KV_RANK, 4, 2 * NOPE)
    p_kv = jnp.transpose(dwkv_f, (1, 0, 2))
    p_out = dw_out.reshape(4, D_MODEL // 4, D_MODEL)
    flat = jnp.concatenate([dng.reshape(-1), dgq.reshape(-1), dgkv.reshape(-1), dga.reshape(-1), dgc.reshape(-1),
                            dgf.reshape(-1), dcw[:3].reshape(-1), gmeta.reshape(-1)])
    n_small = flat.shape[0]
    rows_small = -(-n_small // 1024) * 8
    small = jnp.pad(flat, (0, rows_small * 128 - n_small)).reshape(rows_small, 128)
    g_w_in, g_w_q, g_w_kv, g_w_out, small_sum = _reduce_grads([p_in, p_q, p_kv, p_out], small)
    ssum = small_sum.reshape(-1)

    def take(off, n):
        return ssum[off:off + n], off + n

    off = 0
    g_norm, off = take(off, D_MODEL)
    g_qn, off = take(off, Q_RANK)
    g_kvn, off = take(off, KV_RANK)
    g_ga, off = take(off, CONV_W)
    g_gc, off = take(off, CONV_W)
    g_gf, off = take(off, D_MODEL)
    g_cw_all, off = take(off, 3 * CONV_W)
    g_meta_all, off = take(off, N_META * D_MODEL)
    chip = 2 * lax.axis_index("x") + lax.axis_index("y")
    g_conv = lax.dynamic_slice(g_cw_all.reshape(3, CONV_W), (0, chip * 128), (3, 128))
    g_mt = lax.dynamic_slice(g_meta_all.reshape(N_META, D_MODEL), (0, chip * 256), (N_META, 256))

    grads = {
        "meta_tokens": g_mt, "norm_g": g_norm.reshape(1, -1), "w_in": g_w_in[None], "q_norm_g": g_qn.reshape(1, -1),
        "w_q_up": g_w_q[None], "kv_norm_g": g_kvn.reshape(1, -1), "w_kv_up": g_w_kv[None], "conv_w": g_conv[None],
        "attn_out_g": g_ga.reshape(1, -1), "conv_out_g": g_gc.reshape(1, -1), "w_out": g_w_out[None],
        "final_norm_g": g_gf,
    }
    weights = {
        "meta_tokens": (meta_tokens, m_meta_tokens, v_meta_tokens), "norm_g": (norm_g, m_norm_g, v_norm_g),
        "w_in": (w_in, m_w_in, v_w_in), "q_norm_g": (q_norm_g, m_q_norm_g, v_q_norm_g),
        "w_q_up": (w_q_up, m_w_q_up, v_w_q_up), "kv_norm_g": (kv_norm_g, m_kv_norm_g, v_kv_norm_g),
        "w_kv_up": (w_kv_up, m_w_kv_up, v_w_kv_up), "conv_w": (conv_w, m_conv_w, v_conv_w),
        "attn_out_g": (attn_out_g, m_attn_out_g, v_attn_out_g), "conv_out_g": (conv_out_g, m_conv_out_g, v_conv_out_g),
        "w_out": (w_out, m_w_out, v_w_out), "final_norm_g": (final_norm_g, m_final_norm_g, v_final_norm_g),
    }
    names = list(weights)
    deltas, new_m, new_v = [], [], []
    for nme in names:
        w_, m_, v_ = weights[nme]
        g_ = grads[nme].reshape(w_.shape)
        grads[nme] = g_
        d_, nm_, nv_ = _adamw(w_, g_, m_, v_, "adamw_" + nme)
        deltas.append(d_)
        new_m.append(nm_)
        new_v.append(nv_)

    loss = lax.psum(loss_acc[0, 0], ("x", "y", "c"))
    grad_x = gx.reshape(nb, s, D_MODEL)
    return (loss, grad_x, *[grads[nme] for nme in names], *deltas, *new_m, *new_v)
```

```python
import functools

import jax
import jax.numpy as jnp
from jax import lax
from jax.experimental import pallas as pl
from jax.experimental.pallas import tpu as pltpu

F32 = jnp.float32
BF16 = jnp.bfloat16

D_MODEL = 1024
N_META = 16
HEADS = 4
NOPE = 128
ROPE = 64
VDIM = 128
QK_PAD = 256
Q_RANK = 256
KV_RANK = 128
CONV_W = 512
CONV_GROUP = 64
ROPE_THETA = 10000.0
EPS = 1e-6
ATTN_SCALE = (NOPE + ROPE) ** -0.5
IN_DIM = 3008
IN_PAD = 3072
BLK_ZA, BLK_CB, BLK_CC, BLK_CH, BLK_ZC = 1, 2, 3, 4, 5
NEG_INF = -1e30

ADAM_LR = 0.001
ADAM_B1 = 0.9
ADAM_B2 = 0.999
ADAM_EPS = 1e-08
ADAM_WD = 0.01
ADAM_STEP = 10

ROW_TILE = 256
VMEM_LIMIT = 56 * 1024 * 1024

NT = (((1,), (1,)), ((), ()))
TN = (((0,), (0,)), ((), ()))


def _cparams(*sem):
    return pltpu.CompilerParams(dimension_semantics=sem, vmem_limit_bytes=VMEM_LIMIT)


def _dot(a, b):
    return jnp.dot(a, b, preferred_element_type=F32)


def _dot_nt(a, b):
    return lax.dot_general(a, b, NT, preferred_element_type=F32)


def _dot_tn(a, b):
    return lax.dot_general(a, b, TN, preferred_element_type=F32)


def _rms(x, g):
    r = lax.rsqrt(jnp.mean(x * x, axis=-1, keepdims=True) + EPS)
    return x * r * g, r


def _rms_bwd(dy, x, r, g):
    xh = x * r
    dyg = dy * g
    dx = r * (dyg - xh * jnp.mean(dyg * xh, axis=-1, keepdims=True))
    return dx, dy * xh


def _sigmoid(z):
    return 1.0 / (1.0 + jnp.exp(-z))


def _rope(b, c, sa, sb):
    return b * c + pltpu.roll(b, 96, 1) * sa + pltpu.roll(b, 32, 1) * sb


def _rope_bwd(d, c, sa, sb):
    return d * c + pltpu.roll(d * sa, 32, 1) + pltpu.roll(d * sb, 96, 1)


def _group_mean(x, gmat):
    hi = x.astype(BF16)
    lo = (x - hi.astype(F32)).astype(BF16)
    return _dot(hi, gmat) + _dot(lo, gmat)


def _row_of(col, rows):
    return jnp.transpose(jnp.broadcast_to(col, (rows, 128)))[0:1, :]


def _rope_tables(n_pos):
    half = ROPE // 2
    inv_freq = 1.0 / (ROPE_THETA ** (jnp.arange(half, dtype=F32) / half))
    ang = jnp.arange(n_pos, dtype=F32)[:, None] * inv_freq[None, :]
    cos, sin, z = jnp.cos(ang), jnp.sin(ang), jnp.zeros((n_pos, half), F32)
    c = jnp.concatenate([cos, cos, z, z], axis=1)
    sa = jnp.concatenate([-sin, z, z, z], axis=1)
    sb = jnp.concatenate([z, sin, z, z], axis=1)
    return c, sa, sb


def _fwd_proj(x2d, tabs, norm_g, w_in_p, q_norm_g, wq_p, kv_norm_g, wkv_p, nb, s, tm, name):
    nt = s // tm
    c_t, sa_t, sb_t = tabs

    def body(x_ref, c_ref, sa_ref, sb_ref, g_ref, w_ref, gq_ref, wq_ref, gkv_ref, wkv_ref,
             p_ref, q_ref, k_ref, v_ref):
        u, _ = _rms(x_ref[...], g_ref[...])
        p = _dot(u.astype(BF16), w_ref[...])
        p_ref[...] = p
        c, sa, sb = c_ref[...], sa_ref[...], sb_ref[...]
        qn, _ = _rms(p[:, 0:Q_RANK], gq_ref[...])
        q = _dot(qn.astype(BF16), wq_ref[...])
        kvn, _ = _rms(p[:, Q_RANK:Q_RANK + KV_RANK], gkv_ref[...])
        kv = _dot(kvn.astype(BF16), wkv_ref[...])
        kpe = _rope(p[:, 384:512], c, sa, sb)
        for h in range(HEADS):
            pe = _rope(q[:, QK_PAD * h + NOPE:QK_PAD * (h + 1)], c, sa, sb)
            qh = jnp.concatenate([q[:, QK_PAD * h:QK_PAD * h + NOPE], pe], axis=1)
            q_ref[0, h] = (qh * ATTN_SCALE).astype(BF16)
            k_ref[0, h] = jnp.concatenate([kv[:, NOPE * h:NOPE * (h + 1)], kpe], axis=1).astype(BF16)
            v_ref[0, h] = kv[:, 512 + VDIM * h:512 + VDIM * (h + 1)].astype(BF16)

    full = lambda a: pl.BlockSpec(a.shape, lambda i: (0,) * a.ndim)
    tab = pl.BlockSpec((tm, 128), lambda i: (i % nt, 0))
    return pl.pallas_call(
        body, name=name, grid=(nb * nt,),
        in_specs=[pl.BlockSpec((tm, D_MODEL), lambda i: (i, 0)), tab, tab, tab,
                  full(norm_g), full(w_in_p), full(q_norm_g), full(wq_p), full(kv_norm_g), full(wkv_p)],
        out_specs=[pl.BlockSpec((tm, IN_PAD), lambda i: (i, 0)),
                   pl.BlockSpec((1, HEADS, tm, QK_PAD), lambda i: (i // nt, 0, i % nt, 0)),
                   pl.BlockSpec((1, HEADS, tm, QK_PAD), lambda i: (i // nt, 0, i % nt, 0)),
                   pl.BlockSpec((1, HEADS, tm, VDIM), lambda i: (i // nt, 0, i % nt, 0))],
        out_shape=[jax.ShapeDtypeStruct((nb * s, IN_PAD), F32),
                   jax.ShapeDtypeStruct((nb, HEADS, s, QK_PAD), BF16),
                   jax.ShapeDtypeStruct((nb, HEADS, s, QK_PAD), BF16),
                   jax.ShapeDtypeStruct((nb, HEADS, s, VDIM), BF16)],
        compiler_params=_cparams("parallel"),
    )(x2d, c_t, sa_t, sb_t, norm_g, w_in_p, q_norm_g, wq_p, kv_norm_g, wkv_p)


def _attn_fwd(q, k, v, km, vm, nb, s, tq):
    nq = s // tq

    def body(q_ref, k_ref, v_ref, km_ref, vm_ref, o_ref, lse_ref, s_scr, p_scr):
        row = lax.broadcasted_iota(jnp.int32, (tq, tq), 0)
        col = lax.broadcasted_iota(jnp.int32, (tq, tq), 1)
        for i in range(nq):
            slot = i % 2
            qi = q_ref[0, 0, i * tq:(i + 1) * tq, :]
            sm = _dot_nt(qi, km_ref[0, 0])
            m128 = None
            for j in range(i + 1):
                sc = _dot_nt(qi, k_ref[0, 0, j * tq:(j + 1) * tq, :])
                if j == i:
                    sc = jnp.where(col <= row, sc, NEG_INF)
                s_scr[slot, :, j * tq:(j + 1) * tq] = sc
                mx = sc[:, 0:128]
                for c0 in range(128, tq, 128):
                    mx = jnp.maximum(mx, sc[:, c0:c0 + 128])
                m128 = mx if m128 is None else jnp.maximum(m128, mx)
            m = jnp.maximum(jnp.max(m128, axis=1, keepdims=True), jnp.max(sm, axis=1, keepdims=True))
            pm = jnp.exp(sm - m)
            l128 = None
            for j in range(i + 1):
                p = jnp.exp(s_scr[slot, :, j * tq:(j + 1) * tq] - m)
                p_scr[slot, :, j * tq:(j + 1) * tq] = p.astype(BF16)
                ps = p[:, 0:128]
                for c0 in range(128, tq, 128):
                    ps = ps + p[:, c0:c0 + 128]
                l128 = ps if l128 is None else l128 + ps
            l = jnp.sum(l128, axis=1, keepdims=True) + jnp.sum(pm, axis=1, keepdims=True)
            n = (i + 1) * tq
            acc = _dot(p_scr[slot, :, 0:n], v_ref[0, 0, 0:n, :]) + _dot(pm.astype(BF16), vm_ref[0, 0])
            o_ref[0, 0, i * tq:(i + 1) * tq, :] = acc / l
            lse_ref[0, 0, :, i * tq:(i + 1) * tq] = _row_of(m + jnp.log(l), tq)

    hblk = lambda w: pl.BlockSpec((1, 1, s, w), lambda b, h: (b, h, 0, 0))
    mblk = lambda w: pl.BlockSpec((1, 1, N_META, w), lambda b, h: (0, h, 0, 0))
    return pl.pallas_call(
        body, name="attn_fwd", grid=(nb, HEADS),
        in_specs=[hblk(QK_PAD), hblk(QK_PAD), hblk(VDIM), mblk(QK_PAD), mblk(VDIM)],
        out_specs=[hblk(VDIM), pl.BlockSpec((1, 1, 1, s), lambda b, h: (b, h, 0, 0))],
        out_shape=[jax.ShapeDtypeStruct((nb, HEADS, s, VDIM), F32),
                   jax.ShapeDtypeStruct((nb, HEADS, 1, s), F32)],
        scratch_shapes=[pltpu.VMEM((2, tq, s), F32), pltpu.VMEM((2, tq, s), BF16)],
        compiler_params=_cparams("parallel", "parallel"),
    )(q, k, v, km, vm)


def _shift_rows(a, prev, n_rows):
    rid = lax.broadcasted_iota(jnp.int32, a.shape, 0)
    a1 = jnp.where(rid == 0, prev[7:8, :], pltpu.roll(a, 1, 0))
    a2 = jnp.where(rid == 0, prev[6:7, :], jnp.where(rid == 1, prev[7:8, :], pltpu.roll(a, 2, 0)))
    return a1, a2


def _attn_gate(o, za, ga_h):
    on, r = _rms(o, ga_h)
    return on * (za * _sigmoid(za)), on, r


def _out_fwd_bwd(x2d, tgt2d, o, p, pm, conv_w, ga, gc, gmat, w_out, gf, nb, s, tm):
    nt = s // tm
    r = nb * s
    prev_idx = lambda i: jnp.maximum(i * (tm // 8) - 1, 0)

    def body(x_ref, t_ref, o_ref, za_ref, cb_ref, cc_ref, ch_ref, zc_ref, ccp_ref, chp_ref, mc_ref, mh_ref,
             cw_ref, ga_ref, gc_ref, gm_ref, w_ref, gf_ref,
             dh_ref, dy_ref, dw_ref, dgf_ref, loss_ref):
        i = pl.program_id(0)

        @pl.when(i == 0)
        def _():
            dw_ref[...] = jnp.zeros_like(dw_ref)
            dgf_ref[...] = jnp.zeros_like(dgf_ref)
            loss_ref[...] = jnp.zeros_like(loss_ref)

        ya = []
        for h in range(HEADS):
            y, _, _ = _attn_gate(o_ref[0, h], za_ref[:, VDIM * h:VDIM * (h + 1)],
                                 ga_ref[:, VDIM * h:VDIM * (h + 1)])
            ya.append(y)
        cc = cc_ref[...] * ch_ref[...]
        prev = jnp.where(i % nt == 0, mc_ref[8:16, :] * mh_ref[8:16, :], ccp_ref[...] * chp_ref[...])
        cc1, cc2 = _shift_rows(cc, prev, tm)
        yc = cb_ref[...] * (cw_ref[0:1, :] * cc2 + cw_ref[1:2, :] * cc1 + cw_ref[2:3, :] * cc)
        rg = lax.rsqrt(_group_mean(yc * yc, gm_ref[...]) + EPS)
        zc = zc_ref[...]
        yconv = yc * rg * gc_ref[...] * (zc * _sigmoid(zc))
        ycat = jnp.concatenate(ya + [yconv], axis=1).astype(BF16)
        h2 = x_ref[...] + _dot(ycat, w_ref[...])
        gfv = gf_ref[...]
        y, r2 = _rms(h2, gfv)
        e = y - t_ref[...]
        loss_ref[...] += 0.5 * jnp.sum(e * e) / D_MODEL
        dyv = e * (1.0 / D_MODEL)
        dh2, dgf = _rms_bwd(dyv, h2, r2, gfv)
        dgf_ref[...] += jnp.sum(dgf, axis=0, keepdims=True)
        dh_ref[...] = dh2
        dhb = dh2.astype(BF16)
        dy_ref[...] = _dot_nt(dhb, w_ref[...])
        dw_ref[...] += _dot_tn(ycat, dhb)

    row = lambda w, j: pl.BlockSpec((tm, w), lambda i: (i, j))
    pblk = lambda j: pl.BlockSpec((tm, 512), lambda i: (i, j))
    pprev = lambda j: pl.BlockSpec((8, 512), lambda i: (prev_idx(i), j))
    mblk = lambda j: pl.BlockSpec((N_META, 512), lambda i: (0, j))
    full = lambda a: pl.BlockSpec(a.shape, lambda i: (0,) * a.ndim)
    return pl.pallas_call(
        body, name="out_fwd_bwd", grid=(nb * nt,),
        in_specs=[row(D_MODEL, 0), row(D_MODEL, 0),
                  pl.BlockSpec((1, HEADS, tm, VDIM), lambda i: (i // nt, 0, i % nt, 0)),
                  pblk(BLK_ZA), pblk(BLK_CB), pblk(BLK_CC), pblk(BLK_CH), pblk(BLK_ZC),
                  pprev(BLK_CC), pprev(BLK_CH), mblk(BLK_CC), mblk(BLK_CH),
                  full(conv_w), full(ga), full(gc), full(gmat), full(w_out), full(gf)],
        out_specs=[row(D_MODEL, 0), row(D_MODEL, 0),
                   pl.BlockSpec((D_MODEL, D_MODEL), lambda i: (0, 0)),
                   pl.BlockSpec((1, D_MODEL), lambda i: (0, 0)),
                   pl.BlockSpec((1, 128), lambda i: (0, 0))],
        out_shape=[jax.ShapeDtypeStruct((r, D_MODEL), F32), jax.ShapeDtypeStruct((r, D_MODEL), F32),
                   jax.ShapeDtypeStruct((D_MODEL, D_MODEL), F32), jax.ShapeDtypeStruct((1, D_MODEL), F32),
                   jax.ShapeDtypeStruct((1, 128), F32)],
        compiler_params=_cparams("arbitrary"),
    )(x2d, tgt2d, o, p, p, p, p, p, p, p, pm, pm, conv_w, ga, gc, gmat, w_out, gf)


def _gate_bwd(dycat, o, p, pm, conv_w, ga, gc, gmat, nb, s, tm):
    nt = s // tm
    r = nb * s
    ext = tm + 8
    prev_idx = lambda i: jnp.maximum(i * (tm // 8) - 1, 0)
    next_idx = lambda i: jnp.minimum((i + 1) * (tm // 8), r // 8 - 1)

    def body(dya_ref, dyc_ref, dycn_ref, o_ref, za_ref, cb_ref, cbn_ref, cc_ref, ccp_ref, ccn_ref,
             ch_ref, chp_ref, chn_ref, zc_ref, zcn_ref, mc_ref, mh_ref, cw_ref, ga_ref, gc_ref, gm_ref,
             dpb_ref, do_ref, dl_ref, dccm_ref, dga_ref, dgc_ref, dcw_ref):
        i = pl.program_id(0)

        @pl.when(i == 0)
        def _():
            dga_ref[...] = jnp.zeros_like(dga_ref)
            dgc_ref[...] = jnp.zeros_like(dgc_ref)
            dcw_ref[...] = jnp.zeros_like(dcw_ref)

        dga = []
        for h in range(HEADS):
            hs = slice(VDIM * h, VDIM * (h + 1))
            oh, za, gah, dya = o_ref[0, h], za_ref[:, hs], ga_ref[:, hs], dya_ref[:, hs]
            sg = _sigmoid(za)
            on, ro = _rms(oh, gah)
            don = dya * (za * sg)
            dpb_ref[:, hs] = dya * on * (sg * (1.0 + za * (1.0 - sg)))
            do, dg = _rms_bwd(don, oh, ro, gah)
            dga.append(jnp.sum(dg, axis=0, keepdims=True))
            dob = do.astype(BF16)
            do_ref[0, h] = dob
            dl_ref[0, h] = _row_of(jnp.sum(dob.astype(F32) * oh, axis=1, keepdims=True), tm)
        dga_ref[...] += jnp.concatenate(dga, axis=1)

        cat = lambda a, b: jnp.concatenate([a[...], b[...]], axis=0)
        cch = cat(cc_ref, ccn_ref)
        chh = cat(ch_ref, chn_ref)
        cb = cat(cb_ref, cbn_ref)
        zc = cat(zc_ref, zcn_ref)
        dy = cat(dyc_ref, dycn_ref)
        first = i % nt == 0
        last = i % nt == nt - 1
        cc = cch * chh
        prev = jnp.where(first, mc_ref[8:16, :] * mh_ref[8:16, :], ccp_ref[...] * chp_ref[...])
        cc1, cc2 = _shift_rows(cc, prev, ext)
        w0, w1, w2 = cw_ref[0:1, :], cw_ref[1:2, :], cw_ref[2:3, :]
        dw = w0 * cc2 + w1 * cc1 + w2 * cc
        yc = cb * dw
        rg = lax.rsqrt(_group_mean(yc * yc, gm_ref[...]) + EPS)
        ych = yc * rg
        gcv = gc_ref[...]
        sg = _sigmoid(zc)
        dycn = dy * (zc * sg)
        dzc = dy * (ych * gcv) * (sg * (1.0 + zc * (1.0 - sg)))
        dgc_ref[...] += jnp.sum((dycn * ych)[:tm], axis=0, keepdims=True)
        dycg = dycn * gcv
        dyc = rg * (dycg - ych * _group_mean(dycg * ych, gm_ref[...]))
        rid = lax.broadcasted_iota(jnp.int32, (ext, CONV_W), 0)
        ddw = jnp.where(jnp.logical_and(last, rid >= tm), 0.0, dyc * cb)
        dcb = dyc * dw
        dcc = w2 * ddw + w1 * pltpu.roll(ddw, ext - 1, 0) + w0 * pltpu.roll(ddw, ext - 2, 0)
        dpb_ref[:, 512:1024] = dcb[:tm]
        dpb_ref[:, 1024:1536] = (dcc * chh)[:tm]
        dpb_ref[:, 1536:2048] = (dcc * cch)[:tm]
        dpb_ref[:, 2048:2560] = dzc[:tm]
        rs = lambda a: jnp.sum(a[:tm], axis=0, keepdims=True)
        dcw_ref[0:1, :] += rs(ddw * cc2)
        dcw_ref[1:2, :] += rs(ddw * cc1)
        dcw_ref[2:3, :] += rs(ddw * cc)

        @pl.when(first)
        def _():
            d0, d1 = ddw[0:1, :], ddw[1:2, :]
            r8 = lax.broadcasted_iota(jnp.int32, (8, CONV_W), 0)
            dccm_ref[0] = jnp.where(r8 == 7, w1 * d0 + w0 * d1, jnp.where(r8 == 6, w0 * d0, 0.0))

    row = lambda j: pl.BlockSpec((tm, 512), lambda i: (i, j))
    prv = lambda j: pl.BlockSpec((8, 512), lambda i: (prev_idx(i), j))
    nxt = lambda j: pl.BlockSpec((8, 512), lambda i: (next_idx(i), j))
    mblk = lambda j: pl.BlockSpec((N_META, 512), lambda i: (0, j))
    full = lambda a: pl.BlockSpec(a.shape, lambda i: (0,) * a.ndim)
    hb = lambda w: pl.BlockSpec((1, HEADS, tm, w), lambda i: (i // nt, 0, i % nt, 0))
    acc = lambda rr: pl.BlockSpec((rr, 512), lambda i: (0, 0))
    return pl.pallas_call(
        body, name="gate_bwd", grid=(nb * nt,),
        in_specs=[row(0), row(1), nxt(1), hb(VDIM),
                  row(BLK_ZA), row(BLK_CB), nxt(BLK_CB), row(BLK_CC), prv(BLK_CC), nxt(BLK_CC),
                  row(BLK_CH), prv(BLK_CH), nxt(BLK_CH), row(BLK_ZC), nxt(BLK_ZC),
                  mblk(BLK_CC), mblk(BLK_CH), full(conv_w), full(ga), full(gc), full(gmat)],
        out_specs=[pl.BlockSpec((tm, 2560), lambda i: (i, 0)), hb(VDIM),
                   pl.BlockSpec((1, HEADS, 1, tm), lambda i: (i // nt, 0, 0, i % nt)),
                   pl.BlockSpec((1, 8, 512), lambda i: (i // nt, 0, 0)),
                   acc(1), acc(1), acc(8)],
        out_shape=[jax.ShapeDtypeStruct((r, 2560), F32), jax.ShapeDtypeStruct((nb, HEADS, s, VDIM), BF16),
                   jax.ShapeDtypeStruct((nb, HEADS, 1, s), F32), jax.ShapeDtypeStruct((nb, 8, 512), F32),
                   jax.ShapeDtypeStruct((1, 512), F32), jax.ShapeDtypeStruct((1, 512), F32),
                   jax.ShapeDtypeStruct((8, 512), F32)],
        compiler_params=_cparams("arbitrary"),
    )(dycat, dycat, dycat, o, p, p, p, p, p, p, p, p, p, p, p, pm, pm, conv_w, ga, gc, gmat)


def _attn_bwd(q, k, v, do, lse, delta, km, vm, nb, s, t):
    n = s // t

    def body(q_ref, k_ref, v_ref, do_ref, lse_ref, dl_ref, km_ref, vm_ref,
             dq_ref, dk_ref, dv_ref, dkm_ref, dvm_ref, p_scr, ds_scr):
        b = pl.program_id(1)

        @pl.when(b == 0)
        def _():
            dkm_ref[...] = jnp.zeros_like(dkm_ref)
            dvm_ref[...] = jnp.zeros_like(dvm_ref)

        kr = lax.broadcasted_iota(jnp.int32, (t, t), 0)
        qc = lax.broadcasted_iota(jnp.int32, (t, t), 1)
        km_v, vm_v = km_ref[0, 0], vm_ref[0, 0]
        ptm = jnp.exp(_dot_nt(km_v, q_ref[0, 0]) - lse_ref[0, 0])
        dstm = (ptm * (_dot_nt(vm_v, do_ref[0, 0]) - dl_ref[0, 0])).astype(BF16)
        dkm_ref[0] += _dot(dstm, q_ref[0, 0])
        dvm_ref[0] += _dot(ptm.astype(BF16), do_ref[0, 0])
        dq_ref[0, 0] = _dot_tn(dstm, km_v)
        for j in range(n):
            slot = j % 2
            kj = k_ref[0, 0, j * t:(j + 1) * t, :]
            vj = v_ref[0, 0, j * t:(j + 1) * t, :]
            for i in range(j, n):
                cs = slice(i * t, (i + 1) * t)
                qi = q_ref[0, 0, cs, :]
                doi = do_ref[0, 0, cs, :]
                st = _dot_nt(kj, qi)
                if i == j:
                    st = jnp.where(kr <= qc, st, NEG_INF)
                pt = jnp.exp(st - lse_ref[0, 0, :, cs])
                dst = (pt * (_dot_nt(vj, doi) - dl_ref[0, 0, :, cs])).astype(BF16)
                p_scr[slot, :, cs] = pt.astype(BF16)
                ds_scr[slot, :, cs] = dst
                dq_ref[0, 0, cs, :] += _dot_tn(dst, kj)
            dv_ref[0, 0, j * t:(j + 1) * t, :] = _dot(p_scr[slot, :, j * t:s], do_ref[0, 0, j * t:s, :])
            dk_ref[0, 0, j * t:(j + 1) * t, :] = _dot(ds_scr[slot, :, j * t:s], q_ref[0, 0, j * t:s, :])

    big = lambda w: pl.BlockSpec((1, 1, s, w), lambda h, b: (b, h, 0, 0))
    rowv = pl.BlockSpec((1, 1, 1, s), lambda h, b: (b, h, 0, 0))
    mk = lambda w: pl.BlockSpec((1, 1, N_META, w), lambda h, b: (0, h, 0, 0))
    mo = lambda w: pl.BlockSpec((1, N_META, w), lambda h, b: (h, 0, 0))
    return pl.pallas_call(
        body, name="attn_bwd", grid=(HEADS, nb),
        in_specs=[big(QK_PAD), big(QK_PAD), big(VDIM), big(VDIM), rowv, rowv, mk(QK_PAD), mk(VDIM)],
        out_specs=[big(QK_PAD), big(QK_PAD), big(VDIM), mo(QK_PAD), mo(VDIM)],
        out_shape=[jax.ShapeDtypeStruct((nb, HEADS, s, QK_PAD), F32),
                   jax.ShapeDtypeStruct((nb, HEADS, s, QK_PAD), F32),
                   jax.ShapeDtypeStruct((nb, HEADS, s, VDIM), F32),
                   jax.ShapeDtypeStruct((HEADS, N_META, QK_PAD), F32),
                   jax.ShapeDtypeStruct((HEADS, N_META, VDIM), F32)],
        scratch_shapes=[pltpu.VMEM((2, t, s), BF16), pltpu.VMEM((2, t, s), BF16)],
        compiler_params=_cparams("arbitrary", "arbitrary"),
    )(q, k, v, do, lse, delta, km, vm)


def _up_bwd(dq, dk, dv, dkm, dvm, p, pm, tabs, tabs_m, wq_p, wkv_p, gq, gkv, nb, s, tm):
    nt = s // tm
    n = nb * nt
    c_t, sa_t, sb_t = tabs
    cm_t, sam_t, sbm_t = tabs_m

    def kv_path(dkh, dvh, pa, c, sa, sb, wkv, gkvv):
        dkpe = dkh[0][:, NOPE:]
        for h in range(1, HEADS):
            dkpe = dkpe + dkh[h][:, NOPE:]
        dkr = _rope_bwd(dkpe, c, sa, sb)
        dkv = jnp.concatenate([d[:, :NOPE] for d in dkh] + list(dvh), axis=1).astype(BF16)
        ckv = pa[:, Q_RANK:Q_RANK + KV_RANK]
        kvn, rkv = _rms(ckv, gkvv)
        dckv, dg = _rms_bwd(_dot_nt(dkv, wkv), ckv, rkv, gkvv)
        return dckv, dkr, kvn.astype(BF16), dkv, jnp.sum(dg, axis=0, keepdims=True)

    def body(dq_ref, dk_ref, dv_ref, pa_ref, c_ref, sa_ref, sb_ref,
             dkm_ref, dvm_ref, pam_ref, cm_ref, sam_ref, sbm_ref,
             wq_ref, wkv_ref, gq_ref, gkv_ref,
             dpa_ref, dpam_ref, dwq_ref, dwkv_ref, dgq_ref, dgkv_ref):
        i = pl.program_id(0)

        @pl.when(i == 0)
        def _():
            dwq_ref[...] = jnp.zeros_like(dwq_ref)
            dwkv_ref[...] = jnp.zeros_like(dwkv_ref)
            dgq_ref[...] = jnp.zeros_like(dgq_ref)
            dgkv_ref[...] = jnp.zeros_like(dgkv_ref)

        @pl.when(i < n)
        def _():
            c, sa, sb = c_ref[...], sa_ref[...], sb_ref[...]
            pa = pa_ref[...]
            parts = []
            for h in range(HEADS):
                dqh = dq_ref[0, h] * ATTN_SCALE
                parts += [dqh[:, :NOPE], _rope_bwd(dqh[:, NOPE:], c, sa, sb)]
            dql = jnp.concatenate(parts, axis=1).astype(BF16)
            cq = pa[:, 0:Q_RANK]
            gqv = gq_ref[...]
            qn, rq = _rms(cq, gqv)
            dwq_ref[...] += _dot_tn(qn.astype(BF16), dql)
            dcq, dg = _rms_bwd(_dot_nt(dql, wq_ref[...]), cq, rq, gqv)
            dgq_ref[...] += jnp.sum(dg, axis=0, keepdims=True)
            dckv, dkr, kvn, dkv, dgk = kv_path([dk_ref[0, h] for h in range(HEADS)],
                                               [dv_ref[0, h] for h in range(HEADS)],
                                               pa, c, sa, sb, wkv_ref[...], gkv_ref[...])
            dwkv_ref[...] += _dot_tn(kvn, dkv)
            dgkv_ref[...] += dgk
            dpa_ref[...] = jnp.concatenate([dcq, dckv, dkr], axis=1)

        @pl.when(i == n)
        def _():
            dckv, dkr, kvn, dkv, dgk = kv_path([dkm_ref[h] for h in range(HEADS)],
                                               [dvm_ref[h] for h in range(HEADS)],
                                               pam_ref[...], cm_ref[...], sam_ref[...], sbm_ref[...],
                                               wkv_ref[...], gkv_ref[...])
            dwkv_ref[...] += _dot_tn(kvn, dkv)
            dgkv_ref[...] += dgk
            dpam_ref[...] = jnp.concatenate([jnp.zeros((N_META, Q_RANK), F32), dckv, dkr], axis=1)

    cl = lambda i: jnp.minimum(i, n - 1)
    hb = lambda w: pl.BlockSpec((1, HEADS, tm, w), lambda i: (cl(i) // nt, 0, cl(i) % nt, 0))
    tab = pl.BlockSpec((tm, 128), lambda i: (cl(i) % nt, 0))
    full = lambda a: pl.BlockSpec(a.shape, lambda i: (0,) * a.ndim)
    const = lambda shape: pl.BlockSpec(shape, lambda i: (0,) * len(shape))
    return pl.pallas_call(
        body, name="up_bwd", grid=(n + 1,),
        in_specs=[hb(QK_PAD), hb(QK_PAD), hb(VDIM), pl.BlockSpec((tm, 512), lambda i: (cl(i), 0)), tab, tab, tab,
                  full(dkm), full(dvm), pl.BlockSpec((N_META, 512), lambda i: (0, 0)),
                  full(cm_t), full(sam_t), full(sbm_t), full(wq_p), full(wkv_p), full(gq), full(gkv)],
        out_specs=[pl.BlockSpec((tm, 512), lambda i: (cl(i), 0)), const((N_META, 512)),
                   const((Q_RANK, HEADS * QK_PAD)), const((KV_RANK, 1024)), const((1, Q_RANK)), const((1, KV_RANK))],
        out_shape=[jax.ShapeDtypeStruct((nb * s, 512), F32), jax.ShapeDtypeStruct((N_META, 512), F32),
                   jax.ShapeDtypeStruct((Q_RANK, HEADS * QK_PAD), F32), jax.ShapeDtypeStruct((KV_RANK, 1024), F32),
                   jax.ShapeDtypeStruct((1, Q_RANK), F32), jax.ShapeDtypeStruct((1, KV_RANK), F32)],
        compiler_params=_cparams("arbitrary"),
    )(dq, dk, dv, p, c_t, sa_t, sb_t, dkm, dvm, pm, cm_t, sam_t, sbm_t, wq_p, wkv_p, gq, gkv)


def _in_bwd(x2d, dh2, dpa, dpb, meta, dpam, dccm, pm, w_in_p, norm_g, nb, s, tm):
    nt = s // tm
    n = nb * nt

    def body(x_ref, dh_ref, dpa_ref, dpb_ref, mt_ref, dpam_ref, dccm_ref, mc_ref, mh_ref, w_ref, g_ref,
             gx_ref, gm_ref, dw_hbm, dg_ref, acc_ref, sem):
        i = pl.program_id(0)

        @pl.when(i == 0)
        def _():
            acc_ref[...] = jnp.zeros_like(acc_ref)
            dg_ref[...] = jnp.zeros_like(dg_ref)

        def rows(x, dp, dres):
            g = g_ref[...]
            u, r1 = _rms(x, g)
            dpb16 = dp.astype(BF16)
            acc_ref[...] += _dot_tn(u.astype(BF16), dpb16)
            dx, dg = _rms_bwd(_dot_nt(dpb16, w_ref[...]), x, r1, g)
            dg_ref[...] += jnp.sum(dg, axis=0, keepdims=True)
            return dx if dres is None else dx + dres

        @pl.when(i < n)
        def _():
            dp = jnp.concatenate([dpa_ref[...], dpb_ref[...]], axis=1)
            gx_ref[...] = rows(x_ref[...], dp, dh_ref[...])

        @pl.when(i == n)
        def _():
            dcc = dccm_ref[0]
            for b in range(1, nb):
                dcc = dcc + dccm_ref[b]
            z8 = jnp.zeros((8, CONV_W), F32)
            dc = jnp.concatenate([z8, dcc * mh_ref[8:16, :]], axis=0)
            dh = jnp.concatenate([z8, dcc * mc_ref[8:16, :]], axis=0)
            z = jnp.zeros((N_META, CONV_W), F32)
            dp = jnp.concatenate([dpam_ref[...], z, z, dc, dh, z], axis=1)
            gm_ref[...] = rows(mt_ref[...], dp, None)
            cp = pltpu.make_async_copy(acc_ref, dw_hbm, sem)
            cp.start()
            cp.wait()

    cl = lambda i: jnp.minimum(i, n - 1)
    row = lambda w: pl.BlockSpec((tm, w), lambda i: (cl(i), 0))
    full = lambda a: pl.BlockSpec(a.shape, lambda i: (0,) * a.ndim)
    mblk = lambda j: pl.BlockSpec((N_META, 512), lambda i: (0, j))
    return pl.pallas_call(
        body, name="in_bwd", grid=(n + 1,),
        in_specs=[row(D_MODEL), row(D_MODEL), row(512), row(2560), full(meta), full(dpam), full(dccm),
                  mblk(BLK_CC), mblk(BLK_CH), full(w_in_p), full(norm_g)],
        out_specs=[row(D_MODEL), pl.BlockSpec((N_META, D_MODEL), lambda i: (0, 0)),
                   pl.BlockSpec(memory_space=pl.ANY), pl.BlockSpec((1, D_MODEL), lambda i: (0, 0))],
        out_shape=[jax.ShapeDtypeStruct((nb * s, D_MODEL), F32), jax.ShapeDtypeStruct((N_META, D_MODEL), F32),
                   jax.ShapeDtypeStruct((D_MODEL, IN_PAD), F32), jax.ShapeDtypeStruct((1, D_MODEL), F32)],
        scratch_shapes=[pltpu.VMEM((D_MODEL, IN_PAD), F32), pltpu.SemaphoreType.DMA],
        compiler_params=_cparams("arbitrary"),
    )(x2d, dh2, dpa, dpb, meta, dpam, dccm, pm, pm, w_in_p, norm_g)


def _gather_shards(shards, split):
    n = len(shards)

    def body(*refs):
        ins, outs = refs[:n], refs[n:2 * n]
        send_sems, recv_sems, fwd_send, fwd_recv, loc_sems = refs[2 * n:]
        x, y, c = lax.axis_index("x"), lax.axis_index("y"), lax.axis_index("c")
        mine = 2 * x + y
        chips = [(1 - x, y), (x, 1 - y), (1 - x, 1 - y)]

        def half_rows(a, half):
            r2 = shards[a].shape[0] // 2
            return pl.ds(pl.multiple_of(half * r2, r2), r2)

        def ici(a, j, block, half):
            px, py = chips[j]
            src, dst = ins[a], outs[a].at[block]
            if split[a]:
                src, dst = ins[a].at[half_rows(a, half)], outs[a].at[block, half_rows(a, half)]
            return pltpu.make_async_remote_copy(
                src_ref=src, dst_ref=dst, send_sem=send_sems.at[a, j], recv_sem=recv_sems.at[a, j],
                device_id=(px, py, c), device_id_type=pl.DeviceIdType.MESH)

        def fwd(a, j, half):
            px, py = chips[j]
            ref = outs[a].at[2 * px + py, half_rows(a, half)]
            return pltpu.make_async_remote_copy(
                src_ref=ref, dst_ref=ref, send_sem=fwd_send.at[a, j], recv_sem=fwd_recv.at[a, j],
                device_id=(x, y, 1 - c), device_id_type=pl.DeviceIdType.MESH)

        local = [pltpu.make_async_copy(ins[a], outs[a].at[mine], loc_sems.at[a]) for a in range(n)]
        sends = [ici(a, j, mine, c) for a in range(n) for j in range(3)]
        for cp in local + sends:
            cp.start()
        passed = []
        for a in range(n):
            for j, (px, py) in enumerate(chips):
                ici(a, j, 2 * px + py, c).wait_recv()
                if split[a]:
                    cp = fwd(a, j, c)
                    cp.start()
                    passed.append(cp)
        for a in range(n):
            if split[a]:
                for j in range(3):
                    fwd(a, j, 1 - c).wait_recv()
        for cp in sends + passed:
            cp.wait_send()
        for cp in local:
            cp.wait()

    hbm = pl.BlockSpec(memory_space=pl.ANY)
    return pl.pallas_call(
        body, name="gather_shards",
        in_specs=[hbm] * n, out_specs=[hbm] * n,
        out_shape=[jax.ShapeDtypeStruct((4,) + a.shape, a.dtype) for a in shards],
        scratch_shapes=[pltpu.SemaphoreType.DMA((n, 3)), pltpu.SemaphoreType.DMA((n, 3)),
                        pltpu.SemaphoreType.DMA((n, 3)), pltpu.SemaphoreType.DMA((n, 3)),
                        pltpu.SemaphoreType.DMA((n,))],
        compiler_params=pltpu.CompilerParams(vmem_limit_bytes=VMEM_LIMIT),
    )(*shards)


def _reduce_grads(parts, small):
    n = len(parts)
    shapes = [a.shape[1:] for a in parts]
    halves = [(sh[0] // 2, sh[1]) for sh in shapes]

    def body(*refs):
        pin, sm_in = refs[:n], refs[n]
        gout, sm_out = refs[n + 1:2 * n + 1], refs[2 * n + 1]
        scr = refs[2 * n + 2:]
        own, sib, wire, rbuf = scr[:n], scr[n:2 * n], scr[2 * n:3 * n], scr[3 * n:4 * n]
        (sbuf, send_sems, recv_sems, loc_sems, pre_send, pre_recv, post_send, post_recv,
         sm_send, sm_recv) = scr[4 * n:]
        x, y, c = lax.axis_index("x"), lax.axis_index("y"), lax.axis_index("c")
        mine = 2 * x + y
        me = 4 * x + 2 * y + c
        sibling = (x, y, 1 - c)
        chips = [(1 - x, y), (x, 1 - y), (1 - x, 1 - y)]

        def rows(a, half):
            r2 = halves[a][0]
            return pl.ds(pl.multiple_of(half * r2, r2), r2)

        def pre(a):
            return pltpu.make_async_remote_copy(
                src_ref=pin[a].at[:, rows(a, 1 - c), :], dst_ref=sib[a], send_sem=pre_send.at[a],
                recv_sem=pre_recv.at[a], device_id=sibling, device_id_type=pl.DeviceIdType.MESH)

        def ici(a, j):
            px, py = chips[j]
            return pltpu.make_async_remote_copy(
                src_ref=wire[a].at[2 * px + py], dst_ref=rbuf[a].at[j], send_sem=send_sems.at[a, j],
                recv_sem=recv_sems.at[a, j], device_id=(px, py, c), device_id_type=pl.DeviceIdType.MESH)

        def post(a, half):
            ref = gout[a].at[rows(a, half), :]
            return pltpu.make_async_remote_copy(
                src_ref=ref, dst_ref=ref, send_sem=post_send.at[a], recv_sem=post_recv.at[a],
                device_id=sibling, device_id_type=pl.DeviceIdType.MESH)

        def small_copy(kk):
            peer = (x ^ (kk >> 2), y ^ ((kk >> 1) & 1), c ^ (kk & 1))
            return pltpu.make_async_remote_copy(
                src_ref=sm_in, dst_ref=sbuf.at[kk], send_sem=sm_send.at[kk - 1], recv_sem=sm_recv.at[kk - 1],
                device_id=peer, device_id_type=pl.DeviceIdType.MESH)

        local = [pltpu.make_async_copy(pin[a].at[:, rows(a, c), :], own[a], loc_sems.at[a]) for a in range(n)]
        pres = [pre(a) for a in range(n)]
        smalls = [small_copy(kk) for kk in range(1, 8)]
        for cp in local + pres + smalls:
            cp.start()
        sbuf[0] = sm_in[...]
        sends = []
        for a in range(n):
            local[a].wait()
            pres[a].wait_recv()
            for blk in range(4):
                tot = own[a][blk] + sib[a][blk]
                own[a][blk] = tot
                wire[a][blk] = tot.astype(BF16)
            for j in range(3):
                cp = ici(a, j)
                cp.start()
                sends.append(cp)
        for cp in smalls:
            cp.wait_recv()
        total = sbuf[me]
        for d in range(1, 8):
            total = total + sbuf[me ^ d]
        sm_out[...] = total
        posts = []
        for a in range(n):
            for j in range(3):
                ici(a, j).wait_recv()
            fin = own[a][mine]
            for j in range(3):
                fin = fin + rbuf[a][j].astype(F32)
            gout[a][rows(a, c), :] = fin
            cp = post(a, c)
            cp.start()
            posts.append(cp)
        for a in range(n):
            post(a, 1 - c).wait_recv()
        for cp in pres + sends + smalls + posts:
            cp.wait_send()

    hbm = pl.BlockSpec(memory_space=pl.ANY)
    vmem = pl.BlockSpec(memory_space=pltpu.VMEM)
    dma = pltpu.SemaphoreType.DMA
    return pl.pallas_call(
        body, name="reduce_grads",
        in_specs=[hbm] * n + [vmem], out_specs=[vmem] * (n + 1),
        out_shape=[jax.ShapeDtypeStruct(sh, F32) for sh in shapes] + [jax.ShapeDtypeStruct(small.shape, F32)],
        scratch_shapes=([pltpu.VMEM((4,) + hs, F32) for hs in halves] + [pltpu.VMEM((4,) + hs, F32) for hs in halves]
                        + [pltpu.VMEM((4,) + hs, BF16) for hs in halves]
                        + [pltpu.VMEM((3,) + hs, BF16) for hs in halves]
                        + [pltpu.VMEM((8,) + small.shape, F32), dma((n, 3)), dma((n, 3)), dma((n,)),
                           dma((n,)), dma((n,)), dma((n,)), dma((n,)), dma((7,)), dma((7,))]),
        compiler_params=pltpu.CompilerParams(vmem_limit_bytes=VMEM_LIMIT),
    )(*parts, small)


def _adamw(w, g, m, v, name):
    shape = w.shape
    w2, g2, m2, v2 = (a.reshape((-1, shape[-1])) for a in (w, g, m, v))

    def body(w_ref, g_ref, m_ref, v_ref, d_ref, nm_ref, nv_ref):
        gv = g_ref[...]
        nm = ADAM_B1 * m_ref[...] + (1.0 - ADAM_B1) * gv
        nv = ADAM_B2 * v_ref[...] + (1.0 - ADAM_B2) * (gv * gv)
        m_hat = nm / (1.0 - ADAM_B1 ** ADAM_STEP)
        v_hat = nv / (1.0 - ADAM_B2 ** ADAM_STEP)
        d_ref[...] = -ADAM_LR * (m_hat / (jnp.sqrt(v_hat) + ADAM_EPS) + ADAM_WD * w_ref[...])
        nm_ref[...] = nm
        nv_ref[...] = nv

    out = pl.pallas_call(
        body, name=name,
        out_shape=[jax.ShapeDtypeStruct(w2.shape, F32)] * 3,
        compiler_params=pltpu.CompilerParams(vmem_limit_bytes=VMEM_LIMIT),
    )(w2, g2, m2, v2)
    return tuple(a.reshape(shape) for a in out)


def kernel(x, meta_tokens, norm_g, w_in, q_norm_g, w_q_up, kv_norm_g, w_kv_up, conv_w, attn_out_g, conv_out_g, w_out, final_norm_g, loss_target, m_meta_tokens, m_norm_g, m_w_in, m_q_norm_g, m_w_q_up, m_kv_norm_g, m_w_kv_up, m_conv_w, m_attn_out_g, m_conv_out_g, m_w_out, m_final_norm_g, v_meta_tokens, v_norm_g, v_w_in, v_q_norm_g, v_w_q_up, v_kv_norm_g, v_w_kv_up, v_conv_w, v_attn_out_g, v_conv_out_g, v_w_out, v_final_norm_g):
    nb, s, _ = x.shape
    tm = ROW_TILE
    assert s % tm == 0 and tm % 8 == 0
    r = nb * s

    g_in, g_q, g_kv, g_out, g_cw, g_meta = _gather_shards([
        w_in[0].astype(BF16), w_q_up[0].astype(BF16), w_kv_up[0].astype(BF16), w_out[0].astype(BF16),
        conv_w[0], meta_tokens], [True, True, True, True, False, False])
    w_in_f = jnp.transpose(g_in, (1, 0, 2)).reshape(D_MODEL, IN_DIM)
    w_in_p = jnp.concatenate([w_in_f[:, :448], jnp.zeros((D_MODEL, 64), BF16), w_in_f[:, 448:]], axis=1)
    wq_f = jnp.transpose(g_q, (1, 0, 2)).reshape(Q_RANK, HEADS, NOPE + ROPE)
    wq_p = jnp.pad(wq_f, ((0, 0), (0, 0), (0, QK_PAD - NOPE - ROPE))).reshape(Q_RANK, HEADS * QK_PAD)
    wkv_f = jnp.transpose(g_kv, (1, 0, 2)).reshape(KV_RANK, HEADS, 2, NOPE)
    wkv_p = jnp.transpose(wkv_f, (0, 2, 1, 3)).reshape(KV_RANK, 2 * HEADS * NOPE)
    w_out_f = g_out.reshape(D_MODEL, D_MODEL)
    conv_f = jnp.transpose(g_cw, (1, 0, 2)).reshape(3, CONV_W)
    meta_f = jnp.transpose(g_meta, (1, 0, 2)).reshape(N_META, D_MODEL)

    c_all, sa_all, sb_all = _rope_tables(N_META + s)
    tabs_m = (c_all[:N_META], sa_all[:N_META], sb_all[:N_META])
    tabs = (c_all[N_META:], sa_all[N_META:], sb_all[N_META:])
    gid = jnp.arange(CONV_W) // CONV_GROUP
    gmat = jnp.where(gid[:, None] == gid[None, :], 1.0 / CONV_GROUP, 0.0).astype(BF16)
    ga, gc = attn_out_g, conv_out_g
    gf = final_norm_g.reshape(1, D_MODEL)

    x2d = x.reshape(r, D_MODEL)
    tgt2d = loss_target.reshape(r, D_MODEL)

    pm, _, km, vm = _fwd_proj(meta_f, tabs_m, norm_g, w_in_p, q_norm_g, wq_p, kv_norm_g, wkv_p,
                              1, N_META, N_META, "fwd_proj_meta")
    p, q, k, v = _fwd_proj(x2d, tabs, norm_g, w_in_p, q_norm_g, wq_p, kv_norm_g, wkv_p, nb, s, tm, "fwd_proj")
    o, lse = _attn_fwd(q, k, v, km, vm, nb, s, tm)
    dh2, dycat, dw_out, dgf, loss_acc = _out_fwd_bwd(x2d, tgt2d, o, p, pm, conv_f, ga, gc, gmat, w_out_f, gf,
                                                     nb, s, tm)
    dpb, do, delta, dccm, dga, dgc, dcw = _gate_bwd(dycat, o, p, pm, conv_f, ga, gc, gmat, nb, s, tm)
    dq, dk, dv, dkm, dvm = _attn_bwd(q, k, v, do, lse, delta, km, vm, nb, s, tm)
    dpa, dpam, dwq_p, dwkv_p, dgq, dgkv = _up_bwd(dq, dk, dv, dkm, dvm, p, pm, tabs, tabs_m, wq_p, wkv_p,
                                                  q_norm_g, kv_norm_g, nb, s, tm)
    gx, gmeta, dw_in_p, dng = _in_bwd(x2d, dh2, dpa, dpb, meta_f, dpam, dccm, pm, w_in_p, norm_g, nb, s, tm)

    dw_in_f = jnp.concatenate([dw_in_p[:, :448], dw_in_p[:, 512:]], axis=1)
    p_in = jnp.transpose(dw_in_f.reshape(D_MODEL, 4, IN_DIM // 4), (1, 0, 2))
    p_q = jnp.transpose(dwq_p.reshape(Q_RANK, HEADS, QK_PAD)[:, :, :NOPE + ROPE], (1, 0, 2))
    dwkv_f = jnp.transpose(dwkv_p.reshape(KV_RANK, 2, HEADS, NOPE), (0, 2, 1, 3)).reshape(KV_RANK, 4, 2 * NOPE)
    p_kv = jnp.transpose(dwkv_f, (1, 0, 2))
    p_out = dw_out.reshape(4, D_MODEL // 4, D_MODEL)
    flat = jnp.concatenate([dng.reshape(-1), dgq.reshape(-1), dgkv.reshape(-1), dga.reshape(-1), dgc.reshape(-1),
                            dgf.reshape(-1), dcw[:3].reshape(-1), gmeta.reshape(-1)])
    n_small = flat.shape[0]
    rows_small = -(-n_small // 1024) * 8
    small = jnp.pad(flat, (0, rows_small * 128 - n_small)).reshape(rows_small, 128)
    g_w_in, g_w_q, g_w_kv, g_w_out, small_sum = _reduce_grads([p_in, p_q, p_kv, p_out], small)
    ssum = small_sum.reshape(-1)

    def take(off, n):
        return ssum[off:off + n], off + n

    off = 0
    g_norm, off = take(off, D_MODEL)
    g_qn, off = take(off, Q_RANK)
    g_kvn, off = take(off, KV_RANK)
    g_ga, off = take(off, CONV_W)
    g_gc, off = take(off, CONV_W)
    g_gf, off = take(off, D_MODEL)
    g_cw_all, off = take(off, 3 * CONV_W)
    g_meta_all, off = take(off, N_META * D_MODEL)
    chip = 2 * lax.axis_index("x") + lax.axis_index("y")
    g_conv = lax.dynamic_slice(g_cw_all.reshape(3, CONV_W), (0, chip * 128), (3, 128))
    g_mt = lax.dynamic_slice(g_meta_all.reshape(N_META, D_MODEL), (0, chip * 256), (N_META, 256))

    grads = {
        "meta_tokens": g_mt, "norm_g": g_norm.reshape(1, -1), "w_in": g_w_in[None], "q_norm_g": g_qn.reshape(1, -1),
        "w_q_up": g_w_q[None], "kv_norm_g": g_kvn.reshape(1, -1), "w_kv_up": g_w_kv[None], "conv_w": g_conv[None],
        "attn_out_g": g_ga.reshape(1, -1), "conv_out_g": g_gc.reshape(1, -1), "w_out": g_w_out[None],
        "final_norm_g": g_gf,
    }
    weights = {
        "meta_tokens": (meta_tokens, m_meta_tokens, v_meta_tokens), "norm_g": (norm_g, m_norm_g, v_norm_g),
        "w_in": (w_in, m_w_in, v_w_in), "q_norm_g": (q_norm_g, m_q_norm_g, v_q_norm_g),
        "w_q_up": (w_q_up, m_w_q_up, v_w_q_up), "kv_norm_g": (kv_norm_g, m_kv_norm_g, v_kv_norm_g),
        "w_kv_up": (w_kv_up, m_w_kv_up, v_w_kv_up), "conv_w": (conv_w, m_conv_w, v_conv_w),
        "attn_out_g": (attn_out_g, m_attn_out_g, v_attn_out_g), "conv_out_g": (conv_out_g, m_conv_out_g, v_conv_out_g),
        "w_out": (w_out, m_w_out, v_w_out), "final_norm_g": (final_norm_g, m_final_norm_g, v_final_norm_g),
    }
    names = list(weights)
    deltas, new_m, new_v = [], [], []
    for nme in names:
        w_, m_, v_ = weights[nme]
        g_ = grads[nme].reshape(w_.shape)
        grads[nme] = g_
        d_, nm_, nv_ = _adamw(w_, g_, m_, v_, "adamw_" + nme)
        deltas.append(d_)
        new_m.append(nm_)
        new_v.append(nv_)

    loss = lax.psum(loss_acc[0, 0], ("x", "y", "c"))
    grad_x = gx.reshape(nb, s, D_MODEL)
    return (loss, grad_x, *[grads[nme] for nme in names], *deltas, *new_m, *new_v)
```

```python
import functools

import jax
import jax.numpy as jnp
import numpy as np
from jax import lax
from jax.experimental import pallas as pl
from jax.experimental.pallas import tpu as pltpu

F32 = jnp.float32
BF16 = jnp.bfloat16

D_MODEL = 1024
N_META = 16
HEADS = 4
NOPE = 128
ROPE = 64
VDIM = 128
QK_PAD = 256
Q_RANK = 256
KV_RANK = 128
CONV_W = 512
CONV_GROUP = 64
ROPE_THETA = 10000.0
EPS = 1e-6
ATTN_SCALE = (NOPE + ROPE) ** -0.5
IN_DIM = 3008
IN_PAD = 3072
BLK_ZA, BLK_CB, BLK_CC, BLK_CH, BLK_ZC = 1, 2, 3, 4, 5
NEG_INF = -1e30

ADAM_LR = 0.001
ADAM_B1 = 0.9
ADAM_B2 = 0.999
ADAM_EPS = 1e-08
ADAM_WD = 0.01
ADAM_STEP = 10

ROW_TILE = 256
VMEM_LIMIT = 56 * 1024 * 1024

NT = (((1,), (1,)), ((), ()))
TN = (((0,), (0,)), ((), ()))


def _cparams(*sem):
    return pltpu.CompilerParams(dimension_semantics=sem, vmem_limit_bytes=VMEM_LIMIT)


def _dot(a, b):
    return jnp.dot(a, b, preferred_element_type=F32)


def _dot_nt(a, b):
    return lax.dot_general(a, b, NT, preferred_element_type=F32)


def _dot_tn(a, b):
    return lax.dot_general(a, b, TN, preferred_element_type=F32)


def _rms(x, g):
    r = lax.rsqrt(jnp.mean(x * x, axis=-1, keepdims=True) + EPS)
    return x * r * g, r


def _rms_bwd(dy, x, r, g):
    xh = x * r
    dyg = dy * g
    dx = r * (dyg - xh * jnp.mean(dyg * xh, axis=-1, keepdims=True))
    return dx, dy * xh


def _sigmoid(z):
    return 1.0 / (1.0 + jnp.exp(-z))


def _rope(b, c, sa, sb):
    return b * c + pltpu.roll(b, 96, 1) * sa + pltpu.roll(b, 32, 1) * sb


def _rope_bwd(d, c, sa, sb):
    return d * c + pltpu.roll(d * sa, 32, 1) + pltpu.roll(d * sb, 96, 1)


def _group_mean(x, gmat):
    hi = x.astype(BF16)
    lo = (x - hi.astype(F32)).astype(BF16)
    return _dot(hi, gmat) + _dot(lo, gmat)


def _row_of(col, rows):
    return jnp.transpose(jnp.broadcast_to(col, (rows, 128)))[0:1, :]


def _rope_tables(n_pos):
    half = ROPE // 2
    inv_freq = (np.float32(1.0) / (np.float32(ROPE_THETA) ** (np.arange(half, dtype=np.float32) / np.float32(half))))
    ang = np.arange(n_pos, dtype=np.float32)[:, None] * inv_freq.astype(np.float32)[None, :]
    cos, sin = np.cos(ang).astype(np.float32), np.sin(ang).astype(np.float32)
    z = np.zeros((n_pos, half), np.float32)
    c = np.concatenate([cos, cos, z, z], axis=1)
    sa = np.concatenate([-sin, z, z, z], axis=1)
    sb = np.concatenate([z, sin, z, z], axis=1)
    return jnp.asarray(c), jnp.asarray(sa), jnp.asarray(sb)


def _fwd_proj(x2d, tabs, norm_g, w_in_p, q_norm_g, wq_p, kv_norm_g, wkv_p, nb, s, tm, name):
    nt = s // tm
    c_t, sa_t, sb_t = tabs

    def body(x_ref, c_ref, sa_ref, sb_ref, g_ref, w_ref, gq_ref, wq_ref, gkv_ref, wkv_ref,
             p_ref, q_ref, k_ref, v_ref):
        u, _ = _rms(x_ref[...], g_ref[...])
        p = _dot_nt(u.astype(BF16), w_ref[...])
        p_ref[...] = p
        c, sa, sb = c_ref[...], sa_ref[...], sb_ref[...]
        qn, _ = _rms(p[:, 0:Q_RANK], gq_ref[...])
        q = _dot_nt(qn.astype(BF16), wq_ref[...])
        kvn, _ = _rms(p[:, Q_RANK:Q_RANK + KV_RANK], gkv_ref[...])
        kv = _dot_nt(kvn.astype(BF16), wkv_ref[...])
        kpe = _rope(p[:, 384:512], c, sa, sb)
        for h in range(HEADS):
            pe = _rope(q[:, QK_PAD * h + NOPE:QK_PAD * (h + 1)], c, sa, sb)
            qh = jnp.concatenate([q[:, QK_PAD * h:QK_PAD * h + NOPE], pe], axis=1)
            q_ref[0, h] = (qh * ATTN_SCALE).astype(BF16)
            k_ref[0, h] = jnp.concatenate([kv[:, NOPE * h:NOPE * (h + 1)], kpe], axis=1).astype(BF16)
            v_ref[0, h] = kv[:, 512 + VDIM * h:512 + VDIM * (h + 1)].astype(BF16)

    full = lambda a: pl.BlockSpec(a.shape, lambda i: (0,) * a.ndim)
    tab = pl.BlockSpec((tm, 128), lambda i: (i % nt, 0))
    return pl.pallas_call(
        body, name=name, grid=(nb * nt,),
        in_specs=[pl.BlockSpec((tm, D_MODEL), lambda i: (i, 0)), tab, tab, tab,
                  full(norm_g), full(w_in_p), full(q_norm_g), full(wq_p), full(kv_norm_g), full(wkv_p)],
        out_specs=[pl.BlockSpec((tm, IN_PAD), lambda i: (i, 0)),
                   pl.BlockSpec((1, HEADS, tm, QK_PAD), lambda i: (i // nt, 0, i % nt, 0)),
                   pl.BlockSpec((1, HEADS, tm, QK_PAD), lambda i: (i // nt, 0, i % nt, 0)),
                   pl.BlockSpec((1, HEADS, tm, VDIM), lambda i: (i // nt, 0, i % nt, 0))],
        out_shape=[jax.ShapeDtypeStruct((nb * s, IN_PAD), F32),
                   jax.ShapeDtypeStruct((nb, HEADS, s, QK_PAD), BF16),
                   jax.ShapeDtypeStruct((nb, HEADS, s, QK_PAD), BF16),
                   jax.ShapeDtypeStruct((nb, HEADS, s, VDIM), BF16)],
        compiler_params=_cparams("parallel"),
    )(x2d, c_t, sa_t, sb_t, norm_g, w_in_p, q_norm_g, wq_p, kv_norm_g, wkv_p)


def _attn_fwd(q, k, v, km, vm, nb, s, tq):
    nq = s // tq

    def body(q_ref, k_ref, v_ref, km_ref, vm_ref, o_ref, lse_ref, s_scr, p_scr):
        row = lax.broadcasted_iota(jnp.int32, (tq, tq), 0)
        col = lax.broadcasted_iota(jnp.int32, (tq, tq), 1)
        for i in range(nq):
            slot = i % 2
            qi = q_ref[0, 0, i * tq:(i + 1) * tq, :]
            sm = _dot_nt(qi, km_ref[0, 0])
            m128 = None
            for j in range(i + 1):
                sc = _dot_nt(qi, k_ref[0, 0, j * tq:(j + 1) * tq, :])
                if j == i:
                    sc = jnp.where(col <= row, sc, NEG_INF)
                s_scr[slot, :, j * tq:(j + 1) * tq] = sc
                mx = sc[:, 0:128]
                for c0 in range(128, tq, 128):
                    mx = jnp.maximum(mx, sc[:, c0:c0 + 128])
                m128 = mx if m128 is None else jnp.maximum(m128, mx)
            m = jnp.maximum(jnp.max(m128, axis=1, keepdims=True), jnp.max(sm, axis=1, keepdims=True))
            pm = jnp.exp(sm - m)
            l128 = None
            for j in range(i + 1):
                p = jnp.exp(s_scr[slot, :, j * tq:(j + 1) * tq] - m)
                p_scr[slot, :, j * tq:(j + 1) * tq] = p.astype(BF16)
                ps = p[:, 0:128]
                for c0 in range(128, tq, 128):
                    ps = ps + p[:, c0:c0 + 128]
                l128 = ps if l128 is None else l128 + ps
            l = jnp.sum(l128, axis=1, keepdims=True) + jnp.sum(pm, axis=1, keepdims=True)
            n = (i + 1) * tq
            acc = _dot(p_scr[slot, :, 0:n], v_ref[0, 0, 0:n, :]) + _dot(pm.astype(BF16), vm_ref[0, 0])
            o_ref[0, 0, i * tq:(i + 1) * tq, :] = acc / l
            lse_ref[0, 0, :, i * tq:(i + 1) * tq] = _row_of(m + jnp.log(l), tq)

    hblk = lambda w: pl.BlockSpec((1, 1, s, w), lambda b, h: (b, h, 0, 0))
    mblk = lambda w: pl.BlockSpec((1, 1, N_META, w), lambda b, h: (0, h, 0, 0))
    return pl.pallas_call(
        body, name="attn_fwd", grid=(nb, HEADS),
        in_specs=[hblk(QK_PAD), hblk(QK_PAD), hblk(VDIM), mblk(QK_PAD), mblk(VDIM)],
        out_specs=[hblk(VDIM), pl.BlockSpec((1, 1, 1, s), lambda b, h: (b, h, 0, 0))],
        out_shape=[jax.ShapeDtypeStruct((nb, HEADS, s, VDIM), F32),
                   jax.ShapeDtypeStruct((nb, HEADS, 1, s), F32)],
        scratch_shapes=[pltpu.VMEM((2, tq, s), F32), pltpu.VMEM((2, tq, s), BF16)],
        compiler_params=_cparams("parallel", "parallel"),
    )(q, k, v, km, vm)


def _shift_rows(a, prev, n_rows):
    rid = lax.broadcasted_iota(jnp.int32, a.shape, 0)
    a1 = jnp.where(rid == 0, prev[7:8, :], pltpu.roll(a, 1, 0))
    a2 = jnp.where(rid == 0, prev[6:7, :], jnp.where(rid == 1, prev[7:8, :], pltpu.roll(a, 2, 0)))
    return a1, a2


def _attn_gate(o, za, ga_h):
    on, r = _rms(o, ga_h)
    return on * (za * _sigmoid(za)), on, r


def _out_fwd_bwd(x2d, tgt2d, o, p, pm, conv_w, ga, gc, gmat, w_out, gf, nb, s, tm):
    nt = s // tm
    r = nb * s
    prev_idx = lambda i: jnp.maximum(i * (tm // 8) - 1, 0)

    def body(x_ref, t_ref, o_ref, za_ref, cb_ref, cc_ref, ch_ref, zc_ref, ccp_ref, chp_ref, mc_ref, mh_ref,
             cw_ref, ga_ref, gc_ref, gm_ref, w_ref, gf_ref,
             dh_ref, dy_ref, dw_ref, dgf_ref, loss_ref):
        i = pl.program_id(0)

        @pl.when(i == 0)
        def _():
            dw_ref[...] = jnp.zeros_like(dw_ref)
            dgf_ref[...] = jnp.zeros_like(dgf_ref)
            loss_ref[...] = jnp.zeros_like(loss_ref)

        ya = []
        for h in range(HEADS):
            y, _, _ = _attn_gate(o_ref[0, h], za_ref[:, VDIM * h:VDIM * (h + 1)],
                                 ga_ref[:, VDIM * h:VDIM * (h + 1)])
            ya.append(y)
        cc = cc_ref[...] * ch_ref[...]
        prev = jnp.where(i % nt == 0, mc_ref[8:16, :] * mh_ref[8:16, :], ccp_ref[...] * chp_ref[...])
        cc1, cc2 = _shift_rows(cc, prev, tm)
        yc = cb_ref[...] * (cw_ref[0:1, :] * cc2 + cw_ref[1:2, :] * cc1 + cw_ref[2:3, :] * cc)
        rg = lax.rsqrt(_group_mean(yc * yc, gm_ref[...]) + EPS)
        zc = zc_ref[...]
        yconv = yc * rg * gc_ref[...] * (zc * _sigmoid(zc))
        ycat = jnp.concatenate(ya + [yconv], axis=1).astype(BF16)
        h2 = x_ref[...] + _dot(ycat, w_ref[...])
        gfv = gf_ref[...]
        y, r2 = _rms(h2, gfv)
        e = y - t_ref[...]
        loss_ref[...] += 0.5 * jnp.sum(e * e) / D_MODEL
        dyv = e * (1.0 / D_MODEL)
        dh2, dgf = _rms_bwd(dyv, h2, r2, gfv)
        dgf_ref[...] += jnp.sum(dgf, axis=0, keepdims=True)
        dh_ref[...] = dh2
        dhb = dh2.astype(BF16)
        dy_ref[...] = _dot_nt(dhb, w_ref[...])
        dw_ref[...] += _dot_tn(ycat, dhb)

    row = lambda w, j: pl.BlockSpec((tm, w), lambda i: (i, j))
    pblk = lambda j: pl.BlockSpec((tm, 512), lambda i: (i, j))
    pprev = lambda j: pl.BlockSpec((8, 512), lambda i: (prev_idx(i), j))
    mblk = lambda j: pl.BlockSpec((N_META, 512), lambda i: (0, j))
    full = lambda a: pl.BlockSpec(a.shape, lambda i: (0,) * a.ndim)
    return pl.pallas_call(
        body, name="out_fwd_bwd", grid=(nb * nt,),
        in_specs=[row(D_MODEL, 0), row(D_MODEL, 0),
                  pl.BlockSpec((1, HEADS, tm, VDIM), lambda i: (i // nt, 0, i % nt, 0)),
                  pblk(BLK_ZA), pblk(BLK_CB), pblk(BLK_CC), pblk(BLK_CH), pblk(BLK_ZC),
                  pprev(BLK_CC), pprev(BLK_CH), mblk(BLK_CC), mblk(BLK_CH),
                  full(conv_w), full(ga), full(gc), full(gmat), full(w_out), full(gf)],
        out_specs=[row(D_MODEL, 0), row(D_MODEL, 0),
                   pl.BlockSpec((D_MODEL, D_MODEL), lambda i: (0, 0)),
                   pl.BlockSpec((1, D_MODEL), lambda i: (0, 0)),
                   pl.BlockSpec((1, 128), lambda i: (0, 0))],
        out_shape=[jax.ShapeDtypeStruct((r, D_MODEL), F32), jax.ShapeDtypeStruct((r, D_MODEL), F32),
                   jax.ShapeDtypeStruct((D_MODEL, D_MODEL), F32), jax.ShapeDtypeStruct((1, D_MODEL), F32),
                   jax.ShapeDtypeStruct((1, 128), F32)],
        compiler_params=_cparams("arbitrary"),
    )(x2d, tgt2d, o, p, p, p, p, p, p, p, pm, pm, conv_w, ga, gc, gmat, w_out, gf)


def _gate_bwd(dycat, o, p, pm, conv_w, ga, gc, gmat, nb, s, tm):
    nt = s // tm
    r = nb * s
    ext = tm + 8
    prev_idx = lambda i: jnp.maximum(i * (tm // 8) - 1, 0)
    next_idx = lambda i: jnp.minimum((i + 1) * (tm // 8), r // 8 - 1)

    def body(dya_ref, dyc_ref, dycn_ref, o_ref, za_ref, cb_ref, cbn_ref, cc_ref, ccp_ref, ccn_ref,
             ch_ref, chp_ref, chn_ref, zc_ref, zcn_ref, mc_ref, mh_ref, cw_ref, ga_ref, gc_ref, gm_ref,
             dpb_ref, do_ref, dl_ref, dccm_ref, dga_ref, dgc_ref, dcw_ref):
        i = pl.program_id(0)

        @pl.when(i == 0)
        def _():
            dga_ref[...] = jnp.zeros_like(dga_ref)
            dgc_ref[...] = jnp.zeros_like(dgc_ref)
            dcw_ref[...] = jnp.zeros_like(dcw_ref)

        dga = []
        for h in range(HEADS):
            hs = slice(VDIM * h, VDIM * (h + 1))
            oh, za, gah, dya = o_ref[0, h], za_ref[:, hs], ga_ref[:, hs], dya_ref[:, hs]
            sg = _sigmoid(za)
            on, ro = _rms(oh, gah)
            don = dya * (za * sg)
            dpb_ref[:, hs] = dya * on * (sg * (1.0 + za * (1.0 - sg)))
            do, dg = _rms_bwd(don, oh, ro, gah)
            dga.append(jnp.sum(dg, axis=0, keepdims=True))
            dob = do.astype(BF16)
            do_ref[0, h] = dob
            dl_ref[0, h] = _row_of(jnp.sum(dob.astype(F32) * oh, axis=1, keepdims=True), tm)
        dga_ref[...] += jnp.concatenate(dga, axis=1)

        cat = lambda a, b: jnp.concatenate([a[...], b[...]], axis=0)
        cch = cat(cc_ref, ccn_ref)
        chh = cat(ch_ref, chn_ref)
        cb = cat(cb_ref, cbn_ref)
        zc = cat(zc_ref, zcn_ref)
        dy = cat(dyc_ref, dycn_ref)
        first = i % nt == 0
        last = i % nt == nt - 1
        cc = cch * chh
        prev = jnp.where(first, mc_ref[8:16, :] * mh_ref[8:16, :], ccp_ref[...] * chp_ref[...])
        cc1, cc2 = _shift_rows(cc, prev, ext)
        w0, w1, w2 = cw_ref[0:1, :], cw_ref[1:2, :], cw_ref[2:3, :]
        dw = w0 * cc2 + w1 * cc1 + w2 * cc
        yc = cb * dw
        rg = lax.rsqrt(_group_mean(yc * yc, gm_ref[...]) + EPS)
        ych = yc * rg
        gcv = gc_ref[...]
        sg = _sigmoid(zc)
        dycn = dy * (zc * sg)
        dzc = dy * (ych * gcv) * (sg * (1.0 + zc * (1.0 - sg)))
        dgc_ref[...] += jnp.sum((dycn * ych)[:tm], axis=0, keepdims=True)
        dycg = dycn * gcv
        dyc = rg * (dycg - ych * _group_mean(dycg * ych, gm_ref[...]))
        rid = lax.broadcasted_iota(jnp.int32, (ext, CONV_W), 0)
        ddw = jnp.where(jnp.logical_and(last, rid >= tm), 0.0, dyc * cb)
        dcb = dyc * dw
        dcc = w2 * ddw + w1 * pltpu.roll(ddw, ext - 1, 0) + w0 * pltpu.roll(ddw, ext - 2, 0)
        dpb_ref[:, 512:1024] = dcb[:tm]
        dpb_ref[:, 1024:1536] = (dcc * chh)[:tm]
        dpb_ref[:, 1536:2048] = (dcc * cch)[:tm]
        dpb_ref[:, 2048:2560] = dzc[:tm]
        rs = lambda a: jnp.sum(a[:tm], axis=0, keepdims=True)
        dcw_ref[0:1, :] += rs(ddw * cc2)
        dcw_ref[1:2, :] += rs(ddw * cc1)
        dcw_ref[2:3, :] += rs(ddw * cc)

        @pl.when(first)
        def _():
            d0, d1 = ddw[0:1, :], ddw[1:2, :]
            r8 = lax.broadcasted_iota(jnp.int32, (8, CONV_W), 0)
            dccm_ref[0] = jnp.where(r8 == 7, w1 * d0 + w0 * d1, jnp.where(r8 == 6, w0 * d0, 0.0))

    row = lambda j: pl.BlockSpec((tm, 512), lambda i: (i, j))
    prv = lambda j: pl.BlockSpec((8, 512), lambda i: (prev_idx(i), j))
    nxt = lambda j: pl.BlockSpec((8, 512), lambda i: (next_idx(i), j))
    mblk = lambda j: pl.BlockSpec((N_META, 512), lambda i: (0, j))
    full = lambda a: pl.BlockSpec(a.shape, lambda i: (0,) * a.ndim)
    hb = lambda w: pl.BlockSpec((1, HEADS, tm, w), lambda i: (i // nt, 0, i % nt, 0))
    acc = lambda rr: pl.BlockSpec((rr, 512), lambda i: (0, 0))
    return pl.pallas_call(
        body, name="gate_bwd", grid=(nb * nt,),
        in_specs=[row(0), row(1), nxt(1), hb(VDIM),
                  row(BLK_ZA), row(BLK_CB), nxt(BLK_CB), row(BLK_CC), prv(BLK_CC), nxt(BLK_CC),
                  row(BLK_CH), prv(BLK_CH), nxt(BLK_CH), row(BLK_ZC), nxt(BLK_ZC),
                  mblk(BLK_CC), mblk(BLK_CH), full(conv_w), full(ga), full(gc), full(gmat)],
        out_specs=[pl.BlockSpec((tm, 2560), lambda i: (i, 0)), hb(VDIM),
                   pl.BlockSpec((1, HEADS, 1, tm), lambda i: (i // nt, 0, 0, i % nt)),
                   pl.BlockSpec((1, 8, 512), lambda i: (i // nt, 0, 0)),
                   acc(1), acc(1), acc(8)],
        out_shape=[jax.ShapeDtypeStruct((r, 2560), F32), jax.ShapeDtypeStruct((nb, HEADS, s, VDIM), BF16),
                   jax.ShapeDtypeStruct((nb, HEADS, 1, s), F32), jax.ShapeDtypeStruct((nb, 8, 512), F32),
                   jax.ShapeDtypeStruct((1, 512), F32), jax.ShapeDtypeStruct((1, 512), F32),
                   jax.ShapeDtypeStruct((8, 512), F32)],
        compiler_params=_cparams("arbitrary"),
    )(dycat, dycat, dycat, o, p, p, p, p, p, p, p, p, p, p, p, pm, pm, conv_w, ga, gc, gmat)


def _attn_bwd(q, k, v, do, lse, delta, km, vm, nb, s, t):
    n = s // t

    def body(q_ref, k_ref, v_ref, do_ref, lse_ref, dl_ref, km_ref, vm_ref,
             dq_ref, dk_ref, dv_ref, dkm_ref, dvm_ref, p_scr, ds_scr):
        b = pl.program_id(1)

        @pl.when(b == 0)
        def _():
            dkm_ref[...] = jnp.zeros_like(dkm_ref)
            dvm_ref[...] = jnp.zeros_like(dvm_ref)

        kr = lax.broadcasted_iota(jnp.int32, (t, t), 0)
        qc = lax.broadcasted_iota(jnp.int32, (t, t), 1)
        km_v, vm_v = km_ref[0, 0], vm_ref[0, 0]
        ptm = jnp.exp(_dot_nt(km_v, q_ref[0, 0]) - lse_ref[0, 0])
        dstm = (ptm * (_dot_nt(vm_v, do_ref[0, 0]) - dl_ref[0, 0])).astype(BF16)
        dkm_ref[0] += _dot(dstm, q_ref[0, 0])
        dvm_ref[0] += _dot(ptm.astype(BF16), do_ref[0, 0])
        dq_ref[0, 0] = _dot_tn(dstm, km_v)
        for j in range(n):
            slot = j % 2
            kj = k_ref[0, 0, j * t:(j + 1) * t, :]
            vj = v_ref[0, 0, j * t:(j + 1) * t, :]
            for i in range(j, n):
                cs = slice(i * t, (i + 1) * t)
                qi = q_ref[0, 0, cs, :]
                doi = do_ref[0, 0, cs, :]
                st = _dot_nt(kj, qi)
                if i == j:
                    st = jnp.where(kr <= qc, st, NEG_INF)
                pt = jnp.exp(st - lse_ref[0, 0, :, cs])
                dst = (pt * (_dot_nt(vj, doi) - dl_ref[0, 0, :, cs])).astype(BF16)
                p_scr[slot, :, cs] = pt.astype(BF16)
                ds_scr[slot, :, cs] = dst
                dq_ref[0, 0, cs, :] += _dot_tn(dst, kj)
            dv_ref[0, 0, j * t:(j + 1) * t, :] = _dot(p_scr[slot, :, j * t:s], do_ref[0, 0, j * t:s, :])
            dk_ref[0, 0, j * t:(j + 1) * t, :] = _dot(ds_scr[slot, :, j * t:s], q_ref[0, 0, j * t:s, :])

    big = lambda w: pl.BlockSpec((1, 1, s, w), lambda h, b: (b, h, 0, 0))
    rowv = pl.BlockSpec((1, 1, 1, s), lambda h, b: (b, h, 0, 0))
    mk = lambda w: pl.BlockSpec((1, 1, N_META, w), lambda h, b: (0, h, 0, 0))
    mo = lambda w: pl.BlockSpec((1, N_META, w), lambda h, b: (h, 0, 0))
    return pl.pallas_call(
        body, name="attn_bwd", grid=(HEADS, nb),
        in_specs=[big(QK_PAD), big(QK_PAD), big(VDIM), big(VDIM), rowv, rowv, mk(QK_PAD), mk(VDIM)],
        out_specs=[big(QK_PAD), big(QK_PAD), big(VDIM), mo(QK_PAD), mo(VDIM)],
        out_shape=[jax.ShapeDtypeStruct((nb, HEADS, s, QK_PAD), F32),
                   jax.ShapeDtypeStruct((nb, HEADS, s, QK_PAD), F32),
                   jax.ShapeDtypeStruct((nb, HEADS, s, VDIM), F32),
                   jax.ShapeDtypeStruct((HEADS, N_META, QK_PAD), F32),
                   jax.ShapeDtypeStruct((HEADS, N_META, VDIM), F32)],
        scratch_shapes=[pltpu.VMEM((2, t, s), BF16), pltpu.VMEM((2, t, s), BF16)],
        compiler_params=_cparams("arbitrary", "arbitrary"),
    )(q, k, v, do, lse, delta, km, vm)


def _up_bwd(dq, dk, dv, dkm, dvm, p, pm, tabs, tabs_m, wq_p, wkv_p, gq, gkv, nb, s, tm):
    nt = s // tm
    n = nb * nt
    c_t, sa_t, sb_t = tabs
    cm_t, sam_t, sbm_t = tabs_m

    def kv_path(dkh, dvh, pa, c, sa, sb, wkv, gkvv):
        dkpe = dkh[0][:, NOPE:]
        for h in range(1, HEADS):
            dkpe = dkpe + dkh[h][:, NOPE:]
        dkr = _rope_bwd(dkpe, c, sa, sb)
        dkv = jnp.concatenate([d[:, :NOPE] for d in dkh] + list(dvh), axis=1).astype(BF16)
        ckv = pa[:, Q_RANK:Q_RANK + KV_RANK]
        kvn, rkv = _rms(ckv, gkvv)
        dckv, dg = _rms_bwd(_dot(dkv, wkv), ckv, rkv, gkvv)
        return dckv, dkr, kvn.astype(BF16), dkv, jnp.sum(dg, axis=0, keepdims=True)

    def body(dq_ref, dk_ref, dv_ref, pa_ref, c_ref, sa_ref, sb_ref,
             dkm_ref, dvm_ref, pam_ref, cm_ref, sam_ref, sbm_ref,
             wq_ref, wkv_ref, gq_ref, gkv_ref,
             dpa_ref, dpam_ref, pq_ref, pkv_ref, dgq_ref, dgkv_ref, dwq_ref, dwkv_ref):
        i = pl.program_id(0)

        @pl.when(i == 0)
        def _():
            dwq_ref[...] = jnp.zeros_like(dwq_ref)
            dwkv_ref[...] = jnp.zeros_like(dwkv_ref)
            dgq_ref[...] = jnp.zeros_like(dgq_ref)
            dgkv_ref[...] = jnp.zeros_like(dgkv_ref)

        @pl.when(i < n)
        def _():
            c, sa, sb = c_ref[...], sa_ref[...], sb_ref[...]
            pa = pa_ref[...]
            parts = []
            for h in range(HEADS):
                dqh = dq_ref[0, h] * ATTN_SCALE
                parts += [dqh[:, :NOPE], _rope_bwd(dqh[:, NOPE:], c, sa, sb)]
            dql = jnp.concatenate(parts, axis=1).astype(BF16)
            cq = pa[:, 0:Q_RANK]
            gqv = gq_ref[...]
            qn, rq = _rms(cq, gqv)
            dwq_ref[...] += _dot_tn(dql, qn.astype(BF16))
            dcq, dg = _rms_bwd(_dot(dql, wq_ref[...]), cq, rq, gqv)
            dgq_ref[...] += jnp.sum(dg, axis=0, keepdims=True)
            dckv, dkr, kvn, dkv, dgk = kv_path([dk_ref[0, h] for h in range(HEADS)],
                                               [dv_ref[0, h] for h in range(HEADS)],
                                               pa, c, sa, sb, wkv_ref[...], gkv_ref[...])
            dwkv_ref[...] += _dot_tn(dkv, kvn)
            dgkv_ref[...] += dgk
            dpa_ref[...] = jnp.concatenate([dcq, dckv, dkr], axis=1)

        @pl.when(i == n)
        def _():
            dckv, dkr, kvn, dkv, dgk = kv_path([dkm_ref[h] for h in range(HEADS)],
                                               [dvm_ref[h] for h in range(HEADS)],
                                               pam_ref[...], cm_ref[...], sam_ref[...], sbm_ref[...],
                                               wkv_ref[...], gkv_ref[...])
            dwkv_ref[...] += _dot_tn(dkv, kvn)
            dgkv_ref[...] += dgk
            dpam_ref[...] = jnp.concatenate([jnp.zeros((N_META, Q_RANK), F32), dckv, dkr], axis=1)
            for h in range(HEADS):
                pq_ref[h] = dwq_ref[QK_PAD * h:QK_PAD * h + NOPE + ROPE, :]
                pkv_ref[h, 0:NOPE, :] = dwkv_ref[NOPE * h:NOPE * (h + 1), :]
                pkv_ref[h, NOPE:NOPE + VDIM, :] = dwkv_ref[512 + VDIM * h:512 + VDIM * (h + 1), :]

    cl = lambda i: jnp.minimum(i, n - 1)
    hb = lambda w: pl.BlockSpec((1, HEADS, tm, w), lambda i: (cl(i) // nt, 0, cl(i) % nt, 0))
    tab = pl.BlockSpec((tm, 128), lambda i: (cl(i) % nt, 0))
    full = lambda a: pl.BlockSpec(a.shape, lambda i: (0,) * a.ndim)
    const = lambda shape: pl.BlockSpec(shape, lambda i: (0,) * len(shape))
    return pl.pallas_call(
        body, name="up_bwd", grid=(n + 1,),
        in_specs=[hb(QK_PAD), hb(QK_PAD), hb(VDIM), pl.BlockSpec((tm, 512), lambda i: (cl(i), 0)), tab, tab, tab,
                  full(dkm), full(dvm), pl.BlockSpec((N_META, 512), lambda i: (0, 0)),
                  full(cm_t), full(sam_t), full(sbm_t), full(wq_p), full(wkv_p), full(gq), full(gkv)],
        out_specs=[pl.BlockSpec((tm, 512), lambda i: (cl(i), 0)), const((N_META, 512)),
                   const((HEADS, NOPE + ROPE, Q_RANK)), const((HEADS, NOPE + VDIM, KV_RANK)),
                   const((1, Q_RANK)), const((1, KV_RANK))],
        out_shape=[jax.ShapeDtypeStruct((nb * s, 512), F32), jax.ShapeDtypeStruct((N_META, 512), F32),
                   jax.ShapeDtypeStruct((HEADS, NOPE + ROPE, Q_RANK), F32),
                   jax.ShapeDtypeStruct((HEADS, NOPE + VDIM, KV_RANK), F32),
                   jax.ShapeDtypeStruct((1, Q_RANK), F32), jax.ShapeDtypeStruct((1, KV_RANK), F32)],
        scratch_shapes=[pltpu.VMEM((HEADS * QK_PAD, Q_RANK), F32), pltpu.VMEM((1024, KV_RANK), F32)],
        compiler_params=_cparams("arbitrary"),
    )(dq, dk, dv, p, c_t, sa_t, sb_t, dkm, dvm, pm, cm_t, sam_t, sbm_t, wq_p, wkv_p, gq, gkv)


def _in_bwd(x2d, dh2, dpa, dpb, meta, dpam, dccm, pm, w_in_p, norm_g, nb, s, tm):
    nt = s // tm
    n = nb * nt

    def body(x_ref, dh_ref, dpa_ref, dpb_ref, mt_ref, dpam_ref, dccm_ref, mc_ref, mh_ref, w_ref, g_ref,
             gx_ref, gm_ref, dw_hbm, dg_ref, acc_ref, sems):
        i = pl.program_id(0)

        @pl.when(i == 0)
        def _():
            acc_ref[...] = jnp.zeros_like(acc_ref)
            dg_ref[...] = jnp.zeros_like(dg_ref)

        def rows(x, dp, dres):
            g = g_ref[...]
            u, r1 = _rms(x, g)
            dpb16 = dp.astype(BF16)
            acc_ref[...] += _dot_tn(dpb16, u.astype(BF16))
            dx, dg = _rms_bwd(_dot(dpb16, w_ref[...]), x, r1, g)
            dg_ref[...] += jnp.sum(dg, axis=0, keepdims=True)
            return dx if dres is None else dx + dres

        @pl.when(i < n)
        def _():
            dp = jnp.concatenate([dpa_ref[...], dpb_ref[...]], axis=1)
            gx_ref[...] = rows(x_ref[...], dp, dh_ref[...])

        @pl.when(i == n)
        def _():
            dcc = dccm_ref[0]
            for b in range(1, nb):
                dcc = dcc + dccm_ref[b]
            z8 = jnp.zeros((8, CONV_W), F32)
            dc = jnp.concatenate([z8, dcc * mh_ref[8:16, :]], axis=0)
            dh = jnp.concatenate([z8, dcc * mc_ref[8:16, :]], axis=0)
            z = jnp.zeros((N_META, CONV_W), F32)
            dp = jnp.concatenate([dpam_ref[...], z, z, dc, dh, z], axis=1)
            gm_ref[...] = rows(mt_ref[...], dp, None)
            per = IN_DIM // 4
            cps = [pltpu.make_async_copy(acc_ref.at[0:448], dw_hbm.at[0, 0:448], sems.at[0]),
                   pltpu.make_async_copy(acc_ref.at[512:per + 64], dw_hbm.at[0, 448:per], sems.at[1])]
            for qq in range(1, 4):
                cps.append(pltpu.make_async_copy(acc_ref.at[per * qq + 64:per * (qq + 1) + 64], dw_hbm.at[qq],
                                                 sems.at[qq + 1]))
            for cp in cps:
                cp.start()
            for cp in cps:
                cp.wait()

    cl = lambda i: jnp.minimum(i, n - 1)
    row = lambda w: pl.BlockSpec((tm, w), lambda i: (cl(i), 0))
    full = lambda a: pl.BlockSpec(a.shape, lambda i: (0,) * a.ndim)
    mblk = lambda j: pl.BlockSpec((N_META, 512), lambda i: (0, j))
    return pl.pallas_call(
        body, name="in_bwd", grid=(n + 1,),
        in_specs=[row(D_MODEL), row(D_MODEL), row(512), row(2560), full(meta), full(dpam), full(dccm),
                  mblk(BLK_CC), mblk(BLK_CH), full(w_in_p), full(norm_g)],
        out_specs=[row(D_MODEL), pl.BlockSpec((N_META, D_MODEL), lambda i: (0, 0)),
                   pl.BlockSpec(memory_space=pl.ANY), pl.BlockSpec((1, D_MODEL), lambda i: (0, 0))],
        out_shape=[jax.ShapeDtypeStruct((nb * s, D_MODEL), F32), jax.ShapeDtypeStruct((N_META, D_MODEL), F32),
                   jax.ShapeDtypeStruct((4, IN_DIM // 4, D_MODEL), F32), jax.ShapeDtypeStruct((1, D_MODEL), F32)],
        scratch_shapes=[pltpu.VMEM((IN_PAD, D_MODEL), F32), pltpu.SemaphoreType.DMA((5,))],
        compiler_params=_cparams("arbitrary"),
    )(x2d, dh2, dpa, dpb, meta, dpam, dccm, pm, pm, w_in_p, norm_g)


W_IN_PIECES = ((0, 384, 752, 0, 64, 0), (384, 64, 752, 384, 448, 1), (448, 304, 752, 512, 512, 1))
W_Q_PIECES = ((0, 96, 256, 0, 0, 0), (96, 96, 256, 96, 96, 1))
W_KV_PIECES = ((0, 128, 128, 0, 0, 0), (128, 128, 128, 512, 512, 1))
W_OUT_PIECES = ((0, 128, 256, 0, 0, 0), (128, 128, 256, 128, 128, 1))


def _gather_weights(split, pieces, out_rows, whole, zero_fills):
    ns, nw, nz = len(split), len(whole), len(zero_fills)
    flat = [(a, pc) for a in range(ns) for pc in pieces[a]]
    nk = len(flat)

    def body(*refs):
        ins, wins, zins = refs[:ns], refs[ns:ns + nw], refs[ns + nw:ns + nw + nz]
        outs, wouts = refs[ns + nw + nz:2 * ns + nw + nz], refs[2 * ns + nw + nz:2 * (ns + nw) + nz]
        send_sems, recv_sems, fwd_send, fwd_recv, loc_sems, w_send, w_recv, w_loc, z_sems = refs[2 * (ns + nw) + nz:]
        x, y, c = lax.axis_index("x"), lax.axis_index("y"), lax.axis_index("c")
        mine = 2 * x + y
        chips = [(1 - x, y), (x, 1 - y), (1 - x, 1 - y)]
        chip_of = [2 * px + py for px, py in chips]

        def src(k):
            a, (s0, nr, _, _, _, _) = flat[k]
            return ins[a].at[s0:s0 + nr]

        def dst(k, q):
            a, (_, nr, per, first, rest, _) = flat[k]
            row = per * q + first + (rest - first) * jnp.minimum(q, 1)
            return outs[a].at[pl.ds(pl.multiple_of(row, 16), nr)]

        def ici(k, j, q):
            px, py = chips[j]
            return pltpu.make_async_remote_copy(
                src_ref=src(k), dst_ref=dst(k, q), send_sem=send_sems.at[k, j], recv_sem=recv_sems.at[k, j],
                device_id=(px, py, c), device_id_type=pl.DeviceIdType.MESH)

        def fwd(k, j):
            ref = dst(k, chip_of[j])
            return pltpu.make_async_remote_copy(
                src_ref=ref, dst_ref=ref, send_sem=fwd_send.at[k, j], recv_sem=fwd_recv.at[k, j],
                device_id=(x, y, 1 - c), device_id_type=pl.DeviceIdType.MESH)

        def wcopy(b, j, q):
            px, py = chips[j]
            return pltpu.make_async_remote_copy(
                src_ref=wins[b], dst_ref=wouts[b].at[q], send_sem=w_send.at[b, j], recv_sem=w_recv.at[b, j],
                device_id=(px, py, c), device_id_type=pl.DeviceIdType.MESH)

        local = [pltpu.make_async_copy(src(k), dst(k, mine), loc_sems.at[k]) for k in range(nk)]
        local += [pltpu.make_async_copy(wins[b], wouts[b].at[mine], w_loc.at[b]) for b in range(nw)]
        for z, (a, _, row0) in enumerate(zero_fills):
            local.append(pltpu.make_async_copy(zins[z], outs[a].at[row0:row0 + zins[z].shape[0]], z_sems.at[z]))
        wsends = [wcopy(b, j, mine) for b in range(nw) for j in range(3)]
        for cp in local + wsends:
            cp.start()

        for half in (0, 1):
            @pl.when(c == half)
            def _(half=half):
                my_k = [k for k in range(nk) if flat[k][1][5] == half]
                other_k = [k for k in range(nk) if flat[k][1][5] != half]
                sends = [ici(k, j, mine) for k in my_k for j in range(3)]
                for cp in sends:
                    cp.start()
                passed = []
                for k in my_k:
                    for j in range(3):
                        ici(k, j, chip_of[j]).wait_recv()
                        cp = fwd(k, j)
                        cp.start()
                        passed.append(cp)
                for k in other_k:
                    for j in range(3):
                        fwd(k, j).wait_recv()
                for cp in sends + passed:
                    cp.wait_send()

        for b in range(nw):
            for j in range(3):
                wcopy(b, j, chip_of[j]).wait_recv()
        for cp in wsends:
            cp.wait_send()
        for cp in local:
            cp.wait()

    hbm = pl.BlockSpec(memory_space=pl.ANY)
    dma = pltpu.SemaphoreType.DMA
    zeros = [z for _, z, _ in zero_fills]
    return pl.pallas_call(
        body, name="gather_weights",
        in_specs=[hbm] * (ns + nw + nz), out_specs=[hbm] * (ns + nw),
        out_shape=([jax.ShapeDtypeStruct((out_rows[a], split[a].shape[1]), split[a].dtype) for a in range(ns)]
                   + [jax.ShapeDtypeStruct((4,) + w.shape, w.dtype) for w in whole]),
        scratch_shapes=[dma((nk, 3)), dma((nk, 3)), dma((nk, 3)), dma((nk, 3)), dma((nk,)),
                        dma((nw, 3)), dma((nw, 3)), dma((nw,)), dma((nz,))],
        compiler_params=pltpu.CompilerParams(vmem_limit_bytes=VMEM_LIMIT),
    )(*split, *whole, *zeros)


def _reduce_grads(parts, small):
    n = len(parts)
    shapes = [a.shape[1:] for a in parts]
    halves = [(sh[0] // 2, sh[1]) for sh in shapes]

    def body(*refs):
        pin, sm_in = refs[:n], refs[n]
        gout, sm_out = refs[n + 1:2 * n + 1], refs[2 * n + 1]
        scr = refs[2 * n + 2:]
        own, sib, wire, rbuf = scr[:n], scr[n:2 * n], scr[2 * n:3 * n], scr[3 * n:4 * n]
        (sbuf, send_sems, recv_sems, loc_sems, pre_send, pre_recv, post_send, post_recv,
         sm_send, sm_recv) = scr[4 * n:]
        x, y, c = lax.axis_index("x"), lax.axis_index("y"), lax.axis_index("c")
        mine = 2 * x + y
        me = 4 * x + 2 * y + c
        sibling = (x, y, 1 - c)
        chips = [(1 - x, y), (x, 1 - y), (1 - x, 1 - y)]

        def rows(a, half):
            r2 = halves[a][0]
            return pl.ds(pl.multiple_of(half * r2, r2), r2)

        def pre(a):
            return pltpu.make_async_remote_copy(
                src_ref=pin[a].at[:, rows(a, 1 - c), :], dst_ref=sib[a], send_sem=pre_send.at[a],
                recv_sem=pre_recv.at[a], device_id=sibling, device_id_type=pl.DeviceIdType.MESH)

        def ici(a, j):
            px, py = chips[j]
            return pltpu.make_async_remote_copy(
                src_ref=wire[a].at[2 * px + py], dst_ref=rbuf[a].at[j], send_sem=send_sems.at[a, j],
                recv_sem=recv_sems.at[a, j], device_id=(px, py, c), device_id_type=pl.DeviceIdType.MESH)

        def post(a, half):
            ref = gout[a].at[rows(a, half), :]
            return pltpu.make_async_remote_copy(
                src_ref=ref, dst_ref=ref, send_sem=post_send.at[a], recv_sem=post_recv.at[a],
                device_id=sibling, device_id_type=pl.DeviceIdType.MESH)

        def small_copy(kk):
            peer = (x ^ (kk >> 2), y ^ ((kk >> 1) & 1), c ^ (kk & 1))
            return pltpu.make_async_remote_copy(
                src_ref=sm_in, dst_ref=sbuf.at[kk], send_sem=sm_send.at[kk - 1], recv_sem=sm_recv.at[kk - 1],
                device_id=peer, device_id_type=pl.DeviceIdType.MESH)

        local = [pltpu.make_async_copy(pin[a].at[:, rows(a, c), :], own[a], loc_sems.at[a]) for a in range(n)]
        pres = [pre(a) for a in range(n)]
        smalls = [small_copy(kk) for kk in range(1, 8)]
        for cp in local + pres + smalls:
            cp.start()
        sbuf[0] = sm_in[...]
        sends = []
        for a in range(n):
            local[a].wait()
            pres[a].wait_recv()
            for blk in range(4):
                tot = own[a][blk] + sib[a][blk]
                own[a][blk] = tot
                wire[a][blk] = tot.astype(BF16)
            for j in range(3):
                cp = ici(a, j)
                cp.start()
                sends.append(cp)
        for cp in smalls:
            cp.wait_recv()
        total = sbuf[me]
        for d in range(1, 8):
            total = total + sbuf[me ^ d]
        sm_out[...] = total
        posts = []
        for a in range(n):
            for j in range(3):
                ici(a, j).wait_recv()
            fin = own[a][mine]
            for j in range(3):
                fin = fin + rbuf[a][j].astype(F32)
            gout[a][rows(a, c), :] = fin
            cp = post(a, c)
            cp.start()
            posts.append(cp)
        for a in range(n):
            post(a, 1 - c).wait_recv()
        for cp in pres + sends + smalls + posts:
            cp.wait_send()

    hbm = pl.BlockSpec(memory_space=pl.ANY)
    vmem = pl.BlockSpec(memory_space=pltpu.VMEM)
    dma = pltpu.SemaphoreType.DMA
    return pl.pallas_call(
        body, name="reduce_grads",
        in_specs=[hbm] * n + [vmem], out_specs=[vmem] * (n + 1),
        out_shape=[jax.ShapeDtypeStruct(sh, F32) for sh in shapes] + [jax.ShapeDtypeStruct(small.shape, F32)],
        scratch_shapes=([pltpu.VMEM((4,) + hs, F32) for hs in halves] + [pltpu.VMEM((4,) + hs, F32) for hs in halves]
                        + [pltpu.VMEM((4,) + hs, BF16) for hs in halves]
                        + [pltpu.VMEM((3,) + hs, BF16) for hs in halves]
                        + [pltpu.VMEM((8,) + small.shape, F32), dma((n, 3)), dma((n, 3)), dma((n,)),
                           dma((n,)), dma((n,)), dma((n,)), dma((n,)), dma((7,)), dma((7,))]),
        compiler_params=pltpu.CompilerParams(vmem_limit_bytes=VMEM_LIMIT),
    )(*parts, small)


def _adamw(w, g, m, v, name):
    shape = w.shape
    w2, g2, m2, v2 = (a.reshape((-1, shape[-1])) for a in (w, g, m, v))

    def body(w_ref, g_ref, m_ref, v_ref, d_ref, nm_ref, nv_ref):
        gv = g_ref[...]
        nm = ADAM_B1 * m_ref[...] + (1.0 - ADAM_B1) * gv
        nv = ADAM_B2 * v_ref[...] + (1.0 - ADAM_B2) * (gv * gv)
        m_hat = nm / (1.0 - ADAM_B1 ** ADAM_STEP)
        v_hat = nv / (1.0 - ADAM_B2 ** ADAM_STEP)
        d_ref[...] = -ADAM_LR * (m_hat / (jnp.sqrt(v_hat) + ADAM_EPS) + ADAM_WD * w_ref[...])
        nm_ref[...] = nm
        nv_ref[...] = nv

    out = pl.pallas_call(
        body, name=name,
        out_shape=[jax.ShapeDtypeStruct(w2.shape, F32)] * 3,
        compiler_params=pltpu.CompilerParams(vmem_limit_bytes=VMEM_LIMIT),
    )(w2, g2, m2, v2)
    return tuple(a.reshape(shape) for a in out)


def kernel(x, meta_tokens, norm_g, w_in, q_norm_g, w_q_up, kv_norm_g, w_kv_up, conv_w, attn_out_g, conv_out_g, w_out, final_norm_g, loss_target, m_meta_tokens, m_norm_g, m_w_in, m_q_norm_g, m_w_q_up, m_kv_norm_g, m_w_kv_up, m_conv_w, m_attn_out_g, m_conv_out_g, m_w_out, m_final_norm_g, v_meta_tokens, v_norm_g, v_w_in, v_q_norm_g, v_w_q_up, v_kv_norm_g, v_w_kv_up, v_conv_w, v_attn_out_g, v_conv_out_g, v_w_out, v_final_norm_g):
    nb, s, _ = x.shape
    tm = ROW_TILE
    assert s % tm == 0 and tm % 8 == 0
    r = nb * s

    tr = lambda a: jnp.transpose(a[0])
    w_in_p, wq_p, wkv_p, w_out_f, g_cw, g_meta = _gather_weights(
        [tr(w_in).astype(BF16), tr(w_q_up).astype(BF16), tr(w_kv_up).astype(BF16), w_out[0].astype(BF16)],
        [W_IN_PIECES, W_Q_PIECES, W_KV_PIECES, W_OUT_PIECES], [IN_PAD, HEADS * QK_PAD, 1024, D_MODEL],
        [conv_w[0], meta_tokens],
        [(0, jnp.zeros((64, D_MODEL), BF16), 448)]
        + [(1, jnp.zeros((64, Q_RANK), BF16), QK_PAD * h + NOPE + ROPE) for h in range(HEADS)])
    conv_f = jnp.transpose(g_cw, (1, 0, 2)).reshape(3, CONV_W)
    meta_f = jnp.transpose(g_meta, (1, 0, 2)).reshape(N_META, D_MODEL)

    c_all, sa_all, sb_all = _rope_tables(N_META + s)
    tabs_m = (c_all[:N_META], sa_all[:N_META], sb_all[:N_META])
    tabs = (c_all[N_META:], sa_all[N_META:], sb_all[N_META:])
    gid = np.arange(CONV_W) // CONV_GROUP
    gmat = jnp.asarray(np.where(gid[:, None] == gid[None, :], 1.0 / CONV_GROUP, 0.0), BF16)
    ga, gc = attn_out_g, conv_out_g
    gf = final_norm_g.reshape(1, D_MODEL)

    x2d = x.reshape(r, D_MODEL)
    tgt2d = loss_target.reshape(r, D_MODEL)

    pm, _, km, vm = _fwd_proj(meta_f, tabs_m, norm_g, w_in_p, q_norm_g, wq_p, kv_norm_g, wkv_p,
                              1, N_META, N_META, "fwd_proj_meta")
    p, q, k, v = _fwd_proj(x2d, tabs, norm_g, w_in_p, q_norm_g, wq_p, kv_norm_g, wkv_p, nb, s, tm, "fwd_proj")
    o, lse = _attn_fwd(q, k, v, km, vm, nb, s, tm)
    dh2, dycat, dw_out, dgf, loss_acc = _out_fwd_bwd(x2d, tgt2d, o, p, pm, conv_f, ga, gc, gmat, w_out_f, gf,
                                                     nb, s, tm)
    dpb, do, delta, dccm, dga, dgc, dcw = _gate_bwd(dycat, o, p, pm, conv_f, ga, gc, gmat, nb, s, tm)
    dq, dk, dv, dkm, dvm = _attn_bwd(q, k, v, do, lse, delta, km, vm, nb, s, tm)
    dpa, dpam, p_q, p_kv, dgq, dgkv = _up_bwd(dq, dk, dv, dkm, dvm, p, pm, tabs, tabs_m, wq_p, wkv_p,
                                              q_norm_g, kv_norm_g, nb, s, tm)
    gx, gmeta, p_in, dng = _in_bwd(x2d, dh2, dpa, dpb, meta_f, dpam, dccm, pm, w_in_p, norm_g, nb, s, tm)

    p_out = dw_out.reshape(4, D_MODEL // 4, D_MODEL)
    flat = jnp.concatenate([dng.reshape(-1), dgq.reshape(-1), dgkv.reshape(-1), dga.reshape(-1), dgc.reshape(-1),
                            dgf.reshape(-1), dcw[:3].reshape(-1), gmeta.reshape(-1), loss_acc[0, 0:1]])
    n_small = flat.shape[0]
    rows_small = -(-n_small // 1024) * 8
    small = jnp.pad(flat, (0, rows_small * 128 - n_small)).reshape(rows_small, 128)
    g_w_in_t, g_w_q_t, g_w_kv_t, g_w_out, small_sum = _reduce_grads([p_in, p_q, p_kv, p_out], small)
    ssum = small_sum.reshape(-1)

    def take(off, n):
        return ssum[off:off + n], off + n

    off = 0
    g_norm, off = take(off, D_MODEL)
    g_qn, off = take(off, Q_RANK)
    g_kvn, off = take(off, KV_RANK)
    g_ga, off = take(off, CONV_W)
    g_gc, off = take(off, CONV_W)
    g_gf, off = take(off, D_MODEL)
    g_cw_all, off = take(off, 3 * CONV_W)
    g_meta_all, off = take(off, N_META * D_MODEL)
    loss = ssum[off]
    chip = 2 * lax.axis_index("x") + lax.axis_index("y")
    g_conv = lax.dynamic_slice(g_cw_all.reshape(3, CONV_W), (0, chip * 128), (3, 128))
    g_mt = lax.dynamic_slice(g_meta_all.reshape(N_META, D_MODEL), (0, chip * 256), (N_META, 256))

    grads = {
        "meta_tokens": g_mt, "norm_g": g_norm.reshape(1, -1), "w_in": g_w_in_t, "q_norm_g": g_qn.reshape(1, -1),
        "w_q_up": g_w_q_t, "kv_norm_g": g_kvn.reshape(1, -1), "w_kv_up": jnp.transpose(g_w_kv_t)[None],
        "conv_w": g_conv[None], "attn_out_g": g_ga.reshape(1, -1), "conv_out_g": g_gc.reshape(1, -1),
        "w_out": g_w_out[None], "final_norm_g": g_gf,
    }
    transposed = ("w_in", "w_q_up")
    weights = {
        "meta_tokens": (meta_tokens, m_meta_tokens, v_meta_tokens), "norm_g": (norm_g, m_norm_g, v_norm_g),
        "w_in": (w_in, m_w_in, v_w_in), "q_norm_g": (q_norm_g, m_q_norm_g, v_q_norm_g),
        "w_q_up": (w_q_up, m_w_q_up, v_w_q_up), "kv_norm_g": (kv_norm_g, m_kv_norm_g, v_kv_norm_g),
        "w_kv_up": (w_kv_up, m_w_kv_up, v_w_kv_up), "conv_w": (conv_w, m_conv_w, v_conv_w),
        "attn_out_g": (attn_out_g, m_attn_out_g, v_attn_out_g), "conv_out_g": (conv_out_g, m_conv_out_g, v_conv_out_g),
        "w_out": (w_out, m_w_out, v_w_out), "final_norm_g": (final_norm_g, m_final_norm_g, v_final_norm_g),
    }
    names = list(weights)
    deltas, new_m, new_v = [], [], []
    for nme in names:
        w_, m_, v_ = weights[nme]
        if nme in transposed:
            res = _adamw(tr(w_), grads[nme], tr(m_), tr(v_), "adamw_" + nme)
            g_, d_, nm_, nv_ = (jnp.transpose(a)[None] for a in (grads[nme],) + res)
        else:
            g_ = grads[nme].reshape(w_.shape)
            d_, nm_, nv_ = _adamw(w_, g_, m_, v_, "adamw_" + nme)
        grads[nme] = g_
        deltas.append(d_)
        new_m.append(nm_)
        new_v.append(nv_)

    grad_x = gx.reshape(nb, s, D_MODEL)
    return (loss, grad_x, *[grads[nme] for nme in names], *deltas, *new_m, *new_v)
```

```python
import functools

import jax
import jax.numpy as jnp
import numpy as np
from jax import lax
from jax.experimental import pallas as pl
from jax.experimental.pallas import tpu as pltpu

F32 = jnp.float32
BF16 = jnp.bfloat16

D_MODEL = 1024
N_META = 16
HEADS = 4
NOPE = 128
ROPE = 64
VDIM = 128
QK_PAD = 256
Q_RANK = 256
KV_RANK = 128
CONV_W = 512
CONV_GROUP = 64
ROPE_THETA = 10000.0
EPS = 1e-6
ATTN_SCALE = (NOPE + ROPE) ** -0.5
IN_DIM = 3008
IN_PAD = 3072
BLK_ZA, BLK_CB, BLK_CC, BLK_CH, BLK_ZC = 1, 2, 3, 4, 5
NEG_INF = -1e30

ADAM_LR = 0.001
ADAM_B1 = 0.9
ADAM_B2 = 0.999
ADAM_EPS = 1e-08
ADAM_WD = 0.01
ADAM_STEP = 10

ROW_TILE = 512
ATTN_TILE = 256
VMEM_LIMIT = 56 * 1024 * 1024

NT = (((1,), (1,)), ((), ()))
TN = (((0,), (0,)), ((), ()))


def _cparams(*sem):
    return pltpu.CompilerParams(dimension_semantics=sem, vmem_limit_bytes=VMEM_LIMIT)


def _dot(a, b):
    return jnp.dot(a, b, preferred_element_type=F32)


def _dot_nt(a, b):
    return lax.dot_general(a, b, NT, preferred_element_type=F32)


def _dot_tn(a, b):
    return lax.dot_general(a, b, TN, preferred_element_type=F32)


def _rms(x, g):
    r = lax.rsqrt(jnp.mean(x * x, axis=-1, keepdims=True) + EPS)
    return x * r * g, r


def _rms_bwd(dy, x, r, g):
    xh = x * r
    dyg = dy * g
    dx = r * (dyg - xh * jnp.mean(dyg * xh, axis=-1, keepdims=True))
    return dx, dy * xh


def _sigmoid(z):
    return 1.0 / (1.0 + jnp.exp(-z))


def _rope(b, c, sa, sb):
    return b * c + pltpu.roll(b, 96, 1) * sa + pltpu.roll(b, 32, 1) * sb


def _rope_bwd(d, c, sa, sb):
    return d * c + pltpu.roll(d * sa, 32, 1) + pltpu.roll(d * sb, 96, 1)


def _group_mean(x, gmat):
    hi = x.astype(BF16)
    lo = (x - hi.astype(F32)).astype(BF16)
    return _dot(hi, gmat) + _dot(lo, gmat)


def _row_of(col, rows):
    return jnp.transpose(jnp.broadcast_to(col, (rows, 128)))[0:1, :]


def _rope_tables(n_pos):
    half = ROPE // 2
    inv_freq = (np.float32(1.0) / (np.float32(ROPE_THETA) ** (np.arange(half, dtype=np.float32) / np.float32(half))))
    ang = np.arange(n_pos, dtype=np.float32)[:, None] * inv_freq.astype(np.float32)[None, :]
    cos, sin = np.cos(ang).astype(np.float32), np.sin(ang).astype(np.float32)
    z = np.zeros((n_pos, half), np.float32)
    c = np.concatenate([cos, cos, z, z], axis=1)
    sa = np.concatenate([-sin, z, z, z], axis=1)
    sb = np.concatenate([z, sin, z, z], axis=1)
    return jnp.asarray(c), jnp.asarray(sa), jnp.asarray(sb)


def _fwd_proj(x2d, tabs, norm_g, w_in_p, q_norm_g, wq_p, kv_norm_g, wkv_p, nb, s, tm, name):
    nt = s // tm
    c_t, sa_t, sb_t = tabs

    def body(x_ref, c_ref, sa_ref, sb_ref, g_ref, w_ref, gq_ref, wq_ref, gkv_ref, wkv_ref,
             p_ref, q_ref, k_ref, v_ref):
        u, _ = _rms(x_ref[...], g_ref[...])
        p = _dot_nt(u.astype(BF16), w_ref[...])
        p_ref[...] = p
        c, sa, sb = c_ref[...], sa_ref[...], sb_ref[...]
        qn, _ = _rms(p[:, 0:Q_RANK], gq_ref[...])
        q = _dot_nt(qn.astype(BF16), wq_ref[...])
        kvn, _ = _rms(p[:, Q_RANK:Q_RANK + KV_RANK], gkv_ref[...])
        kv = _dot_nt(kvn.astype(BF16), wkv_ref[...])
        kpe = _rope(p[:, 384:512], c, sa, sb)
        for h in range(HEADS):
            pe = _rope(q[:, QK_PAD * h + NOPE:QK_PAD * (h + 1)], c, sa, sb)
            qh = jnp.concatenate([q[:, QK_PAD * h:QK_PAD * h + NOPE], pe], axis=1)
            q_ref[0, h] = (qh * ATTN_SCALE).astype(BF16)
            k_ref[0, h] = jnp.concatenate([kv[:, NOPE * h:NOPE * (h + 1)], kpe], axis=1).astype(BF16)
            v_ref[0, h] = kv[:, 512 + VDIM * h:512 + VDIM * (h + 1)].astype(BF16)

    full = lambda a: pl.BlockSpec(a.shape, lambda i: (0,) * a.ndim)
    tab = pl.BlockSpec((tm, 128), lambda i: (i % nt, 0))
    return pl.pallas_call(
        body, name=name, grid=(nb * nt,),
        in_specs=[pl.BlockSpec((tm, D_MODEL), lambda i: (i, 0)), tab, tab, tab,
                  full(norm_g), full(w_in_p), full(q_norm_g), full(wq_p), full(kv_norm_g), full(wkv_p)],
        out_specs=[pl.BlockSpec((tm, IN_PAD), lambda i: (i, 0)),
                   pl.BlockSpec((1, HEADS, tm, QK_PAD), lambda i: (i // nt, 0, i % nt, 0)),
                   pl.BlockSpec((1, HEADS, tm, QK_PAD), lambda i: (i // nt, 0, i % nt, 0)),
                   pl.BlockSpec((1, HEADS, tm, VDIM), lambda i: (i // nt, 0, i % nt, 0))],
        out_shape=[jax.ShapeDtypeStruct((nb * s, IN_PAD), F32),
                   jax.ShapeDtypeStruct((nb, HEADS, s, QK_PAD), BF16),
                   jax.ShapeDtypeStruct((nb, HEADS, s, QK_PAD), BF16),
                   jax.ShapeDtypeStruct((nb, HEADS, s, VDIM), BF16)],
        compiler_params=_cparams("parallel"),
    )(x2d, c_t, sa_t, sb_t, norm_g, w_in_p, q_norm_g, wq_p, kv_norm_g, wkv_p)


def _attn_fwd(q, k, v, km, vm, nb, s, tq):
    nq = s // tq

    def body(q_ref, k_ref, v_ref, km_ref, vm_ref, o_ref, lse_ref, s_scr, p_scr):
        row = lax.broadcasted_iota(jnp.int32, (tq, tq), 0)
        col = lax.broadcasted_iota(jnp.int32, (tq, tq), 1)
        for i in range(nq):
            slot = i % 2
            qi = q_ref[0, 0, i * tq:(i + 1) * tq, :]
            sm = _dot_nt(qi, km_ref[0, 0])
            m128 = None
            for j in range(i + 1):
                sc = _dot_nt(qi, k_ref[0, 0, j * tq:(j + 1) * tq, :])
                if j == i:
                    sc = jnp.where(col <= row, sc, NEG_INF)
                s_scr[slot, :, j * tq:(j + 1) * tq] = sc
                mx = sc[:, 0:128]
                for c0 in range(128, tq, 128):
                    mx = jnp.maximum(mx, sc[:, c0:c0 + 128])
                m128 = mx if m128 is None else jnp.maximum(m128, mx)
            m = jnp.maximum(jnp.max(m128, axis=1, keepdims=True), jnp.max(sm, axis=1, keepdims=True))
            pm = jnp.exp(sm - m)
            l128 = None
            for j in range(i + 1):
                p = jnp.exp(s_scr[slot, :, j * tq:(j + 1) * tq] - m)
                p_scr[slot, :, j * tq:(j + 1) * tq] = p.astype(BF16)
                ps = p[:, 0:128]
                for c0 in range(128, tq, 128):
                    ps = ps + p[:, c0:c0 + 128]
                l128 = ps if l128 is None else l128 + ps
            l = jnp.sum(l128, axis=1, keepdims=True) + jnp.sum(pm, axis=1, keepdims=True)
            n = (i + 1) * tq
            acc = _dot(p_scr[slot, :, 0:n], v_ref[0, 0, 0:n, :]) + _dot(pm.astype(BF16), vm_ref[0, 0])
            o_ref[0, 0, i * tq:(i + 1) * tq, :] = acc / l
            lse_ref[0, 0, :, i * tq:(i + 1) * tq] = _row_of(m + jnp.log(l), tq)

    hblk = lambda w: pl.BlockSpec((1, 1, s, w), lambda b, h: (b, h, 0, 0))
    mblk = lambda w: pl.BlockSpec((1, 1, N_META, w), lambda b, h: (0, h, 0, 0))
    return pl.pallas_call(
        body, name="attn_fwd", grid=(nb, HEADS),
        in_specs=[hblk(QK_PAD), hblk(QK_PAD), hblk(VDIM), mblk(QK_PAD), mblk(VDIM)],
        out_specs=[hblk(VDIM), pl.BlockSpec((1, 1, 1, s), lambda b, h: (b, h, 0, 0))],
        out_shape=[jax.ShapeDtypeStruct((nb, HEADS, s, VDIM), F32),
                   jax.ShapeDtypeStruct((nb, HEADS, 1, s), F32)],
        scratch_shapes=[pltpu.VMEM((2, tq, s), F32), pltpu.VMEM((2, tq, s), BF16)],
        compiler_params=_cparams("parallel", "parallel"),
    )(q, k, v, km, vm)


def _shift_rows(a, prev, n_rows):
    rid = lax.broadcasted_iota(jnp.int32, a.shape, 0)
    a1 = jnp.where(rid == 0, prev[7:8, :], pltpu.roll(a, 1, 0))
    a2 = jnp.where(rid == 0, prev[6:7, :], jnp.where(rid == 1, prev[7:8, :], pltpu.roll(a, 2, 0)))
    return a1, a2


def _attn_gate(o, za, ga_h):
    on, r = _rms(o, ga_h)
    return on * (za * _sigmoid(za)), on, r


def _out_fwd_bwd(x2d, tgt2d, o, p, pm, conv_w, ga, gc, gmat, w_out, gf, nb, s, tm):
    nt = s // tm
    r = nb * s
    prev_idx = lambda i: jnp.maximum(i * (tm // 8) - 1, 0)

    def body(x_ref, t_ref, o_ref, za_ref, cb_ref, cc_ref, ch_ref, zc_ref, ccp_ref, chp_ref, mc_ref, mh_ref,
             cw_ref, ga_ref, gc_ref, gm_ref, w_ref, gf_ref,
             dh_ref, dy_ref, dw_ref, dgf_ref, loss_ref):
        i = pl.program_id(0)

        @pl.when(i == 0)
        def _():
            dw_ref[...] = jnp.zeros_like(dw_ref)
            dgf_ref[...] = jnp.zeros_like(dgf_ref)
            loss_ref[...] = jnp.zeros_like(loss_ref)

        ya = []
        for h in range(HEADS):
            y, _, _ = _attn_gate(o_ref[0, h], za_ref[:, VDIM * h:VDIM * (h + 1)],
                                 ga_ref[:, VDIM * h:VDIM * (h + 1)])
            ya.append(y)
        cc = cc_ref[...] * ch_ref[...]
        prev = jnp.where(i % nt == 0, mc_ref[8:16, :] * mh_ref[8:16, :], ccp_ref[...] * chp_ref[...])
        cc1, cc2 = _shift_rows(cc, prev, tm)
        yc = cb_ref[...] * (cw_ref[0:1, :] * cc2 + cw_ref[1:2, :] * cc1 + cw_ref[2:3, :] * cc)
        rg = lax.rsqrt(_group_mean(yc * yc, gm_ref[...]) + EPS)
        zc = zc_ref[...]
        yconv = yc * rg * gc_ref[...] * (zc * _sigmoid(zc))
        ycat = jnp.concatenate(ya + [yconv], axis=1).astype(BF16)
        h2 = x_ref[...] + _dot(ycat, w_ref[...])
        gfv = gf_ref[...]
        y, r2 = _rms(h2, gfv)
        e = y - t_ref[...]
        loss_ref[...] += 0.5 * jnp.sum(e * e) / D_MODEL
        dyv = e * (1.0 / D_MODEL)
        dh2, dgf = _rms_bwd(dyv, h2, r2, gfv)
        dgf_ref[...] += jnp.sum(dgf, axis=0, keepdims=True)
        dh_ref[...] = dh2
        dhb = dh2.astype(BF16)
        dy_ref[...] = _dot_nt(dhb, w_ref[...])
        dw_ref[...] += _dot_tn(ycat, dhb)

    row = lambda w, j: pl.BlockSpec((tm, w), lambda i: (i, j))
    pblk = lambda j: pl.BlockSpec((tm, 512), lambda i: (i, j))
    pprev = lambda j: pl.BlockSpec((8, 512), lambda i: (prev_idx(i), j))
    mblk = lambda j: pl.BlockSpec((N_META, 512), lambda i: (0, j))
    full = lambda a: pl.BlockSpec(a.shape, lambda i: (0,) * a.ndim)
    return pl.pallas_call(
        body, name="out_fwd_bwd", grid=(nb * nt,),
        in_specs=[row(D_MODEL, 0), row(D_MODEL, 0),
                  pl.BlockSpec((1, HEADS, tm, VDIM), lambda i: (i // nt, 0, i % nt, 0)),
                  pblk(BLK_ZA), pblk(BLK_CB), pblk(BLK_CC), pblk(BLK_CH), pblk(BLK_ZC),
                  pprev(BLK_CC), pprev(BLK_CH), mblk(BLK_CC), mblk(BLK_CH),
                  full(conv_w), full(ga), full(gc), full(gmat), full(w_out), full(gf)],
        out_specs=[row(D_MODEL, 0), row(D_MODEL, 0),
                   pl.BlockSpec((D_MODEL, D_MODEL), lambda i: (0, 0)),
                   pl.BlockSpec((1, D_MODEL), lambda i: (0, 0)),
                   pl.BlockSpec((1, 128), lambda i: (0, 0))],
        out_shape=[jax.ShapeDtypeStruct((r, D_MODEL), F32), jax.ShapeDtypeStruct((r, D_MODEL), F32),
                   jax.ShapeDtypeStruct((D_MODEL, D_MODEL), F32), jax.ShapeDtypeStruct((1, D_MODEL), F32),
                   jax.ShapeDtypeStruct((1, 128), F32)],
        compiler_params=_cparams("arbitrary"),
    )(x2d, tgt2d, o, p, p, p, p, p, p, p, pm, pm, conv_w, ga, gc, gmat, w_out, gf)


def _gate_bwd(dycat, o, p, pm, conv_w, ga, gc, gmat, nb, s, tm):
    nt = s // tm
    r = nb * s
    ext = tm + 8
    prev_idx = lambda i: jnp.maximum(i * (tm // 8) - 1, 0)
    next_idx = lambda i: jnp.minimum((i + 1) * (tm // 8), r // 8 - 1)

    def body(dya_ref, dyc_ref, dycn_ref, o_ref, za_ref, cb_ref, cbn_ref, cc_ref, ccp_ref, ccn_ref,
             ch_ref, chp_ref, chn_ref, zc_ref, zcn_ref, mc_ref, mh_ref, cw_ref, ga_ref, gc_ref, gm_ref,
             dpb_ref, do_ref, dl_ref, dccm_ref, dga_ref, dgc_ref, dcw_ref):
        i = pl.program_id(0)

        @pl.when(i == 0)
        def _():
            dga_ref[...] = jnp.zeros_like(dga_ref)
            dgc_ref[...] = jnp.zeros_like(dgc_ref)
            dcw_ref[...] = jnp.zeros_like(dcw_ref)

        dga = []
        for h in range(HEADS):
            hs = slice(VDIM * h, VDIM * (h + 1))
            oh, za, gah, dya = o_ref[0, h], za_ref[:, hs], ga_ref[:, hs], dya_ref[:, hs]
            sg = _sigmoid(za)
            on, ro = _rms(oh, gah)
            don = dya * (za * sg)
            dpb_ref[:, hs] = (dya * on * (sg * (1.0 + za * (1.0 - sg)))).astype(BF16)
            do, dg = _rms_bwd(don, oh, ro, gah)
            dga.append(jnp.sum(dg, axis=0, keepdims=True))
            dob = do.astype(BF16)
            do_ref[0, h] = dob
            dl_ref[0, h] = _row_of(jnp.sum(dob.astype(F32) * oh, axis=1, keepdims=True), tm)
        dga_ref[...] += jnp.concatenate(dga, axis=1)

        cat = lambda a, b: jnp.concatenate([a[...], b[...]], axis=0)
        cch = cat(cc_ref, ccn_ref)
        chh = cat(ch_ref, chn_ref)
        cb = cat(cb_ref, cbn_ref)
        zc = cat(zc_ref, zcn_ref)
        dy = cat(dyc_ref, dycn_ref)
        first = i % nt == 0
        last = i % nt == nt - 1
        cc = cch * chh
        prev = jnp.where(first, mc_ref[8:16, :] * mh_ref[8:16, :], ccp_ref[...] * chp_ref[...])
        cc1, cc2 = _shift_rows(cc, prev, ext)
        w0, w1, w2 = cw_ref[0:1, :], cw_ref[1:2, :], cw_ref[2:3, :]
        dw = w0 * cc2 + w1 * cc1 + w2 * cc
        yc = cb * dw
        rg = lax.rsqrt(_group_mean(yc * yc, gm_ref[...]) + EPS)
        ych = yc * rg
        gcv = gc_ref[...]
        sg = _sigmoid(zc)
        dycn = dy * (zc * sg)
        dzc = dy * (ych * gcv) * (sg * (1.0 + zc * (1.0 - sg)))
        dgc_ref[...] += jnp.sum((dycn * ych)[:tm], axis=0, keepdims=True)
        dycg = dycn * gcv
        dyc = rg * (dycg - ych * _group_mean(dycg * ych, gm_ref[...]))
        rid = lax.broadcasted_iota(jnp.int32, (ext, CONV_W), 0)
        ddw = jnp.where(jnp.logical_and(last, rid >= tm), 0.0, dyc * cb)
        dcb = dyc * dw
        dcc = w2 * ddw + w1 * pltpu.roll(ddw, ext - 1, 0) + w0 * pltpu.roll(ddw, ext - 2, 0)
        dpb_ref[:, 512:1024] = dcb[:tm].astype(BF16)
        dpb_ref[:, 1024:1536] = (dcc * chh)[:tm].astype(BF16)
        dpb_ref[:, 1536:2048] = (dcc * cch)[:tm].astype(BF16)
        dpb_ref[:, 2048:2560] = dzc[:tm].astype(BF16)
        rs = lambda a: jnp.sum(a[:tm], axis=0, keepdims=True)
        dcw_ref[0:1, :] += rs(ddw * cc2)
        dcw_ref[1:2, :] += rs(ddw * cc1)
        dcw_ref[2:3, :] += rs(ddw * cc)

        @pl.when(first)
        def _():
            d0, d1 = ddw[0:1, :], ddw[1:2, :]
            r8 = lax.broadcasted_iota(jnp.int32, (8, CONV_W), 0)
            dccm_ref[0] = jnp.where(r8 == 7, w1 * d0 + w0 * d1, jnp.where(r8 == 6, w0 * d0, 0.0))

    row = lambda j: pl.BlockSpec((tm, 512), lambda i: (i, j))
    prv = lambda j: pl.BlockSpec((8, 512), lambda i: (prev_idx(i), j))
    nxt = lambda j: pl.BlockSpec((8, 512), lambda i: (next_idx(i), j))
    mblk = lambda j: pl.BlockSpec((N_META, 512), lambda i: (0, j))
    full = lambda a: pl.BlockSpec(a.shape, lambda i: (0,) * a.ndim)
    hb = lambda w: pl.BlockSpec((1, HEADS, tm, w), lambda i: (i // nt, 0, i % nt, 0))
    acc = lambda rr: pl.BlockSpec((rr, 512), lambda i: (0, 0))
    return pl.pallas_call(
        body, name="gate_bwd", grid=(nb * nt,),
        in_specs=[row(0), row(1), nxt(1), hb(VDIM),
                  row(BLK_ZA), row(BLK_CB), nxt(BLK_CB), row(BLK_CC), prv(BLK_CC), nxt(BLK_CC),
                  row(BLK_CH), prv(BLK_CH), nxt(BLK_CH), row(BLK_ZC), nxt(BLK_ZC),
                  mblk(BLK_CC), mblk(BLK_CH), full(conv_w), full(ga), full(gc), full(gmat)],
        out_specs=[pl.BlockSpec((tm, 2560), lambda i: (i, 0)), hb(VDIM),
                   pl.BlockSpec((1, HEADS, 1, tm), lambda i: (i // nt, 0, 0, i % nt)),
                   pl.BlockSpec((1, 8, 512), lambda i: (i // nt, 0, 0)),
                   acc(1), acc(1), acc(8)],
        out_shape=[jax.ShapeDtypeStruct((r, 2560), BF16), jax.ShapeDtypeStruct((nb, HEADS, s, VDIM), BF16),
                   jax.ShapeDtypeStruct((nb, HEADS, 1, s), F32), jax.ShapeDtypeStruct((nb, 8, 512), F32),
                   jax.ShapeDtypeStruct((1, 512), F32), jax.ShapeDtypeStruct((1, 512), F32),
                   jax.ShapeDtypeStruct((8, 512), F32)],
        compiler_params=_cparams("arbitrary"),
    )(dycat, dycat, dycat, o, p, p, p, p, p, p, p, p, p, p, p, pm, pm, conv_w, ga, gc, gmat)


def _attn_bwd(q, k, v, do, lse, delta, km, vm, nb, s, t):
    n = s // t

    def body(q_ref, k_ref, v_ref, do_ref, lse_ref, dl_ref, km_ref, vm_ref,
             dq_ref, dk_ref, dv_ref, dkm_ref, dvm_ref, p_scr, ds_scr):
        b = pl.program_id(1)

        @pl.when(b == 0)
        def _():
            dkm_ref[...] = jnp.zeros_like(dkm_ref)
            dvm_ref[...] = jnp.zeros_like(dvm_ref)

        kr = lax.broadcasted_iota(jnp.int32, (t, t), 0)
        qc = lax.broadcasted_iota(jnp.int32, (t, t), 1)
        km_v, vm_v = km_ref[0, 0], vm_ref[0, 0]
        ptm = jnp.exp(_dot_nt(km_v, q_ref[0, 0]) - lse_ref[0, 0])
        dstm = (ptm * (_dot_nt(vm_v, do_ref[0, 0]) - dl_ref[0, 0])).astype(BF16)
        dkm_ref[0] += _dot(dstm, q_ref[0, 0])
        dvm_ref[0] += _dot(ptm.astype(BF16), do_ref[0, 0])
        dq_ref[0, 0] = _dot_tn(dstm, km_v)
        for j in range(n):
            slot = j % 2
            kj = k_ref[0, 0, j * t:(j + 1) * t, :]
            vj = v_ref[0, 0, j * t:(j + 1) * t, :]
            for i in range(j, n):
                cs = slice(i * t, (i + 1) * t)
                qi = q_ref[0, 0, cs, :]
                doi = do_ref[0, 0, cs, :]
                st = _dot_nt(kj, qi)
                if i == j:
                    st = jnp.where(kr <= qc, st, NEG_INF)
                pt = jnp.exp(st - lse_ref[0, 0, :, cs])
                dst = (pt * (_dot_nt(vj, doi) - dl_ref[0, 0, :, cs])).astype(BF16)
                p_scr[slot, :, cs] = pt.astype(BF16)
                ds_scr[slot, :, cs] = dst
                dq_ref[0, 0, cs, :] += _dot_tn(dst, kj)
            dv_ref[0, 0, j * t:(j + 1) * t, :] = _dot(p_scr[slot, :, j * t:s], do_ref[0, 0, j * t:s, :])
            dk_ref[0, 0, j * t:(j + 1) * t, :] = _dot(ds_scr[slot, :, j * t:s], q_ref[0, 0, j * t:s, :])

    big = lambda w: pl.BlockSpec((1, 1, s, w), lambda h, b: (b, h, 0, 0))
    rowv = pl.BlockSpec((1, 1, 1, s), lambda h, b: (b, h, 0, 0))
    mk = lambda w: pl.BlockSpec((1, 1, N_META, w), lambda h, b: (0, h, 0, 0))
    mo = lambda w: pl.BlockSpec((1, N_META, w), lambda h, b: (h, 0, 0))
    return pl.pallas_call(
        body, name="attn_bwd", grid=(HEADS, nb),
        in_specs=[big(QK_PAD), big(QK_PAD), big(VDIM), big(VDIM), rowv, rowv, mk(QK_PAD), mk(VDIM)],
        out_specs=[big(QK_PAD), big(QK_PAD), big(VDIM), mo(QK_PAD), mo(VDIM)],
        out_shape=[jax.ShapeDtypeStruct((nb, HEADS, s, QK_PAD), F32),
                   jax.ShapeDtypeStruct((nb, HEADS, s, QK_PAD), F32),
                   jax.ShapeDtypeStruct((nb, HEADS, s, VDIM), F32),
                   jax.ShapeDtypeStruct((HEADS, N_META, QK_PAD), F32),
                   jax.ShapeDtypeStruct((HEADS, N_META, VDIM), F32)],
        scratch_shapes=[pltpu.VMEM((2, t, s), BF16), pltpu.VMEM((2, t, s), BF16)],
        compiler_params=_cparams("arbitrary", "arbitrary"),
    )(q, k, v, do, lse, delta, km, vm)


def _up_bwd(dq, dk, dv, dkm, dvm, p, pm, tabs, tabs_m, wq_p, wkv_p, gq, gkv, nb, s, tm):
    nt = s // tm
    n = nb * nt
    c_t, sa_t, sb_t = tabs
    cm_t, sam_t, sbm_t = tabs_m

    def kv_path(dkh, dvh, pa, c, sa, sb, wkv, gkvv):
        dkpe = dkh[0][:, NOPE:]
        for h in range(1, HEADS):
            dkpe = dkpe + dkh[h][:, NOPE:]
        dkr = _rope_bwd(dkpe, c, sa, sb)
        dkv = jnp.concatenate([d[:, :NOPE] for d in dkh] + list(dvh), axis=1).astype(BF16)
        ckv = pa[:, Q_RANK:Q_RANK + KV_RANK]
        kvn, rkv = _rms(ckv, gkvv)
        dckv, dg = _rms_bwd(_dot(dkv, wkv), ckv, rkv, gkvv)
        return dckv, dkr, kvn.astype(BF16), dkv, jnp.sum(dg, axis=0, keepdims=True)

    def body(dq_ref, dk_ref, dv_ref, pa_ref, c_ref, sa_ref, sb_ref,
             dkm_ref, dvm_ref, pam_ref, cm_ref, sam_ref, sbm_ref,
             wq_ref, wkv_ref, gq_ref, gkv_ref,
             dpa_ref, dpam_ref, pq_ref, pkv_ref, dgq_ref, dgkv_ref, dwq_ref, dwkv_ref):
        i = pl.program_id(0)

        @pl.when(i == 0)
        def _():
            dwq_ref[...] = jnp.zeros_like(dwq_ref)
            dwkv_ref[...] = jnp.zeros_like(dwkv_ref)
            dgq_ref[...] = jnp.zeros_like(dgq_ref)
            dgkv_ref[...] = jnp.zeros_like(dgkv_ref)

        @pl.when(i < n)
        def _():
            c, sa, sb = c_ref[...], sa_ref[...], sb_ref[...]
            pa = pa_ref[...]
            parts = []
            for h in range(HEADS):
                dqh = dq_ref[0, h] * ATTN_SCALE
                parts += [dqh[:, :NOPE], _rope_bwd(dqh[:, NOPE:], c, sa, sb)]
            dql = jnp.concatenate(parts, axis=1).astype(BF16)
            cq = pa[:, 0:Q_RANK]
            gqv = gq_ref[...]
            qn, rq = _rms(cq, gqv)
            dwq_ref[...] += _dot_tn(dql, qn.astype(BF16))
            dcq, dg = _rms_bwd(_dot(dql, wq_ref[...]), cq, rq, gqv)
            dgq_ref[...] += jnp.sum(dg, axis=0, keepdims=True)
            dckv, dkr, kvn, dkv, dgk = kv_path([dk_ref[0, h] for h in range(HEADS)],
                                               [dv_ref[0, h] for h in range(HEADS)],
                                               pa, c, sa, sb, wkv_ref[...], gkv_ref[...])
            dwkv_ref[...] += _dot_tn(dkv, kvn)
            dgkv_ref[...] += dgk
            dpa_ref[...] = jnp.concatenate([dcq, dckv, dkr], axis=1).astype(BF16)

        @pl.when(i == n)
        def _():
            dckv, dkr, kvn, dkv, dgk = kv_path([dkm_ref[h] for h in range(HEADS)],
                                               [dvm_ref[h] for h in range(HEADS)],
                                               pam_ref[...], cm_ref[...], sam_ref[...], sbm_ref[...],
                                               wkv_ref[...], gkv_ref[...])
            dwkv_ref[...] += _dot_tn(dkv, kvn)
            dgkv_ref[...] += dgk
            dpam_ref[...] = jnp.concatenate([jnp.zeros((N_META, Q_RANK), F32), dckv, dkr], axis=1)
            for h in range(HEADS):
                pq_ref[h] = dwq_ref[QK_PAD * h:QK_PAD * h + NOPE + ROPE, :]
                pkv_ref[h, 0:NOPE, :] = dwkv_ref[NOPE * h:NOPE * (h + 1), :]
                pkv_ref[h, NOPE:NOPE + VDIM, :] = dwkv_ref[512 + VDIM * h:512 + VDIM * (h + 1), :]

    cl = lambda i: jnp.minimum(i, n - 1)
    hb = lambda w: pl.BlockSpec((1, HEADS, tm, w), lambda i: (cl(i) // nt, 0, cl(i) % nt, 0))
    tab = pl.BlockSpec((tm, 128), lambda i: (cl(i) % nt, 0))
    full = lambda a: pl.BlockSpec(a.shape, lambda i: (0,) * a.ndim)
    const = lambda shape: pl.BlockSpec(shape, lambda i: (0,) * len(shape))
    return pl.pallas_call(
        body, name="up_bwd", grid=(n + 1,),
        in_specs=[hb(QK_PAD), hb(QK_PAD), hb(VDIM), pl.BlockSpec((tm, 512), lambda i: (cl(i), 0)), tab, tab, tab,
                  full(dkm), full(dvm), pl.BlockSpec((N_META, 512), lambda i: (0, 0)),
                  full(cm_t), full(sam_t), full(sbm_t), full(wq_p), full(wkv_p), full(gq), full(gkv)],
        out_specs=[pl.BlockSpec((tm, 512), lambda i: (cl(i), 0)), const((N_META, 512)),
                   const((HEADS, NOPE + ROPE, Q_RANK)), const((HEADS, NOPE + VDIM, KV_RANK)),
                   const((1, Q_RANK)), const((1, KV_RANK))],
        out_shape=[jax.ShapeDtypeStruct((nb * s, 512), BF16), jax.ShapeDtypeStruct((N_META, 512), F32),
                   jax.ShapeDtypeStruct((HEADS, NOPE + ROPE, Q_RANK), F32),
                   jax.ShapeDtypeStruct((HEADS, NOPE + VDIM, KV_RANK), F32),
                   jax.ShapeDtypeStruct((1, Q_RANK), F32), jax.ShapeDtypeStruct((1, KV_RANK), F32)],
        scratch_shapes=[pltpu.VMEM((HEADS * QK_PAD, Q_RANK), F32), pltpu.VMEM((1024, KV_RANK), F32)],
        compiler_params=_cparams("arbitrary"),
    )(dq, dk, dv, p, c_t, sa_t, sb_t, dkm, dvm, pm, cm_t, sam_t, sbm_t, wq_p, wkv_p, gq, gkv)


def _in_bwd(x2d, dh2, dpa, dpb, meta, dpam, dccm, pm, w_in_p, norm_g, nb, s, tm):
    nt = s // tm
    n = nb * nt

    def body(x_ref, dh_ref, dpa_ref, dpb_ref, mt_ref, dpam_ref, dccm_ref, mc_ref, mh_ref, w_ref, g_ref,
             gx_ref, gm_ref, dw_hbm, dg_ref, acc_ref, sems):
        i = pl.program_id(0)

        @pl.when(i == 0)
        def _():
            acc_ref[...] = jnp.zeros_like(acc_ref)
            dg_ref[...] = jnp.zeros_like(dg_ref)

        def rows(x, dp, dres):
            g = g_ref[...]
            u, r1 = _rms(x, g)
            dpb16 = dp.astype(BF16)
            acc_ref[...] += _dot_tn(dpb16, u.astype(BF16))
            dx, dg = _rms_bwd(_dot(dpb16, w_ref[...]), x, r1, g)
            dg_ref[...] += jnp.sum(dg, axis=0, keepdims=True)
            return dx if dres is None else dx + dres

        @pl.when(i < n)
        def _():
            dp = jnp.concatenate([dpa_ref[...], dpb_ref[...]], axis=1)
            gx_ref[...] = rows(x_ref[...], dp, dh_ref[...])

        @pl.when(i == n)
        def _():
            dcc = dccm_ref[0]
            for b in range(1, nb):
                dcc = dcc + dccm_ref[b]
            z8 = jnp.zeros((8, CONV_W), F32)
            dc = jnp.concatenate([z8, dcc * mh_ref[8:16, :]], axis=0)
            dh = jnp.concatenate([z8, dcc * mc_ref[8:16, :]], axis=0)
            z = jnp.zeros((N_META, CONV_W), F32)
            dp = jnp.concatenate([dpam_ref[...], z, z, dc, dh, z], axis=1)
            gm_ref[...] = rows(mt_ref[...], dp, None)
            per = IN_DIM // 4
            cps = [pltpu.make_async_copy(acc_ref.at[0:448], dw_hbm.at[0, 0:448], sems.at[0]),
                   pltpu.make_async_copy(acc_ref.at[512:per + 64], dw_hbm.at[0, 448:per], sems.at[1])]
            for qq in range(1, 4):
                cps.append(pltpu.make_async_copy(acc_ref.at[per * qq + 64:per * (qq + 1) + 64], dw_hbm.at[qq],
                                                 sems.at[qq + 1]))
            for cp in cps:
                cp.start()
            for cp in cps:
                cp.wait()

    cl = lambda i: jnp.minimum(i, n - 1)
    row = lambda w: pl.BlockSpec((tm, w), lambda i: (cl(i), 0))
    full = lambda a: pl.BlockSpec(a.shape, lambda i: (0,) * a.ndim)
    mblk = lambda j: pl.BlockSpec((N_META, 512), lambda i: (0, j))
    return pl.pallas_call(
        body, name="in_bwd", grid=(n + 1,),
        in_specs=[row(D_MODEL), row(D_MODEL), row(512), row(2560), full(meta), full(dpam), full(dccm),
                  mblk(BLK_CC), mblk(BLK_CH), full(w_in_p), full(norm_g)],
        out_specs=[row(D_MODEL), pl.BlockSpec((N_META, D_MODEL), lambda i: (0, 0)),
                   pl.BlockSpec(memory_space=pl.ANY), pl.BlockSpec((1, D_MODEL), lambda i: (0, 0))],
        out_shape=[jax.ShapeDtypeStruct((nb * s, D_MODEL), F32), jax.ShapeDtypeStruct((N_META, D_MODEL), F32),
                   jax.ShapeDtypeStruct((4, IN_DIM // 4, D_MODEL), F32), jax.ShapeDtypeStruct((1, D_MODEL), F32)],
        scratch_shapes=[pltpu.VMEM((IN_PAD, D_MODEL), F32), pltpu.SemaphoreType.DMA((5,))],
        compiler_params=_cparams("arbitrary"),
    )(x2d, dh2, dpa, dpb, meta, dpam, dccm, pm, pm, w_in_p, norm_g)


W_IN_PIECES = ((0, 384, 752, 0, 64, 0), (384, 64, 752, 384, 448, 1), (448, 304, 752, 512, 512, 1))
W_Q_PIECES = ((0, 96, 256, 0, 0, 0), (96, 96, 256, 96, 96, 1))
W_KV_PIECES = ((0, 128, 128, 0, 0, 0), (128, 128, 128, 512, 512, 1))
W_OUT_PIECES = ((0, 128, 256, 0, 0, 0), (128, 128, 256, 128, 128, 1))


def _gather_weights(split, pieces, out_rows, whole, zero_fills):
    ns, nw, nz = len(split), len(whole), len(zero_fills)
    flat = [(a, pc) for a in range(ns) for pc in pieces[a]]
    nk = len(flat)

    def body(*refs):
        ins, wins, zins = refs[:ns], refs[ns:ns + nw], refs[ns + nw:ns + nw + nz]
        outs, wouts = refs[ns + nw + nz:2 * ns + nw + nz], refs[2 * ns + nw + nz:2 * (ns + nw) + nz]
        send_sems, recv_sems, fwd_send, fwd_recv, loc_sems, w_send, w_recv, w_loc, z_sems = refs[2 * (ns + nw) + nz:]
        x, y, c = lax.axis_index("x"), lax.axis_index("y"), lax.axis_index("c")
        mine = 2 * x + y
        chips = [(1 - x, y), (x, 1 - y), (1 - x, 1 - y)]
        chip_of = [2 * px + py for px, py in chips]

        def src(k):
            a, (s0, nr, _, _, _, _) = flat[k]
            return ins[a].at[s0:s0 + nr]

        def dst(k, q):
            a, (_, nr, per, first, rest, _) = flat[k]
            row = per * q + first + (rest - first) * jnp.minimum(q, 1)
            return outs[a].at[pl.ds(pl.multiple_of(row, 16), nr)]

        def ici(k, j, q):
            px, py = chips[j]
            return pltpu.make_async_remote_copy(
                src_ref=src(k), dst_ref=dst(k, q), send_sem=send_sems.at[k, j], recv_sem=recv_sems.at[k, j],
                device_id=(px, py, c), device_id_type=pl.DeviceIdType.MESH)

        def fwd(k, j):
            ref = dst(k, chip_of[j])
            return pltpu.make_async_remote_copy(
                src_ref=ref, dst_ref=ref, send_sem=fwd_send.at[k, j], recv_sem=fwd_recv.at[k, j],
                device_id=(x, y, 1 - c), device_id_type=pl.DeviceIdType.MESH)

        def wcopy(b, j, q):
            px, py = chips[j]
            return pltpu.make_async_remote_copy(
                src_ref=wins[b], dst_ref=wouts[b].at[q], send_sem=w_send.at[b, j], recv_sem=w_recv.at[b, j],
                device_id=(px, py, c), device_id_type=pl.DeviceIdType.MESH)

        local = [pltpu.make_async_copy(src(k), dst(k, mine), loc_sems.at[k]) for k in range(nk)]
        local += [pltpu.make_async_copy(wins[b], wouts[b].at[mine], w_loc.at[b]) for b in range(nw)]
        for z, (a, _, row0) in enumerate(zero_fills):
            local.append(pltpu.make_async_copy(zins[z], outs[a].at[row0:row0 + zins[z].shape[0]], z_sems.at[z]))
        wsends = [wcopy(b, j, mine) for b in range(nw) for j in range(3)]
        for cp in local + wsends:
            cp.start()

        for half in (0, 1):
            @pl.when(c == half)
            def _(half=half):
                my_k = [k for k in range(nk) if flat[k][1][5] == half]
                other_k = [k for k in range(nk) if flat[k][1][5] != half]
                sends = [ici(k, j, mine) for k in my_k for j in range(3)]
                for cp in sends:
                    cp.start()
                passed = []
                for k in my_k:
                    for j in range(3):
                        ici(k, j, chip_of[j]).wait_recv()
                        cp = fwd(k, j)
                        cp.start()
                        passed.append(cp)
                for k in other_k:
                    for j in range(3):
                        fwd(k, j).wait_recv()
                for cp in sends + passed:
                    cp.wait_send()

        for b in range(nw):
            for j in range(3):
                wcopy(b, j, chip_of[j]).wait_recv()
        for cp in wsends:
            cp.wait_send()
        for cp in local:
            cp.wait()

    hbm = pl.BlockSpec(memory_space=pl.ANY)
    dma = pltpu.SemaphoreType.DMA
    zeros = [z for _, z, _ in zero_fills]
    return pl.pallas_call(
        body, name="gather_weights",
        in_specs=[hbm] * (ns + nw + nz), out_specs=[hbm] * (ns + nw),
        out_shape=([jax.ShapeDtypeStruct((out_rows[a], split[a].shape[1]), split[a].dtype) for a in range(ns)]
                   + [jax.ShapeDtypeStruct((4,) + w.shape, w.dtype) for w in whole]),
        scratch_shapes=[dma((nk, 3)), dma((nk, 3)), dma((nk, 3)), dma((nk, 3)), dma((nk,)),
                        dma((nw, 3)), dma((nw, 3)), dma((nw,)), dma((nz,))],
        compiler_params=pltpu.CompilerParams(vmem_limit_bytes=VMEM_LIMIT),
    )(*split, *whole, *zeros)


def _reduce_grads(parts, small):
    n = len(parts)
    shapes = [a.shape[1:] for a in parts]
    halves = [(sh[0] // 2, sh[1]) for sh in shapes]

    def body(*refs):
        pin, sm_in = refs[:n], refs[n]
        gout, sm_out = refs[n + 1:2 * n + 1], refs[2 * n + 1]
        scr = refs[2 * n + 2:]
        own, sib, wire, rbuf = scr[:n], scr[n:2 * n], scr[2 * n:3 * n], scr[3 * n:4 * n]
        (sbuf, send_sems, recv_sems, loc_sems, pre_send, pre_recv, post_send, post_recv,
         sm_send, sm_recv) = scr[4 * n:]
        x, y, c = lax.axis_index("x"), lax.axis_index("y"), lax.axis_index("c")
        mine = 2 * x + y
        me = 4 * x + 2 * y + c
        sibling = (x, y, 1 - c)
        chips = [(1 - x, y), (x, 1 - y), (1 - x, 1 - y)]

        def rows(a, half):
            r2 = halves[a][0]
            return pl.ds(pl.multiple_of(half * r2, r2), r2)

        def pre(a):
            return pltpu.make_async_remote_copy(
                src_ref=pin[a].at[:, rows(a, 1 - c), :], dst_ref=sib[a], send_sem=pre_send.at[a],
                recv_sem=pre_recv.at[a], device_id=sibling, device_id_type=pl.DeviceIdType.MESH)

        def ici(a, j):
            px, py = chips[j]
            return pltpu.make_async_remote_copy(
                src_ref=wire[a].at[2 * px + py], dst_ref=rbuf[a].at[j], send_sem=send_sems.at[a, j],
                recv_sem=recv_sems.at[a, j], device_id=(px, py, c), device_id_type=pl.DeviceIdType.MESH)

        def post(a, half):
            ref = gout[a].at[rows(a, half), :]
            return pltpu.make_async_remote_copy(
                src_ref=ref, dst_ref=ref, send_sem=post_send.at[a], recv_sem=post_recv.at[a],
                device_id=sibling, device_id_type=pl.DeviceIdType.MESH)

        def small_copy(kk):
            peer = (x ^ (kk >> 2), y ^ ((kk >> 1) & 1), c ^ (kk & 1))
            return pltpu.make_async_remote_copy(
                src_ref=sm_in, dst_ref=sbuf.at[kk], send_sem=sm_send.at[kk - 1], recv_sem=sm_recv.at[kk - 1],
                device_id=peer, device_id_type=pl.DeviceIdType.MESH)

        local = [pltpu.make_async_copy(pin[a].at[:, rows(a, c), :], own[a], loc_sems.at[a]) for a in range(n)]
        pres = [pre(a) for a in range(n)]
        smalls = [small_copy(kk) for kk in range(1, 8)]
        for cp in local + pres + smalls:
            cp.start()
        sbuf[0] = sm_in[...]
        sends = []
        for a in range(n):
            local[a].wait()
            pres[a].wait_recv()
            for blk in range(4):
                tot = own[a][blk] + sib[a][blk]
                own[a][blk] = tot
                wire[a][blk] = tot.astype(BF16)
            for j in range(3):
                cp = ici(a, j)
                cp.start()
                sends.append(cp)
        for cp in smalls:
            cp.wait_recv()
        total = sbuf[me]
        for d in range(1, 8):
            total = total + sbuf[me ^ d]
        sm_out[...] = total
        posts = []
        for a in range(n):
            for j in range(3):
                ici(a, j).wait_recv()
            fin = own[a][mine]
            for j in range(3):
                fin = fin + rbuf[a][j].astype(F32)
            gout[a][rows(a, c), :] = fin
            cp = post(a, c)
            cp.start()
            posts.append(cp)
        for a in range(n):
            post(a, 1 - c).wait_recv()
        for cp in pres + sends + smalls + posts:
            cp.wait_send()

    hbm = pl.BlockSpec(memory_space=pl.ANY)
    vmem = pl.BlockSpec(memory_space=pltpu.VMEM)
    dma = pltpu.SemaphoreType.DMA
    return pl.pallas_call(
        body, name="reduce_grads",
        in_specs=[hbm] * n + [vmem], out_specs=[vmem] * (n + 1),
        out_shape=[jax.ShapeDtypeStruct(sh, F32) for sh in shapes] + [jax.ShapeDtypeStruct(small.shape, F32)],
        scratch_shapes=([pltpu.VMEM((4,) + hs, F32) for hs in halves] + [pltpu.VMEM((4,) + hs, F32) for hs in halves]
                        + [pltpu.VMEM((4,) + hs, BF16) for hs in halves]
                        + [pltpu.VMEM((3,) + hs, BF16) for hs in halves]
                        + [pltpu.VMEM((8,) + small.shape, F32), dma((n, 3)), dma((n, 3)), dma((n,)),
                           dma((n,)), dma((n,)), dma((n,)), dma((n,)), dma((7,)), dma((7,))]),
        compiler_params=pltpu.CompilerParams(vmem_limit_bytes=VMEM_LIMIT),
    )(*parts, small)


def _adamw(w, g, m, v, name):
    shape = w.shape
    w2, g2, m2, v2 = (a.reshape((-1, shape[-1])) for a in (w, g, m, v))

    def body(w_ref, g_ref, m_ref, v_ref, d_ref, nm_ref, nv_ref):
        gv = g_ref[...]
        nm = ADAM_B1 * m_ref[...] + (1.0 - ADAM_B1) * gv
        nv = ADAM_B2 * v_ref[...] + (1.0 - ADAM_B2) * (gv * gv)
        m_hat = nm / (1.0 - ADAM_B1 ** ADAM_STEP)
        v_hat = nv / (1.0 - ADAM_B2 ** ADAM_STEP)
        d_ref[...] = -ADAM_LR * (m_hat / (jnp.sqrt(v_hat) + ADAM_EPS) + ADAM_WD * w_ref[...])
        nm_ref[...] = nm
        nv_ref[...] = nv

    out = pl.pallas_call(
        body, name=name,
        out_shape=[jax.ShapeDtypeStruct(w2.shape, F32)] * 3,
        compiler_params=pltpu.CompilerParams(vmem_limit_bytes=VMEM_LIMIT),
    )(w2, g2, m2, v2)
    return tuple(a.reshape(shape) for a in out)


def kernel(x, meta_tokens, norm_g, w_in, q_norm_g, w_q_up, kv_norm_g, w_kv_up, conv_w, attn_out_g, conv_out_g, w_out, final_norm_g, loss_target, m_meta_tokens, m_norm_g, m_w_in, m_q_norm_g, m_w_q_up, m_kv_norm_g, m_w_kv_up, m_conv_w, m_attn_out_g, m_conv_out_g, m_w_out, m_final_norm_g, v_meta_tokens, v_norm_g, v_w_in, v_q_norm_g, v_w_q_up, v_kv_norm_g, v_w_kv_up, v_conv_w, v_attn_out_g, v_conv_out_g, v_w_out, v_final_norm_g):
    nb, s, _ = x.shape
    tm = min(ROW_TILE, s)
    ta = min(ATTN_TILE, s)
    assert s % tm == 0 and s % ta == 0 and tm % 16 == 0
    r = nb * s

    tr = lambda a: jnp.transpose(a[0])
    w_in_p, wq_p, wkv_p, w_out_f, g_cw, g_meta = _gather_weights(
        [tr(w_in).astype(BF16), tr(w_q_up).astype(BF16), tr(w_kv_up).astype(BF16), w_out[0].astype(BF16)],
        [W_IN_PIECES, W_Q_PIECES, W_KV_PIECES, W_OUT_PIECES], [IN_PAD, HEADS * QK_PAD, 1024, D_MODEL],
        [conv_w[0], meta_tokens],
        [(0, jnp.zeros((64, D_MODEL), BF16), 448)]
        + [(1, jnp.zeros((64, Q_RANK), BF16), QK_PAD * h + NOPE + ROPE) for h in range(HEADS)])
    conv_f = jnp.transpose(g_cw, (1, 0, 2)).reshape(3, CONV_W)
    meta_f = jnp.transpose(g_meta, (1, 0, 2)).reshape(N_META, D_MODEL)

    c_all, sa_all, sb_all = _rope_tables(N_META + s)
    tabs_m = (c_all[:N_META], sa_all[:N_META], sb_all[:N_META])
    tabs = (c_all[N_META:], sa_all[N_META:], sb_all[N_META:])
    gid = np.arange(CONV_W) // CONV_GROUP
    gmat = jnp.asarray(np.where(gid[:, None] == gid[None, :], 1.0 / CONV_GROUP, 0.0), BF16)
    ga, gc = attn_out_g, conv_out_g
    gf = final_norm_g.reshape(1, D_MODEL)

    x2d = x.reshape(r, D_MODEL)
    tgt2d = loss_target.reshape(r, D_MODEL)

    pm, _, km, vm = _fwd_proj(meta_f, tabs_m, norm_g, w_in_p, q_norm_g, wq_p, kv_norm_g, wkv_p,
                              1, N_META, N_META, "fwd_proj_meta")
    p, q, k, v = _fwd_proj(x2d, tabs, norm_g, w_in_p, q_norm_g, wq_p, kv_norm_g, wkv_p, nb, s, tm, "fwd_proj")
    o, lse = _attn_fwd(q, k, v, km, vm, nb, s, ta)
    dh2, dycat, dw_out, dgf, loss_acc = _out_fwd_bwd(x2d, tgt2d, o, p, pm, conv_f, ga, gc, gmat, w_out_f, gf,
                                                     nb, s, tm)
    dpb, do, delta, dccm, dga, dgc, dcw = _gate_bwd(dycat, o, p, pm, conv_f, ga, gc, gmat, nb, s, tm)
    dq, dk, dv, dkm, dvm = _attn_bwd(q, k, v, do, lse, delta, km, vm, nb, s, ta)
    dpa, dpam, p_q, p_kv, dgq, dgkv = _up_bwd(dq, dk, dv, dkm, dvm, p, pm, tabs, tabs_m, wq_p, wkv_p,
                                              q_norm_g, kv_norm_g, nb, s, tm)
    gx, gmeta, p_in, dng = _in_bwd(x2d, dh2, dpa, dpb, meta_f, dpam, dccm, pm, w_in_p, norm_g, nb, s, tm)

    p_out = dw_out.reshape(4, D_MODEL // 4, D_MODEL)
    flat = jnp.concatenate([dng.reshape(-1), dgq.reshape(-1), dgkv.reshape(-1), dga.reshape(-1), dgc.reshape(-1),
                            dgf.reshape(-1), dcw[:3].reshape(-1), gmeta.reshape(-1), loss_acc[0, 0:1]])
    n_small = flat.shape[0]
    rows_small = -(-n_small // 1024) * 8
    small = jnp.pad(flat, (0, rows_small * 128 - n_small)).reshape(rows_small, 128)
    g_w_in_t, g_w_q_t, g_w_kv_t, g_w_out, small_sum = _reduce_grads([p_in, p_q, p_kv, p_out], small)
    ssum = small_sum.reshape(-1)

    def take(off, n):
        return ssum[off:off + n], off + n

    off = 0
    g_norm, off = take(off, D_MODEL)
    g_qn, off = take(off, Q_RANK)
    g_kvn, off = take(off, KV_RANK)
    g_ga, off = take(off, CONV_W)
    g_gc, off = take(off, CONV_W)
    g_gf, off = take(off, D_MODEL)
    g_cw_all, off = take(off, 3 * CONV_W)
    g_meta_all, off = take(off, N_META * D_MODEL)
    loss = ssum[off]
    chip = 2 * lax.axis_index("x") + lax.axis_index("y")
    g_conv = lax.dynamic_slice(g_cw_all.reshape(3, CONV_W), (0, chip * 128), (3, 128))
    g_mt = lax.dynamic_slice(g_meta_all.reshape(N_META, D_MODEL), (0, chip * 256), (N_META, 256))

    grads = {
        "meta_tokens": g_mt, "norm_g": g_norm.reshape(1, -1), "w_in": g_w_in_t, "q_norm_g": g_qn.reshape(1, -1),
        "w_q_up": g_w_q_t, "kv_norm_g": g_kvn.reshape(1, -1), "w_kv_up": jnp.transpose(g_w_kv_t)[None],
        "conv_w": g_conv[None], "attn_out_g": g_ga.reshape(1, -1), "conv_out_g": g_gc.reshape(1, -1),
        "w_out": g_w_out[None], "final_norm_g": g_gf,
    }
    transposed = ("w_in", "w_q_up")
    weights = {
        "meta_tokens": (meta_tokens, m_meta_tokens, v_meta_tokens), "norm_g": (norm_g, m_norm_g, v_norm_g),
        "w_in": (w_in, m_w_in, v_w_in), "q_norm_g": (q_norm_g, m_q_norm_g, v_q_norm_g),
        "w_q_up": (w_q_up, m_w_q_up, v_w_q_up), "kv_norm_g": (kv_norm_g, m_kv_norm_g, v_kv_norm_g),
        "w_kv_up": (w_kv_up, m_w_kv_up, v_w_kv_up), "conv_w": (conv_w, m_conv_w, v_conv_w),
        "attn_out_g": (attn_out_g, m_attn_out_g, v_attn_out_g), "conv_out_g": (conv_out_g, m_conv_out_g, v_conv_out_g),
        "w_out": (w_out, m_w_out, v_w_out), "final_norm_g": (final_norm_g, m_final_norm_g, v_final_norm_g),
    }
    names = list(weights)
    deltas, new_m, new_v = [], [], []
    for nme in names:
        w_, m_, v_ = weights[nme]
        if nme in transposed:
            res = _adamw(tr(w_), grads[nme], tr(m_), tr(v_), "adamw_" + nme)
            g_, d_, nm_, nv_ = (jnp.transpose(a)[None] for a in (grads[nme],) + res)
        else:
            g_ = grads[nme].reshape(w_.shape)
            d_, nm_, nv_ = _adamw(w_, g_, m_, v_, "adamw_" + nme)
        grads[nme] = g_
        deltas.append(d_)
        new_m.append(nm_)
        new_v.append(nv_)

    grad_x = gx.reshape(nb, s, D_MODEL)
    return (loss, grad_x, *[grads[nme] for nme in names], *deltas, *new_m, *new_v)
```

```python
import functools

import jax
import jax.numpy as jnp
import numpy as np
from jax import lax
from jax.experimental import pallas as pl
from jax.experimental.pallas import tpu as pltpu

F32 = jnp.float32
BF16 = jnp.bfloat16

D_MODEL = 1024
N_META = 16
HEADS = 4
NOPE = 128
ROPE = 64
VDIM = 128
QK_PAD = 256
Q_RANK = 256
KV_RANK = 128
CONV_W = 512
CONV_GROUP = 64
ROPE_THETA = 10000.0
EPS = 1e-6
ATTN_SCALE = (NOPE + ROPE) ** -0.5
IN_DIM = 3008
IN_PAD = 3072
BLK_ZA, BLK_CB, BLK_CC, BLK_CH, BLK_ZC = 1, 2, 3, 4, 5
NEG_INF = -1e30

ADAM_LR = 0.001
ADAM_B1 = 0.9
ADAM_B2 = 0.999
ADAM_EPS = 1e-08
ADAM_WD = 0.01
ADAM_STEP = 10

ROW_TILE = 512
ATTN_TILE = 256
VMEM_LIMIT = 56 * 1024 * 1024

NT = (((1,), (1,)), ((), ()))
TN = (((0,), (0,)), ((), ()))


def _cparams(*sem):
    return pltpu.CompilerParams(dimension_semantics=sem, vmem_limit_bytes=VMEM_LIMIT)


def _dot(a, b):
    return jnp.dot(a, b, preferred_element_type=F32)


def _dot_nt(a, b):
    return lax.dot_general(a, b, NT, preferred_element_type=F32)


def _dot_tn(a, b):
    return lax.dot_general(a, b, TN, preferred_element_type=F32)


def _rms(x, g):
    r = lax.rsqrt(jnp.mean(x * x, axis=-1, keepdims=True) + EPS)
    return x * r * g, r


def _rms_bwd(dy, x, r, g):
    xh = x * r
    dyg = dy * g
    dx = r * (dyg - xh * jnp.mean(dyg * xh, axis=-1, keepdims=True))
    return dx, dy * xh


def _sigmoid(z):
    return 1.0 / (1.0 + jnp.exp(-z))


def _rope(b, c, sa, sb):
    return b * c + pltpu.roll(b, 96, 1) * sa + pltpu.roll(b, 32, 1) * sb


def _rope_bwd(d, c, sa, sb):
    return d * c + pltpu.roll(d * sa, 32, 1) + pltpu.roll(d * sb, 96, 1)


def _group_mean(x, gmat):
    hi = x.astype(BF16)
    lo = (x - hi.astype(F32)).astype(BF16)
    return _dot(hi, gmat) + _dot(lo, gmat)


def _row_of(col, rows):
    return jnp.transpose(jnp.broadcast_to(col, (rows, 128)))[0:1, :]


def _rope_tables(n_pos):
    half = ROPE // 2
    inv_freq = (np.float32(1.0) / (np.float32(ROPE_THETA) ** (np.arange(half, dtype=np.float32) / np.float32(half))))
    ang = np.arange(n_pos, dtype=np.float32)[:, None] * inv_freq.astype(np.float32)[None, :]
    cos, sin = np.cos(ang).astype(np.float32), np.sin(ang).astype(np.float32)
    z = np.zeros((n_pos, half), np.float32)
    c = np.concatenate([cos, cos, z, z], axis=1)
    sa = np.concatenate([-sin, z, z, z], axis=1)
    sb = np.concatenate([z, sin, z, z], axis=1)
    return jnp.asarray(c), jnp.asarray(sa), jnp.asarray(sb)


W_IN_PIECES = ((0, 384, 752, 0, 64, 0), (384, 64, 752, 384, 448, 1), (448, 304, 752, 512, 512, 1))
W_Q_PIECES = ((0, 96, 256, 0, 0, 0), (96, 96, 256, 96, 96, 1))
W_KV_PIECES = ((0, 128, 128, 0, 0, 0), (128, 128, 128, 512, 512, 1))
W_OUT_PIECES = ((0, 128, 256, 0, 0, 0), (128, 128, 256, 128, 128, 1))


class _StagedGather:
    def __init__(self, pieces):
        self.pieces = pieces

    def scratch(self):
        nk, dma = len(self.pieces), pltpu.SemaphoreType.DMA
        return [dma((nk, 3)), dma((nk, 3)), dma((nk, 3)), dma((nk, 3)), dma((nk,))]

    def run(self, stage, src_ref, out_ref, scr):
        send_sems, recv_sems, fwd_send, fwd_recv, loc_sems = scr
        pieces = self.pieces
        nk = len(pieces)
        x, y, c = lax.axis_index("x"), lax.axis_index("y"), lax.axis_index("c")
        mine = 2 * x + y
        chips = [(1 - x, y), (x, 1 - y), (1 - x, 1 - y)]
        chip_of = [2 * px + py for px, py in chips]
        mesh = pl.DeviceIdType.MESH

        def src(k):
            s0, nr = pieces[k][0], pieces[k][1]
            return src_ref.at[s0:s0 + nr]

        def dst(k, q):
            _, nr, per, first, rest, _ = pieces[k]
            row = per * q + first + (rest - first) * jnp.minimum(q, 1)
            return out_ref.at[pl.ds(pl.multiple_of(row, 16), nr)]

        def ici(k, j, q):
            px, py = chips[j]
            return pltpu.make_async_remote_copy(
                src_ref=src(k), dst_ref=dst(k, q), send_sem=send_sems.at[k, j], recv_sem=recv_sems.at[k, j],
                device_id=(px, py, c), device_id_type=mesh)

        def fwd(k, j):
            ref = dst(k, chip_of[j])
            return pltpu.make_async_remote_copy(
                src_ref=ref, dst_ref=ref, send_sem=fwd_send.at[k, j], recv_sem=fwd_recv.at[k, j],
                device_id=(x, y, 1 - c), device_id_type=mesh)

        local = [pltpu.make_async_copy(src(k), dst(k, mine), loc_sems.at[k]) for k in range(nk)]
        if stage == 0:
            for cp in local:
                cp.start()
        if stage == 2:
            for cp in local:
                cp.wait()
        for half in (0, 1):
            @pl.when(c == half)
            def _(half=half):
                my_k = [k for k in range(nk) if pieces[k][5] == half]
                other_k = [k for k in range(nk) if pieces[k][5] != half]
                for k in my_k:
                    for j in range(3):
                        if stage == 0:
                            ici(k, j, mine).start()
                        elif stage == 1:
                            ici(k, j, chip_of[j]).wait_recv()
                            fwd(k, j).start()
                        else:
                            ici(k, j, mine).wait_send()
                            fwd(k, j).wait_send()
                if stage == 2:
                    for k in other_k:
                        for j in range(3):
                            fwd(k, j).wait_recv()


def _fwd_proj(x2d, tabs, norm_g, w_in_p, q_norm_g, wq_p, kv_norm_g, wkv_p, nb, s, tm, name, w_out_shard=None):
    nt = s // tm
    n_steps = nb * nt
    c_t, sa_t, sb_t = tabs
    hosted = w_out_shard is not None
    gat = _StagedGather(W_OUT_PIECES)
    assert not hosted or n_steps >= 3

    def body(x_ref, c_ref, sa_ref, sb_ref, g_ref, w_ref, gq_ref, wq_ref, gkv_ref, wkv_ref, *rest):
        if hosted:
            wos_ref, p_ref, q_ref, k_ref, v_ref, wo_ref = rest[:6]
            for stage, at in enumerate((0, n_steps - 2, n_steps - 1)):
                @pl.when(pl.program_id(0) == at)
                def _(stage=stage):
                    gat.run(stage, wos_ref, wo_ref, rest[6:])
        else:
            p_ref, q_ref, k_ref, v_ref = rest
        u, _ = _rms(x_ref[...], g_ref[...])
        p = _dot_nt(u.astype(BF16), w_ref[...])
        p_ref[...] = p
        c, sa, sb = c_ref[...], sa_ref[...], sb_ref[...]
        qn, _ = _rms(p[:, 0:Q_RANK], gq_ref[...])
        q = _dot_nt(qn.astype(BF16), wq_ref[...])
        kvn, _ = _rms(p[:, Q_RANK:Q_RANK + KV_RANK], gkv_ref[...])
        kv = _dot_nt(kvn.astype(BF16), wkv_ref[...])
        kpe = _rope(p[:, 384:512], c, sa, sb)
        for h in range(HEADS):
            pe = _rope(q[:, QK_PAD * h + NOPE:QK_PAD * (h + 1)], c, sa, sb)
            qh = jnp.concatenate([q[:, QK_PAD * h:QK_PAD * h + NOPE], pe], axis=1)
            q_ref[0, h] = (qh * ATTN_SCALE).astype(BF16)
            k_ref[0, h] = jnp.concatenate([kv[:, NOPE * h:NOPE * (h + 1)], kpe], axis=1).astype(BF16)
            v_ref[0, h] = kv[:, 512 + VDIM * h:512 + VDIM * (h + 1)].astype(BF16)

    full = lambda a: pl.BlockSpec(a.shape, lambda i: (0,) * a.ndim)
    tab = pl.BlockSpec((tm, 128), lambda i: (i % nt, 0))
    hbm = pl.BlockSpec(memory_space=pl.ANY)
    extra_in = [w_out_shard] if hosted else []
    return pl.pallas_call(
        body, name=name, grid=(n_steps,),
        in_specs=[pl.BlockSpec((tm, D_MODEL), lambda i: (i, 0)), tab, tab, tab,
                  full(norm_g), full(w_in_p), full(q_norm_g), full(wq_p), full(kv_norm_g), full(wkv_p)]
        + [hbm] * hosted,
        out_specs=[pl.BlockSpec((tm, IN_PAD), lambda i: (i, 0)),
                   pl.BlockSpec((1, HEADS, tm, QK_PAD), lambda i: (i // nt, 0, i % nt, 0)),
                   pl.BlockSpec((1, HEADS, tm, QK_PAD), lambda i: (i // nt, 0, i % nt, 0)),
                   pl.BlockSpec((1, HEADS, tm, VDIM), lambda i: (i // nt, 0, i % nt, 0))] + [hbm] * hosted,
        out_shape=[jax.ShapeDtypeStruct((nb * s, IN_PAD), F32),
                   jax.ShapeDtypeStruct((nb, HEADS, s, QK_PAD), BF16),
                   jax.ShapeDtypeStruct((nb, HEADS, s, QK_PAD), BF16),
                   jax.ShapeDtypeStruct((nb, HEADS, s, VDIM), BF16)]
        + [jax.ShapeDtypeStruct((D_MODEL, D_MODEL), BF16)] * hosted,
        scratch_shapes=gat.scratch() if hosted else [],
        compiler_params=_cparams("arbitrary" if hosted else "parallel"),
    )(x2d, c_t, sa_t, sb_t, norm_g, w_in_p, q_norm_g, wq_p, kv_norm_g, wkv_p, *extra_in)


def _attn_fwd(q, k, v, km, vm, nb, s, tq):
    nq = s // tq

    def body(q_ref, k_ref, v_ref, km_ref, vm_ref, o_ref, lse_ref, s_scr, p_scr):
        row = lax.broadcasted_iota(jnp.int32, (tq, tq), 0)
        col = lax.broadcasted_iota(jnp.int32, (tq, tq), 1)
        for i in range(nq):
            slot = i % 2
            qi = q_ref[0, 0, i * tq:(i + 1) * tq, :]
            sm = _dot_nt(qi, km_ref[0, 0])
            m128 = None
            for j in range(i + 1):
                sc = _dot_nt(qi, k_ref[0, 0, j * tq:(j + 1) * tq, :])
                if j == i:
                    sc = jnp.where(col <= row, sc, NEG_INF)
                s_scr[slot, :, j * tq:(j + 1) * tq] = sc
                mx = sc[:, 0:128]
                for c0 in range(128, tq, 128):
                    mx = jnp.maximum(mx, sc[:, c0:c0 + 128])
                m128 = mx if m128 is None else jnp.maximum(m128, mx)
            m = jnp.maximum(jnp.max(m128, axis=1, keepdims=True), jnp.max(sm, axis=1, keepdims=True))
            pm = jnp.exp(sm - m)
            l128 = None
            for j in range(i + 1):
                p = jnp.exp(s_scr[slot, :, j * tq:(j + 1) * tq] - m)
                p_scr[slot, :, j * tq:(j + 1) * tq] = p.astype(BF16)
                ps = p[:, 0:128]
                for c0 in range(128, tq, 128):
                    ps = ps + p[:, c0:c0 + 128]
                l128 = ps if l128 is None else l128 + ps
            l = jnp.sum(l128, axis=1, keepdims=True) + jnp.sum(pm, axis=1, keepdims=True)
            n = (i + 1) * tq
            acc = _dot(p_scr[slot, :, 0:n], v_ref[0, 0, 0:n, :]) + _dot(pm.astype(BF16), vm_ref[0, 0])
            o_ref[0, 0, i * tq:(i + 1) * tq, :] = acc / l
            lse_ref[0, 0, :, i * tq:(i + 1) * tq] = _row_of(m + jnp.log(l), tq)

    hblk = lambda w: pl.BlockSpec((1, 1, s, w), lambda b, h: (b, h, 0, 0))
    mblk = lambda w: pl.BlockSpec((1, 1, N_META, w), lambda b, h: (0, h, 0, 0))
    return pl.pallas_call(
        body, name="attn_fwd", grid=(nb, HEADS),
        in_specs=[hblk(QK_PAD), hblk(QK_PAD), hblk(VDIM), mblk(QK_PAD), mblk(VDIM)],
        out_specs=[hblk(VDIM), pl.BlockSpec((1, 1, 1, s), lambda b, h: (b, h, 0, 0))],
        out_shape=[jax.ShapeDtypeStruct((nb, HEADS, s, VDIM), F32),
                   jax.ShapeDtypeStruct((nb, HEADS, 1, s), F32)],
        scratch_shapes=[pltpu.VMEM((2, tq, s), F32), pltpu.VMEM((2, tq, s), BF16)],
        compiler_params=_cparams("parallel", "parallel"),
    )(q, k, v, km, vm)


def _shift_rows(a, prev, n_rows):
    rid = lax.broadcasted_iota(jnp.int32, a.shape, 0)
    a1 = jnp.where(rid == 0, prev[7:8, :], pltpu.roll(a, 1, 0))
    a2 = jnp.where(rid == 0, prev[6:7, :], jnp.where(rid == 1, prev[7:8, :], pltpu.roll(a, 2, 0)))
    return a1, a2


def _attn_gate(o, za, ga_h):
    on, r = _rms(o, ga_h)
    return on * (za * _sigmoid(za)), on, r


def _out_fwd_bwd(x2d, tgt2d, o, p, pm, conv_w, ga, gc, gmat, w_out, gf, nb, s, tm):
    nt = s // tm
    r = nb * s
    prev_idx = lambda i: jnp.maximum(i * (tm // 8) - 1, 0)

    def body(x_ref, t_ref, o_ref, za_ref, cb_ref, cc_ref, ch_ref, zc_ref, ccp_ref, chp_ref, mc_ref, mh_ref,
             cw_ref, ga_ref, gc_ref, gm_ref, w_ref, gf_ref,
             dh_ref, dy_ref, dw_ref, dgf_ref, loss_ref):
        i = pl.program_id(0)

        @pl.when(i == 0)
        def _():
            dw_ref[...] = jnp.zeros_like(dw_ref)
            dgf_ref[...] = jnp.zeros_like(dgf_ref)
            loss_ref[...] = jnp.zeros_like(loss_ref)

        ya = []
        for h in range(HEADS):
            y, _, _ = _attn_gate(o_ref[0, h], za_ref[:, VDIM * h:VDIM * (h + 1)],
                                 ga_ref[:, VDIM * h:VDIM * (h + 1)])
            ya.append(y)
        cc = cc_ref[...] * ch_ref[...]
        prev = jnp.where(i % nt == 0, mc_ref[8:16, :] * mh_ref[8:16, :], ccp_ref[...] * chp_ref[...])
        cc1, cc2 = _shift_rows(cc, prev, tm)
        yc = cb_ref[...] * (cw_ref[0:1, :] * cc2 + cw_ref[1:2, :] * cc1 + cw_ref[2:3, :] * cc)
        rg = lax.rsqrt(_group_mean(yc * yc, gm_ref[...]) + EPS)
        zc = zc_ref[...]
        yconv = yc * rg * gc_ref[...] * (zc * _sigmoid(zc))
        ycat = jnp.concatenate(ya + [yconv], axis=1).astype(BF16)
        h2 = x_ref[...] + _dot(ycat, w_ref[...])
        gfv = gf_ref[...]
        y, r2 = _rms(h2, gfv)
        e = y - t_ref[...]
        loss_ref[...] += 0.5 * jnp.sum(e * e) / D_MODEL
        dyv = e * (1.0 / D_MODEL)
        dh2, dgf = _rms_bwd(dyv, h2, r2, gfv)
        dgf_ref[...] += jnp.sum(dgf, axis=0, keepdims=True)
        dh_ref[...] = dh2
        dhb = dh2.astype(BF16)
        dy_ref[...] = _dot_nt(dhb, w_ref[...])
        dw_ref[...] += _dot_tn(ycat, dhb)

    row = lambda w, j: pl.BlockSpec((tm, w), lambda i: (i, j))
    pblk = lambda j: pl.BlockSpec((tm, 512), lambda i: (i, j))
    pprev = lambda j: pl.BlockSpec((8, 512), lambda i: (prev_idx(i), j))
    mblk = lambda j: pl.BlockSpec((N_META, 512), lambda i: (0, j))
    full = lambda a: pl.BlockSpec(a.shape, lambda i: (0,) * a.ndim)
    return pl.pallas_call(
        body, name="out_fwd_bwd", grid=(nb * nt,),
        in_specs=[row(D_MODEL, 0), row(D_MODEL, 0),
                  pl.BlockSpec((1, HEADS, tm, VDIM), lambda i: (i // nt, 0, i % nt, 0)),
                  pblk(BLK_ZA), pblk(BLK_CB), pblk(BLK_CC), pblk(BLK_CH), pblk(BLK_ZC),
                  pprev(BLK_CC), pprev(BLK_CH), mblk(BLK_CC), mblk(BLK_CH),
                  full(conv_w), full(ga), full(gc), full(gmat), full(w_out), full(gf)],
        out_specs=[row(D_MODEL, 0), row(D_MODEL, 0),
                   pl.BlockSpec((D_MODEL, D_MODEL), lambda i: (0, 0)),
                   pl.BlockSpec((1, D_MODEL), lambda i: (0, 0)),
                   pl.BlockSpec((1, 128), lambda i: (0, 0))],
        out_shape=[jax.ShapeDtypeStruct((r, D_MODEL), F32), jax.ShapeDtypeStruct((r, D_MODEL), F32),
                   jax.ShapeDtypeStruct((D_MODEL, D_MODEL), F32), jax.ShapeDtypeStruct((1, D_MODEL), F32),
                   jax.ShapeDtypeStruct((1, 128), F32)],
        compiler_params=_cparams("arbitrary"),
    )(x2d, tgt2d, o, p, p, p, p, p, p, p, pm, pm, conv_w, ga, gc, gmat, w_out, gf)


def _gate_bwd(dycat, o, p, pm, conv_w, ga, gc, gmat, nb, s, tm):
    nt = s // tm
    r = nb * s
    ext = tm + 8
    prev_idx = lambda i: jnp.maximum(i * (tm // 8) - 1, 0)
    next_idx = lambda i: jnp.minimum((i + 1) * (tm // 8), r // 8 - 1)

    def body(dya_ref, dyc_ref, dycn_ref, o_ref, za_ref, cb_ref, cbn_ref, cc_ref, ccp_ref, ccn_ref,
             ch_ref, chp_ref, chn_ref, zc_ref, zcn_ref, mc_ref, mh_ref, cw_ref, ga_ref, gc_ref, gm_ref,
             dpb_ref, do_ref, dl_ref, dccm_ref, dga_ref, dgc_ref, dcw_ref):
        i = pl.program_id(0)

        @pl.when(i == 0)
        def _():
            dga_ref[...] = jnp.zeros_like(dga_ref)
            dgc_ref[...] = jnp.zeros_like(dgc_ref)
            dcw_ref[...] = jnp.zeros_like(dcw_ref)

        dga = []
        for h in range(HEADS):
            hs = slice(VDIM * h, VDIM * (h + 1))
            oh, za, gah, dya = o_ref[0, h], za_ref[:, hs], ga_ref[:, hs], dya_ref[:, hs]
            sg = _sigmoid(za)
            on, ro = _rms(oh, gah)
            don = dya * (za * sg)
            dpb_ref[:, hs] = (dya * on * (sg * (1.0 + za * (1.0 - sg)))).astype(BF16)
            do, dg = _rms_bwd(don, oh, ro, gah)
            dga.append(jnp.sum(dg, axis=0, keepdims=True))
            dob = do.astype(BF16)
            do_ref[0, h] = dob
            dl_ref[0, h] = _row_of(jnp.sum(dob.astype(F32) * oh, axis=1, keepdims=True), tm)
        dga_ref[...] += jnp.concatenate(dga, axis=1)

        cat = lambda a, b: jnp.concatenate([a[...], b[...]], axis=0)
        cch = cat(cc_ref, ccn_ref)
        chh = cat(ch_ref, chn_ref)
        cb = cat(cb_ref, cbn_ref)
        zc = cat(zc_ref, zcn_ref)
        dy = cat(dyc_ref, dycn_ref)
        first = i % nt == 0
        last = i % nt == nt - 1
        cc = cch * chh
        prev = jnp.where(first, mc_ref[8:16, :] * mh_ref[8:16, :], ccp_ref[...] * chp_ref[...])
        cc1, cc2 = _shift_rows(cc, prev, ext)
        w0, w1, w2 = cw_ref[0:1, :], cw_ref[1:2, :], cw_ref[2:3, :]
        dw = w0 * cc2 + w1 * cc1 + w2 * cc
        yc = cb * dw
        rg = lax.rsqrt(_group_mean(yc * yc, gm_ref[...]) + EPS)
        ych = yc * rg
        gcv = gc_ref[...]
        sg = _sigmoid(zc)
        dycn = dy * (zc * sg)
        dzc = dy * (ych * gcv) * (sg * (1.0 + zc * (1.0 - sg)))
        dgc_ref[...] += jnp.sum((dycn * ych)[:tm], axis=0, keepdims=True)
        dycg = dycn * gcv
        dyc = rg * (dycg - ych * _group_mean(dycg * ych, gm_ref[...]))
        rid = lax.broadcasted_iota(jnp.int32, (ext, CONV_W), 0)
        ddw = jnp.where(jnp.logical_and(last, rid >= tm), 0.0, dyc * cb)
        dcb = dyc * dw
        dcc = w2 * ddw + w1 * pltpu.roll(ddw, ext - 1, 0) + w0 * pltpu.roll(ddw, ext - 2, 0)
        dpb_ref[:, 512:1024] = dcb[:tm].astype(BF16)
        dpb_ref[:, 1024:1536] = (dcc * chh)[:tm].astype(BF16)
        dpb_ref[:, 1536:2048] = (dcc * cch)[:tm].astype(BF16)
        dpb_ref[:, 2048:2560] = dzc[:tm].astype(BF16)
        rs = lambda a: jnp.sum(a[:tm], axis=0, keepdims=True)
        dcw_ref[0:1, :] += rs(ddw * cc2)
        dcw_ref[1:2, :] += rs(ddw * cc1)
        dcw_ref[2:3, :] += rs(ddw * cc)

        @pl.when(first)
        def _():
            d0, d1 = ddw[0:1, :], ddw[1:2, :]
            r8 = lax.broadcasted_iota(jnp.int32, (8, CONV_W), 0)
            dccm_ref[0] = jnp.where(r8 == 7, w1 * d0 + w0 * d1, jnp.where(r8 == 6, w0 * d0, 0.0))

    row = lambda j: pl.BlockSpec((tm, 512), lambda i: (i, j))
    prv = lambda j: pl.BlockSpec((8, 512), lambda i: (prev_idx(i), j))
    nxt = lambda j: pl.BlockSpec((8, 512), lambda i: (next_idx(i), j))
    mblk = lambda j: pl.BlockSpec((N_META, 512), lambda i: (0, j))
    full = lambda a: pl.BlockSpec(a.shape, lambda i: (0,) * a.ndim)
    hb = lambda w: pl.BlockSpec((1, HEADS, tm, w), lambda i: (i // nt, 0, i % nt, 0))
    acc = lambda rr: pl.BlockSpec((rr, 512), lambda i: (0, 0))
    return pl.pallas_call(
        body, name="gate_bwd", grid=(nb * nt,),
        in_specs=[row(0), row(1), nxt(1), hb(VDIM),
                  row(BLK_ZA), row(BLK_CB), nxt(BLK_CB), row(BLK_CC), prv(BLK_CC), nxt(BLK_CC),
                  row(BLK_CH), prv(BLK_CH), nxt(BLK_CH), row(BLK_ZC), nxt(BLK_ZC),
                  mblk(BLK_CC), mblk(BLK_CH), full(conv_w), full(ga), full(gc), full(gmat)],
        out_specs=[pl.BlockSpec((tm, 2560), lambda i: (i, 0)), hb(VDIM),
                   pl.BlockSpec((1, HEADS, 1, tm), lambda i: (i // nt, 0, 0, i % nt)),
                   pl.BlockSpec((1, 8, 512), lambda i: (i // nt, 0, 0)),
                   acc(1), acc(1), acc(8)],
        out_shape=[jax.ShapeDtypeStruct((r, 2560), BF16), jax.ShapeDtypeStruct((nb, HEADS, s, VDIM), BF16),
                   jax.ShapeDtypeStruct((nb, HEADS, 1, s), F32), jax.ShapeDtypeStruct((nb, 8, 512), F32),
                   jax.ShapeDtypeStruct((1, 512), F32), jax.ShapeDtypeStruct((1, 512), F32),
                   jax.ShapeDtypeStruct((8, 512), F32)],
        compiler_params=_cparams("arbitrary"),
    )(dycat, dycat, dycat, o, p, p, p, p, p, p, p, p, p, p, p, pm, pm, conv_w, ga, gc, gmat)


class _StagedReduce:
    LOC, PRE_S, PRE_R, ICI_S, ICI_R, POST_S, POST_R, OUT, N_SEM = 0, 1, 2, 3, 6, 9, 10, 11, 12

    def __init__(self, shard_shape):
        self.half = (shard_shape[0] // 2, shard_shape[1])

    def scratch(self):
        h = self.half
        return [pltpu.VMEM((4,) + h, F32), pltpu.VMEM((4,) + h, F32), pltpu.VMEM((4,) + h, BF16),
                pltpu.VMEM((3,) + h, BF16), pltpu.VMEM(h, F32), pltpu.SemaphoreType.DMA((self.N_SEM,))]

    def run(self, stage, pin, gout, scr):
        own, sib, wire, rbuf, fin, sems = scr
        r2 = self.half[0]
        x, y, c = lax.axis_index("x"), lax.axis_index("y"), lax.axis_index("c")
        mine = 2 * x + y
        sibling = (x, y, 1 - c)
        chips = [(1 - x, y), (x, 1 - y), (1 - x, 1 - y)]
        rows = lambda half: pl.ds(pl.multiple_of(half * r2, r2), r2)
        mesh = pl.DeviceIdType.MESH

        loc = pltpu.make_async_copy(pin.at[:, rows(c), :], own, sems.at[self.LOC])
        pre = pltpu.make_async_remote_copy(
            src_ref=pin.at[:, rows(1 - c), :], dst_ref=sib, send_sem=sems.at[self.PRE_S],
            recv_sem=sems.at[self.PRE_R], device_id=sibling, device_id_type=mesh)

        def ici(j):
            px, py = chips[j]
            return pltpu.make_async_remote_copy(
                src_ref=wire.at[2 * px + py], dst_ref=rbuf.at[j], send_sem=sems.at[self.ICI_S + j],
                recv_sem=sems.at[self.ICI_R + j], device_id=(px, py, c), device_id_type=mesh)

        def post(half):
            return pltpu.make_async_remote_copy(
                src_ref=fin, dst_ref=gout.at[rows(half), :], send_sem=sems.at[self.POST_S],
                recv_sem=sems.at[self.POST_R], device_id=sibling, device_id_type=mesh)

        keep = pltpu.make_async_copy(fin, gout.at[rows(c), :], sems.at[self.OUT])
        if stage == 0:
            loc.start()
            pre.start()
        elif stage == 1:
            loc.wait()
            pre.wait_recv()
            for blk in range(4):
                tot = own[blk] + sib[blk]
                own[blk] = tot
                wire[blk] = tot.astype(BF16)
            for j in range(3):
                ici(j).start()
        elif stage == 2:
            for j in range(3):
                ici(j).wait_recv()
            tot = own[mine]
            for j in range(3):
                tot = tot + rbuf[j].astype(F32)
            fin[...] = tot
            post(c).start()
            keep.start()
        else:
            post(1 - c).wait_recv()
            pre.wait_send()
            for j in range(3):
                ici(j).wait_send()
            post(c).wait_send()
            keep.wait()


def _attn_bwd(q, k, v, do, lse, delta, km, vm, p_out, nb, s, t):
    n = s // t
    red = _StagedReduce(p_out.shape[1:])
    n_steps = HEADS * nb
    assert n_steps >= 4

    def body(q_ref, k_ref, v_ref, do_ref, lse_ref, dl_ref, km_ref, vm_ref, pout_ref,
             dq_ref, dk_ref, dv_ref, dkm_ref, dvm_ref, gout_ref, p_scr, ds_scr, *red_scr):
        b = pl.program_id(1)
        step = pl.program_id(0) * nb + b
        for stage, at in enumerate((0, 1, n_steps - 2, n_steps - 1)):
            @pl.when(step == at)
            def _(stage=stage):
                red.run(stage, pout_ref, gout_ref, red_scr)

        @pl.when(b == 0)
        def _():
            dkm_ref[...] = jnp.zeros_like(dkm_ref)
            dvm_ref[...] = jnp.zeros_like(dvm_ref)

        kr = lax.broadcasted_iota(jnp.int32, (t, t), 0)
        qc = lax.broadcasted_iota(jnp.int32, (t, t), 1)
        km_v, vm_v = km_ref[0, 0], vm_ref[0, 0]
        ptm = jnp.exp(_dot_nt(km_v, q_ref[0, 0]) - lse_ref[0, 0])
        dstm = (ptm * (_dot_nt(vm_v, do_ref[0, 0]) - dl_ref[0, 0])).astype(BF16)
        dkm_ref[0] += _dot(dstm, q_ref[0, 0])
        dvm_ref[0] += _dot(ptm.astype(BF16), do_ref[0, 0])
        dq_ref[0, 0] = _dot_tn(dstm, km_v)
        for j in range(n):
            slot = j % 2
            kj = k_ref[0, 0, j * t:(j + 1) * t, :]
            vj = v_ref[0, 0, j * t:(j + 1) * t, :]
            for i in range(j, n):
                cs = slice(i * t, (i + 1) * t)
                qi = q_ref[0, 0, cs, :]
                doi = do_ref[0, 0, cs, :]
                st = _dot_nt(kj, qi)
                if i == j:
                    st = jnp.where(kr <= qc, st, NEG_INF)
                pt = jnp.exp(st - lse_ref[0, 0, :, cs])
                dst = (pt * (_dot_nt(vj, doi) - dl_ref[0, 0, :, cs])).astype(BF16)
                p_scr[slot, :, cs] = pt.astype(BF16)
                ds_scr[slot, :, cs] = dst
                dq_ref[0, 0, cs, :] += _dot_tn(dst, kj)
            dv_ref[0, 0, j * t:(j + 1) * t, :] = _dot(p_scr[slot, :, j * t:s], do_ref[0, 0, j * t:s, :])
            dk_ref[0, 0, j * t:(j + 1) * t, :] = _dot(ds_scr[slot, :, j * t:s], q_ref[0, 0, j * t:s, :])

    big = lambda w: pl.BlockSpec((1, 1, s, w), lambda h, b: (b, h, 0, 0))
    rowv = pl.BlockSpec((1, 1, 1, s), lambda h, b: (b, h, 0, 0))
    mk = lambda w: pl.BlockSpec((1, 1, N_META, w), lambda h, b: (0, h, 0, 0))
    mo = lambda w: pl.BlockSpec((1, N_META, w), lambda h, b: (h, 0, 0))
    return pl.pallas_call(
        body, name="attn_bwd", grid=(HEADS, nb),
        in_specs=[big(QK_PAD), big(QK_PAD), big(VDIM), big(VDIM), rowv, rowv, mk(QK_PAD), mk(VDIM),
                  pl.BlockSpec(memory_space=pl.ANY)],
        out_specs=[big(QK_PAD), big(QK_PAD), big(VDIM), mo(QK_PAD), mo(VDIM), pl.BlockSpec(memory_space=pl.ANY)],
        out_shape=[jax.ShapeDtypeStruct((nb, HEADS, s, QK_PAD), F32),
                   jax.ShapeDtypeStruct((nb, HEADS, s, QK_PAD), F32),
                   jax.ShapeDtypeStruct((nb, HEADS, s, VDIM), F32),
                   jax.ShapeDtypeStruct((HEADS, N_META, QK_PAD), F32),
                   jax.ShapeDtypeStruct((HEADS, N_META, VDIM), F32),
                   jax.ShapeDtypeStruct(p_out.shape[1:], F32)],
        scratch_shapes=[pltpu.VMEM((2, t, s), BF16), pltpu.VMEM((2, t, s), BF16)] + red.scratch(),
        compiler_params=_cparams("arbitrary", "arbitrary"),
    )(q, k, v, do, lse, delta, km, vm, p_out)


def _up_bwd(dq, dk, dv, dkm, dvm, p, pm, tabs, tabs_m, wq_p, wkv_p, gq, gkv, nb, s, tm):
    nt = s // tm
    n = nb * nt
    c_t, sa_t, sb_t = tabs
    cm_t, sam_t, sbm_t = tabs_m

    def kv_path(dkh, dvh, pa, c, sa, sb, wkv, gkvv):
        dkpe = dkh[0][:, NOPE:]
        for h in range(1, HEADS):
            dkpe = dkpe + dkh[h][:, NOPE:]
        dkr = _rope_bwd(dkpe, c, sa, sb)
        dkv = jnp.concatenate([d[:, :NOPE] for d in dkh] + list(dvh), axis=1).astype(BF16)
        ckv = pa[:, Q_RANK:Q_RANK + KV_RANK]
        kvn, rkv = _rms(ckv, gkvv)
        dckv, dg = _rms_bwd(_dot(dkv, wkv), ckv, rkv, gkvv)
        return dckv, dkr, kvn.astype(BF16), dkv, jnp.sum(dg, axis=0, keepdims=True)

    def body(dq_ref, dk_ref, dv_ref, pa_ref, c_ref, sa_ref, sb_ref,
             dkm_ref, dvm_ref, pam_ref, cm_ref, sam_ref, sbm_ref,
             wq_ref, wkv_ref, gq_ref, gkv_ref,
             dpa_ref, dpam_ref, pq_ref, pkv_ref, dgq_ref, dgkv_ref, dwq_ref, dwkv_ref):
        i = pl.program_id(0)

        @pl.when(i == 0)
        def _():
            dwq_ref[...] = jnp.zeros_like(dwq_ref)
            dwkv_ref[...] = jnp.zeros_like(dwkv_ref)
            dgq_ref[...] = jnp.zeros_like(dgq_ref)
            dgkv_ref[...] = jnp.zeros_like(dgkv_ref)

        @pl.when(i < n)
        def _():
            c, sa, sb = c_ref[...], sa_ref[...], sb_ref[...]
            pa = pa_ref[...]
            parts = []
            for h in range(HEADS):
                dqh = dq_ref[0, h] * ATTN_SCALE
                parts += [dqh[:, :NOPE], _rope_bwd(dqh[:, NOPE:], c, sa, sb)]
            dql = jnp.concatenate(parts, axis=1).astype(BF16)
            cq = pa[:, 0:Q_RANK]
            gqv = gq_ref[...]
            qn, rq = _rms(cq, gqv)
            dwq_ref[...] += _dot_tn(dql, qn.astype(BF16))
            dcq, dg = _rms_bwd(_dot(dql, wq_ref[...]), cq, rq, gqv)
            dgq_ref[...] += jnp.sum(dg, axis=0, keepdims=True)
            dckv, dkr, kvn, dkv, dgk = kv_path([dk_ref[0, h] for h in range(HEADS)],
                                               [dv_ref[0, h] for h in range(HEADS)],
                                               pa, c, sa, sb, wkv_ref[...], gkv_ref[...])
            dwkv_ref[...] += _dot_tn(dkv, kvn)
            dgkv_ref[...] += dgk
            dpa_ref[...] = jnp.concatenate([dcq, dckv, dkr], axis=1).astype(BF16)

        @pl.when(i == n)
        def _():
            dckv, dkr, kvn, dkv, dgk = kv_path([dkm_ref[h] for h in range(HEADS)],
                                               [dvm_ref[h] for h in range(HEADS)],
                                               pam_ref[...], cm_ref[...], sam_ref[...], sbm_ref[...],
                                               wkv_ref[...], gkv_ref[...])
            dwkv_ref[...] += _dot_tn(dkv, kvn)
            dgkv_ref[...] += dgk
            dpam_ref[...] = jnp.concatenate([jnp.zeros((N_META, Q_RANK), F32), dckv, dkr], axis=1)
            for h in range(HEADS):
                pq_ref[h] = dwq_ref[QK_PAD * h:QK_PAD * h + NOPE + ROPE, :]
                pkv_ref[h, 0:NOPE, :] = dwkv_ref[NOPE * h:NOPE * (h + 1), :]
                pkv_ref[h, NOPE:NOPE + VDIM, :] = dwkv_ref[512 + VDIM * h:512 + VDIM * (h + 1), :]

    cl = lambda i: jnp.minimum(i, n - 1)
    hb = lambda w: pl.BlockSpec((1, HEADS, tm, w), lambda i: (cl(i) // nt, 0, cl(i) % nt, 0))
    tab = pl.BlockSpec((tm, 128), lambda i: (cl(i) % nt, 0))
    full = lambda a: pl.BlockSpec(a.shape, lambda i: (0,) * a.ndim)
    const = lambda shape: pl.BlockSpec(shape, lambda i: (0,) * len(shape))
    return pl.pallas_call(
        body, name="up_bwd", grid=(n + 1,),
        in_specs=[hb(QK_PAD), hb(QK_PAD), hb(VDIM), pl.BlockSpec((tm, 512), lambda i: (cl(i), 0)), tab, tab, tab,
                  full(dkm), full(dvm), pl.BlockSpec((N_META, 512), lambda i: (0, 0)),
                  full(cm_t), full(sam_t), full(sbm_t), full(wq_p), full(wkv_p), full(gq), full(gkv)],
        out_specs=[pl.BlockSpec((tm, 512), lambda i: (cl(i), 0)), const((N_META, 512)),
                   const((HEADS, NOPE + ROPE, Q_RANK)), const((HEADS, NOPE + VDIM, KV_RANK)),
                   const((1, Q_RANK)), const((1, KV_RANK))],
        out_shape=[jax.ShapeDtypeStruct((nb * s, 512), BF16), jax.ShapeDtypeStruct((N_META, 512), F32),
                   jax.ShapeDtypeStruct((HEADS, NOPE + ROPE, Q_RANK), F32),
                   jax.ShapeDtypeStruct((HEADS, NOPE + VDIM, KV_RANK), F32),
                   jax.ShapeDtypeStruct((1, Q_RANK), F32), jax.ShapeDtypeStruct((1, KV_RANK), F32)],
        scratch_shapes=[pltpu.VMEM((HEADS * QK_PAD, Q_RANK), F32), pltpu.VMEM((1024, KV_RANK), F32)],
        compiler_params=_cparams("arbitrary"),
    )(dq, dk, dv, p, c_t, sa_t, sb_t, dkm, dvm, pm, cm_t, sam_t, sbm_t, wq_p, wkv_p, gq, gkv)


def _in_bwd(x2d, dh2, dpa, dpb, meta, dpam, dccm, pm, w_in_p, norm_g, nb, s, tm):
    nt = s // tm
    n = nb * nt

    def body(x_ref, dh_ref, dpa_ref, dpb_ref, mt_ref, dpam_ref, dccm_ref, mc_ref, mh_ref, w_ref, g_ref,
             gx_ref, gm_ref, dw_hbm, dg_ref, acc_ref, sems):
        i = pl.program_id(0)

        @pl.when(i == 0)
        def _():
            acc_ref[...] = jnp.zeros_like(acc_ref)
            dg_ref[...] = jnp.zeros_like(dg_ref)

        def rows(x, dp, dres):
            g = g_ref[...]
            u, r1 = _rms(x, g)
            dpb16 = dp.astype(BF16)
            acc_ref[...] += _dot_tn(dpb16, u.astype(BF16))
            dx, dg = _rms_bwd(_dot(dpb16, w_ref[...]), x, r1, g)
            dg_ref[...] += jnp.sum(dg, axis=0, keepdims=True)
            return dx if dres is None else dx + dres

        @pl.when(i < n)
        def _():
            dp = jnp.concatenate([dpa_ref[...], dpb_ref[...]], axis=1)
            gx_ref[...] = rows(x_ref[...], dp, dh_ref[...])

        @pl.when(i == n)
        def _():
            dcc = dccm_ref[0]
            for b in range(1, nb):
                dcc = dcc + dccm_ref[b]
            z8 = jnp.zeros((8, CONV_W), F32)
            dc = jnp.concatenate([z8, dcc * mh_ref[8:16, :]], axis=0)
            dh = jnp.concatenate([z8, dcc * mc_ref[8:16, :]], axis=0)
            z = jnp.zeros((N_META, CONV_W), F32)
            dp = jnp.concatenate([dpam_ref[...], z, z, dc, dh, z], axis=1)
            gm_ref[...] = rows(mt_ref[...], dp, None)
            per = IN_DIM // 4
            cps = [pltpu.make_async_copy(acc_ref.at[0:448], dw_hbm.at[0, 0:448], sems.at[0]),
                   pltpu.make_async_copy(acc_ref.at[512:per + 64], dw_hbm.at[0, 448:per], sems.at[1])]
            for qq in range(1, 4):
                cps.append(pltpu.make_async_copy(acc_ref.at[per * qq + 64:per * (qq + 1) + 64], dw_hbm.at[qq],
                                                 sems.at[qq + 1]))
            for cp in cps:
                cp.start()
            for cp in cps:
                cp.wait()

    cl = lambda i: jnp.minimum(i, n - 1)
    row = lambda w: pl.BlockSpec((tm, w), lambda i: (cl(i), 0))
    full = lambda a: pl.BlockSpec(a.shape, lambda i: (0,) * a.ndim)
    mblk = lambda j: pl.BlockSpec((N_META, 512), lambda i: (0, j))
    return pl.pallas_call(
        body, name="in_bwd", grid=(n + 1,),
        in_specs=[row(D_MODEL), row(D_MODEL), row(512), row(2560), full(meta), full(dpam), full(dccm),
                  mblk(BLK_CC), mblk(BLK_CH), full(w_in_p), full(norm_g)],
        out_specs=[row(D_MODEL), pl.BlockSpec((N_META, D_MODEL), lambda i: (0, 0)),
                   pl.BlockSpec(memory_space=pl.ANY), pl.BlockSpec((1, D_MODEL), lambda i: (0, 0))],
        out_shape=[jax.ShapeDtypeStruct((nb * s, D_MODEL), F32), jax.ShapeDtypeStruct((N_META, D_MODEL), F32),
                   jax.ShapeDtypeStruct((4, IN_DIM // 4, D_MODEL), F32), jax.ShapeDtypeStruct((1, D_MODEL), F32)],
        scratch_shapes=[pltpu.VMEM((IN_PAD, D_MODEL), F32), pltpu.SemaphoreType.DMA((5,))],
        compiler_params=_cparams("arbitrary"),
    )(x2d, dh2, dpa, dpb, meta, dpam, dccm, pm, pm, w_in_p, norm_g)


def _gather_weights(split, pieces, out_rows, whole, zero_fills):
    ns, nw, nz = len(split), len(whole), len(zero_fills)
    flat = [(a, pc) for a in range(ns) for pc in pieces[a]]
    nk = len(flat)

    def body(*refs):
        ins, wins, zins = refs[:ns], refs[ns:ns + nw], refs[ns + nw:ns + nw + nz]
        outs, wouts = refs[ns + nw + nz:2 * ns + nw + nz], refs[2 * ns + nw + nz:2 * (ns + nw) + nz]
        send_sems, recv_sems, fwd_send, fwd_recv, loc_sems, w_send, w_recv, w_loc, z_sems = refs[2 * (ns + nw) + nz:]
        x, y, c = lax.axis_index("x"), lax.axis_index("y"), lax.axis_index("c")
        mine = 2 * x + y
        chips = [(1 - x, y), (x, 1 - y), (1 - x, 1 - y)]
        chip_of = [2 * px + py for px, py in chips]

        def src(k):
            a, (s0, nr, _, _, _, _) = flat[k]
            return ins[a].at[s0:s0 + nr]

        def dst(k, q):
            a, (_, nr, per, first, rest, _) = flat[k]
            row = per * q + first + (rest - first) * jnp.minimum(q, 1)
            return outs[a].at[pl.ds(pl.multiple_of(row, 16), nr)]

        def ici(k, j, q):
            px, py = chips[j]
            return pltpu.make_async_remote_copy(
                src_ref=src(k), dst_ref=dst(k, q), send_sem=send_sems.at[k, j], recv_sem=recv_sems.at[k, j],
                device_id=(px, py, c), device_id_type=pl.DeviceIdType.MESH)

        def fwd(k, j):
            ref = dst(k, chip_of[j])
            return pltpu.make_async_remote_copy(
                src_ref=ref, dst_ref=ref, send_sem=fwd_send.at[k, j], recv_sem=fwd_recv.at[k, j],
                device_id=(x, y, 1 - c), device_id_type=pl.DeviceIdType.MESH)

        def wcopy(b, j, q):
            px, py = chips[j]
            return pltpu.make_async_remote_copy(
                src_ref=wins[b], dst_ref=wouts[b].at[q], send_sem=w_send.at[b, j], recv_sem=w_recv.at[b, j],
                device_id=(px, py, c), device_id_type=pl.DeviceIdType.MESH)

        local = [pltpu.make_async_copy(src(k), dst(k, mine), loc_sems.at[k]) for k in range(nk)]
        local += [pltpu.make_async_copy(wins[b], wouts[b].at[mine], w_loc.at[b]) for b in range(nw)]
        for z, (a, _, row0) in enumerate(zero_fills):
            local.append(pltpu.make_async_copy(zins[z], outs[a].at[row0:row0 + zins[z].shape[0]], z_sems.at[z]))
        wsends = [wcopy(b, j, mine) for b in range(nw) for j in range(3)]
        for cp in local + wsends:
            cp.start()

        for half in (0, 1):
            @pl.when(c == half)
            def _(half=half):
                my_k = [k for k in range(nk) if flat[k][1][5] == half]
                other_k = [k for k in range(nk) if flat[k][1][5] != half]
                sends = [ici(k, j, mine) for k in my_k for j in range(3)]
                for cp in sends:
                    cp.start()
                passed = []
                for k in my_k:
                    for j in range(3):
                        ici(k, j, chip_of[j]).wait_recv()
                        cp = fwd(k, j)
                        cp.start()
                        passed.append(cp)
                for k in other_k:
                    for j in range(3):
                        fwd(k, j).wait_recv()
                for cp in sends + passed:
                    cp.wait_send()

        for b in range(nw):
            for j in range(3):
                wcopy(b, j, chip_of[j]).wait_recv()
        for cp in wsends:
            cp.wait_send()
        for cp in local:
            cp.wait()

    hbm = pl.BlockSpec(memory_space=pl.ANY)
    dma = pltpu.SemaphoreType.DMA
    zeros = [z for _, z, _ in zero_fills]
    return pl.pallas_call(
        body, name="gather_weights",
        in_specs=[hbm] * (ns + nw + nz), out_specs=[hbm] * (ns + nw),
        out_shape=([jax.ShapeDtypeStruct((out_rows[a], split[a].shape[1]), split[a].dtype) for a in range(ns)]
                   + [jax.ShapeDtypeStruct((4,) + w.shape, w.dtype) for w in whole]),
        scratch_shapes=[dma((nk, 3)), dma((nk, 3)), dma((nk, 3)), dma((nk, 3)), dma((nk,)),
                        dma((nw, 3)), dma((nw, 3)), dma((nw,)), dma((nz,))],
        compiler_params=pltpu.CompilerParams(vmem_limit_bytes=VMEM_LIMIT),
    )(*split, *whole, *zeros)


def _reduce_grads(parts, small):
    n = len(parts)
    shapes = [a.shape[1:] for a in parts]
    halves = [(sh[0] // 2, sh[1]) for sh in shapes]

    def body(*refs):
        pin, sm_in = refs[:n], refs[n]
        gout, sm_out = refs[n + 1:2 * n + 1], refs[2 * n + 1]
        scr = refs[2 * n + 2:]
        own, sib, wire, rbuf = scr[:n], scr[n:2 * n], scr[2 * n:3 * n], scr[3 * n:4 * n]
        (sbuf, send_sems, recv_sems, loc_sems, pre_send, pre_recv, post_send, post_recv,
         sm_send, sm_recv) = scr[4 * n:]
        x, y, c = lax.axis_index("x"), lax.axis_index("y"), lax.axis_index("c")
        mine = 2 * x + y
        me = 4 * x + 2 * y + c
        sibling = (x, y, 1 - c)
        chips = [(1 - x, y), (x, 1 - y), (1 - x, 1 - y)]

        def rows(a, half):
            r2 = halves[a][0]
            return pl.ds(pl.multiple_of(half * r2, r2), r2)

        def pre(a):
            return pltpu.make_async_remote_copy(
                src_ref=pin[a].at[:, rows(a, 1 - c), :], dst_ref=sib[a], send_sem=pre_send.at[a],
                recv_sem=pre_recv.at[a], device_id=sibling, device_id_type=pl.DeviceIdType.MESH)

        def ici(a, j):
            px, py = chips[j]
            return pltpu.make_async_remote_copy(
                src_ref=wire[a].at[2 * px + py], dst_ref=rbuf[a].at[j], send_sem=send_sems.at[a, j],
                recv_sem=recv_sems.at[a, j], device_id=(px, py, c), device_id_type=pl.DeviceIdType.MESH)

        def post(a, half):
            ref = gout[a].at[rows(a, half), :]
            return pltpu.make_async_remote_copy(
                src_ref=ref, dst_ref=ref, send_sem=post_send.at[a], recv_sem=post_recv.at[a],
                device_id=sibling, device_id_type=pl.DeviceIdType.MESH)

        def small_copy(kk):
            peer = (x ^ (kk >> 2), y ^ ((kk >> 1) & 1), c ^ (kk & 1))
            return pltpu.make_async_remote_copy(
                src_ref=sm_in, dst_ref=sbuf.at[kk], send_sem=sm_send.at[kk - 1], recv_sem=sm_recv.at[kk - 1],
                device_id=peer, device_id_type=pl.DeviceIdType.MESH)

        local = [pltpu.make_async_copy(pin[a].at[:, rows(a, c), :], own[a], loc_sems.at[a]) for a in range(n)]
        pres = [pre(a) for a in range(n)]
        smalls = [small_copy(kk) for kk in range(1, 8)]
        for cp in local + pres + smalls:
            cp.start()
        sbuf[0] = sm_in[...]
        sends = []
        for a in range(n):
            local[a].wait()
            pres[a].wait_recv()
            for blk in range(4):
                tot = own[a][blk] + sib[a][blk]
                own[a][blk] = tot
                wire[a][blk] = tot.astype(BF16)
            for j in range(3):
                cp = ici(a, j)
                cp.start()
                sends.append(cp)
        for cp in smalls:
            cp.wait_recv()
        total = sbuf[me]
        for d in range(1, 8):
            total = total + sbuf[me ^ d]
        sm_out[...] = total
        posts = []
        for a in range(n):
            for j in range(3):
                ici(a, j).wait_recv()
            fin = own[a][mine]
            for j in range(3):
                fin = fin + rbuf[a][j].astype(F32)
            gout[a][rows(a, c), :] = fin
            cp = post(a, c)
            cp.start()
            posts.append(cp)
        for a in range(n):
            post(a, 1 - c).wait_recv()
        for cp in pres + sends + smalls + posts:
            cp.wait_send()

    hbm = pl.BlockSpec(memory_space=pl.ANY)
    vmem = pl.BlockSpec(memory_space=pltpu.VMEM)
    dma = pltpu.SemaphoreType.DMA
    return pl.pallas_call(
        body, name="reduce_grads",
        in_specs=[hbm] * n + [vmem], out_specs=[vmem] * (n + 1),
        out_shape=[jax.ShapeDtypeStruct(sh, F32) for sh in shapes] + [jax.ShapeDtypeStruct(small.shape, F32)],
        scratch_shapes=([pltpu.VMEM((4,) + hs, F32) for hs in halves] + [pltpu.VMEM((4,) + hs, F32) for hs in halves]
                        + [pltpu.VMEM((4,) + hs, BF16) for hs in halves]
                        + [pltpu.VMEM((3,) + hs, BF16) for hs in halves]
                        + [pltpu.VMEM((8,) + small.shape, F32), dma((n, 3)), dma((n, 3)), dma((n,)),
                           dma((n,)), dma((n,)), dma((n,)), dma((n,)), dma((7,)), dma((7,))]),
        compiler_params=pltpu.CompilerParams(vmem_limit_bytes=VMEM_LIMIT),
    )(*parts, small)


def _adamw(w, g, m, v, name):
    shape = w.shape
    w2, g2, m2, v2 = (a.reshape((-1, shape[-1])) for a in (w, g, m, v))

    def body(w_ref, g_ref, m_ref, v_ref, d_ref, nm_ref, nv_ref):
        gv = g_ref[...]
        nm = ADAM_B1 * m_ref[...] + (1.0 - ADAM_B1) * gv
        nv = ADAM_B2 * v_ref[...] + (1.0 - ADAM_B2) * (gv * gv)
        m_hat = nm / (1.0 - ADAM_B1 ** ADAM_STEP)
        v_hat = nv / (1.0 - ADAM_B2 ** ADAM_STEP)
        d_ref[...] = -ADAM_LR * (m_hat / (jnp.sqrt(v_hat) + ADAM_EPS) + ADAM_WD * w_ref[...])
        nm_ref[...] = nm
        nv_ref[...] = nv

    out = pl.pallas_call(
        body, name=name,
        out_shape=[jax.ShapeDtypeStruct(w2.shape, F32)] * 3,
        compiler_params=pltpu.CompilerParams(vmem_limit_bytes=VMEM_LIMIT),
    )(w2, g2, m2, v2)
    return tuple(a.reshape(shape) for a in out)


def kernel(x, meta_tokens, norm_g, w_in, q_norm_g, w_q_up, kv_norm_g, w_kv_up, conv_w, attn_out_g, conv_out_g, w_out, final_norm_g, loss_target, m_meta_tokens, m_norm_g, m_w_in, m_q_norm_g, m_w_q_up, m_kv_norm_g, m_w_kv_up, m_conv_w, m_attn_out_g, m_conv_out_g, m_w_out, m_final_norm_g, v_meta_tokens, v_norm_g, v_w_in, v_q_norm_g, v_w_q_up, v_kv_norm_g, v_w_kv_up, v_conv_w, v_attn_out_g, v_conv_out_g, v_w_out, v_final_norm_g):
    nb, s, _ = x.shape
    tm = min(ROW_TILE, s)
    ta = min(ATTN_TILE, s)
    assert s % tm == 0 and s % ta == 0 and tm % 16 == 0
    r = nb * s

    tr = lambda a: jnp.transpose(a[0])
    w_in_p, wq_p, wkv_p, g_cw, g_meta = _gather_weights(
        [tr(w_in).astype(BF16), tr(w_q_up).astype(BF16), tr(w_kv_up).astype(BF16)],
        [W_IN_PIECES, W_Q_PIECES, W_KV_PIECES], [IN_PAD, HEADS * QK_PAD, 1024],
        [conv_w[0], meta_tokens],
        [(0, jnp.zeros((64, D_MODEL), BF16), 448)]
        + [(1, jnp.zeros((64, Q_RANK), BF16), QK_PAD * h + NOPE + ROPE) for h in range(HEADS)])
    conv_f = jnp.transpose(g_cw, (1, 0, 2)).reshape(3, CONV_W)
    meta_f = jnp.transpose(g_meta, (1, 0, 2)).reshape(N_META, D_MODEL)

    c_all, sa_all, sb_all = _rope_tables(N_META + s)
    tabs_m = (c_all[:N_META], sa_all[:N_META], sb_all[:N_META])
    tabs = (c_all[N_META:], sa_all[N_META:], sb_all[N_META:])
    gid = np.arange(CONV_W) // CONV_GROUP
    gmat = jnp.asarray(np.where(gid[:, None] == gid[None, :], 1.0 / CONV_GROUP, 0.0), BF16)
    ga, gc = attn_out_g, conv_out_g
    gf = final_norm_g.reshape(1, D_MODEL)

    x2d = x.reshape(r, D_MODEL)
    tgt2d = loss_target.reshape(r, D_MODEL)

    pm, _, km, vm = _fwd_proj(meta_f, tabs_m, norm_g, w_in_p, q_norm_g, wq_p, kv_norm_g, wkv_p,
                              1, N_META, N_META, "fwd_proj_meta")
    p, q, k, v, w_out_f = _fwd_proj(x2d, tabs, norm_g, w_in_p, q_norm_g, wq_p, kv_norm_g, wkv_p, nb, s, tm,
                                    "fwd_proj", w_out[0].astype(BF16))
    o, lse = _attn_fwd(q, k, v, km, vm, nb, s, ta)
    dh2, dycat, dw_out, dgf, loss_acc = _out_fwd_bwd(x2d, tgt2d, o, p, pm, conv_f, ga, gc, gmat, w_out_f, gf,
                                                     nb, s, tm)
    dpb, do, delta, dccm, dga, dgc, dcw = _gate_bwd(dycat, o, p, pm, conv_f, ga, gc, gmat, nb, s, tm)
    p_out = dw_out.reshape(4, D_MODEL // 4, D_MODEL)
    dq, dk, dv, dkm, dvm, g_w_out = _attn_bwd(q, k, v, do, lse, delta, km, vm, p_out, nb, s, ta)
    dpa, dpam, p_q, p_kv, dgq, dgkv = _up_bwd(dq, dk, dv, dkm, dvm, p, pm, tabs, tabs_m, wq_p, wkv_p,
                                              q_norm_g, kv_norm_g, nb, s, tm)
    gx, gmeta, p_in, dng = _in_bwd(x2d, dh2, dpa, dpb, meta_f, dpam, dccm, pm, w_in_p, norm_g, nb, s, tm)

    flat =jnp.concatenate([dng.reshape(-1), dgq.reshape(-1), dgkv.reshape(-1), dga.reshape(-1), dgc.reshape(-1),
                            dgf.reshape(-1), dcw[:3].reshape(-1), gmeta.reshape(-1), loss_acc[0, 0:1]])
    n_small = flat.shape[0]
    rows_small = -(-n_small // 1024) * 8
    small = jnp.pad(flat, (0, rows_small * 128 - n_small)).reshape(rows_small, 128)
    g_w_in_t, g_w_q_t, g_w_kv_t, small_sum = _reduce_grads([p_in, p_q, p_kv], small)
    ssum = small_sum.reshape(-1)

    def take(off, n):
        return ssum[off:off + n], off + n

    off = 0
    g_norm, off = take(off, D_MODEL)
    g_qn, off = take(off, Q_RANK)
    g_kvn, off = take(off, KV_RANK)
    g_ga, off = take(off, CONV_W)
    g_gc, off = take(off, CONV_W)
    g_gf, off = take(off, D_MODEL)
    g_cw_all, off = take(off, 3 * CONV_W)
    g_meta_all, off = take(off, N_META * D_MODEL)
    loss = ssum[off]
    chip = 2 * lax.axis_index("x") + lax.axis_index("y")
    g_conv = lax.dynamic_slice(g_cw_all.reshape(3, CONV_W), (0, chip * 128), (3, 128))
    g_mt = lax.dynamic_slice(g_meta_all.reshape(N_META, D_MODEL), (0, chip * 256), (N_META, 256))

    grads = {
        "meta_tokens": g_mt, "norm_g": g_norm.reshape(1, -1), "w_in": g_w_in_t, "q_norm_g": g_qn.reshape(1, -1),
        "w_q_up": g_w_q_t, "kv_norm_g": g_kvn.reshape(1, -1), "w_kv_up": jnp.transpose(g_w_kv_t)[None],
        "conv_w": g_conv[None], "attn_out_g": g_ga.reshape(1, -1), "conv_out_g": g_gc.reshape(1, -1),
        "w_out": g_w_out[None], "final_norm_g": g_gf,
    }
    transposed = ("w_in", "w_q_up")
    weights = {
        "meta_tokens": (meta_tokens, m_meta_tokens, v_meta_tokens), "norm_g": (norm_g, m_norm_g, v_norm_g),
        "w_in": (w_in, m_w_in, v_w_in), "q_norm_g": (q_norm_g, m_q_norm_g, v_q_norm_g),
        "w_q_up": (w_q_up, m_w_q_up, v_w_q_up), "kv_norm_g": (kv_norm_g, m_kv_norm_g, v_kv_norm_g),
        "w_kv_up": (w_kv_up, m_w_kv_up, v_w_kv_up), "conv_w": (conv_w, m_conv_w, v_conv_w),
        "attn_out_g": (attn_out_g, m_attn_out_g, v_attn_out_g), "conv_out_g": (conv_out_g, m_conv_out_g, v_conv_out_g),
        "w_out": (w_out, m_w_out, v_w_out), "final_norm_g": (final_norm_g, m_final_norm_g, v_final_norm_g),
    }
    names = list(weights)
    deltas, new_m, new_v = [], [], []
    for nme in names:
        w_, m_, v_ = weights[nme]
        if nme in transposed:
            res = _adamw(tr(w_), grads[nme], tr(m_), tr(v_), "adamw_" + nme)
            g_, d_, nm_, nv_ = (jnp.transpose(a)[None] for a in (grads[nme],) + res)
        else:
            g_ = grads[nme].reshape(w_.shape)
            d_, nm_, nv_ = _adamw(w_, g_, m_, v_, "adamw_" + nme)
        grads[nme] = g_
        deltas.append(d_)
        new_m.append(nm_)
        new_v.append(nv_)

    grad_x = gx.reshape(nb, s, D_MODEL)
    return (loss, grad_x, *[grads[nme] for nme in names], *deltas, *new_m, *new_v)
```

```python
import functools

import jax
import jax.numpy as jnp
import numpy as np
from jax import lax
from jax.experimental import pallas as pl
from jax.experimental.pallas import tpu as pltpu

F32 = jnp.float32
BF16 = jnp.bfloat16

D_MODEL = 1024
N_META = 16
HEADS = 4
NOPE = 128
ROPE = 64
VDIM = 128
QK_PAD = 256
Q_RANK = 256
KV_RANK = 128
CONV_W = 512
CONV_GROUP = 64
ROPE_THETA = 10000.0
EPS = 1e-6
ATTN_SCALE = (NOPE + ROPE) ** -0.5
IN_DIM = 3008
IN_PAD = 3072
BLK_ZA, BLK_CB, BLK_CC, BLK_CH, BLK_ZC = 1, 2, 3, 4, 5
NEG_INF = -1e30

ADAM_LR = 0.001
ADAM_B1 = 0.9
ADAM_B2 = 0.999
ADAM_EPS = 1e-08
ADAM_WD = 0.01
ADAM_STEP = 10

ROW_TILE = 512
ATTN_TILE = 256
VMEM_LIMIT = 56 * 1024 * 1024

NT = (((1,), (1,)), ((), ()))
TN = (((0,), (0,)), ((), ()))


def _cparams(*sem):
    return pltpu.CompilerParams(dimension_semantics=sem, vmem_limit_bytes=VMEM_LIMIT)


def _dot(a, b):
    return jnp.dot(a, b, preferred_element_type=F32)


def _dot_nt(a, b):
    return lax.dot_general(a, b, NT, preferred_element_type=F32)


def _dot_tn(a, b):
    return lax.dot_general(a, b, TN, preferred_element_type=F32)


def _rms(x, g):
    r = lax.rsqrt(jnp.mean(x * x, axis=-1, keepdims=True) + EPS)
    return x * r * g, r


def _rms_bwd(dy, x, r, g):
    xh = x * r
    dyg = dy * g
    dx = r * (dyg - xh * jnp.mean(dyg * xh, axis=-1, keepdims=True))
    return dx, dy * xh


def _sigmoid(z):
    return 1.0 / (1.0 + jnp.exp(-z))


def _rope(b, c, sa, sb):
    return b * c + pltpu.roll(b, 96, 1) * sa + pltpu.roll(b, 32, 1) * sb


def _rope_bwd(d, c, sa, sb):
    return d * c + pltpu.roll(d * sa, 32, 1) + pltpu.roll(d * sb, 96, 1)


def _group_mean(x, gmat):
    hi = x.astype(BF16)
    lo = (x - hi.astype(F32)).astype(BF16)
    return _dot(hi, gmat) + _dot(lo, gmat)


def _row_of(col, rows):
    return jnp.transpose(jnp.broadcast_to(col, (rows, 128)))[0:1, :]


def _rope_tables(n_pos):
    half = ROPE // 2
    inv_freq = (np.float32(1.0) / (np.float32(ROPE_THETA) ** (np.arange(half, dtype=np.float32) / np.float32(half))))
    ang = np.arange(n_pos, dtype=np.float32)[:, None] * inv_freq.astype(np.float32)[None, :]
    cos, sin = np.cos(ang).astype(np.float32), np.sin(ang).astype(np.float32)
    z = np.zeros((n_pos, half), np.float32)
    c = np.concatenate([cos, cos, z, z], axis=1)
    sa = np.concatenate([-sin, z, z, z], axis=1)
    sb = np.concatenate([z, sin, z, z], axis=1)
    return jnp.asarray(c), jnp.asarray(sa), jnp.asarray(sb)


W_IN_PIECES = ((0, 384, 752, 0, 64, 0), (384, 64, 752, 384, 448, 1), (448, 304, 752, 512, 512, 1))
W_Q_PIECES = ((0, 96, 256, 0, 0, 0), (96, 96, 256, 96, 96, 1))
W_KV_PIECES = ((0, 128, 128, 0, 0, 0), (128, 128, 128, 512, 512, 1))
W_OUT_PIECES = ((0, 128, 256, 0, 0, 0), (128, 128, 256, 128, 128, 1))


class _StagedGather:
    def __init__(self, pieces):
        self.pieces = pieces

    def scratch(self):
        nk, dma = len(self.pieces), pltpu.SemaphoreType.DMA
        return [dma((nk, 3)), dma((nk, 3)), dma((nk, 3)), dma((nk, 3)), dma((nk,))]

    def run(self, stage, src_ref, out_ref, scr):
        send_sems, recv_sems, fwd_send, fwd_recv, loc_sems = scr
        pieces = self.pieces
        nk = len(pieces)
        x, y, c = lax.axis_index("x"), lax.axis_index("y"), lax.axis_index("c")
        mine = 2 * x + y
        chips = [(1 - x, y), (x, 1 - y), (1 - x, 1 - y)]
        chip_of = [2 * px + py for px, py in chips]
        mesh = pl.DeviceIdType.MESH

        def src(k):
            s0, nr = pieces[k][0], pieces[k][1]
            return src_ref.at[s0:s0 + nr]

        def dst(k, q):
            _, nr, per, first, rest, _ = pieces[k]
            row = per * q + first + (rest - first) * jnp.minimum(q, 1)
            return out_ref.at[pl.ds(pl.multiple_of(row, 16), nr)]

        def ici(k, j, q):
            px, py = chips[j]
            return pltpu.make_async_remote_copy(
                src_ref=src(k), dst_ref=dst(k, q), send_sem=send_sems.at[k, j], recv_sem=recv_sems.at[k, j],
                device_id=(px, py, c), device_id_type=mesh)

        def fwd(k, j):
            ref = dst(k, chip_of[j])
            return pltpu.make_async_remote_copy(
                src_ref=ref, dst_ref=ref, send_sem=fwd_send.at[k, j], recv_sem=fwd_recv.at[k, j],
                device_id=(x, y, 1 - c), device_id_type=mesh)

        local = [pltpu.make_async_copy(src(k), dst(k, mine), loc_sems.at[k]) for k in range(nk)]
        if stage == 0:
            for cp in local:
                cp.start()
        if stage == 2:
            for cp in local:
                cp.wait()
        for half in (0, 1):
            @pl.when(c == half)
            def _(half=half):
                my_k = [k for k in range(nk) if pieces[k][5] == half]
                other_k = [k for k in range(nk) if pieces[k][5] != half]
                for k in my_k:
                    for j in range(3):
                        if stage == 0:
                            ici(k, j, mine).start()
                        elif stage == 1:
                            ici(k, j, chip_of[j]).wait_recv()
                            fwd(k, j).start()
                        else:
                            ici(k, j, mine).wait_send()
                            fwd(k, j).wait_send()
                if stage == 2:
                    for k in other_k:
                        for j in range(3):
                            fwd(k, j).wait_recv()


def _fwd_proj(x2d, meta, tabs, tabs_m, norm_g, w_in_p, q_norm_g, wq_p, kv_norm_g, wkv_p, w_out_shard, nb, s, tm):
    nt = s // tm
    n = nb * nt
    n_steps = n + 1
    c_t, sa_t, sb_t = tabs
    cm_t, sam_t, sbm_t = tabs_m
    gat = _StagedGather(W_OUT_PIECES)
    assert n_steps >= 3

    def body(x_ref, c_ref, sa_ref, sb_ref, mt_ref, cm_ref, sam_ref, sbm_ref,
             g_ref, w_ref, gq_ref, wq_ref, gkv_ref, wkv_ref, wos_ref,
             p_ref, q_ref, k_ref, v_ref, pm_ref, km_ref, vm_ref, wo_ref, *gat_scr):
        i = pl.program_id(0)
        for stage, at in enumerate((0, n_steps - 2, n_steps - 1)):
            @pl.when(i == at)
            def _(stage=stage):
                gat.run(stage, wos_ref, wo_ref, gat_scr)

        def project(xv, c, sa, sb, p_out, q_out, k_out, v_out):
            u, _ = _rms(xv, g_ref[...])
            p = _dot_nt(u.astype(BF16), w_ref[...])
            p_out[...] = p
            qn, _ = _rms(p[:, 0:Q_RANK], gq_ref[...])
            q = _dot_nt(qn.astype(BF16), wq_ref[...])
            kvn, _ = _rms(p[:, Q_RANK:Q_RANK + KV_RANK], gkv_ref[...])
            kv = _dot_nt(kvn.astype(BF16), wkv_ref[...])
            kpe = _rope(p[:, 384:512], c, sa, sb)
            for h in range(HEADS):
                if q_out is not None:
                    pe = _rope(q[:, QK_PAD * h + NOPE:QK_PAD * (h + 1)], c, sa, sb)
                    qh = jnp.concatenate([q[:, QK_PAD * h:QK_PAD * h + NOPE], pe], axis=1)
                    q_out[0, h] = (qh * ATTN_SCALE).astype(BF16)
                k_out[0, h] = jnp.concatenate([kv[:, NOPE * h:NOPE * (h + 1)], kpe], axis=1).astype(BF16)
                v_out[0, h] = kv[:, 512 + VDIM * h:512 + VDIM * (h + 1)].astype(BF16)

        @pl.when(i < n)
        def _():
            project(x_ref[...], c_ref[...], sa_ref[...], sb_ref[...], p_ref, q_ref, k_ref, v_ref)

        @pl.when(i == n)
        def _():
            project(mt_ref[...], cm_ref[...], sam_ref[...], sbm_ref[...], pm_ref, None, km_ref, vm_ref)

    cl = lambda i: jnp.minimum(i, n - 1)
    full = lambda a: pl.BlockSpec(a.shape, lambda i: (0,) * a.ndim)
    const = lambda shape: pl.BlockSpec(shape, lambda i: (0,) * len(shape))
    tab = pl.BlockSpec((tm, 128), lambda i: (cl(i) % nt, 0))
    hb = lambda w: pl.BlockSpec((1, HEADS, tm, w), lambda i: (cl(i) // nt, 0, cl(i) % nt, 0))
    hbm = pl.BlockSpec(memory_space=pl.ANY)
    return pl.pallas_call(
        body, name="fwd_proj", grid=(n_steps,),
        in_specs=[pl.BlockSpec((tm, D_MODEL), lambda i: (cl(i), 0)), tab, tab, tab,
                  full(meta), full(cm_t), full(sam_t), full(sbm_t),
                  full(norm_g), full(w_in_p), full(q_norm_g), full(wq_p), full(kv_norm_g), full(wkv_p), hbm],
        out_specs=[pl.BlockSpec((tm, IN_PAD), lambda i: (cl(i), 0)), hb(QK_PAD), hb(QK_PAD), hb(VDIM),
                   const((N_META, IN_PAD)), const((1, HEADS, N_META, QK_PAD)), const((1, HEADS, N_META, VDIM)), hbm],
        out_shape=[jax.ShapeDtypeStruct((nb * s, IN_PAD), F32),
                   jax.ShapeDtypeStruct((nb, HEADS, s, QK_PAD), BF16),
                   jax.ShapeDtypeStruct((nb, HEADS, s, QK_PAD), BF16),
                   jax.ShapeDtypeStruct((nb, HEADS, s, VDIM), BF16),
                   jax.ShapeDtypeStruct((N_META, IN_PAD), F32),
                   jax.ShapeDtypeStruct((1, HEADS, N_META, QK_PAD), BF16),
                   jax.ShapeDtypeStruct((1, HEADS, N_META, VDIM), BF16),
                   jax.ShapeDtypeStruct((D_MODEL, D_MODEL), BF16)],
        scratch_shapes=gat.scratch(),
        compiler_params=_cparams("arbitrary"),
    )(x2d, c_t, sa_t, sb_t, meta, cm_t, sam_t, sbm_t, norm_g, w_in_p, q_norm_g, wq_p, kv_norm_g, wkv_p, w_out_shard)


def _attn_fwd(q, k, v, km, vm, nb, s, tq):
    nq = s // tq

    def body(q_ref, k_ref, v_ref, km_ref, vm_ref, o_ref, lse_ref, s_scr, p_scr):
        row = lax.broadcasted_iota(jnp.int32, (tq, tq), 0)
        col = lax.broadcasted_iota(jnp.int32, (tq, tq), 1)
        for i in range(nq):
            slot = i % 2
            qi = q_ref[0, 0, i * tq:(i + 1) * tq, :]
            sm = _dot_nt(qi, km_ref[0, 0])
            m128 = None
            for j in range(i + 1):
                sc = _dot_nt(qi, k_ref[0, 0, j * tq:(j + 1) * tq, :])
                if j == i:
                    sc = jnp.where(col <= row, sc, NEG_INF)
                s_scr[slot, :, j * tq:(j + 1) * tq] = sc
                mx = sc[:, 0:128]
                for c0 in range(128, tq, 128):
                    mx = jnp.maximum(mx, sc[:, c0:c0 + 128])
                m128 = mx if m128 is None else jnp.maximum(m128, mx)
            m = jnp.maximum(jnp.max(m128, axis=1, keepdims=True), jnp.max(sm, axis=1, keepdims=True))
            pm = jnp.exp(sm - m)
            l128 = None
            for j in range(i + 1):
                p = jnp.exp(s_scr[slot, :, j * tq:(j + 1) * tq] - m)
                p_scr[slot, :, j * tq:(j + 1) * tq] = p.astype(BF16)
                ps = p[:, 0:128]
                for c0 in range(128, tq, 128):
                    ps = ps + p[:, c0:c0 + 128]
                l128 = ps if l128 is None else l128 + ps
            l = jnp.sum(l128, axis=1, keepdims=True) + jnp.sum(pm, axis=1, keepdims=True)
            n = (i + 1) * tq
            acc = _dot(p_scr[slot, :, 0:n], v_ref[0, 0, 0:n, :]) + _dot(pm.astype(BF16), vm_ref[0, 0])
            o_ref[0, 0, i * tq:(i + 1) * tq, :] = acc / l
            lse_ref[0, 0, :, i * tq:(i + 1) * tq] = _row_of(m + jnp.log(l), tq)

    hblk = lambda w: pl.BlockSpec((1, 1, s, w), lambda b, h: (b, h, 0, 0))
    mblk = lambda w: pl.BlockSpec((1, 1, N_META, w), lambda b, h: (0, h, 0, 0))
    return pl.pallas_call(
        body, name="attn_fwd", grid=(nb, HEADS),
        in_specs=[hblk(QK_PAD), hblk(QK_PAD), hblk(VDIM), mblk(QK_PAD), mblk(VDIM)],
        out_specs=[hblk(VDIM), pl.BlockSpec((1, 1, 1, s), lambda b, h: (b, h, 0, 0))],
        out_shape=[jax.ShapeDtypeStruct((nb, HEADS, s, VDIM), F32),
                   jax.ShapeDtypeStruct((nb, HEADS, 1, s), F32)],
        scratch_shapes=[pltpu.VMEM((2, tq, s), F32), pltpu.VMEM((2, tq, s), BF16)],
        compiler_params=_cparams("parallel", "parallel"),
    )(q, k, v, km, vm)


def _shift_rows(a, prev, n_rows):
    rid = lax.broadcasted_iota(jnp.int32, a.shape, 0)
    a1 = jnp.where(rid == 0, prev[7:8, :], pltpu.roll(a, 1, 0))
    a2 = jnp.where(rid == 0, prev[6:7, :], jnp.where(rid == 1, prev[7:8, :], pltpu.roll(a, 2, 0)))
    return a1, a2


def _attn_gate(o, za, ga_h):
    on, r = _rms(o, ga_h)
    return on * (za * _sigmoid(za)), on, r


def _out_fwd_bwd(x2d, tgt2d, o, p, pm, conv_w, ga, gc, gmat, w_out, gf, nb, s, tm):
    nt = s // tm
    r = nb * s
    prev_idx = lambda i: jnp.maximum(i * (tm // 8) - 1, 0)

    def body(x_ref, t_ref, o_ref, za_ref, cb_ref, cc_ref, ch_ref, zc_ref, ccp_ref, chp_ref, mc_ref, mh_ref,
             cw_ref, ga_ref, gc_ref, gm_ref, w_ref, gf_ref,
             dh_ref, dy_ref, dw_ref, dgf_ref, loss_ref):
        i = pl.program_id(0)

        @pl.when(i == 0)
        def _():
            dw_ref[...] = jnp.zeros_like(dw_ref)
            dgf_ref[...] = jnp.zeros_like(dgf_ref)
            loss_ref[...] = jnp.zeros_like(loss_ref)

        ya = []
        for h in range(HEADS):
            y, _, _ = _attn_gate(o_ref[0, h], za_ref[:, VDIM * h:VDIM * (h + 1)],
                                 ga_ref[:, VDIM * h:VDIM * (h + 1)])
            ya.append(y)
        cc = cc_ref[...] * ch_ref[...]
        prev = jnp.where(i % nt == 0, mc_ref[8:16, :] * mh_ref[8:16, :], ccp_ref[...] * chp_ref[...])
        cc1, cc2 = _shift_rows(cc, prev, tm)
        yc = cb_ref[...] * (cw_ref[0:1, :] * cc2 + cw_ref[1:2, :] * cc1 + cw_ref[2:3, :] * cc)
        rg = lax.rsqrt(_group_mean(yc * yc, gm_ref[...]) + EPS)
        zc = zc_ref[...]
        yconv = yc * rg * gc_ref[...] * (zc * _sigmoid(zc))
        ycat = jnp.concatenate(ya + [yconv], axis=1).astype(BF16)
        h2 = x_ref[...] + _dot(ycat, w_ref[...])
        gfv = gf_ref[...]
        y, r2 = _rms(h2, gfv)
        e = y - t_ref[...]
        loss_ref[...] += 0.5 * jnp.sum(e * e) / D_MODEL
        dyv = e * (1.0 / D_MODEL)
        dh2, dgf = _rms_bwd(dyv, h2, r2, gfv)
        dgf_ref[...] += jnp.sum(dgf, axis=0, keepdims=True)
        dh_ref[...] = dh2
        dhb = dh2.astype(BF16)
        dy_ref[...] = _dot_nt(dhb, w_ref[...])
        dw_ref[...] += _dot_tn(ycat, dhb)

    row = lambda w, j: pl.BlockSpec((tm, w), lambda i: (i, j))
    pblk = lambda j: pl.BlockSpec((tm, 512), lambda i: (i, j))
    pprev = lambda j: pl.BlockSpec((8, 512), lambda i: (prev_idx(i), j))
    mblk = lambda j: pl.BlockSpec((N_META, 512), lambda i: (0, j))
    full = lambda a: pl.BlockSpec(a.shape, lambda i: (0,) * a.ndim)
    return pl.pallas_call(
        body, name="out_fwd_bwd", grid=(nb * nt,),
        in_specs=[row(D_MODEL, 0), row(D_MODEL, 0),
                  pl.BlockSpec((1, HEADS, tm, VDIM), lambda i: (i // nt, 0, i % nt, 0)),
                  pblk(BLK_ZA), pblk(BLK_CB), pblk(BLK_CC), pblk(BLK_CH), pblk(BLK_ZC),
                  pprev(BLK_CC), pprev(BLK_CH), mblk(BLK_CC), mblk(BLK_CH),
                  full(conv_w), full(ga), full(gc), full(gmat), full(w_out), full(gf)],
        out_specs=[row(D_MODEL, 0), row(D_MODEL, 0),
                   pl.BlockSpec((D_MODEL, D_MODEL), lambda i: (0, 0)),
                   pl.BlockSpec((1, D_MODEL), lambda i: (0, 0)),
                   pl.BlockSpec((1, 128), lambda i: (0, 0))],
        out_shape=[jax.ShapeDtypeStruct((r, D_MODEL), F32), jax.ShapeDtypeStruct((r, D_MODEL), F32),
                   jax.ShapeDtypeStruct((D_MODEL, D_MODEL), F32), jax.ShapeDtypeStruct((1, D_MODEL), F32),
                   jax.ShapeDtypeStruct((1, 128), F32)],
        compiler_params=_cparams("arbitrary"),
    )(x2d, tgt2d, o, p, p, p, p, p, p, p, pm, pm, conv_w, ga, gc, gmat, w_out, gf)


def _gate_bwd(dycat, o, p, pm, conv_w, ga, gc, gmat, nb, s, tm):
    nt = s // tm
    r = nb * s
    ext = tm + 8
    prev_idx = lambda i: jnp.maximum(i * (tm // 8) - 1, 0)
    next_idx = lambda i: jnp.minimum((i + 1) * (tm // 8), r // 8 - 1)

    def body(dya_ref, dyc_ref, dycn_ref, o_ref, za_ref, cb_ref, cbn_ref, cc_ref, ccp_ref, ccn_ref,
             ch_ref, chp_ref, chn_ref, zc_ref, zcn_ref, mc_ref, mh_ref, cw_ref, ga_ref, gc_ref, gm_ref,
             dpb_ref, do_ref, dl_ref, dccm_ref, dga_ref, dgc_ref, dcw_ref):
        i = pl.program_id(0)

        @pl.when(i == 0)
        def _():
            dga_ref[...] = jnp.zeros_like(dga_ref)
            dgc_ref[...] = jnp.zeros_like(dgc_ref)
            dcw_ref[...] = jnp.zeros_like(dcw_ref)

        dga = []
        for h in range(HEADS):
            hs = slice(VDIM * h, VDIM * (h + 1))
            oh, za, gah, dya = o_ref[0, h], za_ref[:, hs], ga_ref[:, hs], dya_ref[:, hs]
            sg = _sigmoid(za)
            on, ro = _rms(oh, gah)
            don = dya * (za * sg)
            dpb_ref[:, hs] = (dya * on * (sg * (1.0 + za * (1.0 - sg)))).astype(BF16)
            do, dg = _rms_bwd(don, oh, ro, gah)
            dga.append(jnp.sum(dg, axis=0, keepdims=True))
            dob = do.astype(BF16)
            do_ref[0, h] = dob
            dl_ref[0, h] = _row_of(jnp.sum(dob.astype(F32) * oh, axis=1, keepdims=True), tm)
        dga_ref[...] += jnp.concatenate(dga, axis=1)

        cat = lambda a, b: jnp.concatenate([a[...], b[...]], axis=0)
        cch = cat(cc_ref, ccn_ref)
        chh = cat(ch_ref, chn_ref)
        cb = cat(cb_ref, cbn_ref)
        zc = cat(zc_ref, zcn_ref)
        dy = cat(dyc_ref, dycn_ref)
        first = i % nt == 0
        last = i % nt == nt - 1
        cc = cch * chh
        prev = jnp.where(first, mc_ref[8:16, :] * mh_ref[8:16, :], ccp_ref[...] * chp_ref[...])
        cc1, cc2 = _shift_rows(cc, prev, ext)
        w0, w1, w2 = cw_ref[0:1, :], cw_ref[1:2, :], cw_ref[2:3, :]
        dw = w0 * cc2 + w1 * cc1 + w2 * cc
        yc = cb * dw
        rg = lax.rsqrt(_group_mean(yc * yc, gm_ref[...]) + EPS)
        ych = yc * rg
        gcv = gc_ref[...]
        sg = _sigmoid(zc)
        dycn = dy * (zc * sg)
        dzc = dy * (ych * gcv) * (sg * (1.0 + zc * (1.0 - sg)))
        dgc_ref[...] += jnp.sum((dycn * ych)[:tm], axis=0, keepdims=True)
        dycg = dycn * gcv
        dyc = rg * (dycg - ych * _group_mean(dycg * ych, gm_ref[...]))
        rid = lax.broadcasted_iota(jnp.int32, (ext, CONV_W), 0)
        ddw = jnp.where(jnp.logical_and(last, rid >= tm), 0.0, dyc * cb)
        dcb = dyc * dw
        dcc = w2 * ddw + w1 * pltpu.roll(ddw, ext - 1, 0) + w0 * pltpu.roll(ddw, ext - 2, 0)
        dpb_ref[:, 512:1024] = dcb[:tm].astype(BF16)
        dpb_ref[:, 1024:1536] = (dcc * chh)[:tm].astype(BF16)
        dpb_ref[:, 1536:2048] = (dcc * cch)[:tm].astype(BF16)
        dpb_ref[:, 2048:2560] = dzc[:tm].astype(BF16)
        rs = lambda a: jnp.sum(a[:tm], axis=0, keepdims=True)
        dcw_ref[0:1, :] += rs(ddw * cc2)
        dcw_ref[1:2, :] += rs(ddw * cc1)
        dcw_ref[2:3, :] += rs(ddw * cc)

        @pl.when(first)
        def _():
            d0, d1 = ddw[0:1, :], ddw[1:2, :]
            r8 = lax.broadcasted_iota(jnp.int32, (8, CONV_W), 0)
            dccm_ref[0] = jnp.where(r8 == 7, w1 * d0 + w0 * d1, jnp.where(r8 == 6, w0 * d0, 0.0))

    row = lambda j: pl.BlockSpec((tm, 512), lambda i: (i, j))
    prv = lambda j: pl.BlockSpec((8, 512), lambda i: (prev_idx(i), j))
    nxt = lambda j: pl.BlockSpec((8, 512), lambda i: (next_idx(i), j))
    mblk = lambda j: pl.BlockSpec((N_META, 512), lambda i: (0, j))
    full = lambda a: pl.BlockSpec(a.shape, lambda i: (0,) * a.ndim)
    hb = lambda w: pl.BlockSpec((1, HEADS, tm, w), lambda i: (i // nt, 0, i % nt, 0))
    acc = lambda rr: pl.BlockSpec((rr, 512), lambda i: (0, 0))
    return pl.pallas_call(
        body, name="gate_bwd", grid=(nb * nt,),
        in_specs=[row(0), row(1), nxt(1), hb(VDIM),
                  row(BLK_ZA), row(BLK_CB), nxt(BLK_CB), row(BLK_CC), prv(BLK_CC), nxt(BLK_CC),
                  row(BLK_CH), prv(BLK_CH), nxt(BLK_CH), row(BLK_ZC), nxt(BLK_ZC),
                  mblk(BLK_CC), mblk(BLK_CH), full(conv_w), full(ga), full(gc), full(gmat)],
        out_specs=[pl.BlockSpec((tm, 2560), lambda i: (i, 0)), hb(VDIM),
                   pl.BlockSpec((1, HEADS, 1, tm), lambda i: (i // nt, 0, 0, i % nt)),
                   pl.BlockSpec((1, 8, 512), lambda i: (i // nt, 0, 0)),
                   acc(1), acc(1), acc(8)],
        out_shape=[jax.ShapeDtypeStruct((r, 2560), BF16), jax.ShapeDtypeStruct((nb, HEADS, s, VDIM), BF16),
                   jax.ShapeDtypeStruct((nb, HEADS, 1, s), F32), jax.ShapeDtypeStruct((nb, 8, 512), F32),
                   jax.ShapeDtypeStruct((1, 512), F32), jax.ShapeDtypeStruct((1, 512), F32),
                   jax.ShapeDtypeStruct((8, 512), F32)],
        compiler_params=_cparams("arbitrary"),
    )(dycat, dycat, dycat, o, p, p, p, p, p, p, p, p, p, p, p, pm, pm, conv_w, ga, gc, gmat)


class _StagedReduce:
    LOC, PRE_S, PRE_R, ICI_S, ICI_R, POST_S, POST_R, OUT, N_SEM = 0, 1, 2, 3, 6, 9, 10, 11, 12

    def __init__(self, shard_shape):
        self.half = (shard_shape[0] // 2, shard_shape[1])

    def scratch(self):
        h = self.half
        return [pltpu.VMEM((4,) + h, F32), pltpu.VMEM((4,) + h, F32), pltpu.VMEM((4,) + h, BF16),
                pltpu.VMEM((3,) + h, BF16), pltpu.VMEM(h, F32), pltpu.SemaphoreType.DMA((self.N_SEM,))]

    def run(self, stage, pin, gout, scr):
        own, sib, wire, rbuf, fin, sems = scr
        r2 = self.half[0]
        x, y, c = lax.axis_index("x"), lax.axis_index("y"), lax.axis_index("c")
        mine = 2 * x + y
        sibling = (x, y, 1 - c)
        chips = [(1 - x, y), (x, 1 - y), (1 - x, 1 - y)]
        rows = lambda half: pl.ds(pl.multiple_of(half * r2, r2), r2)
        mesh = pl.DeviceIdType.MESH

        loc = pltpu.make_async_copy(pin.at[:, rows(c), :], own, sems.at[self.LOC])
        pre = pltpu.make_async_remote_copy(
            src_ref=pin.at[:, rows(1 - c), :], dst_ref=sib, send_sem=sems.at[self.PRE_S],
            recv_sem=sems.at[self.PRE_R], device_id=sibling, device_id_type=mesh)

        def ici(j):
            px, py = chips[j]
            return pltpu.make_async_remote_copy(
                src_ref=wire.at[2 * px + py], dst_ref=rbuf.at[j], send_sem=sems.at[self.ICI_S + j],
                recv_sem=sems.at[self.ICI_R + j], device_id=(px, py, c), device_id_type=mesh)

        def post(half):
            return pltpu.make_async_remote_copy(
                src_ref=fin, dst_ref=gout.at[rows(half), :], send_sem=sems.at[self.POST_S],
                recv_sem=sems.at[self.POST_R], device_id=sibling, device_id_type=mesh)

        keep = pltpu.make_async_copy(fin, gout.at[rows(c), :], sems.at[self.OUT])
        if stage == 0:
            loc.start()
            pre.start()
        elif stage == 1:
            loc.wait()
            pre.wait_recv()
            for blk in range(4):
                tot = own[blk] + sib[blk]
                own[blk] = tot
                wire[blk] = tot.astype(BF16)
            for j in range(3):
                ici(j).start()
        elif stage == 2:
            for j in range(3):
                ici(j).wait_recv()
            tot = own[mine]
            for j in range(3):
                tot = tot + rbuf[j].astype(F32)
            fin[...] = tot
            post(c).start()
            keep.start()
        else:
            post(1 - c).wait_recv()
            pre.wait_send()
            for j in range(3):
                ici(j).wait_send()
            post(c).wait_send()
            keep.wait()


def _attn_bwd(q, k, v, do, lse, delta, km, vm, p_out, nb, s, t):
    n = s // t
    red = _StagedReduce(p_out.shape[1:])
    n_steps = HEADS * nb
    assert n_steps >= 4

    def body(q_ref, k_ref, v_ref, do_ref, lse_ref, dl_ref, km_ref, vm_ref, pout_ref,
             dq_ref, dk_ref, dv_ref, dkm_ref, dvm_ref, gout_ref, p_scr, ds_scr, dq_acc, *red_scr):
        b = pl.program_id(1)
        step = pl.program_id(0) * nb + b
        for stage, at in enumerate((0, 1, n_steps - 2, n_steps - 1)):
            @pl.when(step == at)
            def _(stage=stage):
                red.run(stage, pout_ref, gout_ref, red_scr)

        @pl.when(b == 0)
        def _():
            dkm_ref[...] = jnp.zeros_like(dkm_ref)
            dvm_ref[...] = jnp.zeros_like(dvm_ref)

        kr = lax.broadcasted_iota(jnp.int32, (t, t), 0)
        qc = lax.broadcasted_iota(jnp.int32, (t, t), 1)
        km_v, vm_v = km_ref[0, 0], vm_ref[0, 0]
        ptm = jnp.exp(_dot_nt(km_v, q_ref[0, 0]) - lse_ref[0, 0])
        dstm = (ptm * (_dot_nt(vm_v, do_ref[0, 0]) - dl_ref[0, 0])).astype(BF16)
        dkm_ref[0] += _dot(dstm, q_ref[0, 0])
        dvm_ref[0] += _dot(ptm.astype(BF16), do_ref[0, 0])
        dq_acc[...] = _dot_tn(dstm, km_v)
        for j in range(n):
            slot = j % 2
            kj = k_ref[0, 0, j * t:(j + 1) * t, :]
            vj = v_ref[0, 0, j * t:(j + 1) * t, :]
            for i in range(j, n):
                cs = slice(i * t, (i + 1) * t)
                qi = q_ref[0, 0, cs, :]
                doi = do_ref[0, 0, cs, :]
                st = _dot_nt(kj, qi)
                if i == j:
                    st = jnp.where(kr <= qc, st, NEG_INF)
                pt = jnp.exp(st - lse_ref[0, 0, :, cs])
                dst = (pt * (_dot_nt(vj, doi) - dl_ref[0, 0, :, cs])).astype(BF16)
                p_scr[slot, :, cs] = pt.astype(BF16)
                ds_scr[slot, :, cs] = dst
                dq_acc[cs, :] += _dot_tn(dst, kj)
            dv_ref[0, 0, j * t:(j + 1) * t, :] = _dot(p_scr[slot, :, j * t:s], do_ref[0, 0, j * t:s, :]).astype(BF16)
            dk_ref[0, 0, j * t:(j + 1) * t, :] = _dot(ds_scr[slot, :, j * t:s], q_ref[0, 0, j * t:s, :]).astype(BF16)
        dq_ref[0, 0] = dq_acc[...].astype(BF16)

    big = lambda w: pl.BlockSpec((1, 1, s, w), lambda h, b: (b, h, 0, 0))
    rowv = pl.BlockSpec((1, 1, 1, s), lambda h, b: (b, h, 0, 0))
    mk = lambda w: pl.BlockSpec((1, 1, N_META, w), lambda h, b: (0, h, 0, 0))
    mo = lambda w: pl.BlockSpec((1, N_META, w), lambda h, b: (h, 0, 0))
    return pl.pallas_call(
        body, name="attn_bwd", grid=(HEADS, nb),
        in_specs=[big(QK_PAD), big(QK_PAD), big(VDIM), big(VDIM), rowv, rowv, mk(QK_PAD), mk(VDIM),
                  pl.BlockSpec(memory_space=pl.ANY)],
        out_specs=[big(QK_PAD), big(QK_PAD), big(VDIM), mo(QK_PAD), mo(VDIM), pl.BlockSpec(memory_space=pl.ANY)],
        out_shape=[jax.ShapeDtypeStruct((nb, HEADS, s, QK_PAD), BF16),
                   jax.ShapeDtypeStruct((nb, HEADS, s, QK_PAD), BF16),
                   jax.ShapeDtypeStruct((nb, HEADS, s, VDIM), BF16),
                   jax.ShapeDtypeStruct((HEADS, N_META, QK_PAD), F32),
                   jax.ShapeDtypeStruct((HEADS, N_META, VDIM), F32),
                   jax.ShapeDtypeStruct(p_out.shape[1:], F32)],
        scratch_shapes=[pltpu.VMEM((2, t, s), BF16), pltpu.VMEM((2, t, s), BF16), pltpu.VMEM((s, QK_PAD), F32)]
        + red.scratch(),
        compiler_params=_cparams("arbitrary", "arbitrary"),
    )(q, k, v, do, lse, delta, km, vm, p_out)


def _up_bwd(dq, dk, dv, dkm, dvm, p, pm, tabs, tabs_m, wq_p, wkv_p, gq, gkv, nb, s, tm):
    nt = s // tm
    n = nb * nt
    c_t, sa_t, sb_t = tabs
    cm_t, sam_t, sbm_t = tabs_m

    def kv_path(dkh, dvh, pa, c, sa, sb, wkv, gkvv):
        dkpe = dkh[0][:, NOPE:]
        for h in range(1, HEADS):
            dkpe = dkpe + dkh[h][:, NOPE:]
        dkr = _rope_bwd(dkpe, c, sa, sb)
        dkv = jnp.concatenate([d[:, :NOPE] for d in dkh] + list(dvh), axis=1).astype(BF16)
        ckv = pa[:, Q_RANK:Q_RANK + KV_RANK]
        kvn, rkv = _rms(ckv, gkvv)
        dckv, dg = _rms_bwd(_dot(dkv, wkv), ckv, rkv, gkvv)
        return dckv, dkr, kvn.astype(BF16), dkv, jnp.sum(dg, axis=0, keepdims=True)

    def body(dq_ref, dk_ref, dv_ref, pa_ref, c_ref, sa_ref, sb_ref,
             dkm_ref, dvm_ref, pam_ref, cm_ref, sam_ref, sbm_ref,
             wq_ref, wkv_ref, gq_ref, gkv_ref,
             dpa_ref, dpam_ref, pq_ref, pkv_ref, dgq_ref, dgkv_ref, dwq_ref, dwkv_ref):
        i = pl.program_id(0)

        @pl.when(i == 0)
        def _():
            dwq_ref[...] = jnp.zeros_like(dwq_ref)
            dwkv_ref[...] = jnp.zeros_like(dwkv_ref)
            dgq_ref[...] = jnp.zeros_like(dgq_ref)
            dgkv_ref[...] = jnp.zeros_like(dgkv_ref)

        @pl.when(i < n)
        def _():
            c, sa, sb = c_ref[...], sa_ref[...], sb_ref[...]
            pa = pa_ref[...]
            parts = []
            for h in range(HEADS):
                dqh = dq_ref[0, h].astype(F32) * ATTN_SCALE
                parts += [dqh[:, :NOPE], _rope_bwd(dqh[:, NOPE:], c, sa, sb)]
            dql = jnp.concatenate(parts, axis=1).astype(BF16)
            cq = pa[:, 0:Q_RANK]
            gqv = gq_ref[...]
            qn, rq = _rms(cq, gqv)
            dwq_ref[...] += _dot_tn(dql, qn.astype(BF16))
            dcq, dg = _rms_bwd(_dot(dql, wq_ref[...]), cq, rq, gqv)
            dgq_ref[...] += jnp.sum(dg, axis=0, keepdims=True)
            dckv, dkr, kvn, dkv, dgk = kv_path([dk_ref[0, h].astype(F32) for h in range(HEADS)],
                                               [dv_ref[0, h].astype(F32) for h in range(HEADS)],
                                               pa, c, sa, sb, wkv_ref[...], gkv_ref[...])
            dwkv_ref[...] += _dot_tn(dkv, kvn)
            dgkv_ref[...] += dgk
            dpa_ref[...] = jnp.concatenate([dcq, dckv, dkr], axis=1).astype(BF16)

        @pl.when(i == n)
        def _():
            dckv, dkr, kvn, dkv, dgk = kv_path([dkm_ref[h] for h in range(HEADS)],
                                               [dvm_ref[h] for h in range(HEADS)],
                                               pam_ref[...], cm_ref[...], sam_ref[...], sbm_ref[...],
                                               wkv_ref[...], gkv_ref[...])
            dwkv_ref[...] += _dot_tn(dkv, kvn)
            dgkv_ref[...] += dgk
            dpam_ref[...] = jnp.concatenate([jnp.zeros((N_META, Q_RANK), F32), dckv, dkr], axis=1)
            for h in range(HEADS):
                pq_ref[h] = dwq_ref[QK_PAD * h:QK_PAD * h + NOPE + ROPE, :]
                pkv_ref[h, 0:NOPE, :] = dwkv_ref[NOPE * h:NOPE * (h + 1), :]
                pkv_ref[h, NOPE:NOPE + VDIM, :] = dwkv_ref[512 + VDIM * h:512 + VDIM * (h + 1), :]

    cl = lambda i: jnp.minimum(i, n - 1)
    hb = lambda w: pl.BlockSpec((1, HEADS, tm, w), lambda i: (cl(i) // nt, 0, cl(i) % nt, 0))
    tab = pl.BlockSpec((tm, 128), lambda i: (cl(i) % nt, 0))
    full = lambda a: pl.BlockSpec(a.shape, lambda i: (0,) * a.ndim)
    const = lambda shape: pl.BlockSpec(shape, lambda i: (0,) * len(shape))
    return pl.pallas_call(
        body, name="up_bwd", grid=(n + 1,),
        in_specs=[hb(QK_PAD), hb(QK_PAD), hb(VDIM), pl.BlockSpec((tm, 512), lambda i: (cl(i), 0)), tab, tab, tab,
                  full(dkm), full(dvm), pl.BlockSpec((N_META, 512), lambda i: (0, 0)),
                  full(cm_t), full(sam_t), full(sbm_t), full(wq_p), full(wkv_p), full(gq), full(gkv)],
        out_specs=[pl.BlockSpec((tm, 512), lambda i: (cl(i), 0)), const((N_META, 512)),
                   const((HEADS, NOPE + ROPE, Q_RANK)), const((HEADS, NOPE + VDIM, KV_RANK)),
                   const((1, Q_RANK)), const((1, KV_RANK))],
        out_shape=[jax.ShapeDtypeStruct((nb * s, 512), BF16), jax.ShapeDtypeStruct((N_META, 512), F32),
                   jax.ShapeDtypeStruct((HEADS, NOPE + ROPE, Q_RANK), F32),
                   jax.ShapeDtypeStruct((HEADS, NOPE + VDIM, KV_RANK), F32),
                   jax.ShapeDtypeStruct((1, Q_RANK), F32), jax.ShapeDtypeStruct((1, KV_RANK), F32)],
        scratch_shapes=[pltpu.VMEM((HEADS * QK_PAD, Q_RANK), F32), pltpu.VMEM((1024, KV_RANK), F32)],
        compiler_params=_cparams("arbitrary"),
    )(dq, dk, dv, p, c_t, sa_t, sb_t, dkm, dvm, pm, cm_t, sam_t, sbm_t, wq_p, wkv_p, gq, gkv)


def _in_bwd(x2d, dh2, dpa, dpb, meta, dpam, dccm, pm, w_in_p, norm_g, nb, s, tm):
    nt = s // tm
    n = nb * nt

    def body(x_ref, dh_ref, dpa_ref, dpb_ref, mt_ref, dpam_ref, dccm_ref, mc_ref, mh_ref, w_ref, g_ref,
             gx_ref, gm_ref, dw_hbm, dg_ref, acc_ref, sems):
        i = pl.program_id(0)

        @pl.when(i == 0)
        def _():
            acc_ref[...] = jnp.zeros_like(acc_ref)
            dg_ref[...] = jnp.zeros_like(dg_ref)

        def rows(x, dp, dres):
            g = g_ref[...]
            u, r1 = _rms(x, g)
            dpb16 = dp.astype(BF16)
            acc_ref[...] += _dot_tn(dpb16, u.astype(BF16))
            dx, dg = _rms_bwd(_dot(dpb16, w_ref[...]), x, r1, g)
            dg_ref[...] += jnp.sum(dg, axis=0, keepdims=True)
            return dx if dres is None else dx + dres

        @pl.when(i < n)
        def _():
            dp = jnp.concatenate([dpa_ref[...], dpb_ref[...]], axis=1)
            gx_ref[...] = rows(x_ref[...], dp, dh_ref[...])

        @pl.when(i == n)
        def _():
            dcc = dccm_ref[0]
            for b in range(1, nb):
                dcc = dcc + dccm_ref[b]
            z8 = jnp.zeros((8, CONV_W), F32)
            dc = jnp.concatenate([z8, dcc * mh_ref[8:16, :]], axis=0)
            dh = jnp.concatenate([z8, dcc * mc_ref[8:16, :]], axis=0)
            z = jnp.zeros((N_META, CONV_W), F32)
            dp = jnp.concatenate([dpam_ref[...], z, z, dc, dh, z], axis=1)
            gm_ref[...] = rows(mt_ref[...], dp, None)
            per = IN_DIM // 4
            cps = [pltpu.make_async_copy(acc_ref.at[0:448], dw_hbm.at[0, 0:448], sems.at[0]),
                   pltpu.make_async_copy(acc_ref.at[512:per + 64], dw_hbm.at[0, 448:per], sems.at[1])]
            for qq in range(1, 4):
                cps.append(pltpu.make_async_copy(acc_ref.at[per * qq + 64:per * (qq + 1) + 64], dw_hbm.at[qq],
                                                 sems.at[qq + 1]))
            for cp in cps:
                cp.start()
            for cp in cps:
                cp.wait()

    cl = lambda i: jnp.minimum(i, n - 1)
    row = lambda w: pl.BlockSpec((tm, w), lambda i: (cl(i), 0))
    full = lambda a: pl.BlockSpec(a.shape, lambda i: (0,) * a.ndim)
    mblk = lambda j: pl.BlockSpec((N_META, 512), lambda i: (0, j))
    return pl.pallas_call(
        body, name="in_bwd", grid=(n + 1,),
        in_specs=[row(D_MODEL), row(D_MODEL), row(512), row(2560), full(meta), full(dpam), full(dccm),
                  mblk(BLK_CC), mblk(BLK_CH), full(w_in_p), full(norm_g)],
        out_specs=[row(D_MODEL), pl.BlockSpec((N_META, D_MODEL), lambda i: (0, 0)),
                   pl.BlockSpec(memory_space=pl.ANY), pl.BlockSpec((1, D_MODEL), lambda i: (0, 0))],
        out_shape=[jax.ShapeDtypeStruct((nb * s, D_MODEL), F32), jax.ShapeDtypeStruct((N_META, D_MODEL), F32),
                   jax.ShapeDtypeStruct((4, IN_DIM // 4, D_MODEL), F32), jax.ShapeDtypeStruct((1, D_MODEL), F32)],
        scratch_shapes=[pltpu.VMEM((IN_PAD, D_MODEL), F32), pltpu.SemaphoreType.DMA((5,))],
        compiler_params=_cparams("arbitrary"),
    )(x2d, dh2, dpa, dpb, meta, dpam, dccm, pm, pm, w_in_p, norm_g)


def _gather_weights(split, pieces, out_rows, whole, zero_fills):
    ns, nw, nz = len(split), len(whole), len(zero_fills)
    flat = [(a, pc) for a in range(ns) for pc in pieces[a]]
    nk = len(flat)

    def body(*refs):
        ins, wins, zins = refs[:ns], refs[ns:ns + nw], refs[ns + nw:ns + nw + nz]
        outs, wouts = refs[ns + nw + nz:2 * ns + nw + nz], refs[2 * ns + nw + nz:2 * (ns + nw) + nz]
        send_sems, recv_sems, fwd_send, fwd_recv, loc_sems, w_send, w_recv, w_loc, z_sems = refs[2 * (ns + nw) + nz:]
        x, y, c = lax.axis_index("x"), lax.axis_index("y"), lax.axis_index("c")
        mine = 2 * x + y
        chips = [(1 - x, y), (x, 1 - y), (1 - x, 1 - y)]
        chip_of = [2 * px + py for px, py in chips]

        def src(k):
            a, (s0, nr, _, _, _, _) = flat[k]
            return ins[a].at[s0:s0 + nr]

        def dst(k, q):
            a, (_, nr, per, first, rest, _) = flat[k]
            row = per * q + first + (rest - first) * jnp.minimum(q, 1)
            return outs[a].at[pl.ds(pl.multiple_of(row, 16), nr)]

        def ici(k, j, q):
            px, py = chips[j]
            return pltpu.make_async_remote_copy(
                src_ref=src(k), dst_ref=dst(k, q), send_sem=send_sems.at[k, j], recv_sem=recv_sems.at[k, j],
                device_id=(px, py, c), device_id_type=pl.DeviceIdType.MESH)

        def fwd(k, j):
            ref = dst(k, chip_of[j])
            return pltpu.make_async_remote_copy(
                src_ref=ref, dst_ref=ref, send_sem=fwd_send.at[k, j], recv_sem=fwd_recv.at[k, j],
                device_id=(x, y, 1 - c), device_id_type=pl.DeviceIdType.MESH)

        def wcopy(b, j, q):
            px, py = chips[j]
            return pltpu.make_async_remote_copy(
                src_ref=wins[b], dst_ref=wouts[b].at[q], send_sem=w_send.at[b, j], recv_sem=w_recv.at[b, j],
                device_id=(px, py, c), device_id_type=pl.DeviceIdType.MESH)

        local = [pltpu.make_async_copy(src(k), dst(k, mine), loc_sems.at[k]) for k in range(nk)]
        local += [pltpu.make_async_copy(wins[b], wouts[b].at[mine], w_loc.at[b]) for b in range(nw)]
        for z, (a, _, row0) in enumerate(zero_fills):
            local.append(pltpu.make_async_copy(zins[z], outs[a].at[row0:row0 + zins[z].shape[0]], z_sems.at[z]))
        wsends = [wcopy(b, j, mine) for b in range(nw) for j in range(3)]
        for cp in local + wsends:
            cp.start()

        for half in (0, 1):
            @pl.when(c == half)
            def _(half=half):
                my_k = [k for k in range(nk) if flat[k][1][5] == half]
                other_k = [k for k in range(nk) if flat[k][1][5] != half]
                sends = [ici(k, j, mine) for k in my_k for j in range(3)]
                for cp in sends:
                    cp.start()
                passed = []
                for k in my_k:
                    for j in range(3):
                        ici(k, j, chip_of[j]).wait_recv()
                        cp = fwd(k, j)
                        cp.start()
                        passed.append(cp)
                for k in other_k:
                    for j in range(3):
                        fwd(k, j).wait_recv()
                for cp in sends + passed:
                    cp.wait_send()

        for b in range(nw):
            for j in range(3):
                wcopy(b, j, chip_of[j]).wait_recv()
        for cp in wsends:
            cp.wait_send()
        for cp in local:
            cp.wait()

    hbm = pl.BlockSpec(memory_space=pl.ANY)
    dma = pltpu.SemaphoreType.DMA
    zeros = [z for _, z, _ in zero_fills]
    return pl.pallas_call(
        body, name="gather_weights",
        in_specs=[hbm] * (ns + nw + nz), out_specs=[hbm] * (ns + nw),
        out_shape=([jax.ShapeDtypeStruct((out_rows[a], split[a].shape[1]), split[a].dtype) for a in range(ns)]
                   + [jax.ShapeDtypeStruct((4,) + w.shape, w.dtype) for w in whole]),
        scratch_shapes=[dma((nk, 3)), dma((nk, 3)), dma((nk, 3)), dma((nk, 3)), dma((nk,)),
                        dma((nw, 3)), dma((nw, 3)), dma((nw,)), dma((nz,))],
        compiler_params=pltpu.CompilerParams(vmem_limit_bytes=VMEM_LIMIT),
    )(*split, *whole, *zeros)


def _reduce_grads(parts, small):
    n = len(parts)
    shapes = [a.shape[1:] for a in parts]
    halves = [(sh[0] // 2, sh[1]) for sh in shapes]

    def body(*refs):
        pin, sm_in = refs[:n], refs[n]
        gout, sm_out = refs[n + 1:2 * n + 1], refs[2 * n + 1]
        scr = refs[2 * n + 2:]
        own, sib, wire, rbuf = scr[:n], scr[n:2 * n], scr[2 * n:3 * n], scr[3 * n:4 * n]
        (sbuf, send_sems, recv_sems, loc_sems, pre_send, pre_recv, post_send, post_recv,
         sm_send, sm_recv) = scr[4 * n:]
        x, y, c = lax.axis_index("x"), lax.axis_index("y"), lax.axis_index("c")
        mine = 2 * x + y
        me = 4 * x + 2 * y + c
        sibling = (x, y, 1 - c)
        chips = [(1 - x, y), (x, 1 - y), (1 - x, 1 - y)]

        def rows(a, half):
            r2 = halves[a][0]
            return pl.ds(pl.multiple_of(half * r2, r2), r2)

        def pre(a):
            return pltpu.make_async_remote_copy(
                src_ref=pin[a].at[:, rows(a, 1 - c), :], dst_ref=sib[a], send_sem=pre_send.at[a],
                recv_sem=pre_recv.at[a], device_id=sibling, device_id_type=pl.DeviceIdType.MESH)

        def ici(a, j):
            px, py = chips[j]
            return pltpu.make_async_remote_copy(
                src_ref=wire[a].at[2 * px + py], dst_ref=rbuf[a].at[j], send_sem=send_sems.at[a, j],
                recv_sem=recv_sems.at[a, j], device_id=(px, py, c), device_id_type=pl.DeviceIdType.MESH)

        def post(a, half):
            ref = gout[a].at[rows(a, half), :]
            return pltpu.make_async_remote_copy(
                src_ref=ref, dst_ref=ref, send_sem=post_send.at[a], recv_sem=post_recv.at[a],
                device_id=sibling, device_id_type=pl.DeviceIdType.MESH)

        def small_copy(kk):
            peer = (x ^ (kk >> 2), y ^ ((kk >> 1) & 1), c ^ (kk & 1))
            return pltpu.make_async_remote_copy(
                src_ref=sm_in, dst_ref=sbuf.at[kk], send_sem=sm_send.at[kk - 1], recv_sem=sm_recv.at[kk - 1],
                device_id=peer, device_id_type=pl.DeviceIdType.MESH)

        local = [pltpu.make_async_copy(pin[a].at[:, rows(a, c), :], own[a], loc_sems.at[a]) for a in range(n)]
        pres = [pre(a) for a in range(n)]
        smalls = [small_copy(kk) for kk in range(1, 8)]
        for cp in local + pres + smalls:
            cp.start()
        sbuf[0] = sm_in[...]
        sends = []
        for a in range(n):
            local[a].wait()
            pres[a].wait_recv()
            for blk in range(4):
                tot = own[a][blk] + sib[a][blk]
                own[a][blk] = tot
                wire[a][blk] = tot.astype(BF16)
            for j in range(3):
                cp = ici(a, j)
                cp.start()
                sends.append(cp)
        for cp in smalls:
            cp.wait_recv()
        total = sbuf[me]
        for d in range(1, 8):
            total = total + sbuf[me ^ d]
        sm_out[...] = total
        posts = []
        for a in range(n):
            for j in range(3):
                ici(a, j).wait_recv()
            fin = own[a][mine]
            for j in range(3):
                fin = fin + rbuf[a][j].astype(F32)
            gout[a][rows(a, c), :] = fin
            cp = post(a, c)
            cp.start()
            posts.append(cp)
        for a in range(n):
            post(a, 1 - c).wait_recv()
        for cp in pres + sends + smalls + posts:
            cp.wait_send()

    hbm = pl.BlockSpec(memory_space=pl.ANY)
    vmem = pl.BlockSpec(memory_space=pltpu.VMEM)
    dma = pltpu.SemaphoreType.DMA
    return pl.pallas_call(
        body, name="reduce_grads",
        in_specs=[hbm] * n + [vmem], out_specs=[vmem] * (n + 1),
        out_shape=[jax.ShapeDtypeStruct(sh, F32) for sh in shapes] + [jax.ShapeDtypeStruct(small.shape, F32)],
        scratch_shapes=([pltpu.VMEM((4,) + hs, F32) for hs in halves] + [pltpu.VMEM((4,) + hs, F32) for hs in halves]
                        + [pltpu.VMEM((4,) + hs, BF16) for hs in halves]
                        + [pltpu.VMEM((3,) + hs, BF16) for hs in halves]
                        + [pltpu.VMEM((8,) + small.shape, F32), dma((n, 3)), dma((n, 3)), dma((n,)),
                           dma((n,)), dma((n,)), dma((n,)), dma((n,)), dma((7,)), dma((7,))]),
        compiler_params=pltpu.CompilerParams(vmem_limit_bytes=VMEM_LIMIT),
    )(*parts, small)


def _adamw(w, g, m, v, name):
    shape = w.shape
    w2, g2, m2, v2 = (a.reshape((-1, shape[-1])) for a in (w, g, m, v))

    def body(w_ref, g_ref, m_ref, v_ref, d_ref, nm_ref, nv_ref):
        gv = g_ref[...]
        nm = ADAM_B1 * m_ref[...] + (1.0 - ADAM_B1) * gv
        nv = ADAM_B2 * v_ref[...] + (1.0 - ADAM_B2) * (gv * gv)
        m_hat = nm / (1.0 - ADAM_B1 ** ADAM_STEP)
        v_hat = nv / (1.0 - ADAM_B2 ** ADAM_STEP)
        d_ref[...] = -ADAM_LR * (m_hat / (jnp.sqrt(v_hat) + ADAM_EPS) + ADAM_WD * w_ref[...])
        nm_ref[...] = nm
        nv_ref[...] = nv

    rows, cols = w2.shape
    nblk = cols // 256 if cols % 256 == 0 and rows >= 64 else 1
    blk = pl.BlockSpec((rows, cols // nblk), lambda j: (0, j))
    out = pl.pallas_call(
        body, name=name, grid=(nblk,), in_specs=[blk] * 4, out_specs=[blk] * 3,
        out_shape=[jax.ShapeDtypeStruct(w2.shape, F32)] * 3,
        compiler_params=_cparams("parallel"),
    )(w2, g2, m2, v2)
    return tuple(a.reshape(shape) for a in out)


def kernel(x, meta_tokens, norm_g, w_in, q_norm_g, w_q_up, kv_norm_g, w_kv_up, conv_w, attn_out_g, conv_out_g, w_out, final_norm_g, loss_target, m_meta_tokens, m_norm_g, m_w_in, m_q_norm_g, m_w_q_up, m_kv_norm_g, m_w_kv_up, m_conv_w, m_attn_out_g, m_conv_out_g, m_w_out, m_final_norm_g, v_meta_tokens, v_norm_g, v_w_in, v_q_norm_g, v_w_q_up, v_kv_norm_g, v_w_kv_up, v_conv_w, v_attn_out_g, v_conv_out_g, v_w_out, v_final_norm_g):
    nb, s, _ = x.shape
    tm = min(ROW_TILE, s)
    ta = min(ATTN_TILE, s)
    assert s % tm == 0 and s % ta == 0 and tm % 16 == 0
    r = nb * s

    tr = lambda a: jnp.transpose(a[0])
    w_in_p, wq_p, wkv_p, g_cw, g_meta = _gather_weights(
        [tr(w_in).astype(BF16), tr(w_q_up).astype(BF16), tr(w_kv_up).astype(BF16)],
        [W_IN_PIECES, W_Q_PIECES, W_KV_PIECES], [IN_PAD, HEADS * QK_PAD, 1024],
        [conv_w[0], meta_tokens],
        [(0, jnp.zeros((64, D_MODEL), BF16), 448)]
        + [(1, jnp.zeros((64, Q_RANK), BF16), QK_PAD * h + NOPE + ROPE) for h in range(HEADS)])
    conv_f = jnp.transpose(g_cw, (1, 0, 2)).reshape(3, CONV_W)
    meta_f = jnp.transpose(g_meta, (1, 0, 2)).reshape(N_META, D_MODEL)

    c_all, sa_all, sb_all = _rope_tables(N_META + s)
    tabs_m = (c_all[:N_META], sa_all[:N_META], sb_all[:N_META])
    tabs = (c_all[N_META:], sa_all[N_META:], sb_all[N_META:])
    gid = np.arange(CONV_W) // CONV_GROUP
    gmat = jnp.asarray(np.where(gid[:, None] == gid[None, :], 1.0 / CONV_GROUP, 0.0), BF16)
    ga, gc = attn_out_g, conv_out_g
    gf = final_norm_g.reshape(1, D_MODEL)

    x2d = x.reshape(r, D_MODEL)
    tgt2d = loss_target.reshape(r, D_MODEL)

    p, q, k, v, pm, km, vm, w_out_f = _fwd_proj(x2d, meta_f, tabs, tabs_m, norm_g, w_in_p, q_norm_g, wq_p,
                                                kv_norm_g, wkv_p, w_out[0].astype(BF16), nb, s, tm)
    o, lse = _attn_fwd(q, k, v, km, vm, nb, s, ta)
    dh2, dycat, dw_out, dgf, loss_acc = _out_fwd_bwd(x2d, tgt2d, o, p, pm, conv_f, ga, gc, gmat, w_out_f, gf,
                                                     nb, s, tm)
    dpb, do, delta, dccm, dga, dgc, dcw = _gate_bwd(dycat, o, p, pm, conv_f, ga, gc, gmat, nb, s, tm)
    p_out = dw_out.reshape(4, D_MODEL // 4, D_MODEL)
    dq, dk, dv, dkm, dvm, g_w_out = _attn_bwd(q, k, v, do, lse, delta, km, vm, p_out, nb, s, ta)
    dpa, dpam, p_q, p_kv, dgq, dgkv = _up_bwd(dq, dk, dv, dkm, dvm, p, pm, tabs, tabs_m, wq_p, wkv_p,
                                              q_norm_g, kv_norm_g, nb, s, tm)
    gx, gmeta, p_in, dng = _in_bwd(x2d, dh2, dpa, dpb, meta_f, dpam, dccm, pm, w_in_p, norm_g, nb, s, tm)

    flat =jnp.concatenate([dng.reshape(-1), dgq.reshape(-1), dgkv.reshape(-1), dga.reshape(-1), dgc.reshape(-1),
                            dgf.reshape(-1), dcw[:3].reshape(-1), gmeta.reshape(-1), loss_acc[0, 0:1]])
    n_small = flat.shape[0]
    rows_small = -(-n_small // 1024) * 8
    small = jnp.pad(flat, (0, rows_small * 128 - n_small)).reshape(rows_small, 128)
    g_w_in_t, g_w_q_t, g_w_kv_t, small_sum = _reduce_grads([p_in, p_q, p_kv], small)
    ssum = small_sum.reshape(-1)

    def take(off, n):
        return ssum[off:off + n], off + n

    off = 0
    g_norm, off = take(off, D_MODEL)
    g_qn, off = take(off, Q_RANK)
    g_kvn, off = take(off, KV_RANK)
    g_ga, off = take(off, CONV_W)
    g_gc, off = take(off, CONV_W)
    g_gf, off = take(off, D_MODEL)
    g_cw_all, off = take(off, 3 * CONV_W)
    g_meta_all, off = take(off, N_META * D_MODEL)
    loss = ssum[off]
    chip = 2 * lax.axis_index("x") + lax.axis_index("y")
    g_conv = lax.dynamic_slice(g_cw_all.reshape(3, CONV_W), (0, chip * 128), (3, 128))
    g_mt = lax.dynamic_slice(g_meta_all.reshape(N_META, D_MODEL), (0, chip * 256), (N_META, 256))

    grads = {
        "meta_tokens": g_mt, "norm_g": g_norm.reshape(1, -1), "w_in": g_w_in_t, "q_norm_g": g_qn.reshape(1, -1),
        "w_q_up": g_w_q_t, "kv_norm_g": g_kvn.reshape(1, -1), "w_kv_up": jnp.transpose(g_w_kv_t)[None],
        "conv_w": g_conv[None], "attn_out_g": g_ga.reshape(1, -1), "conv_out_g": g_gc.reshape(1, -1),
        "w_out": g_w_out[None], "final_norm_g": g_gf,
    }
    transposed = ("w_in", "w_q_up")
    weights = {
        "meta_tokens": (meta_tokens, m_meta_tokens, v_meta_tokens), "norm_g": (norm_g, m_norm_g, v_norm_g),
        "w_in": (w_in, m_w_in, v_w_in), "q_norm_g": (q_norm_g, m_q_norm_g, v_q_norm_g),
        "w_q_up": (w_q_up, m_w_q_up, v_w_q_up), "kv_norm_g": (kv_norm_g, m_kv_norm_g, v_kv_norm_g),
        "w_kv_up": (w_kv_up, m_w_kv_up, v_w_kv_up), "conv_w": (conv_w, m_conv_w, v_conv_w),
        "attn_out_g": (attn_out_g, m_attn_out_g, v_attn_out_g), "conv_out_g": (conv_out_g, m_conv_out_g, v_conv_out_g),
        "w_out": (w_out, m_w_out, v_w_out), "final_norm_g": (final_norm_g, m_final_norm_g, v_final_norm_g),
    }
    names = list(weights)
    deltas, new_m, new_v = [], [], []
    for nme in names:
        w_, m_, v_ = weights[nme]
        if nme in transposed:
            res = _adamw(tr(w_), grads[nme], tr(m_), tr(v_), "adamw_" + nme)
            g_, d_, nm_, nv_ = (jnp.transpose(a)[None] for a in (grads[nme],) + res)
        else:
            g_ = grads[nme].reshape(w_.shape)
            d_, nm_, nv_ = _adamw(w_, g_, m_, v_, "adamw_" + nme)
        grads[nme] = g_
        deltas.append(d_)
        new_m.append(nm_)
        new_v.append(nv_)

    grad_x = gx.reshape(nb, s, D_MODEL)
    return (loss, grad_x, *[grads[nme] for nme in names], *deltas, *new_m, *new_v)
```

```python
import functools

import jax
import jax.numpy as jnp
import numpy as np
from jax import lax
from jax.experimental import pallas as pl
from jax.experimental.pallas import tpu as pltpu

F32 = jnp.float32
BF16 = jnp.bfloat16

D_MODEL = 1024
N_META = 16
HEADS = 4
NOPE = 128
ROPE = 64
VDIM = 128
QK_PAD = 256
Q_RANK = 256
KV_RANK = 128
CONV_W = 512
CONV_GROUP = 64
ROPE_THETA = 10000.0
EPS = 1e-6
ATTN_SCALE = (NOPE + ROPE) ** -0.5
IN_DIM = 3008
IN_PAD = 3072
BLK_ZA, BLK_CB, BLK_CC, BLK_CH, BLK_ZC = 1, 2, 3, 4, 5
NEG_INF = -1e30

ADAM_LR = 0.001
ADAM_B1 = 0.9
ADAM_B2 = 0.999
ADAM_EPS = 1e-08
ADAM_WD = 0.01
ADAM_STEP = 10

ROW_TILE = 512
ATTN_TILE = 256
VMEM_LIMIT = 56 * 1024 * 1024

NT = (((1,), (1,)), ((), ()))
TN = (((0,), (0,)), ((), ()))


def _cparams(*sem):
    return pltpu.CompilerParams(dimension_semantics=sem, vmem_limit_bytes=VMEM_LIMIT)


def _dot(a, b):
    return jnp.dot(a, b, preferred_element_type=F32)


def _dot_nt(a, b):
    return lax.dot_general(a, b, NT, preferred_element_type=F32)


def _dot_tn(a, b):
    return lax.dot_general(a, b, TN, preferred_element_type=F32)


def _rms(x, g):
    r = lax.rsqrt(jnp.mean(x * x, axis=-1, keepdims=True) + EPS)
    return x * r * g, r


def _rms_bwd(dy, x, r, g):
    xh = x * r
    dyg = dy * g
    dx = r * (dyg - xh * jnp.mean(dyg * xh, axis=-1, keepdims=True))
    return dx, dy * xh


def _sigmoid(z):
    return 1.0 / (1.0 + jnp.exp(-z))


def _rope(b, c, sa, sb):
    return b * c + pltpu.roll(b, 96, 1) * sa + pltpu.roll(b, 32, 1) * sb


def _rope_bwd(d, c, sa, sb):
    return d * c + pltpu.roll(d * sa, 32, 1) + pltpu.roll(d * sb, 96, 1)


def _group_mean(x, gmat):
    hi = x.astype(BF16)
    lo = (x - hi.astype(F32)).astype(BF16)
    return _dot(hi, gmat) + _dot(lo, gmat)


def _row_of(col, rows):
    return jnp.transpose(jnp.broadcast_to(col, (rows, 128)))[0:1, :]


def _rope_tables(n_pos):
    half = ROPE // 2
    inv_freq = (np.float32(1.0) / (np.float32(ROPE_THETA) ** (np.arange(half, dtype=np.float32) / np.float32(half))))
    ang = np.arange(n_pos, dtype=np.float32)[:, None] * inv_freq.astype(np.float32)[None, :]
    cos, sin = np.cos(ang).astype(np.float32), np.sin(ang).astype(np.float32)
    z = np.zeros((n_pos, half), np.float32)
    c = np.concatenate([cos, cos, z, z], axis=1)
    sa = np.concatenate([-sin, z, z, z], axis=1)
    sb = np.concatenate([z, sin, z, z], axis=1)
    return jnp.asarray(c), jnp.asarray(sa), jnp.asarray(sb)


W_IN_PIECES = ((0, 384, 752, 0, 64, 0), (384, 64, 752, 384, 448, 1), (448, 304, 752, 512, 512, 1))
W_Q_PIECES = ((0, 96, 256, 0, 0, 0), (96, 96, 256, 96, 96, 1))
W_KV_PIECES = ((0, 128, 128, 0, 0, 0), (128, 128, 128, 512, 512, 1))
W_OUT_PIECES = ((0, 128, 256, 0, 0, 0), (128, 128, 256, 128, 128, 1))


class _StagedGather:
    def __init__(self, pieces):
        self.pieces = pieces

    def scratch(self):
        nk, dma = len(self.pieces), pltpu.SemaphoreType.DMA
        return [dma((nk, 3)), dma((nk, 3)), dma((nk, 3)), dma((nk, 3)), dma((nk,))]

    def run(self, stage, src_ref, out_ref, scr):
        send_sems, recv_sems, fwd_send, fwd_recv, loc_sems = scr
        pieces = self.pieces
        nk = len(pieces)
        x, y, c = lax.axis_index("x"), lax.axis_index("y"), lax.axis_index("c")
        mine = 2 * x + y
        chips = [(1 - x, y), (x, 1 - y), (1 - x, 1 - y)]
        chip_of = [2 * px + py for px, py in chips]
        mesh = pl.DeviceIdType.MESH

        def src(k):
            s0, nr = pieces[k][0], pieces[k][1]
            return src_ref.at[s0:s0 + nr]

        def dst(k, q):
            _, nr, per, first, rest, _ = pieces[k]
            row = per * q + first + (rest - first) * jnp.minimum(q, 1)
            return out_ref.at[pl.ds(pl.multiple_of(row, 16), nr)]

        def ici(k, j, q):
            px, py = chips[j]
            return pltpu.make_async_remote_copy(
                src_ref=src(k), dst_ref=dst(k, q), send_sem=send_sems.at[k, j], recv_sem=recv_sems.at[k, j],
                device_id=(px, py, c), device_id_type=mesh)

        def fwd(k, j):
            ref = dst(k, chip_of[j])
            return pltpu.make_async_remote_copy(
                src_ref=ref, dst_ref=ref, send_sem=fwd_send.at[k, j], recv_sem=fwd_recv.at[k, j],
                device_id=(x, y, 1 - c), device_id_type=mesh)

        local = [pltpu.make_async_copy(src(k), dst(k, mine), loc_sems.at[k]) for k in range(nk)]
        if stage == 0:
            for cp in local:
                cp.start()
        if stage == 2:
            for cp in local:
                cp.wait()
        for half in (0, 1):
            @pl.when(c == half)
            def _(half=half):
                my_k = [k for k in range(nk) if pieces[k][5] == half]
                other_k = [k for k in range(nk) if pieces[k][5] != half]
                for k in my_k:
                    for j in range(3):
                        if stage == 0:
                            ici(k, j, mine).start()
                        elif stage == 1:
                            ici(k, j, chip_of[j]).wait_recv()
                            fwd(k, j).start()
                        else:
                            ici(k, j, mine).wait_send()
                            fwd(k, j).wait_send()
                if stage == 2:
                    for k in other_k:
                        for j in range(3):
                            fwd(k, j).wait_recv()


def _fwd_proj(x2d, meta, tabs, tabs_m, norm_g, w_in_p, q_norm_g, wq_p, kv_norm_g, wkv_p, w_out_shard, nb, s, tm):
    nt = s // tm
    n = nb * nt
    n_steps = n + 1
    c_t, sa_t, sb_t = tabs
    cm_t, sam_t, sbm_t = tabs_m
    gat = _StagedGather(W_OUT_PIECES)
    assert n_steps >= 3

    def body(x_ref, c_ref, sa_ref, sb_ref, mt_ref, cm_ref, sam_ref, sbm_ref,
             g_ref, w_ref, gq_ref, wq_ref, gkv_ref, wkv_ref, wos_ref,
             p_ref, q_ref, k_ref, v_ref, pm_ref, km_ref, vm_ref, wo_ref, *gat_scr):
        i = pl.program_id(0)
        for stage, at in enumerate((0, n_steps - 2, n_steps - 1)):
            @pl.when(i == at)
            def _(stage=stage):
                gat.run(stage, wos_ref, wo_ref, gat_scr)

        def project(xv, c, sa, sb, p_out, q_out, k_out, v_out):
            u, _ = _rms(xv, g_ref[...])
            p = _dot_nt(u.astype(BF16), w_ref[...])
            p_out[...] = p
            qn, _ = _rms(p[:, 0:Q_RANK], gq_ref[...])
            q = _dot_nt(qn.astype(BF16), wq_ref[...])
            kvn, _ = _rms(p[:, Q_RANK:Q_RANK + KV_RANK], gkv_ref[...])
            kv = _dot_nt(kvn.astype(BF16), wkv_ref[...])
            kpe = _rope(p[:, 384:512], c, sa, sb)
            for h in range(HEADS):
                if q_out is not None:
                    pe = _rope(q[:, QK_PAD * h + NOPE:QK_PAD * (h + 1)], c, sa, sb)
                    qh = jnp.concatenate([q[:, QK_PAD * h:QK_PAD * h + NOPE], pe], axis=1)
                    q_out[0, h] = (qh * ATTN_SCALE).astype(BF16)
                k_out[0, h] = jnp.concatenate([kv[:, NOPE * h:NOPE * (h + 1)], kpe], axis=1).astype(BF16)
                v_out[0, h] = kv[:, 512 + VDIM * h:512 + VDIM * (h + 1)].astype(BF16)

        @pl.when(i < n)
        def _():
            project(x_ref[...], c_ref[...], sa_ref[...], sb_ref[...], p_ref, q_ref, k_ref, v_ref)

        @pl.when(i == n)
        def _():
            project(mt_ref[...], cm_ref[...], sam_ref[...], sbm_ref[...], pm_ref, None, km_ref, vm_ref)

    cl = lambda i: jnp.minimum(i, n - 1)
    full = lambda a: pl.BlockSpec(a.shape, lambda i: (0,) * a.ndim)
    const = lambda shape: pl.BlockSpec(shape, lambda i: (0,) * len(shape))
    tab = pl.BlockSpec((tm, 128), lambda i: (cl(i) % nt, 0))
    hb = lambda w: pl.BlockSpec((1, HEADS, tm, w), lambda i: (cl(i) // nt, 0, cl(i) % nt, 0))
    hbm = pl.BlockSpec(memory_space=pl.ANY)
    return pl.pallas_call(
        body, name="fwd_proj", grid=(n_steps,),
        in_specs=[pl.BlockSpec((tm, D_MODEL), lambda i: (cl(i), 0)), tab, tab, tab,
                  full(meta), full(cm_t), full(sam_t), full(sbm_t),
                  full(norm_g), full(w_in_p), full(q_norm_g), full(wq_p), full(kv_norm_g), full(wkv_p), hbm],
        out_specs=[pl.BlockSpec((tm, IN_PAD), lambda i: (cl(i), 0)), hb(QK_PAD), hb(QK_PAD), hb(VDIM),
                   const((N_META, IN_PAD)), const((1, HEADS, N_META, QK_PAD)), const((1, HEADS, N_META, VDIM)), hbm],
        out_shape=[jax.ShapeDtypeStruct((nb * s, IN_PAD), F32),
                   jax.ShapeDtypeStruct((nb, HEADS, s, QK_PAD), BF16),
                   jax.ShapeDtypeStruct((nb, HEADS, s, QK_PAD), BF16),
                   jax.ShapeDtypeStruct((nb, HEADS, s, VDIM), BF16),
                   jax.ShapeDtypeStruct((N_META, IN_PAD), F32),
                   jax.ShapeDtypeStruct((1, HEADS, N_META, QK_PAD), BF16),
                   jax.ShapeDtypeStruct((1, HEADS, N_META, VDIM), BF16),
                   jax.ShapeDtypeStruct((D_MODEL, D_MODEL), BF16)],
        scratch_shapes=gat.scratch(),
        compiler_params=_cparams("arbitrary"),
    )(x2d, c_t, sa_t, sb_t, meta, cm_t, sam_t, sbm_t, norm_g, w_in_p, q_norm_g, wq_p, kv_norm_g, wkv_p, w_out_shard)


def _attn_fwd(q, k, v, km, vm, nb, s, tq):
    nq = s // tq

    def body(q_ref, k_ref, v_ref, km_ref, vm_ref, o_ref, lse_ref, s_scr, p_scr):
        row = lax.broadcasted_iota(jnp.int32, (tq, tq), 0)
        col = lax.broadcasted_iota(jnp.int32, (tq, tq), 1)
        for i in range(nq):
            slot = i % 2
            qi = q_ref[0, 0, i * tq:(i + 1) * tq, :]
            sm = _dot_nt(qi, km_ref[0, 0])
            m128 = None
            for j in range(i + 1):
                sc = _dot_nt(qi, k_ref[0, 0, j * tq:(j + 1) * tq, :])
                if j == i:
                    sc = jnp.where(col <= row, sc, NEG_INF)
                s_scr[slot, :, j * tq:(j + 1) * tq] = sc
                mx = sc[:, 0:128]
                for c0 in range(128, tq, 128):
                    mx = jnp.maximum(mx, sc[:, c0:c0 + 128])
                m128 = mx if m128 is None else jnp.maximum(m128, mx)
            m = jnp.maximum(jnp.max(m128, axis=1, keepdims=True), jnp.max(sm, axis=1, keepdims=True))
            pm = jnp.exp(sm - m)
            l128 = None
            for j in range(i + 1):
                p = jnp.exp(s_scr[slot, :, j * tq:(j + 1) * tq] - m)
                p_scr[slot, :, j * tq:(j + 1) * tq] = p.astype(BF16)
                ps = p[:, 0:128]
                for c0 in range(128, tq, 128):
                    ps = ps + p[:, c0:c0 + 128]
                l128 = ps if l128 is None else l128 + ps
            l = jnp.sum(l128, axis=1, keepdims=True) + jnp.sum(pm, axis=1, keepdims=True)
            n = (i + 1) * tq
            acc = _dot(p_scr[slot, :, 0:n], v_ref[0, 0, 0:n, :]) + _dot(pm.astype(BF16), vm_ref[0, 0])
            o_ref[0, 0, i * tq:(i + 1) * tq, :] = acc / l
            lse_ref[0, 0, :, i * tq:(i + 1) * tq] = _row_of(m + jnp.log(l), tq)

    hblk = lambda w: pl.BlockSpec((1, 1, s, w), lambda b, h: (b, h, 0, 0))
    mblk = lambda w: pl.BlockSpec((1, 1, N_META, w), lambda b, h: (0, h, 0, 0))
    return pl.pallas_call(
        body, name="attn_fwd", grid=(nb, HEADS),
        in_specs=[hblk(QK_PAD), hblk(QK_PAD), hblk(VDIM), mblk(QK_PAD), mblk(VDIM)],
        out_specs=[hblk(VDIM), pl.BlockSpec((1, 1, 1, s), lambda b, h: (b, h, 0, 0))],
        out_shape=[jax.ShapeDtypeStruct((nb, HEADS, s, VDIM), F32),
                   jax.ShapeDtypeStruct((nb, HEADS, 1, s), F32)],
        scratch_shapes=[pltpu.VMEM((2, tq, s), F32), pltpu.VMEM((2, tq, s), BF16)],
        compiler_params=_cparams("parallel", "parallel"),
    )(q, k, v, km, vm)


def _shift_rows(a, prev, n_rows):
    rid = lax.broadcasted_iota(jnp.int32, a.shape, 0)
    a1 = jnp.where(rid == 0, prev[7:8, :], pltpu.roll(a, 1, 0))
    a2 = jnp.where(rid == 0, prev[6:7, :], jnp.where(rid == 1, prev[7:8, :], pltpu.roll(a, 2, 0)))
    return a1, a2


def _attn_gate(o, za, ga_h):
    on, r = _rms(o, ga_h)
    return on * (za * _sigmoid(za)), on, r


def _out_fwd_bwd(x2d, tgt2d, o, p, pm, conv_w, ga, gc, gmat, w_out, gf, nb, s, tm):
    nt = s // tm
    r = nb * s
    prev_idx = lambda i: jnp.maximum(i * (tm // 8) - 1, 0)

    def body(x_ref, t_ref, o_ref, za_ref, cb_ref, cc_ref, ch_ref, zc_ref, ccp_ref, chp_ref, mc_ref, mh_ref,
             cw_ref, ga_ref, gc_ref, gm_ref, w_ref, gf_ref,
             dh_ref, dy_ref, dw_ref, dgf_ref, loss_ref):
        i = pl.program_id(0)

        @pl.when(i == 0)
        def _():
            dw_ref[...] = jnp.zeros_like(dw_ref)
            dgf_ref[...] = jnp.zeros_like(dgf_ref)
            loss_ref[...] = jnp.zeros_like(loss_ref)

        ya = []
        for h in range(HEADS):
            y, _, _ = _attn_gate(o_ref[0, h], za_ref[:, VDIM * h:VDIM * (h + 1)],
                                 ga_ref[:, VDIM * h:VDIM * (h + 1)])
            ya.append(y)
        cc = cc_ref[...] * ch_ref[...]
        prev = jnp.where(i % nt == 0, mc_ref[8:16, :] * mh_ref[8:16, :], ccp_ref[...] * chp_ref[...])
        cc1, cc2 = _shift_rows(cc, prev, tm)
        yc = cb_ref[...] * (cw_ref[0:1, :] * cc2 + cw_ref[1:2, :] * cc1 + cw_ref[2:3, :] * cc)
        rg = lax.rsqrt(_group_mean(yc * yc, gm_ref[...]) + EPS)
        zc = zc_ref[...]
        yconv = yc * rg * gc_ref[...] * (zc * _sigmoid(zc))
        ycat = jnp.concatenate(ya + [yconv], axis=1).astype(BF16)
        h2 = x_ref[...] + _dot(ycat, w_ref[...])
        gfv = gf_ref[...]
        y, r2 = _rms(h2, gfv)
        e = y - t_ref[...]
        loss_ref[...] += 0.5 * jnp.sum(e * e) / D_MODEL
        dyv = e * (1.0 / D_MODEL)
        dh2, dgf = _rms_bwd(dyv, h2, r2, gfv)
        dgf_ref[...] += jnp.sum(dgf, axis=0, keepdims=True)
        dh_ref[...] = dh2
        dhb = dh2.astype(BF16)
        dy_ref[...] = _dot_nt(dhb, w_ref[...])
        dw_ref[...] += _dot_tn(ycat, dhb)

    row = lambda w, j: pl.BlockSpec((tm, w), lambda i: (i, j))
    pblk = lambda j: pl.BlockSpec((tm, 512), lambda i: (i, j))
    pprev = lambda j: pl.BlockSpec((8, 512), lambda i: (prev_idx(i), j))
    mblk = lambda j: pl.BlockSpec((N_META, 512), lambda i: (0, j))
    full = lambda a: pl.BlockSpec(a.shape, lambda i: (0,) * a.ndim)
    return pl.pallas_call(
        body, name="out_fwd_bwd", grid=(nb * nt,),
        in_specs=[row(D_MODEL, 0), row(D_MODEL, 0),
                  pl.BlockSpec((1, HEADS, tm, VDIM), lambda i: (i // nt, 0, i % nt, 0)),
                  pblk(BLK_ZA), pblk(BLK_CB), pblk(BLK_CC), pblk(BLK_CH), pblk(BLK_ZC),
                  pprev(BLK_CC), pprev(BLK_CH), mblk(BLK_CC), mblk(BLK_CH),
                  full(conv_w), full(ga), full(gc), full(gmat), full(w_out), full(gf)],
        out_specs=[row(D_MODEL, 0), row(D_MODEL, 0),
                   pl.BlockSpec((D_MODEL, D_MODEL), lambda i: (0, 0)),
                   pl.BlockSpec((1, D_MODEL), lambda i: (0, 0)),
                   pl.BlockSpec((1, 128), lambda i: (0, 0))],
        out_shape=[jax.ShapeDtypeStruct((r, D_MODEL), F32), jax.ShapeDtypeStruct((r, D_MODEL), F32),
                   jax.ShapeDtypeStruct((D_MODEL, D_MODEL), F32), jax.ShapeDtypeStruct((1, D_MODEL), F32),
                   jax.ShapeDtypeStruct((1, 128), F32)],
        compiler_params=_cparams("arbitrary"),
    )(x2d, tgt2d, o, p, p, p, p, p, p, p, pm, pm, conv_w, ga, gc, gmat, w_out, gf)


def _gate_bwd(dycat, o, p, pm, conv_w, ga, gc, gmat, x2d, meta, norm_g, nb, s, tm):
    nt = s // tm
    r = nb * s
    ext = tm + 8
    prev_idx = lambda i: jnp.maximum(i * (tm // 8) - 1, 0)
    next_idx = lambda i: jnp.minimum((i + 1) * (tm // 8), r // 8 - 1)

    def body(dya_ref, dyc_ref, dycn_ref, o_ref, za_ref, cb_ref, cbn_ref, cc_ref, ccp_ref, ccn_ref,
             ch_ref, chp_ref, chn_ref, zc_ref, zcn_ref, mc_ref, mh_ref, cw_ref, ga_ref, gc_ref, gm_ref,
             x_ref, mt_ref, ng_ref,
             dpb_ref, do_ref, dl_ref, dccm_ref, dga_ref, dgc_ref, dcw_ref, dw_hbm, acc_ref, dccs_ref, sems):
        i = pl.program_id(0)

        @pl.when(i == 0)
        def _():
            dga_ref[...] = jnp.zeros_like(dga_ref)
            dgc_ref[...] = jnp.zeros_like(dgc_ref)
            dcw_ref[...] = jnp.zeros_like(dcw_ref)
            acc_ref[...] = jnp.zeros_like(acc_ref)
            dccs_ref[...] = jnp.zeros_like(dccs_ref)

        dga = []
        for h in range(HEADS):
            hs = slice(VDIM * h, VDIM * (h + 1))
            oh, za, gah, dya = o_ref[0, h], za_ref[:, hs], ga_ref[:, hs], dya_ref[:, hs]
            sg = _sigmoid(za)
            on, ro = _rms(oh, gah)
            don = dya * (za * sg)
            dpb_ref[:, hs] = (dya * on * (sg * (1.0 + za * (1.0 - sg)))).astype(BF16)
            do, dg = _rms_bwd(don, oh, ro, gah)
            dga.append(jnp.sum(dg, axis=0, keepdims=True))
            dob = do.astype(BF16)
            do_ref[0, h] = dob
            dl_ref[0, h] = _row_of(jnp.sum(dob.astype(F32) * oh, axis=1, keepdims=True), tm)
        dga_ref[...] += jnp.concatenate(dga, axis=1)

        cat = lambda a, b: jnp.concatenate([a[...], b[...]], axis=0)
        cch = cat(cc_ref, ccn_ref)
        chh = cat(ch_ref, chn_ref)
        cb = cat(cb_ref, cbn_ref)
        zc = cat(zc_ref, zcn_ref)
        dy = cat(dyc_ref, dycn_ref)
        first = i % nt == 0
        last = i % nt == nt - 1
        cc = cch * chh
        prev = jnp.where(first, mc_ref[8:16, :] * mh_ref[8:16, :], ccp_ref[...] * chp_ref[...])
        cc1, cc2 = _shift_rows(cc, prev, ext)
        w0, w1, w2 = cw_ref[0:1, :], cw_ref[1:2, :], cw_ref[2:3, :]
        dw = w0 * cc2 + w1 * cc1 + w2 * cc
        yc = cb * dw
        rg = lax.rsqrt(_group_mean(yc * yc, gm_ref[...]) + EPS)
        ych = yc * rg
        gcv = gc_ref[...]
        sg = _sigmoid(zc)
        dycn = dy * (zc * sg)
        dzc = dy * (ych * gcv) * (sg * (1.0 + zc * (1.0 - sg)))
        dgc_ref[...] += jnp.sum((dycn * ych)[:tm], axis=0, keepdims=True)
        dycg = dycn * gcv
        dyc = rg * (dycg - ych * _group_mean(dycg * ych, gm_ref[...]))
        rid = lax.broadcasted_iota(jnp.int32, (ext, CONV_W), 0)
        ddw = jnp.where(jnp.logical_and(last, rid >= tm), 0.0, dyc * cb)
        dcb = dyc * dw
        dcc = w2 * ddw + w1 * pltpu.roll(ddw, ext - 1, 0) + w0 * pltpu.roll(ddw, ext - 2, 0)
        dpb_ref[:, 512:1024] = dcb[:tm].astype(BF16)
        dpb_ref[:, 1024:1536] = (dcc * chh)[:tm].astype(BF16)
        dpb_ref[:, 1536:2048] = (dcc * cch)[:tm].astype(BF16)
        dpb_ref[:, 2048:2560] = dzc[:tm].astype(BF16)
        rs = lambda a: jnp.sum(a[:tm], axis=0, keepdims=True)
        dcw_ref[0:1, :] += rs(ddw * cc2)
        dcw_ref[1:2, :] += rs(ddw * cc1)
        dcw_ref[2:3, :] += rs(ddw * cc)

        @pl.when(first)
        def _():
            d0, d1 = ddw[0:1, :], ddw[1:2, :]
            r8 = lax.broadcasted_iota(jnp.int32, (8, CONV_W), 0)
            dcm = jnp.where(r8 == 7, w1 * d0 + w0 * d1, jnp.where(r8 == 6, w0 * d0, 0.0))
            dccm_ref[0] = dcm
            dccs_ref[...] += dcm

        ngv = ng_ref[...]
        u, _ = _rms(x_ref[...], ngv)
        acc_ref[512:IN_PAD, :] += _dot_tn(dpb_ref[...], u.astype(BF16))

        @pl.when(i == nb * nt - 1)
        def _():
            dcs = dccs_ref[...]
            z8 = jnp.zeros((8, CONV_W), F32)
            z = jnp.zeros((N_META, CONV_W), F32)
            dpm = jnp.concatenate([z, z, jnp.concatenate([z8, dcs * mh_ref[8:16, :]], axis=0),
                                   jnp.concatenate([z8, dcs * mc_ref[8:16, :]], axis=0), z], axis=1).astype(BF16)
            um, _ = _rms(mt_ref[...], ngv)
            acc_ref[512:IN_PAD, :] += _dot_tn(dpm, um.astype(BF16))
            per = IN_DIM // 4
            cps = [pltpu.make_async_copy(acc_ref.at[0:448], dw_hbm.at[0, 0:448], sems.at[0]),
                   pltpu.make_async_copy(acc_ref.at[512:per + 64], dw_hbm.at[0, 448:per], sems.at[1])]
            for qq in range(1, 4):
                cps.append(pltpu.make_async_copy(acc_ref.at[per * qq + 64:per * (qq + 1) + 64], dw_hbm.at[qq],
                                                 sems.at[qq + 1]))
            for cp in cps:
                cp.start()
            for cp in cps:
                cp.wait()

    row = lambda j: pl.BlockSpec((tm, 512), lambda i: (i, j))
    prv = lambda j: pl.BlockSpec((8, 512), lambda i: (prev_idx(i), j))
    nxt = lambda j: pl.BlockSpec((8, 512), lambda i: (next_idx(i), j))
    mblk = lambda j: pl.BlockSpec((N_META, 512), lambda i: (0, j))
    full = lambda a: pl.BlockSpec(a.shape, lambda i: (0,) * a.ndim)
    hb = lambda w: pl.BlockSpec((1, HEADS, tm, w), lambda i: (i // nt, 0, i % nt, 0))
    acc = lambda rr: pl.BlockSpec((rr, 512), lambda i: (0, 0))
    return pl.pallas_call(
        body, name="gate_bwd", grid=(nb * nt,),
        in_specs=[row(0), row(1), nxt(1), hb(VDIM),
                  row(BLK_ZA), row(BLK_CB), nxt(BLK_CB), row(BLK_CC), prv(BLK_CC), nxt(BLK_CC),
                  row(BLK_CH), prv(BLK_CH), nxt(BLK_CH), row(BLK_ZC), nxt(BLK_ZC),
                  mblk(BLK_CC), mblk(BLK_CH), full(conv_w), full(ga), full(gc), full(gmat),
                  pl.BlockSpec((tm, D_MODEL), lambda i: (i, 0)), full(meta), full(norm_g)],
        out_specs=[pl.BlockSpec((tm, 2560), lambda i: (i, 0)), hb(VDIM),
                   pl.BlockSpec((1, HEADS, 1, tm), lambda i: (i // nt, 0, 0, i % nt)),
                   pl.BlockSpec((1, 8, 512), lambda i: (i // nt, 0, 0)),
                   acc(1), acc(1), acc(8), pl.BlockSpec(memory_space=pl.ANY)],
        out_shape=[jax.ShapeDtypeStruct((r, 2560), BF16), jax.ShapeDtypeStruct((nb, HEADS, s, VDIM), BF16),
                   jax.ShapeDtypeStruct((nb, HEADS, 1, s), F32), jax.ShapeDtypeStruct((nb, 8, 512), F32),
                   jax.ShapeDtypeStruct((1, 512), F32), jax.ShapeDtypeStruct((1, 512), F32),
                   jax.ShapeDtypeStruct((8, 512), F32), jax.ShapeDtypeStruct((4, IN_DIM // 4, D_MODEL), F32)],
        scratch_shapes=[pltpu.VMEM((IN_PAD, D_MODEL), F32), pltpu.VMEM((8, CONV_W), F32),
                        pltpu.SemaphoreType.DMA((5,))],
        compiler_params=_cparams("arbitrary"),
    )(dycat, dycat, dycat, o, p, p, p, p, p, p, p, p, p, p, p, pm, pm, conv_w, ga, gc, gmat, x2d, meta, norm_g)


class _StagedReduce:
    LOC, PRE_S, PRE_R, ICI_S, ICI_R, POST_S, POST_R, OUT, N_SEM = 0, 1, 2, 3, 6, 9, 10, 11, 12

    def __init__(self, shard_shape):
        self.half = (shard_shape[0] // 2, shard_shape[1])

    def scratch(self):
        h = self.half
        return [pltpu.VMEM((4,) + h, F32), pltpu.VMEM((4,) + h, F32), pltpu.VMEM((4,) + h, BF16),
                pltpu.VMEM((3,) + h, BF16), pltpu.VMEM(h, F32), pltpu.SemaphoreType.DMA((self.N_SEM,))]

    def run(self, stage, pin, gout, scr):
        own, sib, wire, rbuf, fin, sems = scr
        r2 = self.half[0]
        x, y, c = lax.axis_index("x"), lax.axis_index("y"), lax.axis_index("c")
        mine = 2 * x + y
        sibling = (x, y, 1 - c)
        chips = [(1 - x, y), (x, 1 - y), (1 - x, 1 - y)]
        rows = lambda half: pl.ds(pl.multiple_of(half * r2, r2), r2)
        mesh = pl.DeviceIdType.MESH

        loc = pltpu.make_async_copy(pin.at[:, rows(c), :], own, sems.at[self.LOC])
        pre = pltpu.make_async_remote_copy(
            src_ref=pin.at[:, rows(1 - c), :], dst_ref=sib, send_sem=sems.at[self.PRE_S],
            recv_sem=sems.at[self.PRE_R], device_id=sibling, device_id_type=mesh)

        def ici(j):
            px, py = chips[j]
            return pltpu.make_async_remote_copy(
                src_ref=wire.at[2 * px + py], dst_ref=rbuf.at[j], send_sem=sems.at[self.ICI_S + j],
                recv_sem=sems.at[self.ICI_R + j], device_id=(px, py, c), device_id_type=mesh)

        def post(half):
            return pltpu.make_async_remote_copy(
                src_ref=fin, dst_ref=gout.at[rows(half), :], send_sem=sems.at[self.POST_S],
                recv_sem=sems.at[self.POST_R], device_id=sibling, device_id_type=mesh)

        keep = pltpu.make_async_copy(fin, gout.at[rows(c), :], sems.at[self.OUT])
        if stage == 0:
            loc.start()
            pre.start()
        elif stage == 1:
            loc.wait()
            pre.wait_recv()
            for blk in range(4):
                tot = own[blk] + sib[blk]
                own[blk] = tot
                wire[blk] = tot.astype(BF16)
            for j in range(3):
                ici(j).start()
        elif stage == 2:
            for j in range(3):
                ici(j).wait_recv()
            tot = own[mine]
            for j in range(3):
                tot = tot + rbuf[j].astype(F32)
            fin[...] = tot
            post(c).start()
            keep.start()
        else:
            post(1 - c).wait_recv()
            pre.wait_send()
            for j in range(3):
                ici(j).wait_send()
            post(c).wait_send()
            keep.wait()


def _attn_bwd(q, k, v, do, lse, delta, km, vm, early, nb, s, t):
    n = s // t
    ne = len(early)
    reds = [_StagedReduce(a.shape[1:]) for a in early]
    n_steps = HEADS * nb
    assert n_steps >= 4

    def body(q_ref, k_ref, v_ref, do_ref, lse_ref, dl_ref, km_ref, vm_ref, *rest):
        pin_refs, rest = rest[:ne], rest[ne:]
        dq_ref, dk_ref, dv_ref, dkm_ref, dvm_ref = rest[:5]
        gout_refs, (p_scr, ds_scr, dq_acc), red_scr = rest[5:5 + ne], rest[5 + ne:8 + ne], rest[8 + ne:]
        b = pl.program_id(1)
        step = pl.program_id(0) * nb + b
        for stage, at in enumerate((0, 1, n_steps - 2, n_steps - 1)):
            @pl.when(step == at)
            def _(stage=stage):
                for a, red in enumerate(reds):
                    red.run(stage, pin_refs[a], gout_refs[a], red_scr[6 * a:6 * a + 6])

        @pl.when(b == 0)
        def _():
            dkm_ref[...] = jnp.zeros_like(dkm_ref)
            dvm_ref[...] = jnp.zeros_like(dvm_ref)

        kr = lax.broadcasted_iota(jnp.int32, (t, t), 0)
        qc = lax.broadcasted_iota(jnp.int32, (t, t), 1)
        km_v, vm_v = km_ref[0, 0], vm_ref[0, 0]
        ptm = jnp.exp(_dot_nt(km_v, q_ref[0, 0]) - lse_ref[0, 0])
        dstm = (ptm * (_dot_nt(vm_v, do_ref[0, 0]) - dl_ref[0, 0])).astype(BF16)
        dkm_ref[0] += _dot(dstm, q_ref[0, 0])
        dvm_ref[0] += _dot(ptm.astype(BF16), do_ref[0, 0])
        dq_acc[...] = _dot_tn(dstm, km_v)
        for j in range(n):
            slot = j % 2
            kj = k_ref[0, 0, j * t:(j + 1) * t, :]
            vj = v_ref[0, 0, j * t:(j + 1) * t, :]
            for i in range(j, n):
                cs = slice(i * t, (i + 1) * t)
                qi = q_ref[0, 0, cs, :]
                doi = do_ref[0, 0, cs, :]
                st = _dot_nt(kj, qi)
                if i == j:
                    st = jnp.where(kr <= qc, st, NEG_INF)
                pt = jnp.exp(st - lse_ref[0, 0, :, cs])
                dst = (pt * (_dot_nt(vj, doi) - dl_ref[0, 0, :, cs])).astype(BF16)
                p_scr[slot, :, cs] = pt.astype(BF16)
                ds_scr[slot, :, cs] = dst
                dq_acc[cs, :] += _dot_tn(dst, kj)
            dv_ref[0, 0, j * t:(j + 1) * t, :] = _dot(p_scr[slot, :, j * t:s], do_ref[0, 0, j * t:s, :]).astype(BF16)
            dk_ref[0, 0, j * t:(j + 1) * t, :] = _dot(ds_scr[slot, :, j * t:s], q_ref[0, 0, j * t:s, :]).astype(BF16)
        dq_ref[0, 0] = dq_acc[...].astype(BF16)

    big = lambda w: pl.BlockSpec((1, 1, s, w), lambda h, b: (b, h, 0, 0))
    rowv = pl.BlockSpec((1, 1, 1, s), lambda h, b: (b, h, 0, 0))
    mk = lambda w: pl.BlockSpec((1, 1, N_META, w), lambda h, b: (0, h, 0, 0))
    mo = lambda w: pl.BlockSpec((1, N_META, w), lambda h, b: (h, 0, 0))
    return pl.pallas_call(
        body, name="attn_bwd", grid=(HEADS, nb),
        in_specs=[big(QK_PAD), big(QK_PAD), big(VDIM), big(VDIM), rowv, rowv, mk(QK_PAD), mk(VDIM)]
        + [pl.BlockSpec(memory_space=pl.ANY)] * ne,
        out_specs=[big(QK_PAD), big(QK_PAD), big(VDIM), mo(QK_PAD), mo(VDIM)]
        + [pl.BlockSpec(memory_space=pl.ANY)] * ne,
        out_shape=[jax.ShapeDtypeStruct((nb, HEADS, s, QK_PAD), BF16),
                   jax.ShapeDtypeStruct((nb, HEADS, s, QK_PAD), BF16),
                   jax.ShapeDtypeStruct((nb, HEADS, s, VDIM), BF16),
                   jax.ShapeDtypeStruct((HEADS, N_META, QK_PAD), F32),
                   jax.ShapeDtypeStruct((HEADS, N_META, VDIM), F32)]
        + [jax.ShapeDtypeStruct(a.shape[1:], F32) for a in early],
        scratch_shapes=[pltpu.VMEM((2, t, s), BF16), pltpu.VMEM((2, t, s), BF16), pltpu.VMEM((s, QK_PAD), F32)]
        + [sc for red in reds for sc in red.scratch()],
        compiler_params=_cparams("arbitrary", "arbitrary"),
    )(q, k, v, do, lse, delta, km, vm, *early)


def _up_bwd(dq, dk, dv, dkm, dvm, p, pm, tabs, tabs_m, wq_p, wkv_p, gq, gkv, nb, s, tm):
    nt = s // tm
    n = nb * nt
    c_t, sa_t, sb_t = tabs
    cm_t, sam_t, sbm_t = tabs_m

    def kv_path(dkh, dvh, pa, c, sa, sb, wkv, gkvv):
        dkpe = dkh[0][:, NOPE:]
        for h in range(1, HEADS):
            dkpe = dkpe + dkh[h][:, NOPE:]
        dkr = _rope_bwd(dkpe, c, sa, sb)
        dkv = jnp.concatenate([d[:, :NOPE] for d in dkh] + list(dvh), axis=1).astype(BF16)
        ckv = pa[:, Q_RANK:Q_RANK + KV_RANK]
        kvn, rkv = _rms(ckv, gkvv)
        dckv, dg = _rms_bwd(_dot(dkv, wkv), ckv, rkv, gkvv)
        return dckv, dkr, kvn.astype(BF16), dkv, jnp.sum(dg, axis=0, keepdims=True)

    def body(dq_ref, dk_ref, dv_ref, pa_ref, c_ref, sa_ref, sb_ref,
             dkm_ref, dvm_ref, pam_ref, cm_ref, sam_ref, sbm_ref,
             wq_ref, wkv_ref, gq_ref, gkv_ref,
             dpa_ref, dpam_ref, pq_ref, pkv_ref, dgq_ref, dgkv_ref, dwq_ref, dwkv_ref):
        i = pl.program_id(0)

        @pl.when(i == 0)
        def _():
            dwq_ref[...] = jnp.zeros_like(dwq_ref)
            dwkv_ref[...] = jnp.zeros_like(dwkv_ref)
            dgq_ref[...] = jnp.zeros_like(dgq_ref)
            dgkv_ref[...] = jnp.zeros_like(dgkv_ref)

        @pl.when(i < n)
        def _():
            c, sa, sb = c_ref[...], sa_ref[...], sb_ref[...]
            pa = pa_ref[...]
            parts = []
            for h in range(HEADS):
                dqh = dq_ref[0, h].astype(F32) * ATTN_SCALE
                parts += [dqh[:, :NOPE], _rope_bwd(dqh[:, NOPE:], c, sa, sb)]
            dql = jnp.concatenate(parts, axis=1).astype(BF16)
            cq = pa[:, 0:Q_RANK]
            gqv = gq_ref[...]
            qn, rq = _rms(cq, gqv)
            dwq_ref[...] += _dot_tn(dql, qn.astype(BF16))
            dcq, dg = _rms_bwd(_dot(dql, wq_ref[...]), cq, rq, gqv)
            dgq_ref[...] += jnp.sum(dg, axis=0, keepdims=True)
            dckv, dkr, kvn, dkv, dgk = kv_path([dk_ref[0, h].astype(F32) for h in range(HEADS)],
                                               [dv_ref[0, h].astype(F32) for h in range(HEADS)],
                                               pa, c, sa, sb, wkv_ref[...], gkv_ref[...])
            dwkv_ref[...] += _dot_tn(dkv, kvn)
            dgkv_ref[...] += dgk
            dpa_ref[...] = jnp.concatenate([dcq, dckv, dkr], axis=1).astype(BF16)

        @pl.when(i == n)
        def _():
            dckv, dkr, kvn, dkv, dgk = kv_path([dkm_ref[h] for h in range(HEADS)],
                                               [dvm_ref[h] for h in range(HEADS)],
                                               pam_ref[...], cm_ref[...], sam_ref[...], sbm_ref[...],
                                               wkv_ref[...], gkv_ref[...])
            dwkv_ref[...] += _dot_tn(dkv, kvn)
            dgkv_ref[...] += dgk
            dpam_ref[...] = jnp.concatenate([jnp.zeros((N_META, Q_RANK), F32), dckv, dkr], axis=1)
            for h in range(HEADS):
                pq_ref[h] = dwq_ref[QK_PAD * h:QK_PAD * h + NOPE + ROPE, :]
                pkv_ref[h, 0:NOPE, :] = dwkv_ref[NOPE * h:NOPE * (h + 1), :]
                pkv_ref[h, NOPE:NOPE + VDIM, :] = dwkv_ref[512 + VDIM * h:512 + VDIM * (h + 1), :]

    cl = lambda i: jnp.minimum(i, n - 1)
    hb = lambda w: pl.BlockSpec((1, HEADS, tm, w), lambda i: (cl(i) // nt, 0, cl(i) % nt, 0))
    tab = pl.BlockSpec((tm, 128), lambda i: (cl(i) % nt, 0))
    full = lambda a: pl.BlockSpec(a.shape, lambda i: (0,) * a.ndim)
    const = lambda shape: pl.BlockSpec(shape, lambda i: (0,) * len(shape))
    return pl.pallas_call(
        body, name="up_bwd", grid=(n + 1,),
        in_specs=[hb(QK_PAD), hb(QK_PAD), hb(VDIM), pl.BlockSpec((tm, 512), lambda i: (cl(i), 0)), tab, tab, tab,
                  full(dkm), full(dvm), pl.BlockSpec((N_META, 512), lambda i: (0, 0)),
                  full(cm_t), full(sam_t), full(sbm_t), full(wq_p), full(wkv_p), full(gq), full(gkv)],
        out_specs=[pl.BlockSpec((tm, 512), lambda i: (cl(i), 0)), const((N_META, 512)),
                   const((HEADS, NOPE + ROPE, Q_RANK)), const((HEADS, NOPE + VDIM, KV_RANK)),
                   const((1, Q_RANK)), const((1, KV_RANK))],
        out_shape=[jax.ShapeDtypeStruct((nb * s, 512), BF16), jax.ShapeDtypeStruct((N_META, 512), F32),
                   jax.ShapeDtypeStruct((HEADS, NOPE + ROPE, Q_RANK), F32),
                   jax.ShapeDtypeStruct((HEADS, NOPE + VDIM, KV_RANK), F32),
                   jax.ShapeDtypeStruct((1, Q_RANK), F32), jax.ShapeDtypeStruct((1, KV_RANK), F32)],
        scratch_shapes=[pltpu.VMEM((HEADS * QK_PAD, Q_RANK), F32), pltpu.VMEM((1024, KV_RANK), F32)],
        compiler_params=_cparams("arbitrary"),
    )(dq, dk, dv, p, c_t, sa_t, sb_t, dkm, dvm, pm, cm_t, sam_t, sbm_t, wq_p, wkv_p, gq, gkv)


def _in_bwd(x2d, dh2, dpa, dpb, meta, dpam, dccm, pm, w_in_p, norm_g, nb, s, tm):
    nt = s // tm
    n = nb * nt

    def body(x_ref, dh_ref, dpa_ref, dpb_ref, mt_ref, dpam_ref, dccm_ref, mc_ref, mh_ref, w_ref, g_ref,
             gx_ref, gm_ref, dwa_ref, dg_ref, acc_ref):
        i = pl.program_id(0)

        @pl.when(i == 0)
        def _():
            acc_ref[...] = jnp.zeros_like(acc_ref)
            dg_ref[...] = jnp.zeros_like(dg_ref)

        def rows(x, dp, dres):
            g = g_ref[...]
            u, r1 = _rms(x, g)
            dpb16 = dp.astype(BF16)
            acc_ref[...] += _dot_tn(dpb16[:, 0:512], u.astype(BF16))
            dx, dg = _rms_bwd(_dot(dpb16, w_ref[...]), x, r1, g)
            dg_ref[...] += jnp.sum(dg, axis=0, keepdims=True)
            return dx if dres is None else dx + dres

        @pl.when(i < n)
        def _():
            dp = jnp.concatenate([dpa_ref[...], dpb_ref[...]], axis=1)
            gx_ref[...] = rows(x_ref[...], dp, dh_ref[...])

        @pl.when(i == n)
        def _():
            dcc = dccm_ref[0]
            for b in range(1, nb):
                dcc = dcc + dccm_ref[b]
            z8 = jnp.zeros((8, CONV_W), F32)
            dc = jnp.concatenate([z8, dcc * mh_ref[8:16, :]], axis=0)
            dh = jnp.concatenate([z8, dcc * mc_ref[8:16, :]], axis=0)
            z = jnp.zeros((N_META, CONV_W), F32)
            dp = jnp.concatenate([dpam_ref[...], z, z, dc, dh, z], axis=1)
            gm_ref[...] = rows(mt_ref[...], dp, None)
            dwa_ref[...] = acc_ref[0:448, :]

    cl = lambda i: jnp.minimum(i, n - 1)
    row = lambda w: pl.BlockSpec((tm, w), lambda i: (cl(i), 0))
    full = lambda a: pl.BlockSpec(a.shape, lambda i: (0,) * a.ndim)
    mblk = lambda j: pl.BlockSpec((N_META, 512), lambda i: (0, j))
    return pl.pallas_call(
        body, name="in_bwd", grid=(n + 1,),
        in_specs=[row(D_MODEL), row(D_MODEL), row(512), row(2560), full(meta), full(dpam), full(dccm),
                  mblk(BLK_CC), mblk(BLK_CH), full(w_in_p), full(norm_g)],
        out_specs=[row(D_MODEL), pl.BlockSpec((N_META, D_MODEL), lambda i: (0, 0)),
                   pl.BlockSpec((448, D_MODEL), lambda i: (0, 0)), pl.BlockSpec((1, D_MODEL), lambda i: (0, 0))],
        out_shape=[jax.ShapeDtypeStruct((nb * s, D_MODEL), F32), jax.ShapeDtypeStruct((N_META, D_MODEL), F32),
                   jax.ShapeDtypeStruct((448, D_MODEL), F32), jax.ShapeDtypeStruct((1, D_MODEL), F32)],
        scratch_shapes=[pltpu.VMEM((512, D_MODEL), F32)],
        compiler_params=_cparams("arbitrary"),
    )(x2d, dh2, dpa, dpb, meta, dpam, dccm, pm, pm, w_in_p, norm_g)


def _gather_weights(split, pieces, out_rows, whole, zero_fills):
    ns, nw, nz = len(split), len(whole), len(zero_fills)
    flat = [(a, pc) for a in range(ns) for pc in pieces[a]]
    nk = len(flat)

    def body(*refs):
        ins, wins, zins = refs[:ns], refs[ns:ns + nw], refs[ns + nw:ns + nw + nz]
        outs, wouts = refs[ns + nw + nz:2 * ns + nw + nz], refs[2 * ns + nw + nz:2 * (ns + nw) + nz]
        send_sems, recv_sems, fwd_send, fwd_recv, loc_sems, w_send, w_recv, w_loc, z_sems = refs[2 * (ns + nw) + nz:]
        x, y, c = lax.axis_index("x"), lax.axis_index("y"), lax.axis_index("c")
        mine = 2 * x + y
        chips = [(1 - x, y), (x, 1 - y), (1 - x, 1 - y)]
        chip_of = [2 * px + py for px, py in chips]

        def src(k):
            a, (s0, nr, _, _, _, _) = flat[k]
            return ins[a].at[s0:s0 + nr]

        def dst(k, q):
            a, (_, nr, per, first, rest, _) = flat[k]
            row = per * q + first + (rest - first) * jnp.minimum(q, 1)
            return outs[a].at[pl.ds(pl.multiple_of(row, 16), nr)]

        def ici(k, j, q):
            px, py = chips[j]
            return pltpu.make_async_remote_copy(
                src_ref=src(k), dst_ref=dst(k, q), send_sem=send_sems.at[k, j], recv_sem=recv_sems.at[k, j],
                device_id=(px, py, c), device_id_type=pl.DeviceIdType.MESH)

        def fwd(k, j):
            ref = dst(k, chip_of[j])
            return pltpu.make_async_remote_copy(
                src_ref=ref, dst_ref=ref, send_sem=fwd_send.at[k, j], recv_sem=fwd_recv.at[k, j],
                device_id=(x, y, 1 - c), device_id_type=pl.DeviceIdType.MESH)

        def wcopy(b, j, q):
            px, py = chips[j]
            return pltpu.make_async_remote_copy(
                src_ref=wins[b], dst_ref=wouts[b].at[q], send_sem=w_send.at[b, j], recv_sem=w_recv.at[b, j],
                device_id=(px, py, c), device_id_type=pl.DeviceIdType.MESH)

        local = [pltpu.make_async_copy(src(k), dst(k, mine), loc_sems.at[k]) for k in range(nk)]
        local += [pltpu.make_async_copy(wins[b], wouts[b].at[mine], w_loc.at[b]) for b in range(nw)]
        for z, (a, _, row0) in enumerate(zero_fills):
            local.append(pltpu.make_async_copy(zins[z], outs[a].at[row0:row0 + zins[z].shape[0]], z_sems.at[z]))
        wsends = [wcopy(b, j, mine) for b in range(nw) for j in range(3)]
        for cp in local + wsends:
            cp.start()

        for half in (0, 1):
            @pl.when(c == half)
            def _(half=half):
                my_k = [k for k in range(nk) if flat[k][1][5] == half]
                other_k = [k for k in range(nk) if flat[k][1][5] != half]
                sends = [ici(k, j, mine) for k in my_k for j in range(3)]
                for cp in sends:
                    cp.start()
                passed = []
                for k in my_k:
                    for j in range(3):
                        ici(k, j, chip_of[j]).wait_recv()
                        cp = fwd(k, j)
                        cp.start()
                        passed.append(cp)
                for k in other_k:
                    for j in range(3):
                        fwd(k, j).wait_recv()
                for cp in sends + passed:
                    cp.wait_send()

        for b in range(nw):
            for j in range(3):
                wcopy(b, j, chip_of[j]).wait_recv()
        for cp in wsends:
            cp.wait_send()
        for cp in local:
            cp.wait()

    hbm = pl.BlockSpec(memory_space=pl.ANY)
    dma = pltpu.SemaphoreType.DMA
    zeros = [z for _, z, _ in zero_fills]
    return pl.pallas_call(
        body, name="gather_weights",
        in_specs=[hbm] * (ns + nw + nz), out_specs=[hbm] * (ns + nw),
        out_shape=([jax.ShapeDtypeStruct((out_rows[a], split[a].shape[1]), split[a].dtype) for a in range(ns)]
                   + [jax.ShapeDtypeStruct((4,) + w.shape, w.dtype) for w in whole]),
        scratch_shapes=[dma((nk, 3)), dma((nk, 3)), dma((nk, 3)), dma((nk, 3)), dma((nk,)),
                        dma((nw, 3)), dma((nw, 3)), dma((nw,)), dma((nz,))],
        compiler_params=pltpu.CompilerParams(vmem_limit_bytes=VMEM_LIMIT),
    )(*split, *whole, *zeros)


def _reduce_grads(parts, small):
    n = len(parts)
    shapes = [a.shape[1:] for a in parts]
    halves = [(sh[0] // 2, sh[1]) for sh in shapes]

    def body(*refs):
        pin, sm_in = refs[:n], refs[n]
        gout, sm_out = refs[n + 1:2 * n + 1], refs[2 * n + 1]
        scr = refs[2 * n + 2:]
        own, sib, wire, rbuf = scr[:n], scr[n:2 * n], scr[2 * n:3 * n], scr[3 * n:4 * n]
        (sbuf, send_sems, recv_sems, loc_sems, pre_send, pre_recv, post_send, post_recv,
         sm_send, sm_recv) = scr[4 * n:]
        x, y, c = lax.axis_index("x"), lax.axis_index("y"), lax.axis_index("c")
        mine = 2 * x + y
        me = 4 * x + 2 * y + c
        sibling = (x, y, 1 - c)
        chips = [(1 - x, y), (x, 1 - y), (1 - x, 1 - y)]

        def rows(a, half):
            r2 = halves[a][0]
            return pl.ds(pl.multiple_of(half * r2, r2), r2)

        def pre(a):
            return pltpu.make_async_remote_copy(
                src_ref=pin[a].at[:, rows(a, 1 - c), :], dst_ref=sib[a], send_sem=pre_send.at[a],
                recv_sem=pre_recv.at[a], device_id=sibling, device_id_type=pl.DeviceIdType.MESH)

        def ici(a, j):
            px, py = chips[j]
            return pltpu.make_async_remote_copy(
                src_ref=wire[a].at[2 * px + py], dst_ref=rbuf[a].at[j], send_sem=send_sems.at[a, j],
                recv_sem=recv_sems.at[a, j], device_id=(px, py, c), device_id_type=pl.DeviceIdType.MESH)

        def post(a, half):
            ref = gout[a].at[rows(a, half), :]
            return pltpu.make_async_remote_copy(
                src_ref=ref, dst_ref=ref, send_sem=post_send.at[a], recv_sem=post_recv.at[a],
                device_id=sibling, device_id_type=pl.DeviceIdType.MESH)

        def small_copy(kk):
            peer = (x ^ (kk >> 2), y ^ ((kk >> 1) & 1), c ^ (kk & 1))
            return pltpu.make_async_remote_copy(
                src_ref=sm_in, dst_ref=sbuf.at[kk], send_sem=sm_send.at[kk - 1], recv_sem=sm_recv.at[kk - 1],
                device_id=peer, device_id_type=pl.DeviceIdType.MESH)

        local = [pltpu.make_async_copy(pin[a].at[:, rows(a, c), :], own[a], loc_sems.at[a]) for a in range(n)]
        pres = [pre(a) for a in range(n)]
        smalls = [small_copy(kk) for kk in range(1, 8)]
        for cp in local + pres + smalls:
            cp.start()
        sbuf[0] = sm_in[...]
        sends = []
        for a in range(n):
            local[a].wait()
            pres[a].wait_recv()
            for blk in range(4):
                tot = own[a][blk] + sib[a][blk]
                own[a][blk] = tot
                wire[a][blk] = tot.astype(BF16)
            for j in range(3):
                cp = ici(a, j)
                cp.start()
                sends.append(cp)
        for cp in smalls:
            cp.wait_recv()
        total = sbuf[me]
        for d in range(1, 8):
            total = total + sbuf[me ^ d]
        sm_out[...] = total
        posts = []
        for a in range(n):
            for j in range(3):
                ici(a, j).wait_recv()
            fin = own[a][mine]
            for j in range(3):
                fin = fin + rbuf[a][j].astype(F32)
            gout[a][rows(a, c), :] = fin
            cp = post(a, c)
            cp.start()
            posts.append(cp)
        for a in range(n):
            post(a, 1 - c).wait_recv()
        for cp in pres + sends + smalls + posts:
            cp.wait_send()

    hbm = pl.BlockSpec(memory_space=pl.ANY)
    vmem = pl.BlockSpec(memory_space=pltpu.VMEM)
    dma = pltpu.SemaphoreType.DMA
    return pl.pallas_call(
        body, name="reduce_grads",
        in_specs=[hbm] * n + [vmem], out_specs=[vmem] * (n + 1),
        out_shape=[jax.ShapeDtypeStruct(sh, F32) for sh in shapes] + [jax.ShapeDtypeStruct(small.shape, F32)],
        scratch_shapes=([pltpu.VMEM((4,) + hs, F32) for hs in halves] + [pltpu.VMEM((4,) + hs, F32) for hs in halves]
                        + [pltpu.VMEM((4,) + hs, BF16) for hs in halves]
                        + [pltpu.VMEM((3,) + hs, BF16) for hs in halves]
                        + [pltpu.VMEM((8,) + small.shape, F32), dma((n, 3)), dma((n, 3)), dma((n,)),
                           dma((n,)), dma((n,)), dma((n,)), dma((n,)), dma((7,)), dma((7,))]),
        compiler_params=pltpu.CompilerParams(vmem_limit_bytes=VMEM_LIMIT),
    )(*parts, small)


def _adamw(w, g, m, v, name):
    shape = w.shape
    w2, g2, m2, v2 = (a.reshape((-1, shape[-1])) for a in (w, g, m, v))

    def body(w_ref, g_ref, m_ref, v_ref, d_ref, nm_ref, nv_ref):
        gv = g_ref[...]
        nm = ADAM_B1 * m_ref[...] + (1.0 - ADAM_B1) * gv
        nv = ADAM_B2 * v_ref[...] + (1.0 - ADAM_B2) * (gv * gv)
        m_hat = nm / (1.0 - ADAM_B1 ** ADAM_STEP)
        v_hat = nv / (1.0 - ADAM_B2 ** ADAM_STEP)
        d_ref[...] = -ADAM_LR * (m_hat / (jnp.sqrt(v_hat) + ADAM_EPS) + ADAM_WD * w_ref[...])
        nm_ref[...] = nm
        nv_ref[...] = nv

    rows, cols = w2.shape
    nblk = cols // 256 if cols % 256 == 0 and rows >= 64 else 1
    blk = pl.BlockSpec((rows, cols // nblk), lambda j: (0, j))
    out = pl.pallas_call(
        body, name=name, grid=(nblk,), in_specs=[blk] * 4, out_specs=[blk] * 3,
        out_shape=[jax.ShapeDtypeStruct(w2.shape, F32)] * 3,
        compiler_params=_cparams("parallel"),
    )(w2, g2, m2, v2)
    return tuple(a.reshape(shape) for a in out)


def kernel(x, meta_tokens, norm_g, w_in, q_norm_g, w_q_up, kv_norm_g, w_kv_up, conv_w, attn_out_g, conv_out_g, w_out, final_norm_g, loss_target, m_meta_tokens, m_norm_g, m_w_in, m_q_norm_g, m_w_q_up, m_kv_norm_g, m_w_kv_up, m_conv_w, m_attn_out_g, m_conv_out_g, m_w_out, m_final_norm_g, v_meta_tokens, v_norm_g, v_w_in, v_q_norm_g, v_w_q_up, v_kv_norm_g, v_w_kv_up, v_conv_w, v_attn_out_g, v_conv_out_g, v_w_out, v_final_norm_g):
    nb, s, _ = x.shape
    tm = min(ROW_TILE, s)
    ta = min(ATTN_TILE, s)
    assert s % tm == 0 and s % ta == 0 and tm % 16 == 0
    r = nb * s

    tr = lambda a: jnp.transpose(a[0])
    w_in_p, wq_p, wkv_p, g_cw, g_meta = _gather_weights(
        [tr(w_in).astype(BF16), tr(w_q_up).astype(BF16), tr(w_kv_up).astype(BF16)],
        [W_IN_PIECES, W_Q_PIECES, W_KV_PIECES], [IN_PAD, HEADS * QK_PAD, 1024],
        [conv_w[0], meta_tokens],
        [(0, jnp.zeros((64, D_MODEL), BF16), 448)]
        + [(1, jnp.zeros((64, Q_RANK), BF16), QK_PAD * h + NOPE + ROPE) for h in range(HEADS)])
    conv_f = jnp.transpose(g_cw, (1, 0, 2)).reshape(3, CONV_W)
    meta_f = jnp.transpose(g_meta, (1, 0, 2)).reshape(N_META, D_MODEL)

    c_all, sa_all, sb_all = _rope_tables(N_META + s)
    tabs_m = (c_all[:N_META], sa_all[:N_META], sb_all[:N_META])
    tabs = (c_all[N_META:], sa_all[N_META:], sb_all[N_META:])
    gid = np.arange(CONV_W) // CONV_GROUP
    gmat = jnp.asarray(np.where(gid[:, None] == gid[None, :], 1.0 / CONV_GROUP, 0.0), BF16)
    ga, gc = attn_out_g, conv_out_g
    gf = final_norm_g.reshape(1, D_MODEL)

    x2d = x.reshape(r, D_MODEL)
    tgt2d = loss_target.reshape(r, D_MODEL)

    p, q, k, v, pm, km, vm, w_out_f = _fwd_proj(x2d, meta_f, tabs, tabs_m, norm_g, w_in_p, q_norm_g, wq_p,
                                                kv_norm_g, wkv_p, w_out[0].astype(BF16), nb, s, tm)
    o, lse = _attn_fwd(q, k, v, km, vm, nb, s, ta)
    dh2, dycat, dw_out, dgf, loss_acc = _out_fwd_bwd(x2d, tgt2d, o, p, pm, conv_f, ga, gc, gmat, w_out_f, gf,
                                                     nb, s, tm)
    dpb, do, delta, dccm, dga, dgc, dcw, p_in_late = _gate_bwd(dycat, o, p, pm, conv_f, ga, gc, gmat,
                                                               x2d, meta_f, norm_g, nb, s, tm)
    p_out = dw_out.reshape(4, D_MODEL // 4, D_MODEL)
    dq, dk, dv, dkm, dvm, g_w_out, g_in_late = _attn_bwd(q, k, v, do, lse, delta, km, vm, [p_out, p_in_late],
                                                         nb, s, ta)
    dpa, dpam, p_q, p_kv, dgq, dgkv = _up_bwd(dq, dk, dv, dkm, dvm, p, pm, tabs, tabs_m, wq_p, wkv_p,
                                              q_norm_g, kv_norm_g, nb, s, tm)
    gx, gmeta, p_in_first, dng = _in_bwd(x2d, dh2, dpa, dpb, meta_f, dpam, dccm, pm, w_in_p, norm_g, nb, s, tm)

    flat =jnp.concatenate([dng.reshape(-1), dgq.reshape(-1), dgkv.reshape(-1), dga.reshape(-1), dgc.reshape(-1),
                            dgf.reshape(-1), dcw[:3].reshape(-1), gmeta.reshape(-1), loss_acc[0, 0:1]])
    n_small = flat.shape[0]
    rows_small = -(-n_small // 1024) * 8
    small = jnp.pad(flat, (0, rows_small * 128 - n_small)).reshape(rows_small, 128)
    p_in_early = jnp.concatenate([p_in_first[None], jnp.zeros((3,) + p_in_first.shape, F32)], axis=0)
    g_in_early, g_w_q_t, g_w_kv_t, small_sum = _reduce_grads([p_in_early, p_q, p_kv], small)
    g_w_in_t = g_in_late + jnp.pad(g_in_early, ((0, IN_DIM // 4 - 448), (0, 0)))
    ssum = small_sum.reshape(-1)

    def take(off, n):
        return ssum[off:off + n], off + n

    off = 0
    g_norm, off = take(off, D_MODEL)
    g_qn, off = take(off, Q_RANK)
    g_kvn, off = take(off, KV_RANK)
    g_ga, off = take(off, CONV_W)
    g_gc, off = take(off, CONV_W)
    g_gf, off = take(off, D_MODEL)
    g_cw_all, off = take(off, 3 * CONV_W)
    g_meta_all, off = take(off, N_META * D_MODEL)
    loss = ssum[off]
    chip = 2 * lax.axis_index("x") + lax.axis_index("y")
    g_conv = lax.dynamic_slice(g_cw_all.reshape(3, CONV_W), (0, chip * 128), (3, 128))
    g_mt = lax.dynamic_slice(g_meta_all.reshape(N_META, D_MODEL), (0, chip * 256), (N_META, 256))

    grads = {
        "meta_tokens": g_mt, "norm_g": g_norm.reshape(1, -1), "w_in": g_w_in_t, "q_norm_g": g_qn.reshape(1, -1),
        "w_q_up": g_w_q_t, "kv_norm_g": g_kvn.reshape(1, -1), "w_kv_up": jnp.transpose(g_w_kv_t)[None],
        "conv_w": g_conv[None], "attn_out_g": g_ga.reshape(1, -1), "conv_out_g": g_gc.reshape(1, -1),
        "w_out": g_w_out[None], "final_norm_g": g_gf,
    }
    transposed = ("w_in", "w_q_up")
    weights = {
        "meta_tokens": (meta_tokens, m_meta_tokens, v_meta_tokens), "norm_g": (norm_g, m_norm_g, v_norm_g),
        "w_in": (w_in, m_w_in, v_w_in), "q_norm_g": (q_norm_g, m_q_norm_g, v_q_norm_g),
        "w_q_up": (w_q_up, m_w_q_up, v_w_q_up), "kv_norm_g": (kv_norm_g, m_kv_norm_g, v_kv_norm_g),
        "w_kv_up": (w_kv_up, m_w_kv_up, v_w_kv_up), "conv_w": (conv_w, m_conv_w, v_conv_w),
        "attn_out_g": (attn_out_g, m_attn_out_g, v_attn_out_g), "conv_out_g": (conv_out_g, m_conv_out_g, v_conv_out_g),
        "w_out": (w_out, m_w_out, v_w_out), "final_norm_g": (final_norm_g, m_final_norm_g, v_final_norm_g),
    }
    names = list(weights)
    deltas, new_m, new_v = [], [], []
    for nme in names:
        w_, m_, v_ = weights[nme]
        if nme in transposed:
            res = _adamw(tr(w_), grads[nme], tr(m_), tr(v_), "adamw_" + nme)
            g_, d_, nm_, nv_ = (jnp.transpose(a)[None] for a in (grads[nme],) + res)
        else:
            g_ = grads[nme].reshape(w_.shape)
            d_, nm_, nv_ = _adamw(w_, g_, m_, v_, "adamw_" + nme)
        grads[nme] = g_
        deltas.append(d_)
        new_m.append(nm_)
        new_v.append(nv_)

    grad_x = gx.reshape(nb, s, D_MODEL)
    return (loss, grad_x, *[grads[nme] for nme in names], *deltas, *new_m, *new_v)
```

```python
import functools

import jax
import jax.numpy as jnp
import numpy as np
from jax import lax
from jax.experimental import pallas as pl
from jax.experimental.pallas import tpu as pltpu

F32 = jnp.float32
BF16 = jnp.bfloat16

D_MODEL = 1024
N_META = 16
HEADS = 4
NOPE = 128
ROPE = 64
VDIM = 128
QK_PAD = 256
Q_RANK = 256
KV_RANK = 128
CONV_W = 512
CONV_GROUP = 64
ROPE_THETA = 10000.0
EPS = 1e-6
ATTN_SCALE = (NOPE + ROPE) ** -0.5
IN_DIM = 3008
IN_PAD = 3072
BLK_ZA, BLK_CB, BLK_CC, BLK_CH, BLK_ZC = 1, 2, 3, 4, 5
NEG_INF = -1e30

ADAM_LR = 0.001
ADAM_B1 = 0.9
ADAM_B2 = 0.999
ADAM_EPS = 1e-08
ADAM_WD = 0.01
ADAM_STEP = 10

ROW_TILE = 512
ATTN_TILE = 256
GATE_TILE = 256
VMEM_LIMIT = 56 * 1024 * 1024

NT = (((1,), (1,)), ((), ()))
TN = (((0,), (0,)), ((), ()))


def _cparams(*sem):
    return pltpu.CompilerParams(dimension_semantics=sem, vmem_limit_bytes=VMEM_LIMIT)


def _dot(a, b):
    return jnp.dot(a, b, preferred_element_type=F32)


def _dot_nt(a, b):
    return lax.dot_general(a, b, NT, preferred_element_type=F32)


def _dot_tn(a, b):
    return lax.dot_general(a, b, TN, preferred_element_type=F32)


def _rms(x, g):
    r = lax.rsqrt(jnp.mean(x * x, axis=-1, keepdims=True) + EPS)
    return x * r * g, r


def _rms_bwd(dy, x, r, g):
    xh = x * r
    dyg = dy * g
    dx = r * (dyg - xh * jnp.mean(dyg * xh, axis=-1, keepdims=True))
    return dx, dy * xh


def _sigmoid(z):
    return 1.0 / (1.0 + jnp.exp(-z))


def _rope(b, c, sa, sb):
    return b * c + pltpu.roll(b, 96, 1) * sa + pltpu.roll(b, 32, 1) * sb


def _rope_bwd(d, c, sa, sb):
    return d * c + pltpu.roll(d * sa, 32, 1) + pltpu.roll(d * sb, 96, 1)


def _group_mean(x, gmat):
    hi = x.astype(BF16)
    lo = (x - hi.astype(F32)).astype(BF16)
    return _dot(hi, gmat) + _dot(lo, gmat)


def _row_of(col, rows):
    return jnp.transpose(jnp.broadcast_to(col, (rows, 128)))[0:1, :]


def _rope_tables(n_pos):
    half = ROPE // 2
    inv_freq = (np.float32(1.0) / (np.float32(ROPE_THETA) ** (np.arange(half, dtype=np.float32) / np.float32(half))))
    ang = np.arange(n_pos, dtype=np.float32)[:, None] * inv_freq.astype(np.float32)[None, :]
    cos, sin = np.cos(ang).astype(np.float32), np.sin(ang).astype(np.float32)
    z = np.zeros((n_pos, half), np.float32)
    c = np.concatenate([cos, cos, z, z], axis=1)
    sa = np.concatenate([-sin, z, z, z], axis=1)
    sb = np.concatenate([z, sin, z, z], axis=1)
    return jnp.asarray(c), jnp.asarray(sa), jnp.asarray(sb)


W_IN_PIECES = ((0, 384, 752, 0, 64, 0), (384, 64, 752, 384, 448, 1), (448, 304, 752, 512, 512, 1))
W_Q_PIECES = ((0, 96, 256, 0, 0, 0), (96, 96, 256, 96, 96, 1))
W_KV_PIECES = ((0, 128, 128, 0, 0, 0), (128, 128, 128, 512, 512, 1))
W_OUT_PIECES = ((0, 128, 256, 0, 0, 0), (128, 128, 256, 128, 128, 1))


class _StagedGather:
    def __init__(self, pieces):
        self.pieces = pieces

    def scratch(self):
        nk, dma = len(self.pieces), pltpu.SemaphoreType.DMA
        return [dma((nk, 3)), dma((nk, 3)), dma((nk, 3)), dma((nk, 3)), dma((nk,))]

    def run(self, stage, src_ref, out_ref, scr):
        send_sems, recv_sems, fwd_send, fwd_recv, loc_sems = scr
        pieces = self.pieces
        nk = len(pieces)
        x, y, c = lax.axis_index("x"), lax.axis_index("y"), lax.axis_index("c")
        mine = 2 * x + y
        chips = [(1 - x, y), (x, 1 - y), (1 - x, 1 - y)]
        chip_of = [2 * px + py for px, py in chips]
        mesh = pl.DeviceIdType.MESH

        def src(k):
            s0, nr = pieces[k][0], pieces[k][1]
            return src_ref.at[s0:s0 + nr]

        def dst(k, q):
            _, nr, per, first, rest, _ = pieces[k]
            row = per * q + first + (rest - first) * jnp.minimum(q, 1)
            return out_ref.at[pl.ds(pl.multiple_of(row, 16), nr)]

        def ici(k, j, q):
            px, py = chips[j]
            return pltpu.make_async_remote_copy(
                src_ref=src(k), dst_ref=dst(k, q), send_sem=send_sems.at[k, j], recv_sem=recv_sems.at[k, j],
                device_id=(px, py, c), device_id_type=mesh)

        def fwd(k, j):
            ref = dst(k, chip_of[j])
            return pltpu.make_async_remote_copy(
                src_ref=ref, dst_ref=ref, send_sem=fwd_send.at[k, j], recv_sem=fwd_recv.at[k, j],
                device_id=(x, y, 1 - c), device_id_type=mesh)

        local = [pltpu.make_async_copy(src(k), dst(k, mine), loc_sems.at[k]) for k in range(nk)]
        if stage == 0:
            for cp in local:
                cp.start()
        if stage == 2:
            for cp in local:
                cp.wait()
        for half in (0, 1):
            @pl.when(c == half)
            def _(half=half):
                my_k = [k for k in range(nk) if pieces[k][5] == half]
                other_k = [k for k in range(nk) if pieces[k][5] != half]
                for k in my_k:
                    for j in range(3):
                        if stage == 0:
                            ici(k, j, mine).start()
                        elif stage == 1:
                            ici(k, j, chip_of[j]).wait_recv()
                            fwd(k, j).start()
                        else:
                            ici(k, j, mine).wait_send()
                            fwd(k, j).wait_send()
                if stage == 2:
                    for k in other_k:
                        for j in range(3):
                            fwd(k, j).wait_recv()


def _fwd_proj(x2d, meta, tabs, tabs_m, norm_g, w_in_p, q_norm_g, wq_p, kv_norm_g, wkv_p, w_out_shard, nb, s, tm):
    nt = s // tm
    n = nb * nt
    n_steps = n + 1
    c_t, sa_t, sb_t = tabs
    cm_t, sam_t, sbm_t = tabs_m
    gat = _StagedGather(W_OUT_PIECES)
    assert n_steps >= 3

    def body(x_ref, c_ref, sa_ref, sb_ref, mt_ref, cm_ref, sam_ref, sbm_ref,
             g_ref, w_ref, gq_ref, wq_ref, gkv_ref, wkv_ref, wos_ref,
             p_ref, q_ref, k_ref, v_ref, pm_ref, km_ref, vm_ref, wo_ref, *gat_scr):
        i = pl.program_id(0)
        for stage, at in enumerate((0, n_steps - 2, n_steps - 1)):
            @pl.when(i == at)
            def _(stage=stage):
                gat.run(stage, wos_ref, wo_ref, gat_scr)

        def project(xv, c, sa, sb, p_out, q_out, k_out, v_out):
            u, _ = _rms(xv, g_ref[...])
            p = _dot_nt(u.astype(BF16), w_ref[...])
            p_out[...] = p
            qn, _ = _rms(p[:, 0:Q_RANK], gq_ref[...])
            q = _dot_nt(qn.astype(BF16), wq_ref[...])
            kvn, _ = _rms(p[:, Q_RANK:Q_RANK + KV_RANK], gkv_ref[...])
            kv = _dot_nt(kvn.astype(BF16), wkv_ref[...])
            kpe = _rope(p[:, 384:512], c, sa, sb)
            for h in range(HEADS):
                if q_out is not None:
                    pe = _rope(q[:, QK_PAD * h + NOPE:QK_PAD * (h + 1)], c, sa, sb)
                    qh = jnp.concatenate([q[:, QK_PAD * h:QK_PAD * h + NOPE], pe], axis=1)
                    q_out[0, h] = (qh * ATTN_SCALE).astype(BF16)
                k_out[0, h] = jnp.concatenate([kv[:, NOPE * h:NOPE * (h + 1)], kpe], axis=1).astype(BF16)
                v_out[0, h] = kv[:, 512 + VDIM * h:512 + VDIM * (h + 1)].astype(BF16)

        @pl.when(i < n)
        def _():
            project(x_ref[...], c_ref[...], sa_ref[...], sb_ref[...], p_ref, q_ref, k_ref, v_ref)

        @pl.when(i == n)
        def _():
            project(mt_ref[...], cm_ref[...], sam_ref[...], sbm_ref[...], pm_ref, None, km_ref, vm_ref)

    cl = lambda i: jnp.minimum(i, n - 1)
    full = lambda a: pl.BlockSpec(a.shape, lambda i: (0,) * a.ndim)
    const = lambda shape: pl.BlockSpec(shape, lambda i: (0,) * len(shape))
    tab = pl.BlockSpec((tm, 128), lambda i: (cl(i) % nt, 0))
    hb = lambda w: pl.BlockSpec((1, HEADS, tm, w), lambda i: (cl(i) // nt, 0, cl(i) % nt, 0))
    hbm = pl.BlockSpec(memory_space=pl.ANY)
    return pl.pallas_call(
        body, name="fwd_proj", grid=(n_steps,),
        in_specs=[pl.BlockSpec((tm, D_MODEL), lambda i: (cl(i), 0)), tab, tab, tab,
                  full(meta), full(cm_t), full(sam_t), full(sbm_t),
                  full(norm_g), full(w_in_p), full(q_norm_g), full(wq_p), full(kv_norm_g), full(wkv_p), hbm],
        out_specs=[pl.BlockSpec((tm, IN_PAD), lambda i: (cl(i), 0)), hb(QK_PAD), hb(QK_PAD), hb(VDIM),
                   const((N_META, IN_PAD)), const((1, HEADS, N_META, QK_PAD)), const((1, HEADS, N_META, VDIM)), hbm],
        out_shape=[jax.ShapeDtypeStruct((nb * s, IN_PAD), F32),
                   jax.ShapeDtypeStruct((nb, HEADS, s, QK_PAD), BF16),
                   jax.ShapeDtypeStruct((nb, HEADS, s, QK_PAD), BF16),
                   jax.ShapeDtypeStruct((nb, HEADS, s, VDIM), BF16),
                   jax.ShapeDtypeStruct((N_META, IN_PAD), F32),
                   jax.ShapeDtypeStruct((1, HEADS, N_META, QK_PAD), BF16),
                   jax.ShapeDtypeStruct((1, HEADS, N_META, VDIM), BF16),
                   jax.ShapeDtypeStruct((D_MODEL, D_MODEL), BF16)],
        scratch_shapes=gat.scratch(),
        compiler_params=_cparams("arbitrary"),
    )(x2d, c_t, sa_t, sb_t, meta, cm_t, sam_t, sbm_t, norm_g, w_in_p, q_norm_g, wq_p, kv_norm_g, wkv_p, w_out_shard)


def _attn_fwd(q, k, v, km, vm, nb, s, tq):
    nq = s // tq

    def body(q_ref, k_ref, v_ref, km_ref, vm_ref, o_ref, lse_ref, s_scr, p_scr):
        row = lax.broadcasted_iota(jnp.int32, (tq, tq), 0)
        col = lax.broadcasted_iota(jnp.int32, (tq, tq), 1)
        for i in range(nq):
            slot = i % 2
            qi = q_ref[0, 0, i * tq:(i + 1) * tq, :]
            sm = _dot_nt(qi, km_ref[0, 0])
            m128 = None
            for j in range(i + 1):
                sc = _dot_nt(qi, k_ref[0, 0, j * tq:(j + 1) * tq, :])
                if j == i:
                    sc = jnp.where(col <= row, sc, NEG_INF)
                s_scr[slot, :, j * tq:(j + 1) * tq] = sc
                mx = sc[:, 0:128]
                for c0 in range(128, tq, 128):
                    mx = jnp.maximum(mx, sc[:, c0:c0 + 128])
                m128 = mx if m128 is None else jnp.maximum(m128, mx)
            m = jnp.maximum(jnp.max(m128, axis=1, keepdims=True), jnp.max(sm, axis=1, keepdims=True))
            pm = jnp.exp(sm - m)
            l128 = None
            for j in range(i + 1):
                p = jnp.exp(s_scr[slot, :, j * tq:(j + 1) * tq] - m)
                p_scr[slot, :, j * tq:(j + 1) * tq] = p.astype(BF16)
                ps = p[:, 0:128]
                for c0 in range(128, tq, 128):
                    ps = ps + p[:, c0:c0 + 128]
                l128 = ps if l128 is None else l128 + ps
            l = jnp.sum(l128, axis=1, keepdims=True) + jnp.sum(pm, axis=1, keepdims=True)
            n = (i + 1) * tq
            acc = _dot(p_scr[slot, :, 0:n], v_ref[0, 0, 0:n, :]) + _dot(pm.astype(BF16), vm_ref[0, 0])
            o_ref[0, 0, i * tq:(i + 1) * tq, :] = acc / l
            lse_ref[0, 0, :, i * tq:(i + 1) * tq] = _row_of(m + jnp.log(l), tq)

    hblk = lambda w: pl.BlockSpec((1, 1, s, w), lambda b, h: (b, h, 0, 0))
    mblk = lambda w: pl.BlockSpec((1, 1, N_META, w), lambda b, h: (0, h, 0, 0))
    return pl.pallas_call(
        body, name="attn_fwd", grid=(nb, HEADS),
        in_specs=[hblk(QK_PAD), hblk(QK_PAD), hblk(VDIM), mblk(QK_PAD), mblk(VDIM)],
        out_specs=[hblk(VDIM), pl.BlockSpec((1, 1, 1, s), lambda b, h: (b, h, 0, 0))],
        out_shape=[jax.ShapeDtypeStruct((nb, HEADS, s, VDIM), F32),
                   jax.ShapeDtypeStruct((nb, HEADS, 1, s), F32)],
        scratch_shapes=[pltpu.VMEM((2, tq, s), F32), pltpu.VMEM((2, tq, s), BF16)],
        compiler_params=_cparams("parallel", "parallel"),
    )(q, k, v, km, vm)


def _shift_rows(a, prev, n_rows):
    rid = lax.broadcasted_iota(jnp.int32, a.shape, 0)
    a1 = jnp.where(rid == 0, prev[7:8, :], pltpu.roll(a, 1, 0))
    a2 = jnp.where(rid == 0, prev[6:7, :], jnp.where(rid == 1, prev[7:8, :], pltpu.roll(a, 2, 0)))
    return a1, a2


def _attn_gate(o, za, ga_h):
    on, r = _rms(o, ga_h)
    return on * (za * _sigmoid(za)), on, r


def _out_fwd_bwd(x2d, tgt2d, o, p, pm, conv_w, ga, gc, gmat, w_out, gf, nb, s, tm):
    nt = s // tm
    r = nb * s
    prev_idx = lambda i: jnp.maximum(i * (tm // 8) - 1, 0)

    def body(x_ref, t_ref, o_ref, za_ref, cb_ref, cc_ref, ch_ref, zc_ref, ccp_ref, chp_ref, mc_ref, mh_ref,
             cw_ref, ga_ref, gc_ref, gm_ref, w_ref, gf_ref,
             dh_ref, dy_ref, dw_ref, dgf_ref, loss_ref):
        i = pl.program_id(0)

        @pl.when(i == 0)
        def _():
            dw_ref[...] = jnp.zeros_like(dw_ref)
            dgf_ref[...] = jnp.zeros_like(dgf_ref)
            loss_ref[...] = jnp.zeros_like(loss_ref)

        ya = []
        for h in range(HEADS):
            y, _, _ = _attn_gate(o_ref[0, h], za_ref[:, VDIM * h:VDIM * (h + 1)],
                                 ga_ref[:, VDIM * h:VDIM * (h + 1)])
            ya.append(y)
        cc = cc_ref[...] * ch_ref[...]
        prev = jnp.where(i % nt == 0, mc_ref[8:16, :] * mh_ref[8:16, :], ccp_ref[...] * chp_ref[...])
        cc1, cc2 = _shift_rows(cc, prev, tm)
        yc = cb_ref[...] * (cw_ref[0:1, :] * cc2 + cw_ref[1:2, :] * cc1 + cw_ref[2:3, :] * cc)
        rg = lax.rsqrt(_group_mean(yc * yc, gm_ref[...]) + EPS)
        zc = zc_ref[...]
        yconv = yc * rg * gc_ref[...] * (zc * _sigmoid(zc))
        ycat = jnp.concatenate(ya + [yconv], axis=1).astype(BF16)
        h2 = x_ref[...] + _dot(ycat, w_ref[...])
        gfv = gf_ref[...]
        y, r2 = _rms(h2, gfv)
        e = y - t_ref[...]
        loss_ref[...] += 0.5 * jnp.sum(e * e) / D_MODEL
        dyv = e * (1.0 / D_MODEL)
        dh2, dgf = _rms_bwd(dyv, h2, r2, gfv)
        dgf_ref[...] += jnp.sum(dgf, axis=0, keepdims=True)
        dh_ref[...] = dh2
        dhb = dh2.astype(BF16)
        dy_ref[...] = _dot_nt(dhb, w_ref[...])
        dw_ref[...] += _dot_tn(ycat, dhb)

    row = lambda w, j: pl.BlockSpec((tm, w), lambda i: (i, j))
    pblk = lambda j: pl.BlockSpec((tm, 512), lambda i: (i, j))
    pprev = lambda j: pl.BlockSpec((8, 512), lambda i: (prev_idx(i), j))
    mblk = lambda j: pl.BlockSpec((N_META, 512), lambda i: (0, j))
    full = lambda a: pl.BlockSpec(a.shape, lambda i: (0,) * a.ndim)
    return pl.pallas_call(
        body, name="out_fwd_bwd", grid=(nb * nt,),
        in_specs=[row(D_MODEL, 0), row(D_MODEL, 0),
                  pl.BlockSpec((1, HEADS, tm, VDIM), lambda i: (i // nt, 0, i % nt, 0)),
                  pblk(BLK_ZA), pblk(BLK_CB), pblk(BLK_CC), pblk(BLK_CH), pblk(BLK_ZC),
                  pprev(BLK_CC), pprev(BLK_CH), mblk(BLK_CC), mblk(BLK_CH),
                  full(conv_w), full(ga), full(gc), full(gmat), full(w_out), full(gf)],
        out_specs=[row(D_MODEL, 0), row(D_MODEL, 0),
                   pl.BlockSpec((D_MODEL, D_MODEL), lambda i: (0, 0)),
                   pl.BlockSpec((1, D_MODEL), lambda i: (0, 0)),
                   pl.BlockSpec((1, 128), lambda i: (0, 0))],
        out_shape=[jax.ShapeDtypeStruct((r, D_MODEL), F32), jax.ShapeDtypeStruct((r, D_MODEL), F32),
                   jax.ShapeDtypeStruct((D_MODEL, D_MODEL), F32), jax.ShapeDtypeStruct((1, D_MODEL), F32),
                   jax.ShapeDtypeStruct((1, 128), F32)],
        compiler_params=_cparams("arbitrary"),
    )(x2d, tgt2d, o, p, p, p, p, p, p, p, pm, pm, conv_w, ga, gc, gmat, w_out, gf)


def _gate_bwd(dycat, o, p, pm, conv_w, ga, gc, gmat, x2d, meta, norm_g, nb, s, tm):
    nt = s // tm
    r = nb * s
    ext = tm + 8
    prev_idx = lambda i: jnp.maximum(i * (tm // 8) - 1, 0)
    next_idx = lambda i: jnp.minimum((i + 1) * (tm // 8), r // 8 - 1)

    def body(dya_ref, dyc_ref, dycn_ref, o_ref, za_ref, cb_ref, cbn_ref, cc_ref, ccp_ref, ccn_ref,
             ch_ref, chp_ref, chn_ref, zc_ref, zcn_ref, mc_ref, mh_ref, cw_ref, ga_ref, gc_ref, gm_ref,
             x_ref, mt_ref, ng_ref,
             dpb_ref, do_ref, dl_ref, dccm_ref, dga_ref, dgc_ref, dcw_ref, dw_hbm,
             acc_ref, dccs_ref, dps_ref, us_ref, sems):
        i = pl.program_id(0)

        @pl.when(i == 0)
        def _():
            dga_ref[...] = jnp.zeros_like(dga_ref)
            dgc_ref[...] = jnp.zeros_like(dgc_ref)
            dcw_ref[...] = jnp.zeros_like(dcw_ref)
            acc_ref[...] = jnp.zeros_like(acc_ref)
            dccs_ref[...] = jnp.zeros_like(dccs_ref)
            dps_ref[1] = jnp.zeros(dps_ref.shape[1:], BF16)
            us_ref[1] = jnp.zeros(us_ref.shape[1:], BF16)

        slot = i % 2
        acc_ref[512:IN_PAD, :] += _dot_tn(dps_ref[1 - slot], us_ref[1 - slot])

        dga = []
        for h in range(HEADS):
            hs = slice(VDIM * h, VDIM * (h + 1))
            oh, za, gah, dya = o_ref[0, h], za_ref[:, hs], ga_ref[:, hs], dya_ref[:, hs]
            sg = _sigmoid(za)
            on, ro = _rms(oh, gah)
            don = dya * (za * sg)
            dpb_ref[:, hs] = (dya * on * (sg * (1.0 + za * (1.0 - sg)))).astype(BF16)
            do, dg = _rms_bwd(don, oh, ro, gah)
            dga.append(jnp.sum(dg, axis=0, keepdims=True))
            dob = do.astype(BF16)
            do_ref[0, h] = dob
            dl_ref[0, h] = _row_of(jnp.sum(dob.astype(F32) * oh, axis=1, keepdims=True), tm)
        dga_ref[...] += jnp.concatenate(dga, axis=1)

        cat = lambda a, b: jnp.concatenate([a[...], b[...]], axis=0)
        cch = cat(cc_ref, ccn_ref)
        chh = cat(ch_ref, chn_ref)
        cb = cat(cb_ref, cbn_ref)
        zc = cat(zc_ref, zcn_ref)
        dy = cat(dyc_ref, dycn_ref)
        first = i % nt == 0
        last = i % nt == nt - 1
        cc = cch * chh
        prev = jnp.where(first, mc_ref[8:16, :] * mh_ref[8:16, :], ccp_ref[...] * chp_ref[...])
        cc1, cc2 = _shift_rows(cc, prev, ext)
        w0, w1, w2 = cw_ref[0:1, :], cw_ref[1:2, :], cw_ref[2:3, :]
        dw = w0 * cc2 + w1 * cc1 + w2 * cc
        yc = cb * dw
        rg = lax.rsqrt(_group_mean(yc * yc, gm_ref[...]) + EPS)
        ych = yc * rg
        gcv = gc_ref[...]
        sg = _sigmoid(zc)
        dycn = dy * (zc * sg)
        dzc = dy * (ych * gcv) * (sg * (1.0 + zc * (1.0 - sg)))
        dgc_ref[...] += jnp.sum((dycn * ych)[:tm], axis=0, keepdims=True)
        dycg = dycn * gcv
        dyc = rg * (dycg - ych * _group_mean(dycg * ych, gm_ref[...]))
        rid = lax.broadcasted_iota(jnp.int32, (ext, CONV_W), 0)
        ddw = jnp.where(jnp.logical_and(last, rid >= tm), 0.0, dyc * cb)
        dcb = dyc * dw
        dcc = w2 * ddw + w1 * pltpu.roll(ddw, ext - 1, 0) + w0 * pltpu.roll(ddw, ext - 2, 0)
        dpb_ref[:, 512:1024] = dcb[:tm].astype(BF16)
        dpb_ref[:, 1024:1536] = (dcc * chh)[:tm].astype(BF16)
        dpb_ref[:, 1536:2048] = (dcc * cch)[:tm].astype(BF16)
        dpb_ref[:, 2048:2560] = dzc[:tm].astype(BF16)
        rs = lambda a: jnp.sum(a[:tm], axis=0, keepdims=True)
        dcw_ref[0:1, :] += rs(ddw * cc2)
        dcw_ref[1:2, :] += rs(ddw * cc1)
        dcw_ref[2:3, :] += rs(ddw * cc)

        @pl.when(first)
        def _():
            d0, d1 = ddw[0:1, :], ddw[1:2, :]
            r8 = lax.broadcasted_iota(jnp.int32, (8, CONV_W), 0)
            dcm = jnp.where(r8 == 7, w1 * d0 + w0 * d1, jnp.where(r8 == 6, w0 * d0, 0.0))
            dccm_ref[0] = dcm
            dccs_ref[...] += dcm

        ngv = ng_ref[...]
        u, _ = _rms(x_ref[...], ngv)
        us_ref[slot] = u.astype(BF16)
        dps_ref[slot] = dpb_ref[...]

        @pl.when(i == nb * nt - 1)
        def _():
            acc_ref[512:IN_PAD, :] += _dot_tn(dps_ref[slot], us_ref[slot])
            dcs = dccs_ref[...]
            z8 = jnp.zeros((8, CONV_W), F32)
            z = jnp.zeros((N_META, CONV_W), F32)
            dpm = jnp.concatenate([z, z, jnp.concatenate([z8, dcs * mh_ref[8:16, :]], axis=0),
                                   jnp.concatenate([z8, dcs * mc_ref[8:16, :]], axis=0), z], axis=1).astype(BF16)
            um, _ = _rms(mt_ref[...], ngv)
            acc_ref[512:IN_PAD, :] += _dot_tn(dpm, um.astype(BF16))
            per = IN_DIM // 4
            cps = [pltpu.make_async_copy(acc_ref.at[0:448], dw_hbm.at[0, 0:448], sems.at[0]),
                   pltpu.make_async_copy(acc_ref.at[512:per + 64], dw_hbm.at[0, 448:per], sems.at[1])]
            for qq in range(1, 4):
                cps.append(pltpu.make_async_copy(acc_ref.at[per * qq + 64:per * (qq + 1) + 64], dw_hbm.at[qq],
                                                 sems.at[qq + 1]))
            for cp in cps:
                cp.start()
            for cp in cps:
                cp.wait()

    row = lambda j: pl.BlockSpec((tm, 512), lambda i: (i, j))
    prv = lambda j: pl.BlockSpec((8, 512), lambda i: (prev_idx(i), j))
    nxt = lambda j: pl.BlockSpec((8, 512), lambda i: (next_idx(i), j))
    mblk = lambda j: pl.BlockSpec((N_META, 512), lambda i: (0, j))
    full = lambda a: pl.BlockSpec(a.shape, lambda i: (0,) * a.ndim)
    hb = lambda w: pl.BlockSpec((1, HEADS, tm, w), lambda i: (i // nt, 0, i % nt, 0))
    acc = lambda rr: pl.BlockSpec((rr, 512), lambda i: (0, 0))
    return pl.pallas_call(
        body, name="gate_bwd", grid=(nb * nt,),
        in_specs=[row(0), row(1), nxt(1), hb(VDIM),
                  row(BLK_ZA), row(BLK_CB), nxt(BLK_CB), row(BLK_CC), prv(BLK_CC), nxt(BLK_CC),
                  row(BLK_CH), prv(BLK_CH), nxt(BLK_CH), row(BLK_ZC), nxt(BLK_ZC),
                  mblk(BLK_CC), mblk(BLK_CH), full(conv_w), full(ga), full(gc), full(gmat),
                  pl.BlockSpec((tm, D_MODEL), lambda i: (i, 0)), full(meta), full(norm_g)],
        out_specs=[pl.BlockSpec((tm, 2560), lambda i: (i, 0)), hb(VDIM),
                   pl.BlockSpec((1, HEADS, 1, tm), lambda i: (i // nt, 0, 0, i % nt)),
                   pl.BlockSpec((1, 8, 512), lambda i: (i // nt, 0, 0)),
                   acc(1), acc(1), acc(8), pl.BlockSpec(memory_space=pl.ANY)],
        out_shape=[jax.ShapeDtypeStruct((r, 2560), BF16), jax.ShapeDtypeStruct((nb, HEADS, s, VDIM), BF16),
                   jax.ShapeDtypeStruct((nb, HEADS, 1, s), F32), jax.ShapeDtypeStruct((nb, 8, 512), F32),
                   jax.ShapeDtypeStruct((1, 512), F32), jax.ShapeDtypeStruct((1, 512), F32),
                   jax.ShapeDtypeStruct((8, 512), F32), jax.ShapeDtypeStruct((4, IN_DIM // 4, D_MODEL), F32)],
        scratch_shapes=[pltpu.VMEM((IN_PAD, D_MODEL), F32), pltpu.VMEM((8, CONV_W), F32),
                        pltpu.VMEM((2, tm, 2560), BF16), pltpu.VMEM((2, tm, D_MODEL), BF16),
                        pltpu.SemaphoreType.DMA((5,))],
        compiler_params=_cparams("arbitrary"),
    )(dycat, dycat, dycat, o, p, p, p, p, p, p, p, p, p, p, p, pm, pm, conv_w, ga, gc, gmat, x2d, meta, norm_g)


class _StagedReduce:
    LOC, PRE_S, PRE_R, ICI_S, ICI_R, POST_S, POST_R, OUT, N_SEM = 0, 1, 2, 3, 6, 9, 10, 11, 12

    def __init__(self, shard_shape):
        self.half = (shard_shape[0] // 2, shard_shape[1])

    def scratch(self):
        h = self.half
        return [pltpu.VMEM((4,) + h, F32), pltpu.VMEM((4,) + h, F32), pltpu.VMEM((4,) + h, BF16),
                pltpu.VMEM((3,) + h, BF16), pltpu.VMEM(h, F32), pltpu.SemaphoreType.DMA((self.N_SEM,))]

    def run(self, stage, pin, gout, scr):
        own, sib, wire, rbuf, fin, sems = scr
        r2 = self.half[0]
        x, y, c = lax.axis_index("x"), lax.axis_index("y"), lax.axis_index("c")
        mine = 2 * x + y
        sibling = (x, y, 1 - c)
        chips = [(1 - x, y), (x, 1 - y), (1 - x, 1 - y)]
        rows = lambda half: pl.ds(pl.multiple_of(half * r2, r2), r2)
        mesh = pl.DeviceIdType.MESH

        loc = pltpu.make_async_copy(pin.at[:, rows(c), :], own, sems.at[self.LOC])
        pre = pltpu.make_async_remote_copy(
            src_ref=pin.at[:, rows(1 - c), :], dst_ref=sib, send_sem=sems.at[self.PRE_S],
            recv_sem=sems.at[self.PRE_R], device_id=sibling, device_id_type=mesh)

        def ici(j):
            px, py = chips[j]
            return pltpu.make_async_remote_copy(
                src_ref=wire.at[2 * px + py], dst_ref=rbuf.at[j], send_sem=sems.at[self.ICI_S + j],
                recv_sem=sems.at[self.ICI_R + j], device_id=(px, py, c), device_id_type=mesh)

        def post(half):
            return pltpu.make_async_remote_copy(
                src_ref=fin, dst_ref=gout.at[rows(half), :], send_sem=sems.at[self.POST_S],
                recv_sem=sems.at[self.POST_R], device_id=sibling, device_id_type=mesh)

        keep = pltpu.make_async_copy(fin, gout.at[rows(c), :], sems.at[self.OUT])
        if stage == 0:
            loc.start()
            pre.start()
        elif stage == 1:
            loc.wait()
            pre.wait_recv()
            for blk in range(4):
                tot = own[blk] + sib[blk]
                own[blk] = tot
                wire[blk] = tot.astype(BF16)
            for j in range(3):
                ici(j).start()
        elif stage == 2:
            for j in range(3):
                ici(j).wait_recv()
            tot = own[mine]
            for j in range(3):
                tot = tot + rbuf[j].astype(F32)
            fin[...] = tot
            post(c).start()
            keep.start()
        else:
            post(1 - c).wait_recv()
            pre.wait_send()
            for j in range(3):
                ici(j).wait_send()
            post(c).wait_send()
            keep.wait()


def _attn_bwd(q, k, v, do, lse, delta, km, vm, early, nb, s, t):
    n = s // t
    ne = len(early)
    reds = [_StagedReduce(a.shape[1:]) for a in early]
    n_steps = HEADS * nb
    assert n_steps >= 4

    def body(q_ref, k_ref, v_ref, do_ref, lse_ref, dl_ref, km_ref, vm_ref, *rest):
        pin_refs, rest = rest[:ne], rest[ne:]
        dq_ref, dk_ref, dv_ref, dkm_ref, dvm_ref = rest[:5]
        gout_refs, (p_scr, ds_scr, dq_acc), red_scr = rest[5:5 + ne], rest[5 + ne:8 + ne], rest[8 + ne:]
        b = pl.program_id(1)
        step = pl.program_id(0) * nb + b
        for stage, at in enumerate((0, 1, n_steps - 2, n_steps - 1)):
            @pl.when(step == at)
            def _(stage=stage):
                for a, red in enumerate(reds):
                    red.run(stage, pin_refs[a], gout_refs[a], red_scr[6 * a:6 * a + 6])

        @pl.when(b == 0)
        def _():
            dkm_ref[...] = jnp.zeros_like(dkm_ref)
            dvm_ref[...] = jnp.zeros_like(dvm_ref)

        kr = lax.broadcasted_iota(jnp.int32, (t, t), 0)
        qc = lax.broadcasted_iota(jnp.int32, (t, t), 1)
        km_v, vm_v = km_ref[0, 0], vm_ref[0, 0]
        ptm = jnp.exp(_dot_nt(km_v, q_ref[0, 0]) - lse_ref[0, 0])
        dstm = (ptm * (_dot_nt(vm_v, do_ref[0, 0]) - dl_ref[0, 0])).astype(BF16)
        dkm_ref[0] += _dot(dstm, q_ref[0, 0])
        dvm_ref[0] += _dot(ptm.astype(BF16), do_ref[0, 0])
        dq_acc[...] = _dot_tn(dstm, km_v)
        for j in range(n):
            slot = j % 2
            kj = k_ref[0, 0, j * t:(j + 1) * t, :]
            vj = v_ref[0, 0, j * t:(j + 1) * t, :]
            for i in range(j, n):
                cs = slice(i * t, (i + 1) * t)
                qi = q_ref[0, 0, cs, :]
                doi = do_ref[0, 0, cs, :]
                st = _dot_nt(kj, qi)
                if i == j:
                    st = jnp.where(kr <= qc, st, NEG_INF)
                pt = jnp.exp(st - lse_ref[0, 0, :, cs])
                dst = (pt * (_dot_nt(vj, doi) - dl_ref[0, 0, :, cs])).astype(BF16)
                p_scr[slot, :, cs] = pt.astype(BF16)
                ds_scr[slot, :, cs] = dst
                dq_acc[cs, :] += _dot_tn(dst, kj)
            dv_ref[0, 0, j * t:(j + 1) * t, :] = _dot(p_scr[slot, :, j * t:s], do_ref[0, 0, j * t:s, :]).astype(BF16)
            dk_ref[0, 0, j * t:(j + 1) * t, :] = _dot(ds_scr[slot, :, j * t:s], q_ref[0, 0, j * t:s, :]).astype(BF16)
        dq_ref[0, 0] = dq_acc[...].astype(BF16)

    big = lambda w: pl.BlockSpec((1, 1, s, w), lambda h, b: (b, h, 0, 0))
    rowv = pl.BlockSpec((1, 1, 1, s), lambda h, b: (b, h, 0, 0))
    mk = lambda w: pl.BlockSpec((1, 1, N_META, w), lambda h, b: (0, h, 0, 0))
    mo = lambda w: pl.BlockSpec((1, N_META, w), lambda h, b: (h, 0, 0))
    return pl.pallas_call(
        body, name="attn_bwd", grid=(HEADS, nb),
        in_specs=[big(QK_PAD), big(QK_PAD), big(VDIM), big(VDIM), rowv, rowv, mk(QK_PAD), mk(VDIM)]
        + [pl.BlockSpec(memory_space=pl.ANY)] * ne,
        out_specs=[big(QK_PAD), big(QK_PAD), big(VDIM), mo(QK_PAD), mo(VDIM)]
        + [pl.BlockSpec(memory_space=pl.ANY)] * ne,
        out_shape=[jax.ShapeDtypeStruct((nb, HEADS, s, QK_PAD), BF16),
                   jax.ShapeDtypeStruct((nb, HEADS, s, QK_PAD), BF16),
                   jax.ShapeDtypeStruct((nb, HEADS, s, VDIM), BF16),
                   jax.ShapeDtypeStruct((HEADS, N_META, QK_PAD), F32),
                   jax.ShapeDtypeStruct((HEADS, N_META, VDIM), F32)]
        + [jax.ShapeDtypeStruct(a.shape[1:], F32) for a in early],
        scratch_shapes=[pltpu.VMEM((2, t, s), BF16), pltpu.VMEM((2, t, s), BF16), pltpu.VMEM((s, QK_PAD), F32)]
        + [sc for red in reds for sc in red.scratch()],
        compiler_params=_cparams("arbitrary", "arbitrary"),
    )(q, k, v, do, lse, delta, km, vm, *early)


def _up_bwd(dq, dk, dv, dkm, dvm, p, pm, tabs, tabs_m, wq_p, wkv_p, gq, gkv, nb, s, tm):
    nt = s // tm
    n = nb * nt
    c_t, sa_t, sb_t = tabs
    cm_t, sam_t, sbm_t = tabs_m

    def kv_path(dkh, dvh, pa, c, sa, sb, wkv, gkvv):
        dkpe = dkh[0][:, NOPE:]
        for h in range(1, HEADS):
            dkpe = dkpe + dkh[h][:, NOPE:]
        dkr = _rope_bwd(dkpe, c, sa, sb)
        dkv = jnp.concatenate([d[:, :NOPE] for d in dkh] + list(dvh), axis=1).astype(BF16)
        ckv = pa[:, Q_RANK:Q_RANK + KV_RANK]
        kvn, rkv = _rms(ckv, gkvv)
        dckv, dg = _rms_bwd(_dot(dkv, wkv), ckv, rkv, gkvv)
        return dckv, dkr, kvn.astype(BF16), dkv, jnp.sum(dg, axis=0, keepdims=True)

    def body(dq_ref, dk_ref, dv_ref, pa_ref, c_ref, sa_ref, sb_ref,
             dkm_ref, dvm_ref, pam_ref, cm_ref, sam_ref, sbm_ref,
             wq_ref, wkv_ref, gq_ref, gkv_ref,
             dpa_ref, dpam_ref, pq_ref, pkv_ref, dgq_ref, dgkv_ref, dwq_ref, dwkv_ref):
        i = pl.program_id(0)

        @pl.when(i == 0)
        def _():
            dwq_ref[...] = jnp.zeros_like(dwq_ref)
            dwkv_ref[...] = jnp.zeros_like(dwkv_ref)
            dgq_ref[...] = jnp.zeros_like(dgq_ref)
            dgkv_ref[...] = jnp.zeros_like(dgkv_ref)

        @pl.when(i < n)
        def _():
            c, sa, sb = c_ref[...], sa_ref[...], sb_ref[...]
            pa = pa_ref[...]
            parts = []
            for h in range(HEADS):
                dqh = dq_ref[0, h].astype(F32) * ATTN_SCALE
                parts += [dqh[:, :NOPE], _rope_bwd(dqh[:, NOPE:], c, sa, sb)]
            dql = jnp.concatenate(parts, axis=1).astype(BF16)
            cq = pa[:, 0:Q_RANK]
            gqv = gq_ref[...]
            qn, rq = _rms(cq, gqv)
            dwq_ref[...] += _dot_tn(dql, qn.astype(BF16))
            dcq, dg = _rms_bwd(_dot(dql, wq_ref[...]), cq, rq, gqv)
            dgq_ref[...] += jnp.sum(dg, axis=0, keepdims=True)
            dckv, dkr, kvn, dkv, dgk = kv_path([dk_ref[0, h].astype(F32) for h in range(HEADS)],
                                               [dv_ref[0, h].astype(F32) for h in range(HEADS)],
                                               pa, c, sa, sb, wkv_ref[...], gkv_ref[...])
            dwkv_ref[...] += _dot_tn(dkv, kvn)
            dgkv_ref[...] += dgk
            dpa_ref[...] = jnp.concatenate([dcq, dckv, dkr], axis=1).astype(BF16)

        @pl.when(i == n)
        def _():
            dckv, dkr, kvn, dkv, dgk = kv_path([dkm_ref[h] for h in range(HEADS)],
                                               [dvm_ref[h] for h in range(HEADS)],
                                               pam_ref[...], cm_ref[...], sam_ref[...], sbm_ref[...],
                                               wkv_ref[...], gkv_ref[...])
            dwkv_ref[...] += _dot_tn(dkv, kvn)
            dgkv_ref[...] += dgk
            dpam_ref[...] = jnp.concatenate([jnp.zeros((N_META, Q_RANK), F32), dckv, dkr], axis=1)
            for h in range(HEADS):
                pq_ref[h] = dwq_ref[QK_PAD * h:QK_PAD * h + NOPE + ROPE, :]
                pkv_ref[h, 0:NOPE, :] = dwkv_ref[NOPE * h:NOPE * (h + 1), :]
                pkv_ref[h, NOPE:NOPE + VDIM, :] = dwkv_ref[512 + VDIM * h:512 + VDIM * (h + 1), :]

    cl = lambda i: jnp.minimum(i, n - 1)
    hb = lambda w: pl.BlockSpec((1, HEADS, tm, w), lambda i: (cl(i) // nt, 0, cl(i) % nt, 0))
    tab = pl.BlockSpec((tm, 128), lambda i: (cl(i) % nt, 0))
    full = lambda a: pl.BlockSpec(a.shape, lambda i: (0,) * a.ndim)
    const = lambda shape: pl.BlockSpec(shape, lambda i: (0,) * len(shape))
    return pl.pallas_call(
        body, name="up_bwd", grid=(n + 1,),
        in_specs=[hb(QK_PAD), hb(QK_PAD), hb(VDIM), pl.BlockSpec((tm, 512), lambda i: (cl(i), 0)), tab, tab, tab,
                  full(dkm), full(dvm), pl.BlockSpec((N_META, 512), lambda i: (0, 0)),
                  full(cm_t), full(sam_t), full(sbm_t), full(wq_p), full(wkv_p), full(gq), full(gkv)],
        out_specs=[pl.BlockSpec((tm, 512), lambda i: (cl(i), 0)), const((N_META, 512)),
                   const((HEADS, NOPE + ROPE, Q_RANK)), const((HEADS, NOPE + VDIM, KV_RANK)),
                   const((1, Q_RANK)), const((1, KV_RANK))],
        out_shape=[jax.ShapeDtypeStruct((nb * s, 512), BF16), jax.ShapeDtypeStruct((N_META, 512), F32),
                   jax.ShapeDtypeStruct((HEADS, NOPE + ROPE, Q_RANK), F32),
                   jax.ShapeDtypeStruct((HEADS, NOPE + VDIM, KV_RANK), F32),
                   jax.ShapeDtypeStruct((1, Q_RANK), F32), jax.ShapeDtypeStruct((1, KV_RANK), F32)],
        scratch_shapes=[pltpu.VMEM((HEADS * QK_PAD, Q_RANK), F32), pltpu.VMEM((1024, KV_RANK), F32)],
        compiler_params=_cparams("arbitrary"),
    )(dq, dk, dv, p, c_t, sa_t, sb_t, dkm, dvm, pm, cm_t, sam_t, sbm_t, wq_p, wkv_p, gq, gkv)


def _in_bwd(x2d, dh2, dpa, dpb, meta, dpam, dccm, pm, w_in_p, norm_g, nb, s, tm):
    nt = s // tm
    n = nb * nt

    def body(x_ref, dh_ref, dpa_ref, dpb_ref, mt_ref, dpam_ref, dccm_ref, mc_ref, mh_ref, w_ref, g_ref,
             gx_ref, gm_ref, dwa_ref, dg_ref, acc_ref):
        i = pl.program_id(0)

        @pl.when(i == 0)
        def _():
            acc_ref[...] = jnp.zeros_like(acc_ref)
            dg_ref[...] = jnp.zeros_like(dg_ref)

        def rows(x, dp, dres):
            g = g_ref[...]
            u, r1 = _rms(x, g)
            dpb16 = dp.astype(BF16)
            acc_ref[...] += _dot_tn(dpb16[:, 0:512], u.astype(BF16))
            dx, dg = _rms_bwd(_dot(dpb16, w_ref[...]), x, r1, g)
            dg_ref[...] += jnp.sum(dg, axis=0, keepdims=True)
            return dx if dres is None else dx + dres

        @pl.when(i < n)
        def _():
            dp = jnp.concatenate([dpa_ref[...], dpb_ref[...]], axis=1)
            gx_ref[...] = rows(x_ref[...], dp, dh_ref[...])

        @pl.when(i == n)
        def _():
            dcc = dccm_ref[0]
            for b in range(1, nb):
                dcc = dcc + dccm_ref[b]
            z8 = jnp.zeros((8, CONV_W), F32)
            dc = jnp.concatenate([z8, dcc * mh_ref[8:16, :]], axis=0)
            dh = jnp.concatenate([z8, dcc * mc_ref[8:16, :]], axis=0)
            z = jnp.zeros((N_META, CONV_W), F32)
            dp = jnp.concatenate([dpam_ref[...], z, z, dc, dh, z], axis=1)
            gm_ref[...] = rows(mt_ref[...], dp, None)
            dwa_ref[...] = acc_ref[0:448, :]

    cl = lambda i: jnp.minimum(i, n - 1)
    row = lambda w: pl.BlockSpec((tm, w), lambda i: (cl(i), 0))
    full = lambda a: pl.BlockSpec(a.shape, lambda i: (0,) * a.ndim)
    mblk = lambda j: pl.BlockSpec((N_META, 512), lambda i: (0, j))
    return pl.pallas_call(
        body, name="in_bwd", grid=(n + 1,),
        in_specs=[row(D_MODEL), row(D_MODEL), row(512), row(2560), full(meta), full(dpam), full(dccm),
                  mblk(BLK_CC), mblk(BLK_CH), full(w_in_p), full(norm_g)],
        out_specs=[row(D_MODEL), pl.BlockSpec((N_META, D_MODEL), lambda i: (0, 0)),
                   pl.BlockSpec((448, D_MODEL), lambda i: (0, 0)), pl.BlockSpec((1, D_MODEL), lambda i: (0, 0))],
        out_shape=[jax.ShapeDtypeStruct((nb * s, D_MODEL), F32), jax.ShapeDtypeStruct((N_META, D_MODEL), F32),
                   jax.ShapeDtypeStruct((448, D_MODEL), F32), jax.ShapeDtypeStruct((1, D_MODEL), F32)],
        scratch_shapes=[pltpu.VMEM((512, D_MODEL), F32)],
        compiler_params=_cparams("arbitrary"),
    )(x2d, dh2, dpa, dpb, meta, dpam, dccm, pm, pm, w_in_p, norm_g)


def _gather_weights(split, pieces, out_rows, whole, zero_fills):
    ns, nw, nz = len(split), len(whole), len(zero_fills)
    flat = [(a, pc) for a in range(ns) for pc in pieces[a]]
    nk = len(flat)

    def body(*refs):
        ins, wins, zins = refs[:ns], refs[ns:ns + nw], refs[ns + nw:ns + nw + nz]
        outs, wouts = refs[ns + nw + nz:2 * ns + nw + nz], refs[2 * ns + nw + nz:2 * (ns + nw) + nz]
        send_sems, recv_sems, fwd_send, fwd_recv, loc_sems, w_send, w_recv, w_loc, z_sems = refs[2 * (ns + nw) + nz:]
        x, y, c = lax.axis_index("x"), lax.axis_index("y"), lax.axis_index("c")
        mine = 2 * x + y
        chips = [(1 - x, y), (x, 1 - y), (1 - x, 1 - y)]
        chip_of = [2 * px + py for px, py in chips]

        def src(k):
            a, (s0, nr, _, _, _, _) = flat[k]
            return ins[a].at[s0:s0 + nr]

        def dst(k, q):
            a, (_, nr, per, first, rest, _) = flat[k]
            row = per * q + first + (rest - first) * jnp.minimum(q, 1)
            return outs[a].at[pl.ds(pl.multiple_of(row, 16), nr)]

        def ici(k, j, q):
            px, py = chips[j]
            return pltpu.make_async_remote_copy(
                src_ref=src(k), dst_ref=dst(k, q), send_sem=send_sems.at[k, j], recv_sem=recv_sems.at[k, j],
                device_id=(px, py, c), device_id_type=pl.DeviceIdType.MESH)

        def fwd(k, j):
            ref = dst(k, chip_of[j])
            return pltpu.make_async_remote_copy(
                src_ref=ref, dst_ref=ref, send_sem=fwd_send.at[k, j], recv_sem=fwd_recv.at[k, j],
                device_id=(x, y, 1 - c), device_id_type=pl.DeviceIdType.MESH)

        def wcopy(b, j, q):
            px, py = chips[j]
            return pltpu.make_async_remote_copy(
                src_ref=wins[b], dst_ref=wouts[b].at[q], send_sem=w_send.at[b, j], recv_sem=w_recv.at[b, j],
                device_id=(px, py, c), device_id_type=pl.DeviceIdType.MESH)

        local = [pltpu.make_async_copy(src(k), dst(k, mine), loc_sems.at[k]) for k in range(nk)]
        local += [pltpu.make_async_copy(wins[b], wouts[b].at[mine], w_loc.at[b]) for b in range(nw)]
        for z, (a, _, row0) in enumerate(zero_fills):
            local.append(pltpu.make_async_copy(zins[z], outs[a].at[row0:row0 + zins[z].shape[0]], z_sems.at[z]))
        wsends = [wcopy(b, j, mine) for b in range(nw) for j in range(3)]
        for cp in local + wsends:
            cp.start()

        for half in (0, 1):
            @pl.when(c == half)
            def _(half=half):
                my_k = [k for k in range(nk) if flat[k][1][5] == half]
                other_k = [k for k in range(nk) if flat[k][1][5] != half]
                sends = [ici(k, j, mine) for k in my_k for j in range(3)]
                for cp in sends:
                    cp.start()
                passed = []
                for k in my_k:
                    for j in range(3):
                        ici(k, j, chip_of[j]).wait_recv()
                        cp = fwd(k, j)
                        cp.start()
                        passed.append(cp)
                for k in other_k:
                    for j in range(3):
                        fwd(k, j).wait_recv()
                for cp in sends + passed:
                    cp.wait_send()

        for b in range(nw):
            for j in range(3):
                wcopy(b, j, chip_of[j]).wait_recv()
        for cp in wsends:
            cp.wait_send()
        for cp in local:
            cp.wait()

    hbm = pl.BlockSpec(memory_space=pl.ANY)
    dma = pltpu.SemaphoreType.DMA
    zeros = [z for _, z, _ in zero_fills]
    return pl.pallas_call(
        body, name="gather_weights",
        in_specs=[hbm] * (ns + nw + nz), out_specs=[hbm] * (ns + nw),
        out_shape=([jax.ShapeDtypeStruct((out_rows[a], split[a].shape[1]), split[a].dtype) for a in range(ns)]
                   + [jax.ShapeDtypeStruct((4,) + w.shape, w.dtype) for w in whole]),
        scratch_shapes=[dma((nk, 3)), dma((nk, 3)), dma((nk, 3)), dma((nk, 3)), dma((nk,)),
                        dma((nw, 3)), dma((nw, 3)), dma((nw,)), dma((nz,))],
        compiler_params=pltpu.CompilerParams(vmem_limit_bytes=VMEM_LIMIT),
    )(*split, *whole, *zeros)


def _reduce_grads(parts, small):
    n = len(parts)
    shapes = [a.shape[1:] for a in parts]
    halves = [(sh[0] // 2, sh[1]) for sh in shapes]

    def body(*refs):
        pin, sm_in = refs[:n], refs[n]
        gout, sm_out = refs[n + 1:2 * n + 1], refs[2 * n + 1]
        scr = refs[2 * n + 2:]
        own, sib, wire, rbuf = scr[:n], scr[n:2 * n], scr[2 * n:3 * n], scr[3 * n:4 * n]
        (sbuf, send_sems, recv_sems, loc_sems, pre_send, pre_recv, post_send, post_recv,
         sm_send, sm_recv) = scr[4 * n:]
        x, y, c = lax.axis_index("x"), lax.axis_index("y"), lax.axis_index("c")
        mine = 2 * x + y
        me = 4 * x + 2 * y + c
        sibling = (x, y, 1 - c)
        chips = [(1 - x, y), (x, 1 - y), (1 - x, 1 - y)]

        def rows(a, half):
            r2 = halves[a][0]
            return pl.ds(pl.multiple_of(half * r2, r2), r2)

        def pre(a):
            return pltpu.make_async_remote_copy(
                src_ref=pin[a].at[:, rows(a, 1 - c), :], dst_ref=sib[a], send_sem=pre_send.at[a],
                recv_sem=pre_recv.at[a], device_id=sibling, device_id_type=pl.DeviceIdType.MESH)

        def ici(a, j):
            px, py = chips[j]
            return pltpu.make_async_remote_copy(
                src_ref=wire[a].at[2 * px + py], dst_ref=rbuf[a].at[j], send_sem=send_sems.at[a, j],
                recv_sem=recv_sems.at[a, j], device_id=(px, py, c), device_id_type=pl.DeviceIdType.MESH)

        def post(a, half):
            ref = gout[a].at[rows(a, half), :]
            return pltpu.make_async_remote_copy(
                src_ref=ref, dst_ref=ref, send_sem=post_send.at[a], recv_sem=post_recv.at[a],
                device_id=sibling, device_id_type=pl.DeviceIdType.MESH)

        def small_copy(kk):
            peer = (x ^ (kk >> 2), y ^ ((kk >> 1) & 1), c ^ (kk & 1))
            return pltpu.make_async_remote_copy(
                src_ref=sm_in, dst_ref=sbuf.at[kk], send_sem=sm_send.at[kk - 1], recv_sem=sm_recv.at[kk - 1],
                device_id=peer, device_id_type=pl.DeviceIdType.MESH)

        local = [pltpu.make_async_copy(pin[a].at[:, rows(a, c), :], own[a], loc_sems.at[a]) for a in range(n)]
        pres = [pre(a) for a in range(n)]
        smalls = [small_copy(kk) for kk in range(1, 8)]
        for cp in local + pres + smalls:
            cp.start()
        sbuf[0] = sm_in[...]
        sends = []
        for a in range(n):
            local[a].wait()
            pres[a].wait_recv()
            for blk in range(4):
                tot = own[a][blk] + sib[a][blk]
                own[a][blk] = tot
                wire[a][blk] = tot.astype(BF16)
            for j in range(3):
                cp = ici(a, j)
                cp.start()
                sends.append(cp)
        for cp in smalls:
            cp.wait_recv()
        total = sbuf[me]
        for d in range(1, 8):
            total = total + sbuf[me ^ d]
        sm_out[...] = total
        posts = []
        for a in range(n):
            for j in range(3):
                ici(a, j).wait_recv()
            fin = own[a][mine]
            for j in range(3):
                fin = fin + rbuf[a][j].astype(F32)
            gout[a][rows(a, c), :] = fin
            cp = post(a, c)
            cp.start()
            posts.append(cp)
        for a in range(n):
            post(a, 1 - c).wait_recv()
        for cp in pres + sends + smalls + posts:
            cp.wait_send()

    hbm = pl.BlockSpec(memory_space=pl.ANY)
    vmem = pl.BlockSpec(memory_space=pltpu.VMEM)
    dma = pltpu.SemaphoreType.DMA
    return pl.pallas_call(
        body, name="reduce_grads",
        in_specs=[hbm] * n + [vmem], out_specs=[vmem] * (n + 1),
        out_shape=[jax.ShapeDtypeStruct(sh, F32) for sh in shapes] + [jax.ShapeDtypeStruct(small.shape, F32)],
        scratch_shapes=([pltpu.VMEM((4,) + hs, F32) for hs in halves] + [pltpu.VMEM((4,) + hs, F32) for hs in halves]
                        + [pltpu.VMEM((4,) + hs, BF16) for hs in halves]
                        + [pltpu.VMEM((3,) + hs, BF16) for hs in halves]
                        + [pltpu.VMEM((8,) + small.shape, F32), dma((n, 3)), dma((n, 3)), dma((n,)),
                           dma((n,)), dma((n,)), dma((n,)), dma((n,)), dma((7,)), dma((7,))]),
        compiler_params=pltpu.CompilerParams(vmem_limit_bytes=VMEM_LIMIT),
    )(*parts, small)


def _adamw(w, g, m, v, name):
    shape = w.shape
    w2, g2, m2, v2 = (a.reshape((-1, shape[-1])) for a in (w, g, m, v))

    def body(w_ref, g_ref, m_ref, v_ref, d_ref, nm_ref, nv_ref):
        gv = g_ref[...]
        nm = ADAM_B1 * m_ref[...] + (1.0 - ADAM_B1) * gv
        nv = ADAM_B2 * v_ref[...] + (1.0 - ADAM_B2) * (gv * gv)
        m_hat = nm / (1.0 - ADAM_B1 ** ADAM_STEP)
        v_hat = nv / (1.0 - ADAM_B2 ** ADAM_STEP)
        d_ref[...] = -ADAM_LR * (m_hat / (jnp.sqrt(v_hat) + ADAM_EPS) + ADAM_WD * w_ref[...])
        nm_ref[...] = nm
        nv_ref[...] = nv

    rows, cols = w2.shape
    nblk = cols // 256 if cols % 256 == 0 and rows >= 64 else 1
    blk = pl.BlockSpec((rows, cols // nblk), lambda j: (0, j))
    out = pl.pallas_call(
        body, name=name, grid=(nblk,), in_specs=[blk] * 4, out_specs=[blk] * 3,
        out_shape=[jax.ShapeDtypeStruct(w2.shape, F32)] * 3,
        compiler_params=_cparams("parallel"),
    )(w2, g2, m2, v2)
    return tuple(a.reshape(shape) for a in out)


def kernel(x, meta_tokens, norm_g, w_in, q_norm_g, w_q_up, kv_norm_g, w_kv_up, conv_w, attn_out_g, conv_out_g, w_out, final_norm_g, loss_target, m_meta_tokens, m_norm_g, m_w_in, m_q_norm_g, m_w_q_up, m_kv_norm_g, m_w_kv_up, m_conv_w, m_attn_out_g, m_conv_out_g, m_w_out, m_final_norm_g, v_meta_tokens, v_norm_g, v_w_in, v_q_norm_g, v_w_q_up, v_kv_norm_g, v_w_kv_up, v_conv_w, v_attn_out_g, v_conv_out_g, v_w_out, v_final_norm_g):
    nb, s, _ = x.shape
    tm = min(ROW_TILE, s)
    ta = min(ATTN_TILE, s)
    assert s % tm == 0 and s % ta == 0 and tm % 16 == 0
    r = nb * s

    tr = lambda a: jnp.transpose(a[0])
    w_in_p, wq_p, wkv_p, g_cw, g_meta = _gather_weights(
        [tr(w_in).astype(BF16), tr(w_q_up).astype(BF16), tr(w_kv_up).astype(BF16)],
        [W_IN_PIECES, W_Q_PIECES, W_KV_PIECES], [IN_PAD, HEADS * QK_PAD, 1024],
        [conv_w[0], meta_tokens],
        [(0, jnp.zeros((64, D_MODEL), BF16), 448)]
        + [(1, jnp.zeros((64, Q_RANK), BF16), QK_PAD * h + NOPE + ROPE) for h in range(HEADS)])
    conv_f = jnp.transpose(g_cw, (1, 0, 2)).reshape(3, CONV_W)
    meta_f = jnp.transpose(g_meta, (1, 0, 2)).reshape(N_META, D_MODEL)

    c_all, sa_all, sb_all = _rope_tables(N_META + s)
    tabs_m = (c_all[:N_META], sa_all[:N_META], sb_all[:N_META])
    tabs = (c_all[N_META:], sa_all[N_META:], sb_all[N_META:])
    gid = np.arange(CONV_W) // CONV_GROUP
    gmat = jnp.asarray(np.where(gid[:, None] == gid[None, :], 1.0 / CONV_GROUP, 0.0), BF16)
    ga, gc = attn_out_g, conv_out_g
    gf = final_norm_g.reshape(1, D_MODEL)

    x2d = x.reshape(r, D_MODEL)
    tgt2d = loss_target.reshape(r, D_MODEL)

    p, q, k, v, pm, km, vm, w_out_f = _fwd_proj(x2d, meta_f, tabs, tabs_m, norm_g, w_in_p, q_norm_g, wq_p,
                                                kv_norm_g, wkv_p, w_out[0].astype(BF16), nb, s, tm)
    o, lse = _attn_fwd(q, k, v, km, vm, nb, s, ta)
    dh2, dycat, dw_out, dgf, loss_acc = _out_fwd_bwd(x2d, tgt2d, o, p, pm, conv_f, ga, gc, gmat, w_out_f, gf,
                                                     nb, s, tm)
    dpb, do, delta, dccm, dga, dgc, dcw, p_in_late = _gate_bwd(dycat, o, p, pm, conv_f, ga, gc, gmat,
                                                               x2d, meta_f, norm_g, nb, s, min(GATE_TILE, s))
    p_out = dw_out.reshape(4, D_MODEL // 4, D_MODEL)
    dq, dk, dv, dkm, dvm, g_w_out, g_in_late = _attn_bwd(q, k, v, do, lse, delta, km, vm, [p_out, p_in_late],
                                                         nb, s, ta)
    dpa, dpam, p_q, p_kv, dgq, dgkv = _up_bwd(dq, dk, dv, dkm, dvm, p, pm, tabs, tabs_m, wq_p, wkv_p,
                                              q_norm_g, kv_norm_g, nb, s, tm)
    gx, gmeta, p_in_first, dng = _in_bwd(x2d, dh2, dpa, dpb, meta_f, dpam, dccm, pm, w_in_p, norm_g, nb, s, tm)

    flat =jnp.concatenate([dng.reshape(-1), dgq.reshape(-1), dgkv.reshape(-1), dga.reshape(-1), dgc.reshape(-1),
                            dgf.reshape(-1), dcw[:3].reshape(-1), gmeta.reshape(-1), loss_acc[0, 0:1]])
    n_small = flat.shape[0]
    rows_small = -(-n_small // 1024) * 8
    small = jnp.pad(flat, (0, rows_small * 128 - n_small)).reshape(rows_small, 128)
    p_in_early = jnp.concatenate([p_in_first[None], jnp.zeros((3,) + p_in_first.shape, F32)], axis=0)
    g_in_early, g_w_q_t, g_w_kv_t, small_sum = _reduce_grads([p_in_early, p_q, p_kv], small)
    g_w_in_t = g_in_late + jnp.pad(g_in_early, ((0, IN_DIM // 4 - 448), (0, 0)))
    ssum = small_sum.reshape(-1)

    def take(off, n):
        return ssum[off:off + n], off + n

    off = 0
    g_norm, off = take(off, D_MODEL)
    g_qn, off = take(off, Q_RANK)
    g_kvn, off = take(off, KV_RANK)
    g_ga, off = take(off, CONV_W)
    g_gc, off = take(off, CONV_W)
    g_gf, off = take(off, D_MODEL)
    g_cw_all, off = take(off, 3 * CONV_W)
    g_meta_all, off = take(off, N_META * D_MODEL)
    loss = ssum[off]
    chip = 2 * lax.axis_index("x") + lax.axis_index("y")
    g_conv = lax.dynamic_slice(g_cw_all.reshape(3, CONV_W), (0, chip * 128), (3, 128))
    g_mt = lax.dynamic_slice(g_meta_all.reshape(N_META, D_MODEL), (0, chip * 256), (N_META, 256))

    grads = {
        "meta_tokens": g_mt, "norm_g": g_norm.reshape(1, -1), "w_in": g_w_in_t, "q_norm_g": g_qn.reshape(1, -1),
        "w_q_up": g_w_q_t, "kv_norm_g": g_kvn.reshape(1, -1), "w_kv_up": jnp.transpose(g_w_kv_t)[None],
        "conv_w": g_conv[None], "attn_out_g": g_ga.reshape(1, -1), "conv_out_g": g_gc.reshape(1, -1),
        "w_out": g_w_out[None], "final_norm_g": g_gf,
    }
    transposed = ("w_in", "w_q_up")
    weights = {
        "meta_tokens": (meta_tokens, m_meta_tokens, v_meta_tokens), "norm_g": (norm_g, m_norm_g, v_norm_g),
        "w_in": (w_in, m_w_in, v_w_in), "q_norm_g": (q_norm_g, m_q_norm_g, v_q_norm_g),
        "w_q_up": (w_q_up, m_w_q_up, v_w_q_up), "kv_norm_g": (kv_norm_g, m_kv_norm_g, v_kv_norm_g),
        "w_kv_up": (w_kv_up, m_w_kv_up, v_w_kv_up), "conv_w": (conv_w, m_conv_w, v_conv_w),
        "attn_out_g": (attn_out_g, m_attn_out_g, v_attn_out_g), "conv_out_g": (conv_out_g, m_conv_out_g, v_conv_out_g),
        "w_out": (w_out, m_w_out, v_w_out), "final_norm_g": (final_norm_g, m_final_norm_g, v_final_norm_g),
    }
    names = list(weights)
    deltas, new_m, new_v = [], [], []
    for nme in names:
        w_, m_, v_ = weights[nme]
        if nme in transposed:
            res = _adamw(tr(w_), grads[nme], tr(m_), tr(v_), "adamw_" + nme)
            g_, d_, nm_, nv_ = (jnp.transpose(a)[None] for a in (grads[nme],) + res)
        else:
            g_ = grads[nme].reshape(w_.shape)
            d_, nm_, nv_ = _adamw(w_, g_, m_, v_, "adamw_" + nme)
        grads[nme] = g_
        deltas.append(d_)
        new_m.append(nm_)
        new_v.append(nv_)

    grad_x = gx.reshape(nb, s, D_MODEL)
    return (loss, grad_x, *[grads[nme] for nme in names], *deltas, *new_m, *new_v)
```

```python
import functools

import jax
import jax.numpy as jnp
import numpy as np
from jax import lax
from jax.experimental import pallas as pl
from jax.experimental.pallas import tpu as pltpu

F32 = jnp.float32
BF16 = jnp.bfloat16

D_MODEL = 1024
N_META = 16
HEADS = 4
NOPE = 128
ROPE = 64
VDIM = 128
QK_PAD = 256
Q_RANK = 256
KV_RANK = 128
CONV_W = 512
CONV_GROUP = 64
ROPE_THETA = 10000.0
EPS = 1e-6
ATTN_SCALE = (NOPE + ROPE) ** -0.5
IN_DIM = 3008
IN_PAD = 3072
BLK_ZA, BLK_CB, BLK_CC, BLK_CH, BLK_ZC = 1, 2, 3, 4, 5
NEG_INF = -1e30

ADAM_LR = 0.001
ADAM_B1 = 0.9
ADAM_B2 = 0.999
ADAM_EPS = 1e-08
ADAM_WD = 0.01
ADAM_STEP = 10

ROW_TILE = 512
ATTN_TILE = 256
VMEM_LIMIT = 56 * 1024 * 1024

NT = (((1,), (1,)), ((), ()))
TN = (((0,), (0,)), ((), ()))


def _cparams(*sem):
    return pltpu.CompilerParams(dimension_semantics=sem, vmem_limit_bytes=VMEM_LIMIT)


def _dot(a, b):
    return jnp.dot(a, b, preferred_element_type=F32)


def _dot_nt(a, b):
    return lax.dot_general(a, b, NT, preferred_element_type=F32)


def _dot_tn(a, b):
    return lax.dot_general(a, b, TN, preferred_element_type=F32)


def _rms(x, g):
    r = lax.rsqrt(jnp.mean(x * x, axis=-1, keepdims=True) + EPS)
    return x * r * g, r


def _rms_bwd(dy, x, r, g):
    xh = x * r
    dyg = dy * g
    dx = r * (dyg - xh * jnp.mean(dyg * xh, axis=-1, keepdims=True))
    return dx, dy * xh


def _sigmoid(z):
    return 1.0 / (1.0 + jnp.exp(-z))


def _rope(b, c, sa, sb):
    return b * c + pltpu.roll(b, 96, 1) * sa + pltpu.roll(b, 32, 1) * sb


def _rope_bwd(d, c, sa, sb):
    return d * c + pltpu.roll(d * sa, 32, 1) + pltpu.roll(d * sb, 96, 1)


def _group_mean(x, gmat):
    hi = x.astype(BF16)
    lo = (x - hi.astype(F32)).astype(BF16)
    return _dot(hi, gmat) + _dot(lo, gmat)


def _row_of(col, rows):
    return jnp.transpose(jnp.broadcast_to(col, (rows, 128)))[0:1, :]


def _rope_tables(n_pos):
    half = ROPE // 2
    inv_freq = (np.float32(1.0) / (np.float32(ROPE_THETA) ** (np.arange(half, dtype=np.float32) / np.float32(half))))
    ang = np.arange(n_pos, dtype=np.float32)[:, None] * inv_freq.astype(np.float32)[None, :]
    cos, sin = np.cos(ang).astype(np.float32), np.sin(ang).astype(np.float32)
    z = np.zeros((n_pos, half), np.float32)
    c = np.concatenate([cos, cos, z, z], axis=1)
    sa = np.concatenate([-sin, z, z, z], axis=1)
    sb = np.concatenate([z, sin, z, z], axis=1)
    return jnp.asarray(c), jnp.asarray(sa), jnp.asarray(sb)


W_IN_PIECES = ((0, 384, 752, 0, 64, 0), (384, 64, 752, 384, 448, 1), (448, 304, 752, 512, 512, 1))
W_Q_PIECES = ((0, 96, 256, 0, 0, 0), (96, 96, 256, 96, 96, 1))
W_KV_PIECES = ((0, 128, 128, 0, 0, 0), (128, 128, 128, 512, 512, 1))
W_OUT_PIECES = ((0, 128, 256, 0, 0, 0), (128, 128, 256, 128, 128, 1))


class _StagedGather:
    def __init__(self, pieces):
        self.pieces = pieces

    def scratch(self):
        nk, dma = len(self.pieces), pltpu.SemaphoreType.DMA
        return [dma((nk, 3)), dma((nk, 3)), dma((nk, 3)), dma((nk, 3)), dma((nk,))]

    def run(self, stage, src_ref, out_ref, scr):
        send_sems, recv_sems, fwd_send, fwd_recv, loc_sems = scr
        pieces = self.pieces
        nk = len(pieces)
        x, y, c = lax.axis_index("x"), lax.axis_index("y"), lax.axis_index("c")
        mine = 2 * x + y
        chips = [(1 - x, y), (x, 1 - y), (1 - x, 1 - y)]
        chip_of = [2 * px + py for px, py in chips]
        mesh = pl.DeviceIdType.MESH

        def src(k):
            s0, nr = pieces[k][0], pieces[k][1]
            return src_ref.at[s0:s0 + nr]

        def dst(k, q):
            _, nr, per, first, rest, _ = pieces[k]
            row = per * q + first + (rest - first) * jnp.minimum(q, 1)
            return out_ref.at[pl.ds(pl.multiple_of(row, 16), nr)]

        def ici(k, j, q):
            px, py = chips[j]
            return pltpu.make_async_remote_copy(
                src_ref=src(k), dst_ref=dst(k, q), send_sem=send_sems.at[k, j], recv_sem=recv_sems.at[k, j],
                device_id=(px, py, c), device_id_type=mesh)

        def fwd(k, j):
            ref = dst(k, chip_of[j])
            return pltpu.make_async_remote_copy(
                src_ref=ref, dst_ref=ref, send_sem=fwd_send.at[k, j], recv_sem=fwd_recv.at[k, j],
                device_id=(x, y, 1 - c), device_id_type=mesh)

        local = [pltpu.make_async_copy(src(k), dst(k, mine), loc_sems.at[k]) for k in range(nk)]
        if stage == 0:
            for cp in local:
                cp.start()
        if stage == 2:
            for cp in local:
                cp.wait()
        for half in (0, 1):
            @pl.when(c == half)
            def _(half=half):
                my_k = [k for k in range(nk) if pieces[k][5] == half]
                other_k = [k for k in range(nk) if pieces[k][5] != half]
                for k in my_k:
                    for j in range(3):
                        if stage == 0:
                            ici(k, j, mine).start()
                        elif stage == 1:
                            ici(k, j, chip_of[j]).wait_recv()
                            fwd(k, j).start()
                        else:
                            ici(k, j, mine).wait_send()
                            fwd(k, j).wait_send()
                if stage == 2:
                    for k in other_k:
                        for j in range(3):
                            fwd(k, j).wait_recv()


def _fwd_proj(x2d, meta, tabs, tabs_m, norm_g, w_in_p, q_norm_g, wq_p, kv_norm_g, wkv_p, w_out_shard, nb, s, tm):
    nt = s // tm
    n = nb * nt
    n_steps = n + 1
    c_t, sa_t, sb_t = tabs
    cm_t, sam_t, sbm_t = tabs_m
    gat = _StagedGather(W_OUT_PIECES)
    assert n_steps >= 3

    def body(x_ref, c_ref, sa_ref, sb_ref, mt_ref, cm_ref, sam_ref, sbm_ref,
             g_ref, w_ref, gq_ref, wq_ref, gkv_ref, wkv_ref, wos_ref,
             p_ref, q_ref, k_ref, v_ref, pm_ref, km_ref, vm_ref, wo_ref, *gat_scr):
        i = pl.program_id(0)
        for stage, at in enumerate((0, n_steps - 2, n_steps - 1)):
            @pl.when(i == at)
            def _(stage=stage):
                gat.run(stage, wos_ref, wo_ref, gat_scr)

        def project(xv, c, sa, sb, p_out, q_out, k_out, v_out):
            u, _ = _rms(xv, g_ref[...])
            p = _dot_nt(u.astype(BF16), w_ref[...])
            p_out[...] = p
            qn, _ = _rms(p[:, 0:Q_RANK], gq_ref[...])
            q = _dot_nt(qn.astype(BF16), wq_ref[...])
            kvn, _ = _rms(p[:, Q_RANK:Q_RANK + KV_RANK], gkv_ref[...])
            kv = _dot_nt(kvn.astype(BF16), wkv_ref[...])
            kpe = _rope(p[:, 384:512], c, sa, sb)
            for h in range(HEADS):
                if q_out is not None:
                    pe = _rope(q[:, QK_PAD * h + NOPE:QK_PAD * (h + 1)], c, sa, sb)
                    qh = jnp.concatenate([q[:, QK_PAD * h:QK_PAD * h + NOPE], pe], axis=1)
                    q_out[0, h] = (qh * ATTN_SCALE).astype(BF16)
                k_out[0, h] = jnp.concatenate([kv[:, NOPE * h:NOPE * (h + 1)], kpe], axis=1).astype(BF16)
                v_out[0, h] = kv[:, 512 + VDIM * h:512 + VDIM * (h + 1)].astype(BF16)

        @pl.when(i < n)
        def _():
            project(x_ref[...], c_ref[...], sa_ref[...], sb_ref[...], p_ref, q_ref, k_ref, v_ref)

        @pl.when(i == n)
        def _():
            project(mt_ref[...], cm_ref[...], sam_ref[...], sbm_ref[...], pm_ref, None, km_ref, vm_ref)

    cl = lambda i: jnp.minimum(i, n - 1)
    full = lambda a: pl.BlockSpec(a.shape, lambda i: (0,) * a.ndim)
    const = lambda shape: pl.BlockSpec(shape, lambda i: (0,) * len(shape))
    tab = pl.BlockSpec((tm, 128), lambda i: (cl(i) % nt, 0))
    hb = lambda w: pl.BlockSpec((1, HEADS, tm, w), lambda i: (cl(i) // nt, 0, cl(i) % nt, 0))
    hbm = pl.BlockSpec(memory_space=pl.ANY)
    return pl.pallas_call(
        body, name="fwd_proj", grid=(n_steps,),
        in_specs=[pl.BlockSpec((tm, D_MODEL), lambda i: (cl(i), 0)), tab, tab, tab,
                  full(meta), full(cm_t), full(sam_t), full(sbm_t),
                  full(norm_g), full(w_in_p), full(q_norm_g), full(wq_p), full(kv_norm_g), full(wkv_p), hbm],
        out_specs=[pl.BlockSpec((tm, IN_PAD), lambda i: (cl(i), 0)), hb(QK_PAD), hb(QK_PAD), hb(VDIM),
                   const((N_META, IN_PAD)), const((1, HEADS, N_META, QK_PAD)), const((1, HEADS, N_META, VDIM)), hbm],
        out_shape=[jax.ShapeDtypeStruct((nb * s, IN_PAD), F32),
                   jax.ShapeDtypeStruct((nb, HEADS, s, QK_PAD), BF16),
                   jax.ShapeDtypeStruct((nb, HEADS, s, QK_PAD), BF16),
                   jax.ShapeDtypeStruct((nb, HEADS, s, VDIM), BF16),
                   jax.ShapeDtypeStruct((N_META, IN_PAD), F32),
                   jax.ShapeDtypeStruct((1, HEADS, N_META, QK_PAD), BF16),
                   jax.ShapeDtypeStruct((1, HEADS, N_META, VDIM), BF16),
                   jax.ShapeDtypeStruct((D_MODEL, D_MODEL), BF16)],
        scratch_shapes=gat.scratch(),
        compiler_params=_cparams("arbitrary"),
    )(x2d, c_t, sa_t, sb_t, meta, cm_t, sam_t, sbm_t, norm_g, w_in_p, q_norm_g, wq_p, kv_norm_g, wkv_p, w_out_shard)


def _attn_fwd(q, k, v, km, vm, nb, s, tq):
    nq = s // tq

    def body(q_ref, k_ref, v_ref, km_ref, vm_ref, o_ref, lse_ref, s_scr, p_scr):
        row = lax.broadcasted_iota(jnp.int32, (tq, tq), 0)
        col = lax.broadcasted_iota(jnp.int32, (tq, tq), 1)
        for i in range(nq):
            slot = i % 2
            qi = q_ref[0, 0, i * tq:(i + 1) * tq, :]
            sm = _dot_nt(qi, km_ref[0, 0])
            m128 = None
            for j in range(i + 1):
                sc = _dot_nt(qi, k_ref[0, 0, j * tq:(j + 1) * tq, :])
                if j == i:
                    sc = jnp.where(col <= row, sc, NEG_INF)
                s_scr[slot, :, j * tq:(j + 1) * tq] = sc
                mx = sc[:, 0:128]
                for c0 in range(128, tq, 128):
                    mx = jnp.maximum(mx, sc[:, c0:c0 + 128])
                m128 = mx if m128 is None else jnp.maximum(m128, mx)
            m = jnp.maximum(jnp.max(m128, axis=1, keepdims=True), jnp.max(sm, axis=1, keepdims=True))
            pm = jnp.exp(sm - m)
            l128 = None
            for j in range(i + 1):
                p = jnp.exp(s_scr[slot, :, j * tq:(j + 1) * tq] - m)
                p_scr[slot, :, j * tq:(j + 1) * tq] = p.astype(BF16)
                ps = p[:, 0:128]
                for c0 in range(128, tq, 128):
                    ps = ps + p[:, c0:c0 + 128]
                l128 = ps if l128 is None else l128 + ps
            l = jnp.sum(l128, axis=1, keepdims=True) + jnp.sum(pm, axis=1, keepdims=True)
            n = (i + 1) * tq
            acc = _dot(p_scr[slot, :, 0:n], v_ref[0, 0, 0:n, :]) + _dot(pm.astype(BF16), vm_ref[0, 0])
            o_ref[0, 0, i * tq:(i + 1) * tq, :] = acc / l
            lse_ref[0, 0, :, i * tq:(i + 1) * tq] = _row_of(m + jnp.log(l), tq)

    hblk = lambda w: pl.BlockSpec((1, 1, s, w), lambda b, h: (b, h, 0, 0))
    mblk = lambda w: pl.BlockSpec((1, 1, N_META, w), lambda b, h: (0, h, 0, 0))
    return pl.pallas_call(
        body, name="attn_fwd", grid=(nb, HEADS),
        in_specs=[hblk(QK_PAD), hblk(QK_PAD), hblk(VDIM), mblk(QK_PAD), mblk(VDIM)],
        out_specs=[hblk(VDIM), pl.BlockSpec((1, 1, 1, s), lambda b, h: (b, h, 0, 0))],
        out_shape=[jax.ShapeDtypeStruct((nb, HEADS, s, VDIM), F32),
                   jax.ShapeDtypeStruct((nb, HEADS, 1, s), F32)],
        scratch_shapes=[pltpu.VMEM((2, tq, s), F32), pltpu.VMEM((2, tq, s), BF16)],
        compiler_params=_cparams("parallel", "parallel"),
    )(q, k, v, km, vm)


def _shift_rows(a, prev, n_rows):
    rid = lax.broadcasted_iota(jnp.int32, a.shape, 0)
    a1 = jnp.where(rid == 0, prev[7:8, :], pltpu.roll(a, 1, 0))
    a2 = jnp.where(rid == 0, prev[6:7, :], jnp.where(rid == 1, prev[7:8, :], pltpu.roll(a, 2, 0)))
    return a1, a2


def _attn_gate(o, za, ga_h):
    on, r = _rms(o, ga_h)
    return on * (za * _sigmoid(za)), on, r


def _out_fwd_bwd(x2d, tgt2d, o, p, pm, conv_w, ga, gc, gmat, w_out, gf, nb, s, tm):
    nt = s // tm
    r = nb * s
    prev_idx = lambda i: jnp.maximum(i * (tm // 8) - 1, 0)

    def body(x_ref, t_ref, o_ref, za_ref, cb_ref, cc_ref, ch_ref, zc_ref, ccp_ref, chp_ref, mc_ref, mh_ref,
             cw_ref, ga_ref, gc_ref, gm_ref, w_ref, gf_ref,
             dh_ref, dy_ref, dw_ref, dgf_ref, loss_ref):
        i = pl.program_id(0)

        @pl.when(i == 0)
        def _():
            dw_ref[...] = jnp.zeros_like(dw_ref)
            dgf_ref[...] = jnp.zeros_like(dgf_ref)
            loss_ref[...] = jnp.zeros_like(loss_ref)

        ya = []
        for h in range(HEADS):
            y, _, _ = _attn_gate(o_ref[0, h], za_ref[:, VDIM * h:VDIM * (h + 1)],
                                 ga_ref[:, VDIM * h:VDIM * (h + 1)])
            ya.append(y)
        cc = cc_ref[...] * ch_ref[...]
        prev = jnp.where(i % nt == 0, mc_ref[8:16, :] * mh_ref[8:16, :], ccp_ref[...] * chp_ref[...])
        cc1, cc2 = _shift_rows(cc, prev, tm)
        yc = cb_ref[...] * (cw_ref[0:1, :] * cc2 + cw_ref[1:2, :] * cc1 + cw_ref[2:3, :] * cc)
        rg = lax.rsqrt(_group_mean(yc * yc, gm_ref[...]) + EPS)
        zc = zc_ref[...]
        yconv = yc * rg * gc_ref[...] * (zc * _sigmoid(zc))
        ycat = jnp.concatenate(ya + [yconv], axis=1).astype(BF16)
        h2 = x_ref[...] + _dot(ycat, w_ref[...])
        gfv = gf_ref[...]
        y, r2 = _rms(h2, gfv)
        e = y - t_ref[...]
        loss_ref[...] += 0.5 * jnp.sum(e * e) / D_MODEL
        dyv = e * (1.0 / D_MODEL)
        dh2, dgf = _rms_bwd(dyv, h2, r2, gfv)
        dgf_ref[...] += jnp.sum(dgf, axis=0, keepdims=True)
        dh_ref[...] = dh2
        dhb = dh2.astype(BF16)
        dy_ref[...] = _dot_nt(dhb, w_ref[...])
        dw_ref[...] += _dot_tn(ycat, dhb)

    row = lambda w, j: pl.BlockSpec((tm, w), lambda i: (i, j))
    pblk = lambda j: pl.BlockSpec((tm, 512), lambda i: (i, j))
    pprev = lambda j: pl.BlockSpec((8, 512), lambda i: (prev_idx(i), j))
    mblk = lambda j: pl.BlockSpec((N_META, 512), lambda i: (0, j))
    full = lambda a: pl.BlockSpec(a.shape, lambda i: (0,) * a.ndim)
    return pl.pallas_call(
        body, name="out_fwd_bwd", grid=(nb * nt,),
        in_specs=[row(D_MODEL, 0), row(D_MODEL, 0),
                  pl.BlockSpec((1, HEADS, tm, VDIM), lambda i: (i // nt, 0, i % nt, 0)),
                  pblk(BLK_ZA), pblk(BLK_CB), pblk(BLK_CC), pblk(BLK_CH), pblk(BLK_ZC),
                  pprev(BLK_CC), pprev(BLK_CH), mblk(BLK_CC), mblk(BLK_CH),
                  full(conv_w), full(ga), full(gc), full(gmat), full(w_out), full(gf)],
        out_specs=[row(D_MODEL, 0), row(D_MODEL, 0),
                   pl.BlockSpec((D_MODEL, D_MODEL), lambda i: (0, 0)),
                   pl.BlockSpec((1, D_MODEL), lambda i: (0, 0)),
                   pl.BlockSpec((1, 128), lambda i: (0, 0))],
        out_shape=[jax.ShapeDtypeStruct((r, D_MODEL), F32), jax.ShapeDtypeStruct((r, D_MODEL), F32),
                   jax.ShapeDtypeStruct((D_MODEL, D_MODEL), F32), jax.ShapeDtypeStruct((1, D_MODEL), F32),
                   jax.ShapeDtypeStruct((1, 128), F32)],
        compiler_params=_cparams("arbitrary"),
    )(x2d, tgt2d, o, p, p, p, p, p, p, p, pm, pm, conv_w, ga, gc, gmat, w_out, gf)


def _gate_bwd(dycat, o, p, pm, conv_w, ga, gc, gmat, nb, s, tm):
    nt = s // tm
    r = nb * s
    ext = tm + 8
    prev_idx = lambda i: jnp.maximum(i * (tm // 8) - 1, 0)
    next_idx = lambda i: jnp.minimum((i + 1) * (tm // 8), r // 8 - 1)

    def body(dya_ref, dyc_ref, dycn_ref, o_ref, za_ref, cb_ref, cbn_ref, cc_ref, ccp_ref, ccn_ref,
             ch_ref, chp_ref, chn_ref, zc_ref, zcn_ref, mc_ref, mh_ref, cw_ref, ga_ref, gc_ref, gm_ref,
             dpb_ref, do_ref, dl_ref, dccm_ref, dga_ref, dgc_ref, dcw_ref):
        i = pl.program_id(0)

        @pl.when(i == 0)
        def _():
            dga_ref[...] = jnp.zeros_like(dga_ref)
            dgc_ref[...] = jnp.zeros_like(dgc_ref)
            dcw_ref[...] = jnp.zeros_like(dcw_ref)

        dga = []
        for h in range(HEADS):
            hs = slice(VDIM * h, VDIM * (h + 1))
            oh, za, gah, dya = o_ref[0, h], za_ref[:, hs], ga_ref[:, hs], dya_ref[:, hs]
            sg = _sigmoid(za)
            on, ro = _rms(oh, gah)
            don = dya * (za * sg)
            dpb_ref[:, hs] = (dya * on * (sg * (1.0 + za * (1.0 - sg)))).astype(BF16)
            do, dg = _rms_bwd(don, oh, ro, gah)
            dga.append(jnp.sum(dg, axis=0, keepdims=True))
            dob = do.astype(BF16)
            do_ref[0, h] = dob
            dl_ref[0, h] = _row_of(jnp.sum(dob.astype(F32) * oh, axis=1, keepdims=True), tm)
        dga_ref[...] += jnp.concatenate(dga, axis=1)

        cat = lambda a, b: jnp.concatenate([a[...], b[...]], axis=0)
        cch = cat(cc_ref, ccn_ref)
        chh = cat(ch_ref, chn_ref)
        cb = cat(cb_ref, cbn_ref)
        zc = cat(zc_ref, zcn_ref)
        dy = cat(dyc_ref, dycn_ref)
        first = i % nt == 0
        last = i % nt == nt - 1
        cc = cch * chh
        prev = jnp.where(first, mc_ref[8:16, :] * mh_ref[8:16, :], ccp_ref[...] * chp_ref[...])
        cc1, cc2 = _shift_rows(cc, prev, ext)
        w0, w1, w2 = cw_ref[0:1, :], cw_ref[1:2, :], cw_ref[2:3, :]
        dw = w0 * cc2 + w1 * cc1 + w2 * cc
        yc = cb * dw
        rg = lax.rsqrt(_group_mean(yc * yc, gm_ref[...]) + EPS)
        ych = yc * rg
        gcv = gc_ref[...]
        sg = _sigmoid(zc)
        dycn = dy * (zc * sg)
        dzc = dy * (ych * gcv) * (sg * (1.0 + zc * (1.0 - sg)))
        dgc_ref[...] += jnp.sum((dycn * ych)[:tm], axis=0, keepdims=True)
        dycg = dycn * gcv
        dyc = rg * (dycg - ych * _group_mean(dycg * ych, gm_ref[...]))
        rid = lax.broadcasted_iota(jnp.int32, (ext, CONV_W), 0)
        ddw = jnp.where(jnp.logical_and(last, rid >= tm), 0.0, dyc * cb)
        dcb = dyc * dw
        dcc = w2 * ddw + w1 * pltpu.roll(ddw, ext - 1, 0) + w0 * pltpu.roll(ddw, ext - 2, 0)
        dpb_ref[:, 512:1024] = dcb[:tm].astype(BF16)
        dpb_ref[:, 1024:1536] = (dcc * chh)[:tm].astype(BF16)
        dpb_ref[:, 1536:2048] = (dcc * cch)[:tm].astype(BF16)
        dpb_ref[:, 2048:2560] = dzc[:tm].astype(BF16)
        rs = lambda a: jnp.sum(a[:tm], axis=0, keepdims=True)
        dcw_ref[0:1, :] += rs(ddw * cc2)
        dcw_ref[1:2, :] += rs(ddw * cc1)
        dcw_ref[2:3, :] += rs(ddw * cc)

        @pl.when(first)
        def _():
            d0, d1 = ddw[0:1, :], ddw[1:2, :]
            r8 = lax.broadcasted_iota(jnp.int32, (8, CONV_W), 0)
            dccm_ref[0] = jnp.where(r8 == 7, w1 * d0 + w0 * d1, jnp.where(r8 == 6, w0 * d0, 0.0))

    row = lambda j: pl.BlockSpec((tm, 512), lambda i: (i, j))
    prv = lambda j: pl.BlockSpec((8, 512), lambda i: (prev_idx(i), j))
    nxt = lambda j: pl.BlockSpec((8, 512), lambda i: (next_idx(i), j))
    mblk = lambda j: pl.BlockSpec((N_META, 512), lambda i: (0, j))
    full = lambda a: pl.BlockSpec(a.shape, lambda i: (0,) * a.ndim)
    hb = lambda w: pl.BlockSpec((1, HEADS, tm, w), lambda i: (i // nt, 0, i % nt, 0))
    acc = lambda rr: pl.BlockSpec((rr, 512), lambda i: (0, 0))
    return pl.pallas_call(
        body, name="gate_bwd", grid=(nb * nt,),
        in_specs=[row(0), row(1), nxt(1), hb(VDIM),
                  row(BLK_ZA), row(BLK_CB), nxt(BLK_CB), row(BLK_CC), prv(BLK_CC), nxt(BLK_CC),
                  row(BLK_CH), prv(BLK_CH), nxt(BLK_CH), row(BLK_ZC), nxt(BLK_ZC),
                  mblk(BLK_CC), mblk(BLK_CH), full(conv_w), full(ga), full(gc), full(gmat)],
        out_specs=[pl.BlockSpec((tm, 2560), lambda i: (i, 0)), hb(VDIM),
                   pl.BlockSpec((1, HEADS, 1, tm), lambda i: (i // nt, 0, 0, i % nt)),
                   pl.BlockSpec((1, 8, 512), lambda i: (i // nt, 0, 0)),
                   acc(1), acc(1), acc(8)],
        out_shape=[jax.ShapeDtypeStruct((r, 2560), BF16), jax.ShapeDtypeStruct((nb, HEADS, s, VDIM), BF16),
                   jax.ShapeDtypeStruct((nb, HEADS, 1, s), F32), jax.ShapeDtypeStruct((nb, 8, 512), F32),
                   jax.ShapeDtypeStruct((1, 512), F32), jax.ShapeDtypeStruct((1, 512), F32),
                   jax.ShapeDtypeStruct((8, 512), F32)],
        compiler_params=_cparams("arbitrary"),
    )(dycat, dycat, dycat, o, p, p, p, p, p, p, p, p, p, p, p, pm, pm, conv_w, ga, gc, gmat)


class _StagedReduce:
    LOC, PRE_S, PRE_R, ICI_S, ICI_R, POST_S, POST_R, OUT, N_SEM = 0, 1, 2, 3, 6, 9, 10, 11, 12

    def __init__(self, shard_shape):
        self.half = (shard_shape[0] // 2, shard_shape[1])

    def scratch(self):
        h = self.half
        return [pltpu.VMEM((4,) + h, F32), pltpu.VMEM((4,) + h, F32), pltpu.VMEM((4,) + h, BF16),
                pltpu.VMEM((3,) + h, BF16), pltpu.VMEM(h, F32), pltpu.SemaphoreType.DMA((self.N_SEM,))]

    def run(self, stage, pin, gout, scr):
        own, sib, wire, rbuf, fin, sems = scr
        r2 = self.half[0]
        x, y, c = lax.axis_index("x"), lax.axis_index("y"), lax.axis_index("c")
        mine = 2 * x + y
        sibling = (x, y, 1 - c)
        chips = [(1 - x, y), (x, 1 - y), (1 - x, 1 - y)]
        rows = lambda half: pl.ds(pl.multiple_of(half * r2, r2), r2)
        mesh = pl.DeviceIdType.MESH

        loc = pltpu.make_async_copy(pin.at[:, rows(c), :], own, sems.at[self.LOC])
        pre = pltpu.make_async_remote_copy(
            src_ref=pin.at[:, rows(1 - c), :], dst_ref=sib, send_sem=sems.at[self.PRE_S],
            recv_sem=sems.at[self.PRE_R], device_id=sibling, device_id_type=mesh)

        def ici(j):
            px, py = chips[j]
            return pltpu.make_async_remote_copy(
                src_ref=wire.at[2 * px + py], dst_ref=rbuf.at[j], send_sem=sems.at[self.ICI_S + j],
                recv_sem=sems.at[self.ICI_R + j], device_id=(px, py, c), device_id_type=mesh)

        def post(half):
            return pltpu.make_async_remote_copy(
                src_ref=fin, dst_ref=gout.at[rows(half), :], send_sem=sems.at[self.POST_S],
                recv_sem=sems.at[self.POST_R], device_id=sibling, device_id_type=mesh)

        keep = pltpu.make_async_copy(fin, gout.at[rows(c), :], sems.at[self.OUT])
        if stage == 0:
            loc.start()
            pre.start()
        elif stage == 1:
            loc.wait()
            pre.wait_recv()
            for blk in range(4):
                tot = own[blk] + sib[blk]
                own[blk] = tot
                wire[blk] = tot.astype(BF16)
            for j in range(3):
                ici(j).start()
        elif stage == 2:
            for j in range(3):
                ici(j).wait_recv()
            tot = own[mine]
            for j in range(3):
                tot = tot + rbuf[j].astype(F32)
            fin[...] = tot
            post(c).start()
            keep.start()
        else:
            post(1 - c).wait_recv()
            pre.wait_send()
            for j in range(3):
                ici(j).wait_send()
            post(c).wait_send()
            keep.wait()


def _attn_bwd(q, k, v, do, lse, delta, km, vm, early, nb, s, t):
    n = s // t
    ne = len(early)
    reds = [_StagedReduce(a.shape[1:]) for a in early]
    n_steps = HEADS * nb
    assert n_steps >= 4

    def body(q_ref, k_ref, v_ref, do_ref, lse_ref, dl_ref, km_ref, vm_ref, *rest):
        pin_refs, rest = rest[:ne], rest[ne:]
        dq_ref, dk_ref, dv_ref, dkm_ref, dvm_ref = rest[:5]
        gout_refs, (p_scr, ds_scr, dq_acc), red_scr = rest[5:5 + ne], rest[5 + ne:8 + ne], rest[8 + ne:]
        b = pl.program_id(1)
        step = pl.program_id(0) * nb + b
        for stage, at in enumerate((0, 1, n_steps - 2, n_steps - 1)):
            @pl.when(step == at)
            def _(stage=stage):
                for a, red in enumerate(reds):
                    red.run(stage, pin_refs[a], gout_refs[a], red_scr[6 * a:6 * a + 6])

        @pl.when(b == 0)
        def _():
            dkm_ref[...] = jnp.zeros_like(dkm_ref)
            dvm_ref[...] = jnp.zeros_like(dvm_ref)

        kr = lax.broadcasted_iota(jnp.int32, (t, t), 0)
        qc = lax.broadcasted_iota(jnp.int32, (t, t), 1)
        km_v, vm_v = km_ref[0, 0], vm_ref[0, 0]
        ptm = jnp.exp(_dot_nt(km_v, q_ref[0, 0]) - lse_ref[0, 0])
        dstm = (ptm * (_dot_nt(vm_v, do_ref[0, 0]) - dl_ref[0, 0])).astype(BF16)
        dkm_ref[0] += _dot(dstm, q_ref[0, 0])
        dvm_ref[0] += _dot(ptm.astype(BF16), do_ref[0, 0])
        dq_acc[...] = _dot_tn(dstm, km_v)
        for j in range(n):
            slot = j % 2
            kj = k_ref[0, 0, j * t:(j + 1) * t, :]
            vj = v_ref[0, 0, j * t:(j + 1) * t, :]
            for i in range(j, n):
                cs = slice(i * t, (i + 1) * t)
                qi = q_ref[0, 0, cs, :]
                doi = do_ref[0, 0, cs, :]
                st = _dot_nt(kj, qi)
                if i == j:
                    st = jnp.where(kr <= qc, st, NEG_INF)
                pt = jnp.exp(st - lse_ref[0, 0, :, cs])
                dst = (pt * (_dot_nt(vj, doi) - dl_ref[0, 0, :, cs])).astype(BF16)
                p_scr[slot, :, cs] = pt.astype(BF16)
                ds_scr[slot, :, cs] = dst
                dq_acc[cs, :] += _dot_tn(dst, kj)
            dv_ref[0, 0, j * t:(j + 1) * t, :] = _dot(p_scr[slot, :, j * t:s], do_ref[0, 0, j * t:s, :]).astype(BF16)
            dk_ref[0, 0, j * t:(j + 1) * t, :] = _dot(ds_scr[slot, :, j * t:s], q_ref[0, 0, j * t:s, :]).astype(BF16)
        dq_ref[0, 0] = dq_acc[...].astype(BF16)

    big = lambda w: pl.BlockSpec((1, 1, s, w), lambda h, b: (b, h, 0, 0))
    rowv = pl.BlockSpec((1, 1, 1, s), lambda h, b: (b, h, 0, 0))
    mk = lambda w: pl.BlockSpec((1, 1, N_META, w), lambda h, b: (0, h, 0, 0))
    mo = lambda w: pl.BlockSpec((1, N_META, w), lambda h, b: (h, 0, 0))
    return pl.pallas_call(
        body, name="attn_bwd", grid=(HEADS, nb),
        in_specs=[big(QK_PAD), big(QK_PAD), big(VDIM), big(VDIM), rowv, rowv, mk(QK_PAD), mk(VDIM)]
        + [pl.BlockSpec(memory_space=pl.ANY)] * ne,
        out_specs=[big(QK_PAD), big(QK_PAD), big(VDIM), mo(QK_PAD), mo(VDIM)]
        + [pl.BlockSpec(memory_space=pl.ANY)] * ne,
        out_shape=[jax.ShapeDtypeStruct((nb, HEADS, s, QK_PAD), BF16),
                   jax.ShapeDtypeStruct((nb, HEADS, s, QK_PAD), BF16),
                   jax.ShapeDtypeStruct((nb, HEADS, s, VDIM), BF16),
                   jax.ShapeDtypeStruct((HEADS, N_META, QK_PAD), F32),
                   jax.ShapeDtypeStruct((HEADS, N_META, VDIM), F32)]
        + [jax.ShapeDtypeStruct(a.shape[1:], F32) for a in early],
        scratch_shapes=[pltpu.VMEM((2, t, s), BF16), pltpu.VMEM((2, t, s), BF16), pltpu.VMEM((s, QK_PAD), F32)]
        + [sc for red in reds for sc in red.scratch()],
        compiler_params=_cparams("arbitrary", "arbitrary"),
    )(q, k, v, do, lse, delta, km, vm, *early)


def _up_bwd(dq, dk, dv, dkm, dvm, p, pm, tabs, tabs_m, wq_p, wkv_p, gq, gkv, nb, s, tm):
    nt = s // tm
    n = nb * nt
    c_t, sa_t, sb_t = tabs
    cm_t, sam_t, sbm_t = tabs_m

    def kv_path(dkh, dvh, pa, c, sa, sb, wkv, gkvv):
        dkpe = dkh[0][:, NOPE:]
        for h in range(1, HEADS):
            dkpe = dkpe + dkh[h][:, NOPE:]
        dkr = _rope_bwd(dkpe, c, sa, sb)
        dkv = jnp.concatenate([d[:, :NOPE] for d in dkh] + list(dvh), axis=1).astype(BF16)
        ckv = pa[:, Q_RANK:Q_RANK + KV_RANK]
        kvn, rkv = _rms(ckv, gkvv)
        dckv, dg = _rms_bwd(_dot(dkv, wkv), ckv, rkv, gkvv)
        return dckv, dkr, kvn.astype(BF16), dkv, jnp.sum(dg, axis=0, keepdims=True)

    def body(dq_ref, dk_ref, dv_ref, pa_ref, c_ref, sa_ref, sb_ref,
             dkm_ref, dvm_ref, pam_ref, cm_ref, sam_ref, sbm_ref,
             wq_ref, wkv_ref, gq_ref, gkv_ref,
             dpa_ref, dpam_ref, pq_ref, pkv_ref, dgq_ref, dgkv_ref, dwq_ref, dwkv_ref):
        i = pl.program_id(0)

        @pl.when(i == 0)
        def _():
            dwq_ref[...] = jnp.zeros_like(dwq_ref)
            dwkv_ref[...] = jnp.zeros_like(dwkv_ref)
            dgq_ref[...] = jnp.zeros_like(dgq_ref)
            dgkv_ref[...] = jnp.zeros_like(dgkv_ref)

        @pl.when(i < n)
        def _():
            c, sa, sb = c_ref[...], sa_ref[...], sb_ref[...]
            pa = pa_ref[...]
            parts = []
            for h in range(HEADS):
                dqh = dq_ref[0, h].astype(F32) * ATTN_SCALE
                parts += [dqh[:, :NOPE], _rope_bwd(dqh[:, NOPE:], c, sa, sb)]
            dql = jnp.concatenate(parts, axis=1).astype(BF16)
            cq = pa[:, 0:Q_RANK]
            gqv = gq_ref[...]
            qn, rq = _rms(cq, gqv)
            dwq_ref[...] += _dot_tn(dql, qn.astype(BF16))
            dcq, dg = _rms_bwd(_dot(dql, wq_ref[...]), cq, rq, gqv)
            dgq_ref[...] += jnp.sum(dg, axis=0, keepdims=True)
            dckv, dkr, kvn, dkv, dgk = kv_path([dk_ref[0, h].astype(F32) for h in range(HEADS)],
                                               [dv_ref[0, h].astype(F32) for h in range(HEADS)],
                                               pa, c, sa, sb, wkv_ref[...], gkv_ref[...])
            dwkv_ref[...] += _dot_tn(dkv, kvn)
            dgkv_ref[...] += dgk
            dpa_ref[...] = jnp.concatenate([dcq, dckv, dkr], axis=1).astype(BF16)

        @pl.when(i == n)
        def _():
            dckv, dkr, kvn, dkv, dgk = kv_path([dkm_ref[h] for h in range(HEADS)],
                                               [dvm_ref[h] for h in range(HEADS)],
                                               pam_ref[...], cm_ref[...], sam_ref[...], sbm_ref[...],
                                               wkv_ref[...], gkv_ref[...])
            dwkv_ref[...] += _dot_tn(dkv, kvn)
            dgkv_ref[...] += dgk
            dpam_ref[...] = jnp.concatenate([jnp.zeros((N_META, Q_RANK), F32), dckv, dkr], axis=1)
            for h in range(HEADS):
                pq_ref[h] = dwq_ref[QK_PAD * h:QK_PAD * h + NOPE + ROPE, :]
                pkv_ref[h, 0:NOPE, :] = dwkv_ref[NOPE * h:NOPE * (h + 1), :]
                pkv_ref[h, NOPE:NOPE + VDIM, :] = dwkv_ref[512 + VDIM * h:512 + VDIM * (h + 1), :]

    cl = lambda i: jnp.minimum(i, n - 1)
    hb = lambda w: pl.BlockSpec((1, HEADS, tm, w), lambda i: (cl(i) // nt, 0, cl(i) % nt, 0))
    tab = pl.BlockSpec((tm, 128), lambda i: (cl(i) % nt, 0))
    full = lambda a: pl.BlockSpec(a.shape, lambda i: (0,) * a.ndim)
    const = lambda shape: pl.BlockSpec(shape, lambda i: (0,) * len(shape))
    return pl.pallas_call(
        body, name="up_bwd", grid=(n + 1,),
        in_specs=[hb(QK_PAD), hb(QK_PAD), hb(VDIM), pl.BlockSpec((tm, 512), lambda i: (cl(i), 0)), tab, tab, tab,
                  full(dkm), full(dvm), pl.BlockSpec((N_META, 512), lambda i: (0, 0)),
                  full(cm_t), full(sam_t), full(sbm_t), full(wq_p), full(wkv_p), full(gq), full(gkv)],
        out_specs=[pl.BlockSpec((tm, 512), lambda i: (cl(i), 0)), const((N_META, 512)),
                   const((HEADS, NOPE + ROPE, Q_RANK)), const((HEADS, NOPE + VDIM, KV_RANK)),
                   const((1, Q_RANK)), const((1, KV_RANK))],
        out_shape=[jax.ShapeDtypeStruct((nb * s, 512), BF16), jax.ShapeDtypeStruct((N_META, 512), F32),
                   jax.ShapeDtypeStruct((HEADS, NOPE + ROPE, Q_RANK), F32),
                   jax.ShapeDtypeStruct((HEADS, NOPE + VDIM, KV_RANK), F32),
                   jax.ShapeDtypeStruct((1, Q_RANK), F32), jax.ShapeDtypeStruct((1, KV_RANK), F32)],
        scratch_shapes=[pltpu.VMEM((HEADS * QK_PAD, Q_RANK), F32), pltpu.VMEM((1024, KV_RANK), F32)],
        compiler_params=_cparams("arbitrary"),
    )(dq, dk, dv, p, c_t, sa_t, sb_t, dkm, dvm, pm, cm_t, sam_t, sbm_t, wq_p, wkv_p, gq, gkv)


def _meta_dp(dpam_ref, dccm_ref, mc_ref, mh_ref, nb):
    dcc = dccm_ref[0]
    for b in range(1, nb):
        dcc = dcc + dccm_ref[b]
    z8 = jnp.zeros((8, CONV_W), F32)
    dc = jnp.concatenate([z8, dcc * mh_ref[8:16, :]], axis=0)
    dh = jnp.concatenate([z8, dcc * mc_ref[8:16, :]], axis=0)
    z = jnp.zeros((N_META, CONV_W), F32)
    return jnp.concatenate([dpam_ref[...], z, z, dc, dh, z], axis=1).astype(BF16)


def _in_bwd_dw(x2d, dpa, dpb, meta, dpam, dccm, pm, norm_g, nb, s, tm):
    nt = s // tm
    n = nb * nt

    def body(x_ref, dpa_ref, dpb_ref, mt_ref, dpam_ref, dccm_ref, mc_ref, mh_ref, g_ref, dw_hbm, acc_ref, sems):
        i = pl.program_id(0)

        @pl.when(i == 0)
        def _():
            acc_ref[...] = jnp.zeros_like(acc_ref)

        def rows(x, dp):
            u, _ = _rms(x, g_ref[...])
            acc_ref[...] += _dot_tn(dp, u.astype(BF16))

        @pl.when(i < n)
        def _():
            rows(x_ref[...], jnp.concatenate([dpa_ref[...], dpb_ref[...]], axis=1))

        @pl.when(i == n)
        def _():
            rows(mt_ref[...], _meta_dp(dpam_ref, dccm_ref, mc_ref, mh_ref, nb))
            per = IN_DIM // 4
            cps = [pltpu.make_async_copy(acc_ref.at[0:448], dw_hbm.at[0, 0:448], sems.at[0]),
                   pltpu.make_async_copy(acc_ref.at[512:per + 64], dw_hbm.at[0, 448:per], sems.at[1])]
            for qq in range(1, 4):
                cps.append(pltpu.make_async_copy(acc_ref.at[per * qq + 64:per * (qq + 1) + 64], dw_hbm.at[qq],
                                                 sems.at[qq + 1]))
            for cp in cps:
                cp.start()
            for cp in cps:
                cp.wait()

    cl = lambda i: jnp.minimum(i, n - 1)
    row = lambda w: pl.BlockSpec((tm, w), lambda i: (cl(i), 0))
    full = lambda a: pl.BlockSpec(a.shape, lambda i: (0,) * a.ndim)
    mblk = lambda j: pl.BlockSpec((N_META, 512), lambda i: (0, j))
    return pl.pallas_call(
        body, name="in_bwd_dw", grid=(n + 1,),
        in_specs=[row(D_MODEL), row(512), row(2560), full(meta), full(dpam), full(dccm),
                  mblk(BLK_CC), mblk(BLK_CH), full(norm_g)],
        out_specs=pl.BlockSpec(memory_space=pl.ANY),
        out_shape=jax.ShapeDtypeStruct((4, IN_DIM // 4, D_MODEL), F32),
        scratch_shapes=[pltpu.VMEM((IN_PAD, D_MODEL), F32), pltpu.SemaphoreType.DMA((5,))],
        compiler_params=_cparams("arbitrary"),
    )(x2d, dpa, dpb, meta, dpam, dccm, pm, pm, norm_g)


def _in_bwd_dx(x2d, dh2, dpa, dpb, meta, dpam, dccm, pm, w_in_p, norm_g, late, nb, s, tm):
    nt = s // tm
    n = nb * nt
    n_steps = n + 1
    ne = len(late)
    reds = [_StagedReduce(a.shape[1:]) for a in late]
    assert n_steps >= 4

    def body(x_ref, dh_ref, dpa_ref, dpb_ref, mt_ref, dpam_ref, dccm_ref, mc_ref, mh_ref, w_ref, g_ref, *rest):
        pin_refs, (gx_ref, gm_ref, dg_ref) = rest[:ne], rest[ne:ne + 3]
        gout_refs, red_scr = rest[ne + 3:2 * ne + 3], rest[2 * ne + 3:]
        i = pl.program_id(0)
        for stage, at in enumerate((0, 1, n_steps - 2, n_steps - 1)):
            @pl.when(i == at)
            def _(stage=stage):
                for a, red in enumerate(reds):
                    red.run(stage, pin_refs[a], gout_refs[a], red_scr[6 * a:6 * a + 6])

        @pl.when(i == 0)
        def _():
            dg_ref[...] = jnp.zeros_like(dg_ref)

        def rows(x, dp, dres):
            g = g_ref[...]
            _, r1 = _rms(x, g)
            dx, dg = _rms_bwd(_dot(dp, w_ref[...]), x, r1, g)
            dg_ref[...] += jnp.sum(dg, axis=0, keepdims=True)
            return dx if dres is None else dx + dres

        @pl.when(i < n)
        def _():
            gx_ref[...] = rows(x_ref[...], jnp.concatenate([dpa_ref[...], dpb_ref[...]], axis=1), dh_ref[...])

        @pl.when(i == n)
        def _():
            gm_ref[...] = rows(mt_ref[...], _meta_dp(dpam_ref, dccm_ref, mc_ref, mh_ref, nb), None)

    cl = lambda i: jnp.minimum(i, n - 1)
    row = lambda w: pl.BlockSpec((tm, w), lambda i: (cl(i), 0))
    full = lambda a: pl.BlockSpec(a.shape, lambda i: (0,) * a.ndim)
    mblk = lambda j: pl.BlockSpec((N_META, 512), lambda i: (0, j))
    hbm = pl.BlockSpec(memory_space=pl.ANY)
    return pl.pallas_call(
        body, name="in_bwd_dx", grid=(n_steps,),
        in_specs=[row(D_MODEL), row(D_MODEL), row(512), row(2560), full(meta), full(dpam), full(dccm),
                  mblk(BLK_CC), mblk(BLK_CH), full(w_in_p), full(norm_g)] + [hbm] * ne,
        out_specs=[row(D_MODEL), pl.BlockSpec((N_META, D_MODEL), lambda i: (0, 0)),
                   pl.BlockSpec((1, D_MODEL), lambda i: (0, 0))] + [hbm] * ne,
        out_shape=[jax.ShapeDtypeStruct((nb * s, D_MODEL), F32), jax.ShapeDtypeStruct((N_META, D_MODEL), F32),
                   jax.ShapeDtypeStruct((1, D_MODEL), F32)]
        + [jax.ShapeDtypeStruct(a.shape[1:], F32) for a in late],
        scratch_shapes=[sc for red in reds for sc in red.scratch()],
        compiler_params=_cparams("arbitrary"),
    )(x2d, dh2, dpa, dpb, meta, dpam, dccm, pm, pm, w_in_p, norm_g, *late)


def _gather_weights(split, pieces, out_rows, whole, zero_fills):
    ns, nw, nz = len(split), len(whole), len(zero_fills)
    flat = [(a, pc) for a in range(ns) for pc in pieces[a]]
    nk = len(flat)

    def body(*refs):
        ins, wins, zins = refs[:ns], refs[ns:ns + nw], refs[ns + nw:ns + nw + nz]
        outs, wouts = refs[ns + nw + nz:2 * ns + nw + nz], refs[2 * ns + nw + nz:2 * (ns + nw) + nz]
        send_sems, recv_sems, fwd_send, fwd_recv, loc_sems, w_send, w_recv, w_loc, z_sems = refs[2 * (ns + nw) + nz:]
        x, y, c = lax.axis_index("x"), lax.axis_index("y"), lax.axis_index("c")
        mine = 2 * x + y
        chips = [(1 - x, y), (x, 1 - y), (1 - x, 1 - y)]
        chip_of = [2 * px + py for px, py in chips]

        def src(k):
            a, (s0, nr, _, _, _, _) = flat[k]
            return ins[a].at[s0:s0 + nr]

        def dst(k, q):
            a, (_, nr, per, first, rest, _) = flat[k]
            row = per * q + first + (rest - first) * jnp.minimum(q, 1)
            return outs[a].at[pl.ds(pl.multiple_of(row, 16), nr)]

        def ici(k, j, q):
            px, py = chips[j]
            return pltpu.make_async_remote_copy(
                src_ref=src(k), dst_ref=dst(k, q), send_sem=send_sems.at[k, j], recv_sem=recv_sems.at[k, j],
                device_id=(px, py, c), device_id_type=pl.DeviceIdType.MESH)

        def fwd(k, j):
            ref = dst(k, chip_of[j])
            return pltpu.make_async_remote_copy(
                src_ref=ref, dst_ref=ref, send_sem=fwd_send.at[k, j], recv_sem=fwd_recv.at[k, j],
                device_id=(x, y, 1 - c), device_id_type=pl.DeviceIdType.MESH)

        def wcopy(b, j, q):
            px, py = chips[j]
            return pltpu.make_async_remote_copy(
                src_ref=wins[b], dst_ref=wouts[b].at[q], send_sem=w_send.at[b, j], recv_sem=w_recv.at[b, j],
                device_id=(px, py, c), device_id_type=pl.DeviceIdType.MESH)

        local = [pltpu.make_async_copy(src(k), dst(k, mine), loc_sems.at[k]) for k in range(nk)]
        local += [pltpu.make_async_copy(wins[b], wouts[b].at[mine], w_loc.at[b]) for b in range(nw)]
        for z, (a, _, row0) in enumerate(zero_fills):
            local.append(pltpu.make_async_copy(zins[z], outs[a].at[row0:row0 + zins[z].shape[0]], z_sems.at[z]))
        wsends = [wcopy(b, j, mine) for b in range(nw) for j in range(3)]
        for cp in local + wsends:
            cp.start()

        for half in (0, 1):
            @pl.when(c == half)
            def _(half=half):
                my_k = [k for k in range(nk) if flat[k][1][5] == half]
                other_k = [k for k in range(nk) if flat[k][1][5] != half]
                sends = [ici(k, j, mine) for k in my_k for j in range(3)]
                for cp in sends:
                    cp.start()
                passed = []
                for k in my_k:
                    for j in range(3):
                        ici(k, j, chip_of[j]).wait_recv()
                        cp = fwd(k, j)
                        cp.start()
                        passed.append(cp)
                for k in other_k:
                    for j in range(3):
                        fwd(k, j).wait_recv()
                for cp in sends + passed:
                    cp.wait_send()

        for b in range(nw):
            for j in range(3):
                wcopy(b, j, chip_of[j]).wait_recv()
        for cp in wsends:
            cp.wait_send()
        for cp in local:
            cp.wait()

    hbm = pl.BlockSpec(memory_space=pl.ANY)
    dma = pltpu.SemaphoreType.DMA
    zeros = [z for _, z, _ in zero_fills]
    return pl.pallas_call(
        body, name="gather_weights",
        in_specs=[hbm] * (ns + nw + nz), out_specs=[hbm] * (ns + nw),
        out_shape=([jax.ShapeDtypeStruct((out_rows[a], split[a].shape[1]), split[a].dtype) for a in range(ns)]
                   + [jax.ShapeDtypeStruct((4,) + w.shape, w.dtype) for w in whole]),
        scratch_shapes=[dma((nk, 3)), dma((nk, 3)), dma((nk, 3)), dma((nk, 3)), dma((nk,)),
                        dma((nw, 3)), dma((nw, 3)), dma((nw,)), dma((nz,))],
        compiler_params=pltpu.CompilerParams(vmem_limit_bytes=VMEM_LIMIT),
    )(*split, *whole, *zeros)


def _allreduce_small(small):
    def body(sm_in, sm_out, sbuf, sm_send, sm_recv):
        x, y, c = lax.axis_index("x"), lax.axis_index("y"), lax.axis_index("c")
        me = 4 * x + 2 * y + c

        def small_copy(kk):
            peer = (x ^ (kk >> 2), y ^ ((kk >> 1) & 1), c ^ (kk & 1))
            return pltpu.make_async_remote_copy(
                src_ref=sm_in, dst_ref=sbuf.at[kk], send_sem=sm_send.at[kk - 1], recv_sem=sm_recv.at[kk - 1],
                device_id=peer, device_id_type=pl.DeviceIdType.MESH)

        smalls = [small_copy(kk) for kk in range(1, 8)]
        for cp in smalls:
            cp.start()
        sbuf[0] = sm_in[...]
        for cp in smalls:
            cp.wait_recv()
        total = sbuf[me]
        for d in range(1, 8):
            total = total + sbuf[me ^ d]
        sm_out[...] = total
        for cp in smalls:
            cp.wait_send()

    vmem = pl.BlockSpec(memory_space=pltpu.VMEM)
    dma = pltpu.SemaphoreType.DMA
    return pl.pallas_call(
        body, name="allreduce_small", in_specs=[vmem], out_specs=vmem,
        out_shape=jax.ShapeDtypeStruct(small.shape, F32),
        scratch_shapes=[pltpu.VMEM((8,) + small.shape, F32), dma((7,)), dma((7,))],
        compiler_params=pltpu.CompilerParams(vmem_limit_bytes=VMEM_LIMIT),
    )(small)


def _adamw(w, g, m, v, name):
    shape = w.shape
    w2, g2, m2, v2 = (a.reshape((-1, shape[-1])) for a in (w, g, m, v))

    def body(w_ref, g_ref, m_ref, v_ref, d_ref, nm_ref, nv_ref):
        gv = g_ref[...]
        nm = ADAM_B1 * m_ref[...] + (1.0 - ADAM_B1) * gv
        nv = ADAM_B2 * v_ref[...] + (1.0 - ADAM_B2) * (gv * gv)
        m_hat = nm / (1.0 - ADAM_B1 ** ADAM_STEP)
        v_hat = nv / (1.0 - ADAM_B2 ** ADAM_STEP)
        d_ref[...] = -ADAM_LR * (m_hat / (jnp.sqrt(v_hat) + ADAM_EPS) + ADAM_WD * w_ref[...])
        nm_ref[...] = nm
        nv_ref[...] = nv

    rows, cols = w2.shape
    nblk = cols // 256 if cols % 256 == 0 and rows >= 64 else 1
    blk = pl.BlockSpec((rows, cols // nblk), lambda j: (0, j))
    out = pl.pallas_call(
        body, name=name, grid=(nblk,), in_specs=[blk] * 4, out_specs=[blk] * 3,
        out_shape=[jax.ShapeDtypeStruct(w2.shape, F32)] * 3,
        compiler_params=_cparams("parallel"),
    )(w2, g2, m2, v2)
    return tuple(a.reshape(shape) for a in out)


def kernel(x, meta_tokens, norm_g, w_in, q_norm_g, w_q_up, kv_norm_g, w_kv_up, conv_w, attn_out_g, conv_out_g, w_out, final_norm_g, loss_target, m_meta_tokens, m_norm_g, m_w_in, m_q_norm_g, m_w_q_up, m_kv_norm_g, m_w_kv_up, m_conv_w, m_attn_out_g, m_conv_out_g, m_w_out, m_final_norm_g, v_meta_tokens, v_norm_g, v_w_in, v_q_norm_g, v_w_q_up, v_kv_norm_g, v_w_kv_up, v_conv_w, v_attn_out_g, v_conv_out_g, v_w_out, v_final_norm_g):
    nb, s, _ = x.shape
    tm = min(ROW_TILE, s)
    ta = min(ATTN_TILE, s)
    assert s % tm == 0 and s % ta == 0 and tm % 16 == 0
    r = nb * s

    tr = lambda a: jnp.transpose(a[0])
    w_in_p, wq_p, wkv_p, g_cw, g_meta = _gather_weights(
        [tr(w_in).astype(BF16), tr(w_q_up).astype(BF16), tr(w_kv_up).astype(BF16)],
        [W_IN_PIECES, W_Q_PIECES, W_KV_PIECES], [IN_PAD, HEADS * QK_PAD, 1024],
        [conv_w[0], meta_tokens],
        [(0, jnp.zeros((64, D_MODEL), BF16), 448)]
        + [(1, jnp.zeros((64, Q_RANK), BF16), QK_PAD * h + NOPE + ROPE) for h in range(HEADS)])
    conv_f = jnp.transpose(g_cw, (1, 0, 2)).reshape(3, CONV_W)
    meta_f = jnp.transpose(g_meta, (1, 0, 2)).reshape(N_META, D_MODEL)

    c_all, sa_all, sb_all = _rope_tables(N_META + s)
    tabs_m = (c_all[:N_META], sa_all[:N_META], sb_all[:N_META])
    tabs = (c_all[N_META:], sa_all[N_META:], sb_all[N_META:])
    gid = np.arange(CONV_W) // CONV_GROUP
    gmat = jnp.asarray(np.where(gid[:, None] == gid[None, :], 1.0 / CONV_GROUP, 0.0), BF16)
    ga, gc = attn_out_g, conv_out_g
    gf = final_norm_g.reshape(1, D_MODEL)

    x2d = x.reshape(r, D_MODEL)
    tgt2d = loss_target.reshape(r, D_MODEL)

    p, q, k, v, pm, km, vm, w_out_f = _fwd_proj(x2d, meta_f, tabs, tabs_m, norm_g, w_in_p, q_norm_g, wq_p,
                                                kv_norm_g, wkv_p, w_out[0].astype(BF16), nb, s, tm)
    o, lse = _attn_fwd(q, k, v, km, vm, nb, s, ta)
    dh2, dycat, dw_out, dgf, loss_acc = _out_fwd_bwd(x2d, tgt2d, o, p, pm, conv_f, ga, gc, gmat, w_out_f, gf,
                                                     nb, s, tm)
    dpb, do, delta, dccm, dga, dgc, dcw = _gate_bwd(dycat, o, p, pm, conv_f, ga, gc, gmat, nb, s, tm)
    p_out = dw_out.reshape(4, D_MODEL // 4, D_MODEL)
    dq, dk, dv, dkm, dvm, g_w_out = _attn_bwd(q, k, v, do, lse, delta, km, vm, [p_out], nb, s, ta)
    dpa, dpam, p_q, p_kv, dgq, dgkv = _up_bwd(dq, dk, dv, dkm, dvm, p, pm, tabs, tabs_m, wq_p, wkv_p,
                                              q_norm_g, kv_norm_g, nb, s, tm)
    p_in = _in_bwd_dw(x2d, dpa, dpb, meta_f, dpam, dccm, pm, norm_g, nb, s, tm)
    gx, gmeta, dng, g_w_in_t, g_w_q_t, g_w_kv_t = _in_bwd_dx(x2d, dh2, dpa, dpb, meta_f, dpam, dccm, pm, w_in_p,
                                                            norm_g, [p_in, p_q, p_kv], nb, s, tm)

    flat =jnp.concatenate([dng.reshape(-1), dgq.reshape(-1), dgkv.reshape(-1), dga.reshape(-1), dgc.reshape(-1),
                            dgf.reshape(-1), dcw[:3].reshape(-1), gmeta.reshape(-1), loss_acc[0, 0:1]])
    n_small = flat.shape[0]
    rows_small = -(-n_small // 1024) * 8
    small = jnp.pad(flat, (0, rows_small * 128 - n_small)).reshape(rows_small, 128)
    small_sum = _allreduce_small(small)
    ssum = small_sum.reshape(-1)

    def take(off, n):
        return ssum[off:off + n], off + n

    off = 0
    g_norm, off = take(off, D_MODEL)
    g_qn, off = take(off, Q_RANK)
    g_kvn, off = take(off, KV_RANK)
    g_ga, off = take(off, CONV_W)
    g_gc, off = take(off, CONV_W)
    g_gf, off = take(off, D_MODEL)
    g_cw_all, off = take(off, 3 * CONV_W)
    g_meta_all, off = take(off, N_META * D_MODEL)
    loss = ssum[off]
    chip = 2 * lax.axis_index("x") + lax.axis_index("y")
    g_conv = lax.dynamic_slice(g_cw_all.reshape(3, CONV_W), (0, chip * 128), (3, 128))
    g_mt = lax.dynamic_slice(g_meta_all.reshape(N_META, D_MODEL), (0, chip * 256), (N_META, 256))

    grads = {
        "meta_tokens": g_mt, "norm_g": g_norm.reshape(1, -1), "w_in": g_w_in_t, "q_norm_g": g_qn.reshape(1, -1),
        "w_q_up": g_w_q_t, "kv_norm_g": g_kvn.reshape(1, -1), "w_kv_up": jnp.transpose(g_w_kv_t)[None],
        "conv_w": g_conv[None], "attn_out_g": g_ga.reshape(1, -1), "conv_out_g": g_gc.reshape(1, -1),
        "w_out": g_w_out[None], "final_norm_g": g_gf,
    }
    transposed = ("w_in", "w_q_up")
    weights = {
        "meta_tokens": (meta_tokens, m_meta_tokens, v_meta_tokens), "norm_g": (norm_g, m_norm_g, v_norm_g),
        "w_in": (w_in, m_w_in, v_w_in), "q_norm_g": (q_norm_g, m_q_norm_g, v_q_norm_g),
        "w_q_up": (w_q_up, m_w_q_up, v_w_q_up), "kv_norm_g": (kv_norm_g, m_kv_norm_g, v_kv_norm_g),
        "w_kv_up": (w_kv_up, m_w_kv_up, v_w_kv_up), "conv_w": (conv_w, m_conv_w, v_conv_w),
        "attn_out_g": (attn_out_g, m_attn_out_g, v_attn_out_g), "conv_out_g": (conv_out_g, m_conv_out_g, v_conv_out_g),
        "w_out": (w_out, m_w_out, v_w_out), "final_norm_g": (final_norm_g, m_final_norm_g, v_final_norm_g),
    }
    names = list(weights)
    deltas, new_m, new_v = [], [], []
    for nme in names:
        w_, m_, v_ = weights[nme]
        if nme in transposed:
            res = _adamw(tr(w_), grads[nme], tr(m_), tr(v_), "adamw_" + nme)
            g_, d_, nm_, nv_ = (jnp.transpose(a)[None] for a in (grads[nme],) + res)
        else:
            g_ = grads[nme].reshape(w_.shape)
            d_, nm_, nv_ = _adamw(w_, g_, m_, v_, "adamw_" + nme)
        grads[nme] = g_
        deltas.append(d_)
        new_m.append(nm_)
        new_v.append(nv_)

    grad_x = gx.reshape(nb, s, D_MODEL)
    return (loss, grad_x, *[grads[nme] for nme in names], *deltas, *new_m, *new_v)
```

```python
import functools

import jax
import jax.numpy as jnp
import numpy as np
from jax import lax
from jax.experimental import pallas as pl
from jax.experimental.pallas import tpu as pltpu

F32 = jnp.float32
BF16 = jnp.bfloat16

D_MODEL = 1024
N_META = 16
HEADS = 4
NOPE = 128
ROPE = 64
VDIM = 128
QK_PAD = 256
Q_RANK = 256
KV_RANK = 128
CONV_W = 512
CONV_GROUP = 64
ROPE_THETA = 10000.0
EPS = 1e-6
ATTN_SCALE = (NOPE + ROPE) ** -0.5
IN_DIM = 3008
IN_PAD = 3072
BLK_ZA, BLK_CB, BLK_CC, BLK_CH, BLK_ZC = 1, 2, 3, 4, 5
NEG_INF = -1e30

ADAM_LR = 0.001
ADAM_B1 = 0.9
ADAM_B2 = 0.999
ADAM_EPS = 1e-08
ADAM_WD = 0.01
ADAM_STEP = 10

ROW_TILE = 512
ATTN_TILE = 256
VMEM_LIMIT = 56 * 1024 * 1024

NT = (((1,), (1,)), ((), ()))
TN = (((0,), (0,)), ((), ()))


def _cparams(*sem):
    return pltpu.CompilerParams(dimension_semantics=sem, vmem_limit_bytes=VMEM_LIMIT)


def _dot(a, b):
    return jnp.dot(a, b, preferred_element_type=F32)


def _dot_nt(a, b):
    return lax.dot_general(a, b, NT, preferred_element_type=F32)


def _dot_tn(a, b):
    return lax.dot_general(a, b, TN, preferred_element_type=F32)


def _rms(x, g):
    r = lax.rsqrt(jnp.mean(x * x, axis=-1, keepdims=True) + EPS)
    return x * r * g, r


def _rms_bwd(dy, x, r, g):
    xh = x * r
    dyg = dy * g
    dx = r * (dyg - xh * jnp.mean(dyg * xh, axis=-1, keepdims=True))
    return dx, dy * xh


def _sigmoid(z):
    return 1.0 / (1.0 + jnp.exp(-z))


def _rope(b, c, sa, sb):
    return b * c + pltpu.roll(b, 96, 1) * sa + pltpu.roll(b, 32, 1) * sb


def _rope_bwd(d, c, sa, sb):
    return d * c + pltpu.roll(d * sa, 32, 1) + pltpu.roll(d * sb, 96, 1)


def _group_mean(x, gmat):
    hi = x.astype(BF16)
    lo = (x - hi.astype(F32)).astype(BF16)
    return _dot(hi, gmat) + _dot(lo, gmat)


def _row_of(col, rows):
    return jnp.transpose(jnp.broadcast_to(col, (rows, 128)))[0:1, :]


def _rope_tables(n_pos):
    half = ROPE // 2
    inv_freq = (np.float32(1.0) / (np.float32(ROPE_THETA) ** (np.arange(half, dtype=np.float32) / np.float32(half))))
    ang = np.arange(n_pos, dtype=np.float32)[:, None] * inv_freq.astype(np.float32)[None, :]
    cos, sin = np.cos(ang).astype(np.float32), np.sin(ang).astype(np.float32)
    z = np.zeros((n_pos, half), np.float32)
    c = np.concatenate([cos, cos, z, z], axis=1)
    sa = np.concatenate([-sin, z, z, z], axis=1)
    sb = np.concatenate([z, sin, z, z], axis=1)
    return jnp.asarray(c), jnp.asarray(sa), jnp.asarray(sb)


W_IN_PIECES = ((0, 384, 752, 0, 64, 0), (384, 64, 752, 384, 448, 1), (448, 304, 752, 512, 512, 1))
W_Q_PIECES = ((0, 96, 256, 0, 0, 0), (96, 96, 256, 96, 96, 1))
W_KV_PIECES = ((0, 128, 128, 0, 0, 0), (128, 128, 128, 512, 512, 1))
W_OUT_PIECES = ((0, 128, 256, 0, 0, 0), (128, 128, 256, 128, 128, 1))


class _StagedGather:
    def __init__(self, pieces):
        self.pieces = pieces

    def scratch(self):
        nk, dma = len(self.pieces), pltpu.SemaphoreType.DMA
        return [dma((nk, 3)), dma((nk, 3)), dma((nk, 3)), dma((nk, 3)), dma((nk,))]

    def run(self, stage, src_ref, out_ref, scr):
        send_sems, recv_sems, fwd_send, fwd_recv, loc_sems = scr
        pieces = self.pieces
        nk = len(pieces)
        x, y, c = lax.axis_index("x"), lax.axis_index("y"), lax.axis_index("c")
        mine = 2 * x + y
        chips = [(1 - x, y), (x, 1 - y), (1 - x, 1 - y)]
        chip_of = [2 * px + py for px, py in chips]
        mesh = pl.DeviceIdType.MESH

        def src(k):
            s0, nr = pieces[k][0], pieces[k][1]
            return src_ref.at[s0:s0 + nr]

        def dst(k, q):
            _, nr, per, first, rest, _ = pieces[k]
            row = per * q + first + (rest - first) * jnp.minimum(q, 1)
            return out_ref.at[pl.ds(pl.multiple_of(row, 16), nr)]

        def ici(k, j, q):
            px, py = chips[j]
            return pltpu.make_async_remote_copy(
                src_ref=src(k), dst_ref=dst(k, q), send_sem=send_sems.at[k, j], recv_sem=recv_sems.at[k, j],
                device_id=(px, py, c), device_id_type=mesh)

        def fwd(k, j):
            ref = dst(k, chip_of[j])
            return pltpu.make_async_remote_copy(
                src_ref=ref, dst_ref=ref, send_sem=fwd_send.at[k, j], recv_sem=fwd_recv.at[k, j],
                device_id=(x, y, 1 - c), device_id_type=mesh)

        local = [pltpu.make_async_copy(src(k), dst(k, mine), loc_sems.at[k]) for k in range(nk)]
        if stage == 0:
            for cp in local:
                cp.start()
        if stage == 2:
            for cp in local:
                cp.wait()
        for half in (0, 1):
            @pl.when(c == half)
            def _(half=half):
                my_k = [k for k in range(nk) if pieces[k][5] == half]
                other_k = [k for k in range(nk) if pieces[k][5] != half]
                for k in my_k:
                    for j in range(3):
                        if stage == 0:
                            ici(k, j, mine).start()
                        elif stage == 1:
                            ici(k, j, chip_of[j]).wait_recv()
                            fwd(k, j).start()
                        else:
                            ici(k, j, mine).wait_send()
                            fwd(k, j).wait_send()
                if stage == 2:
                    for k in other_k:
                        for j in range(3):
                            fwd(k, j).wait_recv()


def _fwd_proj(x2d, meta, tabs, tabs_m, norm_g, w_in_p, q_norm_g, wq_p, kv_norm_g, wkv_p, w_out_shard, nb, s, tm):
    nt = s // tm
    n = nb * nt
    n_steps = n + 1
    c_t, sa_t, sb_t = tabs
    cm_t, sam_t, sbm_t = tabs_m
    gat = _StagedGather(W_OUT_PIECES)
    assert n_steps >= 3

    def body(x_ref, c_ref, sa_ref, sb_ref, mt_ref, cm_ref, sam_ref, sbm_ref,
             g_ref, w_ref, gq_ref, wq_ref, gkv_ref, wkv_ref, wos_ref,
             p_ref, q_ref, k_ref, v_ref, pm_ref, km_ref, vm_ref, wo_ref, *gat_scr):
        i = pl.program_id(0)
        for stage, at in enumerate((0, n_steps - 2, n_steps - 1)):
            @pl.when(i == at)
            def _(stage=stage):
                gat.run(stage, wos_ref, wo_ref, gat_scr)

        def project(xv, c, sa, sb, p_out, q_out, k_out, v_out):
            u, _ = _rms(xv, g_ref[...])
            p = _dot_nt(u.astype(BF16), w_ref[...])
            p_out[...] = p
            qn, _ = _rms(p[:, 0:Q_RANK], gq_ref[...])
            q = _dot_nt(qn.astype(BF16), wq_ref[...])
            kvn, _ = _rms(p[:, Q_RANK:Q_RANK + KV_RANK], gkv_ref[...])
            kv = _dot_nt(kvn.astype(BF16), wkv_ref[...])
            kpe = _rope(p[:, 384:512], c, sa, sb)
            for h in range(HEADS):
                if q_out is not None:
                    pe = _rope(q[:, QK_PAD * h + NOPE:QK_PAD * (h + 1)], c, sa, sb)
                    qh = jnp.concatenate([q[:, QK_PAD * h:QK_PAD * h + NOPE], pe], axis=1)
                    q_out[0, h] = (qh * ATTN_SCALE).astype(BF16)
                k_out[0, h] = jnp.concatenate([kv[:, NOPE * h:NOPE * (h + 1)], kpe], axis=1).astype(BF16)
                v_out[0, h] = kv[:, 512 + VDIM * h:512 + VDIM * (h + 1)].astype(BF16)

        @pl.when(i < n)
        def _():
            project(x_ref[...], c_ref[...], sa_ref[...], sb_ref[...], p_ref, q_ref, k_ref, v_ref)

        @pl.when(i == n)
        def _():
            project(mt_ref[...], cm_ref[...], sam_ref[...], sbm_ref[...], pm_ref, None, km_ref, vm_ref)

    cl = lambda i: jnp.minimum(i, n - 1)
    full = lambda a: pl.BlockSpec(a.shape, lambda i: (0,) * a.ndim)
    const = lambda shape: pl.BlockSpec(shape, lambda i: (0,) * len(shape))
    tab = pl.BlockSpec((tm, 128), lambda i: (cl(i) % nt, 0))
    hb = lambda w: pl.BlockSpec((1, HEADS, tm, w), lambda i: (cl(i) // nt, 0, cl(i) % nt, 0))
    hbm = pl.BlockSpec(memory_space=pl.ANY)
    return pl.pallas_call(
        body, name="fwd_proj", grid=(n_steps,),
        in_specs=[pl.BlockSpec((tm, D_MODEL), lambda i: (cl(i), 0)), tab, tab, tab,
                  full(meta), full(cm_t), full(sam_t), full(sbm_t),
                  full(norm_g), full(w_in_p), full(q_norm_g), full(wq_p), full(kv_norm_g), full(wkv_p), hbm],
        out_specs=[pl.BlockSpec((tm, IN_PAD), lambda i: (cl(i), 0)), hb(QK_PAD), hb(QK_PAD), hb(VDIM),
                   const((N_META, IN_PAD)), const((1, HEADS, N_META, QK_PAD)), const((1, HEADS, N_META, VDIM)), hbm],
        out_shape=[jax.ShapeDtypeStruct((nb * s, IN_PAD), F32),
                   jax.ShapeDtypeStruct((nb, HEADS, s, QK_PAD), BF16),
                   jax.ShapeDtypeStruct((nb, HEADS, s, QK_PAD), BF16),
                   jax.ShapeDtypeStruct((nb, HEADS, s, VDIM), BF16),
                   jax.ShapeDtypeStruct((N_META, IN_PAD), F32),
                   jax.ShapeDtypeStruct((1, HEADS, N_META, QK_PAD), BF16),
                   jax.ShapeDtypeStruct((1, HEADS, N_META, VDIM), BF16),
                   jax.ShapeDtypeStruct((D_MODEL, D_MODEL), BF16)],
        scratch_shapes=gat.scratch(),
        compiler_params=_cparams("arbitrary"),
    )(x2d, c_t, sa_t, sb_t, meta, cm_t, sam_t, sbm_t, norm_g, w_in_p, q_norm_g, wq_p, kv_norm_g, wkv_p, w_out_shard)


def _attn_fwd(q, k, v, km, vm, nb, s, tq):
    nq = s // tq

    def body(q_ref, k_ref, v_ref, km_ref, vm_ref, o_ref, lse_ref, s_scr, p_scr):
        row = lax.broadcasted_iota(jnp.int32, (tq, tq), 0)
        col = lax.broadcasted_iota(jnp.int32, (tq, tq), 1)
        def scores(i):
            slot = i % 2
            qi = q_ref[0, 0, i * tq:(i + 1) * tq, :]
            sm = _dot_nt(qi, km_ref[0, 0])
            m128 = None
            for j in range(i + 1):
                sc = _dot_nt(qi, k_ref[0, 0, j * tq:(j + 1) * tq, :])
                if j == i:
                    sc = jnp.where(col <= row, sc, NEG_INF)
                s_scr[slot, :, j * tq:(j + 1) * tq] = sc
                mx = sc[:, 0:128]
                for c0 in range(128, tq, 128):
                    mx = jnp.maximum(mx, sc[:, c0:c0 + 128])
                m128 = mx if m128 is None else jnp.maximum(m128, mx)
            return sm, jnp.maximum(jnp.max(m128, axis=1, keepdims=True), jnp.max(sm, axis=1, keepdims=True))

        nxt = scores(0)
        for i in range(nq):
            slot = i % 2
            sm, m = nxt
            if i + 1 < nq:
                nxt = scores(i + 1)
            pm = jnp.exp(sm - m)
            l128 = None
            for j in range(i + 1):
                p = jnp.exp(s_scr[slot, :, j * tq:(j + 1) * tq] - m)
                p_scr[slot, :, j * tq:(j + 1) * tq] = p.astype(BF16)
                ps = p[:, 0:128]
                for c0 in range(128, tq, 128):
                    ps = ps + p[:, c0:c0 + 128]
                l128 = ps if l128 is None else l128 + ps
            l = jnp.sum(l128, axis=1, keepdims=True) + jnp.sum(pm, axis=1, keepdims=True)
            n = (i + 1) * tq
            acc = _dot(p_scr[slot, :, 0:n], v_ref[0, 0, 0:n, :]) + _dot(pm.astype(BF16), vm_ref[0, 0])
            o_ref[0, 0, i * tq:(i + 1) * tq, :] = acc / l
            lse_ref[0, 0, :, i * tq:(i + 1) * tq] = _row_of(m + jnp.log(l), tq)

    hblk = lambda w: pl.BlockSpec((1, 1, s, w), lambda b, h: (b, h, 0, 0))
    mblk = lambda w: pl.BlockSpec((1, 1, N_META, w), lambda b, h: (0, h, 0, 0))
    return pl.pallas_call(
        body, name="attn_fwd", grid=(nb, HEADS),
        in_specs=[hblk(QK_PAD), hblk(QK_PAD), hblk(VDIM), mblk(QK_PAD), mblk(VDIM)],
        out_specs=[hblk(VDIM), pl.BlockSpec((1, 1, 1, s), lambda b, h: (b, h, 0, 0))],
        out_shape=[jax.ShapeDtypeStruct((nb, HEADS, s, VDIM), F32),
                   jax.ShapeDtypeStruct((nb, HEADS, 1, s), F32)],
        scratch_shapes=[pltpu.VMEM((2, tq, s), F32), pltpu.VMEM((2, tq, s), BF16)],
        compiler_params=_cparams("parallel", "parallel"),
    )(q, k, v, km, vm)


def _shift_rows(a, prev, n_rows):
    rid = lax.broadcasted_iota(jnp.int32, a.shape, 0)
    a1 = jnp.where(rid == 0, prev[7:8, :], pltpu.roll(a, 1, 0))
    a2 = jnp.where(rid == 0, prev[6:7, :], jnp.where(rid == 1, prev[7:8, :], pltpu.roll(a, 2, 0)))
    return a1, a2


def _attn_gate(o, za, ga_h):
    on, r = _rms(o, ga_h)
    return on * (za * _sigmoid(za)), on, r


def _out_fwd_bwd(x2d, tgt2d, o, p, pm, conv_w, ga, gc, gmat, w_out, gf, nb, s, tm):
    nt = s // tm
    r = nb * s
    prev_idx = lambda i: jnp.maximum(i * (tm // 8) - 1, 0)

    def body(x_ref, t_ref, o_ref, za_ref, cb_ref, cc_ref, ch_ref, zc_ref, ccp_ref, chp_ref, mc_ref, mh_ref,
             cw_ref, ga_ref, gc_ref, gm_ref, w_ref, gf_ref,
             dh_ref, dy_ref, dw_ref, dgf_ref, loss_ref):
        i = pl.program_id(0)

        @pl.when(i == 0)
        def _():
            dw_ref[...] = jnp.zeros_like(dw_ref)
            dgf_ref[...] = jnp.zeros_like(dgf_ref)
            loss_ref[...] = jnp.zeros_like(loss_ref)

        ya = []
        for h in range(HEADS):
            y, _, _ = _attn_gate(o_ref[0, h], za_ref[:, VDIM * h:VDIM * (h + 1)],
                                 ga_ref[:, VDIM * h:VDIM * (h + 1)])
            ya.append(y)
        cc = cc_ref[...] * ch_ref[...]
        prev = jnp.where(i % nt == 0, mc_ref[8:16, :] * mh_ref[8:16, :], ccp_ref[...] * chp_ref[...])
        cc1, cc2 = _shift_rows(cc, prev, tm)
        yc = cb_ref[...] * (cw_ref[0:1, :] * cc2 + cw_ref[1:2, :] * cc1 + cw_ref[2:3, :] * cc)
        rg = lax.rsqrt(_group_mean(yc * yc, gm_ref[...]) + EPS)
        zc = zc_ref[...]
        yconv = yc * rg * gc_ref[...] * (zc * _sigmoid(zc))
        ycat = jnp.concatenate(ya + [yconv], axis=1).astype(BF16)
        h2 = x_ref[...] + _dot(ycat, w_ref[...])
        gfv = gf_ref[...]
        y, r2 = _rms(h2, gfv)
        e = y - t_ref[...]
        loss_ref[...] += 0.5 * jnp.sum(e * e) / D_MODEL
        dyv = e * (1.0 / D_MODEL)
        dh2, dgf = _rms_bwd(dyv, h2, r2, gfv)
        dgf_ref[...] += jnp.sum(dgf, axis=0, keepdims=True)
        dh_ref[...] = dh2
        dhb = dh2.astype(BF16)
        dy_ref[...] = _dot_nt(dhb, w_ref[...])
        dw_ref[...] += _dot_tn(ycat, dhb)

    row = lambda w, j: pl.BlockSpec((tm, w), lambda i: (i, j))
    pblk = lambda j: pl.BlockSpec((tm, 512), lambda i: (i, j))
    pprev = lambda j: pl.BlockSpec((8, 512), lambda i: (prev_idx(i), j))
    mblk = lambda j: pl.BlockSpec((N_META, 512), lambda i: (0, j))
    full = lambda a: pl.BlockSpec(a.shape, lambda i: (0,) * a.ndim)
    return pl.pallas_call(
        body, name="out_fwd_bwd", grid=(nb * nt,),
        in_specs=[row(D_MODEL, 0), row(D_MODEL, 0),
                  pl.BlockSpec((1, HEADS, tm, VDIM), lambda i: (i // nt, 0, i % nt, 0)),
                  pblk(BLK_ZA), pblk(BLK_CB), pblk(BLK_CC), pblk(BLK_CH), pblk(BLK_ZC),
                  pprev(BLK_CC), pprev(BLK_CH), mblk(BLK_CC), mblk(BLK_CH),
                  full(conv_w), full(ga), full(gc), full(gmat), full(w_out), full(gf)],
        out_specs=[row(D_MODEL, 0), row(D_MODEL, 0),
                   pl.BlockSpec((D_MODEL, D_MODEL), lambda i: (0, 0)),
                   pl.BlockSpec((1, D_MODEL), lambda i: (0, 0)),
                   pl.BlockSpec((1, 128), lambda i: (0, 0))],
        out_shape=[jax.ShapeDtypeStruct((r, D_MODEL), F32), jax.ShapeDtypeStruct((r, D_MODEL), F32),
                   jax.ShapeDtypeStruct((D_MODEL, D_MODEL), F32), jax.ShapeDtypeStruct((1, D_MODEL), F32),
                   jax.ShapeDtypeStruct((1, 128), F32)],
        compiler_params=_cparams("arbitrary"),
    )(x2d, tgt2d, o, p, p, p, p, p, p, p, pm, pm, conv_w, ga, gc, gmat, w_out, gf)


def _gate_bwd(dycat, o, p, pm, conv_w, ga, gc, gmat, nb, s, tm):
    nt = s // tm
    r = nb * s
    ext = tm + 8
    prev_idx = lambda i: jnp.maximum(i * (tm // 8) - 1, 0)
    next_idx = lambda i: jnp.minimum((i + 1) * (tm // 8), r // 8 - 1)

    def body(dya_ref, dyc_ref, dycn_ref, o_ref, za_ref, cb_ref, cbn_ref, cc_ref, ccp_ref, ccn_ref,
             ch_ref, chp_ref, chn_ref, zc_ref, zcn_ref, mc_ref, mh_ref, cw_ref, ga_ref, gc_ref, gm_ref,
             dpb_ref, do_ref, dl_ref, dccm_ref, dga_ref, dgc_ref, dcw_ref):
        i = pl.program_id(0)

        @pl.when(i == 0)
        def _():
            dga_ref[...] = jnp.zeros_like(dga_ref)
            dgc_ref[...] = jnp.zeros_like(dgc_ref)
            dcw_ref[...] = jnp.zeros_like(dcw_ref)

        dga = []
        for h in range(HEADS):
            hs = slice(VDIM * h, VDIM * (h + 1))
            oh, za, gah, dya = o_ref[0, h], za_ref[:, hs], ga_ref[:, hs], dya_ref[:, hs]
            sg = _sigmoid(za)
            on, ro = _rms(oh, gah)
            don = dya * (za * sg)
            dpb_ref[:, hs] = (dya * on * (sg * (1.0 + za * (1.0 - sg)))).astype(BF16)
            do, dg = _rms_bwd(don, oh, ro, gah)
            dga.append(jnp.sum(dg, axis=0, keepdims=True))
            dob = do.astype(BF16)
            do_ref[0, h] = dob
            dl_ref[0, h] = _row_of(jnp.sum(dob.astype(F32) * oh, axis=1, keepdims=True), tm)
        dga_ref[...] += jnp.concatenate(dga, axis=1)

        cat = lambda a, b: jnp.concatenate([a[...], b[...]], axis=0)
        cch = cat(cc_ref, ccn_ref)
        chh = cat(ch_ref, chn_ref)
        cb = cat(cb_ref, cbn_ref)
        zc = cat(zc_ref, zcn_ref)
        dy = cat(dyc_ref, dycn_ref)
        first = i % nt == 0
        last = i % nt == nt - 1
        cc = cch * chh
        prev = jnp.where(first, mc_ref[8:16, :] * mh_ref[8:16, :], ccp_ref[...] * chp_ref[...])
        cc1, cc2 = _shift_rows(cc, prev, ext)
        w0, w1, w2 = cw_ref[0:1, :], cw_ref[1:2, :], cw_ref[2:3, :]
        dw = w0 * cc2 + w1 * cc1 + w2 * cc
        yc = cb * dw
        rg = lax.rsqrt(_group_mean(yc * yc, gm_ref[...]) + EPS)
        ych = yc * rg
        gcv = gc_ref[...]
        sg = _sigmoid(zc)
        dycn = dy * (zc * sg)
        dzc = dy * (ych * gcv) * (sg * (1.0 + zc * (1.0 - sg)))
        dgc_ref[...] += jnp.sum((dycn * ych)[:tm], axis=0, keepdims=True)
        dycg = dycn * gcv
        dyc = rg * (dycg - ych * _group_mean(dycg * ych, gm_ref[...]))
        rid = lax.broadcasted_iota(jnp.int32, (ext, CONV_W), 0)
        ddw = jnp.where(jnp.logical_and(last, rid >= tm), 0.0, dyc * cb)
        dcb = dyc * dw
        dcc = w2 * ddw + w1 * pltpu.roll(ddw, ext - 1, 0) + w0 * pltpu.roll(ddw, ext - 2, 0)
        dpb_ref[:, 512:1024] = dcb[:tm].astype(BF16)
        dpb_ref[:, 1024:1536] = (dcc * chh)[:tm].astype(BF16)
        dpb_ref[:, 1536:2048] = (dcc * cch)[:tm].astype(BF16)
        dpb_ref[:, 2048:2560] = dzc[:tm].astype(BF16)
        rs = lambda a: jnp.sum(a[:tm], axis=0, keepdims=True)
        dcw_ref[0:1, :] += rs(ddw * cc2)
        dcw_ref[1:2, :] += rs(ddw * cc1)
        dcw_ref[2:3, :] += rs(ddw * cc)

        @pl.when(first)
        def _():
            d0, d1 = ddw[0:1, :], ddw[1:2, :]
            r8 = lax.broadcasted_iota(jnp.int32, (8, CONV_W), 0)
            dccm_ref[0] = jnp.where(r8 == 7, w1 * d0 + w0 * d1, jnp.where(r8 == 6, w0 * d0, 0.0))

    row = lambda j: pl.BlockSpec((tm, 512), lambda i: (i, j))
    prv = lambda j: pl.BlockSpec((8, 512), lambda i: (prev_idx(i), j))
    nxt = lambda j: pl.BlockSpec((8, 512), lambda i: (next_idx(i), j))
    mblk = lambda j: pl.BlockSpec((N_META, 512), lambda i: (0, j))
    full = lambda a: pl.BlockSpec(a.shape, lambda i: (0,) * a.ndim)
    hb = lambda w: pl.BlockSpec((1, HEADS, tm, w), lambda i: (i // nt, 0, i % nt, 0))
    acc = lambda rr: pl.BlockSpec((rr, 512), lambda i: (0, 0))
    return pl.pallas_call(
        body, name="gate_bwd", grid=(nb * nt,),
        in_specs=[row(0), row(1), nxt(1), hb(VDIM),
                  row(BLK_ZA), row(BLK_CB), nxt(BLK_CB), row(BLK_CC), prv(BLK_CC), nxt(BLK_CC),
                  row(BLK_CH), prv(BLK_CH), nxt(BLK_CH), row(BLK_ZC), nxt(BLK_ZC),
                  mblk(BLK_CC), mblk(BLK_CH), full(conv_w), full(ga), full(gc), full(gmat)],
        out_specs=[pl.BlockSpec((tm, 2560), lambda i: (i, 0)), hb(VDIM),
                   pl.BlockSpec((1, HEADS, 1, tm), lambda i: (i // nt, 0, 0, i % nt)),
                   pl.BlockSpec((1, 8, 512), lambda i: (i // nt, 0, 0)),
                   acc(1), acc(1), acc(8)],
        out_shape=[jax.ShapeDtypeStruct((r, 2560), BF16), jax.ShapeDtypeStruct((nb, HEADS, s, VDIM), BF16),
                   jax.ShapeDtypeStruct((nb, HEADS, 1, s), F32), jax.ShapeDtypeStruct((nb, 8, 512), F32),
                   jax.ShapeDtypeStruct((1, 512), F32), jax.ShapeDtypeStruct((1, 512), F32),
                   jax.ShapeDtypeStruct((8, 512), F32)],
        compiler_params=_cparams("arbitrary"),
    )(dycat, dycat, dycat, o, p, p, p, p, p, p, p, p, p, p, p, pm, pm, conv_w, ga, gc, gmat)


class _StagedReduce:
    LOC, PRE_S, PRE_R, ICI_S, ICI_R, POST_S, POST_R, OUT, N_SEM = 0, 1, 2, 3, 6, 9, 10, 11, 12

    def __init__(self, shard_shape):
        self.half = (shard_shape[0] // 2, shard_shape[1])

    def scratch(self):
        h = self.half
        return [pltpu.VMEM((4,) + h, F32), pltpu.VMEM((4,) + h, F32), pltpu.VMEM((4,) + h, BF16),
                pltpu.VMEM((3,) + h, BF16), pltpu.VMEM(h, F32), pltpu.SemaphoreType.DMA((self.N_SEM,))]

    def run(self, stage, pin, gout, scr):
        own, sib, wire, rbuf, fin, sems = scr
        r2 = self.half[0]
        x, y, c = lax.axis_index("x"), lax.axis_index("y"), lax.axis_index("c")
        mine = 2 * x + y
        sibling = (x, y, 1 - c)
        chips = [(1 - x, y), (x, 1 - y), (1 - x, 1 - y)]
        rows = lambda half: pl.ds(pl.multiple_of(half * r2, r2), r2)
        mesh = pl.DeviceIdType.MESH

        loc = pltpu.make_async_copy(pin.at[:, rows(c), :], own, sems.at[self.LOC])
        pre = pltpu.make_async_remote_copy(
            src_ref=pin.at[:, rows(1 - c), :], dst_ref=sib, send_sem=sems.at[self.PRE_S],
            recv_sem=sems.at[self.PRE_R], device_id=sibling, device_id_type=mesh)

        def ici(j):
            px, py = chips[j]
            return pltpu.make_async_remote_copy(
                src_ref=wire.at[2 * px + py], dst_ref=rbuf.at[j], send_sem=sems.at[self.ICI_S + j],
                recv_sem=sems.at[self.ICI_R + j], device_id=(px, py, c), device_id_type=mesh)

        def post(half):
            return pltpu.make_async_remote_copy(
                src_ref=fin, dst_ref=gout.at[rows(half), :], send_sem=sems.at[self.POST_S],
                recv_sem=sems.at[self.POST_R], device_id=sibling, device_id_type=mesh)

        keep = pltpu.make_async_copy(fin, gout.at[rows(c), :], sems.at[self.OUT])
        if stage == 0:
            loc.start()
            pre.start()
        elif stage == 1:
            loc.wait()
            pre.wait_recv()
            for blk in range(4):
                tot = own[blk] + sib[blk]
                own[blk] = tot
                wire[blk] = tot.astype(BF16)
            for j in range(3):
                ici(j).start()
        elif stage == 2:
            for j in range(3):
                ici(j).wait_recv()
            tot = own[mine]
            for j in range(3):
                tot = tot + rbuf[j].astype(F32)
            fin[...] = tot
            post(c).start()
            keep.start()
        else:
            post(1 - c).wait_recv()
            pre.wait_send()
            for j in range(3):
                ici(j).wait_send()
            post(c).wait_send()
            keep.wait()


def _attn_bwd(q, k, v, do, lse, delta, km, vm, early, nb, s, t):
    n = s // t
    ne = len(early)
    reds = [_StagedReduce(a.shape[1:]) for a in early]
    n_steps = HEADS * nb
    assert n_steps >= 4

    def body(q_ref, k_ref, v_ref, do_ref, lse_ref, dl_ref, km_ref, vm_ref, *rest):
        pin_refs, rest = rest[:ne], rest[ne:]
        dq_ref, dk_ref, dv_ref, dkm_ref, dvm_ref = rest[:5]
        gout_refs, (p_scr, ds_scr, dq_acc), red_scr = rest[5:5 + ne], rest[5 + ne:8 + ne], rest[8 + ne:]
        b = pl.program_id(1)
        step = pl.program_id(0) * nb + b
        for stage, at in enumerate((0, 1, n_steps - 2, n_steps - 1)):
            @pl.when(step == at)
            def _(stage=stage):
                for a, red in enumerate(reds):
                    red.run(stage, pin_refs[a], gout_refs[a], red_scr[6 * a:6 * a + 6])

        @pl.when(b == 0)
        def _():
            dkm_ref[...] = jnp.zeros_like(dkm_ref)
            dvm_ref[...] = jnp.zeros_like(dvm_ref)

        kr = lax.broadcasted_iota(jnp.int32, (t, t), 0)
        qc = lax.broadcasted_iota(jnp.int32, (t, t), 1)
        km_v, vm_v = km_ref[0, 0], vm_ref[0, 0]
        ptm = jnp.exp(_dot_nt(km_v, q_ref[0, 0]) - lse_ref[0, 0])
        dstm = (ptm * (_dot_nt(vm_v, do_ref[0, 0]) - dl_ref[0, 0])).astype(BF16)
        dkm_ref[0] += _dot(dstm, q_ref[0, 0])
        dvm_ref[0] += _dot(ptm.astype(BF16), do_ref[0, 0])
        dq_acc[...] = _dot_tn(dstm, km_v)
        def tiles(j):
            slot = j % 2
            kj = k_ref[0, 0, j * t:(j + 1) * t, :]
            vj = v_ref[0, 0, j * t:(j + 1) * t, :]
            for i in range(j, n):
                cs = slice(i * t, (i + 1) * t)
                qi = q_ref[0, 0, cs, :]
                doi = do_ref[0, 0, cs, :]
                st = _dot_nt(kj, qi)
                if i == j:
                    st = jnp.where(kr <= qc, st, NEG_INF)
                pt = jnp.exp(st - lse_ref[0, 0, :, cs])
                dst = (pt * (_dot_nt(vj, doi) - dl_ref[0, 0, :, cs])).astype(BF16)
                p_scr[slot, :, cs] = pt.astype(BF16)
                ds_scr[slot, :, cs] = dst
                dq_acc[cs, :] += _dot_tn(dst, kj)

        tiles(0)
        for j in range(n):
            slot = j % 2
            if j + 1 < n:
                tiles(j + 1)
            dv_ref[0, 0, j * t:(j + 1) * t, :] = _dot(p_scr[slot, :, j * t:s], do_ref[0, 0, j * t:s, :]).astype(BF16)
            dk_ref[0, 0, j * t:(j + 1) * t, :] = _dot(ds_scr[slot, :, j * t:s], q_ref[0, 0, j * t:s, :]).astype(BF16)
        dq_ref[0, 0] = dq_acc[...].astype(BF16)

    big = lambda w: pl.BlockSpec((1, 1, s, w), lambda h, b: (b, h, 0, 0))
    rowv = pl.BlockSpec((1, 1, 1, s), lambda h, b: (b, h, 0, 0))
    mk = lambda w: pl.BlockSpec((1, 1, N_META, w), lambda h, b: (0, h, 0, 0))
    mo = lambda w: pl.BlockSpec((1, N_META, w), lambda h, b: (h, 0, 0))
    return pl.pallas_call(
        body, name="attn_bwd", grid=(HEADS, nb),
        in_specs=[big(QK_PAD), big(QK_PAD), big(VDIM), big(VDIM), rowv, rowv, mk(QK_PAD), mk(VDIM)]
        + [pl.BlockSpec(memory_space=pl.ANY)] * ne,
        out_specs=[big(QK_PAD), big(QK_PAD), big(VDIM), mo(QK_PAD), mo(VDIM)]
        + [pl.BlockSpec(memory_space=pl.ANY)] * ne,
        out_shape=[jax.ShapeDtypeStruct((nb, HEADS, s, QK_PAD), BF16),
                   jax.ShapeDtypeStruct((nb, HEADS, s, QK_PAD), BF16),
                   jax.ShapeDtypeStruct((nb, HEADS, s, VDIM), BF16),
                   jax.ShapeDtypeStruct((HEADS, N_META, QK_PAD), F32),
                   jax.ShapeDtypeStruct((HEADS, N_META, VDIM), F32)]
        + [jax.ShapeDtypeStruct(a.shape[1:], F32) for a in early],
        scratch_shapes=[pltpu.VMEM((2, t, s), BF16), pltpu.VMEM((2, t, s), BF16), pltpu.VMEM((s, QK_PAD), F32)]
        + [sc for red in reds for sc in red.scratch()],
        compiler_params=_cparams("arbitrary", "arbitrary"),
    )(q, k, v, do, lse, delta, km, vm, *early)


def _up_bwd(dq, dk, dv, dkm, dvm, p, pm, tabs, tabs_m, wq_p, wkv_p, gq, gkv, nb, s, tm):
    nt = s // tm
    n = nb * nt
    c_t, sa_t, sb_t = tabs
    cm_t, sam_t, sbm_t = tabs_m

    def kv_path(dkh, dvh, pa, c, sa, sb, wkv, gkvv):
        dkpe = dkh[0][:, NOPE:]
        for h in range(1, HEADS):
            dkpe = dkpe + dkh[h][:, NOPE:]
        dkr = _rope_bwd(dkpe, c, sa, sb)
        dkv = jnp.concatenate([d[:, :NOPE] for d in dkh] + list(dvh), axis=1).astype(BF16)
        ckv = pa[:, Q_RANK:Q_RANK + KV_RANK]
        kvn, rkv = _rms(ckv, gkvv)
        dckv, dg = _rms_bwd(_dot(dkv, wkv), ckv, rkv, gkvv)
        return dckv, dkr, kvn.astype(BF16), dkv, jnp.sum(dg, axis=0, keepdims=True)

    def body(dq_ref, dk_ref, dv_ref, pa_ref, c_ref, sa_ref, sb_ref,
             dkm_ref, dvm_ref, pam_ref, cm_ref, sam_ref, sbm_ref,
             wq_ref, wkv_ref, gq_ref, gkv_ref,
             dpa_ref, dpam_ref, pq_ref, pkv_ref, dgq_ref, dgkv_ref, dwq_ref, dwkv_ref):
        i = pl.program_id(0)

        @pl.when(i == 0)
        def _():
            dwq_ref[...] = jnp.zeros_like(dwq_ref)
            dwkv_ref[...] = jnp.zeros_like(dwkv_ref)
            dgq_ref[...] = jnp.zeros_like(dgq_ref)
            dgkv_ref[...] = jnp.zeros_like(dgkv_ref)

        @pl.when(i < n)
        def _():
            c, sa, sb = c_ref[...], sa_ref[...], sb_ref[...]
            pa = pa_ref[...]
            parts = []
            for h in range(HEADS):
                dqh = dq_ref[0, h].astype(F32) * ATTN_SCALE
                parts += [dqh[:, :NOPE], _rope_bwd(dqh[:, NOPE:], c, sa, sb)]
            dql = jnp.concatenate(parts, axis=1).astype(BF16)
            cq = pa[:, 0:Q_RANK]
            gqv = gq_ref[...]
            qn, rq = _rms(cq, gqv)
            dwq_ref[...] += _dot_tn(dql, qn.astype(BF16))
            dcq, dg = _rms_bwd(_dot(dql, wq_ref[...]), cq, rq, gqv)
            dgq_ref[...] += jnp.sum(dg, axis=0, keepdims=True)
            dckv, dkr, kvn, dkv, dgk = kv_path([dk_ref[0, h].astype(F32) for h in range(HEADS)],
                                               [dv_ref[0, h].astype(F32) for h in range(HEADS)],
                                               pa, c, sa, sb, wkv_ref[...], gkv_ref[...])
            dwkv_ref[...] += _dot_tn(dkv, kvn)
            dgkv_ref[...] += dgk
            dpa_ref[...] = jnp.concatenate([dcq, dckv, dkr], axis=1).astype(BF16)

        @pl.when(i == n)
        def _():
            dckv, dkr, kvn, dkv, dgk = kv_path([dkm_ref[h] for h in range(HEADS)],
                                               [dvm_ref[h] for h in range(HEADS)],
                                               pam_ref[...], cm_ref[...], sam_ref[...], sbm_ref[...],
                                               wkv_ref[...], gkv_ref[...])
            dwkv_ref[...] += _dot_tn(dkv, kvn)
            dgkv_ref[...] += dgk
            dpam_ref[...] = jnp.concatenate([jnp.zeros((N_META, Q_RANK), F32), dckv, dkr], axis=1)
            for h in range(HEADS):
                pq_ref[h] = dwq_ref[QK_PAD * h:QK_PAD * h + NOPE + ROPE, :]
                pkv_ref[h, 0:NOPE, :] = dwkv_ref[NOPE * h:NOPE * (h + 1), :]
                pkv_ref[h, NOPE:NOPE + VDIM, :] = dwkv_ref[512 + VDIM * h:512 + VDIM * (h + 1), :]

    cl = lambda i: jnp.minimum(i, n - 1)
    hb = lambda w: pl.BlockSpec((1, HEADS, tm, w), lambda i: (cl(i) // nt, 0, cl(i) % nt, 0))
    tab = pl.BlockSpec((tm, 128), lambda i: (cl(i) % nt, 0))
    full = lambda a: pl.BlockSpec(a.shape, lambda i: (0,) * a.ndim)
    const = lambda shape: pl.BlockSpec(shape, lambda i: (0,) * len(shape))
    return pl.pallas_call(
        body, name="up_bwd", grid=(n + 1,),
        in_specs=[hb(QK_PAD), hb(QK_PAD), hb(VDIM), pl.BlockSpec((tm, 512), lambda i: (cl(i), 0)), tab, tab, tab,
                  full(dkm), full(dvm), pl.BlockSpec((N_META, 512), lambda i: (0, 0)),
                  full(cm_t), full(sam_t), full(sbm_t), full(wq_p), full(wkv_p), full(gq), full(gkv)],
        out_specs=[pl.BlockSpec((tm, 512), lambda i: (cl(i), 0)), const((N_META, 512)),
                   const((HEADS, NOPE + ROPE, Q_RANK)), const((HEADS, NOPE + VDIM, KV_RANK)),
                   const((1, Q_RANK)), const((1, KV_RANK))],
        out_shape=[jax.ShapeDtypeStruct((nb * s, 512), BF16), jax.ShapeDtypeStruct((N_META, 512), F32),
                   jax.ShapeDtypeStruct((HEADS, NOPE + ROPE, Q_RANK), F32),
                   jax.ShapeDtypeStruct((HEADS, NOPE + VDIM, KV_RANK), F32),
                   jax.ShapeDtypeStruct((1, Q_RANK), F32), jax.ShapeDtypeStruct((1, KV_RANK), F32)],
        scratch_shapes=[pltpu.VMEM((HEADS * QK_PAD, Q_RANK), F32), pltpu.VMEM((1024, KV_RANK), F32)],
        compiler_params=_cparams("arbitrary"),
    )(dq, dk, dv, p, c_t, sa_t, sb_t, dkm, dvm, pm, cm_t, sam_t, sbm_t, wq_p, wkv_p, gq, gkv)


def _in_bwd(x2d, dh2, dpa, dpb, meta, dpam, dccm, pm, w_in_p, norm_g, nb, s, tm):
    nt = s // tm
    n = nb * nt

    def body(x_ref, dh_ref, dpa_ref, dpb_ref, mt_ref, dpam_ref, dccm_ref, mc_ref, mh_ref, w_ref, g_ref,
             gx_ref, gm_ref, dw_hbm, dg_ref, acc_ref, sems):
        i = pl.program_id(0)

        @pl.when(i == 0)
        def _():
            acc_ref[...] = jnp.zeros_like(acc_ref)
            dg_ref[...] = jnp.zeros_like(dg_ref)

        def rows(x, dp, dres):
            g = g_ref[...]
            u, r1 = _rms(x, g)
            dpb16 = dp.astype(BF16)
            acc_ref[...] += _dot_tn(dpb16, u.astype(BF16))
            dx, dg = _rms_bwd(_dot(dpb16, w_ref[...]), x, r1, g)
            dg_ref[...] += jnp.sum(dg, axis=0, keepdims=True)
            return dx if dres is None else dx + dres

        @pl.when(i < n)
        def _():
            dp = jnp.concatenate([dpa_ref[...], dpb_ref[...]], axis=1)
            gx_ref[...] = rows(x_ref[...], dp, dh_ref[...])

        @pl.when(i == n)
        def _():
            dcc = dccm_ref[0]
            for b in range(1, nb):
                dcc = dcc + dccm_ref[b]
            z8 = jnp.zeros((8, CONV_W), F32)
            dc = jnp.concatenate([z8, dcc * mh_ref[8:16, :]], axis=0)
            dh = jnp.concatenate([z8, dcc * mc_ref[8:16, :]], axis=0)
            z = jnp.zeros((N_META, CONV_W), F32)
            dp = jnp.concatenate([dpam_ref[...], z, z, dc, dh, z], axis=1)
            gm_ref[...] = rows(mt_ref[...], dp, None)
            per = IN_DIM // 4
            cps = [pltpu.make_async_copy(acc_ref.at[0:448], dw_hbm.at[0, 0:448], sems.at[0]),
                   pltpu.make_async_copy(acc_ref.at[512:per + 64], dw_hbm.at[0, 448:per], sems.at[1])]
            for qq in range(1, 4):
                cps.append(pltpu.make_async_copy(acc_ref.at[per * qq + 64:per * (qq + 1) + 64], dw_hbm.at[qq],
                                                 sems.at[qq + 1]))
            for cp in cps:
                cp.start()
            for cp in cps:
                cp.wait()

    cl = lambda i: jnp.minimum(i, n - 1)
    row = lambda w: pl.BlockSpec((tm, w), lambda i: (cl(i), 0))
    full = lambda a: pl.BlockSpec(a.shape, lambda i: (0,) * a.ndim)
    mblk = lambda j: pl.BlockSpec((N_META, 512), lambda i: (0, j))
    return pl.pallas_call(
        body, name="in_bwd", grid=(n + 1,),
        in_specs=[row(D_MODEL), row(D_MODEL), row(512), row(2560), full(meta), full(dpam), full(dccm),
                  mblk(BLK_CC), mblk(BLK_CH), full(w_in_p), full(norm_g)],
        out_specs=[row(D_MODEL), pl.BlockSpec((N_META, D_MODEL), lambda i: (0, 0)),
                   pl.BlockSpec(memory_space=pl.ANY), pl.BlockSpec((1, D_MODEL), lambda i: (0, 0))],
        out_shape=[jax.ShapeDtypeStruct((nb * s, D_MODEL), F32), jax.ShapeDtypeStruct((N_META, D_MODEL), F32),
                   jax.ShapeDtypeStruct((4, IN_DIM // 4, D_MODEL), F32), jax.ShapeDtypeStruct((1, D_MODEL), F32)],
        scratch_shapes=[pltpu.VMEM((IN_PAD, D_MODEL), F32), pltpu.SemaphoreType.DMA((5,))],
        compiler_params=_cparams("arbitrary"),
    )(x2d, dh2, dpa, dpb, meta, dpam, dccm, pm, pm, w_in_p, norm_g)


def _gather_weights(split, pieces, out_rows, whole, zero_fills):
    ns, nw, nz = len(split), len(whole), len(zero_fills)
    flat = [(a, pc) for a in range(ns) for pc in pieces[a]]
    nk = len(flat)

    def body(*refs):
        ins, wins, zins = refs[:ns], refs[ns:ns + nw], refs[ns + nw:ns + nw + nz]
        outs, wouts = refs[ns + nw + nz:2 * ns + nw + nz], refs[2 * ns + nw + nz:2 * (ns + nw) + nz]
        send_sems, recv_sems, fwd_send, fwd_recv, loc_sems, w_send, w_recv, w_loc, z_sems = refs[2 * (ns + nw) + nz:]
        x, y, c = lax.axis_index("x"), lax.axis_index("y"), lax.axis_index("c")
        mine = 2 * x + y
        chips = [(1 - x, y), (x, 1 - y), (1 - x, 1 - y)]
        chip_of = [2 * px + py for px, py in chips]

        def src(k):
            a, (s0, nr, _, _, _, _) = flat[k]
            return ins[a].at[s0:s0 + nr]

        def dst(k, q):
            a, (_, nr, per, first, rest, _) = flat[k]
            row = per * q + first + (rest - first) * jnp.minimum(q, 1)
            return outs[a].at[pl.ds(pl.multiple_of(row, 16), nr)]

        def ici(k, j, q):
            px, py = chips[j]
            return pltpu.make_async_remote_copy(
                src_ref=src(k), dst_ref=dst(k, q), send_sem=send_sems.at[k, j], recv_sem=recv_sems.at[k, j],
                device_id=(px, py, c), device_id_type=pl.DeviceIdType.MESH)

        def fwd(k, j):
            ref = dst(k, chip_of[j])
            return pltpu.make_async_remote_copy(
                src_ref=ref, dst_ref=ref, send_sem=fwd_send.at[k, j], recv_sem=fwd_recv.at[k, j],
                device_id=(x, y, 1 - c), device_id_type=pl.DeviceIdType.MESH)

        def wcopy(b, j, q):
            px, py = chips[j]
            return pltpu.make_async_remote_copy(
                src_ref=wins[b], dst_ref=wouts[b].at[q], send_sem=w_send.at[b, j], recv_sem=w_recv.at[b, j],
                device_id=(px, py, c), device_id_type=pl.DeviceIdType.MESH)

        local = [pltpu.make_async_copy(src(k), dst(k, mine), loc_sems.at[k]) for k in range(nk)]
        local += [pltpu.make_async_copy(wins[b], wouts[b].at[mine], w_loc.at[b]) for b in range(nw)]
        for z, (a, _, row0) in enumerate(zero_fills):
            local.append(pltpu.make_async_copy(zins[z], outs[a].at[row0:row0 + zins[z].shape[0]], z_sems.at[z]))
        wsends = [wcopy(b, j, mine) for b in range(nw) for j in range(3)]
        for cp in local + wsends:
            cp.start()

        for half in (0, 1):
            @pl.when(c == half)
            def _(half=half):
                my_k = [k for k in range(nk) if flat[k][1][5] == half]
                other_k = [k for k in range(nk) if flat[k][1][5] != half]
                sends = [ici(k, j, mine) for k in my_k for j in range(3)]
                for cp in sends:
                    cp.start()
                passed = []
                for k in my_k:
                    for j in range(3):
                        ici(k, j, chip_of[j]).wait_recv()
                        cp = fwd(k, j)
                        cp.start()
                        passed.append(cp)
                for k in other_k:
                    for j in range(3):
                        fwd(k, j).wait_recv()
                for cp in sends + passed:
                    cp.wait_send()

        for b in range(nw):
            for j in range(3):
                wcopy(b, j, chip_of[j]).wait_recv()
        for cp in wsends:
            cp.wait_send()
        for cp in local:
            cp.wait()

    hbm = pl.BlockSpec(memory_space=pl.ANY)
    dma = pltpu.SemaphoreType.DMA
    zeros = [z for _, z, _ in zero_fills]
    return pl.pallas_call(
        body, name="gather_weights",
        in_specs=[hbm] * (ns + nw + nz), out_specs=[hbm] * (ns + nw),
        out_shape=([jax.ShapeDtypeStruct((out_rows[a], split[a].shape[1]), split[a].dtype) for a in range(ns)]
                   + [jax.ShapeDtypeStruct((4,) + w.shape, w.dtype) for w in whole]),
        scratch_shapes=[dma((nk, 3)), dma((nk, 3)), dma((nk, 3)), dma((nk, 3)), dma((nk,)),
                        dma((nw, 3)), dma((nw, 3)), dma((nw,)), dma((nz,))],
        compiler_params=pltpu.CompilerParams(vmem_limit_bytes=VMEM_LIMIT),
    )(*split, *whole, *zeros)


def _reduce_grads(parts, small):
    n = len(parts)
    shapes = [a.shape[1:] for a in parts]
    halves = [(sh[0] // 2, sh[1]) for sh in shapes]

    def body(*refs):
        pin, sm_in = refs[:n], refs[n]
        gout, sm_out = refs[n + 1:2 * n + 1], refs[2 * n + 1]
        scr = refs[2 * n + 2:]
        own, sib, wire, rbuf = scr[:n], scr[n:2 * n], scr[2 * n:3 * n], scr[3 * n:4 * n]
        (sbuf, send_sems, recv_sems, loc_sems, pre_send, pre_recv, post_send, post_recv,
         sm_send, sm_recv) = scr[4 * n:]
        x, y, c = lax.axis_index("x"), lax.axis_index("y"), lax.axis_index("c")
        mine = 2 * x + y
        me = 4 * x + 2 * y + c
        sibling = (x, y, 1 - c)
        chips = [(1 - x, y), (x, 1 - y), (1 - x, 1 - y)]

        def rows(a, half):
            r2 = halves[a][0]
            return pl.ds(pl.multiple_of(half * r2, r2), r2)

        def pre(a):
            return pltpu.make_async_remote_copy(
                src_ref=pin[a].at[:, rows(a, 1 - c), :], dst_ref=sib[a], send_sem=pre_send.at[a],
                recv_sem=pre_recv.at[a], device_id=sibling, device_id_type=pl.DeviceIdType.MESH)

        def ici(a, j):
            px, py = chips[j]
            return pltpu.make_async_remote_copy(
                src_ref=wire[a].at[2 * px + py], dst_ref=rbuf[a].at[j], send_sem=send_sems.at[a, j],
                recv_sem=recv_sems.at[a, j], device_id=(px, py, c), device_id_type=pl.DeviceIdType.MESH)

        def post(a, half):
            ref = gout[a].at[rows(a, half), :]
            return pltpu.make_async_remote_copy(
                src_ref=ref, dst_ref=ref, send_sem=post_send.at[a], recv_sem=post_recv.at[a],
                device_id=sibling, device_id_type=pl.DeviceIdType.MESH)

        def small_copy(kk):
            peer = (x ^ (kk >> 2), y ^ ((kk >> 1) & 1), c ^ (kk & 1))
            return pltpu.make_async_remote_copy(
                src_ref=sm_in, dst_ref=sbuf.at[kk], send_sem=sm_send.at[kk - 1], recv_sem=sm_recv.at[kk - 1],
                device_id=peer, device_id_type=pl.DeviceIdType.MESH)

        local = [pltpu.make_async_copy(pin[a].at[:, rows(a, c), :], own[a], loc_sems.at[a]) for a in range(n)]
        pres = [pre(a) for a in range(n)]
        smalls = [small_copy(kk) for kk in range(1, 8)]
        for cp in local + pres + smalls:
            cp.start()
        sbuf[0] = sm_in[...]
        sends = []
        for a in range(n):
            local[a].wait()
            pres[a].wait_recv()
            for blk in range(4):
                tot = own[a][blk] + sib[a][blk]
                own[a][blk] = tot
                wire[a][blk] = tot.astype(BF16)
            for j in range(3):
                cp = ici(a, j)
                cp.start()
                sends.append(cp)
        for cp in smalls:
            cp.wait_recv()
        total = sbuf[me]
        for d in range(1, 8):
            total = total + sbuf[me ^ d]
        sm_out[...] = total
        posts = []
        for a in range(n):
            for j in range(3):
                ici(a, j).wait_recv()
            fin = own[a][mine]
            for j in range(3):
                fin = fin + rbuf[a][j].astype(F32)
            gout[a][rows(a, c), :] = fin
            cp = post(a, c)
            cp.start()
            posts.append(cp)
        for a in range(n):
            post(a, 1 - c).wait_recv()
        for cp in pres + sends + smalls + posts:
            cp.wait_send()

    hbm = pl.BlockSpec(memory_space=pl.ANY)
    vmem = pl.BlockSpec(memory_space=pltpu.VMEM)
    dma = pltpu.SemaphoreType.DMA
    return pl.pallas_call(
        body, name="reduce_grads",
        in_specs=[hbm] * n + [vmem], out_specs=[vmem] * (n + 1),
        out_shape=[jax.ShapeDtypeStruct(sh, F32) for sh in shapes] + [jax.ShapeDtypeStruct(small.shape, F32)],
        scratch_shapes=([pltpu.VMEM((4,) + hs, F32) for hs in halves] + [pltpu.VMEM((4,) + hs, F32) for hs in halves]
                        + [pltpu.VMEM((4,) + hs, BF16) for hs in halves]
                        + [pltpu.VMEM((3,) + hs, BF16) for hs in halves]
                        + [pltpu.VMEM((8,) + small.shape, F32), dma((n, 3)), dma((n, 3)), dma((n,)),
                           dma((n,)), dma((n,)), dma((n,)), dma((n,)), dma((7,)), dma((7,))]),
        compiler_params=pltpu.CompilerParams(vmem_limit_bytes=VMEM_LIMIT),
    )(*parts, small)


def _adamw(w, g, m, v, name):
    shape = w.shape
    w2, g2, m2, v2 = (a.reshape((-1, shape[-1])) for a in (w, g, m, v))

    def body(w_ref, g_ref, m_ref, v_ref, d_ref, nm_ref, nv_ref):
        gv = g_ref[...]
        nm = ADAM_B1 * m_ref[...] + (1.0 - ADAM_B1) * gv
        nv = ADAM_B2 * v_ref[...] + (1.0 - ADAM_B2) * (gv * gv)
        m_hat = nm / (1.0 - ADAM_B1 ** ADAM_STEP)
        v_hat = nv / (1.0 - ADAM_B2 ** ADAM_STEP)
        d_ref[...] = -ADAM_LR * (m_hat / (jnp.sqrt(v_hat) + ADAM_EPS) + ADAM_WD * w_ref[...])
        nm_ref[...] = nm
        nv_ref[...] = nv

    rows, cols = w2.shape
    nblk = cols // 256 if cols % 256 == 0 and rows >= 64 else 1
    blk = pl.BlockSpec((rows, cols // nblk), lambda j: (0, j))
    out = pl.pallas_call(
        body, name=name, grid=(nblk,), in_specs=[blk] * 4, out_specs=[blk] * 3,
        out_shape=[jax.ShapeDtypeStruct(w2.shape, F32)] * 3,
        compiler_params=_cparams("parallel"),
    )(w2, g2, m2, v2)
    return tuple(a.reshape(shape) for a in out)


def kernel(x, meta_tokens, norm_g, w_in, q_norm_g, w_q_up, kv_norm_g, w_kv_up, conv_w, attn_out_g, conv_out_g, w_out, final_norm_g, loss_target, m_meta_tokens, m_norm_g, m_w_in, m_q_norm_g, m_w_q_up, m_kv_norm_g, m_w_kv_up, m_conv_w, m_attn_out_g, m_conv_out_g, m_w_out, m_final_norm_g, v_meta_tokens, v_norm_g, v_w_in, v_q_norm_g, v_w_q_up, v_kv_norm_g, v_w_kv_up, v_conv_w, v_attn_out_g, v_conv_out_g, v_w_out, v_final_norm_g):
    nb, s, _ = x.shape
    tm = min(ROW_TILE, s)
    ta = min(ATTN_TILE, s)
    assert s % tm == 0 and s % ta == 0 and tm % 16 == 0
    r = nb * s

    tr = lambda a: jnp.transpose(a[0])
    w_in_p, wq_p, wkv_p, g_cw, g_meta = _gather_weights(
        [tr(w_in).astype(BF16), tr(w_q_up).astype(BF16), tr(w_kv_up).astype(BF16)],
        [W_IN_PIECES, W_Q_PIECES, W_KV_PIECES], [IN_PAD, HEADS * QK_PAD, 1024],
        [conv_w[0], meta_tokens],
        [(0, jnp.zeros((64, D_MODEL), BF16), 448)]
        + [(1, jnp.zeros((64, Q_RANK), BF16), QK_PAD * h + NOPE + ROPE) for h in range(HEADS)])
    conv_f = jnp.transpose(g_cw, (1, 0, 2)).reshape(3, CONV_W)
    meta_f = jnp.transpose(g_meta, (1, 0, 2)).reshape(N_META, D_MODEL)

    c_all, sa_all, sb_all = _rope_tables(N_META + s)
    tabs_m = (c_all[:N_META], sa_all[:N_META], sb_all[:N_META])
    tabs = (c_all[N_META:], sa_all[N_META:], sb_all[N_META:])
    gid = np.arange(CONV_W) // CONV_GROUP
    gmat = jnp.asarray(np.where(gid[:, None] == gid[None, :], 1.0 / CONV_GROUP, 0.0), BF16)
    ga, gc = attn_out_g, conv_out_g
    gf = final_norm_g.reshape(1, D_MODEL)

    x2d = x.reshape(r, D_MODEL)
    tgt2d = loss_target.reshape(r, D_MODEL)

    p, q, k, v, pm, km, vm, w_out_f = _fwd_proj(x2d, meta_f, tabs, tabs_m, norm_g, w_in_p, q_norm_g, wq_p,
                                                kv_norm_g, wkv_p, w_out[0].astype(BF16), nb, s, tm)
    o, lse = _attn_fwd(q, k, v, km, vm, nb, s, ta)
    dh2, dycat, dw_out, dgf, loss_acc = _out_fwd_bwd(x2d, tgt2d, o, p, pm, conv_f, ga, gc, gmat, w_out_f, gf,
                                                     nb, s, tm)
    dpb, do, delta, dccm, dga, dgc, dcw = _gate_bwd(dycat, o, p, pm, conv_f, ga, gc, gmat, nb, s, tm)
    p_out = dw_out.reshape(4, D_MODEL // 4, D_MODEL)
    dq, dk, dv, dkm, dvm, g_w_out = _attn_bwd(q, k, v, do, lse, delta, km, vm, [p_out], nb, s, ta)
    dpa, dpam, p_q, p_kv, dgq, dgkv = _up_bwd(dq, dk, dv, dkm, dvm, p, pm, tabs, tabs_m, wq_p, wkv_p,
                                              q_norm_g, kv_norm_g, nb, s, tm)
    gx, gmeta, p_in, dng = _in_bwd(x2d, dh2, dpa, dpb, meta_f, dpam, dccm, pm, w_in_p, norm_g, nb, s, tm)

    flat =jnp.concatenate([dng.reshape(-1), dgq.reshape(-1), dgkv.reshape(-1), dga.reshape(-1), dgc.reshape(-1),
                            dgf.reshape(-1), dcw[:3].reshape(-1), gmeta.reshape(-1), loss_acc[0, 0:1]])
    n_small = flat.shape[0]
    rows_small = -(-n_small // 1024) * 8
    small = jnp.pad(flat, (0, rows_small * 128 - n_small)).reshape(rows_small, 128)
    g_w_in_t, g_w_q_t, g_w_kv_t, small_sum = _reduce_grads([p_in, p_q, p_kv], small)
    ssum = small_sum.reshape(-1)

    def take(off, n):
        return ssum[off:off + n], off + n

    off = 0
    g_norm, off = take(off, D_MODEL)
    g_qn, off = take(off, Q_RANK)
    g_kvn, off = take(off, KV_RANK)
    g_ga, off = take(off, CONV_W)
    g_gc, off = take(off, CONV_W)
    g_gf, off = take(off, D_MODEL)
    g_cw_all, off = take(off, 3 * CONV_W)
    g_meta_all, off = take(off, N_META * D_MODEL)
    loss = ssum[off]
    chip = 2 * lax.axis_index("x") + lax.axis_index("y")
    g_conv = lax.dynamic_slice(g_cw_all.reshape(3, CONV_W), (0, chip * 128), (3, 128))
    g_mt = lax.dynamic_slice(g_meta_all.reshape(N_META, D_MODEL), (0, chip * 256), (N_META, 256))

    grads = {
        "meta_tokens": g_mt, "norm_g": g_norm.reshape(1, -1), "w_in": g_w_in_t, "q_norm_g": g_qn.reshape(1, -1),
        "w_q_up": g_w_q_t, "kv_norm_g": g_kvn.reshape(1, -1), "w_kv_up": jnp.transpose(g_w_kv_t)[None],
        "conv_w": g_conv[None], "attn_out_g": g_ga.reshape(1, -1), "conv_out_g": g_gc.reshape(1, -1),
        "w_out": g_w_out[None], "final_norm_g": g_gf,
    }
    transposed = ("w_in", "w_q_up")
    weights = {
        "meta_tokens": (meta_tokens, m_meta_tokens, v_meta_tokens), "norm_g": (norm_g, m_norm_g, v_norm_g),
        "w_in": (w_in, m_w_in, v_w_in), "q_norm_g": (q_norm_g, m_q_norm_g, v_q_norm_g),
        "w_q_up": (w_q_up, m_w_q_up, v_w_q_up), "kv_norm_g": (kv_norm_g, m_kv_norm_g, v_kv_norm_g),
        "w_kv_up": (w_kv_up, m_w_kv_up, v_w_kv_up), "conv_w": (conv_w, m_conv_w, v_conv_w),
        "attn_out_g": (attn_out_g, m_attn_out_g, v_attn_out_g), "conv_out_g": (conv_out_g, m_conv_out_g, v_conv_out_g),
        "w_out": (w_out, m_w_out, v_w_out), "final_norm_g": (final_norm_g, m_final_norm_g, v_final_norm_g),
    }
    names = list(weights)
    deltas, new_m, new_v = [], [], []
    for nme in names:
        w_, m_, v_ = weights[nme]
        if nme in transposed:
            res = _adamw(tr(w_), grads[nme], tr(m_), tr(v_), "adamw_" + nme)
            g_, d_, nm_, nv_ = (jnp.transpose(a)[None] for a in (grads[nme],) + res)
        else:
            g_ = grads[nme].reshape(w_.shape)
            d_, nm_, nv_ = _adamw(w_, g_, m_, v_, "adamw_" + nme)
        grads[nme] = g_
        deltas.append(d_)
        new_m.append(nm_)
        new_v.append(nv_)

    grad_x = gx.reshape(nb, s, D_MODEL)
    return (loss, grad_x, *[grads[nme] for nme in names], *deltas, *new_m, *new_v)
```

```python
import functools

import jax
import jax.numpy as jnp
import numpy as np
from jax import lax
from jax.experimental import pallas as pl
from jax.experimental.pallas import tpu as pltpu

F32 = jnp.float32
BF16 = jnp.bfloat16

D_MODEL = 1024
N_META = 16
HEADS = 4
NOPE = 128
ROPE = 64
VDIM = 128
QK_PAD = 256
Q_RANK = 256
KV_RANK = 128
CONV_W = 512
CONV_GROUP = 64
ROPE_THETA = 10000.0
EPS = 1e-6
ATTN_SCALE = (NOPE + ROPE) ** -0.5
IN_DIM = 3008
IN_PAD = 3072
BLK_ZA, BLK_CB, BLK_CC, BLK_CH, BLK_ZC = 1, 2, 3, 4, 5
NEG_INF = -1e30

ADAM_LR = 0.001
ADAM_B1 = 0.9
ADAM_B2 = 0.999
ADAM_EPS = 1e-08
ADAM_WD = 0.01
ADAM_STEP = 10

ROW_TILE = 512
ATTN_TILE = 256
VMEM_LIMIT = 56 * 1024 * 1024

NT = (((1,), (1,)), ((), ()))
TN = (((0,), (0,)), ((), ()))


def _cparams(*sem):
    return pltpu.CompilerParams(dimension_semantics=sem, vmem_limit_bytes=VMEM_LIMIT)


def _dot(a, b):
    return jnp.dot(a, b, preferred_element_type=F32)


def _dot_nt(a, b):
    return lax.dot_general(a, b, NT, preferred_element_type=F32)


def _dot_tn(a, b):
    return lax.dot_general(a, b, TN, preferred_element_type=F32)


def _rms(x, g):
    r = lax.rsqrt(jnp.mean(x * x, axis=-1, keepdims=True) + EPS)
    return x * r * g, r


def _rms_bwd(dy, x, r, g):
    xh = x * r
    dyg = dy * g
    dx = r * (dyg - xh * jnp.mean(dyg * xh, axis=-1, keepdims=True))
    return dx, dy * xh


def _sigmoid(z):
    return 1.0 / (1.0 + jnp.exp(-z))


def _rope(b, c, sa, sb):
    return b * c + pltpu.roll(b, 96, 1) * sa + pltpu.roll(b, 32, 1) * sb


def _rope_bwd(d, c, sa, sb):
    return d * c + pltpu.roll(d * sa, 32, 1) + pltpu.roll(d * sb, 96, 1)


def _group_mean(x, gmat):
    hi = x.astype(BF16)
    lo = (x - hi.astype(F32)).astype(BF16)
    return _dot(hi, gmat) + _dot(lo, gmat)


def _row_of(col, rows):
    return jnp.transpose(jnp.broadcast_to(col, (rows, 128)))[0:1, :]


def _rope_tables(n_pos):
    half = ROPE // 2
    inv_freq = (np.float32(1.0) / (np.float32(ROPE_THETA) ** (np.arange(half, dtype=np.float32) / np.float32(half))))
    ang = np.arange(n_pos, dtype=np.float32)[:, None] * inv_freq.astype(np.float32)[None, :]
    cos, sin = np.cos(ang).astype(np.float32), np.sin(ang).astype(np.float32)
    z = np.zeros((n_pos, half), np.float32)
    c = np.concatenate([cos, cos, z, z], axis=1)
    sa = np.concatenate([-sin, z, z, z], axis=1)
    sb = np.concatenate([z, sin, z, z], axis=1)
    return jnp.asarray(c), jnp.asarray(sa), jnp.asarray(sb)


W_IN_PIECES = ((0, 384, 752, 0, 64, 0), (384, 64, 752, 384, 448, 1), (448, 304, 752, 512, 512, 1))
W_Q_PIECES = ((0, 96, 256, 0, 0, 0), (96, 96, 256, 96, 96, 1))
W_KV_PIECES = ((0, 128, 128, 0, 0, 0), (128, 128, 128, 512, 512, 1))
W_OUT_PIECES = ((0, 128, 256, 0, 0, 0), (128, 128, 256, 128, 128, 1))


class _StagedGather:
    def __init__(self, pieces):
        self.pieces = pieces

    def scratch(self):
        nk, dma = len(self.pieces), pltpu.SemaphoreType.DMA
        return [dma((nk, 3)), dma((nk, 3)), dma((nk, 3)), dma((nk, 3)), dma((nk,))]

    def run(self, stage, src_ref, out_ref, scr):
        send_sems, recv_sems, fwd_send, fwd_recv, loc_sems = scr
        pieces = self.pieces
        nk = len(pieces)
        x, y, c = lax.axis_index("x"), lax.axis_index("y"), lax.axis_index("c")
        mine = 2 * x + y
        chips = [(1 - x, y), (x, 1 - y), (1 - x, 1 - y)]
        chip_of = [2 * px + py for px, py in chips]
        mesh = pl.DeviceIdType.MESH

        def src(k):
            s0, nr = pieces[k][0], pieces[k][1]
            return src_ref.at[s0:s0 + nr]

        def dst(k, q):
            _, nr, per, first, rest, _ = pieces[k]
            row = per * q + first + (rest - first) * jnp.minimum(q, 1)
            return out_ref.at[pl.ds(pl.multiple_of(row, 16), nr)]

        def ici(k, j, q):
            px, py = chips[j]
            return pltpu.make_async_remote_copy(
                src_ref=src(k), dst_ref=dst(k, q), send_sem=send_sems.at[k, j], recv_sem=recv_sems.at[k, j],
                device_id=(px, py, c), device_id_type=mesh)

        def fwd(k, j):
            ref = dst(k, chip_of[j])
            return pltpu.make_async_remote_copy(
                src_ref=ref, dst_ref=ref, send_sem=fwd_send.at[k, j], recv_sem=fwd_recv.at[k, j],
                device_id=(x, y, 1 - c), device_id_type=mesh)

        local = [pltpu.make_async_copy(src(k), dst(k, mine), loc_sems.at[k]) for k in range(nk)]
        if stage == 0:
            for cp in local:
                cp.start()
        if stage == 2:
            for cp in local:
                cp.wait()
        for half in (0, 1):
            @pl.when(c == half)
            def _(half=half):
                my_k = [k for k in range(nk) if pieces[k][5] == half]
                other_k = [k for k in range(nk) if pieces[k][5] != half]
                for k in my_k:
                    for j in range(3):
                        if stage == 0:
                            ici(k, j, mine).start()
                        elif stage == 1:
                            ici(k, j, chip_of[j]).wait_recv()
                            fwd(k, j).start()
                        else:
                            ici(k, j, mine).wait_send()
                            fwd(k, j).wait_send()
                if stage == 2:
                    for k in other_k:
                        for j in range(3):
                            fwd(k, j).wait_recv()


def _fwd_proj(x2d, meta, tabs, tabs_m, norm_g, w_in_p, q_norm_g, wq_p, kv_norm_g, wkv_p, w_out_shard, nb, s, tm):
    nt = s // tm
    n = nb * nt
    n_steps = n + 1
    c_t, sa_t, sb_t = tabs
    cm_t, sam_t, sbm_t = tabs_m
    gat = _StagedGather(W_OUT_PIECES)
    assert n_steps >= 3

    def body(x_ref, c_ref, sa_ref, sb_ref, mt_ref, cm_ref, sam_ref, sbm_ref,
             g_ref, w_ref, gq_ref, wq_ref, gkv_ref, wkv_ref, wos_ref,
             p_ref, q_ref, k_ref, v_ref, pm_ref, km_ref, vm_ref, wo_ref, *gat_scr):
        i = pl.program_id(0)
        for stage, at in enumerate((0, n_steps - 2, n_steps - 1)):
            @pl.when(i == at)
            def _(stage=stage):
                gat.run(stage, wos_ref, wo_ref, gat_scr)

        def project(xv, c, sa, sb, p_out, q_out, k_out, v_out):
            u, _ = _rms(xv, g_ref[...])
            p = _dot_nt(u.astype(BF16), w_ref[...])
            p_out[...] = p
            qn, _ = _rms(p[:, 0:Q_RANK], gq_ref[...])
            q = _dot_nt(qn.astype(BF16), wq_ref[...])
            kvn, _ = _rms(p[:, Q_RANK:Q_RANK + KV_RANK], gkv_ref[...])
            kv = _dot_nt(kvn.astype(BF16), wkv_ref[...])
            kpe = _rope(p[:, 384:512], c, sa, sb)
            for h in range(HEADS):
                if q_out is not None:
                    pe = _rope(q[:, QK_PAD * h + NOPE:QK_PAD * (h + 1)], c, sa, sb)
                    qh = jnp.concatenate([q[:, QK_PAD * h:QK_PAD * h + NOPE], pe], axis=1)
                    q_out[0, h] = (qh * ATTN_SCALE).astype(BF16)
                k_out[0, h] = jnp.concatenate([kv[:, NOPE * h:NOPE * (h + 1)], kpe], axis=1).astype(BF16)
                v_out[0, h] = kv[:, 512 + VDIM * h:512 + VDIM * (h + 1)].astype(BF16)

        @pl.when(i < n)
        def _():
            project(x_ref[...], c_ref[...], sa_ref[...], sb_ref[...], p_ref, q_ref, k_ref, v_ref)

        @pl.when(i == n)
        def _():
            project(mt_ref[...], cm_ref[...], sam_ref[...], sbm_ref[...], pm_ref, None, km_ref, vm_ref)

    cl = lambda i: jnp.minimum(i, n - 1)
    full = lambda a: pl.BlockSpec(a.shape, lambda i: (0,) * a.ndim)
    const = lambda shape: pl.BlockSpec(shape, lambda i: (0,) * len(shape))
    tab = pl.BlockSpec((tm, 128), lambda i: (cl(i) % nt, 0))
    hb = lambda w: pl.BlockSpec((1, HEADS, tm, w), lambda i: (cl(i) // nt, 0, cl(i) % nt, 0))
    hbm = pl.BlockSpec(memory_space=pl.ANY)
    return pl.pallas_call(
        body, name="fwd_proj", grid=(n_steps,),
        in_specs=[pl.BlockSpec((tm, D_MODEL), lambda i: (cl(i), 0)), tab, tab, tab,
                  full(meta), full(cm_t), full(sam_t), full(sbm_t),
                  full(norm_g), full(w_in_p), full(q_norm_g), full(wq_p), full(kv_norm_g), full(wkv_p), hbm],
        out_specs=[pl.BlockSpec((tm, IN_PAD), lambda i: (cl(i), 0)), hb(QK_PAD), hb(QK_PAD), hb(VDIM),
                   const((N_META, IN_PAD)), const((1, HEADS, N_META, QK_PAD)), const((1, HEADS, N_META, VDIM)), hbm],
        out_shape=[jax.ShapeDtypeStruct((nb * s, IN_PAD), F32),
                   jax.ShapeDtypeStruct((nb, HEADS, s, QK_PAD), BF16),
                   jax.ShapeDtypeStruct((nb, HEADS, s, QK_PAD), BF16),
                   jax.ShapeDtypeStruct((nb, HEADS, s, VDIM), BF16),
                   jax.ShapeDtypeStruct((N_META, IN_PAD), F32),
                   jax.ShapeDtypeStruct((1, HEADS, N_META, QK_PAD), BF16),
                   jax.ShapeDtypeStruct((1, HEADS, N_META, VDIM), BF16),
                   jax.ShapeDtypeStruct((D_MODEL, D_MODEL), BF16)],
        scratch_shapes=gat.scratch(),
        compiler_params=_cparams("arbitrary"),
    )(x2d, c_t, sa_t, sb_t, meta, cm_t, sam_t, sbm_t, norm_g, w_in_p, q_norm_g, wq_p, kv_norm_g, wkv_p, w_out_shard)


def _attn_fwd(q, k, v, km, vm, nb, s, tq):
    nq = s // tq

    def body(q_ref, k_ref, v_ref, km_ref, vm_ref, o_ref, lse_ref, s_scr, p_scr):
        row = lax.broadcasted_iota(jnp.int32, (tq, tq), 0)
        col = lax.broadcasted_iota(jnp.int32, (tq, tq), 1)
        def scores(i):
            slot = i % 2
            qi = q_ref[0, 0, i * tq:(i + 1) * tq, :]
            sm = _dot_nt(qi, km_ref[0, 0])
            m128 = None
            for j in range(i + 1):
                sc = _dot_nt(qi, k_ref[0, 0, j * tq:(j + 1) * tq, :])
                if j == i:
                    sc = jnp.where(col <= row, sc, NEG_INF)
                s_scr[slot, :, j * tq:(j + 1) * tq] = sc
                mx = sc[:, 0:128]
                for c0 in range(128, tq, 128):
                    mx = jnp.maximum(mx, sc[:, c0:c0 + 128])
                m128 = mx if m128 is None else jnp.maximum(m128, mx)
            return sm, jnp.maximum(jnp.max(m128, axis=1, keepdims=True), jnp.max(sm, axis=1, keepdims=True))

        nxt = scores(0)
        for i in range(nq):
            slot = i % 2
            sm, m = nxt
            if i + 1 < nq:
                nxt = scores(i + 1)
            pm = jnp.exp(sm - m)
            l128 = None
            for j in range(i + 1):
                p = jnp.exp(s_scr[slot, :, j * tq:(j + 1) * tq] - m)
                p_scr[slot, :, j * tq:(j + 1) * tq] = p.astype(BF16)
                ps = p[:, 0:128]
                for c0 in range(128, tq, 128):
                    ps = ps + p[:, c0:c0 + 128]
                l128 = ps if l128 is None else l128 + ps
            l = jnp.sum(l128, axis=1, keepdims=True) + jnp.sum(pm, axis=1, keepdims=True)
            n = (i + 1) * tq
            acc = _dot(p_scr[slot, :, 0:n], v_ref[0, 0, 0:n, :]) + _dot(pm.astype(BF16), vm_ref[0, 0])
            o_ref[0, 0, i * tq:(i + 1) * tq, :] = acc / l
            lse_ref[0, 0, :, i * tq:(i + 1) * tq] = _row_of(m + jnp.log(l), tq)

    hblk = lambda w: pl.BlockSpec((1, 1, s, w), lambda b, h: (b, h, 0, 0))
    mblk = lambda w: pl.BlockSpec((1, 1, N_META, w), lambda b, h: (0, h, 0, 0))
    return pl.pallas_call(
        body, name="attn_fwd", grid=(nb, HEADS),
        in_specs=[hblk(QK_PAD), hblk(QK_PAD), hblk(VDIM), mblk(QK_PAD), mblk(VDIM)],
        out_specs=[hblk(VDIM), pl.BlockSpec((1, 1, 1, s), lambda b, h: (b, h, 0, 0))],
        out_shape=[jax.ShapeDtypeStruct((nb, HEADS, s, VDIM), F32),
                   jax.ShapeDtypeStruct((nb, HEADS, 1, s), F32)],
        scratch_shapes=[pltpu.VMEM((2, tq, s), F32), pltpu.VMEM((2, tq, s), BF16)],
        compiler_params=_cparams("parallel", "parallel"),
    )(q, k, v, km, vm)


def _shift_rows(a, prev, n_rows):
    rid = lax.broadcasted_iota(jnp.int32, a.shape, 0)
    a1 = jnp.where(rid == 0, prev[7:8, :], pltpu.roll(a, 1, 0))
    a2 = jnp.where(rid == 0, prev[6:7, :], jnp.where(rid == 1, prev[7:8, :], pltpu.roll(a, 2, 0)))
    return a1, a2


def _attn_gate(o, za, ga_h):
    on, r = _rms(o, ga_h)
    return on * (za * _sigmoid(za)), on, r


def _out_fwd_bwd(x2d, tgt2d, o, p, pm, conv_w, ga, gc, gmat, w_out, gf, nb, s, tm):
    nt = s // tm
    r = nb * s
    prev_idx = lambda i: jnp.maximum(i * (tm // 8) - 1, 0)

    def body(x_ref, t_ref, o_ref, za_ref, cb_ref, cc_ref, ch_ref, zc_ref, ccp_ref, chp_ref, mc_ref, mh_ref,
             cw_ref, ga_ref, gc_ref, gm_ref, w_ref, gf_ref,
             dh_ref, dy_ref, dw_ref, dgf_ref, loss_ref):
        i = pl.program_id(0)

        @pl.when(i == 0)
        def _():
            dw_ref[...] = jnp.zeros_like(dw_ref)
            dgf_ref[...] = jnp.zeros_like(dgf_ref)
            loss_ref[...] = jnp.zeros_like(loss_ref)

        ya = []
        for h in range(HEADS):
            y, _, _ = _attn_gate(o_ref[0, h], za_ref[:, VDIM * h:VDIM * (h + 1)],
                                 ga_ref[:, VDIM * h:VDIM * (h + 1)])
            ya.append(y)
        cc = cc_ref[...] * ch_ref[...]
        prev = jnp.where(i % nt == 0, mc_ref[8:16, :] * mh_ref[8:16, :], ccp_ref[...] * chp_ref[...])
        cc1, cc2 = _shift_rows(cc, prev, tm)
        yc = cb_ref[...] * (cw_ref[0:1, :] * cc2 + cw_ref[1:2, :] * cc1 + cw_ref[2:3, :] * cc)
        rg = lax.rsqrt(_group_mean(yc * yc, gm_ref[...]) + EPS)
        zc = zc_ref[...]
        yconv = yc * rg * gc_ref[...] * (zc * _sigmoid(zc))
        ycat = jnp.concatenate(ya + [yconv], axis=1).astype(BF16)
        h2 = x_ref[...] + _dot(ycat, w_ref[...])
        gfv = gf_ref[...]
        y, r2 = _rms(h2, gfv)
        e = y - t_ref[...]
        loss_ref[...] += 0.5 * jnp.sum(e * e) / D_MODEL
        dyv = e * (1.0 / D_MODEL)
        dh2, dgf = _rms_bwd(dyv, h2, r2, gfv)
        dgf_ref[...] += jnp.sum(dgf, axis=0, keepdims=True)
        dh_ref[...] = dh2
        dhb = dh2.astype(BF16)
        dy_ref[...] = _dot_nt(dhb, w_ref[...])
        dw_ref[...] += _dot_tn(ycat, dhb)

    row = lambda w, j: pl.BlockSpec((tm, w), lambda i: (i, j))
    pblk = lambda j: pl.BlockSpec((tm, 512), lambda i: (i, j))
    pprev = lambda j: pl.BlockSpec((8, 512), lambda i: (prev_idx(i), j))
    mblk = lambda j: pl.BlockSpec((N_META, 512), lambda i: (0, j))
    full = lambda a: pl.BlockSpec(a.shape, lambda i: (0,) * a.ndim)
    return pl.pallas_call(
        body, name="out_fwd_bwd", grid=(nb * nt,),
        in_specs=[row(D_MODEL, 0), row(D_MODEL, 0),
                  pl.BlockSpec((1, HEADS, tm, VDIM), lambda i: (i // nt, 0, i % nt, 0)),
                  pblk(BLK_ZA), pblk(BLK_CB), pblk(BLK_CC), pblk(BLK_CH), pblk(BLK_ZC),
                  pprev(BLK_CC), pprev(BLK_CH), mblk(BLK_CC), mblk(BLK_CH),
                  full(conv_w), full(ga), full(gc), full(gmat), full(w_out), full(gf)],
        out_specs=[row(D_MODEL, 0), row(D_MODEL, 0),
                   pl.BlockSpec((D_MODEL, D_MODEL), lambda i: (0, 0)),
                   pl.BlockSpec((1, D_MODEL), lambda i: (0, 0)),
                   pl.BlockSpec((1, 128), lambda i: (0, 0))],
        out_shape=[jax.ShapeDtypeStruct((r, D_MODEL), F32), jax.ShapeDtypeStruct((r, D_MODEL), F32),
                   jax.ShapeDtypeStruct((D_MODEL, D_MODEL), F32), jax.ShapeDtypeStruct((1, D_MODEL), F32),
                   jax.ShapeDtypeStruct((1, 128), F32)],
        compiler_params=_cparams("arbitrary"),
    )(x2d, tgt2d, o, p, p, p, p, p, p, p, pm, pm, conv_w, ga, gc, gmat, w_out, gf)


def _gate_bwd(dycat, o, p, pm, conv_w, ga, gc, gmat, nb, s, tm):
    nt = s // tm
    r = nb * s
    ext = tm + 8
    prev_idx = lambda i: jnp.maximum(i * (tm // 8) - 1, 0)
    next_idx = lambda i: jnp.minimum((i + 1) * (tm // 8), r // 8 - 1)

    def body(dya_ref, dyc_ref, dycn_ref, o_ref, za_ref, cb_ref, cbn_ref, cc_ref, ccp_ref, ccn_ref,
             ch_ref, chp_ref, chn_ref, zc_ref, zcn_ref, mc_ref, mh_ref, cw_ref, ga_ref, gc_ref, gm_ref,
             dpb_ref, do_ref, dl_ref, dccm_ref, dga_ref, dgc_ref, dcw_ref):
        i = pl.program_id(0)

        @pl.when(i == 0)
        def _():
            dga_ref[...] = jnp.zeros_like(dga_ref)
            dgc_ref[...] = jnp.zeros_like(dgc_ref)
            dcw_ref[...] = jnp.zeros_like(dcw_ref)

        dga = []
        for h in range(HEADS):
            hs = slice(VDIM * h, VDIM * (h + 1))
            oh, za, gah, dya = o_ref[0, h], za_ref[:, hs], ga_ref[:, hs], dya_ref[:, hs]
            sg = _sigmoid(za)
            on, ro = _rms(oh, gah)
            don = dya * (za * sg)
            dpb_ref[:, hs] = (dya * on * (sg * (1.0 + za * (1.0 - sg)))).astype(BF16)
            do, dg = _rms_bwd(don, oh, ro, gah)
            dga.append(jnp.sum(dg, axis=0, keepdims=True))
            dob = do.astype(BF16)
            do_ref[0, h] = dob
            dl_ref[0, h] = _row_of(jnp.sum(dob.astype(F32) * oh, axis=1, keepdims=True), tm)
        dga_ref[...] += jnp.concatenate(dga, axis=1)

        cat = lambda a, b: jnp.concatenate([a[...], b[...]], axis=0)
        cch = cat(cc_ref, ccn_ref)
        chh = cat(ch_ref, chn_ref)
        cb = cat(cb_ref, cbn_ref)
        zc = cat(zc_ref, zcn_ref)
        dy = cat(dyc_ref, dycn_ref)
        first = i % nt == 0
        last = i % nt == nt - 1
        cc = cch * chh
        prev = jnp.where(first, mc_ref[8:16, :] * mh_ref[8:16, :], ccp_ref[...] * chp_ref[...])
        cc1, cc2 = _shift_rows(cc, prev, ext)
        w0, w1, w2 = cw_ref[0:1, :], cw_ref[1:2, :], cw_ref[2:3, :]
        dw = w0 * cc2 + w1 * cc1 + w2 * cc
        yc = cb * dw
        rg = lax.rsqrt(_group_mean(yc * yc, gm_ref[...]) + EPS)
        ych = yc * rg
        gcv = gc_ref[...]
        sg = _sigmoid(zc)
        dycn = dy * (zc * sg)
        dzc = dy * (ych * gcv) * (sg * (1.0 + zc * (1.0 - sg)))
        dgc_ref[...] += jnp.sum((dycn * ych)[:tm], axis=0, keepdims=True)
        dycg = dycn * gcv
        dyc = rg * (dycg - ych * _group_mean(dycg * ych, gm_ref[...]))
        rid = lax.broadcasted_iota(jnp.int32, (ext, CONV_W), 0)
        ddw = jnp.where(jnp.logical_and(last, rid >= tm), 0.0, dyc * cb)
        dcb = dyc * dw
        dcc = w2 * ddw + w1 * pltpu.roll(ddw, ext - 1, 0) + w0 * pltpu.roll(ddw, ext - 2, 0)
        dpb_ref[:, 512:1024] = dcb[:tm].astype(BF16)
        dpb_ref[:, 1024:1536] = (dcc * chh)[:tm].astype(BF16)
        dpb_ref[:, 1536:2048] = (dcc * cch)[:tm].astype(BF16)
        dpb_ref[:, 2048:2560] = dzc[:tm].astype(BF16)
        rs = lambda a: jnp.sum(a[:tm], axis=0, keepdims=True)
        dcw_ref[0:1, :] += rs(ddw * cc2)
        dcw_ref[1:2, :] += rs(ddw * cc1)
        dcw_ref[2:3, :] += rs(ddw * cc)

        @pl.when(first)
        def _():
            d0, d1 = ddw[0:1, :], ddw[1:2, :]
            r8 = lax.broadcasted_iota(jnp.int32, (8, CONV_W), 0)
            dccm_ref[0] = jnp.where(r8 == 7, w1 * d0 + w0 * d1, jnp.where(r8 == 6, w0 * d0, 0.0))

    row = lambda j: pl.BlockSpec((tm, 512), lambda i: (i, j))
    prv = lambda j: pl.BlockSpec((8, 512), lambda i: (prev_idx(i), j))
    nxt = lambda j: pl.BlockSpec((8, 512), lambda i: (next_idx(i), j))
    mblk = lambda j: pl.BlockSpec((N_META, 512), lambda i: (0, j))
    full = lambda a: pl.BlockSpec(a.shape, lambda i: (0,) * a.ndim)
    hb = lambda w: pl.BlockSpec((1, HEADS, tm, w), lambda i: (i // nt, 0, i % nt, 0))
    acc = lambda rr: pl.BlockSpec((rr, 512), lambda i: (0, 0))
    return pl.pallas_call(
        body, name="gate_bwd", grid=(nb * nt,),
        in_specs=[row(0), row(1), nxt(1), hb(VDIM),
                  row(BLK_ZA), row(BLK_CB), nxt(BLK_CB), row(BLK_CC), prv(BLK_CC), nxt(BLK_CC),
                  row(BLK_CH), prv(BLK_CH), nxt(BLK_CH), row(BLK_ZC), nxt(BLK_ZC),
                  mblk(BLK_CC), mblk(BLK_CH), full(conv_w), full(ga), full(gc), full(gmat)],
        out_specs=[pl.BlockSpec((tm, 2560), lambda i: (i, 0)), hb(VDIM),
                   pl.BlockSpec((1, HEADS, 1, tm), lambda i: (i // nt, 0, 0, i % nt)),
                   pl.BlockSpec((1, 8, 512), lambda i: (i // nt, 0, 0)),
                   acc(1), acc(1), acc(8)],
        out_shape=[jax.ShapeDtypeStruct((r, 2560), BF16), jax.ShapeDtypeStruct((nb, HEADS, s, VDIM), BF16),
                   jax.ShapeDtypeStruct((nb, HEADS, 1, s), F32), jax.ShapeDtypeStruct((nb, 8, 512), F32),
                   jax.ShapeDtypeStruct((1, 512), F32), jax.ShapeDtypeStruct((1, 512), F32),
                   jax.ShapeDtypeStruct((8, 512), F32)],
        compiler_params=_cparams("arbitrary"),
    )(dycat, dycat, dycat, o, p, p, p, p, p, p, p, p, p, p, p, pm, pm, conv_w, ga, gc, gmat)


class _StagedReduce:
    LOC, PRE_S, PRE_R, ICI_S, ICI_R, POST_S, POST_R, OUT, N_SEM = 0, 1, 2, 3, 6, 9, 10, 11, 12

    def __init__(self, shard_shape):
        self.half = (shard_shape[0] // 2, shard_shape[1])

    def scratch(self):
        h = self.half
        return [pltpu.VMEM((4,) + h, F32), pltpu.VMEM((4,) + h, F32), pltpu.VMEM((4,) + h, BF16),
                pltpu.VMEM((3,) + h, BF16), pltpu.VMEM(h, F32), pltpu.SemaphoreType.DMA((self.N_SEM,))]

    def run(self, stage, pin, gout, scr):
        own, sib, wire, rbuf, fin, sems = scr
        r2 = self.half[0]
        x, y, c = lax.axis_index("x"), lax.axis_index("y"), lax.axis_index("c")
        mine = 2 * x + y
        sibling = (x, y, 1 - c)
        chips = [(1 - x, y), (x, 1 - y), (1 - x, 1 - y)]
        rows = lambda half: pl.ds(pl.multiple_of(half * r2, r2), r2)
        mesh = pl.DeviceIdType.MESH

        loc = pltpu.make_async_copy(pin.at[:, rows(c), :], own, sems.at[self.LOC])
        pre = pltpu.make_async_remote_copy(
            src_ref=pin.at[:, rows(1 - c), :], dst_ref=sib, send_sem=sems.at[self.PRE_S],
            recv_sem=sems.at[self.PRE_R], device_id=sibling, device_id_type=mesh)

        def ici(j):
            px, py = chips[j]
            return pltpu.make_async_remote_copy(
                src_ref=wire.at[2 * px + py], dst_ref=rbuf.at[j], send_sem=sems.at[self.ICI_S + j],
                recv_sem=sems.at[self.ICI_R + j], device_id=(px, py, c), device_id_type=mesh)

        def post(half):
            return pltpu.make_async_remote_copy(
                src_ref=fin, dst_ref=gout.at[rows(half), :], send_sem=sems.at[self.POST_S],
                recv_sem=sems.at[self.POST_R], device_id=sibling, device_id_type=mesh)

        keep = pltpu.make_async_copy(fin, gout.at[rows(c), :], sems.at[self.OUT])
        if stage == 0:
            loc.start()
            pre.start()
        elif stage == 1:
            loc.wait()
            pre.wait_recv()
            for blk in range(4):
                tot = own[blk] + sib[blk]
                own[blk] = tot
                wire[blk] = tot.astype(BF16)
            for j in range(3):
                ici(j).start()
        elif stage == 2:
            for j in range(3):
                ici(j).wait_recv()
            tot = own[mine]
            for j in range(3):
                tot = tot + rbuf[j].astype(F32)
            fin[...] = tot
            post(c).start()
            keep.start()
        else:
            post(1 - c).wait_recv()
            pre.wait_send()
            for j in range(3):
                ici(j).wait_send()
            post(c).wait_send()
            keep.wait()


def _attn_bwd(q, k, v, do, lse, delta, km, vm, early, nb, s, t):
    n = s // t
    ne = len(early)
    reds = [_StagedReduce(a.shape[1:]) for a in early]
    n_steps = HEADS * nb
    assert n_steps >= 4

    def body(q_ref, k_ref, v_ref, do_ref, lse_ref, dl_ref, km_ref, vm_ref, *rest):
        pin_refs, rest = rest[:ne], rest[ne:]
        dq_ref, dk_ref, dv_ref, dkm_ref, dvm_ref = rest[:5]
        gout_refs, (p_scr, ds_scr, dq_acc), red_scr = rest[5:5 + ne], rest[5 + ne:8 + ne], rest[8 + ne:]
        b = pl.program_id(1)
        step = pl.program_id(0) * nb + b
        for stage, at in enumerate((0, 1, n_steps - 2, n_steps - 1)):
            @pl.when(step == at)
            def _(stage=stage):
                for a, red in enumerate(reds):
                    red.run(stage, pin_refs[a], gout_refs[a], red_scr[6 * a:6 * a + 6])

        @pl.when(b == 0)
        def _():
            dkm_ref[...] = jnp.zeros_like(dkm_ref)
            dvm_ref[...] = jnp.zeros_like(dvm_ref)

        kr = lax.broadcasted_iota(jnp.int32, (t, t), 0)
        qc = lax.broadcasted_iota(jnp.int32, (t, t), 1)
        km_v, vm_v = km_ref[0, 0], vm_ref[0, 0]
        ptm = jnp.exp(_dot_nt(km_v, q_ref[0, 0]) - lse_ref[0, 0])
        dstm = (ptm * (_dot_nt(vm_v, do_ref[0, 0]) - dl_ref[0, 0])).astype(BF16)
        dkm_ref[0] += _dot(dstm, q_ref[0, 0])
        dvm_ref[0] += _dot(ptm.astype(BF16), do_ref[0, 0])
        dq_acc[...] = _dot_tn(dstm, km_v)
        def tiles(j):
            slot = j % 2
            kj = k_ref[0, 0, j * t:(j + 1) * t, :]
            vj = v_ref[0, 0, j * t:(j + 1) * t, :]
            def products(i):
                cs = slice(i * t, (i + 1) * t)
                return _dot_nt(kj, q_ref[0, 0, cs, :]), _dot_nt(vj, do_ref[0, 0, cs, :])

            nxt = products(j)
            for i in range(j, n):
                cs = slice(i * t, (i + 1) * t)
                st, dpt = nxt
                if i + 1 < n:
                    nxt = products(i + 1)
                if i == j:
                    st = jnp.where(kr <= qc, st, NEG_INF)
                pt = jnp.exp(st - lse_ref[0, 0, :, cs])
                dst = (pt * (dpt - dl_ref[0, 0, :, cs])).astype(BF16)
                p_scr[slot, :, cs] = pt.astype(BF16)
                ds_scr[slot, :, cs] = dst
                dq_acc[cs, :] += _dot_tn(dst, kj)

        for j in range(n):
            slot = j % 2
            tiles(j)
            dv_ref[0, 0, j * t:(j + 1) * t, :] = _dot(p_scr[slot, :, j * t:s], do_ref[0, 0, j * t:s, :]).astype(BF16)
            dk_ref[0, 0, j * t:(j + 1) * t, :] = _dot(ds_scr[slot, :, j * t:s], q_ref[0, 0, j * t:s, :]).astype(BF16)
        dq_ref[0, 0] = dq_acc[...].astype(BF16)

    big = lambda w: pl.BlockSpec((1, 1, s, w), lambda h, b: (b, h, 0, 0))
    rowv = pl.BlockSpec((1, 1, 1, s), lambda h, b: (b, h, 0, 0))
    mk = lambda w: pl.BlockSpec((1, 1, N_META, w), lambda h, b: (0, h, 0, 0))
    mo = lambda w: pl.BlockSpec((1, N_META, w), lambda h, b: (h, 0, 0))
    return pl.pallas_call(
        body, name="attn_bwd", grid=(HEADS, nb),
        in_specs=[big(QK_PAD), big(QK_PAD), big(VDIM), big(VDIM), rowv, rowv, mk(QK_PAD), mk(VDIM)]
        + [pl.BlockSpec(memory_space=pl.ANY)] * ne,
        out_specs=[big(QK_PAD), big(QK_PAD), big(VDIM), mo(QK_PAD), mo(VDIM)]
        + [pl.BlockSpec(memory_space=pl.ANY)] * ne,
        out_shape=[jax.ShapeDtypeStruct((nb, HEADS, s, QK_PAD), BF16),
                   jax.ShapeDtypeStruct((nb, HEADS, s, QK_PAD), BF16),
                   jax.ShapeDtypeStruct((nb, HEADS, s, VDIM), BF16),
                   jax.ShapeDtypeStruct((HEADS, N_META, QK_PAD), F32),
                   jax.ShapeDtypeStruct((HEADS, N_META, VDIM), F32)]
        + [jax.ShapeDtypeStruct(a.shape[1:], F32) for a in early],
        scratch_shapes=[pltpu.VMEM((2, t, s), BF16), pltpu.VMEM((2, t, s), BF16), pltpu.VMEM((s, QK_PAD), F32)]
        + [sc for red in reds for sc in red.scratch()],
        compiler_params=_cparams("arbitrary", "arbitrary"),
    )(q, k, v, do, lse, delta, km, vm, *early)


def _up_bwd(dq, dk, dv, dkm, dvm, p, pm, tabs, tabs_m, wq_p, wkv_p, gq, gkv, nb, s, tm):
    nt = s // tm
    n = nb * nt
    c_t, sa_t, sb_t = tabs
    cm_t, sam_t, sbm_t = tabs_m

    def kv_path(dkh, dvh, pa, c, sa, sb, wkv, gkvv):
        dkpe = dkh[0][:, NOPE:]
        for h in range(1, HEADS):
            dkpe = dkpe + dkh[h][:, NOPE:]
        dkr = _rope_bwd(dkpe, c, sa, sb)
        dkv = jnp.concatenate([d[:, :NOPE] for d in dkh] + list(dvh), axis=1).astype(BF16)
        ckv = pa[:, Q_RANK:Q_RANK + KV_RANK]
        kvn, rkv = _rms(ckv, gkvv)
        dckv, dg = _rms_bwd(_dot(dkv, wkv), ckv, rkv, gkvv)
        return dckv, dkr, kvn.astype(BF16), dkv, jnp.sum(dg, axis=0, keepdims=True)

    def body(dq_ref, dk_ref, dv_ref, pa_ref, c_ref, sa_ref, sb_ref,
             dkm_ref, dvm_ref, pam_ref, cm_ref, sam_ref, sbm_ref,
             wq_ref, wkv_ref, gq_ref, gkv_ref,
             dpa_ref, dpam_ref, pq_ref, pkv_ref, dgq_ref, dgkv_ref, dwq_ref, dwkv_ref):
        i = pl.program_id(0)

        @pl.when(i == 0)
        def _():
            dwq_ref[...] = jnp.zeros_like(dwq_ref)
            dwkv_ref[...] = jnp.zeros_like(dwkv_ref)
            dgq_ref[...] = jnp.zeros_like(dgq_ref)
            dgkv_ref[...] = jnp.zeros_like(dgkv_ref)

        @pl.when(i < n)
        def _():
            c, sa, sb = c_ref[...], sa_ref[...], sb_ref[...]
            pa = pa_ref[...]
            parts = []
            for h in range(HEADS):
                dqh = dq_ref[0, h].astype(F32) * ATTN_SCALE
                parts += [dqh[:, :NOPE], _rope_bwd(dqh[:, NOPE:], c, sa, sb)]
            dql = jnp.concatenate(parts, axis=1).astype(BF16)
            cq = pa[:, 0:Q_RANK]
            gqv = gq_ref[...]
            qn, rq = _rms(cq, gqv)
            dwq_ref[...] += _dot_tn(dql, qn.astype(BF16))
            dcq, dg = _rms_bwd(_dot(dql, wq_ref[...]), cq, rq, gqv)
            dgq_ref[...] += jnp.sum(dg, axis=0, keepdims=True)
            dckv, dkr, kvn, dkv, dgk = kv_path([dk_ref[0, h].astype(F32) for h in range(HEADS)],
                                               [dv_ref[0, h].astype(F32) for h in range(HEADS)],
                                               pa, c, sa, sb, wkv_ref[...], gkv_ref[...])
            dwkv_ref[...] += _dot_tn(dkv, kvn)
            dgkv_ref[...] += dgk
            dpa_ref[...] = jnp.concatenate([dcq, dckv, dkr], axis=1).astype(BF16)

        @pl.when(i == n)
        def _():
            dckv, dkr, kvn, dkv, dgk = kv_path([dkm_ref[h] for h in range(HEADS)],
                                               [dvm_ref[h] for h in range(HEADS)],
                                               pam_ref[...], cm_ref[...], sam_ref[...], sbm_ref[...],
                                               wkv_ref[...], gkv_ref[...])
            dwkv_ref[...] += _dot_tn(dkv, kvn)
            dgkv_ref[...] += dgk
            dpam_ref[...] = jnp.concatenate([jnp.zeros((N_META, Q_RANK), F32), dckv, dkr], axis=1)
            for h in range(HEADS):
                pq_ref[h] = dwq_ref[QK_PAD * h:QK_PAD * h + NOPE + ROPE, :]
                pkv_ref[h, 0:NOPE, :] = dwkv_ref[NOPE * h:NOPE * (h + 1), :]
                pkv_ref[h, NOPE:NOPE + VDIM, :] = dwkv_ref[512 + VDIM * h:512 + VDIM * (h + 1), :]

    cl = lambda i: jnp.minimum(i, n - 1)
    hb = lambda w: pl.BlockSpec((1, HEADS, tm, w), lambda i: (cl(i) // nt, 0, cl(i) % nt, 0))
    tab = pl.BlockSpec((tm, 128), lambda i: (cl(i) % nt, 0))
    full = lambda a: pl.BlockSpec(a.shape, lambda i: (0,) * a.ndim)
    const = lambda shape: pl.BlockSpec(shape, lambda i: (0,) * len(shape))
    return pl.pallas_call(
        body, name="up_bwd", grid=(n + 1,),
        in_specs=[hb(QK_PAD), hb(QK_PAD), hb(VDIM), pl.BlockSpec((tm, 512), lambda i: (cl(i), 0)), tab, tab, tab,
                  full(dkm), full(dvm), pl.BlockSpec((N_META, 512), lambda i: (0, 0)),
                  full(cm_t), full(sam_t), full(sbm_t), full(wq_p), full(wkv_p), full(gq), full(gkv)],
        out_specs=[pl.BlockSpec((tm, 512), lambda i: (cl(i), 0)), const((N_META, 512)),
                   const((HEADS, NOPE + ROPE, Q_RANK)), const((HEADS, NOPE + VDIM, KV_RANK)),
                   const((1, Q_RANK)), const((1, KV_RANK))],
        out_shape=[jax.ShapeDtypeStruct((nb * s, 512), BF16), jax.ShapeDtypeStruct((N_META, 512), F32),
                   jax.ShapeDtypeStruct((HEADS, NOPE + ROPE, Q_RANK), F32),
                   jax.ShapeDtypeStruct((HEADS, NOPE + VDIM, KV_RANK), F32),
                   jax.ShapeDtypeStruct((1, Q_RANK), F32), jax.ShapeDtypeStruct((1, KV_RANK), F32)],
        scratch_shapes=[pltpu.VMEM((HEADS * QK_PAD, Q_RANK), F32), pltpu.VMEM((1024, KV_RANK), F32)],
        compiler_params=_cparams("arbitrary"),
    )(dq, dk, dv, p, c_t, sa_t, sb_t, dkm, dvm, pm, cm_t, sam_t, sbm_t, wq_p, wkv_p, gq, gkv)


def _in_bwd(x2d, dh2, dpa, dpb, meta, dpam, dccm, pm, w_in_p, norm_g, nb, s, tm):
    nt = s // tm
    n = nb * nt

    def body(x_ref, dh_ref, dpa_ref, dpb_ref, mt_ref, dpam_ref, dccm_ref, mc_ref, mh_ref, w_ref, g_ref,
             gx_ref, gm_ref, dw_hbm, dg_ref, acc_ref, sems):
        i = pl.program_id(0)

        @pl.when(i == 0)
        def _():
            acc_ref[...] = jnp.zeros_like(acc_ref)
            dg_ref[...] = jnp.zeros_like(dg_ref)

        def rows(x, dp, dres):
            g = g_ref[...]
            u, r1 = _rms(x, g)
            dpb16 = dp.astype(BF16)
            acc_ref[...] += _dot_tn(dpb16, u.astype(BF16))
            dx, dg = _rms_bwd(_dot(dpb16, w_ref[...]), x, r1, g)
            dg_ref[...] += jnp.sum(dg, axis=0, keepdims=True)
            return dx if dres is None else dx + dres

        @pl.when(i < n)
        def _():
            dp = jnp.concatenate([dpa_ref[...], dpb_ref[...]], axis=1)
            gx_ref[...] = rows(x_ref[...], dp, dh_ref[...])

        @pl.when(i == n)
        def _():
            dcc = dccm_ref[0]
            for b in range(1, nb):
                dcc = dcc + dccm_ref[b]
            z8 = jnp.zeros((8, CONV_W), F32)
            dc = jnp.concatenate([z8, dcc * mh_ref[8:16, :]], axis=0)
            dh = jnp.concatenate([z8, dcc * mc_ref[8:16, :]], axis=0)
            z = jnp.zeros((N_META, CONV_W), F32)
            dp = jnp.concatenate([dpam_ref[...], z, z, dc, dh, z], axis=1)
            gm_ref[...] = rows(mt_ref[...], dp, None)
            per = IN_DIM // 4
            cps = [pltpu.make_async_copy(acc_ref.at[0:448], dw_hbm.at[0, 0:448], sems.at[0]),
                   pltpu.make_async_copy(acc_ref.at[512:per + 64], dw_hbm.at[0, 448:per], sems.at[1])]
            for qq in range(1, 4):
                cps.append(pltpu.make_async_copy(acc_ref.at[per * qq + 64:per * (qq + 1) + 64], dw_hbm.at[qq],
                                                 sems.at[qq + 1]))
            for cp in cps:
                cp.start()
            for cp in cps:
                cp.wait()

    cl = lambda i: jnp.minimum(i, n - 1)
    row = lambda w: pl.BlockSpec((tm, w), lambda i: (cl(i), 0))
    full = lambda a: pl.BlockSpec(a.shape, lambda i: (0,) * a.ndim)
    mblk = lambda j: pl.BlockSpec((N_META, 512), lambda i: (0, j))
    return pl.pallas_call(
        body, name="in_bwd", grid=(n + 1,),
        in_specs=[row(D_MODEL), row(D_MODEL), row(512), row(2560), full(meta), full(dpam), full(dccm),
                  mblk(BLK_CC), mblk(BLK_CH), full(w_in_p), full(norm_g)],
        out_specs=[row(D_MODEL), pl.BlockSpec((N_META, D_MODEL), lambda i: (0, 0)),
                   pl.BlockSpec(memory_space=pl.ANY), pl.BlockSpec((1, D_MODEL), lambda i: (0, 0))],
        out_shape=[jax.ShapeDtypeStruct((nb * s, D_MODEL), F32), jax.ShapeDtypeStruct((N_META, D_MODEL), F32),
                   jax.ShapeDtypeStruct((4, IN_DIM // 4, D_MODEL), F32), jax.ShapeDtypeStruct((1, D_MODEL), F32)],
        scratch_shapes=[pltpu.VMEM((IN_PAD, D_MODEL), F32), pltpu.SemaphoreType.DMA((5,))],
        compiler_params=_cparams("arbitrary"),
    )(x2d, dh2, dpa, dpb, meta, dpam, dccm, pm, pm, w_in_p, norm_g)


def _gather_weights(split, pieces, out_rows, whole, zero_fills):
    ns, nw, nz = len(split), len(whole), len(zero_fills)
    flat = [(a, pc) for a in range(ns) for pc in pieces[a]]
    nk = len(flat)

    def body(*refs):
        ins, wins, zins = refs[:ns], refs[ns:ns + nw], refs[ns + nw:ns + nw + nz]
        outs, wouts = refs[ns + nw + nz:2 * ns + nw + nz], refs[2 * ns + nw + nz:2 * (ns + nw) + nz]
        send_sems, recv_sems, fwd_send, fwd_recv, loc_sems, w_send, w_recv, w_loc, z_sems = refs[2 * (ns + nw) + nz:]
        x, y, c = lax.axis_index("x"), lax.axis_index("y"), lax.axis_index("c")
        mine = 2 * x + y
        chips = [(1 - x, y), (x, 1 - y), (1 - x, 1 - y)]
        chip_of = [2 * px + py for px, py in chips]

        def src(k):
            a, (s0, nr, _, _, _, _) = flat[k]
            return ins[a].at[s0:s0 + nr]

        def dst(k, q):
            a, (_, nr, per, first, rest, _) = flat[k]
            row = per * q + first + (rest - first) * jnp.minimum(q, 1)
            return outs[a].at[pl.ds(pl.multiple_of(row, 16), nr)]

        def ici(k, j, q):
            px, py = chips[j]
            return pltpu.make_async_remote_copy(
                src_ref=src(k), dst_ref=dst(k, q), send_sem=send_sems.at[k, j], recv_sem=recv_sems.at[k, j],
                device_id=(px, py, c), device_id_type=pl.DeviceIdType.MESH)

        def fwd(k, j):
            ref = dst(k, chip_of[j])
            return pltpu.make_async_remote_copy(
                src_ref=ref, dst_ref=ref, send_sem=fwd_send.at[k, j], recv_sem=fwd_recv.at[k, j],
                device_id=(x, y, 1 - c), device_id_type=pl.DeviceIdType.MESH)

        def wcopy(b, j, q):
            px, py = chips[j]
            return pltpu.make_async_remote_copy(
                src_ref=wins[b], dst_ref=wouts[b].at[q], send_sem=w_send.at[b, j], recv_sem=w_recv.at[b, j],
                device_id=(px, py, c), device_id_type=pl.DeviceIdType.MESH)

        local = [pltpu.make_async_copy(src(k), dst(k, mine), loc_sems.at[k]) for k in range(nk)]
        local += [pltpu.make_async_copy(wins[b], wouts[b].at[mine], w_loc.at[b]) for b in range(nw)]
        for z, (a, _, row0) in enumerate(zero_fills):
            local.append(pltpu.make_async_copy(zins[z], outs[a].at[row0:row0 + zins[z].shape[0]], z_sems.at[z]))
        wsends = [wcopy(b, j, mine) for b in range(nw) for j in range(3)]
        for cp in local + wsends:
            cp.start()

        for half in (0, 1):
            @pl.when(c == half)
            def _(half=half):
                my_k = [k for k in range(nk) if flat[k][1][5] == half]
                other_k = [k for k in range(nk) if flat[k][1][5] != half]
                sends = [ici(k, j, mine) for k in my_k for j in range(3)]
                for cp in sends:
                    cp.start()
                passed = []
                for k in my_k:
                    for j in range(3):
                        ici(k, j, chip_of[j]).wait_recv()
                        cp = fwd(k, j)
                        cp.start()
                        passed.append(cp)
                for k in other_k:
                    for j in range(3):
                        fwd(k, j).wait_recv()
                for cp in sends + passed:
                    cp.wait_send()

        for b in range(nw):
            for j in range(3):
                wcopy(b, j, chip_of[j]).wait_recv()
        for cp in wsends:
            cp.wait_send()
        for cp in local:
            cp.wait()

    hbm = pl.BlockSpec(memory_space=pl.ANY)
    dma = pltpu.SemaphoreType.DMA
    zeros = [z for _, z, _ in zero_fills]
    return pl.pallas_call(
        body, name="gather_weights",
        in_specs=[hbm] * (ns + nw + nz), out_specs=[hbm] * (ns + nw),
        out_shape=([jax.ShapeDtypeStruct((out_rows[a], split[a].shape[1]), split[a].dtype) for a in range(ns)]
                   + [jax.ShapeDtypeStruct((4,) + w.shape, w.dtype) for w in whole]),
        scratch_shapes=[dma((nk, 3)), dma((nk, 3)), dma((nk, 3)), dma((nk, 3)), dma((nk,)),
                        dma((nw, 3)), dma((nw, 3)), dma((nw,)), dma((nz,))],
        compiler_params=pltpu.CompilerParams(vmem_limit_bytes=VMEM_LIMIT),
    )(*split, *whole, *zeros)


def _reduce_grads(parts, small):
    n = len(parts)
    shapes = [a.shape[1:] for a in parts]
    halves = [(sh[0] // 2, sh[1]) for sh in shapes]

    def body(*refs):
        pin, sm_in = refs[:n], refs[n]
        gout, sm_out = refs[n + 1:2 * n + 1], refs[2 * n + 1]
        scr = refs[2 * n + 2:]
        own, sib, wire, rbuf = scr[:n], scr[n:2 * n], scr[2 * n:3 * n], scr[3 * n:4 * n]
        (sbuf, send_sems, recv_sems, loc_sems, pre_send, pre_recv, post_send, post_recv,
         sm_send, sm_recv) = scr[4 * n:]
        x, y, c = lax.axis_index("x"), lax.axis_index("y"), lax.axis_index("c")
        mine = 2 * x + y
        me = 4 * x + 2 * y + c
        sibling = (x, y, 1 - c)
        chips = [(1 - x, y), (x, 1 - y), (1 - x, 1 - y)]

        def rows(a, half):
            r2 = halves[a][0]
            return pl.ds(pl.multiple_of(half * r2, r2), r2)

        def pre(a):
            return pltpu.make_async_remote_copy(
                src_ref=pin[a].at[:, rows(a, 1 - c), :], dst_ref=sib[a], send_sem=pre_send.at[a],
                recv_sem=pre_recv.at[a], device_id=sibling, device_id_type=pl.DeviceIdType.MESH)

        def ici(a, j):
            px, py = chips[j]
            return pltpu.make_async_remote_copy(
                src_ref=wire[a].at[2 * px + py], dst_ref=rbuf[a].at[j], send_sem=send_sems.at[a, j],
                recv_sem=recv_sems.at[a, j], device_id=(px, py, c), device_id_type=pl.DeviceIdType.MESH)

        def post(a, half):
            ref = gout[a].at[rows(a, half), :]
            return pltpu.make_async_remote_copy(
                src_ref=ref, dst_ref=ref, send_sem=post_send.at[a], recv_sem=post_recv.at[a],
                device_id=sibling, device_id_type=pl.DeviceIdType.MESH)

        def small_copy(kk):
            peer = (x ^ (kk >> 2), y ^ ((kk >> 1) & 1), c ^ (kk & 1))
            return pltpu.make_async_remote_copy(
                src_ref=sm_in, dst_ref=sbuf.at[kk], send_sem=sm_send.at[kk - 1], recv_sem=sm_recv.at[kk - 1],
                device_id=peer, device_id_type=pl.DeviceIdType.MESH)

        local = [pltpu.make_async_copy(pin[a].at[:, rows(a, c), :], own[a], loc_sems.at[a]) for a in range(n)]
        pres = [pre(a) for a in range(n)]
        smalls = [small_copy(kk) for kk in range(1, 8)]
        for cp in local + pres + smalls:
            cp.start()
        sbuf[0] = sm_in[...]
        sends = []
        for a in range(n):
            local[a].wait()
            pres[a].wait_recv()
            for blk in range(4):
                tot = own[a][blk] + sib[a][blk]
                own[a][blk] = tot
                wire[a][blk] = tot.astype(BF16)
            for j in range(3):
                cp = ici(a, j)
                cp.start()
                sends.append(cp)
        for cp in smalls:
            cp.wait_recv()
        total = sbuf[me]
        for d in range(1, 8):
            total = total + sbuf[me ^ d]
        sm_out[...] = total
        posts = []
        for a in range(n):
            for j in range(3):
                ici(a, j).wait_recv()
            fin = own[a][mine]
            for j in range(3):
                fin = fin + rbuf[a][j].astype(F32)
            gout[a][rows(a, c), :] = fin
            cp = post(a, c)
            cp.start()
            posts.append(cp)
        for a in range(n):
            post(a, 1 - c).wait_recv()
        for cp in pres + sends + smalls + posts:
            cp.wait_send()

    hbm = pl.BlockSpec(memory_space=pl.ANY)
    vmem = pl.BlockSpec(memory_space=pltpu.VMEM)
    dma = pltpu.SemaphoreType.DMA
    return pl.pallas_call(
        body, name="reduce_grads",
        in_specs=[hbm] * n + [vmem], out_specs=[vmem] * (n + 1),
        out_shape=[jax.ShapeDtypeStruct(sh, F32) for sh in shapes] + [jax.ShapeDtypeStruct(small.shape, F32)],
        scratch_shapes=([pltpu.VMEM((4,) + hs, F32) for hs in halves] + [pltpu.VMEM((4,) + hs, F32) for hs in halves]
                        + [pltpu.VMEM((4,) + hs, BF16) for hs in halves]
                        + [pltpu.VMEM((3,) + hs, BF16) for hs in halves]
                        + [pltpu.VMEM((8,) + small.shape, F32), dma((n, 3)), dma((n, 3)), dma((n,)),
                           dma((n,)), dma((n,)), dma((n,)), dma((n,)), dma((7,)), dma((7,))]),
        compiler_params=pltpu.CompilerParams(vmem_limit_bytes=VMEM_LIMIT),
    )(*parts, small)


def _adamw(w, g, m, v, name):
    shape = w.shape
    w2, g2, m2, v2 = (a.reshape((-1, shape[-1])) for a in (w, g, m, v))

    def body(w_ref, g_ref, m_ref, v_ref, d_ref, nm_ref, nv_ref):
        gv = g_ref[...]
        nm = ADAM_B1 * m_ref[...] + (1.0 - ADAM_B1) * gv
        nv = ADAM_B2 * v_ref[...] + (1.0 - ADAM_B2) * (gv * gv)
        m_hat = nm / (1.0 - ADAM_B1 ** ADAM_STEP)
        v_hat = nv / (1.0 - ADAM_B2 ** ADAM_STEP)
        d_ref[...] = -ADAM_LR * (m_hat / (jnp.sqrt(v_hat) + ADAM_EPS) + ADAM_WD * w_ref[...])
        nm_ref[...] = nm
        nv_ref[...] = nv

    rows, cols = w2.shape
    nblk = cols // 256 if cols % 256 == 0 and rows >= 64 else 1
    blk = pl.BlockSpec((rows, cols // nblk), lambda j: (0, j))
    out = pl.pallas_call(
        body, name=name, grid=(nblk,), in_specs=[blk] * 4, out_specs=[blk] * 3,
        out_shape=[jax.ShapeDtypeStruct(w2.shape, F32)] * 3,
        compiler_params=_cparams("parallel"),
    )(w2, g2, m2, v2)
    return tuple(a.reshape(shape) for a in out)


def kernel(x, meta_tokens, norm_g, w_in, q_norm_g, w_q_up, kv_norm_g, w_kv_up, conv_w, attn_out_g, conv_out_g, w_out, final_norm_g, loss_target, m_meta_tokens, m_norm_g, m_w_in, m_q_norm_g, m_w_q_up, m_kv_norm_g, m_w_kv_up, m_conv_w, m_attn_out_g, m_conv_out_g, m_w_out, m_final_norm_g, v_meta_tokens, v_norm_g, v_w_in, v_q_norm_g, v_w_q_up, v_kv_norm_g, v_w_kv_up, v_conv_w, v_attn_out_g, v_conv_out_g, v_w_out, v_final_norm_g):
    nb, s, _ = x.shape
    tm = min(ROW_TILE, s)
    ta = min(ATTN_TILE, s)
    assert s % tm == 0 and s % ta == 0 and tm % 16 == 0
    r = nb * s

    tr = lambda a: jnp.transpose(a[0])
    w_in_p, wq_p, wkv_p, g_cw, g_meta = _gather_weights(
        [tr(w_in).astype(BF16), tr(w_q_up).astype(BF16), tr(w_kv_up).astype(BF16)],
        [W_IN_PIECES, W_Q_PIECES, W_KV_PIECES], [IN_PAD, HEADS * QK_PAD, 1024],
        [conv_w[0], meta_tokens],
        [(0, jnp.zeros((64, D_MODEL), BF16), 448)]
        + [(1, jnp.zeros((64, Q_RANK), BF16), QK_PAD * h + NOPE + ROPE) for h in range(HEADS)])
    conv_f = jnp.transpose(g_cw, (1, 0, 2)).reshape(3, CONV_W)
    meta_f = jnp.transpose(g_meta, (1, 0, 2)).reshape(N_META, D_MODEL)

    c_all, sa_all, sb_all = _rope_tables(N_META + s)
    tabs_m = (c_all[:N_META], sa_all[:N_META], sb_all[:N_META])
    tabs = (c_all[N_META:], sa_all[N_META:], sb_all[N_META:])
    gid = np.arange(CONV_W) // CONV_GROUP
    gmat = jnp.asarray(np.where(gid[:, None] == gid[None, :], 1.0 / CONV_GROUP, 0.0), BF16)
    ga, gc = attn_out_g, conv_out_g
    gf = final_norm_g.reshape(1, D_MODEL)

    x2d = x.reshape(r, D_MODEL)
    tgt2d = loss_target.reshape(r, D_MODEL)

    p, q, k, v, pm, km, vm, w_out_f = _fwd_proj(x2d, meta_f, tabs, tabs_m, norm_g, w_in_p, q_norm_g, wq_p,
                                                kv_norm_g, wkv_p, w_out[0].astype(BF16), nb, s, tm)
    o, lse = _attn_fwd(q, k, v, km, vm, nb, s, ta)
    dh2, dycat, dw_out, dgf, loss_acc = _out_fwd_bwd(x2d, tgt2d, o, p, pm, conv_f, ga, gc, gmat, w_out_f, gf,
                                                     nb, s, tm)
    dpb, do, delta, dccm, dga, dgc, dcw = _gate_bwd(dycat, o, p, pm, conv_f, ga, gc, gmat, nb, s, tm)
    p_out = dw_out.reshape(4, D_MODEL // 4, D_MODEL)
    dq, dk, dv, dkm, dvm, g_w_out = _attn_bwd(q, k, v, do, lse, delta, km, vm, [p_out], nb, s, ta)
    dpa, dpam, p_q, p_kv, dgq, dgkv = _up_bwd(dq, dk, dv, dkm, dvm, p, pm, tabs, tabs_m, wq_p, wkv_p,
                                              q_norm_g, kv_norm_g, nb, s, tm)
    gx, gmeta, p_in, dng = _in_bwd(x2d, dh2, dpa, dpb, meta_f, dpam, dccm, pm, w_in_p, norm_g, nb, s, tm)

    flat =jnp.concatenate([dng.reshape(-1), dgq.reshape(-1), dgkv.reshape(-1), dga.reshape(-1), dgc.reshape(-1),
                            dgf.reshape(-1), dcw[:3].reshape(-1), gmeta.reshape(-1), loss_acc[0, 0:1]])
    n_small = flat.shape[0]
    rows_small = -(-n_small // 1024) * 8
    small = jnp.pad(flat, (0, rows_small * 128 - n_small)).reshape(rows_small, 128)
    g_w_in_t, g_w_q_t, g_w_kv_t, small_sum = _reduce_grads([p_in, p_q, p_kv], small)
    ssum = small_sum.reshape(-1)

    def take(off, n):
        return ssum[off:off + n], off + n

    off = 0
    g_norm, off = take(off, D_MODEL)
    g_qn, off = take(off, Q_RANK)
    g_kvn, off = take(off, KV_RANK)
    g_ga, off = take(off, CONV_W)
    g_gc, off = take(off, CONV_W)
    g_gf, off = take(off, D_MODEL)
    g_cw_all, off = take(off, 3 * CONV_W)
    g_meta_all, off = take(off, N_META * D_MODEL)
    loss = ssum[off]
    chip = 2 * lax.axis_index("x") + lax.axis_index("y")
    g_conv = lax.dynamic_slice(g_cw_all.reshape(3, CONV_W), (0, chip * 128), (3, 128))
    g_mt = lax.dynamic_slice(g_meta_all.reshape(N_META, D_MODEL), (0, chip * 256), (N_META, 256))

    grads = {
        "meta_tokens": g_mt, "norm_g": g_norm.reshape(1, -1), "w_in": g_w_in_t, "q_norm_g": g_qn.reshape(1, -1),
        "w_q_up": g_w_q_t, "kv_norm_g": g_kvn.reshape(1, -1), "w_kv_up": jnp.transpose(g_w_kv_t)[None],
        "conv_w": g_conv[None], "attn_out_g": g_ga.reshape(1, -1), "conv_out_g": g_gc.reshape(1, -1),
        "w_out": g_w_out[None], "final_norm_g": g_gf,
    }
    transposed = ("w_in", "w_q_up")
    weights = {
        "meta_tokens": (meta_tokens, m_meta_tokens, v_meta_tokens), "norm_g": (norm_g, m_norm_g, v_norm_g),
        "w_in": (w_in, m_w_in, v_w_in), "q_norm_g": (q_norm_g, m_q_norm_g, v_q_norm_g),
        "w_q_up": (w_q_up, m_w_q_up, v_w_q_up), "kv_norm_g": (kv_norm_g, m_kv_norm_g, v_kv_norm_g),
        "w_kv_up": (w_kv_up, m_w_kv_up, v_w_kv_up), "conv_w": (conv_w, m_conv_w, v_conv_w),
        "attn_out_g": (attn_out_g, m_attn_out_g, v_attn_out_g), "conv_out_g": (conv_out_g, m_conv_out_g, v_conv_out_g),
        "w_out": (w_out, m_w_out, v_w_out), "final_norm_g": (final_norm_g, m_final_norm_g, v_final_norm_g),
    }
    names = list(weights)
    deltas, new_m, new_v = [], [], []
    for nme in names:
        w_, m_, v_ = weights[nme]
        if nme in transposed:
            res = _adamw(tr(w_), grads[nme], tr(m_), tr(v_), "adamw_" + nme)
            g_, d_, nm_, nv_ = (jnp.transpose(a)[None] for a in (grads[nme],) + res)
        else:
            g_ = grads[nme].reshape(w_.shape)
            d_, nm_, nv_ = _adamw(w_, g_, m_, v_, "adamw_" + nme)
        grads[nme] = g_
        deltas.append(d_)
        new_m.append(nm_)
        new_v.append(nv_)

    grad_x = gx.reshape(nb, s, D_MODEL)
    return (loss, grad_x, *[grads[nme] for nme in names], *deltas, *new_m, *new_v)
```

```python
import functools

import jax
import jax.numpy as jnp
import numpy as np
from jax import lax
from jax.experimental import pallas as pl
from jax.experimental.pallas import tpu as pltpu

F32 = jnp.float32
BF16 = jnp.bfloat16

D_MODEL = 1024
N_META = 16
HEADS = 4
NOPE = 128
ROPE = 64
VDIM = 128
QK_PAD = 256
Q_RANK = 256
KV_RANK = 128
CONV_W = 512
CONV_GROUP = 64
ROPE_THETA = 10000.0
EPS = 1e-6
ATTN_SCALE = (NOPE + ROPE) ** -0.5
IN_DIM = 3008
IN_PAD = 3072
BLK_ZA, BLK_CB, BLK_CC, BLK_CH, BLK_ZC = 1, 2, 3, 4, 5
NEG_INF = -1e30

ADAM_LR = 0.001
ADAM_B1 = 0.9
ADAM_B2 = 0.999
ADAM_EPS = 1e-08
ADAM_WD = 0.01
ADAM_STEP = 10

ROW_TILE = 512
ATTN_TILE = 256
VMEM_LIMIT = 56 * 1024 * 1024

NT = (((1,), (1,)), ((), ()))
TN = (((0,), (0,)), ((), ()))


def _cparams(*sem):
    return pltpu.CompilerParams(dimension_semantics=sem, vmem_limit_bytes=VMEM_LIMIT)


def _dot(a, b):
    return jnp.dot(a, b, preferred_element_type=F32)


def _dot_nt(a, b):
    return lax.dot_general(a, b, NT, preferred_element_type=F32)


def _dot_tn(a, b):
    return lax.dot_general(a, b, TN, preferred_element_type=F32)


def _rms(x, g):
    r = lax.rsqrt(jnp.mean(x * x, axis=-1, keepdims=True) + EPS)
    return x * r * g, r


def _rms_bwd(dy, x, r, g):
    xh = x * r
    dyg = dy * g
    dx = r * (dyg - xh * jnp.mean(dyg * xh, axis=-1, keepdims=True))
    return dx, dy * xh


def _sigmoid(z):
    return 1.0 / (1.0 + jnp.exp(-z))


def _rope(b, c, sa, sb):
    return b * c + pltpu.roll(b, 96, 1) * sa + pltpu.roll(b, 32, 1) * sb


def _rope_bwd(d, c, sa, sb):
    return d * c + pltpu.roll(d * sa, 32, 1) + pltpu.roll(d * sb, 96, 1)


def _group_mean(x, gmat):
    hi = x.astype(BF16)
    lo = (x - hi.astype(F32)).astype(BF16)
    return _dot(hi, gmat) + _dot(lo, gmat)


def _row_of(col, rows):
    return jnp.transpose(jnp.broadcast_to(col, (rows, 128)))[0:1, :]


def _rope_tables(n_pos):
    half = ROPE // 2
    inv_freq = (np.float32(1.0) / (np.float32(ROPE_THETA) ** (np.arange(half, dtype=np.float32) / np.float32(half))))
    ang = np.arange(n_pos, dtype=np.float32)[:, None] * inv_freq.astype(np.float32)[None, :]
    cos, sin = np.cos(ang).astype(np.float32), np.sin(ang).astype(np.float32)
    z = np.zeros((n_pos, half), np.float32)
    c = np.concatenate([cos, cos, z, z], axis=1)
    sa = np.concatenate([-sin, z, z, z], axis=1)
    sb = np.concatenate([z, sin, z, z], axis=1)
    return jnp.asarray(c), jnp.asarray(sa), jnp.asarray(sb)


W_IN_PIECES = ((0, 384, 752, 0, 64, 0), (384, 64, 752, 384, 448, 1), (448, 304, 752, 512, 512, 1))
W_Q_PIECES = ((0, 96, 256, 0, 0, 0), (96, 96, 256, 96, 96, 1))
W_KV_PIECES = ((0, 128, 128, 0, 0, 0), (128, 128, 128, 512, 512, 1))
W_OUT_PIECES = ((0, 128, 256, 0, 0, 0), (128, 128, 256, 128, 128, 1))


class _StagedGather:
    def __init__(self, pieces):
        self.pieces = pieces

    def scratch(self):
        nk, dma = len(self.pieces), pltpu.SemaphoreType.DMA
        return [dma((nk, 3)), dma((nk, 3)), dma((nk, 3)), dma((nk, 3)), dma((nk,))]

    def run(self, stage, src_ref, out_ref, scr):
        send_sems, recv_sems, fwd_send, fwd_recv, loc_sems = scr
        pieces = self.pieces
        nk = len(pieces)
        x, y, c = lax.axis_index("x"), lax.axis_index("y"), lax.axis_index("c")
        mine = 2 * x + y
        chips = [(1 - x, y), (x, 1 - y), (1 - x, 1 - y)]
        chip_of = [2 * px + py for px, py in chips]
        mesh = pl.DeviceIdType.MESH

        def src(k):
            s0, nr = pieces[k][0], pieces[k][1]
            return src_ref.at[s0:s0 + nr]

        def dst(k, q):
            _, nr, per, first, rest, _ = pieces[k]
            row = per * q + first + (rest - first) * jnp.minimum(q, 1)
            return out_ref.at[pl.ds(pl.multiple_of(row, 16), nr)]

        def ici(k, j, q):
            px, py = chips[j]
            return pltpu.make_async_remote_copy(
                src_ref=src(k), dst_ref=dst(k, q), send_sem=send_sems.at[k, j], recv_sem=recv_sems.at[k, j],
                device_id=(px, py, c), device_id_type=mesh)

        def fwd(k, j):
            ref = dst(k, chip_of[j])
            return pltpu.make_async_remote_copy(
                src_ref=ref, dst_ref=ref, send_sem=fwd_send.at[k, j], recv_sem=fwd_recv.at[k, j],
                device_id=(x, y, 1 - c), device_id_type=mesh)

        local = [pltpu.make_async_copy(src(k), dst(k, mine), loc_sems.at[k]) for k in range(nk)]
        if stage == 0:
            for cp in local:
                cp.start()
        if stage == 2:
            for cp in local:
                cp.wait()
        for half in (0, 1):
            @pl.when(c == half)
            def _(half=half):
                my_k = [k for k in range(nk) if pieces[k][5] == half]
                other_k = [k for k in range(nk) if pieces[k][5] != half]
                for k in my_k:
                    for j in range(3):
                        if stage == 0:
                            ici(k, j, mine).start()
                        elif stage == 1:
                            ici(k, j, chip_of[j]).wait_recv()
                            fwd(k, j).start()
                        else:
                            ici(k, j, mine).wait_send()
                            fwd(k, j).wait_send()
                if stage == 2:
                    for k in other_k:
                        for j in range(3):
                            fwd(k, j).wait_recv()


def _fwd_proj(x2d, meta, tabs, tabs_m, norm_g, w_in_p, q_norm_g, wq_p, kv_norm_g, wkv_p, w_out_shard, nb, s, tm):
    nt = s // tm
    n = nb * nt
    n_steps = n + 1
    c_t, sa_t, sb_t = tabs
    cm_t, sam_t, sbm_t = tabs_m
    gat = _StagedGather(W_OUT_PIECES)
    assert n_steps >= 3

    def body(x_ref, c_ref, sa_ref, sb_ref, mt_ref, cm_ref, sam_ref, sbm_ref,
             g_ref, w_ref, gq_ref, wq_ref, gkv_ref, wkv_ref, wos_ref,
             p_ref, q_ref, k_ref, v_ref, pm_ref, km_ref, vm_ref, wo_ref, *gat_scr):
        i = pl.program_id(0)
        for stage, at in enumerate((0, n_steps - 2, n_steps - 1)):
            @pl.when(i == at)
            def _(stage=stage):
                gat.run(stage, wos_ref, wo_ref, gat_scr)

        def project(xv, c, sa, sb, p_out, q_out, k_out, v_out):
            u, _ = _rms(xv, g_ref[...])
            p = _dot_nt(u.astype(BF16), w_ref[...])
            p_out[...] = p
            qn, _ = _rms(p[:, 0:Q_RANK], gq_ref[...])
            q = _dot_nt(qn.astype(BF16), wq_ref[...])
            kvn, _ = _rms(p[:, Q_RANK:Q_RANK + KV_RANK], gkv_ref[...])
            kv = _dot_nt(kvn.astype(BF16), wkv_ref[...])
            kpe = _rope(p[:, 384:512], c, sa, sb)
            for h in range(HEADS):
                if q_out is not None:
                    pe = _rope(q[:, QK_PAD * h + NOPE:QK_PAD * (h + 1)], c, sa, sb)
                    qh = jnp.concatenate([q[:, QK_PAD * h:QK_PAD * h + NOPE], pe], axis=1)
                    q_out[0, h] = (qh * ATTN_SCALE).astype(BF16)
                k_out[0, h] = jnp.concatenate([kv[:, NOPE * h:NOPE * (h + 1)], kpe], axis=1).astype(BF16)
                v_out[0, h] = kv[:, 512 + VDIM * h:512 + VDIM * (h + 1)].astype(BF16)

        @pl.when(i < n)
        def _():
            project(x_ref[...], c_ref[...], sa_ref[...], sb_ref[...], p_ref, q_ref, k_ref, v_ref)

        @pl.when(i == n)
        def _():
            project(mt_ref[...], cm_ref[...], sam_ref[...], sbm_ref[...], pm_ref, None, km_ref, vm_ref)

    cl = lambda i: jnp.minimum(i, n - 1)
    full = lambda a: pl.BlockSpec(a.shape, lambda i: (0,) * a.ndim)
    const = lambda shape: pl.BlockSpec(shape, lambda i: (0,) * len(shape))
    tab = pl.BlockSpec((tm, 128), lambda i: (cl(i) % nt, 0))
    hb = lambda w: pl.BlockSpec((1, HEADS, tm, w), lambda i: (cl(i) // nt, 0, cl(i) % nt, 0))
    hbm = pl.BlockSpec(memory_space=pl.ANY)
    return pl.pallas_call(
        body, name="fwd_proj", grid=(n_steps,),
        in_specs=[pl.BlockSpec((tm, D_MODEL), lambda i: (cl(i), 0)), tab, tab, tab,
                  full(meta), full(cm_t), full(sam_t), full(sbm_t),
                  full(norm_g), full(w_in_p), full(q_norm_g), full(wq_p), full(kv_norm_g), full(wkv_p), hbm],
        out_specs=[pl.BlockSpec((tm, IN_PAD), lambda i: (cl(i), 0)), hb(QK_PAD), hb(QK_PAD), hb(VDIM),
                   const((N_META, IN_PAD)), const((1, HEADS, N_META, QK_PAD)), const((1, HEADS, N_META, VDIM)), hbm],
        out_shape=[jax.ShapeDtypeStruct((nb * s, IN_PAD), F32),
                   jax.ShapeDtypeStruct((nb, HEADS, s, QK_PAD), BF16),
                   jax.ShapeDtypeStruct((nb, HEADS, s, QK_PAD), BF16),
                   jax.ShapeDtypeStruct((nb, HEADS, s, VDIM), BF16),
                   jax.ShapeDtypeStruct((N_META, IN_PAD), F32),
                   jax.ShapeDtypeStruct((1, HEADS, N_META, QK_PAD), BF16),
                   jax.ShapeDtypeStruct((1, HEADS, N_META, VDIM), BF16),
                   jax.ShapeDtypeStruct((D_MODEL, D_MODEL), BF16)],
        scratch_shapes=gat.scratch(),
        compiler_params=_cparams("arbitrary"),
    )(x2d, c_t, sa_t, sb_t, meta, cm_t, sam_t, sbm_t, norm_g, w_in_p, q_norm_g, wq_p, kv_norm_g, wkv_p, w_out_shard)


def _attn_fwd(q, k, v, km, vm, nb, s, tq):
    nq = s // tq

    def body(q_ref, k_ref, v_ref, km_ref, vm_ref, o_ref, lse_ref, s_scr, p_scr):
        row = lax.broadcasted_iota(jnp.int32, (tq, tq), 0)
        col = lax.broadcasted_iota(jnp.int32, (tq, tq), 1)
        def scores(i):
            slot = i % 2
            qi = q_ref[0, 0, i * tq:(i + 1) * tq, :]
            sm = _dot_nt(qi, km_ref[0, 0])
            m128 = None
            for j in range(i + 1):
                sc = _dot_nt(qi, k_ref[0, 0, j * tq:(j + 1) * tq, :])
                if j == i:
                    sc = jnp.where(col <= row, sc, NEG_INF)
                s_scr[slot, :, j * tq:(j + 1) * tq] = sc
                mx = sc[:, 0:128]
                for c0 in range(128, tq, 128):
                    mx = jnp.maximum(mx, sc[:, c0:c0 + 128])
                m128 = mx if m128 is None else jnp.maximum(m128, mx)
            return sm, jnp.maximum(jnp.max(m128, axis=1, keepdims=True), jnp.max(sm, axis=1, keepdims=True))

        def weighted_sum(i, pm, l):
            n = (i + 1) * tq
            acc = _dot(p_scr[i % 2, :, 0:n], v_ref[0, 0, 0:n, :]) + _dot(pm.astype(BF16), vm_ref[0, 0])
            o_ref[0, 0, i * tq:(i + 1) * tq, :] = acc / l

        nxt, pending = scores(0), None
        for i in range(nq):
            slot = i % 2
            sm, m = nxt
            if i + 1 < nq:
                nxt = scores(i + 1)
            pm = jnp.exp(sm - m)
            l128 = None
            for j in range(i + 1):
                p = jnp.exp(s_scr[slot, :, j * tq:(j + 1) * tq] - m)
                p_scr[slot, :, j * tq:(j + 1) * tq] = p.astype(BF16)
                ps = p[:, 0:128]
                for c0 in range(128, tq, 128):
                    ps = ps + p[:, c0:c0 + 128]
                l128 = ps if l128 is None else l128 + ps
            l = jnp.sum(l128, axis=1, keepdims=True) + jnp.sum(pm, axis=1, keepdims=True)
            lse_ref[0, 0, :, i * tq:(i + 1) * tq] = _row_of(m + jnp.log(l), tq)
            if pending is not None:
                weighted_sum(*pending)
            pending = (i, pm, l)
        weighted_sum(*pending)

    hblk = lambda w: pl.BlockSpec((1, 1, s, w), lambda b, h: (b, h, 0, 0))
    mblk = lambda w: pl.BlockSpec((1, 1, N_META, w), lambda b, h: (0, h, 0, 0))
    return pl.pallas_call(
        body, name="attn_fwd", grid=(nb, HEADS),
        in_specs=[hblk(QK_PAD), hblk(QK_PAD), hblk(VDIM), mblk(QK_PAD), mblk(VDIM)],
        out_specs=[hblk(VDIM), pl.BlockSpec((1, 1, 1, s), lambda b, h: (b, h, 0, 0))],
        out_shape=[jax.ShapeDtypeStruct((nb, HEADS, s, VDIM), F32),
                   jax.ShapeDtypeStruct((nb, HEADS, 1, s), F32)],
        scratch_shapes=[pltpu.VMEM((2, tq, s), F32), pltpu.VMEM((2, tq, s), BF16)],
        compiler_params=_cparams("parallel", "parallel"),
    )(q, k, v, km, vm)


def _shift_rows(a, prev, n_rows):
    rid = lax.broadcasted_iota(jnp.int32, a.shape, 0)
    a1 = jnp.where(rid == 0, prev[7:8, :], pltpu.roll(a, 1, 0))
    a2 = jnp.where(rid == 0, prev[6:7, :], jnp.where(rid == 1, prev[7:8, :], pltpu.roll(a, 2, 0)))
    return a1, a2


def _attn_gate(o, za, ga_h):
    on, r = _rms(o, ga_h)
    return on * (za * _sigmoid(za)), on, r


def _out_fwd_bwd(x2d, tgt2d, o, p, pm, conv_w, ga, gc, gmat, w_out, gf, nb, s, tm):
    nt = s // tm
    r = nb * s
    prev_idx = lambda i: jnp.maximum(i * (tm // 8) - 1, 0)

    def body(x_ref, t_ref, o_ref, za_ref, cb_ref, cc_ref, ch_ref, zc_ref, ccp_ref, chp_ref, mc_ref, mh_ref,
             cw_ref, ga_ref, gc_ref, gm_ref, w_ref, gf_ref,
             dh_ref, dy_ref, dw_ref, dgf_ref, loss_ref):
        i = pl.program_id(0)

        @pl.when(i == 0)
        def _():
            dw_ref[...] = jnp.zeros_like(dw_ref)
            dgf_ref[...] = jnp.zeros_like(dgf_ref)
            loss_ref[...] = jnp.zeros_like(loss_ref)

        ya = []
        for h in range(HEADS):
            y, _, _ = _attn_gate(o_ref[0, h], za_ref[:, VDIM * h:VDIM * (h + 1)],
                                 ga_ref[:, VDIM * h:VDIM * (h + 1)])
            ya.append(y)
        cc = cc_ref[...] * ch_ref[...]
        prev = jnp.where(i % nt == 0, mc_ref[8:16, :] * mh_ref[8:16, :], ccp_ref[...] * chp_ref[...])
        cc1, cc2 = _shift_rows(cc, prev, tm)
        yc = cb_ref[...] * (cw_ref[0:1, :] * cc2 + cw_ref[1:2, :] * cc1 + cw_ref[2:3, :] * cc)
        rg = lax.rsqrt(_group_mean(yc * yc, gm_ref[...]) + EPS)
        zc = zc_ref[...]
        yconv = yc * rg * gc_ref[...] * (zc * _sigmoid(zc))
        ycat = jnp.concatenate(ya + [yconv], axis=1).astype(BF16)
        h2 = x_ref[...] + _dot(ycat, w_ref[...])
        gfv = gf_ref[...]
        y, r2 = _rms(h2, gfv)
        e = y - t_ref[...]
        loss_ref[...] += 0.5 * jnp.sum(e * e) / D_MODEL
        dyv = e * (1.0 / D_MODEL)
        dh2, dgf = _rms_bwd(dyv, h2, r2, gfv)
        dgf_ref[...] += jnp.sum(dgf, axis=0, keepdims=True)
        dh_ref[...] = dh2
        dhb = dh2.astype(BF16)
        dy_ref[...] = _dot_nt(dhb, w_ref[...])
        dw_ref[...] += _dot_tn(ycat, dhb)

    row = lambda w, j: pl.BlockSpec((tm, w), lambda i: (i, j))
    pblk = lambda j: pl.BlockSpec((tm, 512), lambda i: (i, j))
    pprev = lambda j: pl.BlockSpec((8, 512), lambda i: (prev_idx(i), j))
    mblk = lambda j: pl.BlockSpec((N_META, 512), lambda i: (0, j))
    full = lambda a: pl.BlockSpec(a.shape, lambda i: (0,) * a.ndim)
    return pl.pallas_call(
        body, name="out_fwd_bwd", grid=(nb * nt,),
        in_specs=[row(D_MODEL, 0), row(D_MODEL, 0),
                  pl.BlockSpec((1, HEADS, tm, VDIM), lambda i: (i // nt, 0, i % nt, 0)),
                  pblk(BLK_ZA), pblk(BLK_CB), pblk(BLK_CC), pblk(BLK_CH), pblk(BLK_ZC),
                  pprev(BLK_CC), pprev(BLK_CH), mblk(BLK_CC), mblk(BLK_CH),
                  full(conv_w), full(ga), full(gc), full(gmat), full(w_out), full(gf)],
        out_specs=[row(D_MODEL, 0), row(D_MODEL, 0),
                   pl.BlockSpec((D_MODEL, D_MODEL), lambda i: (0, 0)),
                   pl.BlockSpec((1, D_MODEL), lambda i: (0, 0)),
                   pl.BlockSpec((1, 128), lambda i: (0, 0))],
        out_shape=[jax.ShapeDtypeStruct((r, D_MODEL), F32), jax.ShapeDtypeStruct((r, D_MODEL), F32),
                   jax.ShapeDtypeStruct((D_MODEL, D_MODEL), F32), jax.ShapeDtypeStruct((1, D_MODEL), F32),
                   jax.ShapeDtypeStruct((1, 128), F32)],
        compiler_params=_cparams("arbitrary"),
    )(x2d, tgt2d, o, p, p, p, p, p, p, p, pm, pm, conv_w, ga, gc, gmat, w_out, gf)


def _gate_bwd(dycat, o, p, pm, conv_w, ga, gc, gmat, nb, s, tm):
    nt = s // tm
    r = nb * s
    ext = tm + 8
    prev_idx = lambda i: jnp.maximum(i * (tm // 8) - 1, 0)
    next_idx = lambda i: jnp.minimum((i + 1) * (tm // 8), r // 8 - 1)

    def body(dya_ref, dyc_ref, dycn_ref, o_ref, za_ref, cb_ref, cbn_ref, cc_ref, ccp_ref, ccn_ref,
             ch_ref, chp_ref, chn_ref, zc_ref, zcn_ref, mc_ref, mh_ref, cw_ref, ga_ref, gc_ref, gm_ref,
             dpb_ref, do_ref, dl_ref, dccm_ref, dga_ref, dgc_ref, dcw_ref):
        i = pl.program_id(0)

        @pl.when(i == 0)
        def _():
            dga_ref[...] = jnp.zeros_like(dga_ref)
            dgc_ref[...] = jnp.zeros_like(dgc_ref)
            dcw_ref[...] = jnp.zeros_like(dcw_ref)

        dga = []
        for h in range(HEADS):
            hs = slice(VDIM * h, VDIM * (h + 1))
            oh, za, gah, dya = o_ref[0, h], za_ref[:, hs], ga_ref[:, hs], dya_ref[:, hs]
            sg = _sigmoid(za)
            on, ro = _rms(oh, gah)
            don = dya * (za * sg)
            dpb_ref[:, hs] = (dya * on * (sg * (1.0 + za * (1.0 - sg)))).astype(BF16)
            do, dg = _rms_bwd(don, oh, ro, gah)
            dga.append(jnp.sum(dg, axis=0, keepdims=True))
            dob = do.astype(BF16)
            do_ref[0, h] = dob
            dl_ref[0, h] = _row_of(jnp.sum(dob.astype(F32) * oh, axis=1, keepdims=True), tm)
        dga_ref[...] += jnp.concatenate(dga, axis=1)

        cat = lambda a, b: jnp.concatenate([a[...], b[...]], axis=0)
        cch = cat(cc_ref, ccn_ref)
        chh = cat(ch_ref, chn_ref)
        cb = cat(cb_ref, cbn_ref)
        zc = cat(zc_ref, zcn_ref)
        dy = cat(dyc_ref, dycn_ref)
        first = i % nt == 0
        last = i % nt == nt - 1
        cc = cch * chh
        prev = jnp.where(first, mc_ref[8:16, :] * mh_ref[8:16, :], ccp_ref[...] * chp_ref[...])
        cc1, cc2 = _shift_rows(cc, prev, ext)
        w0, w1, w2 = cw_ref[0:1, :], cw_ref[1:2, :], cw_ref[2:3, :]
        dw = w0 * cc2 + w1 * cc1 + w2 * cc
        yc = cb * dw
        rg = lax.rsqrt(_group_mean(yc * yc, gm_ref[...]) + EPS)
        ych = yc * rg
        gcv = gc_ref[...]
        sg = _sigmoid(zc)
        dycn = dy * (zc * sg)
        dzc = dy * (ych * gcv) * (sg * (1.0 + zc * (1.0 - sg)))
        dgc_ref[...] += jnp.sum((dycn * ych)[:tm], axis=0, keepdims=True)
        dycg = dycn * gcv
        dyc = rg * (dycg - ych * _group_mean(dycg * ych, gm_ref[...]))
        rid = lax.broadcasted_iota(jnp.int32, (ext, CONV_W), 0)
        ddw = jnp.where(jnp.logical_and(last, rid >= tm), 0.0, dyc * cb)
        dcb = dyc * dw
        dcc = w2 * ddw + w1 * pltpu.roll(ddw, ext - 1, 0) + w0 * pltpu.roll(ddw, ext - 2, 0)
        dpb_ref[:, 512:1024] = dcb[:tm].astype(BF16)
        dpb_ref[:, 1024:1536] = (dcc * chh)[:tm].astype(BF16)
        dpb_ref[:, 1536:2048] = (dcc * cch)[:tm].astype(BF16)
        dpb_ref[:, 2048:2560] = dzc[:tm].astype(BF16)
        rs = lambda a: jnp.sum(a[:tm], axis=0, keepdims=True)
        dcw_ref[0:1, :] += rs(ddw * cc2)
        dcw_ref[1:2, :] += rs(ddw * cc1)
        dcw_ref[2:3, :] += rs(ddw * cc)

        @pl.when(first)
        def _():
            d0, d1 = ddw[0:1, :], ddw[1:2, :]
            r8 = lax.broadcasted_iota(jnp.int32, (8, CONV_W), 0)
            dccm_ref[0] = jnp.where(r8 == 7, w1 * d0 + w0 * d1, jnp.where(r8 == 6, w0 * d0, 0.0))

    row = lambda j: pl.BlockSpec((tm, 512), lambda i: (i, j))
    prv = lambda j: pl.BlockSpec((8, 512), lambda i: (prev_idx(i), j))
    nxt = lambda j: pl.BlockSpec((8, 512), lambda i: (next_idx(i), j))
    mblk = lambda j: pl.BlockSpec((N_META, 512), lambda i: (0, j))
    full = lambda a: pl.BlockSpec(a.shape, lambda i: (0,) * a.ndim)
    hb = lambda w: pl.BlockSpec((1, HEADS, tm, w), lambda i: (i // nt, 0, i % nt, 0))
    acc = lambda rr: pl.BlockSpec((rr, 512), lambda i: (0, 0))
    return pl.pallas_call(
        body, name="gate_bwd", grid=(nb * nt,),
        in_specs=[row(0), row(1), nxt(1), hb(VDIM),
                  row(BLK_ZA), row(BLK_CB), nxt(BLK_CB), row(BLK_CC), prv(BLK_CC), nxt(BLK_CC),
                  row(BLK_CH), prv(BLK_CH), nxt(BLK_CH), row(BLK_ZC), nxt(BLK_ZC),
                  mblk(BLK_CC), mblk(BLK_CH), full(conv_w), full(ga), full(gc), full(gmat)],
        out_specs=[pl.BlockSpec((tm, 2560), lambda i: (i, 0)), hb(VDIM),
                   pl.BlockSpec((1, HEADS, 1, tm), lambda i: (i // nt, 0, 0, i % nt)),
                   pl.BlockSpec((1, 8, 512), lambda i: (i // nt, 0, 0)),
                   acc(1), acc(1), acc(8)],
        out_shape=[jax.ShapeDtypeStruct((r, 2560), BF16), jax.ShapeDtypeStruct((nb, HEADS, s, VDIM), BF16),
                   jax.ShapeDtypeStruct((nb, HEADS, 1, s), F32), jax.ShapeDtypeStruct((nb, 8, 512), F32),
                   jax.ShapeDtypeStruct((1, 512), F32), jax.ShapeDtypeStruct((1, 512), F32),
                   jax.ShapeDtypeStruct((8, 512), F32)],
        compiler_params=_cparams("arbitrary"),
    )(dycat, dycat, dycat, o, p, p, p, p, p, p, p, p, p, p, p, pm, pm, conv_w, ga, gc, gmat)


class _StagedReduce:
    LOC, PRE_S, PRE_R, ICI_S, ICI_R, POST_S, POST_R, OUT, N_SEM = 0, 1, 2, 3, 6, 9, 10, 11, 12

    def __init__(self, shard_shape):
        self.half = (shard_shape[0] // 2, shard_shape[1])

    def scratch(self):
        h = self.half
        return [pltpu.VMEM((4,) + h, F32), pltpu.VMEM((4,) + h, F32), pltpu.VMEM((4,) + h, BF16),
                pltpu.VMEM((3,) + h, BF16), pltpu.VMEM(h, F32), pltpu.SemaphoreType.DMA((self.N_SEM,))]

    def run(self, stage, pin, gout, scr):
        own, sib, wire, rbuf, fin, sems = scr
        r2 = self.half[0]
        x, y, c = lax.axis_index("x"), lax.axis_index("y"), lax.axis_index("c")
        mine = 2 * x + y
        sibling = (x, y, 1 - c)
        chips = [(1 - x, y), (x, 1 - y), (1 - x, 1 - y)]
        rows = lambda half: pl.ds(pl.multiple_of(half * r2, r2), r2)
        mesh = pl.DeviceIdType.MESH

        loc = pltpu.make_async_copy(pin.at[:, rows(c), :], own, sems.at[self.LOC])
        pre = pltpu.make_async_remote_copy(
            src_ref=pin.at[:, rows(1 - c), :], dst_ref=sib, send_sem=sems.at[self.PRE_S],
            recv_sem=sems.at[self.PRE_R], device_id=sibling, device_id_type=mesh)

        def ici(j):
            px, py = chips[j]
            return pltpu.make_async_remote_copy(
                src_ref=wire.at[2 * px + py], dst_ref=rbuf.at[j], send_sem=sems.at[self.ICI_S + j],
                recv_sem=sems.at[self.ICI_R + j], device_id=(px, py, c), device_id_type=mesh)

        def post(half):
            return pltpu.make_async_remote_copy(
                src_ref=fin, dst_ref=gout.at[rows(half), :], send_sem=sems.at[self.POST_S],
                recv_sem=sems.at[self.POST_R], device_id=sibling, device_id_type=mesh)

        keep = pltpu.make_async_copy(fin, gout.at[rows(c), :], sems.at[self.OUT])
        if stage == 0:
            loc.start()
            pre.start()
        elif stage == 1:
            loc.wait()
            pre.wait_recv()
            for blk in range(4):
                tot = own[blk] + sib[blk]
                own[blk] = tot
                wire[blk] = tot.astype(BF16)
            for j in range(3):
                ici(j).start()
        elif stage == 2:
            for j in range(3):
                ici(j).wait_recv()
            tot = own[mine]
            for j in range(3):
                tot = tot + rbuf[j].astype(F32)
            fin[...] = tot
            post(c).start()
            keep.start()
        else:
            post(1 - c).wait_recv()
            pre.wait_send()
            for j in range(3):
                ici(j).wait_send()
            post(c).wait_send()
            keep.wait()


def _attn_bwd(q, k, v, do, lse, delta, km, vm, early, nb, s, t):
    n = s // t
    ne = len(early)
    reds = [_StagedReduce(a.shape[1:]) for a in early]
    n_steps = HEADS * nb
    assert n_steps >= 4

    def body(q_ref, k_ref, v_ref, do_ref, lse_ref, dl_ref, km_ref, vm_ref, *rest):
        pin_refs, rest = rest[:ne], rest[ne:]
        dq_ref, dk_ref, dv_ref, dkm_ref, dvm_ref = rest[:5]
        gout_refs, (p_scr, ds_scr, dq_acc), red_scr = rest[5:5 + ne], rest[5 + ne:8 + ne], rest[8 + ne:]
        b = pl.program_id(1)
        step = pl.program_id(0) * nb + b
        for stage, at in enumerate((0, 1, n_steps - 2, n_steps - 1)):
            @pl.when(step == at)
            def _(stage=stage):
                for a, red in enumerate(reds):
                    red.run(stage, pin_refs[a], gout_refs[a], red_scr[6 * a:6 * a + 6])

        @pl.when(b == 0)
        def _():
            dkm_ref[...] = jnp.zeros_like(dkm_ref)
            dvm_ref[...] = jnp.zeros_like(dvm_ref)

        kr = lax.broadcasted_iota(jnp.int32, (t, t), 0)
        qc = lax.broadcasted_iota(jnp.int32, (t, t), 1)
        km_v, vm_v = km_ref[0, 0], vm_ref[0, 0]
        ptm = jnp.exp(_dot_nt(km_v, q_ref[0, 0]) - lse_ref[0, 0])
        dstm = (ptm * (_dot_nt(vm_v, do_ref[0, 0]) - dl_ref[0, 0])).astype(BF16)
        dkm_ref[0] += _dot(dstm, q_ref[0, 0])
        dvm_ref[0] += _dot(ptm.astype(BF16), do_ref[0, 0])
        dq_acc[...] = _dot_tn(dstm, km_v)
        def tiles(j):
            slot = j % 2
            kj = k_ref[0, 0, j * t:(j + 1) * t, :]
            vj = v_ref[0, 0, j * t:(j + 1) * t, :]
            def products(i):
                cs = slice(i * t, (i + 1) * t)
                return _dot_nt(kj, q_ref[0, 0, cs, :]), _dot_nt(vj, do_ref[0, 0, cs, :])

            nxt, pending = products(j), None
            for i in range(j, n):
                cs = slice(i * t, (i + 1) * t)
                st, dpt = nxt
                if i + 1 < n:
                    nxt = products(i + 1)
                if i == j:
                    st = jnp.where(kr <= qc, st, NEG_INF)
                pt = jnp.exp(st - lse_ref[0, 0, :, cs])
                dst = (pt * (dpt - dl_ref[0, 0, :, cs])).astype(BF16)
                p_scr[slot, :, cs] = pt.astype(BF16)
                ds_scr[slot, :, cs] = dst
                if pending is not None:
                    dq_acc[pending[0], :] += _dot_tn(pending[1], kj)
                pending = (cs, dst)
            dq_acc[pending[0], :] += _dot_tn(pending[1], kj)

        for j in range(n):
            slot = j % 2
            tiles(j)
            dv_ref[0, 0, j * t:(j + 1) * t, :] = _dot(p_scr[slot, :, j * t:s], do_ref[0, 0, j * t:s, :]).astype(BF16)
            dk_ref[0, 0, j * t:(j + 1) * t, :] = _dot(ds_scr[slot, :, j * t:s], q_ref[0, 0, j * t:s, :]).astype(BF16)
        dq_ref[0, 0] = dq_acc[...].astype(BF16)

    big = lambda w: pl.BlockSpec((1, 1, s, w), lambda h, b: (b, h, 0, 0))
    rowv = pl.BlockSpec((1, 1, 1, s), lambda h, b: (b, h, 0, 0))
    mk = lambda w: pl.BlockSpec((1, 1, N_META, w), lambda h, b: (0, h, 0, 0))
    mo = lambda w: pl.BlockSpec((1, N_META, w), lambda h, b: (h, 0, 0))
    return pl.pallas_call(
        body, name="attn_bwd", grid=(HEADS, nb),
        in_specs=[big(QK_PAD), big(QK_PAD), big(VDIM), big(VDIM), rowv, rowv, mk(QK_PAD), mk(VDIM)]
        + [pl.BlockSpec(memory_space=pl.ANY)] * ne,
        out_specs=[big(QK_PAD), big(QK_PAD), big(VDIM), mo(QK_PAD), mo(VDIM)]
        + [pl.BlockSpec(memory_space=pl.ANY)] * ne,
        out_shape=[jax.ShapeDtypeStruct((nb, HEADS, s, QK_PAD), BF16),
                   jax.ShapeDtypeStruct((nb, HEADS, s, QK_PAD), BF16),
                   jax.ShapeDtypeStruct((nb, HEADS, s, VDIM), BF16),
                   jax.ShapeDtypeStruct((HEADS, N_META, QK_PAD), F32),
                   jax.ShapeDtypeStruct((HEADS, N_META, VDIM), F32)]
        + [jax.ShapeDtypeStruct(a.shape[1:], F32) for a in early],
        scratch_shapes=[pltpu.VMEM((2, t, s), BF16), pltpu.VMEM((2, t, s), BF16), pltpu.VMEM((s, QK_PAD), F32)]
        + [sc for red in reds for sc in red.scratch()],
        compiler_params=_cparams("arbitrary", "arbitrary"),
    )(q, k, v, do, lse, delta, km, vm, *early)


def _up_bwd(dq, dk, dv, dkm, dvm, p, pm, tabs, tabs_m, wq_p, wkv_p, gq, gkv, nb, s, tm):
    nt = s // tm
    n = nb * nt
    c_t, sa_t, sb_t = tabs
    cm_t, sam_t, sbm_t = tabs_m

    def kv_path(dkh, dvh, pa, c, sa, sb, wkv, gkvv):
        dkpe = dkh[0][:, NOPE:]
        for h in range(1, HEADS):
            dkpe = dkpe + dkh[h][:, NOPE:]
        dkr = _rope_bwd(dkpe, c, sa, sb)
        dkv = jnp.concatenate([d[:, :NOPE] for d in dkh] + list(dvh), axis=1).astype(BF16)
        ckv = pa[:, Q_RANK:Q_RANK + KV_RANK]
        kvn, rkv = _rms(ckv, gkvv)
        dckv, dg = _rms_bwd(_dot(dkv, wkv), ckv, rkv, gkvv)
        return dckv, dkr, kvn.astype(BF16), dkv, jnp.sum(dg, axis=0, keepdims=True)

    def body(dq_ref, dk_ref, dv_ref, pa_ref, c_ref, sa_ref, sb_ref,
             dkm_ref, dvm_ref, pam_ref, cm_ref, sam_ref, sbm_ref,
             wq_ref, wkv_ref, gq_ref, gkv_ref,
             dpa_ref, dpam_ref, pq_ref, pkv_ref, dgq_ref, dgkv_ref, dwq_ref, dwkv_ref):
        i = pl.program_id(0)

        @pl.when(i == 0)
        def _():
            dwq_ref[...] = jnp.zeros_like(dwq_ref)
            dwkv_ref[...] = jnp.zeros_like(dwkv_ref)
            dgq_ref[...] = jnp.zeros_like(dgq_ref)
            dgkv_ref[...] = jnp.zeros_like(dgkv_ref)

        @pl.when(i < n)
        def _():
            c, sa, sb = c_ref[...], sa_ref[...], sb_ref[...]
            pa = pa_ref[...]
            parts = []
            for h in range(HEADS):
                dqh = dq_ref[0, h].astype(F32) * ATTN_SCALE
                parts += [dqh[:, :NOPE], _rope_bwd(dqh[:, NOPE:], c, sa, sb)]
            dql = jnp.concatenate(parts, axis=1).astype(BF16)
            cq = pa[:, 0:Q_RANK]
            gqv = gq_ref[...]
            qn, rq = _rms(cq, gqv)
            dwq_ref[...] += _dot_tn(dql, qn.astype(BF16))
            dcq, dg = _rms_bwd(_dot(dql, wq_ref[...]), cq, rq, gqv)
            dgq_ref[...] += jnp.sum(dg, axis=0, keepdims=True)
            dckv, dkr, kvn, dkv, dgk = kv_path([dk_ref[0, h].astype(F32) for h in range(HEADS)],
                                               [dv_ref[0, h].astype(F32) for h in range(HEADS)],
                                               pa, c, sa, sb, wkv_ref[...], gkv_ref[...])
            dwkv_ref[...] += _dot_tn(dkv, kvn)
            dgkv_ref[...] += dgk
            dpa_ref[...] = jnp.concatenate([dcq, dckv, dkr], axis=1).astype(BF16)

        @pl.when(i == n)
        def _():
            dckv, dkr, kvn, dkv, dgk = kv_path([dkm_ref[h] for h in range(HEADS)],
                                               [dvm_ref[h] for h in range(HEADS)],
                                               pam_ref[...], cm_ref[...], sam_ref[...], sbm_ref[...],
                                               wkv_ref[...], gkv_ref[...])
            dwkv_ref[...] += _dot_tn(dkv, kvn)
            dgkv_ref[...] += dgk
            dpam_ref[...] = jnp.concatenate([jnp.zeros((N_META, Q_RANK), F32), dckv, dkr], axis=1)
            for h in range(HEADS):
                pq_ref[h] = dwq_ref[QK_PAD * h:QK_PAD * h + NOPE + ROPE, :]
                pkv_ref[h, 0:NOPE, :] = dwkv_ref[NOPE * h:NOPE * (h + 1), :]
                pkv_ref[h, NOPE:NOPE + VDIM, :] = dwkv_ref[512 + VDIM * h:512 + VDIM * (h + 1), :]

    cl = lambda i: jnp.minimum(i, n - 1)
    hb = lambda w: pl.BlockSpec((1, HEADS, tm, w), lambda i: (cl(i) // nt, 0, cl(i) % nt, 0))
    tab = pl.BlockSpec((tm, 128), lambda i: (cl(i) % nt, 0))
    full = lambda a: pl.BlockSpec(a.shape, lambda i: (0,) * a.ndim)
    const = lambda shape: pl.BlockSpec(shape, lambda i: (0,) * len(shape))
    return pl.pallas_call(
        body, name="up_bwd", grid=(n + 1,),
        in_specs=[hb(QK_PAD), hb(QK_PAD), hb(VDIM), pl.BlockSpec((tm, 512), lambda i: (cl(i), 0)), tab, tab, tab,
                  full(dkm), full(dvm), pl.BlockSpec((N_META, 512), lambda i: (0, 0)),
                  full(cm_t), full(sam_t), full(sbm_t), full(wq_p), full(wkv_p), full(gq), full(gkv)],
        out_specs=[pl.BlockSpec((tm, 512), lambda i: (cl(i), 0)), const((N_META, 512)),
                   const((HEADS, NOPE + ROPE, Q_RANK)), const((HEADS, NOPE + VDIM, KV_RANK)),
                   const((1, Q_RANK)), const((1, KV_RANK))],
        out_shape=[jax.ShapeDtypeStruct((nb * s, 512), BF16), jax.ShapeDtypeStruct((N_META, 512), F32),
                   jax.ShapeDtypeStruct((HEADS, NOPE + ROPE, Q_RANK), F32),
                   jax.ShapeDtypeStruct((HEADS, NOPE + VDIM, KV_RANK), F32),
                   jax.ShapeDtypeStruct((1, Q_RANK), F32), jax.ShapeDtypeStruct((1, KV_RANK), F32)],
        scratch_shapes=[pltpu.VMEM((HEADS * QK_PAD, Q_RANK), F32), pltpu.VMEM((1024, KV_RANK), F32)],
        compiler_params=_cparams("arbitrary"),
    )(dq, dk, dv, p, c_t, sa_t, sb_t, dkm, dvm, pm, cm_t, sam_t, sbm_t, wq_p, wkv_p, gq, gkv)


def _in_bwd(x2d, dh2, dpa, dpb, meta, dpam, dccm, pm, w_in_p, norm_g, nb, s, tm):
    nt = s // tm
    n = nb * nt

    def body(x_ref, dh_ref, dpa_ref, dpb_ref, mt_ref, dpam_ref, dccm_ref, mc_ref, mh_ref, w_ref, g_ref,
             gx_ref, gm_ref, dw_hbm, dg_ref, acc_ref, sems):
        i = pl.program_id(0)

        @pl.when(i == 0)
        def _():
            acc_ref[...] = jnp.zeros_like(acc_ref)
            dg_ref[...] = jnp.zeros_like(dg_ref)

        def rows(x, dp, dres):
            g = g_ref[...]
            dpb16 = dp.astype(BF16)
            du = _dot(dpb16, w_ref[...])
            u, r1 = _rms(x, g)
            acc_ref[...] += _dot_tn(dpb16, u.astype(BF16))
            dx, dg = _rms_bwd(du, x, r1, g)
            dg_ref[...] += jnp.sum(dg, axis=0, keepdims=True)
            return dx if dres is None else dx + dres

        @pl.when(i < n)
        def _():
            dp = jnp.concatenate([dpa_ref[...], dpb_ref[...]], axis=1)
            gx_ref[...] = rows(x_ref[...], dp, dh_ref[...])

        @pl.when(i == n)
        def _():
            dcc = dccm_ref[0]
            for b in range(1, nb):
                dcc = dcc + dccm_ref[b]
            z8 = jnp.zeros((8, CONV_W), F32)
            dc = jnp.concatenate([z8, dcc * mh_ref[8:16, :]], axis=0)
            dh = jnp.concatenate([z8, dcc * mc_ref[8:16, :]], axis=0)
            z = jnp.zeros((N_META, CONV_W), F32)
            dp = jnp.concatenate([dpam_ref[...], z, z, dc, dh, z], axis=1)
            gm_ref[...] = rows(mt_ref[...], dp, None)
            per = IN_DIM // 4
            cps = [pltpu.make_async_copy(acc_ref.at[0:448], dw_hbm.at[0, 0:448], sems.at[0]),
                   pltpu.make_async_copy(acc_ref.at[512:per + 64], dw_hbm.at[0, 448:per], sems.at[1])]
            for qq in range(1, 4):
                cps.append(pltpu.make_async_copy(acc_ref.at[per * qq + 64:per * (qq + 1) + 64], dw_hbm.at[qq],
                                                 sems.at[qq + 1]))
            for cp in cps:
                cp.start()
            for cp in cps:
                cp.wait()

    cl = lambda i: jnp.minimum(i, n - 1)
    row = lambda w: pl.BlockSpec((tm, w), lambda i: (cl(i), 0))
    full = lambda a: pl.BlockSpec(a.shape, lambda i: (0,) * a.ndim)
    mblk = lambda j: pl.BlockSpec((N_META, 512), lambda i: (0, j))
    return pl.pallas_call(
        body, name="in_bwd", grid=(n + 1,),
        in_specs=[row(D_MODEL), row(D_MODEL), row(512), row(2560), full(meta), full(dpam), full(dccm),
                  mblk(BLK_CC), mblk(BLK_CH), full(w_in_p), full(norm_g)],
        out_specs=[row(D_MODEL), pl.BlockSpec((N_META, D_MODEL), lambda i: (0, 0)),
                   pl.BlockSpec(memory_space=pl.ANY), pl.BlockSpec((1, D_MODEL), lambda i: (0, 0))],
        out_shape=[jax.ShapeDtypeStruct((nb * s, D_MODEL), F32), jax.ShapeDtypeStruct((N_META, D_MODEL), F32),
                   jax.ShapeDtypeStruct((4, IN_DIM // 4, D_MODEL), F32), jax.ShapeDtypeStruct((1, D_MODEL), F32)],
        scratch_shapes=[pltpu.VMEM((IN_PAD, D_MODEL), F32), pltpu.SemaphoreType.DMA((5,))],
        compiler_params=_cparams("arbitrary"),
    )(x2d, dh2, dpa, dpb, meta, dpam, dccm, pm, pm, w_in_p, norm_g)


def _gather_weights(split, pieces, out_rows, whole, zero_fills):
    ns, nw, nz = len(split), len(whole), len(zero_fills)
    flat = [(a, pc) for a in range(ns) for pc in pieces[a]]
    nk = len(flat)

    def body(*refs):
        ins, wins, zins = refs[:ns], refs[ns:ns + nw], refs[ns + nw:ns + nw + nz]
        outs, wouts = refs[ns + nw + nz:2 * ns + nw + nz], refs[2 * ns + nw + nz:2 * (ns + nw) + nz]
        send_sems, recv_sems, fwd_send, fwd_recv, loc_sems, w_send, w_recv, w_loc, z_sems = refs[2 * (ns + nw) + nz:]
        x, y, c = lax.axis_index("x"), lax.axis_index("y"), lax.axis_index("c")
        mine = 2 * x + y
        chips = [(1 - x, y), (x, 1 - y), (1 - x, 1 - y)]
        chip_of = [2 * px + py for px, py in chips]

        def src(k):
            a, (s0, nr, _, _, _, _) = flat[k]
            return ins[a].at[s0:s0 + nr]

        def dst(k, q):
            a, (_, nr, per, first, rest, _) = flat[k]
            row = per * q + first + (rest - first) * jnp.minimum(q, 1)
            return outs[a].at[pl.ds(pl.multiple_of(row, 16), nr)]

        def ici(k, j, q):
            px, py = chips[j]
            return pltpu.make_async_remote_copy(
                src_ref=src(k), dst_ref=dst(k, q), send_sem=send_sems.at[k, j], recv_sem=recv_sems.at[k, j],
                device_id=(px, py, c), device_id_type=pl.DeviceIdType.MESH)

        def fwd(k, j):
            ref = dst(k, chip_of[j])
            return pltpu.make_async_remote_copy(
                src_ref=ref, dst_ref=ref, send_sem=fwd_send.at[k, j], recv_sem=fwd_recv.at[k, j],
                device_id=(x, y, 1 - c), device_id_type=pl.DeviceIdType.MESH)

        def wcopy(b, j, q):
            px, py = chips[j]
            return pltpu.make_async_remote_copy(
                src_ref=wins[b], dst_ref=wouts[b].at[q], send_sem=w_send.at[b, j], recv_sem=w_recv.at[b, j],
                device_id=(px, py, c), device_id_type=pl.DeviceIdType.MESH)

        local = [pltpu.make_async_copy(src(k), dst(k, mine), loc_sems.at[k]) for k in range(nk)]
        local += [pltpu.make_async_copy(wins[b], wouts[b].at[mine], w_loc.at[b]) for b in range(nw)]
        for z, (a, _, row0) in enumerate(zero_fills):
            local.append(pltpu.make_async_copy(zins[z], outs[a].at[row0:row0 + zins[z].shape[0]], z_sems.at[z]))
        wsends = [wcopy(b, j, mine) for b in range(nw) for j in range(3)]
        for cp in local + wsends:
            cp.start()

        for half in (0, 1):
            @pl.when(c == half)
            def _(half=half):
                my_k = [k for k in range(nk) if flat[k][1][5] == half]
                other_k = [k for k in range(nk) if flat[k][1][5] != half]
                sends = [ici(k, j, mine) for k in my_k for j in range(3)]
                for cp in sends:
                    cp.start()
                passed = []
                for k in my_k:
                    for j in range(3):
                        ici(k, j, chip_of[j]).wait_recv()
                        cp = fwd(k, j)
                        cp.start()
                        passed.append(cp)
                for k in other_k:
                    for j in range(3):
                        fwd(k, j).wait_recv()
                for cp in sends + passed:
                    cp.wait_send()

        for b in range(nw):
            for j in range(3):
                wcopy(b, j, chip_of[j]).wait_recv()
        for cp in wsends:
            cp.wait_send()
        for cp in local:
            cp.wait()

    hbm = pl.BlockSpec(memory_space=pl.ANY)
    dma = pltpu.SemaphoreType.DMA
    zeros = [z for _, z, _ in zero_fills]
    return pl.pallas_call(
        body, name="gather_weights",
        in_specs=[hbm] * (ns + nw + nz), out_specs=[hbm] * (ns + nw),
        out_shape=([jax.ShapeDtypeStruct((out_rows[a], split[a].shape[1]), split[a].dtype) for a in range(ns)]
                   + [jax.ShapeDtypeStruct((4,) + w.shape, w.dtype) for w in whole]),
        scratch_shapes=[dma((nk, 3)), dma((nk, 3)), dma((nk, 3)), dma((nk, 3)), dma((nk,)),
                        dma((nw, 3)), dma((nw, 3)), dma((nw,)), dma((nz,))],
        compiler_params=pltpu.CompilerParams(vmem_limit_bytes=VMEM_LIMIT),
    )(*split, *whole, *zeros)


def _reduce_grads(parts, small):
    n = len(parts)
    shapes = [a.shape[1:] for a in parts]
    halves = [(sh[0] // 2, sh[1]) for sh in shapes]

    def body(*refs):
        pin, sm_in = refs[:n], refs[n]
        gout, sm_out = refs[n + 1:2 * n + 1], refs[2 * n + 1]
        scr = refs[2 * n + 2:]
        own, sib, wire, rbuf = scr[:n], scr[n:2 * n], scr[2 * n:3 * n], scr[3 * n:4 * n]
        (sbuf, send_sems, recv_sems, loc_sems, pre_send, pre_recv, post_send, post_recv,
         sm_send, sm_recv) = scr[4 * n:]
        x, y, c = lax.axis_index("x"), lax.axis_index("y"), lax.axis_index("c")
        mine = 2 * x + y
        me = 4 * x + 2 * y + c
        sibling = (x, y, 1 - c)
        chips = [(1 - x, y), (x, 1 - y), (1 - x, 1 - y)]

        def rows(a, half):
            r2 = halves[a][0]
            return pl.ds(pl.multiple_of(half * r2, r2), r2)

        def pre(a):
            return pltpu.make_async_remote_copy(
                src_ref=pin[a].at[:, rows(a, 1 - c), :], dst_ref=sib[a], send_sem=pre_send.at[a],
                recv_sem=pre_recv.at[a], device_id=sibling, device_id_type=pl.DeviceIdType.MESH)

        def ici(a, j):
            px, py = chips[j]
            return pltpu.make_async_remote_copy(
                src_ref=wire[a].at[2 * px + py], dst_ref=rbuf[a].at[j], send_sem=send_sems.at[a, j],
                recv_sem=recv_sems.at[a, j], device_id=(px, py, c), device_id_type=pl.DeviceIdType.MESH)

        def post(a, half):
            ref = gout[a].at[rows(a, half), :]
            return pltpu.make_async_remote_copy(
                src_ref=ref, dst_ref=ref, send_sem=post_send.at[a], recv_sem=post_recv.at[a],
                device_id=sibling, device_id_type=pl.DeviceIdType.MESH)

        def small_copy(kk):
            peer = (x ^ (kk >> 2), y ^ ((kk >> 1) & 1), c ^ (kk & 1))
            return pltpu.make_async_remote_copy(
                src_ref=sm_in, dst_ref=sbuf.at[kk], send_sem=sm_send.at[kk - 1], recv_sem=sm_recv.at[kk - 1],
                device_id=peer, device_id_type=pl.DeviceIdType.MESH)

        local = [pltpu.make_async_copy(pin[a].at[:, rows(a, c), :], own[a], loc_sems.at[a]) for a in range(n)]
        pres = [pre(a) for a in range(n)]
        smalls = [small_copy(kk) for kk in range(1, 8)]
        for cp in local + pres + smalls:
            cp.start()
        sbuf[0] = sm_in[...]
        sends = []
        for a in range(n):
            local[a].wait()
            pres[a].wait_recv()
            for blk in range(4):
                tot = own[a][blk] + sib[a][blk]
                own[a][blk] = tot
                wire[a][blk] = tot.astype(BF16)
            for j in range(3):
                cp = ici(a, j)
                cp.start()
                sends.append(cp)
        for cp in smalls:
            cp.wait_recv()
        total = sbuf[me]
        for d in range(1, 8):
            total = total + sbuf[me ^ d]
        sm_out[...] = total
        posts = []
        for a in range(n):
            for j in range(3):
                ici(a, j).wait_recv()
            fin = own[a][mine]
            for j in range(3):
                fin = fin + rbuf[a][j].astype(F32)
            gout[a][rows(a, c), :] = fin
            cp = post(a, c)
            cp.start()
            posts.append(cp)
        for a in range(n):
            post(a, 1 - c).wait_recv()
        for cp in pres + sends + smalls + posts:
            cp.wait_send()

    hbm = pl.BlockSpec(memory_space=pl.ANY)
    vmem = pl.BlockSpec(memory_space=pltpu.VMEM)
    dma = pltpu.SemaphoreType.DMA
    return pl.pallas_call(
        body, name="reduce_grads",
        in_specs=[hbm] * n + [vmem], out_specs=[vmem] * (n + 1),
        out_shape=[jax.ShapeDtypeStruct(sh, F32) for sh in shapes] + [jax.ShapeDtypeStruct(small.shape, F32)],
        scratch_shapes=([pltpu.VMEM((4,) + hs, F32) for hs in halves] + [pltpu.VMEM((4,) + hs, F32) for hs in halves]
                        + [pltpu.VMEM((4,) + hs, BF16) for hs in halves]
                        + [pltpu.VMEM((3,) + hs, BF16) for hs in halves]
                        + [pltpu.VMEM((8,) + small.shape, F32), dma((n, 3)), dma((n, 3)), dma((n,)),
                           dma((n,)), dma((n,)), dma((n,)), dma((n,)), dma((7,)), dma((7,))]),
        compiler_params=pltpu.CompilerParams(vmem_limit_bytes=VMEM_LIMIT),
    )(*parts, small)


def _adamw(w, g, m, v, name):
    shape = w.shape
    w2, g2, m2, v2 = (a.reshape((-1, shape[-1])) for a in (w, g, m, v))

    def body(w_ref, g_ref, m_ref, v_ref, d_ref, nm_ref, nv_ref):
        gv = g_ref[...]
        nm = ADAM_B1 * m_ref[...] + (1.0 - ADAM_B1) * gv
        nv = ADAM_B2 * v_ref[...] + (1.0 - ADAM_B2) * (gv * gv)
        m_hat = nm / (1.0 - ADAM_B1 ** ADAM_STEP)
        v_hat = nv / (1.0 - ADAM_B2 ** ADAM_STEP)
        d_ref[...] = -ADAM_LR * (m_hat / (jnp.sqrt(v_hat) + ADAM_EPS) + ADAM_WD * w_ref[...])
        nm_ref[...] = nm
        nv_ref[...] = nv

    rows, cols = w2.shape
    nblk = cols // 256 if cols % 256 == 0 and rows >= 64 else 1
    blk = pl.BlockSpec((rows, cols // nblk), lambda j: (0, j))
    out = pl.pallas_call(
        body, name=name, grid=(nblk,), in_specs=[blk] * 4, out_specs=[blk] * 3,
        out_shape=[jax.ShapeDtypeStruct(w2.shape, F32)] * 3,
        compiler_params=_cparams("parallel"),
    )(w2, g2, m2, v2)
    return tuple(a.reshape(shape) for a in out)


def kernel(x, meta_tokens, norm_g, w_in, q_norm_g, w_q_up, kv_norm_g, w_kv_up, conv_w, attn_out_g, conv_out_g, w_out, final_norm_g, loss_target, m_meta_tokens, m_norm_g, m_w_in, m_q_norm_g, m_w_q_up, m_kv_norm_g, m_w_kv_up, m_conv_w, m_attn_out_g, m_conv_out_g, m_w_out, m_final_norm_g, v_meta_tokens, v_norm_g, v_w_in, v_q_norm_g, v_w_q_up, v_kv_norm_g, v_w_kv_up, v_conv_w, v_attn_out_g, v_conv_out_g, v_w_out, v_final_norm_g):
    nb, s, _ = x.shape
    tm = min(ROW_TILE, s)
    ta = min(ATTN_TILE, s)
    assert s % tm == 0 and s % ta == 0 and tm % 16 == 0
    r = nb * s

    tr = lambda a: jnp.transpose(a[0])
    w_in_p, wq_p, wkv_p, g_cw, g_meta = _gather_weights(
        [tr(w_in).astype(BF16), tr(w_q_up).astype(BF16), tr(w_kv_up).astype(BF16)],
        [W_IN_PIECES, W_Q_PIECES, W_KV_PIECES], [IN_PAD, HEADS * QK_PAD, 1024],
        [conv_w[0], meta_tokens],
        [(0, jnp.zeros((64, D_MODEL), BF16), 448)]
        + [(1, jnp.zeros((64, Q_RANK), BF16), QK_PAD * h + NOPE + ROPE) for h in range(HEADS)])
    conv_f = jnp.transpose(g_cw, (1, 0, 2)).reshape(3, CONV_W)
    meta_f = jnp.transpose(g_meta, (1, 0, 2)).reshape(N_META, D_MODEL)

    c_all, sa_all, sb_all = _rope_tables(N_META + s)
    tabs_m = (c_all[:N_META], sa_all[:N_META], sb_all[:N_META])
    tabs = (c_all[N_META:], sa_all[N_META:], sb_all[N_META:])
    gid = np.arange(CONV_W) // CONV_GROUP
    gmat = jnp.asarray(np.where(gid[:, None] == gid[None, :], 1.0 / CONV_GROUP, 0.0), BF16)
    ga, gc = attn_out_g, conv_out_g
    gf = final_norm_g.reshape(1, D_MODEL)

    x2d = x.reshape(r, D_MODEL)
    tgt2d = loss_target.reshape(r, D_MODEL)

    p, q, k, v, pm, km, vm, w_out_f = _fwd_proj(x2d, meta_f, tabs, tabs_m, norm_g, w_in_p, q_norm_g, wq_p,
                                                kv_norm_g, wkv_p, w_out[0].astype(BF16), nb, s, tm)
    o, lse = _attn_fwd(q, k, v, km, vm, nb, s, ta)
    dh2, dycat, dw_out, dgf, loss_acc = _out_fwd_bwd(x2d, tgt2d, o, p, pm, conv_f, ga, gc, gmat, w_out_f, gf,
                                                     nb, s, tm)
    dpb, do, delta, dccm, dga, dgc, dcw = _gate_bwd(dycat, o, p, pm, conv_f, ga, gc, gmat, nb, s, tm)
    p_out = dw_out.reshape(4, D_MODEL // 4, D_MODEL)
    dq, dk, dv, dkm, dvm, g_w_out = _attn_bwd(q, k, v, do, lse, delta, km, vm, [p_out], nb, s, ta)
    dpa, dpam, p_q, p_kv, dgq, dgkv = _up_bwd(dq, dk, dv, dkm, dvm, p, pm, tabs, tabs_m, wq_p, wkv_p,
                                              q_norm_g, kv_norm_g, nb, s, tm)
    gx, gmeta, p_in, dng = _in_bwd(x2d, dh2, dpa, dpb, meta_f, dpam, dccm, pm, w_in_p, norm_g, nb, s, tm)

    flat =jnp.concatenate([dng.reshape(-1), dgq.reshape(-1), dgkv.reshape(-1), dga.reshape(-1), dgc.reshape(-1),
                            dgf.reshape(-1), dcw[:3].reshape(-1), gmeta.reshape(-1), loss_acc[0, 0:1]])
    n_small = flat.shape[0]
    rows_small = -(-n_small // 1024) * 8
    small = jnp.pad(flat, (0, rows_small * 128 - n_small)).reshape(rows_small, 128)
    g_w_in_t, g_w_q_t, g_w_kv_t, small_sum = _reduce_grads([p_in, p_q, p_kv], small)
    ssum = small_sum.reshape(-1)

    def take(off, n):
        return ssum[off:off + n], off + n

    off = 0
    g_norm, off = take(off, D_MODEL)
    g_qn, off = take(off, Q_RANK)
    g_kvn, off = take(off, KV_RANK)
    g_ga, off = take(off, CONV_W)
    g_gc, off = take(off, CONV_W)
    g_gf, off = take(off, D_MODEL)
    g_cw_all, off = take(off, 3 * CONV_W)
    g_meta_all, off = take(off, N_META * D_MODEL)
    loss = ssum[off]
    chip = 2 * lax.axis_index("x") + lax.axis_index("y")
    g_conv = lax.dynamic_slice(g_cw_all.reshape(3, CONV_W), (0, chip * 128), (3, 128))
    g_mt = lax.dynamic_slice(g_meta_all.reshape(N_META, D_MODEL), (0, chip * 256), (N_META, 256))

    grads = {
        "meta_tokens": g_mt, "norm_g": g_norm.reshape(1, -1), "w_in": g_w_in_t, "q_norm_g": g_qn.reshape(1, -1),
        "w_q_up": g_w_q_t, "kv_norm_g": g_kvn.reshape(1, -1), "w_kv_up": jnp.transpose(g_w_kv_t)[None],
        "conv_w": g_conv[None], "attn_out_g": g_ga.reshape(1, -1), "conv_out_g": g_gc.reshape(1, -1),
        "w_out": g_w_out[None], "final_norm_g": g_gf,
    }
    transposed = ("w_in", "w_q_up")
    weights = {
        "meta_tokens": (meta_tokens, m_meta_tokens, v_meta_tokens), "norm_g": (norm_g, m_norm_g, v_norm_g),
        "w_in": (w_in, m_w_in, v_w_in), "q_norm_g": (q_norm_g, m_q_norm_g, v_q_norm_g),
        "w_q_up": (w_q_up, m_w_q_up, v_w_q_up), "kv_norm_g": (kv_norm_g, m_kv_norm_g, v_kv_norm_g),
        "w_kv_up": (w_kv_up, m_w_kv_up, v_w_kv_up), "conv_w": (conv_w, m_conv_w, v_conv_w),
        "attn_out_g": (attn_out_g, m_attn_out_g, v_attn_out_g), "conv_out_g": (conv_out_g, m_conv_out_g, v_conv_out_g),
        "w_out": (w_out, m_w_out, v_w_out), "final_norm_g": (final_norm_g, m_final_norm_g, v_final_norm_g),
    }
    names = list(weights)
    deltas, new_m, new_v = [], [], []
    for nme in names:
        w_, m_, v_ = weights[nme]
        if nme in transposed:
            res = _adamw(tr(w_), grads[nme], tr(m_), tr(v_), "adamw_" + nme)
            g_, d_, nm_, nv_ = (jnp.transpose(a)[None] for a in (grads[nme],) + res)
        else:
            g_ = grads[nme].reshape(w_.shape)
            d_, nm_, nv_ = _adamw(w_, g_, m_, v_, "adamw_" + nme)
        grads[nme] = g_
        deltas.append(d_)
        new_m.append(nm_)
        new_v.append(nv_)

    grad_x = gx.reshape(nb, s, D_MODEL)
    return (loss, grad_x, *[grads[nme] for nme in names], *deltas, *new_m, *new_v)
```

```python
import functools

import jax
import jax.numpy as jnp
import numpy as np
from jax import lax
from jax.experimental import pallas as pl
from jax.experimental.pallas import tpu as pltpu

F32 = jnp.float32
BF16 = jnp.bfloat16

D_MODEL = 1024
N_META = 16
HEADS = 4
NOPE = 128
ROPE = 64
VDIM = 128
QK_PAD = 256
Q_RANK = 256
KV_RANK = 128
CONV_W = 512
CONV_GROUP = 64
ROPE_THETA = 10000.0
EPS = 1e-6
ATTN_SCALE = (NOPE + ROPE) ** -0.5
IN_DIM = 3008
IN_PAD = 3072
BLK_ZA, BLK_CB, BLK_CC, BLK_CH, BLK_ZC = 1, 2, 3, 4, 5
NEG_INF = -1e30

ADAM_LR = 0.001
ADAM_B1 = 0.9
ADAM_B2 = 0.999
ADAM_EPS = 1e-08
ADAM_WD = 0.01
ADAM_STEP = 10

ROW_TILE = 512
ATTN_TILE = 256
HALO = 16
VMEM_LIMIT = 56 * 1024 * 1024

NT = (((1,), (1,)), ((), ()))
TN = (((0,), (0,)), ((), ()))


def _cparams(*sem):
    return pltpu.CompilerParams(dimension_semantics=sem, vmem_limit_bytes=VMEM_LIMIT)


def _dot(a, b):
    return jnp.dot(a, b, preferred_element_type=F32)


def _dot_nt(a, b):
    return lax.dot_general(a, b, NT, preferred_element_type=F32)


def _dot_tn(a, b):
    return lax.dot_general(a, b, TN, preferred_element_type=F32)


def _rms(x, g):
    r = lax.rsqrt(jnp.mean(x * x, axis=-1, keepdims=True) + EPS)
    return x * r * g, r


def _rms_bwd(dy, x, r, g):
    xh = x * r
    dyg = dy * g
    dx = r * (dyg - xh * jnp.mean(dyg * xh, axis=-1, keepdims=True))
    return dx, dy * xh


def _f32(a):
    return a[...].astype(F32)


def _sigmoid(z):
    return 1.0 / (1.0 + jnp.exp(-z))


def _rope(b, c, sa, sb):
    return b * c + pltpu.roll(b, 96, 1) * sa + pltpu.roll(b, 32, 1) * sb


def _rope_bwd(d, c, sa, sb):
    return d * c + pltpu.roll(d * sa, 32, 1) + pltpu.roll(d * sb, 96, 1)


def _group_mean(x, gmat):
    hi = x.astype(BF16)
    lo = (x - hi.astype(F32)).astype(BF16)
    return _dot(hi, gmat) + _dot(lo, gmat)


def _row_of(col, rows):
    return jnp.transpose(jnp.broadcast_to(col, (rows, 128)))[0:1, :]


def _rope_tables(n_pos):
    half = ROPE // 2
    inv_freq = (np.float32(1.0) / (np.float32(ROPE_THETA) ** (np.arange(half, dtype=np.float32) / np.float32(half))))
    ang = np.arange(n_pos, dtype=np.float32)[:, None] * inv_freq.astype(np.float32)[None, :]
    cos, sin = np.cos(ang).astype(np.float32), np.sin(ang).astype(np.float32)
    z = np.zeros((n_pos, half), np.float32)
    c = np.concatenate([cos, cos, z, z], axis=1)
    sa = np.concatenate([-sin, z, z, z], axis=1)
    sb = np.concatenate([z, sin, z, z], axis=1)
    return jnp.asarray(c), jnp.asarray(sa), jnp.asarray(sb)


W_IN_PIECES = ((0, 384, 752, 0, 64, 0), (384, 64, 752, 384, 448, 1), (448, 304, 752, 512, 512, 1))
W_Q_PIECES = ((0, 96, 256, 0, 0, 0), (96, 96, 256, 96, 96, 1))
W_KV_PIECES = ((0, 128, 128, 0, 0, 0), (128, 128, 128, 512, 512, 1))
W_OUT_PIECES = ((0, 128, 256, 0, 0, 0), (128, 128, 256, 128, 128, 1))


class _StagedGather:
    def __init__(self, pieces):
        self.pieces = pieces

    def scratch(self):
        nk, dma = len(self.pieces), pltpu.SemaphoreType.DMA
        return [dma((nk, 3)), dma((nk, 3)), dma((nk, 3)), dma((nk, 3)), dma((nk,))]

    def run(self, stage, src_ref, out_ref, scr):
        send_sems, recv_sems, fwd_send, fwd_recv, loc_sems = scr
        pieces = self.pieces
        nk = len(pieces)
        x, y, c = lax.axis_index("x"), lax.axis_index("y"), lax.axis_index("c")
        mine = 2 * x + y
        chips = [(1 - x, y), (x, 1 - y), (1 - x, 1 - y)]
        chip_of = [2 * px + py for px, py in chips]
        mesh = pl.DeviceIdType.MESH

        def src(k):
            s0, nr = pieces[k][0], pieces[k][1]
            return src_ref.at[s0:s0 + nr]

        def dst(k, q):
            _, nr, per, first, rest, _ = pieces[k]
            row = per * q + first + (rest - first) * jnp.minimum(q, 1)
            return out_ref.at[pl.ds(pl.multiple_of(row, 16), nr)]

        def ici(k, j, q):
            px, py = chips[j]
            return pltpu.make_async_remote_copy(
                src_ref=src(k), dst_ref=dst(k, q), send_sem=send_sems.at[k, j], recv_sem=recv_sems.at[k, j],
                device_id=(px, py, c), device_id_type=mesh)

        def fwd(k, j):
            ref = dst(k, chip_of[j])
            return pltpu.make_async_remote_copy(
                src_ref=ref, dst_ref=ref, send_sem=fwd_send.at[k, j], recv_sem=fwd_recv.at[k, j],
                device_id=(x, y, 1 - c), device_id_type=mesh)

        local = [pltpu.make_async_copy(src(k), dst(k, mine), loc_sems.at[k]) for k in range(nk)]
        if stage == 0:
            for cp in local:
                cp.start()
        if stage == 2:
            for cp in local:
                cp.wait()
        for half in (0, 1):
            @pl.when(c == half)
            def _(half=half):
                my_k = [k for k in range(nk) if pieces[k][5] == half]
                other_k = [k for k in range(nk) if pieces[k][5] != half]
                for k in my_k:
                    for j in range(3):
                        if stage == 0:
                            ici(k, j, mine).start()
                        elif stage == 1:
                            ici(k, j, chip_of[j]).wait_recv()
                            fwd(k, j).start()
                        else:
                            ici(k, j, mine).wait_send()
                            fwd(k, j).wait_send()
                if stage == 2:
                    for k in other_k:
                        for j in range(3):
                            fwd(k, j).wait_recv()


def _fwd_proj(x2d, meta, tabs, tabs_m, norm_g, w_in_p, q_norm_g, wq_p, kv_norm_g, wkv_p, w_out_shard, nb, s, tm):
    nt = s // tm
    n = nb * nt
    n_steps = n + 1
    c_t, sa_t, sb_t = tabs
    cm_t, sam_t, sbm_t = tabs_m
    gat = _StagedGather(W_OUT_PIECES)
    assert n_steps >= 3

    def body(x_ref, c_ref, sa_ref, sb_ref, mt_ref, cm_ref, sam_ref, sbm_ref,
             g_ref, w_ref, gq_ref, wq_ref, gkv_ref, wkv_ref, wos_ref,
             p_ref, q_ref, k_ref, v_ref, pm_ref, km_ref, vm_ref, wo_ref, *gat_scr):
        i = pl.program_id(0)
        for stage, at in enumerate((0, n_steps - 2, n_steps - 1)):
            @pl.when(i == at)
            def _(stage=stage):
                gat.run(stage, wos_ref, wo_ref, gat_scr)

        def project(xv, c, sa, sb, p_out, q_out, k_out, v_out):
            u, _ = _rms(xv, g_ref[...])
            pb = _dot_nt(u.astype(BF16), w_ref[...]).astype(BF16)
            p_out[...] = pb
            p = pb.astype(F32)
            qn, _ = _rms(p[:, 0:Q_RANK], gq_ref[...])
            q = _dot_nt(qn.astype(BF16), wq_ref[...])
            kvn, _ = _rms(p[:, Q_RANK:Q_RANK + KV_RANK], gkv_ref[...])
            kv = _dot_nt(kvn.astype(BF16), wkv_ref[...])
            kpe = _rope(p[:, 384:512], c, sa, sb)
            for h in range(HEADS):
                if q_out is not None:
                    pe = _rope(q[:, QK_PAD * h + NOPE:QK_PAD * (h + 1)], c, sa, sb)
                    qh = jnp.concatenate([q[:, QK_PAD * h:QK_PAD * h + NOPE], pe], axis=1)
                    q_out[0, h] = (qh * ATTN_SCALE).astype(BF16)
                k_out[0, h] = jnp.concatenate([kv[:, NOPE * h:NOPE * (h + 1)], kpe], axis=1).astype(BF16)
                v_out[0, h] = kv[:, 512 + VDIM * h:512 + VDIM * (h + 1)].astype(BF16)

        @pl.when(i < n)
        def _():
            project(x_ref[...], c_ref[...], sa_ref[...], sb_ref[...], p_ref, q_ref, k_ref, v_ref)

        @pl.when(i == n)
        def _():
            project(mt_ref[...], cm_ref[...], sam_ref[...], sbm_ref[...], pm_ref, None, km_ref, vm_ref)

    cl = lambda i: jnp.minimum(i, n - 1)
    full = lambda a: pl.BlockSpec(a.shape, lambda i: (0,) * a.ndim)
    const = lambda shape: pl.BlockSpec(shape, lambda i: (0,) * len(shape))
    tab = pl.BlockSpec((tm, 128), lambda i: (cl(i) % nt, 0))
    hb = lambda w: pl.BlockSpec((1, HEADS, tm, w), lambda i: (cl(i) // nt, 0, cl(i) % nt, 0))
    hbm = pl.BlockSpec(memory_space=pl.ANY)
    return pl.pallas_call(
        body, name="fwd_proj", grid=(n_steps,),
        in_specs=[pl.BlockSpec((tm, D_MODEL), lambda i: (cl(i), 0)), tab, tab, tab,
                  full(meta), full(cm_t), full(sam_t), full(sbm_t),
                  full(norm_g), full(w_in_p), full(q_norm_g), full(wq_p), full(kv_norm_g), full(wkv_p), hbm],
        out_specs=[pl.BlockSpec((tm, IN_PAD), lambda i: (cl(i), 0)), hb(QK_PAD), hb(QK_PAD), hb(VDIM),
                   const((N_META, IN_PAD)), const((1, HEADS, N_META, QK_PAD)), const((1, HEADS, N_META, VDIM)), hbm],
        out_shape=[jax.ShapeDtypeStruct((nb * s, IN_PAD), BF16),
                   jax.ShapeDtypeStruct((nb, HEADS, s, QK_PAD), BF16),
                   jax.ShapeDtypeStruct((nb, HEADS, s, QK_PAD), BF16),
                   jax.ShapeDtypeStruct((nb, HEADS, s, VDIM), BF16),
                   jax.ShapeDtypeStruct((N_META, IN_PAD), BF16),
                   jax.ShapeDtypeStruct((1, HEADS, N_META, QK_PAD), BF16),
                   jax.ShapeDtypeStruct((1, HEADS, N_META, VDIM), BF16),
                   jax.ShapeDtypeStruct((D_MODEL, D_MODEL), BF16)],
        scratch_shapes=gat.scratch(),
        compiler_params=_cparams("arbitrary"),
    )(x2d, c_t, sa_t, sb_t, meta, cm_t, sam_t, sbm_t, norm_g, w_in_p, q_norm_g, wq_p, kv_norm_g, wkv_p, w_out_shard)


def _attn_fwd(q, k, v, km, vm, nb, s, tq):
    nq = s // tq

    def body(q_ref, k_ref, v_ref, km_ref, vm_ref, o_ref, lse_ref, s_scr, p_scr):
        row = lax.broadcasted_iota(jnp.int32, (tq, tq), 0)
        col = lax.broadcasted_iota(jnp.int32, (tq, tq), 1)
        def scores(i):
            slot = i % 2
            qi = q_ref[0, 0, i * tq:(i + 1) * tq, :]
            sm = _dot_nt(qi, km_ref[0, 0])
            m128 = None
            for j in range(i + 1):
                sc = _dot_nt(qi, k_ref[0, 0, j * tq:(j + 1) * tq, :])
                if j == i:
                    sc = jnp.where(col <= row, sc, NEG_INF)
                s_scr[slot, :, j * tq:(j + 1) * tq] = sc
                mx = sc[:, 0:128]
                for c0 in range(128, tq, 128):
                    mx = jnp.maximum(mx, sc[:, c0:c0 + 128])
                m128 = mx if m128 is None else jnp.maximum(m128, mx)
            return sm, jnp.maximum(jnp.max(m128, axis=1, keepdims=True), jnp.max(sm, axis=1, keepdims=True))

        def weighted_sum(i, pm, l):
            n = (i + 1) * tq
            acc = _dot(p_scr[i % 2, :, 0:n], v_ref[0, 0, 0:n, :]) + _dot(pm.astype(BF16), vm_ref[0, 0])
            o_ref[0, 0, i * tq:(i + 1) * tq, :] = acc / l

        nxt, pending = scores(0), None
        for i in range(nq):
            slot = i % 2
            sm, m = nxt
            if i + 1 < nq:
                nxt = scores(i + 1)
            pm = jnp.exp(sm - m)
            l128 = None
            for j in range(i + 1):
                p = jnp.exp(s_scr[slot, :, j * tq:(j + 1) * tq] - m)
                p_scr[slot, :, j * tq:(j + 1) * tq] = p.astype(BF16)
                ps = p[:, 0:128]
                for c0 in range(128, tq, 128):
                    ps = ps + p[:, c0:c0 + 128]
                l128 = ps if l128 is None else l128 + ps
            l = jnp.sum(l128, axis=1, keepdims=True) + jnp.sum(pm, axis=1, keepdims=True)
            lse_ref[0, 0, :, i * tq:(i + 1) * tq] = _row_of(m + jnp.log(l), tq)
            if pending is not None:
                weighted_sum(*pending)
            pending = (i, pm, l)
        weighted_sum(*pending)

    hblk = lambda w: pl.BlockSpec((1, 1, s, w), lambda b, h: (b, h, 0, 0))
    mblk = lambda w: pl.BlockSpec((1, 1, N_META, w), lambda b, h: (0, h, 0, 0))
    return pl.pallas_call(
        body, name="attn_fwd", grid=(nb, HEADS),
        in_specs=[hblk(QK_PAD), hblk(QK_PAD), hblk(VDIM), mblk(QK_PAD), mblk(VDIM)],
        out_specs=[hblk(VDIM), pl.BlockSpec((1, 1, 1, s), lambda b, h: (b, h, 0, 0))],
        out_shape=[jax.ShapeDtypeStruct((nb, HEADS, s, VDIM), F32),
                   jax.ShapeDtypeStruct((nb, HEADS, 1, s), F32)],
        scratch_shapes=[pltpu.VMEM((2, tq, s), F32), pltpu.VMEM((2, tq, s), BF16)],
        compiler_params=_cparams("parallel", "parallel"),
    )(q, k, v, km, vm)


def _shift_rows(a, prev):
    rid = lax.broadcasted_iota(jnp.int32, a.shape, 0)
    p1, p2 = prev[HALO - 1:HALO, :], prev[HALO - 2:HALO - 1, :]
    a1 = jnp.where(rid == 0, p1, pltpu.roll(a, 1, 0))
    a2 = jnp.where(rid == 0, p2, jnp.where(rid == 1, p1, pltpu.roll(a, 2, 0)))
    return a1, a2


def _attn_gate(o, za, ga_h):
    on, r = _rms(o, ga_h)
    return on * (za * _sigmoid(za)), on, r


def _out_fwd_bwd(x2d, tgt2d, o, p, pm, conv_w, ga, gc, gmat, w_out, gf, nb, s, tm):
    nt = s // tm
    r = nb * s
    prev_idx = lambda i: jnp.maximum(i * (tm // HALO) - 1, 0)

    def body(x_ref, t_ref, o_ref, za_ref, cb_ref, cc_ref, ch_ref, zc_ref, ccp_ref, chp_ref, mc_ref, mh_ref,
             cw_ref, ga_ref, gc_ref, gm_ref, w_ref, gf_ref,
             dh_ref, dy_ref, dw_ref, dgf_ref, loss_ref):
        i = pl.program_id(0)

        @pl.when(i == 0)
        def _():
            dw_ref[...] = jnp.zeros_like(dw_ref)
            dgf_ref[...] = jnp.zeros_like(dgf_ref)
            loss_ref[...] = jnp.zeros_like(loss_ref)

        ya = []
        for h in range(HEADS):
            y, _, _ = _attn_gate(o_ref[0, h], za_ref[:, VDIM * h:VDIM * (h + 1)].astype(F32),
                                 ga_ref[:, VDIM * h:VDIM * (h + 1)])
            ya.append(y)
        cc = _f32(cc_ref) * _f32(ch_ref)
        prev = jnp.where(i % nt == 0, _f32(mc_ref) * _f32(mh_ref), _f32(ccp_ref) * _f32(chp_ref))
        cc1, cc2 = _shift_rows(cc, prev)
        yc = _f32(cb_ref) * (cw_ref[0:1, :] * cc2 + cw_ref[1:2, :] * cc1 + cw_ref[2:3, :] * cc)
        rg = lax.rsqrt(_group_mean(yc * yc, gm_ref[...]) + EPS)
        zc = _f32(zc_ref)
        yconv = yc * rg * gc_ref[...] * (zc * _sigmoid(zc))
        ycat = jnp.concatenate(ya + [yconv], axis=1).astype(BF16)
        h2 = x_ref[...] + _dot(ycat, w_ref[...])
        gfv = gf_ref[...]
        y, r2 = _rms(h2, gfv)
        e = y - t_ref[...]
        loss_ref[...] += 0.5 * jnp.sum(e * e) / D_MODEL
        dyv = e * (1.0 / D_MODEL)
        dh2, dgf = _rms_bwd(dyv, h2, r2, gfv)
        dgf_ref[...] += jnp.sum(dgf, axis=0, keepdims=True)
        dh_ref[...] = dh2
        dhb = dh2.astype(BF16)
        dy_ref[...] = _dot_nt(dhb, w_ref[...])
        dw_ref[...] += _dot_tn(ycat, dhb)

    row = lambda w, j: pl.BlockSpec((tm, w), lambda i: (i, j))
    pblk = lambda j: pl.BlockSpec((tm, 512), lambda i: (i, j))
    pprev = lambda j: pl.BlockSpec((HALO, 512), lambda i: (prev_idx(i), j))
    mblk = lambda j: pl.BlockSpec((N_META, 512), lambda i: (0, j))
    full = lambda a: pl.BlockSpec(a.shape, lambda i: (0,) * a.ndim)
    return pl.pallas_call(
        body, name="out_fwd_bwd", grid=(nb * nt,),
        in_specs=[row(D_MODEL, 0), row(D_MODEL, 0),
                  pl.BlockSpec((1, HEADS, tm, VDIM), lambda i: (i // nt, 0, i % nt, 0)),
                  pblk(BLK_ZA), pblk(BLK_CB), pblk(BLK_CC), pblk(BLK_CH), pblk(BLK_ZC),
                  pprev(BLK_CC), pprev(BLK_CH), mblk(BLK_CC), mblk(BLK_CH),
                  full(conv_w), full(ga), full(gc), full(gmat), full(w_out), full(gf)],
        out_specs=[row(D_MODEL, 0), row(D_MODEL, 0),
                   pl.BlockSpec((D_MODEL, D_MODEL), lambda i: (0, 0)),
                   pl.BlockSpec((1, D_MODEL), lambda i: (0, 0)),
                   pl.BlockSpec((1, 128), lambda i: (0, 0))],
        out_shape=[jax.ShapeDtypeStruct((r, D_MODEL), F32), jax.ShapeDtypeStruct((r, D_MODEL), F32),
                   jax.ShapeDtypeStruct((D_MODEL, D_MODEL), F32), jax.ShapeDtypeStruct((1, D_MODEL), F32),
                   jax.ShapeDtypeStruct((1, 128), F32)],
        compiler_params=_cparams("arbitrary"),
    )(x2d, tgt2d, o, p, p, p, p, p, p, p, pm, pm, conv_w, ga, gc, gmat, w_out, gf)


def _gate_bwd(dycat, o, p, pm, conv_w, ga, gc, gmat, nb, s, tm):
    nt = s // tm
    r = nb * s
    ext = tm + HALO
    prev_idx = lambda i: jnp.maximum(i * (tm // HALO) - 1, 0)
    next_idx = lambda i: jnp.minimum((i + 1) * (tm // HALO), r // HALO - 1)

    def body(dya_ref, dyc_ref, dycn_ref, o_ref, za_ref, cb_ref, cbn_ref, cc_ref, ccp_ref, ccn_ref,
             ch_ref, chp_ref, chn_ref, zc_ref, zcn_ref, mc_ref, mh_ref, cw_ref, ga_ref, gc_ref, gm_ref,
             dpb_ref, do_ref, dl_ref, dccm_ref, dga_ref, dgc_ref, dcw_ref):
        i = pl.program_id(0)

        @pl.when(i == 0)
        def _():
            dga_ref[...] = jnp.zeros_like(dga_ref)
            dgc_ref[...] = jnp.zeros_like(dgc_ref)
            dcw_ref[...] = jnp.zeros_like(dcw_ref)

        dga = []
        for h in range(HEADS):
            hs = slice(VDIM * h, VDIM * (h + 1))
            oh, za, gah, dya = o_ref[0, h], za_ref[:, hs].astype(F32), ga_ref[:, hs], dya_ref[:, hs]
            sg = _sigmoid(za)
            on, ro = _rms(oh, gah)
            don = dya * (za * sg)
            dpb_ref[:, hs] = (dya * on * (sg * (1.0 + za * (1.0 - sg)))).astype(BF16)
            do, dg = _rms_bwd(don, oh, ro, gah)
            dga.append(jnp.sum(dg, axis=0, keepdims=True))
            dob = do.astype(BF16)
            do_ref[0, h] = dob
            dl_ref[0, h] = _row_of(jnp.sum(dob.astype(F32) * oh, axis=1, keepdims=True), tm)
        dga_ref[...] += jnp.concatenate(dga, axis=1)

        cat = lambda a, b: jnp.concatenate([_f32(a), _f32(b)], axis=0)
        cch = cat(cc_ref, ccn_ref)
        chh = cat(ch_ref, chn_ref)
        cb = cat(cb_ref, cbn_ref)
        zc = cat(zc_ref, zcn_ref)
        dy = cat(dyc_ref, dycn_ref)
        first = i % nt == 0
        last = i % nt == nt - 1
        cc = cch * chh
        prev = jnp.where(first, _f32(mc_ref) * _f32(mh_ref), _f32(ccp_ref) * _f32(chp_ref))
        cc1, cc2 = _shift_rows(cc, prev)
        w0, w1, w2 = cw_ref[0:1, :], cw_ref[1:2, :], cw_ref[2:3, :]
        dw = w0 * cc2 + w1 * cc1 + w2 * cc
        yc = cb * dw
        rg = lax.rsqrt(_group_mean(yc * yc, gm_ref[...]) + EPS)
        ych = yc * rg
        gcv = gc_ref[...]
        sg = _sigmoid(zc)
        dycn = dy * (zc * sg)
        dzc = dy * (ych * gcv) * (sg * (1.0 + zc * (1.0 - sg)))
        dgc_ref[...] += jnp.sum((dycn * ych)[:tm], axis=0, keepdims=True)
        dycg = dycn * gcv
        dyc = rg * (dycg - ych * _group_mean(dycg * ych, gm_ref[...]))
        rid = lax.broadcasted_iota(jnp.int32, (ext, CONV_W), 0)
        ddw = jnp.where(jnp.logical_and(last, rid >= tm), 0.0, dyc * cb)
        dcb = dyc * dw
        dcc = w2 * ddw + w1 * pltpu.roll(ddw, ext - 1, 0) + w0 * pltpu.roll(ddw, ext - 2, 0)
        dpb_ref[:, 512:1024] = dcb[:tm].astype(BF16)
        dpb_ref[:, 1024:1536] = (dcc * chh)[:tm].astype(BF16)
        dpb_ref[:, 1536:2048] = (dcc * cch)[:tm].astype(BF16)
        dpb_ref[:, 2048:2560] = dzc[:tm].astype(BF16)
        rs = lambda a: jnp.sum(a[:tm], axis=0, keepdims=True)
        dcw_ref[0:1, :] += rs(ddw * cc2)
        dcw_ref[1:2, :] += rs(ddw * cc1)
        dcw_ref[2:3, :] += rs(ddw * cc)

        @pl.when(first)
        def _():
            d0, d1 = ddw[0:1, :], ddw[1:2, :]
            r8 = lax.broadcasted_iota(jnp.int32, (8, CONV_W), 0)
            dccm_ref[0] = jnp.where(r8 == 7, w1 * d0 + w0 * d1, jnp.where(r8 == 6, w0 * d0, 0.0))

    row = lambda j: pl.BlockSpec((tm, 512), lambda i: (i, j))
    prv = lambda j: pl.BlockSpec((HALO, 512), lambda i: (prev_idx(i), j))
    nxt = lambda j: pl.BlockSpec((HALO, 512), lambda i: (next_idx(i), j))
    mblk = lambda j: pl.BlockSpec((N_META, 512), lambda i: (0, j))
    full = lambda a: pl.BlockSpec(a.shape, lambda i: (0,) * a.ndim)
    hb = lambda w: pl.BlockSpec((1, HEADS, tm, w), lambda i: (i // nt, 0, i % nt, 0))
    acc = lambda rr: pl.BlockSpec((rr, 512), lambda i: (0, 0))
    return pl.pallas_call(
        body, name="gate_bwd", grid=(nb * nt,),
        in_specs=[row(0), row(1), nxt(1), hb(VDIM),
                  row(BLK_ZA), row(BLK_CB), nxt(BLK_CB), row(BLK_CC), prv(BLK_CC), nxt(BLK_CC),
                  row(BLK_CH), prv(BLK_CH), nxt(BLK_CH), row(BLK_ZC), nxt(BLK_ZC),
                  mblk(BLK_CC), mblk(BLK_CH), full(conv_w), full(ga), full(gc), full(gmat)],
        out_specs=[pl.BlockSpec((tm, 2560), lambda i: (i, 0)), hb(VDIM),
                   pl.BlockSpec((1, HEADS, 1, tm), lambda i: (i // nt, 0, 0, i % nt)),
                   pl.BlockSpec((1, 8, 512), lambda i: (i // nt, 0, 0)),
                   acc(1), acc(1), acc(8)],
        out_shape=[jax.ShapeDtypeStruct((r, 2560), BF16), jax.ShapeDtypeStruct((nb, HEADS, s, VDIM), BF16),
                   jax.ShapeDtypeStruct((nb, HEADS, 1, s), F32), jax.ShapeDtypeStruct((nb, 8, 512), F32),
                   jax.ShapeDtypeStruct((1, 512), F32), jax.ShapeDtypeStruct((1, 512), F32),
                   jax.ShapeDtypeStruct((8, 512), F32)],
        compiler_params=_cparams("arbitrary"),
    )(dycat, dycat, dycat, o, p, p, p, p, p, p, p, p, p, p, p, pm, pm, conv_w, ga, gc, gmat)


class _StagedReduce:
    LOC, PRE_S, PRE_R, ICI_S, ICI_R, POST_S, POST_R, OUT, N_SEM = 0, 1, 2, 3, 6, 9, 10, 11, 12

    def __init__(self, shard_shape):
        self.half = (shard_shape[0] // 2, shard_shape[1])

    def scratch(self):
        h = self.half
        return [pltpu.VMEM((4,) + h, F32), pltpu.VMEM((4,) + h, F32), pltpu.VMEM((4,) + h, BF16),
                pltpu.VMEM((3,) + h, BF16), pltpu.VMEM(h, F32), pltpu.SemaphoreType.DMA((self.N_SEM,))]

    def run(self, stage, pin, gout, scr):
        own, sib, wire, rbuf, fin, sems = scr
        r2 = self.half[0]
        x, y, c = lax.axis_index("x"), lax.axis_index("y"), lax.axis_index("c")
        mine = 2 * x + y
        sibling = (x, y, 1 - c)
        chips = [(1 - x, y), (x, 1 - y), (1 - x, 1 - y)]
        rows = lambda half: pl.ds(pl.multiple_of(half * r2, r2), r2)
        mesh = pl.DeviceIdType.MESH

        loc = pltpu.make_async_copy(pin.at[:, rows(c), :], own, sems.at[self.LOC])
        pre = pltpu.make_async_remote_copy(
            src_ref=pin.at[:, rows(1 - c), :], dst_ref=sib, send_sem=sems.at[self.PRE_S],
            recv_sem=sems.at[self.PRE_R], device_id=sibling, device_id_type=mesh)

        def ici(j):
            px, py = chips[j]
            return pltpu.make_async_remote_copy(
                src_ref=wire.at[2 * px + py], dst_ref=rbuf.at[j], send_sem=sems.at[self.ICI_S + j],
                recv_sem=sems.at[self.ICI_R + j], device_id=(px, py, c), device_id_type=mesh)

        def post(half):
            return pltpu.make_async_remote_copy(
                src_ref=fin, dst_ref=gout.at[rows(half), :], send_sem=sems.at[self.POST_S],
                recv_sem=sems.at[self.POST_R], device_id=sibling, device_id_type=mesh)

        keep = pltpu.make_async_copy(fin, gout.at[rows(c), :], sems.at[self.OUT])
        if stage == 0:
            loc.start()
            pre.start()
        elif stage == 1:
            loc.wait()
            pre.wait_recv()
            for blk in range(4):
                tot = own[blk] + sib[blk]
                own[blk] = tot
                wire[blk] = tot.astype(BF16)
            for j in range(3):
                ici(j).start()
        elif stage == 2:
            for j in range(3):
                ici(j).wait_recv()
            tot = own[mine]
            for j in range(3):
                tot = tot + rbuf[j].astype(F32)
            fin[...] = tot
            post(c).start()
            keep.start()
        else:
            post(1 - c).wait_recv()
            pre.wait_send()
            for j in range(3):
                ici(j).wait_send()
            post(c).wait_send()
            keep.wait()


def _attn_bwd(q, k, v, do, lse, delta, km, vm, early, nb, s, t):
    n = s // t
    ne = len(early)
    reds = [_StagedReduce(a.shape[1:]) for a in early]
    n_steps = HEADS * nb
    assert n_steps >= 4

    def body(q_ref, k_ref, v_ref, do_ref, lse_ref, dl_ref, km_ref, vm_ref, *rest):
        pin_refs, rest = rest[:ne], rest[ne:]
        dq_ref, dk_ref, dv_ref, dkm_ref, dvm_ref = rest[:5]
        gout_refs, (p_scr, ds_scr, dq_acc), red_scr = rest[5:5 + ne], rest[5 + ne:8 + ne], rest[8 + ne:]
        b = pl.program_id(1)
        step = pl.program_id(0) * nb + b
        for stage, at in enumerate((0, 1, n_steps - 2, n_steps - 1)):
            @pl.when(step == at)
            def _(stage=stage):
                for a, red in enumerate(reds):
                    red.run(stage, pin_refs[a], gout_refs[a], red_scr[6 * a:6 * a + 6])

        @pl.when(b == 0)
        def _():
            dkm_ref[...] = jnp.zeros_like(dkm_ref)
            dvm_ref[...] = jnp.zeros_like(dvm_ref)

        kr = lax.broadcasted_iota(jnp.int32, (t, t), 0)
        qc = lax.broadcasted_iota(jnp.int32, (t, t), 1)
        km_v, vm_v = km_ref[0, 0], vm_ref[0, 0]
        ptm = jnp.exp(_dot_nt(km_v, q_ref[0, 0]) - lse_ref[0, 0])
        dstm = (ptm * (_dot_nt(vm_v, do_ref[0, 0]) - dl_ref[0, 0])).astype(BF16)
        dkm_ref[0] += _dot(dstm, q_ref[0, 0])
        dvm_ref[0] += _dot(ptm.astype(BF16), do_ref[0, 0])
        dq_acc[...] = _dot_tn(dstm, km_v)
        def tiles(j):
            slot = j % 2
            kj = k_ref[0, 0, j * t:(j + 1) * t, :]
            vj = v_ref[0, 0, j * t:(j + 1) * t, :]
            def products(i):
                cs = slice(i * t, (i + 1) * t)
                return _dot_nt(kj, q_ref[0, 0, cs, :]), _dot_nt(vj, do_ref[0, 0, cs, :])

            nxt, pending = products(j), None
            for i in range(j, n):
                cs = slice(i * t, (i + 1) * t)
                st, dpt = nxt
                if i + 1 < n:
                    nxt = products(i + 1)
                if i == j:
                    st = jnp.where(kr <= qc, st, NEG_INF)
                pt = jnp.exp(st - lse_ref[0, 0, :, cs])
                dst = (pt * (dpt - dl_ref[0, 0, :, cs])).astype(BF16)
                p_scr[slot, :, cs] = pt.astype(BF16)
                ds_scr[slot, :, cs] = dst
                if pending is not None:
                    dq_acc[pending[0], :] += _dot_tn(pending[1], kj)
                pending = (cs, dst)
            dq_acc[pending[0], :] += _dot_tn(pending[1], kj)

        for j in range(n):
            slot = j % 2
            tiles(j)
            dv_ref[0, 0, j * t:(j + 1) * t, :] = _dot(p_scr[slot, :, j * t:s], do_ref[0, 0, j * t:s, :]).astype(BF16)
            dk_ref[0, 0, j * t:(j + 1) * t, :] = _dot(ds_scr[slot, :, j * t:s], q_ref[0, 0, j * t:s, :]).astype(BF16)
        dq_ref[0, 0] = dq_acc[...].astype(BF16)

    big = lambda w: pl.BlockSpec((1, 1, s, w), lambda h, b: (b, h, 0, 0))
    rowv = pl.BlockSpec((1, 1, 1, s), lambda h, b: (b, h, 0, 0))
    mk = lambda w: pl.BlockSpec((1, 1, N_META, w), lambda h, b: (0, h, 0, 0))
    mo = lambda w: pl.BlockSpec((1, N_META, w), lambda h, b: (h, 0, 0))
    return pl.pallas_call(
        body, name="attn_bwd", grid=(HEADS, nb),
        in_specs=[big(QK_PAD), big(QK_PAD), big(VDIM), big(VDIM), rowv, rowv, mk(QK_PAD), mk(VDIM)]
        + [pl.BlockSpec(memory_space=pl.ANY)] * ne,
        out_specs=[big(QK_PAD), big(QK_PAD), big(VDIM), mo(QK_PAD), mo(VDIM)]
        + [pl.BlockSpec(memory_space=pl.ANY)] * ne,
        out_shape=[jax.ShapeDtypeStruct((nb, HEADS, s, QK_PAD), BF16),
                   jax.ShapeDtypeStruct((nb, HEADS, s, QK_PAD), BF16),
                   jax.ShapeDtypeStruct((nb, HEADS, s, VDIM), BF16),
                   jax.ShapeDtypeStruct((HEADS, N_META, QK_PAD), F32),
                   jax.ShapeDtypeStruct((HEADS, N_META, VDIM), F32)]
        + [jax.ShapeDtypeStruct(a.shape[1:], F32) for a in early],
        scratch_shapes=[pltpu.VMEM((2, t, s), BF16), pltpu.VMEM((2, t, s), BF16), pltpu.VMEM((s, QK_PAD), F32)]
        + [sc for red in reds for sc in red.scratch()],
        compiler_params=_cparams("arbitrary", "arbitrary"),
    )(q, k, v, do, lse, delta, km, vm, *early)


def _up_bwd(dq, dk, dv, dkm, dvm, p, pm, tabs, tabs_m, wq_p, wkv_p, gq, gkv, nb, s, tm):
    nt = s // tm
    n = nb * nt
    c_t, sa_t, sb_t = tabs
    cm_t, sam_t, sbm_t = tabs_m

    def kv_path(dkh, dvh, pa, c, sa, sb, wkv, gkvv):
        dkpe = dkh[0][:, NOPE:]
        for h in range(1, HEADS):
            dkpe = dkpe + dkh[h][:, NOPE:]
        dkr = _rope_bwd(dkpe, c, sa, sb)
        dkv = jnp.concatenate([d[:, :NOPE] for d in dkh] + list(dvh), axis=1).astype(BF16)
        ckv = pa[:, Q_RANK:Q_RANK + KV_RANK]
        kvn, rkv = _rms(ckv, gkvv)
        dckv, dg = _rms_bwd(_dot(dkv, wkv), ckv, rkv, gkvv)
        return dckv, dkr, kvn.astype(BF16), dkv, jnp.sum(dg, axis=0, keepdims=True)

    def body(dq_ref, dk_ref, dv_ref, pa_ref, c_ref, sa_ref, sb_ref,
             dkm_ref, dvm_ref, pam_ref, cm_ref, sam_ref, sbm_ref,
             wq_ref, wkv_ref, gq_ref, gkv_ref,
             dpa_ref, dpam_ref, pq_ref, pkv_ref, dgq_ref, dgkv_ref, dwq_ref, dwkv_ref):
        i = pl.program_id(0)

        @pl.when(i == 0)
        def _():
            dwq_ref[...] = jnp.zeros_like(dwq_ref)
            dwkv_ref[...] = jnp.zeros_like(dwkv_ref)
            dgq_ref[...] = jnp.zeros_like(dgq_ref)
            dgkv_ref[...] = jnp.zeros_like(dgkv_ref)

        @pl.when(i < n)
        def _():
            c, sa, sb = c_ref[...], sa_ref[...], sb_ref[...]
            pa = _f32(pa_ref)
            parts = []
            for h in range(HEADS):
                dqh = dq_ref[0, h].astype(F32) * ATTN_SCALE
                parts += [dqh[:, :NOPE], _rope_bwd(dqh[:, NOPE:], c, sa, sb)]
            dql = jnp.concatenate(parts, axis=1).astype(BF16)
            cq = pa[:, 0:Q_RANK]
            gqv = gq_ref[...]
            qn, rq = _rms(cq, gqv)
            dwq_ref[...] += _dot_tn(dql, qn.astype(BF16))
            dcq, dg = _rms_bwd(_dot(dql, wq_ref[...]), cq, rq, gqv)
            dgq_ref[...] += jnp.sum(dg, axis=0, keepdims=True)
            dckv, dkr, kvn, dkv, dgk = kv_path([dk_ref[0, h].astype(F32) for h in range(HEADS)],
                                               [dv_ref[0, h].astype(F32) for h in range(HEADS)],
                                               pa, c, sa, sb, wkv_ref[...], gkv_ref[...])
            dwkv_ref[...] += _dot_tn(dkv, kvn)
            dgkv_ref[...] += dgk
            dpa_ref[...] = jnp.concatenate([dcq, dckv, dkr], axis=1).astype(BF16)

        @pl.when(i == n)
        def _():
            dckv, dkr, kvn, dkv, dgk = kv_path([dkm_ref[h] for h in range(HEADS)],
                                               [dvm_ref[h] for h in range(HEADS)],
                                               _f32(pam_ref), cm_ref[...], sam_ref[...], sbm_ref[...],
                                               wkv_ref[...], gkv_ref[...])
            dwkv_ref[...] += _dot_tn(dkv, kvn)
            dgkv_ref[...] += dgk
            dpam_ref[...] = jnp.concatenate([jnp.zeros((N_META, Q_RANK), F32), dckv, dkr], axis=1)
            for h in range(HEADS):
                pq_ref[h] = dwq_ref[QK_PAD * h:QK_PAD * h + NOPE + ROPE, :]
                pkv_ref[h, 0:NOPE, :] = dwkv_ref[NOPE * h:NOPE * (h + 1), :]
                pkv_ref[h, NOPE:NOPE + VDIM, :] = dwkv_ref[512 + VDIM * h:512 + VDIM * (h + 1), :]

    cl = lambda i: jnp.minimum(i, n - 1)
    hb = lambda w: pl.BlockSpec((1, HEADS, tm, w), lambda i: (cl(i) // nt, 0, cl(i) % nt, 0))
    tab = pl.BlockSpec((tm, 128), lambda i: (cl(i) % nt, 0))
    full = lambda a: pl.BlockSpec(a.shape, lambda i: (0,) * a.ndim)
    const = lambda shape: pl.BlockSpec(shape, lambda i: (0,) * len(shape))
    return pl.pallas_call(
        body, name="up_bwd", grid=(n + 1,),
        in_specs=[hb(QK_PAD), hb(QK_PAD), hb(VDIM), pl.BlockSpec((tm, 512), lambda i: (cl(i), 0)), tab, tab, tab,
                  full(dkm), full(dvm), pl.BlockSpec((N_META, 512), lambda i: (0, 0)),
                  full(cm_t), full(sam_t), full(sbm_t), full(wq_p), full(wkv_p), full(gq), full(gkv)],
        out_specs=[pl.BlockSpec((tm, 512), lambda i: (cl(i), 0)), const((N_META, 512)),
                   const((HEADS, NOPE + ROPE, Q_RANK)), const((HEADS, NOPE + VDIM, KV_RANK)),
                   const((1, Q_RANK)), const((1, KV_RANK))],
        out_shape=[jax.ShapeDtypeStruct((nb * s, 512), BF16), jax.ShapeDtypeStruct((N_META, 512), F32),
                   jax.ShapeDtypeStruct((HEADS, NOPE + ROPE, Q_RANK), F32),
                   jax.ShapeDtypeStruct((HEADS, NOPE + VDIM, KV_RANK), F32),
                   jax.ShapeDtypeStruct((1, Q_RANK), F32), jax.ShapeDtypeStruct((1, KV_RANK), F32)],
        scratch_shapes=[pltpu.VMEM((HEADS * QK_PAD, Q_RANK), F32), pltpu.VMEM((1024, KV_RANK), F32)],
        compiler_params=_cparams("arbitrary"),
    )(dq, dk, dv, p, c_t, sa_t, sb_t, dkm, dvm, pm, cm_t, sam_t, sbm_t, wq_p, wkv_p, gq, gkv)


def _in_bwd(x2d, dh2, dpa, dpb, meta, dpam, dccm, pm, w_in_p, norm_g, nb, s, tm):
    nt = s // tm
    n = nb * nt

    def body(x_ref, dh_ref, dpa_ref, dpb_ref, mt_ref, dpam_ref, dccm_ref, mc_ref, mh_ref, w_ref, g_ref,
             gx_ref, gm_ref, dw_hbm, dg_ref, acc_ref, sems):
        i = pl.program_id(0)

        @pl.when(i == 0)
        def _():
            acc_ref[...] = jnp.zeros_like(acc_ref)
            dg_ref[...] = jnp.zeros_like(dg_ref)

        def rows(x, dp, dres):
            g = g_ref[...]
            dpb16 = dp.astype(BF16)
            du = _dot(dpb16, w_ref[...])
            u, r1 = _rms(x, g)
            acc_ref[...] += _dot_tn(dpb16, u.astype(BF16))
            dx, dg = _rms_bwd(du, x, r1, g)
            dg_ref[...] += jnp.sum(dg, axis=0, keepdims=True)
            return dx if dres is None else dx + dres

        @pl.when(i < n)
        def _():
            dp = jnp.concatenate([dpa_ref[...], dpb_ref[...]], axis=1)
            gx_ref[...] = rows(x_ref[...], dp, dh_ref[...])

        @pl.when(i == n)
        def _():
            dcc = dccm_ref[0]
            for b in range(1, nb):
                dcc = dcc + dccm_ref[b]
            z8 = jnp.zeros((8, CONV_W), F32)
            dc = jnp.concatenate([z8, dcc * _f32(mh_ref)[8:16, :]], axis=0)
            dh = jnp.concatenate([z8, dcc * _f32(mc_ref)[8:16, :]], axis=0)
            z = jnp.zeros((N_META, CONV_W), F32)
            dp = jnp.concatenate([dpam_ref[...], z, z, dc, dh, z], axis=1)
            gm_ref[...] = rows(mt_ref[...], dp, None)
            per = IN_DIM // 4
            cps = [pltpu.make_async_copy(acc_ref.at[0:448], dw_hbm.at[0, 0:448], sems.at[0]),
                   pltpu.make_async_copy(acc_ref.at[512:per + 64], dw_hbm.at[0, 448:per], sems.at[1])]
            for qq in range(1, 4):
                cps.append(pltpu.make_async_copy(acc_ref.at[per * qq + 64:per * (qq + 1) + 64], dw_hbm.at[qq],
                                                 sems.at[qq + 1]))
            for cp in cps:
                cp.start()
            for cp in cps:
                cp.wait()

    cl = lambda i: jnp.minimum(i, n - 1)
    row = lambda w: pl.BlockSpec((tm, w), lambda i: (cl(i), 0))
    full = lambda a: pl.BlockSpec(a.shape, lambda i: (0,) * a.ndim)
    mblk = lambda j: pl.BlockSpec((N_META, 512), lambda i: (0, j))
    return pl.pallas_call(
        body, name="in_bwd", grid=(n + 1,),
        in_specs=[row(D_MODEL), row(D_MODEL), row(512), row(2560), full(meta), full(dpam), full(dccm),
                  mblk(BLK_CC), mblk(BLK_CH), full(w_in_p), full(norm_g)],
        out_specs=[row(D_MODEL), pl.BlockSpec((N_META, D_MODEL), lambda i: (0, 0)),
                   pl.BlockSpec(memory_space=pl.ANY), pl.BlockSpec((1, D_MODEL), lambda i: (0, 0))],
        out_shape=[jax.ShapeDtypeStruct((nb * s, D_MODEL), F32), jax.ShapeDtypeStruct((N_META, D_MODEL), F32),
                   jax.ShapeDtypeStruct((4, IN_DIM // 4, D_MODEL), F32), jax.ShapeDtypeStruct((1, D_MODEL), F32)],
        scratch_shapes=[pltpu.VMEM((IN_PAD, D_MODEL), F32), pltpu.SemaphoreType.DMA((5,))],
        compiler_params=_cparams("arbitrary"),
    )(x2d, dh2, dpa, dpb, meta, dpam, dccm, pm, pm, w_in_p, norm_g)


def _gather_weights(split, pieces, out_rows, whole, zero_fills):
    ns, nw, nz = len(split), len(whole), len(zero_fills)
    flat = [(a, pc) for a in range(ns) for pc in pieces[a]]
    nk = len(flat)

    def body(*refs):
        ins, wins, zins = refs[:ns], refs[ns:ns + nw], refs[ns + nw:ns + nw + nz]
        outs, wouts = refs[ns + nw + nz:2 * ns + nw + nz], refs[2 * ns + nw + nz:2 * (ns + nw) + nz]
        send_sems, recv_sems, fwd_send, fwd_recv, loc_sems, w_send, w_recv, w_loc, z_sems = refs[2 * (ns + nw) + nz:]
        x, y, c = lax.axis_index("x"), lax.axis_index("y"), lax.axis_index("c")
        mine = 2 * x + y
        chips = [(1 - x, y), (x, 1 - y), (1 - x, 1 - y)]
        chip_of = [2 * px + py for px, py in chips]

        def src(k):
            a, (s0, nr, _, _, _, _) = flat[k]
            return ins[a].at[s0:s0 + nr]

        def dst(k, q):
            a, (_, nr, per, first, rest, _) = flat[k]
            row = per * q + first + (rest - first) * jnp.minimum(q, 1)
            return outs[a].at[pl.ds(pl.multiple_of(row, 16), nr)]

        def ici(k, j, q):
            px, py = chips[j]
            return pltpu.make_async_remote_copy(
                src_ref=src(k), dst_ref=dst(k, q), send_sem=send_sems.at[k, j], recv_sem=recv_sems.at[k, j],
                device_id=(px, py, c), device_id_type=pl.DeviceIdType.MESH)

        def fwd(k, j):
            ref = dst(k, chip_of[j])
            return pltpu.make_async_remote_copy(
                src_ref=ref, dst_ref=ref, send_sem=fwd_send.at[k, j], recv_sem=fwd_recv.at[k, j],
                device_id=(x, y, 1 - c), device_id_type=pl.DeviceIdType.MESH)

        def wcopy(b, j, q):
            px, py = chips[j]
            return pltpu.make_async_remote_copy(
                src_ref=wins[b], dst_ref=wouts[b].at[q], send_sem=w_send.at[b, j], recv_sem=w_recv.at[b, j],
                device_id=(px, py, c), device_id_type=pl.DeviceIdType.MESH)

        local = [pltpu.make_async_copy(src(k), dst(k, mine), loc_sems.at[k]) for k in range(nk)]
        local += [pltpu.make_async_copy(wins[b], wouts[b].at[mine], w_loc.at[b]) for b in range(nw)]
        for z, (a, _, row0) in enumerate(zero_fills):
            local.append(pltpu.make_async_copy(zins[z], outs[a].at[row0:row0 + zins[z].shape[0]], z_sems.at[z]))
        wsends = [wcopy(b, j, mine) for b in range(nw) for j in range(3)]
        for cp in local + wsends:
            cp.start()

        for half in (0, 1):
            @pl.when(c == half)
            def _(half=half):
                my_k = [k for k in range(nk) if flat[k][1][5] == half]
                other_k = [k for k in range(nk) if flat[k][1][5] != half]
                sends = [ici(k, j, mine) for k in my_k for j in range(3)]
                for cp in sends:
                    cp.start()
                passed = []
                for k in my_k:
                    for j in range(3):
                        ici(k, j, chip_of[j]).wait_recv()
                        cp = fwd(k, j)
                        cp.start()
                        passed.append(cp)
                for k in other_k:
                    for j in range(3):
                        fwd(k, j).wait_recv()
                for cp in sends + passed:
                    cp.wait_send()

        for b in range(nw):
            for j in range(3):
                wcopy(b, j, chip_of[j]).wait_recv()
        for cp in wsends:
            cp.wait_send()
        for cp in local:
            cp.wait()

    hbm = pl.BlockSpec(memory_space=pl.ANY)
    dma = pltpu.SemaphoreType.DMA
    zeros = [z for _, z, _ in zero_fills]
    return pl.pallas_call(
        body, name="gather_weights",
        in_specs=[hbm] * (ns + nw + nz), out_specs=[hbm] * (ns + nw),
        out_shape=([jax.ShapeDtypeStruct((out_rows[a], split[a].shape[1]), split[a].dtype) for a in range(ns)]
                   + [jax.ShapeDtypeStruct((4,) + w.shape, w.dtype) for w in whole]),
        scratch_shapes=[dma((nk, 3)), dma((nk, 3)), dma((nk, 3)), dma((nk, 3)), dma((nk,)),
                        dma((nw, 3)), dma((nw, 3)), dma((nw,)), dma((nz,))],
        compiler_params=pltpu.CompilerParams(vmem_limit_bytes=VMEM_LIMIT),
    )(*split, *whole, *zeros)


def _reduce_grads(parts, small):
    n = len(parts)
    shapes = [a.shape[1:] for a in parts]
    halves = [(sh[0] // 2, sh[1]) for sh in shapes]

    def body(*refs):
        pin, sm_in = refs[:n], refs[n]
        gout, sm_out = refs[n + 1:2 * n + 1], refs[2 * n + 1]
        scr = refs[2 * n + 2:]
        own, sib, wire, rbuf = scr[:n], scr[n:2 * n], scr[2 * n:3 * n], scr[3 * n:4 * n]
        (sbuf, send_sems, recv_sems, loc_sems, pre_send, pre_recv, post_send, post_recv,
         sm_send, sm_recv) = scr[4 * n:]
        x, y, c = lax.axis_index("x"), lax.axis_index("y"), lax.axis_index("c")
        mine = 2 * x + y
        me = 4 * x + 2 * y + c
        sibling = (x, y, 1 - c)
        chips = [(1 - x, y), (x, 1 - y), (1 - x, 1 - y)]

        def rows(a, half):
            r2 = halves[a][0]
            return pl.ds(pl.multiple_of(half * r2, r2), r2)

        def pre(a):
            return pltpu.make_async_remote_copy(
                src_ref=pin[a].at[:, rows(a, 1 - c), :], dst_ref=sib[a], send_sem=pre_send.at[a],
                recv_sem=pre_recv.at[a], device_id=sibling, device_id_type=pl.DeviceIdType.MESH)

        def ici(a, j):
            px, py = chips[j]
            return pltpu.make_async_remote_copy(
                src_ref=wire[a].at[2 * px + py], dst_ref=rbuf[a].at[j], send_sem=send_sems.at[a, j],
                recv_sem=recv_sems.at[a, j], device_id=(px, py, c), device_id_type=pl.DeviceIdType.MESH)

        def post(a, half):
            ref = gout[a].at[rows(a, half), :]
            return pltpu.make_async_remote_copy(
                src_ref=ref, dst_ref=ref, send_sem=post_send.at[a], recv_sem=post_recv.at[a],
                device_id=sibling, device_id_type=pl.DeviceIdType.MESH)

        def small_copy(kk):
            peer = (x ^ (kk >> 2), y ^ ((kk >> 1) & 1), c ^ (kk & 1))
            return pltpu.make_async_remote_copy(
                src_ref=sm_in, dst_ref=sbuf.at[kk], send_sem=sm_send.at[kk - 1], recv_sem=sm_recv.at[kk - 1],
                device_id=peer, device_id_type=pl.DeviceIdType.MESH)

        local = [pltpu.make_async_copy(pin[a].at[:, rows(a, c), :], own[a], loc_sems.at[a]) for a in range(n)]
        pres = [pre(a) for a in range(n)]
        smalls = [small_copy(kk) for kk in range(1, 8)]
        for cp in local + pres + smalls:
            cp.start()
        sbuf[0] = sm_in[...]
        sends = []
        for a in range(n):
            local[a].wait()
            pres[a].wait_recv()
            for blk in range(4):
                tot = own[a][blk] + sib[a][blk]
                own[a][blk] = tot
                wire[a][blk] = tot.astype(BF16)
            for j in range(3):
                cp = ici(a, j)
                cp.start()
                sends.append(cp)
        for cp in smalls:
            cp.wait_recv()
        total = sbuf[me]
        for d in range(1, 8):
            total = total + sbuf[me ^ d]
        sm_out[...] = total
        posts = []
        for a in range(n):
            for j in range(3):
                ici(a, j).wait_recv()
            fin = own[a][mine]
            for j in range(3):
                fin = fin + rbuf[a][j].astype(F32)
            gout[a][rows(a, c), :] = fin
            cp = post(a, c)
            cp.start()
            posts.append(cp)
        for a in range(n):
            post(a, 1 - c).wait_recv()
        for cp in pres + sends + smalls + posts:
            cp.wait_send()

    hbm = pl.BlockSpec(memory_space=pl.ANY)
    vmem = pl.BlockSpec(memory_space=pltpu.VMEM)
    dma = pltpu.SemaphoreType.DMA
    return pl.pallas_call(
        body, name="reduce_grads",
        in_specs=[hbm] * n + [vmem], out_specs=[vmem] * (n + 1),
        out_shape=[jax.ShapeDtypeStruct(sh, F32) for sh in shapes] + [jax.ShapeDtypeStruct(small.shape, F32)],
        scratch_shapes=([pltpu.VMEM((4,) + hs, F32) for hs in halves] + [pltpu.VMEM((4,) + hs, F32) for hs in halves]
                        + [pltpu.VMEM((4,) + hs, BF16) for hs in halves]
                        + [pltpu.VMEM((3,) + hs, BF16) for hs in halves]
                        + [pltpu.VMEM((8,) + small.shape, F32), dma((n, 3)), dma((n, 3)), dma((n,)),
                           dma((n,)), dma((n,)), dma((n,)), dma((n,)), dma((7,)), dma((7,))]),
        compiler_params=pltpu.CompilerParams(vmem_limit_bytes=VMEM_LIMIT),
    )(*parts, small)


def _adamw(w, g, m, v, name):
    shape = w.shape
    w2, g2, m2, v2 = (a.reshape((-1, shape[-1])) for a in (w, g, m, v))

    def body(w_ref, g_ref, m_ref, v_ref, d_ref, nm_ref, nv_ref):
        gv = g_ref[...]
        nm = ADAM_B1 * m_ref[...] + (1.0 - ADAM_B1) * gv
        nv = ADAM_B2 * v_ref[...] + (1.0 - ADAM_B2) * (gv * gv)
        m_hat = nm / (1.0 - ADAM_B1 ** ADAM_STEP)
        v_hat = nv / (1.0 - ADAM_B2 ** ADAM_STEP)
        d_ref[...] = -ADAM_LR * (m_hat / (jnp.sqrt(v_hat) + ADAM_EPS) + ADAM_WD * w_ref[...])
        nm_ref[...] = nm
        nv_ref[...] = nv

    rows, cols = w2.shape
    nblk = cols // 256 if cols % 256 == 0 and rows >= 64 else 1
    blk = pl.BlockSpec((rows, cols // nblk), lambda j: (0, j))
    out = pl.pallas_call(
        body, name=name, grid=(nblk,), in_specs=[blk] * 4, out_specs=[blk] * 3,
        out_shape=[jax.ShapeDtypeStruct(w2.shape, F32)] * 3,
        compiler_params=_cparams("parallel"),
    )(w2, g2, m2, v2)
    return tuple(a.reshape(shape) for a in out)


def kernel(x, meta_tokens, norm_g, w_in, q_norm_g, w_q_up, kv_norm_g, w_kv_up, conv_w, attn_out_g, conv_out_g, w_out, final_norm_g, loss_target, m_meta_tokens, m_norm_g, m_w_in, m_q_norm_g, m_w_q_up, m_kv_norm_g, m_w_kv_up, m_conv_w, m_attn_out_g, m_conv_out_g, m_w_out, m_final_norm_g, v_meta_tokens, v_norm_g, v_w_in, v_q_norm_g, v_w_q_up, v_kv_norm_g, v_w_kv_up, v_conv_w, v_attn_out_g, v_conv_out_g, v_w_out, v_final_norm_g):
    nb, s, _ = x.shape
    tm = min(ROW_TILE, s)
    ta = min(ATTN_TILE, s)
    assert s % tm == 0 and s % ta == 0 and tm % 16 == 0
    r = nb * s

    tr = lambda a: jnp.transpose(a[0])
    w_in_p, wq_p, wkv_p, g_cw, g_meta = _gather_weights(
        [tr(w_in).astype(BF16), tr(w_q_up).astype(BF16), tr(w_kv_up).astype(BF16)],
        [W_IN_PIECES, W_Q_PIECES, W_KV_PIECES], [IN_PAD, HEADS * QK_PAD, 1024],
        [conv_w[0], meta_tokens],
        [(0, jnp.zeros((64, D_MODEL), BF16), 448)]
        + [(1, jnp.zeros((64, Q_RANK), BF16), QK_PAD * h + NOPE + ROPE) for h in range(HEADS)])
    conv_f = jnp.transpose(g_cw, (1, 0, 2)).reshape(3, CONV_W)
    meta_f = jnp.transpose(g_meta, (1, 0, 2)).reshape(N_META, D_MODEL)

    c_all, sa_all, sb_all = _rope_tables(N_META + s)
    tabs_m = (c_all[:N_META], sa_all[:N_META], sb_all[:N_META])
    tabs = (c_all[N_META:], sa_all[N_META:], sb_all[N_META:])
    gid = np.arange(CONV_W) // CONV_GROUP
    gmat = jnp.asarray(np.where(gid[:, None] == gid[None, :], 1.0 / CONV_GROUP, 0.0), BF16)
    ga, gc = attn_out_g, conv_out_g
    gf = final_norm_g.reshape(1, D_MODEL)

    x2d = x.reshape(r, D_MODEL)
    tgt2d = loss_target.reshape(r, D_MODEL)

    p, q, k, v, pm, km, vm, w_out_f = _fwd_proj(x2d, meta_f, tabs, tabs_m, norm_g, w_in_p, q_norm_g, wq_p,
                                                kv_norm_g, wkv_p, w_out[0].astype(BF16), nb, s, tm)
    o, lse = _attn_fwd(q, k, v, km, vm, nb, s, ta)
    dh2, dycat, dw_out, dgf, loss_acc = _out_fwd_bwd(x2d, tgt2d, o, p, pm, conv_f, ga, gc, gmat, w_out_f, gf,
                                                     nb, s, tm)
    dpb, do, delta, dccm, dga, dgc, dcw = _gate_bwd(dycat, o, p, pm, conv_f, ga, gc, gmat, nb, s, tm)
    p_out = dw_out.reshape(4, D_MODEL // 4, D_MODEL)
    dq, dk, dv, dkm, dvm, g_w_out = _attn_bwd(q, k, v, do, lse, delta, km, vm, [p_out], nb, s, ta)
    dpa, dpam, p_q, p_kv, dgq, dgkv = _up_bwd(dq, dk, dv, dkm, dvm, p, pm, tabs, tabs_m, wq_p, wkv_p,
                                              q_norm_g, kv_norm_g, nb, s, tm)
    gx, gmeta, p_in, dng = _in_bwd(x2d, dh2, dpa, dpb, meta_f, dpam, dccm, pm, w_in_p, norm_g, nb, s, tm)

    flat =jnp.concatenate([dng.reshape(-1), dgq.reshape(-1), dgkv.reshape(-1), dga.reshape(-1), dgc.reshape(-1),
                            dgf.reshape(-1), dcw[:3].reshape(-1), gmeta.reshape(-1), loss_acc[0, 0:1]])
    n_small = flat.shape[0]
    rows_small = -(-n_small // 1024) * 8
    small = jnp.pad(flat, (0, rows_small * 128 - n_small)).reshape(rows_small, 128)
    g_w_in_t, g_w_q_t, g_w_kv_t, small_sum = _reduce_grads([p_in, p_q, p_kv], small)
    ssum = small_sum.reshape(-1)

    def take(off, n):
        return ssum[off:off + n], off + n

    off = 0
    g_norm, off = take(off, D_MODEL)
    g_qn, off = take(off, Q_RANK)
    g_kvn, off = take(off, KV_RANK)
    g_ga, off = take(off, CONV_W)
    g_gc, off = take(off, CONV_W)
    g_gf, off = take(off, D_MODEL)
    g_cw_all, off = take(off, 3 * CONV_W)
    g_meta_all, off = take(off, N_META * D_MODEL)
    loss = ssum[off]
    chip = 2 * lax.axis_index("x") + lax.axis_index("y")
    g_conv = lax.dynamic_slice(g_cw_all.reshape(3, CONV_W), (0, chip * 128), (3, 128))
    g_mt = lax.dynamic_slice(g_meta_all.reshape(N_META, D_MODEL), (0, chip * 256), (N_META, 256))

    grads = {
        "meta_tokens": g_mt, "norm_g": g_norm.reshape(1, -1), "w_in": g_w_in_t, "q_norm_g": g_qn.reshape(1, -1),
        "w_q_up": g_w_q_t, "kv_norm_g": g_kvn.reshape(1, -1), "w_kv_up": jnp.transpose(g_w_kv_t)[None],
        "conv_w": g_conv[None], "attn_out_g": g_ga.reshape(1, -1), "conv_out_g": g_gc.reshape(1, -1),
        "w_out": g_w_out[None], "final_norm_g": g_gf,
    }
    transposed = ("w_in", "w_q_up")
    weights = {
        "meta_tokens": (meta_tokens, m_meta_tokens, v_meta_tokens), "norm_g": (norm_g, m_norm_g, v_norm_g),
        "w_in": (w_in, m_w_in, v_w_in), "q_norm_g": (q_norm_g, m_q_norm_g, v_q_norm_g),
        "w_q_up": (w_q_up, m_w_q_up, v_w_q_up), "kv_norm_g": (kv_norm_g, m_kv_norm_g, v_kv_norm_g),
        "w_kv_up": (w_kv_up, m_w_kv_up, v_w_kv_up), "conv_w": (conv_w, m_conv_w, v_conv_w),
        "attn_out_g": (attn_out_g, m_attn_out_g, v_attn_out_g), "conv_out_g": (conv_out_g, m_conv_out_g, v_conv_out_g),
        "w_out": (w_out, m_w_out, v_w_out), "final_norm_g": (final_norm_g, m_final_norm_g, v_final_norm_g),
    }
    names = list(weights)
    deltas, new_m, new_v = [], [], []
    for nme in names:
        w_, m_, v_ = weights[nme]
        if nme in transposed:
            res = _adamw(tr(w_), grads[nme], tr(m_), tr(v_), "adamw_" + nme)
            g_, d_, nm_, nv_ = (jnp.transpose(a)[None] for a in (grads[nme],) + res)
        else:
            g_ = grads[nme].reshape(w_.shape)
            d_, nm_, nv_ = _adamw(w_, g_, m_, v_, "adamw_" + nme)
        grads[nme] = g_
        deltas.append(d_)
        new_m.append(nm_)
        new_v.append(nv_)

    grad_x = gx.reshape(nb, s, D_MODEL)
    return (loss, grad_x, *[grads[nme] for nme in names], *deltas, *new_m, *new_v)
```

```python
import functools

import jax
import jax.numpy as jnp
import numpy as np
from jax import lax
from jax.experimental import pallas as pl
from jax.experimental.pallas import tpu as pltpu

F32 = jnp.float32
BF16 = jnp.bfloat16

D_MODEL = 1024
N_META = 16
HEADS = 4
NOPE = 128
ROPE = 64
VDIM = 128
QK_PAD = 256
Q_RANK = 256
KV_RANK = 128
CONV_W = 512
CONV_GROUP = 64
ROPE_THETA = 10000.0
EPS = 1e-6
ATTN_SCALE = (NOPE + ROPE) ** -0.5
IN_DIM = 3008
IN_PAD = 3072
BLK_ZA, BLK_CB, BLK_CC, BLK_CH, BLK_ZC = 1, 2, 3, 4, 5
NEG_INF = -1e30

ADAM_LR = 0.001
ADAM_B1 = 0.9
ADAM_B2 = 0.999
ADAM_EPS = 1e-08
ADAM_WD = 0.01
ADAM_STEP = 10

ROW_TILE = 512
ATTN_TILE = 256
VMEM_LIMIT = 56 * 1024 * 1024

NT = (((1,), (1,)), ((), ()))
TN = (((0,), (0,)), ((), ()))


def _cparams(*sem):
    return pltpu.CompilerParams(dimension_semantics=sem, vmem_limit_bytes=VMEM_LIMIT)


def _dot(a, b):
    return jnp.dot(a, b, preferred_element_type=F32)


def _dot_nt(a, b):
    return lax.dot_general(a, b, NT, preferred_element_type=F32)


def _dot_tn(a, b):
    return lax.dot_general(a, b, TN, preferred_element_type=F32)


def _rms(x, g):
    r = lax.rsqrt(jnp.mean(x * x, axis=-1, keepdims=True) + EPS)
    return x * r * g, r


def _rms_bwd(dy, x, r, g):
    xh = x * r
    dyg = dy * g
    dx = r * (dyg - xh * jnp.mean(dyg * xh, axis=-1, keepdims=True))
    return dx, dy * xh


def _sigmoid(z):
    return 1.0 / (1.0 + jnp.exp(-z))


def _rope(b, c, sa, sb):
    return b * c + pltpu.roll(b, 96, 1) * sa + pltpu.roll(b, 32, 1) * sb


def _rope_bwd(d, c, sa, sb):
    return d * c + pltpu.roll(d * sa, 32, 1) + pltpu.roll(d * sb, 96, 1)


def _group_mean(x, gmat):
    hi = x.astype(BF16)
    lo = (x - hi.astype(F32)).astype(BF16)
    return _dot(hi, gmat) + _dot(lo, gmat)


def _row_of(col, rows):
    return jnp.transpose(jnp.broadcast_to(col, (rows, 128)))[0:1, :]


def _rope_tables(n_pos):
    half = ROPE // 2
    inv_freq = (np.float32(1.0) / (np.float32(ROPE_THETA) ** (np.arange(half, dtype=np.float32) / np.float32(half))))
    ang = np.arange(n_pos, dtype=np.float32)[:, None] * inv_freq.astype(np.float32)[None, :]
    cos, sin = np.cos(ang).astype(np.float32), np.sin(ang).astype(np.float32)
    z = np.zeros((n_pos, half), np.float32)
    c = np.concatenate([cos, cos, z, z], axis=1)
    sa = np.concatenate([-sin, z, z, z], axis=1)
    sb = np.concatenate([z, sin, z, z], axis=1)
    return jnp.asarray(c), jnp.asarray(sa), jnp.asarray(sb)


W_IN_PIECES = ((0, 384, 752, 0, 64, 0), (384, 64, 752, 384, 448, 1), (448, 304, 752, 512, 512, 1))
W_Q_PIECES = ((0, 96, 256, 0, 0, 0), (96, 96, 256, 96, 96, 1))
W_KV_PIECES = ((0, 128, 128, 0, 0, 0), (128, 128, 128, 512, 512, 1))
W_OUT_PIECES = ((0, 128, 256, 0, 0, 0), (128, 128, 256, 128, 128, 1))


class _StagedGather:
    def __init__(self, pieces):
        self.pieces = pieces

    def scratch(self):
        nk, dma = len(self.pieces), pltpu.SemaphoreType.DMA
        return [dma((nk, 3)), dma((nk, 3)), dma((nk, 3)), dma((nk, 3)), dma((nk,))]

    def run(self, stage, src_ref, out_ref, scr):
        send_sems, recv_sems, fwd_send, fwd_recv, loc_sems = scr
        pieces = self.pieces
        nk = len(pieces)
        x, y, c = lax.axis_index("x"), lax.axis_index("y"), lax.axis_index("c")
        mine = 2 * x + y
        chips = [(1 - x, y), (x, 1 - y), (1 - x, 1 - y)]
        chip_of = [2 * px + py for px, py in chips]
        mesh = pl.DeviceIdType.MESH

        def src(k):
            s0, nr = pieces[k][0], pieces[k][1]
            return src_ref.at[s0:s0 + nr]

        def dst(k, q):
            _, nr, per, first, rest, _ = pieces[k]
            row = per * q + first + (rest - first) * jnp.minimum(q, 1)
            return out_ref.at[pl.ds(pl.multiple_of(row, 16), nr)]

        def ici(k, j, q):
            px, py = chips[j]
            return pltpu.make_async_remote_copy(
                src_ref=src(k), dst_ref=dst(k, q), send_sem=send_sems.at[k, j], recv_sem=recv_sems.at[k, j],
                device_id=(px, py, c), device_id_type=mesh)

        def fwd(k, j):
            ref = dst(k, chip_of[j])
            return pltpu.make_async_remote_copy(
                src_ref=ref, dst_ref=ref, send_sem=fwd_send.at[k, j], recv_sem=fwd_recv.at[k, j],
                device_id=(x, y, 1 - c), device_id_type=mesh)

        local = [pltpu.make_async_copy(src(k), dst(k, mine), loc_sems.at[k]) for k in range(nk)]
        if stage == 0:
            for cp in local:
                cp.start()
        if stage == 2:
            for cp in local:
                cp.wait()
        for half in (0, 1):
            @pl.when(c == half)
            def _(half=half):
                my_k = [k for k in range(nk) if pieces[k][5] == half]
                other_k = [k for k in range(nk) if pieces[k][5] != half]
                for k in my_k:
                    for j in range(3):
                        if stage == 0:
                            ici(k, j, mine).start()
                        elif stage == 1:
                            ici(k, j, chip_of[j]).wait_recv()
                            fwd(k, j).start()
                        else:
                            ici(k, j, mine).wait_send()
                            fwd(k, j).wait_send()
                if stage == 2:
                    for k in other_k:
                        for j in range(3):
                            fwd(k, j).wait_recv()


def _fwd_proj(x2d, meta, tabs, tabs_m, norm_g, w_in_p, q_norm_g, wq_p, kv_norm_g, wkv_p, w_out_shard, nb, s, tm):
    nt = s // tm
    n = nb * nt
    n_steps = n + 1
    c_t, sa_t, sb_t = tabs
    cm_t, sam_t, sbm_t = tabs_m
    gat = _StagedGather(W_OUT_PIECES)
    assert n_steps >= 3

    def body(x_ref, c_ref, sa_ref, sb_ref, mt_ref, cm_ref, sam_ref, sbm_ref,
             g_ref, w_ref, gq_ref, wq_ref, gkv_ref, wkv_ref, wos_ref,
             p_ref, q_ref, k_ref, v_ref, pm_ref, km_ref, vm_ref, wo_ref, *gat_scr):
        i = pl.program_id(0)
        for stage, at in enumerate((0, n_steps - 2, n_steps - 1)):
            @pl.when(i == at)
            def _(stage=stage):
                gat.run(stage, wos_ref, wo_ref, gat_scr)

        def project(xv, c, sa, sb, p_out, q_out, k_out, v_out):
            u, _ = _rms(xv, g_ref[...])
            p = _dot_nt(u.astype(BF16), w_ref[...])
            p_out[...] = p
            qn, _ = _rms(p[:, 0:Q_RANK], gq_ref[...])
            q = _dot_nt(qn.astype(BF16), wq_ref[...])
            kvn, _ = _rms(p[:, Q_RANK:Q_RANK + KV_RANK], gkv_ref[...])
            kv = _dot_nt(kvn.astype(BF16), wkv_ref[...])
            kpe = _rope(p[:, 384:512], c, sa, sb)
            for h in range(HEADS):
                if q_out is not None:
                    pe = _rope(q[:, QK_PAD * h + NOPE:QK_PAD * (h + 1)], c, sa, sb)
                    qh = jnp.concatenate([q[:, QK_PAD * h:QK_PAD * h + NOPE], pe], axis=1)
                    q_out[0, h] = (qh * ATTN_SCALE).astype(BF16)
                k_out[0, h] = jnp.concatenate([kv[:, NOPE * h:NOPE * (h + 1)], kpe], axis=1).astype(BF16)
                v_out[0, h] = kv[:, 512 + VDIM * h:512 + VDIM * (h + 1)].astype(BF16)

        @pl.when(i < n)
        def _():
            project(x_ref[...], c_ref[...], sa_ref[...], sb_ref[...], p_ref, q_ref, k_ref, v_ref)

        @pl.when(i == n)
        def _():
            project(mt_ref[...], cm_ref[...], sam_ref[...], sbm_ref[...], pm_ref, None, km_ref, vm_ref)

    cl = lambda i: jnp.minimum(i, n - 1)
    full = lambda a: pl.BlockSpec(a.shape, lambda i: (0,) * a.ndim)
    const = lambda shape: pl.BlockSpec(shape, lambda i: (0,) * len(shape))
    tab = pl.BlockSpec((tm, 128), lambda i: (cl(i) % nt, 0))
    hb = lambda w: pl.BlockSpec((1, HEADS, tm, w), lambda i: (cl(i) // nt, 0, cl(i) % nt, 0))
    hbm = pl.BlockSpec(memory_space=pl.ANY)
    return pl.pallas_call(
        body, name="fwd_proj", grid=(n_steps,),
        in_specs=[pl.BlockSpec((tm, D_MODEL), lambda i: (cl(i), 0)), tab, tab, tab,
                  full(meta), full(cm_t), full(sam_t), full(sbm_t),
                  full(norm_g), full(w_in_p), full(q_norm_g), full(wq_p), full(kv_norm_g), full(wkv_p), hbm],
        out_specs=[pl.BlockSpec((tm, IN_PAD), lambda i: (cl(i), 0)), hb(QK_PAD), hb(QK_PAD), hb(VDIM),
                   const((N_META, IN_PAD)), const((1, HEADS, N_META, QK_PAD)), const((1, HEADS, N_META, VDIM)), hbm],
        out_shape=[jax.ShapeDtypeStruct((nb * s, IN_PAD), F32),
                   jax.ShapeDtypeStruct((nb, HEADS, s, QK_PAD), BF16),
                   jax.ShapeDtypeStruct((nb, HEADS, s, QK_PAD), BF16),
                   jax.ShapeDtypeStruct((nb, HEADS, s, VDIM), BF16),
                   jax.ShapeDtypeStruct((N_META, IN_PAD), F32),
                   jax.ShapeDtypeStruct((1, HEADS, N_META, QK_PAD), BF16),
                   jax.ShapeDtypeStruct((1, HEADS, N_META, VDIM), BF16),
                   jax.ShapeDtypeStruct((D_MODEL, D_MODEL), BF16)],
        scratch_shapes=gat.scratch(),
        compiler_params=_cparams("arbitrary"),
    )(x2d, c_t, sa_t, sb_t, meta, cm_t, sam_t, sbm_t, norm_g, w_in_p, q_norm_g, wq_p, kv_norm_g, wkv_p, w_out_shard)


def _attn_fwd(q, k, v, km, vm, nb, s, tq):
    nq = s // tq

    def body(q_ref, k_ref, v_ref, km_ref, vm_ref, o_ref, lse_ref, s_scr, p_scr):
        row = lax.broadcasted_iota(jnp.int32, (tq, tq), 0)
        col = lax.broadcasted_iota(jnp.int32, (tq, tq), 1)
        def scores(i):
            slot = i % 2
            qi = q_ref[0, 0, i * tq:(i + 1) * tq, :]
            sm = _dot_nt(qi, km_ref[0, 0])
            m128 = None
            for j in range(i + 1):
                sc = _dot_nt(qi, k_ref[0, 0, j * tq:(j + 1) * tq, :])
                if j == i:
                    sc = jnp.where(col <= row, sc, NEG_INF)
                s_scr[slot, :, j * tq:(j + 1) * tq] = sc
                mx = sc[:, 0:128]
                for c0 in range(128, tq, 128):
                    mx = jnp.maximum(mx, sc[:, c0:c0 + 128])
                m128 = mx if m128 is None else jnp.maximum(m128, mx)
            return sm, jnp.maximum(jnp.max(m128, axis=1, keepdims=True), jnp.max(sm, axis=1, keepdims=True))

        def weighted_sum(i, pm, l):
            n = (i + 1) * tq
            acc = _dot(p_scr[i % 2, :, 0:n], v_ref[0, 0, 0:n, :]) + _dot(pm.astype(BF16), vm_ref[0, 0])
            o_ref[0, 0, i * tq:(i + 1) * tq, :] = acc / l

        nxt, pending = scores(0), None
        for i in range(nq):
            slot = i % 2
            sm, m = nxt
            if i + 1 < nq:
                nxt = scores(i + 1)
            pm = jnp.exp(sm - m)
            l128 = None
            for j in range(i + 1):
                p = jnp.exp(s_scr[slot, :, j * tq:(j + 1) * tq] - m)
                p_scr[slot, :, j * tq:(j + 1) * tq] = p.astype(BF16)
                ps = p[:, 0:128]
                for c0 in range(128, tq, 128):
                    ps = ps + p[:, c0:c0 + 128]
                l128 = ps if l128 is None else l128 + ps
            l = jnp.sum(l128, axis=1, keepdims=True) + jnp.sum(pm, axis=1, keepdims=True)
            lse_ref[0, 0, :, i * tq:(i + 1) * tq] = _row_of(m + jnp.log(l), tq)
            if pending is not None:
                weighted_sum(*pending)
            pending = (i, pm, l)
        weighted_sum(*pending)

    hblk = lambda w: pl.BlockSpec((1, 1, s, w), lambda b, h: (b, h, 0, 0))
    mblk = lambda w: pl.BlockSpec((1, 1, N_META, w), lambda b, h: (0, h, 0, 0))
    return pl.pallas_call(
        body, name="attn_fwd", grid=(nb, HEADS),
        in_specs=[hblk(QK_PAD), hblk(QK_PAD), hblk(VDIM), mblk(QK_PAD), mblk(VDIM)],
        out_specs=[hblk(VDIM), pl.BlockSpec((1, 1, 1, s), lambda b, h: (b, h, 0, 0))],
        out_shape=[jax.ShapeDtypeStruct((nb, HEADS, s, VDIM), F32),
                   jax.ShapeDtypeStruct((nb, HEADS, 1, s), F32)],
        scratch_shapes=[pltpu.VMEM((2, tq, s), F32), pltpu.VMEM((2, tq, s), BF16)],
        compiler_params=_cparams("parallel", "parallel"),
    )(q, k, v, km, vm)


def _shift_rows(a, prev, n_rows):
    rid = lax.broadcasted_iota(jnp.int32, a.shape, 0)
    a1 = jnp.where(rid == 0, prev[7:8, :], pltpu.roll(a, 1, 0))
    a2 = jnp.where(rid == 0, prev[6:7, :], jnp.where(rid == 1, prev[7:8, :], pltpu.roll(a, 2, 0)))
    return a1, a2


def _attn_gate(o, za, ga_h):
    on, r = _rms(o, ga_h)
    return on * (za * _sigmoid(za)), on, r


def _out_fwd_bwd(x2d, tgt2d, o, p, pm, conv_w, ga, gc, gmat, w_out, gf, nb, s, tm):
    nt = s // tm
    r = nb * s
    prev_idx = lambda i: jnp.maximum(i * (tm // 8) - 1, 0)

    def body(x_ref, t_ref, o_ref, za_ref, cb_ref, cc_ref, ch_ref, zc_ref, ccp_ref, chp_ref, mc_ref, mh_ref,
             cw_ref, ga_ref, gc_ref, gm_ref, w_ref, gf_ref,
             dh_ref, dy_ref, dw_ref, dgf_ref, loss_ref):
        i = pl.program_id(0)

        @pl.when(i == 0)
        def _():
            dw_ref[...] = jnp.zeros_like(dw_ref)
            dgf_ref[...] = jnp.zeros_like(dgf_ref)
            loss_ref[...] = jnp.zeros_like(loss_ref)

        ya = []
        for h in range(HEADS):
            y, _, _ = _attn_gate(o_ref[0, h], za_ref[:, VDIM * h:VDIM * (h + 1)],
                                 ga_ref[:, VDIM * h:VDIM * (h + 1)])
            ya.append(y)
        cc = cc_ref[...] * ch_ref[...]
        prev = jnp.where(i % nt == 0, mc_ref[8:16, :] * mh_ref[8:16, :], ccp_ref[...] * chp_ref[...])
        cc1, cc2 = _shift_rows(cc, prev, tm)
        yc = cb_ref[...] * (cw_ref[0:1, :] * cc2 + cw_ref[1:2, :] * cc1 + cw_ref[2:3, :] * cc)
        rg = lax.rsqrt(_group_mean(yc * yc, gm_ref[...]) + EPS)
        zc = zc_ref[...]
        yconv = yc * rg * gc_ref[...] * (zc * _sigmoid(zc))
        ycat = jnp.concatenate(ya + [yconv], axis=1).astype(BF16)
        h2 = x_ref[...] + _dot(ycat, w_ref[...])
        gfv = gf_ref[...]
        y, r2 = _rms(h2, gfv)
        e = y - t_ref[...]
        loss_ref[...] += 0.5 * jnp.sum(e * e) / D_MODEL
        dyv = e * (1.0 / D_MODEL)
        dh2, dgf = _rms_bwd(dyv, h2, r2, gfv)
        dgf_ref[...] += jnp.sum(dgf, axis=0, keepdims=True)
        dh_ref[...] = dh2
        dhb = dh2.astype(BF16)
        dy_ref[...] = _dot_nt(dhb, w_ref[...])
        dw_ref[...] += _dot_tn(ycat, dhb)

    row = lambda w, j: pl.BlockSpec((tm, w), lambda i: (i, j))
    pblk = lambda j: pl.BlockSpec((tm, 512), lambda i: (i, j))
    pprev = lambda j: pl.BlockSpec((8, 512), lambda i: (prev_idx(i), j))
    mblk = lambda j: pl.BlockSpec((N_META, 512), lambda i: (0, j))
    full = lambda a: pl.BlockSpec(a.shape, lambda i: (0,) * a.ndim)
    return pl.pallas_call(
        body, name="out_fwd_bwd", grid=(nb * nt,),
        in_specs=[row(D_MODEL, 0), row(D_MODEL, 0),
                  pl.BlockSpec((1, HEADS, tm, VDIM), lambda i: (i // nt, 0, i % nt, 0)),
                  pblk(BLK_ZA), pblk(BLK_CB), pblk(BLK_CC), pblk(BLK_CH), pblk(BLK_ZC),
                  pprev(BLK_CC), pprev(BLK_CH), mblk(BLK_CC), mblk(BLK_CH),
                  full(conv_w), full(ga), full(gc), full(gmat), full(w_out), full(gf)],
        out_specs=[row(D_MODEL, 0), row(D_MODEL, 0),
                   pl.BlockSpec((D_MODEL, D_MODEL), lambda i: (0, 0)),
                   pl.BlockSpec((1, D_MODEL), lambda i: (0, 0)),
                   pl.BlockSpec((1, 128), lambda i: (0, 0))],
        out_shape=[jax.ShapeDtypeStruct((r, D_MODEL), F32), jax.ShapeDtypeStruct((r, D_MODEL), F32),
                   jax.ShapeDtypeStruct((D_MODEL, D_MODEL), F32), jax.ShapeDtypeStruct((1, D_MODEL), F32),
                   jax.ShapeDtypeStruct((1, 128), F32)],
        compiler_params=_cparams("arbitrary"),
    )(x2d, tgt2d, o, p, p, p, p, p, p, p, pm, pm, conv_w, ga, gc, gmat, w_out, gf)


def _gate_bwd(dycat, o, p, pm, conv_w, ga, gc, gmat, nb, s, tm):
    nt = s // tm
    r = nb * s
    ext = tm + 8
    prev_idx = lambda i: jnp.maximum(i * (tm // 8) - 1, 0)
    next_idx = lambda i: jnp.minimum((i + 1) * (tm // 8), r // 8 - 1)

    def body(dya_ref, dyc_ref, dycn_ref, o_ref, za_ref, cb_ref, cbn_ref, cc_ref, ccp_ref, ccn_ref,
             ch_ref, chp_ref, chn_ref, zc_ref, zcn_ref, mc_ref, mh_ref, cw_ref, ga_ref, gc_ref, gm_ref,
             dpb_ref, do_ref, dl_ref, dccm_ref, dga_ref, dgc_ref, dcw_ref):
        i = pl.program_id(0)

        @pl.when(i == 0)
        def _():
            dga_ref[...] = jnp.zeros_like(dga_ref)
            dgc_ref[...] = jnp.zeros_like(dgc_ref)
            dcw_ref[...] = jnp.zeros_like(dcw_ref)

        dga = []
        for h in range(HEADS):
            hs = slice(VDIM * h, VDIM * (h + 1))
            oh, za, gah, dya = o_ref[0, h], za_ref[:, hs], ga_ref[:, hs], dya_ref[:, hs]
            sg = _sigmoid(za)
            on, ro = _rms(oh, gah)
            don = dya * (za * sg)
            dpb_ref[:, hs] = (dya * on * (sg * (1.0 + za * (1.0 - sg)))).astype(BF16)
            do, dg = _rms_bwd(don, oh, ro, gah)
            dga.append(jnp.sum(dg, axis=0, keepdims=True))
            dob = do.astype(BF16)
            do_ref[0, h] = dob
            dl_ref[0, h] = _row_of(jnp.sum(dob.astype(F32) * oh, axis=1, keepdims=True), tm)
        dga_ref[...] += jnp.concatenate(dga, axis=1)

        cat = lambda a, b: jnp.concatenate([a[...], b[...]], axis=0)
        cch = cat(cc_ref, ccn_ref)
        chh = cat(ch_ref, chn_ref)
        cb = cat(cb_ref, cbn_ref)
        zc = cat(zc_ref, zcn_ref)
        dy = cat(dyc_ref, dycn_ref)
        first = i % nt == 0
        last = i % nt == nt - 1
        cc = cch * chh
        prev = jnp.where(first, mc_ref[8:16, :] * mh_ref[8:16, :], ccp_ref[...] * chp_ref[...])
        cc1, cc2 = _shift_rows(cc, prev, ext)
        w0, w1, w2 = cw_ref[0:1, :], cw_ref[1:2, :], cw_ref[2:3, :]
        dw = w0 * cc2 + w1 * cc1 + w2 * cc
        yc = cb * dw
        rg = lax.rsqrt(_group_mean(yc * yc, gm_ref[...]) + EPS)
        ych = yc * rg
        gcv = gc_ref[...]
        sg = _sigmoid(zc)
        dycn = dy * (zc * sg)
        dzc = dy * (ych * gcv) * (sg * (1.0 + zc * (1.0 - sg)))
        dgc_ref[...] += jnp.sum((dycn * ych)[:tm], axis=0, keepdims=True)
        dycg = dycn * gcv
        dyc = rg * (dycg - ych * _group_mean(dycg * ych, gm_ref[...]))
        rid = lax.broadcasted_iota(jnp.int32, (ext, CONV_W), 0)
        ddw = jnp.where(jnp.logical_and(last, rid >= tm), 0.0, dyc * cb)
        dcb = dyc * dw
        dcc = w2 * ddw + w1 * pltpu.roll(ddw, ext - 1, 0) + w0 * pltpu.roll(ddw, ext - 2, 0)
        dpb_ref[:, 512:1024] = dcb[:tm].astype(BF16)
        dpb_ref[:, 1024:1536] = (dcc * chh)[:tm].astype(BF16)
        dpb_ref[:, 1536:2048] = (dcc * cch)[:tm].astype(BF16)
        dpb_ref[:, 2048:2560] = dzc[:tm].astype(BF16)
        rs = lambda a: jnp.sum(a[:tm], axis=0, keepdims=True)
        dcw_ref[0:1, :] += rs(ddw * cc2)
        dcw_ref[1:2, :] += rs(ddw * cc1)
        dcw_ref[2:3, :] += rs(ddw * cc)

        @pl.when(first)
        def _():
            d0, d1 = ddw[0:1, :], ddw[1:2, :]
            r8 = lax.broadcasted_iota(jnp.int32, (8, CONV_W), 0)
            dccm_ref[0] = jnp.where(r8 == 7, w1 * d0 + w0 * d1, jnp.where(r8 == 6, w0 * d0, 0.0))

    row = lambda j: pl.BlockSpec((tm, 512), lambda i: (i, j))
    prv = lambda j: pl.BlockSpec((8, 512), lambda i: (prev_idx(i), j))
    nxt = lambda j: pl.BlockSpec((8, 512), lambda i: (next_idx(i), j))
    mblk = lambda j: pl.BlockSpec((N_META, 512), lambda i: (0, j))
    full = lambda a: pl.BlockSpec(a.shape, lambda i: (0,) * a.ndim)
    hb = lambda w: pl.BlockSpec((1, HEADS, tm, w), lambda i: (i // nt, 0, i % nt, 0))
    acc = lambda rr: pl.BlockSpec((rr, 512), lambda i: (0, 0))
    return pl.pallas_call(
        body, name="gate_bwd", grid=(nb * nt,),
        in_specs=[row(0), row(1), nxt(1), hb(VDIM),
                  row(BLK_ZA), row(BLK_CB), nxt(BLK_CB), row(BLK_CC), prv(BLK_CC), nxt(BLK_CC),
                  row(BLK_CH), prv(BLK_CH), nxt(BLK_CH), row(BLK_ZC), nxt(BLK_ZC),
                  mblk(BLK_CC), mblk(BLK_CH), full(conv_w), full(ga), full(gc), full(gmat)],
        out_specs=[pl.BlockSpec((tm, 2560), lambda i: (i, 0)), hb(VDIM),
                   pl.BlockSpec((1, HEADS, 1, tm), lambda i: (i // nt, 0, 0, i % nt)),
                   pl.BlockSpec((1, 8, 512), lambda i: (i // nt, 0, 0)),
                   acc(1), acc(1), acc(8)],
        out_shape=[jax.ShapeDtypeStruct((r, 2560), BF16), jax.ShapeDtypeStruct((nb, HEADS, s, VDIM), BF16),
                   jax.ShapeDtypeStruct((nb, HEADS, 1, s), F32), jax.ShapeDtypeStruct((nb, 8, 512), F32),
                   jax.ShapeDtypeStruct((1, 512), F32), jax.ShapeDtypeStruct((1, 512), F32),
                   jax.ShapeDtypeStruct((8, 512), F32)],
        compiler_params=_cparams("arbitrary"),
    )(dycat, dycat, dycat, o, p, p, p, p, p, p, p, p, p, p, p, pm, pm, conv_w, ga, gc, gmat)


class _StagedReduce:
    LOC, PRE_S, PRE_R, ICI_S, ICI_R, POST_S, POST_R, OUT, N_SEM = 0, 1, 2, 3, 6, 9, 10, 11, 12

    def __init__(self, shard_shape):
        self.half = (shard_shape[0] // 2, shard_shape[1])

    def scratch(self):
        h = self.half
        return [pltpu.VMEM((4,) + h, F32), pltpu.VMEM((4,) + h, F32), pltpu.VMEM((4,) + h, BF16),
                pltpu.VMEM((3,) + h, BF16), pltpu.VMEM(h, F32), pltpu.SemaphoreType.DMA((self.N_SEM,))]

    def run(self, stage, pin, gout, scr):
        own, sib, wire, rbuf, fin, sems = scr
        r2 = self.half[0]
        x, y, c = lax.axis_index("x"), lax.axis_index("y"), lax.axis_index("c")
        mine = 2 * x + y
        sibling = (x, y, 1 - c)
        chips = [(1 - x, y), (x, 1 - y), (1 - x, 1 - y)]
        rows = lambda half: pl.ds(pl.multiple_of(half * r2, r2), r2)
        mesh = pl.DeviceIdType.MESH

        loc = pltpu.make_async_copy(pin.at[:, rows(c), :], own, sems.at[self.LOC])
        pre = pltpu.make_async_remote_copy(
            src_ref=pin.at[:, rows(1 - c), :], dst_ref=sib, send_sem=sems.at[self.PRE_S],
            recv_sem=sems.at[self.PRE_R], device_id=sibling, device_id_type=mesh)

        def ici(j):
            px, py = chips[j]
            return pltpu.make_async_remote_copy(
                src_ref=wire.at[2 * px + py], dst_ref=rbuf.at[j], send_sem=sems.at[self.ICI_S + j],
                recv_sem=sems.at[self.ICI_R + j], device_id=(px, py, c), device_id_type=mesh)

        def post(half):
            return pltpu.make_async_remote_copy(
                src_ref=fin, dst_ref=gout.at[rows(half), :], send_sem=sems.at[self.POST_S],
                recv_sem=sems.at[self.POST_R], device_id=sibling, device_id_type=mesh)

        keep = pltpu.make_async_copy(fin, gout.at[rows(c), :], sems.at[self.OUT])
        if stage == 0:
            loc.start()
            pre.start()
        elif stage == 1:
            loc.wait()
            pre.wait_recv()
            for blk in range(4):
                tot = own[blk] + sib[blk]
                own[blk] = tot
                wire[blk] = tot.astype(BF16)
            for j in range(3):
                ici(j).start()
        elif stage == 2:
            for j in range(3):
                ici(j).wait_recv()
            tot = own[mine]
            for j in range(3):
                tot = tot + rbuf[j].astype(F32)
            fin[...] = tot
            post(c).start()
            keep.start()
        else:
            post(1 - c).wait_recv()
            pre.wait_send()
            for j in range(3):
                ici(j).wait_send()
            post(c).wait_send()
            keep.wait()


def _attn_bwd(q, k, v, do, lse, delta, km, vm, early, nb, s, t):
    n = s // t
    ne = len(early)
    reds = [_StagedReduce(a.shape[1:]) for a in early]
    n_steps = HEADS * nb
    assert n_steps >= 4

    def body(q_ref, k_ref, v_ref, do_ref, lse_ref, dl_ref, km_ref, vm_ref, *rest):
        pin_refs, rest = rest[:ne], rest[ne:]
        dq_ref, dk_ref, dv_ref, dkm_ref, dvm_ref = rest[:5]
        gout_refs, (p_scr, ds_scr, dq_acc), red_scr = rest[5:5 + ne], rest[5 + ne:8 + ne], rest[8 + ne:]
        b = pl.program_id(1)
        step = pl.program_id(0) * nb + b
        for stage, at in enumerate((0, 1, n_steps - 2, n_steps - 1)):
            @pl.when(step == at)
            def _(stage=stage):
                for a, red in enumerate(reds):
                    red.run(stage, pin_refs[a], gout_refs[a], red_scr[6 * a:6 * a + 6])

        @pl.when(b == 0)
        def _():
            dkm_ref[...] = jnp.zeros_like(dkm_ref)
            dvm_ref[...] = jnp.zeros_like(dvm_ref)

        kr = lax.broadcasted_iota(jnp.int32, (t, t), 0)
        qc = lax.broadcasted_iota(jnp.int32, (t, t), 1)
        km_v, vm_v = km_ref[0, 0], vm_ref[0, 0]
        ptm = jnp.exp(_dot_nt(km_v, q_ref[0, 0]) - lse_ref[0, 0])
        dstm = (ptm * (_dot_nt(vm_v, do_ref[0, 0]) - dl_ref[0, 0])).astype(BF16)
        dkm_ref[0] += _dot(dstm, q_ref[0, 0])
        dvm_ref[0] += _dot(ptm.astype(BF16), do_ref[0, 0])
        dq_acc[...] = _dot_tn(dstm, km_v)
        def tiles(j):
            slot = j % 2
            kj = k_ref[0, 0, j * t:(j + 1) * t, :]
            vj = v_ref[0, 0, j * t:(j + 1) * t, :]
            def products(i):
                cs = slice(i * t, (i + 1) * t)
                return _dot_nt(kj, q_ref[0, 0, cs, :]), _dot_nt(vj, do_ref[0, 0, cs, :])

            nxt, pending = products(j), None
            for i in range(j, n):
                cs = slice(i * t, (i + 1) * t)
                st, dpt = nxt
                if i + 1 < n:
                    nxt = products(i + 1)
                if i == j:
                    st = jnp.where(kr <= qc, st, NEG_INF)
                pt = jnp.exp(st - lse_ref[0, 0, :, cs])
                dst = (pt * (dpt - dl_ref[0, 0, :, cs])).astype(BF16)
                p_scr[slot, :, cs] = pt.astype(BF16)
                ds_scr[slot, :, cs] = dst
                if pending is not None:
                    dq_acc[pending[0], :] += _dot_tn(pending[1], kj)
                pending = (cs, dst)
            dq_acc[pending[0], :] += _dot_tn(pending[1], kj)

        for j in range(n):
            slot = j % 2
            tiles(j)
            dv_ref[0, 0, j * t:(j + 1) * t, :] = _dot(p_scr[slot, :, j * t:s], do_ref[0, 0, j * t:s, :]).astype(BF16)
            dk_ref[0, 0, j * t:(j + 1) * t, :] = _dot(ds_scr[slot, :, j * t:s], q_ref[0, 0, j * t:s, :]).astype(BF16)
        dq_ref[0, 0] = dq_acc[...].astype(BF16)

    big = lambda w: pl.BlockSpec((1, 1, s, w), lambda h, b: (b, h, 0, 0))
    rowv = pl.BlockSpec((1, 1, 1, s), lambda h, b: (b, h, 0, 0))
    mk = lambda w: pl.BlockSpec((1, 1, N_META, w), lambda h, b: (0, h, 0, 0))
    mo = lambda w: pl.BlockSpec((1, N_META, w), lambda h, b: (h, 0, 0))
    return pl.pallas_call(
        body, name="attn_bwd", grid=(HEADS, nb),
        in_specs=[big(QK_PAD), big(QK_PAD), big(VDIM), big(VDIM), rowv, rowv, mk(QK_PAD), mk(VDIM)]
        + [pl.BlockSpec(memory_space=pl.ANY)] * ne,
        out_specs=[big(QK_PAD), big(QK_PAD), big(VDIM), mo(QK_PAD), mo(VDIM)]
        + [pl.BlockSpec(memory_space=pl.ANY)] * ne,
        out_shape=[jax.ShapeDtypeStruct((nb, HEADS, s, QK_PAD), BF16),
                   jax.ShapeDtypeStruct((nb, HEADS, s, QK_PAD), BF16),
                   jax.ShapeDtypeStruct((nb, HEADS, s, VDIM), BF16),
                   jax.ShapeDtypeStruct((HEADS, N_META, QK_PAD), F32),
                   jax.ShapeDtypeStruct((HEADS, N_META, VDIM), F32)]
        + [jax.ShapeDtypeStruct(a.shape[1:], F32) for a in early],
        scratch_shapes=[pltpu.VMEM((2, t, s), BF16), pltpu.VMEM((2, t, s), BF16), pltpu.VMEM((s, QK_PAD), F32)]
        + [sc for red in reds for sc in red.scratch()],
        compiler_params=_cparams("arbitrary", "arbitrary"),
    )(q, k, v, do, lse, delta, km, vm, *early)


def _up_bwd(dq, dk, dv, dkm, dvm, p, pm, tabs, tabs_m, wq_p, wkv_p, gq, gkv, nb, s, tm):
    nt = s // tm
    n = nb * nt
    c_t, sa_t, sb_t = tabs
    cm_t, sam_t, sbm_t = tabs_m

    def kv_path(dkh, dvh, pa, c, sa, sb, wkv, gkvv):
        dkpe = dkh[0][:, NOPE:]
        for h in range(1, HEADS):
            dkpe = dkpe + dkh[h][:, NOPE:]
        dkr = _rope_bwd(dkpe, c, sa, sb)
        dkv = jnp.concatenate([d[:, :NOPE] for d in dkh] + list(dvh), axis=1).astype(BF16)
        ckv = pa[:, Q_RANK:Q_RANK + KV_RANK]
        kvn, rkv = _rms(ckv, gkvv)
        dckv, dg = _rms_bwd(_dot(dkv, wkv), ckv, rkv, gkvv)
        return dckv, dkr, kvn.astype(BF16), dkv, jnp.sum(dg, axis=0, keepdims=True)

    def body(dq_ref, dk_ref, dv_ref, pa_ref, c_ref, sa_ref, sb_ref,
             dkm_ref, dvm_ref, pam_ref, cm_ref, sam_ref, sbm_ref,
             wq_ref, wkv_ref, gq_ref, gkv_ref,
             dpa_ref, dpam_ref, pq_ref, pkv_ref, dgq_ref, dgkv_ref, dwq_ref, dwkv_ref):
        i = pl.program_id(0)

        @pl.when(i == 0)
        def _():
            dwq_ref[...] = jnp.zeros_like(dwq_ref)
            dwkv_ref[...] = jnp.zeros_like(dwkv_ref)
            dgq_ref[...] = jnp.zeros_like(dgq_ref)
            dgkv_ref[...] = jnp.zeros_like(dgkv_ref)

        @pl.when(i < n)
        def _():
            c, sa, sb = c_ref[...], sa_ref[...], sb_ref[...]
            pa = pa_ref[...]
            parts = []
            for h in range(HEADS):
                dqh = dq_ref[0, h].astype(F32) * ATTN_SCALE
                parts += [dqh[:, :NOPE], _rope_bwd(dqh[:, NOPE:], c, sa, sb)]
            dql = jnp.concatenate(parts, axis=1).astype(BF16)
            cq = pa[:, 0:Q_RANK]
            gqv = gq_ref[...]
            qn, rq = _rms(cq, gqv)
            dwq_ref[...] += _dot_tn(dql, qn.astype(BF16))
            dcq, dg = _rms_bwd(_dot(dql, wq_ref[...]), cq, rq, gqv)
            dgq_ref[...] += jnp.sum(dg, axis=0, keepdims=True)
            dckv, dkr, kvn, dkv, dgk = kv_path([dk_ref[0, h].astype(F32) for h in range(HEADS)],
                                               [dv_ref[0, h].astype(F32) for h in range(HEADS)],
                                               pa, c, sa, sb, wkv_ref[...], gkv_ref[...])
            dwkv_ref[...] += _dot_tn(dkv, kvn)
            dgkv_ref[...] += dgk
            dpa_ref[...] = jnp.concatenate([dcq, dckv, dkr], axis=1).astype(BF16)

        @pl.when(i == n)
        def _():
            dckv, dkr, kvn, dkv, dgk = kv_path([dkm_ref[h] for h in range(HEADS)],
                                               [dvm_ref[h] for h in range(HEADS)],
                                               pam_ref[...], cm_ref[...], sam_ref[...], sbm_ref[...],
                                               wkv_ref[...], gkv_ref[...])
            dwkv_ref[...] += _dot_tn(dkv, kvn)
            dgkv_ref[...] += dgk
            dpam_ref[...] = jnp.concatenate([jnp.zeros((N_META, Q_RANK), F32), dckv, dkr], axis=1)
            for h in range(HEADS):
                pq_ref[h] = dwq_ref[QK_PAD * h:QK_PAD * h + NOPE + ROPE, :]
                pkv_ref[h, 0:NOPE, :] = dwkv_ref[NOPE * h:NOPE * (h + 1), :]
                pkv_ref[h, NOPE:NOPE + VDIM, :] = dwkv_ref[512 + VDIM * h:512 + VDIM * (h + 1), :]

    cl = lambda i: jnp.minimum(i, n - 1)
    hb = lambda w: pl.BlockSpec((1, HEADS, tm, w), lambda i: (cl(i) // nt, 0, cl(i) % nt, 0))
    tab = pl.BlockSpec((tm, 128), lambda i: (cl(i) % nt, 0))
    full = lambda a: pl.BlockSpec(a.shape, lambda i: (0,) * a.ndim)
    const = lambda shape: pl.BlockSpec(shape, lambda i: (0,) * len(shape))
    return pl.pallas_call(
        body, name="up_bwd", grid=(n + 1,),
        in_specs=[hb(QK_PAD), hb(QK_PAD), hb(VDIM), pl.BlockSpec((tm, 512), lambda i: (cl(i), 0)), tab, tab, tab,
                  full(dkm), full(dvm), pl.BlockSpec((N_META, 512), lambda i: (0, 0)),
                  full(cm_t), full(sam_t), full(sbm_t), full(wq_p), full(wkv_p), full(gq), full(gkv)],
        out_specs=[pl.BlockSpec((tm, 512), lambda i: (cl(i), 0)), const((N_META, 512)),
                   const((HEADS, NOPE + ROPE, Q_RANK)), const((HEADS, NOPE + VDIM, KV_RANK)),
                   const((1, Q_RANK)), const((1, KV_RANK))],
        out_shape=[jax.ShapeDtypeStruct((nb * s, 512), BF16), jax.ShapeDtypeStruct((N_META, 512), F32),
                   jax.ShapeDtypeStruct((HEADS, NOPE + ROPE, Q_RANK), F32),
                   jax.ShapeDtypeStruct((HEADS, NOPE + VDIM, KV_RANK), F32),
                   jax.ShapeDtypeStruct((1, Q_RANK), F32), jax.ShapeDtypeStruct((1, KV_RANK), F32)],
        scratch_shapes=[pltpu.VMEM((HEADS * QK_PAD, Q_RANK), F32), pltpu.VMEM((1024, KV_RANK), F32)],
        compiler_params=_cparams("arbitrary"),
    )(dq, dk, dv, p, c_t, sa_t, sb_t, dkm, dvm, pm, cm_t, sam_t, sbm_t, wq_p, wkv_p, gq, gkv)


def _in_bwd(x2d, dh2, dpa, dpb, meta, dpam, dccm, pm, w_in_p, norm_g, nb, s, tm):
    nt = s // tm
    n = nb * nt

    def body(x_ref, dh_ref, dpa_ref, dpb_ref, mt_ref, dpam_ref, dccm_ref, mc_ref, mh_ref, w_ref, g_ref,
             gx_ref, gm_ref, dw_hbm, dg_ref, acc_ref, sems):
        i = pl.program_id(0)

        @pl.when(i == 0)
        def _():
            acc_ref[...] = jnp.zeros_like(acc_ref)
            dg_ref[...] = jnp.zeros_like(dg_ref)

        def rows(x, dp, dres):
            g = g_ref[...]
            dpb16 = dp.astype(BF16)
            du = _dot(dpb16, w_ref[...])
            u, r1 = _rms(x, g)
            acc_ref[...] += _dot_tn(dpb16, u.astype(BF16))
            dx, dg = _rms_bwd(du, x, r1, g)
            dg_ref[...] += jnp.sum(dg, axis=0, keepdims=True)
            return dx if dres is None else dx + dres

        @pl.when(i < n)
        def _():
            dp = jnp.concatenate([dpa_ref[...], dpb_ref[...]], axis=1)
            gx_ref[...] = rows(x_ref[...], dp, dh_ref[...])

        @pl.when(i == n)
        def _():
            dcc = dccm_ref[0]
            for b in range(1, nb):
                dcc = dcc + dccm_ref[b]
            z8 = jnp.zeros((8, CONV_W), F32)
            dc = jnp.concatenate([z8, dcc * mh_ref[8:16, :]], axis=0)
            dh = jnp.concatenate([z8, dcc * mc_ref[8:16, :]], axis=0)
            z = jnp.zeros((N_META, CONV_W), F32)
            dp = jnp.concatenate([dpam_ref[...], z, z, dc, dh, z], axis=1)
            gm_ref[...] = rows(mt_ref[...], dp, None)
            per = IN_DIM // 4
            cps = [pltpu.make_async_copy(acc_ref.at[0:448], dw_hbm.at[0, 0:448], sems.at[0]),
                   pltpu.make_async_copy(acc_ref.at[512:per + 64], dw_hbm.at[0, 448:per], sems.at[1])]
            for qq in range(1, 4):
                cps.append(pltpu.make_async_copy(acc_ref.at[per * qq + 64:per * (qq + 1) + 64], dw_hbm.at[qq],
                                                 sems.at[qq + 1]))
            for cp in cps:
                cp.start()
            for cp in cps:
                cp.wait()

    cl = lambda i: jnp.minimum(i, n - 1)
    row = lambda w: pl.BlockSpec((tm, w), lambda i: (cl(i), 0))
    full = lambda a: pl.BlockSpec(a.shape, lambda i: (0,) * a.ndim)
    mblk = lambda j: pl.BlockSpec((N_META, 512), lambda i: (0, j))
    return pl.pallas_call(
        body, name="in_bwd", grid=(n + 1,),
        in_specs=[row(D_MODEL), row(D_MODEL), row(512), row(2560), full(meta), full(dpam), full(dccm),
                  mblk(BLK_CC), mblk(BLK_CH), full(w_in_p), full(norm_g)],
        out_specs=[row(D_MODEL), pl.BlockSpec((N_META, D_MODEL), lambda i: (0, 0)),
                   pl.BlockSpec(memory_space=pl.ANY), pl.BlockSpec((1, D_MODEL), lambda i: (0, 0))],
        out_shape=[jax.ShapeDtypeStruct((nb * s, D_MODEL), F32), jax.ShapeDtypeStruct((N_META, D_MODEL), F32),
                   jax.ShapeDtypeStruct((4, IN_DIM // 4, D_MODEL), F32), jax.ShapeDtypeStruct((1, D_MODEL), F32)],
        scratch_shapes=[pltpu.VMEM((IN_PAD, D_MODEL), F32), pltpu.SemaphoreType.DMA((5,))],
        compiler_params=_cparams("arbitrary"),
    )(x2d, dh2, dpa, dpb, meta, dpam, dccm, pm, pm, w_in_p, norm_g)


def _gather_weights(split, pieces, out_rows, whole, zero_fills):
    ns, nw, nz = len(split), len(whole), len(zero_fills)
    flat = [(a, pc) for a in range(ns) for pc in pieces[a]]
    nk = len(flat)

    def body(*refs):
        ins, wins, zins = refs[:ns], refs[ns:ns + nw], refs[ns + nw:ns + nw + nz]
        outs, wouts = refs[ns + nw + nz:2 * ns + nw + nz], refs[2 * ns + nw + nz:2 * (ns + nw) + nz]
        send_sems, recv_sems, fwd_send, fwd_recv, loc_sems, w_send, w_recv, w_loc, z_sems = refs[2 * (ns + nw) + nz:]
        x, y, c = lax.axis_index("x"), lax.axis_index("y"), lax.axis_index("c")
        mine = 2 * x + y
        chips = [(1 - x, y), (x, 1 - y), (1 - x, 1 - y)]
        chip_of = [2 * px + py for px, py in chips]

        def src(k):
            a, (s0, nr, _, _, _, _) = flat[k]
            return ins[a].at[s0:s0 + nr]

        def dst(k, q):
            a, (_, nr, per, first, rest, _) = flat[k]
            row = per * q + first + (rest - first) * jnp.minimum(q, 1)
            return outs[a].at[pl.ds(pl.multiple_of(row, 16), nr)]

        def ici(k, j, q):
            px, py = chips[j]
            return pltpu.make_async_remote_copy(
                src_ref=src(k), dst_ref=dst(k, q), send_sem=send_sems.at[k, j], recv_sem=recv_sems.at[k, j],
                device_id=(px, py, c), device_id_type=pl.DeviceIdType.MESH)

        def fwd(k, j):
            ref = dst(k, chip_of[j])
            return pltpu.make_async_remote_copy(
                src_ref=ref, dst_ref=ref, send_sem=fwd_send.at[k, j], recv_sem=fwd_recv.at[k, j],
                device_id=(x, y, 1 - c), device_id_type=pl.DeviceIdType.MESH)

        def wcopy(b, j, q):
            px, py = chips[j]
            return pltpu.make_async_remote_copy(
                src_ref=wins[b], dst_ref=wouts[b].at[q], send_sem=w_send.at[b, j], recv_sem=w_recv.at[b, j],
                device_id=(px, py, c), device_id_type=pl.DeviceIdType.MESH)

        local = [pltpu.make_async_copy(src(k), dst(k, mine), loc_sems.at[k]) for k in range(nk)]
        local += [pltpu.make_async_copy(wins[b], wouts[b].at[mine], w_loc.at[b]) for b in range(nw)]
        for z, (a, _, row0) in enumerate(zero_fills):
            local.append(pltpu.make_async_copy(zins[z], outs[a].at[row0:row0 + zins[z].shape[0]], z_sems.at[z]))
        wsends = [wcopy(b, j, mine) for b in range(nw) for j in range(3)]
        for cp in local + wsends:
            cp.start()

        for half in (0, 1):
            @pl.when(c == half)
            def _(half=half):
                my_k = [k for k in range(nk) if flat[k][1][5] == half]
                other_k = [k for k in range(nk) if flat[k][1][5] != half]
                sends = [ici(k, j, mine) for k in my_k for j in range(3)]
                for cp in sends:
                    cp.start()
                passed = []
                for k in my_k:
                    for j in range(3):
                        ici(k, j, chip_of[j]).wait_recv()
                        cp = fwd(k, j)
                        cp.start()
                        passed.append(cp)
                for k in other_k:
                    for j in range(3):
                        fwd(k, j).wait_recv()
                for cp in sends + passed:
                    cp.wait_send()

        for b in range(nw):
            for j in range(3):
                wcopy(b, j, chip_of[j]).wait_recv()
        for cp in wsends:
            cp.wait_send()
        for cp in local:
            cp.wait()

    hbm = pl.BlockSpec(memory_space=pl.ANY)
    dma = pltpu.SemaphoreType.DMA
    zeros = [z for _, z, _ in zero_fills]
    return pl.pallas_call(
        body, name="gather_weights",
        in_specs=[hbm] * (ns + nw + nz), out_specs=[hbm] * (ns + nw),
        out_shape=([jax.ShapeDtypeStruct((out_rows[a], split[a].shape[1]), split[a].dtype) for a in range(ns)]
                   + [jax.ShapeDtypeStruct((4,) + w.shape, w.dtype) for w in whole]),
        scratch_shapes=[dma((nk, 3)), dma((nk, 3)), dma((nk, 3)), dma((nk, 3)), dma((nk,)),
                        dma((nw, 3)), dma((nw, 3)), dma((nw,)), dma((nz,))],
        compiler_params=pltpu.CompilerParams(vmem_limit_bytes=VMEM_LIMIT),
    )(*split, *whole, *zeros)


def _reduce_grads(parts, small):
    n = len(parts)
    shapes = [a.shape[1:] for a in parts]
    halves = [(sh[0] // 2, sh[1]) for sh in shapes]

    def body(*refs):
        pin, sm_in = refs[:n], refs[n]
        gout, sm_out = refs[n + 1:2 * n + 1], refs[2 * n + 1]
        scr = refs[2 * n + 2:]
        own, sib, wire, rbuf = scr[:n], scr[n:2 * n], scr[2 * n:3 * n], scr[3 * n:4 * n]
        (sbuf, send_sems, recv_sems, loc_sems, pre_send, pre_recv, post_send, post_recv,
         sm_send, sm_recv) = scr[4 * n:]
        x, y, c = lax.axis_index("x"), lax.axis_index("y"), lax.axis_index("c")
        mine = 2 * x + y
        me = 4 * x + 2 * y + c
        sibling = (x, y, 1 - c)
        chips = [(1 - x, y), (x, 1 - y), (1 - x, 1 - y)]

        def rows(a, half):
            r2 = halves[a][0]
            return pl.ds(pl.multiple_of(half * r2, r2), r2)

        def pre(a):
            return pltpu.make_async_remote_copy(
                src_ref=pin[a].at[:, rows(a, 1 - c), :], dst_ref=sib[a], send_sem=pre_send.at[a],
                recv_sem=pre_recv.at[a], device_id=sibling, device_id_type=pl.DeviceIdType.MESH)

        def ici(a, j):
            px, py = chips[j]
            return pltpu.make_async_remote_copy(
                src_ref=wire[a].at[2 * px + py], dst_ref=rbuf[a].at[j], send_sem=send_sems.at[a, j],
                recv_sem=recv_sems.at[a, j], device_id=(px, py, c), device_id_type=pl.DeviceIdType.MESH)

        def post(a, half):
            ref = gout[a].at[rows(a, half), :]
            return pltpu.make_async_remote_copy(
                src_ref=ref, dst_ref=ref, send_sem=post_send.at[a], recv_sem=post_recv.at[a],
                device_id=sibling, device_id_type=pl.DeviceIdType.MESH)

        def small_copy(kk):
            peer = (x ^ (kk >> 2), y ^ ((kk >> 1) & 1), c ^ (kk & 1))
            return pltpu.make_async_remote_copy(
                src_ref=sm_in, dst_ref=sbuf.at[kk], send_sem=sm_send.at[kk - 1], recv_sem=sm_recv.at[kk - 1],
                device_id=peer, device_id_type=pl.DeviceIdType.MESH)

        local = [pltpu.make_async_copy(pin[a].at[:, rows(a, c), :], own[a], loc_sems.at[a]) for a in range(n)]
        pres = [pre(a) for a in range(n)]
        smalls = [small_copy(kk) for kk in range(1, 8)]
        for cp in local + pres + smalls:
            cp.start()
        sbuf[0] = sm_in[...]
        sends = []
        for a in range(n):
            local[a].wait()
            pres[a].wait_recv()
            for blk in range(4):
                tot = own[a][blk] + sib[a][blk]
                own[a][blk] = tot
                wire[a][blk] = tot.astype(BF16)
            for j in range(3):
                cp = ici(a, j)
                cp.start()
                sends.append(cp)
        for cp in smalls:
            cp.wait_recv()
        total = sbuf[me]
        for d in range(1, 8):
            total = total + sbuf[me ^ d]
        sm_out[...] = total
        posts = []
        for a in range(n):
            for j in range(3):
                ici(a, j).wait_recv()
            fin = own[a][mine]
            for j in range(3):
                fin = fin + rbuf[a][j].astype(F32)
            gout[a][rows(a, c), :] = fin
            cp = post(a, c)
            cp.start()
            posts.append(cp)
        for a in range(n):
            post(a, 1 - c).wait_recv()
        for cp in pres + sends + smalls + posts:
            cp.wait_send()

    hbm = pl.BlockSpec(memory_space=pl.ANY)
    vmem = pl.BlockSpec(memory_space=pltpu.VMEM)
    dma = pltpu.SemaphoreType.DMA
    return pl.pallas_call(
        body, name="reduce_grads",
        in_specs=[hbm] * n + [vmem], out_specs=[vmem] * (n + 1),
        out_shape=[jax.ShapeDtypeStruct(sh, F32) for sh in shapes] + [jax.ShapeDtypeStruct(small.shape, F32)],
        scratch_shapes=([pltpu.VMEM((4,) + hs, F32) for hs in halves] + [pltpu.VMEM((4,) + hs, F32) for hs in halves]
                        + [pltpu.VMEM((4,) + hs, BF16) for hs in halves]
                        + [pltpu.VMEM((3,) + hs, BF16) for hs in halves]
                        + [pltpu.VMEM((8,) + small.shape, F32), dma((n, 3)), dma((n, 3)), dma((n,)),
                           dma((n,)), dma((n,)), dma((n,)), dma((n,)), dma((7,)), dma((7,))]),
        compiler_params=pltpu.CompilerParams(vmem_limit_bytes=VMEM_LIMIT),
    )(*parts, small)


def _adamw_update(w_ref, g_ref, m_ref, v_ref, d_ref, nm_ref, nv_ref):
    gv = g_ref[...]
    nm = ADAM_B1 * m_ref[...] + (1.0 - ADAM_B1) * gv
    nv = ADAM_B2 * v_ref[...] + (1.0 - ADAM_B2) * (gv * gv)
    m_hat = nm / (1.0 - ADAM_B1 ** ADAM_STEP)
    v_hat = nv / (1.0 - ADAM_B2 ** ADAM_STEP)
    d_ref[...] = -ADAM_LR * (m_hat / (jnp.sqrt(v_hat) + ADAM_EPS) + ADAM_WD * w_ref[...])
    nm_ref[...] = nm
    nv_ref[...] = nv


def _adamw_small(ws, gs, ms, vs):
    k = len(ws)

    def body(*refs):
        ins, outs = refs[:4 * k], refs[4 * k:]
        for a in range(k):
            _adamw_update(ins[a], ins[k + a], ins[2 * k + a], ins[3 * k + a], outs[a], outs[k + a], outs[2 * k + a])

    out = pl.pallas_call(
        body, name="adamw_small",
        out_shape=[jax.ShapeDtypeStruct(w.shape, F32) for w in ws] * 3,
        compiler_params=pltpu.CompilerParams(vmem_limit_bytes=VMEM_LIMIT),
    )(*ws, *gs, *ms, *vs)
    return out[:k], out[k:2 * k], out[2 * k:]


def _adamw(w, g, m, v, name):
    shape = w.shape
    w2, g2, m2, v2 = (a.reshape((-1, shape[-1])) for a in (w, g, m, v))

    def body(w_ref, g_ref, m_ref, v_ref, d_ref, nm_ref, nv_ref):
        _adamw_update(w_ref, g_ref, m_ref, v_ref, d_ref, nm_ref, nv_ref)

    rows, cols = w2.shape
    nblk = cols // 256 if cols % 256 == 0 and rows >= 64 else 1
    blk = pl.BlockSpec((rows, cols // nblk), lambda j: (0, j))
    out = pl.pallas_call(
        body, name=name, grid=(nblk,), in_specs=[blk] * 4, out_specs=[blk] * 3,
        out_shape=[jax.ShapeDtypeStruct(w2.shape, F32)] * 3,
        compiler_params=_cparams("parallel"),
    )(w2, g2, m2, v2)
    return tuple(a.reshape(shape) for a in out)


def kernel(x, meta_tokens, norm_g, w_in, q_norm_g, w_q_up, kv_norm_g, w_kv_up, conv_w, attn_out_g, conv_out_g, w_out, final_norm_g, loss_target, m_meta_tokens, m_norm_g, m_w_in, m_q_norm_g, m_w_q_up, m_kv_norm_g, m_w_kv_up, m_conv_w, m_attn_out_g, m_conv_out_g, m_w_out, m_final_norm_g, v_meta_tokens, v_norm_g, v_w_in, v_q_norm_g, v_w_q_up, v_kv_norm_g, v_w_kv_up, v_conv_w, v_attn_out_g, v_conv_out_g, v_w_out, v_final_norm_g):
    nb, s, _ = x.shape
    tm = min(ROW_TILE, s)
    ta = min(ATTN_TILE, s)
    assert s % tm == 0 and s % ta == 0 and tm % 16 == 0
    r = nb * s

    tr = lambda a: jnp.transpose(a[0])
    w_in_p, wq_p, wkv_p, g_cw, g_meta = _gather_weights(
        [tr(w_in).astype(BF16), tr(w_q_up).astype(BF16), tr(w_kv_up).astype(BF16)],
        [W_IN_PIECES, W_Q_PIECES, W_KV_PIECES], [IN_PAD, HEADS * QK_PAD, 1024],
        [jnp.transpose(conv_w, (1, 0, 2)), meta_tokens],
        [(0, jnp.zeros((64, D_MODEL), BF16), 448)]
        + [(1, jnp.zeros((64, Q_RANK), BF16), QK_PAD * h + NOPE + ROPE) for h in range(HEADS)])
    conv_f = jnp.transpose(g_cw[:, :, 0, :], (1, 0, 2)).reshape(3, CONV_W)
    meta_f = jnp.transpose(g_meta, (1, 0, 2)).reshape(N_META, D_MODEL)

    c_all, sa_all, sb_all = _rope_tables(N_META + s)
    tabs_m = (c_all[:N_META], sa_all[:N_META], sb_all[:N_META])
    tabs = (c_all[N_META:], sa_all[N_META:], sb_all[N_META:])
    gid = np.arange(CONV_W) // CONV_GROUP
    gmat = jnp.asarray(np.where(gid[:, None] == gid[None, :], 1.0 / CONV_GROUP, 0.0), BF16)
    ga, gc = attn_out_g, conv_out_g
    gf = final_norm_g.reshape(1, D_MODEL)

    x2d = x.reshape(r, D_MODEL)
    tgt2d = loss_target.reshape(r, D_MODEL)

    p, q, k, v, pm, km, vm, w_out_f = _fwd_proj(x2d, meta_f, tabs, tabs_m, norm_g, w_in_p, q_norm_g, wq_p,
                                                kv_norm_g, wkv_p, w_out[0].astype(BF16), nb, s, tm)
    o, lse = _attn_fwd(q, k, v, km, vm, nb, s, ta)
    dh2, dycat, dw_out, dgf, loss_acc = _out_fwd_bwd(x2d, tgt2d, o, p, pm, conv_f, ga, gc, gmat, w_out_f, gf,
                                                     nb, s, tm)
    dpb, do, delta, dccm, dga, dgc, dcw = _gate_bwd(dycat, o, p, pm, conv_f, ga, gc, gmat, nb, s, tm)
    p_out = dw_out.reshape(4, D_MODEL // 4, D_MODEL)
    dq, dk, dv, dkm, dvm, g_w_out = _attn_bwd(q, k, v, do, lse, delta, km, vm, [p_out], nb, s, ta)
    dpa, dpam, p_q, p_kv, dgq, dgkv = _up_bwd(dq, dk, dv, dkm, dvm, p, pm, tabs, tabs_m, wq_p, wkv_p,
                                              q_norm_g, kv_norm_g, nb, s, tm)
    gx, gmeta, p_in, dng = _in_bwd(x2d, dh2, dpa, dpb, meta_f, dpam, dccm, pm, w_in_p, norm_g, nb, s, tm)

    flat =jnp.concatenate([dng.reshape(-1), dgq.reshape(-1), dgkv.reshape(-1), dga.reshape(-1), dgc.reshape(-1),
                            dgf.reshape(-1), dcw[:3].reshape(-1), gmeta.reshape(-1), loss_acc[0, 0:1]])
    n_small = flat.shape[0]
    rows_small = -(-n_small // 1024) * 8
    small = jnp.pad(flat, (0, rows_small * 128 - n_small)).reshape(rows_small, 128)
    g_w_in_t, g_w_q_t, g_w_kv_t, small_sum = _reduce_grads([p_in, p_q, p_kv], small)
    ssum = small_sum.reshape(-1)

    def take(off, n):
        return ssum[off:off + n], off + n

    off = 0
    g_norm, off = take(off, D_MODEL)
    g_qn, off = take(off, Q_RANK)
    g_kvn, off = take(off, KV_RANK)
    g_ga, off = take(off, CONV_W)
    g_gc, off = take(off, CONV_W)
    g_gf, off = take(off, D_MODEL)
    g_cw_all, off = take(off, 3 * CONV_W)
    g_meta_all, off = take(off, N_META * D_MODEL)
    loss = ssum[off]
    chip = 2 * lax.axis_index("x") + lax.axis_index("y")
    g_conv = lax.dynamic_slice(g_cw_all.reshape(3, CONV_W), (0, chip * 128), (3, 128))
    g_mt = lax.dynamic_slice(g_meta_all.reshape(N_META, D_MODEL), (0, chip * 256), (N_META, 256))

    grads = {
        "meta_tokens": g_mt, "norm_g": g_norm.reshape(1, -1), "w_in": g_w_in_t, "q_norm_g": g_qn.reshape(1, -1),
        "w_q_up": g_w_q_t, "kv_norm_g": g_kvn.reshape(1, -1), "w_kv_up": jnp.transpose(g_w_kv_t)[None],
        "conv_w": g_conv[None], "attn_out_g": g_ga.reshape(1, -1), "conv_out_g": g_gc.reshape(1, -1),
        "w_out": g_w_out[None], "final_norm_g": g_gf,
    }
    transposed = ("w_in", "w_q_up")
    weights = {
        "meta_tokens": (meta_tokens, m_meta_tokens, v_meta_tokens), "norm_g": (norm_g, m_norm_g, v_norm_g),
        "w_in": (w_in, m_w_in, v_w_in), "q_norm_g": (q_norm_g, m_q_norm_g, v_q_norm_g),
        "w_q_up": (w_q_up, m_w_q_up, v_w_q_up), "kv_norm_g": (kv_norm_g, m_kv_norm_g, v_kv_norm_g),
        "w_kv_up": (w_kv_up, m_w_kv_up, v_w_kv_up), "conv_w": (conv_w, m_conv_w, v_conv_w),
        "attn_out_g": (attn_out_g, m_attn_out_g, v_attn_out_g), "conv_out_g": (conv_out_g, m_conv_out_g, v_conv_out_g),
        "w_out": (w_out, m_w_out, v_w_out), "final_norm_g": (final_norm_g, m_final_norm_g, v_final_norm_g),
    }
    names = list(weights)
    big = ("w_in", "w_q_up", "w_kv_up", "w_out")
    small = [nme for nme in names if nme not in big]

    def view(nme, a):
        if nme == "conv_w":
            return jnp.transpose(a.reshape(1, 3, -1), (1, 0, 2))
        return a.reshape(1, -1) if a.ndim == 1 else a

    def unview(nme, a):
        if nme == "conv_w":
            return jnp.transpose(a, (1, 0, 2))
        return a.reshape(weights[nme][0].shape)

    res_small = _adamw_small(*[[view(nme, a) for nme, a in zip(small, col)] for col in (
        [weights[nme][0] for nme in small], [grads[nme] for nme in small],
        [weights[nme][1] for nme in small], [weights[nme][2] for nme in small])])
    upd = {}
    for nme in names:
        w_, m_, v_ = weights[nme]
        if nme in transposed:
            res = _adamw(tr(w_), grads[nme], tr(m_), tr(v_), "adamw_" + nme)
            upd[nme] = tuple(jnp.transpose(a)[None] for a in (grads[nme],) + res)
        elif nme in big:
            g_ = grads[nme].reshape(w_.shape)
            upd[nme] = (g_,) + _adamw(w_, g_, m_, v_, "adamw_" + nme)
        else:
            j = small.index(nme)
            upd[nme] = (grads[nme].reshape(w_.shape),) + tuple(unview(nme, r[j]) for r in res_small)
    grads = {nme: upd[nme][0] for nme in names}
    deltas, new_m, new_v = ([upd[nme][j] for nme in names] for j in (1, 2, 3))

    grad_x = gx.reshape(nb, s, D_MODEL)
    return (loss, grad_x, *[grads[nme] for nme in names], *deltas, *new_m, *new_v)
```

```python
import functools

import jax
import jax.numpy as jnp
import numpy as np
from jax import lax
from jax.experimental import pallas as pl
from jax.experimental.pallas import tpu as pltpu

F32 = jnp.float32
BF16 = jnp.bfloat16

D_MODEL = 1024
N_META = 16
HEADS = 4
NOPE = 128
ROPE = 64
VDIM = 128
QK_PAD = 256
Q_RANK = 256
KV_RANK = 128
CONV_W = 512
CONV_GROUP = 64
ROPE_THETA = 10000.0
EPS = 1e-6
ATTN_SCALE = (NOPE + ROPE) ** -0.5
IN_DIM = 3008
IN_PAD = 3072
BLK_ZA, BLK_CB, BLK_CC, BLK_CH, BLK_ZC = 1, 2, 3, 4, 5
NEG_INF = -1e30

ADAM_LR = 0.001
ADAM_B1 = 0.9
ADAM_B2 = 0.999
ADAM_EPS = 1e-08
ADAM_WD = 0.01
ADAM_STEP = 10

ROW_TILE = 512
ATTN_TILE = 256
VMEM_LIMIT = 56 * 1024 * 1024

NT = (((1,), (1,)), ((), ()))
TN = (((0,), (0,)), ((), ()))


def _cparams(*sem):
    return pltpu.CompilerParams(dimension_semantics=sem, vmem_limit_bytes=VMEM_LIMIT)


def _dot(a, b):
    return jnp.dot(a, b, preferred_element_type=F32)


def _dot_nt(a, b):
    return lax.dot_general(a, b, NT, preferred_element_type=F32)


def _dot_tn(a, b):
    return lax.dot_general(a, b, TN, preferred_element_type=F32)


def _rms(x, g):
    r = lax.rsqrt(jnp.mean(x * x, axis=-1, keepdims=True) + EPS)
    return x * r * g, r


def _rms_bwd(dy, x, r, g):
    xh = x * r
    dyg = dy * g
    dx = r * (dyg - xh * jnp.mean(dyg * xh, axis=-1, keepdims=True))
    return dx, dy * xh


def _sigmoid(z):
    return 1.0 / (1.0 + jnp.exp(-z))


def _rope(b, c, sa, sb):
    return b * c + pltpu.roll(b, 96, 1) * sa + pltpu.roll(b, 32, 1) * sb


def _rope_bwd(d, c, sa, sb):
    return d * c + pltpu.roll(d * sa, 32, 1) + pltpu.roll(d * sb, 96, 1)


def _group_mean(x, gmat):
    hi = x.astype(BF16)
    lo = (x - hi.astype(F32)).astype(BF16)
    return _dot(hi, gmat) + _dot(lo, gmat)


def _row_of(col, rows):
    return jnp.transpose(jnp.broadcast_to(col, (rows, 128)))[0:1, :]


def _rope_tables(n_pos):
    half = ROPE // 2
    inv_freq = (np.float32(1.0) / (np.float32(ROPE_THETA) ** (np.arange(half, dtype=np.float32) / np.float32(half))))
    ang = np.arange(n_pos, dtype=np.float32)[:, None] * inv_freq.astype(np.float32)[None, :]
    cos, sin = np.cos(ang).astype(np.float32), np.sin(ang).astype(np.float32)
    z = np.zeros((n_pos, half), np.float32)
    c = np.concatenate([cos, cos, z, z], axis=1)
    sa = np.concatenate([-sin, z, z, z], axis=1)
    sb = np.concatenate([z, sin, z, z], axis=1)
    return jnp.asarray(c), jnp.asarray(sa), jnp.asarray(sb)


W_IN_PIECES = ((0, 384, 752, 0, 64, 0), (384, 64, 752, 384, 448, 1), (448, 304, 752, 512, 512, 1))
W_Q_PIECES = ((0, 96, 256, 0, 0, 0), (96, 96, 256, 96, 96, 1))
W_KV_PIECES = ((0, 128, 128, 0, 0, 0), (128, 128, 128, 512, 512, 1))
W_OUT_PIECES = ((0, 128, 256, 0, 0, 0), (128, 128, 256, 128, 128, 1))


class _StagedGather:
    def __init__(self, pieces):
        self.pieces = pieces

    def scratch(self):
        nk, dma = len(self.pieces), pltpu.SemaphoreType.DMA
        return [dma((nk, 3)), dma((nk, 3)), dma((nk, 3)), dma((nk, 3)), dma((nk,))]

    def run(self, stage, src_ref, out_ref, scr):
        send_sems, recv_sems, fwd_send, fwd_recv, loc_sems = scr
        pieces = self.pieces
        nk = len(pieces)
        x, y, c = lax.axis_index("x"), lax.axis_index("y"), lax.axis_index("c")
        mine = 2 * x + y
        chips = [(1 - x, y), (x, 1 - y), (1 - x, 1 - y)]
        chip_of = [2 * px + py for px, py in chips]
        mesh = pl.DeviceIdType.MESH

        def src(k):
            s0, nr = pieces[k][0], pieces[k][1]
            return src_ref.at[s0:s0 + nr]

        def dst(k, q):
            _, nr, per, first, rest, _ = pieces[k]
            row = per * q + first + (rest - first) * jnp.minimum(q, 1)
            return out_ref.at[pl.ds(pl.multiple_of(row, 16), nr)]

        def ici(k, j, q):
            px, py = chips[j]
            return pltpu.make_async_remote_copy(
                src_ref=src(k), dst_ref=dst(k, q), send_sem=send_sems.at[k, j], recv_sem=recv_sems.at[k, j],
                device_id=(px, py, c), device_id_type=mesh)

        def fwd(k, j):
            ref = dst(k, chip_of[j])
            return pltpu.make_async_remote_copy(
                src_ref=ref, dst_ref=ref, send_sem=fwd_send.at[k, j], recv_sem=fwd_recv.at[k, j],
                device_id=(x, y, 1 - c), device_id_type=mesh)

        local = [pltpu.make_async_copy(src(k), dst(k, mine), loc_sems.at[k]) for k in range(nk)]
        if stage == 0:
            for cp in local:
                cp.start()
        if stage == 2:
            for cp in local:
                cp.wait()
        for half in (0, 1):
            @pl.when(c == half)
            def _(half=half):
                my_k = [k for k in range(nk) if pieces[k][5] == half]
                other_k = [k for k in range(nk) if pieces[k][5] != half]
                for k in my_k:
                    for j in range(3):
                        if stage == 0:
                            ici(k, j, mine).start()
                        elif stage == 1:
                            ici(k, j, chip_of[j]).wait_recv()
                            fwd(k, j).start()
                        else:
                            ici(k, j, mine).wait_send()
                            fwd(k, j).wait_send()
                if stage == 2:
                    for k in other_k:
                        for j in range(3):
                            fwd(k, j).wait_recv()


def _fwd_proj(x2d, meta, tabs, tabs_m, norm_g, w_in_p, q_norm_g, wq_p, kv_norm_g, wkv_p, w_out_shard, nb, s, tm):
    nt = s // tm
    n = nb * nt
    n_steps = n + 1
    c_t, sa_t, sb_t = tabs
    cm_t, sam_t, sbm_t = tabs_m
    gat = _StagedGather(W_OUT_PIECES)
    assert n_steps >= 3

    def body(x_ref, c_ref, sa_ref, sb_ref, mt_ref, cm_ref, sam_ref, sbm_ref,
             g_ref, w_ref, gq_ref, wq_ref, gkv_ref, wkv_ref, wos_ref,
             p_ref, q_ref, k_ref, v_ref, pm_ref, km_ref, vm_ref, wo_ref, *gat_scr):
        i = pl.program_id(0)
        for stage, at in enumerate((0, n_steps - 2, n_steps - 1)):
            @pl.when(i == at)
            def _(stage=stage):
                gat.run(stage, wos_ref, wo_ref, gat_scr)

        def project(xv, c, sa, sb, p_out, q_out, k_out, v_out):
            u, _ = _rms(xv, g_ref[...])
            p = _dot_nt(u.astype(BF16), w_ref[...])
            p_out[...] = p
            qn, _ = _rms(p[:, 0:Q_RANK], gq_ref[...])
            q = _dot_nt(qn.astype(BF16), wq_ref[...])
            kvn, _ = _rms(p[:, Q_RANK:Q_RANK + KV_RANK], gkv_ref[...])
            kv = _dot_nt(kvn.astype(BF16), wkv_ref[...])
            kpe = _rope(p[:, 384:512], c, sa, sb)
            for h in range(HEADS):
                if q_out is not None:
                    pe = _rope(q[:, QK_PAD * h + NOPE:QK_PAD * (h + 1)], c, sa, sb)
                    qh = jnp.concatenate([q[:, QK_PAD * h:QK_PAD * h + NOPE], pe], axis=1)
                    q_out[0, h] = (qh * ATTN_SCALE).astype(BF16)
                k_out[0, h] = jnp.concatenate([kv[:, NOPE * h:NOPE * (h + 1)], kpe], axis=1).astype(BF16)
                v_out[0, h] = kv[:, 512 + VDIM * h:512 + VDIM * (h + 1)].astype(BF16)

        @pl.when(i < n)
        def _():
            project(x_ref[...], c_ref[...], sa_ref[...], sb_ref[...], p_ref, q_ref, k_ref, v_ref)

        @pl.when(i == n)
        def _():
            project(mt_ref[...], cm_ref[...], sam_ref[...], sbm_ref[...], pm_ref, None, km_ref, vm_ref)

    cl = lambda i: jnp.minimum(i, n - 1)
    full = lambda a: pl.BlockSpec(a.shape, lambda i: (0,) * a.ndim)
    const = lambda shape: pl.BlockSpec(shape, lambda i: (0,) * len(shape))
    tab = pl.BlockSpec((tm, 128), lambda i: (cl(i) % nt, 0))
    hb = lambda w: pl.BlockSpec((1, HEADS, tm, w), lambda i: (cl(i) // nt, 0, cl(i) % nt, 0))
    hbm = pl.BlockSpec(memory_space=pl.ANY)
    return pl.pallas_call(
        body, name="fwd_proj", grid=(n_steps,),
        in_specs=[pl.BlockSpec((tm, D_MODEL), lambda i: (cl(i), 0)), tab, tab, tab,
                  full(meta), full(cm_t), full(sam_t), full(sbm_t),
                  full(norm_g), full(w_in_p), full(q_norm_g), full(wq_p), full(kv_norm_g), full(wkv_p), hbm],
        out_specs=[pl.BlockSpec((tm, IN_PAD), lambda i: (cl(i), 0)), hb(QK_PAD), hb(QK_PAD), hb(VDIM),
                   const((N_META, IN_PAD)), const((1, HEADS, N_META, QK_PAD)), const((1, HEADS, N_META, VDIM)), hbm],
        out_shape=[jax.ShapeDtypeStruct((nb * s, IN_PAD), F32),
                   jax.ShapeDtypeStruct((nb, HEADS, s, QK_PAD), BF16),
                   jax.ShapeDtypeStruct((nb, HEADS, s, QK_PAD), BF16),
                   jax.ShapeDtypeStruct((nb, HEADS, s, VDIM), BF16),
                   jax.ShapeDtypeStruct((N_META, IN_PAD), F32),
                   jax.ShapeDtypeStruct((1, HEADS, N_META, QK_PAD), BF16),
                   jax.ShapeDtypeStruct((1, HEADS, N_META, VDIM), BF16),
                   jax.ShapeDtypeStruct((D_MODEL, D_MODEL), BF16)],
        scratch_shapes=gat.scratch(),
        compiler_params=_cparams("arbitrary"),
    )(x2d, c_t, sa_t, sb_t, meta, cm_t, sam_t, sbm_t, norm_g, w_in_p, q_norm_g, wq_p, kv_norm_g, wkv_p, w_out_shard)


def _attn_fwd(q, k, v, km, vm, nb, s, tq):
    nq = s // tq

    def body(q_ref, k_ref, v_ref, km_ref, vm_ref, o_ref, lse_ref, s_scr, p_scr):
        row = lax.broadcasted_iota(jnp.int32, (tq, tq), 0)
        col = lax.broadcasted_iota(jnp.int32, (tq, tq), 1)
        def scores(i):
            slot = i % 2
            qi = q_ref[0, 0, i * tq:(i + 1) * tq, :]
            sm = _dot_nt(qi, km_ref[0, 0])
            m128 = None
            for j in range(i + 1):
                sc = _dot_nt(qi, k_ref[0, 0, j * tq:(j + 1) * tq, :])
                if j == i:
                    sc = jnp.where(col <= row, sc, NEG_INF)
                s_scr[slot, :, j * tq:(j + 1) * tq] = sc
                mx = sc[:, 0:128]
                for c0 in range(128, tq, 128):
                    mx = jnp.maximum(mx, sc[:, c0:c0 + 128])
                m128 = mx if m128 is None else jnp.maximum(m128, mx)
            return sm, jnp.maximum(jnp.max(m128, axis=1, keepdims=True), jnp.max(sm, axis=1, keepdims=True))

        def weighted_sum(i, pm, l):
            n = (i + 1) * tq
            acc = _dot(p_scr[i % 2, :, 0:n], v_ref[0, 0, 0:n, :]) + _dot(pm.astype(BF16), vm_ref[0, 0])
            o_ref[0, 0, i * tq:(i + 1) * tq, :] = acc / l

        nxt, pending = scores(0), None
        for i in range(nq):
            slot = i % 2
            sm, m = nxt
            if i + 1 < nq:
                nxt = scores(i + 1)
            pm = jnp.exp(sm - m)
            l128 = None
            for j in range(i + 1):
                p = jnp.exp(s_scr[slot, :, j * tq:(j + 1) * tq] - m)
                p_scr[slot, :, j * tq:(j + 1) * tq] = p.astype(BF16)
                ps = p[:, 0:128]
                for c0 in range(128, tq, 128):
                    ps = ps + p[:, c0:c0 + 128]
                l128 = ps if l128 is None else l128 + ps
            l = jnp.sum(l128, axis=1, keepdims=True) + jnp.sum(pm, axis=1, keepdims=True)
            lse_ref[0, 0, :, i * tq:(i + 1) * tq] = _row_of(m + jnp.log(l), tq)
            if pending is not None:
                weighted_sum(*pending)
            pending = (i, pm, l)
        weighted_sum(*pending)

    hblk = lambda w: pl.BlockSpec((1, 1, s, w), lambda b, h: (b, h, 0, 0))
    mblk = lambda w: pl.BlockSpec((1, 1, N_META, w), lambda b, h: (0, h, 0, 0))
    return pl.pallas_call(
        body, name="attn_fwd", grid=(nb, HEADS),
        in_specs=[hblk(QK_PAD), hblk(QK_PAD), hblk(VDIM), mblk(QK_PAD), mblk(VDIM)],
        out_specs=[hblk(VDIM), pl.BlockSpec((1, 1, 1, s), lambda b, h: (b, h, 0, 0))],
        out_shape=[jax.ShapeDtypeStruct((nb, HEADS, s, VDIM), F32),
                   jax.ShapeDtypeStruct((nb, HEADS, 1, s), F32)],
        scratch_shapes=[pltpu.VMEM((2, tq, s), F32), pltpu.VMEM((2, tq, s), BF16)],
        compiler_params=_cparams("parallel", "parallel"),
    )(q, k, v, km, vm)


def _shift_rows(a, prev, n_rows):
    rid = lax.broadcasted_iota(jnp.int32, a.shape, 0)
    a1 = jnp.where(rid == 0, prev[7:8, :], pltpu.roll(a, 1, 0))
    a2 = jnp.where(rid == 0, prev[6:7, :], jnp.where(rid == 1, prev[7:8, :], pltpu.roll(a, 2, 0)))
    return a1, a2


def _attn_gate(o, za, ga_h):
    on, r = _rms(o, ga_h)
    return on * (za * _sigmoid(za)), on, r


def _out_fwd_bwd(x2d, tgt2d, o, p, pm, conv_w, ga, gc, gmat, w_out, gf, nb, s, tm):
    nt = s // tm
    r = nb * s
    prev_idx = lambda i: jnp.maximum(i * (tm // 8) - 1, 0)

    def body(x_ref, t_ref, o_ref, za_ref, cb_ref, cc_ref, ch_ref, zc_ref, ccp_ref, chp_ref, mc_ref, mh_ref,
             cw_ref, ga_ref, gc_ref, gm_ref, w_ref, gf_ref,
             dh_ref, dy_ref, dw_ref, dgf_ref, loss_ref):
        i = pl.program_id(0)

        @pl.when(i == 0)
        def _():
            dw_ref[...] = jnp.zeros_like(dw_ref)
            dgf_ref[...] = jnp.zeros_like(dgf_ref)
            loss_ref[...] = jnp.zeros_like(loss_ref)

        ya = []
        for h in range(HEADS):
            y, _, _ = _attn_gate(o_ref[0, h], za_ref[:, VDIM * h:VDIM * (h + 1)],
                                 ga_ref[:, VDIM * h:VDIM * (h + 1)])
            ya.append(y)
        cc = cc_ref[...] * ch_ref[...]
        prev = jnp.where(i % nt == 0, mc_ref[8:16, :] * mh_ref[8:16, :], ccp_ref[...] * chp_ref[...])
        cc1, cc2 = _shift_rows(cc, prev, tm)
        yc = cb_ref[...] * (cw_ref[0:1, :] * cc2 + cw_ref[1:2, :] * cc1 + cw_ref[2:3, :] * cc)
        rg = lax.rsqrt(_group_mean(yc * yc, gm_ref[...]) + EPS)
        zc = zc_ref[...]
        yconv = yc * rg * gc_ref[...] * (zc * _sigmoid(zc))
        ycat = jnp.concatenate(ya + [yconv], axis=1).astype(BF16)
        h2 = x_ref[...] + _dot(ycat, w_ref[...])
        gfv = gf_ref[...]
        y, r2 = _rms(h2, gfv)
        e = y - t_ref[...]
        loss_ref[...] += 0.5 * jnp.sum(e * e) / D_MODEL
        dyv = e * (1.0 / D_MODEL)
        dh2, dgf = _rms_bwd(dyv, h2, r2, gfv)
        dgf_ref[...] += jnp.sum(dgf, axis=0, keepdims=True)
        dh_ref[...] = dh2
        dhb = dh2.astype(BF16)
        dy_ref[...] = _dot_nt(dhb, w_ref[...])
        dw_ref[...] += _dot_tn(ycat, dhb)

    row = lambda w, j: pl.BlockSpec((tm, w), lambda i: (i, j))
    pblk = lambda j: pl.BlockSpec((tm, 512), lambda i: (i, j))
    pprev = lambda j: pl.BlockSpec((8, 512), lambda i: (prev_idx(i), j))
    mblk = lambda j: pl.BlockSpec((N_META, 512), lambda i: (0, j))
    full = lambda a: pl.BlockSpec(a.shape, lambda i: (0,) * a.ndim)
    return pl.pallas_call(
        body, name="out_fwd_bwd", grid=(nb * nt,),
        in_specs=[row(D_MODEL, 0), row(D_MODEL, 0),
                  pl.BlockSpec((1, HEADS, tm, VDIM), lambda i: (i // nt, 0, i % nt, 0)),
                  pblk(BLK_ZA), pblk(BLK_CB), pblk(BLK_CC), pblk(BLK_CH), pblk(BLK_ZC),
                  pprev(BLK_CC), pprev(BLK_CH), mblk(BLK_CC), mblk(BLK_CH),
                  full(conv_w), full(ga), full(gc), full(gmat), full(w_out), full(gf)],
        out_specs=[row(D_MODEL, 0), row(D_MODEL, 0),
                   pl.BlockSpec((D_MODEL, D_MODEL), lambda i: (0, 0)),
                   pl.BlockSpec((1, D_MODEL), lambda i: (0, 0)),
                   pl.BlockSpec((1, 128), lambda i: (0, 0))],
        out_shape=[jax.ShapeDtypeStruct((r, D_MODEL), F32), jax.ShapeDtypeStruct((r, D_MODEL), F32),
                   jax.ShapeDtypeStruct((D_MODEL, D_MODEL), F32), jax.ShapeDtypeStruct((1, D_MODEL), F32),
                   jax.ShapeDtypeStruct((1, 128), F32)],
        compiler_params=_cparams("arbitrary"),
    )(x2d, tgt2d, o, p, p, p, p, p, p, p, pm, pm, conv_w, ga, gc, gmat, w_out, gf)


def _gate_bwd(dycat, o, p, pm, conv_w, ga, gc, gmat, nb, s, tm):
    nt = s // tm
    r = nb * s
    ext = tm + 8
    prev_idx = lambda i: jnp.maximum(i * (tm // 8) - 1, 0)
    next_idx = lambda i: jnp.minimum((i + 1) * (tm // 8), r // 8 - 1)

    def body(dya_ref, dyc_ref, dycn_ref, o_ref, za_ref, cb_ref, cbn_ref, cc_ref, ccp_ref, ccn_ref,
             ch_ref, chp_ref, chn_ref, zc_ref, zcn_ref, mc_ref, mh_ref, cw_ref, ga_ref, gc_ref, gm_ref,
             dpb_ref, do_ref, dl_ref, dccm_ref, dga_ref, dgc_ref, dcw_ref):
        i = pl.program_id(0)

        @pl.when(i == 0)
        def _():
            dga_ref[...] = jnp.zeros_like(dga_ref)
            dgc_ref[...] = jnp.zeros_like(dgc_ref)
            dcw_ref[...] = jnp.zeros_like(dcw_ref)

        dga = []
        for h in range(HEADS):
            hs = slice(VDIM * h, VDIM * (h + 1))
            oh, za, gah, dya = o_ref[0, h], za_ref[:, hs], ga_ref[:, hs], dya_ref[:, hs]
            sg = _sigmoid(za)
            on, ro = _rms(oh, gah)
            don = dya * (za * sg)
            dpb_ref[:, hs] = (dya * on * (sg * (1.0 + za * (1.0 - sg)))).astype(BF16)
            do, dg = _rms_bwd(don, oh, ro, gah)
            dga.append(jnp.sum(dg, axis=0, keepdims=True))
            dob = do.astype(BF16)
            do_ref[0, h] = dob
            dl_ref[0, h] = _row_of(jnp.sum(dob.astype(F32) * oh, axis=1, keepdims=True), tm)
        dga_ref[...] += jnp.concatenate(dga, axis=1)

        cat = lambda a, b: jnp.concatenate([a[...], b[...]], axis=0)
        cch = cat(cc_ref, ccn_ref)
        chh = cat(ch_ref, chn_ref)
        cb = cat(cb_ref, cbn_ref)
        zc = cat(zc_ref, zcn_ref)
        dy = cat(dyc_ref, dycn_ref)
        first = i % nt == 0
        last = i % nt == nt - 1
        cc = cch * chh
        prev = jnp.where(first, mc_ref[8:16, :] * mh_ref[8:16, :], ccp_ref[...] * chp_ref[...])
        cc1, cc2 = _shift_rows(cc, prev, ext)
        w0, w1, w2 = cw_ref[0:1, :], cw_ref[1:2, :], cw_ref[2:3, :]
        dw = w0 * cc2 + w1 * cc1 + w2 * cc
        yc = cb * dw
        rg = lax.rsqrt(_group_mean(yc * yc, gm_ref[...]) + EPS)
        ych = yc * rg
        gcv = gc_ref[...]
        sg = _sigmoid(zc)
        dycn = dy * (zc * sg)
        dzc = dy * (ych * gcv) * (sg * (1.0 + zc * (1.0 - sg)))
        dgc_ref[...] += jnp.sum((dycn * ych)[:tm], axis=0, keepdims=True)
        dycg = dycn * gcv
        dyc = rg * (dycg - ych * _group_mean(dycg * ych, gm_ref[...]))
        rid = lax.broadcasted_iota(jnp.int32, (ext, CONV_W), 0)
        ddw = jnp.where(jnp.logical_and(last, rid >= tm), 0.0, dyc * cb)
        dcb = dyc * dw
        dcc = w2 * ddw + w1 * pltpu.roll(ddw, ext - 1, 0) + w0 * pltpu.roll(ddw, ext - 2, 0)
        dpb_ref[:, 512:1024] = dcb[:tm].astype(BF16)
        dpb_ref[:, 1024:1536] = (dcc * chh)[:tm].astype(BF16)
        dpb_ref[:, 1536:2048] = (dcc * cch)[:tm].astype(BF16)
        dpb_ref[:, 2048:2560] = dzc[:tm].astype(BF16)
        rs = lambda a: jnp.sum(a[:tm], axis=0, keepdims=True)
        dcw_ref[0:1, :] += rs(ddw * cc2)
        dcw_ref[1:2, :] += rs(ddw * cc1)
        dcw_ref[2:3, :] += rs(ddw * cc)

        @pl.when(first)
        def _():
            d0, d1 = ddw[0:1, :], ddw[1:2, :]
            r8 = lax.broadcasted_iota(jnp.int32, (8, CONV_W), 0)
            dccm_ref[0] = jnp.where(r8 == 7, w1 * d0 + w0 * d1, jnp.where(r8 == 6, w0 * d0, 0.0))

    row = lambda j: pl.BlockSpec((tm, 512), lambda i: (i, j))
    prv = lambda j: pl.BlockSpec((8, 512), lambda i: (prev_idx(i), j))
    nxt = lambda j: pl.BlockSpec((8, 512), lambda i: (next_idx(i), j))
    mblk = lambda j: pl.BlockSpec((N_META, 512), lambda i: (0, j))
    full = lambda a: pl.BlockSpec(a.shape, lambda i: (0,) * a.ndim)
    hb = lambda w: pl.BlockSpec((1, HEADS, tm, w), lambda i: (i // nt, 0, i % nt, 0))
    acc = lambda rr: pl.BlockSpec((rr, 512), lambda i: (0, 0))
    return pl.pallas_call(
        body, name="gate_bwd", grid=(nb * nt,),
        in_specs=[row(0), row(1), nxt(1), hb(VDIM),
                  row(BLK_ZA), row(BLK_CB), nxt(BLK_CB), row(BLK_CC), prv(BLK_CC), nxt(BLK_CC),
                  row(BLK_CH), prv(BLK_CH), nxt(BLK_CH), row(BLK_ZC), nxt(BLK_ZC),
                  mblk(BLK_CC), mblk(BLK_CH), full(conv_w), full(ga), full(gc), full(gmat)],
        out_specs=[pl.BlockSpec((tm, 2560), lambda i: (i, 0)), hb(VDIM),
                   pl.BlockSpec((1, HEADS, 1, tm), lambda i: (i // nt, 0, 0, i % nt)),
                   pl.BlockSpec((1, 8, 512), lambda i: (i // nt, 0, 0)),
                   acc(1), acc(1), acc(8)],
        out_shape=[jax.ShapeDtypeStruct((r, 2560), BF16), jax.ShapeDtypeStruct((nb, HEADS, s, VDIM), BF16),
                   jax.ShapeDtypeStruct((nb, HEADS, 1, s), F32), jax.ShapeDtypeStruct((nb, 8, 512), F32),
                   jax.ShapeDtypeStruct((1, 512), F32), jax.ShapeDtypeStruct((1, 512), F32),
                   jax.ShapeDtypeStruct((8, 512), F32)],
        compiler_params=_cparams("arbitrary"),
    )(dycat, dycat, dycat, o, p, p, p, p, p, p, p, p, p, p, p, pm, pm, conv_w, ga, gc, gmat)


class _StagedReduce:
    LOC, PRE_S, PRE_R, ICI_S, ICI_R, POST_S, POST_R, OUT, N_SEM = 0, 1, 2, 3, 6, 9, 10, 11, 12

    def __init__(self, shard_shape):
        self.half = (shard_shape[0] // 2, shard_shape[1])

    def scratch(self):
        h = self.half
        return [pltpu.VMEM((4,) + h, F32), pltpu.VMEM((4,) + h, F32), pltpu.VMEM((4,) + h, BF16),
                pltpu.VMEM((3,) + h, BF16), pltpu.VMEM(h, F32), pltpu.SemaphoreType.DMA((self.N_SEM,))]

    def run(self, stage, pin, gout, scr):
        own, sib, wire, rbuf, fin, sems = scr
        r2 = self.half[0]
        x, y, c = lax.axis_index("x"), lax.axis_index("y"), lax.axis_index("c")
        mine = 2 * x + y
        sibling = (x, y, 1 - c)
        chips = [(1 - x, y), (x, 1 - y), (1 - x, 1 - y)]
        rows = lambda half: pl.ds(pl.multiple_of(half * r2, r2), r2)
        mesh = pl.DeviceIdType.MESH

        loc = pltpu.make_async_copy(pin.at[:, rows(c), :], own, sems.at[self.LOC])
        pre = pltpu.make_async_remote_copy(
            src_ref=pin.at[:, rows(1 - c), :], dst_ref=sib, send_sem=sems.at[self.PRE_S],
            recv_sem=sems.at[self.PRE_R], device_id=sibling, device_id_type=mesh)

        def ici(j):
            px, py = chips[j]
            return pltpu.make_async_remote_copy(
                src_ref=wire.at[2 * px + py], dst_ref=rbuf.at[j], send_sem=sems.at[self.ICI_S + j],
                recv_sem=sems.at[self.ICI_R + j], device_id=(px, py, c), device_id_type=mesh)

        def post(half):
            return pltpu.make_async_remote_copy(
                src_ref=fin, dst_ref=gout.at[rows(half), :], send_sem=sems.at[self.POST_S],
                recv_sem=sems.at[self.POST_R], device_id=sibling, device_id_type=mesh)

        keep = pltpu.make_async_copy(fin, gout.at[rows(c), :], sems.at[self.OUT])
        if stage == 0:
            loc.start()
            pre.start()
        elif stage == 1:
            loc.wait()
            pre.wait_recv()
            for blk in range(4):
                tot = own[blk] + sib[blk]
                own[blk] = tot
                wire[blk] = tot.astype(BF16)
            for j in range(3):
                ici(j).start()
        elif stage == 2:
            for j in range(3):
                ici(j).wait_recv()
            tot = own[mine]
            for j in range(3):
                tot = tot + rbuf[j].astype(F32)
            fin[...] = tot
            post(c).start()
            keep.start()
        else:
            post(1 - c).wait_recv()
            pre.wait_send()
            for j in range(3):
                ici(j).wait_send()
            post(c).wait_send()
            keep.wait()


def _attn_bwd(q, k, v, do, lse, delta, km, vm, early, nb, s, t):
    n = s // t
    ne = len(early)
    reds = [_StagedReduce(a.shape[1:]) for a in early]
    n_steps = HEADS * nb
    assert n_steps >= 4

    def body(q_ref, k_ref, v_ref, do_ref, lse_ref, dl_ref, km_ref, vm_ref, *rest):
        pin_refs, rest = rest[:ne], rest[ne:]
        dq_ref, dk_ref, dv_ref, dkm_ref, dvm_ref = rest[:5]
        gout_refs, (p_scr, ds_scr, dq_acc), red_scr = rest[5:5 + ne], rest[5 + ne:8 + ne], rest[8 + ne:]
        b = pl.program_id(1)
        step = pl.program_id(0) * nb + b
        for stage, at in enumerate((0, 1, n_steps - 2, n_steps - 1)):
            @pl.when(step == at)
            def _(stage=stage):
                for a, red in enumerate(reds):
                    red.run(stage, pin_refs[a], gout_refs[a], red_scr[6 * a:6 * a + 6])

        @pl.when(b == 0)
        def _():
            dkm_ref[...] = jnp.zeros_like(dkm_ref)
            dvm_ref[...] = jnp.zeros_like(dvm_ref)

        kr = lax.broadcasted_iota(jnp.int32, (t, t), 0)
        qc = lax.broadcasted_iota(jnp.int32, (t, t), 1)
        km_v, vm_v = km_ref[0, 0], vm_ref[0, 0]
        ptm = jnp.exp(_dot_nt(km_v, q_ref[0, 0]) - lse_ref[0, 0])
        dstm = (ptm * (_dot_nt(vm_v, do_ref[0, 0]) - dl_ref[0, 0])).astype(BF16)
        dkm_ref[0] += _dot(dstm, q_ref[0, 0])
        dvm_ref[0] += _dot(ptm.astype(BF16), do_ref[0, 0])
        dq_acc[...] = _dot_tn(dstm, km_v)
        def tiles(j):
            slot = j % 2
            kj = k_ref[0, 0, j * t:(j + 1) * t, :]
            vj = v_ref[0, 0, j * t:(j + 1) * t, :]
            def products(i):
                cs = slice(i * t, (i + 1) * t)
                return _dot_nt(kj, q_ref[0, 0, cs, :]), _dot_nt(vj, do_ref[0, 0, cs, :])

            nxt, pending = products(j), None
            for i in range(j, n):
                cs = slice(i * t, (i + 1) * t)
                st, dpt = nxt
                if i + 1 < n:
                    nxt = products(i + 1)
                if i == j:
                    st = jnp.where(kr <= qc, st, NEG_INF)
                pt = jnp.exp(st - lse_ref[0, 0, :, cs])
                dst = (pt * (dpt - dl_ref[0, 0, :, cs])).astype(BF16)
                p_scr[slot, :, cs] = pt.astype(BF16)
                ds_scr[slot, :, cs] = dst
                if pending is not None:
                    dq_acc[pending[0], :] += _dot_tn(pending[1], kj)
                pending = (cs, dst)
            dq_acc[pending[0], :] += _dot_tn(pending[1], kj)

        for j in range(n):
            slot = j % 2
            tiles(j)
            dv_ref[0, 0, j * t:(j + 1) * t, :] = _dot(p_scr[slot, :, j * t:s], do_ref[0, 0, j * t:s, :]).astype(BF16)
            dk_ref[0, 0, j * t:(j + 1) * t, :] = _dot(ds_scr[slot, :, j * t:s], q_ref[0, 0, j * t:s, :]).astype(BF16)
        dq_ref[0, 0] = dq_acc[...].astype(BF16)

    big = lambda w: pl.BlockSpec((1, 1, s, w), lambda h, b: (b, h, 0, 0))
    rowv = pl.BlockSpec((1, 1, 1, s), lambda h, b: (b, h, 0, 0))
    mk = lambda w: pl.BlockSpec((1, 1, N_META, w), lambda h, b: (0, h, 0, 0))
    mo = lambda w: pl.BlockSpec((1, N_META, w), lambda h, b: (h, 0, 0))
    return pl.pallas_call(
        body, name="attn_bwd", grid=(HEADS, nb),
        in_specs=[big(QK_PAD), big(QK_PAD), big(VDIM), big(VDIM), rowv, rowv, mk(QK_PAD), mk(VDIM)]
        + [pl.BlockSpec(memory_space=pl.ANY)] * ne,
        out_specs=[big(QK_PAD), big(QK_PAD), big(VDIM), mo(QK_PAD), mo(VDIM)]
        + [pl.BlockSpec(memory_space=pl.ANY)] * ne,
        out_shape=[jax.ShapeDtypeStruct((nb, HEADS, s, QK_PAD), BF16),
                   jax.ShapeDtypeStruct((nb, HEADS, s, QK_PAD), BF16),
                   jax.ShapeDtypeStruct((nb, HEADS, s, VDIM), BF16),
                   jax.ShapeDtypeStruct((HEADS, N_META, QK_PAD), F32),
                   jax.ShapeDtypeStruct((HEADS, N_META, VDIM), F32)]
        + [jax.ShapeDtypeStruct(a.shape[1:], F32) for a in early],
        scratch_shapes=[pltpu.VMEM((2, t, s), BF16), pltpu.VMEM((2, t, s), BF16), pltpu.VMEM((s, QK_PAD), F32)]
        + [sc for red in reds for sc in red.scratch()],
        compiler_params=_cparams("arbitrary", "arbitrary"),
    )(q, k, v, do, lse, delta, km, vm, *early)


def _up_bwd(dq, dk, dv, dkm, dvm, p, pm, tabs, tabs_m, wq_p, wkv_p, gq, gkv, nb, s, tm):
    nt = s // tm
    n = nb * nt
    c_t, sa_t, sb_t = tabs
    cm_t, sam_t, sbm_t = tabs_m

    def kv_path(dkh, dvh, pa, c, sa, sb, wkv, gkvv):
        dkpe = dkh[0][:, NOPE:]
        for h in range(1, HEADS):
            dkpe = dkpe + dkh[h][:, NOPE:]
        dkr = _rope_bwd(dkpe, c, sa, sb)
        dkv = jnp.concatenate([d[:, :NOPE] for d in dkh] + list(dvh), axis=1).astype(BF16)
        ckv = pa[:, Q_RANK:Q_RANK + KV_RANK]
        kvn, rkv = _rms(ckv, gkvv)
        dckv, dg = _rms_bwd(_dot(dkv, wkv), ckv, rkv, gkvv)
        return dckv, dkr, kvn.astype(BF16), dkv, jnp.sum(dg, axis=0, keepdims=True)

    def body(dq_ref, dk_ref, dv_ref, pa_ref, c_ref, sa_ref, sb_ref,
             dkm_ref, dvm_ref, pam_ref, cm_ref, sam_ref, sbm_ref,
             wq_ref, wkv_ref, gq_ref, gkv_ref,
             dpa_ref, dpam_ref, pq_ref, pkv_ref, dgq_ref, dgkv_ref, dwq_ref, dwkv_ref):
        i = pl.program_id(0)

        @pl.when(i == 0)
        def _():
            dwq_ref[...] = jnp.zeros_like(dwq_ref)
            dwkv_ref[...] = jnp.zeros_like(dwkv_ref)
            dgq_ref[...] = jnp.zeros_like(dgq_ref)
            dgkv_ref[...] = jnp.zeros_like(dgkv_ref)

        @pl.when(i < n)
        def _():
            c, sa, sb = c_ref[...], sa_ref[...], sb_ref[...]
            pa = pa_ref[...]
            parts = []
            for h in range(HEADS):
                dqh = dq_ref[0, h].astype(F32) * ATTN_SCALE
                parts += [dqh[:, :NOPE], _rope_bwd(dqh[:, NOPE:], c, sa, sb)]
            dql = jnp.concatenate(parts, axis=1).astype(BF16)
            cq = pa[:, 0:Q_RANK]
            gqv = gq_ref[...]
            qn, rq = _rms(cq, gqv)
            dwq_ref[...] += _dot_tn(dql, qn.astype(BF16))
            dcq, dg = _rms_bwd(_dot(dql, wq_ref[...]), cq, rq, gqv)
            dgq_ref[...] += jnp.sum(dg, axis=0, keepdims=True)
            dckv, dkr, kvn, dkv, dgk = kv_path([dk_ref[0, h].astype(F32) for h in range(HEADS)],
                                               [dv_ref[0, h].astype(F32) for h in range(HEADS)],
                                               pa, c, sa, sb, wkv_ref[...], gkv_ref[...])
            dwkv_ref[...] += _dot_tn(dkv, kvn)
            dgkv_ref[...] += dgk
            dpa_ref[...] = jnp.concatenate([dcq, dckv, dkr], axis=1).astype(BF16)

        @pl.when(i == n)
        def _():
            dckv, dkr, kvn, dkv, dgk = kv_path([dkm_ref[h] for h in range(HEADS)],
                                               [dvm_ref[h] for h in range(HEADS)],
                                               pam_ref[...], cm_ref[...], sam_ref[...], sbm_ref[...],
                                               wkv_ref[...], gkv_ref[...])
            dwkv_ref[...] += _dot_tn(dkv, kvn)
            dgkv_ref[...] += dgk
            dpam_ref[...] = jnp.concatenate([jnp.zeros((N_META, Q_RANK), F32), dckv, dkr], axis=1)
            for h in range(HEADS):
                pq_ref[h] = dwq_ref[QK_PAD * h:QK_PAD * h + NOPE + ROPE, :]
                pkv_ref[h, 0:NOPE, :] = dwkv_ref[NOPE * h:NOPE * (h + 1), :]
                pkv_ref[h, NOPE:NOPE + VDIM, :] = dwkv_ref[512 + VDIM * h:512 + VDIM * (h + 1), :]

    cl = lambda i: jnp.minimum(i, n - 1)
    hb = lambda w: pl.BlockSpec((1, HEADS, tm, w), lambda i: (cl(i) // nt, 0, cl(i) % nt, 0))
    tab = pl.BlockSpec((tm, 128), lambda i: (cl(i) % nt, 0))
    full = lambda a: pl.BlockSpec(a.shape, lambda i: (0,) * a.ndim)
    const = lambda shape: pl.BlockSpec(shape, lambda i: (0,) * len(shape))
    return pl.pallas_call(
        body, name="up_bwd", grid=(n + 1,),
        in_specs=[hb(QK_PAD), hb(QK_PAD), hb(VDIM), pl.BlockSpec((tm, 512), lambda i: (cl(i), 0)), tab, tab, tab,
                  full(dkm), full(dvm), pl.BlockSpec((N_META, 512), lambda i: (0, 0)),
                  full(cm_t), full(sam_t), full(sbm_t), full(wq_p), full(wkv_p), full(gq), full(gkv)],
        out_specs=[pl.BlockSpec((tm, 512), lambda i: (cl(i), 0)), const((N_META, 512)),
                   const((HEADS, NOPE + ROPE, Q_RANK)), const((HEADS, NOPE + VDIM, KV_RANK)),
                   const((1, Q_RANK)), const((1, KV_RANK))],
        out_shape=[jax.ShapeDtypeStruct((nb * s, 512), BF16), jax.ShapeDtypeStruct((N_META, 512), F32),
                   jax.ShapeDtypeStruct((HEADS, NOPE + ROPE, Q_RANK), F32),
                   jax.ShapeDtypeStruct((HEADS, NOPE + VDIM, KV_RANK), F32),
                   jax.ShapeDtypeStruct((1, Q_RANK), F32), jax.ShapeDtypeStruct((1, KV_RANK), F32)],
        scratch_shapes=[pltpu.VMEM((HEADS * QK_PAD, Q_RANK), F32), pltpu.VMEM((1024, KV_RANK), F32)],
        compiler_params=_cparams("arbitrary"),
    )(dq, dk, dv, p, c_t, sa_t, sb_t, dkm, dvm, pm, cm_t, sam_t, sbm_t, wq_p, wkv_p, gq, gkv)


def _in_bwd(x2d, dh2, dpa, dpb, meta, dpam, dccm, pm, w_in_p, norm_g, nb, s, tm):
    nt = s // tm
    n = nb * nt

    def body(x_ref, dh_ref, dpa_ref, dpb_ref, mt_ref, dpam_ref, dccm_ref, mc_ref, mh_ref, w_ref, g_ref,
             gx_ref, gm_ref, dw_hbm, dg_ref, acc_ref, sems):
        i = pl.program_id(0)

        @pl.when(i == 0)
        def _():
            acc_ref[...] = jnp.zeros_like(acc_ref)
            dg_ref[...] = jnp.zeros_like(dg_ref)

        def rows(x, dp, dres):
            g = g_ref[...]
            dpb16 = dp.astype(BF16)
            du = _dot(dpb16, w_ref[...])
            u, r1 = _rms(x, g)
            acc_ref[...] += _dot_tn(dpb16, u.astype(BF16))
            dx, dg = _rms_bwd(du, x, r1, g)
            dg_ref[...] += jnp.sum(dg, axis=0, keepdims=True)
            return dx if dres is None else dx + dres

        @pl.when(i < n)
        def _():
            dp = jnp.concatenate([dpa_ref[...], dpb_ref[...]], axis=1)
            gx_ref[...] = rows(x_ref[...], dp, dh_ref[...])

        @pl.when(i == n)
        def _():
            dcc = dccm_ref[0]
            for b in range(1, nb):
                dcc = dcc + dccm_ref[b]
            z8 = jnp.zeros((8, CONV_W), F32)
            dc = jnp.concatenate([z8, dcc * mh_ref[8:16, :]], axis=0)
            dh = jnp.concatenate([z8, dcc * mc_ref[8:16, :]], axis=0)
            z = jnp.zeros((N_META, CONV_W), F32)
            dp = jnp.concatenate([dpam_ref[...], z, z, dc, dh, z], axis=1)
            gm_ref[...] = rows(mt_ref[...], dp, None)
            per = IN_DIM // 4
            cps = [pltpu.make_async_copy(acc_ref.at[0:448], dw_hbm.at[0, 0:448], sems.at[0]),
                   pltpu.make_async_copy(acc_ref.at[512:per + 64], dw_hbm.at[0, 448:per], sems.at[1])]
            for qq in range(1, 4):
                cps.append(pltpu.make_async_copy(acc_ref.at[per * qq + 64:per * (qq + 1) + 64], dw_hbm.at[qq],
                                                 sems.at[qq + 1]))
            for cp in cps:
                cp.start()
            for cp in cps:
                cp.wait()

    cl = lambda i: jnp.minimum(i, n - 1)
    row = lambda w: pl.BlockSpec((tm, w), lambda i: (cl(i), 0))
    full = lambda a: pl.BlockSpec(a.shape, lambda i: (0,) * a.ndim)
    mblk = lambda j: pl.BlockSpec((N_META, 512), lambda i: (0, j))
    return pl.pallas_call(
        body, name="in_bwd", grid=(n + 1,),
        in_specs=[row(D_MODEL), row(D_MODEL), row(512), row(2560), full(meta), full(dpam), full(dccm),
                  mblk(BLK_CC), mblk(BLK_CH), full(w_in_p), full(norm_g)],
        out_specs=[row(D_MODEL), pl.BlockSpec((N_META, D_MODEL), lambda i: (0, 0)),
                   pl.BlockSpec(memory_space=pl.ANY), pl.BlockSpec((1, D_MODEL), lambda i: (0, 0))],
        out_shape=[jax.ShapeDtypeStruct((nb * s, D_MODEL), F32), jax.ShapeDtypeStruct((N_META, D_MODEL), F32),
                   jax.ShapeDtypeStruct((4, IN_DIM // 4, D_MODEL), F32), jax.ShapeDtypeStruct((1, D_MODEL), F32)],
        scratch_shapes=[pltpu.VMEM((IN_PAD, D_MODEL), F32), pltpu.SemaphoreType.DMA((5,))],
        compiler_params=_cparams("arbitrary"),
    )(x2d, dh2, dpa, dpb, meta, dpam, dccm, pm, pm, w_in_p, norm_g)


def _gather_weights(split, pieces, out_rows, whole, zero_fills):
    ns, nw, nz = len(split), len(whole), len(zero_fills)
    flat = [(a, pc) for a in range(ns) for pc in pieces[a]]
    nk = len(flat)

    def body(*refs):
        ins, wins, zins = refs[:ns], refs[ns:ns + nw], refs[ns + nw:ns + nw + nz]
        outs, wouts = refs[ns + nw + nz:2 * ns + nw + nz], refs[2 * ns + nw + nz:2 * (ns + nw) + nz]
        send_sems, recv_sems, fwd_send, fwd_recv, loc_sems, w_send, w_recv, w_loc, z_sems = refs[2 * (ns + nw) + nz:]
        x, y, c = lax.axis_index("x"), lax.axis_index("y"), lax.axis_index("c")
        mine = 2 * x + y
        chips = [(1 - x, y), (x, 1 - y), (1 - x, 1 - y)]
        chip_of = [2 * px + py for px, py in chips]

        def src(k):
            a, (s0, nr, _, _, _, _) = flat[k]
            return ins[a].at[s0:s0 + nr]

        def dst(k, q):
            a, (_, nr, per, first, rest, _) = flat[k]
            row = per * q + first + (rest - first) * jnp.minimum(q, 1)
            return outs[a].at[pl.ds(pl.multiple_of(row, 16), nr)]

        def ici(k, j, q):
            px, py = chips[j]
            return pltpu.make_async_remote_copy(
                src_ref=src(k), dst_ref=dst(k, q), send_sem=send_sems.at[k, j], recv_sem=recv_sems.at[k, j],
                device_id=(px, py, c), device_id_type=pl.DeviceIdType.MESH)

        def fwd(k, j):
            ref = dst(k, chip_of[j])
            return pltpu.make_async_remote_copy(
                src_ref=ref, dst_ref=ref, send_sem=fwd_send.at[k, j], recv_sem=fwd_recv.at[k, j],
                device_id=(x, y, 1 - c), device_id_type=pl.DeviceIdType.MESH)

        def wcopy(b, j, q):
            px, py = chips[j]
            return pltpu.make_async_remote_copy(
                src_ref=wins[b], dst_ref=wouts[b].at[q], send_sem=w_send.at[b, j], recv_sem=w_recv.at[b, j],
                device_id=(px, py, c), device_id_type=pl.DeviceIdType.MESH)

        local = [pltpu.make_async_copy(src(k), dst(k, mine), loc_sems.at[k]) for k in range(nk)]
        local += [pltpu.make_async_copy(wins[b], wouts[b].at[mine], w_loc.at[b]) for b in range(nw)]
        for z, (a, _, row0) in enumerate(zero_fills):
            local.append(pltpu.make_async_copy(zins[z], outs[a].at[row0:row0 + zins[z].shape[0]], z_sems.at[z]))
        wsends = [wcopy(b, j, mine) for b in range(nw) for j in range(3)]
        for cp in local + wsends:
            cp.start()

        for half in (0, 1):
            @pl.when(c == half)
            def _(half=half):
                my_k = [k for k in range(nk) if flat[k][1][5] == half]
                other_k = [k for k in range(nk) if flat[k][1][5] != half]
                sends = [ici(k, j, mine) for k in my_k for j in range(3)]
                for cp in sends:
                    cp.start()
                passed = []
                for k in my_k:
                    for j in range(3):
                        ici(k, j, chip_of[j]).wait_recv()
                        cp = fwd(k, j)
                        cp.start()
                        passed.append(cp)
                for k in other_k:
                    for j in range(3):
                        fwd(k, j).wait_recv()
                for cp in sends + passed:
                    cp.wait_send()

        for b in range(nw):
            for j in range(3):
                wcopy(b, j, chip_of[j]).wait_recv()
        for cp in wsends:
            cp.wait_send()
        for cp in local:
            cp.wait()

    hbm = pl.BlockSpec(memory_space=pl.ANY)
    dma = pltpu.SemaphoreType.DMA
    zeros = [z for _, z, _ in zero_fills]
    return pl.pallas_call(
        body, name="gather_weights",
        in_specs=[hbm] * (ns + nw + nz), out_specs=[hbm] * (ns + nw),
        out_shape=([jax.ShapeDtypeStruct((out_rows[a], split[a].shape[1]), split[a].dtype) for a in range(ns)]
                   + [jax.ShapeDtypeStruct((4,) + w.shape, w.dtype) for w in whole]),
        scratch_shapes=[dma((nk, 3)), dma((nk, 3)), dma((nk, 3)), dma((nk, 3)), dma((nk,)),
                        dma((nw, 3)), dma((nw, 3)), dma((nw,)), dma((nz,))],
        compiler_params=pltpu.CompilerParams(vmem_limit_bytes=VMEM_LIMIT),
    )(*split, *whole, *zeros)


def _reduce_grads(parts, small):
    n = len(parts)
    shapes = [a.shape[1:] for a in parts]
    halves = [(sh[0] // 2, sh[1]) for sh in shapes]

    def body(*refs):
        pin, sm_in = refs[:n], refs[n]
        gout, sm_out = refs[n + 1:2 * n + 1], refs[2 * n + 1]
        scr = refs[2 * n + 2:]
        own, sib, wire, rbuf = scr[:n], scr[n:2 * n], scr[2 * n:3 * n], scr[3 * n:4 * n]
        (sbuf, send_sems, recv_sems, loc_sems, pre_send, pre_recv, post_send, post_recv,
         sm_send, sm_recv) = scr[4 * n:]
        x, y, c = lax.axis_index("x"), lax.axis_index("y"), lax.axis_index("c")
        mine = 2 * x + y
        me = 4 * x + 2 * y + c
        sibling = (x, y, 1 - c)
        chips = [(1 - x, y), (x, 1 - y), (1 - x, 1 - y)]

        def rows(a, half):
            r2 = halves[a][0]
            return pl.ds(pl.multiple_of(half * r2, r2), r2)

        chip_of = [2 * px + py for px, py in chips]
        blocks = chip_of + [mine]

        def pre(a, k):
            return pltpu.make_async_remote_copy(
                src_ref=pin[a].at[blocks[k], rows(a, 1 - c), :], dst_ref=sib[a].at[blocks[k]],
                send_sem=pre_send.at[a, k], recv_sem=pre_recv.at[a, k], device_id=sibling,
                device_id_type=pl.DeviceIdType.MESH)

        def ici(a, j):
            px, py = chips[j]
            return pltpu.make_async_remote_copy(
                src_ref=wire[a].at[2 * px + py], dst_ref=rbuf[a].at[j], send_sem=send_sems.at[a, j],
                recv_sem=recv_sems.at[a, j], device_id=(px, py, c), device_id_type=pl.DeviceIdType.MESH)

        def post(a, half):
            ref = gout[a].at[rows(a, half), :]
            return pltpu.make_async_remote_copy(
                src_ref=ref, dst_ref=ref, send_sem=post_send.at[a], recv_sem=post_recv.at[a],
                device_id=sibling, device_id_type=pl.DeviceIdType.MESH)

        def small_copy(kk):
            peer = (x ^ (kk >> 2), y ^ ((kk >> 1) & 1), c ^ (kk & 1))
            return pltpu.make_async_remote_copy(
                src_ref=sm_in, dst_ref=sbuf.at[kk], send_sem=sm_send.at[kk - 1], recv_sem=sm_recv.at[kk - 1],
                device_id=peer, device_id_type=pl.DeviceIdType.MESH)

        local = [[pltpu.make_async_copy(pin[a].at[blocks[k], rows(a, c), :], own[a].at[blocks[k]], loc_sems.at[a, k])
                  for k in range(4)] for a in range(n)]
        pres = [[pre(a, k) for k in range(4)] for a in range(n)]
        smalls = [small_copy(kk) for kk in range(1, 8)]
        for a in range(n):
            for k in range(4):
                local[a][k].start()
                pres[a][k].start()
        for cp in smalls:
            cp.start()
        sbuf[0] = sm_in[...]
        sends = []
        for a in range(n):
            for k in range(4):
                local[a][k].wait()
                pres[a][k].wait_recv()
                tot = own[a][blocks[k]] + sib[a][blocks[k]]
                own[a][blocks[k]] = tot
                if k < 3:
                    wire[a][blocks[k]] = tot.astype(BF16)
                    cp = ici(a, k)
                    cp.start()
                    sends.append(cp)
        for cp in smalls:
            cp.wait_recv()
        total = sbuf[me]
        for d in range(1, 8):
            total = total + sbuf[me ^ d]
        sm_out[...] = total
        posts = []
        for a in range(n):
            for j in range(3):
                ici(a, j).wait_recv()
            fin = own[a][mine]
            for j in range(3):
                fin = fin + rbuf[a][j].astype(F32)
            gout[a][rows(a, c), :] = fin
            cp = post(a, c)
            cp.start()
            posts.append(cp)
        for a in range(n):
            post(a, 1 - c).wait_recv()
        for cp in [cp for row in pres for cp in row] + sends + smalls + posts:
            cp.wait_send()

    hbm = pl.BlockSpec(memory_space=pl.ANY)
    vmem = pl.BlockSpec(memory_space=pltpu.VMEM)
    dma = pltpu.SemaphoreType.DMA
    return pl.pallas_call(
        body, name="reduce_grads",
        in_specs=[hbm] * n + [vmem], out_specs=[vmem] * (n + 1),
        out_shape=[jax.ShapeDtypeStruct(sh, F32) for sh in shapes] + [jax.ShapeDtypeStruct(small.shape, F32)],
        scratch_shapes=([pltpu.VMEM((4,) + hs, F32) for hs in halves] + [pltpu.VMEM((4,) + hs, F32) for hs in halves]
                        + [pltpu.VMEM((4,) + hs, BF16) for hs in halves]
                        + [pltpu.VMEM((3,) + hs, BF16) for hs in halves]
                        + [pltpu.VMEM((8,) + small.shape, F32), dma((n, 3)), dma((n, 3)), dma((n, 4)),
                           dma((n, 4)), dma((n, 4)), dma((n,)), dma((n,)), dma((7,)), dma((7,))]),
        compiler_params=pltpu.CompilerParams(vmem_limit_bytes=VMEM_LIMIT),
    )(*parts, small)


def _adamw_update(w_ref, g_ref, m_ref, v_ref, d_ref, nm_ref, nv_ref):
    gv = g_ref[...]
    nm = ADAM_B1 * m_ref[...] + (1.0 - ADAM_B1) * gv
    nv = ADAM_B2 * v_ref[...] + (1.0 - ADAM_B2) * (gv * gv)
    m_hat = nm / (1.0 - ADAM_B1 ** ADAM_STEP)
    v_hat = nv / (1.0 - ADAM_B2 ** ADAM_STEP)
    d_ref[...] = -ADAM_LR * (m_hat / (jnp.sqrt(v_hat) + ADAM_EPS) + ADAM_WD * w_ref[...])
    nm_ref[...] = nm
    nv_ref[...] = nv


def _adamw_small(ws, gs, ms, vs):
    k = len(ws)

    def body(*refs):
        ins, outs = refs[:4 * k], refs[4 * k:]
        for a in range(k):
            _adamw_update(ins[a], ins[k + a], ins[2 * k + a], ins[3 * k + a], outs[a], outs[k + a], outs[2 * k + a])

    out = pl.pallas_call(
        body, name="adamw_small",
        out_shape=[jax.ShapeDtypeStruct(w.shape, F32) for w in ws] * 3,
        compiler_params=pltpu.CompilerParams(vmem_limit_bytes=VMEM_LIMIT),
    )(*ws, *gs, *ms, *vs)
    return out[:k], out[k:2 * k], out[2 * k:]


def _adamw(w, g, m, v, name):
    shape = w.shape
    w2, g2, m2, v2 = (a.reshape((-1, shape[-1])) for a in (w, g, m, v))

    def body(w_ref, g_ref, m_ref, v_ref, d_ref, nm_ref, nv_ref):
        _adamw_update(w_ref, g_ref, m_ref, v_ref, d_ref, nm_ref, nv_ref)

    rows, cols = w2.shape
    nblk = cols // 256 if cols % 256 == 0 and rows >= 64 else 1
    blk = pl.BlockSpec((rows, cols // nblk), lambda j: (0, j))
    out = pl.pallas_call(
        body, name=name, grid=(nblk,), in_specs=[blk] * 4, out_specs=[blk] * 3,
        out_shape=[jax.ShapeDtypeStruct(w2.shape, F32)] * 3,
        compiler_params=_cparams("parallel"),
    )(w2, g2, m2, v2)
    return tuple(a.reshape(shape) for a in out)


def kernel(x, meta_tokens, norm_g, w_in, q_norm_g, w_q_up, kv_norm_g, w_kv_up, conv_w, attn_out_g, conv_out_g, w_out, final_norm_g, loss_target, m_meta_tokens, m_norm_g, m_w_in, m_q_norm_g, m_w_q_up, m_kv_norm_g, m_w_kv_up, m_conv_w, m_attn_out_g, m_conv_out_g, m_w_out, m_final_norm_g, v_meta_tokens, v_norm_g, v_w_in, v_q_norm_g, v_w_q_up, v_kv_norm_g, v_w_kv_up, v_conv_w, v_attn_out_g, v_conv_out_g, v_w_out, v_final_norm_g):
    nb, s, _ = x.shape
    tm = min(ROW_TILE, s)
    ta = min(ATTN_TILE, s)
    assert s % tm == 0 and s % ta == 0 and tm % 16 == 0
    r = nb * s

    tr = lambda a: jnp.transpose(a[0])
    w_in_p, wq_p, wkv_p, g_cw, g_meta = _gather_weights(
        [tr(w_in).astype(BF16), tr(w_q_up).astype(BF16), tr(w_kv_up).astype(BF16)],
        [W_IN_PIECES, W_Q_PIECES, W_KV_PIECES], [IN_PAD, HEADS * QK_PAD, 1024],
        [jnp.transpose(conv_w, (1, 0, 2)), meta_tokens],
        [(0, jnp.zeros((64, D_MODEL), BF16), 448)]
        + [(1, jnp.zeros((64, Q_RANK), BF16), QK_PAD * h + NOPE + ROPE) for h in range(HEADS)])
    conv_f = jnp.transpose(g_cw[:, :, 0, :], (1, 0, 2)).reshape(3, CONV_W)
    meta_f = jnp.transpose(g_meta, (1, 0, 2)).reshape(N_META, D_MODEL)

    c_all, sa_all, sb_all = _rope_tables(N_META + s)
    tabs_m = (c_all[:N_META], sa_all[:N_META], sb_all[:N_META])
    tabs = (c_all[N_META:], sa_all[N_META:], sb_all[N_META:])
    gid = np.arange(CONV_W) // CONV_GROUP
    gmat = jnp.asarray(np.where(gid[:, None] == gid[None, :], 1.0 / CONV_GROUP, 0.0), BF16)
    ga, gc = attn_out_g, conv_out_g
    gf = final_norm_g.reshape(1, D_MODEL)

    x2d = x.reshape(r, D_MODEL)
    tgt2d = loss_target.reshape(r, D_MODEL)

    p, q, k, v, pm, km, vm, w_out_f = _fwd_proj(x2d, meta_f, tabs, tabs_m, norm_g, w_in_p, q_norm_g, wq_p,
                                                kv_norm_g, wkv_p, w_out[0].astype(BF16), nb, s, tm)
    o, lse = _attn_fwd(q, k, v, km, vm, nb, s, ta)
    dh2, dycat, dw_out, dgf, loss_acc = _out_fwd_bwd(x2d, tgt2d, o, p, pm, conv_f, ga, gc, gmat, w_out_f, gf,
                                                     nb, s, tm)
    dpb, do, delta, dccm, dga, dgc, dcw = _gate_bwd(dycat, o, p, pm, conv_f, ga, gc, gmat, nb, s, tm)
    p_out = dw_out.reshape(4, D_MODEL // 4, D_MODEL)
    dq, dk, dv, dkm, dvm, g_w_out = _attn_bwd(q, k, v, do, lse, delta, km, vm, [p_out], nb, s, ta)
    dpa, dpam, p_q, p_kv, dgq, dgkv = _up_bwd(dq, dk, dv, dkm, dvm, p, pm, tabs, tabs_m, wq_p, wkv_p,
                                              q_norm_g, kv_norm_g, nb, s, tm)
    gx, gmeta, p_in, dng = _in_bwd(x2d, dh2, dpa, dpb, meta_f, dpam, dccm, pm, w_in_p, norm_g, nb, s, tm)

    flat =jnp.concatenate([dng.reshape(-1), dgq.reshape(-1), dgkv.reshape(-1), dga.reshape(-1), dgc.reshape(-1),
                            dgf.reshape(-1), dcw[:3].reshape(-1), gmeta.reshape(-1), loss_acc[0, 0:1]])
    n_small = flat.shape[0]
    rows_small = -(-n_small // 1024) * 8
    small = jnp.pad(flat, (0, rows_small * 128 - n_small)).reshape(rows_small, 128)
    g_w_in_t, g_w_q_t, g_w_kv_t, small_sum = _reduce_grads([p_in, p_q, p_kv], small)
    ssum = small_sum.reshape(-1)

    def take(off, n):
        return ssum[off:off + n], off + n

    off = 0
    g_norm, off = take(off, D_MODEL)
    g_qn, off = take(off, Q_RANK)
    g_kvn, off = take(off, KV_RANK)
    g_ga, off = take(off, CONV_W)
    g_gc, off = take(off, CONV_W)
    g_gf, off = take(off, D_MODEL)
    g_cw_all, off = take(off, 3 * CONV_W)
    g_meta_all, off = take(off, N_META * D_MODEL)
    loss = ssum[off]
    chip = 2 * lax.axis_index("x") + lax.axis_index("y")
    g_conv = lax.dynamic_slice(g_cw_all.reshape(3, CONV_W), (0, chip * 128), (3, 128))
    g_mt = lax.dynamic_slice(g_meta_all.reshape(N_META, D_MODEL), (0, chip * 256), (N_META, 256))

    grads = {
        "meta_tokens": g_mt, "norm_g": g_norm.reshape(1, -1), "w_in": g_w_in_t, "q_norm_g": g_qn.reshape(1, -1),
        "w_q_up": g_w_q_t, "kv_norm_g": g_kvn.reshape(1, -1), "w_kv_up": jnp.transpose(g_w_kv_t)[None],
        "conv_w": g_conv[None], "attn_out_g": g_ga.reshape(1, -1), "conv_out_g": g_gc.reshape(1, -1),
        "w_out": g_w_out[None], "final_norm_g": g_gf,
    }
    transposed = ("w_in", "w_q_up")
    weights = {
        "meta_tokens": (meta_tokens, m_meta_tokens, v_meta_tokens), "norm_g": (norm_g, m_norm_g, v_norm_g),
        "w_in": (w_in, m_w_in, v_w_in), "q_norm_g": (q_norm_g, m_q_norm_g, v_q_norm_g),
        "w_q_up": (w_q_up, m_w_q_up, v_w_q_up), "kv_norm_g": (kv_norm_g, m_kv_norm_g, v_kv_norm_g),
        "w_kv_up": (w_kv_up, m_w_kv_up, v_w_kv_up), "conv_w": (conv_w, m_conv_w, v_conv_w),
        "attn_out_g": (attn_out_g, m_attn_out_g, v_attn_out_g), "conv_out_g": (conv_out_g, m_conv_out_g, v_conv_out_g),
        "w_out": (w_out, m_w_out, v_w_out), "final_norm_g": (final_norm_g, m_final_norm_g, v_final_norm_g),
    }
    names = list(weights)
    small = [nme for nme in names if nme != "w_in"]

    def view(nme, a):
        if nme in transposed:
            return a if a.ndim == 2 else tr(a)
        if nme == "conv_w":
            return jnp.transpose(a.reshape(1, 3, -1), (1, 0, 2))
        if a.ndim == 3:
            return a[0]
        return a.reshape(1, -1) if a.ndim == 1 else a

    def unview(nme, a):
        if nme in transposed:
            return jnp.transpose(a)[None]
        if nme == "conv_w":
            return jnp.transpose(a, (1, 0, 2))
        return a.reshape(weights[nme][0].shape)

    res_small = _adamw_small(*[[view(nme, a) for nme, a in zip(small, col)] for col in (
        [weights[nme][0] for nme in small], [grads[nme] for nme in small],
        [weights[nme][1] for nme in small], [weights[nme][2] for nme in small])])
    w_, m_, v_ = weights["w_in"]
    res = _adamw(tr(w_), grads["w_in"], tr(m_), tr(v_), "adamw_w_in")
    upd = {"w_in": tuple(jnp.transpose(a)[None] for a in (grads["w_in"],) + res)}
    for j, nme in enumerate(small):
        upd[nme] = (unview(nme, view(nme, grads[nme])),) + tuple(unview(nme, r[j]) for r in res_small)
    grads = {nme: upd[nme][0] for nme in names}
    deltas, new_m, new_v = ([upd[nme][j] for nme in names] for j in (1, 2, 3))

    grad_x = gx.reshape(nb, s, D_MODEL)
    return (loss, grad_x, *[grads[nme] for nme in names], *deltas, *new_m, *new_v)
```

```python
import functools

import jax
import jax.numpy as jnp
import numpy as np
from jax import lax
from jax.experimental import pallas as pl
from jax.experimental.pallas import tpu as pltpu

F32 = jnp.float32
BF16 = jnp.bfloat16

D_MODEL = 1024
N_META = 16
HEADS = 4
NOPE = 128
ROPE = 64
VDIM = 128
QK_PAD = 256
Q_RANK = 256
KV_RANK = 128
CONV_W = 512
CONV_GROUP = 64
ROPE_THETA = 10000.0
EPS = 1e-6
ATTN_SCALE = (NOPE + ROPE) ** -0.5
IN_DIM = 3008
IN_PAD = 3072
BLK_ZA, BLK_CB, BLK_CC, BLK_CH, BLK_ZC = 1, 2, 3, 4, 5
NEG_INF = -1e30

ADAM_LR = 0.001
ADAM_B1 = 0.9
ADAM_B2 = 0.999
ADAM_EPS = 1e-08
ADAM_WD = 0.01
ADAM_STEP = 10

ROW_TILE = 512
ATTN_TILE = 256
VMEM_LIMIT = 56 * 1024 * 1024

NT = (((1,), (1,)), ((), ()))
TN = (((0,), (0,)), ((), ()))


def _cparams(*sem):
    return pltpu.CompilerParams(dimension_semantics=sem, vmem_limit_bytes=VMEM_LIMIT)


def _dot(a, b):
    return jnp.dot(a, b, preferred_element_type=F32)


def _dot_nt(a, b):
    return lax.dot_general(a, b, NT, preferred_element_type=F32)


def _dot_tn(a, b):
    return lax.dot_general(a, b, TN, preferred_element_type=F32)


def _rms(x, g):
    r = lax.rsqrt(jnp.mean(x * x, axis=-1, keepdims=True) + EPS)
    return x * r * g, r


def _rms_bwd(dy, x, r, g):
    xh = x * r
    dyg = dy * g
    dx = r * (dyg - xh * jnp.mean(dyg * xh, axis=-1, keepdims=True))
    return dx, dy * xh


def _sigmoid(z):
    return 1.0 / (1.0 + jnp.exp(-z))


def _rope(b, c, sa, sb):
    return b * c + pltpu.roll(b, 96, 1) * sa + pltpu.roll(b, 32, 1) * sb


def _rope_bwd(d, c, sa, sb):
    return d * c + pltpu.roll(d * sa, 32, 1) + pltpu.roll(d * sb, 96, 1)


def _group_mean(x, gmat):
    hi = x.astype(BF16)
    lo = (x - hi.astype(F32)).astype(BF16)
    return _dot(hi, gmat) + _dot(lo, gmat)


def _row_of(col, rows):
    return jnp.transpose(jnp.broadcast_to(col, (rows, 128)))[0:1, :]


def _rope_tables(n_pos):
    half = ROPE // 2
    inv_freq = (np.float32(1.0) / (np.float32(ROPE_THETA) ** (np.arange(half, dtype=np.float32) / np.float32(half))))
    ang = np.arange(n_pos, dtype=np.float32)[:, None] * inv_freq.astype(np.float32)[None, :]
    cos, sin = np.cos(ang).astype(np.float32), np.sin(ang).astype(np.float32)
    z = np.zeros((n_pos, half), np.float32)
    c = np.concatenate([cos, cos, z, z], axis=1)
    sa = np.concatenate([-sin, z, z, z], axis=1)
    sb = np.concatenate([z, sin, z, z], axis=1)
    return jnp.asarray(c), jnp.asarray(sa), jnp.asarray(sb)


W_IN_PIECES = ((0, 384, 752, 0, 64, 0), (384, 64, 752, 384, 448, 1), (448, 304, 752, 512, 512, 1))
W_Q_PIECES = ((0, 96, 256, 0, 0, 0), (96, 96, 256, 96, 96, 1))
W_KV_PIECES = ((0, 128, 128, 0, 0, 0), (128, 128, 128, 512, 512, 1))
W_OUT_PIECES = ((0, 128, 256, 0, 0, 0), (128, 128, 256, 128, 128, 1))


class _StagedGather:
    def __init__(self, pieces):
        self.pieces = pieces

    def scratch(self):
        nk, dma = len(self.pieces), pltpu.SemaphoreType.DMA
        return [dma((nk, 3)), dma((nk, 3)), dma((nk, 3)), dma((nk, 3)), dma((nk,))]

    def run(self, stage, src_ref, out_ref, scr):
        send_sems, recv_sems, fwd_send, fwd_recv, loc_sems = scr
        pieces = self.pieces
        nk = len(pieces)
        x, y, c = lax.axis_index("x"), lax.axis_index("y"), lax.axis_index("c")
        mine = 2 * x + y
        chips = [(1 - x, y), (x, 1 - y), (1 - x, 1 - y)]
        chip_of = [2 * px + py for px, py in chips]
        mesh = pl.DeviceIdType.MESH

        def src(k):
            s0, nr = pieces[k][0], pieces[k][1]
            return src_ref.at[s0:s0 + nr]

        def dst(k, q):
            _, nr, per, first, rest, _ = pieces[k]
            row = per * q + first + (rest - first) * jnp.minimum(q, 1)
            return out_ref.at[pl.ds(pl.multiple_of(row, 16), nr)]

        def ici(k, j, q):
            px, py = chips[j]
            return pltpu.make_async_remote_copy(
                src_ref=src(k), dst_ref=dst(k, q), send_sem=send_sems.at[k, j], recv_sem=recv_sems.at[k, j],
                device_id=(px, py, c), device_id_type=mesh)

        def fwd(k, j):
            ref = dst(k, chip_of[j])
            return pltpu.make_async_remote_copy(
                src_ref=ref, dst_ref=ref, send_sem=fwd_send.at[k, j], recv_sem=fwd_recv.at[k, j],
                device_id=(x, y, 1 - c), device_id_type=mesh)

        local = [pltpu.make_async_copy(src(k), dst(k, mine), loc_sems.at[k]) for k in range(nk)]
        if stage == 0:
            for cp in local:
                cp.start()
        if stage == 2:
            for cp in local:
                cp.wait()
        for half in (0, 1):
            @pl.when(c == half)
            def _(half=half):
                my_k = [k for k in range(nk) if pieces[k][5] == half]
                other_k = [k for k in range(nk) if pieces[k][5] != half]
                for k in my_k:
                    for j in range(3):
                        if stage == 0:
                            ici(k, j, mine).start()
                        elif stage == 1:
                            ici(k, j, chip_of[j]).wait_recv()
                            fwd(k, j).start()
                        else:
                            ici(k, j, mine).wait_send()
                            fwd(k, j).wait_send()
                if stage == 2:
                    for k in other_k:
                        for j in range(3):
                            fwd(k, j).wait_recv()


def _fwd_proj(x2d, meta, tabs, tabs_m, norm_g, w_in_p, q_norm_g, wq_p, kv_norm_g, wkv_p, w_out_shard, nb, s, tm):
    nt = s // tm
    n = nb * nt
    n_steps = n + 1
    c_t, sa_t, sb_t = tabs
    cm_t, sam_t, sbm_t = tabs_m
    gat = _StagedGather(W_OUT_PIECES)
    assert n_steps >= 3

    def body(x_ref, c_ref, sa_ref, sb_ref, mt_ref, cm_ref, sam_ref, sbm_ref,
             g_ref, w_ref, gq_ref, wq_ref, gkv_ref, wkv_ref, wos_ref,
             p_ref, q_ref, k_ref, v_ref, pm_ref, km_ref, vm_ref, wo_ref, *gat_scr):
        i = pl.program_id(0)
        for stage, at in enumerate((0, n_steps - 2, n_steps - 1)):
            @pl.when(i == at)
            def _(stage=stage):
                gat.run(stage, wos_ref, wo_ref, gat_scr)

        def project(xv, c, sa, sb, p_out, q_out, k_out, v_out):
            u, _ = _rms(xv, g_ref[...])
            p = _dot_nt(u.astype(BF16), w_ref[...])
            p_out[...] = p
            qn, _ = _rms(p[:, 0:Q_RANK], gq_ref[...])
            q = _dot_nt(qn.astype(BF16), wq_ref[...])
            kvn, _ = _rms(p[:, Q_RANK:Q_RANK + KV_RANK], gkv_ref[...])
            kv = _dot_nt(kvn.astype(BF16), wkv_ref[...])
            kpe = _rope(p[:, 384:512], c, sa, sb)
            for h in range(HEADS):
                if q_out is not None:
                    pe = _rope(q[:, QK_PAD * h + NOPE:QK_PAD * (h + 1)], c, sa, sb)
                    qh = jnp.concatenate([q[:, QK_PAD * h:QK_PAD * h + NOPE], pe], axis=1)
                    q_out[0, h] = (qh * ATTN_SCALE).astype(BF16)
                k_out[0, h] = jnp.concatenate([kv[:, NOPE * h:NOPE * (h + 1)], kpe], axis=1).astype(BF16)
                v_out[0, h] = kv[:, 512 + VDIM * h:512 + VDIM * (h + 1)].astype(BF16)

        @pl.when(i < n)
        def _():
            project(x_ref[...], c_ref[...], sa_ref[...], sb_ref[...], p_ref, q_ref, k_ref, v_ref)

        @pl.when(i == n)
        def _():
            project(mt_ref[...], cm_ref[...], sam_ref[...], sbm_ref[...], pm_ref, None, km_ref, vm_ref)

    cl = lambda i: jnp.minimum(i, n - 1)
    full = lambda a: pl.BlockSpec(a.shape, lambda i: (0,) * a.ndim)
    const = lambda shape: pl.BlockSpec(shape, lambda i: (0,) * len(shape))
    tab = pl.BlockSpec((tm, 128), lambda i: (cl(i) % nt, 0))
    hb = lambda w: pl.BlockSpec((1, HEADS, tm, w), lambda i: (cl(i) // nt, 0, cl(i) % nt, 0))
    hbm = pl.BlockSpec(memory_space=pl.ANY)
    return pl.pallas_call(
        body, name="fwd_proj", grid=(n_steps,),
        in_specs=[pl.BlockSpec((tm, D_MODEL), lambda i: (cl(i), 0)), tab, tab, tab,
                  full(meta), full(cm_t), full(sam_t), full(sbm_t),
                  full(norm_g), full(w_in_p), full(q_norm_g), full(wq_p), full(kv_norm_g), full(wkv_p), hbm],
        out_specs=[pl.BlockSpec((tm, IN_PAD), lambda i: (cl(i), 0)), hb(QK_PAD), hb(QK_PAD), hb(VDIM),
                   const((N_META, IN_PAD)), const((1, HEADS, N_META, QK_PAD)), const((1, HEADS, N_META, VDIM)), hbm],
        out_shape=[jax.ShapeDtypeStruct((nb * s, IN_PAD), F32),
                   jax.ShapeDtypeStruct((nb, HEADS, s, QK_PAD), BF16),
                   jax.ShapeDtypeStruct((nb, HEADS, s, QK_PAD), BF16),
                   jax.ShapeDtypeStruct((nb, HEADS, s, VDIM), BF16),
                   jax.ShapeDtypeStruct((N_META, IN_PAD), F32),
                   jax.ShapeDtypeStruct((1, HEADS, N_META, QK_PAD), BF16),
                   jax.ShapeDtypeStruct((1, HEADS, N_META, VDIM), BF16),
                   jax.ShapeDtypeStruct((D_MODEL, D_MODEL), BF16)],
        scratch_shapes=gat.scratch(),
        compiler_params=_cparams("arbitrary"),
    )(x2d, c_t, sa_t, sb_t, meta, cm_t, sam_t, sbm_t, norm_g, w_in_p, q_norm_g, wq_p, kv_norm_g, wkv_p, w_out_shard)


def _attn_fwd(q, k, v, km, vm, nb, s, tq):
    nq = s // tq

    def body(q_ref, k_ref, v_ref, km_ref, vm_ref, o_ref, lse_ref, s_scr, p_scr):
        row = lax.broadcasted_iota(jnp.int32, (tq, tq), 0)
        col = lax.broadcasted_iota(jnp.int32, (tq, tq), 1)
        def scores(i):
            slot = i % 2
            qi = q_ref[0, 0, i * tq:(i + 1) * tq, :]
            sm = _dot_nt(qi, km_ref[0, 0])
            m128 = None
            for j in range(i + 1):
                sc = _dot_nt(qi, k_ref[0, 0, j * tq:(j + 1) * tq, :])
                if j == i:
                    sc = jnp.where(col <= row, sc, NEG_INF)
                s_scr[slot, :, j * tq:(j + 1) * tq] = sc
                mx = sc[:, 0:128]
                for c0 in range(128, tq, 128):
                    mx = jnp.maximum(mx, sc[:, c0:c0 + 128])
                m128 = mx if m128 is None else jnp.maximum(m128, mx)
            return sm, jnp.maximum(jnp.max(m128, axis=1, keepdims=True), jnp.max(sm, axis=1, keepdims=True))

        def weighted_sum(i, pm, l):
            n = (i + 1) * tq
            acc = _dot(p_scr[i % 2, :, 0:n], v_ref[0, 0, 0:n, :]) + _dot(pm.astype(BF16), vm_ref[0, 0])
            o_ref[0, 0, i * tq:(i + 1) * tq, :] = acc / l

        nxt, pending = scores(0), None
        for i in range(nq):
            slot = i % 2
            sm, m = nxt
            if i + 1 < nq:
                nxt = scores(i + 1)
            pm = jnp.exp(sm - m)
            l128 = None
            for j in range(i + 1):
                p = jnp.exp(s_scr[slot, :, j * tq:(j + 1) * tq] - m)
                p_scr[slot, :, j * tq:(j + 1) * tq] = p.astype(BF16)
                ps = p[:, 0:128]
                for c0 in range(128, tq, 128):
                    ps = ps + p[:, c0:c0 + 128]
                l128 = ps if l128 is None else l128 + ps
            l = jnp.sum(l128, axis=1, keepdims=True) + jnp.sum(pm, axis=1, keepdims=True)
            lse_ref[0, 0, :, i * tq:(i + 1) * tq] = _row_of(m + jnp.log(l), tq)
            if pending is not None:
                weighted_sum(*pending)
            pending = (i, pm, l)
        weighted_sum(*pending)

    hblk = lambda w: pl.BlockSpec((1, 1, s, w), lambda b, h: (b, h, 0, 0))
    mblk = lambda w: pl.BlockSpec((1, 1, N_META, w), lambda b, h: (0, h, 0, 0))
    return pl.pallas_call(
        body, name="attn_fwd", grid=(nb, HEADS),
        in_specs=[hblk(QK_PAD), hblk(QK_PAD), hblk(VDIM), mblk(QK_PAD), mblk(VDIM)],
        out_specs=[hblk(VDIM), pl.BlockSpec((1, 1, 1, s), lambda b, h: (b, h, 0, 0))],
        out_shape=[jax.ShapeDtypeStruct((nb, HEADS, s, VDIM), F32),
                   jax.ShapeDtypeStruct((nb, HEADS, 1, s), F32)],
        scratch_shapes=[pltpu.VMEM((2, tq, s), F32), pltpu.VMEM((2, tq, s), BF16)],
        compiler_params=_cparams("parallel", "parallel"),
    )(q, k, v, km, vm)


def _shift_rows(a, prev, n_rows):
    rid = lax.broadcasted_iota(jnp.int32, a.shape, 0)
    a1 = jnp.where(rid == 0, prev[7:8, :], pltpu.roll(a, 1, 0))
    a2 = jnp.where(rid == 0, prev[6:7, :], jnp.where(rid == 1, prev[7:8, :], pltpu.roll(a, 2, 0)))
    return a1, a2


def _attn_gate(o, za, ga_h):
    on, r = _rms(o, ga_h)
    return on * (za * _sigmoid(za)), on, r


def _out_fwd_bwd(x2d, tgt2d, o, p, pm, conv_w, ga, gc, gmat, w_out, gf, nb, s, tm):
    nt = s // tm
    r = nb * s
    prev_idx = lambda i: jnp.maximum(i * (tm // 8) - 1, 0)

    def body(x_ref, t_ref, o_ref, za_ref, cb_ref, cc_ref, ch_ref, zc_ref, ccp_ref, chp_ref, mc_ref, mh_ref,
             cw_ref, ga_ref, gc_ref, gm_ref, w_ref, gf_ref,
             dh_ref, dy_ref, dw_ref, dgf_ref, loss_ref):
        i = pl.program_id(0)

        @pl.when(i == 0)
        def _():
            dw_ref[...] = jnp.zeros_like(dw_ref)
            dgf_ref[...] = jnp.zeros_like(dgf_ref)
            loss_ref[...] = jnp.zeros_like(loss_ref)

        ya = []
        for h in range(HEADS):
            y, _, _ = _attn_gate(o_ref[0, h], za_ref[:, VDIM * h:VDIM * (h + 1)],
                                 ga_ref[:, VDIM * h:VDIM * (h + 1)])
            ya.append(y)
        cc = cc_ref[...] * ch_ref[...]
        prev = jnp.where(i % nt == 0, mc_ref[8:16, :] * mh_ref[8:16, :], ccp_ref[...] * chp_ref[...])
        cc1, cc2 = _shift_rows(cc, prev, tm)
        yc = cb_ref[...] * (cw_ref[0:1, :] * cc2 + cw_ref[1:2, :] * cc1 + cw_ref[2:3, :] * cc)
        rg = lax.rsqrt(_group_mean(yc * yc, gm_ref[...]) + EPS)
        zc = zc_ref[...]
        yconv = yc * rg * gc_ref[...] * (zc * _sigmoid(zc))
        ycat = jnp.concatenate(ya + [yconv], axis=1).astype(BF16)
        h2 = x_ref[...] + _dot(ycat, w_ref[...])
        gfv = gf_ref[...]
        y, r2 = _rms(h2, gfv)
        e = y - t_ref[...]
        loss_ref[...] += 0.5 * jnp.sum(e * e) / D_MODEL
        dyv = e * (1.0 / D_MODEL)
        dh2, dgf = _rms_bwd(dyv, h2, r2, gfv)
        dgf_ref[...] += jnp.sum(dgf, axis=0, keepdims=True)
        dh_ref[...] = dh2
        dhb = dh2.astype(BF16)
        dy_ref[...] = _dot_nt(dhb, w_ref[...])
        dw_ref[...] += _dot_tn(ycat, dhb)

    row = lambda w, j: pl.BlockSpec((tm, w), lambda i: (i, j))
    pblk = lambda j: pl.BlockSpec((tm, 512), lambda i: (i, j))
    pprev = lambda j: pl.BlockSpec((8, 512), lambda i: (prev_idx(i), j))
    mblk = lambda j: pl.BlockSpec((N_META, 512), lambda i: (0, j))
    full = lambda a: pl.BlockSpec(a.shape, lambda i: (0,) * a.ndim)
    return pl.pallas_call(
        body, name="out_fwd_bwd", grid=(nb * nt,),
        in_specs=[row(D_MODEL, 0), row(D_MODEL, 0),
                  pl.BlockSpec((1, HEADS, tm, VDIM), lambda i: (i // nt, 0, i % nt, 0)),
                  pblk(BLK_ZA), pblk(BLK_CB), pblk(BLK_CC), pblk(BLK_CH), pblk(BLK_ZC),
                  pprev(BLK_CC), pprev(BLK_CH), mblk(BLK_CC), mblk(BLK_CH),
                  full(conv_w), full(ga), full(gc), full(gmat), full(w_out), full(gf)],
        out_specs=[row(D_MODEL, 0), row(D_MODEL, 0),
                   pl.BlockSpec((D_MODEL, D_MODEL), lambda i: (0, 0)),
                   pl.BlockSpec((1, D_MODEL), lambda i: (0, 0)),
                   pl.BlockSpec((1, 128), lambda i: (0, 0))],
        out_shape=[jax.ShapeDtypeStruct((r, D_MODEL), F32), jax.ShapeDtypeStruct((r, D_MODEL), F32),
                   jax.ShapeDtypeStruct((D_MODEL, D_MODEL), F32), jax.ShapeDtypeStruct((1, D_MODEL), F32),
                   jax.ShapeDtypeStruct((1, 128), F32)],
        compiler_params=_cparams("arbitrary"),
    )(x2d, tgt2d, o, p, p, p, p, p, p, p, pm, pm, conv_w, ga, gc, gmat, w_out, gf)


def _gate_bwd(dycat, o, p, pm, conv_w, ga, gc, gmat, nb, s, tm):
    nt = s // tm
    r = nb * s
    ext = tm + 8
    prev_idx = lambda i: jnp.maximum(i * (tm // 8) - 1, 0)
    next_idx = lambda i: jnp.minimum((i + 1) * (tm // 8), r // 8 - 1)

    def body(dya_ref, dyc_ref, dycn_ref, o_ref, za_ref, cb_ref, cbn_ref, cc_ref, ccp_ref, ccn_ref,
             ch_ref, chp_ref, chn_ref, zc_ref, zcn_ref, mc_ref, mh_ref, cw_ref, ga_ref, gc_ref, gm_ref,
             dpb_ref, do_ref, dl_ref, dccm_ref, dga_ref, dgc_ref, dcw_ref):
        i = pl.program_id(0)

        @pl.when(i == 0)
        def _():
            dga_ref[...] = jnp.zeros_like(dga_ref)
            dgc_ref[...] = jnp.zeros_like(dgc_ref)
            dcw_ref[...] = jnp.zeros_like(dcw_ref)

        dga = []
        for h in range(HEADS):
            hs = slice(VDIM * h, VDIM * (h + 1))
            oh, za, gah, dya = o_ref[0, h], za_ref[:, hs], ga_ref[:, hs], dya_ref[:, hs]
            sg = _sigmoid(za)
            on, ro = _rms(oh, gah)
            don = dya * (za * sg)
            dpb_ref[:, hs] = (dya * on * (sg * (1.0 + za * (1.0 - sg)))).astype(BF16)
            do, dg = _rms_bwd(don, oh, ro, gah)
            dga.append(jnp.sum(dg, axis=0, keepdims=True))
            dob = do.astype(BF16)
            do_ref[0, h] = dob
            dl_ref[0, h] = _row_of(jnp.sum(dob.astype(F32) * oh, axis=1, keepdims=True), tm)
        dga_ref[...] += jnp.concatenate(dga, axis=1)

        cat = lambda a, b: jnp.concatenate([a[...], b[...]], axis=0)
        cch = cat(cc_ref, ccn_ref)
        chh = cat(ch_ref, chn_ref)
        cb = cat(cb_ref, cbn_ref)
        zc = cat(zc_ref, zcn_ref)
        dy = cat(dyc_ref, dycn_ref)
        first = i % nt == 0
        last = i % nt == nt - 1
        cc = cch * chh
        prev = jnp.where(first, mc_ref[8:16, :] * mh_ref[8:16, :], ccp_ref[...] * chp_ref[...])
        cc1, cc2 = _shift_rows(cc, prev, ext)
        w0, w1, w2 = cw_ref[0:1, :], cw_ref[1:2, :], cw_ref[2:3, :]
        dw = w0 * cc2 + w1 * cc1 + w2 * cc
        yc = cb * dw
        rg = lax.rsqrt(_group_mean(yc * yc, gm_ref[...]) + EPS)
        ych = yc * rg
        gcv = gc_ref[...]
        sg = _sigmoid(zc)
        dycn = dy * (zc * sg)
        dzc = dy * (ych * gcv) * (sg * (1.0 + zc * (1.0 - sg)))
        dgc_ref[...] += jnp.sum((dycn * ych)[:tm], axis=0, keepdims=True)
        dycg = dycn * gcv
        dyc = rg * (dycg - ych * _group_mean(dycg * ych, gm_ref[...]))
        rid = lax.broadcasted_iota(jnp.int32, (ext, CONV_W), 0)
        ddw = jnp.where(jnp.logical_and(last, rid >= tm), 0.0, dyc * cb)
        dcb = dyc * dw
        dcc = w2 * ddw + w1 * pltpu.roll(ddw, ext - 1, 0) + w0 * pltpu.roll(ddw, ext - 2, 0)
        dpb_ref[:, 512:1024] = dcb[:tm].astype(BF16)
        dpb_ref[:, 1024:1536] = (dcc * chh)[:tm].astype(BF16)
        dpb_ref[:, 1536:2048] = (dcc * cch)[:tm].astype(BF16)
        dpb_ref[:, 2048:2560] = dzc[:tm].astype(BF16)
        rs = lambda a: jnp.sum(a[:tm], axis=0, keepdims=True)
        dcw_ref[0:1, :] += rs(ddw * cc2)
        dcw_ref[1:2, :] += rs(ddw * cc1)
        dcw_ref[2:3, :] += rs(ddw * cc)

        @pl.when(first)
        def _():
            d0, d1 = ddw[0:1, :], ddw[1:2, :]
            r8 = lax.broadcasted_iota(jnp.int32, (8, CONV_W), 0)
            dccm_ref[0] = jnp.where(r8 == 7, w1 * d0 + w0 * d1, jnp.where(r8 == 6, w0 * d0, 0.0))

    row = lambda j: pl.BlockSpec((tm, 512), lambda i: (i, j))
    prv = lambda j: pl.BlockSpec((8, 512), lambda i: (prev_idx(i), j))
    nxt = lambda j: pl.BlockSpec((8, 512), lambda i: (next_idx(i), j))
    mblk = lambda j: pl.BlockSpec((N_META, 512), lambda i: (0, j))
    full = lambda a: pl.BlockSpec(a.shape, lambda i: (0,) * a.ndim)
    hb = lambda w: pl.BlockSpec((1, HEADS, tm, w), lambda i: (i // nt, 0, i % nt, 0))
    acc = lambda rr: pl.BlockSpec((rr, 512), lambda i: (0, 0))
    return pl.pallas_call(
        body, name="gate_bwd", grid=(nb * nt,),
        in_specs=[row(0), row(1), nxt(1), hb(VDIM),
                  row(BLK_ZA), row(BLK_CB), nxt(BLK_CB), row(BLK_CC), prv(BLK_CC), nxt(BLK_CC),
                  row(BLK_CH), prv(BLK_CH), nxt(BLK_CH), row(BLK_ZC), nxt(BLK_ZC),
                  mblk(BLK_CC), mblk(BLK_CH), full(conv_w), full(ga), full(gc), full(gmat)],
        out_specs=[pl.BlockSpec((tm, 2560), lambda i: (i, 0)), hb(VDIM),
                   pl.BlockSpec((1, HEADS, 1, tm), lambda i: (i // nt, 0, 0, i % nt)),
                   pl.BlockSpec((1, 8, 512), lambda i: (i // nt, 0, 0)),
                   acc(1), acc(1), acc(8)],
        out_shape=[jax.ShapeDtypeStruct((r, 2560), BF16), jax.ShapeDtypeStruct((nb, HEADS, s, VDIM), BF16),
                   jax.ShapeDtypeStruct((nb, HEADS, 1, s), F32), jax.ShapeDtypeStruct((nb, 8, 512), F32),
                   jax.ShapeDtypeStruct((1, 512), F32), jax.ShapeDtypeStruct((1, 512), F32),
                   jax.ShapeDtypeStruct((8, 512), F32)],
        compiler_params=_cparams("arbitrary"),
    )(dycat, dycat, dycat, o, p, p, p, p, p, p, p, p, p, p, p, pm, pm, conv_w, ga, gc, gmat)


class _StagedReduce:
    LOC, PRE_S, PRE_R, ICI_S, ICI_R, POST_S, POST_R, OUT, N_SEM = 0, 1, 2, 3, 6, 9, 10, 11, 12

    def __init__(self, shard_shape):
        self.half = (shard_shape[0] // 2, shard_shape[1])

    def scratch(self):
        h = self.half
        return [pltpu.VMEM((4,) + h, F32), pltpu.VMEM((4,) + h, F32), pltpu.VMEM((4,) + h, BF16),
                pltpu.VMEM((3,) + h, BF16), pltpu.VMEM(h, F32), pltpu.SemaphoreType.DMA((self.N_SEM,))]

    def run(self, stage, pin, gout, scr):
        own, sib, wire, rbuf, fin, sems = scr
        r2 = self.half[0]
        x, y, c = lax.axis_index("x"), lax.axis_index("y"), lax.axis_index("c")
        mine = 2 * x + y
        sibling = (x, y, 1 - c)
        chips = [(1 - x, y), (x, 1 - y), (1 - x, 1 - y)]
        rows = lambda half: pl.ds(pl.multiple_of(half * r2, r2), r2)
        mesh = pl.DeviceIdType.MESH

        loc = pltpu.make_async_copy(pin.at[:, rows(c), :], own, sems.at[self.LOC])
        pre = pltpu.make_async_remote_copy(
            src_ref=pin.at[:, rows(1 - c), :], dst_ref=sib, send_sem=sems.at[self.PRE_S],
            recv_sem=sems.at[self.PRE_R], device_id=sibling, device_id_type=mesh)

        def ici(j):
            px, py = chips[j]
            return pltpu.make_async_remote_copy(
                src_ref=wire.at[2 * px + py], dst_ref=rbuf.at[j], send_sem=sems.at[self.ICI_S + j],
                recv_sem=sems.at[self.ICI_R + j], device_id=(px, py, c), device_id_type=mesh)

        def post(half):
            return pltpu.make_async_remote_copy(
                src_ref=fin, dst_ref=gout.at[rows(half), :], send_sem=sems.at[self.POST_S],
                recv_sem=sems.at[self.POST_R], device_id=sibling, device_id_type=mesh)

        keep = pltpu.make_async_copy(fin, gout.at[rows(c), :], sems.at[self.OUT])
        if stage == 0:
            loc.start()
            pre.start()
        elif stage == 1:
            loc.wait()
            pre.wait_recv()
            for blk in range(4):
                tot = own[blk] + sib[blk]
                own[blk] = tot
                wire[blk] = tot.astype(BF16)
            for j in range(3):
                ici(j).start()
        elif stage == 2:
            for j in range(3):
                ici(j).wait_recv()
            tot = own[mine]
            for j in range(3):
                tot = tot + rbuf[j].astype(F32)
            fin[...] = tot
            post(c).start()
            keep.start()
        else:
            post(1 - c).wait_recv()
            pre.wait_send()
            for j in range(3):
                ici(j).wait_send()
            post(c).wait_send()
            keep.wait()


def _attn_bwd(q, k, v, do, lse, delta, km, vm, early, nb, s, t):
    n = s // t
    ne = len(early)
    reds = [_StagedReduce(a.shape[1:]) for a in early]
    n_steps = HEADS * nb
    assert n_steps >= 4

    def body(q_ref, k_ref, v_ref, do_ref, lse_ref, dl_ref, km_ref, vm_ref, *rest):
        pin_refs, rest = rest[:ne], rest[ne:]
        dq_ref, dk_ref, dv_ref, dkm_ref, dvm_ref = rest[:5]
        gout_refs, (p_scr, ds_scr, dq_acc), red_scr = rest[5:5 + ne], rest[5 + ne:8 + ne], rest[8 + ne:]
        b = pl.program_id(1)
        step = pl.program_id(0) * nb + b
        for stage, at in enumerate((0, 1, n_steps - 2, n_steps - 1)):
            @pl.when(step == at)
            def _(stage=stage):
                for a, red in enumerate(reds):
                    red.run(stage, pin_refs[a], gout_refs[a], red_scr[6 * a:6 * a + 6])

        @pl.when(b == 0)
        def _():
            dkm_ref[...] = jnp.zeros_like(dkm_ref)
            dvm_ref[...] = jnp.zeros_like(dvm_ref)

        kr = lax.broadcasted_iota(jnp.int32, (t, t), 0)
        qc = lax.broadcasted_iota(jnp.int32, (t, t), 1)
        km_v, vm_v = km_ref[0, 0], vm_ref[0, 0]
        ptm = jnp.exp(_dot_nt(km_v, q_ref[0, 0]) - lse_ref[0, 0])
        dstm = (ptm * (_dot_nt(vm_v, do_ref[0, 0]) - dl_ref[0, 0])).astype(BF16)
        dkm_ref[0] += _dot(dstm, q_ref[0, 0])
        dvm_ref[0] += _dot(ptm.astype(BF16), do_ref[0, 0])
        dq_acc[...] = _dot_tn(dstm, km_v)
        def tiles(j):
            slot = j % 2
            kj = k_ref[0, 0, j * t:(j + 1) * t, :]
            vj = v_ref[0, 0, j * t:(j + 1) * t, :]
            def products(i):
                cs = slice(i * t, (i + 1) * t)
                return _dot_nt(kj, q_ref[0, 0, cs, :]), _dot_nt(vj, do_ref[0, 0, cs, :])

            nxt, pending = products(j), None
            for i in range(j, n):
                cs = slice(i * t, (i + 1) * t)
                st, dpt = nxt
                if i + 1 < n:
                    nxt = products(i + 1)
                if i == j:
                    st = jnp.where(kr <= qc, st, NEG_INF)
                pt = jnp.exp(st - lse_ref[0, 0, :, cs])
                dst = (pt * (dpt - dl_ref[0, 0, :, cs])).astype(BF16)
                p_scr[slot, :, cs] = pt.astype(BF16)
                ds_scr[slot, :, cs] = dst
                if pending is not None:
                    dq_acc[pending[0], :] += _dot_tn(pending[1], kj)
                pending = (cs, dst)
            dq_acc[pending[0], :] += _dot_tn(pending[1], kj)

        for j in range(n):
            slot = j % 2
            tiles(j)
            dv_ref[0, 0, j * t:(j + 1) * t, :] = _dot(p_scr[slot, :, j * t:s], do_ref[0, 0, j * t:s, :]).astype(BF16)
            dk_ref[0, 0, j * t:(j + 1) * t, :] = _dot(ds_scr[slot, :, j * t:s], q_ref[0, 0, j * t:s, :]).astype(BF16)
        dq_ref[0, 0] = dq_acc[...].astype(BF16)

    big = lambda w: pl.BlockSpec((1, 1, s, w), lambda h, b: (b, h, 0, 0))
    rowv = pl.BlockSpec((1, 1, 1, s), lambda h, b: (b, h, 0, 0))
    mk = lambda w: pl.BlockSpec((1, 1, N_META, w), lambda h, b: (0, h, 0, 0))
    mo = lambda w: pl.BlockSpec((1, N_META, w), lambda h, b: (h, 0, 0))
    return pl.pallas_call(
        body, name="attn_bwd", grid=(HEADS, nb),
        in_specs=[big(QK_PAD), big(QK_PAD), big(VDIM), big(VDIM), rowv, rowv, mk(QK_PAD), mk(VDIM)]
        + [pl.BlockSpec(memory_space=pl.ANY)] * ne,
        out_specs=[big(QK_PAD), big(QK_PAD), big(VDIM), mo(QK_PAD), mo(VDIM)]
        + [pl.BlockSpec(memory_space=pl.ANY)] * ne,
        out_shape=[jax.ShapeDtypeStruct((nb, HEADS, s, QK_PAD), BF16),
                   jax.ShapeDtypeStruct((nb, HEADS, s, QK_PAD), BF16),
                   jax.ShapeDtypeStruct((nb, HEADS, s, VDIM), BF16),
                   jax.ShapeDtypeStruct((HEADS, N_META, QK_PAD), F32),
                   jax.ShapeDtypeStruct((HEADS, N_META, VDIM), F32)]
        + [jax.ShapeDtypeStruct(a.shape[1:], F32) for a in early],
        scratch_shapes=[pltpu.VMEM((2, t, s), BF16), pltpu.VMEM((2, t, s), BF16), pltpu.VMEM((s, QK_PAD), F32)]
        + [sc for red in reds for sc in red.scratch()],
        compiler_params=_cparams("arbitrary", "arbitrary"),
    )(q, k, v, do, lse, delta, km, vm, *early)


def _up_bwd(dq, dk, dv, dkm, dvm, p, pm, tabs, tabs_m, wq_p, wkv_p, gq, gkv, nb, s, tm):
    nt = s // tm
    n = nb * nt
    c_t, sa_t, sb_t = tabs
    cm_t, sam_t, sbm_t = tabs_m

    def kv_path(dkh, dvh, pa, c, sa, sb, wkv, gkvv):
        dkpe = dkh[0][:, NOPE:]
        for h in range(1, HEADS):
            dkpe = dkpe + dkh[h][:, NOPE:]
        dkr = _rope_bwd(dkpe, c, sa, sb)
        dkv = jnp.concatenate([d[:, :NOPE] for d in dkh] + list(dvh), axis=1).astype(BF16)
        ckv = pa[:, Q_RANK:Q_RANK + KV_RANK]
        kvn, rkv = _rms(ckv, gkvv)
        dckv, dg = _rms_bwd(_dot(dkv, wkv), ckv, rkv, gkvv)
        return dckv, dkr, kvn.astype(BF16), dkv, jnp.sum(dg, axis=0, keepdims=True)

    def body(dq_ref, dk_ref, dv_ref, pa_ref, c_ref, sa_ref, sb_ref,
             dkm_ref, dvm_ref, pam_ref, cm_ref, sam_ref, sbm_ref,
             wq_ref, wkv_ref, gq_ref, gkv_ref,
             dpa_ref, dpam_ref, pq_ref, pkv_ref, dgq_ref, dgkv_ref, dwq_ref, dwkv_ref):
        i = pl.program_id(0)

        @pl.when(i == 0)
        def _():
            dwq_ref[...] = jnp.zeros_like(dwq_ref)
            dwkv_ref[...] = jnp.zeros_like(dwkv_ref)
            dgq_ref[...] = jnp.zeros_like(dgq_ref)
            dgkv_ref[...] = jnp.zeros_like(dgkv_ref)

        @pl.when(i < n)
        def _():
            c, sa, sb = c_ref[...], sa_ref[...], sb_ref[...]
            pa = pa_ref[...]
            parts = []
            for h in range(HEADS):
                dqh = dq_ref[0, h].astype(F32) * ATTN_SCALE
                parts += [dqh[:, :NOPE], _rope_bwd(dqh[:, NOPE:], c, sa, sb)]
            dql = jnp.concatenate(parts, axis=1).astype(BF16)
            cq = pa[:, 0:Q_RANK]
            gqv = gq_ref[...]
            qn, rq = _rms(cq, gqv)
            dwq_ref[...] += _dot_tn(dql, qn.astype(BF16))
            dcq, dg = _rms_bwd(_dot(dql, wq_ref[...]), cq, rq, gqv)
            dgq_ref[...] += jnp.sum(dg, axis=0, keepdims=True)
            dckv, dkr, kvn, dkv, dgk = kv_path([dk_ref[0, h].astype(F32) for h in range(HEADS)],
                                               [dv_ref[0, h].astype(F32) for h in range(HEADS)],
                                               pa, c, sa, sb, wkv_ref[...], gkv_ref[...])
            dwkv_ref[...] += _dot_tn(dkv, kvn)
            dgkv_ref[...] += dgk
            dpa_ref[...] = jnp.concatenate([dcq, dckv, dkr], axis=1).astype(BF16)

        @pl.when(i == n)
        def _():
            dckv, dkr, kvn, dkv, dgk = kv_path([dkm_ref[h] for h in range(HEADS)],
                                               [dvm_ref[h] for h in range(HEADS)],
                                               pam_ref[...], cm_ref[...], sam_ref[...], sbm_ref[...],
                                               wkv_ref[...], gkv_ref[...])
            dwkv_ref[...] += _dot_tn(dkv, kvn)
            dgkv_ref[...] += dgk
            dpam_ref[...] = jnp.concatenate([jnp.zeros((N_META, Q_RANK), F32), dckv, dkr], axis=1)
            for h in range(HEADS):
                pq_ref[h] = dwq_ref[QK_PAD * h:QK_PAD * h + NOPE + ROPE, :]
                pkv_ref[h, 0:NOPE, :] = dwkv_ref[NOPE * h:NOPE * (h + 1), :]
                pkv_ref[h, NOPE:NOPE + VDIM, :] = dwkv_ref[512 + VDIM * h:512 + VDIM * (h + 1), :]

    cl = lambda i: jnp.minimum(i, n - 1)
    hb = lambda w: pl.BlockSpec((1, HEADS, tm, w), lambda i: (cl(i) // nt, 0, cl(i) % nt, 0))
    tab = pl.BlockSpec((tm, 128), lambda i: (cl(i) % nt, 0))
    full = lambda a: pl.BlockSpec(a.shape, lambda i: (0,) * a.ndim)
    const = lambda shape: pl.BlockSpec(shape, lambda i: (0,) * len(shape))
    return pl.pallas_call(
        body, name="up_bwd", grid=(n + 1,),
        in_specs=[hb(QK_PAD), hb(QK_PAD), hb(VDIM), pl.BlockSpec((tm, 512), lambda i: (cl(i), 0)), tab, tab, tab,
                  full(dkm), full(dvm), pl.BlockSpec((N_META, 512), lambda i: (0, 0)),
                  full(cm_t), full(sam_t), full(sbm_t), full(wq_p), full(wkv_p), full(gq), full(gkv)],
        out_specs=[pl.BlockSpec((tm, 512), lambda i: (cl(i), 0)), const((N_META, 512)),
                   const((HEADS, NOPE + ROPE, Q_RANK)), const((HEADS, NOPE + VDIM, KV_RANK)),
                   const((1, Q_RANK)), const((1, KV_RANK))],
        out_shape=[jax.ShapeDtypeStruct((nb * s, 512), BF16), jax.ShapeDtypeStruct((N_META, 512), F32),
                   jax.ShapeDtypeStruct((HEADS, NOPE + ROPE, Q_RANK), F32),
                   jax.ShapeDtypeStruct((HEADS, NOPE + VDIM, KV_RANK), F32),
                   jax.ShapeDtypeStruct((1, Q_RANK), F32), jax.ShapeDtypeStruct((1, KV_RANK), F32)],
        scratch_shapes=[pltpu.VMEM((HEADS * QK_PAD, Q_RANK), F32), pltpu.VMEM((1024, KV_RANK), F32)],
        compiler_params=_cparams("arbitrary"),
    )(dq, dk, dv, p, c_t, sa_t, sb_t, dkm, dvm, pm, cm_t, sam_t, sbm_t, wq_p, wkv_p, gq, gkv)


def _in_bwd(x2d, dh2, dpa, dpb, meta, dpam, dccm, pm, w_in_p, norm_g, nb, s, tm):
    nt = s // tm
    n = nb * nt

    def body(x_ref, dh_ref, dpa_ref, dpb_ref, mt_ref, dpam_ref, dccm_ref, mc_ref, mh_ref, w_ref, g_ref,
             gx_ref, gm_ref, dw_hbm, dg_ref, acc_ref, sems):
        i = pl.program_id(0)

        @pl.when(i == 0)
        def _():
            acc_ref[...] = jnp.zeros_like(acc_ref)
            dg_ref[...] = jnp.zeros_like(dg_ref)

        def rows(x, dp, dres):
            g = g_ref[...]
            dpb16 = dp.astype(BF16)
            du = _dot(dpb16, w_ref[...])
            u, r1 = _rms(x, g)
            acc_ref[...] += _dot_tn(dpb16, u.astype(BF16))
            dx, dg = _rms_bwd(du, x, r1, g)
            dg_ref[...] += jnp.sum(dg, axis=0, keepdims=True)
            return dx if dres is None else dx + dres

        @pl.when(i < n)
        def _():
            dp = jnp.concatenate([dpa_ref[...], dpb_ref[...]], axis=1)
            gx_ref[...] = rows(x_ref[...], dp, dh_ref[...])

        @pl.when(i == n)
        def _():
            dcc = dccm_ref[0]
            for b in range(1, nb):
                dcc = dcc + dccm_ref[b]
            z8 = jnp.zeros((8, CONV_W), F32)
            dc = jnp.concatenate([z8, dcc * mh_ref[8:16, :]], axis=0)
            dh = jnp.concatenate([z8, dcc * mc_ref[8:16, :]], axis=0)
            z = jnp.zeros((N_META, CONV_W), F32)
            dp = jnp.concatenate([dpam_ref[...], z, z, dc, dh, z], axis=1)
            gm_ref[...] = rows(mt_ref[...], dp, None)
            per = IN_DIM // 4
            cps = [pltpu.make_async_copy(acc_ref.at[0:448], dw_hbm.at[0, 0:448], sems.at[0]),
                   pltpu.make_async_copy(acc_ref.at[512:per + 64], dw_hbm.at[0, 448:per], sems.at[1])]
            for qq in range(1, 4):
                cps.append(pltpu.make_async_copy(acc_ref.at[per * qq + 64:per * (qq + 1) + 64], dw_hbm.at[qq],
                                                 sems.at[qq + 1]))
            for cp in cps:
                cp.start()
            for cp in cps:
                cp.wait()

    cl = lambda i: jnp.minimum(i, n - 1)
    row = lambda w: pl.BlockSpec((tm, w), lambda i: (cl(i), 0))
    full = lambda a: pl.BlockSpec(a.shape, lambda i: (0,) * a.ndim)
    mblk = lambda j: pl.BlockSpec((N_META, 512), lambda i: (0, j))
    return pl.pallas_call(
        body, name="in_bwd", grid=(n + 1,),
        in_specs=[row(D_MODEL), row(D_MODEL), row(512), row(2560), full(meta), full(dpam), full(dccm),
                  mblk(BLK_CC), mblk(BLK_CH), full(w_in_p), full(norm_g)],
        out_specs=[row(D_MODEL), pl.BlockSpec((N_META, D_MODEL), lambda i: (0, 0)),
                   pl.BlockSpec(memory_space=pl.ANY), pl.BlockSpec((1, D_MODEL), lambda i: (0, 0))],
        out_shape=[jax.ShapeDtypeStruct((nb * s, D_MODEL), F32), jax.ShapeDtypeStruct((N_META, D_MODEL), F32),
                   jax.ShapeDtypeStruct((4, IN_DIM // 4, D_MODEL), F32), jax.ShapeDtypeStruct((1, D_MODEL), F32)],
        scratch_shapes=[pltpu.VMEM((IN_PAD, D_MODEL), F32), pltpu.SemaphoreType.DMA((5,))],
        compiler_params=_cparams("arbitrary"),
    )(x2d, dh2, dpa, dpb, meta, dpam, dccm, pm, pm, w_in_p, norm_g)


def _gather_weights(split, pieces, out_rows, whole, zero_fills):
    ns, nw, nz = len(split), len(whole), len(zero_fills)
    flat = [(a, pc) for a in range(ns) for pc in pieces[a]]
    nk = len(flat)

    def body(*refs):
        ins, wins, zins = refs[:ns], refs[ns:ns + nw], refs[ns + nw:ns + nw + nz]
        outs, wouts = refs[ns + nw + nz:2 * ns + nw + nz], refs[2 * ns + nw + nz:2 * (ns + nw) + nz]
        send_sems, recv_sems, fwd_send, fwd_recv, loc_sems, w_send, w_recv, w_loc, z_sems = refs[2 * (ns + nw) + nz:]
        x, y, c = lax.axis_index("x"), lax.axis_index("y"), lax.axis_index("c")
        mine = 2 * x + y
        chips = [(1 - x, y), (x, 1 - y), (1 - x, 1 - y)]
        chip_of = [2 * px + py for px, py in chips]

        def src(k):
            a, (s0, nr, _, _, _, _) = flat[k]
            return ins[a].at[s0:s0 + nr]

        def dst(k, q):
            a, (_, nr, per, first, rest, _) = flat[k]
            row = per * q + first + (rest - first) * jnp.minimum(q, 1)
            return outs[a].at[pl.ds(pl.multiple_of(row, 16), nr)]

        def ici(k, j, q):
            px, py = chips[j]
            return pltpu.make_async_remote_copy(
                src_ref=src(k), dst_ref=dst(k, q), send_sem=send_sems.at[k, j], recv_sem=recv_sems.at[k, j],
                device_id=(px, py, c), device_id_type=pl.DeviceIdType.MESH)

        def fwd(k, j):
            ref = dst(k, chip_of[j])
            return pltpu.make_async_remote_copy(
                src_ref=ref, dst_ref=ref, send_sem=fwd_send.at[k, j], recv_sem=fwd_recv.at[k, j],
                device_id=(x, y, 1 - c), device_id_type=pl.DeviceIdType.MESH)

        def wcopy(b, j, q):
            px, py = chips[j]
            return pltpu.make_async_remote_copy(
                src_ref=wins[b], dst_ref=wouts[b].at[q], send_sem=w_send.at[b, j], recv_sem=w_recv.at[b, j],
                device_id=(px, py, c), device_id_type=pl.DeviceIdType.MESH)

        local = [pltpu.make_async_copy(src(k), dst(k, mine), loc_sems.at[k]) for k in range(nk)]
        local += [pltpu.make_async_copy(wins[b], wouts[b].at[mine], w_loc.at[b]) for b in range(nw)]
        for z, (a, _, row0) in enumerate(zero_fills):
            local.append(pltpu.make_async_copy(zins[z], outs[a].at[row0:row0 + zins[z].shape[0]], z_sems.at[z]))
        wsends = [wcopy(b, j, mine) for b in range(nw) for j in range(3)]
        for cp in local + wsends:
            cp.start()

        for half in (0, 1):
            @pl.when(c == half)
            def _(half=half):
                my_k = [k for k in range(nk) if flat[k][1][5] == half]
                other_k = [k for k in range(nk) if flat[k][1][5] != half]
                sends = [ici(k, j, mine) for k in my_k for j in range(3)]
                for cp in sends:
                    cp.start()
                passed = []
                for k in my_k:
                    for j in range(3):
                        ici(k, j, chip_of[j]).wait_recv()
                        cp = fwd(k, j)
                        cp.start()
                        passed.append(cp)
                for k in other_k:
                    for j in range(3):
                        fwd(k, j).wait_recv()
                for cp in sends + passed:
                    cp.wait_send()

        for b in range(nw):
            for j in range(3):
                wcopy(b, j, chip_of[j]).wait_recv()
        for cp in wsends:
            cp.wait_send()
        for cp in local:
            cp.wait()

    vmem = pl.BlockSpec(memory_space=pltpu.VMEM)
    dma = pltpu.SemaphoreType.DMA
    zeros = [z for _, z, _ in zero_fills]
    return pl.pallas_call(
        body, name="gather_weights",
        in_specs=[vmem] * (ns + nw + nz), out_specs=[vmem] * (ns + nw),
        out_shape=([jax.ShapeDtypeStruct((out_rows[a], split[a].shape[1]), split[a].dtype) for a in range(ns)]
                   + [jax.ShapeDtypeStruct((4,) + w.shape, w.dtype) for w in whole]),
        scratch_shapes=[dma((nk, 3)), dma((nk, 3)), dma((nk, 3)), dma((nk, 3)), dma((nk,)),
                        dma((nw, 3)), dma((nw, 3)), dma((nw,)), dma((nz,))],
        compiler_params=pltpu.CompilerParams(vmem_limit_bytes=VMEM_LIMIT),
    )(*split, *whole, *zeros)


def _reduce_grads(parts, small):
    n = len(parts)
    shapes = [a.shape[1:] for a in parts]
    halves = [(sh[0] // 2, sh[1]) for sh in shapes]

    def body(*refs):
        pin, sm_in = refs[:n], refs[n]
        gout, sm_out = refs[n + 1:2 * n + 1], refs[2 * n + 1]
        scr = refs[2 * n + 2:]
        own, sib, wire, rbuf = scr[:n], scr[n:2 * n], scr[2 * n:3 * n], scr[3 * n:4 * n]
        (sbuf, send_sems, recv_sems, loc_sems, pre_send, pre_recv, post_send, post_recv,
         sm_send, sm_recv) = scr[4 * n:]
        x, y, c = lax.axis_index("x"), lax.axis_index("y"), lax.axis_index("c")
        mine = 2 * x + y
        me = 4 * x + 2 * y + c
        sibling = (x, y, 1 - c)
        chips = [(1 - x, y), (x, 1 - y), (1 - x, 1 - y)]

        def rows(a, half):
            r2 = halves[a][0]
            return pl.ds(pl.multiple_of(half * r2, r2), r2)

        chip_of = [2 * px + py for px, py in chips]
        blocks = chip_of + [mine]

        def pre(a, k):
            return pltpu.make_async_remote_copy(
                src_ref=pin[a].at[blocks[k], rows(a, 1 - c), :], dst_ref=sib[a].at[blocks[k]],
                send_sem=pre_send.at[a, k], recv_sem=pre_recv.at[a, k], device_id=sibling,
                device_id_type=pl.DeviceIdType.MESH)

        def ici(a, j):
            px, py = chips[j]
            return pltpu.make_async_remote_copy(
                src_ref=wire[a].at[2 * px + py], dst_ref=rbuf[a].at[j], send_sem=send_sems.at[a, j],
                recv_sem=recv_sems.at[a, j], device_id=(px, py, c), device_id_type=pl.DeviceIdType.MESH)

        def post(a, half):
            ref = gout[a].at[rows(a, half), :]
            return pltpu.make_async_remote_copy(
                src_ref=ref, dst_ref=ref, send_sem=post_send.at[a], recv_sem=post_recv.at[a],
                device_id=sibling, device_id_type=pl.DeviceIdType.MESH)

        def small_copy(kk):
            peer = (x ^ (kk >> 2), y ^ ((kk >> 1) & 1), c ^ (kk & 1))
            return pltpu.make_async_remote_copy(
                src_ref=sm_in, dst_ref=sbuf.at[kk], send_sem=sm_send.at[kk - 1], recv_sem=sm_recv.at[kk - 1],
                device_id=peer, device_id_type=pl.DeviceIdType.MESH)

        local = [[pltpu.make_async_copy(pin[a].at[blocks[k], rows(a, c), :], own[a].at[blocks[k]], loc_sems.at[a, k])
                  for k in range(4)] for a in range(n)]
        pres = [[pre(a, k) for k in range(4)] for a in range(n)]
        smalls = [small_copy(kk) for kk in range(1, 8)]
        for a in range(n):
            for k in range(4):
                local[a][k].start()
                pres[a][k].start()
        for cp in smalls:
            cp.start()
        sbuf[0] = sm_in[...]
        sends = []
        for a in range(n):
            for k in range(4):
                local[a][k].wait()
                pres[a][k].wait_recv()
                tot = own[a][blocks[k]] + sib[a][blocks[k]]
                own[a][blocks[k]] = tot
                if k < 3:
                    wire[a][blocks[k]] = tot.astype(BF16)
                    cp = ici(a, k)
                    cp.start()
                    sends.append(cp)
        for cp in smalls:
            cp.wait_recv()
        total = sbuf[me]
        for d in range(1, 8):
            total = total + sbuf[me ^ d]
        sm_out[...] = total
        posts = []
        for a in range(n):
            for j in range(3):
                ici(a, j).wait_recv()
            fin = own[a][mine]
            for j in range(3):
                fin = fin + rbuf[a][j].astype(F32)
            gout[a][rows(a, c), :] = fin
            cp = post(a, c)
            cp.start()
            posts.append(cp)
        for a in range(n):
            post(a, 1 - c).wait_recv()
        for cp in [cp for row in pres for cp in row] + sends + smalls + posts:
            cp.wait_send()

    hbm = pl.BlockSpec(memory_space=pl.ANY)
    vmem = pl.BlockSpec(memory_space=pltpu.VMEM)
    dma = pltpu.SemaphoreType.DMA
    return pl.pallas_call(
        body, name="reduce_grads",
        in_specs=[hbm] * n + [vmem], out_specs=[vmem] * (n + 1),
        out_shape=[jax.ShapeDtypeStruct(sh, F32) for sh in shapes] + [jax.ShapeDtypeStruct(small.shape, F32)],
        scratch_shapes=([pltpu.VMEM((4,) + hs, F32) for hs in halves] + [pltpu.VMEM((4,) + hs, F32) for hs in halves]
                        + [pltpu.VMEM((4,) + hs, BF16) for hs in halves]
                        + [pltpu.VMEM((3,) + hs, BF16) for hs in halves]
                        + [pltpu.VMEM((8,) + small.shape, F32), dma((n, 3)), dma((n, 3)), dma((n, 4)),
                           dma((n, 4)), dma((n, 4)), dma((n,)), dma((n,)), dma((7,)), dma((7,))]),
        compiler_params=pltpu.CompilerParams(vmem_limit_bytes=VMEM_LIMIT),
    )(*parts, small)


def _adamw_update(w_ref, g_ref, m_ref, v_ref, d_ref, nm_ref, nv_ref):
    gv = g_ref[...]
    nm = ADAM_B1 * m_ref[...] + (1.0 - ADAM_B1) * gv
    nv = ADAM_B2 * v_ref[...] + (1.0 - ADAM_B2) * (gv * gv)
    m_hat = nm / (1.0 - ADAM_B1 ** ADAM_STEP)
    v_hat = nv / (1.0 - ADAM_B2 ** ADAM_STEP)
    d_ref[...] = -ADAM_LR * (m_hat / (jnp.sqrt(v_hat) + ADAM_EPS) + ADAM_WD * w_ref[...])
    nm_ref[...] = nm
    nv_ref[...] = nv


def _adamw_small(ws, gs, ms, vs):
    k = len(ws)

    def body(*refs):
        ins, outs = refs[:4 * k], refs[4 * k:]
        for a in range(k):
            _adamw_update(ins[a], ins[k + a], ins[2 * k + a], ins[3 * k + a], outs[a], outs[k + a], outs[2 * k + a])

    out = pl.pallas_call(
        body, name="adamw_small",
        out_shape=[jax.ShapeDtypeStruct(w.shape, F32) for w in ws] * 3,
        compiler_params=pltpu.CompilerParams(vmem_limit_bytes=VMEM_LIMIT),
    )(*ws, *gs, *ms, *vs)
    return out[:k], out[k:2 * k], out[2 * k:]


def _adamw(w, g, m, v, name):
    shape = w.shape
    w2, g2, m2, v2 = (a.reshape((-1, shape[-1])) for a in (w, g, m, v))

    def body(w_ref, g_ref, m_ref, v_ref, d_ref, nm_ref, nv_ref):
        _adamw_update(w_ref, g_ref, m_ref, v_ref, d_ref, nm_ref, nv_ref)

    rows, cols = w2.shape
    nblk = cols // 256 if cols % 256 == 0 and rows >= 64 else 1
    blk = pl.BlockSpec((rows, cols // nblk), lambda j: (0, j))
    out = pl.pallas_call(
        body, name=name, grid=(nblk,), in_specs=[blk] * 4, out_specs=[blk] * 3,
        out_shape=[jax.ShapeDtypeStruct(w2.shape, F32)] * 3,
        compiler_params=_cparams("parallel"),
    )(w2, g2, m2, v2)
    return tuple(a.reshape(shape) for a in out)


def kernel(x, meta_tokens, norm_g, w_in, q_norm_g, w_q_up, kv_norm_g, w_kv_up, conv_w, attn_out_g, conv_out_g, w_out, final_norm_g, loss_target, m_meta_tokens, m_norm_g, m_w_in, m_q_norm_g, m_w_q_up, m_kv_norm_g, m_w_kv_up, m_conv_w, m_attn_out_g, m_conv_out_g, m_w_out, m_final_norm_g, v_meta_tokens, v_norm_g, v_w_in, v_q_norm_g, v_w_q_up, v_kv_norm_g, v_w_kv_up, v_conv_w, v_attn_out_g, v_conv_out_g, v_w_out, v_final_norm_g):
    nb, s, _ = x.shape
    tm = min(ROW_TILE, s)
    ta = min(ATTN_TILE, s)
    assert s % tm == 0 and s % ta == 0 and tm % 16 == 0
    r = nb * s

    tr = lambda a: jnp.transpose(a[0])
    w_in_p, wq_p, wkv_p, g_cw, g_meta = _gather_weights(
        [tr(w_in).astype(BF16), tr(w_q_up).astype(BF16), tr(w_kv_up).astype(BF16)],
        [W_IN_PIECES, W_Q_PIECES, W_KV_PIECES], [IN_PAD, HEADS * QK_PAD, 1024],
        [jnp.transpose(conv_w, (1, 0, 2)), meta_tokens],
        [(0, jnp.zeros((64, D_MODEL), BF16), 448)]
        + [(1, jnp.zeros((64, Q_RANK), BF16), QK_PAD * h + NOPE + ROPE) for h in range(HEADS)])
    conv_f = jnp.transpose(g_cw[:, :, 0, :], (1, 0, 2)).reshape(3, CONV_W)
    meta_f = jnp.transpose(g_meta, (1, 0, 2)).reshape(N_META, D_MODEL)

    c_all, sa_all, sb_all = _rope_tables(N_META + s)
    tabs_m = (c_all[:N_META], sa_all[:N_META], sb_all[:N_META])
    tabs = (c_all[N_META:], sa_all[N_META:], sb_all[N_META:])
    gid = np.arange(CONV_W) // CONV_GROUP
    gmat = jnp.asarray(np.where(gid[:, None] == gid[None, :], 1.0 / CONV_GROUP, 0.0), BF16)
    ga, gc = attn_out_g, conv_out_g
    gf = final_norm_g.reshape(1, D_MODEL)

    x2d = x.reshape(r, D_MODEL)
    tgt2d = loss_target.reshape(r, D_MODEL)

    p, q, k, v, pm, km, vm, w_out_f = _fwd_proj(x2d, meta_f, tabs, tabs_m, norm_g, w_in_p, q_norm_g, wq_p,
                                                kv_norm_g, wkv_p, w_out[0].astype(BF16), nb, s, tm)
    o, lse = _attn_fwd(q, k, v, km, vm, nb, s, ta)
    dh2, dycat, dw_out, dgf, loss_acc = _out_fwd_bwd(x2d, tgt2d, o, p, pm, conv_f, ga, gc, gmat, w_out_f, gf,
                                                     nb, s, tm)
    dpb, do, delta, dccm, dga, dgc, dcw = _gate_bwd(dycat, o, p, pm, conv_f, ga, gc, gmat, nb, s, tm)
    p_out = dw_out.reshape(4, D_MODEL // 4, D_MODEL)
    dq, dk, dv, dkm, dvm, g_w_out = _attn_bwd(q, k, v, do, lse, delta, km, vm, [p_out], nb, s, ta)
    dpa, dpam, p_q, p_kv, dgq, dgkv = _up_bwd(dq, dk, dv, dkm, dvm, p, pm, tabs, tabs_m, wq_p, wkv_p,
                                              q_norm_g, kv_norm_g, nb, s, tm)
    gx, gmeta, p_in, dng = _in_bwd(x2d, dh2, dpa, dpb, meta_f, dpam, dccm, pm, w_in_p, norm_g, nb, s, tm)

    flat =jnp.concatenate([dng.reshape(-1), dgq.reshape(-1), dgkv.reshape(-1), dga.reshape(-1), dgc.reshape(-1),
                            dgf.reshape(-1), dcw[:3].reshape(-1), gmeta.reshape(-1), loss_acc[0, 0:1]])
    n_small = flat.shape[0]
    rows_small = -(-n_small // 1024) * 8
    small = jnp.pad(flat, (0, rows_small * 128 - n_small)).reshape(rows_small, 128)
    g_w_in_t, g_w_q_t, g_w_kv_t, small_sum = _reduce_grads([p_in, p_q, p_kv], small)
    ssum = small_sum.reshape(-1)

    def take(off, n):
        return ssum[off:off + n], off + n

    off = 0
    g_norm, off = take(off, D_MODEL)
    g_qn, off = take(off, Q_RANK)
    g_kvn, off = take(off, KV_RANK)
    g_ga, off = take(off, CONV_W)
    g_gc, off = take(off, CONV_W)
    g_gf, off = take(off, D_MODEL)
    g_cw_all, off = take(off, 3 * CONV_W)
    g_meta_all, off = take(off, N_META * D_MODEL)
    loss = ssum[off]
    chip = 2 * lax.axis_index("x") + lax.axis_index("y")
    g_conv = lax.dynamic_slice(g_cw_all.reshape(3, CONV_W), (0, chip * 128), (3, 128))
    g_mt = lax.dynamic_slice(g_meta_all.reshape(N_META, D_MODEL), (0, chip * 256), (N_META, 256))

    grads = {
        "meta_tokens": g_mt, "norm_g": g_norm.reshape(1, -1), "w_in": g_w_in_t, "q_norm_g": g_qn.reshape(1, -1),
        "w_q_up": g_w_q_t, "kv_norm_g": g_kvn.reshape(1, -1), "w_kv_up": jnp.transpose(g_w_kv_t)[None],
        "conv_w": g_conv[None], "attn_out_g": g_ga.reshape(1, -1), "conv_out_g": g_gc.reshape(1, -1),
        "w_out": g_w_out[None], "final_norm_g": g_gf,
    }
    transposed = ("w_in", "w_q_up")
    weights = {
        "meta_tokens": (meta_tokens, m_meta_tokens, v_meta_tokens), "norm_g": (norm_g, m_norm_g, v_norm_g),
        "w_in": (w_in, m_w_in, v_w_in), "q_norm_g": (q_norm_g, m_q_norm_g, v_q_norm_g),
        "w_q_up": (w_q_up, m_w_q_up, v_w_q_up), "kv_norm_g": (kv_norm_g, m_kv_norm_g, v_kv_norm_g),
        "w_kv_up": (w_kv_up, m_w_kv_up, v_w_kv_up), "conv_w": (conv_w, m_conv_w, v_conv_w),
        "attn_out_g": (attn_out_g, m_attn_out_g, v_attn_out_g), "conv_out_g": (conv_out_g, m_conv_out_g, v_conv_out_g),
        "w_out": (w_out, m_w_out, v_w_out), "final_norm_g": (final_norm_g, m_final_norm_g, v_final_norm_g),
    }
    names = list(weights)
    small = [nme for nme in names if nme != "w_in"]

    def view(nme, a):
        if nme in transposed:
            return a if a.ndim == 2 else tr(a)
        if nme == "conv_w":
            return jnp.transpose(a.reshape(1, 3, -1), (1, 0, 2))
        if a.ndim == 3:
            return a[0]
        return a.reshape(1, -1) if a.ndim == 1 else a

    def unview(nme, a):
        if nme in transposed:
            return jnp.transpose(a)[None]
        if nme == "conv_w":
            return jnp.transpose(a, (1, 0, 2))
        return a.reshape(weights[nme][0].shape)

    res_small = _adamw_small(*[[view(nme, a) for nme, a in zip(small, col)] for col in (
        [weights[nme][0] for nme in small], [grads[nme] for nme in small],
        [weights[nme][1] for nme in small], [weights[nme][2] for nme in small])])
    w_, m_, v_ = weights["w_in"]
    res = _adamw(tr(w_), grads["w_in"], tr(m_), tr(v_), "adamw_w_in")
    upd = {"w_in": tuple(jnp.transpose(a)[None] for a in (grads["w_in"],) + res)}
    for j, nme in enumerate(small):
        upd[nme] = (unview(nme, view(nme, grads[nme])),) + tuple(unview(nme, r[j]) for r in res_small)
    grads = {nme: upd[nme][0] for nme in names}
    deltas, new_m, new_v = ([upd[nme][j] for nme in names] for j in (1, 2, 3))

    grad_x = gx.reshape(nb, s, D_MODEL)
    return (loss, grad_x, *[grads[nme] for nme in names], *deltas, *new_m, *new_v)
```

```python
import functools

import jax
import jax.numpy as jnp
import numpy as np
from jax import lax
from jax.experimental import pallas as pl
from jax.experimental.pallas import tpu as pltpu

F32 = jnp.float32
BF16 = jnp.bfloat16

D_MODEL = 1024
N_META = 16
HEADS = 4
NOPE = 128
ROPE = 64
VDIM = 128
QK_PAD = 256
Q_RANK = 256
KV_RANK = 128
CONV_W = 512
CONV_GROUP = 64
ROPE_THETA = 10000.0
EPS = 1e-6
ATTN_SCALE = (NOPE + ROPE) ** -0.5
IN_DIM = 3008
IN_PAD = 3072
BLK_ZA, BLK_CB, BLK_CC, BLK_CH, BLK_ZC = 1, 2, 3, 4, 5
NEG_INF = -1e30

ADAM_LR = 0.001
ADAM_B1 = 0.9
ADAM_B2 = 0.999
ADAM_EPS = 1e-08
ADAM_WD = 0.01
ADAM_STEP = 10

ROW_TILE = 512
ATTN_TILE = 256
VMEM_LIMIT = 56 * 1024 * 1024

NT = (((1,), (1,)), ((), ()))
TN = (((0,), (0,)), ((), ()))


def _cparams(*sem):
    return pltpu.CompilerParams(dimension_semantics=sem, vmem_limit_bytes=VMEM_LIMIT)


def _dot(a, b):
    return jnp.dot(a, b, preferred_element_type=F32)


def _dot_nt(a, b):
    return lax.dot_general(a, b, NT, preferred_element_type=F32)


def _dot_tn(a, b):
    return lax.dot_general(a, b, TN, preferred_element_type=F32)


def _rms(x, g):
    r = lax.rsqrt(jnp.mean(x * x, axis=-1, keepdims=True) + EPS)
    return x * r * g, r


def _rms_bwd(dy, x, r, g):
    xh = x * r
    dyg = dy * g
    dx = r * (dyg - xh * jnp.mean(dyg * xh, axis=-1, keepdims=True))
    return dx, dy * xh


def _sigmoid(z):
    return 1.0 / (1.0 + jnp.exp(-z))


def _rope(b, c, sa, sb):
    return b * c + pltpu.roll(b, 96, 1) * sa + pltpu.roll(b, 32, 1) * sb


def _rope_bwd(d, c, sa, sb):
    return d * c + pltpu.roll(d * sa, 32, 1) + pltpu.roll(d * sb, 96, 1)


def _group_mean(x, gmat):
    hi = x.astype(BF16)
    lo = (x - hi.astype(F32)).astype(BF16)
    return _dot(hi, gmat) + _dot(lo, gmat)


def _row_of(col, rows):
    return jnp.transpose(jnp.broadcast_to(col, (rows, 128)))[0:1, :]


def _rope_tables(n_pos):
    half = ROPE // 2
    inv_freq = (np.float32(1.0) / (np.float32(ROPE_THETA) ** (np.arange(half, dtype=np.float32) / np.float32(half))))
    ang = np.arange(n_pos, dtype=np.float32)[:, None] * inv_freq.astype(np.float32)[None, :]
    cos, sin = np.cos(ang).astype(np.float32), np.sin(ang).astype(np.float32)
    z = np.zeros((n_pos, half), np.float32)
    c = np.concatenate([cos, cos, z, z], axis=1)
    sa = np.concatenate([-sin, z, z, z], axis=1)
    sb = np.concatenate([z, sin, z, z], axis=1)
    return jnp.asarray(c), jnp.asarray(sa), jnp.asarray(sb)


W_IN_PIECES = ((0, 384, 752, 0, 64, 0), (384, 64, 752, 384, 448, 1), (448, 304, 752, 512, 512, 1))
W_Q_PIECES = ((0, 96, 256, 0, 0, 0), (96, 96, 256, 96, 96, 1))
W_KV_PIECES = ((0, 128, 128, 0, 0, 0), (128, 128, 128, 512, 512, 1))
W_OUT_PIECES = ((0, 128, 256, 0, 0, 0), (128, 128, 256, 128, 128, 1))


class _StagedGather:
    def __init__(self, pieces):
        self.pieces = pieces

    def scratch(self):
        nk, dma = len(self.pieces), pltpu.SemaphoreType.DMA
        return [dma((nk, 3)), dma((nk, 3)), dma((nk, 3)), dma((nk, 3)), dma((nk,))]

    def run(self, stage, src_ref, out_ref, scr):
        send_sems, recv_sems, fwd_send, fwd_recv, loc_sems = scr
        pieces = self.pieces
        nk = len(pieces)
        x, y, c = lax.axis_index("x"), lax.axis_index("y"), lax.axis_index("c")
        mine = 2 * x + y
        chips = [(1 - x, y), (x, 1 - y), (1 - x, 1 - y)]
        chip_of = [2 * px + py for px, py in chips]
        mesh = pl.DeviceIdType.MESH

        def src(k):
            s0, nr = pieces[k][0], pieces[k][1]
            return src_ref.at[s0:s0 + nr]

        def dst(k, q):
            _, nr, per, first, rest, _ = pieces[k]
            row = per * q + first + (rest - first) * jnp.minimum(q, 1)
            return out_ref.at[pl.ds(pl.multiple_of(row, 16), nr)]

        def ici(k, j, q):
            px, py = chips[j]
            return pltpu.make_async_remote_copy(
                src_ref=src(k), dst_ref=dst(k, q), send_sem=send_sems.at[k, j], recv_sem=recv_sems.at[k, j],
                device_id=(px, py, c), device_id_type=mesh)

        def fwd(k, j):
            ref = dst(k, chip_of[j])
            return pltpu.make_async_remote_copy(
                src_ref=ref, dst_ref=ref, send_sem=fwd_send.at[k, j], recv_sem=fwd_recv.at[k, j],
                device_id=(x, y, 1 - c), device_id_type=mesh)

        local = [pltpu.make_async_copy(src(k), dst(k, mine), loc_sems.at[k]) for k in range(nk)]
        if stage == 0:
            for cp in local:
                cp.start()
        if stage == 2:
            for cp in local:
                cp.wait()
        for half in (0, 1):
            @pl.when(c == half)
            def _(half=half):
                my_k = [k for k in range(nk) if pieces[k][5] == half]
                other_k = [k for k in range(nk) if pieces[k][5] != half]
                for k in my_k:
                    for j in range(3):
                        if stage == 0:
                            ici(k, j, mine).start()
                        elif stage == 1:
                            ici(k, j, chip_of[j]).wait_recv()
                            fwd(k, j).start()
                        else:
                            ici(k, j, mine).wait_send()
                            fwd(k, j).wait_send()
                if stage == 2:
                    for k in other_k:
                        for j in range(3):
                            fwd(k, j).wait_recv()


def _fwd_proj(x2d, meta, tabs, tabs_m, norm_g, w_in_p, q_norm_g, wq_p, kv_norm_g, wkv_p, w_out_shard, nb, s, tm):
    nt = s // tm
    n = nb * nt
    n_steps = n + 1
    c_t, sa_t, sb_t = tabs
    cm_t, sam_t, sbm_t = tabs_m
    gat = _StagedGather(W_OUT_PIECES)
    assert n_steps >= 3

    def body(x_ref, c_ref, sa_ref, sb_ref, mt_ref, cm_ref, sam_ref, sbm_ref,
             g_ref, w_ref, gq_ref, wq_ref, gkv_ref, wkv_ref, wos_ref,
             p_ref, q_ref, k_ref, v_ref, pm_ref, km_ref, vm_ref, wo_ref, *gat_scr):
        i = pl.program_id(0)
        for stage, at in enumerate((0, n_steps - 2, n_steps - 1)):
            @pl.when(i == at)
            def _(stage=stage):
                gat.run(stage, wos_ref, wo_ref, gat_scr)

        def project(xv, c, sa, sb, p_out, q_out, k_out, v_out):
            u, _ = _rms(xv, g_ref[...])
            p = _dot_nt(u.astype(BF16), w_ref[...])
            p_out[...] = p
            qn, _ = _rms(p[:, 0:Q_RANK], gq_ref[...])
            q = _dot_nt(qn.astype(BF16), wq_ref[...])
            kvn, _ = _rms(p[:, Q_RANK:Q_RANK + KV_RANK], gkv_ref[...])
            kv = _dot_nt(kvn.astype(BF16), wkv_ref[...])
            kpe = _rope(p[:, 384:512], c, sa, sb)
            for h in range(HEADS):
                if q_out is not None:
                    pe = _rope(q[:, QK_PAD * h + NOPE:QK_PAD * (h + 1)], c, sa, sb)
                    qh = jnp.concatenate([q[:, QK_PAD * h:QK_PAD * h + NOPE], pe], axis=1)
                    q_out[0, h] = (qh * ATTN_SCALE).astype(BF16)
                k_out[0, h] = jnp.concatenate([kv[:, NOPE * h:NOPE * (h + 1)], kpe], axis=1).astype(BF16)
                v_out[0, h] = kv[:, 512 + VDIM * h:512 + VDIM * (h + 1)].astype(BF16)

        @pl.when(i < n)
        def _():
            project(x_ref[...], c_ref[...], sa_ref[...], sb_ref[...], p_ref, q_ref, k_ref, v_ref)

        @pl.when(i == n)
        def _():
            project(mt_ref[...], cm_ref[...], sam_ref[...], sbm_ref[...], pm_ref, None, km_ref, vm_ref)

    cl = lambda i: jnp.minimum(i, n - 1)
    full = lambda a: pl.BlockSpec(a.shape, lambda i: (0,) * a.ndim)
    const = lambda shape: pl.BlockSpec(shape, lambda i: (0,) * len(shape))
    tab = pl.BlockSpec((tm, 128), lambda i: (cl(i) % nt, 0))
    hb = lambda w: pl.BlockSpec((1, HEADS, tm, w), lambda i: (cl(i) // nt, 0, cl(i) % nt, 0))
    whole = pl.BlockSpec(memory_space=pl.ANY)
    return pl.pallas_call(
        body, name="fwd_proj", grid=(n_steps,),
        in_specs=[pl.BlockSpec((tm, D_MODEL), lambda i: (cl(i), 0)), tab, tab, tab,
                  full(meta), full(cm_t), full(sam_t), full(sbm_t),
                  full(norm_g), full(w_in_p), full(q_norm_g), full(wq_p), full(kv_norm_g), full(wkv_p), whole],
        out_specs=[pl.BlockSpec((tm, IN_PAD), lambda i: (cl(i), 0)), hb(QK_PAD), hb(QK_PAD), hb(VDIM),
                   const((N_META, IN_PAD)), const((1, HEADS, N_META, QK_PAD)), const((1, HEADS, N_META, VDIM)),
                   whole],
        out_shape=[jax.ShapeDtypeStruct((nb * s, IN_PAD), F32),
                   jax.ShapeDtypeStruct((nb, HEADS, s, QK_PAD), BF16),
                   jax.ShapeDtypeStruct((nb, HEADS, s, QK_PAD), BF16),
                   jax.ShapeDtypeStruct((nb, HEADS, s, VDIM), BF16),
                   jax.ShapeDtypeStruct((N_META, IN_PAD), F32),
                   jax.ShapeDtypeStruct((1, HEADS, N_META, QK_PAD), BF16),
                   jax.ShapeDtypeStruct((1, HEADS, N_META, VDIM), BF16),
                   jax.ShapeDtypeStruct((D_MODEL, D_MODEL), BF16)],
        scratch_shapes=gat.scratch(),
        compiler_params=_cparams("arbitrary"),
    )(x2d, c_t, sa_t, sb_t, meta, cm_t, sam_t, sbm_t, norm_g, w_in_p, q_norm_g, wq_p, kv_norm_g, wkv_p, w_out_shard)


def _attn_fwd(q, k, v, km, vm, nb, s, tq):
    nq = s // tq

    def body(q_ref, k_ref, v_ref, km_ref, vm_ref, o_ref, lse_ref, s_scr, p_scr):
        row = lax.broadcasted_iota(jnp.int32, (tq, tq), 0)
        col = lax.broadcasted_iota(jnp.int32, (tq, tq), 1)
        def scores(i):
            slot = i % 2
            qi = q_ref[0, 0, i * tq:(i + 1) * tq, :]
            sm = _dot_nt(qi, km_ref[0, 0])
            m128 = None
            for j in range(i + 1):
                sc = _dot_nt(qi, k_ref[0, 0, j * tq:(j + 1) * tq, :])
                if j == i:
                    sc = jnp.where(col <= row, sc, NEG_INF)
                s_scr[slot, :, j * tq:(j + 1) * tq] = sc
                mx = sc[:, 0:128]
                for c0 in range(128, tq, 128):
                    mx = jnp.maximum(mx, sc[:, c0:c0 + 128])
                m128 = mx if m128 is None else jnp.maximum(m128, mx)
            return sm, jnp.maximum(jnp.max(m128, axis=1, keepdims=True), jnp.max(sm, axis=1, keepdims=True))

        def weighted_sum(i, pm, l):
            n = (i + 1) * tq
            acc = _dot(p_scr[i % 2, :, 0:n], v_ref[0, 0, 0:n, :]) + _dot(pm.astype(BF16), vm_ref[0, 0])
            o_ref[0, 0, i * tq:(i + 1) * tq, :] = acc / l

        nxt, pending = scores(0), None
        for i in range(nq):
            slot = i % 2
            sm, m = nxt
            if i + 1 < nq:
                nxt = scores(i + 1)
            pm = jnp.exp(sm - m)
            l128 = None
            for j in range(i + 1):
                p = jnp.exp(s_scr[slot, :, j * tq:(j + 1) * tq] - m)
                p_scr[slot, :, j * tq:(j + 1) * tq] = p.astype(BF16)
                ps = p[:, 0:128]
                for c0 in range(128, tq, 128):
                    ps = ps + p[:, c0:c0 + 128]
                l128 = ps if l128 is None else l128 + ps
            l = jnp.sum(l128, axis=1, keepdims=True) + jnp.sum(pm, axis=1, keepdims=True)
            lse_ref[0, 0, :, i * tq:(i + 1) * tq] = _row_of(m + jnp.log(l), tq)
            if pending is not None:
                weighted_sum(*pending)
            pending = (i, pm, l)
        weighted_sum(*pending)

    hblk = lambda w: pl.BlockSpec((1, 1, s, w), lambda b, h: (b, h, 0, 0))
    mblk = lambda w: pl.BlockSpec((1, 1, N_META, w), lambda b, h: (0, h, 0, 0))
    return pl.pallas_call(
        body, name="attn_fwd", grid=(nb, HEADS),
        in_specs=[hblk(QK_PAD), hblk(QK_PAD), hblk(VDIM), mblk(QK_PAD), mblk(VDIM)],
        out_specs=[hblk(VDIM), pl.BlockSpec((1, 1, 1, s), lambda b, h: (b, h, 0, 0))],
        out_shape=[jax.ShapeDtypeStruct((nb, HEADS, s, VDIM), F32),
                   jax.ShapeDtypeStruct((nb, HEADS, 1, s), F32)],
        scratch_shapes=[pltpu.VMEM((2, tq, s), F32), pltpu.VMEM((2, tq, s), BF16)],
        compiler_params=_cparams("parallel", "parallel"),
    )(q, k, v, km, vm)


def _shift_rows(a, prev, n_rows):
    rid = lax.broadcasted_iota(jnp.int32, a.shape, 0)
    a1 = jnp.where(rid == 0, prev[7:8, :], pltpu.roll(a, 1, 0))
    a2 = jnp.where(rid == 0, prev[6:7, :], jnp.where(rid == 1, prev[7:8, :], pltpu.roll(a, 2, 0)))
    return a1, a2


def _attn_gate(o, za, ga_h):
    on, r = _rms(o, ga_h)
    return on * (za * _sigmoid(za)), on, r


def _out_fwd_bwd(x2d, tgt2d, o, p, pm, conv_w, ga, gc, gmat, w_out, gf, nb, s, tm):
    nt = s // tm
    r = nb * s
    prev_idx = lambda i: jnp.maximum(i * (tm // 8) - 1, 0)

    def body(x_ref, t_ref, o_ref, za_ref, cb_ref, cc_ref, ch_ref, zc_ref, ccp_ref, chp_ref, mc_ref, mh_ref,
             cw_ref, ga_ref, gc_ref, gm_ref, w_ref, gf_ref,
             dh_ref, dy_ref, dw_ref, dgf_ref, loss_ref):
        i = pl.program_id(0)

        @pl.when(i == 0)
        def _():
            dw_ref[...] = jnp.zeros_like(dw_ref)
            dgf_ref[...] = jnp.zeros_like(dgf_ref)
            loss_ref[...] = jnp.zeros_like(loss_ref)

        ya = []
        for h in range(HEADS):
            y, _, _ = _attn_gate(o_ref[0, h], za_ref[:, VDIM * h:VDIM * (h + 1)],
                                 ga_ref[:, VDIM * h:VDIM * (h + 1)])
            ya.append(y)
        cc = cc_ref[...] * ch_ref[...]
        prev = jnp.where(i % nt == 0, mc_ref[8:16, :] * mh_ref[8:16, :], ccp_ref[...] * chp_ref[...])
        cc1, cc2 = _shift_rows(cc, prev, tm)
        yc = cb_ref[...] * (cw_ref[0:1, :] * cc2 + cw_ref[1:2, :] * cc1 + cw_ref[2:3, :] * cc)
        rg = lax.rsqrt(_group_mean(yc * yc, gm_ref[...]) + EPS)
        zc = zc_ref[...]
        yconv = yc * rg * gc_ref[...] * (zc * _sigmoid(zc))
        ycat = jnp.concatenate(ya + [yconv], axis=1).astype(BF16)
        h2 = x_ref[...] + _dot(ycat, w_ref[...])
        gfv = gf_ref[...]
        y, r2 = _rms(h2, gfv)
        e = y - t_ref[...]
        loss_ref[...] += 0.5 * jnp.sum(e * e) / D_MODEL
        dyv = e * (1.0 / D_MODEL)
        dh2, dgf = _rms_bwd(dyv, h2, r2, gfv)
        dgf_ref[...] += jnp.sum(dgf, axis=0, keepdims=True)
        dh_ref[...] = dh2
        dhb = dh2.astype(BF16)
        dy_ref[...] = _dot_nt(dhb, w_ref[...])
        dw_ref[...] += _dot_tn(ycat, dhb)

    row = lambda w, j: pl.BlockSpec((tm, w), lambda i: (i, j))
    pblk = lambda j: pl.BlockSpec((tm, 512), lambda i: (i, j))
    pprev = lambda j: pl.BlockSpec((8, 512), lambda i: (prev_idx(i), j))
    mblk = lambda j: pl.BlockSpec((N_META, 512), lambda i: (0, j))
    full = lambda a: pl.BlockSpec(a.shape, lambda i: (0,) * a.ndim)
    return pl.pallas_call(
        body, name="out_fwd_bwd", grid=(nb * nt,),
        in_specs=[row(D_MODEL, 0), row(D_MODEL, 0),
                  pl.BlockSpec((1, HEADS, tm, VDIM), lambda i: (i // nt, 0, i % nt, 0)),
                  pblk(BLK_ZA), pblk(BLK_CB), pblk(BLK_CC), pblk(BLK_CH), pblk(BLK_ZC),
                  pprev(BLK_CC), pprev(BLK_CH), mblk(BLK_CC), mblk(BLK_CH),
                  full(conv_w), full(ga), full(gc), full(gmat), full(w_out), full(gf)],
        out_specs=[row(D_MODEL, 0), row(D_MODEL, 0),
                   pl.BlockSpec((D_MODEL, D_MODEL), lambda i: (0, 0)),
                   pl.BlockSpec((1, D_MODEL), lambda i: (0, 0)),
                   pl.BlockSpec((1, 128), lambda i: (0, 0))],
        out_shape=[jax.ShapeDtypeStruct((r, D_MODEL), F32), jax.ShapeDtypeStruct((r, D_MODEL), F32),
                   jax.ShapeDtypeStruct((D_MODEL, D_MODEL), F32), jax.ShapeDtypeStruct((1, D_MODEL), F32),
                   jax.ShapeDtypeStruct((1, 128), F32)],
        compiler_params=_cparams("arbitrary"),
    )(x2d, tgt2d, o, p, p, p, p, p, p, p, pm, pm, conv_w, ga, gc, gmat, w_out, gf)


def _gate_bwd(dycat, o, p, pm, conv_w, ga, gc, gmat, nb, s, tm):
    nt = s // tm
    r = nb * s
    ext = tm + 8
    prev_idx = lambda i: jnp.maximum(i * (tm // 8) - 1, 0)
    next_idx = lambda i: jnp.minimum((i + 1) * (tm // 8), r // 8 - 1)

    def body(dya_ref, dyc_ref, dycn_ref, o_ref, za_ref, cb_ref, cbn_ref, cc_ref, ccp_ref, ccn_ref,
             ch_ref, chp_ref, chn_ref, zc_ref, zcn_ref, mc_ref, mh_ref, cw_ref, ga_ref, gc_ref, gm_ref,
             dpb_ref, do_ref, dl_ref, dccm_ref, dga_ref, dgc_ref, dcw_ref):
        i = pl.program_id(0)

        @pl.when(i == 0)
        def _():
            dga_ref[...] = jnp.zeros_like(dga_ref)
            dgc_ref[...] = jnp.zeros_like(dgc_ref)
            dcw_ref[...] = jnp.zeros_like(dcw_ref)

        dga = []
        for h in range(HEADS):
            hs = slice(VDIM * h, VDIM * (h + 1))
            oh, za, gah, dya = o_ref[0, h], za_ref[:, hs], ga_ref[:, hs], dya_ref[:, hs]
            sg = _sigmoid(za)
            on, ro = _rms(oh, gah)
            don = dya * (za * sg)
            dpb_ref[:, hs] = (dya * on * (sg * (1.0 + za * (1.0 - sg)))).astype(BF16)
            do, dg = _rms_bwd(don, oh, ro, gah)
            dga.append(jnp.sum(dg, axis=0, keepdims=True))
            dob = do.astype(BF16)
            do_ref[0, h] = dob
            dl_ref[0, h] = _row_of(jnp.sum(dob.astype(F32) * oh, axis=1, keepdims=True), tm)
        dga_ref[...] += jnp.concatenate(dga, axis=1)

        cat = lambda a, b: jnp.concatenate([a[...], b[...]], axis=0)
        cch = cat(cc_ref, ccn_ref)
        chh = cat(ch_ref, chn_ref)
        cb = cat(cb_ref, cbn_ref)
        zc = cat(zc_ref, zcn_ref)
        dy = cat(dyc_ref, dycn_ref)
        first = i % nt == 0
        last = i % nt == nt - 1
        cc = cch * chh
        prev = jnp.where(first, mc_ref[8:16, :] * mh_ref[8:16, :], ccp_ref[...] * chp_ref[...])
        cc1, cc2 = _shift_rows(cc, prev, ext)
        w0, w1, w2 = cw_ref[0:1, :], cw_ref[1:2, :], cw_ref[2:3, :]
        dw = w0 * cc2 + w1 * cc1 + w2 * cc
        yc = cb * dw
        rg = lax.rsqrt(_group_mean(yc * yc, gm_ref[...]) + EPS)
        ych = yc * rg
        gcv = gc_ref[...]
        sg = _sigmoid(zc)
        dycn = dy * (zc * sg)
        dzc = dy * (ych * gcv) * (sg * (1.0 + zc * (1.0 - sg)))
        dgc_ref[...] += jnp.sum((dycn * ych)[:tm], axis=0, keepdims=True)
        dycg = dycn * gcv
        dyc = rg * (dycg - ych * _group_mean(dycg * ych, gm_ref[...]))
        rid = lax.broadcasted_iota(jnp.int32, (ext, CONV_W), 0)
        ddw = jnp.where(jnp.logical_and(last, rid >= tm), 0.0, dyc * cb)
        dcb = dyc * dw
        dcc = w2 * ddw + w1 * pltpu.roll(ddw, ext - 1, 0) + w0 * pltpu.roll(ddw, ext - 2, 0)
        dpb_ref[:, 512:1024] = dcb[:tm].astype(BF16)
        dpb_ref[:, 1024:1536] = (dcc * chh)[:tm].astype(BF16)
        dpb_ref[:, 1536:2048] = (dcc * cch)[:tm].astype(BF16)
        dpb_ref[:, 2048:2560] = dzc[:tm].astype(BF16)
        rs = lambda a: jnp.sum(a[:tm], axis=0, keepdims=True)
        dcw_ref[0:1, :] += rs(ddw * cc2)
        dcw_ref[1:2, :] += rs(ddw * cc1)
        dcw_ref[2:3, :] += rs(ddw * cc)

        @pl.when(first)
        def _():
            d0, d1 = ddw[0:1, :], ddw[1:2, :]
            r8 = lax.broadcasted_iota(jnp.int32, (8, CONV_W), 0)
            dccm_ref[0] = jnp.where(r8 == 7, w1 * d0 + w0 * d1, jnp.where(r8 == 6, w0 * d0, 0.0))

    row = lambda j: pl.BlockSpec((tm, 512), lambda i: (i, j))
    prv = lambda j: pl.BlockSpec((8, 512), lambda i: (prev_idx(i), j))
    nxt = lambda j: pl.BlockSpec((8, 512), lambda i: (next_idx(i), j))
    mblk = lambda j: pl.BlockSpec((N_META, 512), lambda i: (0, j))
    full = lambda a: pl.BlockSpec(a.shape, lambda i: (0,) * a.ndim)
    hb = lambda w: pl.BlockSpec((1, HEADS, tm, w), lambda i: (i // nt, 0, i % nt, 0))
    acc = lambda rr: pl.BlockSpec((rr, 512), lambda i: (0, 0))
    return pl.pallas_call(
        body, name="gate_bwd", grid=(nb * nt,),
        in_specs=[row(0), row(1), nxt(1), hb(VDIM),
                  row(BLK_ZA), row(BLK_CB), nxt(BLK_CB), row(BLK_CC), prv(BLK_CC), nxt(BLK_CC),
                  row(BLK_CH), prv(BLK_CH), nxt(BLK_CH), row(BLK_ZC), nxt(BLK_ZC),
                  mblk(BLK_CC), mblk(BLK_CH), full(conv_w), full(ga), full(gc), full(gmat)],
        out_specs=[pl.BlockSpec((tm, 2560), lambda i: (i, 0)), hb(VDIM),
                   pl.BlockSpec((1, HEADS, 1, tm), lambda i: (i // nt, 0, 0, i % nt)),
                   pl.BlockSpec((1, 8, 512), lambda i: (i // nt, 0, 0)),
                   acc(1), acc(1), acc(8)],
        out_shape=[jax.ShapeDtypeStruct((r, 2560), BF16), jax.ShapeDtypeStruct((nb, HEADS, s, VDIM), BF16),
                   jax.ShapeDtypeStruct((nb, HEADS, 1, s), F32), jax.ShapeDtypeStruct((nb, 8, 512), F32),
                   jax.ShapeDtypeStruct((1, 512), F32), jax.ShapeDtypeStruct((1, 512), F32),
                   jax.ShapeDtypeStruct((8, 512), F32)],
        compiler_params=_cparams("arbitrary"),
    )(dycat, dycat, dycat, o, p, p, p, p, p, p, p, p, p, p, p, pm, pm, conv_w, ga, gc, gmat)


class _StagedReduce:
    LOC, PRE_S, PRE_R, ICI_S, ICI_R, POST_S, POST_R, OUT, N_SEM = 0, 1, 2, 3, 6, 9, 10, 11, 12

    def __init__(self, shard_shape):
        self.half = (shard_shape[0] // 2, shard_shape[1])

    def scratch(self):
        h = self.half
        return [pltpu.VMEM((4,) + h, F32), pltpu.VMEM((4,) + h, F32), pltpu.VMEM((4,) + h, BF16),
                pltpu.VMEM((3,) + h, BF16), pltpu.VMEM(h, F32), pltpu.SemaphoreType.DMA((self.N_SEM,))]

    def run(self, stage, pin, gout, scr):
        own, sib, wire, rbuf, fin, sems = scr
        r2 = self.half[0]
        x, y, c = lax.axis_index("x"), lax.axis_index("y"), lax.axis_index("c")
        mine = 2 * x + y
        sibling = (x, y, 1 - c)
        chips = [(1 - x, y), (x, 1 - y), (1 - x, 1 - y)]
        rows = lambda half: pl.ds(pl.multiple_of(half * r2, r2), r2)
        mesh = pl.DeviceIdType.MESH

        loc = pltpu.make_async_copy(pin.at[:, rows(c), :], own, sems.at[self.LOC])
        pre = pltpu.make_async_remote_copy(
            src_ref=pin.at[:, rows(1 - c), :], dst_ref=sib, send_sem=sems.at[self.PRE_S],
            recv_sem=sems.at[self.PRE_R], device_id=sibling, device_id_type=mesh)

        def ici(j):
            px, py = chips[j]
            return pltpu.make_async_remote_copy(
                src_ref=wire.at[2 * px + py], dst_ref=rbuf.at[j], send_sem=sems.at[self.ICI_S + j],
                recv_sem=sems.at[self.ICI_R + j], device_id=(px, py, c), device_id_type=mesh)

        def post(half):
            return pltpu.make_async_remote_copy(
                src_ref=fin, dst_ref=gout.at[rows(half), :], send_sem=sems.at[self.POST_S],
                recv_sem=sems.at[self.POST_R], device_id=sibling, device_id_type=mesh)

        keep = pltpu.make_async_copy(fin, gout.at[rows(c), :], sems.at[self.OUT])
        if stage == 0:
            loc.start()
            pre.start()
        elif stage == 1:
            loc.wait()
            pre.wait_recv()
            for blk in range(4):
                tot = own[blk] + sib[blk]
                own[blk] = tot
                wire[blk] = tot.astype(BF16)
            for j in range(3):
                ici(j).start()
        elif stage == 2:
            for j in range(3):
                ici(j).wait_recv()
            tot = own[mine]
            for j in range(3):
                tot = tot + rbuf[j].astype(F32)
            fin[...] = tot
            post(c).start()
            keep.start()
        else:
            post(1 - c).wait_recv()
            pre.wait_send()
            for j in range(3):
                ici(j).wait_send()
            post(c).wait_send()
            keep.wait()


def _attn_bwd(q, k, v, do, lse, delta, km, vm, early, nb, s, t):
    n = s // t
    ne = len(early)
    reds = [_StagedReduce(a.shape[1:]) for a in early]
    n_steps = HEADS * nb
    assert n_steps >= 4

    def body(q_ref, k_ref, v_ref, do_ref, lse_ref, dl_ref, km_ref, vm_ref, *rest):
        pin_refs, rest = rest[:ne], rest[ne:]
        dq_ref, dk_ref, dv_ref, dkm_ref, dvm_ref = rest[:5]
        gout_refs, (p_scr, ds_scr, dq_acc), red_scr = rest[5:5 + ne], rest[5 + ne:8 + ne], rest[8 + ne:]
        b = pl.program_id(1)
        step = pl.program_id(0) * nb + b
        for stage, at in enumerate((0, 1, n_steps - 2, n_steps - 1)):
            @pl.when(step == at)
            def _(stage=stage):
                for a, red in enumerate(reds):
                    red.run(stage, pin_refs[a], gout_refs[a], red_scr[6 * a:6 * a + 6])

        @pl.when(b == 0)
        def _():
            dkm_ref[...] = jnp.zeros_like(dkm_ref)
            dvm_ref[...] = jnp.zeros_like(dvm_ref)

        kr = lax.broadcasted_iota(jnp.int32, (t, t), 0)
        qc = lax.broadcasted_iota(jnp.int32, (t, t), 1)
        km_v, vm_v = km_ref[0, 0], vm_ref[0, 0]
        ptm = jnp.exp(_dot_nt(km_v, q_ref[0, 0]) - lse_ref[0, 0])
        dstm = (ptm * (_dot_nt(vm_v, do_ref[0, 0]) - dl_ref[0, 0])).astype(BF16)
        dkm_ref[0] += _dot(dstm, q_ref[0, 0])
        dvm_ref[0] += _dot(ptm.astype(BF16), do_ref[0, 0])
        dq_acc[...] = _dot_tn(dstm, km_v)
        def tiles(j):
            slot = j % 2
            kj = k_ref[0, 0, j * t:(j + 1) * t, :]
            vj = v_ref[0, 0, j * t:(j + 1) * t, :]
            def products(i):
                cs = slice(i * t, (i + 1) * t)
                return _dot_nt(kj, q_ref[0, 0, cs, :]), _dot_nt(vj, do_ref[0, 0, cs, :])

            nxt, pending = products(j), None
            for i in range(j, n):
                cs = slice(i * t, (i + 1) * t)
                st, dpt = nxt
                if i + 1 < n:
                    nxt = products(i + 1)
                if i == j:
                    st = jnp.where(kr <= qc, st, NEG_INF)
                pt = jnp.exp(st - lse_ref[0, 0, :, cs])
                dst = (pt * (dpt - dl_ref[0, 0, :, cs])).astype(BF16)
                p_scr[slot, :, cs] = pt.astype(BF16)
                ds_scr[slot, :, cs] = dst
                if pending is not None:
                    dq_acc[pending[0], :] += _dot_tn(pending[1], kj)
                pending = (cs, dst)
            dq_acc[pending[0], :] += _dot_tn(pending[1], kj)

        for j in range(n):
            slot = j % 2
            tiles(j)
            dv_ref[0, 0, j * t:(j + 1) * t, :] = _dot(p_scr[slot, :, j * t:s], do_ref[0, 0, j * t:s, :]).astype(BF16)
            dk_ref[0, 0, j * t:(j + 1) * t, :] = _dot(ds_scr[slot, :, j * t:s], q_ref[0, 0, j * t:s, :]).astype(BF16)
        dq_ref[0, 0] = dq_acc[...].astype(BF16)

    big = lambda w: pl.BlockSpec((1, 1, s, w), lambda h, b: (b, h, 0, 0))
    rowv = pl.BlockSpec((1, 1, 1, s), lambda h, b: (b, h, 0, 0))
    mk = lambda w: pl.BlockSpec((1, 1, N_META, w), lambda h, b: (0, h, 0, 0))
    mo = lambda w: pl.BlockSpec((1, N_META, w), lambda h, b: (h, 0, 0))
    return pl.pallas_call(
        body, name="attn_bwd", grid=(HEADS, nb),
        in_specs=[big(QK_PAD), big(QK_PAD), big(VDIM), big(VDIM), rowv, rowv, mk(QK_PAD), mk(VDIM)]
        + [pl.BlockSpec(memory_space=pl.ANY)] * ne,
        out_specs=[big(QK_PAD), big(QK_PAD), big(VDIM), mo(QK_PAD), mo(VDIM)]
        + [pl.BlockSpec(memory_space=pl.ANY)] * ne,
        out_shape=[jax.ShapeDtypeStruct((nb, HEADS, s, QK_PAD), BF16),
                   jax.ShapeDtypeStruct((nb, HEADS, s, QK_PAD), BF16),
                   jax.ShapeDtypeStruct((nb, HEADS, s, VDIM), BF16),
                   jax.ShapeDtypeStruct((HEADS, N_META, QK_PAD), F32),
                   jax.ShapeDtypeStruct((HEADS, N_META, VDIM), F32)]
        + [jax.ShapeDtypeStruct(a.shape[1:], F32) for a in early],
        scratch_shapes=[pltpu.VMEM((2, t, s), BF16), pltpu.VMEM((2, t, s), BF16), pltpu.VMEM((s, QK_PAD), F32)]
        + [sc for red in reds for sc in red.scratch()],
        compiler_params=_cparams("arbitrary", "arbitrary"),
    )(q, k, v, do, lse, delta, km, vm, *early)


def _up_bwd(dq, dk, dv, dkm, dvm, p, pm, tabs, tabs_m, wq_p, wkv_p, gq, gkv, nb, s, tm):
    nt = s // tm
    n = nb * nt
    c_t, sa_t, sb_t = tabs
    cm_t, sam_t, sbm_t = tabs_m

    def kv_path(dkh, dvh, pa, c, sa, sb, wkv, gkvv):
        dkpe = dkh[0][:, NOPE:]
        for h in range(1, HEADS):
            dkpe = dkpe + dkh[h][:, NOPE:]
        dkr = _rope_bwd(dkpe, c, sa, sb)
        dkv = jnp.concatenate([d[:, :NOPE] for d in dkh] + list(dvh), axis=1).astype(BF16)
        ckv = pa[:, Q_RANK:Q_RANK + KV_RANK]
        kvn, rkv = _rms(ckv, gkvv)
        dckv, dg = _rms_bwd(_dot(dkv, wkv), ckv, rkv, gkvv)
        return dckv, dkr, kvn.astype(BF16), dkv, jnp.sum(dg, axis=0, keepdims=True)

    def body(dq_ref, dk_ref, dv_ref, pa_ref, c_ref, sa_ref, sb_ref,
             dkm_ref, dvm_ref, pam_ref, cm_ref, sam_ref, sbm_ref,
             wq_ref, wkv_ref, gq_ref, gkv_ref,
             dpa_ref, dpam_ref, pq_ref, pkv_ref, dgq_ref, dgkv_ref, dwq_ref, dwkv_ref):
        i = pl.program_id(0)

        @pl.when(i == 0)
        def _():
            dwq_ref[...] = jnp.zeros_like(dwq_ref)
            dwkv_ref[...] = jnp.zeros_like(dwkv_ref)
            dgq_ref[...] = jnp.zeros_like(dgq_ref)
            dgkv_ref[...] = jnp.zeros_like(dgkv_ref)

        @pl.when(i < n)
        def _():
            c, sa, sb = c_ref[...], sa_ref[...], sb_ref[...]
            pa = pa_ref[...]
            parts = []
            for h in range(HEADS):
                dqh = dq_ref[0, h].astype(F32) * ATTN_SCALE
                parts += [dqh[:, :NOPE], _rope_bwd(dqh[:, NOPE:], c, sa, sb)]
            dql = jnp.concatenate(parts, axis=1).astype(BF16)
            cq = pa[:, 0:Q_RANK]
            gqv = gq_ref[...]
            qn, rq = _rms(cq, gqv)
            dwq_ref[...] += _dot_tn(dql, qn.astype(BF16))
            dcq, dg = _rms_bwd(_dot(dql, wq_ref[...]), cq, rq, gqv)
            dgq_ref[...] += jnp.sum(dg, axis=0, keepdims=True)
            dckv, dkr, kvn, dkv, dgk = kv_path([dk_ref[0, h].astype(F32) for h in range(HEADS)],
                                               [dv_ref[0, h].astype(F32) for h in range(HEADS)],
                                               pa, c, sa, sb, wkv_ref[...], gkv_ref[...])
            dwkv_ref[...] += _dot_tn(dkv, kvn)
            dgkv_ref[...] += dgk
            dpa_ref[...] = jnp.concatenate([dcq, dckv, dkr], axis=1).astype(BF16)

        @pl.when(i == n)
        def _():
            dckv, dkr, kvn, dkv, dgk = kv_path([dkm_ref[h] for h in range(HEADS)],
                                               [dvm_ref[h] for h in range(HEADS)],
                                               pam_ref[...], cm_ref[...], sam_ref[...], sbm_ref[...],
                                               wkv_ref[...], gkv_ref[...])
            dwkv_ref[...] += _dot_tn(dkv, kvn)
            dgkv_ref[...] += dgk
            dpam_ref[...] = jnp.concatenate([jnp.zeros((N_META, Q_RANK), F32), dckv, dkr], axis=1)
            for h in range(HEADS):
                pq_ref[h] = dwq_ref[QK_PAD * h:QK_PAD * h + NOPE + ROPE, :]
                pkv_ref[h, 0:NOPE, :] = dwkv_ref[NOPE * h:NOPE * (h + 1), :]
                pkv_ref[h, NOPE:NOPE + VDIM, :] = dwkv_ref[512 + VDIM * h:512 + VDIM * (h + 1), :]

    cl = lambda i: jnp.minimum(i, n - 1)
    hb = lambda w: pl.BlockSpec((1, HEADS, tm, w), lambda i: (cl(i) // nt, 0, cl(i) % nt, 0))
    tab = pl.BlockSpec((tm, 128), lambda i: (cl(i) % nt, 0))
    full = lambda a: pl.BlockSpec(a.shape, lambda i: (0,) * a.ndim)
    const = lambda shape: pl.BlockSpec(shape, lambda i: (0,) * len(shape))
    return pl.pallas_call(
        body, name="up_bwd", grid=(n + 1,),
        in_specs=[hb(QK_PAD), hb(QK_PAD), hb(VDIM), pl.BlockSpec((tm, 512), lambda i: (cl(i), 0)), tab, tab, tab,
                  full(dkm), full(dvm), pl.BlockSpec((N_META, 512), lambda i: (0, 0)),
                  full(cm_t), full(sam_t), full(sbm_t), full(wq_p), full(wkv_p), full(gq), full(gkv)],
        out_specs=[pl.BlockSpec((tm, 512), lambda i: (cl(i), 0)), const((N_META, 512)),
                   const((HEADS, NOPE + ROPE, Q_RANK)), const((HEADS, NOPE + VDIM, KV_RANK)),
                   const((1, Q_RANK)), const((1, KV_RANK))],
        out_shape=[jax.ShapeDtypeStruct((nb * s, 512), BF16), jax.ShapeDtypeStruct((N_META, 512), F32),
                   jax.ShapeDtypeStruct((HEADS, NOPE + ROPE, Q_RANK), F32),
                   jax.ShapeDtypeStruct((HEADS, NOPE + VDIM, KV_RANK), F32),
                   jax.ShapeDtypeStruct((1, Q_RANK), F32), jax.ShapeDtypeStruct((1, KV_RANK), F32)],
        scratch_shapes=[pltpu.VMEM((HEADS * QK_PAD, Q_RANK), F32), pltpu.VMEM((1024, KV_RANK), F32)],
        compiler_params=_cparams("arbitrary"),
    )(dq, dk, dv, p, c_t, sa_t, sb_t, dkm, dvm, pm, cm_t, sam_t, sbm_t, wq_p, wkv_p, gq, gkv)


def _in_bwd(x2d, dh2, dpa, dpb, meta, dpam, dccm, pm, w_in_p, norm_g, nb, s, tm):
    nt = s // tm
    n = nb * nt

    def body(x_ref, dh_ref, dpa_ref, dpb_ref, mt_ref, dpam_ref, dccm_ref, mc_ref, mh_ref, w_ref, g_ref,
             gx_ref, gm_ref, dw_hbm, dg_ref, acc_ref, sems):
        i = pl.program_id(0)

        @pl.when(i == 0)
        def _():
            acc_ref[...] = jnp.zeros_like(acc_ref)
            dg_ref[...] = jnp.zeros_like(dg_ref)

        def rows(x, dp, dres):
            g = g_ref[...]
            dpb16 = dp.astype(BF16)
            du = _dot(dpb16, w_ref[...])
            u, r1 = _rms(x, g)
            acc_ref[...] += _dot_tn(dpb16, u.astype(BF16))
            dx, dg = _rms_bwd(du, x, r1, g)
            dg_ref[...] += jnp.sum(dg, axis=0, keepdims=True)
            return dx if dres is None else dx + dres

        @pl.when(i < n)
        def _():
            dp = jnp.concatenate([dpa_ref[...], dpb_ref[...]], axis=1)
            gx_ref[...] = rows(x_ref[...], dp, dh_ref[...])

        @pl.when(i == n)
        def _():
            dcc = dccm_ref[0]
            for b in range(1, nb):
                dcc = dcc + dccm_ref[b]
            z8 = jnp.zeros((8, CONV_W), F32)
            dc = jnp.concatenate([z8, dcc * mh_ref[8:16, :]], axis=0)
            dh = jnp.concatenate([z8, dcc * mc_ref[8:16, :]], axis=0)
            z = jnp.zeros((N_META, CONV_W), F32)
            dp = jnp.concatenate([dpam_ref[...], z, z, dc, dh, z], axis=1)
            gm_ref[...] = rows(mt_ref[...], dp, None)
            per = IN_DIM // 4
            cps = [pltpu.make_async_copy(acc_ref.at[0:448], dw_hbm.at[0, 0:448], sems.at[0]),
                   pltpu.make_async_copy(acc_ref.at[512:per + 64], dw_hbm.at[0, 448:per], sems.at[1])]
            for qq in range(1, 4):
                cps.append(pltpu.make_async_copy(acc_ref.at[per * qq + 64:per * (qq + 1) + 64], dw_hbm.at[qq],
                                                 sems.at[qq + 1]))
            for cp in cps:
                cp.start()
            for cp in cps:
                cp.wait()

    cl = lambda i: jnp.minimum(i, n - 1)
    row = lambda w: pl.BlockSpec((tm, w), lambda i: (cl(i), 0))
    full = lambda a: pl.BlockSpec(a.shape, lambda i: (0,) * a.ndim)
    mblk = lambda j: pl.BlockSpec((N_META, 512), lambda i: (0, j))
    return pl.pallas_call(
        body, name="in_bwd", grid=(n + 1,),
        in_specs=[row(D_MODEL), row(D_MODEL), row(512), row(2560), full(meta), full(dpam), full(dccm),
                  mblk(BLK_CC), mblk(BLK_CH), full(w_in_p), full(norm_g)],
        out_specs=[row(D_MODEL), pl.BlockSpec((N_META, D_MODEL), lambda i: (0, 0)),
                   pl.BlockSpec(memory_space=pl.ANY), pl.BlockSpec((1, D_MODEL), lambda i: (0, 0))],
        out_shape=[jax.ShapeDtypeStruct((nb * s, D_MODEL), F32), jax.ShapeDtypeStruct((N_META, D_MODEL), F32),
                   jax.ShapeDtypeStruct((4, IN_DIM // 4, D_MODEL), F32), jax.ShapeDtypeStruct((1, D_MODEL), F32)],
        scratch_shapes=[pltpu.VMEM((IN_PAD, D_MODEL), F32), pltpu.SemaphoreType.DMA((5,))],
        compiler_params=_cparams("arbitrary"),
    )(x2d, dh2, dpa, dpb, meta, dpam, dccm, pm, pm, w_in_p, norm_g)


def _gather_weights(split, pieces, out_rows, whole, zero_fills):
    ns, nw, nz = len(split), len(whole), len(zero_fills)
    flat = [(a, pc) for a in range(ns) for pc in pieces[a]]
    nk = len(flat)

    def body(*refs):
        ins, wins, zins = refs[:ns], refs[ns:ns + nw], refs[ns + nw:ns + nw + nz]
        outs, wouts = refs[ns + nw + nz:2 * ns + nw + nz], refs[2 * ns + nw + nz:2 * (ns + nw) + nz]
        send_sems, recv_sems, fwd_send, fwd_recv, loc_sems, w_send, w_recv, w_loc, z_sems = refs[2 * (ns + nw) + nz:]
        x, y, c = lax.axis_index("x"), lax.axis_index("y"), lax.axis_index("c")
        mine = 2 * x + y
        chips = [(1 - x, y), (x, 1 - y), (1 - x, 1 - y)]
        chip_of = [2 * px + py for px, py in chips]

        def src(k):
            a, (s0, nr, _, _, _, _) = flat[k]
            return ins[a].at[s0:s0 + nr]

        def dst(k, q):
            a, (_, nr, per, first, rest, _) = flat[k]
            row = per * q + first + (rest - first) * jnp.minimum(q, 1)
            return outs[a].at[pl.ds(pl.multiple_of(row, 16), nr)]

        def ici(k, j, q):
            px, py = chips[j]
            return pltpu.make_async_remote_copy(
                src_ref=src(k), dst_ref=dst(k, q), send_sem=send_sems.at[k, j], recv_sem=recv_sems.at[k, j],
                device_id=(px, py, c), device_id_type=pl.DeviceIdType.MESH)

        def fwd(k, j):
            ref = dst(k, chip_of[j])
            return pltpu.make_async_remote_copy(
                src_ref=ref, dst_ref=ref, send_sem=fwd_send.at[k, j], recv_sem=fwd_recv.at[k, j],
                device_id=(x, y, 1 - c), device_id_type=pl.DeviceIdType.MESH)

        def wcopy(b, j, q):
            px, py = chips[j]
            return pltpu.make_async_remote_copy(
                src_ref=wins[b], dst_ref=wouts[b].at[q], send_sem=w_send.at[b, j], recv_sem=w_recv.at[b, j],
                device_id=(px, py, c), device_id_type=pl.DeviceIdType.MESH)

        local = [pltpu.make_async_copy(src(k), dst(k, mine), loc_sems.at[k]) for k in range(nk)]
        local += [pltpu.make_async_copy(wins[b], wouts[b].at[mine], w_loc.at[b]) for b in range(nw)]
        for z, (a, _, row0) in enumerate(zero_fills):
            local.append(pltpu.make_async_copy(zins[z], outs[a].at[row0:row0 + zins[z].shape[0]], z_sems.at[z]))
        wsends = [wcopy(b, j, mine) for b in range(nw) for j in range(3)]
        for cp in local + wsends:
            cp.start()

        for half in (0, 1):
            @pl.when(c == half)
            def _(half=half):
                my_k = [k for k in range(nk) if flat[k][1][5] == half]
                other_k = [k for k in range(nk) if flat[k][1][5] != half]
                sends = [ici(k, j, mine) for k in my_k for j in range(3)]
                for cp in sends:
                    cp.start()
                passed = []
                for k in my_k:
                    for j in range(3):
                        ici(k, j, chip_of[j]).wait_recv()
                        cp = fwd(k, j)
                        cp.start()
                        passed.append(cp)
                for k in other_k:
                    for j in range(3):
                        fwd(k, j).wait_recv()
                for cp in sends + passed:
                    cp.wait_send()

        for b in range(nw):
            for j in range(3):
                wcopy(b, j, chip_of[j]).wait_recv()
        for cp in wsends:
            cp.wait_send()
        for cp in local:
            cp.wait()

    vmem = pl.BlockSpec(memory_space=pltpu.VMEM)
    dma = pltpu.SemaphoreType.DMA
    zeros = [z for _, z, _ in zero_fills]
    return pl.pallas_call(
        body, name="gather_weights",
        in_specs=[vmem] * (ns + nw + nz), out_specs=[vmem] * (ns + nw),
        out_shape=([jax.ShapeDtypeStruct((out_rows[a], split[a].shape[1]), split[a].dtype) for a in range(ns)]
                   + [jax.ShapeDtypeStruct((4,) + w.shape, w.dtype) for w in whole]),
        scratch_shapes=[dma((nk, 3)), dma((nk, 3)), dma((nk, 3)), dma((nk, 3)), dma((nk,)),
                        dma((nw, 3)), dma((nw, 3)), dma((nw,)), dma((nz,))],
        compiler_params=pltpu.CompilerParams(vmem_limit_bytes=VMEM_LIMIT),
    )(*split, *whole, *zeros)


def _reduce_grads(parts, small):
    n = len(parts)
    shapes = [a.shape[1:] for a in parts]
    halves = [(sh[0] // 2, sh[1]) for sh in shapes]

    def body(*refs):
        pin, sm_in = refs[:n], refs[n]
        gout, sm_out = refs[n + 1:2 * n + 1], refs[2 * n + 1]
        scr = refs[2 * n + 2:]
        own, sib, wire, rbuf = scr[:n], scr[n:2 * n], scr[2 * n:3 * n], scr[3 * n:4 * n]
        (sbuf, send_sems, recv_sems, loc_sems, pre_send, pre_recv, post_send, post_recv,
         sm_send, sm_recv) = scr[4 * n:]
        x, y, c = lax.axis_index("x"), lax.axis_index("y"), lax.axis_index("c")
        mine = 2 * x + y
        me = 4 * x + 2 * y + c
        sibling = (x, y, 1 - c)
        chips = [(1 - x, y), (x, 1 - y), (1 - x, 1 - y)]

        def rows(a, half):
            r2 = halves[a][0]
            return pl.ds(pl.multiple_of(half * r2, r2), r2)

        chip_of = [2 * px + py for px, py in chips]
        blocks = chip_of + [mine]

        def pre(a, k):
            return pltpu.make_async_remote_copy(
                src_ref=pin[a].at[blocks[k], rows(a, 1 - c), :], dst_ref=sib[a].at[blocks[k]],
                send_sem=pre_send.at[a, k], recv_sem=pre_recv.at[a, k], device_id=sibling,
                device_id_type=pl.DeviceIdType.MESH)

        def ici(a, j):
            px, py = chips[j]
            return pltpu.make_async_remote_copy(
                src_ref=wire[a].at[2 * px + py], dst_ref=rbuf[a].at[j], send_sem=send_sems.at[a, j],
                recv_sem=recv_sems.at[a, j], device_id=(px, py, c), device_id_type=pl.DeviceIdType.MESH)

        def post(a, half):
            ref = gout[a].at[rows(a, half), :]
            return pltpu.make_async_remote_copy(
                src_ref=ref, dst_ref=ref, send_sem=post_send.at[a], recv_sem=post_recv.at[a],
                device_id=sibling, device_id_type=pl.DeviceIdType.MESH)

        def small_copy(kk):
            peer = (x ^ (kk >> 2), y ^ ((kk >> 1) & 1), c ^ (kk & 1))
            return pltpu.make_async_remote_copy(
                src_ref=sm_in, dst_ref=sbuf.at[kk], send_sem=sm_send.at[kk - 1], recv_sem=sm_recv.at[kk - 1],
                device_id=peer, device_id_type=pl.DeviceIdType.MESH)

        local = [[pltpu.make_async_copy(pin[a].at[blocks[k], rows(a, c), :], own[a].at[blocks[k]], loc_sems.at[a, k])
                  for k in range(4)] for a in range(n)]
        pres = [[pre(a, k) for k in range(4)] for a in range(n)]
        smalls = [small_copy(kk) for kk in range(1, 8)]
        for a in range(n):
            for k in range(4):
                local[a][k].start()
                pres[a][k].start()
        for cp in smalls:
            cp.start()
        sbuf[0] = sm_in[...]
        sends = []
        for a in range(n):
            for k in range(4):
                local[a][k].wait()
                pres[a][k].wait_recv()
                tot = own[a][blocks[k]] + sib[a][blocks[k]]
                own[a][blocks[k]] = tot
                if k < 3:
                    wire[a][blocks[k]] = tot.astype(BF16)
                    cp = ici(a, k)
                    cp.start()
                    sends.append(cp)
        for cp in smalls:
            cp.wait_recv()
        total = sbuf[me]
        for d in range(1, 8):
            total = total + sbuf[me ^ d]
        sm_out[...] = total
        posts = []
        for a in range(n):
            for j in range(3):
                ici(a, j).wait_recv()
            fin = own[a][mine]
            for j in range(3):
                fin = fin + rbuf[a][j].astype(F32)
            gout[a][rows(a, c), :] = fin
            cp = post(a, c)
            cp.start()
            posts.append(cp)
        for a in range(n):
            post(a, 1 - c).wait_recv()
        for cp in [cp for row in pres for cp in row] + sends + smalls + posts:
            cp.wait_send()

    vmem = pl.BlockSpec(memory_space=pltpu.VMEM)
    dma = pltpu.SemaphoreType.DMA
    return pl.pallas_call(
        body, name="reduce_grads",
        in_specs=[vmem] * (n + 1), out_specs=[vmem] * (n + 1),
        out_shape=[jax.ShapeDtypeStruct(sh, F32) for sh in shapes] + [jax.ShapeDtypeStruct(small.shape, F32)],
        scratch_shapes=([pltpu.VMEM((4,) + hs, F32) for hs in halves] + [pltpu.VMEM((4,) + hs, F32) for hs in halves]
                        + [pltpu.VMEM((4,) + hs, BF16) for hs in halves]
                        + [pltpu.VMEM((3,) + hs, BF16) for hs in halves]
                        + [pltpu.VMEM((8,) + small.shape, F32), dma((n, 3)), dma((n, 3)), dma((n, 4)),
                           dma((n, 4)), dma((n, 4)), dma((n,)), dma((n,)), dma((7,)), dma((7,))]),
        compiler_params=pltpu.CompilerParams(vmem_limit_bytes=VMEM_LIMIT),
    )(*parts, small)


def _adamw_update(w_ref, g_ref, m_ref, v_ref, d_ref, nm_ref, nv_ref):
    gv = g_ref[...]
    nm = ADAM_B1 * m_ref[...] + (1.0 - ADAM_B1) * gv
    nv = ADAM_B2 * v_ref[...] + (1.0 - ADAM_B2) * (gv * gv)
    m_hat = nm / (1.0 - ADAM_B1 ** ADAM_STEP)
    v_hat = nv / (1.0 - ADAM_B2 ** ADAM_STEP)
    d_ref[...] = -ADAM_LR * (m_hat / (jnp.sqrt(v_hat) + ADAM_EPS) + ADAM_WD * w_ref[...])
    nm_ref[...] = nm
    nv_ref[...] = nv


def _adamw_small(ws, gs, ms, vs):
    k = len(ws)

    def body(*refs):
        ins, outs = refs[:4 * k], refs[4 * k:]
        for a in range(k):
            _adamw_update(ins[a], ins[k + a], ins[2 * k + a], ins[3 * k + a], outs[a], outs[k + a], outs[2 * k + a])

    out = pl.pallas_call(
        body, name="adamw_small",
        out_shape=[jax.ShapeDtypeStruct(w.shape, F32) for w in ws] * 3,
        compiler_params=pltpu.CompilerParams(vmem_limit_bytes=VMEM_LIMIT),
    )(*ws, *gs, *ms, *vs)
    return out[:k], out[k:2 * k], out[2 * k:]


def _adamw(w, g, m, v, name):
    shape = w.shape
    w2, g2, m2, v2 = (a.reshape((-1, shape[-1])) for a in (w, g, m, v))

    def body(w_ref, g_ref, m_ref, v_ref, d_ref, nm_ref, nv_ref):
        _adamw_update(w_ref, g_ref, m_ref, v_ref, d_ref, nm_ref, nv_ref)

    rows, cols = w2.shape
    nblk = cols // 256 if cols % 256 == 0 and rows >= 64 else 1
    blk = pl.BlockSpec((rows, cols // nblk), lambda j: (0, j))
    out = pl.pallas_call(
        body, name=name, grid=(nblk,), in_specs=[blk] * 4, out_specs=[blk] * 3,
        out_shape=[jax.ShapeDtypeStruct(w2.shape, F32)] * 3,
        compiler_params=_cparams("parallel"),
    )(w2, g2, m2, v2)
    return tuple(a.reshape(shape) for a in out)


def kernel(x, meta_tokens, norm_g, w_in, q_norm_g, w_q_up, kv_norm_g, w_kv_up, conv_w, attn_out_g, conv_out_g, w_out, final_norm_g, loss_target, m_meta_tokens, m_norm_g, m_w_in, m_q_norm_g, m_w_q_up, m_kv_norm_g, m_w_kv_up, m_conv_w, m_attn_out_g, m_conv_out_g, m_w_out, m_final_norm_g, v_meta_tokens, v_norm_g, v_w_in, v_q_norm_g, v_w_q_up, v_kv_norm_g, v_w_kv_up, v_conv_w, v_attn_out_g, v_conv_out_g, v_w_out, v_final_norm_g):
    nb, s, _ = x.shape
    tm = min(ROW_TILE, s)
    ta = min(ATTN_TILE, s)
    assert s % tm == 0 and s % ta == 0 and tm % 16 == 0
    r = nb * s

    tr = lambda a: jnp.transpose(a[0])
    w_in_p, wq_p, wkv_p, g_cw, g_meta = _gather_weights(
        [tr(w_in).astype(BF16), tr(w_q_up).astype(BF16), tr(w_kv_up).astype(BF16)],
        [W_IN_PIECES, W_Q_PIECES, W_KV_PIECES], [IN_PAD, HEADS * QK_PAD, 1024],
        [jnp.transpose(conv_w, (1, 0, 2)), meta_tokens],
        [(0, jnp.zeros((64, D_MODEL), BF16), 448)]
        + [(1, jnp.zeros((64, Q_RANK), BF16), QK_PAD * h + NOPE + ROPE) for h in range(HEADS)])
    conv_f = jnp.transpose(g_cw[:, :, 0, :], (1, 0, 2)).reshape(3, CONV_W)
    meta_f = jnp.transpose(g_meta, (1, 0, 2)).reshape(N_META, D_MODEL)

    c_all, sa_all, sb_all = _rope_tables(N_META + s)
    tabs_m = (c_all[:N_META], sa_all[:N_META], sb_all[:N_META])
    tabs = (c_all[N_META:], sa_all[N_META:], sb_all[N_META:])
    gid = np.arange(CONV_W) // CONV_GROUP
    gmat = jnp.asarray(np.where(gid[:, None] == gid[None, :], 1.0 / CONV_GROUP, 0.0), BF16)
    ga, gc = attn_out_g, conv_out_g
    gf = final_norm_g.reshape(1, D_MODEL)

    x2d = x.reshape(r, D_MODEL)
    tgt2d = loss_target.reshape(r, D_MODEL)

    p, q, k, v, pm, km, vm, w_out_f = _fwd_proj(x2d, meta_f, tabs, tabs_m, norm_g, w_in_p, q_norm_g, wq_p,
                                                kv_norm_g, wkv_p, w_out[0].astype(BF16), nb, s, tm)
    o, lse = _attn_fwd(q, k, v, km, vm, nb, s, ta)
    dh2, dycat, dw_out, dgf, loss_acc = _out_fwd_bwd(x2d, tgt2d, o, p, pm, conv_f, ga, gc, gmat, w_out_f, gf,
                                                     nb, s, tm)
    dpb, do, delta, dccm, dga, dgc, dcw = _gate_bwd(dycat, o, p, pm, conv_f, ga, gc, gmat, nb, s, tm)
    p_out = dw_out.reshape(4, D_MODEL // 4, D_MODEL)
    dq, dk, dv, dkm, dvm, g_w_out = _attn_bwd(q, k, v, do, lse, delta, km, vm, [p_out], nb, s, ta)
    dpa, dpam, p_q, p_kv, dgq, dgkv = _up_bwd(dq, dk, dv, dkm, dvm, p, pm, tabs, tabs_m, wq_p, wkv_p,
                                              q_norm_g, kv_norm_g, nb, s, tm)
    gx, gmeta, p_in, dng = _in_bwd(x2d, dh2, dpa, dpb, meta_f, dpam, dccm, pm, w_in_p, norm_g, nb, s, tm)

    flat =jnp.concatenate([dng.reshape(-1), dgq.reshape(-1), dgkv.reshape(-1), dga.reshape(-1), dgc.reshape(-1),
                            dgf.reshape(-1), dcw[:3].reshape(-1), gmeta.reshape(-1), loss_acc[0, 0:1]])
    n_small = flat.shape[0]
    rows_small = -(-n_small // 1024) * 8
    small = jnp.pad(flat, (0, rows_small * 128 - n_small)).reshape(rows_small, 128)
    g_w_in_t, g_w_q_t, g_w_kv_t, small_sum = _reduce_grads([p_in, p_q, p_kv], small)
    ssum = small_sum.reshape(-1)

    def take(off, n):
        return ssum[off:off + n], off + n

    off = 0
    g_norm, off = take(off, D_MODEL)
    g_qn, off = take(off, Q_RANK)
    g_kvn, off = take(off, KV_RANK)
    g_ga, off = take(off, CONV_W)
    g_gc, off = take(off, CONV_W)
    g_gf, off = take(off, D_MODEL)
    g_cw_all, off = take(off, 3 * CONV_W)
    g_meta_all, off = take(off, N_META * D_MODEL)
    loss = ssum[off]
    chip = 2 * lax.axis_index("x") + lax.axis_index("y")
    g_conv = lax.dynamic_slice(g_cw_all.reshape(3, CONV_W), (0, chip * 128), (3, 128))
    g_mt = lax.dynamic_slice(g_meta_all.reshape(N_META, D_MODEL), (0, chip * 256), (N_META, 256))

    grads = {
        "meta_tokens": g_mt, "norm_g": g_norm.reshape(1, -1), "w_in": g_w_in_t, "q_norm_g": g_qn.reshape(1, -1),
        "w_q_up": g_w_q_t, "kv_norm_g": g_kvn.reshape(1, -1), "w_kv_up": jnp.transpose(g_w_kv_t)[None],
        "conv_w": g_conv[None], "attn_out_g": g_ga.reshape(1, -1), "conv_out_g": g_gc.reshape(1, -1),
        "w_out": g_w_out[None], "final_norm_g": g_gf,
    }
    transposed = ("w_in", "w_q_up")
    weights = {
        "meta_tokens": (meta_tokens, m_meta_tokens, v_meta_tokens), "norm_g": (norm_g, m_norm_g, v_norm_g),
        "w_in": (w_in, m_w_in, v_w_in), "q_norm_g": (q_norm_g, m_q_norm_g, v_q_norm_g),
        "w_q_up": (w_q_up, m_w_q_up, v_w_q_up), "kv_norm_g": (kv_norm_g, m_kv_norm_g, v_kv_norm_g),
        "w_kv_up": (w_kv_up, m_w_kv_up, v_w_kv_up), "conv_w": (conv_w, m_conv_w, v_conv_w),
        "attn_out_g": (attn_out_g, m_attn_out_g, v_attn_out_g), "conv_out_g": (conv_out_g, m_conv_out_g, v_conv_out_g),
        "w_out": (w_out, m_w_out, v_w_out), "final_norm_g": (final_norm_g, m_final_norm_g, v_final_norm_g),
    }
    names = list(weights)
    small = [nme for nme in names if nme != "w_in"]

    def view(nme, a):
        if nme in transposed:
            return a if a.ndim == 2 else tr(a)
        if nme == "conv_w":
            return jnp.transpose(a.reshape(1, 3, -1), (1, 0, 2))
        if a.ndim == 3:
            return a[0]
        return a.reshape(1, -1) if a.ndim == 1 else a

    def unview(nme, a):
        if nme in transposed:
            return jnp.transpose(a)[None]
        if nme == "conv_w":
            return jnp.transpose(a, (1, 0, 2))
        return a.reshape(weights[nme][0].shape)

    res_small = _adamw_small(*[[view(nme, a) for nme, a in zip(small, col)] for col in (
        [weights[nme][0] for nme in small], [grads[nme] for nme in small],
        [weights[nme][1] for nme in small], [weights[nme][2] for nme in small])])
    w_, m_, v_ = weights["w_in"]
    res = _adamw(tr(w_), grads["w_in"], tr(m_), tr(v_), "adamw_w_in")
    upd = {"w_in": tuple(jnp.transpose(a)[None] for a in (grads["w_in"],) + res)}
    for j, nme in enumerate(small):
        upd[nme] = (unview(nme, view(nme, grads[nme])),) + tuple(unview(nme, r[j]) for r in res_small)
    grads = {nme: upd[nme][0] for nme in names}
    deltas, new_m, new_v = ([upd[nme][j] for nme in names] for j in (1, 2, 3))

    grad_x = gx.reshape(nb, s, D_MODEL)
    return (loss, grad_x, *[grads[nme] for nme in names], *deltas, *new_m, *new_v)
```

```python
import functools

import jax
import jax.numpy as jnp
import numpy as np
from jax import lax
from jax.experimental import pallas as pl
from jax.experimental.pallas import tpu as pltpu

F32 = jnp.float32
BF16 = jnp.bfloat16

D_MODEL = 1024
N_META = 16
HEADS = 4
NOPE = 128
ROPE = 64
VDIM = 128
QK_PAD = 256
Q_RANK = 256
KV_RANK = 128
CONV_W = 512
CONV_GROUP = 64
ROPE_THETA = 10000.0
EPS = 1e-6
ATTN_SCALE = (NOPE + ROPE) ** -0.5
IN_DIM = 3008
IN_PAD = 3072
BLK_ZA, BLK_CB, BLK_CC, BLK_CH, BLK_ZC = 1, 2, 3, 4, 5
NEG_INF = -1e30

ADAM_LR = 0.001
ADAM_B1 = 0.9
ADAM_B2 = 0.999
ADAM_EPS = 1e-08
ADAM_WD = 0.01
ADAM_STEP = 10

ROW_TILE = 512
ATTN_TILE = 256
VMEM_LIMIT = 56 * 1024 * 1024

NT = (((1,), (1,)), ((), ()))
TN = (((0,), (0,)), ((), ()))


def _cparams(*sem):
    return pltpu.CompilerParams(dimension_semantics=sem, vmem_limit_bytes=VMEM_LIMIT)


def _dot(a, b):
    return jnp.dot(a, b, preferred_element_type=F32)


def _dot_nt(a, b):
    return lax.dot_general(a, b, NT, preferred_element_type=F32)


def _dot_tn(a, b):
    return lax.dot_general(a, b, TN, preferred_element_type=F32)


def _rms(x, g):
    r = lax.rsqrt(jnp.mean(x * x, axis=-1, keepdims=True) + EPS)
    return x * r * g, r


def _rms_bwd(dy, x, r, g):
    xh = x * r
    dyg = dy * g
    dx = r * (dyg - xh * jnp.mean(dyg * xh, axis=-1, keepdims=True))
    return dx, dy * xh


def _sigmoid(z):
    return 1.0 / (1.0 + jnp.exp(-z))


def _rope(b, c, sa, sb):
    return b * c + pltpu.roll(b, 96, 1) * sa + pltpu.roll(b, 32, 1) * sb


def _rope_bwd(d, c, sa, sb):
    return d * c + pltpu.roll(d * sa, 32, 1) + pltpu.roll(d * sb, 96, 1)


def _group_mean(x, gmat):
    hi = x.astype(BF16)
    lo = (x - hi.astype(F32)).astype(BF16)
    return _dot(hi, gmat) + _dot(lo, gmat)


def _row_of(col, rows):
    return jnp.transpose(jnp.broadcast_to(col, (rows, 128)))[0:1, :]


def _rope_tables(n_pos):
    half = ROPE // 2
    inv_freq = (np.float32(1.0) / (np.float32(ROPE_THETA) ** (np.arange(half, dtype=np.float32) / np.float32(half))))
    ang = np.arange(n_pos, dtype=np.float32)[:, None] * inv_freq.astype(np.float32)[None, :]
    cos, sin = np.cos(ang).astype(np.float32), np.sin(ang).astype(np.float32)
    z = np.zeros((n_pos, half), np.float32)
    c = np.concatenate([cos, cos, z, z], axis=1)
    sa = np.concatenate([-sin, z, z, z], axis=1)
    sb = np.concatenate([z, sin, z, z], axis=1)
    return jnp.asarray(c), jnp.asarray(sa), jnp.asarray(sb)


W_IN_PIECES = ((0, 384, 752, 0, 64, 0), (384, 64, 752, 384, 448, 1), (448, 304, 752, 512, 512, 1))
W_Q_PIECES = ((0, 96, 256, 0, 0, 0), (96, 96, 256, 96, 96, 1))
W_KV_PIECES = ((0, 128, 128, 0, 0, 0), (128, 128, 128, 512, 512, 1))
W_OUT_PIECES = ((0, 128, 256, 0, 0, 0), (128, 128, 256, 128, 128, 1))


class _StagedGather:
    def __init__(self, pieces):
        self.pieces = pieces

    def scratch(self):
        nk, dma = len(self.pieces), pltpu.SemaphoreType.DMA
        return [dma((nk, 3)), dma((nk, 3)), dma((nk, 3)), dma((nk, 3)), dma((nk,))]

    def run(self, stage, src_ref, out_ref, scr):
        send_sems, recv_sems, fwd_send, fwd_recv, loc_sems = scr
        pieces = self.pieces
        nk = len(pieces)
        x, y, c = lax.axis_index("x"), lax.axis_index("y"), lax.axis_index("c")
        mine = 2 * x + y
        chips = [(1 - x, y), (x, 1 - y), (1 - x, 1 - y)]
        chip_of = [2 * px + py for px, py in chips]
        mesh = pl.DeviceIdType.MESH

        def src(k):
            s0, nr = pieces[k][0], pieces[k][1]
            return src_ref.at[s0:s0 + nr]

        def dst(k, q):
            _, nr, per, first, rest, _ = pieces[k]
            row = per * q + first + (rest - first) * jnp.minimum(q, 1)
            return out_ref.at[pl.ds(pl.multiple_of(row, 16), nr)]

        def ici(k, j, q):
            px, py = chips[j]
            return pltpu.make_async_remote_copy(
                src_ref=src(k), dst_ref=dst(k, q), send_sem=send_sems.at[k, j], recv_sem=recv_sems.at[k, j],
                device_id=(px, py, c), device_id_type=mesh)

        def fwd(k, j):
            ref = dst(k, chip_of[j])
            return pltpu.make_async_remote_copy(
                src_ref=ref, dst_ref=ref, send_sem=fwd_send.at[k, j], recv_sem=fwd_recv.at[k, j],
                device_id=(x, y, 1 - c), device_id_type=mesh)

        local = [pltpu.make_async_copy(src(k), dst(k, mine), loc_sems.at[k]) for k in range(nk)]
        if stage == 0:
            for cp in local:
                cp.start()
        if stage == 2:
            for cp in local:
                cp.wait()
        for half in (0, 1):
            @pl.when(c == half)
            def _(half=half):
                my_k = [k for k in range(nk) if pieces[k][5] == half]
                other_k = [k for k in range(nk) if pieces[k][5] != half]
                for k in my_k:
                    for j in range(3):
                        if stage == 0:
                            ici(k, j, mine).start()
                        elif stage == 1:
                            ici(k, j, chip_of[j]).wait_recv()
                            fwd(k, j).start()
                        else:
                            ici(k, j, mine).wait_send()
                            fwd(k, j).wait_send()
                if stage == 2:
                    for k in other_k:
                        for j in range(3):
                            fwd(k, j).wait_recv()


def _fwd_proj(x2d, meta, tabs, tabs_m, norm_g, w_in_p, q_norm_g, wq_p, kv_norm_g, wkv_p, w_out_shard, nb, s, tm):
    nt = s // tm
    n = nb * nt
    n_steps = n + 1
    c_t, sa_t, sb_t = tabs
    cm_t, sam_t, sbm_t = tabs_m
    gat = _StagedGather(W_OUT_PIECES)
    assert n_steps >= 3

    def body(x_ref, c_ref, sa_ref, sb_ref, mt_ref, cm_ref, sam_ref, sbm_ref,
             g_ref, w_ref, gq_ref, wq_ref, gkv_ref, wkv_ref, wos_ref,
             p_ref, q_ref, k_ref, v_ref, pm_ref, km_ref, vm_ref, wo_ref, *gat_scr):
        i = pl.program_id(0)
        for stage, at in enumerate((0, n_steps - 2, n_steps - 1)):
            @pl.when(i == at)
            def _(stage=stage):
                gat.run(stage, wos_ref, wo_ref, gat_scr)

        def project(xv, c, sa, sb, p_out, q_out, k_out, v_out):
            u, _ = _rms(xv, g_ref[...])
            p = _dot_nt(u.astype(BF16), w_ref[...])
            p_out[...] = p
            qn, _ = _rms(p[:, 0:Q_RANK], gq_ref[...])
            q = _dot_nt(qn.astype(BF16), wq_ref[...])
            kvn, _ = _rms(p[:, Q_RANK:Q_RANK + KV_RANK], gkv_ref[...])
            kv = _dot_nt(kvn.astype(BF16), wkv_ref[...])
            kpe = _rope(p[:, 384:512], c, sa, sb)
            for h in range(HEADS):
                if q_out is not None:
                    pe = _rope(q[:, QK_PAD * h + NOPE:QK_PAD * (h + 1)], c, sa, sb)
                    qh = jnp.concatenate([q[:, QK_PAD * h:QK_PAD * h + NOPE], pe], axis=1)
                    q_out[0, h] = (qh * ATTN_SCALE).astype(BF16)
                k_out[0, h] = jnp.concatenate([kv[:, NOPE * h:NOPE * (h + 1)], kpe], axis=1).astype(BF16)
                v_out[0, h] = kv[:, 512 + VDIM * h:512 + VDIM * (h + 1)].astype(BF16)

        @pl.when(i < n)
        def _():
            project(x_ref[...], c_ref[...], sa_ref[...], sb_ref[...], p_ref, q_ref, k_ref, v_ref)

        @pl.when(i == n)
        def _():
            project(mt_ref[...], cm_ref[...], sam_ref[...], sbm_ref[...], pm_ref, None, km_ref, vm_ref)

    cl = lambda i: jnp.minimum(i, n - 1)
    full = lambda a: pl.BlockSpec(a.shape, lambda i: (0,) * a.ndim)
    const = lambda shape: pl.BlockSpec(shape, lambda i: (0,) * len(shape))
    tab = pl.BlockSpec((tm, 128), lambda i: (cl(i) % nt, 0))
    hb = lambda w: pl.BlockSpec((1, HEADS, tm, w), lambda i: (cl(i) // nt, 0, cl(i) % nt, 0))
    whole = pl.BlockSpec(memory_space=pl.ANY)
    return pl.pallas_call(
        body, name="fwd_proj", grid=(n_steps,),
        in_specs=[pl.BlockSpec((tm, D_MODEL), lambda i: (cl(i), 0)), tab, tab, tab,
                  full(meta), full(cm_t), full(sam_t), full(sbm_t),
                  full(norm_g), full(w_in_p), full(q_norm_g), full(wq_p), full(kv_norm_g), full(wkv_p), whole],
        out_specs=[pl.BlockSpec((tm, IN_PAD), lambda i: (cl(i), 0)), hb(QK_PAD), hb(QK_PAD), hb(VDIM),
                   const((N_META, IN_PAD)), const((1, HEADS, N_META, QK_PAD)), const((1, HEADS, N_META, VDIM)),
                   whole],
        out_shape=[jax.ShapeDtypeStruct((nb * s, IN_PAD), F32),
                   jax.ShapeDtypeStruct((nb, HEADS, s, QK_PAD), BF16),
                   jax.ShapeDtypeStruct((nb, HEADS, s, QK_PAD), BF16),
                   jax.ShapeDtypeStruct((nb, HEADS, s, VDIM), BF16),
                   jax.ShapeDtypeStruct((N_META, IN_PAD), F32),
                   jax.ShapeDtypeStruct((1, HEADS, N_META, QK_PAD), BF16),
                   jax.ShapeDtypeStruct((1, HEADS, N_META, VDIM), BF16),
                   jax.ShapeDtypeStruct((D_MODEL, D_MODEL), BF16)],
        scratch_shapes=gat.scratch(),
        compiler_params=_cparams("arbitrary"),
    )(x2d, c_t, sa_t, sb_t, meta, cm_t, sam_t, sbm_t, norm_g, w_in_p, q_norm_g, wq_p, kv_norm_g, wkv_p, w_out_shard)


def _attn_fwd(q, k, v, km, vm, nb, s, tq):
    nq = s // tq

    def body(q_ref, k_ref, v_ref, km_ref, vm_ref, o_ref, lse_ref, s_scr, p_scr):
        row = lax.broadcasted_iota(jnp.int32, (tq, tq), 0)
        col = lax.broadcasted_iota(jnp.int32, (tq, tq), 1)
        def scores(i):
            slot = i % 2
            qi = q_ref[0, 0, i * tq:(i + 1) * tq, :]
            sm = _dot_nt(qi, km_ref[0, 0])
            m128 = None
            for j in range(i + 1):
                sc = _dot_nt(qi, k_ref[0, 0, j * tq:(j + 1) * tq, :])
                if j == i:
                    sc = jnp.where(col <= row, sc, NEG_INF)
                s_scr[slot, :, j * tq:(j + 1) * tq] = sc
                mx = sc[:, 0:128]
                for c0 in range(128, tq, 128):
                    mx = jnp.maximum(mx, sc[:, c0:c0 + 128])
                m128 = mx if m128 is None else jnp.maximum(m128, mx)
            return sm, jnp.maximum(jnp.max(m128, axis=1, keepdims=True), jnp.max(sm, axis=1, keepdims=True))

        def weighted_sum(i, pm, l):
            n = (i + 1) * tq
            acc = _dot(p_scr[i % 2, :, 0:n], v_ref[0, 0, 0:n, :]) + _dot(pm.astype(BF16), vm_ref[0, 0])
            o_ref[0, 0, i * tq:(i + 1) * tq, :] = acc / l

        nxt, pending = scores(0), None
        for i in range(nq):
            slot = i % 2
            sm, m = nxt
            if i + 1 < nq:
                nxt = scores(i + 1)
            pm = jnp.exp(sm - m)
            l128 = None
            for j in range(i + 1):
                p = jnp.exp(s_scr[slot, :, j * tq:(j + 1) * tq] - m)
                p_scr[slot, :, j * tq:(j + 1) * tq] = p.astype(BF16)
                ps = p[:, 0:128]
                for c0 in range(128, tq, 128):
                    ps = ps + p[:, c0:c0 + 128]
                l128 = ps if l128 is None else l128 + ps
            l = jnp.sum(l128, axis=1, keepdims=True) + jnp.sum(pm, axis=1, keepdims=True)
            lse_ref[0, 0, :, i * tq:(i + 1) * tq] = _row_of(m + jnp.log(l), tq)
            if pending is not None:
                weighted_sum(*pending)
            pending = (i, pm, l)
        weighted_sum(*pending)

    hblk = lambda w: pl.BlockSpec((1, 1, s, w), lambda b, h: (b, h, 0, 0))
    mblk = lambda w: pl.BlockSpec((1, 1, N_META, w), lambda b, h: (0, h, 0, 0))
    return pl.pallas_call(
        body, name="attn_fwd", grid=(nb, HEADS),
        in_specs=[hblk(QK_PAD), hblk(QK_PAD), hblk(VDIM), mblk(QK_PAD), mblk(VDIM)],
        out_specs=[hblk(VDIM), pl.BlockSpec((1, 1, 1, s), lambda b, h: (b, h, 0, 0))],
        out_shape=[jax.ShapeDtypeStruct((nb, HEADS, s, VDIM), F32),
                   jax.ShapeDtypeStruct((nb, HEADS, 1, s), F32)],
        scratch_shapes=[pltpu.VMEM((2, tq, s), F32), pltpu.VMEM((2, tq, s), BF16)],
        compiler_params=_cparams("parallel", "parallel"),
    )(q, k, v, km, vm)


def _shift_rows(a, prev, n_rows):
    rid = lax.broadcasted_iota(jnp.int32, a.shape, 0)
    a1 = jnp.where(rid == 0, prev[7:8, :], pltpu.roll(a, 1, 0))
    a2 = jnp.where(rid == 0, prev[6:7, :], jnp.where(rid == 1, prev[7:8, :], pltpu.roll(a, 2, 0)))
    return a1, a2


def _attn_gate(o, za, ga_h):
    on, r = _rms(o, ga_h)
    return on * (za * _sigmoid(za)), on, r


def _out_fwd_bwd(x2d, tgt2d, o, p, pm, conv_w, ga, gc, gmat, w_out, gf, nb, s, tm):
    nt = s // tm
    r = nb * s
    prev_idx = lambda i: jnp.maximum(i * (tm // 8) - 1, 0)

    def body(x_ref, t_ref, o_ref, za_ref, cb_ref, cc_ref, ch_ref, zc_ref, ccp_ref, chp_ref, mc_ref, mh_ref,
             cw_ref, ga_ref, gc_ref, gm_ref, w_ref, gf_ref,
             dh_ref, dy_ref, dw_ref, dgf_ref, loss_ref):
        i = pl.program_id(0)

        @pl.when(i == 0)
        def _():
            dw_ref[...] = jnp.zeros_like(dw_ref)
            dgf_ref[...] = jnp.zeros_like(dgf_ref)
            loss_ref[...] = jnp.zeros_like(loss_ref)

        ya = []
        for h in range(HEADS):
            y, _, _ = _attn_gate(o_ref[0, h], za_ref[:, VDIM * h:VDIM * (h + 1)],
                                 ga_ref[:, VDIM * h:VDIM * (h + 1)])
            ya.append(y)
        cc = cc_ref[...] * ch_ref[...]
        prev = jnp.where(i % nt == 0, mc_ref[8:16, :] * mh_ref[8:16, :], ccp_ref[...] * chp_ref[...])
        cc1, cc2 = _shift_rows(cc, prev, tm)
        yc = cb_ref[...] * (cw_ref[0:1, :] * cc2 + cw_ref[1:2, :] * cc1 + cw_ref[2:3, :] * cc)
        rg = lax.rsqrt(_group_mean(yc * yc, gm_ref[...]) + EPS)
        zc = zc_ref[...]
        yconv = yc * rg * gc_ref[...] * (zc * _sigmoid(zc))
        ycat = jnp.concatenate(ya + [yconv], axis=1).astype(BF16)
        h2 = x_ref[...] + _dot(ycat, w_ref[...])
        gfv = gf_ref[...]
        y, r2 = _rms(h2, gfv)
        e = y - t_ref[...]
        loss_ref[...] += 0.5 * jnp.sum(e * e) / D_MODEL
        dyv = e * (1.0 / D_MODEL)
        dh2, dgf = _rms_bwd(dyv, h2, r2, gfv)
        dgf_ref[...] += jnp.sum(dgf, axis=0, keepdims=True)
        dh_ref[...] = dh2
        dhb = dh2.astype(BF16)
        dy_ref[...] = _dot_nt(dhb, w_ref[...])
        dw_ref[...] += _dot_tn(ycat, dhb)

    row = lambda w, j: pl.BlockSpec((tm, w), lambda i: (i, j))
    pblk = lambda j: pl.BlockSpec((tm, 512), lambda i: (i, j))
    pprev = lambda j: pl.BlockSpec((8, 512), lambda i: (prev_idx(i), j))
    mblk = lambda j: pl.BlockSpec((N_META, 512), lambda i: (0, j))
    full = lambda a: pl.BlockSpec(a.shape, lambda i: (0,) * a.ndim)
    return pl.pallas_call(
        body, name="out_fwd_bwd", grid=(nb * nt,),
        in_specs=[row(D_MODEL, 0), row(D_MODEL, 0),
                  pl.BlockSpec((1, HEADS, tm, VDIM), lambda i: (i // nt, 0, i % nt, 0)),
                  pblk(BLK_ZA), pblk(BLK_CB), pblk(BLK_CC), pblk(BLK_CH), pblk(BLK_ZC),
                  pprev(BLK_CC), pprev(BLK_CH), mblk(BLK_CC), mblk(BLK_CH),
                  full(conv_w), full(ga), full(gc), full(gmat), full(w_out), full(gf)],
        out_specs=[row(D_MODEL, 0), row(D_MODEL, 0),
                   pl.BlockSpec((D_MODEL, D_MODEL), lambda i: (0, 0)),
                   pl.BlockSpec((1, D_MODEL), lambda i: (0, 0)),
                   pl.BlockSpec((1, 128), lambda i: (0, 0))],
        out_shape=[jax.ShapeDtypeStruct((r, D_MODEL), F32), jax.ShapeDtypeStruct((r, D_MODEL), F32),
                   jax.ShapeDtypeStruct((D_MODEL, D_MODEL), F32), jax.ShapeDtypeStruct((1, D_MODEL), F32),
                   jax.ShapeDtypeStruct((1, 128), F32)],
        compiler_params=_cparams("arbitrary"),
    )(x2d, tgt2d, o, p, p, p, p, p, p, p, pm, pm, conv_w, ga, gc, gmat, w_out, gf)


def _gate_bwd(dycat, o, p, pm, conv_w, ga, gc, gmat, nb, s, tm):
    nt = s // tm
    r = nb * s
    ext = tm + 8
    prev_idx = lambda i: jnp.maximum(i * (tm // 8) - 1, 0)
    next_idx = lambda i: jnp.minimum((i + 1) * (tm // 8), r // 8 - 1)

    def body(dya_ref, dyc_ref, dycn_ref, o_ref, za_ref, cb_ref, cbn_ref, cc_ref, ccp_ref, ccn_ref,
             ch_ref, chp_ref, chn_ref, zc_ref, zcn_ref, mc_ref, mh_ref, cw_ref, ga_ref, gc_ref, gm_ref,
             dpb_ref, do_ref, dl_ref, dccm_ref, dga_ref, dgc_ref, dcw_ref):
        i = pl.program_id(0)

        @pl.when(i == 0)
        def _():
            dga_ref[...] = jnp.zeros_like(dga_ref)
            dgc_ref[...] = jnp.zeros_like(dgc_ref)
            dcw_ref[...] = jnp.zeros_like(dcw_ref)

        dga = []
        for h in range(HEADS):
            hs = slice(VDIM * h, VDIM * (h + 1))
            oh, za, gah, dya = o_ref[0, h], za_ref[:, hs], ga_ref[:, hs], dya_ref[:, hs]
            sg = _sigmoid(za)
            on, ro = _rms(oh, gah)
            don = dya * (za * sg)
            dpb_ref[:, hs] = (dya * on * (sg * (1.0 + za * (1.0 - sg)))).astype(BF16)
            do, dg = _rms_bwd(don, oh, ro, gah)
            dga.append(jnp.sum(dg, axis=0, keepdims=True))
            dob = do.astype(BF16)
            do_ref[0, h] = dob
            dl_ref[0, h] = _row_of(jnp.sum(dob.astype(F32) * oh, axis=1, keepdims=True), tm)
        dga_ref[...] += jnp.concatenate(dga, axis=1)

        cat = lambda a, b: jnp.concatenate([a[...], b[...]], axis=0)
        cch = cat(cc_ref, ccn_ref)
        chh = cat(ch_ref, chn_ref)
        cb = cat(cb_ref, cbn_ref)
        zc = cat(zc_ref, zcn_ref)
        dy = cat(dyc_ref, dycn_ref)
        first = i % nt == 0
        last = i % nt == nt - 1
        cc = cch * chh
        prev = jnp.where(first, mc_ref[8:16, :] * mh_ref[8:16, :], ccp_ref[...] * chp_ref[...])
        cc1, cc2 = _shift_rows(cc, prev, ext)
        w0, w1, w2 = cw_ref[0:1, :], cw_ref[1:2, :], cw_ref[2:3, :]
        dw = w0 * cc2 + w1 * cc1 + w2 * cc
        yc = cb * dw
        rg = lax.rsqrt(_group_mean(yc * yc, gm_ref[...]) + EPS)
        ych = yc * rg
        gcv = gc_ref[...]
        sg = _sigmoid(zc)
        dycn = dy * (zc * sg)
        dzc = dy * (ych * gcv) * (sg * (1.0 + zc * (1.0 - sg)))
        dgc_ref[...] += jnp.sum((dycn * ych)[:tm], axis=0, keepdims=True)
        dycg = dycn * gcv
        dyc = rg * (dycg - ych * _group_mean(dycg * ych, gm_ref[...]))
        rid = lax.broadcasted_iota(jnp.int32, (ext, CONV_W), 0)
        ddw = jnp.where(jnp.logical_and(last, rid >= tm), 0.0, dyc * cb)
        dcb = dyc * dw
        dcc = w2 * ddw + w1 * pltpu.roll(ddw, ext - 1, 0) + w0 * pltpu.roll(ddw, ext - 2, 0)
        dpb_ref[:, 512:1024] = dcb[:tm].astype(BF16)
        dpb_ref[:, 1024:1536] = (dcc * chh)[:tm].astype(BF16)
        dpb_ref[:, 1536:2048] = (dcc * cch)[:tm].astype(BF16)
        dpb_ref[:, 2048:2560] = dzc[:tm].astype(BF16)
        rs = lambda a: jnp.sum(a[:tm], axis=0, keepdims=True)
        dcw_ref[0:1, :] += rs(ddw * cc2)
        dcw_ref[1:2, :] += rs(ddw * cc1)
        dcw_ref[2:3, :] += rs(ddw * cc)

        @pl.when(first)
        def _():
            d0, d1 = ddw[0:1, :], ddw[1:2, :]
            r8 = lax.broadcasted_iota(jnp.int32, (8, CONV_W), 0)
            dccm_ref[0] = jnp.where(r8 == 7, w1 * d0 + w0 * d1, jnp.where(r8 == 6, w0 * d0, 0.0))

    row = lambda j: pl.BlockSpec((tm, 512), lambda i: (i, j))
    prv = lambda j: pl.BlockSpec((8, 512), lambda i: (prev_idx(i), j))
    nxt = lambda j: pl.BlockSpec((8, 512), lambda i: (next_idx(i), j))
    mblk = lambda j: pl.BlockSpec((N_META, 512), lambda i: (0, j))
    full = lambda a: pl.BlockSpec(a.shape, lambda i: (0,) * a.ndim)
    hb = lambda w: pl.BlockSpec((1, HEADS, tm, w), lambda i: (i // nt, 0, i % nt, 0))
    acc = lambda rr: pl.BlockSpec((rr, 512), lambda i: (0, 0))
    return pl.pallas_call(
        body, name="gate_bwd", grid=(nb * nt,),
        in_specs=[row(0), row(1), nxt(1), hb(VDIM),
                  row(BLK_ZA), row(BLK_CB), nxt(BLK_CB), row(BLK_CC), prv(BLK_CC), nxt(BLK_CC),
                  row(BLK_CH), prv(BLK_CH), nxt(BLK_CH), row(BLK_ZC), nxt(BLK_ZC),
                  mblk(BLK_CC), mblk(BLK_CH), full(conv_w), full(ga), full(gc), full(gmat)],
        out_specs=[pl.BlockSpec((tm, 2560), lambda i: (i, 0)), hb(VDIM),
                   pl.BlockSpec((1, HEADS, 1, tm), lambda i: (i // nt, 0, 0, i % nt)),
                   pl.BlockSpec((1, 8, 512), lambda i: (i // nt, 0, 0)),
                   acc(1), acc(1), acc(8)],
        out_shape=[jax.ShapeDtypeStruct((r, 2560), BF16), jax.ShapeDtypeStruct((nb, HEADS, s, VDIM), BF16),
                   jax.ShapeDtypeStruct((nb, HEADS, 1, s), F32), jax.ShapeDtypeStruct((nb, 8, 512), F32),
                   jax.ShapeDtypeStruct((1, 512), F32), jax.ShapeDtypeStruct((1, 512), F32),
                   jax.ShapeDtypeStruct((8, 512), F32)],
        compiler_params=_cparams("arbitrary"),
    )(dycat, dycat, dycat, o, p, p, p, p, p, p, p, p, p, p, p, pm, pm, conv_w, ga, gc, gmat)


class _StagedReduce:
    LOC, PRE_S, PRE_R, ICI_S, ICI_R, POST_S, POST_R, OUT, N_SEM = 0, 1, 2, 3, 6, 9, 10, 11, 12

    def __init__(self, shard_shape):
        self.half = (shard_shape[0] // 2, shard_shape[1])

    def scratch(self):
        h = self.half
        return [pltpu.VMEM((4,) + h, F32), pltpu.VMEM((4,) + h, F32), pltpu.VMEM((4,) + h, BF16),
                pltpu.VMEM((3,) + h, BF16), pltpu.VMEM(h, F32), pltpu.SemaphoreType.DMA((self.N_SEM,))]

    def run(self, stage, pin, gout, scr):
        own, sib, wire, rbuf, fin, sems = scr
        r2 = self.half[0]
        x, y, c = lax.axis_index("x"), lax.axis_index("y"), lax.axis_index("c")
        mine = 2 * x + y
        sibling = (x, y, 1 - c)
        chips = [(1 - x, y), (x, 1 - y), (1 - x, 1 - y)]
        rows = lambda half: pl.ds(pl.multiple_of(half * r2, r2), r2)
        mesh = pl.DeviceIdType.MESH

        loc = pltpu.make_async_copy(pin.at[:, rows(c), :], own, sems.at[self.LOC])
        pre = pltpu.make_async_remote_copy(
            src_ref=pin.at[:, rows(1 - c), :], dst_ref=sib, send_sem=sems.at[self.PRE_S],
            recv_sem=sems.at[self.PRE_R], device_id=sibling, device_id_type=mesh)

        def ici(j):
            px, py = chips[j]
            return pltpu.make_async_remote_copy(
                src_ref=wire.at[2 * px + py], dst_ref=rbuf.at[j], send_sem=sems.at[self.ICI_S + j],
                recv_sem=sems.at[self.ICI_R + j], device_id=(px, py, c), device_id_type=mesh)

        def post(half):
            return pltpu.make_async_remote_copy(
                src_ref=fin, dst_ref=gout.at[rows(half), :], send_sem=sems.at[self.POST_S],
                recv_sem=sems.at[self.POST_R], device_id=sibling, device_id_type=mesh)

        keep = pltpu.make_async_copy(fin, gout.at[rows(c), :], sems.at[self.OUT])
        if stage == 0:
            loc.start()
            pre.start()
        elif stage == 1:
            loc.wait()
            pre.wait_recv()
            for blk in range(4):
                tot = own[blk] + sib[blk]
                own[blk] = tot
                wire[blk] = tot.astype(BF16)
            for j in range(3):
                ici(j).start()
        elif stage == 2:
            for j in range(3):
                ici(j).wait_recv()
            tot = own[mine]
            for j in range(3):
                tot = tot + rbuf[j].astype(F32)
            fin[...] = tot
            post(c).start()
            keep.start()
        else:
            post(1 - c).wait_recv()
            pre.wait_send()
            for j in range(3):
                ici(j).wait_send()
            post(c).wait_send()
            keep.wait()


def _attn_bwd(q, k, v, do, lse, delta, km, vm, early, nb, s, t):
    n = s // t
    ne = len(early)
    reds = [_StagedReduce(a.shape[1:]) for a in early]
    n_steps = HEADS * nb
    assert n_steps >= 4

    def body(q_ref, k_ref, v_ref, do_ref, lse_ref, dl_ref, km_ref, vm_ref, *rest):
        pin_refs, rest = rest[:ne], rest[ne:]
        dq_ref, dk_ref, dv_ref, dkm_ref, dvm_ref = rest[:5]
        gout_refs, (p_scr, ds_scr, dq_acc), red_scr = rest[5:5 + ne], rest[5 + ne:8 + ne], rest[8 + ne:]
        b = pl.program_id(1)
        step = pl.program_id(0) * nb + b
        for stage, at in enumerate((0, 1, n_steps - 2, n_steps - 1)):
            @pl.when(step == at)
            def _(stage=stage):
                for a, red in enumerate(reds):
                    red.run(stage, pin_refs[a], gout_refs[a], red_scr[6 * a:6 * a + 6])

        @pl.when(b == 0)
        def _():
            dkm_ref[...] = jnp.zeros_like(dkm_ref)
            dvm_ref[...] = jnp.zeros_like(dvm_ref)

        kr = lax.broadcasted_iota(jnp.int32, (t, t), 0)
        qc = lax.broadcasted_iota(jnp.int32, (t, t), 1)
        km_v, vm_v = km_ref[0, 0], vm_ref[0, 0]
        ptm = jnp.exp(_dot_nt(km_v, q_ref[0, 0]) - lse_ref[0, 0])
        dstm = (ptm * (_dot_nt(vm_v, do_ref[0, 0]) - dl_ref[0, 0])).astype(BF16)
        dkm_ref[0] += _dot(dstm, q_ref[0, 0])
        dvm_ref[0] += _dot(ptm.astype(BF16), do_ref[0, 0])
        dq_acc[...] = _dot_tn(dstm, km_v)
        def tiles(j):
            slot = j % 2
            kj = k_ref[0, 0, j * t:(j + 1) * t, :]
            vj = v_ref[0, 0, j * t:(j + 1) * t, :]
            def products(i):
                cs = slice(i * t, (i + 1) * t)
                return _dot_nt(kj, q_ref[0, 0, cs, :]), _dot_nt(vj, do_ref[0, 0, cs, :])

            nxt, pending = products(j), None
            for i in range(j, n):
                cs = slice(i * t, (i + 1) * t)
                st, dpt = nxt
                if i + 1 < n:
                    nxt = products(i + 1)
                if i == j:
                    st = jnp.where(kr <= qc, st, NEG_INF)
                pt = jnp.exp(st - lse_ref[0, 0, :, cs])
                dst = (pt * (dpt - dl_ref[0, 0, :, cs])).astype(BF16)
                p_scr[slot, :, cs] = pt.astype(BF16)
                ds_scr[slot, :, cs] = dst
                if pending is not None:
                    dq_acc[pending[0], :] += _dot_tn(pending[1], kj)
                pending = (cs, dst)
            dq_acc[pending[0], :] += _dot_tn(pending[1], kj)

        for j in range(n):
            slot = j % 2
            tiles(j)
            dv_ref[0, 0, j * t:(j + 1) * t, :] = _dot(p_scr[slot, :, j * t:s], do_ref[0, 0, j * t:s, :]).astype(BF16)
            dk_ref[0, 0, j * t:(j + 1) * t, :] = _dot(ds_scr[slot, :, j * t:s], q_ref[0, 0, j * t:s, :]).astype(BF16)
        dq_ref[0, 0] = dq_acc[...].astype(BF16)

    big = lambda w: pl.BlockSpec((1, 1, s, w), lambda h, b: (b, h, 0, 0))
    rowv = pl.BlockSpec((1, 1, 1, s), lambda h, b: (b, h, 0, 0))
    mk = lambda w: pl.BlockSpec((1, 1, N_META, w), lambda h, b: (0, h, 0, 0))
    mo = lambda w: pl.BlockSpec((1, N_META, w), lambda h, b: (h, 0, 0))
    return pl.pallas_call(
        body, name="attn_bwd", grid=(HEADS, nb),
        in_specs=[big(QK_PAD), big(QK_PAD), big(VDIM), big(VDIM), rowv, rowv, mk(QK_PAD), mk(VDIM)]
        + [pl.BlockSpec(memory_space=pl.ANY)] * ne,
        out_specs=[big(QK_PAD), big(QK_PAD), big(VDIM), mo(QK_PAD), mo(VDIM)]
        + [pl.BlockSpec(memory_space=pl.ANY)] * ne,
        out_shape=[jax.ShapeDtypeStruct((nb, HEADS, s, QK_PAD), BF16),
                   jax.ShapeDtypeStruct((nb, HEADS, s, QK_PAD), BF16),
                   jax.ShapeDtypeStruct((nb, HEADS, s, VDIM), BF16),
                   jax.ShapeDtypeStruct((HEADS, N_META, QK_PAD), F32),
                   jax.ShapeDtypeStruct((HEADS, N_META, VDIM), F32)]
        + [jax.ShapeDtypeStruct(a.shape[1:], F32) for a in early],
        scratch_shapes=[pltpu.VMEM((2, t, s), BF16), pltpu.VMEM((2, t, s), BF16), pltpu.VMEM((s, QK_PAD), F32)]
        + [sc for red in reds for sc in red.scratch()],
        compiler_params=_cparams("arbitrary", "arbitrary"),
    )(q, k, v, do, lse, delta, km, vm, *early)


def _up_bwd(dq, dk, dv, dkm, dvm, p, pm, tabs, tabs_m, wq_p, wkv_p, gq, gkv, nb, s, tm):
    nt = s // tm
    n = nb * nt
    c_t, sa_t, sb_t = tabs
    cm_t, sam_t, sbm_t = tabs_m

    def kv_path(dkh, dvh, pa, c, sa, sb, wkv, gkvv):
        dkpe = dkh[0][:, NOPE:]
        for h in range(1, HEADS):
            dkpe = dkpe + dkh[h][:, NOPE:]
        dkr = _rope_bwd(dkpe, c, sa, sb)
        dkv = jnp.concatenate([d[:, :NOPE] for d in dkh] + list(dvh), axis=1).astype(BF16)
        ckv = pa[:, Q_RANK:Q_RANK + KV_RANK]
        kvn, rkv = _rms(ckv, gkvv)
        dckv, dg = _rms_bwd(_dot(dkv, wkv), ckv, rkv, gkvv)
        return dckv, dkr, kvn.astype(BF16), dkv, jnp.sum(dg, axis=0, keepdims=True)

    def body(dq_ref, dk_ref, dv_ref, pa_ref, c_ref, sa_ref, sb_ref,
             dkm_ref, dvm_ref, pam_ref, cm_ref, sam_ref, sbm_ref,
             wq_ref, wkv_ref, gq_ref, gkv_ref,
             dpa_ref, dpam_ref, pq_ref, pkv_ref, dgq_ref, dgkv_ref, dwq_ref, dwkv_ref):
        i = pl.program_id(0)

        @pl.when(i == 0)
        def _():
            dwq_ref[...] = jnp.zeros_like(dwq_ref)
            dwkv_ref[...] = jnp.zeros_like(dwkv_ref)
            dgq_ref[...] = jnp.zeros_like(dgq_ref)
            dgkv_ref[...] = jnp.zeros_like(dgkv_ref)

        @pl.when(i < n)
        def _():
            c, sa, sb = c_ref[...], sa_ref[...], sb_ref[...]
            pa = pa_ref[...]
            parts = []
            for h in range(HEADS):
                dqh = dq_ref[0, h].astype(F32) * ATTN_SCALE
                parts += [dqh[:, :NOPE], _rope_bwd(dqh[:, NOPE:], c, sa, sb)]
            dql = jnp.concatenate(parts, axis=1).astype(BF16)
            cq = pa[:, 0:Q_RANK]
            gqv = gq_ref[...]
            qn, rq = _rms(cq, gqv)
            dwq_ref[...] += _dot_tn(dql, qn.astype(BF16))
            dcq, dg = _rms_bwd(_dot(dql, wq_ref[...]), cq, rq, gqv)
            dgq_ref[...] += jnp.sum(dg, axis=0, keepdims=True)
            dckv, dkr, kvn, dkv, dgk = kv_path([dk_ref[0, h].astype(F32) for h in range(HEADS)],
                                               [dv_ref[0, h].astype(F32) for h in range(HEADS)],
                                               pa, c, sa, sb, wkv_ref[...], gkv_ref[...])
            dwkv_ref[...] += _dot_tn(dkv, kvn)
            dgkv_ref[...] += dgk
            dpa_ref[...] = jnp.concatenate([dcq, dckv, dkr], axis=1).astype(BF16)

        @pl.when(i == n)
        def _():
            dckv, dkr, kvn, dkv, dgk = kv_path([dkm_ref[h] for h in range(HEADS)],
                                               [dvm_ref[h] for h in range(HEADS)],
                                               pam_ref[...], cm_ref[...], sam_ref[...], sbm_ref[...],
                                               wkv_ref[...], gkv_ref[...])
            dwkv_ref[...] += _dot_tn(dkv, kvn)
            dgkv_ref[...] += dgk
            dpam_ref[...] = jnp.concatenate([jnp.zeros((N_META, Q_RANK), F32), dckv, dkr], axis=1)
            for h in range(HEADS):
                pq_ref[h] = dwq_ref[QK_PAD * h:QK_PAD * h + NOPE + ROPE, :]
                pkv_ref[h, 0:NOPE, :] = dwkv_ref[NOPE * h:NOPE * (h + 1), :]
                pkv_ref[h, NOPE:NOPE + VDIM, :] = dwkv_ref[512 + VDIM * h:512 + VDIM * (h + 1), :]

    cl = lambda i: jnp.minimum(i, n - 1)
    hb = lambda w: pl.BlockSpec((1, HEADS, tm, w), lambda i: (cl(i) // nt, 0, cl(i) % nt, 0))
    tab = pl.BlockSpec((tm, 128), lambda i: (cl(i) % nt, 0))
    full = lambda a: pl.BlockSpec(a.shape, lambda i: (0,) * a.ndim)
    const = lambda shape: pl.BlockSpec(shape, lambda i: (0,) * len(shape))
    return pl.pallas_call(
        body, name="up_bwd", grid=(n + 1,),
        in_specs=[hb(QK_PAD), hb(QK_PAD), hb(VDIM), pl.BlockSpec((tm, 512), lambda i: (cl(i), 0)), tab, tab, tab,
                  full(dkm), full(dvm), pl.BlockSpec((N_META, 512), lambda i: (0, 0)),
                  full(cm_t), full(sam_t), full(sbm_t), full(wq_p), full(wkv_p), full(gq), full(gkv)],
        out_specs=[pl.BlockSpec((tm, 512), lambda i: (cl(i), 0)), const((N_META, 512)),
                   const((HEADS, NOPE + ROPE, Q_RANK)), const((HEADS, NOPE + VDIM, KV_RANK)),
                   const((1, Q_RANK)), const((1, KV_RANK))],
        out_shape=[jax.ShapeDtypeStruct((nb * s, 512), BF16), jax.ShapeDtypeStruct((N_META, 512), F32),
                   jax.ShapeDtypeStruct((HEADS, NOPE + ROPE, Q_RANK), F32),
                   jax.ShapeDtypeStruct((HEADS, NOPE + VDIM, KV_RANK), F32),
                   jax.ShapeDtypeStruct((1, Q_RANK), F32), jax.ShapeDtypeStruct((1, KV_RANK), F32)],
        scratch_shapes=[pltpu.VMEM((HEADS * QK_PAD, Q_RANK), F32), pltpu.VMEM((1024, KV_RANK), F32)],
        compiler_params=_cparams("arbitrary"),
    )(dq, dk, dv, p, c_t, sa_t, sb_t, dkm, dvm, pm, cm_t, sam_t, sbm_t, wq_p, wkv_p, gq, gkv)


def _in_bwd(x2d, dh2, dpa, dpb, meta, dpam, dccm, pm, w_in_p, norm_g, nb, s, tm):
    nt = s // tm
    n = nb * nt

    def body(x_ref, dh_ref, dpa_ref, dpb_ref, mt_ref, dpam_ref, dccm_ref, mc_ref, mh_ref, w_ref, g_ref,
             gx_ref, gm_ref, dw_hbm, dg_ref, acc_ref, sems):
        i = pl.program_id(0)

        @pl.when(i == 0)
        def _():
            acc_ref[...] = jnp.zeros_like(acc_ref)
            dg_ref[...] = jnp.zeros_like(dg_ref)

        def rows(x, dp, dres):
            g = g_ref[...]
            dpb16 = dp.astype(BF16)
            du = _dot(dpb16, w_ref[...])
            u, r1 = _rms(x, g)
            acc_ref[...] += _dot_tn(dpb16, u.astype(BF16))
            dx, dg = _rms_bwd(du, x, r1, g)
            dg_ref[...] += jnp.sum(dg, axis=0, keepdims=True)
            return dx if dres is None else dx + dres

        @pl.when(i < n)
        def _():
            dp = jnp.concatenate([dpa_ref[...], dpb_ref[...]], axis=1)
            gx_ref[...] = rows(x_ref[...], dp, dh_ref[...])

        @pl.when(i == n)
        def _():
            dcc = dccm_ref[0]
            for b in range(1, nb):
                dcc = dcc + dccm_ref[b]
            z8 = jnp.zeros((8, CONV_W), F32)
            dc = jnp.concatenate([z8, dcc * mh_ref[8:16, :]], axis=0)
            dh = jnp.concatenate([z8, dcc * mc_ref[8:16, :]], axis=0)
            z = jnp.zeros((N_META, CONV_W), F32)
            dp = jnp.concatenate([dpam_ref[...], z, z, dc, dh, z], axis=1)
            gm_ref[...] = rows(mt_ref[...], dp, None)
            per = IN_DIM // 4
            cps = [pltpu.make_async_copy(acc_ref.at[0:448], dw_hbm.at[0, 0:448], sems.at[0]),
                   pltpu.make_async_copy(acc_ref.at[512:per + 64], dw_hbm.at[0, 448:per], sems.at[1])]
            for qq in range(1, 4):
                cps.append(pltpu.make_async_copy(acc_ref.at[per * qq + 64:per * (qq + 1) + 64], dw_hbm.at[qq],
                                                 sems.at[qq + 1]))
            for cp in cps:
                cp.start()
            for cp in cps:
                cp.wait()

    cl = lambda i: jnp.minimum(i, n - 1)
    row = lambda w: pl.BlockSpec((tm, w), lambda i: (cl(i), 0))
    full = lambda a: pl.BlockSpec(a.shape, lambda i: (0,) * a.ndim)
    mblk = lambda j: pl.BlockSpec((N_META, 512), lambda i: (0, j))
    return pl.pallas_call(
        body, name="in_bwd", grid=(n + 1,),
        in_specs=[row(D_MODEL), row(D_MODEL), row(512), row(2560), full(meta), full(dpam), full(dccm),
                  mblk(BLK_CC), mblk(BLK_CH), full(w_in_p), full(norm_g)],
        out_specs=[row(D_MODEL), pl.BlockSpec((N_META, D_MODEL), lambda i: (0, 0)),
                   pl.BlockSpec(memory_space=pl.ANY), pl.BlockSpec((1, D_MODEL), lambda i: (0, 0))],
        out_shape=[jax.ShapeDtypeStruct((nb * s, D_MODEL), F32), jax.ShapeDtypeStruct((N_META, D_MODEL), F32),
                   jax.ShapeDtypeStruct((4, IN_DIM // 4, D_MODEL), F32), jax.ShapeDtypeStruct((1, D_MODEL), F32)],
        scratch_shapes=[pltpu.VMEM((IN_PAD, D_MODEL), F32), pltpu.SemaphoreType.DMA((5,))],
        compiler_params=_cparams("arbitrary"),
    )(x2d, dh2, dpa, dpb, meta, dpam, dccm, pm, pm, w_in_p, norm_g)


def _gather_weights(split, pieces, out_rows, whole, zero_fills):
    ns, nw, nz = len(split), len(whole), len(zero_fills)
    flat = [(a, pc) for a in range(ns) for pc in pieces[a]]
    nk = len(flat)

    def body(*refs):
        ins, wins, zins = refs[:ns], refs[ns:ns + nw], refs[ns + nw:ns + nw + nz]
        outs, wouts = refs[ns + nw + nz:2 * ns + nw + nz], refs[2 * ns + nw + nz:2 * (ns + nw) + nz]
        scr = refs[2 * (ns + nw) + nz:]
        stage, (send_sems, recv_sems, fwd_send, fwd_recv, loc_sems, w_send, w_recv, w_loc, z_sems) = scr[:ns], scr[ns:]
        x, y, c = lax.axis_index("x"), lax.axis_index("y"), lax.axis_index("c")
        mine = 2 * x + y
        chips = [(1 - x, y), (x, 1 - y), (1 - x, 1 - y)]
        chip_of = [2 * px + py for px, py in chips]
        for a in range(ns):
            stage[a][...] = ins[a][...].astype(BF16)

        def src(k):
            a, (s0, nr, _, _, _, _) = flat[k]
            return stage[a].at[s0:s0 + nr]

        def dst(k, q):
            a, (_, nr, per, first, rest, _) = flat[k]
            row = per * q + first + (rest - first) * jnp.minimum(q, 1)
            return outs[a].at[pl.ds(pl.multiple_of(row, 16), nr)]

        def ici(k, j, q):
            px, py = chips[j]
            return pltpu.make_async_remote_copy(
                src_ref=src(k), dst_ref=dst(k, q), send_sem=send_sems.at[k, j], recv_sem=recv_sems.at[k, j],
                device_id=(px, py, c), device_id_type=pl.DeviceIdType.MESH)

        def fwd(k, j):
            ref = dst(k, chip_of[j])
            return pltpu.make_async_remote_copy(
                src_ref=ref, dst_ref=ref, send_sem=fwd_send.at[k, j], recv_sem=fwd_recv.at[k, j],
                device_id=(x, y, 1 - c), device_id_type=pl.DeviceIdType.MESH)

        def wcopy(b, j, q):
            px, py = chips[j]
            return pltpu.make_async_remote_copy(
                src_ref=wins[b], dst_ref=wouts[b].at[q], send_sem=w_send.at[b, j], recv_sem=w_recv.at[b, j],
                device_id=(px, py, c), device_id_type=pl.DeviceIdType.MESH)

        local = [pltpu.make_async_copy(src(k), dst(k, mine), loc_sems.at[k]) for k in range(nk)]
        local += [pltpu.make_async_copy(wins[b], wouts[b].at[mine], w_loc.at[b]) for b in range(nw)]
        for z, (a, _, row0) in enumerate(zero_fills):
            local.append(pltpu.make_async_copy(zins[z], outs[a].at[row0:row0 + zins[z].shape[0]], z_sems.at[z]))
        wsends = [wcopy(b, j, mine) for b in range(nw) for j in range(3)]
        for cp in local + wsends:
            cp.start()

        for half in (0, 1):
            @pl.when(c == half)
            def _(half=half):
                my_k = [k for k in range(nk) if flat[k][1][5] == half]
                other_k = [k for k in range(nk) if flat[k][1][5] != half]
                sends = [ici(k, j, mine) for k in my_k for j in range(3)]
                for cp in sends:
                    cp.start()
                passed = []
                for k in my_k:
                    for j in range(3):
                        ici(k, j, chip_of[j]).wait_recv()
                        cp = fwd(k, j)
                        cp.start()
                        passed.append(cp)
                for k in other_k:
                    for j in range(3):
                        fwd(k, j).wait_recv()
                for cp in sends + passed:
                    cp.wait_send()

        for b in range(nw):
            for j in range(3):
                wcopy(b, j, chip_of[j]).wait_recv()
        for cp in wsends:
            cp.wait_send()
        for cp in local:
            cp.wait()

    vmem = pl.BlockSpec(memory_space=pltpu.VMEM)
    dma = pltpu.SemaphoreType.DMA
    zeros = [z for _, z, _ in zero_fills]
    return pl.pallas_call(
        body, name="gather_weights",
        in_specs=[vmem] * (ns + nw + nz), out_specs=[vmem] * (ns + nw),
        out_shape=([jax.ShapeDtypeStruct((out_rows[a], split[a].shape[1]), BF16) for a in range(ns)]
                   + [jax.ShapeDtypeStruct((4,) + w.shape, w.dtype) for w in whole]),
        scratch_shapes=[pltpu.VMEM(a.shape, BF16) for a in split]
        + [dma((nk, 3)), dma((nk, 3)), dma((nk, 3)), dma((nk, 3)), dma((nk,)),
           dma((nw, 3)), dma((nw, 3)), dma((nw,)), dma((nz,))],
        compiler_params=pltpu.CompilerParams(vmem_limit_bytes=VMEM_LIMIT),
    )(*split, *whole, *zeros)


def _reduce_grads(parts, small):
    n = len(parts)
    shapes = [a.shape[1:] for a in parts]
    halves = [(sh[0] // 2, sh[1]) for sh in shapes]

    def body(*refs):
        pin, sm_in = refs[:n], refs[n]
        gout, sm_out = refs[n + 1:2 * n + 1], refs[2 * n + 1]
        scr = refs[2 * n + 2:]
        own, sib, wire, rbuf = scr[:n], scr[n:2 * n], scr[2 * n:3 * n], scr[3 * n:4 * n]
        (sbuf, send_sems, recv_sems, loc_sems, pre_send, pre_recv, post_send, post_recv,
         sm_send, sm_recv) = scr[4 * n:]
        x, y, c = lax.axis_index("x"), lax.axis_index("y"), lax.axis_index("c")
        mine = 2 * x + y
        me = 4 * x + 2 * y + c
        sibling = (x, y, 1 - c)
        chips = [(1 - x, y), (x, 1 - y), (1 - x, 1 - y)]

        def rows(a, half):
            r2 = halves[a][0]
            return pl.ds(pl.multiple_of(half * r2, r2), r2)

        chip_of = [2 * px + py for px, py in chips]
        blocks = chip_of + [mine]

        def pre(a, k):
            return pltpu.make_async_remote_copy(
                src_ref=pin[a].at[blocks[k], rows(a, 1 - c), :], dst_ref=sib[a].at[blocks[k]],
                send_sem=pre_send.at[a, k], recv_sem=pre_recv.at[a, k], device_id=sibling,
                device_id_type=pl.DeviceIdType.MESH)

        def ici(a, j):
            px, py = chips[j]
            return pltpu.make_async_remote_copy(
                src_ref=wire[a].at[2 * px + py], dst_ref=rbuf[a].at[j], send_sem=send_sems.at[a, j],
                recv_sem=recv_sems.at[a, j], device_id=(px, py, c), device_id_type=pl.DeviceIdType.MESH)

        def post(a, half):
            ref = gout[a].at[rows(a, half), :]
            return pltpu.make_async_remote_copy(
                src_ref=ref, dst_ref=ref, send_sem=post_send.at[a], recv_sem=post_recv.at[a],
                device_id=sibling, device_id_type=pl.DeviceIdType.MESH)

        def small_copy(kk):
            peer = (x ^ (kk >> 2), y ^ ((kk >> 1) & 1), c ^ (kk & 1))
            return pltpu.make_async_remote_copy(
                src_ref=sm_in, dst_ref=sbuf.at[kk], send_sem=sm_send.at[kk - 1], recv_sem=sm_recv.at[kk - 1],
                device_id=peer, device_id_type=pl.DeviceIdType.MESH)

        local = [[pltpu.make_async_copy(pin[a].at[blocks[k], rows(a, c), :], own[a].at[blocks[k]], loc_sems.at[a, k])
                  for k in range(4)] for a in range(n)]
        pres = [[pre(a, k) for k in range(4)] for a in range(n)]
        smalls = [small_copy(kk) for kk in range(1, 8)]
        for a in range(n):
            for k in range(4):
                local[a][k].start()
                pres[a][k].start()
        for cp in smalls:
            cp.start()
        sbuf[0] = sm_in[...]
        sends = []
        for a in range(n):
            for k in range(4):
                local[a][k].wait()
                pres[a][k].wait_recv()
                tot = own[a][blocks[k]] + sib[a][blocks[k]]
                own[a][blocks[k]] = tot
                if k < 3:
                    wire[a][blocks[k]] = tot.astype(BF16)
                    cp = ici(a, k)
                    cp.start()
                    sends.append(cp)
        for cp in smalls:
            cp.wait_recv()
        total = sbuf[me]
        for d in range(1, 8):
            total = total + sbuf[me ^ d]
        sm_out[...] = total
        posts = []
        for a in range(n):
            for j in range(3):
                ici(a, j).wait_recv()
            fin = own[a][mine]
            for j in range(3):
                fin = fin + rbuf[a][j].astype(F32)
            gout[a][rows(a, c), :] = fin
            cp = post(a, c)
            cp.start()
            posts.append(cp)
        for a in range(n):
            post(a, 1 - c).wait_recv()
        for cp in [cp for row in pres for cp in row] + sends + smalls + posts:
            cp.wait_send()

    vmem = pl.BlockSpec(memory_space=pltpu.VMEM)
    dma = pltpu.SemaphoreType.DMA
    return pl.pallas_call(
        body, name="reduce_grads",
        in_specs=[pl.BlockSpec(memory_space=pl.ANY)] * n + [vmem], out_specs=[vmem] * (n + 1),
        out_shape=[jax.ShapeDtypeStruct(sh, F32) for sh in shapes] + [jax.ShapeDtypeStruct(small.shape, F32)],
        scratch_shapes=([pltpu.VMEM((4,) + hs, F32) for hs in halves] + [pltpu.VMEM((4,) + hs, F32) for hs in halves]
                        + [pltpu.VMEM((4,) + hs, BF16) for hs in halves]
                        + [pltpu.VMEM((3,) + hs, BF16) for hs in halves]
                        + [pltpu.VMEM((8,) + small.shape, F32), dma((n, 3)), dma((n, 3)), dma((n, 4)),
                           dma((n, 4)), dma((n, 4)), dma((n,)), dma((n,)), dma((7,)), dma((7,))]),
        compiler_params=pltpu.CompilerParams(vmem_limit_bytes=VMEM_LIMIT),
    )(*parts, small)


def _adamw_update(w_ref, g_ref, m_ref, v_ref, d_ref, nm_ref, nv_ref):
    gv = g_ref[...]
    nm = ADAM_B1 * m_ref[...] + (1.0 - ADAM_B1) * gv
    nv = ADAM_B2 * v_ref[...] + (1.0 - ADAM_B2) * (gv * gv)
    m_hat = nm / (1.0 - ADAM_B1 ** ADAM_STEP)
    v_hat = nv / (1.0 - ADAM_B2 ** ADAM_STEP)
    d_ref[...] = -ADAM_LR * (m_hat / (jnp.sqrt(v_hat) + ADAM_EPS) + ADAM_WD * w_ref[...])
    nm_ref[...] = nm
    nv_ref[...] = nv


def _adamw_small(ws, gs, ms, vs):
    k = len(ws)

    def body(*refs):
        ins, outs = refs[:4 * k], refs[4 * k:]
        for a in range(k):
            _adamw_update(ins[a], ins[k + a], ins[2 * k + a], ins[3 * k + a], outs[a], outs[k + a], outs[2 * k + a])

    out = pl.pallas_call(
        body, name="adamw_small",
        out_shape=[jax.ShapeDtypeStruct(w.shape, F32) for w in ws] * 3,
        compiler_params=pltpu.CompilerParams(vmem_limit_bytes=VMEM_LIMIT),
    )(*ws, *gs, *ms, *vs)
    return out[:k], out[k:2 * k], out[2 * k:]


def _adamw(w, g, m, v, name):
    shape = w.shape
    w2, g2, m2, v2 = (a.reshape((-1, shape[-1])) for a in (w, g, m, v))

    def body(w_ref, g_ref, m_ref, v_ref, d_ref, nm_ref, nv_ref):
        _adamw_update(w_ref, g_ref, m_ref, v_ref, d_ref, nm_ref, nv_ref)

    rows, cols = w2.shape
    nblk = cols // 256 if cols % 256 == 0 and rows >= 64 else 1
    blk = pl.BlockSpec((rows, cols // nblk), lambda j: (0, j))
    out = pl.pallas_call(
        body, name=name, grid=(nblk,), in_specs=[blk] * 4, out_specs=[blk] * 3,
        out_shape=[jax.ShapeDtypeStruct(w2.shape, F32)] * 3,
        compiler_params=_cparams("parallel"),
    )(w2, g2, m2, v2)
    return tuple(a.reshape(shape) for a in out)


def kernel(x, meta_tokens, norm_g, w_in, q_norm_g, w_q_up, kv_norm_g, w_kv_up, conv_w, attn_out_g, conv_out_g, w_out, final_norm_g, loss_target, m_meta_tokens, m_norm_g, m_w_in, m_q_norm_g, m_w_q_up, m_kv_norm_g, m_w_kv_up, m_conv_w, m_attn_out_g, m_conv_out_g, m_w_out, m_final_norm_g, v_meta_tokens, v_norm_g, v_w_in, v_q_norm_g, v_w_q_up, v_kv_norm_g, v_w_kv_up, v_conv_w, v_attn_out_g, v_conv_out_g, v_w_out, v_final_norm_g):
    nb, s, _ = x.shape
    tm = min(ROW_TILE, s)
    ta = min(ATTN_TILE, s)
    assert s % tm == 0 and s % ta == 0 and tm % 16 == 0
    r = nb * s

    tr = lambda a: jnp.transpose(a[0])
    w_in_p, wq_p, wkv_p, g_cw, g_meta = _gather_weights(
        [tr(w_in), tr(w_q_up), tr(w_kv_up)],
        [W_IN_PIECES, W_Q_PIECES, W_KV_PIECES], [IN_PAD, HEADS * QK_PAD, 1024],
        [jnp.transpose(conv_w, (1, 0, 2)), meta_tokens],
        [(0, jnp.zeros((64, D_MODEL), BF16), 448)]
        + [(1, jnp.zeros((64, Q_RANK), BF16), QK_PAD * h + NOPE + ROPE) for h in range(HEADS)])
    conv_f = jnp.transpose(g_cw[:, :, 0, :], (1, 0, 2)).reshape(3, CONV_W)
    meta_f = jnp.transpose(g_meta, (1, 0, 2)).reshape(N_META, D_MODEL)

    c_all, sa_all, sb_all = _rope_tables(N_META + s)
    tabs_m = (c_all[:N_META], sa_all[:N_META], sb_all[:N_META])
    tabs = (c_all[N_META:], sa_all[N_META:], sb_all[N_META:])
    gid = np.arange(CONV_W) // CONV_GROUP
    gmat = jnp.asarray(np.where(gid[:, None] == gid[None, :], 1.0 / CONV_GROUP, 0.0), BF16)
    ga, gc = attn_out_g, conv_out_g
    gf = final_norm_g.reshape(1, D_MODEL)

    x2d = x.reshape(r, D_MODEL)
    tgt2d = loss_target.reshape(r, D_MODEL)

    p, q, k, v, pm, km, vm, w_out_f = _fwd_proj(x2d, meta_f, tabs, tabs_m, norm_g, w_in_p, q_norm_g, wq_p,
                                                kv_norm_g, wkv_p, w_out[0].astype(BF16), nb, s, tm)
    o, lse = _attn_fwd(q, k, v, km, vm, nb, s, ta)
    dh2, dycat, dw_out, dgf, loss_acc = _out_fwd_bwd(x2d, tgt2d, o, p, pm, conv_f, ga, gc, gmat, w_out_f, gf,
                                                     nb, s, tm)
    dpb, do, delta, dccm, dga, dgc, dcw = _gate_bwd(dycat, o, p, pm, conv_f, ga, gc, gmat, nb, s, tm)
    p_out = dw_out.reshape(4, D_MODEL // 4, D_MODEL)
    dq, dk, dv, dkm, dvm, g_w_out = _attn_bwd(q, k, v, do, lse, delta, km, vm, [p_out], nb, s, ta)
    dpa, dpam, p_q, p_kv, dgq, dgkv = _up_bwd(dq, dk, dv, dkm, dvm, p, pm, tabs, tabs_m, wq_p, wkv_p,
                                              q_norm_g, kv_norm_g, nb, s, tm)
    gx, gmeta, p_in, dng = _in_bwd(x2d, dh2, dpa, dpb, meta_f, dpam, dccm, pm, w_in_p, norm_g, nb, s, tm)

    flat =jnp.concatenate([dng.reshape(-1), dgq.reshape(-1), dgkv.reshape(-1), dga.reshape(-1), dgc.reshape(-1),
                            dgf.reshape(-1), dcw[:3].reshape(-1), gmeta.reshape(-1), loss_acc[0, 0:1]])
    n_small = flat.shape[0]
    rows_small = -(-n_small // 1024) * 8
    small = jnp.pad(flat, (0, rows_small * 128 - n_small)).reshape(rows_small, 128)
    g_w_in_t, g_w_q_t, g_w_kv_t, small_sum = _reduce_grads([p_in, p_q, p_kv], small)
    ssum = small_sum.reshape(-1)

    def take(off, n):
        return ssum[off:off + n], off + n

    off = 0
    g_norm, off = take(off, D_MODEL)
    g_qn, off = take(off, Q_RANK)
    g_kvn, off = take(off, KV_RANK)
    g_ga, off = take(off, CONV_W)
    g_gc, off = take(off, CONV_W)
    g_gf, off = take(off, D_MODEL)
    g_cw_all, off = take(off, 3 * CONV_W)
    g_meta_all, off = take(off, N_META * D_MODEL)
    loss = ssum[off]
    chip = 2 * lax.axis_index("x") + lax.axis_index("y")
    g_conv = lax.dynamic_slice(g_cw_all.reshape(3, CONV_W), (0, chip * 128), (3, 128))
    g_mt = lax.dynamic_slice(g_meta_all.reshape(N_META, D_MODEL), (0, chip * 256), (N_META, 256))

    grads = {
        "meta_tokens": g_mt, "norm_g": g_norm.reshape(1, -1), "w_in": g_w_in_t, "q_norm_g": g_qn.reshape(1, -1),
        "w_q_up": g_w_q_t, "kv_norm_g": g_kvn.reshape(1, -1), "w_kv_up": jnp.transpose(g_w_kv_t)[None],
        "conv_w": g_conv[None], "attn_out_g": g_ga.reshape(1, -1), "conv_out_g": g_gc.reshape(1, -1),
        "w_out": g_w_out[None], "final_norm_g": g_gf,
    }
    transposed = ("w_in", "w_q_up")
    weights = {
        "meta_tokens": (meta_tokens, m_meta_tokens, v_meta_tokens), "norm_g": (norm_g, m_norm_g, v_norm_g),
        "w_in": (w_in, m_w_in, v_w_in), "q_norm_g": (q_norm_g, m_q_norm_g, v_q_norm_g),
        "w_q_up": (w_q_up, m_w_q_up, v_w_q_up), "kv_norm_g": (kv_norm_g, m_kv_norm_g, v_kv_norm_g),
        "w_kv_up": (w_kv_up, m_w_kv_up, v_w_kv_up), "conv_w": (conv_w, m_conv_w, v_conv_w),
        "attn_out_g": (attn_out_g, m_attn_out_g, v_attn_out_g), "conv_out_g": (conv_out_g, m_conv_out_g, v_conv_out_g),
        "w_out": (w_out, m_w_out, v_w_out), "final_norm_g": (final_norm_g, m_final_norm_g, v_final_norm_g),
    }
    names = list(weights)
    small = [nme for nme in names if nme != "w_in"]

    def view(nme, a):
        if nme in transposed:
            return a if a.ndim == 2 else tr(a)
        if nme == "conv_w":
            return jnp.transpose(a.reshape(1, 3, -1), (1, 0, 2))
        if a.ndim == 3:
            return a[0]
        return a.reshape(1, -1) if a.ndim == 1 else a

    def unview(nme, a):
        if nme in transposed:
            return jnp.transpose(a)[None]
        if nme == "conv_w":
            return jnp.transpose(a, (1, 0, 2))
        return a.reshape(weights[nme][0].shape)

    res_small = _adamw_small(*[[view(nme, a) for nme, a in zip(small, col)] for col in (
        [weights[nme][0] for nme in small], [grads[nme] for nme in small],
        [weights[nme][1] for nme in small], [weights[nme][2] for nme in small])])
    w_, m_, v_ = weights["w_in"]
    res = _adamw(tr(w_), grads["w_in"], tr(m_), tr(v_), "adamw_w_in")
    upd = {"w_in": tuple(jnp.transpose(a)[None] for a in (grads["w_in"],) + res)}
    for j, nme in enumerate(small):
        upd[nme] = (unview(nme, view(nme, grads[nme])),) + tuple(unview(nme, r[j]) for r in res_small)
    grads = {nme: upd[nme][0] for nme in names}
    deltas, new_m, new_v = ([upd[nme][j] for nme in names] for j in (1, 2, 3))

    grad_x = gx.reshape(nb, s, D_MODEL)
    return (loss, grad_x, *[grads[nme] for nme in names], *deltas, *new_m, *new_v)
```

```python
import functools

import jax
import jax.numpy as jnp
import numpy as np
from jax import lax
from jax.experimental import pallas as pl
from jax.experimental.pallas import tpu as pltpu

F32 = jnp.float32
BF16 = jnp.bfloat16

D_MODEL = 1024
N_META = 16
HEADS = 4
NOPE = 128
ROPE = 64
VDIM = 128
QK_PAD = 256
Q_RANK = 256
KV_RANK = 128
CONV_W = 512
CONV_GROUP = 64
ROPE_THETA = 10000.0
EPS = 1e-6
ATTN_SCALE = (NOPE + ROPE) ** -0.5
IN_DIM = 3008
IN_PAD = 3072
HEAD_ROWS = Q_RANK + KV_RANK + ROPE
IN_HEAD = 512
IN_TAIL = IN_PAD - IN_HEAD
BLK_ZA, BLK_CB, BLK_CC, BLK_CH, BLK_ZC = 0, 1, 2, 3, 4
NEG_INF = -1e30

ADAM_LR = 0.001
ADAM_B1 = 0.9
ADAM_B2 = 0.999
ADAM_EPS = 1e-08
ADAM_WD = 0.01
ADAM_STEP = 10

ROW_TILE = 512
ATTN_TILE = 256
VMEM_LIMIT = 56 * 1024 * 1024

NT = (((1,), (1,)), ((), ()))
TN = (((0,), (0,)), ((), ()))


def _cparams(*sem):
    return pltpu.CompilerParams(dimension_semantics=sem, vmem_limit_bytes=VMEM_LIMIT)


def _dot(a, b):
    return jnp.dot(a, b, preferred_element_type=F32)


def _dot_nt(a, b):
    return lax.dot_general(a, b, NT, preferred_element_type=F32)


def _dot_tn(a, b):
    return lax.dot_general(a, b, TN, preferred_element_type=F32)


def _rms(x, g):
    r = lax.rsqrt(jnp.mean(x * x, axis=-1, keepdims=True) + EPS)
    return x * r * g, r


def _rms_bwd(dy, x, r, g):
    xh = x * r
    dyg = dy * g
    dx = r * (dyg - xh * jnp.mean(dyg * xh, axis=-1, keepdims=True))
    return dx, dy * xh


def _sigmoid(z):
    return 1.0 / (1.0 + jnp.exp(-z))


def _rope(b, c, sa, sb):
    return b * c + pltpu.roll(b, 96, 1) * sa + pltpu.roll(b, 32, 1) * sb


def _rope_bwd(d, c, sa, sb):
    return d * c + pltpu.roll(d * sa, 32, 1) + pltpu.roll(d * sb, 96, 1)


def _group_mean(x, gmat):
    hi = x.astype(BF16)
    lo = (x - hi.astype(F32)).astype(BF16)
    return _dot(hi, gmat) + _dot(lo, gmat)


def _row_of(col, rows):
    return jnp.transpose(jnp.broadcast_to(col, (rows, 128)))[0:1, :]


def _rope_tables(n_pos):
    half = ROPE // 2
    inv_freq = (np.float32(1.0) / (np.float32(ROPE_THETA) ** (np.arange(half, dtype=np.float32) / np.float32(half))))
    ang = np.arange(n_pos, dtype=np.float32)[:, None] * inv_freq.astype(np.float32)[None, :]
    cos, sin = np.cos(ang).astype(np.float32), np.sin(ang).astype(np.float32)
    z = np.zeros((n_pos, half), np.float32)
    c = np.concatenate([cos, cos, z, z], axis=1)
    sa = np.concatenate([-sin, z, z, z], axis=1)
    sb = np.concatenate([z, sin, z, z], axis=1)
    return jnp.asarray(c), jnp.asarray(sa), jnp.asarray(sb)


W_IN_PIECES = ((0, 384, 752, 0, 64, 0), (384, 64, 752, 384, 448, 1), (448, 304, 752, 512, 512, 1))
W_Q_PIECES = ((0, 96, 256, 0, 0, 0), (96, 96, 256, 96, 96, 1))
W_KV_PIECES = ((0, 128, 128, 0, 0, 0), (128, 128, 128, 512, 512, 1))
W_OUT_PIECES = ((0, 128, 256, 0, 0, 0), (128, 128, 256, 128, 128, 1))


class _StagedGather:
    def __init__(self, pieces):
        self.pieces = pieces

    def scratch(self):
        nk, dma = len(self.pieces), pltpu.SemaphoreType.DMA
        return [dma((nk, 3)), dma((nk, 3)), dma((nk, 3)), dma((nk, 3)), dma((nk,))]

    def run(self, stage, src_ref, out_ref, scr):
        send_sems, recv_sems, fwd_send, fwd_recv, loc_sems = scr
        pieces = self.pieces
        nk = len(pieces)
        x, y, c = lax.axis_index("x"), lax.axis_index("y"), lax.axis_index("c")
        mine = 2 * x + y
        chips = [(1 - x, y), (x, 1 - y), (1 - x, 1 - y)]
        chip_of = [2 * px + py for px, py in chips]
        mesh = pl.DeviceIdType.MESH

        def src(k):
            s0, nr = pieces[k][0], pieces[k][1]
            return src_ref.at[s0:s0 + nr]

        def dst(k, q):
            _, nr, per, first, rest, _ = pieces[k]
            row = per * q + first + (rest - first) * jnp.minimum(q, 1)
            return out_ref.at[pl.ds(pl.multiple_of(row, 16), nr)]

        def ici(k, j, q):
            px, py = chips[j]
            return pltpu.make_async_remote_copy(
                src_ref=src(k), dst_ref=dst(k, q), send_sem=send_sems.at[k, j], recv_sem=recv_sems.at[k, j],
                device_id=(px, py, c), device_id_type=mesh)

        def fwd(k, j):
            ref = dst(k, chip_of[j])
            return pltpu.make_async_remote_copy(
                src_ref=ref, dst_ref=ref, send_sem=fwd_send.at[k, j], recv_sem=fwd_recv.at[k, j],
                device_id=(x, y, 1 - c), device_id_type=mesh)

        local = [pltpu.make_async_copy(src(k), dst(k, mine), loc_sems.at[k]) for k in range(nk)]
        if stage == 0:
            for cp in local:
                cp.start()
        if stage == 2:
            for cp in local:
                cp.wait()
        for half in (0, 1):
            @pl.when(c == half)
            def _(half=half):
                my_k = [k for k in range(nk) if pieces[k][5] == half]
                other_k = [k for k in range(nk) if pieces[k][5] != half]
                for k in my_k:
                    for j in range(3):
                        if stage == 0:
                            ici(k, j, mine).start()
                        elif stage == 1:
                            ici(k, j, chip_of[j]).wait_recv()
                            fwd(k, j).start()
                        else:
                            ici(k, j, mine).wait_send()
                            fwd(k, j).wait_send()
                if stage == 2:
                    for k in other_k:
                        for j in range(3):
                            fwd(k, j).wait_recv()


def _fwd_proj(x2d, meta, tabs, tabs_m, norm_g, w_head, q_norm_g, wq_p, kv_norm_g, wkv_p, w_out_shard, nb, s, tm):
    nt = s // tm
    n = nb * nt
    n_steps = n + 1
    c_t, sa_t, sb_t = tabs
    cm_t, sam_t, sbm_t = tabs_m
    gat = _StagedGather(W_OUT_PIECES)
    assert n_steps >= 3

    def body(x_ref, c_ref, sa_ref, sb_ref, mt_ref, cm_ref, sam_ref, sbm_ref,
             g_ref, w_ref, gq_ref, wq_ref, gkv_ref, wkv_ref, wos_ref,
             p_ref, q_ref, k_ref, v_ref, pm_ref, km_ref, vm_ref, wo_ref, *gat_scr):
        i = pl.program_id(0)
        for stage, at in enumerate((0, n_steps - 2, n_steps - 1)):
            @pl.when(i == at)
            def _(stage=stage):
                gat.run(stage, wos_ref, wo_ref, gat_scr)

        def project(xv, c, sa, sb, p_out, q_out, k_out, v_out):
            u, _ = _rms(xv, g_ref[...])
            p = _dot_nt(u.astype(BF16), w_ref[...])
            p_out[...] = p
            qn, _ = _rms(p[:, 0:Q_RANK], gq_ref[...])
            q = _dot_nt(qn.astype(BF16), wq_ref[...])
            kvn, _ = _rms(p[:, Q_RANK:Q_RANK + KV_RANK], gkv_ref[...])
            kv = _dot_nt(kvn.astype(BF16), wkv_ref[...])
            kpe = _rope(p[:, 384:512], c, sa, sb)
            for h in range(HEADS):
                if q_out is not None:
                    pe = _rope(q[:, QK_PAD * h + NOPE:QK_PAD * (h + 1)], c, sa, sb)
                    qh = jnp.concatenate([q[:, QK_PAD * h:QK_PAD * h + NOPE], pe], axis=1)
                    q_out[0, h] = (qh * ATTN_SCALE).astype(BF16)
                k_out[0, h] = jnp.concatenate([kv[:, NOPE * h:NOPE * (h + 1)], kpe], axis=1).astype(BF16)
                v_out[0, h] = kv[:, 512 + VDIM * h:512 + VDIM * (h + 1)].astype(BF16)

        @pl.when(i < n)
        def _():
            project(x_ref[...], c_ref[...], sa_ref[...], sb_ref[...], p_ref, q_ref, k_ref, v_ref)

        @pl.when(i == n)
        def _():
            project(mt_ref[...], cm_ref[...], sam_ref[...], sbm_ref[...], pm_ref, None, km_ref, vm_ref)

    cl = lambda i: jnp.minimum(i, n - 1)
    full = lambda a: pl.BlockSpec(a.shape, lambda i: (0,) * a.ndim)
    const = lambda shape: pl.BlockSpec(shape, lambda i: (0,) * len(shape))
    tab = pl.BlockSpec((tm, 128), lambda i: (cl(i) % nt, 0))
    hb = lambda w: pl.BlockSpec((1, HEADS, tm, w), lambda i: (cl(i) // nt, 0, cl(i) % nt, 0))
    whole = pl.BlockSpec(memory_space=pl.ANY)
    return pl.pallas_call(
        body, name="fwd_proj", grid=(n_steps,),
        in_specs=[pl.BlockSpec((tm, D_MODEL), lambda i: (cl(i), 0)), tab, tab, tab,
                  full(meta), full(cm_t), full(sam_t), full(sbm_t),
                  full(norm_g), full(w_head), full(q_norm_g), full(wq_p), full(kv_norm_g), full(wkv_p), whole],
        out_specs=[pl.BlockSpec((tm, IN_HEAD), lambda i: (cl(i), 0)), hb(QK_PAD), hb(QK_PAD), hb(VDIM),
                   const((N_META, IN_HEAD)), const((1, HEADS, N_META, QK_PAD)), const((1, HEADS, N_META, VDIM)),
                   whole],
        out_shape=[jax.ShapeDtypeStruct((nb * s, IN_HEAD), F32),
                   jax.ShapeDtypeStruct((nb, HEADS, s, QK_PAD), BF16),
                   jax.ShapeDtypeStruct((nb, HEADS, s, QK_PAD), BF16),
                   jax.ShapeDtypeStruct((nb, HEADS, s, VDIM), BF16),
                   jax.ShapeDtypeStruct((N_META, IN_HEAD), F32),
                   jax.ShapeDtypeStruct((1, HEADS, N_META, QK_PAD), BF16),
                   jax.ShapeDtypeStruct((1, HEADS, N_META, VDIM), BF16),
                   jax.ShapeDtypeStruct((D_MODEL, D_MODEL), BF16)],
        scratch_shapes=gat.scratch(),
        compiler_params=_cparams("arbitrary"),
    )(x2d, c_t, sa_t, sb_t, meta, cm_t, sam_t, sbm_t, norm_g, w_head, q_norm_g, wq_p, kv_norm_g, wkv_p, w_out_shard)


def _fwd_proj_tail(x2d, meta, norm_g, w_in_p, nb, s, tm):
    n = nb * s // tm

    def body(x_ref, mt_ref, g_ref, w_ref, p_ref, pm_ref):
        i = pl.program_id(0)

        def project(xv, p_out):
            u, _ = _rms(xv, g_ref[...])
            p_out[...] = _dot_nt(u.astype(BF16), w_ref[IN_HEAD:IN_PAD, :])

        @pl.when(i < n)
        def _():
            project(x_ref[...], p_ref)

        @pl.when(i == n)
        def _():
            project(mt_ref[...], pm_ref)

    cl = lambda i: jnp.minimum(i, n - 1)
    full = lambda a: pl.BlockSpec(a.shape, lambda i: (0,) * a.ndim)
    return pl.pallas_call(
        body, name="fwd_proj_tail", grid=(n + 1,),
        in_specs=[pl.BlockSpec((tm, D_MODEL), lambda i: (cl(i), 0)), full(meta), full(norm_g), full(w_in_p)],
        out_specs=[pl.BlockSpec((tm, IN_TAIL), lambda i: (cl(i), 0)),
                   pl.BlockSpec((N_META, IN_TAIL), lambda i: (0, 0))],
        out_shape=[jax.ShapeDtypeStruct((nb * s, IN_TAIL), F32), jax.ShapeDtypeStruct((N_META, IN_TAIL), F32)],
        compiler_params=_cparams("arbitrary"),
    )(x2d, meta, norm_g, w_in_p)


def _attn_fwd(q, k, v, km, vm, w_in_shard, nb, s, tq):
    nq = s // tq
    n_steps = nb * HEADS
    gat = _StagedGather(W_IN_PIECES)
    assert n_steps >= 3

    def body(q_ref, k_ref, v_ref, km_ref, vm_ref, ws_ref, o_ref, lse_ref, w_ref, s_scr, p_scr,
             src_scr, land_scr, io_sems, *gat_scr):
        step = pl.program_id(0) * HEADS + pl.program_id(1)
        load = pltpu.make_async_copy(ws_ref, src_scr, io_sems.at[0])
        store = pltpu.make_async_copy(land_scr, w_ref, io_sems.at[1])

        @pl.when(step == 0)
        def _():
            load.start()
            land_scr[HEAD_ROWS:IN_HEAD, :] = jnp.zeros((IN_HEAD - HEAD_ROWS, D_MODEL), BF16)
            load.wait()
            gat.run(0, src_scr, land_scr, gat_scr)

        @pl.when(step == n_steps - 2)
        def _():
            gat.run(1, src_scr, land_scr, gat_scr)

        @pl.when(step == n_steps - 1)
        def _():
            gat.run(2, src_scr, land_scr, gat_scr)
            store.start()

        row = lax.broadcasted_iota(jnp.int32, (tq, tq), 0)
        col = lax.broadcasted_iota(jnp.int32, (tq, tq), 1)
        def scores(i):
            slot = i % 2
            qi = q_ref[0, 0, i * tq:(i + 1) * tq, :]
            sm = _dot_nt(qi, km_ref[0, 0])
            m128 = None
            for j in range(i + 1):
                sc = _dot_nt(qi, k_ref[0, 0, j * tq:(j + 1) * tq, :])
                if j == i:
                    sc = jnp.where(col <= row, sc, NEG_INF)
                s_scr[slot, :, j * tq:(j + 1) * tq] = sc
                mx = sc[:, 0:128]
                for c0 in range(128, tq, 128):
                    mx = jnp.maximum(mx, sc[:, c0:c0 + 128])
                m128 = mx if m128 is None else jnp.maximum(m128, mx)
            return sm, jnp.maximum(jnp.max(m128, axis=1, keepdims=True), jnp.max(sm, axis=1, keepdims=True))

        def weighted_sum(i, pm, l):
            n = (i + 1) * tq
            acc = _dot(p_scr[i % 2, :, 0:n], v_ref[0, 0, 0:n, :]) + _dot(pm.astype(BF16), vm_ref[0, 0])
            o_ref[0, 0, i * tq:(i + 1) * tq, :] = acc / l

        nxt, pending = scores(0), None
        for i in range(nq):
            slot = i % 2
            sm, m = nxt
            if i + 1 < nq:
                nxt = scores(i + 1)
            pm = jnp.exp(sm - m)
            l128 = None
            for j in range(i + 1):
                p = jnp.exp(s_scr[slot, :, j * tq:(j + 1) * tq] - m)
                p_scr[slot, :, j * tq:(j + 1) * tq] = p.astype(BF16)
                ps = p[:, 0:128]
                for c0 in range(128, tq, 128):
                    ps = ps + p[:, c0:c0 + 128]
                l128 = ps if l128 is None else l128 + ps
            l = jnp.sum(l128, axis=1, keepdims=True) + jnp.sum(pm, axis=1, keepdims=True)
            lse_ref[0, 0, :, i * tq:(i + 1) * tq] = _row_of(m + jnp.log(l), tq)
            if pending is not None:
                weighted_sum(*pending)
            pending = (i, pm, l)
        weighted_sum(*pending)

        @pl.when(step == n_steps - 1)
        def _():
            store.wait()

    hblk = lambda w: pl.BlockSpec((1, 1, s, w), lambda b, h: (b, h, 0, 0))
    mblk = lambda w: pl.BlockSpec((1, 1, N_META, w), lambda b, h: (0, h, 0, 0))
    whole = pl.BlockSpec(memory_space=pl.ANY)
    return pl.pallas_call(
        body, name="attn_fwd", grid=(nb, HEADS),
        in_specs=[hblk(QK_PAD), hblk(QK_PAD), hblk(VDIM), mblk(QK_PAD), mblk(VDIM), whole],
        out_specs=[hblk(VDIM), pl.BlockSpec((1, 1, 1, s), lambda b, h: (b, h, 0, 0)), whole],
        out_shape=[jax.ShapeDtypeStruct((nb, HEADS, s, VDIM), F32),
                   jax.ShapeDtypeStruct((nb, HEADS, 1, s), F32),
                   jax.ShapeDtypeStruct((IN_PAD, D_MODEL), BF16)],
        scratch_shapes=[pltpu.VMEM((2, tq, s), F32), pltpu.VMEM((2, tq, s), BF16),
                        pltpu.VMEM(w_in_shard.shape, BF16), pltpu.VMEM((IN_PAD, D_MODEL), BF16),
                        pltpu.SemaphoreType.DMA((2,))] + gat.scratch(),
        compiler_params=_cparams("arbitrary", "arbitrary"),
    )(q, k, v, km, vm, w_in_shard)


def _shift_rows(a, prev, n_rows):
    rid = lax.broadcasted_iota(jnp.int32, a.shape, 0)
    a1 = jnp.where(rid == 0, prev[7:8, :], pltpu.roll(a, 1, 0))
    a2 = jnp.where(rid == 0, prev[6:7, :], jnp.where(rid == 1, prev[7:8, :], pltpu.roll(a, 2, 0)))
    return a1, a2


def _attn_gate(o, za, ga_h):
    on, r = _rms(o, ga_h)
    return on * (za * _sigmoid(za)), on, r


def _out_fwd_bwd(x2d, tgt2d, o, p, pm, conv_w, ga, gc, gmat, w_out, gf, nb, s, tm):
    nt = s // tm
    r = nb * s
    prev_idx = lambda i: jnp.maximum(i * (tm // 8) - 1, 0)

    def body(x_ref, t_ref, o_ref, za_ref, cb_ref, cc_ref, ch_ref, zc_ref, ccp_ref, chp_ref, mc_ref, mh_ref,
             cw_ref, ga_ref, gc_ref, gm_ref, w_ref, gf_ref,
             dh_ref, dy_ref, dw_ref, dgf_ref, loss_ref):
        i = pl.program_id(0)

        @pl.when(i == 0)
        def _():
            dw_ref[...] = jnp.zeros_like(dw_ref)
            dgf_ref[...] = jnp.zeros_like(dgf_ref)
            loss_ref[...] = jnp.zeros_like(loss_ref)

        ya = []
        for h in range(HEADS):
            y, _, _ = _attn_gate(o_ref[0, h], za_ref[:, VDIM * h:VDIM * (h + 1)],
                                 ga_ref[:, VDIM * h:VDIM * (h + 1)])
            ya.append(y)
        cc = cc_ref[...] * ch_ref[...]
        prev = jnp.where(i % nt == 0, mc_ref[8:16, :] * mh_ref[8:16, :], ccp_ref[...] * chp_ref[...])
        cc1, cc2 = _shift_rows(cc, prev, tm)
        yc = cb_ref[...] * (cw_ref[0:1, :] * cc2 + cw_ref[1:2, :] * cc1 + cw_ref[2:3, :] * cc)
        rg = lax.rsqrt(_group_mean(yc * yc, gm_ref[...]) + EPS)
        zc = zc_ref[...]
        yconv = yc * rg * gc_ref[...] * (zc * _sigmoid(zc))
        ycat = jnp.concatenate(ya + [yconv], axis=1).astype(BF16)
        h2 = x_ref[...] + _dot(ycat, w_ref[...])
        gfv = gf_ref[...]
        y, r2 = _rms(h2, gfv)
        e = y - t_ref[...]
        loss_ref[...] += 0.5 * jnp.sum(e * e) / D_MODEL
        dyv = e * (1.0 / D_MODEL)
        dh2, dgf = _rms_bwd(dyv, h2, r2, gfv)
        dgf_ref[...] += jnp.sum(dgf, axis=0, keepdims=True)
        dh_ref[...] = dh2
        dhb = dh2.astype(BF16)
        dy_ref[...] = _dot_nt(dhb, w_ref[...])
        dw_ref[...] += _dot_tn(ycat, dhb)

    row = lambda w, j: pl.BlockSpec((tm, w), lambda i: (i, j))
    pblk = lambda j: pl.BlockSpec((tm, 512), lambda i: (i, j))
    pprev = lambda j: pl.BlockSpec((8, 512), lambda i: (prev_idx(i), j))
    mblk = lambda j: pl.BlockSpec((N_META, 512), lambda i: (0, j))
    full = lambda a: pl.BlockSpec(a.shape, lambda i: (0,) * a.ndim)
    return pl.pallas_call(
        body, name="out_fwd_bwd", grid=(nb * nt,),
        in_specs=[row(D_MODEL, 0), row(D_MODEL, 0),
                  pl.BlockSpec((1, HEADS, tm, VDIM), lambda i: (i // nt, 0, i % nt, 0)),
                  pblk(BLK_ZA), pblk(BLK_CB), pblk(BLK_CC), pblk(BLK_CH), pblk(BLK_ZC),
                  pprev(BLK_CC), pprev(BLK_CH), mblk(BLK_CC), mblk(BLK_CH),
                  full(conv_w), full(ga), full(gc), full(gmat), full(w_out), full(gf)],
        out_specs=[row(D_MODEL, 0), row(D_MODEL, 0),
                   pl.BlockSpec((D_MODEL, D_MODEL), lambda i: (0, 0)),
                   pl.BlockSpec((1, D_MODEL), lambda i: (0, 0)),
                   pl.BlockSpec((1, 128), lambda i: (0, 0))],
        out_shape=[jax.ShapeDtypeStruct((r, D_MODEL), F32), jax.ShapeDtypeStruct((r, D_MODEL), F32),
                   jax.ShapeDtypeStruct((D_MODEL, D_MODEL), F32), jax.ShapeDtypeStruct((1, D_MODEL), F32),
                   jax.ShapeDtypeStruct((1, 128), F32)],
        compiler_params=_cparams("arbitrary"),
    )(x2d, tgt2d, o, p, p, p, p, p, p, p, pm, pm, conv_w, ga, gc, gmat, w_out, gf)


def _gate_bwd(dycat, o, p, pm, conv_w, ga, gc, gmat, nb, s, tm):
    nt = s // tm
    r = nb * s
    ext = tm + 8
    prev_idx = lambda i: jnp.maximum(i * (tm // 8) - 1, 0)
    next_idx = lambda i: jnp.minimum((i + 1) * (tm // 8), r // 8 - 1)

    def body(dya_ref, dyc_ref, dycn_ref, o_ref, za_ref, cb_ref, cbn_ref, cc_ref, ccp_ref, ccn_ref,
             ch_ref, chp_ref, chn_ref, zc_ref, zcn_ref, mc_ref, mh_ref, cw_ref, ga_ref, gc_ref, gm_ref,
             dpb_ref, do_ref, dl_ref, dccm_ref, dga_ref, dgc_ref, dcw_ref):
        i = pl.program_id(0)

        @pl.when(i == 0)
        def _():
            dga_ref[...] = jnp.zeros_like(dga_ref)
            dgc_ref[...] = jnp.zeros_like(dgc_ref)
            dcw_ref[...] = jnp.zeros_like(dcw_ref)

        dga = []
        for h in range(HEADS):
            hs = slice(VDIM * h, VDIM * (h + 1))
            oh, za, gah, dya = o_ref[0, h], za_ref[:, hs], ga_ref[:, hs], dya_ref[:, hs]
            sg = _sigmoid(za)
            on, ro = _rms(oh, gah)
            don = dya * (za * sg)
            dpb_ref[:, hs] = (dya * on * (sg * (1.0 + za * (1.0 - sg)))).astype(BF16)
            do, dg = _rms_bwd(don, oh, ro, gah)
            dga.append(jnp.sum(dg, axis=0, keepdims=True))
            dob = do.astype(BF16)
            do_ref[0, h] = dob
            dl_ref[0, h] = _row_of(jnp.sum(dob.astype(F32) * oh, axis=1, keepdims=True), tm)
        dga_ref[...] += jnp.concatenate(dga, axis=1)

        cat = lambda a, b: jnp.concatenate([a[...], b[...]], axis=0)
        cch = cat(cc_ref, ccn_ref)
        chh = cat(ch_ref, chn_ref)
        cb = cat(cb_ref, cbn_ref)
        zc = cat(zc_ref, zcn_ref)
        dy = cat(dyc_ref, dycn_ref)
        first = i % nt == 0
        last = i % nt == nt - 1
        cc = cch * chh
        prev = jnp.where(first, mc_ref[8:16, :] * mh_ref[8:16, :], ccp_ref[...] * chp_ref[...])
        cc1, cc2 = _shift_rows(cc, prev, ext)
        w0, w1, w2 = cw_ref[0:1, :], cw_ref[1:2, :], cw_ref[2:3, :]
        dw = w0 * cc2 + w1 * cc1 + w2 * cc
        yc = cb * dw
        rg = lax.rsqrt(_group_mean(yc * yc, gm_ref[...]) + EPS)
        ych = yc * rg
        gcv = gc_ref[...]
        sg = _sigmoid(zc)
        dycn = dy * (zc * sg)
        dzc = dy * (ych * gcv) * (sg * (1.0 + zc * (1.0 - sg)))
        dgc_ref[...] += jnp.sum((dycn * ych)[:tm], axis=0, keepdims=True)
        dycg = dycn * gcv
        dyc = rg * (dycg - ych * _group_mean(dycg * ych, gm_ref[...]))
        rid = lax.broadcasted_iota(jnp.int32, (ext, CONV_W), 0)
        ddw = jnp.where(jnp.logical_and(last, rid >= tm), 0.0, dyc * cb)
        dcb = dyc * dw
        dcc = w2 * ddw + w1 * pltpu.roll(ddw, ext - 1, 0) + w0 * pltpu.roll(ddw, ext - 2, 0)
        dpb_ref[:, 512:1024] = dcb[:tm].astype(BF16)
        dpb_ref[:, 1024:1536] = (dcc * chh)[:tm].astype(BF16)
        dpb_ref[:, 1536:2048] = (dcc * cch)[:tm].astype(BF16)
        dpb_ref[:, 2048:2560] = dzc[:tm].astype(BF16)
        rs = lambda a: jnp.sum(a[:tm], axis=0, keepdims=True)
        dcw_ref[0:1, :] += rs(ddw * cc2)
        dcw_ref[1:2, :] += rs(ddw * cc1)
        dcw_ref[2:3, :] += rs(ddw * cc)

        @pl.when(first)
        def _():
            d0, d1 = ddw[0:1, :], ddw[1:2, :]
            r8 = lax.broadcasted_iota(jnp.int32, (8, CONV_W), 0)
            dccm_ref[0] = jnp.where(r8 == 7, w1 * d0 + w0 * d1, jnp.where(r8 == 6, w0 * d0, 0.0))

    row = lambda j: pl.BlockSpec((tm, 512), lambda i: (i, j))
    prv = lambda j: pl.BlockSpec((8, 512), lambda i: (prev_idx(i), j))
    nxt = lambda j: pl.BlockSpec((8, 512), lambda i: (next_idx(i), j))
    mblk = lambda j: pl.BlockSpec((N_META, 512), lambda i: (0, j))
    full = lambda a: pl.BlockSpec(a.shape, lambda i: (0,) * a.ndim)
    hb = lambda w: pl.BlockSpec((1, HEADS, tm, w), lambda i: (i // nt, 0, i % nt, 0))
    acc = lambda rr: pl.BlockSpec((rr, 512), lambda i: (0, 0))
    return pl.pallas_call(
        body, name="gate_bwd", grid=(nb * nt,),
        in_specs=[row(0), row(1), nxt(1), hb(VDIM),
                  row(BLK_ZA), row(BLK_CB), nxt(BLK_CB), row(BLK_CC), prv(BLK_CC), nxt(BLK_CC),
                  row(BLK_CH), prv(BLK_CH), nxt(BLK_CH), row(BLK_ZC), nxt(BLK_ZC),
                  mblk(BLK_CC), mblk(BLK_CH), full(conv_w), full(ga), full(gc), full(gmat)],
        out_specs=[pl.BlockSpec((tm, 2560), lambda i: (i, 0)), hb(VDIM),
                   pl.BlockSpec((1, HEADS, 1, tm), lambda i: (i // nt, 0, 0, i % nt)),
                   pl.BlockSpec((1, 8, 512), lambda i: (i // nt, 0, 0)),
                   acc(1), acc(1), acc(8)],
        out_shape=[jax.ShapeDtypeStruct((r, 2560), BF16), jax.ShapeDtypeStruct((nb, HEADS, s, VDIM), BF16),
                   jax.ShapeDtypeStruct((nb, HEADS, 1, s), F32), jax.ShapeDtypeStruct((nb, 8, 512), F32),
                   jax.ShapeDtypeStruct((1, 512), F32), jax.ShapeDtypeStruct((1, 512), F32),
                   jax.ShapeDtypeStruct((8, 512), F32)],
        compiler_params=_cparams("arbitrary"),
    )(dycat, dycat, dycat, o, p, p, p, p, p, p, p, p, p, p, p, pm, pm, conv_w, ga, gc, gmat)


class _StagedReduce:
    LOC, PRE_S, PRE_R, ICI_S, ICI_R, POST_S, POST_R, OUT, N_SEM = 0, 1, 2, 3, 6, 9, 10, 11, 12

    def __init__(self, shard_shape):
        self.half = (shard_shape[0] // 2, shard_shape[1])

    def scratch(self):
        h = self.half
        return [pltpu.VMEM((4,) + h, F32), pltpu.VMEM((4,) + h, F32), pltpu.VMEM((4,) + h, BF16),
                pltpu.VMEM((3,) + h, BF16), pltpu.VMEM(h, F32), pltpu.SemaphoreType.DMA((self.N_SEM,))]

    def run(self, stage, pin, gout, scr):
        own, sib, wire, rbuf, fin, sems = scr
        r2 = self.half[0]
        x, y, c = lax.axis_index("x"), lax.axis_index("y"), lax.axis_index("c")
        mine = 2 * x + y
        sibling = (x, y, 1 - c)
        chips = [(1 - x, y), (x, 1 - y), (1 - x, 1 - y)]
        rows = lambda half: pl.ds(pl.multiple_of(half * r2, r2), r2)
        mesh = pl.DeviceIdType.MESH

        loc = pltpu.make_async_copy(pin.at[:, rows(c), :], own, sems.at[self.LOC])
        pre = pltpu.make_async_remote_copy(
            src_ref=pin.at[:, rows(1 - c), :], dst_ref=sib, send_sem=sems.at[self.PRE_S],
            recv_sem=sems.at[self.PRE_R], device_id=sibling, device_id_type=mesh)

        def ici(j):
            px, py = chips[j]
            return pltpu.make_async_remote_copy(
                src_ref=wire.at[2 * px + py], dst_ref=rbuf.at[j], send_sem=sems.at[self.ICI_S + j],
                recv_sem=sems.at[self.ICI_R + j], device_id=(px, py, c), device_id_type=mesh)

        def post(half):
            return pltpu.make_async_remote_copy(
                src_ref=fin, dst_ref=gout.at[rows(half), :], send_sem=sems.at[self.POST_S],
                recv_sem=sems.at[self.POST_R], device_id=sibling, device_id_type=mesh)

        keep = pltpu.make_async_copy(fin, gout.at[rows(c), :], sems.at[self.OUT])
        if stage == 0:
            loc.start()
            pre.start()
        elif stage == 1:
            loc.wait()
            pre.wait_recv()
            for blk in range(4):
                tot = own[blk] + sib[blk]
                own[blk] = tot
                wire[blk] = tot.astype(BF16)
            for j in range(3):
                ici(j).start()
        elif stage == 2:
            for j in range(3):
                ici(j).wait_recv()
            tot = own[mine]
            for j in range(3):
                tot = tot + rbuf[j].astype(F32)
            fin[...] = tot
            post(c).start()
            keep.start()
        else:
            post(1 - c).wait_recv()
            pre.wait_send()
            for j in range(3):
                ici(j).wait_send()
            post(c).wait_send()
            keep.wait()


def _attn_bwd(q, k, v, do, lse, delta, km, vm, early, nb, s, t):
    n = s // t
    ne = len(early)
    reds = [_StagedReduce(a.shape[1:]) for a in early]
    n_steps = HEADS * nb
    assert n_steps >= 4

    def body(q_ref, k_ref, v_ref, do_ref, lse_ref, dl_ref, km_ref, vm_ref, *rest):
        pin_refs, rest = rest[:ne], rest[ne:]
        dq_ref, dk_ref, dv_ref, dkm_ref, dvm_ref = rest[:5]
        gout_refs, (p_scr, ds_scr, dq_acc), red_scr = rest[5:5 + ne], rest[5 + ne:8 + ne], rest[8 + ne:]
        b = pl.program_id(1)
        step = pl.program_id(0) * nb + b
        for stage, at in enumerate((0, 1, n_steps - 2, n_steps - 1)):
            @pl.when(step == at)
            def _(stage=stage):
                for a, red in enumerate(reds):
                    red.run(stage, pin_refs[a], gout_refs[a], red_scr[6 * a:6 * a + 6])

        @pl.when(b == 0)
        def _():
            dkm_ref[...] = jnp.zeros_like(dkm_ref)
            dvm_ref[...] = jnp.zeros_like(dvm_ref)

        kr = lax.broadcasted_iota(jnp.int32, (t, t), 0)
        qc = lax.broadcasted_iota(jnp.int32, (t, t), 1)
        km_v, vm_v = km_ref[0, 0], vm_ref[0, 0]
        ptm = jnp.exp(_dot_nt(km_v, q_ref[0, 0]) - lse_ref[0, 0])
        dstm = (ptm * (_dot_nt(vm_v, do_ref[0, 0]) - dl_ref[0, 0])).astype(BF16)
        dkm_ref[0] += _dot(dstm, q_ref[0, 0])
        dvm_ref[0] += _dot(ptm.astype(BF16), do_ref[0, 0])
        dq_acc[...] = _dot_tn(dstm, km_v)
        def tiles(j):
            slot = j % 2
            kj = k_ref[0, 0, j * t:(j + 1) * t, :]
            vj = v_ref[0, 0, j * t:(j + 1) * t, :]
            def products(i):
                cs = slice(i * t, (i + 1) * t)
                return _dot_nt(kj, q_ref[0, 0, cs, :]), _dot_nt(vj, do_ref[0, 0, cs, :])

            nxt, pending = products(j), None
            for i in range(j, n):
                cs = slice(i * t, (i + 1) * t)
                st, dpt = nxt
                if i + 1 < n:
                    nxt = products(i + 1)
                if i == j:
                    st = jnp.where(kr <= qc, st, NEG_INF)
                pt = jnp.exp(st - lse_ref[0, 0, :, cs])
                dst = (pt * (dpt - dl_ref[0, 0, :, cs])).astype(BF16)
                p_scr[slot, :, cs] = pt.astype(BF16)
                ds_scr[slot, :, cs] = dst
                if pending is not None:
                    dq_acc[pending[0], :] += _dot_tn(pending[1], kj)
                pending = (cs, dst)
            dq_acc[pending[0], :] += _dot_tn(pending[1], kj)

        for j in range(n):
            slot = j % 2
            tiles(j)
            dv_ref[0, 0, j * t:(j + 1) * t, :] = _dot(p_scr[slot, :, j * t:s], do_ref[0, 0, j * t:s, :]).astype(BF16)
            dk_ref[0, 0, j * t:(j + 1) * t, :] = _dot(ds_scr[slot, :, j * t:s], q_ref[0, 0, j * t:s, :]).astype(BF16)
        dq_ref[0, 0] = dq_acc[...].astype(BF16)

    big = lambda w: pl.BlockSpec((1, 1, s, w), lambda h, b: (b, h, 0, 0))
    rowv = pl.BlockSpec((1, 1, 1, s), lambda h, b: (b, h, 0, 0))
    mk = lambda w: pl.BlockSpec((1, 1, N_META, w), lambda h, b: (0, h, 0, 0))
    mo = lambda w: pl.BlockSpec((1, N_META, w), lambda h, b: (h, 0, 0))
    return pl.pallas_call(
        body, name="attn_bwd", grid=(HEADS, nb),
        in_specs=[big(QK_PAD), big(QK_PAD), big(VDIM), big(VDIM), rowv, rowv, mk(QK_PAD), mk(VDIM)]
        + [pl.BlockSpec(memory_space=pl.ANY)] * ne,
        out_specs=[big(QK_PAD), big(QK_PAD), big(VDIM), mo(QK_PAD), mo(VDIM)]
        + [pl.BlockSpec(memory_space=pl.ANY)] * ne,
        out_shape=[jax.ShapeDtypeStruct((nb, HEADS, s, QK_PAD), BF16),
                   jax.ShapeDtypeStruct((nb, HEADS, s, QK_PAD), BF16),
                   jax.ShapeDtypeStruct((nb, HEADS, s, VDIM), BF16),
                   jax.ShapeDtypeStruct((HEADS, N_META, QK_PAD), F32),
                   jax.ShapeDtypeStruct((HEADS, N_META, VDIM), F32)]
        + [jax.ShapeDtypeStruct(a.shape[1:], F32) for a in early],
        scratch_shapes=[pltpu.VMEM((2, t, s), BF16), pltpu.VMEM((2, t, s), BF16), pltpu.VMEM((s, QK_PAD), F32)]
        + [sc for red in reds for sc in red.scratch()],
        compiler_params=_cparams("arbitrary", "arbitrary"),
    )(q, k, v, do, lse, delta, km, vm, *early)


def _up_bwd(dq, dk, dv, dkm, dvm, p, pm, tabs, tabs_m, wq_p, wkv_p, gq, gkv, nb, s, tm):
    nt = s // tm
    n = nb * nt
    c_t, sa_t, sb_t = tabs
    cm_t, sam_t, sbm_t = tabs_m

    def kv_path(dkh, dvh, pa, c, sa, sb, wkv, gkvv):
        dkpe = dkh[0][:, NOPE:]
        for h in range(1, HEADS):
            dkpe = dkpe + dkh[h][:, NOPE:]
        dkr = _rope_bwd(dkpe, c, sa, sb)
        dkv = jnp.concatenate([d[:, :NOPE] for d in dkh] + list(dvh), axis=1).astype(BF16)
        ckv = pa[:, Q_RANK:Q_RANK + KV_RANK]
        kvn, rkv = _rms(ckv, gkvv)
        dckv, dg = _rms_bwd(_dot(dkv, wkv), ckv, rkv, gkvv)
        return dckv, dkr, kvn.astype(BF16), dkv, jnp.sum(dg, axis=0, keepdims=True)

    def body(dq_ref, dk_ref, dv_ref, pa_ref, c_ref, sa_ref, sb_ref,
             dkm_ref, dvm_ref, pam_ref, cm_ref, sam_ref, sbm_ref,
             wq_ref, wkv_ref, gq_ref, gkv_ref,
             dpa_ref, dpam_ref, pq_ref, pkv_ref, dgq_ref, dgkv_ref, dwq_ref, dwkv_ref):
        i = pl.program_id(0)

        @pl.when(i == 0)
        def _():
            dwq_ref[...] = jnp.zeros_like(dwq_ref)
            dwkv_ref[...] = jnp.zeros_like(dwkv_ref)
            dgq_ref[...] = jnp.zeros_like(dgq_ref)
            dgkv_ref[...] = jnp.zeros_like(dgkv_ref)

        @pl.when(i < n)
        def _():
            c, sa, sb = c_ref[...], sa_ref[...], sb_ref[...]
            pa = pa_ref[...]
            parts = []
            for h in range(HEADS):
                dqh = dq_ref[0, h].astype(F32) * ATTN_SCALE
                parts += [dqh[:, :NOPE], _rope_bwd(dqh[:, NOPE:], c, sa, sb)]
            dql = jnp.concatenate(parts, axis=1).astype(BF16)
            cq = pa[:, 0:Q_RANK]
            gqv = gq_ref[...]
            qn, rq = _rms(cq, gqv)
            dwq_ref[...] += _dot_tn(dql, qn.astype(BF16))
            dcq, dg = _rms_bwd(_dot(dql, wq_ref[...]), cq, rq, gqv)
            dgq_ref[...] += jnp.sum(dg, axis=0, keepdims=True)
            dckv, dkr, kvn, dkv, dgk = kv_path([dk_ref[0, h].astype(F32) for h in range(HEADS)],
                                               [dv_ref[0, h].astype(F32) for h in range(HEADS)],
                                               pa, c, sa, sb, wkv_ref[...], gkv_ref[...])
            dwkv_ref[...] += _dot_tn(dkv, kvn)
            dgkv_ref[...] += dgk
            dpa_ref[...] = jnp.concatenate([dcq, dckv, dkr], axis=1).astype(BF16)

        @pl.when(i == n)
        def _():
            dckv, dkr, kvn, dkv, dgk = kv_path([dkm_ref[h] for h in range(HEADS)],
                                               [dvm_ref[h] for h in range(HEADS)],
                                               pam_ref[...], cm_ref[...], sam_ref[...], sbm_ref[...],
                                               wkv_ref[...], gkv_ref[...])
            dwkv_ref[...] += _dot_tn(dkv, kvn)
            dgkv_ref[...] += dgk
            dpam_ref[...] = jnp.concatenate([jnp.zeros((N_META, Q_RANK), F32), dckv, dkr], axis=1)
            for h in range(HEADS):
                pq_ref[h] = dwq_ref[QK_PAD * h:QK_PAD * h + NOPE + ROPE, :]
                pkv_ref[h, 0:NOPE, :] = dwkv_ref[NOPE * h:NOPE * (h + 1), :]
                pkv_ref[h, NOPE:NOPE + VDIM, :] = dwkv_ref[512 + VDIM * h:512 + VDIM * (h + 1), :]

    cl = lambda i: jnp.minimum(i, n - 1)
    hb = lambda w: pl.BlockSpec((1, HEADS, tm, w), lambda i: (cl(i) // nt, 0, cl(i) % nt, 0))
    tab = pl.BlockSpec((tm, 128), lambda i: (cl(i) % nt, 0))
    full = lambda a: pl.BlockSpec(a.shape, lambda i: (0,) * a.ndim)
    const = lambda shape: pl.BlockSpec(shape, lambda i: (0,) * len(shape))
    return pl.pallas_call(
        body, name="up_bwd", grid=(n + 1,),
        in_specs=[hb(QK_PAD), hb(QK_PAD), hb(VDIM), pl.BlockSpec((tm, 512), lambda i: (cl(i), 0)), tab, tab, tab,
                  full(dkm), full(dvm), pl.BlockSpec((N_META, 512), lambda i: (0, 0)),
                  full(cm_t), full(sam_t), full(sbm_t), full(wq_p), full(wkv_p), full(gq), full(gkv)],
        out_specs=[pl.BlockSpec((tm, 512), lambda i: (cl(i), 0)), const((N_META, 512)),
                   const((HEADS, NOPE + ROPE, Q_RANK)), const((HEADS, NOPE + VDIM, KV_RANK)),
                   const((1, Q_RANK)), const((1, KV_RANK))],
        out_shape=[jax.ShapeDtypeStruct((nb * s, 512), BF16), jax.ShapeDtypeStruct((N_META, 512), F32),
                   jax.ShapeDtypeStruct((HEADS, NOPE + ROPE, Q_RANK), F32),
                   jax.ShapeDtypeStruct((HEADS, NOPE + VDIM, KV_RANK), F32),
                   jax.ShapeDtypeStruct((1, Q_RANK), F32), jax.ShapeDtypeStruct((1, KV_RANK), F32)],
        scratch_shapes=[pltpu.VMEM((HEADS * QK_PAD, Q_RANK), F32), pltpu.VMEM((1024, KV_RANK), F32)],
        compiler_params=_cparams("arbitrary"),
    )(dq, dk, dv, p, c_t, sa_t, sb_t, dkm, dvm, pm, cm_t, sam_t, sbm_t, wq_p, wkv_p, gq, gkv)


def _in_bwd(x2d, dh2, dpa, dpb, meta, dpam, dccm, pm, w_in_p, norm_g, nb, s, tm):
    nt = s // tm
    n = nb * nt

    def body(x_ref, dh_ref, dpa_ref, dpb_ref, mt_ref, dpam_ref, dccm_ref, mc_ref, mh_ref, w_ref, g_ref,
             gx_ref, gm_ref, dw_hbm, dg_ref, acc_ref, sems):
        i = pl.program_id(0)

        @pl.when(i == 0)
        def _():
            acc_ref[...] = jnp.zeros_like(acc_ref)
            dg_ref[...] = jnp.zeros_like(dg_ref)

        def rows(x, dp, dres):
            g = g_ref[...]
            dpb16 = dp.astype(BF16)
            du = _dot(dpb16, w_ref[...])
            u, r1 = _rms(x, g)
            acc_ref[...] += _dot_tn(dpb16, u.astype(BF16))
            dx, dg = _rms_bwd(du, x, r1, g)
            dg_ref[...] += jnp.sum(dg, axis=0, keepdims=True)
            return dx if dres is None else dx + dres

        @pl.when(i < n)
        def _():
            dp = jnp.concatenate([dpa_ref[...], dpb_ref[...]], axis=1)
            gx_ref[...] = rows(x_ref[...], dp, dh_ref[...])

        @pl.when(i == n)
        def _():
            dcc = dccm_ref[0]
            for b in range(1, nb):
                dcc = dcc + dccm_ref[b]
            z8 = jnp.zeros((8, CONV_W), F32)
            dc = jnp.concatenate([z8, dcc * mh_ref[8:16, :]], axis=0)
            dh = jnp.concatenate([z8, dcc * mc_ref[8:16, :]], axis=0)
            z = jnp.zeros((N_META, CONV_W), F32)
            dp = jnp.concatenate([dpam_ref[...], z, z, dc, dh, z], axis=1)
            gm_ref[...] = rows(mt_ref[...], dp, None)
            per = IN_DIM // 4
            cps = [pltpu.make_async_copy(acc_ref.at[0:448], dw_hbm.at[0, 0:448], sems.at[0]),
                   pltpu.make_async_copy(acc_ref.at[512:per + 64], dw_hbm.at[0, 448:per], sems.at[1])]
            for qq in range(1, 4):
                cps.append(pltpu.make_async_copy(acc_ref.at[per * qq + 64:per * (qq + 1) + 64], dw_hbm.at[qq],
                                                 sems.at[qq + 1]))
            for cp in cps:
                cp.start()
            for cp in cps:
                cp.wait()

    cl = lambda i: jnp.minimum(i, n - 1)
    row = lambda w: pl.BlockSpec((tm, w), lambda i: (cl(i), 0))
    full = lambda a: pl.BlockSpec(a.shape, lambda i: (0,) * a.ndim)
    mblk = lambda j: pl.BlockSpec((N_META, 512), lambda i: (0, j))
    return pl.pallas_call(
        body, name="in_bwd", grid=(n + 1,),
        in_specs=[row(D_MODEL), row(D_MODEL), row(512), row(2560), full(meta), full(dpam), full(dccm),
                  mblk(BLK_CC), mblk(BLK_CH), full(w_in_p), full(norm_g)],
        out_specs=[row(D_MODEL), pl.BlockSpec((N_META, D_MODEL), lambda i: (0, 0)),
                   pl.BlockSpec(memory_space=pl.ANY), pl.BlockSpec((1, D_MODEL), lambda i: (0, 0))],
        out_shape=[jax.ShapeDtypeStruct((nb * s, D_MODEL), F32), jax.ShapeDtypeStruct((N_META, D_MODEL), F32),
                   jax.ShapeDtypeStruct((4, IN_DIM // 4, D_MODEL), F32), jax.ShapeDtypeStruct((1, D_MODEL), F32)],
        scratch_shapes=[pltpu.VMEM((IN_PAD, D_MODEL), F32), pltpu.SemaphoreType.DMA((5,))],
        compiler_params=_cparams("arbitrary"),
    )(x2d, dh2, dpa, dpb, meta, dpam, dccm, pm, pm, w_in_p, norm_g)


def _gather_weights(w_in_shard, split, pieces, out_rows, whole, zero_fills):
    ns, nw, nz = len(split), len(whole), len(zero_fills)
    flat = [(a, pc) for a in range(ns) for pc in pieces[a]]
    nk = len(flat)
    hh = HEAD_ROWS // 2

    def body(*refs):
        ins, wins, zins = refs[1:1 + ns], refs[1 + ns:1 + ns + nw], refs[1 + ns + nw:1 + ns + nw + nz]
        n_in = 1 + ns + nw + nz
        head_ref, shard16 = refs[n_in], refs[n_in + 1]
        outs, wouts = refs[n_in + 2:n_in + 2 + ns], refs[n_in + 2 + ns:n_in + 2 + ns + nw]
        scr = refs[n_in + 2 + ns + nw:]
        stage = scr[:ns]
        (send_sems, recv_sems, fwd_send, fwd_recv, loc_sems, w_send, w_recv, w_loc, z_sems,
         h_send, h_recv, h_pass) = scr[ns:]
        x, y, c = lax.axis_index("x"), lax.axis_index("y"), lax.axis_index("c")
        mine = 2 * x + y
        chips = [(1 - x, y), (x, 1 - y), (1 - x, 1 - y)]
        chip_of = [2 * px + py for px, py in chips]
        shard16[...] = refs[0][...].astype(BF16)
        for a in range(ns):
            stage[a][...] = ins[a][...].astype(BF16)

        def head_rows(half):
            return pl.ds(pl.multiple_of(half * hh, 16), hh)

        def head_copy(j):
            px, py = chips[j]
            return pltpu.make_async_remote_copy(
                src_ref=shard16.at[head_rows(c)], dst_ref=head_ref.at[head_rows(c)], send_sem=h_send.at[j],
                recv_sem=h_recv.at[0], device_id=(px, py, c), device_id_type=pl.DeviceIdType.MESH)

        def head_pass(half):
            ref = head_ref.at[head_rows(half)]
            return pltpu.make_async_remote_copy(
                src_ref=ref, dst_ref=ref, send_sem=h_pass.at[0], recv_sem=h_pass.at[1],
                device_id=(x, y, 1 - c), device_id_type=pl.DeviceIdType.MESH)

        head_ref[HEAD_ROWS:IN_HEAD, :] = jnp.zeros((IN_HEAD - HEAD_ROWS, D_MODEL), BF16)

        @pl.when(mine == 0)
        def _():
            for j in range(3):
                head_copy(j).start()
            head_ref[0:HEAD_ROWS, :] = shard16[0:HEAD_ROWS, :]

        def src(k):
            a, (s0, nr, _, _, _, _) = flat[k]
            return stage[a].at[s0:s0 + nr]

        def dst(k, q):
            a, (_, nr, per, first, rest, _) = flat[k]
            row = per * q + first + (rest - first) * jnp.minimum(q, 1)
            return outs[a].at[pl.ds(pl.multiple_of(row, 16), nr)]

        def ici(k, j, q):
            px, py = chips[j]
            return pltpu.make_async_remote_copy(
                src_ref=src(k), dst_ref=dst(k, q), send_sem=send_sems.at[k, j], recv_sem=recv_sems.at[k, j],
                device_id=(px, py, c), device_id_type=pl.DeviceIdType.MESH)

        def fwd(k, j):
            ref = dst(k, chip_of[j])
            return pltpu.make_async_remote_copy(
                src_ref=ref, dst_ref=ref, send_sem=fwd_send.at[k, j], recv_sem=fwd_recv.at[k, j],
                device_id=(x, y, 1 - c), device_id_type=pl.DeviceIdType.MESH)

        def wcopy(b, j, q):
            px, py = chips[j]
            return pltpu.make_async_remote_copy(
                src_ref=wins[b], dst_ref=wouts[b].at[q], send_sem=w_send.at[b, j], recv_sem=w_recv.at[b, j],
                device_id=(px, py, c), device_id_type=pl.DeviceIdType.MESH)

        local = [pltpu.make_async_copy(src(k), dst(k, mine), loc_sems.at[k]) for k in range(nk)]
        local += [pltpu.make_async_copy(wins[b], wouts[b].at[mine], w_loc.at[b]) for b in range(nw)]
        for z, (a, _, row0) in enumerate(zero_fills):
            local.append(pltpu.make_async_copy(zins[z], outs[a].at[row0:row0 + zins[z].shape[0]], z_sems.at[z]))
        wsends = [wcopy(b, j, mine) for b in range(nw) for j in range(3)]
        for cp in local + wsends:
            cp.start()

        for half in (0, 1):
            @pl.when(c == half)
            def _(half=half):
                my_k = [k for k in range(nk) if flat[k][1][5] == half]
                other_k = [k for k in range(nk) if flat[k][1][5] != half]
                sends = [ici(k, j, mine) for k in my_k for j in range(3)]
                for cp in sends:
                    cp.start()
                passed = []
                for k in my_k:
                    for j in range(3):
                        ici(k, j, chip_of[j]).wait_recv()
                        cp = fwd(k, j)
                        cp.start()
                        passed.append(cp)
                for k in other_k:
                    for j in range(3):
                        fwd(k, j).wait_recv()
                for cp in sends + passed:
                    cp.wait_send()

        for b in range(nw):
            for j in range(3):
                wcopy(b, j, chip_of[j]).wait_recv()
        for cp in wsends:
            cp.wait_send()
        for cp in local:
            cp.wait()

        @pl.when(mine == 0)
        def _():
            for j in range(3):
                head_copy(j).wait_send()

        @pl.when(mine != 0)
        def _():
            head_copy(0).wait_recv()
            cp = head_pass(c)
            cp.start()
            head_pass(1 - c).wait_recv()
            cp.wait_send()

    vmem = pl.BlockSpec(memory_space=pltpu.VMEM)
    dma = pltpu.SemaphoreType.DMA
    zeros = [z for _, z, _ in zero_fills]
    return pl.pallas_call(
        body, name="gather_weights",
        in_specs=[vmem] * (1 + ns + nw + nz), out_specs=[vmem] * (2 + ns + nw),
        out_shape=([jax.ShapeDtypeStruct((IN_HEAD, D_MODEL), BF16), jax.ShapeDtypeStruct(w_in_shard.shape, BF16)]
                   + [jax.ShapeDtypeStruct((out_rows[a], split[a].shape[1]), BF16) for a in range(ns)]
                   + [jax.ShapeDtypeStruct((4,) + w.shape, w.dtype) for w in whole]),
        scratch_shapes=[pltpu.VMEM(a.shape, BF16) for a in split]
        + [dma((nk, 3)), dma((nk, 3)), dma((nk, 3)), dma((nk, 3)), dma((nk,)),
           dma((nw, 3)), dma((nw, 3)), dma((nw,)), dma((nz,)), dma((3,)), dma((1,)), dma((2,))],
        compiler_params=pltpu.CompilerParams(vmem_limit_bytes=VMEM_LIMIT),
    )(w_in_shard, *split, *whole, *zeros)


def _reduce_grads(parts, small):
    n = len(parts)
    shapes = [a.shape[1:] for a in parts]
    halves = [(sh[0] // 2, sh[1]) for sh in shapes]

    def body(*refs):
        pin, sm_in = refs[:n], refs[n]
        gout, sm_out = refs[n + 1:2 * n + 1], refs[2 * n + 1]
        scr = refs[2 * n + 2:]
        own, sib, wire, rbuf = scr[:n], scr[n:2 * n], scr[2 * n:3 * n], scr[3 * n:4 * n]
        (sbuf, send_sems, recv_sems, loc_sems, pre_send, pre_recv, post_send, post_recv,
         sm_send, sm_recv) = scr[4 * n:]
        x, y, c = lax.axis_index("x"), lax.axis_index("y"), lax.axis_index("c")
        mine = 2 * x + y
        me = 4 * x + 2 * y + c
        sibling = (x, y, 1 - c)
        chips = [(1 - x, y), (x, 1 - y), (1 - x, 1 - y)]

        def rows(a, half):
            r2 = halves[a][0]
            return pl.ds(pl.multiple_of(half * r2, r2), r2)

        chip_of = [2 * px + py for px, py in chips]
        blocks = chip_of + [mine]

        def pre(a, k):
            return pltpu.make_async_remote_copy(
                src_ref=pin[a].at[blocks[k], rows(a, 1 - c), :], dst_ref=sib[a].at[blocks[k]],
                send_sem=pre_send.at[a, k], recv_sem=pre_recv.at[a, k], device_id=sibling,
                device_id_type=pl.DeviceIdType.MESH)

        def ici(a, j):
            px, py = chips[j]
            return pltpu.make_async_remote_copy(
                src_ref=wire[a].at[2 * px + py], dst_ref=rbuf[a].at[j], send_sem=send_sems.at[a, j],
                recv_sem=recv_sems.at[a, j], device_id=(px, py, c), device_id_type=pl.DeviceIdType.MESH)

        def post(a, half):
            ref = gout[a].at[rows(a, half), :]
            return pltpu.make_async_remote_copy(
                src_ref=ref, dst_ref=ref, send_sem=post_send.at[a], recv_sem=post_recv.at[a],
                device_id=sibling, device_id_type=pl.DeviceIdType.MESH)

        def small_copy(kk):
            peer = (x ^ (kk >> 2), y ^ ((kk >> 1) & 1), c ^ (kk & 1))
            return pltpu.make_async_remote_copy(
                src_ref=sm_in, dst_ref=sbuf.at[kk], send_sem=sm_send.at[kk - 1], recv_sem=sm_recv.at[kk - 1],
                device_id=peer, device_id_type=pl.DeviceIdType.MESH)

        local = [[pltpu.make_async_copy(pin[a].at[blocks[k], rows(a, c), :], own[a].at[blocks[k]], loc_sems.at[a, k])
                  for k in range(4)] for a in range(n)]
        pres = [[pre(a, k) for k in range(4)] for a in range(n)]
        smalls = [small_copy(kk) for kk in range(1, 8)]
        for a in range(n):
            for k in range(4):
                local[a][k].start()
                pres[a][k].start()
        for cp in smalls:
            cp.start()
        sbuf[0] = sm_in[...]
        sends = []
        for a in range(n):
            for k in range(4):
                local[a][k].wait()
                pres[a][k].wait_recv()
                tot = own[a][blocks[k]] + sib[a][blocks[k]]
                own[a][blocks[k]] = tot
                if k < 3:
                    wire[a][blocks[k]] = tot.astype(BF16)
                    cp = ici(a, k)
                    cp.start()
                    sends.append(cp)
        for cp in smalls:
            cp.wait_recv()
        total = sbuf[me]
        for d in range(1, 8):
            total = total + sbuf[me ^ d]
        sm_out[...] = total
        posts = []
        for a in range(n):
            for j in range(3):
                ici(a, j).wait_recv()
            fin = own[a][mine]
            for j in range(3):
                fin = fin + rbuf[a][j].astype(F32)
            gout[a][rows(a, c), :] = fin
            cp = post(a, c)
            cp.start()
            posts.append(cp)
        for a in range(n):
            post(a, 1 - c).wait_recv()
        for cp in [cp for row in pres for cp in row] + sends + smalls + posts:
            cp.wait_send()

    vmem = pl.BlockSpec(memory_space=pltpu.VMEM)
    dma = pltpu.SemaphoreType.DMA
    return pl.pallas_call(
        body, name="reduce_grads",
        in_specs=[pl.BlockSpec(memory_space=pl.ANY)] * n + [vmem], out_specs=[vmem] * (n + 1),
        out_shape=[jax.ShapeDtypeStruct(sh, F32) for sh in shapes] + [jax.ShapeDtypeStruct(small.shape, F32)],
        scratch_shapes=([pltpu.VMEM((4,) + hs, F32) for hs in halves] + [pltpu.VMEM((4,) + hs, F32) for hs in halves]
                        + [pltpu.VMEM((4,) + hs, BF16) for hs in halves]
                        + [pltpu.VMEM((3,) + hs, BF16) for hs in halves]
                        + [pltpu.VMEM((8,) + small.shape, F32), dma((n, 3)), dma((n, 3)), dma((n, 4)),
                           dma((n, 4)), dma((n, 4)), dma((n,)), dma((n,)), dma((7,)), dma((7,))]),
        compiler_params=pltpu.CompilerParams(vmem_limit_bytes=VMEM_LIMIT),
    )(*parts, small)


def _adamw_update(w_ref, g_ref, m_ref, v_ref, d_ref, nm_ref, nv_ref):
    gv = g_ref[...]
    nm = ADAM_B1 * m_ref[...] + (1.0 - ADAM_B1) * gv
    nv = ADAM_B2 * v_ref[...] + (1.0 - ADAM_B2) * (gv * gv)
    m_hat = nm / (1.0 - ADAM_B1 ** ADAM_STEP)
    v_hat = nv / (1.0 - ADAM_B2 ** ADAM_STEP)
    d_ref[...] = -ADAM_LR * (m_hat / (jnp.sqrt(v_hat) + ADAM_EPS) + ADAM_WD * w_ref[...])
    nm_ref[...] = nm
    nv_ref[...] = nv


def _adamw_small(ws, gs, ms, vs):
    k = len(ws)

    def body(*refs):
        ins, outs = refs[:4 * k], refs[4 * k:]
        for a in range(k):
            _adamw_update(ins[a], ins[k + a], ins[2 * k + a], ins[3 * k + a], outs[a], outs[k + a], outs[2 * k + a])

    out = pl.pallas_call(
        body, name="adamw_small",
        out_shape=[jax.ShapeDtypeStruct(w.shape, F32) for w in ws] * 3,
        compiler_params=pltpu.CompilerParams(vmem_limit_bytes=VMEM_LIMIT),
    )(*ws, *gs, *ms, *vs)
    return out[:k], out[k:2 * k], out[2 * k:]


def _adamw(w, g, m, v, name):
    shape = w.shape
    w2, g2, m2, v2 = (a.reshape((-1, shape[-1])) for a in (w, g, m, v))

    def body(w_ref, g_ref, m_ref, v_ref, d_ref, nm_ref, nv_ref):
        _adamw_update(w_ref, g_ref, m_ref, v_ref, d_ref, nm_ref, nv_ref)

    rows, cols = w2.shape
    nblk = cols // 256 if cols % 256 == 0 and rows >= 64 else 1
    blk = pl.BlockSpec((rows, cols // nblk), lambda j: (0, j))
    out = pl.pallas_call(
        body, name=name, grid=(nblk,), in_specs=[blk] * 4, out_specs=[blk] * 3,
        out_shape=[jax.ShapeDtypeStruct(w2.shape, F32)] * 3,
        compiler_params=_cparams("parallel"),
    )(w2, g2, m2, v2)
    return tuple(a.reshape(shape) for a in out)


def kernel(x, meta_tokens, norm_g, w_in, q_norm_g, w_q_up, kv_norm_g, w_kv_up, conv_w, attn_out_g, conv_out_g, w_out, final_norm_g, loss_target, m_meta_tokens, m_norm_g, m_w_in, m_q_norm_g, m_w_q_up, m_kv_norm_g, m_w_kv_up, m_conv_w, m_attn_out_g, m_conv_out_g, m_w_out, m_final_norm_g, v_meta_tokens, v_norm_g, v_w_in, v_q_norm_g, v_w_q_up, v_kv_norm_g, v_w_kv_up, v_conv_w, v_attn_out_g, v_conv_out_g, v_w_out, v_final_norm_g):
    nb, s, _ = x.shape
    tm = min(ROW_TILE, s)
    ta = min(ATTN_TILE, s)
    assert s % tm == 0 and s % ta == 0 and tm % 16 == 0
    r = nb * s

    tr = lambda a: jnp.transpose(a[0])
    w_head, w_in_shard, wq_p, wkv_p, g_cw, g_meta = _gather_weights(
        tr(w_in), [tr(w_q_up), tr(w_kv_up)],
        [W_Q_PIECES, W_KV_PIECES], [HEADS * QK_PAD, 1024],
        [jnp.transpose(conv_w, (1, 0, 2)), meta_tokens],
        [(0, jnp.zeros((64, Q_RANK), BF16), QK_PAD * h + NOPE + ROPE) for h in range(HEADS)])
    conv_f = jnp.transpose(g_cw[:, :, 0, :], (1, 0, 2)).reshape(3, CONV_W)
    meta_f = jnp.transpose(g_meta, (1, 0, 2)).reshape(N_META, D_MODEL)

    c_all, sa_all, sb_all = _rope_tables(N_META + s)
    tabs_m = (c_all[:N_META], sa_all[:N_META], sb_all[:N_META])
    tabs = (c_all[N_META:], sa_all[N_META:], sb_all[N_META:])
    gid = np.arange(CONV_W) // CONV_GROUP
    gmat = jnp.asarray(np.where(gid[:, None] == gid[None, :], 1.0 / CONV_GROUP, 0.0), BF16)
    ga, gc = attn_out_g, conv_out_g
    gf = final_norm_g.reshape(1, D_MODEL)

    x2d = x.reshape(r, D_MODEL)
    tgt2d = loss_target.reshape(r, D_MODEL)

    ph, q, k, v, pmh, km, vm, w_out_f = _fwd_proj(x2d, meta_f, tabs, tabs_m, norm_g, w_head, q_norm_g, wq_p,
                                                  kv_norm_g, wkv_p, w_out[0].astype(BF16), nb, s, tm)
    o, lse, w_in_p = _attn_fwd(q, k, v, km, vm, w_in_shard, nb, s, ta)
    pt, pmt = _fwd_proj_tail(x2d, meta_f, norm_g, w_in_p, nb, s, tm)
    dh2, dycat, dw_out, dgf, loss_acc = _out_fwd_bwd(x2d, tgt2d, o, pt, pmt, conv_f, ga, gc, gmat, w_out_f, gf,
                                                     nb, s, tm)
    dpb, do, delta, dccm, dga, dgc, dcw = _gate_bwd(dycat, o, pt, pmt, conv_f, ga, gc, gmat, nb, s, tm)
    p_out = dw_out.reshape(4, D_MODEL // 4, D_MODEL)
    dq, dk, dv, dkm, dvm, g_w_out = _attn_bwd(q, k, v, do, lse, delta, km, vm, [p_out], nb, s, ta)
    dpa, dpam, p_q, p_kv, dgq, dgkv = _up_bwd(dq, dk, dv, dkm, dvm, ph, pmh, tabs, tabs_m, wq_p, wkv_p,
                                              q_norm_g, kv_norm_g, nb, s, tm)
    gx, gmeta, p_in, dng = _in_bwd(x2d, dh2, dpa, dpb, meta_f, dpam, dccm, pmt, w_in_p, norm_g, nb, s, tm)

    flat =jnp.concatenate([dng.reshape(-1), dgq.reshape(-1), dgkv.reshape(-1), dga.reshape(-1), dgc.reshape(-1),
                            dgf.reshape(-1), dcw[:3].reshape(-1), gmeta.reshape(-1), loss_acc[0, 0:1]])
    n_small = flat.shape[0]
    rows_small = -(-n_small // 1024) * 8
    small = jnp.pad(flat, (0, rows_small * 128 - n_small)).reshape(rows_small, 128)
    g_w_in_t, g_w_q_t, g_w_kv_t, small_sum = _reduce_grads([p_in, p_q, p_kv], small)
    ssum = small_sum.reshape(-1)

    def take(off, n):
        return ssum[off:off + n], off + n

    off = 0
    g_norm, off = take(off, D_MODEL)
    g_qn, off = take(off, Q_RANK)
    g_kvn, off = take(off, KV_RANK)
    g_ga, off = take(off, CONV_W)
    g_gc, off = take(off, CONV_W)
    g_gf, off = take(off, D_MODEL)
    g_cw_all, off = take(off, 3 * CONV_W)
    g_meta_all, off = take(off, N_META * D_MODEL)
    loss = ssum[off]
    chip = 2 * lax.axis_index("x") + lax.axis_index("y")
    g_conv = lax.dynamic_slice(g_cw_all.reshape(3, CONV_W), (0, chip * 128), (3, 128))
    g_mt = lax.dynamic_slice(g_meta_all.reshape(N_META, D_MODEL), (0, chip * 256), (N_META, 256))

    grads = {
        "meta_tokens": g_mt, "norm_g": g_norm.reshape(1, -1), "w_in": g_w_in_t, "q_norm_g": g_qn.reshape(1, -1),
        "w_q_up": g_w_q_t, "kv_norm_g": g_kvn.reshape(1, -1), "w_kv_up": jnp.transpose(g_w_kv_t)[None],
        "conv_w": g_conv[None], "attn_out_g": g_ga.reshape(1, -1), "conv_out_g": g_gc.reshape(1, -1),
        "w_out": g_w_out[None], "final_norm_g": g_gf,
    }
    transposed = ("w_in", "w_q_up")
    weights = {
        "meta_tokens": (meta_tokens, m_meta_tokens, v_meta_tokens), "norm_g": (norm_g, m_norm_g, v_norm_g),
        "w_in": (w_in, m_w_in, v_w_in), "q_norm_g": (q_norm_g, m_q_norm_g, v_q_norm_g),
        "w_q_up": (w_q_up, m_w_q_up, v_w_q_up), "kv_norm_g": (kv_norm_g, m_kv_norm_g, v_kv_norm_g),
        "w_kv_up": (w_kv_up, m_w_kv_up, v_w_kv_up), "conv_w": (conv_w, m_conv_w, v_conv_w),
        "attn_out_g": (attn_out_g, m_attn_out_g, v_attn_out_g), "conv_out_g": (conv_out_g, m_conv_out_g, v_conv_out_g),
        "w_out": (w_out, m_w_out, v_w_out), "final_norm_g": (final_norm_g, m_final_norm_g, v_final_norm_g),
    }
    names = list(weights)
    small = [nme for nme in names if nme != "w_in"]

    def view(nme, a):
        if nme in transposed:
            return a if a.ndim == 2 else tr(a)
        if nme == "conv_w":
            return jnp.transpose(a.reshape(1, 3, -1), (1, 0, 2))
        if a.ndim == 3:
            return a[0]
        return a.reshape(1, -1) if a.ndim == 1 else a

    def unview(nme, a):
        if nme in transposed:
            return jnp.transpose(a)[None]
        if nme == "conv_w":
            return jnp.transpose(a, (1, 0, 2))
        return a.reshape(weights[nme][0].shape)

    res_small = _adamw_small(*[[view(nme, a) for nme, a in zip(small, col)] for col in (
        [weights[nme][0] for nme in small], [grads[nme] for nme in small],
        [weights[nme][1] for nme in small], [weights[nme][2] for nme in small])])
    w_, m_, v_ = weights["w_in"]
    res = _adamw(tr(w_), grads["w_in"], tr(m_), tr(v_), "adamw_w_in")
    upd = {"w_in": tuple(jnp.transpose(a)[None] for a in (grads["w_in"],) + res)}
    for j, nme in enumerate(small):
        upd[nme] = (unview(nme, view(nme, grads[nme])),) + tuple(unview(nme, r[j]) for r in res_small)
    grads = {nme: upd[nme][0] for nme in names}
    deltas, new_m, new_v = ([upd[nme][j] for nme in names] for j in (1, 2, 3))

    grad_x = gx.reshape(nb, s, D_MODEL)
    return (loss, grad_x, *[grads[nme] for nme in names], *deltas, *new_m, *new_v)
```

```python
import functools

import jax
import jax.numpy as jnp
import numpy as np
from jax import lax
from jax.experimental import pallas as pl
from jax.experimental.pallas import tpu as pltpu

F32 = jnp.float32
BF16 = jnp.bfloat16

D_MODEL = 1024
N_META = 16
HEADS = 4
NOPE = 128
ROPE = 64
VDIM = 128
QK_PAD = 256
Q_RANK = 256
KV_RANK = 128
CONV_W = 512
CONV_GROUP = 64
ROPE_THETA = 10000.0
EPS = 1e-6
ATTN_SCALE = (NOPE + ROPE) ** -0.5
IN_DIM = 3008
IN_PAD = 3072
HEAD_ROWS = Q_RANK + KV_RANK + ROPE
IN_HEAD = 512
IN_TAIL = IN_PAD - IN_HEAD
BLK_ZA, BLK_CB, BLK_CC, BLK_CH, BLK_ZC = 0, 1, 2, 3, 4
NEG_INF = -1e30

ADAM_LR = 0.001
ADAM_B1 = 0.9
ADAM_B2 = 0.999
ADAM_EPS = 1e-08
ADAM_WD = 0.01
ADAM_STEP = 10

ROW_TILE = 512
ATTN_TILE = 256
VMEM_LIMIT = 56 * 1024 * 1024

NT = (((1,), (1,)), ((), ()))
TN = (((0,), (0,)), ((), ()))


def _cparams(*sem):
    return pltpu.CompilerParams(dimension_semantics=sem, vmem_limit_bytes=VMEM_LIMIT)


def _dot(a, b):
    return jnp.dot(a, b, preferred_element_type=F32)


def _dot_nt(a, b):
    return lax.dot_general(a, b, NT, preferred_element_type=F32)


def _dot_tn(a, b):
    return lax.dot_general(a, b, TN, preferred_element_type=F32)


def _rms(x, g):
    r = lax.rsqrt(jnp.mean(x * x, axis=-1, keepdims=True) + EPS)
    return x * r * g, r


def _rms_bwd(dy, x, r, g):
    xh = x * r
    dyg = dy * g
    dx = r * (dyg - xh * jnp.mean(dyg * xh, axis=-1, keepdims=True))
    return dx, dy * xh


def _sigmoid(z):
    return 1.0 / (1.0 + jnp.exp(-z))


def _rope(b, c, sa, sb):
    return b * c + pltpu.roll(b, 96, 1) * sa + pltpu.roll(b, 32, 1) * sb


def _rope_bwd(d, c, sa, sb):
    return d * c + pltpu.roll(d * sa, 32, 1) + pltpu.roll(d * sb, 96, 1)


def _group_mean(x, gmat):
    hi = x.astype(BF16)
    lo = (x - hi.astype(F32)).astype(BF16)
    return _dot(hi, gmat) + _dot(lo, gmat)


def _row_of(col, rows):
    return jnp.transpose(jnp.broadcast_to(col, (rows, 128)))[0:1, :]


def _rope_tables(n_pos):
    half = ROPE // 2
    inv_freq = (np.float32(1.0) / (np.float32(ROPE_THETA) ** (np.arange(half, dtype=np.float32) / np.float32(half))))
    ang = np.arange(n_pos, dtype=np.float32)[:, None] * inv_freq.astype(np.float32)[None, :]
    cos, sin = np.cos(ang).astype(np.float32), np.sin(ang).astype(np.float32)
    z = np.zeros((n_pos, half), np.float32)
    c = np.concatenate([cos, cos, z, z], axis=1)
    sa = np.concatenate([-sin, z, z, z], axis=1)
    sb = np.concatenate([z, sin, z, z], axis=1)
    return jnp.asarray(c), jnp.asarray(sa), jnp.asarray(sb)


W_IN_PIECES = ((0, 384, 752, 0, 64, 0), (384, 64, 752, 384, 448, 1), (448, 304, 752, 512, 512, 1))
W_Q_PIECES = ((0, 96, 256, 0, 0, 0), (96, 96, 256, 96, 96, 1))
W_KV_PIECES = ((0, 128, 128, 0, 0, 0), (128, 128, 128, 512, 512, 1))
W_OUT_PIECES = ((0, 128, 256, 0, 0, 0), (128, 128, 256, 128, 128, 1))


class _StagedGather:
    def __init__(self, pieces):
        self.pieces = pieces

    def scratch(self):
        nk, dma = len(self.pieces), pltpu.SemaphoreType.DMA
        return [dma((nk, 3)), dma((nk, 3)), dma((nk, 3)), dma((nk, 3)), dma((nk,))]

    def run(self, stage, src_ref, out_ref, scr):
        send_sems, recv_sems, fwd_send, fwd_recv, loc_sems = scr
        pieces = self.pieces
        nk = len(pieces)
        x, y, c = lax.axis_index("x"), lax.axis_index("y"), lax.axis_index("c")
        mine = 2 * x + y
        chips = [(1 - x, y), (x, 1 - y), (1 - x, 1 - y)]
        chip_of = [2 * px + py for px, py in chips]
        mesh = pl.DeviceIdType.MESH

        def src(k):
            s0, nr = pieces[k][0], pieces[k][1]
            return src_ref.at[s0:s0 + nr]

        def dst(k, q):
            _, nr, per, first, rest, _ = pieces[k]
            row = per * q + first + (rest - first) * jnp.minimum(q, 1)
            return out_ref.at[pl.ds(pl.multiple_of(row, 16), nr)]

        def ici(k, j, q):
            px, py = chips[j]
            return pltpu.make_async_remote_copy(
                src_ref=src(k), dst_ref=dst(k, q), send_sem=send_sems.at[k, j], recv_sem=recv_sems.at[k, j],
                device_id=(px, py, c), device_id_type=mesh)

        def fwd(k, j):
            ref = dst(k, chip_of[j])
            return pltpu.make_async_remote_copy(
                src_ref=ref, dst_ref=ref, send_sem=fwd_send.at[k, j], recv_sem=fwd_recv.at[k, j],
                device_id=(x, y, 1 - c), device_id_type=mesh)

        local = [pltpu.make_async_copy(src(k), dst(k, mine), loc_sems.at[k]) for k in range(nk)]
        if stage == 0:
            for cp in local:
                cp.start()
        if stage == 2:
            for cp in local:
                cp.wait()
        for half in (0, 1):
            @pl.when(c == half)
            def _(half=half):
                my_k = [k for k in range(nk) if pieces[k][5] == half]
                other_k = [k for k in range(nk) if pieces[k][5] != half]
                for k in my_k:
                    for j in range(3):
                        if stage == 0:
                            ici(k, j, mine).start()
                        elif stage == 1:
                            ici(k, j, chip_of[j]).wait_recv()
                            fwd(k, j).start()
                        else:
                            ici(k, j, mine).wait_send()
                            fwd(k, j).wait_send()
                if stage == 2:
                    for k in other_k:
                        for j in range(3):
                            fwd(k, j).wait_recv()


def _fwd_proj(x2d, meta, tabs, tabs_m, norm_g, w_head, q_norm_g, wq_p, kv_norm_g, wkv_p, w_out_shard, nb, s, tm):
    nt = s // tm
    n = nb * nt
    n_steps = n + 1
    c_t, sa_t, sb_t = tabs
    cm_t, sam_t, sbm_t = tabs_m
    gat = _StagedGather(W_OUT_PIECES)
    assert n_steps >= 3

    def body(x_ref, c_ref, sa_ref, sb_ref, mt_ref, cm_ref, sam_ref, sbm_ref,
             g_ref, w_ref, gq_ref, wq_ref, gkv_ref, wkv_ref, wos_ref,
             p_ref, q_ref, k_ref, v_ref, pm_ref, km_ref, vm_ref, wo_ref, *gat_scr):
        i = pl.program_id(0)
        for stage, at in enumerate((0, n_steps - 2, n_steps - 1)):
            @pl.when(i == at)
            def _(stage=stage):
                gat.run(stage, wos_ref, wo_ref, gat_scr)

        def project(xv, c, sa, sb, p_out, q_out, k_out, v_out):
            u, _ = _rms(xv, g_ref[...])
            p = _dot_nt(u.astype(BF16), w_ref[...])
            p_out[...] = p
            qn, _ = _rms(p[:, 0:Q_RANK], gq_ref[...])
            q = _dot_nt(qn.astype(BF16), wq_ref[...])
            kvn, _ = _rms(p[:, Q_RANK:Q_RANK + KV_RANK], gkv_ref[...])
            kv = _dot_nt(kvn.astype(BF16), wkv_ref[...])
            kpe = _rope(p[:, 384:512], c, sa, sb)
            for h in range(HEADS):
                if q_out is not None:
                    pe = _rope(q[:, QK_PAD * h + NOPE:QK_PAD * (h + 1)], c, sa, sb)
                    qh = jnp.concatenate([q[:, QK_PAD * h:QK_PAD * h + NOPE], pe], axis=1)
                    q_out[0, h] = (qh * ATTN_SCALE).astype(BF16)
                k_out[0, h] = jnp.concatenate([kv[:, NOPE * h:NOPE * (h + 1)], kpe], axis=1).astype(BF16)
                v_out[0, h] = kv[:, 512 + VDIM * h:512 + VDIM * (h + 1)].astype(BF16)

        @pl.when(i < n)
        def _():
            project(x_ref[...], c_ref[...], sa_ref[...], sb_ref[...], p_ref, q_ref, k_ref, v_ref)

        @pl.when(i == n)
        def _():
            project(mt_ref[...], cm_ref[...], sam_ref[...], sbm_ref[...], pm_ref, None, km_ref, vm_ref)

    cl = lambda i: jnp.minimum(i, n - 1)
    full = lambda a: pl.BlockSpec(a.shape, lambda i: (0,) * a.ndim)
    const = lambda shape: pl.BlockSpec(shape, lambda i: (0,) * len(shape))
    tab = pl.BlockSpec((tm, 128), lambda i: (cl(i) % nt, 0))
    hb = lambda w: pl.BlockSpec((1, HEADS, tm, w), lambda i: (cl(i) // nt, 0, cl(i) % nt, 0))
    whole = pl.BlockSpec(memory_space=pl.ANY)
    return pl.pallas_call(
        body, name="fwd_proj", grid=(n_steps,),
        in_specs=[pl.BlockSpec((tm, D_MODEL), lambda i: (cl(i), 0)), tab, tab, tab,
                  full(meta), full(cm_t), full(sam_t), full(sbm_t),
                  full(norm_g), full(w_head), full(q_norm_g), full(wq_p), full(kv_norm_g), full(wkv_p), whole],
        out_specs=[pl.BlockSpec((tm, IN_HEAD), lambda i: (cl(i), 0)), hb(QK_PAD), hb(QK_PAD), hb(VDIM),
                   const((N_META, IN_HEAD)), const((1, HEADS, N_META, QK_PAD)), const((1, HEADS, N_META, VDIM)),
                   whole],
        out_shape=[jax.ShapeDtypeStruct((nb * s, IN_HEAD), F32),
                   jax.ShapeDtypeStruct((nb, HEADS, s, QK_PAD), BF16),
                   jax.ShapeDtypeStruct((nb, HEADS, s, QK_PAD), BF16),
                   jax.ShapeDtypeStruct((nb, HEADS, s, VDIM), BF16),
                   jax.ShapeDtypeStruct((N_META, IN_HEAD), F32),
                   jax.ShapeDtypeStruct((1, HEADS, N_META, QK_PAD), BF16),
                   jax.ShapeDtypeStruct((1, HEADS, N_META, VDIM), BF16),
                   jax.ShapeDtypeStruct((D_MODEL, D_MODEL), BF16)],
        scratch_shapes=gat.scratch(),
        compiler_params=_cparams("arbitrary"),
    )(x2d, c_t, sa_t, sb_t, meta, cm_t, sam_t, sbm_t, norm_g, w_head, q_norm_g, wq_p, kv_norm_g, wkv_p, w_out_shard)


def _attn_fwd(q, k, v, km, vm, w_in_shard, nb, s, tq):
    nq = s // tq
    n_steps = nb * HEADS
    gat = _StagedGather(W_IN_PIECES)
    assert n_steps >= 3

    def body(q_ref, k_ref, v_ref, km_ref, vm_ref, ws_ref, o_ref, lse_ref, w_ref, s_scr, p_scr,
             src_scr, land_scr, io_sems, *gat_scr):
        step = pl.program_id(0) * HEADS + pl.program_id(1)
        load = pltpu.make_async_copy(ws_ref, src_scr, io_sems.at[0])
        store = pltpu.make_async_copy(land_scr, w_ref, io_sems.at[1])

        @pl.when(step == 0)
        def _():
            load.start()
            land_scr[HEAD_ROWS:IN_HEAD, :] = jnp.zeros((IN_HEAD - HEAD_ROWS, D_MODEL), BF16)
            load.wait()
            gat.run(0, src_scr, land_scr, gat_scr)

        @pl.when(step == n_steps - 2)
        def _():
            gat.run(1, src_scr, land_scr, gat_scr)

        @pl.when(step == n_steps - 1)
        def _():
            gat.run(2, src_scr, land_scr, gat_scr)
            store.start()

        row = lax.broadcasted_iota(jnp.int32, (tq, tq), 0)
        col = lax.broadcasted_iota(jnp.int32, (tq, tq), 1)
        def scores(i):
            slot = i % 2
            qi = q_ref[0, 0, i * tq:(i + 1) * tq, :]
            sm = _dot_nt(qi, km_ref[0, 0])
            m128 = None
            for j in range(i + 1):
                sc = _dot_nt(qi, k_ref[0, 0, j * tq:(j + 1) * tq, :])
                if j == i:
                    sc = jnp.where(col <= row, sc, NEG_INF)
                s_scr[slot, :, j * tq:(j + 1) * tq] = sc
                mx = sc[:, 0:128]
                for c0 in range(128, tq, 128):
                    mx = jnp.maximum(mx, sc[:, c0:c0 + 128])
                m128 = mx if m128 is None else jnp.maximum(m128, mx)
            return sm, jnp.maximum(jnp.max(m128, axis=1, keepdims=True), jnp.max(sm, axis=1, keepdims=True))

        def weighted_sum(i, pm, l):
            n = (i + 1) * tq
            acc = _dot(p_scr[i % 2, :, 0:n], v_ref[0, 0, 0:n, :]) + _dot(pm.astype(BF16), vm_ref[0, 0])
            o_ref[0, 0, i * tq:(i + 1) * tq, :] = acc / l

        nxt, pending = scores(0), None
        for i in range(nq):
            slot = i % 2
            sm, m = nxt
            if i + 1 < nq:
                nxt = scores(i + 1)
            pm = jnp.exp(sm - m)
            l128 = None
            for j in range(i + 1):
                p = jnp.exp(s_scr[slot, :, j * tq:(j + 1) * tq] - m)
                p_scr[slot, :, j * tq:(j + 1) * tq] = p.astype(BF16)
                ps = p[:, 0:128]
                for c0 in range(128, tq, 128):
                    ps = ps + p[:, c0:c0 + 128]
                l128 = ps if l128 is None else l128 + ps
            l = jnp.sum(l128, axis=1, keepdims=True) + jnp.sum(pm, axis=1, keepdims=True)
            lse_ref[0, 0, :, i * tq:(i + 1) * tq] = _row_of(m + jnp.log(l), tq)
            if pending is not None:
                weighted_sum(*pending)
            pending = (i, pm, l)
        weighted_sum(*pending)

        @pl.when(step == n_steps - 1)
        def _():
            store.wait()

    hblk = lambda w: pl.BlockSpec((1, 1, s, w), lambda b, h: (b, h, 0, 0))
    mblk = lambda w: pl.BlockSpec((1, 1, N_META, w), lambda b, h: (0, h, 0, 0))
    whole = pl.BlockSpec(memory_space=pl.ANY)
    return pl.pallas_call(
        body, name="attn_fwd", grid=(nb, HEADS),
        in_specs=[hblk(QK_PAD), hblk(QK_PAD), hblk(VDIM), mblk(QK_PAD), mblk(VDIM), whole],
        out_specs=[hblk(VDIM), pl.BlockSpec((1, 1, 1, s), lambda b, h: (b, h, 0, 0)), whole],
        out_shape=[jax.ShapeDtypeStruct((nb, HEADS, s, VDIM), F32),
                   jax.ShapeDtypeStruct((nb, HEADS, 1, s), F32),
                   jax.ShapeDtypeStruct((IN_PAD, D_MODEL), BF16)],
        scratch_shapes=[pltpu.VMEM((2, tq, s), F32), pltpu.VMEM((2, tq, s), BF16),
                        pltpu.VMEM(w_in_shard.shape, BF16), pltpu.VMEM((IN_PAD, D_MODEL), BF16),
                        pltpu.SemaphoreType.DMA((2,))] + gat.scratch(),
        compiler_params=_cparams("arbitrary", "arbitrary"),
    )(q, k, v, km, vm, w_in_shard)


def _shift_rows(a, prev, n_rows):
    rid = lax.broadcasted_iota(jnp.int32, a.shape, 0)
    a1 = jnp.where(rid == 0, prev[7:8, :], pltpu.roll(a, 1, 0))
    a2 = jnp.where(rid == 0, prev[6:7, :], jnp.where(rid == 1, prev[7:8, :], pltpu.roll(a, 2, 0)))
    return a1, a2


def _attn_gate(o, za, ga_h):
    on, r = _rms(o, ga_h)
    return on * (za * _sigmoid(za)), on, r


def _out_fwd_bwd(x2d, tgt2d, o, meta, norm_g, w_in_p, conv_w, ga, gc, gmat, w_out, gf, nb, s, tm):
    nt = s // tm
    r = nb * s

    def body(x_ref, t_ref, o_ref, mt_ref, g_ref, wi_ref, cw_ref, ga_ref, gc_ref, gm_ref, w_ref, gf_ref,
             dh_ref, dy_ref, dw_ref, dgf_ref, loss_ref, p_ref, pm_ref, last_cc):
        i = pl.program_id(0)
        blk = lambda ref, j, rows=slice(None): ref[rows, 512 * j:512 * (j + 1)]

        def tail(xv):
            u, _ = _rms(xv, g_ref[...])
            return _dot_nt(u.astype(BF16), wi_ref[IN_HEAD:IN_PAD, :])

        @pl.when(i == 0)
        def _():
            dw_ref[...] = jnp.zeros_like(dw_ref)
            dgf_ref[...] = jnp.zeros_like(dgf_ref)
            loss_ref[...] = jnp.zeros_like(loss_ref)
            last_cc[...] = jnp.zeros_like(last_cc)
            pm_ref[...] = tail(mt_ref[...])

        p_ref[...] = tail(x_ref[...])
        ya = []
        for h in range(HEADS):
            y, _, _ = _attn_gate(o_ref[0, h], p_ref[:, 512 * BLK_ZA + VDIM * h:512 * BLK_ZA + VDIM * (h + 1)],
                                 ga_ref[:, VDIM * h:VDIM * (h + 1)])
            ya.append(y)
        cc = blk(p_ref, BLK_CC) * blk(p_ref, BLK_CH)
        meta_cc = blk(pm_ref, BLK_CC, slice(8, 16)) * blk(pm_ref, BLK_CH, slice(8, 16))
        prev = jnp.where(i % nt == 0, meta_cc, last_cc[...])
        last_cc[...] = cc[tm - 8:tm, :]
        cc1, cc2 = _shift_rows(cc, prev, tm)
        yc = blk(p_ref, BLK_CB) * (cw_ref[0:1, :] * cc2 + cw_ref[1:2, :] * cc1 + cw_ref[2:3, :] * cc)
        rg = lax.rsqrt(_group_mean(yc * yc, gm_ref[...]) + EPS)
        zc = blk(p_ref, BLK_ZC)
        yconv = yc * rg * gc_ref[...] * (zc * _sigmoid(zc))
        ycat = jnp.concatenate(ya + [yconv], axis=1).astype(BF16)
        h2 = x_ref[...] + _dot(ycat, w_ref[...])
        gfv = gf_ref[...]
        y, r2 = _rms(h2, gfv)
        e = y - t_ref[...]
        loss_ref[...] += 0.5 * jnp.sum(e * e) / D_MODEL
        dyv = e * (1.0 / D_MODEL)
        dh2, dgf = _rms_bwd(dyv, h2, r2, gfv)
        dgf_ref[...] += jnp.sum(dgf, axis=0, keepdims=True)
        dh_ref[...] = dh2
        dhb = dh2.astype(BF16)
        dy_ref[...] = _dot_nt(dhb, w_ref[...])
        dw_ref[...] += _dot_tn(ycat, dhb)

    row = lambda w: pl.BlockSpec((tm, w), lambda i: (i, 0))
    const = lambda shape: pl.BlockSpec(shape, lambda i: (0,) * len(shape))
    full = lambda a: const(a.shape)
    return pl.pallas_call(
        body, name="out_fwd_bwd", grid=(nb * nt,),
        in_specs=[row(D_MODEL), row(D_MODEL),
                  pl.BlockSpec((1, HEADS, tm, VDIM), lambda i: (i // nt, 0, i % nt, 0)),
                  full(meta), full(norm_g), full(w_in_p),
                  full(conv_w), full(ga), full(gc), full(gmat), full(w_out), full(gf)],
        out_specs=[row(D_MODEL), row(D_MODEL), const((D_MODEL, D_MODEL)), const((1, D_MODEL)), const((1, 128)),
                   row(IN_TAIL), const((N_META, IN_TAIL))],
        out_shape=[jax.ShapeDtypeStruct((r, D_MODEL), F32), jax.ShapeDtypeStruct((r, D_MODEL), F32),
                   jax.ShapeDtypeStruct((D_MODEL, D_MODEL), F32), jax.ShapeDtypeStruct((1, D_MODEL), F32),
                   jax.ShapeDtypeStruct((1, 128), F32),
                   jax.ShapeDtypeStruct((r, IN_TAIL), F32), jax.ShapeDtypeStruct((N_META, IN_TAIL), F32)],
        scratch_shapes=[pltpu.VMEM((8, 512), F32)],
        compiler_params=_cparams("arbitrary"),
    )(x2d, tgt2d, o, meta, norm_g, w_in_p, conv_w, ga, gc, gmat, w_out, gf)


def _gate_bwd(dycat, o, p, pm, conv_w, ga, gc, gmat, nb, s, tm):
    nt = s // tm
    r = nb * s
    ext = tm + 8
    prev_idx = lambda i: jnp.maximum(i * (tm // 8) - 1, 0)
    next_idx = lambda i: jnp.minimum((i + 1) * (tm // 8), r // 8 - 1)

    def body(dya_ref, dyc_ref, dycn_ref, o_ref, za_ref, cb_ref, cbn_ref, cc_ref, ccp_ref, ccn_ref,
             ch_ref, chp_ref, chn_ref, zc_ref, zcn_ref, mc_ref, mh_ref, cw_ref, ga_ref, gc_ref, gm_ref,
             dpb_ref, do_ref, dl_ref, dccm_ref, dga_ref, dgc_ref, dcw_ref):
        i = pl.program_id(0)

        @pl.when(i == 0)
        def _():
            dga_ref[...] = jnp.zeros_like(dga_ref)
            dgc_ref[...] = jnp.zeros_like(dgc_ref)
            dcw_ref[...] = jnp.zeros_like(dcw_ref)

        dga = []
        for h in range(HEADS):
            hs = slice(VDIM * h, VDIM * (h + 1))
            oh, za, gah, dya = o_ref[0, h], za_ref[:, hs], ga_ref[:, hs], dya_ref[:, hs]
            sg = _sigmoid(za)
            on, ro = _rms(oh, gah)
            don = dya * (za * sg)
            dpb_ref[:, hs] = (dya * on * (sg * (1.0 + za * (1.0 - sg)))).astype(BF16)
            do, dg = _rms_bwd(don, oh, ro, gah)
            dga.append(jnp.sum(dg, axis=0, keepdims=True))
            dob = do.astype(BF16)
            do_ref[0, h] = dob
            dl_ref[0, h] = _row_of(jnp.sum(dob.astype(F32) * oh, axis=1, keepdims=True), tm)
        dga_ref[...] += jnp.concatenate(dga, axis=1)

        cat = lambda a, b: jnp.concatenate([a[...], b[...]], axis=0)
        cch = cat(cc_ref, ccn_ref)
        chh = cat(ch_ref, chn_ref)
        cb = cat(cb_ref, cbn_ref)
        zc = cat(zc_ref, zcn_ref)
        dy = cat(dyc_ref, dycn_ref)
        first = i % nt == 0
        last = i % nt == nt - 1
        cc = cch * chh
        prev = jnp.where(first, mc_ref[8:16, :] * mh_ref[8:16, :], ccp_ref[...] * chp_ref[...])
        cc1, cc2 = _shift_rows(cc, prev, ext)
        w0, w1, w2 = cw_ref[0:1, :], cw_ref[1:2, :], cw_ref[2:3, :]
        dw = w0 * cc2 + w1 * cc1 + w2 * cc
        yc = cb * dw
        rg = lax.rsqrt(_group_mean(yc * yc, gm_ref[...]) + EPS)
        ych = yc * rg
        gcv = gc_ref[...]
        sg = _sigmoid(zc)
        dycn = dy * (zc * sg)
        dzc = dy * (ych * gcv) * (sg * (1.0 + zc * (1.0 - sg)))
        dgc_ref[...] += jnp.sum((dycn * ych)[:tm], axis=0, keepdims=True)
        dycg = dycn * gcv
        dyc = rg * (dycg - ych * _group_mean(dycg * ych, gm_ref[...]))
        rid = lax.broadcasted_iota(jnp.int32, (ext, CONV_W), 0)
        ddw = jnp.where(jnp.logical_and(last, rid >= tm), 0.0, dyc * cb)
        dcb = dyc * dw
        dcc = w2 * ddw + w1 * pltpu.roll(ddw, ext - 1, 0) + w0 * pltpu.roll(ddw, ext - 2, 0)
        dpb_ref[:, 512:1024] = dcb[:tm].astype(BF16)
        dpb_ref[:, 1024:1536] = (dcc * chh)[:tm].astype(BF16)
        dpb_ref[:, 1536:2048] = (dcc * cch)[:tm].astype(BF16)
        dpb_ref[:, 2048:2560] = dzc[:tm].astype(BF16)
        rs = lambda a: jnp.sum(a[:tm], axis=0, keepdims=True)
        dcw_ref[0:1, :] += rs(ddw * cc2)
        dcw_ref[1:2, :] += rs(ddw * cc1)
        dcw_ref[2:3, :] += rs(ddw * cc)

        @pl.when(first)
        def _():
            d0, d1 = ddw[0:1, :], ddw[1:2, :]
            r8 = lax.broadcasted_iota(jnp.int32, (8, CONV_W), 0)
            dccm_ref[0] = jnp.where(r8 == 7, w1 * d0 + w0 * d1, jnp.where(r8 == 6, w0 * d0, 0.0))

    row = lambda j: pl.BlockSpec((tm, 512), lambda i: (i, j))
    prv = lambda j: pl.BlockSpec((8, 512), lambda i: (prev_idx(i), j))
    nxt = lambda j: pl.BlockSpec((8, 512), lambda i: (next_idx(i), j))
    mblk = lambda j: pl.BlockSpec((N_META, 512), lambda i: (0, j))
    full = lambda a: pl.BlockSpec(a.shape, lambda i: (0,) * a.ndim)
    hb = lambda w: pl.BlockSpec((1, HEADS, tm, w), lambda i: (i // nt, 0, i % nt, 0))
    acc = lambda rr: pl.BlockSpec((rr, 512), lambda i: (0, 0))
    return pl.pallas_call(
        body, name="gate_bwd", grid=(nb * nt,),
        in_specs=[row(0), row(1), nxt(1), hb(VDIM),
                  row(BLK_ZA), row(BLK_CB), nxt(BLK_CB), row(BLK_CC), prv(BLK_CC), nxt(BLK_CC),
                  row(BLK_CH), prv(BLK_CH), nxt(BLK_CH), row(BLK_ZC), nxt(BLK_ZC),
                  mblk(BLK_CC), mblk(BLK_CH), full(conv_w), full(ga), full(gc), full(gmat)],
        out_specs=[pl.BlockSpec((tm, 2560), lambda i: (i, 0)), hb(VDIM),
                   pl.BlockSpec((1, HEADS, 1, tm), lambda i: (i // nt, 0, 0, i % nt)),
                   pl.BlockSpec((1, 8, 512), lambda i: (i // nt, 0, 0)),
                   acc(1), acc(1), acc(8)],
        out_shape=[jax.ShapeDtypeStruct((r, 2560), BF16), jax.ShapeDtypeStruct((nb, HEADS, s, VDIM), BF16),
                   jax.ShapeDtypeStruct((nb, HEADS, 1, s), F32), jax.ShapeDtypeStruct((nb, 8, 512), F32),
                   jax.ShapeDtypeStruct((1, 512), F32), jax.ShapeDtypeStruct((1, 512), F32),
                   jax.ShapeDtypeStruct((8, 512), F32)],
        compiler_params=_cparams("arbitrary"),
    )(dycat, dycat, dycat, o, p, p, p, p, p, p, p, p, p, p, p, pm, pm, conv_w, ga, gc, gmat)


class _StagedReduce:
    LOC, PRE_S, PRE_R, ICI_S, ICI_R, POST_S, POST_R, OUT, N_SEM = 0, 1, 2, 3, 6, 9, 10, 11, 12

    def __init__(self, shard_shape):
        self.half = (shard_shape[0] // 2, shard_shape[1])

    def scratch(self):
        h = self.half
        return [pltpu.VMEM((4,) + h, F32), pltpu.VMEM((4,) + h, F32), pltpu.VMEM((4,) + h, BF16),
                pltpu.VMEM((3,) + h, BF16), pltpu.VMEM(h, F32), pltpu.SemaphoreType.DMA((self.N_SEM,))]

    def run(self, stage, pin, gout, scr):
        own, sib, wire, rbuf, fin, sems = scr
        r2 = self.half[0]
        x, y, c = lax.axis_index("x"), lax.axis_index("y"), lax.axis_index("c")
        mine = 2 * x + y
        sibling = (x, y, 1 - c)
        chips = [(1 - x, y), (x, 1 - y), (1 - x, 1 - y)]
        rows = lambda half: pl.ds(pl.multiple_of(half * r2, r2), r2)
        mesh = pl.DeviceIdType.MESH

        loc = pltpu.make_async_copy(pin.at[:, rows(c), :], own, sems.at[self.LOC])
        pre = pltpu.make_async_remote_copy(
            src_ref=pin.at[:, rows(1 - c), :], dst_ref=sib, send_sem=sems.at[self.PRE_S],
            recv_sem=sems.at[self.PRE_R], device_id=sibling, device_id_type=mesh)

        def ici(j):
            px, py = chips[j]
            return pltpu.make_async_remote_copy(
                src_ref=wire.at[2 * px + py], dst_ref=rbuf.at[j], send_sem=sems.at[self.ICI_S + j],
                recv_sem=sems.at[self.ICI_R + j], device_id=(px, py, c), device_id_type=mesh)

        def post(half):
            return pltpu.make_async_remote_copy(
                src_ref=fin, dst_ref=gout.at[rows(half), :], send_sem=sems.at[self.POST_S],
                recv_sem=sems.at[self.POST_R], device_id=sibling, device_id_type=mesh)

        keep = pltpu.make_async_copy(fin, gout.at[rows(c), :], sems.at[self.OUT])
        if stage == 0:
            loc.start()
            pre.start()
        elif stage == 1:
            loc.wait()
            pre.wait_recv()
            for blk in range(4):
                tot = own[blk] + sib[blk]
                own[blk] = tot
                wire[blk] = tot.astype(BF16)
            for j in range(3):
                ici(j).start()
        elif stage == 2:
            for j in range(3):
                ici(j).wait_recv()
            tot = own[mine]
            for j in range(3):
                tot = tot + rbuf[j].astype(F32)
            fin[...] = tot
            post(c).start()
            keep.start()
        else:
            post(1 - c).wait_recv()
            pre.wait_send()
            for j in range(3):
                ici(j).wait_send()
            post(c).wait_send()
            keep.wait()


def _attn_bwd(q, k, v, do, lse, delta, km, vm, early, nb, s, t):
    n = s // t
    ne = len(early)
    reds = [_StagedReduce(a.shape[1:]) for a in early]
    n_steps = HEADS * nb
    assert n_steps >= 4

    def body(q_ref, k_ref, v_ref, do_ref, lse_ref, dl_ref, km_ref, vm_ref, *rest):
        pin_refs, rest = rest[:ne], rest[ne:]
        dq_ref, dk_ref, dv_ref, dkm_ref, dvm_ref = rest[:5]
        gout_refs, (p_scr, ds_scr, dq_acc), red_scr = rest[5:5 + ne], rest[5 + ne:8 + ne], rest[8 + ne:]
        b = pl.program_id(1)
        step = pl.program_id(0) * nb + b
        for stage, at in enumerate((0, 1, n_steps - 2, n_steps - 1)):
            @pl.when(step == at)
            def _(stage=stage):
                for a, red in enumerate(reds):
                    red.run(stage, pin_refs[a], gout_refs[a], red_scr[6 * a:6 * a + 6])

        @pl.when(b == 0)
        def _():
            dkm_ref[...] = jnp.zeros_like(dkm_ref)
            dvm_ref[...] = jnp.zeros_like(dvm_ref)

        kr = lax.broadcasted_iota(jnp.int32, (t, t), 0)
        qc = lax.broadcasted_iota(jnp.int32, (t, t), 1)
        km_v, vm_v = km_ref[0, 0], vm_ref[0, 0]
        ptm = jnp.exp(_dot_nt(km_v, q_ref[0, 0]) - lse_ref[0, 0])
        dstm = (ptm * (_dot_nt(vm_v, do_ref[0, 0]) - dl_ref[0, 0])).astype(BF16)
        dkm_ref[0] += _dot(dstm, q_ref[0, 0])
        dvm_ref[0] += _dot(ptm.astype(BF16), do_ref[0, 0])
        dq_acc[...] = _dot_tn(dstm, km_v)
        def tiles(j):
            slot = j % 2
            kj = k_ref[0, 0, j * t:(j + 1) * t, :]
            vj = v_ref[0, 0, j * t:(j + 1) * t, :]
            def products(i):
                cs = slice(i * t, (i + 1) * t)
                return _dot_nt(kj, q_ref[0, 0, cs, :]), _dot_nt(vj, do_ref[0, 0, cs, :])

            nxt, pending = products(j), None
            for i in range(j, n):
                cs = slice(i * t, (i + 1) * t)
                st, dpt = nxt
                if i + 1 < n:
                    nxt = products(i + 1)
                if i == j:
                    st = jnp.where(kr <= qc, st, NEG_INF)
                pt = jnp.exp(st - lse_ref[0, 0, :, cs])
                dst = (pt * (dpt - dl_ref[0, 0, :, cs])).astype(BF16)
                p_scr[slot, :, cs] = pt.astype(BF16)
                ds_scr[slot, :, cs] = dst
                if pending is not None:
                    dq_acc[pending[0], :] += _dot_tn(pending[1], kj)
                pending = (cs, dst)
            dq_acc[pending[0], :] += _dot_tn(pending[1], kj)

        for j in range(n):
            slot = j % 2
            tiles(j)
            dv_ref[0, 0, j * t:(j + 1) * t, :] = _dot(p_scr[slot, :, j * t:s], do_ref[0, 0, j * t:s, :]).astype(BF16)
            dk_ref[0, 0, j * t:(j + 1) * t, :] = _dot(ds_scr[slot, :, j * t:s], q_ref[0, 0, j * t:s, :]).astype(BF16)
        dq_ref[0, 0] = dq_acc[...].astype(BF16)

    big = lambda w: pl.BlockSpec((1, 1, s, w), lambda h, b: (b, h, 0, 0))
    rowv = pl.BlockSpec((1, 1, 1, s), lambda h, b: (b, h, 0, 0))
    mk = lambda w: pl.BlockSpec((1, 1, N_META, w), lambda h, b: (0, h, 0, 0))
    mo = lambda w: pl.BlockSpec((1, N_META, w), lambda h, b: (h, 0, 0))
    return pl.pallas_call(
        body, name="attn_bwd", grid=(HEADS, nb),
        in_specs=[big(QK_PAD), big(QK_PAD), big(VDIM), big(VDIM), rowv, rowv, mk(QK_PAD), mk(VDIM)]
        + [pl.BlockSpec(memory_space=pl.ANY)] * ne,
        out_specs=[big(QK_PAD), big(QK_PAD), big(VDIM), mo(QK_PAD), mo(VDIM)]
        + [pl.BlockSpec(memory_space=pl.ANY)] * ne,
        out_shape=[jax.ShapeDtypeStruct((nb, HEADS, s, QK_PAD), BF16),
                   jax.ShapeDtypeStruct((nb, HEADS, s, QK_PAD), BF16),
                   jax.ShapeDtypeStruct((nb, HEADS, s, VDIM), BF16),
                   jax.ShapeDtypeStruct((HEADS, N_META, QK_PAD), F32),
                   jax.ShapeDtypeStruct((HEADS, N_META, VDIM), F32)]
        + [jax.ShapeDtypeStruct(a.shape[1:], F32) for a in early],
        scratch_shapes=[pltpu.VMEM((2, t, s), BF16), pltpu.VMEM((2, t, s), BF16), pltpu.VMEM((s, QK_PAD), F32)]
        + [sc for red in reds for sc in red.scratch()],
        compiler_params=_cparams("arbitrary", "arbitrary"),
    )(q, k, v, do, lse, delta, km, vm, *early)


def _up_bwd(dq, dk, dv, dkm, dvm, p, pm, tabs, tabs_m, wq_p, wkv_p, gq, gkv, nb, s, tm):
    nt = s // tm
    n = nb * nt
    c_t, sa_t, sb_t = tabs
    cm_t, sam_t, sbm_t = tabs_m

    def kv_path(dkh, dvh, pa, c, sa, sb, wkv, gkvv):
        dkpe = dkh[0][:, NOPE:]
        for h in range(1, HEADS):
            dkpe = dkpe + dkh[h][:, NOPE:]
        dkr = _rope_bwd(dkpe, c, sa, sb)
        dkv = jnp.concatenate([d[:, :NOPE] for d in dkh] + list(dvh), axis=1).astype(BF16)
        ckv = pa[:, Q_RANK:Q_RANK + KV_RANK]
        kvn, rkv = _rms(ckv, gkvv)
        dckv, dg = _rms_bwd(_dot(dkv, wkv), ckv, rkv, gkvv)
        return dckv, dkr, kvn.astype(BF16), dkv, jnp.sum(dg, axis=0, keepdims=True)

    def body(dq_ref, dk_ref, dv_ref, pa_ref, c_ref, sa_ref, sb_ref,
             dkm_ref, dvm_ref, pam_ref, cm_ref, sam_ref, sbm_ref,
             wq_ref, wkv_ref, gq_ref, gkv_ref,
             dpa_ref, dpam_ref, pq_ref, pkv_ref, dgq_ref, dgkv_ref, dwq_ref, dwkv_ref):
        i = pl.program_id(0)

        @pl.when(i == 0)
        def _():
            dwq_ref[...] = jnp.zeros_like(dwq_ref)
            dwkv_ref[...] = jnp.zeros_like(dwkv_ref)
            dgq_ref[...] = jnp.zeros_like(dgq_ref)
            dgkv_ref[...] = jnp.zeros_like(dgkv_ref)

        @pl.when(i < n)
        def _():
            c, sa, sb = c_ref[...], sa_ref[...], sb_ref[...]
            pa = pa_ref[...]
            parts = []
            for h in range(HEADS):
                dqh = dq_ref[0, h].astype(F32) * ATTN_SCALE
                parts += [dqh[:, :NOPE], _rope_bwd(dqh[:, NOPE:], c, sa, sb)]
            dql = jnp.concatenate(parts, axis=1).astype(BF16)
            cq = pa[:, 0:Q_RANK]
            gqv = gq_ref[...]
            qn, rq = _rms(cq, gqv)
            dwq_ref[...] += _dot_tn(dql, qn.astype(BF16))
            dcq, dg = _rms_bwd(_dot(dql, wq_ref[...]), cq, rq, gqv)
            dgq_ref[...] += jnp.sum(dg, axis=0, keepdims=True)
            dckv, dkr, kvn, dkv, dgk = kv_path([dk_ref[0, h].astype(F32) for h in range(HEADS)],
                                               [dv_ref[0, h].astype(F32) for h in range(HEADS)],
                                               pa, c, sa, sb, wkv_ref[...], gkv_ref[...])
            dwkv_ref[...] += _dot_tn(dkv, kvn)
            dgkv_ref[...] += dgk
            dpa_ref[...] = jnp.concatenate([dcq, dckv, dkr], axis=1).astype(BF16)

        @pl.when(i == n)
        def _():
            dckv, dkr, kvn, dkv, dgk = kv_path([dkm_ref[h] for h in range(HEADS)],
                                               [dvm_ref[h] for h in range(HEADS)],
                                               pam_ref[...], cm_ref[...], sam_ref[...], sbm_ref[...],
                                               wkv_ref[...], gkv_ref[...])
            dwkv_ref[...] += _dot_tn(dkv, kvn)
            dgkv_ref[...] += dgk
            dpam_ref[...] = jnp.concatenate([jnp.zeros((N_META, Q_RANK), F32), dckv, dkr], axis=1)
            for h in range(HEADS):
                pq_ref[h] = dwq_ref[QK_PAD * h:QK_PAD * h + NOPE + ROPE, :]
                pkv_ref[h, 0:NOPE, :] = dwkv_ref[NOPE * h:NOPE * (h + 1), :]
                pkv_ref[h, NOPE:NOPE + VDIM, :] = dwkv_ref[512 + VDIM * h:512 + VDIM * (h + 1), :]

    cl = lambda i: jnp.minimum(i, n - 1)
    hb = lambda w: pl.BlockSpec((1, HEADS, tm, w), lambda i: (cl(i) // nt, 0, cl(i) % nt, 0))
    tab = pl.BlockSpec((tm, 128), lambda i: (cl(i) % nt, 0))
    full = lambda a: pl.BlockSpec(a.shape, lambda i: (0,) * a.ndim)
    const = lambda shape: pl.BlockSpec(shape, lambda i: (0,) * len(shape))
    return pl.pallas_call(
        body, name="up_bwd", grid=(n + 1,),
        in_specs=[hb(QK_PAD), hb(QK_PAD), hb(VDIM), pl.BlockSpec((tm, 512), lambda i: (cl(i), 0)), tab, tab, tab,
                  full(dkm), full(dvm), pl.BlockSpec((N_META, 512), lambda i: (0, 0)),
                  full(cm_t), full(sam_t), full(sbm_t), full(wq_p), full(wkv_p), full(gq), full(gkv)],
        out_specs=[pl.BlockSpec((tm, 512), lambda i: (cl(i), 0)), const((N_META, 512)),
                   const((HEADS, NOPE + ROPE, Q_RANK)), const((HEADS, NOPE + VDIM, KV_RANK)),
                   const((1, Q_RANK)), const((1, KV_RANK))],
        out_shape=[jax.ShapeDtypeStruct((nb * s, 512), BF16), jax.ShapeDtypeStruct((N_META, 512), F32),
                   jax.ShapeDtypeStruct((HEADS, NOPE + ROPE, Q_RANK), F32),
                   jax.ShapeDtypeStruct((HEADS, NOPE + VDIM, KV_RANK), F32),
                   jax.ShapeDtypeStruct((1, Q_RANK), F32), jax.ShapeDtypeStruct((1, KV_RANK), F32)],
        scratch_shapes=[pltpu.VMEM((HEADS * QK_PAD, Q_RANK), F32), pltpu.VMEM((1024, KV_RANK), F32)],
        compiler_params=_cparams("arbitrary"),
    )(dq, dk, dv, p, c_t, sa_t, sb_t, dkm, dvm, pm, cm_t, sam_t, sbm_t, wq_p, wkv_p, gq, gkv)


def _in_bwd(x2d, dh2, dpa, dpb, meta, dpam, dccm, pm, w_in_p, norm_g, nb, s, tm):
    nt = s // tm
    n = nb * nt

    def body(x_ref, dh_ref, dpa_ref, dpb_ref, mt_ref, dpam_ref, dccm_ref, mc_ref, mh_ref, w_ref, g_ref,
             gx_ref, gm_ref, dw_hbm, dg_ref, acc_ref, sems):
        i = pl.program_id(0)

        @pl.when(i == 0)
        def _():
            acc_ref[...] = jnp.zeros_like(acc_ref)
            dg_ref[...] = jnp.zeros_like(dg_ref)

        def rows(x, dp, dres):
            g = g_ref[...]
            dpb16 = dp.astype(BF16)
            du = _dot(dpb16, w_ref[...])
            u, r1 = _rms(x, g)
            acc_ref[...] += _dot_tn(dpb16, u.astype(BF16))
            dx, dg = _rms_bwd(du, x, r1, g)
            dg_ref[...] += jnp.sum(dg, axis=0, keepdims=True)
            return dx if dres is None else dx + dres

        @pl.when(i < n)
        def _():
            dp = jnp.concatenate([dpa_ref[...], dpb_ref[...]], axis=1)
            gx_ref[...] = rows(x_ref[...], dp, dh_ref[...])

        @pl.when(i == n)
        def _():
            dcc = dccm_ref[0]
            for b in range(1, nb):
                dcc = dcc + dccm_ref[b]
            z8 = jnp.zeros((8, CONV_W), F32)
            dc = jnp.concatenate([z8, dcc * mh_ref[8:16, :]], axis=0)
            dh = jnp.concatenate([z8, dcc * mc_ref[8:16, :]], axis=0)
            z = jnp.zeros((N_META, CONV_W), F32)
            dp = jnp.concatenate([dpam_ref[...], z, z, dc, dh, z], axis=1)
            gm_ref[...] = rows(mt_ref[...], dp, None)
            per = IN_DIM // 4
            cps = [pltpu.make_async_copy(acc_ref.at[0:448], dw_hbm.at[0, 0:448], sems.at[0]),
                   pltpu.make_async_copy(acc_ref.at[512:per + 64], dw_hbm.at[0, 448:per], sems.at[1])]
            for qq in range(1, 4):
                cps.append(pltpu.make_async_copy(acc_ref.at[per * qq + 64:per * (qq + 1) + 64], dw_hbm.at[qq],
                                                 sems.at[qq + 1]))
            for cp in cps:
                cp.start()
            for cp in cps:
                cp.wait()

    cl = lambda i: jnp.minimum(i, n - 1)
    row = lambda w: pl.BlockSpec((tm, w), lambda i: (cl(i), 0))
    full = lambda a: pl.BlockSpec(a.shape, lambda i: (0,) * a.ndim)
    mblk = lambda j: pl.BlockSpec((N_META, 512), lambda i: (0, j))
    return pl.pallas_call(
        body, name="in_bwd", grid=(n + 1,),
        in_specs=[row(D_MODEL), row(D_MODEL), row(512), row(2560), full(meta), full(dpam), full(dccm),
                  mblk(BLK_CC), mblk(BLK_CH), full(w_in_p), full(norm_g)],
        out_specs=[row(D_MODEL), pl.BlockSpec((N_META, D_MODEL), lambda i: (0, 0)),
                   pl.BlockSpec(memory_space=pl.ANY), pl.BlockSpec((1, D_MODEL), lambda i: (0, 0))],
        out_shape=[jax.ShapeDtypeStruct((nb * s, D_MODEL), F32), jax.ShapeDtypeStruct((N_META, D_MODEL), F32),
                   jax.ShapeDtypeStruct((4, IN_DIM // 4, D_MODEL), F32), jax.ShapeDtypeStruct((1, D_MODEL), F32)],
        scratch_shapes=[pltpu.VMEM((IN_PAD, D_MODEL), F32), pltpu.SemaphoreType.DMA((5,))],
        compiler_params=_cparams("arbitrary"),
    )(x2d, dh2, dpa, dpb, meta, dpam, dccm, pm, pm, w_in_p, norm_g)


def _gather_weights(w_in_shard, split, pieces, out_rows, whole, zero_fills):
    ns, nw, nz = len(split), len(whole), len(zero_fills)
    flat = [(a, pc) for a in range(ns) for pc in pieces[a]]
    nk = len(flat)
    hh = HEAD_ROWS // 2

    def body(*refs):
        ins, wins, zins = refs[1:1 + ns], refs[1 + ns:1 + ns + nw], refs[1 + ns + nw:1 + ns + nw + nz]
        n_in = 1 + ns + nw + nz
        head_ref, shard16 = refs[n_in], refs[n_in + 1]
        outs, wouts = refs[n_in + 2:n_in + 2 + ns], refs[n_in + 2 + ns:n_in + 2 + ns + nw]
        scr = refs[n_in + 2 + ns + nw:]
        stage = scr[:ns]
        (send_sems, recv_sems, fwd_send, fwd_recv, loc_sems, w_send, w_recv, w_loc, z_sems,
         h_send, h_recv, h_pass) = scr[ns:]
        x, y, c = lax.axis_index("x"), lax.axis_index("y"), lax.axis_index("c")
        mine = 2 * x + y
        chips = [(1 - x, y), (x, 1 - y), (1 - x, 1 - y)]
        chip_of = [2 * px + py for px, py in chips]
        shard16[...] = refs[0][...].astype(BF16)
        for a in range(ns):
            stage[a][...] = ins[a][...].astype(BF16)

        def head_rows(half):
            return pl.ds(pl.multiple_of(half * hh, 16), hh)

        def head_copy(j):
            px, py = chips[j]
            return pltpu.make_async_remote_copy(
                src_ref=shard16.at[head_rows(c)], dst_ref=head_ref.at[head_rows(c)], send_sem=h_send.at[j],
                recv_sem=h_recv.at[0], device_id=(px, py, c), device_id_type=pl.DeviceIdType.MESH)

        def head_pass(half):
            ref = head_ref.at[head_rows(half)]
            return pltpu.make_async_remote_copy(
                src_ref=ref, dst_ref=ref, send_sem=h_pass.at[0], recv_sem=h_pass.at[1],
                device_id=(x, y, 1 - c), device_id_type=pl.DeviceIdType.MESH)

        head_ref[HEAD_ROWS:IN_HEAD, :] = jnp.zeros((IN_HEAD - HEAD_ROWS, D_MODEL), BF16)

        @pl.when(mine == 0)
        def _():
            for j in range(3):
                head_copy(j).start()
            head_ref[0:HEAD_ROWS, :] = shard16[0:HEAD_ROWS, :]

        def src(k):
            a, (s0, nr, _, _, _, _) = flat[k]
            return stage[a].at[s0:s0 + nr]

        def dst(k, q):
            a, (_, nr, per, first, rest, _) = flat[k]
            row = per * q + first + (rest - first) * jnp.minimum(q, 1)
            return outs[a].at[pl.ds(pl.multiple_of(row, 16), nr)]

        def ici(k, j, q):
            px, py = chips[j]
            return pltpu.make_async_remote_copy(
                src_ref=src(k), dst_ref=dst(k, q), send_sem=send_sems.at[k, j], recv_sem=recv_sems.at[k, j],
                device_id=(px, py, c), device_id_type=pl.DeviceIdType.MESH)

        def fwd(k, j):
            ref = dst(k, chip_of[j])
            return pltpu.make_async_remote_copy(
                src_ref=ref, dst_ref=ref, send_sem=fwd_send.at[k, j], recv_sem=fwd_recv.at[k, j],
                device_id=(x, y, 1 - c), device_id_type=pl.DeviceIdType.MESH)

        def wcopy(b, j, q):
            px, py = chips[j]
            return pltpu.make_async_remote_copy(
                src_ref=wins[b], dst_ref=wouts[b].at[q], send_sem=w_send.at[b, j], recv_sem=w_recv.at[b, j],
                device_id=(px, py, c), device_id_type=pl.DeviceIdType.MESH)

        local = [pltpu.make_async_copy(src(k), dst(k, mine), loc_sems.at[k]) for k in range(nk)]
        local += [pltpu.make_async_copy(wins[b], wouts[b].at[mine], w_loc.at[b]) for b in range(nw)]
        for z, (a, _, row0) in enumerate(zero_fills):
            local.append(pltpu.make_async_copy(zins[z], outs[a].at[row0:row0 + zins[z].shape[0]], z_sems.at[z]))
        wsends = [wcopy(b, j, mine) for b in range(nw) for j in range(3)]
        for cp in local + wsends:
            cp.start()

        for half in (0, 1):
            @pl.when(c == half)
            def _(half=half):
                my_k = [k for k in range(nk) if flat[k][1][5] == half]
                other_k = [k for k in range(nk) if flat[k][1][5] != half]
                sends = [ici(k, j, mine) for k in my_k for j in range(3)]
                for cp in sends:
                    cp.start()
                passed = []
                for k in my_k:
                    for j in range(3):
                        ici(k, j, chip_of[j]).wait_recv()
                        cp = fwd(k, j)
                        cp.start()
                        passed.append(cp)
                for k in other_k:
                    for j in range(3):
                        fwd(k, j).wait_recv()
                for cp in sends + passed:
                    cp.wait_send()

        for b in range(nw):
            for j in range(3):
                wcopy(b, j, chip_of[j]).wait_recv()
        for cp in wsends:
            cp.wait_send()
        for cp in local:
            cp.wait()

        @pl.when(mine == 0)
        def _():
            for j in range(3):
                head_copy(j).wait_send()

        @pl.when(mine != 0)
        def _():
            head_copy(0).wait_recv()
            cp = head_pass(c)
            cp.start()
            head_pass(1 - c).wait_recv()
            cp.wait_send()

    vmem = pl.BlockSpec(memory_space=pltpu.VMEM)
    dma = pltpu.SemaphoreType.DMA
    zeros = [z for _, z, _ in zero_fills]
    return pl.pallas_call(
        body, name="gather_weights",
        in_specs=[vmem] * (1 + ns + nw + nz), out_specs=[vmem] * (2 + ns + nw),
        out_shape=([jax.ShapeDtypeStruct((IN_HEAD, D_MODEL), BF16), jax.ShapeDtypeStruct(w_in_shard.shape, BF16)]
                   + [jax.ShapeDtypeStruct((out_rows[a], split[a].shape[1]), BF16) for a in range(ns)]
                   + [jax.ShapeDtypeStruct((4,) + w.shape, w.dtype) for w in whole]),
        scratch_shapes=[pltpu.VMEM(a.shape, BF16) for a in split]
        + [dma((nk, 3)), dma((nk, 3)), dma((nk, 3)), dma((nk, 3)), dma((nk,)),
           dma((nw, 3)), dma((nw, 3)), dma((nw,)), dma((nz,)), dma((3,)), dma((1,)), dma((2,))],
        compiler_params=pltpu.CompilerParams(vmem_limit_bytes=VMEM_LIMIT),
    )(w_in_shard, *split, *whole, *zeros)


def _reduce_grads(parts, small):
    n = len(parts)
    shapes = [a.shape[1:] for a in parts]
    halves = [(sh[0] // 2, sh[1]) for sh in shapes]

    def body(*refs):
        pin, sm_in = refs[:n], refs[n]
        gout, sm_out = refs[n + 1:2 * n + 1], refs[2 * n + 1]
        scr = refs[2 * n + 2:]
        own, sib, wire, rbuf = scr[:n], scr[n:2 * n], scr[2 * n:3 * n], scr[3 * n:4 * n]
        (sbuf, send_sems, recv_sems, loc_sems, pre_send, pre_recv, post_send, post_recv,
         sm_send, sm_recv) = scr[4 * n:]
        x, y, c = lax.axis_index("x"), lax.axis_index("y"), lax.axis_index("c")
        mine = 2 * x + y
        me = 4 * x + 2 * y + c
        sibling = (x, y, 1 - c)
        chips = [(1 - x, y), (x, 1 - y), (1 - x, 1 - y)]

        def rows(a, half):
            r2 = halves[a][0]
            return pl.ds(pl.multiple_of(half * r2, r2), r2)

        chip_of = [2 * px + py for px, py in chips]
        blocks = chip_of + [mine]

        def pre(a, k):
            return pltpu.make_async_remote_copy(
                src_ref=pin[a].at[blocks[k], rows(a, 1 - c), :], dst_ref=sib[a].at[blocks[k]],
                send_sem=pre_send.at[a, k], recv_sem=pre_recv.at[a, k], device_id=sibling,
                device_id_type=pl.DeviceIdType.MESH)

        def ici(a, j):
            px, py = chips[j]
            return pltpu.make_async_remote_copy(
                src_ref=wire[a].at[2 * px + py], dst_ref=rbuf[a].at[j], send_sem=send_sems.at[a, j],
                recv_sem=recv_sems.at[a, j], device_id=(px, py, c), device_id_type=pl.DeviceIdType.MESH)

        def post(a, half):
            ref = gout[a].at[rows(a, half), :]
            return pltpu.make_async_remote_copy(
                src_ref=ref, dst_ref=ref, send_sem=post_send.at[a], recv_sem=post_recv.at[a],
                device_id=sibling, device_id_type=pl.DeviceIdType.MESH)

        def small_copy(kk):
            peer = (x ^ (kk >> 2), y ^ ((kk >> 1) & 1), c ^ (kk & 1))
            return pltpu.make_async_remote_copy(
                src_ref=sm_in, dst_ref=sbuf.at[kk], send_sem=sm_send.at[kk - 1], recv_sem=sm_recv.at[kk - 1],
                device_id=peer, device_id_type=pl.DeviceIdType.MESH)

        local = [[pltpu.make_async_copy(pin[a].at[blocks[k], rows(a, c), :], own[a].at[blocks[k]], loc_sems.at[a, k])
                  for k in range(4)] for a in range(n)]
        pres = [[pre(a, k) for k in range(4)] for a in range(n)]
        smalls = [small_copy(kk) for kk in range(1, 8)]
        for a in range(n):
            for k in range(4):
                local[a][k].start()
                pres[a][k].start()
        for cp in smalls:
            cp.start()
        sbuf[0] = sm_in[...]
        sends = []
        for a in range(n):
            for k in range(4):
                local[a][k].wait()
                pres[a][k].wait_recv()
                tot = own[a][blocks[k]] + sib[a][blocks[k]]
                own[a][blocks[k]] = tot
                if k < 3:
                    wire[a][blocks[k]] = tot.astype(BF16)
                    cp = ici(a, k)
                    cp.start()
                    sends.append(cp)
        for cp in smalls:
            cp.wait_recv()
        total = sbuf[me]
        for d in range(1, 8):
            total = total + sbuf[me ^ d]
        sm_out[...] = total
        posts = []
        for a in range(n):
            for j in range(3):
                ici(a, j).wait_recv()
            fin = own[a][mine]
            for j in range(3):
                fin = fin + rbuf[a][j].astype(F32)
            gout[a][rows(a, c), :] = fin
            cp = post(a, c)
            cp.start()
            posts.append(cp)
        for a in range(n):
            post(a, 1 - c).wait_recv()
        for cp in [cp for row in pres for cp in row] + sends + smalls + posts:
            cp.wait_send()

    vmem = pl.BlockSpec(memory_space=pltpu.VMEM)
    dma = pltpu.SemaphoreType.DMA
    return pl.pallas_call(
        body, name="reduce_grads",
        in_specs=[pl.BlockSpec(memory_space=pl.ANY)] * n + [vmem], out_specs=[vmem] * (n + 1),
        out_shape=[jax.ShapeDtypeStruct(sh, F32) for sh in shapes] + [jax.ShapeDtypeStruct(small.shape, F32)],
        scratch_shapes=([pltpu.VMEM((4,) + hs, F32) for hs in halves] + [pltpu.VMEM((4,) + hs, F32) for hs in halves]
                        + [pltpu.VMEM((4,) + hs, BF16) for hs in halves]
                        + [pltpu.VMEM((3,) + hs, BF16) for hs in halves]
                        + [pltpu.VMEM((8,) + small.shape, F32), dma((n, 3)), dma((n, 3)), dma((n, 4)),
                           dma((n, 4)), dma((n, 4)), dma((n,)), dma((n,)), dma((7,)), dma((7,))]),
        compiler_params=pltpu.CompilerParams(vmem_limit_bytes=VMEM_LIMIT),
    )(*parts, small)


def _adamw_update(w_ref, g_ref, m_ref, v_ref, d_ref, nm_ref, nv_ref):
    gv = g_ref[...]
    nm = ADAM_B1 * m_ref[...] + (1.0 - ADAM_B1) * gv
    nv = ADAM_B2 * v_ref[...] + (1.0 - ADAM_B2) * (gv * gv)
    m_hat = nm / (1.0 - ADAM_B1 ** ADAM_STEP)
    v_hat = nv / (1.0 - ADAM_B2 ** ADAM_STEP)
    d_ref[...] = -ADAM_LR * (m_hat / (jnp.sqrt(v_hat) + ADAM_EPS) + ADAM_WD * w_ref[...])
    nm_ref[...] = nm
    nv_ref[...] = nv


def _adamw_small(ws, gs, ms, vs):
    k = len(ws)

    def body(*refs):
        ins, outs = refs[:4 * k], refs[4 * k:]
        for a in range(k):
            _adamw_update(ins[a], ins[k + a], ins[2 * k + a], ins[3 * k + a], outs[a], outs[k + a], outs[2 * k + a])

    out = pl.pallas_call(
        body, name="adamw_small",
        out_shape=[jax.ShapeDtypeStruct(w.shape, F32) for w in ws] * 3,
        compiler_params=pltpu.CompilerParams(vmem_limit_bytes=VMEM_LIMIT),
    )(*ws, *gs, *ms, *vs)
    return out[:k], out[k:2 * k], out[2 * k:]


def _adamw(w, g, m, v, name):
    shape = w.shape
    w2, g2, m2, v2 = (a.reshape((-1, shape[-1])) for a in (w, g, m, v))

    def body(w_ref, g_ref, m_ref, v_ref, d_ref, nm_ref, nv_ref):
        _adamw_update(w_ref, g_ref, m_ref, v_ref, d_ref, nm_ref, nv_ref)

    rows, cols = w2.shape
    nblk = cols // 256 if cols % 256 == 0 and rows >= 64 else 1
    blk = pl.BlockSpec((rows, cols // nblk), lambda j: (0, j))
    out = pl.pallas_call(
        body, name=name, grid=(nblk,), in_specs=[blk] * 4, out_specs=[blk] * 3,
        out_shape=[jax.ShapeDtypeStruct(w2.shape, F32)] * 3,
        compiler_params=_cparams("parallel"),
    )(w2, g2, m2, v2)
    return tuple(a.reshape(shape) for a in out)


def kernel(x, meta_tokens, norm_g, w_in, q_norm_g, w_q_up, kv_norm_g, w_kv_up, conv_w, attn_out_g, conv_out_g, w_out, final_norm_g, loss_target, m_meta_tokens, m_norm_g, m_w_in, m_q_norm_g, m_w_q_up, m_kv_norm_g, m_w_kv_up, m_conv_w, m_attn_out_g, m_conv_out_g, m_w_out, m_final_norm_g, v_meta_tokens, v_norm_g, v_w_in, v_q_norm_g, v_w_q_up, v_kv_norm_g, v_w_kv_up, v_conv_w, v_attn_out_g, v_conv_out_g, v_w_out, v_final_norm_g):
    nb, s, _ = x.shape
    tm = min(ROW_TILE, s)
    ta = min(ATTN_TILE, s)
    assert s % tm == 0 and s % ta == 0 and tm % 16 == 0
    r = nb * s

    tr = lambda a: jnp.transpose(a[0])
    w_head, w_in_shard, wq_p, wkv_p, g_cw, g_meta = _gather_weights(
        tr(w_in), [tr(w_q_up), tr(w_kv_up)],
        [W_Q_PIECES, W_KV_PIECES], [HEADS * QK_PAD, 1024],
        [jnp.transpose(conv_w, (1, 0, 2)), meta_tokens],
        [(0, jnp.zeros((64, Q_RANK), BF16), QK_PAD * h + NOPE + ROPE) for h in range(HEADS)])
    conv_f = jnp.transpose(g_cw[:, :, 0, :], (1, 0, 2)).reshape(3, CONV_W)
    meta_f = jnp.transpose(g_meta, (1, 0, 2)).reshape(N_META, D_MODEL)

    c_all, sa_all, sb_all = _rope_tables(N_META + s)
    tabs_m = (c_all[:N_META], sa_all[:N_META], sb_all[:N_META])
    tabs = (c_all[N_META:], sa_all[N_META:], sb_all[N_META:])
    gid = np.arange(CONV_W) // CONV_GROUP
    gmat = jnp.asarray(np.where(gid[:, None] == gid[None, :], 1.0 / CONV_GROUP, 0.0), BF16)
    ga, gc = attn_out_g, conv_out_g
    gf = final_norm_g.reshape(1, D_MODEL)

    x2d = x.reshape(r, D_MODEL)
    tgt2d = loss_target.reshape(r, D_MODEL)

    ph, q, k, v, pmh, km, vm, w_out_f = _fwd_proj(x2d, meta_f, tabs, tabs_m, norm_g, w_head, q_norm_g, wq_p,
                                                  kv_norm_g, wkv_p, w_out[0].astype(BF16), nb, s, tm)
    o, lse, w_in_p = _attn_fwd(q, k, v, km, vm, w_in_shard, nb, s, ta)
    dh2, dycat, dw_out, dgf, loss_acc, pt, pmt = _out_fwd_bwd(x2d, tgt2d, o, meta_f, norm_g, w_in_p, conv_f, ga, gc,
                                                              gmat, w_out_f, gf, nb, s, tm)
    dpb, do, delta, dccm, dga, dgc, dcw = _gate_bwd(dycat, o, pt, pmt, conv_f, ga, gc, gmat, nb, s, tm)
    p_out = dw_out.reshape(4, D_MODEL // 4, D_MODEL)
    dq, dk, dv, dkm, dvm, g_w_out = _attn_bwd(q, k, v, do, lse, delta, km, vm, [p_out], nb, s, ta)
    dpa, dpam, p_q, p_kv, dgq, dgkv = _up_bwd(dq, dk, dv, dkm, dvm, ph, pmh, tabs, tabs_m, wq_p, wkv_p,
                                              q_norm_g, kv_norm_g, nb, s, tm)
    gx, gmeta, p_in, dng = _in_bwd(x2d, dh2, dpa, dpb, meta_f, dpam, dccm, pmt, w_in_p, norm_g, nb, s, tm)

    flat =jnp.concatenate([dng.reshape(-1), dgq.reshape(-1), dgkv.reshape(-1), dga.reshape(-1), dgc.reshape(-1),
                            dgf.reshape(-1), dcw[:3].reshape(-1), gmeta.reshape(-1), loss_acc[0, 0:1]])
    n_small = flat.shape[0]
    rows_small = -(-n_small // 1024) * 8
    small = jnp.pad(flat, (0, rows_small * 128 - n_small)).reshape(rows_small, 128)
    g_w_in_t, g_w_q_t, g_w_kv_t, small_sum = _reduce_grads([p_in, p_q, p_kv], small)
    ssum = small_sum.reshape(-1)

    def take(off, n):
        return ssum[off:off + n], off + n

    off = 0
    g_norm, off = take(off, D_MODEL)
    g_qn, off = take(off, Q_RANK)
    g_kvn, off = take(off, KV_RANK)
    g_ga, off = take(off, CONV_W)
    g_gc, off = take(off, CONV_W)
    g_gf, off = take(off, D_MODEL)
    g_cw_all, off = take(off, 3 * CONV_W)
    g_meta_all, off = take(off, N_META * D_MODEL)
    loss = ssum[off]
    chip = 2 * lax.axis_index("x") + lax.axis_index("y")
    g_conv = lax.dynamic_slice(g_cw_all.reshape(3, CONV_W), (0, chip * 128), (3, 128))
    g_mt = lax.dynamic_slice(g_meta_all.reshape(N_META, D_MODEL), (0, chip * 256), (N_META, 256))

    grads = {
        "meta_tokens": g_mt, "norm_g": g_norm.reshape(1, -1), "w_in": g_w_in_t, "q_norm_g": g_qn.reshape(1, -1),
        "w_q_up": g_w_q_t, "kv_norm_g": g_kvn.reshape(1, -1), "w_kv_up": jnp.transpose(g_w_kv_t)[None],
        "conv_w": g_conv[None], "attn_out_g": g_ga.reshape(1, -1), "conv_out_g": g_gc.reshape(1, -1),
        "w_out": g_w_out[None], "final_norm_g": g_gf,
    }
    transposed = ("w_in", "w_q_up")
    weights = {
        "meta_tokens": (meta_tokens, m_meta_tokens, v_meta_tokens), "norm_g": (norm_g, m_norm_g, v_norm_g),
        "w_in": (w_in, m_w_in, v_w_in), "q_norm_g": (q_norm_g, m_q_norm_g, v_q_norm_g),
        "w_q_up": (w_q_up, m_w_q_up, v_w_q_up), "kv_norm_g": (kv_norm_g, m_kv_norm_g, v_kv_norm_g),
        "w_kv_up": (w_kv_up, m_w_kv_up, v_w_kv_up), "conv_w": (conv_w, m_conv_w, v_conv_w),
        "attn_out_g": (attn_out_g, m_attn_out_g, v_attn_out_g), "conv_out_g": (conv_out_g, m_conv_out_g, v_conv_out_g),
        "w_out": (w_out, m_w_out, v_w_out), "final_norm_g": (final_norm_g, m_final_norm_g, v_final_norm_g),
    }
    names = list(weights)
    small = [nme for nme in names if nme != "w_in"]

    def view(nme, a):
        if nme in transposed:
            return a if a.ndim == 2 else tr(a)
        if nme == "conv_w":
            return jnp.transpose(a.reshape(1, 3, -1), (1, 0, 2))
        if a.ndim == 3:
            return a[0]
        return a.reshape(1, -1) if a.ndim == 1 else a

    def unview(nme, a):
        if nme in transposed:
            return jnp.transpose(a)[None]
        if nme == "conv_w":
            return jnp.transpose(a, (1, 0, 2))
        return a.reshape(weights[nme][0].shape)

    res_small = _adamw_small(*[[view(nme, a) for nme, a in zip(small, col)] for col in (
        [weights[nme][0] for nme in small], [grads[nme] for nme in small],
        [weights[nme][1] for nme in small], [weights[nme][2] for nme in small])])
    w_, m_, v_ = weights["w_in"]
    res = _adamw(tr(w_), grads["w_in"], tr(m_), tr(v_), "adamw_w_in")
    upd = {"w_in": tuple(jnp.transpose(a)[None] for a in (grads["w_in"],) + res)}
    for j, nme in enumerate(small):
        upd[nme] = (unview(nme, view(nme, grads[nme])),) + tuple(unview(nme, r[j]) for r in res_small)
    grads = {nme: upd[nme][0] for nme in names}
    deltas, new_m, new_v = ([upd[nme][j] for nme in names] for j in (1, 2, 3))

    grad_x = gx.reshape(nb, s, D_MODEL)
    return (loss, grad_x, *[grads[nme] for nme in names], *deltas, *new_m, *new_v)
```

```python
import functools

import jax
import jax.numpy as jnp
import numpy as np
from jax import lax
from jax.experimental import pallas as pl
from jax.experimental.pallas import tpu as pltpu

F32 = jnp.float32
BF16 = jnp.bfloat16

D_MODEL = 1024
N_META = 16
HEADS = 4
NOPE = 128
ROPE = 64
VDIM = 128
QK_PAD = 256
Q_RANK = 256
KV_RANK = 128
CONV_W = 512
CONV_GROUP = 64
ROPE_THETA = 10000.0
EPS = 1e-6
ATTN_SCALE = (NOPE + ROPE) ** -0.5
IN_DIM = 3008
IN_PAD = 3072
HEAD_ROWS = Q_RANK + KV_RANK + ROPE
IN_HEAD = 512
IN_TAIL = IN_PAD - IN_HEAD
BLK_ZA, BLK_CB, BLK_CC, BLK_CH, BLK_ZC = 0, 1, 2, 3, 4
NEG_INF = -1e30

ADAM_LR = 0.001
ADAM_B1 = 0.9
ADAM_B2 = 0.999
ADAM_EPS = 1e-08
ADAM_WD = 0.01
ADAM_STEP = 10

ROW_TILE = 512
ATTN_TILE = 256
VMEM_LIMIT = 56 * 1024 * 1024

NT = (((1,), (1,)), ((), ()))
TN = (((0,), (0,)), ((), ()))


def _cparams(*sem):
    return pltpu.CompilerParams(dimension_semantics=sem, vmem_limit_bytes=VMEM_LIMIT)


def _dot(a, b):
    return jnp.dot(a, b, preferred_element_type=F32)


def _dot_nt(a, b):
    return lax.dot_general(a, b, NT, preferred_element_type=F32)


def _dot_tn(a, b):
    return lax.dot_general(a, b, TN, preferred_element_type=F32)


def _rms(x, g):
    r = lax.rsqrt(jnp.mean(x * x, axis=-1, keepdims=True) + EPS)
    return x * r * g, r


def _rms_bwd(dy, x, r, g):
    xh = x * r
    dyg = dy * g
    dx = r * (dyg - xh * jnp.mean(dyg * xh, axis=-1, keepdims=True))
    return dx, dy * xh


def _sigmoid(z):
    return 1.0 / (1.0 + jnp.exp(-z))


def _rope(b, c, sa, sb):
    return b * c + pltpu.roll(b, 96, 1) * sa + pltpu.roll(b, 32, 1) * sb


def _rope_bwd(d, c, sa, sb):
    return d * c + pltpu.roll(d * sa, 32, 1) + pltpu.roll(d * sb, 96, 1)


def _group_mean(x, gmat):
    hi = x.astype(BF16)
    lo = (x - hi.astype(F32)).astype(BF16)
    return _dot(hi, gmat) + _dot(lo, gmat)


def _row_of(col, rows):
    return jnp.transpose(jnp.broadcast_to(col, (rows, 128)))[0:1, :]


def _rope_tables(n_pos):
    half = ROPE // 2
    inv_freq = (np.float32(1.0) / (np.float32(ROPE_THETA) ** (np.arange(half, dtype=np.float32) / np.float32(half))))
    ang = np.arange(n_pos, dtype=np.float32)[:, None] * inv_freq.astype(np.float32)[None, :]
    cos, sin = np.cos(ang).astype(np.float32), np.sin(ang).astype(np.float32)
    z = np.zeros((n_pos, half), np.float32)
    c = np.concatenate([cos, cos, z, z], axis=1)
    sa = np.concatenate([-sin, z, z, z], axis=1)
    sb = np.concatenate([z, sin, z, z], axis=1)
    return jnp.asarray(c), jnp.asarray(sa), jnp.asarray(sb)


W_IN_PIECES_1 = ((0, 128, 752, 0, 64, 0), (384, 64, 752, 384, 448, 1), (448, 64, 752, 512, 512, 1))
W_IN_PIECES_2 = ((128, 256, 752, 128, 192, 0), (512, 240, 752, 576, 576, 1))
W_Q_PIECES = ((0, 96, 256, 0, 0, 0), (96, 96, 256, 96, 96, 1))
W_KV_PIECES = ((0, 128, 128, 0, 0, 0), (128, 128, 128, 512, 512, 1))
W_OUT_PIECES = ((0, 128, 256, 0, 0, 0), (128, 128, 256, 128, 128, 1))


class _StagedGather:
    def __init__(self, pieces, zero_rows=None):
        self.pieces = pieces
        self.zero_rows = zero_rows

    def scratch(self):
        nk, dma = len(self.pieces), pltpu.SemaphoreType.DMA
        return [dma((nk, 3)), dma((nk, 3)), dma((nk, 3)), dma((nk, 3)), dma((nk,))]

    def vmem_scratch(self, shard_shape, out_shape):
        return [pltpu.VMEM(shard_shape, BF16), pltpu.VMEM(out_shape, BF16),
                pltpu.SemaphoreType.DMA((4 * len(self.pieces) + 2,))] + self.scratch()

    def run_vmem(self, stage, shard_ref, out_ref, scr):
        src_scr, land_scr, io_sems = scr[:3]
        spans = []
        for _, nr, per, first, rest, _ in self.pieces:
            spans += [(per * q + (first if q == 0 else rest), nr) for q in range(4)]
        if self.zero_rows is not None:
            spans.append(self.zero_rows)
        flush = [pltpu.make_async_copy(land_scr.at[r0:r0 + nr], out_ref.at[r0:r0 + nr], io_sems.at[n])
                 for n, (r0, nr) in enumerate(spans)]
        if stage == 0:
            load = pltpu.make_async_copy(shard_ref, src_scr, io_sems.at[len(spans)])
            load.start()
            if self.zero_rows is not None:
                r0, nr = self.zero_rows
                land_scr[r0:r0 + nr, :] = jnp.zeros((nr, land_scr.shape[1]), BF16)
            load.wait()
        if stage < 3:
            self.run(stage, src_scr, land_scr, scr[3:])
        for cp in flush:
            if stage == 2:
                cp.start()
            if stage == 3:
                cp.wait()

    def run(self, stage, src_ref, out_ref, scr):
        send_sems, recv_sems, fwd_send, fwd_recv, loc_sems = scr
        pieces = self.pieces
        nk = len(pieces)
        x, y, c = lax.axis_index("x"), lax.axis_index("y"), lax.axis_index("c")
        mine = 2 * x + y
        chips = [(1 - x, y), (x, 1 - y), (1 - x, 1 - y)]
        chip_of = [2 * px + py for px, py in chips]
        mesh = pl.DeviceIdType.MESH

        def src(k):
            s0, nr = pieces[k][0], pieces[k][1]
            return src_ref.at[s0:s0 + nr]

        def dst(k, q):
            _, nr, per, first, rest, _ = pieces[k]
            row = per * q + first + (rest - first) * jnp.minimum(q, 1)
            return out_ref.at[pl.ds(pl.multiple_of(row, 16), nr)]

        def ici(k, j, q):
            px, py = chips[j]
            return pltpu.make_async_remote_copy(
                src_ref=src(k), dst_ref=dst(k, q), send_sem=send_sems.at[k, j], recv_sem=recv_sems.at[k, j],
                device_id=(px, py, c), device_id_type=mesh)

        def fwd(k, j):
            ref = dst(k, chip_of[j])
            return pltpu.make_async_remote_copy(
                src_ref=ref, dst_ref=ref, send_sem=fwd_send.at[k, j], recv_sem=fwd_recv.at[k, j],
                device_id=(x, y, 1 - c), device_id_type=mesh)

        local = [pltpu.make_async_copy(src(k), dst(k, mine), loc_sems.at[k]) for k in range(nk)]
        if stage == 0:
            for cp in local:
                cp.start()
        if stage == 2:
            for cp in local:
                cp.wait()
        for half in (0, 1):
            @pl.when(c == half)
            def _(half=half):
                my_k = [k for k in range(nk) if pieces[k][5] == half]
                other_k = [k for k in range(nk) if pieces[k][5] != half]
                for k in my_k:
                    for j in range(3):
                        if stage == 0:
                            ici(k, j, mine).start()
                        elif stage == 1:
                            ici(k, j, chip_of[j]).wait_recv()
                            fwd(k, j).start()
                        else:
                            ici(k, j, mine).wait_send()
                            fwd(k, j).wait_send()
                if stage == 2:
                    for k in other_k:
                        for j in range(3):
                            fwd(k, j).wait_recv()


def _fwd_proj(x2d, meta, tabs, tabs_m, norm_g, w_head, q_norm_g, wq_p, kv_norm_g, wkv_p, w_out_shard, w_in_shard,
              nb, s, tm):
    nt = s // tm
    n = nb * nt
    n_steps = n + 1
    c_t, sa_t, sb_t = tabs
    cm_t, sam_t, sbm_t = tabs_m
    gat = _StagedGather(W_OUT_PIECES)
    gat_in = _StagedGather(W_IN_PIECES_1)
    n_sems = len(gat.scratch())
    assert n_steps >= 3

    def body(x_ref, c_ref, sa_ref, sb_ref, mt_ref, cm_ref, sam_ref, sbm_ref,
             g_ref, w_ref, gq_ref, wq_ref, gkv_ref, wkv_ref, wos_ref, wis_ref,
             p_ref, q_ref, k_ref, v_ref, pm_ref, km_ref, vm_ref, wo_ref, wi_ref, *scr):
        gat_scr, gat_in_scr = scr[:n_sems], scr[n_sems:]
        i = pl.program_id(0)
        for stage, at in enumerate((0, n_steps - 2, n_steps - 1)):
            @pl.when(i == at)
            def _(stage=stage):
                gat_in.run_vmem(stage, wis_ref, wi_ref, gat_in_scr)
                gat.run(stage, wos_ref, wo_ref, gat_scr)

        def project(xv, c, sa, sb, p_out, q_out, k_out, v_out):
            u, _ = _rms(xv, g_ref[...])
            p = _dot_nt(u.astype(BF16), w_ref[...])
            p_out[...] = p
            qn, _ = _rms(p[:, 0:Q_RANK], gq_ref[...])
            q = _dot_nt(qn.astype(BF16), wq_ref[...])
            kvn, _ = _rms(p[:, Q_RANK:Q_RANK + KV_RANK], gkv_ref[...])
            kv = _dot_nt(kvn.astype(BF16), wkv_ref[...])
            kpe = _rope(p[:, 384:512], c, sa, sb)
            for h in range(HEADS):
                if q_out is not None:
                    pe = _rope(q[:, QK_PAD * h + NOPE:QK_PAD * (h + 1)], c, sa, sb)
                    qh = jnp.concatenate([q[:, QK_PAD * h:QK_PAD * h + NOPE], pe], axis=1)
                    q_out[0, h] = (qh * ATTN_SCALE).astype(BF16)
                k_out[0, h] = jnp.concatenate([kv[:, NOPE * h:NOPE * (h + 1)], kpe], axis=1).astype(BF16)
                v_out[0, h] = kv[:, 512 + VDIM * h:512 + VDIM * (h + 1)].astype(BF16)

        @pl.when(i < n)
        def _():
            project(x_ref[...], c_ref[...], sa_ref[...], sb_ref[...], p_ref, q_ref, k_ref, v_ref)

        @pl.when(i == n)
        def _():
            project(mt_ref[...], cm_ref[...], sam_ref[...], sbm_ref[...], pm_ref, None, km_ref, vm_ref)
            gat_in.run_vmem(3, wis_ref, wi_ref, gat_in_scr)

    cl = lambda i: jnp.minimum(i, n - 1)
    full = lambda a: pl.BlockSpec(a.shape, lambda i: (0,) * a.ndim)
    const = lambda shape: pl.BlockSpec(shape, lambda i: (0,) * len(shape))
    tab = pl.BlockSpec((tm, 128), lambda i: (cl(i) % nt, 0))
    hb = lambda w: pl.BlockSpec((1, HEADS, tm, w), lambda i: (cl(i) // nt, 0, cl(i) % nt, 0))
    whole = pl.BlockSpec(memory_space=pl.ANY)
    return pl.pallas_call(
        body, name="fwd_proj", grid=(n_steps,),
        in_specs=[pl.BlockSpec((tm, D_MODEL), lambda i: (cl(i), 0)), tab, tab, tab,
                  full(meta), full(cm_t), full(sam_t), full(sbm_t),
                  full(norm_g), full(w_head), full(q_norm_g), full(wq_p), full(kv_norm_g), full(wkv_p), whole, whole],
        out_specs=[pl.BlockSpec((tm, IN_HEAD), lambda i: (cl(i), 0)), hb(QK_PAD), hb(QK_PAD), hb(VDIM),
                   const((N_META, IN_HEAD)), const((1, HEADS, N_META, QK_PAD)), const((1, HEADS, N_META, VDIM)),
                   whole, whole],
        out_shape=[jax.ShapeDtypeStruct((nb * s, IN_HEAD), F32),
                   jax.ShapeDtypeStruct((nb, HEADS, s, QK_PAD), BF16),
                   jax.ShapeDtypeStruct((nb, HEADS, s, QK_PAD), BF16),
                   jax.ShapeDtypeStruct((nb, HEADS, s, VDIM), BF16),
                   jax.ShapeDtypeStruct((N_META, IN_HEAD), F32),
                   jax.ShapeDtypeStruct((1, HEADS, N_META, QK_PAD), BF16),
                   jax.ShapeDtypeStruct((1, HEADS, N_META, VDIM), BF16),
                   jax.ShapeDtypeStruct((D_MODEL, D_MODEL), BF16),
                   jax.ShapeDtypeStruct((IN_PAD, D_MODEL), BF16)],
        scratch_shapes=gat.scratch() + gat_in.vmem_scratch(w_in_shard.shape, (IN_PAD, D_MODEL)),
        compiler_params=_cparams("arbitrary"),
    )(x2d, c_t, sa_t, sb_t, meta, cm_t, sam_t, sbm_t, norm_g, w_head, q_norm_g, wq_p, kv_norm_g, wkv_p, w_out_shard,
      w_in_shard)


def _attn_fwd(q, k, v, km, vm, w_in_shard, w_in_part, nb, s, tq):
    nq = s // tq
    n_steps = nb * HEADS
    gat = _StagedGather(W_IN_PIECES_2, zero_rows=(HEAD_ROWS, IN_HEAD - HEAD_ROWS))
    assert n_steps >= 3

    def body(q_ref, k_ref, v_ref, km_ref, vm_ref, ws_ref, _, o_ref, lse_ref, w_ref, s_scr, p_scr, *gat_scr):
        step = pl.program_id(0) * HEADS + pl.program_id(1)
        for stage, at in enumerate((0, n_steps - 2, n_steps - 1)):
            @pl.when(step == at)
            def _(stage=stage):
                gat.run_vmem(stage, ws_ref, w_ref, gat_scr)

        row = lax.broadcasted_iota(jnp.int32, (tq, tq), 0)
        col = lax.broadcasted_iota(jnp.int32, (tq, tq), 1)
        def scores(i):
            slot = i % 2
            qi = q_ref[0, 0, i * tq:(i + 1) * tq, :]
            sm = _dot_nt(qi, km_ref[0, 0])
            m128 = None
            for j in range(i + 1):
                sc = _dot_nt(qi, k_ref[0, 0, j * tq:(j + 1) * tq, :])
                if j == i:
                    sc = jnp.where(col <= row, sc, NEG_INF)
                s_scr[slot, :, j * tq:(j + 1) * tq] = sc
                mx = sc[:, 0:128]
                for c0 in range(128, tq, 128):
                    mx = jnp.maximum(mx, sc[:, c0:c0 + 128])
                m128 = mx if m128 is None else jnp.maximum(m128, mx)
            return sm, jnp.maximum(jnp.max(m128, axis=1, keepdims=True), jnp.max(sm, axis=1, keepdims=True))

        def weighted_sum(i, pm, l):
            n = (i + 1) * tq
            acc = _dot(p_scr[i % 2, :, 0:n], v_ref[0, 0, 0:n, :]) + _dot(pm.astype(BF16), vm_ref[0, 0])
            o_ref[0, 0, i * tq:(i + 1) * tq, :] = acc / l

        nxt, pending = scores(0), None
        for i in range(nq):
            slot = i % 2
            sm, m = nxt
            if i + 1 < nq:
                nxt = scores(i + 1)
            pm = jnp.exp(sm - m)
            l128 = None
            for j in range(i + 1):
                p = jnp.exp(s_scr[slot, :, j * tq:(j + 1) * tq] - m)
                p_scr[slot, :, j * tq:(j + 1) * tq] = p.astype(BF16)
                ps = p[:, 0:128]
                for c0 in range(128, tq, 128):
                    ps = ps + p[:, c0:c0 + 128]
                l128 = ps if l128 is None else l128 + ps
            l = jnp.sum(l128, axis=1, keepdims=True) + jnp.sum(pm, axis=1, keepdims=True)
            lse_ref[0, 0, :, i * tq:(i + 1) * tq] = _row_of(m + jnp.log(l), tq)
            if pending is not None:
                weighted_sum(*pending)
            pending = (i, pm, l)
        weighted_sum(*pending)

        @pl.when(step == n_steps - 1)
        def _():
            gat.run_vmem(3, ws_ref, w_ref, gat_scr)

    hblk = lambda w: pl.BlockSpec((1, 1, s, w), lambda b, h: (b, h, 0, 0))
    mblk = lambda w: pl.BlockSpec((1, 1, N_META, w), lambda b, h: (0, h, 0, 0))
    whole = pl.BlockSpec(memory_space=pl.ANY)
    return pl.pallas_call(
        body, name="attn_fwd", grid=(nb, HEADS),
        in_specs=[hblk(QK_PAD), hblk(QK_PAD), hblk(VDIM), mblk(QK_PAD), mblk(VDIM), whole, whole],
        out_specs=[hblk(VDIM), pl.BlockSpec((1, 1, 1, s), lambda b, h: (b, h, 0, 0)), whole],
        out_shape=[jax.ShapeDtypeStruct((nb, HEADS, s, VDIM), F32),
                   jax.ShapeDtypeStruct((nb, HEADS, 1, s), F32),
                   jax.ShapeDtypeStruct(w_in_part.shape, BF16)],
        input_output_aliases={6: 2},
        scratch_shapes=[pltpu.VMEM((2, tq, s), F32), pltpu.VMEM((2, tq, s), BF16)]
        + gat.vmem_scratch(w_in_shard.shape, w_in_part.shape),
        compiler_params=_cparams("arbitrary", "arbitrary"),
    )(q, k, v, km, vm, w_in_shard, w_in_part)


def _shift_rows(a, prev, n_rows):
    rid = lax.broadcasted_iota(jnp.int32, a.shape, 0)
    a1 = jnp.where(rid == 0, prev[7:8, :], pltpu.roll(a, 1, 0))
    a2 = jnp.where(rid == 0, prev[6:7, :], jnp.where(rid == 1, prev[7:8, :], pltpu.roll(a, 2, 0)))
    return a1, a2


def _attn_gate(o, za, ga_h):
    on, r = _rms(o, ga_h)
    return on * (za * _sigmoid(za)), on, r


def _out_fwd_bwd(x2d, tgt2d, o, meta, norm_g, w_in_p, conv_w, ga, gc, gmat, w_out, gf, nb, s, tm):
    nt = s // tm
    r = nb * s

    def body(x_ref, t_ref, o_ref, mt_ref, g_ref, wi_ref, cw_ref, ga_ref, gc_ref, gm_ref, w_ref, gf_ref,
             dh_ref, dy_ref, dw_ref, dgf_ref, loss_ref, p_ref, pm_ref, last_cc):
        i = pl.program_id(0)
        blk = lambda ref, j, rows=slice(None): ref[rows, 512 * j:512 * (j + 1)]

        def tail(xv):
            u, _ = _rms(xv, g_ref[...])
            return _dot_nt(u.astype(BF16), wi_ref[IN_HEAD:IN_PAD, :])

        @pl.when(i == 0)
        def _():
            dw_ref[...] = jnp.zeros_like(dw_ref)
            dgf_ref[...] = jnp.zeros_like(dgf_ref)
            loss_ref[...] = jnp.zeros_like(loss_ref)
            last_cc[...] = jnp.zeros_like(last_cc)
            pm_ref[...] = tail(mt_ref[...])

        p_ref[...] = tail(x_ref[...])
        ya = []
        for h in range(HEADS):
            y, _, _ = _attn_gate(o_ref[0, h], p_ref[:, 512 * BLK_ZA + VDIM * h:512 * BLK_ZA + VDIM * (h + 1)],
                                 ga_ref[:, VDIM * h:VDIM * (h + 1)])
            ya.append(y)
        cc = blk(p_ref, BLK_CC) * blk(p_ref, BLK_CH)
        meta_cc = blk(pm_ref, BLK_CC, slice(8, 16)) * blk(pm_ref, BLK_CH, slice(8, 16))
        prev = jnp.where(i % nt == 0, meta_cc, last_cc[...])
        last_cc[...] = cc[tm - 8:tm, :]
        cc1, cc2 = _shift_rows(cc, prev, tm)
        yc = blk(p_ref, BLK_CB) * (cw_ref[0:1, :] * cc2 + cw_ref[1:2, :] * cc1 + cw_ref[2:3, :] * cc)
        rg = lax.rsqrt(_group_mean(yc * yc, gm_ref[...]) + EPS)
        zc = blk(p_ref, BLK_ZC)
        yconv = yc * rg * gc_ref[...] * (zc * _sigmoid(zc))
        ycat = jnp.concatenate(ya + [yconv], axis=1).astype(BF16)
        h2 = x_ref[...] + _dot(ycat, w_ref[...])
        gfv = gf_ref[...]
        y, r2 = _rms(h2, gfv)
        e = y - t_ref[...]
        loss_ref[...] += 0.5 * jnp.sum(e * e) / D_MODEL
        dyv = e * (1.0 / D_MODEL)
        dh2, dgf = _rms_bwd(dyv, h2, r2, gfv)
        dgf_ref[...] += jnp.sum(dgf, axis=0, keepdims=True)
        dh_ref[...] = dh2
        dhb = dh2.astype(BF16)
        dy_ref[...] = _dot_nt(dhb, w_ref[...])
        dw_ref[...] += _dot_tn(ycat, dhb)

    row = lambda w: pl.BlockSpec((tm, w), lambda i: (i, 0))
    const = lambda shape: pl.BlockSpec(shape, lambda i: (0,) * len(shape))
    full = lambda a: const(a.shape)
    return pl.pallas_call(
        body, name="out_fwd_bwd", grid=(nb * nt,),
        in_specs=[row(D_MODEL), row(D_MODEL),
                  pl.BlockSpec((1, HEADS, tm, VDIM), lambda i: (i // nt, 0, i % nt, 0)),
                  full(meta), full(norm_g), full(w_in_p),
                  full(conv_w), full(ga), full(gc), full(gmat), full(w_out), full(gf)],
        out_specs=[row(D_MODEL), row(D_MODEL), const((D_MODEL, D_MODEL)), const((1, D_MODEL)), const((1, 128)),
                   row(IN_TAIL), const((N_META, IN_TAIL))],
        out_shape=[jax.ShapeDtypeStruct((r, D_MODEL), F32), jax.ShapeDtypeStruct((r, D_MODEL), F32),
                   jax.ShapeDtypeStruct((D_MODEL, D_MODEL), F32), jax.ShapeDtypeStruct((1, D_MODEL), F32),
                   jax.ShapeDtypeStruct((1, 128), F32),
                   jax.ShapeDtypeStruct((r, IN_TAIL), F32), jax.ShapeDtypeStruct((N_META, IN_TAIL), F32)],
        scratch_shapes=[pltpu.VMEM((8, 512), F32)],
        compiler_params=_cparams("arbitrary"),
    )(x2d, tgt2d, o, meta, norm_g, w_in_p, conv_w, ga, gc, gmat, w_out, gf)


def _gate_bwd(dycat, o, p, pm, conv_w, ga, gc, gmat, nb, s, tm):
    nt = s // tm
    r = nb * s
    ext = tm + 8
    prev_idx = lambda i: jnp.maximum(i * (tm // 8) - 1, 0)
    next_idx = lambda i: jnp.minimum((i + 1) * (tm // 8), r // 8 - 1)

    def body(dya_ref, dyc_ref, dycn_ref, o_ref, za_ref, cb_ref, cbn_ref, cc_ref, ccp_ref, ccn_ref,
             ch_ref, chp_ref, chn_ref, zc_ref, zcn_ref, mc_ref, mh_ref, cw_ref, ga_ref, gc_ref, gm_ref,
             dpb_ref, do_ref, dl_ref, dccm_ref, dga_ref, dgc_ref, dcw_ref):
        i = pl.program_id(0)

        @pl.when(i == 0)
        def _():
            dga_ref[...] = jnp.zeros_like(dga_ref)
            dgc_ref[...] = jnp.zeros_like(dgc_ref)
            dcw_ref[...] = jnp.zeros_like(dcw_ref)

        dga = []
        for h in range(HEADS):
            hs = slice(VDIM * h, VDIM * (h + 1))
            oh, za, gah, dya = o_ref[0, h], za_ref[:, hs], ga_ref[:, hs], dya_ref[:, hs]
            sg = _sigmoid(za)
            on, ro = _rms(oh, gah)
            don = dya * (za * sg)
            dpb_ref[:, hs] = (dya * on * (sg * (1.0 + za * (1.0 - sg)))).astype(BF16)
            do, dg = _rms_bwd(don, oh, ro, gah)
            dga.append(jnp.sum(dg, axis=0, keepdims=True))
            dob = do.astype(BF16)
            do_ref[0, h] = dob
            dl_ref[0, h] = _row_of(jnp.sum(dob.astype(F32) * oh, axis=1, keepdims=True), tm)
        dga_ref[...] += jnp.concatenate(dga, axis=1)

        cat = lambda a, b: jnp.concatenate([a[...], b[...]], axis=0)
        cch = cat(cc_ref, ccn_ref)
        chh = cat(ch_ref, chn_ref)
        cb = cat(cb_ref, cbn_ref)
        zc = cat(zc_ref, zcn_ref)
        dy = cat(dyc_ref, dycn_ref)
        first = i % nt == 0
        last = i % nt == nt - 1
        cc = cch * chh
        prev = jnp.where(first, mc_ref[8:16, :] * mh_ref[8:16, :], ccp_ref[...] * chp_ref[...])
        cc1, cc2 = _shift_rows(cc, prev, ext)
        w0, w1, w2 = cw_ref[0:1, :], cw_ref[1:2, :], cw_ref[2:3, :]
        dw = w0 * cc2 + w1 * cc1 + w2 * cc
        yc = cb * dw
        rg = lax.rsqrt(_group_mean(yc * yc, gm_ref[...]) + EPS)
        ych = yc * rg
        gcv = gc_ref[...]
        sg = _sigmoid(zc)
        dycn = dy * (zc * sg)
        dzc = dy * (ych * gcv) * (sg * (1.0 + zc * (1.0 - sg)))
        dgc_ref[...] += jnp.sum((dycn * ych)[:tm], axis=0, keepdims=True)
        dycg = dycn * gcv
        dyc = rg * (dycg - ych * _group_mean(dycg * ych, gm_ref[...]))
        rid = lax.broadcasted_iota(jnp.int32, (ext, CONV_W), 0)
        ddw = jnp.where(jnp.logical_and(last, rid >= tm), 0.0, dyc * cb)
        dcb = dyc * dw
        dcc = w2 * ddw + w1 * pltpu.roll(ddw, ext - 1, 0) + w0 * pltpu.roll(ddw, ext - 2, 0)
        dpb_ref[:, 512:1024] = dcb[:tm].astype(BF16)
        dpb_ref[:, 1024:1536] = (dcc * chh)[:tm].astype(BF16)
        dpb_ref[:, 1536:2048] = (dcc * cch)[:tm].astype(BF16)
        dpb_ref[:, 2048:2560] = dzc[:tm].astype(BF16)
        rs = lambda a: jnp.sum(a[:tm], axis=0, keepdims=True)
        dcw_ref[0:1, :] += rs(ddw * cc2)
        dcw_ref[1:2, :] += rs(ddw * cc1)
        dcw_ref[2:3, :] += rs(ddw * cc)

        @pl.when(first)
        def _():
            d0, d1 = ddw[0:1, :], ddw[1:2, :]
            r8 = lax.broadcasted_iota(jnp.int32, (8, CONV_W), 0)
            dccm_ref[0] = jnp.where(r8 == 7, w1 * d0 + w0 * d1, jnp.where(r8 == 6, w0 * d0, 0.0))

    row = lambda j: pl.BlockSpec((tm, 512), lambda i: (i, j))
    prv = lambda j: pl.BlockSpec((8, 512), lambda i: (prev_idx(i), j))
    nxt = lambda j: pl.BlockSpec((8, 512), lambda i: (next_idx(i), j))
    mblk = lambda j: pl.BlockSpec((N_META, 512), lambda i: (0, j))
    full = lambda a: pl.BlockSpec(a.shape, lambda i: (0,) * a.ndim)
    hb = lambda w: pl.BlockSpec((1, HEADS, tm, w), lambda i: (i // nt, 0, i % nt, 0))
    acc = lambda rr: pl.BlockSpec((rr, 512), lambda i: (0, 0))
    return pl.pallas_call(
        body, name="gate_bwd", grid=(nb * nt,),
        in_specs=[row(0), row(1), nxt(1), hb(VDIM),
                  row(BLK_ZA), row(BLK_CB), nxt(BLK_CB), row(BLK_CC), prv(BLK_CC), nxt(BLK_CC),
                  row(BLK_CH), prv(BLK_CH), nxt(BLK_CH), row(BLK_ZC), nxt(BLK_ZC),
                  mblk(BLK_CC), mblk(BLK_CH), full(conv_w), full(ga), full(gc), full(gmat)],
        out_specs=[pl.BlockSpec((tm, 2560), lambda i: (i, 0)), hb(VDIM),
                   pl.BlockSpec((1, HEADS, 1, tm), lambda i: (i // nt, 0, 0, i % nt)),
                   pl.BlockSpec((1, 8, 512), lambda i: (i // nt, 0, 0)),
                   acc(1), acc(1), acc(8)],
        out_shape=[jax.ShapeDtypeStruct((r, 2560), BF16), jax.ShapeDtypeStruct((nb, HEADS, s, VDIM), BF16),
                   jax.ShapeDtypeStruct((nb, HEADS, 1, s), F32), jax.ShapeDtypeStruct((nb, 8, 512), F32),
                   jax.ShapeDtypeStruct((1, 512), F32), jax.ShapeDtypeStruct((1, 512), F32),
                   jax.ShapeDtypeStruct((8, 512), F32)],
        compiler_params=_cparams("arbitrary"),
    )(dycat, dycat, dycat, o, p, p, p, p, p, p, p, p, p, p, p, pm, pm, conv_w, ga, gc, gmat)


class _StagedReduce:
    LOC, PRE_S, PRE_R, ICI_S, ICI_R, POST_S, POST_R, OUT, N_SEM = 0, 1, 2, 3, 6, 9, 10, 11, 12

    def __init__(self, shard_shape):
        self.half = (shard_shape[0] // 2, shard_shape[1])

    def scratch(self):
        h = self.half
        return [pltpu.VMEM((4,) + h, F32), pltpu.VMEM((4,) + h, F32), pltpu.VMEM((4,) + h, BF16),
                pltpu.VMEM((3,) + h, BF16), pltpu.VMEM(h, F32), pltpu.SemaphoreType.DMA((self.N_SEM,))]

    def run(self, stage, pin, gout, scr):
        own, sib, wire, rbuf, fin, sems = scr
        r2 = self.half[0]
        x, y, c = lax.axis_index("x"), lax.axis_index("y"), lax.axis_index("c")
        mine = 2 * x + y
        sibling = (x, y, 1 - c)
        chips = [(1 - x, y), (x, 1 - y), (1 - x, 1 - y)]
        rows = lambda half: pl.ds(pl.multiple_of(half * r2, r2), r2)
        mesh = pl.DeviceIdType.MESH

        loc = pltpu.make_async_copy(pin.at[:, rows(c), :], own, sems.at[self.LOC])
        pre = pltpu.make_async_remote_copy(
            src_ref=pin.at[:, rows(1 - c), :], dst_ref=sib, send_sem=sems.at[self.PRE_S],
            recv_sem=sems.at[self.PRE_R], device_id=sibling, device_id_type=mesh)

        def ici(j):
            px, py = chips[j]
            return pltpu.make_async_remote_copy(
                src_ref=wire.at[2 * px + py], dst_ref=rbuf.at[j], send_sem=sems.at[self.ICI_S + j],
                recv_sem=sems.at[self.ICI_R + j], device_id=(px, py, c), device_id_type=mesh)

        def post(half):
            return pltpu.make_async_remote_copy(
                src_ref=fin, dst_ref=gout.at[rows(half), :], send_sem=sems.at[self.POST_S],
                recv_sem=sems.at[self.POST_R], device_id=sibling, device_id_type=mesh)

        keep = pltpu.make_async_copy(fin, gout.at[rows(c), :], sems.at[self.OUT])
        if stage == 0:
            loc.start()
            pre.start()
        elif stage == 1:
            loc.wait()
            pre.wait_recv()
            for blk in range(4):
                tot = own[blk] + sib[blk]
                own[blk] = tot
                wire[blk] = tot.astype(BF16)
            for j in range(3):
                ici(j).start()
        elif stage == 2:
            for j in range(3):
                ici(j).wait_recv()
            tot = own[mine]
            for j in range(3):
                tot = tot + rbuf[j].astype(F32)
            fin[...] = tot
            post(c).start()
            keep.start()
        else:
            post(1 - c).wait_recv()
            pre.wait_send()
            for j in range(3):
                ici(j).wait_send()
            post(c).wait_send()
            keep.wait()


def _attn_bwd(q, k, v, do, lse, delta, km, vm, early, nb, s, t):
    n = s // t
    ne = len(early)
    reds = [_StagedReduce(a.shape[1:]) for a in early]
    n_steps = HEADS * nb
    assert n_steps >= 4

    def body(q_ref, k_ref, v_ref, do_ref, lse_ref, dl_ref, km_ref, vm_ref, *rest):
        pin_refs, rest = rest[:ne], rest[ne:]
        dq_ref, dk_ref, dv_ref, dkm_ref, dvm_ref = rest[:5]
        gout_refs, (p_scr, ds_scr, dq_acc), red_scr = rest[5:5 + ne], rest[5 + ne:8 + ne], rest[8 + ne:]
        b = pl.program_id(1)
        step = pl.program_id(0) * nb + b
        for stage, at in enumerate((0, 1, n_steps - 2, n_steps - 1)):
            @pl.when(step == at)
            def _(stage=stage):
                for a, red in enumerate(reds):
                    red.run(stage, pin_refs[a], gout_refs[a], red_scr[6 * a:6 * a + 6])

        @pl.when(b == 0)
        def _():
            dkm_ref[...] = jnp.zeros_like(dkm_ref)
            dvm_ref[...] = jnp.zeros_like(dvm_ref)

        kr = lax.broadcasted_iota(jnp.int32, (t, t), 0)
        qc = lax.broadcasted_iota(jnp.int32, (t, t), 1)
        km_v, vm_v = km_ref[0, 0], vm_ref[0, 0]
        ptm = jnp.exp(_dot_nt(km_v, q_ref[0, 0]) - lse_ref[0, 0])
        dstm = (ptm * (_dot_nt(vm_v, do_ref[0, 0]) - dl_ref[0, 0])).astype(BF16)
        dkm_ref[0] += _dot(dstm, q_ref[0, 0])
        dvm_ref[0] += _dot(ptm.astype(BF16), do_ref[0, 0])
        dq_acc[...] = _dot_tn(dstm, km_v)
        def tiles(j):
            slot = j % 2
            kj = k_ref[0, 0, j * t:(j + 1) * t, :]
            vj = v_ref[0, 0, j * t:(j + 1) * t, :]
            def products(i):
                cs = slice(i * t, (i + 1) * t)
                return _dot_nt(kj, q_ref[0, 0, cs, :]), _dot_nt(vj, do_ref[0, 0, cs, :])

            nxt, pending = products(j), None
            for i in range(j, n):
                cs = slice(i * t, (i + 1) * t)
                st, dpt = nxt
                if i + 1 < n:
                    nxt = products(i + 1)
                if i == j:
                    st = jnp.where(kr <= qc, st, NEG_INF)
                pt = jnp.exp(st - lse_ref[0, 0, :, cs])
                dst = (pt * (dpt - dl_ref[0, 0, :, cs])).astype(BF16)
                p_scr[slot, :, cs] = pt.astype(BF16)
                ds_scr[slot, :, cs] = dst
                if pending is not None:
                    dq_acc[pending[0], :] += _dot_tn(pending[1], kj)
                pending = (cs, dst)
            dq_acc[pending[0], :] += _dot_tn(pending[1], kj)

        for j in range(n):
            slot = j % 2
            tiles(j)
            dv_ref[0, 0, j * t:(j + 1) * t, :] = _dot(p_scr[slot, :, j * t:s], do_ref[0, 0, j * t:s, :]).astype(BF16)
            dk_ref[0, 0, j * t:(j + 1) * t, :] = _dot(ds_scr[slot, :, j * t:s], q_ref[0, 0, j * t:s, :]).astype(BF16)
        dq_ref[0, 0] = dq_acc[...].astype(BF16)

    big = lambda w: pl.BlockSpec((1, 1, s, w), lambda h, b: (b, h, 0, 0))
    rowv = pl.BlockSpec((1, 1, 1, s), lambda h, b: (b, h, 0, 0))
    mk = lambda w: pl.BlockSpec((1, 1, N_META, w), lambda h, b: (0, h, 0, 0))
    mo = lambda w: pl.BlockSpec((1, N_META, w), lambda h, b: (h, 0, 0))
    return pl.pallas_call(
        body, name="attn_bwd", grid=(HEADS, nb),
        in_specs=[big(QK_PAD), big(QK_PAD), big(VDIM), big(VDIM), rowv, rowv, mk(QK_PAD), mk(VDIM)]
        + [pl.BlockSpec(memory_space=pl.ANY)] * ne,
        out_specs=[big(QK_PAD), big(QK_PAD), big(VDIM), mo(QK_PAD), mo(VDIM)]
        + [pl.BlockSpec(memory_space=pl.ANY)] * ne,
        out_shape=[jax.ShapeDtypeStruct((nb, HEADS, s, QK_PAD), BF16),
                   jax.ShapeDtypeStruct((nb, HEADS, s, QK_PAD), BF16),
                   jax.ShapeDtypeStruct((nb, HEADS, s, VDIM), BF16),
                   jax.ShapeDtypeStruct((HEADS, N_META, QK_PAD), F32),
                   jax.ShapeDtypeStruct((HEADS, N_META, VDIM), F32)]
        + [jax.ShapeDtypeStruct(a.shape[1:], F32) for a in early],
        scratch_shapes=[pltpu.VMEM((2, t, s), BF16), pltpu.VMEM((2, t, s), BF16), pltpu.VMEM((s, QK_PAD), F32)]
        + [sc for red in reds for sc in red.scratch()],
        compiler_params=_cparams("arbitrary", "arbitrary"),
    )(q, k, v, do, lse, delta, km, vm, *early)


def _up_bwd(dq, dk, dv, dkm, dvm, p, pm, tabs, tabs_m, wq_p, wkv_p, gq, gkv, nb, s, tm):
    nt = s // tm
    n = nb * nt
    c_t, sa_t, sb_t = tabs
    cm_t, sam_t, sbm_t = tabs_m

    def kv_path(dkh, dvh, pa, c, sa, sb, wkv, gkvv):
        dkpe = dkh[0][:, NOPE:]
        for h in range(1, HEADS):
            dkpe = dkpe + dkh[h][:, NOPE:]
        dkr = _rope_bwd(dkpe, c, sa, sb)
        dkv = jnp.concatenate([d[:, :NOPE] for d in dkh] + list(dvh), axis=1).astype(BF16)
        ckv = pa[:, Q_RANK:Q_RANK + KV_RANK]
        kvn, rkv = _rms(ckv, gkvv)
        dckv, dg = _rms_bwd(_dot(dkv, wkv), ckv, rkv, gkvv)
        return dckv, dkr, kvn.astype(BF16), dkv, jnp.sum(dg, axis=0, keepdims=True)

    def body(dq_ref, dk_ref, dv_ref, pa_ref, c_ref, sa_ref, sb_ref,
             dkm_ref, dvm_ref, pam_ref, cm_ref, sam_ref, sbm_ref,
             wq_ref, wkv_ref, gq_ref, gkv_ref,
             dpa_ref, dpam_ref, pq_ref, pkv_ref, dgq_ref, dgkv_ref, dwq_ref, dwkv_ref):
        i = pl.program_id(0)

        @pl.when(i == 0)
        def _():
            dwq_ref[...] = jnp.zeros_like(dwq_ref)
            dwkv_ref[...] = jnp.zeros_like(dwkv_ref)
            dgq_ref[...] = jnp.zeros_like(dgq_ref)
            dgkv_ref[...] = jnp.zeros_like(dgkv_ref)

        @pl.when(i < n)
        def _():
            c, sa, sb = c_ref[...], sa_ref[...], sb_ref[...]
            pa = pa_ref[...]
            parts = []
            for h in range(HEADS):
                dqh = dq_ref[0, h].astype(F32) * ATTN_SCALE
                parts += [dqh[:, :NOPE], _rope_bwd(dqh[:, NOPE:], c, sa, sb)]
            dql = jnp.concatenate(parts, axis=1).astype(BF16)
            cq = pa[:, 0:Q_RANK]
            gqv = gq_ref[...]
            qn, rq = _rms(cq, gqv)
            dwq_ref[...] += _dot_tn(dql, qn.astype(BF16))
            dcq, dg = _rms_bwd(_dot(dql, wq_ref[...]), cq, rq, gqv)
            dgq_ref[...] += jnp.sum(dg, axis=0, keepdims=True)
            dckv, dkr, kvn, dkv, dgk = kv_path([dk_ref[0, h].astype(F32) for h in range(HEADS)],
                                               [dv_ref[0, h].astype(F32) for h in range(HEADS)],
                                               pa, c, sa, sb, wkv_ref[...], gkv_ref[...])
            dwkv_ref[...] += _dot_tn(dkv, kvn)
            dgkv_ref[...] += dgk
            dpa_ref[...] = jnp.concatenate([dcq, dckv, dkr], axis=1).astype(BF16)

        @pl.when(i == n)
        def _():
            dckv, dkr, kvn, dkv, dgk = kv_path([dkm_ref[h] for h in range(HEADS)],
                                               [dvm_ref[h] for h in range(HEADS)],
                                               pam_ref[...], cm_ref[...], sam_ref[...], sbm_ref[...],
                                               wkv_ref[...], gkv_ref[...])
            dwkv_ref[...] += _dot_tn(dkv, kvn)
            dgkv_ref[...] += dgk
            dpam_ref[...] = jnp.concatenate([jnp.zeros((N_META, Q_RANK), F32), dckv, dkr], axis=1)
            for h in range(HEADS):
                pq_ref[h] = dwq_ref[QK_PAD * h:QK_PAD * h + NOPE + ROPE, :]
                pkv_ref[h, 0:NOPE, :] = dwkv_ref[NOPE * h:NOPE * (h + 1), :]
                pkv_ref[h, NOPE:NOPE + VDIM, :] = dwkv_ref[512 + VDIM * h:512 + VDIM * (h + 1), :]

    cl = lambda i: jnp.minimum(i, n - 1)
    hb = lambda w: pl.BlockSpec((1, HEADS, tm, w), lambda i: (cl(i) // nt, 0, cl(i) % nt, 0))
    tab = pl.BlockSpec((tm, 128), lambda i: (cl(i) % nt, 0))
    full = lambda a: pl.BlockSpec(a.shape, lambda i: (0,) * a.ndim)
    const = lambda shape: pl.BlockSpec(shape, lambda i: (0,) * len(shape))
    return pl.pallas_call(
        body, name="up_bwd", grid=(n + 1,),
        in_specs=[hb(QK_PAD), hb(QK_PAD), hb(VDIM), pl.BlockSpec((tm, 512), lambda i: (cl(i), 0)), tab, tab, tab,
                  full(dkm), full(dvm), pl.BlockSpec((N_META, 512), lambda i: (0, 0)),
                  full(cm_t), full(sam_t), full(sbm_t), full(wq_p), full(wkv_p), full(gq), full(gkv)],
        out_specs=[pl.BlockSpec((tm, 512), lambda i: (cl(i), 0)), const((N_META, 512)),
                   const((HEADS, NOPE + ROPE, Q_RANK)), const((HEADS, NOPE + VDIM, KV_RANK)),
                   const((1, Q_RANK)), const((1, KV_RANK))],
        out_shape=[jax.ShapeDtypeStruct((nb * s, 512), BF16), jax.ShapeDtypeStruct((N_META, 512), F32),
                   jax.ShapeDtypeStruct((HEADS, NOPE + ROPE, Q_RANK), F32),
                   jax.ShapeDtypeStruct((HEADS, NOPE + VDIM, KV_RANK), F32),
                   jax.ShapeDtypeStruct((1, Q_RANK), F32), jax.ShapeDtypeStruct((1, KV_RANK), F32)],
        scratch_shapes=[pltpu.VMEM((HEADS * QK_PAD, Q_RANK), F32), pltpu.VMEM((1024, KV_RANK), F32)],
        compiler_params=_cparams("arbitrary"),
    )(dq, dk, dv, p, c_t, sa_t, sb_t, dkm, dvm, pm, cm_t, sam_t, sbm_t, wq_p, wkv_p, gq, gkv)


def _in_bwd(x2d, dh2, dpa, dpb, meta, dpam, dccm, pm, w_in_p, norm_g, nb, s, tm):
    nt = s // tm
    n = nb * nt

    def body(x_ref, dh_ref, dpa_ref, dpb_ref, mt_ref, dpam_ref, dccm_ref, mc_ref, mh_ref, w_ref, g_ref,
             gx_ref, gm_ref, dw_hbm, dg_ref, acc_ref, sems):
        i = pl.program_id(0)

        @pl.when(i == 0)
        def _():
            acc_ref[...] = jnp.zeros_like(acc_ref)
            dg_ref[...] = jnp.zeros_like(dg_ref)

        def rows(x, dp, dres):
            g = g_ref[...]
            dpb16 = dp.astype(BF16)
            du = _dot(dpb16, w_ref[...])
            u, r1 = _rms(x, g)
            acc_ref[...] += _dot_tn(dpb16, u.astype(BF16))
            dx, dg = _rms_bwd(du, x, r1, g)
            dg_ref[...] += jnp.sum(dg, axis=0, keepdims=True)
            return dx if dres is None else dx + dres

        @pl.when(i < n)
        def _():
            dp = jnp.concatenate([dpa_ref[...], dpb_ref[...]], axis=1)
            gx_ref[...] = rows(x_ref[...], dp, dh_ref[...])

        @pl.when(i == n)
        def _():
            dcc = dccm_ref[0]
            for b in range(1, nb):
                dcc = dcc + dccm_ref[b]
            z8 = jnp.zeros((8, CONV_W), F32)
            dc = jnp.concatenate([z8, dcc * mh_ref[8:16, :]], axis=0)
            dh = jnp.concatenate([z8, dcc * mc_ref[8:16, :]], axis=0)
            z = jnp.zeros((N_META, CONV_W), F32)
            dp = jnp.concatenate([dpam_ref[...], z, z, dc, dh, z], axis=1)
            gm_ref[...] = rows(mt_ref[...], dp, None)
            per = IN_DIM // 4
            cps = [pltpu.make_async_copy(acc_ref.at[0:448], dw_hbm.at[0, 0:448], sems.at[0]),
                   pltpu.make_async_copy(acc_ref.at[512:per + 64], dw_hbm.at[0, 448:per], sems.at[1])]
            for qq in range(1, 4):
                cps.append(pltpu.make_async_copy(acc_ref.at[per * qq + 64:per * (qq + 1) + 64], dw_hbm.at[qq],
                                                 sems.at[qq + 1]))
            for cp in cps:
                cp.start()
            for cp in cps:
                cp.wait()

    cl = lambda i: jnp.minimum(i, n - 1)
    row = lambda w: pl.BlockSpec((tm, w), lambda i: (cl(i), 0))
    full = lambda a: pl.BlockSpec(a.shape, lambda i: (0,) * a.ndim)
    mblk = lambda j: pl.BlockSpec((N_META, 512), lambda i: (0, j))
    return pl.pallas_call(
        body, name="in_bwd", grid=(n + 1,),
        in_specs=[row(D_MODEL), row(D_MODEL), row(512), row(2560), full(meta), full(dpam), full(dccm),
                  mblk(BLK_CC), mblk(BLK_CH), full(w_in_p), full(norm_g)],
        out_specs=[row(D_MODEL), pl.BlockSpec((N_META, D_MODEL), lambda i: (0, 0)),
                   pl.BlockSpec(memory_space=pl.ANY), pl.BlockSpec((1, D_MODEL), lambda i: (0, 0))],
        out_shape=[jax.ShapeDtypeStruct((nb * s, D_MODEL), F32), jax.ShapeDtypeStruct((N_META, D_MODEL), F32),
                   jax.ShapeDtypeStruct((4, IN_DIM // 4, D_MODEL), F32), jax.ShapeDtypeStruct((1, D_MODEL), F32)],
        scratch_shapes=[pltpu.VMEM((IN_PAD, D_MODEL), F32), pltpu.SemaphoreType.DMA((5,))],
        compiler_params=_cparams("arbitrary"),
    )(x2d, dh2, dpa, dpb, meta, dpam, dccm, pm, pm, w_in_p, norm_g)


def _gather_weights(w_in_shard, split, pieces, out_rows, whole, zero_fills):
    ns, nw, nz = len(split), len(whole), len(zero_fills)
    flat = [(a, pc) for a in range(ns) for pc in pieces[a]]
    nk = len(flat)
    hh = HEAD_ROWS // 2

    def body(*refs):
        ins, wins, zins = refs[1:1 + ns], refs[1 + ns:1 + ns + nw], refs[1 + ns + nw:1 + ns + nw + nz]
        n_in = 1 + ns + nw + nz
        head_ref, shard16 = refs[n_in], refs[n_in + 1]
        outs, wouts = refs[n_in + 2:n_in + 2 + ns], refs[n_in + 2 + ns:n_in + 2 + ns + nw]
        scr = refs[n_in + 2 + ns + nw:]
        stage = scr[:ns]
        (send_sems, recv_sems, fwd_send, fwd_recv, loc_sems, w_send, w_recv, w_loc, z_sems,
         h_send, h_recv, h_pass) = scr[ns:]
        x, y, c = lax.axis_index("x"), lax.axis_index("y"), lax.axis_index("c")
        mine = 2 * x + y
        chips = [(1 - x, y), (x, 1 - y), (1 - x, 1 - y)]
        chip_of = [2 * px + py for px, py in chips]
        shard16[...] = refs[0][...].astype(BF16)
        for a in range(ns):
            stage[a][...] = ins[a][...].astype(BF16)

        def head_rows(half):
            return pl.ds(pl.multiple_of(half * hh, 16), hh)

        def head_copy(j):
            px, py = chips[j]
            return pltpu.make_async_remote_copy(
                src_ref=shard16.at[head_rows(c)], dst_ref=head_ref.at[head_rows(c)], send_sem=h_send.at[j],
                recv_sem=h_recv.at[0], device_id=(px, py, c), device_id_type=pl.DeviceIdType.MESH)

        def head_pass(half):
            ref = head_ref.at[head_rows(half)]
            return pltpu.make_async_remote_copy(
                src_ref=ref, dst_ref=ref, send_sem=h_pass.at[0], recv_sem=h_pass.at[1],
                device_id=(x, y, 1 - c), device_id_type=pl.DeviceIdType.MESH)

        head_ref[HEAD_ROWS:IN_HEAD, :] = jnp.zeros((IN_HEAD - HEAD_ROWS, D_MODEL), BF16)

        @pl.when(mine == 0)
        def _():
            for j in range(3):
                head_copy(j).start()
            head_ref[0:HEAD_ROWS, :] = shard16[0:HEAD_ROWS, :]

        def src(k):
            a, (s0, nr, _, _, _, _) = flat[k]
            return stage[a].at[s0:s0 + nr]

        def dst(k, q):
            a, (_, nr, per, first, rest, _) = flat[k]
            row = per * q + first + (rest - first) * jnp.minimum(q, 1)
            return outs[a].at[pl.ds(pl.multiple_of(row, 16), nr)]

        def ici(k, j, q):
            px, py = chips[j]
            return pltpu.make_async_remote_copy(
                src_ref=src(k), dst_ref=dst(k, q), send_sem=send_sems.at[k, j], recv_sem=recv_sems.at[k, j],
                device_id=(px, py, c), device_id_type=pl.DeviceIdType.MESH)

        def fwd(k, j):
            ref = dst(k, chip_of[j])
            return pltpu.make_async_remote_copy(
                src_ref=ref, dst_ref=ref, send_sem=fwd_send.at[k, j], recv_sem=fwd_recv.at[k, j],
                device_id=(x, y, 1 - c), device_id_type=pl.DeviceIdType.MESH)

        def wcopy(b, j, q):
            px, py = chips[j]
            return pltpu.make_async_remote_copy(
                src_ref=wins[b], dst_ref=wouts[b].at[q], send_sem=w_send.at[b, j], recv_sem=w_recv.at[b, j],
                device_id=(px, py, c), device_id_type=pl.DeviceIdType.MESH)

        local = [pltpu.make_async_copy(src(k), dst(k, mine), loc_sems.at[k]) for k in range(nk)]
        local += [pltpu.make_async_copy(wins[b], wouts[b].at[mine], w_loc.at[b]) for b in range(nw)]
        for z, (a, _, row0) in enumerate(zero_fills):
            local.append(pltpu.make_async_copy(zins[z], outs[a].at[row0:row0 + zins[z].shape[0]], z_sems.at[z]))
        wsends = [wcopy(b, j, mine) for b in range(nw) for j in range(3)]
        for cp in local + wsends:
            cp.start()

        for half in (0, 1):
            @pl.when(c == half)
            def _(half=half):
                my_k = [k for k in range(nk) if flat[k][1][5] == half]
                other_k = [k for k in range(nk) if flat[k][1][5] != half]
                sends = [ici(k, j, mine) for k in my_k for j in range(3)]
                for cp in sends:
                    cp.start()
                passed = []
                for k in my_k:
                    for j in range(3):
                        ici(k, j, chip_of[j]).wait_recv()
                        cp = fwd(k, j)
                        cp.start()
                        passed.append(cp)
                for k in other_k:
                    for j in range(3):
                        fwd(k, j).wait_recv()
                for cp in sends + passed:
                    cp.wait_send()

        for b in range(nw):
            for j in range(3):
                wcopy(b, j, chip_of[j]).wait_recv()
        for cp in wsends:
            cp.wait_send()
        for cp in local:
            cp.wait()

        @pl.when(mine == 0)
        def _():
            for j in range(3):
                head_copy(j).wait_send()

        @pl.when(mine != 0)
        def _():
            head_copy(0).wait_recv()
            cp = head_pass(c)
            cp.start()
            head_pass(1 - c).wait_recv()
            cp.wait_send()

    vmem = pl.BlockSpec(memory_space=pltpu.VMEM)
    dma = pltpu.SemaphoreType.DMA
    zeros = [z for _, z, _ in zero_fills]
    return pl.pallas_call(
        body, name="gather_weights",
        in_specs=[vmem] * (1 + ns + nw + nz), out_specs=[vmem] * (2 + ns + nw),
        out_shape=([jax.ShapeDtypeStruct((IN_HEAD, D_MODEL), BF16), jax.ShapeDtypeStruct(w_in_shard.shape, BF16)]
                   + [jax.ShapeDtypeStruct((out_rows[a], split[a].shape[1]), BF16) for a in range(ns)]
                   + [jax.ShapeDtypeStruct((4,) + w.shape, w.dtype) for w in whole]),
        scratch_shapes=[pltpu.VMEM(a.shape, BF16) for a in split]
        + [dma((nk, 3)), dma((nk, 3)), dma((nk, 3)), dma((nk, 3)), dma((nk,)),
           dma((nw, 3)), dma((nw, 3)), dma((nw,)), dma((nz,)), dma((3,)), dma((1,)), dma((2,))],
        compiler_params=pltpu.CompilerParams(vmem_limit_bytes=VMEM_LIMIT),
    )(w_in_shard, *split, *whole, *zeros)


def _reduce_grads(parts, small):
    n = len(parts)
    shapes = [a.shape[1:] for a in parts]
    halves = [(sh[0] // 2, sh[1]) for sh in shapes]

    def body(*refs):
        pin, sm_in = refs[:n], refs[n]
        gout, sm_out = refs[n + 1:2 * n + 1], refs[2 * n + 1]
        scr = refs[2 * n + 2:]
        own, sib, wire, rbuf = scr[:n], scr[n:2 * n], scr[2 * n:3 * n], scr[3 * n:4 * n]
        (sbuf, send_sems, recv_sems, loc_sems, pre_send, pre_recv, post_send, post_recv,
         sm_send, sm_recv) = scr[4 * n:]
        x, y, c = lax.axis_index("x"), lax.axis_index("y"), lax.axis_index("c")
        mine = 2 * x + y
        me = 4 * x + 2 * y + c
        sibling = (x, y, 1 - c)
        chips = [(1 - x, y), (x, 1 - y), (1 - x, 1 - y)]

        def rows(a, half):
            r2 = halves[a][0]
            return pl.ds(pl.multiple_of(half * r2, r2), r2)

        chip_of = [2 * px + py for px, py in chips]
        blocks = chip_of + [mine]

        def pre(a, k):
            return pltpu.make_async_remote_copy(
                src_ref=pin[a].at[blocks[k], rows(a, 1 - c), :], dst_ref=sib[a].at[blocks[k]],
                send_sem=pre_send.at[a, k], recv_sem=pre_recv.at[a, k], device_id=sibling,
                device_id_type=pl.DeviceIdType.MESH)

        def ici(a, j):
            px, py = chips[j]
            return pltpu.make_async_remote_copy(
                src_ref=wire[a].at[2 * px + py], dst_ref=rbuf[a].at[j], send_sem=send_sems.at[a, j],
                recv_sem=recv_sems.at[a, j], device_id=(px, py, c), device_id_type=pl.DeviceIdType.MESH)

        def post(a, half):
            ref = gout[a].at[rows(a, half), :]
            return pltpu.make_async_remote_copy(
                src_ref=ref, dst_ref=ref, send_sem=post_send.at[a], recv_sem=post_recv.at[a],
                device_id=sibling, device_id_type=pl.DeviceIdType.MESH)

        def small_copy(kk):
            peer = (x ^ (kk >> 2), y ^ ((kk >> 1) & 1), c ^ (kk & 1))
            return pltpu.make_async_remote_copy(
                src_ref=sm_in, dst_ref=sbuf.at[kk], send_sem=sm_send.at[kk - 1], recv_sem=sm_recv.at[kk - 1],
                device_id=peer, device_id_type=pl.DeviceIdType.MESH)

        local = [[pltpu.make_async_copy(pin[a].at[blocks[k], rows(a, c), :], own[a].at[blocks[k]], loc_sems.at[a, k])
                  for k in range(4)] for a in range(n)]
        pres = [[pre(a, k) for k in range(4)] for a in range(n)]
        smalls = [small_copy(kk) for kk in range(1, 8)]
        for a in range(n):
            for k in range(4):
                local[a][k].start()
                pres[a][k].start()
        for cp in smalls:
            cp.start()
        sbuf[0] = sm_in[...]
        sends = []
        for a in range(n):
            for k in range(4):
                local[a][k].wait()
                pres[a][k].wait_recv()
                tot = own[a][blocks[k]] + sib[a][blocks[k]]
                own[a][blocks[k]] = tot
                if k < 3:
                    wire[a][blocks[k]] = tot.astype(BF16)
                    cp = ici(a, k)
                    cp.start()
                    sends.append(cp)
        for cp in smalls:
            cp.wait_recv()
        total = sbuf[me]
        for d in range(1, 8):
            total = total + sbuf[me ^ d]
        sm_out[...] = total
        posts = []
        for a in range(n):
            for j in range(3):
                ici(a, j).wait_recv()
            fin = own[a][mine]
            for j in range(3):
                fin = fin + rbuf[a][j].astype(F32)
            gout[a][rows(a, c), :] = fin
            cp = post(a, c)
            cp.start()
            posts.append(cp)
        for a in range(n):
            post(a, 1 - c).wait_recv()
        for cp in [cp for row in pres for cp in row] + sends + smalls + posts:
            cp.wait_send()

    vmem = pl.BlockSpec(memory_space=pltpu.VMEM)
    dma = pltpu.SemaphoreType.DMA
    return pl.pallas_call(
        body, name="reduce_grads",
        in_specs=[pl.BlockSpec(memory_space=pl.ANY)] * n + [vmem], out_specs=[vmem] * (n + 1),
        out_shape=[jax.ShapeDtypeStruct(sh, F32) for sh in shapes] + [jax.ShapeDtypeStruct(small.shape, F32)],
        scratch_shapes=([pltpu.VMEM((4,) + hs, F32) for hs in halves] + [pltpu.VMEM((4,) + hs, F32) for hs in halves]
                        + [pltpu.VMEM((4,) + hs, BF16) for hs in halves]
                        + [pltpu.VMEM((3,) + hs, BF16) for hs in halves]
                        + [pltpu.VMEM((8,) + small.shape, F32), dma((n, 3)), dma((n, 3)), dma((n, 4)),
                           dma((n, 4)), dma((n, 4)), dma((n,)), dma((n,)), dma((7,)), dma((7,))]),
        compiler_params=pltpu.CompilerParams(vmem_limit_bytes=VMEM_LIMIT),
    )(*parts, small)


def _adamw_update(w_ref, g_ref, m_ref, v_ref, d_ref, nm_ref, nv_ref):
    gv = g_ref[...]
    nm = ADAM_B1 * m_ref[...] + (1.0 - ADAM_B1) * gv
    nv = ADAM_B2 * v_ref[...] + (1.0 - ADAM_B2) * (gv * gv)
    m_hat = nm / (1.0 - ADAM_B1 ** ADAM_STEP)
    v_hat = nv / (1.0 - ADAM_B2 ** ADAM_STEP)
    d_ref[...] = -ADAM_LR * (m_hat / (jnp.sqrt(v_hat) + ADAM_EPS) + ADAM_WD * w_ref[...])
    nm_ref[...] = nm
    nv_ref[...] = nv


def _adamw_small(ws, gs, ms, vs):
    k = len(ws)

    def body(*refs):
        ins, outs = refs[:4 * k], refs[4 * k:]
        for a in range(k):
            _adamw_update(ins[a], ins[k + a], ins[2 * k + a], ins[3 * k + a], outs[a], outs[k + a], outs[2 * k + a])

    out = pl.pallas_call(
        body, name="adamw_small",
        out_shape=[jax.ShapeDtypeStruct(w.shape, F32) for w in ws] * 3,
        compiler_params=pltpu.CompilerParams(vmem_limit_bytes=VMEM_LIMIT),
    )(*ws, *gs, *ms, *vs)
    return out[:k], out[k:2 * k], out[2 * k:]


def _adamw(w, g, m, v, name):
    shape = w.shape
    w2, g2, m2, v2 = (a.reshape((-1, shape[-1])) for a in (w, g, m, v))

    def body(w_ref, g_ref, m_ref, v_ref, d_ref, nm_ref, nv_ref):
        _adamw_update(w_ref, g_ref, m_ref, v_ref, d_ref, nm_ref, nv_ref)

    rows, cols = w2.shape
    nblk = cols // 256 if cols % 256 == 0 and rows >= 64 else 1
    blk = pl.BlockSpec((rows, cols // nblk), lambda j: (0, j))
    out = pl.pallas_call(
        body, name=name, grid=(nblk,), in_specs=[blk] * 4, out_specs=[blk] * 3,
        out_shape=[jax.ShapeDtypeStruct(w2.shape, F32)] * 3,
        compiler_params=_cparams("parallel"),
    )(w2, g2, m2, v2)
    return tuple(a.reshape(shape) for a in out)


def kernel(x, meta_tokens, norm_g, w_in, q_norm_g, w_q_up, kv_norm_g, w_kv_up, conv_w, attn_out_g, conv_out_g, w_out, final_norm_g, loss_target, m_meta_tokens, m_norm_g, m_w_in, m_q_norm_g, m_w_q_up, m_kv_norm_g, m_w_kv_up, m_conv_w, m_attn_out_g, m_conv_out_g, m_w_out, m_final_norm_g, v_meta_tokens, v_norm_g, v_w_in, v_q_norm_g, v_w_q_up, v_kv_norm_g, v_w_kv_up, v_conv_w, v_attn_out_g, v_conv_out_g, v_w_out, v_final_norm_g):
    nb, s, _ = x.shape
    tm = min(ROW_TILE, s)
    ta = min(ATTN_TILE, s)
    assert s % tm == 0 and s % ta == 0 and tm % 16 == 0
    r = nb * s

    tr = lambda a: jnp.transpose(a[0])
    w_head, w_in_shard, wq_p, wkv_p, g_cw, g_meta = _gather_weights(
        tr(w_in), [tr(w_q_up), tr(w_kv_up)],
        [W_Q_PIECES, W_KV_PIECES], [HEADS * QK_PAD, 1024],
        [jnp.transpose(conv_w, (1, 0, 2)), meta_tokens],
        [(0, jnp.zeros((64, Q_RANK), BF16), QK_PAD * h + NOPE + ROPE) for h in range(HEADS)])
    conv_f = jnp.transpose(g_cw[:, :, 0, :], (1, 0, 2)).reshape(3, CONV_W)
    meta_f = jnp.transpose(g_meta, (1, 0, 2)).reshape(N_META, D_MODEL)

    c_all, sa_all, sb_all = _rope_tables(N_META + s)
    tabs_m = (c_all[:N_META], sa_all[:N_META], sb_all[:N_META])
    tabs = (c_all[N_META:], sa_all[N_META:], sb_all[N_META:])
    gid = np.arange(CONV_W) // CONV_GROUP
    gmat = jnp.asarray(np.where(gid[:, None] == gid[None, :], 1.0 / CONV_GROUP, 0.0), BF16)
    ga, gc = attn_out_g, conv_out_g
    gf = final_norm_g.reshape(1, D_MODEL)

    x2d = x.reshape(r, D_MODEL)
    tgt2d = loss_target.reshape(r, D_MODEL)

    ph, q, k, v, pmh, km, vm, w_out_f, w_in_part = _fwd_proj(
        x2d, meta_f, tabs, tabs_m, norm_g, w_head, q_norm_g, wq_p, kv_norm_g, wkv_p, w_out[0].astype(BF16),
        w_in_shard, nb, s, tm)
    o, lse, w_in_p = _attn_fwd(q, k, v, km, vm, w_in_shard, w_in_part, nb, s, ta)
    dh2, dycat, dw_out, dgf, loss_acc, pt, pmt = _out_fwd_bwd(x2d, tgt2d, o, meta_f, norm_g, w_in_p, conv_f, ga, gc,
                                                              gmat, w_out_f, gf, nb, s, tm)
    dpb, do, delta, dccm, dga, dgc, dcw = _gate_bwd(dycat, o, pt, pmt, conv_f, ga, gc, gmat, nb, s, tm)
    p_out = dw_out.reshape(4, D_MODEL // 4, D_MODEL)
    dq, dk, dv, dkm, dvm, g_w_out = _attn_bwd(q, k, v, do, lse, delta, km, vm, [p_out], nb, s, ta)
    dpa, dpam, p_q, p_kv, dgq, dgkv = _up_bwd(dq, dk, dv, dkm, dvm, ph, pmh, tabs, tabs_m, wq_p, wkv_p,
                                              q_norm_g, kv_norm_g, nb, s, tm)
    gx, gmeta, p_in, dng = _in_bwd(x2d, dh2, dpa, dpb, meta_f, dpam, dccm, pmt, w_in_p, norm_g, nb, s, tm)

    flat =jnp.concatenate([dng.reshape(-1), dgq.reshape(-1), dgkv.reshape(-1), dga.reshape(-1), dgc.reshape(-1),
                            dgf.reshape(-1), dcw[:3].reshape(-1), gmeta.reshape(-1), loss_acc[0, 0:1]])
    n_small = flat.shape[0]
    rows_small = -(-n_small // 1024) * 8
    small = jnp.pad(flat, (0, rows_small * 128 - n_small)).reshape(rows_small, 128)
    g_w_in_t, g_w_q_t, g_w_kv_t, small_sum = _reduce_grads([p_in, p_q, p_kv], small)
    ssum = small_sum.reshape(-1)

    def take(off, n):
        return ssum[off:off + n], off + n

    off = 0
    g_norm, off = take(off, D_MODEL)
    g_qn, off = take(off, Q_RANK)
    g_kvn, off = take(off, KV_RANK)
    g_ga, off = take(off, CONV_W)
    g_gc, off = take(off, CONV_W)
    g_gf, off = take(off, D_MODEL)
    g_cw_all, off = take(off, 3 * CONV_W)
    g_meta_all, off = take(off, N_META * D_MODEL)
    loss = ssum[off]
    chip = 2 * lax.axis_index("x") + lax.axis_index("y")
    g_conv = lax.dynamic_slice(g_cw_all.reshape(3, CONV_W), (0, chip * 128), (3, 128))
    g_mt = lax.dynamic_slice(g_meta_all.reshape(N_META, D_MODEL), (0, chip * 256), (N_META, 256))

    grads = {
        "meta_tokens": g_mt, "norm_g": g_norm.reshape(1, -1), "w_in": g_w_in_t, "q_norm_g": g_qn.reshape(1, -1),
        "w_q_up": g_w_q_t, "kv_norm_g": g_kvn.reshape(1, -1), "w_kv_up": jnp.transpose(g_w_kv_t)[None],
        "conv_w": g_conv[None], "attn_out_g": g_ga.reshape(1, -1), "conv_out_g": g_gc.reshape(1, -1),
        "w_out": g_w_out[None], "final_norm_g": g_gf,
    }
    transposed = ("w_in", "w_q_up")
    weights = {
        "meta_tokens": (meta_tokens, m_meta_tokens, v_meta_tokens), "norm_g": (norm_g, m_norm_g, v_norm_g),
        "w_in": (w_in, m_w_in, v_w_in), "q_norm_g": (q_norm_g, m_q_norm_g, v_q_norm_g),
        "w_q_up": (w_q_up, m_w_q_up, v_w_q_up), "kv_norm_g": (kv_norm_g, m_kv_norm_g, v_kv_norm_g),
        "w_kv_up": (w_kv_up, m_w_kv_up, v_w_kv_up), "conv_w": (conv_w, m_conv_w, v_conv_w),
        "attn_out_g": (attn_out_g, m_attn_out_g, v_attn_out_g), "conv_out_g": (conv_out_g, m_conv_out_g, v_conv_out_g),
        "w_out": (w_out, m_w_out, v_w_out), "final_norm_g": (final_norm_g, m_final_norm_g, v_final_norm_g),
    }
    names = list(weights)
    small = [nme for nme in names if nme != "w_in"]

    def view(nme, a):
        if nme in transposed:
            return a if a.ndim == 2 else tr(a)
        if nme == "conv_w":
            return jnp.transpose(a.reshape(1, 3, -1), (1, 0, 2))
        if a.ndim == 3:
            return a[0]
        return a.reshape(1, -1) if a.ndim == 1 else a

    def unview(nme, a):
        if nme in transposed:
            return jnp.transpose(a)[None]
        if nme == "conv_w":
            return jnp.transpose(a, (1, 0, 2))
        return a.reshape(weights[nme][0].shape)

    res_small = _adamw_small(*[[view(nme, a) for nme, a in zip(small, col)] for col in (
        [weights[nme][0] for nme in small], [grads[nme] for nme in small],
        [weights[nme][1] for nme in small], [weights[nme][2] for nme in small])])
    w_, m_, v_ = weights["w_in"]
    res = _adamw(tr(w_), grads["w_in"], tr(m_), tr(v_), "adamw_w_in")
    upd = {"w_in": tuple(jnp.transpose(a)[None] for a in (grads["w_in"],) + res)}
    for j, nme in enumerate(small):
        upd[nme] = (unview(nme, view(nme, grads[nme])),) + tuple(unview(nme, r[j]) for r in res_small)
    grads = {nme: upd[nme][0] for nme in names}
    deltas, new_m, new_v = ([upd[nme][j] for nme in names] for j in (1, 2, 3))

    grad_x = gx.reshape(nb, s, D_MODEL)
    return (loss, grad_x, *[grads[nme] for nme in names], *deltas, *new_m, *new_v)
```

```python
import functools

import jax
import jax.numpy as jnp
import numpy as np
from jax import lax
from jax.experimental import pallas as pl
from jax.experimental.pallas import tpu as pltpu

F32 = jnp.float32
BF16 = jnp.bfloat16

D_MODEL = 1024
N_META = 16
HEADS = 4
NOPE = 128
ROPE = 64
VDIM = 128
QK_PAD = 256
Q_RANK = 256
KV_RANK = 128
CONV_W = 512
CONV_GROUP = 64
ROPE_THETA = 10000.0
EPS = 1e-6
ATTN_SCALE = (NOPE + ROPE) ** -0.5
IN_DIM = 3008
IN_PAD = 3072
HEAD_ROWS = Q_RANK + KV_RANK + ROPE
IN_HEAD = 512
IN_TAIL = IN_PAD - IN_HEAD
BLK_ZA, BLK_CB, BLK_CC, BLK_CH, BLK_ZC = 0, 1, 2, 3, 4
NEG_INF = -1e30

ADAM_LR = 0.001
ADAM_B1 = 0.9
ADAM_B2 = 0.999
ADAM_EPS = 1e-08
ADAM_WD = 0.01
ADAM_STEP = 10

ROW_TILE = 512
ATTN_TILE = 256
VMEM_LIMIT = 56 * 1024 * 1024

NT = (((1,), (1,)), ((), ()))
TN = (((0,), (0,)), ((), ()))


def _cparams(*sem):
    return pltpu.CompilerParams(dimension_semantics=sem, vmem_limit_bytes=VMEM_LIMIT)


def _dot(a, b):
    return jnp.dot(a, b, preferred_element_type=F32)


def _dot_nt(a, b):
    return lax.dot_general(a, b, NT, preferred_element_type=F32)


def _dot_tn(a, b):
    return lax.dot_general(a, b, TN, preferred_element_type=F32)


def _rms(x, g):
    r = lax.rsqrt(jnp.mean(x * x, axis=-1, keepdims=True) + EPS)
    return x * r * g, r


def _rms_bwd(dy, x, r, g):
    xh = x * r
    dyg = dy * g
    dx = r * (dyg - xh * jnp.mean(dyg * xh, axis=-1, keepdims=True))
    return dx, dy * xh


def _sigmoid(z):
    return 1.0 / (1.0 + jnp.exp(-z))


def _rope(b, c, sa, sb):
    return b * c + pltpu.roll(b, 96, 1) * sa + pltpu.roll(b, 32, 1) * sb


def _rope_bwd(d, c, sa, sb):
    return d * c + pltpu.roll(d * sa, 32, 1) + pltpu.roll(d * sb, 96, 1)


def _group_mean(x, gmat):
    hi = x.astype(BF16)
    lo = (x - hi.astype(F32)).astype(BF16)
    return _dot(hi, gmat) + _dot(lo, gmat)


def _row_of(col, rows):
    return jnp.transpose(jnp.broadcast_to(col, (rows, 128)))[0:1, :]


def _rope_tables(n_pos):
    half = ROPE // 2
    inv_freq = (np.float32(1.0) / (np.float32(ROPE_THETA) ** (np.arange(half, dtype=np.float32) / np.float32(half))))
    ang = np.arange(n_pos, dtype=np.float32)[:, None] * inv_freq.astype(np.float32)[None, :]
    cos, sin = np.cos(ang).astype(np.float32), np.sin(ang).astype(np.float32)
    z = np.zeros((n_pos, half), np.float32)
    c = np.concatenate([cos, cos, z, z], axis=1)
    sa = np.concatenate([-sin, z, z, z], axis=1)
    sb = np.concatenate([z, sin, z, z], axis=1)
    return jnp.asarray(c), jnp.asarray(sa), jnp.asarray(sb)


W_IN_PIECES_1 = ((0, 128, 752, 0, 64, 0), (384, 64, 752, 384, 448, 1), (448, 64, 752, 512, 512, 1))
W_IN_PIECES_2 = ((128, 256, 752, 128, 192, 0), (512, 240, 752, 576, 576, 1))
W_Q_PIECES = ((0, 96, 256, 0, 0, 0), (96, 96, 256, 96, 96, 1))
W_KV_PIECES = ((0, 128, 128, 0, 0, 0), (128, 128, 128, 512, 512, 1))
W_OUT_PIECES = ((0, 128, 256, 0, 0, 0), (128, 128, 256, 128, 128, 1))


class _StagedGather:
    def __init__(self, pieces, zero_rows=None):
        self.pieces = pieces
        self.zero_rows = zero_rows

    def scratch(self):
        nk, dma = len(self.pieces), pltpu.SemaphoreType.DMA
        return [dma((nk, 3)), dma((nk, 3)), dma((nk, 3)), dma((nk, 3)), dma((nk,))]

    def vmem_scratch(self, shard_shape, out_shape):
        return [pltpu.VMEM(shard_shape, BF16), pltpu.VMEM(out_shape, BF16),
                pltpu.SemaphoreType.DMA((4 * len(self.pieces) + 2,))] + self.scratch()

    def run_vmem(self, stage, shard_ref, out_ref, scr):
        src_scr, land_scr, io_sems = scr[:3]
        spans = []
        for _, nr, per, first, rest, _ in self.pieces:
            spans += [(per * q + (first if q == 0 else rest), nr) for q in range(4)]
        if self.zero_rows is not None:
            spans.append(self.zero_rows)
        flush = [pltpu.make_async_copy(land_scr.at[r0:r0 + nr], out_ref.at[r0:r0 + nr], io_sems.at[n])
                 for n, (r0, nr) in enumerate(spans)]
        if stage == 0:
            load = pltpu.make_async_copy(shard_ref, src_scr, io_sems.at[len(spans)])
            load.start()
            if self.zero_rows is not None:
                r0, nr = self.zero_rows
                land_scr[r0:r0 + nr, :] = jnp.zeros((nr, land_scr.shape[1]), BF16)
            load.wait()
        if stage < 3:
            self.run(stage, src_scr, land_scr, scr[3:])
        for cp in flush:
            if stage == 2:
                cp.start()
            if stage == 3:
                cp.wait()

    def run(self, stage, src_ref, out_ref, scr):
        send_sems, recv_sems, fwd_send, fwd_recv, loc_sems = scr
        pieces = self.pieces
        nk = len(pieces)
        x, y, c = lax.axis_index("x"), lax.axis_index("y"), lax.axis_index("c")
        mine = 2 * x + y
        chips = [(1 - x, y), (x, 1 - y), (1 - x, 1 - y)]
        chip_of = [2 * px + py for px, py in chips]
        mesh = pl.DeviceIdType.MESH

        def src(k):
            s0, nr = pieces[k][0], pieces[k][1]
            return src_ref.at[s0:s0 + nr]

        def dst(k, q):
            _, nr, per, first, rest, _ = pieces[k]
            row = per * q + first + (rest - first) * jnp.minimum(q, 1)
            return out_ref.at[pl.ds(pl.multiple_of(row, 16), nr)]

        def ici(k, j, q):
            px, py = chips[j]
            return pltpu.make_async_remote_copy(
                src_ref=src(k), dst_ref=dst(k, q), send_sem=send_sems.at[k, j], recv_sem=recv_sems.at[k, j],
                device_id=(px, py, c), device_id_type=mesh)

        def fwd(k, j):
            ref = dst(k, chip_of[j])
            return pltpu.make_async_remote_copy(
                src_ref=ref, dst_ref=ref, send_sem=fwd_send.at[k, j], recv_sem=fwd_recv.at[k, j],
                device_id=(x, y, 1 - c), device_id_type=mesh)

        local = [pltpu.make_async_copy(src(k), dst(k, mine), loc_sems.at[k]) for k in range(nk)]
        if stage == 0:
            for cp in local:
                cp.start()
        if stage == 2:
            for cp in local:
                cp.wait()
        for half in (0, 1):
            @pl.when(c == half)
            def _(half=half):
                my_k = [k for k in range(nk) if pieces[k][5] == half]
                other_k = [k for k in range(nk) if pieces[k][5] != half]
                for k in my_k:
                    for j in range(3):
                        if stage == 0:
                            ici(k, j, mine).start()
                        elif stage == 1:
                            ici(k, j, chip_of[j]).wait_recv()
                            fwd(k, j).start()
                        else:
                            ici(k, j, mine).wait_send()
                            fwd(k, j).wait_send()
                if stage == 2:
                    for k in other_k:
                        for j in range(3):
                            fwd(k, j).wait_recv()


def _fwd_proj(x2d, meta, tabs, tabs_m, norm_g, w_head, q_norm_g, wq_p, kv_norm_g, wkv_p, w_out_shard, w_in_shard,
              nb, s, tm):
    nt = s // tm
    n = nb * nt
    n_steps = n + 1
    c_t, sa_t, sb_t = tabs
    cm_t, sam_t, sbm_t = tabs_m
    gat = _StagedGather(W_OUT_PIECES)
    gat_in = _StagedGather(W_IN_PIECES_1)
    n_sems = len(gat.scratch())
    assert n_steps >= 3

    def body(x_ref, c_ref, sa_ref, sb_ref, mt_ref, cm_ref, sam_ref, sbm_ref,
             g_ref, w_ref, gq_ref, wq_ref, gkv_ref, wkv_ref, wos_ref, wis_ref,
             p_ref, q_ref, k_ref, v_ref, pm_ref, km_ref, vm_ref, wo_ref, wi_ref, *scr):
        gat_scr, gat_in_scr = scr[:n_sems], scr[n_sems:]
        i = pl.program_id(0)
        for stage, at in enumerate((0, n_steps - 2, n_steps - 1)):
            @pl.when(i == at)
            def _(stage=stage):
                gat_in.run_vmem(stage, wis_ref, wi_ref, gat_in_scr)
                gat.run(stage, wos_ref, wo_ref, gat_scr)

        def project(xv, c, sa, sb, p_out, q_out, k_out, v_out):
            u, _ = _rms(xv, g_ref[...])
            p = _dot_nt(u.astype(BF16), w_ref[...])
            p_out[...] = p
            qn, _ = _rms(p[:, 0:Q_RANK], gq_ref[...])
            q = _dot_nt(qn.astype(BF16), wq_ref[...])
            kvn, _ = _rms(p[:, Q_RANK:Q_RANK + KV_RANK], gkv_ref[...])
            kv = _dot_nt(kvn.astype(BF16), wkv_ref[...])
            kpe = _rope(p[:, 384:512], c, sa, sb)
            for h in range(HEADS):
                if q_out is not None:
                    pe = _rope(q[:, QK_PAD * h + NOPE:QK_PAD * (h + 1)], c, sa, sb)
                    qh = jnp.concatenate([q[:, QK_PAD * h:QK_PAD * h + NOPE], pe], axis=1)
                    q_out[0, h] = (qh * ATTN_SCALE).astype(BF16)
                k_out[0, h] = jnp.concatenate([kv[:, NOPE * h:NOPE * (h + 1)], kpe], axis=1).astype(BF16)
                v_out[0, h] = kv[:, 512 + VDIM * h:512 + VDIM * (h + 1)].astype(BF16)

        @pl.when(i < n)
        def _():
            project(x_ref[...], c_ref[...], sa_ref[...], sb_ref[...], p_ref, q_ref, k_ref, v_ref)

        @pl.when(i == n)
        def _():
            project(mt_ref[...], cm_ref[...], sam_ref[...], sbm_ref[...], pm_ref, None, km_ref, vm_ref)
            gat_in.run_vmem(3, wis_ref, wi_ref, gat_in_scr)

    cl = lambda i: jnp.minimum(i, n - 1)
    full = lambda a: pl.BlockSpec(a.shape, lambda i: (0,) * a.ndim)
    const = lambda shape: pl.BlockSpec(shape, lambda i: (0,) * len(shape))
    tab = pl.BlockSpec((tm, 128), lambda i: (cl(i) % nt, 0))
    hb = lambda w: pl.BlockSpec((1, HEADS, tm, w), lambda i: (cl(i) // nt, 0, cl(i) % nt, 0))
    whole = pl.BlockSpec(memory_space=pl.ANY)
    return pl.pallas_call(
        body, name="fwd_proj", grid=(n_steps,),
        in_specs=[pl.BlockSpec((tm, D_MODEL), lambda i: (cl(i), 0)), tab, tab, tab,
                  full(meta), full(cm_t), full(sam_t), full(sbm_t),
                  full(norm_g), full(w_head), full(q_norm_g), full(wq_p), full(kv_norm_g), full(wkv_p), whole, whole],
        out_specs=[pl.BlockSpec((tm, IN_HEAD), lambda i: (cl(i), 0)), hb(QK_PAD), hb(QK_PAD), hb(VDIM),
                   const((N_META, IN_HEAD)), const((1, HEADS, N_META, QK_PAD)), const((1, HEADS, N_META, VDIM)),
                   whole, whole],
        out_shape=[jax.ShapeDtypeStruct((nb * s, IN_HEAD), F32),
                   jax.ShapeDtypeStruct((nb, HEADS, s, QK_PAD), BF16),
                   jax.ShapeDtypeStruct((nb, HEADS, s, QK_PAD), BF16),
                   jax.ShapeDtypeStruct((nb, HEADS, s, VDIM), BF16),
                   jax.ShapeDtypeStruct((N_META, IN_HEAD), F32),
                   jax.ShapeDtypeStruct((1, HEADS, N_META, QK_PAD), BF16),
                   jax.ShapeDtypeStruct((1, HEADS, N_META, VDIM), BF16),
                   jax.ShapeDtypeStruct((D_MODEL, D_MODEL), BF16),
                   jax.ShapeDtypeStruct((IN_PAD, D_MODEL), BF16)],
        scratch_shapes=gat.scratch() + gat_in.vmem_scratch(w_in_shard.shape, (IN_PAD, D_MODEL)),
        compiler_params=_cparams("arbitrary"),
    )(x2d, c_t, sa_t, sb_t, meta, cm_t, sam_t, sbm_t, norm_g, w_head, q_norm_g, wq_p, kv_norm_g, wkv_p, w_out_shard,
      w_in_shard)


def _attn_fwd(q, k, v, km, vm, w_in_shard, w_in_part, nb, s, tq):
    nq = s // tq
    n_steps = nb * HEADS
    gat = _StagedGather(W_IN_PIECES_2, zero_rows=(HEAD_ROWS, IN_HEAD - HEAD_ROWS))
    assert n_steps >= 3

    def body(q_ref, k_ref, v_ref, km_ref, vm_ref, ws_ref, _, o_ref, lse_ref, w_ref, s_scr, p_scr, *gat_scr):
        step = pl.program_id(0) * HEADS + pl.program_id(1)
        for stage, at in enumerate((0, n_steps - 2, n_steps - 1)):
            @pl.when(step == at)
            def _(stage=stage):
                gat.run_vmem(stage, ws_ref, w_ref, gat_scr)

        row = lax.broadcasted_iota(jnp.int32, (tq, tq), 0)
        col = lax.broadcasted_iota(jnp.int32, (tq, tq), 1)
        def scores(i):
            slot = i % 2
            qi = q_ref[0, 0, i * tq:(i + 1) * tq, :]
            sm = _dot_nt(qi, km_ref[0, 0])
            m128 = None
            for j in range(i + 1):
                sc = _dot_nt(qi, k_ref[0, 0, j * tq:(j + 1) * tq, :])
                if j == i:
                    sc = jnp.where(col <= row, sc, NEG_INF)
                s_scr[slot, :, j * tq:(j + 1) * tq] = sc
                mx = sc[:, 0:128]
                for c0 in range(128, tq, 128):
                    mx = jnp.maximum(mx, sc[:, c0:c0 + 128])
                m128 = mx if m128 is None else jnp.maximum(m128, mx)
            return sm, jnp.maximum(jnp.max(m128, axis=1, keepdims=True), jnp.max(sm, axis=1, keepdims=True))

        def weighted_sum(i, pm, l):
            n = (i + 1) * tq
            acc = _dot(p_scr[i % 2, :, 0:n], v_ref[0, 0, 0:n, :]) + _dot(pm.astype(BF16), vm_ref[0, 0])
            o_ref[0, 0, i * tq:(i + 1) * tq, :] = acc / l

        nxt, pending = scores(0), None
        for i in range(nq):
            slot = i % 2
            sm, m = nxt
            if i + 1 < nq:
                nxt = scores(i + 1)
            pm = jnp.exp(sm - m)
            l128 = None
            for j in range(i + 1):
                p = jnp.exp(s_scr[slot, :, j * tq:(j + 1) * tq] - m)
                p_scr[slot, :, j * tq:(j + 1) * tq] = p.astype(BF16)
                ps = p[:, 0:128]
                for c0 in range(128, tq, 128):
                    ps = ps + p[:, c0:c0 + 128]
                l128 = ps if l128 is None else l128 + ps
            l = jnp.sum(l128, axis=1, keepdims=True) + jnp.sum(pm, axis=1, keepdims=True)
            lse_ref[0, 0, :, i * tq:(i + 1) * tq] = _row_of(m + jnp.log(l), tq)
            if pending is not None:
                weighted_sum(*pending)
            pending = (i, pm, l)
        weighted_sum(*pending)

        @pl.when(step == n_steps - 1)
        def _():
            gat.run_vmem(3, ws_ref, w_ref, gat_scr)

    hblk = lambda w: pl.BlockSpec((1, 1, s, w), lambda b, h: (b, h, 0, 0))
    mblk = lambda w: pl.BlockSpec((1, 1, N_META, w), lambda b, h: (0, h, 0, 0))
    whole = pl.BlockSpec(memory_space=pl.ANY)
    return pl.pallas_call(
        body, name="attn_fwd", grid=(nb, HEADS),
        in_specs=[hblk(QK_PAD), hblk(QK_PAD), hblk(VDIM), mblk(QK_PAD), mblk(VDIM), whole, whole],
        out_specs=[hblk(VDIM), pl.BlockSpec((1, 1, 1, s), lambda b, h: (b, h, 0, 0)), whole],
        out_shape=[jax.ShapeDtypeStruct((nb, HEADS, s, VDIM), F32),
                   jax.ShapeDtypeStruct((nb, HEADS, 1, s), F32),
                   jax.ShapeDtypeStruct(w_in_part.shape, BF16)],
        input_output_aliases={6: 2},
        scratch_shapes=[pltpu.VMEM((2, tq, s), F32), pltpu.VMEM((2, tq, s), BF16)]
        + gat.vmem_scratch(w_in_shard.shape, w_in_part.shape),
        compiler_params=_cparams("arbitrary", "arbitrary"),
    )(q, k, v, km, vm, w_in_shard, w_in_part)


def _shift_rows(a, prev, n_rows):
    rid = lax.broadcasted_iota(jnp.int32, a.shape, 0)
    a1 = jnp.where(rid == 0, prev[7:8, :], pltpu.roll(a, 1, 0))
    a2 = jnp.where(rid == 0, prev[6:7, :], jnp.where(rid == 1, prev[7:8, :], pltpu.roll(a, 2, 0)))
    return a1, a2


def _attn_gate(o, za, ga_h):
    on, r = _rms(o, ga_h)
    return on * (za * _sigmoid(za)), on, r


def _out_fwd_bwd(x2d, tgt2d, o, meta, norm_g, w_in_p, conv_w, ga, gc, gmat, w_out, gf, nb, s, tm):
    nt = s // tm
    r = nb * s

    def body(x_ref, t_ref, o_ref, mt_ref, g_ref, wi_ref, cw_ref, ga_ref, gc_ref, gm_ref, w_ref, gf_ref,
             dh_ref, dy_ref, dw_ref, dgf_ref, loss_ref, p_ref, pm_ref, last_cc):
        i = pl.program_id(0)
        blk = lambda ref, j, rows=slice(None): ref[rows, 512 * j:512 * (j + 1)]

        def tail(xv):
            u, _ = _rms(xv, g_ref[...])
            return _dot_nt(u.astype(BF16), wi_ref[IN_HEAD:IN_PAD, :])

        @pl.when(i == 0)
        def _():
            dw_ref[...] = jnp.zeros_like(dw_ref)
            dgf_ref[...] = jnp.zeros_like(dgf_ref)
            loss_ref[...] = jnp.zeros_like(loss_ref)
            last_cc[...] = jnp.zeros_like(last_cc)
            pm_ref[...] = tail(mt_ref[...])

        u16 = _rms(x_ref[...], g_ref[...])[0].astype(BF16)

        def project(j):
            p_ref[:, 512 * j:512 * (j + 1)] = _dot_nt(u16, wi_ref[IN_HEAD + 512 * j:IN_HEAD + 512 * (j + 1), :])

        project(BLK_ZA)
        project(BLK_CC)
        project(BLK_CH)
        ya = []
        for h in range(HEADS):
            y, _, _ = _attn_gate(o_ref[0, h], p_ref[:, 512 * BLK_ZA + VDIM * h:512 * BLK_ZA + VDIM * (h + 1)],
                                 ga_ref[:, VDIM * h:VDIM * (h + 1)])
            ya.append(y)
        project(BLK_CB)
        project(BLK_ZC)
        cc = blk(p_ref, BLK_CC) * blk(p_ref, BLK_CH)
        meta_cc = blk(pm_ref, BLK_CC, slice(8, 16)) * blk(pm_ref, BLK_CH, slice(8, 16))
        prev = jnp.where(i % nt == 0, meta_cc, last_cc[...])
        last_cc[...] = cc[tm - 8:tm, :]
        cc1, cc2 = _shift_rows(cc, prev, tm)
        yc = blk(p_ref, BLK_CB) * (cw_ref[0:1, :] * cc2 + cw_ref[1:2, :] * cc1 + cw_ref[2:3, :] * cc)
        rg = lax.rsqrt(_group_mean(yc * yc, gm_ref[...]) + EPS)
        zc = blk(p_ref, BLK_ZC)
        yconv = yc * rg * gc_ref[...] * (zc * _sigmoid(zc))
        ycat = jnp.concatenate(ya + [yconv], axis=1).astype(BF16)
        h2 = x_ref[...] + _dot(ycat, w_ref[...])
        gfv = gf_ref[...]
        y, r2 = _rms(h2, gfv)
        e = y - t_ref[...]
        loss_ref[...] += 0.5 * jnp.sum(e * e) / D_MODEL
        dyv = e * (1.0 / D_MODEL)
        dh2, dgf = _rms_bwd(dyv, h2, r2, gfv)
        dgf_ref[...] += jnp.sum(dgf, axis=0, keepdims=True)
        dh_ref[...] = dh2
        dhb = dh2.astype(BF16)
        dy_ref[...] = _dot_nt(dhb, w_ref[...])
        dw_ref[...] += _dot_tn(ycat, dhb)

    row = lambda w: pl.BlockSpec((tm, w), lambda i: (i, 0))
    const = lambda shape: pl.BlockSpec(shape, lambda i: (0,) * len(shape))
    full = lambda a: const(a.shape)
    return pl.pallas_call(
        body, name="out_fwd_bwd", grid=(nb * nt,),
        in_specs=[row(D_MODEL), row(D_MODEL),
                  pl.BlockSpec((1, HEADS, tm, VDIM), lambda i: (i // nt, 0, i % nt, 0)),
                  full(meta), full(norm_g), full(w_in_p),
                  full(conv_w), full(ga), full(gc), full(gmat), full(w_out), full(gf)],
        out_specs=[row(D_MODEL), row(D_MODEL), const((D_MODEL, D_MODEL)), const((1, D_MODEL)), const((1, 128)),
                   row(IN_TAIL), const((N_META, IN_TAIL))],
        out_shape=[jax.ShapeDtypeStruct((r, D_MODEL), F32), jax.ShapeDtypeStruct((r, D_MODEL), F32),
                   jax.ShapeDtypeStruct((D_MODEL, D_MODEL), F32), jax.ShapeDtypeStruct((1, D_MODEL), F32),
                   jax.ShapeDtypeStruct((1, 128), F32),
                   jax.ShapeDtypeStruct((r, IN_TAIL), F32), jax.ShapeDtypeStruct((N_META, IN_TAIL), F32)],
        scratch_shapes=[pltpu.VMEM((8, 512), F32)],
        compiler_params=_cparams("arbitrary"),
    )(x2d, tgt2d, o, meta, norm_g, w_in_p, conv_w, ga, gc, gmat, w_out, gf)


def _gate_bwd(dycat, o, p, pm, conv_w, ga, gc, gmat, nb, s, tm):
    nt = s // tm
    r = nb * s
    ext = tm + 8
    prev_idx = lambda i: jnp.maximum(i * (tm // 8) - 1, 0)
    next_idx = lambda i: jnp.minimum((i + 1) * (tm // 8), r // 8 - 1)

    def body(dya_ref, dyc_ref, dycn_ref, o_ref, za_ref, cb_ref, cbn_ref, cc_ref, ccp_ref, ccn_ref,
             ch_ref, chp_ref, chn_ref, zc_ref, zcn_ref, mc_ref, mh_ref, cw_ref, ga_ref, gc_ref, gm_ref,
             dpb_ref, do_ref, dl_ref, dccm_ref, dga_ref, dgc_ref, dcw_ref):
        i = pl.program_id(0)

        @pl.when(i == 0)
        def _():
            dga_ref[...] = jnp.zeros_like(dga_ref)
            dgc_ref[...] = jnp.zeros_like(dgc_ref)
            dcw_ref[...] = jnp.zeros_like(dcw_ref)

        dga = []
        for h in range(HEADS):
            hs = slice(VDIM * h, VDIM * (h + 1))
            oh, za, gah, dya = o_ref[0, h], za_ref[:, hs], ga_ref[:, hs], dya_ref[:, hs]
            sg = _sigmoid(za)
            on, ro = _rms(oh, gah)
            don = dya * (za * sg)
            dpb_ref[:, hs] = (dya * on * (sg * (1.0 + za * (1.0 - sg)))).astype(BF16)
            do, dg = _rms_bwd(don, oh, ro, gah)
            dga.append(jnp.sum(dg, axis=0, keepdims=True))
            dob = do.astype(BF16)
            do_ref[0, h] = dob
            dl_ref[0, h] = _row_of(jnp.sum(dob.astype(F32) * oh, axis=1, keepdims=True), tm)
        dga_ref[...] += jnp.concatenate(dga, axis=1)

        cat = lambda a, b: jnp.concatenate([a[...], b[...]], axis=0)
        cch = cat(cc_ref, ccn_ref)
        chh = cat(ch_ref, chn_ref)
        cb = cat(cb_ref, cbn_ref)
        zc = cat(zc_ref, zcn_ref)
        dy = cat(dyc_ref, dycn_ref)
        first = i % nt == 0
        last = i % nt == nt - 1
        cc = cch * chh
        prev = jnp.where(first, mc_ref[8:16, :] * mh_ref[8:16, :], ccp_ref[...] * chp_ref[...])
        cc1, cc2 = _shift_rows(cc, prev, ext)
        w0, w1, w2 = cw_ref[0:1, :], cw_ref[1:2, :], cw_ref[2:3, :]
        dw = w0 * cc2 + w1 * cc1 + w2 * cc
        yc = cb * dw
        rg = lax.rsqrt(_group_mean(yc * yc, gm_ref[...]) + EPS)
        ych = yc * rg
        gcv = gc_ref[...]
        sg = _sigmoid(zc)
        dycn = dy * (zc * sg)
        dzc = dy * (ych * gcv) * (sg * (1.0 + zc * (1.0 - sg)))
        dgc_ref[...] += jnp.sum((dycn * ych)[:tm], axis=0, keepdims=True)
        dycg = dycn * gcv
        dyc = rg * (dycg - ych * _group_mean(dycg * ych, gm_ref[...]))
        rid = lax.broadcasted_iota(jnp.int32, (ext, CONV_W), 0)
        ddw = jnp.where(jnp.logical_and(last, rid >= tm), 0.0, dyc * cb)
        dcb = dyc * dw
        dcc = w2 * ddw + w1 * pltpu.roll(ddw, ext - 1, 0) + w0 * pltpu.roll(ddw, ext - 2, 0)
        dpb_ref[:, 512:1024] = dcb[:tm].astype(BF16)
        dpb_ref[:, 1024:1536] = (dcc * chh)[:tm].astype(BF16)
        dpb_ref[:, 1536:2048] = (dcc * cch)[:tm].astype(BF16)
        dpb_ref[:, 2048:2560] = dzc[:tm].astype(BF16)
        rs = lambda a: jnp.sum(a[:tm], axis=0, keepdims=True)
        dcw_ref[0:1, :] += rs(ddw * cc2)
        dcw_ref[1:2, :] += rs(ddw * cc1)
        dcw_ref[2:3, :] += rs(ddw * cc)

        @pl.when(first)
        def _():
            d0, d1 = ddw[0:1, :], ddw[1:2, :]
            r8 = lax.broadcasted_iota(jnp.int32, (8, CONV_W), 0)
            dccm_ref[0] = jnp.where(r8 == 7, w1 * d0 + w0 * d1, jnp.where(r8 == 6, w0 * d0, 0.0))

    row = lambda j: pl.BlockSpec((tm, 512), lambda i: (i, j))
    prv = lambda j: pl.BlockSpec((8, 512), lambda i: (prev_idx(i), j))
    nxt = lambda j: pl.BlockSpec((8, 512), lambda i: (next_idx(i), j))
    mblk = lambda j: pl.BlockSpec((N_META, 512), lambda i: (0, j))
    full = lambda a: pl.BlockSpec(a.shape, lambda i: (0,) * a.ndim)
    hb = lambda w: pl.BlockSpec((1, HEADS, tm, w), lambda i: (i // nt, 0, i % nt, 0))
    acc = lambda rr: pl.BlockSpec((rr, 512), lambda i: (0, 0))
    return pl.pallas_call(
        body, name="gate_bwd", grid=(nb * nt,),
        in_specs=[row(0), row(1), nxt(1), hb(VDIM),
                  row(BLK_ZA), row(BLK_CB), nxt(BLK_CB), row(BLK_CC), prv(BLK_CC), nxt(BLK_CC),
                  row(BLK_CH), prv(BLK_CH), nxt(BLK_CH), row(BLK_ZC), nxt(BLK_ZC),
                  mblk(BLK_CC), mblk(BLK_CH), full(conv_w), full(ga), full(gc), full(gmat)],
        out_specs=[pl.BlockSpec((tm, 2560), lambda i: (i, 0)), hb(VDIM),
                   pl.BlockSpec((1, HEADS, 1, tm), lambda i: (i // nt, 0, 0, i % nt)),
                   pl.BlockSpec((1, 8, 512), lambda i: (i // nt, 0, 0)),
                   acc(1), acc(1), acc(8)],
        out_shape=[jax.ShapeDtypeStruct((r, 2560), BF16), jax.ShapeDtypeStruct((nb, HEADS, s, VDIM), BF16),
                   jax.ShapeDtypeStruct((nb, HEADS, 1, s), F32), jax.ShapeDtypeStruct((nb, 8, 512), F32),
                   jax.ShapeDtypeStruct((1, 512), F32), jax.ShapeDtypeStruct((1, 512), F32),
                   jax.ShapeDtypeStruct((8, 512), F32)],
        compiler_params=_cparams("arbitrary"),
    )(dycat, dycat, dycat, o, p, p, p, p, p, p, p, p, p, p, p, pm, pm, conv_w, ga, gc, gmat)


class _StagedReduce:
    LOC, PRE_S, PRE_R, ICI_S, ICI_R, POST_S, POST_R, OUT, N_SEM = 0, 1, 2, 3, 6, 9, 10, 11, 12

    def __init__(self, shard_shape):
        self.half = (shard_shape[0] // 2, shard_shape[1])

    def scratch(self):
        h = self.half
        return [pltpu.VMEM((4,) + h, F32), pltpu.VMEM((4,) + h, F32), pltpu.VMEM((4,) + h, BF16),
                pltpu.VMEM((3,) + h, BF16), pltpu.VMEM(h, F32), pltpu.SemaphoreType.DMA((self.N_SEM,))]

    def run(self, stage, pin, gout, scr):
        own, sib, wire, rbuf, fin, sems = scr
        r2 = self.half[0]
        x, y, c = lax.axis_index("x"), lax.axis_index("y"), lax.axis_index("c")
        mine = 2 * x + y
        sibling = (x, y, 1 - c)
        chips = [(1 - x, y), (x, 1 - y), (1 - x, 1 - y)]
        rows = lambda half: pl.ds(pl.multiple_of(half * r2, r2), r2)
        mesh = pl.DeviceIdType.MESH

        loc = pltpu.make_async_copy(pin.at[:, rows(c), :], own, sems.at[self.LOC])
        pre = pltpu.make_async_remote_copy(
            src_ref=pin.at[:, rows(1 - c), :], dst_ref=sib, send_sem=sems.at[self.PRE_S],
            recv_sem=sems.at[self.PRE_R], device_id=sibling, device_id_type=mesh)

        def ici(j):
            px, py = chips[j]
            return pltpu.make_async_remote_copy(
                src_ref=wire.at[2 * px + py], dst_ref=rbuf.at[j], send_sem=sems.at[self.ICI_S + j],
                recv_sem=sems.at[self.ICI_R + j], device_id=(px, py, c), device_id_type=mesh)

        def post(half):
            return pltpu.make_async_remote_copy(
                src_ref=fin, dst_ref=gout.at[rows(half), :], send_sem=sems.at[self.POST_S],
                recv_sem=sems.at[self.POST_R], device_id=sibling, device_id_type=mesh)

        keep = pltpu.make_async_copy(fin, gout.at[rows(c), :], sems.at[self.OUT])
        if stage == 0:
            loc.start()
            pre.start()
        elif stage == 1:
            loc.wait()
            pre.wait_recv()
            for blk in range(4):
                tot = own[blk] + sib[blk]
                own[blk] = tot
                wire[blk] = tot.astype(BF16)
            for j in range(3):
                ici(j).start()
        elif stage == 2:
            for j in range(3):
                ici(j).wait_recv()
            tot = own[mine]
            for j in range(3):
                tot = tot + rbuf[j].astype(F32)
            fin[...] = tot
            post(c).start()
            keep.start()
        else:
            post(1 - c).wait_recv()
            pre.wait_send()
            for j in range(3):
                ici(j).wait_send()
            post(c).wait_send()
            keep.wait()


def _attn_bwd(q, k, v, do, lse, delta, km, vm, early, nb, s, t):
    n = s // t
    ne = len(early)
    reds = [_StagedReduce(a.shape[1:]) for a in early]
    n_steps = HEADS * nb
    assert n_steps >= 4

    def body(q_ref, k_ref, v_ref, do_ref, lse_ref, dl_ref, km_ref, vm_ref, *rest):
        pin_refs, rest = rest[:ne], rest[ne:]
        dq_ref, dk_ref, dv_ref, dkm_ref, dvm_ref = rest[:5]
        gout_refs, (p_scr, ds_scr, dq_acc), red_scr = rest[5:5 + ne], rest[5 + ne:8 + ne], rest[8 + ne:]
        b = pl.program_id(1)
        step = pl.program_id(0) * nb + b
        for stage, at in enumerate((0, 1, n_steps - 2, n_steps - 1)):
            @pl.when(step == at)
            def _(stage=stage):
                for a, red in enumerate(reds):
                    red.run(stage, pin_refs[a], gout_refs[a], red_scr[6 * a:6 * a + 6])

        @pl.when(b == 0)
        def _():
            dkm_ref[...] = jnp.zeros_like(dkm_ref)
            dvm_ref[...] = jnp.zeros_like(dvm_ref)

        kr = lax.broadcasted_iota(jnp.int32, (t, t), 0)
        qc = lax.broadcasted_iota(jnp.int32, (t, t), 1)
        km_v, vm_v = km_ref[0, 0], vm_ref[0, 0]
        ptm = jnp.exp(_dot_nt(km_v, q_ref[0, 0]) - lse_ref[0, 0])
        dstm = (ptm * (_dot_nt(vm_v, do_ref[0, 0]) - dl_ref[0, 0])).astype(BF16)
        dkm_ref[0] += _dot(dstm, q_ref[0, 0])
        dvm_ref[0] += _dot(ptm.astype(BF16), do_ref[0, 0])
        dq_acc[...] = _dot_tn(dstm, km_v)
        def tiles(j):
            slot = j % 2
            kj = k_ref[0, 0, j * t:(j + 1) * t, :]
            vj = v_ref[0, 0, j * t:(j + 1) * t, :]
            def products(i):
                cs = slice(i * t, (i + 1) * t)
                return _dot_nt(kj, q_ref[0, 0, cs, :]), _dot_nt(vj, do_ref[0, 0, cs, :])

            nxt, pending = products(j), None
            for i in range(j, n):
                cs = slice(i * t, (i + 1) * t)
                st, dpt = nxt
                if i + 1 < n:
                    nxt = products(i + 1)
                if i == j:
                    st = jnp.where(kr <= qc, st, NEG_INF)
                pt = jnp.exp(st - lse_ref[0, 0, :, cs])
                dst = (pt * (dpt - dl_ref[0, 0, :, cs])).astype(BF16)
                p_scr[slot, :, cs] = pt.astype(BF16)
                ds_scr[slot, :, cs] = dst
                if pending is not None:
                    dq_acc[pending[0], :] += _dot_tn(pending[1], kj)
                pending = (cs, dst)
            dq_acc[pending[0], :] += _dot_tn(pending[1], kj)

        for j in range(n):
            slot = j % 2
            tiles(j)
            dv_ref[0, 0, j * t:(j + 1) * t, :] = _dot(p_scr[slot, :, j * t:s], do_ref[0, 0, j * t:s, :]).astype(BF16)
            dk_ref[0, 0, j * t:(j + 1) * t, :] = _dot(ds_scr[slot, :, j * t:s], q_ref[0, 0, j * t:s, :]).astype(BF16)
        dq_ref[0, 0] = dq_acc[...].astype(BF16)

    big = lambda w: pl.BlockSpec((1, 1, s, w), lambda h, b: (b, h, 0, 0))
    rowv = pl.BlockSpec((1, 1, 1, s), lambda h, b: (b, h, 0, 0))
    mk = lambda w: pl.BlockSpec((1, 1, N_META, w), lambda h, b: (0, h, 0, 0))
    mo = lambda w: pl.BlockSpec((1, N_META, w), lambda h, b: (h, 0, 0))
    return pl.pallas_call(
        body, name="attn_bwd", grid=(HEADS, nb),
        in_specs=[big(QK_PAD), big(QK_PAD), big(VDIM), big(VDIM), rowv, rowv, mk(QK_PAD), mk(VDIM)]
        + [pl.BlockSpec(memory_space=pl.ANY)] * ne,
        out_specs=[big(QK_PAD), big(QK_PAD), big(VDIM), mo(QK_PAD), mo(VDIM)]
        + [pl.BlockSpec(memory_space=pl.ANY)] * ne,
        out_shape=[jax.ShapeDtypeStruct((nb, HEADS, s, QK_PAD), BF16),
                   jax.ShapeDtypeStruct((nb, HEADS, s, QK_PAD), BF16),
                   jax.ShapeDtypeStruct((nb, HEADS, s, VDIM), BF16),
                   jax.ShapeDtypeStruct((HEADS, N_META, QK_PAD), F32),
                   jax.ShapeDtypeStruct((HEADS, N_META, VDIM), F32)]
        + [jax.ShapeDtypeStruct(a.shape[1:], F32) for a in early],
        scratch_shapes=[pltpu.VMEM((2, t, s), BF16), pltpu.VMEM((2, t, s), BF16), pltpu.VMEM((s, QK_PAD), F32)]
        + [sc for red in reds for sc in red.scratch()],
        compiler_params=_cparams("arbitrary", "arbitrary"),
    )(q, k, v, do, lse, delta, km, vm, *early)


def _up_bwd(dq, dk, dv, dkm, dvm, p, pm, tabs, tabs_m, wq_p, wkv_p, gq, gkv, nb, s, tm):
    nt = s // tm
    n = nb * nt
    c_t, sa_t, sb_t = tabs
    cm_t, sam_t, sbm_t = tabs_m

    def kv_path(dkh, dvh, pa, c, sa, sb, wkv, gkvv):
        dkpe = dkh[0][:, NOPE:]
        for h in range(1, HEADS):
            dkpe = dkpe + dkh[h][:, NOPE:]
        dkr = _rope_bwd(dkpe, c, sa, sb)
        dkv = jnp.concatenate([d[:, :NOPE] for d in dkh] + list(dvh), axis=1).astype(BF16)
        ckv = pa[:, Q_RANK:Q_RANK + KV_RANK]
        kvn, rkv = _rms(ckv, gkvv)
        dckv, dg = _rms_bwd(_dot(dkv, wkv), ckv, rkv, gkvv)
        return dckv, dkr, kvn.astype(BF16), dkv, jnp.sum(dg, axis=0, keepdims=True)

    def body(dq_ref, dk_ref, dv_ref, pa_ref, c_ref, sa_ref, sb_ref,
             dkm_ref, dvm_ref, pam_ref, cm_ref, sam_ref, sbm_ref,
             wq_ref, wkv_ref, gq_ref, gkv_ref,
             dpa_ref, dpam_ref, pq_ref, pkv_ref, dgq_ref, dgkv_ref, dwq_ref, dwkv_ref):
        i = pl.program_id(0)

        @pl.when(i == 0)
        def _():
            dwq_ref[...] = jnp.zeros_like(dwq_ref)
            dwkv_ref[...] = jnp.zeros_like(dwkv_ref)
            dgq_ref[...] = jnp.zeros_like(dgq_ref)
            dgkv_ref[...] = jnp.zeros_like(dgkv_ref)

        @pl.when(i < n)
        def _():
            c, sa, sb = c_ref[...], sa_ref[...], sb_ref[...]
            pa = pa_ref[...]
            parts = []
            for h in range(HEADS):
                dqh = dq_ref[0, h].astype(F32) * ATTN_SCALE
                parts += [dqh[:, :NOPE], _rope_bwd(dqh[:, NOPE:], c, sa, sb)]
            dql = jnp.concatenate(parts, axis=1).astype(BF16)
            cq = pa[:, 0:Q_RANK]
            gqv = gq_ref[...]
            qn, rq = _rms(cq, gqv)
            dwq_ref[...] += _dot_tn(dql, qn.astype(BF16))
            dcq, dg = _rms_bwd(_dot(dql, wq_ref[...]), cq, rq, gqv)
            dgq_ref[...] += jnp.sum(dg, axis=0, keepdims=True)
            dckv, dkr, kvn, dkv, dgk = kv_path([dk_ref[0, h].astype(F32) for h in range(HEADS)],
                                               [dv_ref[0, h].astype(F32) for h in range(HEADS)],
                                               pa, c, sa, sb, wkv_ref[...], gkv_ref[...])
            dwkv_ref[...] += _dot_tn(dkv, kvn)
            dgkv_ref[...] += dgk
            dpa_ref[...] = jnp.concatenate([dcq, dckv, dkr], axis=1).astype(BF16)

        @pl.when(i == n)
        def _():
            dckv, dkr, kvn, dkv, dgk = kv_path([dkm_ref[h] for h in range(HEADS)],
                                               [dvm_ref[h] for h in range(HEADS)],
                                               pam_ref[...], cm_ref[...], sam_ref[...], sbm_ref[...],
                                               wkv_ref[...], gkv_ref[...])
            dwkv_ref[...] += _dot_tn(dkv, kvn)
            dgkv_ref[...] += dgk
            dpam_ref[...] = jnp.concatenate([jnp.zeros((N_META, Q_RANK), F32), dckv, dkr], axis=1)
            for h in range(HEADS):
                pq_ref[h] = dwq_ref[QK_PAD * h:QK_PAD * h + NOPE + ROPE, :]
                pkv_ref[h, 0:NOPE, :] = dwkv_ref[NOPE * h:NOPE * (h + 1), :]
                pkv_ref[h, NOPE:NOPE + VDIM, :] = dwkv_ref[512 + VDIM * h:512 + VDIM * (h + 1), :]

    cl = lambda i: jnp.minimum(i, n - 1)
    hb = lambda w: pl.BlockSpec((1, HEADS, tm, w), lambda i: (cl(i) // nt, 0, cl(i) % nt, 0))
    tab = pl.BlockSpec((tm, 128), lambda i: (cl(i) % nt, 0))
    full = lambda a: pl.BlockSpec(a.shape, lambda i: (0,) * a.ndim)
    const = lambda shape: pl.BlockSpec(shape, lambda i: (0,) * len(shape))
    return pl.pallas_call(
        body, name="up_bwd", grid=(n + 1,),
        in_specs=[hb(QK_PAD), hb(QK_PAD), hb(VDIM), pl.BlockSpec((tm, 512), lambda i: (cl(i), 0)), tab, tab, tab,
                  full(dkm), full(dvm), pl.BlockSpec((N_META, 512), lambda i: (0, 0)),
                  full(cm_t), full(sam_t), full(sbm_t), full(wq_p), full(wkv_p), full(gq), full(gkv)],
        out_specs=[pl.BlockSpec((tm, 512), lambda i: (cl(i), 0)), const((N_META, 512)),
                   const((HEADS, NOPE + ROPE, Q_RANK)), const((HEADS, NOPE + VDIM, KV_RANK)),
                   const((1, Q_RANK)), const((1, KV_RANK))],
        out_shape=[jax.ShapeDtypeStruct((nb * s, 512), BF16), jax.ShapeDtypeStruct((N_META, 512), F32),
                   jax.ShapeDtypeStruct((HEADS, NOPE + ROPE, Q_RANK), F32),
                   jax.ShapeDtypeStruct((HEADS, NOPE + VDIM, KV_RANK), F32),
                   jax.ShapeDtypeStruct((1, Q_RANK), F32), jax.ShapeDtypeStruct((1, KV_RANK), F32)],
        scratch_shapes=[pltpu.VMEM((HEADS * QK_PAD, Q_RANK), F32), pltpu.VMEM((1024, KV_RANK), F32)],
        compiler_params=_cparams("arbitrary"),
    )(dq, dk, dv, p, c_t, sa_t, sb_t, dkm, dvm, pm, cm_t, sam_t, sbm_t, wq_p, wkv_p, gq, gkv)


def _in_bwd(x2d, dh2, dpa, dpb, meta, dpam, dccm, pm, w_in_p, norm_g, nb, s, tm):
    nt = s // tm
    n = nb * nt

    def body(x_ref, dh_ref, dpa_ref, dpb_ref, mt_ref, dpam_ref, dccm_ref, mc_ref, mh_ref, w_ref, g_ref,
             gx_ref, gm_ref, dw_hbm, dg_ref, acc_ref, sems):
        i = pl.program_id(0)

        @pl.when(i == 0)
        def _():
            acc_ref[...] = jnp.zeros_like(acc_ref)
            dg_ref[...] = jnp.zeros_like(dg_ref)

        def rows(x, dp, dres):
            g = g_ref[...]
            dpb16 = dp.astype(BF16)
            du = _dot(dpb16, w_ref[...])
            u, r1 = _rms(x, g)
            acc_ref[...] += _dot_tn(dpb16, u.astype(BF16))
            dx, dg = _rms_bwd(du, x, r1, g)
            dg_ref[...] += jnp.sum(dg, axis=0, keepdims=True)
            return dx if dres is None else dx + dres

        @pl.when(i < n)
        def _():
            dp = jnp.concatenate([dpa_ref[...], dpb_ref[...]], axis=1)
            gx_ref[...] = rows(x_ref[...], dp, dh_ref[...])

        @pl.when(i == n)
        def _():
            dcc = dccm_ref[0]
            for b in range(1, nb):
                dcc = dcc + dccm_ref[b]
            z8 = jnp.zeros((8, CONV_W), F32)
            dc = jnp.concatenate([z8, dcc * mh_ref[8:16, :]], axis=0)
            dh = jnp.concatenate([z8, dcc * mc_ref[8:16, :]], axis=0)
            z = jnp.zeros((N_META, CONV_W), F32)
            dp = jnp.concatenate([dpam_ref[...], z, z, dc, dh, z], axis=1)
            gm_ref[...] = rows(mt_ref[...], dp, None)
            per = IN_DIM // 4
            cps = [pltpu.make_async_copy(acc_ref.at[0:448], dw_hbm.at[0, 0:448], sems.at[0]),
                   pltpu.make_async_copy(acc_ref.at[512:per + 64], dw_hbm.at[0, 448:per], sems.at[1])]
            for qq in range(1, 4):
                cps.append(pltpu.make_async_copy(acc_ref.at[per * qq + 64:per * (qq + 1) + 64], dw_hbm.at[qq],
                                                 sems.at[qq + 1]))
            for cp in cps:
                cp.start()
            for cp in cps:
                cp.wait()

    cl = lambda i: jnp.minimum(i, n - 1)
    row = lambda w: pl.BlockSpec((tm, w), lambda i: (cl(i), 0))
    full = lambda a: pl.BlockSpec(a.shape, lambda i: (0,) * a.ndim)
    mblk = lambda j: pl.BlockSpec((N_META, 512), lambda i: (0, j))
    return pl.pallas_call(
        body, name="in_bwd", grid=(n + 1,),
        in_specs=[row(D_MODEL), row(D_MODEL), row(512), row(2560), full(meta), full(dpam), full(dccm),
                  mblk(BLK_CC), mblk(BLK_CH), full(w_in_p), full(norm_g)],
        out_specs=[row(D_MODEL), pl.BlockSpec((N_META, D_MODEL), lambda i: (0, 0)),
                   pl.BlockSpec(memory_space=pl.ANY), pl.BlockSpec((1, D_MODEL), lambda i: (0, 0))],
        out_shape=[jax.ShapeDtypeStruct((nb * s, D_MODEL), F32), jax.ShapeDtypeStruct((N_META, D_MODEL), F32),
                   jax.ShapeDtypeStruct((4, IN_DIM // 4, D_MODEL), F32), jax.ShapeDtypeStruct((1, D_MODEL), F32)],
        scratch_shapes=[pltpu.VMEM((IN_PAD, D_MODEL), F32), pltpu.SemaphoreType.DMA((5,))],
        compiler_params=_cparams("arbitrary"),
    )(x2d, dh2, dpa, dpb, meta, dpam, dccm, pm, pm, w_in_p, norm_g)


def _gather_weights(w_in_shard, split, pieces, out_rows, whole, zero_fills):
    ns, nw, nz = len(split), len(whole), len(zero_fills)
    flat = [(a, pc) for a in range(ns) for pc in pieces[a]]
    nk = len(flat)
    hh = HEAD_ROWS // 2

    def body(*refs):
        ins, wins, zins = refs[1:1 + ns], refs[1 + ns:1 + ns + nw], refs[1 + ns + nw:1 + ns + nw + nz]
        n_in = 1 + ns + nw + nz
        head_ref, shard16 = refs[n_in], refs[n_in + 1]
        outs, wouts = refs[n_in + 2:n_in + 2 + ns], refs[n_in + 2 + ns:n_in + 2 + ns + nw]
        scr = refs[n_in + 2 + ns + nw:]
        stage = scr[:ns]
        (send_sems, recv_sems, fwd_send, fwd_recv, loc_sems, w_send, w_recv, w_loc, z_sems,
         h_send, h_recv, h_pass) = scr[ns:]
        x, y, c = lax.axis_index("x"), lax.axis_index("y"), lax.axis_index("c")
        mine = 2 * x + y
        chips = [(1 - x, y), (x, 1 - y), (1 - x, 1 - y)]
        chip_of = [2 * px + py for px, py in chips]
        shard16[...] = refs[0][...].astype(BF16)
        for a in range(ns):
            stage[a][...] = ins[a][...].astype(BF16)

        def head_rows(half):
            return pl.ds(pl.multiple_of(half * hh, 16), hh)

        def head_copy(j):
            px, py = chips[j]
            return pltpu.make_async_remote_copy(
                src_ref=shard16.at[head_rows(c)], dst_ref=head_ref.at[head_rows(c)], send_sem=h_send.at[j],
                recv_sem=h_recv.at[0], device_id=(px, py, c), device_id_type=pl.DeviceIdType.MESH)

        def head_pass(half):
            ref = head_ref.at[head_rows(half)]
            return pltpu.make_async_remote_copy(
                src_ref=ref, dst_ref=ref, send_sem=h_pass.at[0], recv_sem=h_pass.at[1],
                device_id=(x, y, 1 - c), device_id_type=pl.DeviceIdType.MESH)

        head_ref[HEAD_ROWS:IN_HEAD, :] = jnp.zeros((IN_HEAD - HEAD_ROWS, D_MODEL), BF16)

        @pl.when(mine == 0)
        def _():
            for j in range(3):
                head_copy(j).start()
            head_ref[0:HEAD_ROWS, :] = shard16[0:HEAD_ROWS, :]

        def src(k):
            a, (s0, nr, _, _, _, _) = flat[k]
            return stage[a].at[s0:s0 + nr]

        def dst(k, q):
            a, (_, nr, per, first, rest, _) = flat[k]
            row = per * q + first + (rest - first) * jnp.minimum(q, 1)
            return outs[a].at[pl.ds(pl.multiple_of(row, 16), nr)]

        def ici(k, j, q):
            px, py = chips[j]
            return pltpu.make_async_remote_copy(
                src_ref=src(k), dst_ref=dst(k, q), send_sem=send_sems.at[k, j], recv_sem=recv_sems.at[k, j],
                device_id=(px, py, c), device_id_type=pl.DeviceIdType.MESH)

        def fwd(k, j):
            ref = dst(k, chip_of[j])
            return pltpu.make_async_remote_copy(
                src_ref=ref, dst_ref=ref, send_sem=fwd_send.at[k, j], recv_sem=fwd_recv.at[k, j],
                device_id=(x, y, 1 - c), device_id_type=pl.DeviceIdType.MESH)

        def wcopy(b, j, q):
            px, py = chips[j]
            return pltpu.make_async_remote_copy(
                src_ref=wins[b], dst_ref=wouts[b].at[q], send_sem=w_send.at[b, j], recv_sem=w_recv.at[b, j],
                device_id=(px, py, c), device_id_type=pl.DeviceIdType.MESH)

        local = [pltpu.make_async_copy(src(k), dst(k, mine), loc_sems.at[k]) for k in range(nk)]
        local += [pltpu.make_async_copy(wins[b], wouts[b].at[mine], w_loc.at[b]) for b in range(nw)]
        for z, (a, _, row0) in enumerate(zero_fills):
            local.append(pltpu.make_async_copy(zins[z], outs[a].at[row0:row0 + zins[z].shape[0]], z_sems.at[z]))
        wsends = [wcopy(b, j, mine) for b in range(nw) for j in range(3)]
        for cp in local + wsends:
            cp.start()

        for half in (0, 1):
            @pl.when(c == half)
            def _(half=half):
                my_k = [k for k in range(nk) if flat[k][1][5] == half]
                other_k = [k for k in range(nk) if flat[k][1][5] != half]
                sends = [ici(k, j, mine) for k in my_k for j in range(3)]
                for cp in sends:
                    cp.start()
                passed = []
                for k in my_k:
                    for j in range(3):
                        ici(k, j, chip_of[j]).wait_recv()
                        cp = fwd(k, j)
                        cp.start()
                        passed.append(cp)
                for k in other_k:
                    for j in range(3):
                        fwd(k, j).wait_recv()
                for cp in sends + passed:
                    cp.wait_send()

        for b in range(nw):
            for j in range(3):
                wcopy(b, j, chip_of[j]).wait_recv()
        for cp in wsends:
            cp.wait_send()
        for cp in local:
            cp.wait()

        @pl.when(mine == 0)
        def _():
            for j in range(3):
                head_copy(j).wait_send()

        @pl.when(mine != 0)
        def _():
            head_copy(0).wait_recv()
            cp = head_pass(c)
            cp.start()
            head_pass(1 - c).wait_recv()
            cp.wait_send()

    vmem = pl.BlockSpec(memory_space=pltpu.VMEM)
    dma = pltpu.SemaphoreType.DMA
    zeros = [z for _, z, _ in zero_fills]
    return pl.pallas_call(
        body, name="gather_weights",
        in_specs=[vmem] * (1 + ns + nw + nz), out_specs=[vmem] * (2 + ns + nw),
        out_shape=([jax.ShapeDtypeStruct((IN_HEAD, D_MODEL), BF16), jax.ShapeDtypeStruct(w_in_shard.shape, BF16)]
                   + [jax.ShapeDtypeStruct((out_rows[a], split[a].shape[1]), BF16) for a in range(ns)]
                   + [jax.ShapeDtypeStruct((4,) + w.shape, w.dtype) for w in whole]),
        scratch_shapes=[pltpu.VMEM(a.shape, BF16) for a in split]
        + [dma((nk, 3)), dma((nk, 3)), dma((nk, 3)), dma((nk, 3)), dma((nk,)),
           dma((nw, 3)), dma((nw, 3)), dma((nw,)), dma((nz,)), dma((3,)), dma((1,)), dma((2,))],
        compiler_params=pltpu.CompilerParams(vmem_limit_bytes=VMEM_LIMIT),
    )(w_in_shard, *split, *whole, *zeros)


def _reduce_grads(parts, small):
    n = len(parts)
    shapes = [a.shape[1:] for a in parts]
    halves = [(sh[0] // 2, sh[1]) for sh in shapes]

    def body(*refs):
        pin, sm_in = refs[:n], refs[n]
        gout, sm_out = refs[n + 1:2 * n + 1], refs[2 * n + 1]
        scr = refs[2 * n + 2:]
        own, sib, wire, rbuf = scr[:n], scr[n:2 * n], scr[2 * n:3 * n], scr[3 * n:4 * n]
        (sbuf, send_sems, recv_sems, loc_sems, pre_send, pre_recv, post_send, post_recv,
         sm_send, sm_recv) = scr[4 * n:]
        x, y, c = lax.axis_index("x"), lax.axis_index("y"), lax.axis_index("c")
        mine = 2 * x + y
        me = 4 * x + 2 * y + c
        sibling = (x, y, 1 - c)
        chips = [(1 - x, y), (x, 1 - y), (1 - x, 1 - y)]

        def rows(a, half):
            r2 = halves[a][0]
            return pl.ds(pl.multiple_of(half * r2, r2), r2)

        chip_of = [2 * px + py for px, py in chips]
        blocks = chip_of + [mine]

        def pre(a, k):
            return pltpu.make_async_remote_copy(
                src_ref=pin[a].at[blocks[k], rows(a, 1 - c), :], dst_ref=sib[a].at[blocks[k]],
                send_sem=pre_send.at[a, k], recv_sem=pre_recv.at[a, k], device_id=sibling,
                device_id_type=pl.DeviceIdType.MESH)

        def ici(a, j):
            px, py = chips[j]
            return pltpu.make_async_remote_copy(
                src_ref=wire[a].at[2 * px + py], dst_ref=rbuf[a].at[j], send_sem=send_sems.at[a, j],
                recv_sem=recv_sems.at[a, j], device_id=(px, py, c), device_id_type=pl.DeviceIdType.MESH)

        def post(a, half):
            ref = gout[a].at[rows(a, half), :]
            return pltpu.make_async_remote_copy(
                src_ref=ref, dst_ref=ref, send_sem=post_send.at[a], recv_sem=post_recv.at[a],
                device_id=sibling, device_id_type=pl.DeviceIdType.MESH)

        def small_copy(kk):
            peer = (x ^ (kk >> 2), y ^ ((kk >> 1) & 1), c ^ (kk & 1))
            return pltpu.make_async_remote_copy(
                src_ref=sm_in, dst_ref=sbuf.at[kk], send_sem=sm_send.at[kk - 1], recv_sem=sm_recv.at[kk - 1],
                device_id=peer, device_id_type=pl.DeviceIdType.MESH)

        local = [[pltpu.make_async_copy(pin[a].at[blocks[k], rows(a, c), :], own[a].at[blocks[k]], loc_sems.at[a, k])
                  for k in range(4)] for a in range(n)]
        pres = [[pre(a, k) for k in range(4)] for a in range(n)]
        smalls = [small_copy(kk) for kk in range(1, 8)]
        for a in range(n):
            for k in range(4):
                local[a][k].start()
                pres[a][k].start()
        for cp in smalls:
            cp.start()
        sbuf[0] = sm_in[...]
        sends = []
        for a in range(n):
            for k in range(4):
                local[a][k].wait()
                pres[a][k].wait_recv()
                tot = own[a][blocks[k]] + sib[a][blocks[k]]
                own[a][blocks[k]] = tot
                if k < 3:
                    wire[a][blocks[k]] = tot.astype(BF16)
                    cp = ici(a, k)
                    cp.start()
                    sends.append(cp)
        for cp in smalls:
            cp.wait_recv()
        total = sbuf[me]
        for d in range(1, 8):
            total = total + sbuf[me ^ d]
        sm_out[...] = total
        posts = []
        for a in range(n):
            for j in range(3):
                ici(a, j).wait_recv()
            fin = own[a][mine]
            for j in range(3):
                fin = fin + rbuf[a][j].astype(F32)
            gout[a][rows(a, c), :] = fin
            cp = post(a, c)
            cp.start()
            posts.append(cp)
        for a in range(n):
            post(a, 1 - c).wait_recv()
        for cp in [cp for row in pres for cp in row] + sends + smalls + posts:
            cp.wait_send()

    vmem = pl.BlockSpec(memory_space=pltpu.VMEM)
    dma = pltpu.SemaphoreType.DMA
    return pl.pallas_call(
        body, name="reduce_grads",
        in_specs=[pl.BlockSpec(memory_space=pl.ANY)] * n + [vmem], out_specs=[vmem] * (n + 1),
        out_shape=[jax.ShapeDtypeStruct(sh, F32) for sh in shapes] + [jax.ShapeDtypeStruct(small.shape, F32)],
        scratch_shapes=([pltpu.VMEM((4,) + hs, F32) for hs in halves] + [pltpu.VMEM((4,) + hs, F32) for hs in halves]
                        + [pltpu.VMEM((4,) + hs, BF16) for hs in halves]
                        + [pltpu.VMEM((3,) + hs, BF16) for hs in halves]
                        + [pltpu.VMEM((8,) + small.shape, F32), dma((n, 3)), dma((n, 3)), dma((n, 4)),
                           dma((n, 4)), dma((n, 4)), dma((n,)), dma((n,)), dma((7,)), dma((7,))]),
        compiler_params=pltpu.CompilerParams(vmem_limit_bytes=VMEM_LIMIT),
    )(*parts, small)


def _adamw_update(w_ref, g_ref, m_ref, v_ref, d_ref, nm_ref, nv_ref):
    gv = g_ref[...]
    nm = ADAM_B1 * m_ref[...] + (1.0 - ADAM_B1) * gv
    nv = ADAM_B2 * v_ref[...] + (1.0 - ADAM_B2) * (gv * gv)
    m_hat = nm / (1.0 - ADAM_B1 ** ADAM_STEP)
    v_hat = nv / (1.0 - ADAM_B2 ** ADAM_STEP)
    d_ref[...] = -ADAM_LR * (m_hat / (jnp.sqrt(v_hat) + ADAM_EPS) + ADAM_WD * w_ref[...])
    nm_ref[...] = nm
    nv_ref[...] = nv


def _adamw_small(ws, gs, ms, vs):
    k = len(ws)

    def body(*refs):
        ins, outs = refs[:4 * k], refs[4 * k:]
        for a in range(k):
            _adamw_update(ins[a], ins[k + a], ins[2 * k + a], ins[3 * k + a], outs[a], outs[k + a], outs[2 * k + a])

    out = pl.pallas_call(
        body, name="adamw_small",
        out_shape=[jax.ShapeDtypeStruct(w.shape, F32) for w in ws] * 3,
        compiler_params=pltpu.CompilerParams(vmem_limit_bytes=VMEM_LIMIT),
    )(*ws, *gs, *ms, *vs)
    return out[:k], out[k:2 * k], out[2 * k:]


def _adamw(w, g, m, v, name):
    shape = w.shape
    w2, g2, m2, v2 = (a.reshape((-1, shape[-1])) for a in (w, g, m, v))

    def body(w_ref, g_ref, m_ref, v_ref, d_ref, nm_ref, nv_ref):
        _adamw_update(w_ref, g_ref, m_ref, v_ref, d_ref, nm_ref, nv_ref)

    rows, cols = w2.shape
    nblk = cols // 256 if cols % 256 == 0 and rows >= 64 else 1
    blk = pl.BlockSpec((rows, cols // nblk), lambda j: (0, j))
    out = pl.pallas_call(
        body, name=name, grid=(nblk,), in_specs=[blk] * 4, out_specs=[blk] * 3,
        out_shape=[jax.ShapeDtypeStruct(w2.shape, F32)] * 3,
        compiler_params=_cparams("parallel"),
    )(w2, g2, m2, v2)
    return tuple(a.reshape(shape) for a in out)


def kernel(x, meta_tokens, norm_g, w_in, q_norm_g, w_q_up, kv_norm_g, w_kv_up, conv_w, attn_out_g, conv_out_g, w_out, final_norm_g, loss_target, m_meta_tokens, m_norm_g, m_w_in, m_q_norm_g, m_w_q_up, m_kv_norm_g, m_w_kv_up, m_conv_w, m_attn_out_g, m_conv_out_g, m_w_out, m_final_norm_g, v_meta_tokens, v_norm_g, v_w_in, v_q_norm_g, v_w_q_up, v_kv_norm_g, v_w_kv_up, v_conv_w, v_attn_out_g, v_conv_out_g, v_w_out, v_final_norm_g):
    nb, s, _ = x.shape
    tm = min(ROW_TILE, s)
    ta = min(ATTN_TILE, s)
    assert s % tm == 0 and s % ta == 0 and tm % 16 == 0
    r = nb * s

    tr = lambda a: jnp.transpose(a[0])
    w_head, w_in_shard, wq_p, wkv_p, g_cw, g_meta = _gather_weights(
        tr(w_in), [tr(w_q_up), tr(w_kv_up)],
        [W_Q_PIECES, W_KV_PIECES], [HEADS * QK_PAD, 1024],
        [jnp.transpose(conv_w, (1, 0, 2)), meta_tokens],
        [(0, jnp.zeros((64, Q_RANK), BF16), QK_PAD * h + NOPE + ROPE) for h in range(HEADS)])
    conv_f = jnp.transpose(g_cw[:, :, 0, :], (1, 0, 2)).reshape(3, CONV_W)
    meta_f = jnp.transpose(g_meta, (1, 0, 2)).reshape(N_META, D_MODEL)

    c_all, sa_all, sb_all = _rope_tables(N_META + s)
    tabs_m = (c_all[:N_META], sa_all[:N_META], sb_all[:N_META])
    tabs = (c_all[N_META:], sa_all[N_META:], sb_all[N_META:])
    gid = np.arange(CONV_W) // CONV_GROUP
    gmat = jnp.asarray(np.where(gid[:, None] == gid[None, :], 1.0 / CONV_GROUP, 0.0), BF16)
    ga, gc = attn_out_g, conv_out_g
    gf = final_norm_g.reshape(1, D_MODEL)

    x2d = x.reshape(r, D_MODEL)
    tgt2d = loss_target.reshape(r, D_MODEL)

    ph, q, k, v, pmh, km, vm, w_out_f, w_in_part = _fwd_proj(
        x2d, meta_f, tabs, tabs_m, norm_g, w_head, q_norm_g, wq_p, kv_norm_g, wkv_p, w_out[0].astype(BF16),
        w_in_shard, nb, s, tm)
    o, lse, w_in_p = _attn_fwd(q, k, v, km, vm, w_in_shard, w_in_part, nb, s, ta)
    dh2, dycat, dw_out, dgf, loss_acc, pt, pmt = _out_fwd_bwd(x2d, tgt2d, o, meta_f, norm_g, w_in_p, conv_f, ga, gc,
                                                              gmat, w_out_f, gf, nb, s, tm)
    dpb, do, delta, dccm, dga, dgc, dcw = _gate_bwd(dycat, o, pt, pmt, conv_f, ga, gc, gmat, nb, s, tm)
    p_out = dw_out.reshape(4, D_MODEL // 4, D_MODEL)
    dq, dk, dv, dkm, dvm, g_w_out = _attn_bwd(q, k, v, do, lse, delta, km, vm, [p_out], nb, s, ta)
    dpa, dpam, p_q, p_kv, dgq, dgkv = _up_bwd(dq, dk, dv, dkm, dvm, ph, pmh, tabs, tabs_m, wq_p, wkv_p,
                                              q_norm_g, kv_norm_g, nb, s, tm)
    gx, gmeta, p_in, dng = _in_bwd(x2d, dh2, dpa, dpb, meta_f, dpam, dccm, pmt, w_in_p, norm_g, nb, s, tm)

    flat =jnp.concatenate([dng.reshape(-1), dgq.reshape(-1), dgkv.reshape(-1), dga.reshape(-1), dgc.reshape(-1),
                            dgf.reshape(-1), dcw[:3].reshape(-1), gmeta.reshape(-1), loss_acc[0, 0:1]])
    n_small = flat.shape[0]
    rows_small = -(-n_small // 1024) * 8
    small = jnp.pad(flat, (0, rows_small * 128 - n_small)).reshape(rows_small, 128)
    g_w_in_t, g_w_q_t, g_w_kv_t, small_sum = _reduce_grads([p_in, p_q, p_kv], small)
    ssum = small_sum.reshape(-1)

    def take(off, n):
        return ssum[off:off + n], off + n

    off = 0
    g_norm, off = take(off, D_MODEL)
    g_qn, off = take(off, Q_RANK)
    g_kvn, off = take(off, KV_RANK)
    g_ga, off = take(off, CONV_W)
    g_gc, off = take(off, CONV_W)
    g_gf, off = take(off, D_MODEL)
    g_cw_all, off = take(off, 3 * CONV_W)
    g_meta_all, off = take(off, N_META * D_MODEL)
    loss = ssum[off]
    chip = 2 * lax.axis_index("x") + lax.axis_index("y")
    g_conv = lax.dynamic_slice(g_cw_all.reshape(3, CONV_W), (0, chip * 128), (3, 128))
    g_mt = lax.dynamic_slice(g_meta_all.reshape(N_META, D_MODEL), (0, chip * 256), (N_META, 256))

    grads = {
        "meta_tokens": g_mt, "norm_g": g_norm.reshape(1, -1), "w_in": g_w_in_t, "q_norm_g": g_qn.reshape(1, -1),
        "w_q_up": g_w_q_t, "kv_norm_g": g_kvn.reshape(1, -1), "w_kv_up": jnp.transpose(g_w_kv_t)[None],
        "conv_w": g_conv[None], "attn_out_g": g_ga.reshape(1, -1), "conv_out_g": g_gc.reshape(1, -1),
        "w_out": g_w_out[None], "final_norm_g": g_gf,
    }
    transposed = ("w_in", "w_q_up")
    weights = {
        "meta_tokens": (meta_tokens, m_meta_tokens, v_meta_tokens), "norm_g": (norm_g, m_norm_g, v_norm_g),
        "w_in": (w_in, m_w_in, v_w_in), "q_norm_g": (q_norm_g, m_q_norm_g, v_q_norm_g),
        "w_q_up": (w_q_up, m_w_q_up, v_w_q_up), "kv_norm_g": (kv_norm_g, m_kv_norm_g, v_kv_norm_g),
        "w_kv_up": (w_kv_up, m_w_kv_up, v_w_kv_up), "conv_w": (conv_w, m_conv_w, v_conv_w),
        "attn_out_g": (attn_out_g, m_attn_out_g, v_attn_out_g), "conv_out_g": (conv_out_g, m_conv_out_g, v_conv_out_g),
        "w_out": (w_out, m_w_out, v_w_out), "final_norm_g": (final_norm_g, m_final_norm_g, v_final_norm_g),
    }
    names = list(weights)
    small = [nme for nme in names if nme != "w_in"]

    def view(nme, a):
        if nme in transposed:
            return a if a.ndim == 2 else tr(a)
        if nme == "conv_w":
            return jnp.transpose(a.reshape(1, 3, -1), (1, 0, 2))
        if a.ndim == 3:
            return a[0]
        return a.reshape(1, -1) if a.ndim == 1 else a

    def unview(nme, a):
        if nme in transposed:
            return jnp.transpose(a)[None]
        if nme == "conv_w":
            return jnp.transpose(a, (1, 0, 2))
        return a.reshape(weights[nme][0].shape)

    res_small = _adamw_small(*[[view(nme, a) for nme, a in zip(small, col)] for col in (
        [weights[nme][0] for nme in small], [grads[nme] for nme in small],
        [weights[nme][1] for nme in small], [weights[nme][2] for nme in small])])
    w_, m_, v_ = weights["w_in"]
    res = _adamw(tr(w_), grads["w_in"], tr(m_), tr(v_), "adamw_w_in")
    upd = {"w_in": tuple(jnp.transpose(a)[None] for a in (grads["w_in"],) + res)}
    for j, nme in enumerate(small):
        upd[nme] = (unview(nme, view(nme, grads[nme])),) + tuple(unview(nme, r[j]) for r in res_small)
    grads = {nme: upd[nme][0] for nme in names}
    deltas, new_m, new_v = ([upd[nme][j] for nme in names] for j in (1, 2, 3))

    grad_x = gx.reshape(nb, s, D_MODEL)
    return (loss, grad_x, *[grads[nme] for nme in names], *deltas, *new_m, *new_v)
```

```python
import functools

import jax
import jax.numpy as jnp
import numpy as np
from jax import lax
from jax.experimental import pallas as pl
from jax.experimental.pallas import tpu as pltpu

F32 = jnp.float32
BF16 = jnp.bfloat16

D_MODEL = 1024
N_META = 16
HEADS = 4
NOPE = 128
ROPE = 64
VDIM = 128
QK_PAD = 256
Q_RANK = 256
KV_RANK = 128
CONV_W = 512
CONV_GROUP = 64
ROPE_THETA = 10000.0
EPS = 1e-6
ATTN_SCALE = (NOPE + ROPE) ** -0.5
IN_DIM = 3008
IN_PAD = 3072
HEAD_ROWS = Q_RANK + KV_RANK + ROPE
IN_HEAD = 512
IN_TAIL = IN_PAD - IN_HEAD
BLK_ZA, BLK_CB, BLK_CC, BLK_CH, BLK_ZC = 0, 1, 2, 3, 4
NEG_INF = -1e30

ADAM_LR = 0.001
ADAM_B1 = 0.9
ADAM_B2 = 0.999
ADAM_EPS = 1e-08
ADAM_WD = 0.01
ADAM_STEP = 10

ROW_TILE = 512
ATTN_TILE = 256
VMEM_LIMIT = 56 * 1024 * 1024

NT = (((1,), (1,)), ((), ()))
TN = (((0,), (0,)), ((), ()))


def _cparams(*sem):
    return pltpu.CompilerParams(dimension_semantics=sem, vmem_limit_bytes=VMEM_LIMIT)


def _dot(a, b):
    return jnp.dot(a, b, preferred_element_type=F32)


def _dot_nt(a, b):
    return lax.dot_general(a, b, NT, preferred_element_type=F32)


def _dot_tn(a, b):
    return lax.dot_general(a, b, TN, preferred_element_type=F32)


def _rms(x, g):
    r = lax.rsqrt(jnp.mean(x * x, axis=-1, keepdims=True) + EPS)
    return x * r * g, r


def _rms_bwd(dy, x, r, g):
    xh = x * r
    dyg = dy * g
    dx = r * (dyg - xh * jnp.mean(dyg * xh, axis=-1, keepdims=True))
    return dx, dy * xh


def _sigmoid(z):
    return 1.0 / (1.0 + jnp.exp(-z))


def _rope(b, c, sa, sb):
    return b * c + pltpu.roll(b, 96, 1) * sa + pltpu.roll(b, 32, 1) * sb


def _rope_bwd(d, c, sa, sb):
    return d * c + pltpu.roll(d * sa, 32, 1) + pltpu.roll(d * sb, 96, 1)


def _group_mean(x, gmat):
    hi = x.astype(BF16)
    lo = (x - hi.astype(F32)).astype(BF16)
    return _dot(hi, gmat) + _dot(lo, gmat)


def _row_of(col, rows):
    return jnp.transpose(jnp.broadcast_to(col, (rows, 128)))[0:1, :]


def _rope_tables(n_pos):
    half = ROPE // 2
    inv_freq = (np.float32(1.0) / (np.float32(ROPE_THETA) ** (np.arange(half, dtype=np.float32) / np.float32(half))))
    ang = np.arange(n_pos, dtype=np.float32)[:, None] * inv_freq.astype(np.float32)[None, :]
    cos, sin = np.cos(ang).astype(np.float32), np.sin(ang).astype(np.float32)
    z = np.zeros((n_pos, half), np.float32)
    c = np.concatenate([cos, cos, z, z], axis=1)
    sa = np.concatenate([-sin, z, z, z], axis=1)
    sb = np.concatenate([z, sin, z, z], axis=1)
    return jnp.asarray(c), jnp.asarray(sa), jnp.asarray(sb)


W_IN_PIECES_1 = ((0, 128, 752, 0, 64, 0), (384, 64, 752, 384, 448, 1), (448, 64, 752, 512, 512, 1))
W_IN_PIECES_2 = ((128, 256, 752, 128, 192, 0), (512, 240, 752, 576, 576, 1))
W_Q_PIECES = ((0, 96, 256, 0, 0, 0), (96, 96, 256, 96, 96, 1))
W_KV_PIECES = ((0, 128, 128, 0, 0, 0), (128, 128, 128, 512, 512, 1))
W_OUT_PIECES = ((0, 128, 256, 0, 0, 0), (128, 128, 256, 128, 128, 1))


class _StagedGather:
    STAGES = 4

    @staticmethod
    def steps(n_steps):
        return (0, n_steps // 2, n_steps - 2, n_steps - 1)

    def __init__(self, pieces, zero_rows=None):
        self.pieces = pieces
        self.zero_rows = zero_rows

    def scratch(self):
        nk, dma = len(self.pieces), pltpu.SemaphoreType.DMA
        return [dma((nk, 3)), dma((nk, 3)), dma((nk, 3)), dma((nk, 3)), dma((nk,))]

    def vmem_scratch(self, shard_shape, out_shape):
        return [pltpu.VMEM(shard_shape, BF16), pltpu.VMEM(out_shape, BF16),
                pltpu.SemaphoreType.DMA((4 * len(self.pieces) + 2,))] + self.scratch()

    def run_vmem(self, stage, shard_ref, out_ref, scr):
        src_scr, land_scr, io_sems = scr[:3]
        spans = []
        for _, nr, per, first, rest, _ in self.pieces:
            spans += [(per * q + (first if q == 0 else rest), nr) for q in range(4)]
        if self.zero_rows is not None:
            spans.append(self.zero_rows)
        flush = [pltpu.make_async_copy(land_scr.at[r0:r0 + nr], out_ref.at[r0:r0 + nr], io_sems.at[n])
                 for n, (r0, nr) in enumerate(spans)]
        if stage == 0:
            load = pltpu.make_async_copy(shard_ref, src_scr, io_sems.at[len(spans)])
            load.start()
            if self.zero_rows is not None:
                r0, nr = self.zero_rows
                land_scr[r0:r0 + nr, :] = jnp.zeros((nr, land_scr.shape[1]), BF16)
            load.wait()
        if stage < self.STAGES:
            self.run(stage, src_scr, land_scr, scr[3:])
        for cp in flush:
            if stage == self.STAGES - 1:
                cp.start()
            if stage == self.STAGES:
                cp.wait()

    def run(self, stage, src_ref, out_ref, scr):
        send_sems, recv_sems, fwd_send, fwd_recv, loc_sems = scr
        pieces = self.pieces
        nk = len(pieces)
        x, y, c = lax.axis_index("x"), lax.axis_index("y"), lax.axis_index("c")
        mine = 2 * x + y
        chips = [(1 - x, y), (x, 1 - y), (1 - x, 1 - y)]
        chip_of = [2 * px + py for px, py in chips]
        mesh = pl.DeviceIdType.MESH

        def src(k):
            s0, nr = pieces[k][0], pieces[k][1]
            return src_ref.at[s0:s0 + nr]

        def dst(k, q):
            _, nr, per, first, rest, _ = pieces[k]
            row = per * q + first + (rest - first) * jnp.minimum(q, 1)
            return out_ref.at[pl.ds(pl.multiple_of(row, 16), nr)]

        def ici(k, j, q):
            px, py = chips[j]
            return pltpu.make_async_remote_copy(
                src_ref=src(k), dst_ref=dst(k, q), send_sem=send_sems.at[k, j], recv_sem=recv_sems.at[k, j],
                device_id=(px, py, c), device_id_type=mesh)

        def fwd(k, j):
            ref = dst(k, chip_of[j])
            return pltpu.make_async_remote_copy(
                src_ref=ref, dst_ref=ref, send_sem=fwd_send.at[k, j], recv_sem=fwd_recv.at[k, j],
                device_id=(x, y, 1 - c), device_id_type=mesh)

        def relay(k, half):
            ref = dst(k, chip_of[half])
            px, py = chips[1 - half]
            return pltpu.make_async_remote_copy(
                src_ref=ref, dst_ref=ref, send_sem=send_sems.at[k, 2], recv_sem=recv_sems.at[k, 2],
                device_id=(px, py, c), device_id_type=mesh)

        local = [pltpu.make_async_copy(src(k), dst(k, mine), loc_sems.at[k]) for k in range(nk)]
        if stage == 0:
            for cp in local:
                cp.start()
        if stage == 3:
            for cp in local:
                cp.wait()
        for half in (0, 1):
            @pl.when(c == half)
            def _(half=half):
                my_k = [k for k in range(nk) if pieces[k][5] == half]
                other_k = [k for k in range(nk) if pieces[k][5] != half]
                for k in my_k:
                    if stage == 0:
                        for j in range(2):
                            ici(k, j, mine).start()
                    elif stage == 1:
                        for j in (half, 1 - half):
                            ici(k, j, chip_of[j]).wait_recv()
                            if j == half:
                                relay(k, half).start()
                            fwd(k, j).start()
                    elif stage == 2:
                        ici(k, 2, chip_of[2]).wait_recv()
                        fwd(k, 2).start()
                    else:
                        for j in range(2):
                            ici(k, j, mine).wait_send()
                        relay(k, half).wait_send()
                        for j in range(3):
                            fwd(k, j).wait_send()
                if stage == 3:
                    for k in other_k:
                        for j in range(3):
                            fwd(k, j).wait_recv()


def _fwd_proj(x2d, meta, tabs, tabs_m, norm_g, w_head, q_norm_g, wq_p, kv_norm_g, wkv_p, w_out_shard, w_in_shard,
              nb, s, tm):
    nt = s // tm
    n = nb * nt
    n_steps = n + 1
    c_t, sa_t, sb_t = tabs
    cm_t, sam_t, sbm_t = tabs_m
    gat = _StagedGather(W_OUT_PIECES)
    gat_in = _StagedGather(W_IN_PIECES_1)
    n_sems = len(gat.scratch())
    assert n_steps >= 3

    def body(x_ref, c_ref, sa_ref, sb_ref, mt_ref, cm_ref, sam_ref, sbm_ref,
             g_ref, w_ref, gq_ref, wq_ref, gkv_ref, wkv_ref, wos_ref, wis_ref,
             p_ref, q_ref, k_ref, v_ref, pm_ref, km_ref, vm_ref, wo_ref, wi_ref, *scr):
        gat_scr, gat_in_scr = scr[:n_sems], scr[n_sems:]
        i = pl.program_id(0)
        for stage, at in enumerate(_StagedGather.steps(n_steps)):
            @pl.when(i == at)
            def _(stage=stage):
                gat_in.run_vmem(stage, wis_ref, wi_ref, gat_in_scr)
                gat.run(stage, wos_ref, wo_ref, gat_scr)

        def project(xv, c, sa, sb, p_out, q_out, k_out, v_out):
            u, _ = _rms(xv, g_ref[...])
            p = _dot_nt(u.astype(BF16), w_ref[...])
            p_out[...] = p
            qn, _ = _rms(p[:, 0:Q_RANK], gq_ref[...])
            q = _dot_nt(qn.astype(BF16), wq_ref[...])
            kvn, _ = _rms(p[:, Q_RANK:Q_RANK + KV_RANK], gkv_ref[...])
            kv = _dot_nt(kvn.astype(BF16), wkv_ref[...])
            kpe = _rope(p[:, 384:512], c, sa, sb)
            for h in range(HEADS):
                if q_out is not None:
                    pe = _rope(q[:, QK_PAD * h + NOPE:QK_PAD * (h + 1)], c, sa, sb)
                    qh = jnp.concatenate([q[:, QK_PAD * h:QK_PAD * h + NOPE], pe], axis=1)
                    q_out[0, h] = (qh * ATTN_SCALE).astype(BF16)
                k_out[0, h] = jnp.concatenate([kv[:, NOPE * h:NOPE * (h + 1)], kpe], axis=1).astype(BF16)
                v_out[0, h] = kv[:, 512 + VDIM * h:512 + VDIM * (h + 1)].astype(BF16)

        @pl.when(i < n)
        def _():
            project(x_ref[...], c_ref[...], sa_ref[...], sb_ref[...], p_ref, q_ref, k_ref, v_ref)

        @pl.when(i == n)
        def _():
            project(mt_ref[...], cm_ref[...], sam_ref[...], sbm_ref[...], pm_ref, None, km_ref, vm_ref)
            gat_in.run_vmem(gat_in.STAGES, wis_ref, wi_ref, gat_in_scr)

    cl = lambda i: jnp.minimum(i, n - 1)
    full = lambda a: pl.BlockSpec(a.shape, lambda i: (0,) * a.ndim)
    const = lambda shape: pl.BlockSpec(shape, lambda i: (0,) * len(shape))
    tab = pl.BlockSpec((tm, 128), lambda i: (cl(i) % nt, 0))
    hb = lambda w: pl.BlockSpec((1, HEADS, tm, w), lambda i: (cl(i) // nt, 0, cl(i) % nt, 0))
    whole = pl.BlockSpec(memory_space=pl.ANY)
    return pl.pallas_call(
        body, name="fwd_proj", grid=(n_steps,),
        in_specs=[pl.BlockSpec((tm, D_MODEL), lambda i: (cl(i), 0)), tab, tab, tab,
                  full(meta), full(cm_t), full(sam_t), full(sbm_t),
                  full(norm_g), full(w_head), full(q_norm_g), full(wq_p), full(kv_norm_g), full(wkv_p), whole, whole],
        out_specs=[pl.BlockSpec((tm, IN_HEAD), lambda i: (cl(i), 0)), hb(QK_PAD), hb(QK_PAD), hb(VDIM),
                   const((N_META, IN_HEAD)), const((1, HEADS, N_META, QK_PAD)), const((1, HEADS, N_META, VDIM)),
                   whole, whole],
        out_shape=[jax.ShapeDtypeStruct((nb * s, IN_HEAD), F32),
                   jax.ShapeDtypeStruct((nb, HEADS, s, QK_PAD), BF16),
                   jax.ShapeDtypeStruct((nb, HEADS, s, QK_PAD), BF16),
                   jax.ShapeDtypeStruct((nb, HEADS, s, VDIM), BF16),
                   jax.ShapeDtypeStruct((N_META, IN_HEAD), F32),
                   jax.ShapeDtypeStruct((1, HEADS, N_META, QK_PAD), BF16),
                   jax.ShapeDtypeStruct((1, HEADS, N_META, VDIM), BF16),
                   jax.ShapeDtypeStruct((D_MODEL, D_MODEL), BF16),
                   jax.ShapeDtypeStruct((IN_PAD, D_MODEL), BF16)],
        scratch_shapes=gat.scratch() + gat_in.vmem_scratch(w_in_shard.shape, (IN_PAD, D_MODEL)),
        compiler_params=_cparams("arbitrary"),
    )(x2d, c_t, sa_t, sb_t, meta, cm_t, sam_t, sbm_t, norm_g, w_head, q_norm_g, wq_p, kv_norm_g, wkv_p, w_out_shard,
      w_in_shard)


def _attn_fwd(q, k, v, km, vm, w_in_shard, w_in_part, nb, s, tq):
    nq = s // tq
    n_steps = nb * HEADS
    gat = _StagedGather(W_IN_PIECES_2, zero_rows=(HEAD_ROWS, IN_HEAD - HEAD_ROWS))
    assert n_steps >= 3

    def body(q_ref, k_ref, v_ref, km_ref, vm_ref, ws_ref, _, o_ref, lse_ref, w_ref, s_scr, p_scr, *gat_scr):
        step = pl.program_id(0) * HEADS + pl.program_id(1)
        for stage, at in enumerate(_StagedGather.steps(n_steps)):
            @pl.when(step == at)
            def _(stage=stage):
                gat.run_vmem(stage, ws_ref, w_ref, gat_scr)

        row = lax.broadcasted_iota(jnp.int32, (tq, tq), 0)
        col = lax.broadcasted_iota(jnp.int32, (tq, tq), 1)
        def scores(i):
            slot = i % 2
            qi = q_ref[0, 0, i * tq:(i + 1) * tq, :]
            sm = _dot_nt(qi, km_ref[0, 0])
            m128 = None
            for j in range(i + 1):
                sc = _dot_nt(qi, k_ref[0, 0, j * tq:(j + 1) * tq, :])
                if j == i:
                    sc = jnp.where(col <= row, sc, NEG_INF)
                s_scr[slot, :, j * tq:(j + 1) * tq] = sc
                mx = sc[:, 0:128]
                for c0 in range(128, tq, 128):
                    mx = jnp.maximum(mx, sc[:, c0:c0 + 128])
                m128 = mx if m128 is None else jnp.maximum(m128, mx)
            return sm, jnp.maximum(jnp.max(m128, axis=1, keepdims=True), jnp.max(sm, axis=1, keepdims=True))

        def weighted_sum(i, pm, l):
            n = (i + 1) * tq
            acc = _dot(p_scr[i % 2, :, 0:n], v_ref[0, 0, 0:n, :]) + _dot(pm.astype(BF16), vm_ref[0, 0])
            o_ref[0, 0, i * tq:(i + 1) * tq, :] = acc / l

        nxt, pending = scores(0), None
        for i in range(nq):
            slot = i % 2
            sm, m = nxt
            if i + 1 < nq:
                nxt = scores(i + 1)
            pm = jnp.exp(sm - m)
            l128 = None
            for j in range(i + 1):
                p = jnp.exp(s_scr[slot, :, j * tq:(j + 1) * tq] - m)
                p_scr[slot, :, j * tq:(j + 1) * tq] = p.astype(BF16)
                ps = p[:, 0:128]
                for c0 in range(128, tq, 128):
                    ps = ps + p[:, c0:c0 + 128]
                l128 = ps if l128 is None else l128 + ps
            l = jnp.sum(l128, axis=1, keepdims=True) + jnp.sum(pm, axis=1, keepdims=True)
            lse_ref[0, 0, :, i * tq:(i + 1) * tq] = _row_of(m + jnp.log(l), tq)
            if pending is not None:
                weighted_sum(*pending)
            pending = (i, pm, l)
        weighted_sum(*pending)

        @pl.when(step == n_steps - 1)
        def _():
            gat.run_vmem(gat.STAGES, ws_ref, w_ref, gat_scr)

    hblk = lambda w: pl.BlockSpec((1, 1, s, w), lambda b, h: (b, h, 0, 0))
    mblk = lambda w: pl.BlockSpec((1, 1, N_META, w), lambda b, h: (0, h, 0, 0))
    whole = pl.BlockSpec(memory_space=pl.ANY)
    return pl.pallas_call(
        body, name="attn_fwd", grid=(nb, HEADS),
        in_specs=[hblk(QK_PAD), hblk(QK_PAD), hblk(VDIM), mblk(QK_PAD), mblk(VDIM), whole, whole],
        out_specs=[hblk(VDIM), pl.BlockSpec((1, 1, 1, s), lambda b, h: (b, h, 0, 0)), whole],
        out_shape=[jax.ShapeDtypeStruct((nb, HEADS, s, VDIM), F32),
                   jax.ShapeDtypeStruct((nb, HEADS, 1, s), F32),
                   jax.ShapeDtypeStruct(w_in_part.shape, BF16)],
        input_output_aliases={6: 2},
        scratch_shapes=[pltpu.VMEM((2, tq, s), F32), pltpu.VMEM((2, tq, s), BF16)]
        + gat.vmem_scratch(w_in_shard.shape, w_in_part.shape),
        compiler_params=_cparams("arbitrary", "arbitrary"),
    )(q, k, v, km, vm, w_in_shard, w_in_part)


def _shift_rows(a, prev, n_rows):
    rid = lax.broadcasted_iota(jnp.int32, a.shape, 0)
    a1 = jnp.where(rid == 0, prev[7:8, :], pltpu.roll(a, 1, 0))
    a2 = jnp.where(rid == 0, prev[6:7, :], jnp.where(rid == 1, prev[7:8, :], pltpu.roll(a, 2, 0)))
    return a1, a2


def _attn_gate(o, za, ga_h):
    on, r = _rms(o, ga_h)
    return on * (za * _sigmoid(za)), on, r


def _out_fwd_bwd(x2d, tgt2d, o, meta, norm_g, w_in_p, conv_w, ga, gc, gmat, w_out, gf, nb, s, tm):
    nt = s // tm
    r = nb * s

    def body(x_ref, t_ref, o_ref, mt_ref, g_ref, wi_ref, cw_ref, ga_ref, gc_ref, gm_ref, w_ref, gf_ref,
             dh_ref, dy_ref, dw_ref, dgf_ref, loss_ref, p_ref, pm_ref, last_cc):
        i = pl.program_id(0)
        blk = lambda ref, j, rows=slice(None): ref[rows, 512 * j:512 * (j + 1)]

        def tail(xv):
            u, _ = _rms(xv, g_ref[...])
            return _dot_nt(u.astype(BF16), wi_ref[IN_HEAD:IN_PAD, :])

        @pl.when(i == 0)
        def _():
            dw_ref[...] = jnp.zeros_like(dw_ref)
            dgf_ref[...] = jnp.zeros_like(dgf_ref)
            loss_ref[...] = jnp.zeros_like(loss_ref)
            last_cc[...] = jnp.zeros_like(last_cc)
            pm_ref[...] = tail(mt_ref[...])

        u16 = _rms(x_ref[...], g_ref[...])[0].astype(BF16)

        def project(j):
            p_ref[:, 512 * j:512 * (j + 1)] = _dot_nt(u16, wi_ref[IN_HEAD + 512 * j:IN_HEAD + 512 * (j + 1), :])

        project(BLK_ZA)
        project(BLK_CC)
        project(BLK_CH)
        ya = []
        for h in range(HEADS):
            y, _, _ = _attn_gate(o_ref[0, h], p_ref[:, 512 * BLK_ZA + VDIM * h:512 * BLK_ZA + VDIM * (h + 1)],
                                 ga_ref[:, VDIM * h:VDIM * (h + 1)])
            ya.append(y)
        project(BLK_CB)
        project(BLK_ZC)
        cc = blk(p_ref, BLK_CC) * blk(p_ref, BLK_CH)
        meta_cc = blk(pm_ref, BLK_CC, slice(8, 16)) * blk(pm_ref, BLK_CH, slice(8, 16))
        prev = jnp.where(i % nt == 0, meta_cc, last_cc[...])
        last_cc[...] = cc[tm - 8:tm, :]
        cc1, cc2 = _shift_rows(cc, prev, tm)
        yc = blk(p_ref, BLK_CB) * (cw_ref[0:1, :] * cc2 + cw_ref[1:2, :] * cc1 + cw_ref[2:3, :] * cc)
        rg = lax.rsqrt(_group_mean(yc * yc, gm_ref[...]) + EPS)
        zc = blk(p_ref, BLK_ZC)
        yconv = yc * rg * gc_ref[...] * (zc * _sigmoid(zc))
        ycat = jnp.concatenate(ya + [yconv], axis=1).astype(BF16)
        h2 = x_ref[...] + _dot(ycat, w_ref[...])
        gfv = gf_ref[...]
        y, r2 = _rms(h2, gfv)
        e = y - t_ref[...]
        loss_ref[...] += 0.5 * jnp.sum(e * e) / D_MODEL
        dyv = e * (1.0 / D_MODEL)
        dh2, dgf = _rms_bwd(dyv, h2, r2, gfv)
        dgf_ref[...] += jnp.sum(dgf, axis=0, keepdims=True)
        dh_ref[...] = dh2
        dhb = dh2.astype(BF16)
        dy_ref[...] = _dot_nt(dhb, w_ref[...])
        dw_ref[...] += _dot_tn(ycat, dhb)

    row = lambda w: pl.BlockSpec((tm, w), lambda i: (i, 0))
    const = lambda shape: pl.BlockSpec(shape, lambda i: (0,) * len(shape))
    full = lambda a: const(a.shape)
    return pl.pallas_call(
        body, name="out_fwd_bwd", grid=(nb * nt,),
        in_specs=[row(D_MODEL), row(D_MODEL),
                  pl.BlockSpec((1, HEADS, tm, VDIM), lambda i: (i // nt, 0, i % nt, 0)),
                  full(meta), full(norm_g), full(w_in_p),
                  full(conv_w), full(ga), full(gc), full(gmat), full(w_out), full(gf)],
        out_specs=[row(D_MODEL), row(D_MODEL), const((D_MODEL, D_MODEL)), const((1, D_MODEL)), const((1, 128)),
                   row(IN_TAIL), const((N_META, IN_TAIL))],
        out_shape=[jax.ShapeDtypeStruct((r, D_MODEL), F32), jax.ShapeDtypeStruct((r, D_MODEL), F32),
                   jax.ShapeDtypeStruct((D_MODEL, D_MODEL), F32), jax.ShapeDtypeStruct((1, D_MODEL), F32),
                   jax.ShapeDtypeStruct((1, 128), F32),
                   jax.ShapeDtypeStruct((r, IN_TAIL), F32), jax.ShapeDtypeStruct((N_META, IN_TAIL), F32)],
        scratch_shapes=[pltpu.VMEM((8, 512), F32)],
        compiler_params=_cparams("arbitrary"),
    )(x2d, tgt2d, o, meta, norm_g, w_in_p, conv_w, ga, gc, gmat, w_out, gf)


def _gate_bwd(dycat, o, p, pm, conv_w, ga, gc, gmat, nb, s, tm):
    nt = s // tm
    r = nb * s
    ext = tm + 8
    prev_idx = lambda i: jnp.maximum(i * (tm // 8) - 1, 0)
    next_idx = lambda i: jnp.minimum((i + 1) * (tm // 8), r // 8 - 1)

    def body(dya_ref, dyc_ref, dycn_ref, o_ref, za_ref, cb_ref, cbn_ref, cc_ref, ccp_ref, ccn_ref,
             ch_ref, chp_ref, chn_ref, zc_ref, zcn_ref, mc_ref, mh_ref, cw_ref, ga_ref, gc_ref, gm_ref,
             dpb_ref, do_ref, dl_ref, dccm_ref, dga_ref, dgc_ref, dcw_ref):
        i = pl.program_id(0)

        @pl.when(i == 0)
        def _():
            dga_ref[...] = jnp.zeros_like(dga_ref)
            dgc_ref[...] = jnp.zeros_like(dgc_ref)
            dcw_ref[...] = jnp.zeros_like(dcw_ref)

        dga = []
        for h in range(HEADS):
            hs = slice(VDIM * h, VDIM * (h + 1))
            oh, za, gah, dya = o_ref[0, h], za_ref[:, hs], ga_ref[:, hs], dya_ref[:, hs]
            sg = _sigmoid(za)
            on, ro = _rms(oh, gah)
            don = dya * (za * sg)
            dpb_ref[:, hs] = (dya * on * (sg * (1.0 + za * (1.0 - sg)))).astype(BF16)
            do, dg = _rms_bwd(don, oh, ro, gah)
            dga.append(jnp.sum(dg, axis=0, keepdims=True))
            dob = do.astype(BF16)
            do_ref[0, h] = dob
            dl_ref[0, h] = _row_of(jnp.sum(dob.astype(F32) * oh, axis=1, keepdims=True), tm)
        dga_ref[...] += jnp.concatenate(dga, axis=1)

        cat = lambda a, b: jnp.concatenate([a[...], b[...]], axis=0)
        cch = cat(cc_ref, ccn_ref)
        chh = cat(ch_ref, chn_ref)
        cb = cat(cb_ref, cbn_ref)
        zc = cat(zc_ref, zcn_ref)
        dy = cat(dyc_ref, dycn_ref)
        first = i % nt == 0
        last = i % nt == nt - 1
        cc = cch * chh
        prev = jnp.where(first, mc_ref[8:16, :] * mh_ref[8:16, :], ccp_ref[...] * chp_ref[...])
        cc1, cc2 = _shift_rows(cc, prev, ext)
        w0, w1, w2 = cw_ref[0:1, :], cw_ref[1:2, :], cw_ref[2:3, :]
        dw = w0 * cc2 + w1 * cc1 + w2 * cc
        yc = cb * dw
        rg = lax.rsqrt(_group_mean(yc * yc, gm_ref[...]) + EPS)
        ych = yc * rg
        gcv = gc_ref[...]
        sg = _sigmoid(zc)
        dycn = dy * (zc * sg)
        dzc = dy * (ych * gcv) * (sg * (1.0 + zc * (1.0 - sg)))
        dgc_ref[...] += jnp.sum((dycn * ych)[:tm], axis=0, keepdims=True)
        dycg = dycn * gcv
        dyc = rg * (dycg - ych * _group_mean(dycg * ych, gm_ref[...]))
        rid = lax.broadcasted_iota(jnp.int32, (ext, CONV_W), 0)
        ddw = jnp.where(jnp.logical_and(last, rid >= tm), 0.0, dyc * cb)
        dcb = dyc * dw
        dcc = w2 * ddw + w1 * pltpu.roll(ddw, ext - 1, 0) + w0 * pltpu.roll(ddw, ext - 2, 0)
        dpb_ref[:, 512:1024] = dcb[:tm].astype(BF16)
        dpb_ref[:, 1024:1536] = (dcc * chh)[:tm].astype(BF16)
        dpb_ref[:, 1536:2048] = (dcc * cch)[:tm].astype(BF16)
        dpb_ref[:, 2048:2560] = dzc[:tm].astype(BF16)
        rs = lambda a: jnp.sum(a[:tm], axis=0, keepdims=True)
        dcw_ref[0:1, :] += rs(ddw * cc2)
        dcw_ref[1:2, :] += rs(ddw * cc1)
        dcw_ref[2:3, :] += rs(ddw * cc)

        @pl.when(first)
        def _():
            d0, d1 = ddw[0:1, :], ddw[1:2, :]
            r8 = lax.broadcasted_iota(jnp.int32, (8, CONV_W), 0)
            dccm_ref[0] = jnp.where(r8 == 7, w1 * d0 + w0 * d1, jnp.where(r8 == 6, w0 * d0, 0.0))

    row = lambda j: pl.BlockSpec((tm, 512), lambda i: (i, j))
    prv = lambda j: pl.BlockSpec((8, 512), lambda i: (prev_idx(i), j))
    nxt = lambda j: pl.BlockSpec((8, 512), lambda i: (next_idx(i), j))
    mblk = lambda j: pl.BlockSpec((N_META, 512), lambda i: (0, j))
    full = lambda a: pl.BlockSpec(a.shape, lambda i: (0,) * a.ndim)
    hb = lambda w: pl.BlockSpec((1, HEADS, tm, w), lambda i: (i // nt, 0, i % nt, 0))
    acc = lambda rr: pl.BlockSpec((rr, 512), lambda i: (0, 0))
    return pl.pallas_call(
        body, name="gate_bwd", grid=(nb * nt,),
        in_specs=[row(0), row(1), nxt(1), hb(VDIM),
                  row(BLK_ZA), row(BLK_CB), nxt(BLK_CB), row(BLK_CC), prv(BLK_CC), nxt(BLK_CC),
                  row(BLK_CH), prv(BLK_CH), nxt(BLK_CH), row(BLK_ZC), nxt(BLK_ZC),
                  mblk(BLK_CC), mblk(BLK_CH), full(conv_w), full(ga), full(gc), full(gmat)],
        out_specs=[pl.BlockSpec((tm, 2560), lambda i: (i, 0)), hb(VDIM),
                   pl.BlockSpec((1, HEADS, 1, tm), lambda i: (i // nt, 0, 0, i % nt)),
                   pl.BlockSpec((1, 8, 512), lambda i: (i // nt, 0, 0)),
                   acc(1), acc(1), acc(8)],
        out_shape=[jax.ShapeDtypeStruct((r, 2560), BF16), jax.ShapeDtypeStruct((nb, HEADS, s, VDIM), BF16),
                   jax.ShapeDtypeStruct((nb, HEADS, 1, s), F32), jax.ShapeDtypeStruct((nb, 8, 512), F32),
                   jax.ShapeDtypeStruct((1, 512), F32), jax.ShapeDtypeStruct((1, 512), F32),
                   jax.ShapeDtypeStruct((8, 512), F32)],
        compiler_params=_cparams("arbitrary"),
    )(dycat, dycat, dycat, o, p, p, p, p, p, p, p, p, p, p, p, pm, pm, conv_w, ga, gc, gmat)


class _StagedReduce:
    LOC, PRE_S, PRE_R, ICI_S, ICI_R, POST_S, POST_R, OUT, N_SEM = 0, 1, 2, 3, 6, 9, 10, 11, 12

    def __init__(self, shard_shape):
        self.half = (shard_shape[0] // 2, shard_shape[1])

    def scratch(self):
        h = self.half
        return [pltpu.VMEM((4,) + h, F32), pltpu.VMEM((4,) + h, F32), pltpu.VMEM((4,) + h, BF16),
                pltpu.VMEM((3,) + h, BF16), pltpu.VMEM(h, F32), pltpu.SemaphoreType.DMA((self.N_SEM,))]

    def run(self, stage, pin, gout, scr):
        own, sib, wire, rbuf, fin, sems = scr
        r2 = self.half[0]
        x, y, c = lax.axis_index("x"), lax.axis_index("y"), lax.axis_index("c")
        mine = 2 * x + y
        sibling = (x, y, 1 - c)
        chips = [(1 - x, y), (x, 1 - y), (1 - x, 1 - y)]
        rows = lambda half: pl.ds(pl.multiple_of(half * r2, r2), r2)
        mesh = pl.DeviceIdType.MESH

        loc = pltpu.make_async_copy(pin.at[:, rows(c), :], own, sems.at[self.LOC])
        pre = pltpu.make_async_remote_copy(
            src_ref=pin.at[:, rows(1 - c), :], dst_ref=sib, send_sem=sems.at[self.PRE_S],
            recv_sem=sems.at[self.PRE_R], device_id=sibling, device_id_type=mesh)

        def ici(j):
            px, py = chips[j]
            return pltpu.make_async_remote_copy(
                src_ref=wire.at[2 * px + py], dst_ref=rbuf.at[j], send_sem=sems.at[self.ICI_S + j],
                recv_sem=sems.at[self.ICI_R + j], device_id=(px, py, c), device_id_type=mesh)

        def post(half):
            return pltpu.make_async_remote_copy(
                src_ref=fin, dst_ref=gout.at[rows(half), :], send_sem=sems.at[self.POST_S],
                recv_sem=sems.at[self.POST_R], device_id=sibling, device_id_type=mesh)

        keep = pltpu.make_async_copy(fin, gout.at[rows(c), :], sems.at[self.OUT])
        if stage == 0:
            loc.start()
            pre.start()
        elif stage == 1:
            loc.wait()
            pre.wait_recv()
            for blk in range(4):
                tot = own[blk] + sib[blk]
                own[blk] = tot
                wire[blk] = tot.astype(BF16)
            for j in range(3):
                ici(j).start()
        elif stage == 2:
            for j in range(3):
                ici(j).wait_recv()
            tot = own[mine]
            for j in range(3):
                tot = tot + rbuf[j].astype(F32)
            fin[...] = tot
            post(c).start()
            keep.start()
        else:
            post(1 - c).wait_recv()
            pre.wait_send()
            for j in range(3):
                ici(j).wait_send()
            post(c).wait_send()
            keep.wait()


def _attn_bwd(q, k, v, do, lse, delta, km, vm, early, nb, s, t):
    n = s // t
    ne = len(early)
    reds = [_StagedReduce(a.shape[1:]) for a in early]
    n_steps = HEADS * nb
    assert n_steps >= 4

    def body(q_ref, k_ref, v_ref, do_ref, lse_ref, dl_ref, km_ref, vm_ref, *rest):
        pin_refs, rest = rest[:ne], rest[ne:]
        dq_ref, dk_ref, dv_ref, dkm_ref, dvm_ref = rest[:5]
        gout_refs, (p_scr, ds_scr, dq_acc), red_scr = rest[5:5 + ne], rest[5 + ne:8 + ne], rest[8 + ne:]
        b = pl.program_id(1)
        step = pl.program_id(0) * nb + b
        for stage, at in enumerate((0, 1, n_steps - 2, n_steps - 1)):
            @pl.when(step == at)
            def _(stage=stage):
                for a, red in enumerate(reds):
                    red.run(stage, pin_refs[a], gout_refs[a], red_scr[6 * a:6 * a + 6])

        @pl.when(b == 0)
        def _():
            dkm_ref[...] = jnp.zeros_like(dkm_ref)
            dvm_ref[...] = jnp.zeros_like(dvm_ref)

        kr = lax.broadcasted_iota(jnp.int32, (t, t), 0)
        qc = lax.broadcasted_iota(jnp.int32, (t, t), 1)
        km_v, vm_v = km_ref[0, 0], vm_ref[0, 0]
        ptm = jnp.exp(_dot_nt(km_v, q_ref[0, 0]) - lse_ref[0, 0])
        dstm = (ptm * (_dot_nt(vm_v, do_ref[0, 0]) - dl_ref[0, 0])).astype(BF16)
        dkm_ref[0] += _dot(dstm, q_ref[0, 0])
        dvm_ref[0] += _dot(ptm.astype(BF16), do_ref[0, 0])
        dq_acc[...] = _dot_tn(dstm, km_v)
        def tiles(j):
            slot = j % 2
            kj = k_ref[0, 0, j * t:(j + 1) * t, :]
            vj = v_ref[0, 0, j * t:(j + 1) * t, :]
            def products(i):
                cs = slice(i * t, (i + 1) * t)
                return _dot_nt(kj, q_ref[0, 0, cs, :]), _dot_nt(vj, do_ref[0, 0, cs, :])

            nxt, pending = products(j), None
            for i in range(j, n):
                cs = slice(i * t, (i + 1) * t)
                st, dpt = nxt
                if i + 1 < n:
                    nxt = products(i + 1)
                if i == j:
                    st = jnp.where(kr <= qc, st, NEG_INF)
                pt = jnp.exp(st - lse_ref[0, 0, :, cs])
                dst = (pt * (dpt - dl_ref[0, 0, :, cs])).astype(BF16)
                p_scr[slot, :, cs] = pt.astype(BF16)
                ds_scr[slot, :, cs] = dst
                if pending is not None:
                    dq_acc[pending[0], :] += _dot_tn(pending[1], kj)
                pending = (cs, dst)
            dq_acc[pending[0], :] += _dot_tn(pending[1], kj)

        for j in range(n):
            slot = j % 2
            tiles(j)
            dv_ref[0, 0, j * t:(j + 1) * t, :] = _dot(p_scr[slot, :, j * t:s], do_ref[0, 0, j * t:s, :]).astype(BF16)
            dk_ref[0, 0, j * t:(j + 1) * t, :] = _dot(ds_scr[slot, :, j * t:s], q_ref[0, 0, j * t:s, :]).astype(BF16)
        dq_ref[0, 0] = dq_acc[...].astype(BF16)

    big = lambda w: pl.BlockSpec((1, 1, s, w), lambda h, b: (b, h, 0, 0))
    rowv = pl.BlockSpec((1, 1, 1, s), lambda h, b: (b, h, 0, 0))
    mk = lambda w: pl.BlockSpec((1, 1, N_META, w), lambda h, b: (0, h, 0, 0))
    mo = lambda w: pl.BlockSpec((1, N_META, w), lambda h, b: (h, 0, 0))
    return pl.pallas_call(
        body, name="attn_bwd", grid=(HEADS, nb),
        in_specs=[big(QK_PAD), big(QK_PAD), big(VDIM), big(VDIM), rowv, rowv, mk(QK_PAD), mk(VDIM)]
        + [pl.BlockSpec(memory_space=pl.ANY)] * ne,
        out_specs=[big(QK_PAD), big(QK_PAD), big(VDIM), mo(QK_PAD), mo(VDIM)]
        + [pl.BlockSpec(memory_space=pl.ANY)] * ne,
        out_shape=[jax.ShapeDtypeStruct((nb, HEADS, s, QK_PAD), BF16),
                   jax.ShapeDtypeStruct((nb, HEADS, s, QK_PAD), BF16),
                   jax.ShapeDtypeStruct((nb, HEADS, s, VDIM), BF16),
                   jax.ShapeDtypeStruct((HEADS, N_META, QK_PAD), F32),
                   jax.ShapeDtypeStruct((HEADS, N_META, VDIM), F32)]
        + [jax.ShapeDtypeStruct(a.shape[1:], F32) for a in early],
        scratch_shapes=[pltpu.VMEM((2, t, s), BF16), pltpu.VMEM((2, t, s), BF16), pltpu.VMEM((s, QK_PAD), F32)]
        + [sc for red in reds for sc in red.scratch()],
        compiler_params=_cparams("arbitrary", "arbitrary"),
    )(q, k, v, do, lse, delta, km, vm, *early)


def _up_bwd(dq, dk, dv, dkm, dvm, p, pm, tabs, tabs_m, wq_p, wkv_p, gq, gkv, nb, s, tm):
    nt = s // tm
    n = nb * nt
    c_t, sa_t, sb_t = tabs
    cm_t, sam_t, sbm_t = tabs_m

    def kv_path(dkh, dvh, pa, c, sa, sb, wkv, gkvv):
        dkpe = dkh[0][:, NOPE:]
        for h in range(1, HEADS):
            dkpe = dkpe + dkh[h][:, NOPE:]
        dkr = _rope_bwd(dkpe, c, sa, sb)
        dkv = jnp.concatenate([d[:, :NOPE] for d in dkh] + list(dvh), axis=1).astype(BF16)
        ckv = pa[:, Q_RANK:Q_RANK + KV_RANK]
        kvn, rkv = _rms(ckv, gkvv)
        dckv, dg = _rms_bwd(_dot(dkv, wkv), ckv, rkv, gkvv)
        return dckv, dkr, kvn.astype(BF16), dkv, jnp.sum(dg, axis=0, keepdims=True)

    def body(dq_ref, dk_ref, dv_ref, pa_ref, c_ref, sa_ref, sb_ref,
             dkm_ref, dvm_ref, pam_ref, cm_ref, sam_ref, sbm_ref,
             wq_ref, wkv_ref, gq_ref, gkv_ref,
             dpa_ref, dpam_ref, pq_ref, pkv_ref, dgq_ref, dgkv_ref, dwq_ref, dwkv_ref):
        i = pl.program_id(0)

        @pl.when(i == 0)
        def _():
            dwq_ref[...] = jnp.zeros_like(dwq_ref)
            dwkv_ref[...] = jnp.zeros_like(dwkv_ref)
            dgq_ref[...] = jnp.zeros_like(dgq_ref)
            dgkv_ref[...] = jnp.zeros_like(dgkv_ref)

        @pl.when(i < n)
        def _():
            c, sa, sb = c_ref[...], sa_ref[...], sb_ref[...]
            pa = pa_ref[...]
            parts = []
            for h in range(HEADS):
                dqh = dq_ref[0, h].astype(F32) * ATTN_SCALE
                parts += [dqh[:, :NOPE], _rope_bwd(dqh[:, NOPE:], c, sa, sb)]
            dql = jnp.concatenate(parts, axis=1).astype(BF16)
            cq = pa[:, 0:Q_RANK]
            gqv = gq_ref[...]
            qn, rq = _rms(cq, gqv)
            dwq_ref[...] += _dot_tn(dql, qn.astype(BF16))
            dcq, dg = _rms_bwd(_dot(dql, wq_ref[...]), cq, rq, gqv)
            dgq_ref[...] += jnp.sum(dg, axis=0, keepdims=True)
            dckv, dkr, kvn, dkv, dgk = kv_path([dk_ref[0, h].astype(F32) for h in range(HEADS)],
                                               [dv_ref[0, h].astype(F32) for h in range(HEADS)],
                                               pa, c, sa, sb, wkv_ref[...], gkv_ref[...])
            dwkv_ref[...] += _dot_tn(dkv, kvn)
            dgkv_ref[...] += dgk
            dpa_ref[...] = jnp.concatenate([dcq, dckv, dkr], axis=1).astype(BF16)

        @pl.when(i == n)
        def _():
            dckv, dkr, kvn, dkv, dgk = kv_path([dkm_ref[h] for h in range(HEADS)],
                                               [dvm_ref[h] for h in range(HEADS)],
                                               pam_ref[...], cm_ref[...], sam_ref[...], sbm_ref[...],
                                               wkv_ref[...], gkv_ref[...])
            dwkv_ref[...] += _dot_tn(dkv, kvn)
            dgkv_ref[...] += dgk
            dpam_ref[...] = jnp.concatenate([jnp.zeros((N_META, Q_RANK), F32), dckv, dkr], axis=1)
            for h in range(HEADS):
                pq_ref[h] = dwq_ref[QK_PAD * h:QK_PAD * h + NOPE + ROPE, :]
                pkv_ref[h, 0:NOPE, :] = dwkv_ref[NOPE * h:NOPE * (h + 1), :]
                pkv_ref[h, NOPE:NOPE + VDIM, :] = dwkv_ref[512 + VDIM * h:512 + VDIM * (h + 1), :]

    cl = lambda i: jnp.minimum(i, n - 1)
    hb = lambda w: pl.BlockSpec((1, HEADS, tm, w), lambda i: (cl(i) // nt, 0, cl(i) % nt, 0))
    tab = pl.BlockSpec((tm, 128), lambda i: (cl(i) % nt, 0))
    full = lambda a: pl.BlockSpec(a.shape, lambda i: (0,) * a.ndim)
    const = lambda shape: pl.BlockSpec(shape, lambda i: (0,) * len(shape))
    return pl.pallas_call(
        body, name="up_bwd", grid=(n + 1,),
        in_specs=[hb(QK_PAD), hb(QK_PAD), hb(VDIM), pl.BlockSpec((tm, 512), lambda i: (cl(i), 0)), tab, tab, tab,
                  full(dkm), full(dvm), pl.BlockSpec((N_META, 512), lambda i: (0, 0)),
                  full(cm_t), full(sam_t), full(sbm_t), full(wq_p), full(wkv_p), full(gq), full(gkv)],
        out_specs=[pl.BlockSpec((tm, 512), lambda i: (cl(i), 0)), const((N_META, 512)),
                   const((HEADS, NOPE + ROPE, Q_RANK)), const((HEADS, NOPE + VDIM, KV_RANK)),
                   const((1, Q_RANK)), const((1, KV_RANK))],
        out_shape=[jax.ShapeDtypeStruct((nb * s, 512), BF16), jax.ShapeDtypeStruct((N_META, 512), F32),
                   jax.ShapeDtypeStruct((HEADS, NOPE + ROPE, Q_RANK), F32),
                   jax.ShapeDtypeStruct((HEADS, NOPE + VDIM, KV_RANK), F32),
                   jax.ShapeDtypeStruct((1, Q_RANK), F32), jax.ShapeDtypeStruct((1, KV_RANK), F32)],
        scratch_shapes=[pltpu.VMEM((HEADS * QK_PAD, Q_RANK), F32), pltpu.VMEM((1024, KV_RANK), F32)],
        compiler_params=_cparams("arbitrary"),
    )(dq, dk, dv, p, c_t, sa_t, sb_t, dkm, dvm, pm, cm_t, sam_t, sbm_t, wq_p, wkv_p, gq, gkv)


def _in_bwd(x2d, dh2, dpa, dpb, meta, dpam, dccm, pm, w_in_p, norm_g, nb, s, tm):
    nt = s // tm
    n = nb * nt

    def body(x_ref, dh_ref, dpa_ref, dpb_ref, mt_ref, dpam_ref, dccm_ref, mc_ref, mh_ref, w_ref, g_ref,
             gx_ref, gm_ref, dw_hbm, dg_ref, acc_ref, sems):
        i = pl.program_id(0)

        @pl.when(i == 0)
        def _():
            acc_ref[...] = jnp.zeros_like(acc_ref)
            dg_ref[...] = jnp.zeros_like(dg_ref)

        def rows(x, dp, dres):
            g = g_ref[...]
            dpb16 = dp.astype(BF16)
            du = _dot(dpb16, w_ref[...])
            u, r1 = _rms(x, g)
            acc_ref[...] += _dot_tn(dpb16, u.astype(BF16))
            dx, dg = _rms_bwd(du, x, r1, g)
            dg_ref[...] += jnp.sum(dg, axis=0, keepdims=True)
            return dx if dres is None else dx + dres

        @pl.when(i < n)
        def _():
            dp = jnp.concatenate([dpa_ref[...], dpb_ref[...]], axis=1)
            gx_ref[...] = rows(x_ref[...], dp, dh_ref[...])

        @pl.when(i == n)
        def _():
            dcc = dccm_ref[0]
            for b in range(1, nb):
                dcc = dcc + dccm_ref[b]
            z8 = jnp.zeros((8, CONV_W), F32)
            dc = jnp.concatenate([z8, dcc * mh_ref[8:16, :]], axis=0)
            dh = jnp.concatenate([z8, dcc * mc_ref[8:16, :]], axis=0)
            z = jnp.zeros((N_META, CONV_W), F32)
            dp = jnp.concatenate([dpam_ref[...], z, z, dc, dh, z], axis=1)
            gm_ref[...] = rows(mt_ref[...], dp, None)
            per = IN_DIM // 4
            cps = [pltpu.make_async_copy(acc_ref.at[0:448], dw_hbm.at[0, 0:448], sems.at[0]),
                   pltpu.make_async_copy(acc_ref.at[512:per + 64], dw_hbm.at[0, 448:per], sems.at[1])]
            for qq in range(1, 4):
                cps.append(pltpu.make_async_copy(acc_ref.at[per * qq + 64:per * (qq + 1) + 64], dw_hbm.at[qq],
                                                 sems.at[qq + 1]))
            for cp in cps:
                cp.start()
            for cp in cps:
                cp.wait()

    cl = lambda i: jnp.minimum(i, n - 1)
    row = lambda w: pl.BlockSpec((tm, w), lambda i: (cl(i), 0))
    full = lambda a: pl.BlockSpec(a.shape, lambda i: (0,) * a.ndim)
    mblk = lambda j: pl.BlockSpec((N_META, 512), lambda i: (0, j))
    return pl.pallas_call(
        body, name="in_bwd", grid=(n + 1,),
        in_specs=[row(D_MODEL), row(D_MODEL), row(512), row(2560), full(meta), full(dpam), full(dccm),
                  mblk(BLK_CC), mblk(BLK_CH), full(w_in_p), full(norm_g)],
        out_specs=[row(D_MODEL), pl.BlockSpec((N_META, D_MODEL), lambda i: (0, 0)),
                   pl.BlockSpec(memory_space=pl.ANY), pl.BlockSpec((1, D_MODEL), lambda i: (0, 0))],
        out_shape=[jax.ShapeDtypeStruct((nb * s, D_MODEL), F32), jax.ShapeDtypeStruct((N_META, D_MODEL), F32),
                   jax.ShapeDtypeStruct((4, IN_DIM // 4, D_MODEL), F32), jax.ShapeDtypeStruct((1, D_MODEL), F32)],
        scratch_shapes=[pltpu.VMEM((IN_PAD, D_MODEL), F32), pltpu.SemaphoreType.DMA((5,))],
        compiler_params=_cparams("arbitrary"),
    )(x2d, dh2, dpa, dpb, meta, dpam, dccm, pm, pm, w_in_p, norm_g)


def _gather_weights(w_in_shard, split, pieces, out_rows, whole, zero_fills):
    ns, nw, nz = len(split), len(whole), len(zero_fills)
    flat = [(a, pc) for a in range(ns) for pc in pieces[a]]
    nk = len(flat)
    hh = HEAD_ROWS // 2

    def body(*refs):
        ins, wins, zins = refs[1:1 + ns], refs[1 + ns:1 + ns + nw], refs[1 + ns + nw:1 + ns + nw + nz]
        n_in = 1 + ns + nw + nz
        head_ref, shard16 = refs[n_in], refs[n_in + 1]
        outs, wouts = refs[n_in + 2:n_in + 2 + ns], refs[n_in + 2 + ns:n_in + 2 + ns + nw]
        scr = refs[n_in + 2 + ns + nw:]
        stage = scr[:ns]
        (send_sems, recv_sems, fwd_send, fwd_recv, loc_sems, w_send, w_recv, w_loc, z_sems,
         h_send, h_recv, h_pass) = scr[ns:]
        x, y, c = lax.axis_index("x"), lax.axis_index("y"), lax.axis_index("c")
        mine = 2 * x + y
        chips = [(1 - x, y), (x, 1 - y), (1 - x, 1 - y)]
        chip_of = [2 * px + py for px, py in chips]
        shard16[...] = refs[0][...].astype(BF16)
        for a in range(ns):
            stage[a][...] = ins[a][...].astype(BF16)

        def head_rows(half):
            return pl.ds(pl.multiple_of(half * hh, 16), hh)

        def head_copy(j):
            px, py = chips[j]
            return pltpu.make_async_remote_copy(
                src_ref=shard16.at[head_rows(c)], dst_ref=head_ref.at[head_rows(c)], send_sem=h_send.at[j],
                recv_sem=h_recv.at[0], device_id=(px, py, c), device_id_type=pl.DeviceIdType.MESH)

        def head_pass(half):
            ref = head_ref.at[head_rows(half)]
            return pltpu.make_async_remote_copy(
                src_ref=ref, dst_ref=ref, send_sem=h_pass.at[0], recv_sem=h_pass.at[1],
                device_id=(x, y, 1 - c), device_id_type=pl.DeviceIdType.MESH)

        head_ref[HEAD_ROWS:IN_HEAD, :] = jnp.zeros((IN_HEAD - HEAD_ROWS, D_MODEL), BF16)

        @pl.when(mine == 0)
        def _():
            for j in range(3):
                head_copy(j).start()
            head_ref[0:HEAD_ROWS, :] = shard16[0:HEAD_ROWS, :]

        def src(k):
            a, (s0, nr, _, _, _, _) = flat[k]
            return stage[a].at[s0:s0 + nr]

        def dst(k, q):
            a, (_, nr, per, first, rest, _) = flat[k]
            row = per * q + first + (rest - first) * jnp.minimum(q, 1)
            return outs[a].at[pl.ds(pl.multiple_of(row, 16), nr)]

        def ici(k, j, q):
            px, py = chips[j]
            return pltpu.make_async_remote_copy(
                src_ref=src(k), dst_ref=dst(k, q), send_sem=send_sems.at[k, j], recv_sem=recv_sems.at[k, j],
                device_id=(px, py, c), device_id_type=pl.DeviceIdType.MESH)

        def fwd(k, j):
            ref = dst(k, chip_of[j])
            return pltpu.make_async_remote_copy(
                src_ref=ref, dst_ref=ref, send_sem=fwd_send.at[k, j], recv_sem=fwd_recv.at[k, j],
                device_id=(x, y, 1 - c), device_id_type=pl.DeviceIdType.MESH)

        def wcopy(b, j, q):
            px, py = chips[j]
            return pltpu.make_async_remote_copy(
                src_ref=wins[b], dst_ref=wouts[b].at[q], send_sem=w_send.at[b, j], recv_sem=w_recv.at[b, j],
                device_id=(px, py, c), device_id_type=pl.DeviceIdType.MESH)

        local = [pltpu.make_async_copy(src(k), dst(k, mine), loc_sems.at[k]) for k in range(nk)]
        local += [pltpu.make_async_copy(wins[b], wouts[b].at[mine], w_loc.at[b]) for b in range(nw)]
        for z, (a, _, row0) in enumerate(zero_fills):
            local.append(pltpu.make_async_copy(zins[z], outs[a].at[row0:row0 + zins[z].shape[0]], z_sems.at[z]))
        wsends = [wcopy(b, j, mine) for b in range(nw) for j in range(3)]
        for cp in local + wsends:
            cp.start()

        for half in (0, 1):
            @pl.when(c == half)
            def _(half=half):
                my_k = [k for k in range(nk) if flat[k][1][5] == half]
                other_k = [k for k in range(nk) if flat[k][1][5] != half]
                sends = [ici(k, j, mine) for k in my_k for j in range(3)]
                for cp in sends:
                    cp.start()
                passed = []
                for k in my_k:
                    for j in range(3):
                        ici(k, j, chip_of[j]).wait_recv()
                        cp = fwd(k, j)
                        cp.start()
                        passed.append(cp)
                for k in other_k:
                    for j in range(3):
                        fwd(k, j).wait_recv()
                for cp in sends + passed:
                    cp.wait_send()

        for b in range(nw):
            for j in range(3):
                wcopy(b, j, chip_of[j]).wait_recv()
        for cp in wsends:
            cp.wait_send()
        for cp in local:
            cp.wait()

        @pl.when(mine == 0)
        def _():
            for j in range(3):
                head_copy(j).wait_send()

        @pl.when(mine != 0)
        def _():
            head_copy(0).wait_recv()
            cp = head_pass(c)
            cp.start()
            head_pass(1 - c).wait_recv()
            cp.wait_send()

    vmem = pl.BlockSpec(memory_space=pltpu.VMEM)
    dma = pltpu.SemaphoreType.DMA
    zeros = [z for _, z, _ in zero_fills]
    return pl.pallas_call(
        body, name="gather_weights",
        in_specs=[vmem] * (1 + ns + nw + nz), out_specs=[vmem] * (2 + ns + nw),
        out_shape=([jax.ShapeDtypeStruct((IN_HEAD, D_MODEL), BF16), jax.ShapeDtypeStruct(w_in_shard.shape, BF16)]
                   + [jax.ShapeDtypeStruct((out_rows[a], split[a].shape[1]), BF16) for a in range(ns)]
                   + [jax.ShapeDtypeStruct((4,) + w.shape, w.dtype) for w in whole]),
        scratch_shapes=[pltpu.VMEM(a.shape, BF16) for a in split]
        + [dma((nk, 3)), dma((nk, 3)), dma((nk, 3)), dma((nk, 3)), dma((nk,)),
           dma((nw, 3)), dma((nw, 3)), dma((nw,)), dma((nz,)), dma((3,)), dma((1,)), dma((2,))],
        compiler_params=pltpu.CompilerParams(vmem_limit_bytes=VMEM_LIMIT),
    )(w_in_shard, *split, *whole, *zeros)


def _reduce_grads(parts, small):
    n = len(parts)
    shapes = [a.shape[1:] for a in parts]
    halves = [(sh[0] // 2, sh[1]) for sh in shapes]

    def body(*refs):
        pin, sm_in = refs[:n], refs[n]
        gout, sm_out = refs[n + 1:2 * n + 1], refs[2 * n + 1]
        scr = refs[2 * n + 2:]
        own, sib, wire, rbuf = scr[:n], scr[n:2 * n], scr[2 * n:3 * n], scr[3 * n:4 * n]
        (sbuf, send_sems, recv_sems, loc_sems, pre_send, pre_recv, post_send, post_recv,
         sm_send, sm_recv) = scr[4 * n:]
        x, y, c = lax.axis_index("x"), lax.axis_index("y"), lax.axis_index("c")
        mine = 2 * x + y
        me = 4 * x + 2 * y + c
        sibling = (x, y, 1 - c)
        chips = [(1 - x, y), (x, 1 - y), (1 - x, 1 - y)]

        def rows(a, half):
            r2 = halves[a][0]
            return pl.ds(pl.multiple_of(half * r2, r2), r2)

        chip_of = [2 * px + py for px, py in chips]
        blocks = chip_of + [mine]

        def pre(a, k):
            return pltpu.make_async_remote_copy(
                src_ref=pin[a].at[blocks[k], rows(a, 1 - c), :], dst_ref=sib[a].at[blocks[k]],
                send_sem=pre_send.at[a, k], recv_sem=pre_recv.at[a, k], device_id=sibling,
                device_id_type=pl.DeviceIdType.MESH)

        def ici(a, j):
            px, py = chips[j]
            return pltpu.make_async_remote_copy(
                src_ref=wire[a].at[2 * px + py], dst_ref=rbuf[a].at[j], send_sem=send_sems.at[a, j],
                recv_sem=recv_sems.at[a, j], device_id=(px, py, c), device_id_type=pl.DeviceIdType.MESH)

        def post(a, half):
            ref = gout[a].at[rows(a, half), :]
            return pltpu.make_async_remote_copy(
                src_ref=ref, dst_ref=ref, send_sem=post_send.at[a], recv_sem=post_recv.at[a],
                device_id=sibling, device_id_type=pl.DeviceIdType.MESH)

        def small_copy(kk):
            peer = (x ^ (kk >> 2), y ^ ((kk >> 1) & 1), c ^ (kk & 1))
            return pltpu.make_async_remote_copy(
                src_ref=sm_in, dst_ref=sbuf.at[kk], send_sem=sm_send.at[kk - 1], recv_sem=sm_recv.at[kk - 1],
                device_id=peer, device_id_type=pl.DeviceIdType.MESH)

        local = [[pltpu.make_async_copy(pin[a].at[blocks[k], rows(a, c), :], own[a].at[blocks[k]], loc_sems.at[a, k])
                  for k in range(4)] for a in range(n)]
        pres = [[pre(a, k) for k in range(4)] for a in range(n)]
        smalls = [small_copy(kk) for kk in range(1, 8)]
        for a in range(n):
            for k in range(4):
                local[a][k].start()
                pres[a][k].start()
        for cp in smalls:
            cp.start()
        sbuf[0] = sm_in[...]
        sends = []
        for a in range(n):
            for k in range(4):
                local[a][k].wait()
                pres[a][k].wait_recv()
                tot = own[a][blocks[k]] + sib[a][blocks[k]]
                own[a][blocks[k]] = tot
                if k < 3:
                    wire[a][blocks[k]] = tot.astype(BF16)
                    cp = ici(a, k)
                    cp.start()
                    sends.append(cp)
        for cp in smalls:
            cp.wait_recv()
        total = sbuf[me]
        for d in range(1, 8):
            total = total + sbuf[me ^ d]
        sm_out[...] = total
        posts = []
        for a in range(n):
            for j in range(3):
                ici(a, j).wait_recv()
            fin = own[a][mine]
            for j in range(3):
                fin = fin + rbuf[a][j].astype(F32)
            gout[a][rows(a, c), :] = fin
            cp = post(a, c)
            cp.start()
            posts.append(cp)
        for a in range(n):
            post(a, 1 - c).wait_recv()
        for cp in [cp for row in pres for cp in row] + sends + smalls + posts:
            cp.wait_send()

    vmem = pl.BlockSpec(memory_space=pltpu.VMEM)
    dma = pltpu.SemaphoreType.DMA
    return pl.pallas_call(
        body, name="reduce_grads",
        in_specs=[pl.BlockSpec(memory_space=pl.ANY)] * n + [vmem], out_specs=[vmem] * (n + 1),
        out_shape=[jax.ShapeDtypeStruct(sh, F32) for sh in shapes] + [jax.ShapeDtypeStruct(small.shape, F32)],
        scratch_shapes=([pltpu.VMEM((4,) + hs, F32) for hs in halves] + [pltpu.VMEM((4,) + hs, F32) for hs in halves]
                        + [pltpu.VMEM((4,) + hs, BF16) for hs in halves]
                        + [pltpu.VMEM((3,) + hs, BF16) for hs in halves]
                        + [pltpu.VMEM((8,) + small.shape, F32), dma((n, 3)), dma((n, 3)), dma((n, 4)),
                           dma((n, 4)), dma((n, 4)), dma((n,)), dma((n,)), dma((7,)), dma((7,))]),
        compiler_params=pltpu.CompilerParams(vmem_limit_bytes=VMEM_LIMIT),
    )(*parts, small)


def _adamw_update(w_ref, g_ref, m_ref, v_ref, d_ref, nm_ref, nv_ref):
    gv = g_ref[...]
    nm = ADAM_B1 * m_ref[...] + (1.0 - ADAM_B1) * gv
    nv = ADAM_B2 * v_ref[...] + (1.0 - ADAM_B2) * (gv * gv)
    m_hat = nm / (1.0 - ADAM_B1 ** ADAM_STEP)
    v_hat = nv / (1.0 - ADAM_B2 ** ADAM_STEP)
    d_ref[...] = -ADAM_LR * (m_hat / (jnp.sqrt(v_hat) + ADAM_EPS) + ADAM_WD * w_ref[...])
    nm_ref[...] = nm
    nv_ref[...] = nv


def _adamw_small(ws, gs, ms, vs):
    k = len(ws)

    def body(*refs):
        ins, outs = refs[:4 * k], refs[4 * k:]
        for a in range(k):
            _adamw_update(ins[a], ins[k + a], ins[2 * k + a], ins[3 * k + a], outs[a], outs[k + a], outs[2 * k + a])

    out = pl.pallas_call(
        body, name="adamw_small",
        out_shape=[jax.ShapeDtypeStruct(w.shape, F32) for w in ws] * 3,
        compiler_params=pltpu.CompilerParams(vmem_limit_bytes=VMEM_LIMIT),
    )(*ws, *gs, *ms, *vs)
    return out[:k], out[k:2 * k], out[2 * k:]


def _adamw(w, g, m, v, name):
    shape = w.shape
    w2, g2, m2, v2 = (a.reshape((-1, shape[-1])) for a in (w, g, m, v))

    def body(w_ref, g_ref, m_ref, v_ref, d_ref, nm_ref, nv_ref):
        _adamw_update(w_ref, g_ref, m_ref, v_ref, d_ref, nm_ref, nv_ref)

    rows, cols = w2.shape
    nblk = cols // 256 if cols % 256 == 0 and rows >= 64 else 1
    blk = pl.BlockSpec((rows, cols // nblk), lambda j: (0, j))
    out = pl.pallas_call(
        body, name=name, grid=(nblk,), in_specs=[blk] * 4, out_specs=[blk] * 3,
        out_shape=[jax.ShapeDtypeStruct(w2.shape, F32)] * 3,
        compiler_params=_cparams("parallel"),
    )(w2, g2, m2, v2)
    return tuple(a.reshape(shape) for a in out)


def kernel(x, meta_tokens, norm_g, w_in, q_norm_g, w_q_up, kv_norm_g, w_kv_up, conv_w, attn_out_g, conv_out_g, w_out, final_norm_g, loss_target, m_meta_tokens, m_norm_g, m_w_in, m_q_norm_g, m_w_q_up, m_kv_norm_g, m_w_kv_up, m_conv_w, m_attn_out_g, m_conv_out_g, m_w_out, m_final_norm_g, v_meta_tokens, v_norm_g, v_w_in, v_q_norm_g, v_w_q_up, v_kv_norm_g, v_w_kv_up, v_conv_w, v_attn_out_g, v_conv_out_g, v_w_out, v_final_norm_g):
    nb, s, _ = x.shape
    tm = min(ROW_TILE, s)
    ta = min(ATTN_TILE, s)
    assert s % tm == 0 and s % ta == 0 and tm % 16 == 0
    r = nb * s

    tr = lambda a: jnp.transpose(a[0])
    w_head, w_in_shard, wq_p, wkv_p, g_cw, g_meta = _gather_weights(
        tr(w_in), [tr(w_q_up), tr(w_kv_up)],
        [W_Q_PIECES, W_KV_PIECES], [HEADS * QK_PAD, 1024],
        [jnp.transpose(conv_w, (1, 0, 2)), meta_tokens],
        [(0, jnp.zeros((64, Q_RANK), BF16), QK_PAD * h + NOPE + ROPE) for h in range(HEADS)])
    conv_f = jnp.transpose(g_cw[:, :, 0, :], (1, 0, 2)).reshape(3, CONV_W)
    meta_f = jnp.transpose(g_meta, (1, 0, 2)).reshape(N_META, D_MODEL)

    c_all, sa_all, sb_all = _rope_tables(N_META + s)
    tabs_m = (c_all[:N_META], sa_all[:N_META], sb_all[:N_META])
    tabs = (c_all[N_META:], sa_all[N_META:], sb_all[N_META:])
    gid = np.arange(CONV_W) // CONV_GROUP
    gmat = jnp.asarray(np.where(gid[:, None] == gid[None, :], 1.0 / CONV_GROUP, 0.0), BF16)
    ga, gc = attn_out_g, conv_out_g
    gf = final_norm_g.reshape(1, D_MODEL)

    x2d = x.reshape(r, D_MODEL)
    tgt2d = loss_target.reshape(r, D_MODEL)

    ph, q, k, v, pmh, km, vm, w_out_f, w_in_part = _fwd_proj(
        x2d, meta_f, tabs, tabs_m, norm_g, w_head, q_norm_g, wq_p, kv_norm_g, wkv_p, w_out[0].astype(BF16),
        w_in_shard, nb, s, tm)
    o, lse, w_in_p = _attn_fwd(q, k, v, km, vm, w_in_shard, w_in_part, nb, s, ta)
    dh2, dycat, dw_out, dgf, loss_acc, pt, pmt = _out_fwd_bwd(x2d, tgt2d, o, meta_f, norm_g, w_in_p, conv_f, ga, gc,
                                                              gmat, w_out_f, gf, nb, s, tm)
    dpb, do, delta, dccm, dga, dgc, dcw = _gate_bwd(dycat, o, pt, pmt, conv_f, ga, gc, gmat, nb, s, tm)
    p_out = dw_out.reshape(4, D_MODEL // 4, D_MODEL)
    dq, dk, dv, dkm, dvm, g_w_out = _attn_bwd(q, k, v, do, lse, delta, km, vm, [p_out], nb, s, ta)
    dpa, dpam, p_q, p_kv, dgq, dgkv = _up_bwd(dq, dk, dv, dkm, dvm, ph, pmh, tabs, tabs_m, wq_p, wkv_p,
                                              q_norm_g, kv_norm_g, nb, s, tm)
    gx, gmeta, p_in, dng = _in_bwd(x2d, dh2, dpa, dpb, meta_f, dpam, dccm, pmt, w_in_p, norm_g, nb, s, tm)

    flat =jnp.concatenate([dng.reshape(-1), dgq.reshape(-1), dgkv.reshape(-1), dga.reshape(-1), dgc.reshape(-1),
                            dgf.reshape(-1), dcw[:3].reshape(-1), gmeta.reshape(-1), loss_acc[0, 0:1]])
    n_small = flat.shape[0]
    rows_small = -(-n_small // 1024) * 8
    small = jnp.pad(flat, (0, rows_small * 128 - n_small)).reshape(rows_small, 128)
    g_w_in_t, g_w_q_t, g_w_kv_t, small_sum = _reduce_grads([p_in, p_q, p_kv], small)
    ssum = small_sum.reshape(-1)

    def take(off, n):
        return ssum[off:off + n], off + n

    off = 0
    g_norm, off = take(off, D_MODEL)
    g_qn, off = take(off, Q_RANK)
    g_kvn, off = take(off, KV_RANK)
    g_ga, off = take(off, CONV_W)
    g_gc, off = take(off, CONV_W)
    g_gf, off = take(off, D_MODEL)
    g_cw_all, off = take(off, 3 * CONV_W)
    g_meta_all, off = take(off, N_META * D_MODEL)
    loss = ssum[off]
    chip = 2 * lax.axis_index("x") + lax.axis_index("y")
    g_conv = lax.dynamic_slice(g_cw_all.reshape(3, CONV_W), (0, chip * 128), (3, 128))
    g_mt = lax.dynamic_slice(g_meta_all.reshape(N_META, D_MODEL), (0, chip * 256), (N_META, 256))

    grads = {
        "meta_tokens": g_mt, "norm_g": g_norm.reshape(1, -1), "w_in": g_w_in_t, "q_norm_g": g_qn.reshape(1, -1),
        "w_q_up": g_w_q_t, "kv_norm_g": g_kvn.reshape(1, -1), "w_kv_up": jnp.transpose(g_w_kv_t)[None],
        "conv_w": g_conv[None], "attn_out_g": g_ga.reshape(1, -1), "conv_out_g": g_gc.reshape(1, -1),
        "w_out": g_w_out[None], "final_norm_g": g_gf,
    }
    transposed = ("w_in", "w_q_up")
    weights = {
        "meta_tokens": (meta_tokens, m_meta_tokens, v_meta_tokens), "norm_g": (norm_g, m_norm_g, v_norm_g),
        "w_in": (w_in, m_w_in, v_w_in), "q_norm_g": (q_norm_g, m_q_norm_g, v_q_norm_g),
        "w_q_up": (w_q_up, m_w_q_up, v_w_q_up), "kv_norm_g": (kv_norm_g, m_kv_norm_g, v_kv_norm_g),
        "w_kv_up": (w_kv_up, m_w_kv_up, v_w_kv_up), "conv_w": (conv_w, m_conv_w, v_conv_w),
        "attn_out_g": (attn_out_g, m_attn_out_g, v_attn_out_g), "conv_out_g": (conv_out_g, m_conv_out_g, v_conv_out_g),
        "w_out": (w_out, m_w_out, v_w_out), "final_norm_g": (final_norm_g, m_final_norm_g, v_final_norm_g),
    }
    names = list(weights)
    small = [nme for nme in names if nme != "w_in"]

    def view(nme, a):
        if nme in transposed:
            return a if a.ndim == 2 else tr(a)
        if nme == "conv_w":
            return jnp.transpose(a.reshape(1, 3, -1), (1, 0, 2))
        if a.ndim == 3:
            return a[0]
        return a.reshape(1, -1) if a.ndim == 1 else a

    def unview(nme, a):
        if nme in transposed:
            return jnp.transpose(a)[None]
        if nme == "conv_w":
            return jnp.transpose(a, (1, 0, 2))
        return a.reshape(weights[nme][0].shape)

    res_small = _adamw_small(*[[view(nme, a) for nme, a in zip(small, col)] for col in (
        [weights[nme][0] for nme in small], [grads[nme] for nme in small],
        [weights[nme][1] for nme in small], [weights[nme][2] for nme in small])])
    w_, m_, v_ = weights["w_in"]
    res = _adamw(tr(w_), grads["w_in"], tr(m_), tr(v_), "adamw_w_in")
    upd = {"w_in": tuple(jnp.transpose(a)[None] for a in (grads["w_in"],) + res)}
    for j, nme in enumerate(small):
        upd[nme] = (unview(nme, view(nme, grads[nme])),) + tuple(unview(nme, r[j]) for r in res_small)
    grads = {nme: upd[nme][0] for nme in names}
    deltas, new_m, new_v = ([upd[nme][j] for nme in names] for j in (1, 2, 3))

    grad_x = gx.reshape(nb, s, D_MODEL)
    return (loss, grad_x, *[grads[nme] for nme in names], *deltas, *new_m, *new_v)
```

```python
import functools

import jax
import jax.numpy as jnp
import numpy as np
from jax import lax
from jax.experimental import pallas as pl
from jax.experimental.pallas import tpu as pltpu

F32 = jnp.float32
BF16 = jnp.bfloat16

D_MODEL = 1024
N_META = 16
HEADS = 4
NOPE = 128
ROPE = 64
VDIM = 128
QK_PAD = 256
Q_RANK = 256
KV_RANK = 128
CONV_W = 512
CONV_GROUP = 64
ROPE_THETA = 10000.0
EPS = 1e-6
ATTN_SCALE = (NOPE + ROPE) ** -0.5
IN_DIM = 3008
IN_PAD = 3072
HEAD_ROWS = Q_RANK + KV_RANK + ROPE
HEAD_CHUNKS = 2
IN_HEAD = 512
IN_TAIL = IN_PAD - IN_HEAD
BLK_ZA, BLK_CB, BLK_CC, BLK_CH, BLK_ZC = 0, 1, 2, 3, 4
NEG_INF = -1e30

ADAM_LR = 0.001
ADAM_B1 = 0.9
ADAM_B2 = 0.999
ADAM_EPS = 1e-08
ADAM_WD = 0.01
ADAM_STEP = 10

ROW_TILE = 512
ATTN_TILE = 256
VMEM_LIMIT = 56 * 1024 * 1024

NT = (((1,), (1,)), ((), ()))
TN = (((0,), (0,)), ((), ()))


def _cparams(*sem):
    return pltpu.CompilerParams(dimension_semantics=sem, vmem_limit_bytes=VMEM_LIMIT)


def _dot(a, b):
    return jnp.dot(a, b, preferred_element_type=F32)


def _dot_nt(a, b):
    return lax.dot_general(a, b, NT, preferred_element_type=F32)


def _dot_tn(a, b):
    return lax.dot_general(a, b, TN, preferred_element_type=F32)


def _rms(x, g):
    r = lax.rsqrt(jnp.mean(x * x, axis=-1, keepdims=True) + EPS)
    return x * r * g, r


def _rms_bwd(dy, x, r, g):
    xh = x * r
    dyg = dy * g
    dx = r * (dyg - xh * jnp.mean(dyg * xh, axis=-1, keepdims=True))
    return dx, dy * xh


def _sigmoid(z):
    return 1.0 / (1.0 + jnp.exp(-z))


def _rope(b, c, sa, sb):
    return b * c + pltpu.roll(b, 96, 1) * sa + pltpu.roll(b, 32, 1) * sb


def _rope_bwd(d, c, sa, sb):
    return d * c + pltpu.roll(d * sa, 32, 1) + pltpu.roll(d * sb, 96, 1)


def _group_mean(x, gmat):
    hi = x.astype(BF16)
    lo = (x - hi.astype(F32)).astype(BF16)
    return _dot(hi, gmat) + _dot(lo, gmat)


def _row_of(col, rows):
    return jnp.transpose(jnp.broadcast_to(col, (rows, 128)))[0:1, :]


def _rope_tables(n_pos):
    half = ROPE // 2
    inv_freq = (np.float32(1.0) / (np.float32(ROPE_THETA) ** (np.arange(half, dtype=np.float32) / np.float32(half))))
    ang = np.arange(n_pos, dtype=np.float32)[:, None] * inv_freq.astype(np.float32)[None, :]
    cos, sin = np.cos(ang).astype(np.float32), np.sin(ang).astype(np.float32)
    z = np.zeros((n_pos, half), np.float32)
    c = np.concatenate([cos, cos, z, z], axis=1)
    sa = np.concatenate([-sin, z, z, z], axis=1)
    sb = np.concatenate([z, sin, z, z], axis=1)
    return jnp.asarray(c), jnp.asarray(sa), jnp.asarray(sb)


W_IN_PIECES_1 = ((0, 128, 752, 0, 64, 0), (384, 64, 752, 384, 448, 1), (448, 64, 752, 512, 512, 1))
W_IN_PIECES_2 = ((128, 256, 752, 128, 192, 0), (512, 240, 752, 576, 576, 1))
W_Q_PIECES = ((0, 96, 256, 0, 0, 0), (96, 96, 256, 96, 96, 1))
W_KV_PIECES = ((0, 128, 128, 0, 0, 0), (128, 128, 128, 512, 512, 1))
W_OUT_PIECES = ((0, 128, 256, 0, 0, 0), (128, 128, 256, 128, 128, 1))


class _StagedGather:
    STAGES = 4

    @staticmethod
    def steps(n_steps):
        return (0, 5 * n_steps // 8, 7 * n_steps // 8, n_steps - 1)

    def __init__(self, pieces, zero_rows=None):
        self.pieces = pieces
        self.zero_rows = zero_rows

    def scratch(self):
        nk, dma = len(self.pieces), pltpu.SemaphoreType.DMA
        return [dma((nk, 3)), dma((nk, 3)), dma((nk, 3)), dma((nk, 3)), dma((nk,))]

    def vmem_scratch(self, shard_shape, out_shape):
        return [pltpu.VMEM(shard_shape, BF16), pltpu.VMEM(out_shape, BF16),
                pltpu.SemaphoreType.DMA((4 * len(self.pieces) + 2,))] + self.scratch()

    def run_vmem(self, stage, shard_ref, out_ref, scr):
        src_scr, land_scr, io_sems = scr[:3]
        spans = []
        for _, nr, per, first, rest, _ in self.pieces:
            spans += [(per * q + (first if q == 0 else rest), nr) for q in range(4)]
        if self.zero_rows is not None:
            spans.append(self.zero_rows)
        flush = [pltpu.make_async_copy(land_scr.at[r0:r0 + nr], out_ref.at[r0:r0 + nr], io_sems.at[n])
                 for n, (r0, nr) in enumerate(spans)]
        if stage == 0:
            load = pltpu.make_async_copy(shard_ref, src_scr, io_sems.at[len(spans)])
            load.start()
            if self.zero_rows is not None:
                r0, nr = self.zero_rows
                land_scr[r0:r0 + nr, :] = jnp.zeros((nr, land_scr.shape[1]), BF16)
            load.wait()
        if stage < self.STAGES:
            self.run(stage, src_scr, land_scr, scr[3:])
        for cp in flush:
            if stage == self.STAGES - 1:
                cp.start()
            if stage == self.STAGES:
                cp.wait()

    def run(self, stage, src_ref, out_ref, scr):
        send_sems, recv_sems, fwd_send, fwd_recv, loc_sems = scr
        pieces = self.pieces
        nk = len(pieces)
        x, y, c = lax.axis_index("x"), lax.axis_index("y"), lax.axis_index("c")
        mine = 2 * x + y
        chips = [(1 - x, y), (x, 1 - y), (1 - x, 1 - y)]
        chip_of = [2 * px + py for px, py in chips]
        mesh = pl.DeviceIdType.MESH

        def src(k):
            s0, nr = pieces[k][0], pieces[k][1]
            return src_ref.at[s0:s0 + nr]

        def dst(k, q):
            _, nr, per, first, rest, _ = pieces[k]
            row = per * q + first + (rest - first) * jnp.minimum(q, 1)
            return out_ref.at[pl.ds(pl.multiple_of(row, 16), nr)]

        def ici(k, j, q):
            px, py = chips[j]
            return pltpu.make_async_remote_copy(
                src_ref=src(k), dst_ref=dst(k, q), send_sem=send_sems.at[k, j], recv_sem=recv_sems.at[k, j],
                device_id=(px, py, c), device_id_type=mesh)

        def fwd(k, j):
            ref = dst(k, chip_of[j])
            return pltpu.make_async_remote_copy(
                src_ref=ref, dst_ref=ref, send_sem=fwd_send.at[k, j], recv_sem=fwd_recv.at[k, j],
                device_id=(x, y, 1 - c), device_id_type=mesh)

        def relay(k, half):
            ref = dst(k, chip_of[half])
            px, py = chips[1 - half]
            return pltpu.make_async_remote_copy(
                src_ref=ref, dst_ref=ref, send_sem=send_sems.at[k, 2], recv_sem=recv_sems.at[k, 2],
                device_id=(px, py, c), device_id_type=mesh)

        local = [pltpu.make_async_copy(src(k), dst(k, mine), loc_sems.at[k]) for k in range(nk)]
        if stage == 0:
            for cp in local:
                cp.start()
        if stage == 3:
            for cp in local:
                cp.wait()
        for half in (0, 1):
            @pl.when(c == half)
            def _(half=half):
                my_k = [k for k in range(nk) if pieces[k][5] == half]
                other_k = [k for k in range(nk) if pieces[k][5] != half]
                for k in my_k:
                    if stage == 0:
                        for j in range(2):
                            ici(k, j, mine).start()
                    elif stage == 1:
                        for j in (half, 1 - half):
                            ici(k, j, chip_of[j]).wait_recv()
                            if j == half:
                                relay(k, half).start()
                            fwd(k, j).start()
                    elif stage == 2:
                        ici(k, 2, chip_of[2]).wait_recv()
                        fwd(k, 2).start()
                    else:
                        for j in range(2):
                            ici(k, j, mine).wait_send()
                        relay(k, half).wait_send()
                        for j in range(3):
                            fwd(k, j).wait_send()
                if stage == 3:
                    for k in other_k:
                        for j in range(3):
                            fwd(k, j).wait_recv()


def _fwd_proj(x2d, meta, tabs, tabs_m, norm_g, w_head, q_norm_g, wq_p, kv_norm_g, wkv_p, w_out_shard, w_in_shard,
              nb, s, tm):
    nt = s // tm
    n = nb * nt
    n_steps = n + 1
    c_t, sa_t, sb_t = tabs
    cm_t, sam_t, sbm_t = tabs_m
    gat = _StagedGather(W_OUT_PIECES)
    gat_in = _StagedGather(W_IN_PIECES_1)
    n_sems = len(gat.scratch())
    assert n_steps >= 3

    def body(x_ref, c_ref, sa_ref, sb_ref, mt_ref, cm_ref, sam_ref, sbm_ref,
             g_ref, w_ref, gq_ref, wq_ref, gkv_ref, wkv_ref, wos_ref, wis_ref,
             p_ref, q_ref, k_ref, v_ref, pm_ref, km_ref, vm_ref, wo_ref, wi_ref, *scr):
        gat_scr, gat_in_scr = scr[:n_sems], scr[n_sems:]
        i = pl.program_id(0)
        for stage, at in enumerate(_StagedGather.steps(n_steps)):
            @pl.when(i == at)
            def _(stage=stage):
                gat_in.run_vmem(stage, wis_ref, wi_ref, gat_in_scr)
                gat.run(stage, wos_ref, wo_ref, gat_scr)

        def project(xv, c, sa, sb, p_out, q_out, k_out, v_out):
            u, _ = _rms(xv, g_ref[...])
            p = _dot_nt(u.astype(BF16), w_ref[...])
            p_out[...] = p
            qn, _ = _rms(p[:, 0:Q_RANK], gq_ref[...])
            q = _dot_nt(qn.astype(BF16), wq_ref[...])
            kvn, _ = _rms(p[:, Q_RANK:Q_RANK + KV_RANK], gkv_ref[...])
            kv = _dot_nt(kvn.astype(BF16), wkv_ref[...])
            kpe = _rope(p[:, 384:512], c, sa, sb)
            for h in range(HEADS):
                if q_out is not None:
                    pe = _rope(q[:, QK_PAD * h + NOPE:QK_PAD * (h + 1)], c, sa, sb)
                    qh = jnp.concatenate([q[:, QK_PAD * h:QK_PAD * h + NOPE], pe], axis=1)
                    q_out[0, h] = (qh * ATTN_SCALE).astype(BF16)
                k_out[0, h] = jnp.concatenate([kv[:, NOPE * h:NOPE * (h + 1)], kpe], axis=1).astype(BF16)
                v_out[0, h] = kv[:, 512 + VDIM * h:512 + VDIM * (h + 1)].astype(BF16)

        @pl.when(i < n)
        def _():
            project(x_ref[...], c_ref[...], sa_ref[...], sb_ref[...], p_ref, q_ref, k_ref, v_ref)

        @pl.when(i == n)
        def _():
            project(mt_ref[...], cm_ref[...], sam_ref[...], sbm_ref[...], pm_ref, None, km_ref, vm_ref)
            gat_in.run_vmem(gat_in.STAGES, wis_ref, wi_ref, gat_in_scr)

    cl = lambda i: jnp.minimum(i, n - 1)
    full = lambda a: pl.BlockSpec(a.shape, lambda i: (0,) * a.ndim)
    const = lambda shape: pl.BlockSpec(shape, lambda i: (0,) * len(shape))
    tab = pl.BlockSpec((tm, 128), lambda i: (cl(i) % nt, 0))
    hb = lambda w: pl.BlockSpec((1, HEADS, tm, w), lambda i: (cl(i) // nt, 0, cl(i) % nt, 0))
    whole = pl.BlockSpec(memory_space=pl.ANY)
    return pl.pallas_call(
        body, name="fwd_proj", grid=(n_steps,),
        in_specs=[pl.BlockSpec((tm, D_MODEL), lambda i: (cl(i), 0)), tab, tab, tab,
                  full(meta), full(cm_t), full(sam_t), full(sbm_t),
                  full(norm_g), full(w_head), full(q_norm_g), full(wq_p), full(kv_norm_g), full(wkv_p), whole, whole],
        out_specs=[pl.BlockSpec((tm, IN_HEAD), lambda i: (cl(i), 0)), hb(QK_PAD), hb(QK_PAD), hb(VDIM),
                   const((N_META, IN_HEAD)), const((1, HEADS, N_META, QK_PAD)), const((1, HEADS, N_META, VDIM)),
                   whole, whole],
        out_shape=[jax.ShapeDtypeStruct((nb * s, IN_HEAD), F32),
                   jax.ShapeDtypeStruct((nb, HEADS, s, QK_PAD), BF16),
                   jax.ShapeDtypeStruct((nb, HEADS, s, QK_PAD), BF16),
                   jax.ShapeDtypeStruct((nb, HEADS, s, VDIM), BF16),
                   jax.ShapeDtypeStruct((N_META, IN_HEAD), F32),
                   jax.ShapeDtypeStruct((1, HEADS, N_META, QK_PAD), BF16),
                   jax.ShapeDtypeStruct((1, HEADS, N_META, VDIM), BF16),
                   jax.ShapeDtypeStruct((D_MODEL, D_MODEL), BF16),
                   jax.ShapeDtypeStruct((IN_PAD, D_MODEL), BF16)],
        scratch_shapes=gat.scratch() + gat_in.vmem_scratch(w_in_shard.shape, (IN_PAD, D_MODEL)),
        compiler_params=_cparams("arbitrary"),
    )(x2d, c_t, sa_t, sb_t, meta, cm_t, sam_t, sbm_t, norm_g, w_head, q_norm_g, wq_p, kv_norm_g, wkv_p, w_out_shard,
      w_in_shard)


def _attn_fwd(q, k, v, km, vm, w_in_shard, w_in_part, nb, s, tq):
    nq = s // tq
    n_steps = nb * HEADS
    gat = _StagedGather(W_IN_PIECES_2, zero_rows=(HEAD_ROWS, IN_HEAD - HEAD_ROWS))
    assert n_steps >= 3

    def body(q_ref, k_ref, v_ref, km_ref, vm_ref, ws_ref, _, o_ref, lse_ref, w_ref, s_scr, p_scr, *gat_scr):
        step = pl.program_id(0) * HEADS + pl.program_id(1)
        for stage, at in enumerate(_StagedGather.steps(n_steps)):
            @pl.when(step == at)
            def _(stage=stage):
                gat.run_vmem(stage, ws_ref, w_ref, gat_scr)

        row = lax.broadcasted_iota(jnp.int32, (tq, tq), 0)
        col = lax.broadcasted_iota(jnp.int32, (tq, tq), 1)
        def scores(i):
            slot = i % 2
            qi = q_ref[0, 0, i * tq:(i + 1) * tq, :]
            sm = _dot_nt(qi, km_ref[0, 0])
            m128 = None
            for j in range(i + 1):
                sc = _dot_nt(qi, k_ref[0, 0, j * tq:(j + 1) * tq, :])
                if j == i:
                    sc = jnp.where(col <= row, sc, NEG_INF)
                s_scr[slot, :, j * tq:(j + 1) * tq] = sc
                mx = sc[:, 0:128]
                for c0 in range(128, tq, 128):
                    mx = jnp.maximum(mx, sc[:, c0:c0 + 128])
                m128 = mx if m128 is None else jnp.maximum(m128, mx)
            return sm, jnp.maximum(jnp.max(m128, axis=1, keepdims=True), jnp.max(sm, axis=1, keepdims=True))

        def weighted_sum(i, pm, l):
            n = (i + 1) * tq
            acc = _dot(p_scr[i % 2, :, 0:n], v_ref[0, 0, 0:n, :]) + _dot(pm.astype(BF16), vm_ref[0, 0])
            o_ref[0, 0, i * tq:(i + 1) * tq, :] = acc / l

        nxt, pending = scores(0), None
        for i in range(nq):
            slot = i % 2
            sm, m = nxt
            if i + 1 < nq:
                nxt = scores(i + 1)
            pm = jnp.exp(sm - m)
            l128 = None
            for j in range(i + 1):
                p = jnp.exp(s_scr[slot, :, j * tq:(j + 1) * tq] - m)
                p_scr[slot, :, j * tq:(j + 1) * tq] = p.astype(BF16)
                ps = p[:, 0:128]
                for c0 in range(128, tq, 128):
                    ps = ps + p[:, c0:c0 + 128]
                l128 = ps if l128 is None else l128 + ps
            l = jnp.sum(l128, axis=1, keepdims=True) + jnp.sum(pm, axis=1, keepdims=True)
            lse_ref[0, 0, :, i * tq:(i + 1) * tq] = _row_of(m + jnp.log(l), tq)
            if pending is not None:
                weighted_sum(*pending)
            pending = (i, pm, l)
        weighted_sum(*pending)

        @pl.when(step == n_steps - 1)
        def _():
            gat.run_vmem(gat.STAGES, ws_ref, w_ref, gat_scr)

    hblk = lambda w: pl.BlockSpec((1, 1, s, w), lambda b, h: (b, h, 0, 0))
    mblk = lambda w: pl.BlockSpec((1, 1, N_META, w), lambda b, h: (0, h, 0, 0))
    whole = pl.BlockSpec(memory_space=pl.ANY)
    return pl.pallas_call(
        body, name="attn_fwd", grid=(nb, HEADS),
        in_specs=[hblk(QK_PAD), hblk(QK_PAD), hblk(VDIM), mblk(QK_PAD), mblk(VDIM), whole, whole],
        out_specs=[hblk(VDIM), pl.BlockSpec((1, 1, 1, s), lambda b, h: (b, h, 0, 0)), whole],
        out_shape=[jax.ShapeDtypeStruct((nb, HEADS, s, VDIM), F32),
                   jax.ShapeDtypeStruct((nb, HEADS, 1, s), F32),
                   jax.ShapeDtypeStruct(w_in_part.shape, BF16)],
        input_output_aliases={6: 2},
        scratch_shapes=[pltpu.VMEM((2, tq, s), F32), pltpu.VMEM((2, tq, s), BF16)]
        + gat.vmem_scratch(w_in_shard.shape, w_in_part.shape),
        compiler_params=_cparams("arbitrary", "arbitrary"),
    )(q, k, v, km, vm, w_in_shard, w_in_part)


def _shift_rows(a, prev, n_rows):
    rid = lax.broadcasted_iota(jnp.int32, a.shape, 0)
    a1 = jnp.where(rid == 0, prev[7:8, :], pltpu.roll(a, 1, 0))
    a2 = jnp.where(rid == 0, prev[6:7, :], jnp.where(rid == 1, prev[7:8, :], pltpu.roll(a, 2, 0)))
    return a1, a2


def _attn_gate(o, za, ga_h):
    on, r = _rms(o, ga_h)
    return on * (za * _sigmoid(za)), on, r


def _out_fwd_bwd(x2d, tgt2d, o, meta, norm_g, w_in_p, conv_w, ga, gc, gmat, w_out, gf, nb, s, tm):
    nt = s // tm
    r = nb * s

    def body(x_ref, t_ref, o_ref, mt_ref, g_ref, wi_ref, cw_ref, ga_ref, gc_ref, gm_ref, w_ref, gf_ref,
             dh_ref, dy_ref, dw_ref, dgf_ref, loss_ref, p_ref, pm_ref, last_cc):
        i = pl.program_id(0)
        blk = lambda ref, j, rows=slice(None): ref[rows, 512 * j:512 * (j + 1)]

        def tail(xv):
            u, _ = _rms(xv, g_ref[...])
            return _dot_nt(u.astype(BF16), wi_ref[IN_HEAD:IN_PAD, :])

        @pl.when(i == 0)
        def _():
            dw_ref[...] = jnp.zeros_like(dw_ref)
            dgf_ref[...] = jnp.zeros_like(dgf_ref)
            loss_ref[...] = jnp.zeros_like(loss_ref)
            last_cc[...] = jnp.zeros_like(last_cc)
            pm_ref[...] = tail(mt_ref[...])

        u16 = _rms(x_ref[...], g_ref[...])[0].astype(BF16)

        def project(j):
            p_ref[:, 512 * j:512 * (j + 1)] = _dot_nt(u16, wi_ref[IN_HEAD + 512 * j:IN_HEAD + 512 * (j + 1), :])

        project(BLK_ZA)
        project(BLK_CC)
        project(BLK_CH)
        ya = []
        for h in range(HEADS):
            y, _, _ = _attn_gate(o_ref[0, h], p_ref[:, 512 * BLK_ZA + VDIM * h:512 * BLK_ZA + VDIM * (h + 1)],
                                 ga_ref[:, VDIM * h:VDIM * (h + 1)])
            ya.append(y)
        project(BLK_CB)
        project(BLK_ZC)
        cc = blk(p_ref, BLK_CC) * blk(p_ref, BLK_CH)
        meta_cc = blk(pm_ref, BLK_CC, slice(8, 16)) * blk(pm_ref, BLK_CH, slice(8, 16))
        prev = jnp.where(i % nt == 0, meta_cc, last_cc[...])
        last_cc[...] = cc[tm - 8:tm, :]
        cc1, cc2 = _shift_rows(cc, prev, tm)
        yc = blk(p_ref, BLK_CB) * (cw_ref[0:1, :] * cc2 + cw_ref[1:2, :] * cc1 + cw_ref[2:3, :] * cc)
        rg = lax.rsqrt(_group_mean(yc * yc, gm_ref[...]) + EPS)
        zc = blk(p_ref, BLK_ZC)
        yconv = yc * rg * gc_ref[...] * (zc * _sigmoid(zc))
        ycat = jnp.concatenate(ya + [yconv], axis=1).astype(BF16)
        h2 = x_ref[...] + _dot(ycat, w_ref[...])
        gfv = gf_ref[...]
        y, r2 = _rms(h2, gfv)
        e = y - t_ref[...]
        loss_ref[...] += 0.5 * jnp.sum(e * e) / D_MODEL
        dyv = e * (1.0 / D_MODEL)
        dh2, dgf = _rms_bwd(dyv, h2, r2, gfv)
        dgf_ref[...] += jnp.sum(dgf, axis=0, keepdims=True)
        dh_ref[...] = dh2
        dhb = dh2.astype(BF16)
        dy_ref[...] = _dot_nt(dhb, w_ref[...])
        dw_ref[...] += _dot_tn(ycat, dhb)

    row = lambda w: pl.BlockSpec((tm, w), lambda i: (i, 0))
    const = lambda shape: pl.BlockSpec(shape, lambda i: (0,) * len(shape))
    full = lambda a: const(a.shape)
    return pl.pallas_call(
        body, name="out_fwd_bwd", grid=(nb * nt,),
        in_specs=[row(D_MODEL), row(D_MODEL),
                  pl.BlockSpec((1, HEADS, tm, VDIM), lambda i: (i // nt, 0, i % nt, 0)),
                  full(meta), full(norm_g), full(w_in_p),
                  full(conv_w), full(ga), full(gc), full(gmat), full(w_out), full(gf)],
        out_specs=[row(D_MODEL), row(D_MODEL), const((D_MODEL, D_MODEL)), const((1, D_MODEL)), const((1, 128)),
                   row(IN_TAIL), const((N_META, IN_TAIL))],
        out_shape=[jax.ShapeDtypeStruct((r, D_MODEL), F32), jax.ShapeDtypeStruct((r, D_MODEL), F32),
                   jax.ShapeDtypeStruct((D_MODEL, D_MODEL), F32), jax.ShapeDtypeStruct((1, D_MODEL), F32),
                   jax.ShapeDtypeStruct((1, 128), F32),
                   jax.ShapeDtypeStruct((r, IN_TAIL), F32), jax.ShapeDtypeStruct((N_META, IN_TAIL), F32)],
        scratch_shapes=[pltpu.VMEM((8, 512), F32)],
        compiler_params=_cparams("arbitrary"),
    )(x2d, tgt2d, o, meta, norm_g, w_in_p, conv_w, ga, gc, gmat, w_out, gf)


def _gate_bwd(dycat, o, p, pm, conv_w, ga, gc, gmat, nb, s, tm):
    nt = s // tm
    r = nb * s
    ext = tm + 8
    prev_idx = lambda i: jnp.maximum(i * (tm // 8) - 1, 0)
    next_idx = lambda i: jnp.minimum((i + 1) * (tm // 8), r // 8 - 1)

    def body(dya_ref, dyc_ref, dycn_ref, o_ref, za_ref, cb_ref, cbn_ref, cc_ref, ccp_ref, ccn_ref,
             ch_ref, chp_ref, chn_ref, zc_ref, zcn_ref, mc_ref, mh_ref, cw_ref, ga_ref, gc_ref, gm_ref,
             dpb_ref, do_ref, dl_ref, dccm_ref, dga_ref, dgc_ref, dcw_ref):
        i = pl.program_id(0)

        @pl.when(i == 0)
        def _():
            dga_ref[...] = jnp.zeros_like(dga_ref)
            dgc_ref[...] = jnp.zeros_like(dgc_ref)
            dcw_ref[...] = jnp.zeros_like(dcw_ref)

        dga = []
        for h in range(HEADS):
            hs = slice(VDIM * h, VDIM * (h + 1))
            oh, za, gah, dya = o_ref[0, h], za_ref[:, hs], ga_ref[:, hs], dya_ref[:, hs]
            sg = _sigmoid(za)
            on, ro = _rms(oh, gah)
            don = dya * (za * sg)
            dpb_ref[:, hs] = (dya * on * (sg * (1.0 + za * (1.0 - sg)))).astype(BF16)
            do, dg = _rms_bwd(don, oh, ro, gah)
            dga.append(jnp.sum(dg, axis=0, keepdims=True))
            dob = do.astype(BF16)
            do_ref[0, h] = dob
            dl_ref[0, h] = _row_of(jnp.sum(dob.astype(F32) * oh, axis=1, keepdims=True), tm)
        dga_ref[...] += jnp.concatenate(dga, axis=1)

        cat = lambda a, b: jnp.concatenate([a[...], b[...]], axis=0)
        cch = cat(cc_ref, ccn_ref)
        chh = cat(ch_ref, chn_ref)
        cb = cat(cb_ref, cbn_ref)
        zc = cat(zc_ref, zcn_ref)
        dy = cat(dyc_ref, dycn_ref)
        first = i % nt == 0
        last = i % nt == nt - 1
        cc = cch * chh
        prev = jnp.where(first, mc_ref[8:16, :] * mh_ref[8:16, :], ccp_ref[...] * chp_ref[...])
        cc1, cc2 = _shift_rows(cc, prev, ext)
        w0, w1, w2 = cw_ref[0:1, :], cw_ref[1:2, :], cw_ref[2:3, :]
        dw = w0 * cc2 + w1 * cc1 + w2 * cc
        yc = cb * dw
        rg = lax.rsqrt(_group_mean(yc * yc, gm_ref[...]) + EPS)
        ych = yc * rg
        gcv = gc_ref[...]
        sg = _sigmoid(zc)
        dycn = dy * (zc * sg)
        dzc = dy * (ych * gcv) * (sg * (1.0 + zc * (1.0 - sg)))
        dgc_ref[...] += jnp.sum((dycn * ych)[:tm], axis=0, keepdims=True)
        dycg = dycn * gcv
        dyc = rg * (dycg - ych * _group_mean(dycg * ych, gm_ref[...]))
        rid = lax.broadcasted_iota(jnp.int32, (ext, CONV_W), 0)
        ddw = jnp.where(jnp.logical_and(last, rid >= tm), 0.0, dyc * cb)
        dcb = dyc * dw
        dcc = w2 * ddw + w1 * pltpu.roll(ddw, ext - 1, 0) + w0 * pltpu.roll(ddw, ext - 2, 0)
        dpb_ref[:, 512:1024] = dcb[:tm].astype(BF16)
        dpb_ref[:, 1024:1536] = (dcc * chh)[:tm].astype(BF16)
        dpb_ref[:, 1536:2048] = (dcc * cch)[:tm].astype(BF16)
        dpb_ref[:, 2048:2560] = dzc[:tm].astype(BF16)
        rs = lambda a: jnp.sum(a[:tm], axis=0, keepdims=True)
        dcw_ref[0:1, :] += rs(ddw * cc2)
        dcw_ref[1:2, :] += rs(ddw * cc1)
        dcw_ref[2:3, :] += rs(ddw * cc)

        @pl.when(first)
        def _():
            d0, d1 = ddw[0:1, :], ddw[1:2, :]
            r8 = lax.broadcasted_iota(jnp.int32, (8, CONV_W), 0)
            dccm_ref[0] = jnp.where(r8 == 7, w1 * d0 + w0 * d1, jnp.where(r8 == 6, w0 * d0, 0.0))

    row = lambda j: pl.BlockSpec((tm, 512), lambda i: (i, j))
    prv = lambda j: pl.BlockSpec((8, 512), lambda i: (prev_idx(i), j))
    nxt = lambda j: pl.BlockSpec((8, 512), lambda i: (next_idx(i), j))
    mblk = lambda j: pl.BlockSpec((N_META, 512), lambda i: (0, j))
    full = lambda a: pl.BlockSpec(a.shape, lambda i: (0,) * a.ndim)
    hb = lambda w: pl.BlockSpec((1, HEADS, tm, w), lambda i: (i // nt, 0, i % nt, 0))
    acc = lambda rr: pl.BlockSpec((rr, 512), lambda i: (0, 0))
    return pl.pallas_call(
        body, name="gate_bwd", grid=(nb * nt,),
        in_specs=[row(0), row(1), nxt(1), hb(VDIM),
                  row(BLK_ZA), row(BLK_CB), nxt(BLK_CB), row(BLK_CC), prv(BLK_CC), nxt(BLK_CC),
                  row(BLK_CH), prv(BLK_CH), nxt(BLK_CH), row(BLK_ZC), nxt(BLK_ZC),
                  mblk(BLK_CC), mblk(BLK_CH), full(conv_w), full(ga), full(gc), full(gmat)],
        out_specs=[pl.BlockSpec((tm, 2560), lambda i: (i, 0)), hb(VDIM),
                   pl.BlockSpec((1, HEADS, 1, tm), lambda i: (i // nt, 0, 0, i % nt)),
                   pl.BlockSpec((1, 8, 512), lambda i: (i // nt, 0, 0)),
                   acc(1), acc(1), acc(8)],
        out_shape=[jax.ShapeDtypeStruct((r, 2560), BF16), jax.ShapeDtypeStruct((nb, HEADS, s, VDIM), BF16),
                   jax.ShapeDtypeStruct((nb, HEADS, 1, s), F32), jax.ShapeDtypeStruct((nb, 8, 512), F32),
                   jax.ShapeDtypeStruct((1, 512), F32), jax.ShapeDtypeStruct((1, 512), F32),
                   jax.ShapeDtypeStruct((8, 512), F32)],
        compiler_params=_cparams("arbitrary"),
    )(dycat, dycat, dycat, o, p, p, p, p, p, p, p, p, p, p, p, pm, pm, conv_w, ga, gc, gmat)


class _StagedReduce:
    LOC, PRE_S, PRE_R, ICI_S, ICI_R, POST_S, POST_R, OUT, N_SEM = 0, 1, 2, 3, 6, 9, 10, 11, 12

    def __init__(self, shard_shape):
        self.half = (shard_shape[0] // 2, shard_shape[1])

    def scratch(self):
        h = self.half
        return [pltpu.VMEM((4,) + h, F32), pltpu.VMEM((4,) + h, F32), pltpu.VMEM((4,) + h, BF16),
                pltpu.VMEM((3,) + h, BF16), pltpu.VMEM(h, F32), pltpu.SemaphoreType.DMA((self.N_SEM,))]

    def run(self, stage, pin, gout, scr):
        own, sib, wire, rbuf, fin, sems = scr
        r2 = self.half[0]
        x, y, c = lax.axis_index("x"), lax.axis_index("y"), lax.axis_index("c")
        mine = 2 * x + y
        sibling = (x, y, 1 - c)
        chips = [(1 - x, y), (x, 1 - y), (1 - x, 1 - y)]
        rows = lambda half: pl.ds(pl.multiple_of(half * r2, r2), r2)
        mesh = pl.DeviceIdType.MESH

        loc = pltpu.make_async_copy(pin.at[:, rows(c), :], own, sems.at[self.LOC])
        pre = pltpu.make_async_remote_copy(
            src_ref=pin.at[:, rows(1 - c), :], dst_ref=sib, send_sem=sems.at[self.PRE_S],
            recv_sem=sems.at[self.PRE_R], device_id=sibling, device_id_type=mesh)

        def ici(j):
            px, py = chips[j]
            return pltpu.make_async_remote_copy(
                src_ref=wire.at[2 * px + py], dst_ref=rbuf.at[j], send_sem=sems.at[self.ICI_S + j],
                recv_sem=sems.at[self.ICI_R + j], device_id=(px, py, c), device_id_type=mesh)

        def post(half):
            return pltpu.make_async_remote_copy(
                src_ref=fin, dst_ref=gout.at[rows(half), :], send_sem=sems.at[self.POST_S],
                recv_sem=sems.at[self.POST_R], device_id=sibling, device_id_type=mesh)

        keep = pltpu.make_async_copy(fin, gout.at[rows(c), :], sems.at[self.OUT])
        if stage == 0:
            loc.start()
            pre.start()
        elif stage == 1:
            loc.wait()
            pre.wait_recv()
            for blk in range(4):
                tot = own[blk] + sib[blk]
                own[blk] = tot
                wire[blk] = tot.astype(BF16)
            for j in range(3):
                ici(j).start()
        elif stage == 2:
            for j in range(3):
                ici(j).wait_recv()
            tot = own[mine]
            for j in range(3):
                tot = tot + rbuf[j].astype(F32)
            fin[...] = tot
            post(c).start()
            keep.start()
        else:
            post(1 - c).wait_recv()
            pre.wait_send()
            for j in range(3):
                ici(j).wait_send()
            post(c).wait_send()
            keep.wait()


def _attn_bwd(q, k, v, do, lse, delta, km, vm, early, nb, s, t):
    n = s // t
    ne = len(early)
    reds = [_StagedReduce(a.shape[1:]) for a in early]
    n_steps = HEADS * nb
    assert n_steps >= 4

    def body(q_ref, k_ref, v_ref, do_ref, lse_ref, dl_ref, km_ref, vm_ref, *rest):
        pin_refs, rest = rest[:ne], rest[ne:]
        dq_ref, dk_ref, dv_ref, dkm_ref, dvm_ref = rest[:5]
        gout_refs, (p_scr, ds_scr, dq_acc), red_scr = rest[5:5 + ne], rest[5 + ne:8 + ne], rest[8 + ne:]
        b = pl.program_id(1)
        step = pl.program_id(0) * nb + b
        for stage, at in enumerate((0, 1, n_steps - 2, n_steps - 1)):
            @pl.when(step == at)
            def _(stage=stage):
                for a, red in enumerate(reds):
                    red.run(stage, pin_refs[a], gout_refs[a], red_scr[6 * a:6 * a + 6])

        @pl.when(b == 0)
        def _():
            dkm_ref[...] = jnp.zeros_like(dkm_ref)
            dvm_ref[...] = jnp.zeros_like(dvm_ref)

        kr = lax.broadcasted_iota(jnp.int32, (t, t), 0)
        qc = lax.broadcasted_iota(jnp.int32, (t, t), 1)
        km_v, vm_v = km_ref[0, 0], vm_ref[0, 0]
        ptm = jnp.exp(_dot_nt(km_v, q_ref[0, 0]) - lse_ref[0, 0])
        dstm = (ptm * (_dot_nt(vm_v, do_ref[0, 0]) - dl_ref[0, 0])).astype(BF16)
        dkm_ref[0] += _dot(dstm, q_ref[0, 0])
        dvm_ref[0] += _dot(ptm.astype(BF16), do_ref[0, 0])
        dq_acc[...] = _dot_tn(dstm, km_v)
        def tiles(j):
            slot = j % 2
            kj = k_ref[0, 0, j * t:(j + 1) * t, :]
            vj = v_ref[0, 0, j * t:(j + 1) * t, :]
            def products(i):
                cs = slice(i * t, (i + 1) * t)
                return _dot_nt(kj, q_ref[0, 0, cs, :]), _dot_nt(vj, do_ref[0, 0, cs, :])

            nxt, pending = products(j), None
            for i in range(j, n):
                cs = slice(i * t, (i + 1) * t)
                st, dpt = nxt
                if i + 1 < n:
                    nxt = products(i + 1)
                if i == j:
                    st = jnp.where(kr <= qc, st, NEG_INF)
                pt = jnp.exp(st - lse_ref[0, 0, :, cs])
                dst = (pt * (dpt - dl_ref[0, 0, :, cs])).astype(BF16)
                p_scr[slot, :, cs] = pt.astype(BF16)
                ds_scr[slot, :, cs] = dst
                if pending is not None:
                    dq_acc[pending[0], :] += _dot_tn(pending[1], kj)
                pending = (cs, dst)
            dq_acc[pending[0], :] += _dot_tn(pending[1], kj)

        for j in range(n):
            slot = j % 2
            tiles(j)
            dv_ref[0, 0, j * t:(j + 1) * t, :] = _dot(p_scr[slot, :, j * t:s], do_ref[0, 0, j * t:s, :]).astype(BF16)
            dk_ref[0, 0, j * t:(j + 1) * t, :] = _dot(ds_scr[slot, :, j * t:s], q_ref[0, 0, j * t:s, :]).astype(BF16)
        dq_ref[0, 0] = dq_acc[...].astype(BF16)

    big = lambda w: pl.BlockSpec((1, 1, s, w), lambda h, b: (b, h, 0, 0))
    rowv = pl.BlockSpec((1, 1, 1, s), lambda h, b: (b, h, 0, 0))
    mk = lambda w: pl.BlockSpec((1, 1, N_META, w), lambda h, b: (0, h, 0, 0))
    mo = lambda w: pl.BlockSpec((1, N_META, w), lambda h, b: (h, 0, 0))
    return pl.pallas_call(
        body, name="attn_bwd", grid=(HEADS, nb),
        in_specs=[big(QK_PAD), big(QK_PAD), big(VDIM), big(VDIM), rowv, rowv, mk(QK_PAD), mk(VDIM)]
        + [pl.BlockSpec(memory_space=pl.ANY)] * ne,
        out_specs=[big(QK_PAD), big(QK_PAD), big(VDIM), mo(QK_PAD), mo(VDIM)]
        + [pl.BlockSpec(memory_space=pl.ANY)] * ne,
        out_shape=[jax.ShapeDtypeStruct((nb, HEADS, s, QK_PAD), BF16),
                   jax.ShapeDtypeStruct((nb, HEADS, s, QK_PAD), BF16),
                   jax.ShapeDtypeStruct((nb, HEADS, s, VDIM), BF16),
                   jax.ShapeDtypeStruct((HEADS, N_META, QK_PAD), F32),
                   jax.ShapeDtypeStruct((HEADS, N_META, VDIM), F32)]
        + [jax.ShapeDtypeStruct(a.shape[1:], F32) for a in early],
        scratch_shapes=[pltpu.VMEM((2, t, s), BF16), pltpu.VMEM((2, t, s), BF16), pltpu.VMEM((s, QK_PAD), F32)]
        + [sc for red in reds for sc in red.scratch()],
        compiler_params=_cparams("arbitrary", "arbitrary"),
    )(q, k, v, do, lse, delta, km, vm, *early)


def _up_bwd(dq, dk, dv, dkm, dvm, p, pm, tabs, tabs_m, wq_p, wkv_p, gq, gkv, nb, s, tm):
    nt = s // tm
    n = nb * nt
    c_t, sa_t, sb_t = tabs
    cm_t, sam_t, sbm_t = tabs_m

    def kv_path(dkh, dvh, pa, c, sa, sb, wkv, gkvv):
        dkpe = dkh[0][:, NOPE:]
        for h in range(1, HEADS):
            dkpe = dkpe + dkh[h][:, NOPE:]
        dkr = _rope_bwd(dkpe, c, sa, sb)
        dkv = jnp.concatenate([d[:, :NOPE] for d in dkh] + list(dvh), axis=1).astype(BF16)
        ckv = pa[:, Q_RANK:Q_RANK + KV_RANK]
        kvn, rkv = _rms(ckv, gkvv)
        dckv, dg = _rms_bwd(_dot(dkv, wkv), ckv, rkv, gkvv)
        return dckv, dkr, kvn.astype(BF16), dkv, jnp.sum(dg, axis=0, keepdims=True)

    def body(dq_ref, dk_ref, dv_ref, pa_ref, c_ref, sa_ref, sb_ref,
             dkm_ref, dvm_ref, pam_ref, cm_ref, sam_ref, sbm_ref,
             wq_ref, wkv_ref, gq_ref, gkv_ref,
             dpa_ref, dpam_ref, pq_ref, pkv_ref, dgq_ref, dgkv_ref, dwq_ref, dwkv_ref):
        i = pl.program_id(0)

        @pl.when(i == 0)
        def _():
            dwq_ref[...] = jnp.zeros_like(dwq_ref)
            dwkv_ref[...] = jnp.zeros_like(dwkv_ref)
            dgq_ref[...] = jnp.zeros_like(dgq_ref)
            dgkv_ref[...] = jnp.zeros_like(dgkv_ref)

        @pl.when(i < n)
        def _():
            c, sa, sb = c_ref[...], sa_ref[...], sb_ref[...]
            pa = pa_ref[...]
            parts = []
            for h in range(HEADS):
                dqh = dq_ref[0, h].astype(F32) * ATTN_SCALE
                parts += [dqh[:, :NOPE], _rope_bwd(dqh[:, NOPE:], c, sa, sb)]
            dql = jnp.concatenate(parts, axis=1).astype(BF16)
            cq = pa[:, 0:Q_RANK]
            gqv = gq_ref[...]
            qn, rq = _rms(cq, gqv)
            dwq_ref[...] += _dot_tn(dql, qn.astype(BF16))
            dcq, dg = _rms_bwd(_dot(dql, wq_ref[...]), cq, rq, gqv)
            dgq_ref[...] += jnp.sum(dg, axis=0, keepdims=True)
            dckv, dkr, kvn, dkv, dgk = kv_path([dk_ref[0, h].astype(F32) for h in range(HEADS)],
                                               [dv_ref[0, h].astype(F32) for h in range(HEADS)],
                                               pa, c, sa, sb, wkv_ref[...], gkv_ref[...])
            dwkv_ref[...] += _dot_tn(dkv, kvn)
            dgkv_ref[...] += dgk
            dpa_ref[...] = jnp.concatenate([dcq, dckv, dkr], axis=1).astype(BF16)

        @pl.when(i == n)
        def _():
            dckv, dkr, kvn, dkv, dgk = kv_path([dkm_ref[h] for h in range(HEADS)],
                                               [dvm_ref[h] for h in range(HEADS)],
                                               pam_ref[...], cm_ref[...], sam_ref[...], sbm_ref[...],
                                               wkv_ref[...], gkv_ref[...])
            dwkv_ref[...] += _dot_tn(dkv, kvn)
            dgkv_ref[...] += dgk
            dpam_ref[...] = jnp.concatenate([jnp.zeros((N_META, Q_RANK), F32), dckv, dkr], axis=1)
            for h in range(HEADS):
                pq_ref[h] = dwq_ref[QK_PAD * h:QK_PAD * h + NOPE + ROPE, :]
                pkv_ref[h, 0:NOPE, :] = dwkv_ref[NOPE * h:NOPE * (h + 1), :]
                pkv_ref[h, NOPE:NOPE + VDIM, :] = dwkv_ref[512 + VDIM * h:512 + VDIM * (h + 1), :]

    cl = lambda i: jnp.minimum(i, n - 1)
    hb = lambda w: pl.BlockSpec((1, HEADS, tm, w), lambda i: (cl(i) // nt, 0, cl(i) % nt, 0))
    tab = pl.BlockSpec((tm, 128), lambda i: (cl(i) % nt, 0))
    full = lambda a: pl.BlockSpec(a.shape, lambda i: (0,) * a.ndim)
    const = lambda shape: pl.BlockSpec(shape, lambda i: (0,) * len(shape))
    return pl.pallas_call(
        body, name="up_bwd", grid=(n + 1,),
        in_specs=[hb(QK_PAD), hb(QK_PAD), hb(VDIM), pl.BlockSpec((tm, 512), lambda i: (cl(i), 0)), tab, tab, tab,
                  full(dkm), full(dvm), pl.BlockSpec((N_META, 512), lambda i: (0, 0)),
                  full(cm_t), full(sam_t), full(sbm_t), full(wq_p), full(wkv_p), full(gq), full(gkv)],
        out_specs=[pl.BlockSpec((tm, 512), lambda i: (cl(i), 0)), const((N_META, 512)),
                   const((HEADS, NOPE + ROPE, Q_RANK)), const((HEADS, NOPE + VDIM, KV_RANK)),
                   const((1, Q_RANK)), const((1, KV_RANK))],
        out_shape=[jax.ShapeDtypeStruct((nb * s, 512), BF16), jax.ShapeDtypeStruct((N_META, 512), F32),
                   jax.ShapeDtypeStruct((HEADS, NOPE + ROPE, Q_RANK), F32),
                   jax.ShapeDtypeStruct((HEADS, NOPE + VDIM, KV_RANK), F32),
                   jax.ShapeDtypeStruct((1, Q_RANK), F32), jax.ShapeDtypeStruct((1, KV_RANK), F32)],
        scratch_shapes=[pltpu.VMEM((HEADS * QK_PAD, Q_RANK), F32), pltpu.VMEM((1024, KV_RANK), F32)],
        compiler_params=_cparams("arbitrary"),
    )(dq, dk, dv, p, c_t, sa_t, sb_t, dkm, dvm, pm, cm_t, sam_t, sbm_t, wq_p, wkv_p, gq, gkv)


def _in_bwd(x2d, dh2, dpa, dpb, meta, dpam, dccm, pm, w_in_p, norm_g, nb, s, tm):
    nt = s // tm
    n = nb * nt

    def body(x_ref, dh_ref, dpa_ref, dpb_ref, mt_ref, dpam_ref, dccm_ref, mc_ref, mh_ref, w_ref, g_ref,
             gx_ref, gm_ref, dw_hbm, dg_ref, acc_ref, sems):
        i = pl.program_id(0)

        @pl.when(i == 0)
        def _():
            acc_ref[...] = jnp.zeros_like(acc_ref)
            dg_ref[...] = jnp.zeros_like(dg_ref)

        def rows(x, dp, dres):
            g = g_ref[...]
            dpb16 = dp.astype(BF16)
            du = _dot(dpb16, w_ref[...])
            u, r1 = _rms(x, g)
            acc_ref[...] += _dot_tn(dpb16, u.astype(BF16))
            dx, dg = _rms_bwd(du, x, r1, g)
            dg_ref[...] += jnp.sum(dg, axis=0, keepdims=True)
            return dx if dres is None else dx + dres

        @pl.when(i < n)
        def _():
            dp = jnp.concatenate([dpa_ref[...], dpb_ref[...]], axis=1)
            gx_ref[...] = rows(x_ref[...], dp, dh_ref[...])

        @pl.when(i == n)
        def _():
            dcc = dccm_ref[0]
            for b in range(1, nb):
                dcc = dcc + dccm_ref[b]
            z8 = jnp.zeros((8, CONV_W), F32)
            dc = jnp.concatenate([z8, dcc * mh_ref[8:16, :]], axis=0)
            dh = jnp.concatenate([z8, dcc * mc_ref[8:16, :]], axis=0)
            z = jnp.zeros((N_META, CONV_W), F32)
            dp = jnp.concatenate([dpam_ref[...], z, z, dc, dh, z], axis=1)
            gm_ref[...] = rows(mt_ref[...], dp, None)
            per = IN_DIM // 4
            cps = [pltpu.make_async_copy(acc_ref.at[0:448], dw_hbm.at[0, 0:448], sems.at[0]),
                   pltpu.make_async_copy(acc_ref.at[512:per + 64], dw_hbm.at[0, 448:per], sems.at[1])]
            for qq in range(1, 4):
                cps.append(pltpu.make_async_copy(acc_ref.at[per * qq + 64:per * (qq + 1) + 64], dw_hbm.at[qq],
                                                 sems.at[qq + 1]))
            for cp in cps:
                cp.start()
            for cp in cps:
                cp.wait()

    cl = lambda i: jnp.minimum(i, n - 1)
    row = lambda w: pl.BlockSpec((tm, w), lambda i: (cl(i), 0))
    full = lambda a: pl.BlockSpec(a.shape, lambda i: (0,) * a.ndim)
    mblk = lambda j: pl.BlockSpec((N_META, 512), lambda i: (0, j))
    return pl.pallas_call(
        body, name="in_bwd", grid=(n + 1,),
        in_specs=[row(D_MODEL), row(D_MODEL), row(512), row(2560), full(meta), full(dpam), full(dccm),
                  mblk(BLK_CC), mblk(BLK_CH), full(w_in_p), full(norm_g)],
        out_specs=[row(D_MODEL), pl.BlockSpec((N_META, D_MODEL), lambda i: (0, 0)),
                   pl.BlockSpec(memory_space=pl.ANY), pl.BlockSpec((1, D_MODEL), lambda i: (0, 0))],
        out_shape=[jax.ShapeDtypeStruct((nb * s, D_MODEL), F32), jax.ShapeDtypeStruct((N_META, D_MODEL), F32),
                   jax.ShapeDtypeStruct((4, IN_DIM // 4, D_MODEL), F32), jax.ShapeDtypeStruct((1, D_MODEL), F32)],
        scratch_shapes=[pltpu.VMEM((IN_PAD, D_MODEL), F32), pltpu.SemaphoreType.DMA((5,))],
        compiler_params=_cparams("arbitrary"),
    )(x2d, dh2, dpa, dpb, meta, dpam, dccm, pm, pm, w_in_p, norm_g)


def _gather_weights(w_in_shard, split, pieces, out_rows, whole, zero_fills):
    ns, nw, nz = len(split), len(whole), len(zero_fills)
    flat = [(a, pc) for a in range(ns) for pc in pieces[a]]
    nk = len(flat)
    hh = HEAD_ROWS // 2
    hc = hh // HEAD_CHUNKS
    assert hc * HEAD_CHUNKS == hh and hc % 16 == 0

    def body(*refs):
        ins, wins, zins = refs[1:1 + ns], refs[1 + ns:1 + ns + nw], refs[1 + ns + nw:1 + ns + nw + nz]
        n_in = 1 + ns + nw + nz
        head_ref, shard16 = refs[n_in], refs[n_in + 1]
        outs, wouts = refs[n_in + 2:n_in + 2 + ns], refs[n_in + 2 + ns:n_in + 2 + ns + nw]
        scr = refs[n_in + 2 + ns + nw:]
        stage = scr[:ns]
        (send_sems, recv_sems, fwd_send, fwd_recv, loc_sems, w_send, w_recv, w_loc, z_sems,
         h_send, h_recv, h_relay, h_pass) = scr[ns:]
        x, y, c = lax.axis_index("x"), lax.axis_index("y"), lax.axis_index("c")
        mine = 2 * x + y
        chips = [(1 - x, y), (x, 1 - y), (1 - x, 1 - y)]
        chip_of = [2 * px + py for px, py in chips]
        shard16[...] = refs[0][...].astype(BF16)
        for a in range(ns):
            stage[a][...] = ins[a][...].astype(BF16)

        def head_rows(half, n):
            return pl.ds(pl.multiple_of(half * hh + n * hc, 16), hc)

        def head_copy(j, n):
            px, py = chips[j]
            return pltpu.make_async_remote_copy(
                src_ref=shard16.at[head_rows(c, n)], dst_ref=head_ref.at[head_rows(c, n)], send_sem=h_send.at[j, n],
                recv_sem=h_recv.at[n], device_id=(px, py, c), device_id_type=pl.DeviceIdType.MESH)

        def head_relay(n):
            ref = head_ref.at[head_rows(c, n)]
            return pltpu.make_async_remote_copy(
                src_ref=ref, dst_ref=ref, send_sem=h_relay.at[n], recv_sem=h_recv.at[n],
                device_id=(1, 1, c), device_id_type=pl.DeviceIdType.MESH)

        def head_pass(half, n):
            ref = head_ref.at[head_rows(half, n)]
            return pltpu.make_async_remote_copy(
                src_ref=ref, dst_ref=ref, send_sem=h_pass.at[0, n], recv_sem=h_pass.at[1, n],
                device_id=(x, y, 1 - c), device_id_type=pl.DeviceIdType.MESH)

        head_ref[HEAD_ROWS:IN_HEAD, :] = jnp.zeros((IN_HEAD - HEAD_ROWS, D_MODEL), BF16)

        @pl.when(mine == 0)
        def _():
            for n in range(HEAD_CHUNKS):
                for j in range(2):
                    head_copy(j, n).start()
            head_ref[0:HEAD_ROWS, :] = shard16[0:HEAD_ROWS, :]

        def src(k):
            a, (s0, nr, _, _, _, _) = flat[k]
            return stage[a].at[s0:s0 + nr]

        def dst(k, q):
            a, (_, nr, per, first, rest, _) = flat[k]
            row = per * q + first + (rest - first) * jnp.minimum(q, 1)
            return outs[a].at[pl.ds(pl.multiple_of(row, 16), nr)]

        def ici(k, j, q):
            px, py = chips[j]
            return pltpu.make_async_remote_copy(
                src_ref=src(k), dst_ref=dst(k, q), send_sem=send_sems.at[k, j], recv_sem=recv_sems.at[k, j],
                device_id=(px, py, c), device_id_type=pl.DeviceIdType.MESH)

        def fwd(k, j):
            ref = dst(k, chip_of[j])
            return pltpu.make_async_remote_copy(
                src_ref=ref, dst_ref=ref, send_sem=fwd_send.at[k, j], recv_sem=fwd_recv.at[k, j],
                device_id=(x, y, 1 - c), device_id_type=pl.DeviceIdType.MESH)

        def wcopy(b, j, q):
            px, py = chips[j]
            return pltpu.make_async_remote_copy(
                src_ref=wins[b], dst_ref=wouts[b].at[q], send_sem=w_send.at[b, j], recv_sem=w_recv.at[b, j],
                device_id=(px, py, c), device_id_type=pl.DeviceIdType.MESH)

        local = [pltpu.make_async_copy(src(k), dst(k, mine), loc_sems.at[k]) for k in range(nk)]
        local += [pltpu.make_async_copy(wins[b], wouts[b].at[mine], w_loc.at[b]) for b in range(nw)]
        for z, (a, _, row0) in enumerate(zero_fills):
            local.append(pltpu.make_async_copy(zins[z], outs[a].at[row0:row0 + zins[z].shape[0]], z_sems.at[z]))
        wsends = [wcopy(b, j, mine) for b in range(nw) for j in range(3)]
        for cp in local + wsends:
            cp.start()

        for half in (0, 1):
            @pl.when(c == half)
            def _(half=half):
                my_k = [k for k in range(nk) if flat[k][1][5] == half]
                other_k = [k for k in range(nk) if flat[k][1][5] != half]
                sends = [ici(k, j, mine) for k in my_k for j in range(3)]
                for cp in sends:
                    cp.start()
                passed = []
                for k in my_k:
                    for j in range(3):
                        ici(k, j, chip_of[j]).wait_recv()
                        cp = fwd(k, j)
                        cp.start()
                        passed.append(cp)
                for k in other_k:
                    for j in range(3):
                        fwd(k, j).wait_recv()
                for cp in sends + passed:
                    cp.wait_send()

        for b in range(nw):
            for j in range(3):
                wcopy(b, j, chip_of[j]).wait_recv()
        for cp in wsends:
            cp.wait_send()
        for cp in local:
            cp.wait()

        @pl.when(mine == 0)
        def _():
            for n in range(HEAD_CHUNKS):
                for j in range(2):
                    head_copy(j, n).wait_send()

        @pl.when(mine != 0)
        def _():
            hands_on = mine == 2 - c
            for n in range(HEAD_CHUNKS):
                head_copy(0, n).wait_recv()

                @pl.when(hands_on)
                def _(n=n):
                    head_relay(n).start()

                head_pass(c, n).start()
            for n in range(HEAD_CHUNKS):
                head_pass(1 - c, n).wait_recv()
                head_pass(c, n).wait_send()

                @pl.when(hands_on)
                def _(n=n):
                    head_relay(n).wait_send()

    vmem = pl.BlockSpec(memory_space=pltpu.VMEM)
    dma = pltpu.SemaphoreType.DMA
    zeros = [z for _, z, _ in zero_fills]
    return pl.pallas_call(
        body, name="gather_weights",
        in_specs=[vmem] * (1 + ns + nw + nz), out_specs=[vmem] * (2 + ns + nw),
        out_shape=([jax.ShapeDtypeStruct((IN_HEAD, D_MODEL), BF16), jax.ShapeDtypeStruct(w_in_shard.shape, BF16)]
                   + [jax.ShapeDtypeStruct((out_rows[a], split[a].shape[1]), BF16) for a in range(ns)]
                   + [jax.ShapeDtypeStruct((4,) + w.shape, w.dtype) for w in whole]),
        scratch_shapes=[pltpu.VMEM(a.shape, BF16) for a in split]
        + [dma((nk, 3)), dma((nk, 3)), dma((nk, 3)), dma((nk, 3)), dma((nk,)),
           dma((nw, 3)), dma((nw, 3)), dma((nw,)), dma((nz,)),
           dma((2, HEAD_CHUNKS)), dma((HEAD_CHUNKS,)), dma((HEAD_CHUNKS,)), dma((2, HEAD_CHUNKS))],
        compiler_params=pltpu.CompilerParams(vmem_limit_bytes=VMEM_LIMIT),
    )(w_in_shard, *split, *whole, *zeros)


def _reduce_grads(parts, small):
    n = len(parts)
    shapes = [a.shape[1:] for a in parts]
    halves = [(sh[0] // 2, sh[1]) for sh in shapes]

    def body(*refs):
        pin, sm_in = refs[:n], refs[n]
        gout, sm_out = refs[n + 1:2 * n + 1], refs[2 * n + 1]
        scr = refs[2 * n + 2:]
        own, sib, wire, rbuf = scr[:n], scr[n:2 * n], scr[2 * n:3 * n], scr[3 * n:4 * n]
        (sbuf, send_sems, recv_sems, loc_sems, pre_send, pre_recv, post_send, post_recv,
         sm_send, sm_recv) = scr[4 * n:]
        x, y, c = lax.axis_index("x"), lax.axis_index("y"), lax.axis_index("c")
        mine = 2 * x + y
        me = 4 * x + 2 * y + c
        sibling = (x, y, 1 - c)
        chips = [(1 - x, y), (x, 1 - y), (1 - x, 1 - y)]

        def rows(a, half):
            r2 = halves[a][0]
            return pl.ds(pl.multiple_of(half * r2, r2), r2)

        chip_of = [2 * px + py for px, py in chips]
        blocks = chip_of + [mine]

        def pre(a, k):
            return pltpu.make_async_remote_copy(
                src_ref=pin[a].at[blocks[k], rows(a, 1 - c), :], dst_ref=sib[a].at[blocks[k]],
                send_sem=pre_send.at[a, k], recv_sem=pre_recv.at[a, k], device_id=sibling,
                device_id_type=pl.DeviceIdType.MESH)

        def ici(a, j):
            px, py = chips[j]
            return pltpu.make_async_remote_copy(
                src_ref=wire[a].at[2 * px + py], dst_ref=rbuf[a].at[j], send_sem=send_sems.at[a, j],
                recv_sem=recv_sems.at[a, j], device_id=(px, py, c), device_id_type=pl.DeviceIdType.MESH)

        def post(a, half):
            ref = gout[a].at[rows(a, half), :]
            return pltpu.make_async_remote_copy(
                src_ref=ref, dst_ref=ref, send_sem=post_send.at[a], recv_sem=post_recv.at[a],
                device_id=sibling, device_id_type=pl.DeviceIdType.MESH)

        def small_copy(kk):
            peer = (x ^ (kk >> 2), y ^ ((kk >> 1) & 1), c ^ (kk & 1))
            return pltpu.make_async_remote_copy(
                src_ref=sm_in, dst_ref=sbuf.at[kk], send_sem=sm_send.at[kk - 1], recv_sem=sm_recv.at[kk - 1],
                device_id=peer, device_id_type=pl.DeviceIdType.MESH)

        local = [[pltpu.make_async_copy(pin[a].at[blocks[k], rows(a, c), :], own[a].at[blocks[k]], loc_sems.at[a, k])
                  for k in range(4)] for a in range(n)]
        pres = [[pre(a, k) for k in range(4)] for a in range(n)]
        smalls = [small_copy(kk) for kk in range(1, 8)]
        for a in range(n):
            for k in range(4):
                local[a][k].start()
                pres[a][k].start()
        for cp in smalls:
            cp.start()
        sbuf[0] = sm_in[...]
        sends = []
        for a in range(n):
            for k in range(4):
                local[a][k].wait()
                pres[a][k].wait_recv()
                tot = own[a][blocks[k]] + sib[a][blocks[k]]
                own[a][blocks[k]] = tot
                if k < 3:
                    wire[a][blocks[k]] = tot.astype(BF16)
                    cp = ici(a, k)
                    cp.start()
                    sends.append(cp)
        for cp in smalls:
            cp.wait_recv()
        total = sbuf[me]
        for d in range(1, 8):
            total = total + sbuf[me ^ d]
        sm_out[...] = total
        posts = []
        for a in range(n):
            for j in range(3):
                ici(a, j).wait_recv()
            fin = own[a][mine]
            for j in range(3):
                fin = fin + rbuf[a][j].astype(F32)
            gout[a][rows(a, c), :] = fin
            cp = post(a, c)
            cp.start()
            posts.append(cp)
        for a in range(n):
            post(a, 1 - c).wait_recv()
        for cp in [cp for row in pres for cp in row] + sends + smalls + posts:
            cp.wait_send()

    vmem = pl.BlockSpec(memory_space=pltpu.VMEM)
    dma = pltpu.SemaphoreType.DMA
    return pl.pallas_call(
        body, name="reduce_grads",
        in_specs=[pl.BlockSpec(memory_space=pl.ANY)] * n + [vmem], out_specs=[vmem] * (n + 1),
        out_shape=[jax.ShapeDtypeStruct(sh, F32) for sh in shapes] + [jax.ShapeDtypeStruct(small.shape, F32)],
        scratch_shapes=([pltpu.VMEM((4,) + hs, F32) for hs in halves] + [pltpu.VMEM((4,) + hs, F32) for hs in halves]
                        + [pltpu.VMEM((4,) + hs, BF16) for hs in halves]
                        + [pltpu.VMEM((3,) + hs, BF16) for hs in halves]
                        + [pltpu.VMEM((8,) + small.shape, F32), dma((n, 3)), dma((n, 3)), dma((n, 4)),
                           dma((n, 4)), dma((n, 4)), dma((n,)), dma((n,)), dma((7,)), dma((7,))]),
        compiler_params=pltpu.CompilerParams(vmem_limit_bytes=VMEM_LIMIT),
    )(*parts, small)


def _adamw_update(w_ref, g_ref, m_ref, v_ref, d_ref, nm_ref, nv_ref):
    gv = g_ref[...]
    nm = ADAM_B1 * m_ref[...] + (1.0 - ADAM_B1) * gv
    nv = ADAM_B2 * v_ref[...] + (1.0 - ADAM_B2) * (gv * gv)
    m_hat = nm / (1.0 - ADAM_B1 ** ADAM_STEP)
    v_hat = nv / (1.0 - ADAM_B2 ** ADAM_STEP)
    d_ref[...] = -ADAM_LR * (m_hat / (jnp.sqrt(v_hat) + ADAM_EPS) + ADAM_WD * w_ref[...])
    nm_ref[...] = nm
    nv_ref[...] = nv


def _adamw_small(ws, gs, ms, vs):
    k = len(ws)

    def body(*refs):
        ins, outs = refs[:4 * k], refs[4 * k:]
        for a in range(k):
            _adamw_update(ins[a], ins[k + a], ins[2 * k + a], ins[3 * k + a], outs[a], outs[k + a], outs[2 * k + a])

    out = pl.pallas_call(
        body, name="adamw_small",
        out_shape=[jax.ShapeDtypeStruct(w.shape, F32) for w in ws] * 3,
        compiler_params=pltpu.CompilerParams(vmem_limit_bytes=VMEM_LIMIT),
    )(*ws, *gs, *ms, *vs)
    return out[:k], out[k:2 * k], out[2 * k:]


def _adamw(w, g, m, v, name):
    shape = w.shape
    w2, g2, m2, v2 = (a.reshape((-1, shape[-1])) for a in (w, g, m, v))

    def body(w_ref, g_ref, m_ref, v_ref, d_ref, nm_ref, nv_ref):
        _adamw_update(w_ref, g_ref, m_ref, v_ref, d_ref, nm_ref, nv_ref)

    rows, cols = w2.shape
    nblk = cols // 256 if cols % 256 == 0 and rows >= 64 else 1
    blk = pl.BlockSpec((rows, cols // nblk), lambda j: (0, j))
    out = pl.pallas_call(
        body, name=name, grid=(nblk,), in_specs=[blk] * 4, out_specs=[blk] * 3,
        out_shape=[jax.ShapeDtypeStruct(w2.shape, F32)] * 3,
        compiler_params=_cparams("parallel"),
    )(w2, g2, m2, v2)
    return tuple(a.reshape(shape) for a in out)


def kernel(x, meta_tokens, norm_g, w_in, q_norm_g, w_q_up, kv_norm_g, w_kv_up, conv_w, attn_out_g, conv_out_g, w_out, final_norm_g, loss_target, m_meta_tokens, m_norm_g, m_w_in, m_q_norm_g, m_w_q_up, m_kv_norm_g, m_w_kv_up, m_conv_w, m_attn_out_g, m_conv_out_g, m_w_out, m_final_norm_g, v_meta_tokens, v_norm_g, v_w_in, v_q_norm_g, v_w_q_up, v_kv_norm_g, v_w_kv_up, v_conv_w, v_attn_out_g, v_conv_out_g, v_w_out, v_final_norm_g):
    nb, s, _ = x.shape
    tm = min(ROW_TILE, s)
    ta = min(ATTN_TILE, s)
    assert s % tm == 0 and s % ta == 0 and tm % 16 == 0
    r = nb * s

    tr = lambda a: jnp.transpose(a[0])
    w_head, w_in_shard, wq_p, wkv_p, g_cw, g_meta = _gather_weights(
        tr(w_in), [tr(w_q_up), tr(w_kv_up)],
        [W_Q_PIECES, W_KV_PIECES], [HEADS * QK_PAD, 1024],
        [jnp.transpose(conv_w, (1, 0, 2)), meta_tokens],
        [(0, jnp.zeros((64, Q_RANK), BF16), QK_PAD * h + NOPE + ROPE) for h in range(HEADS)])
    conv_f = jnp.transpose(g_cw[:, :, 0, :], (1, 0, 2)).reshape(3, CONV_W)
    meta_f = jnp.transpose(g_meta, (1, 0, 2)).reshape(N_META, D_MODEL)

    c_all, sa_all, sb_all = _rope_tables(N_META + s)
    tabs_m = (c_all[:N_META], sa_all[:N_META], sb_all[:N_META])
    tabs = (c_all[N_META:], sa_all[N_META:], sb_all[N_META:])
    gid = np.arange(CONV_W) // CONV_GROUP
    gmat = jnp.asarray(np.where(gid[:, None] == gid[None, :], 1.0 / CONV_GROUP, 0.0), BF16)
    ga, gc = attn_out_g, conv_out_g
    gf = final_norm_g.reshape(1, D_MODEL)

    x2d = x.reshape(r, D_MODEL)
    tgt2d = loss_target.reshape(r, D_MODEL)

    ph, q, k, v, pmh, km, vm, w_out_f, w_in_part = _fwd_proj(
        x2d, meta_f, tabs, tabs_m, norm_g, w_head, q_norm_g, wq_p, kv_norm_g, wkv_p, w_out[0].astype(BF16),
        w_in_shard, nb, s, tm)
    o, lse, w_in_p = _attn_fwd(q, k, v, km, vm, w_in_shard, w_in_part, nb, s, ta)
    dh2, dycat, dw_out, dgf, loss_acc, pt, pmt = _out_fwd_bwd(x2d, tgt2d, o, meta_f, norm_g, w_in_p, conv_f, ga, gc,
                                                              gmat, w_out_f, gf, nb, s, tm)
    dpb, do, delta, dccm, dga, dgc, dcw = _gate_bwd(dycat, o, pt, pmt, conv_f, ga, gc, gmat, nb, s, tm)
    p_out = dw_out.reshape(4, D_MODEL // 4, D_MODEL)
    dq, dk, dv, dkm, dvm, g_w_out = _attn_bwd(q, k, v, do, lse, delta, km, vm, [p_out], nb, s, ta)
    dpa, dpam, p_q, p_kv, dgq, dgkv = _up_bwd(dq, dk, dv, dkm, dvm, ph, pmh, tabs, tabs_m, wq_p, wkv_p,
                                              q_norm_g, kv_norm_g, nb, s, tm)
    gx, gmeta, p_in, dng = _in_bwd(x2d, dh2, dpa, dpb, meta_f, dpam, dccm, pmt, w_in_p, norm_g, nb, s, tm)

    flat =jnp.concatenate([dng.reshape(-1), dgq.reshape(-1), dgkv.reshape(-1), dga.reshape(-1), dgc.reshape(-1),
                            dgf.reshape(-1), dcw[:3].reshape(-1), gmeta.reshape(-1), loss_acc[0, 0:1]])
    n_small = flat.shape[0]
    rows_small = -(-n_small // 1024) * 8
    small = jnp.pad(flat, (0, rows_small * 128 - n_small)).reshape(rows_small, 128)
    g_w_in_t, g_w_q_t, g_w_kv_t, small_sum = _reduce_grads([p_in, p_q, p_kv], small)
    ssum = small_sum.reshape(-1)

    def take(off, n):
        return ssum[off:off + n], off + n

    off = 0
    g_norm, off = take(off, D_MODEL)
    g_qn, off = take(off, Q_RANK)
    g_kvn, off = take(off, KV_RANK)
    g_ga, off = take(off, CONV_W)
    g_gc, off = take(off, CONV_W)
    g_gf, off = take(off, D_MODEL)
    g_cw_all, off = take(off, 3 * CONV_W)
    g_meta_all, off = take(off, N_META * D_MODEL)
    loss = ssum[off]
    chip = 2 * lax.axis_index("x") + lax.axis_index("y")
    g_conv = lax.dynamic_slice(g_cw_all.reshape(3, CONV_W), (0, chip * 128), (3, 128))
    g_mt = lax.dynamic_slice(g_meta_all.reshape(N_META, D_MODEL), (0, chip * 256), (N_META, 256))

    grads = {
        "meta_tokens": g_mt, "norm_g": g_norm.reshape(1, -1), "w_in": g_w_in_t, "q_norm_g": g_qn.reshape(1, -1),
        "w_q_up": g_w_q_t, "kv_norm_g": g_kvn.reshape(1, -1), "w_kv_up": jnp.transpose(g_w_kv_t)[None],
        "conv_w": g_conv[None], "attn_out_g": g_ga.reshape(1, -1), "conv_out_g": g_gc.reshape(1, -1),
        "w_out": g_w_out[None], "final_norm_g": g_gf,
    }
    transposed = ("w_in", "w_q_up")
    weights = {
        "meta_tokens": (meta_tokens, m_meta_tokens, v_meta_tokens), "norm_g": (norm_g, m_norm_g, v_norm_g),
        "w_in": (w_in, m_w_in, v_w_in), "q_norm_g": (q_norm_g, m_q_norm_g, v_q_norm_g),
        "w_q_up": (w_q_up, m_w_q_up, v_w_q_up), "kv_norm_g": (kv_norm_g, m_kv_norm_g, v_kv_norm_g),
        "w_kv_up": (w_kv_up, m_w_kv_up, v_w_kv_up), "conv_w": (conv_w, m_conv_w, v_conv_w),
        "attn_out_g": (attn_out_g, m_attn_out_g, v_attn_out_g), "conv_out_g": (conv_out_g, m_conv_out_g, v_conv_out_g),
        "w_out": (w_out, m_w_out, v_w_out), "final_norm_g": (final_norm_g, m_final_norm_g, v_final_norm_g),
    }
    names = list(weights)
    small = [nme for nme in names if nme != "w_in"]

    def view(nme, a):
        if nme in transposed:
            return a if a.ndim == 2 else tr(a)
        if nme == "conv_w":
            return jnp.transpose(a.reshape(1, 3, -1), (1, 0, 2))
        if a.ndim == 3:
            return a[0]
        return a.reshape(1, -1) if a.ndim == 1 else a

    def unview(nme, a):
        if nme in transposed:
            return jnp.transpose(a)[None]
        if nme == "conv_w":
            return jnp.transpose(a, (1, 0, 2))
        return a.reshape(weights[nme][0].shape)

    res_small = _adamw_small(*[[view(nme, a) for nme, a in zip(small, col)] for col in (
        [weights[nme][0] for nme in small], [grads[nme] for nme in small],
        [weights[nme][1] for nme in small], [weights[nme][2] for nme in small])])
    w_, m_, v_ = weights["w_in"]
    res = _adamw(tr(w_), grads["w_in"], tr(m_), tr(v_), "adamw_w_in")
    upd = {"w_in": tuple(jnp.transpose(a)[None] for a in (grads["w_in"],) + res)}
    for j, nme in enumerate(small):
        upd[nme] = (unview(nme, view(nme, grads[nme])),) + tuple(unview(nme, r[j]) for r in res_small)
    grads = {nme: upd[nme][0] for nme in names}
    deltas, new_m, new_v = ([upd[nme][j] for nme in names] for j in (1, 2, 3))

    grad_x = gx.reshape(nb, s, D_MODEL)
    return (loss, grad_x, *[grads[nme] for nme in names], *deltas, *new_m, *new_v)
```

```python
import functools

import jax
import jax.numpy as jnp
import numpy as np
from jax import lax
from jax.experimental import pallas as pl
from jax.experimental.pallas import tpu as pltpu

F32 = jnp.float32
BF16 = jnp.bfloat16

D_MODEL = 1024
N_META = 16
HEADS = 4
NOPE = 128
ROPE = 64
VDIM = 128
QK_PAD = 256
Q_RANK = 256
KV_RANK = 128
CONV_W = 512
CONV_GROUP = 64
ROPE_THETA = 10000.0
EPS = 1e-6
ATTN_SCALE = (NOPE + ROPE) ** -0.5
IN_DIM = 3008
IN_PAD = 3072
HEAD_ROWS = Q_RANK + KV_RANK + ROPE
HEAD_CHUNKS = 2
IN_HEAD = 512
IN_TAIL = IN_PAD - IN_HEAD
BLK_ZA, BLK_CB, BLK_CC, BLK_CH, BLK_ZC = 0, 1, 2, 3, 4
NEG_INF = -1e30

ADAM_LR = 0.001
ADAM_B1 = 0.9
ADAM_B2 = 0.999
ADAM_EPS = 1e-08
ADAM_WD = 0.01
ADAM_STEP = 10

ROW_TILE = 512
ATTN_TILE = 256
VMEM_LIMIT = 56 * 1024 * 1024

NT = (((1,), (1,)), ((), ()))
TN = (((0,), (0,)), ((), ()))


def _cparams(*sem):
    return pltpu.CompilerParams(dimension_semantics=sem, vmem_limit_bytes=VMEM_LIMIT)


def _dot(a, b):
    return jnp.dot(a, b, preferred_element_type=F32)


def _dot_nt(a, b):
    return lax.dot_general(a, b, NT, preferred_element_type=F32)


def _dot_tn(a, b):
    return lax.dot_general(a, b, TN, preferred_element_type=F32)


def _rms(x, g):
    r = lax.rsqrt(jnp.mean(x * x, axis=-1, keepdims=True) + EPS)
    return x * r * g, r


def _rms_bwd(dy, x, r, g):
    xh = x * r
    dyg = dy * g
    dx = r * (dyg - xh * jnp.mean(dyg * xh, axis=-1, keepdims=True))
    return dx, dy * xh


def _sigmoid(z):
    return 1.0 / (1.0 + jnp.exp(-z))


def _rope(b, c, sa, sb):
    return b * c + pltpu.roll(b, 96, 1) * sa + pltpu.roll(b, 32, 1) * sb


def _rope_bwd(d, c, sa, sb):
    return d * c + pltpu.roll(d * sa, 32, 1) + pltpu.roll(d * sb, 96, 1)


def _group_mean(x, gmat):
    hi = x.astype(BF16)
    lo = (x - hi.astype(F32)).astype(BF16)
    return _dot(hi, gmat) + _dot(lo, gmat)


def _row_of(col, rows):
    return jnp.transpose(jnp.broadcast_to(col, (rows, 128)))[0:1, :]


def _rope_tables(n_pos):
    half = ROPE // 2
    inv_freq = (np.float32(1.0) / (np.float32(ROPE_THETA) ** (np.arange(half, dtype=np.float32) / np.float32(half))))
    ang = np.arange(n_pos, dtype=np.float32)[:, None] * inv_freq.astype(np.float32)[None, :]
    cos, sin = np.cos(ang).astype(np.float32), np.sin(ang).astype(np.float32)
    z = np.zeros((n_pos, half), np.float32)
    c = np.concatenate([cos, cos, z, z], axis=1)
    sa = np.concatenate([-sin, z, z, z], axis=1)
    sb = np.concatenate([z, sin, z, z], axis=1)
    return jnp.asarray(c), jnp.asarray(sa), jnp.asarray(sb)


W_IN_PIECES_1 = ((0, 80, 752, 0, 64, 0), (384, 64, 752, 384, 448, 1), (448, 16, 752, 512, 512, 1))
W_IN_PIECES_2 = ((80, 304, 752, 80, 144, 0), (464, 288, 752, 528, 528, 1))
W_Q_PIECES = ((0, 96, 256, 0, 0, 0), (96, 96, 256, 96, 96, 1))
W_KV_PIECES = ((0, 128, 128, 0, 0, 0), (128, 128, 128, 512, 512, 1))
W_OUT_PIECES = ((0, 128, 256, 0, 0, 0), (128, 128, 256, 128, 128, 1))


class _StagedGather:
    STAGES = 4

    @staticmethod
    def steps(n_steps):
        return (0, 5 * n_steps // 8, 7 * n_steps // 8, n_steps - 1)

    def __init__(self, pieces, zero_rows=None):
        self.pieces = pieces
        self.zero_rows = zero_rows

    def scratch(self):
        nk, dma = len(self.pieces), pltpu.SemaphoreType.DMA
        return [dma((nk, 3)), dma((nk, 3)), dma((nk, 3)), dma((nk, 3)), dma((nk,))]

    def vmem_scratch(self, shard_shape, out_shape):
        return [pltpu.VMEM(shard_shape, BF16), pltpu.VMEM(out_shape, BF16),
                pltpu.SemaphoreType.DMA((4 * len(self.pieces) + 2,))] + self.scratch()

    def run_vmem(self, stage, shard_ref, out_ref, scr):
        src_scr, land_scr, io_sems = scr[:3]
        spans = []
        for _, nr, per, first, rest, _ in self.pieces:
            spans += [(per * q + (first if q == 0 else rest), nr) for q in range(4)]
        if self.zero_rows is not None:
            spans.append(self.zero_rows)
        flush = [pltpu.make_async_copy(land_scr.at[r0:r0 + nr], out_ref.at[r0:r0 + nr], io_sems.at[n])
                 for n, (r0, nr) in enumerate(spans)]
        if stage == 0:
            load = pltpu.make_async_copy(shard_ref, src_scr, io_sems.at[len(spans)])
            load.start()
            if self.zero_rows is not None:
                r0, nr = self.zero_rows
                land_scr[r0:r0 + nr, :] = jnp.zeros((nr, land_scr.shape[1]), BF16)
            load.wait()
        if stage < self.STAGES:
            self.run(stage, src_scr, land_scr, scr[3:])
        for cp in flush:
            if stage == self.STAGES - 1:
                cp.start()
            if stage == self.STAGES:
                cp.wait()

    def run(self, stage, src_ref, out_ref, scr):
        send_sems, recv_sems, fwd_send, fwd_recv, loc_sems = scr
        pieces = self.pieces
        nk = len(pieces)
        x, y, c = lax.axis_index("x"), lax.axis_index("y"), lax.axis_index("c")
        mine = 2 * x + y
        chips = [(1 - x, y), (x, 1 - y), (1 - x, 1 - y)]
        chip_of = [2 * px + py for px, py in chips]
        mesh = pl.DeviceIdType.MESH

        def src(k):
            s0, nr = pieces[k][0], pieces[k][1]
            return src_ref.at[s0:s0 + nr]

        def dst(k, q):
            _, nr, per, first, rest, _ = pieces[k]
            row = per * q + first + (rest - first) * jnp.minimum(q, 1)
            return out_ref.at[pl.ds(pl.multiple_of(row, 16), nr)]

        def ici(k, j, q):
            px, py = chips[j]
            return pltpu.make_async_remote_copy(
                src_ref=src(k), dst_ref=dst(k, q), send_sem=send_sems.at[k, j], recv_sem=recv_sems.at[k, j],
                device_id=(px, py, c), device_id_type=mesh)

        def fwd(k, j):
            ref = dst(k, chip_of[j])
            return pltpu.make_async_remote_copy(
                src_ref=ref, dst_ref=ref, send_sem=fwd_send.at[k, j], recv_sem=fwd_recv.at[k, j],
                device_id=(x, y, 1 - c), device_id_type=mesh)

        def relay(k, half):
            ref = dst(k, chip_of[half])
            px, py = chips[1 - half]
            return pltpu.make_async_remote_copy(
                src_ref=ref, dst_ref=ref, send_sem=send_sems.at[k, 2], recv_sem=recv_sems.at[k, 2],
                device_id=(px, py, c), device_id_type=mesh)

        local = [pltpu.make_async_copy(src(k), dst(k, mine), loc_sems.at[k]) for k in range(nk)]
        if stage == 0:
            for cp in local:
                cp.start()
        if stage == 3:
            for cp in local:
                cp.wait()
        for half in (0, 1):
            @pl.when(c == half)
            def _(half=half):
                my_k = [k for k in range(nk) if pieces[k][5] == half]
                other_k = [k for k in range(nk) if pieces[k][5] != half]
                for k in my_k:
                    if stage == 0:
                        for j in range(2):
                            ici(k, j, mine).start()
                    elif stage == 1:
                        for j in (half, 1 - half):
                            ici(k, j, chip_of[j]).wait_recv()
                            if j == half:
                                relay(k, half).start()
                            fwd(k, j).start()
                    elif stage == 2:
                        ici(k, 2, chip_of[2]).wait_recv()
                        fwd(k, 2).start()
                    else:
                        for j in range(2):
                            ici(k, j, mine).wait_send()
                        relay(k, half).wait_send()
                        for j in range(3):
                            fwd(k, j).wait_send()
                if stage == 3:
                    for k in other_k:
                        for j in range(3):
                            fwd(k, j).wait_recv()


def _fwd_proj(x2d, meta, tabs, tabs_m, norm_g, w_head, q_norm_g, wq_p, kv_norm_g, wkv_p, w_out_shard, w_in_shard,
              nb, s, tm):
    nt = s // tm
    n = nb * nt
    n_steps = n + 1
    c_t, sa_t, sb_t = tabs
    cm_t, sam_t, sbm_t = tabs_m
    gat = _StagedGather(W_OUT_PIECES)
    gat_in = _StagedGather(W_IN_PIECES_1)
    n_sems = len(gat.scratch())
    assert n_steps >= 3

    def body(x_ref, c_ref, sa_ref, sb_ref, mt_ref, cm_ref, sam_ref, sbm_ref,
             g_ref, w_ref, gq_ref, wq_ref, gkv_ref, wkv_ref, wos_ref, wis_ref,
             p_ref, q_ref, k_ref, v_ref, pm_ref, km_ref, vm_ref, wo_ref, wi_ref, *scr):
        gat_scr, gat_in_scr = scr[:n_sems], scr[n_sems:]
        i = pl.program_id(0)
        for stage, at in enumerate(_StagedGather.steps(n_steps)):
            @pl.when(i == at)
            def _(stage=stage):
                gat_in.run_vmem(stage, wis_ref, wi_ref, gat_in_scr)
                gat.run(stage, wos_ref, wo_ref, gat_scr)

        def project(xv, c, sa, sb, p_out, q_out, k_out, v_out):
            u, _ = _rms(xv, g_ref[...])
            p = _dot_nt(u.astype(BF16), w_ref[...])
            p_out[...] = p
            qn, _ = _rms(p[:, 0:Q_RANK], gq_ref[...])
            q = _dot_nt(qn.astype(BF16), wq_ref[...])
            kvn, _ = _rms(p[:, Q_RANK:Q_RANK + KV_RANK], gkv_ref[...])
            kv = _dot_nt(kvn.astype(BF16), wkv_ref[...])
            kpe = _rope(p[:, 384:512], c, sa, sb)
            for h in range(HEADS):
                if q_out is not None:
                    pe = _rope(q[:, QK_PAD * h + NOPE:QK_PAD * (h + 1)], c, sa, sb)
                    qh = jnp.concatenate([q[:, QK_PAD * h:QK_PAD * h + NOPE], pe], axis=1)
                    q_out[0, h] = (qh * ATTN_SCALE).astype(BF16)
                k_out[0, h] = jnp.concatenate([kv[:, NOPE * h:NOPE * (h + 1)], kpe], axis=1).astype(BF16)
                v_out[0, h] = kv[:, 512 + VDIM * h:512 + VDIM * (h + 1)].astype(BF16)

        @pl.when(i < n)
        def _():
            project(x_ref[...], c_ref[...], sa_ref[...], sb_ref[...], p_ref, q_ref, k_ref, v_ref)

        @pl.when(i == n)
        def _():
            project(mt_ref[...], cm_ref[...], sam_ref[...], sbm_ref[...], pm_ref, None, km_ref, vm_ref)
            gat_in.run_vmem(gat_in.STAGES, wis_ref, wi_ref, gat_in_scr)

    cl = lambda i: jnp.minimum(i, n - 1)
    full = lambda a: pl.BlockSpec(a.shape, lambda i: (0,) * a.ndim)
    const = lambda shape: pl.BlockSpec(shape, lambda i: (0,) * len(shape))
    tab = pl.BlockSpec((tm, 128), lambda i: (cl(i) % nt, 0))
    hb = lambda w: pl.BlockSpec((1, HEADS, tm, w), lambda i: (cl(i) // nt, 0, cl(i) % nt, 0))
    whole = pl.BlockSpec(memory_space=pl.ANY)
    return pl.pallas_call(
        body, name="fwd_proj", grid=(n_steps,),
        in_specs=[pl.BlockSpec((tm, D_MODEL), lambda i: (cl(i), 0)), tab, tab, tab,
                  full(meta), full(cm_t), full(sam_t), full(sbm_t),
                  full(norm_g), full(w_head), full(q_norm_g), full(wq_p), full(kv_norm_g), full(wkv_p), whole, whole],
        out_specs=[pl.BlockSpec((tm, IN_HEAD), lambda i: (cl(i), 0)), hb(QK_PAD), hb(QK_PAD), hb(VDIM),
                   const((N_META, IN_HEAD)), const((1, HEADS, N_META, QK_PAD)), const((1, HEADS, N_META, VDIM)),
                   whole, whole],
        out_shape=[jax.ShapeDtypeStruct((nb * s, IN_HEAD), F32),
                   jax.ShapeDtypeStruct((nb, HEADS, s, QK_PAD), BF16),
                   jax.ShapeDtypeStruct((nb, HEADS, s, QK_PAD), BF16),
                   jax.ShapeDtypeStruct((nb, HEADS, s, VDIM), BF16),
                   jax.ShapeDtypeStruct((N_META, IN_HEAD), F32),
                   jax.ShapeDtypeStruct((1, HEADS, N_META, QK_PAD), BF16),
                   jax.ShapeDtypeStruct((1, HEADS, N_META, VDIM), BF16),
                   jax.ShapeDtypeStruct((D_MODEL, D_MODEL), BF16),
                   jax.ShapeDtypeStruct((IN_PAD, D_MODEL), BF16)],
        scratch_shapes=gat.scratch() + gat_in.vmem_scratch(w_in_shard.shape, (IN_PAD, D_MODEL)),
        compiler_params=_cparams("arbitrary"),
    )(x2d, c_t, sa_t, sb_t, meta, cm_t, sam_t, sbm_t, norm_g, w_head, q_norm_g, wq_p, kv_norm_g, wkv_p, w_out_shard,
      w_in_shard)


def _attn_fwd(q, k, v, km, vm, w_in_shard, w_in_part, nb, s, tq):
    nq = s // tq
    n_steps = nb * HEADS
    gat = _StagedGather(W_IN_PIECES_2, zero_rows=(HEAD_ROWS, IN_HEAD - HEAD_ROWS))
    assert n_steps >= 3

    def body(q_ref, k_ref, v_ref, km_ref, vm_ref, ws_ref, _, o_ref, lse_ref, w_ref, s_scr, p_scr, *gat_scr):
        step = pl.program_id(0) * HEADS + pl.program_id(1)
        for stage, at in enumerate(_StagedGather.steps(n_steps)):
            @pl.when(step == at)
            def _(stage=stage):
                gat.run_vmem(stage, ws_ref, w_ref, gat_scr)

        row = lax.broadcasted_iota(jnp.int32, (tq, tq), 0)
        col = lax.broadcasted_iota(jnp.int32, (tq, tq), 1)
        def scores(i):
            slot = i % 2
            qi = q_ref[0, 0, i * tq:(i + 1) * tq, :]
            sm = _dot_nt(qi, km_ref[0, 0])
            m128 = None
            for j in range(i + 1):
                sc = _dot_nt(qi, k_ref[0, 0, j * tq:(j + 1) * tq, :])
                if j == i:
                    sc = jnp.where(col <= row, sc, NEG_INF)
                s_scr[slot, :, j * tq:(j + 1) * tq] = sc
                mx = sc[:, 0:128]
                for c0 in range(128, tq, 128):
                    mx = jnp.maximum(mx, sc[:, c0:c0 + 128])
                m128 = mx if m128 is None else jnp.maximum(m128, mx)
            return sm, jnp.maximum(jnp.max(m128, axis=1, keepdims=True), jnp.max(sm, axis=1, keepdims=True))

        def weighted_sum(i, pm, l):
            n = (i + 1) * tq
            acc = _dot(p_scr[i % 2, :, 0:n], v_ref[0, 0, 0:n, :]) + _dot(pm.astype(BF16), vm_ref[0, 0])
            o_ref[0, 0, i * tq:(i + 1) * tq, :] = acc / l

        nxt, pending = scores(0), None
        for i in range(nq):
            slot = i % 2
            sm, m = nxt
            if i + 1 < nq:
                nxt = scores(i + 1)
            pm = jnp.exp(sm - m)
            l128 = None
            for j in range(i + 1):
                p = jnp.exp(s_scr[slot, :, j * tq:(j + 1) * tq] - m)
                p_scr[slot, :, j * tq:(j + 1) * tq] = p.astype(BF16)
                ps = p[:, 0:128]
                for c0 in range(128, tq, 128):
                    ps = ps + p[:, c0:c0 + 128]
                l128 = ps if l128 is None else l128 + ps
            l = jnp.sum(l128, axis=1, keepdims=True) + jnp.sum(pm, axis=1, keepdims=True)
            lse_ref[0, 0, :, i * tq:(i + 1) * tq] = _row_of(m + jnp.log(l), tq)
            if pending is not None:
                weighted_sum(*pending)
            pending = (i, pm, l)
        weighted_sum(*pending)

        @pl.when(step == n_steps - 1)
        def _():
            gat.run_vmem(gat.STAGES, ws_ref, w_ref, gat_scr)

    hblk = lambda w: pl.BlockSpec((1, 1, s, w), lambda b, h: (b, h, 0, 0))
    mblk = lambda w: pl.BlockSpec((1, 1, N_META, w), lambda b, h: (0, h, 0, 0))
    whole = pl.BlockSpec(memory_space=pl.ANY)
    return pl.pallas_call(
        body, name="attn_fwd", grid=(nb, HEADS),
        in_specs=[hblk(QK_PAD), hblk(QK_PAD), hblk(VDIM), mblk(QK_PAD), mblk(VDIM), whole, whole],
        out_specs=[hblk(VDIM), pl.BlockSpec((1, 1, 1, s), lambda b, h: (b, h, 0, 0)), whole],
        out_shape=[jax.ShapeDtypeStruct((nb, HEADS, s, VDIM), F32),
                   jax.ShapeDtypeStruct((nb, HEADS, 1, s), F32),
                   jax.ShapeDtypeStruct(w_in_part.shape, BF16)],
        input_output_aliases={6: 2},
        scratch_shapes=[pltpu.VMEM((2, tq, s), F32), pltpu.VMEM((2, tq, s), BF16)]
        + gat.vmem_scratch(w_in_shard.shape, w_in_part.shape),
        compiler_params=_cparams("arbitrary", "arbitrary"),
    )(q, k, v, km, vm, w_in_shard, w_in_part)


def _shift_rows(a, prev, n_rows):
    rid = lax.broadcasted_iota(jnp.int32, a.shape, 0)
    a1 = jnp.where(rid == 0, prev[7:8, :], pltpu.roll(a, 1, 0))
    a2 = jnp.where(rid == 0, prev[6:7, :], jnp.where(rid == 1, prev[7:8, :], pltpu.roll(a, 2, 0)))
    return a1, a2


def _attn_gate(o, za, ga_h):
    on, r = _rms(o, ga_h)
    return on * (za * _sigmoid(za)), on, r


def _out_fwd_bwd(x2d, tgt2d, o, meta, norm_g, w_in_p, conv_w, ga, gc, gmat, w_out, gf, nb, s, tm):
    nt = s // tm
    r = nb * s

    def body(x_ref, t_ref, o_ref, mt_ref, g_ref, wi_ref, cw_ref, ga_ref, gc_ref, gm_ref, w_ref, gf_ref,
             dh_ref, dy_ref, dw_ref, dgf_ref, loss_ref, p_ref, pm_ref, last_cc):
        i = pl.program_id(0)
        blk = lambda ref, j, rows=slice(None): ref[rows, 512 * j:512 * (j + 1)]

        def tail(xv):
            u, _ = _rms(xv, g_ref[...])
            return _dot_nt(u.astype(BF16), wi_ref[IN_HEAD:IN_PAD, :])

        @pl.when(i == 0)
        def _():
            dw_ref[...] = jnp.zeros_like(dw_ref)
            dgf_ref[...] = jnp.zeros_like(dgf_ref)
            loss_ref[...] = jnp.zeros_like(loss_ref)
            last_cc[...] = jnp.zeros_like(last_cc)
            pm_ref[...] = tail(mt_ref[...])

        u16 = _rms(x_ref[...], g_ref[...])[0].astype(BF16)

        def project(j):
            p_ref[:, 512 * j:512 * (j + 1)] = _dot_nt(u16, wi_ref[IN_HEAD + 512 * j:IN_HEAD + 512 * (j + 1), :])

        project(BLK_ZA)
        project(BLK_CC)
        project(BLK_CH)
        ya = []
        for h in range(HEADS):
            y, _, _ = _attn_gate(o_ref[0, h], p_ref[:, 512 * BLK_ZA + VDIM * h:512 * BLK_ZA + VDIM * (h + 1)],
                                 ga_ref[:, VDIM * h:VDIM * (h + 1)])
            ya.append(y)
        project(BLK_CB)
        project(BLK_ZC)
        cc = blk(p_ref, BLK_CC) * blk(p_ref, BLK_CH)
        meta_cc = blk(pm_ref, BLK_CC, slice(8, 16)) * blk(pm_ref, BLK_CH, slice(8, 16))
        prev = jnp.where(i % nt == 0, meta_cc, last_cc[...])
        last_cc[...] = cc[tm - 8:tm, :]
        cc1, cc2 = _shift_rows(cc, prev, tm)
        yc = blk(p_ref, BLK_CB) * (cw_ref[0:1, :] * cc2 + cw_ref[1:2, :] * cc1 + cw_ref[2:3, :] * cc)
        rg = lax.rsqrt(_group_mean(yc * yc, gm_ref[...]) + EPS)
        zc = blk(p_ref, BLK_ZC)
        yconv = yc * rg * gc_ref[...] * (zc * _sigmoid(zc))
        ycat = jnp.concatenate(ya + [yconv], axis=1).astype(BF16)
        h2 = x_ref[...] + _dot(ycat, w_ref[...])
        gfv = gf_ref[...]
        y, r2 = _rms(h2, gfv)
        e = y - t_ref[...]
        loss_ref[...] += 0.5 * jnp.sum(e * e) / D_MODEL
        dyv = e * (1.0 / D_MODEL)
        dh2, dgf = _rms_bwd(dyv, h2, r2, gfv)
        dgf_ref[...] += jnp.sum(dgf, axis=0, keepdims=True)
        dh_ref[...] = dh2
        dhb = dh2.astype(BF16)
        dy_ref[...] = _dot_nt(dhb, w_ref[...])
        dw_ref[...] += _dot_tn(ycat, dhb)

    row = lambda w: pl.BlockSpec((tm, w), lambda i: (i, 0))
    const = lambda shape: pl.BlockSpec(shape, lambda i: (0,) * len(shape))
    full = lambda a: const(a.shape)
    return pl.pallas_call(
        body, name="out_fwd_bwd", grid=(nb * nt,),
        in_specs=[row(D_MODEL), row(D_MODEL),
                  pl.BlockSpec((1, HEADS, tm, VDIM), lambda i: (i // nt, 0, i % nt, 0)),
                  full(meta), full(norm_g), full(w_in_p),
                  full(conv_w), full(ga), full(gc), full(gmat), full(w_out), full(gf)],
        out_specs=[row(D_MODEL), row(D_MODEL), const((D_MODEL, D_MODEL)), const((1, D_MODEL)), const((1, 128)),
                   row(IN_TAIL), const((N_META, IN_TAIL))],
        out_shape=[jax.ShapeDtypeStruct((r, D_MODEL), F32), jax.ShapeDtypeStruct((r, D_MODEL), F32),
                   jax.ShapeDtypeStruct((D_MODEL, D_MODEL), F32), jax.ShapeDtypeStruct((1, D_MODEL), F32),
                   jax.ShapeDtypeStruct((1, 128), F32),
                   jax.ShapeDtypeStruct((r, IN_TAIL), F32), jax.ShapeDtypeStruct((N_META, IN_TAIL), F32)],
        scratch_shapes=[pltpu.VMEM((8, 512), F32)],
        compiler_params=_cparams("arbitrary"),
    )(x2d, tgt2d, o, meta, norm_g, w_in_p, conv_w, ga, gc, gmat, w_out, gf)


def _gate_bwd(dycat, o, p, pm, conv_w, ga, gc, gmat, nb, s, tm):
    nt = s // tm
    r = nb * s
    ext = tm + 8
    prev_idx = lambda i: jnp.maximum(i * (tm // 8) - 1, 0)
    next_idx = lambda i: jnp.minimum((i + 1) * (tm // 8), r // 8 - 1)

    def body(dya_ref, dyc_ref, dycn_ref, o_ref, za_ref, cb_ref, cbn_ref, cc_ref, ccp_ref, ccn_ref,
             ch_ref, chp_ref, chn_ref, zc_ref, zcn_ref, mc_ref, mh_ref, cw_ref, ga_ref, gc_ref, gm_ref,
             dpb_ref, do_ref, dl_ref, dccm_ref, dga_ref, dgc_ref, dcw_ref):
        i = pl.program_id(0)

        @pl.when(i == 0)
        def _():
            dga_ref[...] = jnp.zeros_like(dga_ref)
            dgc_ref[...] = jnp.zeros_like(dgc_ref)
            dcw_ref[...] = jnp.zeros_like(dcw_ref)

        dga = []
        for h in range(HEADS):
            hs = slice(VDIM * h, VDIM * (h + 1))
            oh, za, gah, dya = o_ref[0, h], za_ref[:, hs], ga_ref[:, hs], dya_ref[:, hs]
            sg = _sigmoid(za)
            on, ro = _rms(oh, gah)
            don = dya * (za * sg)
            dpb_ref[:, hs] = (dya * on * (sg * (1.0 + za * (1.0 - sg)))).astype(BF16)
            do, dg = _rms_bwd(don, oh, ro, gah)
            dga.append(jnp.sum(dg, axis=0, keepdims=True))
            dob = do.astype(BF16)
            do_ref[0, h] = dob
            dl_ref[0, h] = _row_of(jnp.sum(dob.astype(F32) * oh, axis=1, keepdims=True), tm)
        dga_ref[...] += jnp.concatenate(dga, axis=1)

        cat = lambda a, b: jnp.concatenate([a[...], b[...]], axis=0)
        cch = cat(cc_ref, ccn_ref)
        chh = cat(ch_ref, chn_ref)
        cb = cat(cb_ref, cbn_ref)
        zc = cat(zc_ref, zcn_ref)
        dy = cat(dyc_ref, dycn_ref)
        first = i % nt == 0
        last = i % nt == nt - 1
        cc = cch * chh
        prev = jnp.where(first, mc_ref[8:16, :] * mh_ref[8:16, :], ccp_ref[...] * chp_ref[...])
        cc1, cc2 = _shift_rows(cc, prev, ext)
        w0, w1, w2 = cw_ref[0:1, :], cw_ref[1:2, :], cw_ref[2:3, :]
        dw = w0 * cc2 + w1 * cc1 + w2 * cc
        yc = cb * dw
        rg = lax.rsqrt(_group_mean(yc * yc, gm_ref[...]) + EPS)
        ych = yc * rg
        gcv = gc_ref[...]
        sg = _sigmoid(zc)
        dycn = dy * (zc * sg)
        dzc = dy * (ych * gcv) * (sg * (1.0 + zc * (1.0 - sg)))
        dgc_ref[...] += jnp.sum((dycn * ych)[:tm], axis=0, keepdims=True)
        dycg = dycn * gcv
        dyc = rg * (dycg - ych * _group_mean(dycg * ych, gm_ref[...]))
        rid = lax.broadcasted_iota(jnp.int32, (ext, CONV_W), 0)
        ddw = jnp.where(jnp.logical_and(last, rid >= tm), 0.0, dyc * cb)
        dcb = dyc * dw
        dcc = w2 * ddw + w1 * pltpu.roll(ddw, ext - 1, 0) + w0 * pltpu.roll(ddw, ext - 2, 0)
        dpb_ref[:, 512:1024] = dcb[:tm].astype(BF16)
        dpb_ref[:, 1024:1536] = (dcc * chh)[:tm].astype(BF16)
        dpb_ref[:, 1536:2048] = (dcc * cch)[:tm].astype(BF16)
        dpb_ref[:, 2048:2560] = dzc[:tm].astype(BF16)
        rs = lambda a: jnp.sum(a[:tm], axis=0, keepdims=True)
        dcw_ref[0:1, :] += rs(ddw * cc2)
        dcw_ref[1:2, :] += rs(ddw * cc1)
        dcw_ref[2:3, :] += rs(ddw * cc)

        @pl.when(first)
        def _():
            d0, d1 = ddw[0:1, :], ddw[1:2, :]
            r8 = lax.broadcasted_iota(jnp.int32, (8, CONV_W), 0)
            dccm_ref[0] = jnp.where(r8 == 7, w1 * d0 + w0 * d1, jnp.where(r8 == 6, w0 * d0, 0.0))

    row = lambda j: pl.BlockSpec((tm, 512), lambda i: (i, j))
    prv = lambda j: pl.BlockSpec((8, 512), lambda i: (prev_idx(i), j))
    nxt = lambda j: pl.BlockSpec((8, 512), lambda i: (next_idx(i), j))
    mblk = lambda j: pl.BlockSpec((N_META, 512), lambda i: (0, j))
    full = lambda a: pl.BlockSpec(a.shape, lambda i: (0,) * a.ndim)
    hb = lambda w: pl.BlockSpec((1, HEADS, tm, w), lambda i: (i // nt, 0, i % nt, 0))
    acc = lambda rr: pl.BlockSpec((rr, 512), lambda i: (0, 0))
    return pl.pallas_call(
        body, name="gate_bwd", grid=(nb * nt,),
        in_specs=[row(0), row(1), nxt(1), hb(VDIM),
                  row(BLK_ZA), row(BLK_CB), nxt(BLK_CB), row(BLK_CC), prv(BLK_CC), nxt(BLK_CC),
                  row(BLK_CH), prv(BLK_CH), nxt(BLK_CH), row(BLK_ZC), nxt(BLK_ZC),
                  mblk(BLK_CC), mblk(BLK_CH), full(conv_w), full(ga), full(gc), full(gmat)],
        out_specs=[pl.BlockSpec((tm, 2560), lambda i: (i, 0)), hb(VDIM),
                   pl.BlockSpec((1, HEADS, 1, tm), lambda i: (i // nt, 0, 0, i % nt)),
                   pl.BlockSpec((1, 8, 512), lambda i: (i // nt, 0, 0)),
                   acc(1), acc(1), acc(8)],
        out_shape=[jax.ShapeDtypeStruct((r, 2560), BF16), jax.ShapeDtypeStruct((nb, HEADS, s, VDIM), BF16),
                   jax.ShapeDtypeStruct((nb, HEADS, 1, s), F32), jax.ShapeDtypeStruct((nb, 8, 512), F32),
                   jax.ShapeDtypeStruct((1, 512), F32), jax.ShapeDtypeStruct((1, 512), F32),
                   jax.ShapeDtypeStruct((8, 512), F32)],
        compiler_params=_cparams("arbitrary"),
    )(dycat, dycat, dycat, o, p, p, p, p, p, p, p, p, p, p, p, pm, pm, conv_w, ga, gc, gmat)


class _StagedReduce:
    LOC, PRE_S, PRE_R, ICI_S, ICI_R, POST_S, POST_R, OUT, N_SEM = 0, 1, 2, 3, 6, 9, 10, 11, 12

    def __init__(self, shard_shape):
        self.half = (shard_shape[0] // 2, shard_shape[1])

    def scratch(self):
        h = self.half
        return [pltpu.VMEM((4,) + h, F32), pltpu.VMEM((4,) + h, F32), pltpu.VMEM((4,) + h, BF16),
                pltpu.VMEM((3,) + h, BF16), pltpu.VMEM(h, F32), pltpu.SemaphoreType.DMA((self.N_SEM,))]

    def run(self, stage, pin, gout, scr):
        own, sib, wire, rbuf, fin, sems = scr
        r2 = self.half[0]
        x, y, c = lax.axis_index("x"), lax.axis_index("y"), lax.axis_index("c")
        mine = 2 * x + y
        sibling = (x, y, 1 - c)
        chips = [(1 - x, y), (x, 1 - y), (1 - x, 1 - y)]
        rows = lambda half: pl.ds(pl.multiple_of(half * r2, r2), r2)
        mesh = pl.DeviceIdType.MESH

        loc = pltpu.make_async_copy(pin.at[:, rows(c), :], own, sems.at[self.LOC])
        pre = pltpu.make_async_remote_copy(
            src_ref=pin.at[:, rows(1 - c), :], dst_ref=sib, send_sem=sems.at[self.PRE_S],
            recv_sem=sems.at[self.PRE_R], device_id=sibling, device_id_type=mesh)

        def ici(j):
            px, py = chips[j]
            return pltpu.make_async_remote_copy(
                src_ref=wire.at[2 * px + py], dst_ref=rbuf.at[j], send_sem=sems.at[self.ICI_S + j],
                recv_sem=sems.at[self.ICI_R + j], device_id=(px, py, c), device_id_type=mesh)

        def post(half):
            return pltpu.make_async_remote_copy(
                src_ref=fin, dst_ref=gout.at[rows(half), :], send_sem=sems.at[self.POST_S],
                recv_sem=sems.at[self.POST_R], device_id=sibling, device_id_type=mesh)

        keep = pltpu.make_async_copy(fin, gout.at[rows(c), :], sems.at[self.OUT])
        if stage == 0:
            loc.start()
            pre.start()
        elif stage == 1:
            loc.wait()
            pre.wait_recv()
            for blk in range(4):
                tot = own[blk] + sib[blk]
                own[blk] = tot
                wire[blk] = tot.astype(BF16)
            for j in range(3):
                ici(j).start()
        elif stage == 2:
            for j in range(3):
                ici(j).wait_recv()
            tot = own[mine]
            for j in range(3):
                tot = tot + rbuf[j].astype(F32)
            fin[...] = tot
            post(c).start()
            keep.start()
        else:
            post(1 - c).wait_recv()
            pre.wait_send()
            for j in range(3):
                ici(j).wait_send()
            post(c).wait_send()
            keep.wait()


def _attn_bwd(q, k, v, do, lse, delta, km, vm, early, nb, s, t):
    n = s // t
    ne = len(early)
    reds = [_StagedReduce(a.shape[1:]) for a in early]
    n_steps = HEADS * nb
    assert n_steps >= 4

    def body(q_ref, k_ref, v_ref, do_ref, lse_ref, dl_ref, km_ref, vm_ref, *rest):
        pin_refs, rest = rest[:ne], rest[ne:]
        dq_ref, dk_ref, dv_ref, dkm_ref, dvm_ref = rest[:5]
        gout_refs, (p_scr, ds_scr, dq_acc), red_scr = rest[5:5 + ne], rest[5 + ne:8 + ne], rest[8 + ne:]
        b = pl.program_id(1)
        step = pl.program_id(0) * nb + b
        for stage, at in enumerate((0, 1, n_steps - 2, n_steps - 1)):
            @pl.when(step == at)
            def _(stage=stage):
                for a, red in enumerate(reds):
                    red.run(stage, pin_refs[a], gout_refs[a], red_scr[6 * a:6 * a + 6])

        @pl.when(b == 0)
        def _():
            dkm_ref[...] = jnp.zeros_like(dkm_ref)
            dvm_ref[...] = jnp.zeros_like(dvm_ref)

        kr = lax.broadcasted_iota(jnp.int32, (t, t), 0)
        qc = lax.broadcasted_iota(jnp.int32, (t, t), 1)
        km_v, vm_v = km_ref[0, 0], vm_ref[0, 0]
        ptm = jnp.exp(_dot_nt(km_v, q_ref[0, 0]) - lse_ref[0, 0])
        dstm = (ptm * (_dot_nt(vm_v, do_ref[0, 0]) - dl_ref[0, 0])).astype(BF16)
        dkm_ref[0] += _dot(dstm, q_ref[0, 0])
        dvm_ref[0] += _dot(ptm.astype(BF16), do_ref[0, 0])
        dq_acc[...] = _dot_tn(dstm, km_v)
        def tiles(j):
            slot = j % 2
            kj = k_ref[0, 0, j * t:(j + 1) * t, :]
            vj = v_ref[0, 0, j * t:(j + 1) * t, :]
            def products(i):
                cs = slice(i * t, (i + 1) * t)
                return _dot_nt(kj, q_ref[0, 0, cs, :]), _dot_nt(vj, do_ref[0, 0, cs, :])

            nxt, pending = products(j), None
            for i in range(j, n):
                cs = slice(i * t, (i + 1) * t)
                st, dpt = nxt
                if i + 1 < n:
                    nxt = products(i + 1)
                if i == j:
                    st = jnp.where(kr <= qc, st, NEG_INF)
                pt = jnp.exp(st - lse_ref[0, 0, :, cs])
                dst = (pt * (dpt - dl_ref[0, 0, :, cs])).astype(BF16)
                p_scr[slot, :, cs] = pt.astype(BF16)
                ds_scr[slot, :, cs] = dst
                if pending is not None:
                    dq_acc[pending[0], :] += _dot_tn(pending[1], kj)
                pending = (cs, dst)
            dq_acc[pending[0], :] += _dot_tn(pending[1], kj)

        for j in range(n):
            slot = j % 2
            tiles(j)
            dv_ref[0, 0, j * t:(j + 1) * t, :] = _dot(p_scr[slot, :, j * t:s], do_ref[0, 0, j * t:s, :]).astype(BF16)
            dk_ref[0, 0, j * t:(j + 1) * t, :] = _dot(ds_scr[slot, :, j * t:s], q_ref[0, 0, j * t:s, :]).astype(BF16)
        dq_ref[0, 0] = dq_acc[...].astype(BF16)

    big = lambda w: pl.BlockSpec((1, 1, s, w), lambda h, b: (b, h, 0, 0))
    rowv = pl.BlockSpec((1, 1, 1, s), lambda h, b: (b, h, 0, 0))
    mk = lambda w: pl.BlockSpec((1, 1, N_META, w), lambda h, b: (0, h, 0, 0))
    mo = lambda w: pl.BlockSpec((1, N_META, w), lambda h, b: (h, 0, 0))
    return pl.pallas_call(
        body, name="attn_bwd", grid=(HEADS, nb),
        in_specs=[big(QK_PAD), big(QK_PAD), big(VDIM), big(VDIM), rowv, rowv, mk(QK_PAD), mk(VDIM)]
        + [pl.BlockSpec(memory_space=pl.ANY)] * ne,
        out_specs=[big(QK_PAD), big(QK_PAD), big(VDIM), mo(QK_PAD), mo(VDIM)]
        + [pl.BlockSpec(memory_space=pl.ANY)] * ne,
        out_shape=[jax.ShapeDtypeStruct((nb, HEADS, s, QK_PAD), BF16),
                   jax.ShapeDtypeStruct((nb, HEADS, s, QK_PAD), BF16),
                   jax.ShapeDtypeStruct((nb, HEADS, s, VDIM), BF16),
                   jax.ShapeDtypeStruct((HEADS, N_META, QK_PAD), F32),
                   jax.ShapeDtypeStruct((HEADS, N_META, VDIM), F32)]
        + [jax.ShapeDtypeStruct(a.shape[1:], F32) for a in early],
        scratch_shapes=[pltpu.VMEM((2, t, s), BF16), pltpu.VMEM((2, t, s), BF16), pltpu.VMEM((s, QK_PAD), F32)]
        + [sc for red in reds for sc in red.scratch()],
        compiler_params=_cparams("arbitrary", "arbitrary"),
    )(q, k, v, do, lse, delta, km, vm, *early)


def _up_bwd(dq, dk, dv, dkm, dvm, p, pm, tabs, tabs_m, wq_p, wkv_p, gq, gkv, nb, s, tm):
    nt = s // tm
    n = nb * nt
    c_t, sa_t, sb_t = tabs
    cm_t, sam_t, sbm_t = tabs_m

    def kv_path(dkh, dvh, pa, c, sa, sb, wkv, gkvv):
        dkpe = dkh[0][:, NOPE:]
        for h in range(1, HEADS):
            dkpe = dkpe + dkh[h][:, NOPE:]
        dkr = _rope_bwd(dkpe, c, sa, sb)
        dkv = jnp.concatenate([d[:, :NOPE] for d in dkh] + list(dvh), axis=1).astype(BF16)
        ckv = pa[:, Q_RANK:Q_RANK + KV_RANK]
        kvn, rkv = _rms(ckv, gkvv)
        dckv, dg = _rms_bwd(_dot(dkv, wkv), ckv, rkv, gkvv)
        return dckv, dkr, kvn.astype(BF16), dkv, jnp.sum(dg, axis=0, keepdims=True)

    def body(dq_ref, dk_ref, dv_ref, pa_ref, c_ref, sa_ref, sb_ref,
             dkm_ref, dvm_ref, pam_ref, cm_ref, sam_ref, sbm_ref,
             wq_ref, wkv_ref, gq_ref, gkv_ref,
             dpa_ref, dpam_ref, pq_ref, pkv_ref, dgq_ref, dgkv_ref, dwq_ref, dwkv_ref):
        i = pl.program_id(0)

        @pl.when(i == 0)
        def _():
            dwq_ref[...] = jnp.zeros_like(dwq_ref)
            dwkv_ref[...] = jnp.zeros_like(dwkv_ref)
            dgq_ref[...] = jnp.zeros_like(dgq_ref)
            dgkv_ref[...] = jnp.zeros_like(dgkv_ref)

        @pl.when(i < n)
        def _():
            c, sa, sb = c_ref[...], sa_ref[...], sb_ref[...]
            pa = pa_ref[...]
            parts = []
            for h in range(HEADS):
                dqh = dq_ref[0, h].astype(F32) * ATTN_SCALE
                parts += [dqh[:, :NOPE], _rope_bwd(dqh[:, NOPE:], c, sa, sb)]
            dql = jnp.concatenate(parts, axis=1).astype(BF16)
            cq = pa[:, 0:Q_RANK]
            gqv = gq_ref[...]
            qn, rq = _rms(cq, gqv)
            dwq_ref[...] += _dot_tn(dql, qn.astype(BF16))
            dcq, dg = _rms_bwd(_dot(dql, wq_ref[...]), cq, rq, gqv)
            dgq_ref[...] += jnp.sum(dg, axis=0, keepdims=True)
            dckv, dkr, kvn, dkv, dgk = kv_path([dk_ref[0, h].astype(F32) for h in range(HEADS)],
                                               [dv_ref[0, h].astype(F32) for h in range(HEADS)],
                                               pa, c, sa, sb, wkv_ref[...], gkv_ref[...])
            dwkv_ref[...] += _dot_tn(dkv, kvn)
            dgkv_ref[...] += dgk
            dpa_ref[...] = jnp.concatenate([dcq, dckv, dkr], axis=1).astype(BF16)

        @pl.when(i == n)
        def _():
            dckv, dkr, kvn, dkv, dgk = kv_path([dkm_ref[h] for h in range(HEADS)],
                                               [dvm_ref[h] for h in range(HEADS)],
                                               pam_ref[...], cm_ref[...], sam_ref[...], sbm_ref[...],
                                               wkv_ref[...], gkv_ref[...])
            dwkv_ref[...] += _dot_tn(dkv, kvn)
            dgkv_ref[...] += dgk
            dpam_ref[...] = jnp.concatenate([jnp.zeros((N_META, Q_RANK), F32), dckv, dkr], axis=1)
            for h in range(HEADS):
                pq_ref[h] = dwq_ref[QK_PAD * h:QK_PAD * h + NOPE + ROPE, :]
                pkv_ref[h, 0:NOPE, :] = dwkv_ref[NOPE * h:NOPE * (h + 1), :]
                pkv_ref[h, NOPE:NOPE + VDIM, :] = dwkv_ref[512 + VDIM * h:512 + VDIM * (h + 1), :]

    cl = lambda i: jnp.minimum(i, n - 1)
    hb = lambda w: pl.BlockSpec((1, HEADS, tm, w), lambda i: (cl(i) // nt, 0, cl(i) % nt, 0))
    tab = pl.BlockSpec((tm, 128), lambda i: (cl(i) % nt, 0))
    full = lambda a: pl.BlockSpec(a.shape, lambda i: (0,) * a.ndim)
    const = lambda shape: pl.BlockSpec(shape, lambda i: (0,) * len(shape))
    return pl.pallas_call(
        body, name="up_bwd", grid=(n + 1,),
        in_specs=[hb(QK_PAD), hb(QK_PAD), hb(VDIM), pl.BlockSpec((tm, 512), lambda i: (cl(i), 0)), tab, tab, tab,
                  full(dkm), full(dvm), pl.BlockSpec((N_META, 512), lambda i: (0, 0)),
                  full(cm_t), full(sam_t), full(sbm_t), full(wq_p), full(wkv_p), full(gq), full(gkv)],
        out_specs=[pl.BlockSpec((tm, 512), lambda i: (cl(i), 0)), const((N_META, 512)),
                   const((HEADS, NOPE + ROPE, Q_RANK)), const((HEADS, NOPE + VDIM, KV_RANK)),
                   const((1, Q_RANK)), const((1, KV_RANK))],
        out_shape=[jax.ShapeDtypeStruct((nb * s, 512), BF16), jax.ShapeDtypeStruct((N_META, 512), F32),
                   jax.ShapeDtypeStruct((HEADS, NOPE + ROPE, Q_RANK), F32),
                   jax.ShapeDtypeStruct((HEADS, NOPE + VDIM, KV_RANK), F32),
                   jax.ShapeDtypeStruct((1, Q_RANK), F32), jax.ShapeDtypeStruct((1, KV_RANK), F32)],
        scratch_shapes=[pltpu.VMEM((HEADS * QK_PAD, Q_RANK), F32), pltpu.VMEM((1024, KV_RANK), F32)],
        compiler_params=_cparams("arbitrary"),
    )(dq, dk, dv, p, c_t, sa_t, sb_t, dkm, dvm, pm, cm_t, sam_t, sbm_t, wq_p, wkv_p, gq, gkv)


def _in_bwd(x2d, dh2, dpa, dpb, meta, dpam, dccm, pm, w_in_p, norm_g, nb, s, tm):
    nt = s // tm
    n = nb * nt

    def body(x_ref, dh_ref, dpa_ref, dpb_ref, mt_ref, dpam_ref, dccm_ref, mc_ref, mh_ref, w_ref, g_ref,
             gx_ref, gm_ref, dw_hbm, dg_ref, acc_ref, sems):
        i = pl.program_id(0)

        @pl.when(i == 0)
        def _():
            acc_ref[...] = jnp.zeros_like(acc_ref)
            dg_ref[...] = jnp.zeros_like(dg_ref)

        def rows(x, dp, dres):
            g = g_ref[...]
            dpb16 = dp.astype(BF16)
            du = _dot(dpb16, w_ref[...])
            u, r1 = _rms(x, g)
            acc_ref[...] += _dot_tn(dpb16, u.astype(BF16))
            dx, dg = _rms_bwd(du, x, r1, g)
            dg_ref[...] += jnp.sum(dg, axis=0, keepdims=True)
            return dx if dres is None else dx + dres

        @pl.when(i < n)
        def _():
            dp = jnp.concatenate([dpa_ref[...], dpb_ref[...]], axis=1)
            gx_ref[...] = rows(x_ref[...], dp, dh_ref[...])

        @pl.when(i == n)
        def _():
            dcc = dccm_ref[0]
            for b in range(1, nb):
                dcc = dcc + dccm_ref[b]
            z8 = jnp.zeros((8, CONV_W), F32)
            dc = jnp.concatenate([z8, dcc * mh_ref[8:16, :]], axis=0)
            dh = jnp.concatenate([z8, dcc * mc_ref[8:16, :]], axis=0)
            z = jnp.zeros((N_META, CONV_W), F32)
            dp = jnp.concatenate([dpam_ref[...], z, z, dc, dh, z], axis=1)
            gm_ref[...] = rows(mt_ref[...], dp, None)
            per = IN_DIM // 4
            cps = [pltpu.make_async_copy(acc_ref.at[0:448], dw_hbm.at[0, 0:448], sems.at[0]),
                   pltpu.make_async_copy(acc_ref.at[512:per + 64], dw_hbm.at[0, 448:per], sems.at[1])]
            for qq in range(1, 4):
                cps.append(pltpu.make_async_copy(acc_ref.at[per * qq + 64:per * (qq + 1) + 64], dw_hbm.at[qq],
                                                 sems.at[qq + 1]))
            for cp in cps:
                cp.start()
            for cp in cps:
                cp.wait()

    cl = lambda i: jnp.minimum(i, n - 1)
    row = lambda w: pl.BlockSpec((tm, w), lambda i: (cl(i), 0))
    full = lambda a: pl.BlockSpec(a.shape, lambda i: (0,) * a.ndim)
    mblk = lambda j: pl.BlockSpec((N_META, 512), lambda i: (0, j))
    return pl.pallas_call(
        body, name="in_bwd", grid=(n + 1,),
        in_specs=[row(D_MODEL), row(D_MODEL), row(512), row(2560), full(meta), full(dpam), full(dccm),
                  mblk(BLK_CC), mblk(BLK_CH), full(w_in_p), full(norm_g)],
        out_specs=[row(D_MODEL), pl.BlockSpec((N_META, D_MODEL), lambda i: (0, 0)),
                   pl.BlockSpec(memory_space=pl.ANY), pl.BlockSpec((1, D_MODEL), lambda i: (0, 0))],
        out_shape=[jax.ShapeDtypeStruct((nb * s, D_MODEL), F32), jax.ShapeDtypeStruct((N_META, D_MODEL), F32),
                   jax.ShapeDtypeStruct((4, IN_DIM // 4, D_MODEL), F32), jax.ShapeDtypeStruct((1, D_MODEL), F32)],
        scratch_shapes=[pltpu.VMEM((IN_PAD, D_MODEL), F32), pltpu.SemaphoreType.DMA((5,))],
        compiler_params=_cparams("arbitrary"),
    )(x2d, dh2, dpa, dpb, meta, dpam, dccm, pm, pm, w_in_p, norm_g)


def _gather_weights(w_in_shard, split, pieces, out_rows, whole, zero_fills):
    ns, nw, nz = len(split), len(whole), len(zero_fills)
    flat = [(a, pc) for a in range(ns) for pc in pieces[a]]
    nk = len(flat)
    hh = HEAD_ROWS // 2
    hc = hh // HEAD_CHUNKS
    assert hc * HEAD_CHUNKS == hh and hc % 16 == 0

    def body(*refs):
        ins, wins, zins = refs[1:1 + ns], refs[1 + ns:1 + ns + nw], refs[1 + ns + nw:1 + ns + nw + nz]
        n_in = 1 + ns + nw + nz
        head_ref, shard16 = refs[n_in], refs[n_in + 1]
        outs, wouts = refs[n_in + 2:n_in + 2 + ns], refs[n_in + 2 + ns:n_in + 2 + ns + nw]
        scr = refs[n_in + 2 + ns + nw:]
        stage = scr[:ns]
        (send_sems, recv_sems, fwd_send, fwd_recv, loc_sems, w_send, w_recv, w_loc, z_sems,
         h_send, h_recv, h_relay, h_pass) = scr[ns:]
        x, y, c = lax.axis_index("x"), lax.axis_index("y"), lax.axis_index("c")
        mine = 2 * x + y
        chips = [(1 - x, y), (x, 1 - y), (1 - x, 1 - y)]
        chip_of = [2 * px + py for px, py in chips]
        shard16[...] = refs[0][...].astype(BF16)
        for a in range(ns):
            stage[a][...] = ins[a][...].astype(BF16)

        def head_rows(half, n):
            return pl.ds(pl.multiple_of(half * hh + n * hc, 16), hc)

        def head_copy(j, n):
            px, py = chips[j]
            return pltpu.make_async_remote_copy(
                src_ref=shard16.at[head_rows(c, n)], dst_ref=head_ref.at[head_rows(c, n)], send_sem=h_send.at[j, n],
                recv_sem=h_recv.at[n], device_id=(px, py, c), device_id_type=pl.DeviceIdType.MESH)

        def head_relay(n):
            ref = head_ref.at[head_rows(c, n)]
            return pltpu.make_async_remote_copy(
                src_ref=ref, dst_ref=ref, send_sem=h_relay.at[n], recv_sem=h_recv.at[n],
                device_id=(1, 1, c), device_id_type=pl.DeviceIdType.MESH)

        def head_pass(half, n):
            ref = head_ref.at[head_rows(half, n)]
            return pltpu.make_async_remote_copy(
                src_ref=ref, dst_ref=ref, send_sem=h_pass.at[0, n], recv_sem=h_pass.at[1, n],
                device_id=(x, y, 1 - c), device_id_type=pl.DeviceIdType.MESH)

        head_ref[HEAD_ROWS:IN_HEAD, :] = jnp.zeros((IN_HEAD - HEAD_ROWS, D_MODEL), BF16)

        @pl.when(mine == 0)
        def _():
            for n in range(HEAD_CHUNKS):
                for j in range(2):
                    head_copy(j, n).start()
            head_ref[0:HEAD_ROWS, :] = shard16[0:HEAD_ROWS, :]

        def src(k):
            a, (s0, nr, _, _, _, _) = flat[k]
            return stage[a].at[s0:s0 + nr]

        def dst(k, q):
            a, (_, nr, per, first, rest, _) = flat[k]
            row = per * q + first + (rest - first) * jnp.minimum(q, 1)
            return outs[a].at[pl.ds(pl.multiple_of(row, 16), nr)]

        def ici(k, j, q):
            px, py = chips[j]
            return pltpu.make_async_remote_copy(
                src_ref=src(k), dst_ref=dst(k, q), send_sem=send_sems.at[k, j], recv_sem=recv_sems.at[k, j],
                device_id=(px, py, c), device_id_type=pl.DeviceIdType.MESH)

        def fwd(k, j):
            ref = dst(k, chip_of[j])
            return pltpu.make_async_remote_copy(
                src_ref=ref, dst_ref=ref, send_sem=fwd_send.at[k, j], recv_sem=fwd_recv.at[k, j],
                device_id=(x, y, 1 - c), device_id_type=pl.DeviceIdType.MESH)

        def wcopy(b, j, q):
            px, py = chips[j]
            return pltpu.make_async_remote_copy(
                src_ref=wins[b], dst_ref=wouts[b].at[q], send_sem=w_send.at[b, j], recv_sem=w_recv.at[b, j],
                device_id=(px, py, c), device_id_type=pl.DeviceIdType.MESH)

        local = [pltpu.make_async_copy(src(k), dst(k, mine), loc_sems.at[k]) for k in range(nk)]
        local += [pltpu.make_async_copy(wins[b], wouts[b].at[mine], w_loc.at[b]) for b in range(nw)]
        for z, (a, _, row0) in enumerate(zero_fills):
            local.append(pltpu.make_async_copy(zins[z], outs[a].at[row0:row0 + zins[z].shape[0]], z_sems.at[z]))
        wsends = [wcopy(b, j, mine) for b in range(nw) for j in range(3)]
        for cp in local + wsends:
            cp.start()

        for half in (0, 1):
            @pl.when(c == half)
            def _(half=half):
                my_k = [k for k in range(nk) if flat[k][1][5] == half]
                other_k = [k for k in range(nk) if flat[k][1][5] != half]
                sends = [ici(k, j, mine) for k in my_k for j in range(3)]
                for cp in sends:
                    cp.start()
                passed = []
                for k in my_k:
                    for j in range(3):
                        ici(k, j, chip_of[j]).wait_recv()
                        cp = fwd(k, j)
                        cp.start()
                        passed.append(cp)
                for k in other_k:
                    for j in range(3):
                        fwd(k, j).wait_recv()
                for cp in sends + passed:
                    cp.wait_send()

        for b in range(nw):
            for j in range(3):
                wcopy(b, j, chip_of[j]).wait_recv()
        for cp in wsends:
            cp.wait_send()
        for cp in local:
            cp.wait()

        @pl.when(mine == 0)
        def _():
            for n in range(HEAD_CHUNKS):
                for j in range(2):
                    head_copy(j, n).wait_send()

        @pl.when(mine != 0)
        def _():
            hands_on = mine == 2 - c
            for n in range(HEAD_CHUNKS):
                head_copy(0, n).wait_recv()

                @pl.when(hands_on)
                def _(n=n):
                    head_relay(n).start()

                head_pass(c, n).start()
            for n in range(HEAD_CHUNKS):
                head_pass(1 - c, n).wait_recv()
                head_pass(c, n).wait_send()

                @pl.when(hands_on)
                def _(n=n):
                    head_relay(n).wait_send()

    vmem = pl.BlockSpec(memory_space=pltpu.VMEM)
    dma = pltpu.SemaphoreType.DMA
    zeros = [z for _, z, _ in zero_fills]
    return pl.pallas_call(
        body, name="gather_weights",
        in_specs=[vmem] * (1 + ns + nw + nz), out_specs=[vmem] * (2 + ns + nw),
        out_shape=([jax.ShapeDtypeStruct((IN_HEAD, D_MODEL), BF16), jax.ShapeDtypeStruct(w_in_shard.shape, BF16)]
                   + [jax.ShapeDtypeStruct((out_rows[a], split[a].shape[1]), BF16) for a in range(ns)]
                   + [jax.ShapeDtypeStruct((4,) + w.shape, w.dtype) for w in whole]),
        scratch_shapes=[pltpu.VMEM(a.shape, BF16) for a in split]
        + [dma((nk, 3)), dma((nk, 3)), dma((nk, 3)), dma((nk, 3)), dma((nk,)),
           dma((nw, 3)), dma((nw, 3)), dma((nw,)), dma((nz,)),
           dma((2, HEAD_CHUNKS)), dma((HEAD_CHUNKS,)), dma((HEAD_CHUNKS,)), dma((2, HEAD_CHUNKS))],
        compiler_params=pltpu.CompilerParams(vmem_limit_bytes=VMEM_LIMIT),
    )(w_in_shard, *split, *whole, *zeros)


def _reduce_grads(parts, small):
    n = len(parts)
    shapes = [a.shape[1:] for a in parts]
    halves = [(sh[0] // 2, sh[1]) for sh in shapes]

    def body(*refs):
        pin, sm_in = refs[:n], refs[n]
        gout, sm_out = refs[n + 1:2 * n + 1], refs[2 * n + 1]
        scr = refs[2 * n + 2:]
        own, sib, wire, rbuf = scr[:n], scr[n:2 * n], scr[2 * n:3 * n], scr[3 * n:4 * n]
        (sbuf, send_sems, recv_sems, loc_sems, pre_send, pre_recv, post_send, post_recv,
         sm_send, sm_recv) = scr[4 * n:]
        x, y, c = lax.axis_index("x"), lax.axis_index("y"), lax.axis_index("c")
        mine = 2 * x + y
        me = 4 * x + 2 * y + c
        sibling = (x, y, 1 - c)
        chips = [(1 - x, y), (x, 1 - y), (1 - x, 1 - y)]

        def rows(a, half):
            r2 = halves[a][0]
            return pl.ds(pl.multiple_of(half * r2, r2), r2)

        chip_of = [2 * px + py for px, py in chips]
        blocks = chip_of + [mine]

        def pre(a, k):
            return pltpu.make_async_remote_copy(
                src_ref=pin[a].at[blocks[k], rows(a, 1 - c), :], dst_ref=sib[a].at[blocks[k]],
                send_sem=pre_send.at[a, k], recv_sem=pre_recv.at[a, k], device_id=sibling,
                device_id_type=pl.DeviceIdType.MESH)

        def ici(a, j):
            px, py = chips[j]
            return pltpu.make_async_remote_copy(
                src_ref=wire[a].at[2 * px + py], dst_ref=rbuf[a].at[j], send_sem=send_sems.at[a, j],
                recv_sem=recv_sems.at[a, j], device_id=(px, py, c), device_id_type=pl.DeviceIdType.MESH)

        def post(a, half):
            ref = gout[a].at[rows(a, half), :]
            return pltpu.make_async_remote_copy(
                src_ref=ref, dst_ref=ref, send_sem=post_send.at[a], recv_sem=post_recv.at[a],
                device_id=sibling, device_id_type=pl.DeviceIdType.MESH)

        def small_copy(kk):
            peer = (x ^ (kk >> 2), y ^ ((kk >> 1) & 1), c ^ (kk & 1))
            return pltpu.make_async_remote_copy(
                src_ref=sm_in, dst_ref=sbuf.at[kk], send_sem=sm_send.at[kk - 1], recv_sem=sm_recv.at[kk - 1],
                device_id=peer, device_id_type=pl.DeviceIdType.MESH)

        local = [[pltpu.make_async_copy(pin[a].at[blocks[k], rows(a, c), :], own[a].at[blocks[k]], loc_sems.at[a, k])
                  for k in range(4)] for a in range(n)]
        pres = [[pre(a, k) for k in range(4)] for a in range(n)]
        smalls = [small_copy(kk) for kk in range(1, 8)]
        for a in range(n):
            for k in range(4):
                local[a][k].start()
                pres[a][k].start()
        for cp in smalls:
            cp.start()
        sbuf[0] = sm_in[...]
        sends = []
        for a in range(n):
            for k in range(4):
                local[a][k].wait()
                pres[a][k].wait_recv()
                tot = own[a][blocks[k]] + sib[a][blocks[k]]
                own[a][blocks[k]] = tot
                if k < 3:
                    wire[a][blocks[k]] = tot.astype(BF16)
                    cp = ici(a, k)
                    cp.start()
                    sends.append(cp)
        for cp in smalls:
            cp.wait_recv()
        total = sbuf[me]
        for d in range(1, 8):
            total = total + sbuf[me ^ d]
        sm_out[...] = total
        posts = []
        for a in range(n):
            for j in range(3):
                ici(a, j).wait_recv()
            fin = own[a][mine]
            for j in range(3):
                fin = fin + rbuf[a][j].astype(F32)
            gout[a][rows(a, c), :] = fin
            cp = post(a, c)
            cp.start()
            posts.append(cp)
        for a in range(n):
            post(a, 1 - c).wait_recv()
        for cp in [cp for row in pres for cp in row] + sends + smalls + posts:
            cp.wait_send()

    vmem = pl.BlockSpec(memory_space=pltpu.VMEM)
    dma = pltpu.SemaphoreType.DMA
    return pl.pallas_call(
        body, name="reduce_grads",
        in_specs=[pl.BlockSpec(memory_space=pl.ANY)] * n + [vmem], out_specs=[vmem] * (n + 1),
        out_shape=[jax.ShapeDtypeStruct(sh, F32) for sh in shapes] + [jax.ShapeDtypeStruct(small.shape, F32)],
        scratch_shapes=([pltpu.VMEM((4,) + hs, F32) for hs in halves] + [pltpu.VMEM((4,) + hs, F32) for hs in halves]
                        + [pltpu.VMEM((4,) + hs, BF16) for hs in halves]
                        + [pltpu.VMEM((3,) + hs, BF16) for hs in halves]
                        + [pltpu.VMEM((8,) + small.shape, F32), dma((n, 3)), dma((n, 3)), dma((n, 4)),
                           dma((n, 4)), dma((n, 4)), dma((n,)), dma((n,)), dma((7,)), dma((7,))]),
        compiler_params=pltpu.CompilerParams(vmem_limit_bytes=VMEM_LIMIT),
    )(*parts, small)


def _adamw_update(w_ref, g_ref, m_ref, v_ref, d_ref, nm_ref, nv_ref):
    gv = g_ref[...]
    nm = ADAM_B1 * m_ref[...] + (1.0 - ADAM_B1) * gv
    nv = ADAM_B2 * v_ref[...] + (1.0 - ADAM_B2) * (gv * gv)
    m_hat = nm / (1.0 - ADAM_B1 ** ADAM_STEP)
    v_hat = nv / (1.0 - ADAM_B2 ** ADAM_STEP)
    d_ref[...] = -ADAM_LR * (m_hat / (jnp.sqrt(v_hat) + ADAM_EPS) + ADAM_WD * w_ref[...])
    nm_ref[...] = nm
    nv_ref[...] = nv


def _adamw_small(ws, gs, ms, vs):
    k = len(ws)

    def body(*refs):
        ins, outs = refs[:4 * k], refs[4 * k:]
        for a in range(k):
            _adamw_update(ins[a], ins[k + a], ins[2 * k + a], ins[3 * k + a], outs[a], outs[k + a], outs[2 * k + a])

    out = pl.pallas_call(
        body, name="adamw_small",
        out_shape=[jax.ShapeDtypeStruct(w.shape, F32) for w in ws] * 3,
        compiler_params=pltpu.CompilerParams(vmem_limit_bytes=VMEM_LIMIT),
    )(*ws, *gs, *ms, *vs)
    return out[:k], out[k:2 * k], out[2 * k:]


def _adamw(w, g, m, v, name):
    shape = w.shape
    w2, g2, m2, v2 = (a.reshape((-1, shape[-1])) for a in (w, g, m, v))

    def body(w_ref, g_ref, m_ref, v_ref, d_ref, nm_ref, nv_ref):
        _adamw_update(w_ref, g_ref, m_ref, v_ref, d_ref, nm_ref, nv_ref)

    rows, cols = w2.shape
    nblk = cols // 256 if cols % 256 == 0 and rows >= 64 else 1
    blk = pl.BlockSpec((rows, cols // nblk), lambda j: (0, j))
    out = pl.pallas_call(
        body, name=name, grid=(nblk,), in_specs=[blk] * 4, out_specs=[blk] * 3,
        out_shape=[jax.ShapeDtypeStruct(w2.shape, F32)] * 3,
        compiler_params=_cparams("parallel"),
    )(w2, g2, m2, v2)
    return tuple(a.reshape(shape) for a in out)


def kernel(x, meta_tokens, norm_g, w_in, q_norm_g, w_q_up, kv_norm_g, w_kv_up, conv_w, attn_out_g, conv_out_g, w_out, final_norm_g, loss_target, m_meta_tokens, m_norm_g, m_w_in, m_q_norm_g, m_w_q_up, m_kv_norm_g, m_w_kv_up, m_conv_w, m_attn_out_g, m_conv_out_g, m_w_out, m_final_norm_g, v_meta_tokens, v_norm_g, v_w_in, v_q_norm_g, v_w_q_up, v_kv_norm_g, v_w_kv_up, v_conv_w, v_attn_out_g, v_conv_out_g, v_w_out, v_final_norm_g):
    nb, s, _ = x.shape
    tm = min(ROW_TILE, s)
    ta = min(ATTN_TILE, s)
    assert s % tm == 0 and s % ta == 0 and tm % 16 == 0
    r = nb * s

    tr = lambda a: jnp.transpose(a[0])
    w_head, w_in_shard, wq_p, wkv_p, g_cw, g_meta = _gather_weights(
        tr(w_in), [tr(w_q_up), tr(w_kv_up)],
        [W_Q_PIECES, W_KV_PIECES], [HEADS * QK_PAD, 1024],
        [jnp.transpose(conv_w, (1, 0, 2)), meta_tokens],
        [(0, jnp.zeros((64, Q_RANK), BF16), QK_PAD * h + NOPE + ROPE) for h in range(HEADS)])
    conv_f = jnp.transpose(g_cw[:, :, 0, :], (1, 0, 2)).reshape(3, CONV_W)
    meta_f = jnp.transpose(g_meta, (1, 0, 2)).reshape(N_META, D_MODEL)

    c_all, sa_all, sb_all = _rope_tables(N_META + s)
    tabs_m = (c_all[:N_META], sa_all[:N_META], sb_all[:N_META])
    tabs = (c_all[N_META:], sa_all[N_META:], sb_all[N_META:])
    gid = np.arange(CONV_W) // CONV_GROUP
    gmat = jnp.asarray(np.where(gid[:, None] == gid[None, :], 1.0 / CONV_GROUP, 0.0), BF16)
    ga, gc = attn_out_g, conv_out_g
    gf = final_norm_g.reshape(1, D_MODEL)

    x2d = x.reshape(r, D_MODEL)
    tgt2d = loss_target.reshape(r, D_MODEL)

    ph, q, k, v, pmh, km, vm, w_out_f, w_in_part = _fwd_proj(
        x2d, meta_f, tabs, tabs_m, norm_g, w_head, q_norm_g, wq_p, kv_norm_g, wkv_p, w_out[0].astype(BF16),
        w_in_shard, nb, s, tm)
    o, lse, w_in_p = _attn_fwd(q, k, v, km, vm, w_in_shard, w_in_part, nb, s, ta)
    dh2, dycat, dw_out, dgf, loss_acc, pt, pmt = _out_fwd_bwd(x2d, tgt2d, o, meta_f, norm_g, w_in_p, conv_f, ga, gc,
                                                              gmat, w_out_f, gf, nb, s, tm)
    dpb, do, delta, dccm, dga, dgc, dcw = _gate_bwd(dycat, o, pt, pmt, conv_f, ga, gc, gmat, nb, s, tm)
    p_out = dw_out.reshape(4, D_MODEL // 4, D_MODEL)
    dq, dk, dv, dkm, dvm, g_w_out = _attn_bwd(q, k, v, do, lse, delta, km, vm, [p_out], nb, s, ta)
    dpa, dpam, p_q, p_kv, dgq, dgkv = _up_bwd(dq, dk, dv, dkm, dvm, ph, pmh, tabs, tabs_m, wq_p, wkv_p,
                                              q_norm_g, kv_norm_g, nb, s, tm)
    gx, gmeta, p_in, dng = _in_bwd(x2d, dh2, dpa, dpb, meta_f, dpam, dccm, pmt, w_in_p, norm_g, nb, s, tm)

    flat =jnp.concatenate([dng.reshape(-1), dgq.reshape(-1), dgkv.reshape(-1), dga.reshape(-1), dgc.reshape(-1),
                            dgf.reshape(-1), dcw[:3].reshape(-1), gmeta.reshape(-1), loss_acc[0, 0:1]])
    n_small = flat.shape[0]
    rows_small = -(-n_small // 1024) * 8
    small = jnp.pad(flat, (0, rows_small * 128 - n_small)).reshape(rows_small, 128)
    g_w_in_t, g_w_q_t, g_w_kv_t, small_sum = _reduce_grads([p_in, p_q, p_kv], small)
    ssum = small_sum.reshape(-1)

    def take(off, n):
        return ssum[off:off + n], off + n

    off = 0
    g_norm, off = take(off, D_MODEL)
    g_qn, off = take(off, Q_RANK)
    g_kvn, off = take(off, KV_RANK)
    g_ga, off = take(off, CONV_W)
    g_gc, off = take(off, CONV_W)
    g_gf, off = take(off, D_MODEL)
    g_cw_all, off = take(off, 3 * CONV_W)
    g_meta_all, off = take(off, N_META * D_MODEL)
    loss = ssum[off]
    chip = 2 * lax.axis_index("x") + lax.axis_index("y")
    g_conv = lax.dynamic_slice(g_cw_all.reshape(3, CONV_W), (0, chip * 128), (3, 128))
    g_mt = lax.dynamic_slice(g_meta_all.reshape(N_META, D_MODEL), (0, chip * 256), (N_META, 256))

    grads = {
        "meta_tokens": g_mt, "norm_g": g_norm.reshape(1, -1), "w_in": g_w_in_t, "q_norm_g": g_qn.reshape(1, -1),
        "w_q_up": g_w_q_t, "kv_norm_g": g_kvn.reshape(1, -1), "w_kv_up": jnp.transpose(g_w_kv_t)[None],
        "conv_w": g_conv[None], "attn_out_g": g_ga.reshape(1, -1), "conv_out_g": g_gc.reshape(1, -1),
        "w_out": g_w_out[None], "final_norm_g": g_gf,
    }
    transposed = ("w_in", "w_q_up")
    weights = {
        "meta_tokens": (meta_tokens, m_meta_tokens, v_meta_tokens), "norm_g": (norm_g, m_norm_g, v_norm_g),
        "w_in": (w_in, m_w_in, v_w_in), "q_norm_g": (q_norm_g, m_q_norm_g, v_q_norm_g),
        "w_q_up": (w_q_up, m_w_q_up, v_w_q_up), "kv_norm_g": (kv_norm_g, m_kv_norm_g, v_kv_norm_g),
        "w_kv_up": (w_kv_up, m_w_kv_up, v_w_kv_up), "conv_w": (conv_w, m_conv_w, v_conv_w),
        "attn_out_g": (attn_out_g, m_attn_out_g, v_attn_out_g), "conv_out_g": (conv_out_g, m_conv_out_g, v_conv_out_g),
        "w_out": (w_out, m_w_out, v_w_out), "final_norm_g": (final_norm_g, m_final_norm_g, v_final_norm_g),
    }
    names = list(weights)
    small = [nme for nme in names if nme != "w_in"]

    def view(nme, a):
        if nme in transposed:
            return a if a.ndim == 2 else tr(a)
        if nme == "conv_w":
            return jnp.transpose(a.reshape(1, 3, -1), (1, 0, 2))
        if a.ndim == 3:
            return a[0]
        return a.reshape(1, -1) if a.ndim == 1 else a

    def unview(nme, a):
        if nme in transposed:
            return jnp.transpose(a)[None]
        if nme == "conv_w":
            return jnp.transpose(a, (1, 0, 2))
        return a.reshape(weights[nme][0].shape)

    res_small = _adamw_small(*[[view(nme, a) for nme, a in zip(small, col)] for col in (
        [weights[nme][0] for nme in small], [grads[nme] for nme in small],
        [weights[nme][1] for nme in small], [weights[nme][2] for nme in small])])
    w_, m_, v_ = weights["w_in"]
    res = _adamw(tr(w_), grads["w_in"], tr(m_), tr(v_), "adamw_w_in")
    upd = {"w_in": tuple(jnp.transpose(a)[None] for a in (grads["w_in"],) + res)}
    for j, nme in enumerate(small):
        upd[nme] = (unview(nme, view(nme, grads[nme])),) + tuple(unview(nme, r[j]) for r in res_small)
    grads = {nme: upd[nme][0] for nme in names}
    deltas, new_m, new_v = ([upd[nme][j] for nme in names] for j in (1, 2, 3))

    grad_x = gx.reshape(nb, s, D_MODEL)
    return (loss, grad_x, *[grads[nme] for nme in names], *deltas, *new_m, *new_v)
```

```python
import functools

import jax
import jax.numpy as jnp
import numpy as np
from jax import lax
from jax.experimental import pallas as pl
from jax.experimental.pallas import tpu as pltpu

F32 = jnp.float32
BF16 = jnp.bfloat16

D_MODEL = 1024
N_META = 16
HEADS = 4
NOPE = 128
ROPE = 64
VDIM = 128
QK_PAD = 256
Q_RANK = 256
KV_RANK = 128
CONV_W = 512
CONV_GROUP = 64
ROPE_THETA = 10000.0
EPS = 1e-6
ATTN_SCALE = (NOPE + ROPE) ** -0.5
IN_DIM = 3008
IN_PAD = 3072
HEAD_ROWS = Q_RANK + KV_RANK + ROPE
HEAD_CHUNKS = 2
IN_HEAD = 512
IN_TAIL = IN_PAD - IN_HEAD
BLK_ZA, BLK_CB, BLK_CC, BLK_CH, BLK_ZC = 0, 1, 2, 3, 4
NEG_INF = -1e30

ADAM_LR = 0.001
ADAM_B1 = 0.9
ADAM_B2 = 0.999
ADAM_EPS = 1e-08
ADAM_WD = 0.01
ADAM_STEP = 10

ROW_TILE = 512
ATTN_TILE = 256
VMEM_LIMIT = 56 * 1024 * 1024

NT = (((1,), (1,)), ((), ()))
TN = (((0,), (0,)), ((), ()))


def _cparams(*sem):
    return pltpu.CompilerParams(dimension_semantics=sem, vmem_limit_bytes=VMEM_LIMIT)


def _dot(a, b):
    return jnp.dot(a, b, preferred_element_type=F32)


def _dot_nt(a, b):
    return lax.dot_general(a, b, NT, preferred_element_type=F32)


def _dot_tn(a, b):
    return lax.dot_general(a, b, TN, preferred_element_type=F32)


def _rms(x, g):
    r = lax.rsqrt(jnp.mean(x * x, axis=-1, keepdims=True) + EPS)
    return x * r * g, r


def _rms_bwd(dy, x, r, g):
    xh = x * r
    dyg = dy * g
    dx = r * (dyg - xh * jnp.mean(dyg * xh, axis=-1, keepdims=True))
    return dx, dy * xh


def _sigmoid(z):
    return 1.0 / (1.0 + jnp.exp(-z))


def _rope(b, c, sa, sb):
    return b * c + pltpu.roll(b, 96, 1) * sa + pltpu.roll(b, 32, 1) * sb


def _rope_bwd(d, c, sa, sb):
    return d * c + pltpu.roll(d * sa, 32, 1) + pltpu.roll(d * sb, 96, 1)


def _group_mean(x, gmat):
    hi = x.astype(BF16)
    lo = (x - hi.astype(F32)).astype(BF16)
    return _dot(hi, gmat) + _dot(lo, gmat)


def _row_of(col, rows):
    return jnp.transpose(jnp.broadcast_to(col, (rows, 128)))[0:1, :]


def _rope_tables(n_pos):
    half = ROPE // 2
    inv_freq = (np.float32(1.0) / (np.float32(ROPE_THETA) ** (np.arange(half, dtype=np.float32) / np.float32(half))))
    ang = np.arange(n_pos, dtype=np.float32)[:, None] * inv_freq.astype(np.float32)[None, :]
    cos, sin = np.cos(ang).astype(np.float32), np.sin(ang).astype(np.float32)
    z = np.zeros((n_pos, half), np.float32)
    c = np.concatenate([cos, cos, z, z], axis=1)
    sa = np.concatenate([-sin, z, z, z], axis=1)
    sb = np.concatenate([z, sin, z, z], axis=1)
    return jnp.asarray(c), jnp.asarray(sa), jnp.asarray(sb)


W_IN_PIECES_1 = ((0, 80, 752, 0, 64, 0), (384, 64, 752, 384, 448, 1), (448, 16, 752, 512, 512, 1))
W_IN_PIECES_2 = ((80, 304, 752, 80, 144, 0), (464, 288, 752, 528, 528, 1))
W_Q_PIECES = ((0, 96, 256, 0, 0, 0), (96, 96, 256, 96, 96, 1))
W_KV_PIECES = ((0, 128, 128, 0, 0, 0), (128, 128, 128, 512, 512, 1))
W_OUT_PIECES = ((0, 128, 256, 0, 0, 0), (128, 128, 256, 128, 128, 1))


class _StagedGather:
    STAGES = 4

    @staticmethod
    def steps(n_steps):
        return (0, 5 * n_steps // 8, 7 * n_steps // 8, n_steps - 1)

    def __init__(self, pieces, zero_rows=None):
        self.pieces = pieces
        self.zero_rows = zero_rows

    def scratch(self):
        nk, dma = len(self.pieces), pltpu.SemaphoreType.DMA
        return [dma((nk, 3)), dma((nk, 3)), dma((nk, 3)), dma((nk, 3)), dma((nk,))]

    def vmem_scratch(self, shard_shape, out_shape):
        return [pltpu.VMEM(shard_shape, BF16), pltpu.VMEM(out_shape, BF16),
                pltpu.SemaphoreType.DMA((4 * len(self.pieces) + 2,))] + self.scratch()

    def run_vmem(self, stage, shard_ref, out_ref, scr):
        src_scr, land_scr, io_sems = scr[:3]
        spans = []
        for _, nr, per, first, rest, _ in self.pieces:
            spans += [(per * q + (first if q == 0 else rest), nr) for q in range(4)]
        if self.zero_rows is not None:
            spans.append(self.zero_rows)
        flush = [pltpu.make_async_copy(land_scr.at[r0:r0 + nr], out_ref.at[r0:r0 + nr], io_sems.at[n])
                 for n, (r0, nr) in enumerate(spans)]
        if stage == 0:
            load = pltpu.make_async_copy(shard_ref, src_scr, io_sems.at[len(spans)])
            load.start()
            if self.zero_rows is not None:
                r0, nr = self.zero_rows
                land_scr[r0:r0 + nr, :] = jnp.zeros((nr, land_scr.shape[1]), BF16)
            load.wait()
        if stage < self.STAGES:
            self.run(stage, src_scr, land_scr, scr[3:])
        for cp in flush:
            if stage == self.STAGES - 1:
                cp.start()
            if stage == self.STAGES:
                cp.wait()

    def run(self, stage, src_ref, out_ref, scr):
        send_sems, recv_sems, fwd_send, fwd_recv, loc_sems = scr
        pieces = self.pieces
        nk = len(pieces)
        x, y, c = lax.axis_index("x"), lax.axis_index("y"), lax.axis_index("c")
        mine = 2 * x + y
        chips = [(1 - x, y), (x, 1 - y), (1 - x, 1 - y)]
        chip_of = [2 * px + py for px, py in chips]
        mesh = pl.DeviceIdType.MESH

        def src(k):
            s0, nr = pieces[k][0], pieces[k][1]
            return src_ref.at[s0:s0 + nr]

        def dst(k, q):
            _, nr, per, first, rest, _ = pieces[k]
            row = per * q + first + (rest - first) * jnp.minimum(q, 1)
            return out_ref.at[pl.ds(pl.multiple_of(row, 16), nr)]

        def ici(k, j, q):
            px, py = chips[j]
            return pltpu.make_async_remote_copy(
                src_ref=src(k), dst_ref=dst(k, q), send_sem=send_sems.at[k, j], recv_sem=recv_sems.at[k, j],
                device_id=(px, py, c), device_id_type=mesh)

        def fwd(k, j):
            ref = dst(k, chip_of[j])
            return pltpu.make_async_remote_copy(
                src_ref=ref, dst_ref=ref, send_sem=fwd_send.at[k, j], recv_sem=fwd_recv.at[k, j],
                device_id=(x, y, 1 - c), device_id_type=mesh)

        def relay(k, half):
            ref = dst(k, chip_of[half])
            px, py = chips[1 - half]
            return pltpu.make_async_remote_copy(
                src_ref=ref, dst_ref=ref, send_sem=send_sems.at[k, 2], recv_sem=recv_sems.at[k, 2],
                device_id=(px, py, c), device_id_type=mesh)

        local = [pltpu.make_async_copy(src(k), dst(k, mine), loc_sems.at[k]) for k in range(nk)]
        if stage == 0:
            for cp in local:
                cp.start()
        if stage == 3:
            for cp in local:
                cp.wait()
        for half in (0, 1):
            @pl.when(c == half)
            def _(half=half):
                my_k = [k for k in range(nk) if pieces[k][5] == half]
                other_k = [k for k in range(nk) if pieces[k][5] != half]
                for k in my_k:
                    if stage == 0:
                        for j in range(2):
                            ici(k, j, mine).start()
                    elif stage == 1:
                        for j in (half, 1 - half):
                            ici(k, j, chip_of[j]).wait_recv()
                            if j == half:
                                relay(k, half).start()
                            fwd(k, j).start()
                    elif stage == 2:
                        ici(k, 2, chip_of[2]).wait_recv()
                        fwd(k, 2).start()
                    else:
                        for j in range(2):
                            ici(k, j, mine).wait_send()
                        relay(k, half).wait_send()
                        for j in range(3):
                            fwd(k, j).wait_send()
                if stage == 3:
                    for k in other_k:
                        for j in range(3):
                            fwd(k, j).wait_recv()


def _fwd_proj(x2d, meta, tabs, tabs_m, norm_g, w_head, q_norm_g, wq_p, kv_norm_g, wkv_p, w_out_shard, w_in_shard,
              nb, s, tm):
    nt = s // tm
    n = nb * nt
    n_steps = n + 1
    c_t, sa_t, sb_t = tabs
    cm_t, sam_t, sbm_t = tabs_m
    gat = _StagedGather(W_OUT_PIECES)
    gat_in = _StagedGather(W_IN_PIECES_1)
    n_sems = len(gat.scratch())
    assert n_steps >= 3

    def body(x_ref, c_ref, sa_ref, sb_ref, mt_ref, cm_ref, sam_ref, sbm_ref,
             g_ref, w_ref, gq_ref, wq_ref, gkv_ref, wkv_ref, wos_ref, wis_ref,
             p_ref, q_ref, k_ref, v_ref, pm_ref, km_ref, vm_ref, wo_ref, wi_ref, *scr):
        gat_scr, gat_in_scr = scr[:n_sems], scr[n_sems:]
        i = pl.program_id(0)
        for stage, at in enumerate(_StagedGather.steps(n_steps)):
            @pl.when(i == at)
            def _(stage=stage):
                gat_in.run_vmem(stage, wis_ref, wi_ref, gat_in_scr)
                gat.run(stage, wos_ref, wo_ref, gat_scr)

        def project(xv, c, sa, sb, p_out, q_out, k_out, v_out):
            u, _ = _rms(xv, g_ref[...])
            p = _dot_nt(u.astype(BF16), w_ref[...])
            p_out[...] = p
            qn, _ = _rms(p[:, 0:Q_RANK], gq_ref[...])
            q = _dot_nt(qn.astype(BF16), wq_ref[...])
            kvn, _ = _rms(p[:, Q_RANK:Q_RANK + KV_RANK], gkv_ref[...])
            kv = _dot_nt(kvn.astype(BF16), wkv_ref[...])
            kpe = _rope(p[:, 384:512], c, sa, sb)
            for h in range(HEADS):
                if q_out is not None:
                    pe = _rope(q[:, QK_PAD * h + NOPE:QK_PAD * (h + 1)], c, sa, sb)
                    qh = jnp.concatenate([q[:, QK_PAD * h:QK_PAD * h + NOPE], pe], axis=1)
                    q_out[0, h] = (qh * ATTN_SCALE).astype(BF16)
                k_out[0, h] = jnp.concatenate([kv[:, NOPE * h:NOPE * (h + 1)], kpe], axis=1).astype(BF16)
                v_out[0, h] = kv[:, 512 + VDIM * h:512 + VDIM * (h + 1)].astype(BF16)

        @pl.when(i < n)
        def _():
            project(x_ref[...], c_ref[...], sa_ref[...], sb_ref[...], p_ref, q_ref, k_ref, v_ref)

        @pl.when(i == n)
        def _():
            project(mt_ref[...], cm_ref[...], sam_ref[...], sbm_ref[...], pm_ref, None, km_ref, vm_ref)
            gat_in.run_vmem(gat_in.STAGES, wis_ref, wi_ref, gat_in_scr)

    cl = lambda i: jnp.minimum(i, n - 1)
    full = lambda a: pl.BlockSpec(a.shape, lambda i: (0,) * a.ndim)
    const = lambda shape: pl.BlockSpec(shape, lambda i: (0,) * len(shape))
    tab = pl.BlockSpec((tm, 128), lambda i: (cl(i) % nt, 0))
    hb = lambda w: pl.BlockSpec((1, HEADS, tm, w), lambda i: (cl(i) // nt, 0, cl(i) % nt, 0))
    whole = pl.BlockSpec(memory_space=pl.ANY)
    return pl.pallas_call(
        body, name="fwd_proj", grid=(n_steps,),
        in_specs=[pl.BlockSpec((tm, D_MODEL), lambda i: (cl(i), 0)), tab, tab, tab,
                  full(meta), full(cm_t), full(sam_t), full(sbm_t),
                  full(norm_g), full(w_head), full(q_norm_g), full(wq_p), full(kv_norm_g), full(wkv_p), whole, whole],
        out_specs=[pl.BlockSpec((tm, IN_HEAD), lambda i: (cl(i), 0)), hb(QK_PAD), hb(QK_PAD), hb(VDIM),
                   const((N_META, IN_HEAD)), const((1, HEADS, N_META, QK_PAD)), const((1, HEADS, N_META, VDIM)),
                   whole, whole],
        out_shape=[jax.ShapeDtypeStruct((nb * s, IN_HEAD), F32),
                   jax.ShapeDtypeStruct((nb, HEADS, s, QK_PAD), BF16),
                   jax.ShapeDtypeStruct((nb, HEADS, s, QK_PAD), BF16),
                   jax.ShapeDtypeStruct((nb, HEADS, s, VDIM), BF16),
                   jax.ShapeDtypeStruct((N_META, IN_HEAD), F32),
                   jax.ShapeDtypeStruct((1, HEADS, N_META, QK_PAD), BF16),
                   jax.ShapeDtypeStruct((1, HEADS, N_META, VDIM), BF16),
                   jax.ShapeDtypeStruct((D_MODEL, D_MODEL), BF16),
                   jax.ShapeDtypeStruct((IN_PAD, D_MODEL), BF16)],
        scratch_shapes=gat.scratch() + gat_in.vmem_scratch(w_in_shard.shape, (IN_PAD, D_MODEL)),
        compiler_params=_cparams("arbitrary"),
    )(x2d, c_t, sa_t, sb_t, meta, cm_t, sam_t, sbm_t, norm_g, w_head, q_norm_g, wq_p, kv_norm_g, wkv_p, w_out_shard,
      w_in_shard)


def _attn_fwd(q, k, v, km, vm, w_in_shard, w_in_part, nb, s, tq):
    nq = s // tq
    n_steps = nb * HEADS
    gat = _StagedGather(W_IN_PIECES_2, zero_rows=(HEAD_ROWS, IN_HEAD - HEAD_ROWS))
    assert n_steps >= 3

    def body(q_ref, k_ref, v_ref, km_ref, vm_ref, ws_ref, _, o_ref, lse_ref, w_ref, s_scr, p_scr, *gat_scr):
        step = pl.program_id(0) * HEADS + pl.program_id(1)
        for stage, at in enumerate(_StagedGather.steps(n_steps)):
            @pl.when(step == at)
            def _(stage=stage):
                gat.run_vmem(stage, ws_ref, w_ref, gat_scr)

        row = lax.broadcasted_iota(jnp.int32, (tq, tq), 0)
        col = lax.broadcasted_iota(jnp.int32, (tq, tq), 1)
        def scores(i):
            slot = i % 2
            qi = q_ref[0, 0, i * tq:(i + 1) * tq, :]
            sm = _dot_nt(qi, km_ref[0, 0])
            m128 = None
            for j in range(i + 1):
                sc = _dot_nt(qi, k_ref[0, 0, j * tq:(j + 1) * tq, :])
                if j == i:
                    sc = jnp.where(col <= row, sc, NEG_INF)
                s_scr[slot, :, j * tq:(j + 1) * tq] = sc
                mx = sc[:, 0:128]
                for c0 in range(128, tq, 128):
                    mx = jnp.maximum(mx, sc[:, c0:c0 + 128])
                m128 = mx if m128 is None else jnp.maximum(m128, mx)
            return sm, jnp.maximum(jnp.max(m128, axis=1, keepdims=True), jnp.max(sm, axis=1, keepdims=True))

        def weighted_sum(i, pm, l):
            n = (i + 1) * tq
            acc = _dot(p_scr[i % 2, :, 0:n], v_ref[0, 0, 0:n, :]) + _dot(pm.astype(BF16), vm_ref[0, 0])
            o_ref[0, 0, i * tq:(i + 1) * tq, :] = acc / l

        nxt, pending = scores(0), None
        for i in range(nq):
            slot = i % 2
            sm, m = nxt
            if i + 1 < nq:
                nxt = scores(i + 1)
            pm = jnp.exp(sm - m)
            l128 = None
            for j in range(i + 1):
                p = jnp.exp(s_scr[slot, :, j * tq:(j + 1) * tq] - m)
                p_scr[slot, :, j * tq:(j + 1) * tq] = p.astype(BF16)
                ps = p[:, 0:128]
                for c0 in range(128, tq, 128):
                    ps = ps + p[:, c0:c0 + 128]
                l128 = ps if l128 is None else l128 + ps
            l = jnp.sum(l128, axis=1, keepdims=True) + jnp.sum(pm, axis=1, keepdims=True)
            lse_ref[0, 0, :, i * tq:(i + 1) * tq] = _row_of(m + jnp.log(l), tq)
            if pending is not None:
                weighted_sum(*pending)
            pending = (i, pm, l)
        weighted_sum(*pending)

        @pl.when(step == n_steps - 1)
        def _():
            gat.run_vmem(gat.STAGES, ws_ref, w_ref, gat_scr)

    hblk = lambda w: pl.BlockSpec((1, 1, s, w), lambda b, h: (b, h, 0, 0))
    mblk = lambda w: pl.BlockSpec((1, 1, N_META, w), lambda b, h: (0, h, 0, 0))
    whole = pl.BlockSpec(memory_space=pl.ANY)
    return pl.pallas_call(
        body, name="attn_fwd", grid=(nb, HEADS),
        in_specs=[hblk(QK_PAD), hblk(QK_PAD), hblk(VDIM), mblk(QK_PAD), mblk(VDIM), whole, whole],
        out_specs=[hblk(VDIM), pl.BlockSpec((1, 1, 1, s), lambda b, h: (b, h, 0, 0)), whole],
        out_shape=[jax.ShapeDtypeStruct((nb, HEADS, s, VDIM), F32),
                   jax.ShapeDtypeStruct((nb, HEADS, 1, s), F32),
                   jax.ShapeDtypeStruct(w_in_part.shape, BF16)],
        input_output_aliases={6: 2},
        scratch_shapes=[pltpu.VMEM((2, tq, s), F32), pltpu.VMEM((2, tq, s), BF16)]
        + gat.vmem_scratch(w_in_shard.shape, w_in_part.shape),
        compiler_params=_cparams("arbitrary", "arbitrary"),
    )(q, k, v, km, vm, w_in_shard, w_in_part)


def _shift_rows(a, prev, n_rows):
    rid = lax.broadcasted_iota(jnp.int32, a.shape, 0)
    a1 = jnp.where(rid == 0, prev[7:8, :], pltpu.roll(a, 1, 0))
    a2 = jnp.where(rid == 0, prev[6:7, :], jnp.where(rid == 1, prev[7:8, :], pltpu.roll(a, 2, 0)))
    return a1, a2


def _attn_gate(o, za, ga_h):
    on, r = _rms(o, ga_h)
    return on * (za * _sigmoid(za)), on, r


def _out_fwd_bwd(x2d, tgt2d, o, meta, norm_g, w_in_p, conv_w, ga, gc, gmat, w_out, gf, nb, s, tm):
    nt = s // tm
    r = nb * s

    def body(x_ref, t_ref, o_ref, mt_ref, g_ref, wi_ref, cw_ref, ga_ref, gc_ref, gm_ref, w_ref, gf_ref,
             dh_ref, dy_ref, dw_ref, dgf_ref, loss_ref, p_ref, pm_ref, last_cc):
        i = pl.program_id(0)
        blk = lambda ref, j, rows=slice(None): ref[rows, 512 * j:512 * (j + 1)]

        def tail(xv):
            u, _ = _rms(xv, g_ref[...])
            return _dot_nt(u.astype(BF16), wi_ref[IN_HEAD:IN_PAD, :])

        @pl.when(i == 0)
        def _():
            dw_ref[...] = jnp.zeros_like(dw_ref)
            dgf_ref[...] = jnp.zeros_like(dgf_ref)
            loss_ref[...] = jnp.zeros_like(loss_ref)
            last_cc[...] = jnp.zeros_like(last_cc)
            pm_ref[...] = tail(mt_ref[...])

        u16 = _rms(x_ref[...], g_ref[...])[0].astype(BF16)

        def project(j):
            p_ref[:, 512 * j:512 * (j + 1)] = _dot_nt(u16, wi_ref[IN_HEAD + 512 * j:IN_HEAD + 512 * (j + 1), :])

        project(BLK_ZA)
        project(BLK_CC)
        project(BLK_CH)
        ya = []
        for h in range(HEADS):
            y, _, _ = _attn_gate(o_ref[0, h], p_ref[:, 512 * BLK_ZA + VDIM * h:512 * BLK_ZA + VDIM * (h + 1)],
                                 ga_ref[:, VDIM * h:VDIM * (h + 1)])
            ya.append(y)
        project(BLK_CB)
        project(BLK_ZC)
        cc = blk(p_ref, BLK_CC) * blk(p_ref, BLK_CH)
        meta_cc = blk(pm_ref, BLK_CC, slice(8, 16)) * blk(pm_ref, BLK_CH, slice(8, 16))
        prev = jnp.where(i % nt == 0, meta_cc, last_cc[...])
        last_cc[...] = cc[tm - 8:tm, :]
        cc1, cc2 = _shift_rows(cc, prev, tm)
        yc = blk(p_ref, BLK_CB) * (cw_ref[0:1, :] * cc2 + cw_ref[1:2, :] * cc1 + cw_ref[2:3, :] * cc)
        rg = lax.rsqrt(_group_mean(yc * yc, gm_ref[...]) + EPS)
        zc = blk(p_ref, BLK_ZC)
        yconv = yc * rg * gc_ref[...] * (zc * _sigmoid(zc))
        ycat = jnp.concatenate(ya + [yconv], axis=1).astype(BF16)
        h2 = x_ref[...] + _dot(ycat, w_ref[...])
        gfv = gf_ref[...]
        y, r2 = _rms(h2, gfv)
        e = y - t_ref[...]
        loss_ref[...] += 0.5 * jnp.sum(e * e) / D_MODEL
        dyv = e * (1.0 / D_MODEL)
        dh2, dgf = _rms_bwd(dyv, h2, r2, gfv)
        dgf_ref[...] += jnp.sum(dgf, axis=0, keepdims=True)
        dh_ref[...] = dh2
        dhb = dh2.astype(BF16)
        dy_ref[...] = _dot_nt(dhb, w_ref[...])
        dw_ref[...] += _dot_tn(ycat, dhb)

    row = lambda w: pl.BlockSpec((tm, w), lambda i: (i, 0))
    const = lambda shape: pl.BlockSpec(shape, lambda i: (0,) * len(shape))
    full = lambda a: const(a.shape)
    return pl.pallas_call(
        body, name="out_fwd_bwd", grid=(nb * nt,),
        in_specs=[row(D_MODEL), row(D_MODEL),
                  pl.BlockSpec((1, HEADS, tm, VDIM), lambda i: (i // nt, 0, i % nt, 0)),
                  full(meta), full(norm_g), full(w_in_p),
                  full(conv_w), full(ga), full(gc), full(gmat), full(w_out), full(gf)],
        out_specs=[row(D_MODEL), row(D_MODEL), const((D_MODEL, D_MODEL)), const((1, D_MODEL)), const((1, 128)),
                   row(IN_TAIL), const((N_META, IN_TAIL))],
        out_shape=[jax.ShapeDtypeStruct((r, D_MODEL), F32), jax.ShapeDtypeStruct((r, D_MODEL), F32),
                   jax.ShapeDtypeStruct((D_MODEL, D_MODEL), F32), jax.ShapeDtypeStruct((1, D_MODEL), F32),
                   jax.ShapeDtypeStruct((1, 128), F32),
                   jax.ShapeDtypeStruct((r, IN_TAIL), F32), jax.ShapeDtypeStruct((N_META, IN_TAIL), F32)],
        scratch_shapes=[pltpu.VMEM((8, 512), F32)],
        compiler_params=_cparams("arbitrary"),
    )(x2d, tgt2d, o, meta, norm_g, w_in_p, conv_w, ga, gc, gmat, w_out, gf)


def _gate_bwd(dycat, o, p, pm, conv_w, ga, gc, gmat, nb, s, tm):
    nt = s // tm
    r = nb * s
    ext = tm + 8
    prev_idx = lambda i: jnp.maximum(i * (tm // 8) - 1, 0)
    next_idx = lambda i: jnp.minimum((i + 1) * (tm // 8), r // 8 - 1)

    def body(dya_ref, dyc_ref, dycn_ref, o_ref, za_ref, cb_ref, cbn_ref, cc_ref, ccp_ref, ccn_ref,
             ch_ref, chp_ref, chn_ref, zc_ref, zcn_ref, mc_ref, mh_ref, cw_ref, ga_ref, gc_ref, gm_ref,
             dpb_ref, do_ref, dl_ref, dccm_ref, dga_ref, dgc_ref, dcw_ref):
        i = pl.program_id(0)

        @pl.when(i == 0)
        def _():
            dga_ref[...] = jnp.zeros_like(dga_ref)
            dgc_ref[...] = jnp.zeros_like(dgc_ref)
            dcw_ref[...] = jnp.zeros_like(dcw_ref)

        dga = []
        for h in range(HEADS):
            hs = slice(VDIM * h, VDIM * (h + 1))
            oh, za, gah, dya = o_ref[0, h], za_ref[:, hs], ga_ref[:, hs], dya_ref[:, hs]
            sg = _sigmoid(za)
            on, ro = _rms(oh, gah)
            don = dya * (za * sg)
            dpb_ref[:, hs] = (dya * on * (sg * (1.0 + za * (1.0 - sg)))).astype(BF16)
            do, dg = _rms_bwd(don, oh, ro, gah)
            dga.append(jnp.sum(dg, axis=0, keepdims=True))
            dob = do.astype(BF16)
            do_ref[0, h] = dob
            dl_ref[0, h] = _row_of(jnp.sum(dob.astype(F32) * oh, axis=1, keepdims=True), tm)
        dga_ref[...] += jnp.concatenate(dga, axis=1)

        cat = lambda a, b: jnp.concatenate([a[...], b[...]], axis=0)
        cch = cat(cc_ref, ccn_ref)
        chh = cat(ch_ref, chn_ref)
        cb = cat(cb_ref, cbn_ref)
        zc = cat(zc_ref, zcn_ref)
        dy = cat(dyc_ref, dycn_ref)
        first = i % nt == 0
        last = i % nt == nt - 1
        cc = cch * chh
        prev = jnp.where(first, mc_ref[8:16, :] * mh_ref[8:16, :], ccp_ref[...] * chp_ref[...])
        cc1, cc2 = _shift_rows(cc, prev, ext)
        w0, w1, w2 = cw_ref[0:1, :], cw_ref[1:2, :], cw_ref[2:3, :]
        dw = w0 * cc2 + w1 * cc1 + w2 * cc
        yc = cb * dw
        rg = lax.rsqrt(_group_mean(yc * yc, gm_ref[...]) + EPS)
        ych = yc * rg
        gcv = gc_ref[...]
        sg = _sigmoid(zc)
        dycn = dy * (zc * sg)
        dzc = dy * (ych * gcv) * (sg * (1.0 + zc * (1.0 - sg)))
        dgc_ref[...] += jnp.sum((dycn * ych)[:tm], axis=0, keepdims=True)
        dycg = dycn * gcv
        dyc = rg * (dycg - ych * _group_mean(dycg * ych, gm_ref[...]))
        rid = lax.broadcasted_iota(jnp.int32, (ext, CONV_W), 0)
        ddw = jnp.where(jnp.logical_and(last, rid >= tm), 0.0, dyc * cb)
        dcb = dyc * dw
        dcc = w2 * ddw + w1 * pltpu.roll(ddw, ext - 1, 0) + w0 * pltpu.roll(ddw, ext - 2, 0)
        dpb_ref[:, 512:1024] = dcb[:tm].astype(BF16)
        dpb_ref[:, 1024:1536] = (dcc * chh)[:tm].astype(BF16)
        dpb_ref[:, 1536:2048] = (dcc * cch)[:tm].astype(BF16)
        dpb_ref[:, 2048:2560] = dzc[:tm].astype(BF16)
        rs = lambda a: jnp.sum(a[:tm], axis=0, keepdims=True)
        dcw_ref[0:1, :] += rs(ddw * cc2)
        dcw_ref[1:2, :] += rs(ddw * cc1)
        dcw_ref[2:3, :] += rs(ddw * cc)

        @pl.when(first)
        def _():
            d0, d1 = ddw[0:1, :], ddw[1:2, :]
            r8 = lax.broadcasted_iota(jnp.int32, (8, CONV_W), 0)
            dccm_ref[0] = jnp.where(r8 == 7, w1 * d0 + w0 * d1, jnp.where(r8 == 6, w0 * d0, 0.0))

    row = lambda j: pl.BlockSpec((tm, 512), lambda i: (i, j))
    prv = lambda j: pl.BlockSpec((8, 512), lambda i: (prev_idx(i), j))
    nxt = lambda j: pl.BlockSpec((8, 512), lambda i: (next_idx(i), j))
    mblk = lambda j: pl.BlockSpec((N_META, 512), lambda i: (0, j))
    full = lambda a: pl.BlockSpec(a.shape, lambda i: (0,) * a.ndim)
    hb = lambda w: pl.BlockSpec((1, HEADS, tm, w), lambda i: (i // nt, 0, i % nt, 0))
    acc = lambda rr: pl.BlockSpec((rr, 512), lambda i: (0, 0))
    return pl.pallas_call(
        body, name="gate_bwd", grid=(nb * nt,),
        in_specs=[row(0), row(1), nxt(1), hb(VDIM),
                  row(BLK_ZA), row(BLK_CB), nxt(BLK_CB), row(BLK_CC), prv(BLK_CC), nxt(BLK_CC),
                  row(BLK_CH), prv(BLK_CH), nxt(BLK_CH), row(BLK_ZC), nxt(BLK_ZC),
                  mblk(BLK_CC), mblk(BLK_CH), full(conv_w), full(ga), full(gc), full(gmat)],
        out_specs=[pl.BlockSpec((tm, 2560), lambda i: (i, 0)), hb(VDIM),
                   pl.BlockSpec((1, HEADS, 1, tm), lambda i: (i // nt, 0, 0, i % nt)),
                   pl.BlockSpec((1, 8, 512), lambda i: (i // nt, 0, 0)),
                   acc(1), acc(1), acc(8)],
        out_shape=[jax.ShapeDtypeStruct((r, 2560), BF16), jax.ShapeDtypeStruct((nb, HEADS, s, VDIM), BF16),
                   jax.ShapeDtypeStruct((nb, HEADS, 1, s), F32), jax.ShapeDtypeStruct((nb, 8, 512), F32),
                   jax.ShapeDtypeStruct((1, 512), F32), jax.ShapeDtypeStruct((1, 512), F32),
                   jax.ShapeDtypeStruct((8, 512), F32)],
        compiler_params=_cparams("arbitrary"),
    )(dycat, dycat, dycat, o, p, p, p, p, p, p, p, p, p, p, p, pm, pm, conv_w, ga, gc, gmat)


class _StagedReduce:
    LOC, PRE_S, PRE_R, ICI_S, ICI_R, POST_S, POST_R, OUT, N_SEM = 0, 1, 2, 3, 6, 9, 10, 11, 12

    def __init__(self, shard_shape):
        self.half = (shard_shape[0] // 2, shard_shape[1])

    def scratch(self):
        h = self.half
        return [pltpu.VMEM((4,) + h, F32), pltpu.VMEM((4,) + h, F32), pltpu.VMEM((4,) + h, BF16),
                pltpu.VMEM((3,) + h, BF16), pltpu.VMEM(h, F32), pltpu.SemaphoreType.DMA((self.N_SEM,))]

    def run(self, stage, pin, gout, scr):
        own, sib, wire, rbuf, fin, sems = scr
        r2 = self.half[0]
        x, y, c = lax.axis_index("x"), lax.axis_index("y"), lax.axis_index("c")
        mine = 2 * x + y
        sibling = (x, y, 1 - c)
        chips = [(1 - x, y), (x, 1 - y), (1 - x, 1 - y)]
        rows = lambda half: pl.ds(pl.multiple_of(half * r2, r2), r2)
        mesh = pl.DeviceIdType.MESH

        loc = pltpu.make_async_copy(pin.at[:, rows(c), :], own, sems.at[self.LOC])
        pre = pltpu.make_async_remote_copy(
            src_ref=pin.at[:, rows(1 - c), :], dst_ref=sib, send_sem=sems.at[self.PRE_S],
            recv_sem=sems.at[self.PRE_R], device_id=sibling, device_id_type=mesh)

        def ici(j):
            px, py = chips[j]
            return pltpu.make_async_remote_copy(
                src_ref=wire.at[2 * px + py], dst_ref=rbuf.at[j], send_sem=sems.at[self.ICI_S + j],
                recv_sem=sems.at[self.ICI_R + j], device_id=(px, py, c), device_id_type=mesh)

        def post(half):
            return pltpu.make_async_remote_copy(
                src_ref=fin, dst_ref=gout.at[rows(half), :], send_sem=sems.at[self.POST_S],
                recv_sem=sems.at[self.POST_R], device_id=sibling, device_id_type=mesh)

        keep = pltpu.make_async_copy(fin, gout.at[rows(c), :], sems.at[self.OUT])
        if stage == 0:
            loc.start()
            pre.start()
        elif stage == 1:
            loc.wait()
            pre.wait_recv()
            for blk in range(4):
                tot = own[blk] + sib[blk]
                own[blk] = tot
                wire[blk] = tot.astype(BF16)
            for j in range(3):
                ici(j).start()
        elif stage == 2:
            for j in range(3):
                ici(j).wait_recv()
            tot = own[mine]
            for j in range(3):
                tot = tot + rbuf[j].astype(F32)
            fin[...] = tot
            post(c).start()
            keep.start()
        else:
            post(1 - c).wait_recv()
            pre.wait_send()
            for j in range(3):
                ici(j).wait_send()
            post(c).wait_send()
            keep.wait()


def _attn_bwd(q, k, v, do, lse, delta, km, vm, early, nb, s, t):
    n = s // t
    ne = len(early)
    reds = [_StagedReduce(a.shape[1:]) for a in early]
    n_steps = HEADS * nb
    assert n_steps >= 4

    def body(q_ref, k_ref, v_ref, do_ref, lse_ref, dl_ref, km_ref, vm_ref, *rest):
        pin_refs, rest = rest[:ne], rest[ne:]
        dq_ref, dk_ref, dv_ref, dkm_ref, dvm_ref = rest[:5]
        gout_refs, (p_scr, ds_scr, dq_acc), red_scr = rest[5:5 + ne], rest[5 + ne:8 + ne], rest[8 + ne:]
        b = pl.program_id(1)
        step = pl.program_id(0) * nb + b
        for stage, at in enumerate((0, 1, n_steps - 2, n_steps - 1)):
            @pl.when(step == at)
            def _(stage=stage):
                for a, red in enumerate(reds):
                    red.run(stage, pin_refs[a], gout_refs[a], red_scr[6 * a:6 * a + 6])

        @pl.when(b == 0)
        def _():
            dkm_ref[...] = jnp.zeros_like(dkm_ref)
            dvm_ref[...] = jnp.zeros_like(dvm_ref)

        kr = lax.broadcasted_iota(jnp.int32, (t, t), 0)
        qc = lax.broadcasted_iota(jnp.int32, (t, t), 1)
        km_v, vm_v = km_ref[0, 0], vm_ref[0, 0]
        ptm = jnp.exp(_dot_nt(km_v, q_ref[0, 0]) - lse_ref[0, 0])
        dstm = (ptm * (_dot_nt(vm_v, do_ref[0, 0]) - dl_ref[0, 0])).astype(BF16)
        dkm_ref[0] += _dot(dstm, q_ref[0, 0])
        dvm_ref[0] += _dot(ptm.astype(BF16), do_ref[0, 0])
        dq_acc[...] = _dot_tn(dstm, km_v)
        def tiles(j):
            slot = j % 2
            kj = k_ref[0, 0, j * t:(j + 1) * t, :]
            vj = v_ref[0, 0, j * t:(j + 1) * t, :]
            def products(i):
                cs = slice(i * t, (i + 1) * t)
                return _dot_nt(kj, q_ref[0, 0, cs, :]), _dot_nt(vj, do_ref[0, 0, cs, :])

            nxt, pending = products(j), None
            for i in range(j, n):
                cs = slice(i * t, (i + 1) * t)
                st, dpt = nxt
                if i + 1 < n:
                    nxt = products(i + 1)
                if i == j:
                    st = jnp.where(kr <= qc, st, NEG_INF)
                pt = jnp.exp(st - lse_ref[0, 0, :, cs])
                dst = (pt * (dpt - dl_ref[0, 0, :, cs])).astype(BF16)
                p_scr[slot, :, cs] = pt.astype(BF16)
                ds_scr[slot, :, cs] = dst
                if pending is not None:
                    dq_acc[pending[0], :] += _dot_tn(pending[1], kj)
                pending = (cs, dst)
            dq_acc[pending[0], :] += _dot_tn(pending[1], kj)

        for j in range(n):
            slot = j % 2
            tiles(j)
            dv_ref[0, 0, j * t:(j + 1) * t, :] = _dot(p_scr[slot, :, j * t:s], do_ref[0, 0, j * t:s, :]).astype(BF16)
            dk_ref[0, 0, j * t:(j + 1) * t, :] = _dot(ds_scr[slot, :, j * t:s], q_ref[0, 0, j * t:s, :]).astype(BF16)
        dq_ref[0, 0] = dq_acc[...].astype(BF16)

    big = lambda w: pl.BlockSpec((1, 1, s, w), lambda h, b: (b, h, 0, 0))
    rowv = pl.BlockSpec((1, 1, 1, s), lambda h, b: (b, h, 0, 0))
    mk = lambda w: pl.BlockSpec((1, 1, N_META, w), lambda h, b: (0, h, 0, 0))
    mo = lambda w: pl.BlockSpec((1, N_META, w), lambda h, b: (h, 0, 0))
    return pl.pallas_call(
        body, name="attn_bwd", grid=(HEADS, nb),
        in_specs=[big(QK_PAD), big(QK_PAD), big(VDIM), big(VDIM), rowv, rowv, mk(QK_PAD), mk(VDIM)]
        + [pl.BlockSpec(memory_space=pl.ANY)] * ne,
        out_specs=[big(QK_PAD), big(QK_PAD), big(VDIM), mo(QK_PAD), mo(VDIM)]
        + [pl.BlockSpec(memory_space=pl.ANY)] * ne,
        out_shape=[jax.ShapeDtypeStruct((nb, HEADS, s, QK_PAD), BF16),
                   jax.ShapeDtypeStruct((nb, HEADS, s, QK_PAD), BF16),
                   jax.ShapeDtypeStruct((nb, HEADS, s, VDIM), BF16),
                   jax.ShapeDtypeStruct((HEADS, N_META, QK_PAD), F32),
                   jax.ShapeDtypeStruct((HEADS, N_META, VDIM), F32)]
        + [jax.ShapeDtypeStruct(a.shape[1:], F32) for a in early],
        scratch_shapes=[pltpu.VMEM((2, t, s), BF16), pltpu.VMEM((2, t, s), BF16), pltpu.VMEM((s, QK_PAD), F32)]
        + [sc for red in reds for sc in red.scratch()],
        compiler_params=_cparams("arbitrary", "arbitrary"),
    )(q, k, v, do, lse, delta, km, vm, *early)


def _up_bwd(dq, dk, dv, dkm, dvm, p, pm, tabs, tabs_m, wq_p, wkv_p, gq, gkv, nb, s, tm):
    nt = s // tm
    n = nb * nt
    c_t, sa_t, sb_t = tabs
    cm_t, sam_t, sbm_t = tabs_m

    def kv_path(dkh, dvh, pa, c, sa, sb, wkv, gkvv):
        dkpe = dkh[0][:, NOPE:]
        for h in range(1, HEADS):
            dkpe = dkpe + dkh[h][:, NOPE:]
        dkr = _rope_bwd(dkpe, c, sa, sb)
        dkv = jnp.concatenate([d[:, :NOPE] for d in dkh] + list(dvh), axis=1).astype(BF16)
        ckv = pa[:, Q_RANK:Q_RANK + KV_RANK]
        kvn, rkv = _rms(ckv, gkvv)
        dckv, dg = _rms_bwd(_dot(dkv, wkv), ckv, rkv, gkvv)
        return dckv, dkr, kvn.astype(BF16), dkv, jnp.sum(dg, axis=0, keepdims=True)

    def body(dq_ref, dk_ref, dv_ref, pa_ref, c_ref, sa_ref, sb_ref,
             dkm_ref, dvm_ref, pam_ref, cm_ref, sam_ref, sbm_ref,
             wq_ref, wkv_ref, gq_ref, gkv_ref,
             dpa_ref, dpam_ref, pq_ref, pkv_ref, dgq_ref, dgkv_ref, dwq_ref, dwkv_ref):
        i = pl.program_id(0)

        @pl.when(i == 0)
        def _():
            dwq_ref[...] = jnp.zeros_like(dwq_ref)
            dwkv_ref[...] = jnp.zeros_like(dwkv_ref)
            dgq_ref[...] = jnp.zeros_like(dgq_ref)
            dgkv_ref[...] = jnp.zeros_like(dgkv_ref)

        @pl.when(i < n)
        def _():
            c, sa, sb = c_ref[...], sa_ref[...], sb_ref[...]
            pa = pa_ref[...]
            parts = []
            for h in range(HEADS):
                dqh = dq_ref[0, h].astype(F32) * ATTN_SCALE
                parts += [dqh[:, :NOPE], _rope_bwd(dqh[:, NOPE:], c, sa, sb)]
            dql = jnp.concatenate(parts, axis=1).astype(BF16)
            cq = pa[:, 0:Q_RANK]
            gqv = gq_ref[...]
            qn, rq = _rms(cq, gqv)
            dwq_ref[...] += _dot_tn(dql, qn.astype(BF16))
            dcq, dg = _rms_bwd(_dot(dql, wq_ref[...]), cq, rq, gqv)
            dgq_ref[...] += jnp.sum(dg, axis=0, keepdims=True)
            dckv, dkr, kvn, dkv, dgk = kv_path([dk_ref[0, h].astype(F32) for h in range(HEADS)],
                                               [dv_ref[0, h].astype(F32) for h in range(HEADS)],
                                               pa, c, sa, sb, wkv_ref[...], gkv_ref[...])
            dwkv_ref[...] += _dot_tn(dkv, kvn)
            dgkv_ref[...] += dgk
            dpa_ref[...] = jnp.concatenate([dcq, dckv, dkr], axis=1).astype(BF16)

        @pl.when(i == n)
        def _():
            dckv, dkr, kvn, dkv, dgk = kv_path([dkm_ref[h] for h in range(HEADS)],
                                               [dvm_ref[h] for h in range(HEADS)],
                                               pam_ref[...], cm_ref[...], sam_ref[...], sbm_ref[...],
                                               wkv_ref[...], gkv_ref[...])
            dwkv_ref[...] += _dot_tn(dkv, kvn)
            dgkv_ref[...] += dgk
            dpam_ref[...] = jnp.concatenate([jnp.zeros((N_META, Q_RANK), F32), dckv, dkr], axis=1)
            for h in range(HEADS):
                pq_ref[h] = dwq_ref[QK_PAD * h:QK_PAD * h + NOPE + ROPE, :]
                pkv_ref[h, 0:NOPE, :] = dwkv_ref[NOPE * h:NOPE * (h + 1), :]
                pkv_ref[h, NOPE:NOPE + VDIM, :] = dwkv_ref[512 + VDIM * h:512 + VDIM * (h + 1), :]

    cl = lambda i: jnp.minimum(i, n - 1)
    hb = lambda w: pl.BlockSpec((1, HEADS, tm, w), lambda i: (cl(i) // nt, 0, cl(i) % nt, 0))
    tab = pl.BlockSpec((tm, 128), lambda i: (cl(i) % nt, 0))
    full = lambda a: pl.BlockSpec(a.shape, lambda i: (0,) * a.ndim)
    const = lambda shape: pl.BlockSpec(shape, lambda i: (0,) * len(shape))
    return pl.pallas_call(
        body, name="up_bwd", grid=(n + 1,),
        in_specs=[hb(QK_PAD), hb(QK_PAD), hb(VDIM), pl.BlockSpec((tm, 512), lambda i: (cl(i), 0)), tab, tab, tab,
                  full(dkm), full(dvm), pl.BlockSpec((N_META, 512), lambda i: (0, 0)),
                  full(cm_t), full(sam_t), full(sbm_t), full(wq_p), full(wkv_p), full(gq), full(gkv)],
        out_specs=[pl.BlockSpec((tm, 512), lambda i: (cl(i), 0)), const((N_META, 512)),
                   const((HEADS, NOPE + ROPE, Q_RANK)), const((HEADS, NOPE + VDIM, KV_RANK)),
                   const((1, Q_RANK)), const((1, KV_RANK))],
        out_shape=[jax.ShapeDtypeStruct((nb * s, 512), BF16), jax.ShapeDtypeStruct((N_META, 512), F32),
                   jax.ShapeDtypeStruct((HEADS, NOPE + ROPE, Q_RANK), F32),
                   jax.ShapeDtypeStruct((HEADS, NOPE + VDIM, KV_RANK), F32),
                   jax.ShapeDtypeStruct((1, Q_RANK), F32), jax.ShapeDtypeStruct((1, KV_RANK), F32)],
        scratch_shapes=[pltpu.VMEM((HEADS * QK_PAD, Q_RANK), F32), pltpu.VMEM((1024, KV_RANK), F32)],
        compiler_params=_cparams("arbitrary"),
    )(dq, dk, dv, p, c_t, sa_t, sb_t, dkm, dvm, pm, cm_t, sam_t, sbm_t, wq_p, wkv_p, gq, gkv)


def _in_bwd(x2d, dh2, dpa, dpb, meta, dpam, dccm, pm, w_in_p, norm_g, nb, s, tm):
    nt = s // tm
    n = nb * nt

    def body(x_ref, dh_ref, dpa_ref, dpb_ref, mt_ref, dpam_ref, dccm_ref, mc_ref, mh_ref, w_ref, g_ref,
             gx_ref, gm_ref, dw_hbm, dg_ref, acc_ref, sems):
        i = pl.program_id(0)

        @pl.when(i == 0)
        def _():
            acc_ref[...] = jnp.zeros_like(acc_ref)
            dg_ref[...] = jnp.zeros_like(dg_ref)

        def rows(x, dp, dres):
            g = g_ref[...]
            dpb16 = dp.astype(BF16)
            du = _dot(dpb16, w_ref[...])
            u, r1 = _rms(x, g)
            acc_ref[...] += _dot_tn(dpb16, u.astype(BF16))
            dx, dg = _rms_bwd(du, x, r1, g)
            dg_ref[...] += jnp.sum(dg, axis=0, keepdims=True)
            return dx if dres is None else dx + dres

        @pl.when(i < n)
        def _():
            dp = jnp.concatenate([dpa_ref[...], dpb_ref[...]], axis=1)
            gx_ref[...] = rows(x_ref[...], dp, dh_ref[...])

        @pl.when(i == n)
        def _():
            dcc = dccm_ref[0]
            for b in range(1, nb):
                dcc = dcc + dccm_ref[b]
            z8 = jnp.zeros((8, CONV_W), F32)
            dc = jnp.concatenate([z8, dcc * mh_ref[8:16, :]], axis=0)
            dh = jnp.concatenate([z8, dcc * mc_ref[8:16, :]], axis=0)
            z = jnp.zeros((N_META, CONV_W), F32)
            dp = jnp.concatenate([dpam_ref[...], z, z, dc, dh, z], axis=1)
            gm_ref[...] = rows(mt_ref[...], dp, None)
            per = IN_DIM // 4
            cps = [pltpu.make_async_copy(acc_ref.at[0:448], dw_hbm.at[0, 0:448], sems.at[0]),
                   pltpu.make_async_copy(acc_ref.at[512:per + 64], dw_hbm.at[0, 448:per], sems.at[1])]
            for qq in range(1, 4):
                cps.append(pltpu.make_async_copy(acc_ref.at[per * qq + 64:per * (qq + 1) + 64], dw_hbm.at[qq],
                                                 sems.at[qq + 1]))
            for cp in cps:
                cp.start()
            for cp in cps:
                cp.wait()

    cl = lambda i: jnp.minimum(i, n - 1)
    row = lambda w: pl.BlockSpec((tm, w), lambda i: (cl(i), 0))
    full = lambda a: pl.BlockSpec(a.shape, lambda i: (0,) * a.ndim)
    mblk = lambda j: pl.BlockSpec((N_META, 512), lambda i: (0, j))
    return pl.pallas_call(
        body, name="in_bwd", grid=(n + 1,),
        in_specs=[row(D_MODEL), row(D_MODEL), row(512), row(2560), full(meta), full(dpam), full(dccm),
                  mblk(BLK_CC), mblk(BLK_CH), full(w_in_p), full(norm_g)],
        out_specs=[row(D_MODEL), pl.BlockSpec((N_META, D_MODEL), lambda i: (0, 0)),
                   pl.BlockSpec(memory_space=pl.ANY), pl.BlockSpec((1, D_MODEL), lambda i: (0, 0))],
        out_shape=[jax.ShapeDtypeStruct((nb * s, D_MODEL), F32), jax.ShapeDtypeStruct((N_META, D_MODEL), F32),
                   jax.ShapeDtypeStruct((4, IN_DIM // 4, D_MODEL), F32), jax.ShapeDtypeStruct((1, D_MODEL), F32)],
        scratch_shapes=[pltpu.VMEM((IN_PAD, D_MODEL), F32), pltpu.SemaphoreType.DMA((5,))],
        compiler_params=_cparams("arbitrary"),
    )(x2d, dh2, dpa, dpb, meta, dpam, dccm, pm, pm, w_in_p, norm_g)


def _gather_weights(w_in_shard, split, pieces, out_rows, whole, zero_fills):
    ns, nw, nz = len(split), len(whole), len(zero_fills)
    flat = [(a, pc) for a in range(ns) for pc in pieces[a]]
    nk = len(flat)
    hh = HEAD_ROWS // 2
    hc = hh // HEAD_CHUNKS
    assert hc * HEAD_CHUNKS == hh and hc % 16 == 0

    def body(*refs):
        ins, wins, zins = refs[1:1 + ns], refs[1 + ns:1 + ns + nw], refs[1 + ns + nw:1 + ns + nw + nz]
        n_in = 1 + ns + nw + nz
        head_ref, shard16 = refs[n_in], refs[n_in + 1]
        outs, wouts = refs[n_in + 2:n_in + 2 + ns], refs[n_in + 2 + ns:n_in + 2 + ns + nw]
        scr = refs[n_in + 2 + ns + nw:]
        stage = scr[:ns]
        (send_sems, recv_sems, fwd_send, fwd_recv, loc_sems, w_send, w_recv, w_loc, z_sems,
         h_send, h_recv, h_relay, h_pass) = scr[ns:]
        x, y, c = lax.axis_index("x"), lax.axis_index("y"), lax.axis_index("c")
        mine = 2 * x + y
        chips = [(1 - x, y), (x, 1 - y), (1 - x, 1 - y)]
        chip_of = [2 * px + py for px, py in chips]
        shard16[...] = refs[0][...].astype(BF16)
        for a in range(ns):
            stage[a][...] = ins[a][...].astype(BF16)

        def head_rows(half, n):
            return pl.ds(pl.multiple_of(half * hh + n * hc, 16), hc)

        def head_copy(j, n):
            px, py = chips[j]
            return pltpu.make_async_remote_copy(
                src_ref=shard16.at[head_rows(c, n)], dst_ref=head_ref.at[head_rows(c, n)], send_sem=h_send.at[j, n],
                recv_sem=h_recv.at[n], device_id=(px, py, c), device_id_type=pl.DeviceIdType.MESH)

        def head_relay(n):
            ref = head_ref.at[head_rows(c, n)]
            return pltpu.make_async_remote_copy(
                src_ref=ref, dst_ref=ref, send_sem=h_relay.at[n], recv_sem=h_recv.at[n],
                device_id=(1, 1, c), device_id_type=pl.DeviceIdType.MESH)

        def head_pass(half, n):
            ref = head_ref.at[head_rows(half, n)]
            return pltpu.make_async_remote_copy(
                src_ref=ref, dst_ref=ref, send_sem=h_pass.at[0, n], recv_sem=h_pass.at[1, n],
                device_id=(x, y, 1 - c), device_id_type=pl.DeviceIdType.MESH)

        head_ref[HEAD_ROWS:IN_HEAD, :] = jnp.zeros((IN_HEAD - HEAD_ROWS, D_MODEL), BF16)

        @pl.when(mine == 0)
        def _():
            for n in range(HEAD_CHUNKS):
                for j in range(2):
                    head_copy(j, n).start()
            head_ref[0:HEAD_ROWS, :] = shard16[0:HEAD_ROWS, :]

        def src(k):
            a, (s0, nr, _, _, _, _) = flat[k]
            return stage[a].at[s0:s0 + nr]

        def dst(k, q):
            a, (_, nr, per, first, rest, _) = flat[k]
            row = per * q + first + (rest - first) * jnp.minimum(q, 1)
            return outs[a].at[pl.ds(pl.multiple_of(row, 16), nr)]

        def ici(k, j, q):
            px, py = chips[j]
            return pltpu.make_async_remote_copy(
                src_ref=src(k), dst_ref=dst(k, q), send_sem=send_sems.at[k, j], recv_sem=recv_sems.at[k, j],
                device_id=(px, py, c), device_id_type=pl.DeviceIdType.MESH)

        def fwd(k, j):
            ref = dst(k, chip_of[j])
            return pltpu.make_async_remote_copy(
                src_ref=ref, dst_ref=ref, send_sem=fwd_send.at[k, j], recv_sem=fwd_recv.at[k, j],
                device_id=(x, y, 1 - c), device_id_type=pl.DeviceIdType.MESH)

        def wcopy(b, j, q):
            px, py = chips[j]
            return pltpu.make_async_remote_copy(
                src_ref=wins[b], dst_ref=wouts[b].at[q], send_sem=w_send.at[b, j], recv_sem=w_recv.at[b, j],
                device_id=(px, py, c), device_id_type=pl.DeviceIdType.MESH)

        local = [pltpu.make_async_copy(src(k), dst(k, mine), loc_sems.at[k]) for k in range(nk)]
        local += [pltpu.make_async_copy(wins[b], wouts[b].at[mine], w_loc.at[b]) for b in range(nw)]
        for z, (a, _, row0) in enumerate(zero_fills):
            local.append(pltpu.make_async_copy(zins[z], outs[a].at[row0:row0 + zins[z].shape[0]], z_sems.at[z]))
        wsends = [wcopy(b, j, mine) for b in range(nw) for j in range(3)]
        for cp in local + wsends:
            cp.start()

        for half in (0, 1):
            @pl.when(c == half)
            def _(half=half):
                my_k = [k for k in range(nk) if flat[k][1][5] == half]
                other_k = [k for k in range(nk) if flat[k][1][5] != half]
                sends = [ici(k, j, mine) for k in my_k for j in range(3)]
                for cp in sends:
                    cp.start()
                passed = []
                for k in my_k:
                    for j in range(3):
                        ici(k, j, chip_of[j]).wait_recv()
                        cp = fwd(k, j)
                        cp.start()
                        passed.append(cp)
                for k in other_k:
                    for j in range(3):
                        fwd(k, j).wait_recv()
                for cp in sends + passed:
                    cp.wait_send()

        for b in range(nw):
            for j in range(3):
                wcopy(b, j, chip_of[j]).wait_recv()
        for cp in wsends:
            cp.wait_send()
        for cp in local:
            cp.wait()

        @pl.when(mine == 0)
        def _():
            for n in range(HEAD_CHUNKS):
                for j in range(2):
                    head_copy(j, n).wait_send()

        @pl.when(mine != 0)
        def _():
            hands_on = mine == 2 - c
            for n in range(HEAD_CHUNKS):
                head_copy(0, n).wait_recv()

                @pl.when(hands_on)
                def _(n=n):
                    head_relay(n).start()

                head_pass(c, n).start()
            for n in range(HEAD_CHUNKS):
                head_pass(1 - c, n).wait_recv()
                head_pass(c, n).wait_send()

                @pl.when(hands_on)
                def _(n=n):
                    head_relay(n).wait_send()

    vmem = pl.BlockSpec(memory_space=pltpu.VMEM)
    dma = pltpu.SemaphoreType.DMA
    zeros = [z for _, z, _ in zero_fills]
    return pl.pallas_call(
        body, name="gather_weights",
        in_specs=[vmem] * (1 + ns + nw + nz), out_specs=[vmem] * (2 + ns + nw),
        out_shape=([jax.ShapeDtypeStruct((IN_HEAD, D_MODEL), BF16), jax.ShapeDtypeStruct(w_in_shard.shape, BF16)]
                   + [jax.ShapeDtypeStruct((out_rows[a], split[a].shape[1]), BF16) for a in range(ns)]
                   + [jax.ShapeDtypeStruct((4,) + w.shape, w.dtype) for w in whole]),
        scratch_shapes=[pltpu.VMEM(a.shape, BF16) for a in split]
        + [dma((nk, 3)), dma((nk, 3)), dma((nk, 3)), dma((nk, 3)), dma((nk,)),
           dma((nw, 3)), dma((nw, 3)), dma((nw,)), dma((nz,)),
           dma((2, HEAD_CHUNKS)), dma((HEAD_CHUNKS,)), dma((HEAD_CHUNKS,)), dma((2, HEAD_CHUNKS))],
        compiler_params=pltpu.CompilerParams(vmem_limit_bytes=VMEM_LIMIT),
    )(w_in_shard, *split, *whole, *zeros)


def _reduce_grads(parts, small):
    n = len(parts)
    shapes = [a.shape[1:] for a in parts]
    halves = [(sh[0] // 2, sh[1]) for sh in shapes]

    def body(*refs):
        pin, sm_in = refs[:n], refs[n]
        gout, sm_out = refs[n + 1:2 * n + 1], refs[2 * n + 1]
        scr = refs[2 * n + 2:]
        own, sib, wire, rbuf = scr[:n], scr[n:2 * n], scr[2 * n:3 * n], scr[3 * n:4 * n]
        (sbuf, send_sems, recv_sems, loc_sems, pre_send, pre_recv, post_send, post_recv,
         sm_send, sm_recv) = scr[4 * n:]
        x, y, c = lax.axis_index("x"), lax.axis_index("y"), lax.axis_index("c")
        mine = 2 * x + y
        me = 4 * x + 2 * y + c
        sibling = (x, y, 1 - c)

        def rows(a, half):
            r2 = halves[a][0]
            return pl.ds(pl.multiple_of(half * r2, r2), r2)

        near = (jnp.where(c == 0, 1 - x, x), jnp.where(c == 0, y, 1 - y))
        far = (jnp.where(c == 0, x, 1 - x), jnp.where(c == 0, 1 - y, y))
        chip = lambda p: 2 * p[0] + p[1]
        blocks = [3 - mine, chip(near), chip(far), mine]

        def pre(a, k):
            q = blocks[k]
            return pltpu.make_async_remote_copy(
                src_ref=pin[a].at[q, rows(a, 1 - c), :], dst_ref=sib[a].at[q],
                send_sem=pre_send.at[a, q], recv_sem=pre_recv.at[a, q], device_id=sibling,
                device_id_type=pl.DeviceIdType.MESH)

        def ici(a, m):
            px, py = near if m < 2 else far
            return pltpu.make_async_remote_copy(
                src_ref=wire[a].at[blocks[m]], dst_ref=rbuf[a].at[m], send_sem=send_sems.at[a, m],
                recv_sem=recv_sems.at[a, m], device_id=(px, py, c), device_id_type=pl.DeviceIdType.MESH)

        def post(a, half):
            ref = gout[a].at[rows(a, half), :]
            return pltpu.make_async_remote_copy(
                src_ref=ref, dst_ref=ref, send_sem=post_send.at[a], recv_sem=post_recv.at[a],
                device_id=sibling, device_id_type=pl.DeviceIdType.MESH)

        def small_copy(kk):
            peer = (x ^ (kk >> 2), y ^ ((kk >> 1) & 1), c ^ (kk & 1))
            return pltpu.make_async_remote_copy(
                src_ref=sm_in, dst_ref=sbuf.at[kk], send_sem=sm_send.at[kk - 1], recv_sem=sm_recv.at[kk - 1],
                device_id=peer, device_id_type=pl.DeviceIdType.MESH)

        local = [[pltpu.make_async_copy(pin[a].at[blocks[k], rows(a, c), :], own[a].at[blocks[k]], loc_sems.at[a, k])
                  for k in range(4)] for a in range(n)]
        pres = [[pre(a, k) for k in range(4)] for a in range(n)]
        smalls = [small_copy(kk) for kk in range(1, 8)]
        for a in range(n):
            for k in range(4):
                local[a][k].start()
                pres[a][k].start()
        for cp in smalls:
            cp.start()
        sbuf[0] = sm_in[...]
        sends = []
        for a in range(n):
            for k in range(4):
                local[a][k].wait()
                pres[a][k].wait_recv()
                tot = own[a][blocks[k]] + sib[a][blocks[k]]
                if k == 2:
                    ici(a, 0).wait_recv()
                    tot = tot + rbuf[a][0].astype(F32)
                own[a][blocks[k]] = tot
                if k < 3:
                    wire[a][blocks[k]] = tot.astype(BF16)
                    cp = ici(a, k)
                    cp.start()
                    sends.append(cp)
        for cp in smalls:
            cp.wait_recv()
        total = sbuf[me]
        for d in range(1, 8):
            total = total + sbuf[me ^ d]
        sm_out[...] = total
        posts = []
        for a in range(n):
            fin = own[a][mine]
            for m in (1, 2):
                ici(a, m).wait_recv()
                fin = fin + rbuf[a][m].astype(F32)
            gout[a][rows(a, c), :] = fin
            cp = post(a, c)
            cp.start()
            posts.append(cp)
        for a in range(n):
            post(a, 1 - c).wait_recv()
        for cp in [cp for row in pres for cp in row] + sends + smalls + posts:
            cp.wait_send()

    vmem = pl.BlockSpec(memory_space=pltpu.VMEM)
    dma = pltpu.SemaphoreType.DMA
    return pl.pallas_call(
        body, name="reduce_grads",
        in_specs=[pl.BlockSpec(memory_space=pl.ANY)] * n + [vmem], out_specs=[vmem] * (n + 1),
        out_shape=[jax.ShapeDtypeStruct(sh, F32) for sh in shapes] + [jax.ShapeDtypeStruct(small.shape, F32)],
        scratch_shapes=([pltpu.VMEM((4,) + hs, F32) for hs in halves] + [pltpu.VMEM((4,) + hs, F32) for hs in halves]
                        + [pltpu.VMEM((4,) + hs, BF16) for hs in halves]
                        + [pltpu.VMEM((3,) + hs, BF16) for hs in halves]
                        + [pltpu.VMEM((8,) + small.shape, F32), dma((n, 3)), dma((n, 3)), dma((n, 4)),
                           dma((n, 4)), dma((n, 4)), dma((n,)), dma((n,)), dma((7,)), dma((7,))]),
        compiler_params=pltpu.CompilerParams(vmem_limit_bytes=VMEM_LIMIT),
    )(*parts, small)


def _adamw_update(w_ref, g_ref, m_ref, v_ref, d_ref, nm_ref, nv_ref):
    gv = g_ref[...]
    nm = ADAM_B1 * m_ref[...] + (1.0 - ADAM_B1) * gv
    nv = ADAM_B2 * v_ref[...] + (1.0 - ADAM_B2) * (gv * gv)
    m_hat = nm / (1.0 - ADAM_B1 ** ADAM_STEP)
    v_hat = nv / (1.0 - ADAM_B2 ** ADAM_STEP)
    d_ref[...] = -ADAM_LR * (m_hat / (jnp.sqrt(v_hat) + ADAM_EPS) + ADAM_WD * w_ref[...])
    nm_ref[...] = nm
    nv_ref[...] = nv


def _adamw_small(ws, gs, ms, vs):
    k = len(ws)

    def body(*refs):
        ins, outs = refs[:4 * k], refs[4 * k:]
        for a in range(k):
            _adamw_update(ins[a], ins[k + a], ins[2 * k + a], ins[3 * k + a], outs[a], outs[k + a], outs[2 * k + a])

    out = pl.pallas_call(
        body, name="adamw_small",
        out_shape=[jax.ShapeDtypeStruct(w.shape, F32) for w in ws] * 3,
        compiler_params=pltpu.CompilerParams(vmem_limit_bytes=VMEM_LIMIT),
    )(*ws, *gs, *ms, *vs)
    return out[:k], out[k:2 * k], out[2 * k:]


def _adamw(w, g, m, v, name):
    shape = w.shape
    w2, g2, m2, v2 = (a.reshape((-1, shape[-1])) for a in (w, g, m, v))

    def body(w_ref, g_ref, m_ref, v_ref, d_ref, nm_ref, nv_ref):
        _adamw_update(w_ref, g_ref, m_ref, v_ref, d_ref, nm_ref, nv_ref)

    rows, cols = w2.shape
    nblk = cols // 256 if cols % 256 == 0 and rows >= 64 else 1
    blk = pl.BlockSpec((rows, cols // nblk), lambda j: (0, j))
    out = pl.pallas_call(
        body, name=name, grid=(nblk,), in_specs=[blk] * 4, out_specs=[blk] * 3,
        out_shape=[jax.ShapeDtypeStruct(w2.shape, F32)] * 3,
        compiler_params=_cparams("parallel"),
    )(w2, g2, m2, v2)
    return tuple(a.reshape(shape) for a in out)


def kernel(x, meta_tokens, norm_g, w_in, q_norm_g, w_q_up, kv_norm_g, w_kv_up, conv_w, attn_out_g, conv_out_g, w_out, final_norm_g, loss_target, m_meta_tokens, m_norm_g, m_w_in, m_q_norm_g, m_w_q_up, m_kv_norm_g, m_w_kv_up, m_conv_w, m_attn_out_g, m_conv_out_g, m_w_out, m_final_norm_g, v_meta_tokens, v_norm_g, v_w_in, v_q_norm_g, v_w_q_up, v_kv_norm_g, v_w_kv_up, v_conv_w, v_attn_out_g, v_conv_out_g, v_w_out, v_final_norm_g):
    nb, s, _ = x.shape
    tm = min(ROW_TILE, s)
    ta = min(ATTN_TILE, s)
    assert s % tm == 0 and s % ta == 0 and tm % 16 == 0
    r = nb * s

    tr = lambda a: jnp.transpose(a[0])
    w_head, w_in_shard, wq_p, wkv_p, g_cw, g_meta = _gather_weights(
        tr(w_in), [tr(w_q_up), tr(w_kv_up)],
        [W_Q_PIECES, W_KV_PIECES], [HEADS * QK_PAD, 1024],
        [jnp.transpose(conv_w, (1, 0, 2)), meta_tokens],
        [(0, jnp.zeros((64, Q_RANK), BF16), QK_PAD * h + NOPE + ROPE) for h in range(HEADS)])
    conv_f = jnp.transpose(g_cw[:, :, 0, :], (1, 0, 2)).reshape(3, CONV_W)
    meta_f = jnp.transpose(g_meta, (1, 0, 2)).reshape(N_META, D_MODEL)

    c_all, sa_all, sb_all = _rope_tables(N_META + s)
    tabs_m = (c_all[:N_META], sa_all[:N_META], sb_all[:N_META])
    tabs = (c_all[N_META:], sa_all[N_META:], sb_all[N_META:])
    gid = np.arange(CONV_W) // CONV_GROUP
    gmat = jnp.asarray(np.where(gid[:, None] == gid[None, :], 1.0 / CONV_GROUP, 0.0), BF16)
    ga, gc = attn_out_g, conv_out_g
    gf = final_norm_g.reshape(1, D_MODEL)

    x2d = x.reshape(r, D_MODEL)
    tgt2d = loss_target.reshape(r, D_MODEL)

    ph, q, k, v, pmh, km, vm, w_out_f, w_in_part = _fwd_proj(
        x2d, meta_f, tabs, tabs_m, norm_g, w_head, q_norm_g, wq_p, kv_norm_g, wkv_p, w_out[0].astype(BF16),
        w_in_shard, nb, s, tm)
    o, lse, w_in_p = _attn_fwd(q, k, v, km, vm, w_in_shard, w_in_part, nb, s, ta)
    dh2, dycat, dw_out, dgf, loss_acc, pt, pmt = _out_fwd_bwd(x2d, tgt2d, o, meta_f, norm_g, w_in_p, conv_f, ga, gc,
                                                              gmat, w_out_f, gf, nb, s, tm)
    dpb, do, delta, dccm, dga, dgc, dcw = _gate_bwd(dycat, o, pt, pmt, conv_f, ga, gc, gmat, nb, s, tm)
    p_out = dw_out.reshape(4, D_MODEL // 4, D_MODEL)
    dq, dk, dv, dkm, dvm, g_w_out = _attn_bwd(q, k, v, do, lse, delta, km, vm, [p_out], nb, s, ta)
    dpa, dpam, p_q, p_kv, dgq, dgkv = _up_bwd(dq, dk, dv, dkm, dvm, ph, pmh, tabs, tabs_m, wq_p, wkv_p,
                                              q_norm_g, kv_norm_g, nb, s, tm)
    gx, gmeta, p_in, dng = _in_bwd(x2d, dh2, dpa, dpb, meta_f, dpam, dccm, pmt, w_in_p, norm_g, nb, s, tm)

    flat =jnp.concatenate([dng.reshape(-1), dgq.reshape(-1), dgkv.reshape(-1), dga.reshape(-1), dgc.reshape(-1),
                            dgf.reshape(-1), dcw[:3].reshape(-1), gmeta.reshape(-1), loss_acc[0, 0:1]])
    n_small = flat.shape[0]
    rows_small = -(-n_small // 1024) * 8
    small = jnp.pad(flat, (0, rows_small * 128 - n_small)).reshape(rows_small, 128)
    g_w_in_t, g_w_q_t, g_w_kv_t, small_sum = _reduce_grads([p_in, p_q, p_kv], small)
    ssum = small_sum.reshape(-1)

    def take(off, n):
        return ssum[off:off + n], off + n

    off = 0
    g_norm, off = take(off, D_MODEL)
    g_qn, off = take(off, Q_RANK)
    g_kvn, off = take(off, KV_RANK)
    g_ga, off = take(off, CONV_W)
    g_gc, off = take(off, CONV_W)
    g_gf, off = take(off, D_MODEL)
    g_cw_all, off = take(off, 3 * CONV_W)
    g_meta_all, off = take(off, N_META * D_MODEL)
    loss = ssum[off]
    chip = 2 * lax.axis_index("x") + lax.axis_index("y")
    g_conv = lax.dynamic_slice(g_cw_all.reshape(3, CONV_W), (0, chip * 128), (3, 128))
    g_mt = lax.dynamic_slice(g_meta_all.reshape(N_META, D_MODEL), (0, chip * 256), (N_META, 256))

    grads = {
        "meta_tokens": g_mt, "norm_g": g_norm.reshape(1, -1), "w_in": g_w_in_t, "q_norm_g": g_qn.reshape(1, -1),
        "w_q_up": g_w_q_t, "kv_norm_g": g_kvn.reshape(1, -1), "w_kv_up": jnp.transpose(g_w_kv_t)[None],
        "conv_w": g_conv[None], "attn_out_g": g_ga.reshape(1, -1), "conv_out_g": g_gc.reshape(1, -1),
        "w_out": g_w_out[None], "final_norm_g": g_gf,
    }
    transposed = ("w_in", "w_q_up")
    weights = {
        "meta_tokens": (meta_tokens, m_meta_tokens, v_meta_tokens), "norm_g": (norm_g, m_norm_g, v_norm_g),
        "w_in": (w_in, m_w_in, v_w_in), "q_norm_g": (q_norm_g, m_q_norm_g, v_q_norm_g),
        "w_q_up": (w_q_up, m_w_q_up, v_w_q_up), "kv_norm_g": (kv_norm_g, m_kv_norm_g, v_kv_norm_g),
        "w_kv_up": (w_kv_up, m_w_kv_up, v_w_kv_up), "conv_w": (conv_w, m_conv_w, v_conv_w),
        "attn_out_g": (attn_out_g, m_attn_out_g, v_attn_out_g), "conv_out_g": (conv_out_g, m_conv_out_g, v_conv_out_g),
        "w_out": (w_out, m_w_out, v_w_out), "final_norm_g": (final_norm_g, m_final_norm_g, v_final_norm_g),
    }
    names = list(weights)
    small = [nme for nme in names if nme != "w_in"]

    def view(nme, a):
        if nme in transposed:
            return a if a.ndim == 2 else tr(a)
        if nme == "conv_w":
            return jnp.transpose(a.reshape(1, 3, -1), (1, 0, 2))
        if a.ndim == 3:
            return a[0]
        return a.reshape(1, -1) if a.ndim == 1 else a

    def unview(nme, a):
        if nme in transposed:
            return jnp.transpose(a)[None]
        if nme == "conv_w":
            return jnp.transpose(a, (1, 0, 2))
        return a.reshape(weights[nme][0].shape)

    res_small = _adamw_small(*[[view(nme, a) for nme, a in zip(small, col)] for col in (
        [weights[nme][0] for nme in small], [grads[nme] for nme in small],
        [weights[nme][1] for nme in small], [weights[nme][2] for nme in small])])
    w_, m_, v_ = weights["w_in"]
    res = _adamw(tr(w_), grads["w_in"], tr(m_), tr(v_), "adamw_w_in")
    upd = {"w_in": tuple(jnp.transpose(a)[None] for a in (grads["w_in"],) + res)}
    for j, nme in enumerate(small):
        upd[nme] = (unview(nme, view(nme, grads[nme])),) + tuple(unview(nme, r[j]) for r in res_small)
    grads = {nme: upd[nme][0] for nme in names}
    deltas, new_m, new_v = ([upd[nme][j] for nme in names] for j in (1, 2, 3))

    grad_x = gx.reshape(nb, s, D_MODEL)
    return (loss, grad_x, *[grads[nme] for nme in names], *deltas, *new_m, *new_v)
```

```python
import functools

import jax
import jax.numpy as jnp
import numpy as np
from jax import lax
from jax.experimental import pallas as pl
from jax.experimental.pallas import tpu as pltpu

F32 = jnp.float32
BF16 = jnp.bfloat16

D_MODEL = 1024
N_META = 16
HEADS = 4
NOPE = 128
ROPE = 64
VDIM = 128
QK_PAD = 256
Q_RANK = 256
KV_RANK = 128
CONV_W = 512
CONV_GROUP = 64
ROPE_THETA = 10000.0
EPS = 1e-6
ATTN_SCALE = (NOPE + ROPE) ** -0.5
IN_DIM = 3008
IN_PAD = 3072
HEAD_ROWS = Q_RANK + KV_RANK + ROPE
HEAD_CHUNKS = 7
IN_HEAD = 512
IN_TAIL = IN_PAD - IN_HEAD
BLK_ZA, BLK_CB, BLK_CC, BLK_CH, BLK_ZC = 0, 1, 2, 3, 4
NEG_INF = -1e30

ADAM_LR = 0.001
ADAM_B1 = 0.9
ADAM_B2 = 0.999
ADAM_EPS = 1e-08
ADAM_WD = 0.01
ADAM_STEP = 10

ROW_TILE = 512
ATTN_TILE = 256
VMEM_LIMIT = 56 * 1024 * 1024

NT = (((1,), (1,)), ((), ()))
TN = (((0,), (0,)), ((), ()))


def _cparams(*sem):
    return pltpu.CompilerParams(dimension_semantics=sem, vmem_limit_bytes=VMEM_LIMIT)


def _dot(a, b):
    return jnp.dot(a, b, preferred_element_type=F32)


def _dot_nt(a, b):
    return lax.dot_general(a, b, NT, preferred_element_type=F32)


def _dot_tn(a, b):
    return lax.dot_general(a, b, TN, preferred_element_type=F32)


def _rms(x, g):
    r = lax.rsqrt(jnp.mean(x * x, axis=-1, keepdims=True) + EPS)
    return x * r * g, r


def _rms_bwd(dy, x, r, g):
    xh = x * r
    dyg = dy * g
    dx = r * (dyg - xh * jnp.mean(dyg * xh, axis=-1, keepdims=True))
    return dx, dy * xh


def _sigmoid(z):
    return 1.0 / (1.0 + jnp.exp(-z))


def _rope(b, c, sa, sb):
    return b * c + pltpu.roll(b, 96, 1) * sa + pltpu.roll(b, 32, 1) * sb


def _rope_bwd(d, c, sa, sb):
    return d * c + pltpu.roll(d * sa, 32, 1) + pltpu.roll(d * sb, 96, 1)


def _group_mean(x, gmat):
    hi = x.astype(BF16)
    lo = (x - hi.astype(F32)).astype(BF16)
    return _dot(hi, gmat) + _dot(lo, gmat)


def _row_of(col, rows):
    return jnp.transpose(jnp.broadcast_to(col, (rows, 128)))[0:1, :]


def _rope_tables(n_pos):
    half = ROPE // 2
    inv_freq = (np.float32(1.0) / (np.float32(ROPE_THETA) ** (np.arange(half, dtype=np.float32) / np.float32(half))))
    ang = np.arange(n_pos, dtype=np.float32)[:, None] * inv_freq.astype(np.float32)[None, :]
    cos, sin = np.cos(ang).astype(np.float32), np.sin(ang).astype(np.float32)
    z = np.zeros((n_pos, half), np.float32)
    c = np.concatenate([cos, cos, z, z], axis=1)
    sa = np.concatenate([-sin, z, z, z], axis=1)
    sb = np.concatenate([z, sin, z, z], axis=1)
    return jnp.asarray(c), jnp.asarray(sa), jnp.asarray(sb)


W_IN_PIECES_1 = ((0, 80, 752, 0, 64, 0), (384, 64, 752, 384, 448, 1), (448, 16, 752, 512, 512, 1))
W_IN_PIECES_2 = ((80, 304, 752, 80, 144, 0), (464, 288, 752, 528, 528, 1))
W_Q_PIECES = ((0, 96, 256, 0, 0, 0), (96, 96, 256, 96, 96, 1))
W_KV_PIECES = ((0, 128, 128, 0, 0, 0), (128, 128, 128, 512, 512, 1))
W_OUT_PIECES = ((0, 128, 256, 0, 0, 0), (128, 128, 256, 128, 128, 1))


class _StagedGather:
    STAGES = 4

    @staticmethod
    def steps(n_steps):
        return (0, 5 * n_steps // 8, 7 * n_steps // 8, n_steps - 1)

    def __init__(self, pieces, zero_rows=None):
        self.pieces = pieces
        self.zero_rows = zero_rows

    def scratch(self):
        nk, dma = len(self.pieces), pltpu.SemaphoreType.DMA
        return [dma((nk, 3)), dma((nk, 3)), dma((nk, 3)), dma((nk, 3)), dma((nk,))]

    def vmem_scratch(self, shard_shape, out_shape):
        return [pltpu.VMEM(shard_shape, BF16), pltpu.VMEM(out_shape, BF16),
                pltpu.SemaphoreType.DMA((4 * len(self.pieces) + 2,))] + self.scratch()

    def run_vmem(self, stage, shard_ref, out_ref, scr):
        src_scr, land_scr, io_sems = scr[:3]
        spans = []
        for _, nr, per, first, rest, _ in self.pieces:
            spans += [(per * q + (first if q == 0 else rest), nr) for q in range(4)]
        if self.zero_rows is not None:
            spans.append(self.zero_rows)
        flush = [pltpu.make_async_copy(land_scr.at[r0:r0 + nr], out_ref.at[r0:r0 + nr], io_sems.at[n])
                 for n, (r0, nr) in enumerate(spans)]
        if stage == 0:
            load = pltpu.make_async_copy(shard_ref, src_scr, io_sems.at[len(spans)])
            load.start()
            if self.zero_rows is not None:
                r0, nr = self.zero_rows
                land_scr[r0:r0 + nr, :] = jnp.zeros((nr, land_scr.shape[1]), BF16)
            load.wait()
        if stage < self.STAGES:
            self.run(stage, src_scr, land_scr, scr[3:])
        for cp in flush:
            if stage == self.STAGES - 1:
                cp.start()
            if stage == self.STAGES:
                cp.wait()

    def run(self, stage, src_ref, out_ref, scr):
        send_sems, recv_sems, fwd_send, fwd_recv, loc_sems = scr
        pieces = self.pieces
        nk = len(pieces)
        x, y, c = lax.axis_index("x"), lax.axis_index("y"), lax.axis_index("c")
        mine = 2 * x + y
        chips = [(1 - x, y), (x, 1 - y), (1 - x, 1 - y)]
        chip_of = [2 * px + py for px, py in chips]
        mesh = pl.DeviceIdType.MESH

        def src(k):
            s0, nr = pieces[k][0], pieces[k][1]
            return src_ref.at[s0:s0 + nr]

        def dst(k, q):
            _, nr, per, first, rest, _ = pieces[k]
            row = per * q + first + (rest - first) * jnp.minimum(q, 1)
            return out_ref.at[pl.ds(pl.multiple_of(row, 16), nr)]

        def ici(k, j, q):
            px, py = chips[j]
            return pltpu.make_async_remote_copy(
                src_ref=src(k), dst_ref=dst(k, q), send_sem=send_sems.at[k, j], recv_sem=recv_sems.at[k, j],
                device_id=(px, py, c), device_id_type=mesh)

        def fwd(k, j):
            ref = dst(k, chip_of[j])
            return pltpu.make_async_remote_copy(
                src_ref=ref, dst_ref=ref, send_sem=fwd_send.at[k, j], recv_sem=fwd_recv.at[k, j],
                device_id=(x, y, 1 - c), device_id_type=mesh)

        def relay(k, half):
            ref = dst(k, chip_of[half])
            px, py = chips[1 - half]
            return pltpu.make_async_remote_copy(
                src_ref=ref, dst_ref=ref, send_sem=send_sems.at[k, 2], recv_sem=recv_sems.at[k, 2],
                device_id=(px, py, c), device_id_type=mesh)

        local = [pltpu.make_async_copy(src(k), dst(k, mine), loc_sems.at[k]) for k in range(nk)]
        if stage == 0:
            for cp in local:
                cp.start()
        if stage == 3:
            for cp in local:
                cp.wait()
        for half in (0, 1):
            @pl.when(c == half)
            def _(half=half):
                my_k = [k for k in range(nk) if pieces[k][5] == half]
                other_k = [k for k in range(nk) if pieces[k][5] != half]
                for k in my_k:
                    if stage == 0:
                        for j in range(2):
                            ici(k, j, mine).start()
                    elif stage == 1:
                        for j in (half, 1 - half):
                            ici(k, j, chip_of[j]).wait_recv()
                            if j == half:
                                relay(k, half).start()
                            fwd(k, j).start()
                    elif stage == 2:
                        ici(k, 2, chip_of[2]).wait_recv()
                        fwd(k, 2).start()
                    else:
                        for j in range(2):
                            ici(k, j, mine).wait_send()
                        relay(k, half).wait_send()
                        for j in range(3):
                            fwd(k, j).wait_send()
                if stage == 3:
                    for k in other_k:
                        for j in range(3):
                            fwd(k, j).wait_recv()


def _fwd_proj(x2d, meta, tabs, tabs_m, norm_g, w_head, q_norm_g, wq_p, kv_norm_g, wkv_p, w_out_shard, w_in_shard,
              nb, s, tm):
    nt = s // tm
    n = nb * nt
    n_steps = n + 1
    c_t, sa_t, sb_t = tabs
    cm_t, sam_t, sbm_t = tabs_m
    gat = _StagedGather(W_OUT_PIECES)
    gat_in = _StagedGather(W_IN_PIECES_1)
    n_sems = len(gat.scratch())
    assert n_steps >= 3

    def body(x_ref, c_ref, sa_ref, sb_ref, mt_ref, cm_ref, sam_ref, sbm_ref,
             g_ref, w_ref, gq_ref, wq_ref, gkv_ref, wkv_ref, wos_ref, wis_ref,
             p_ref, q_ref, k_ref, v_ref, pm_ref, km_ref, vm_ref, wo_ref, wi_ref, *scr):
        gat_scr, gat_in_scr = scr[:n_sems], scr[n_sems:]
        i = pl.program_id(0)
        for stage, at in enumerate(_StagedGather.steps(n_steps)):
            @pl.when(i == at)
            def _(stage=stage):
                gat_in.run_vmem(stage, wis_ref, wi_ref, gat_in_scr)
                gat.run(stage, wos_ref, wo_ref, gat_scr)

        def project(xv, c, sa, sb, p_out, q_out, k_out, v_out):
            u, _ = _rms(xv, g_ref[...])
            p = _dot_nt(u.astype(BF16), w_ref[...])
            p_out[...] = p
            qn, _ = _rms(p[:, 0:Q_RANK], gq_ref[...])
            q = _dot_nt(qn.astype(BF16), wq_ref[...])
            kvn, _ = _rms(p[:, Q_RANK:Q_RANK + KV_RANK], gkv_ref[...])
            kv = _dot_nt(kvn.astype(BF16), wkv_ref[...])
            kpe = _rope(p[:, 384:512], c, sa, sb)
            for h in range(HEADS):
                if q_out is not None:
                    pe = _rope(q[:, QK_PAD * h + NOPE:QK_PAD * (h + 1)], c, sa, sb)
                    qh = jnp.concatenate([q[:, QK_PAD * h:QK_PAD * h + NOPE], pe], axis=1)
                    q_out[0, h] = (qh * ATTN_SCALE).astype(BF16)
                k_out[0, h] = jnp.concatenate([kv[:, NOPE * h:NOPE * (h + 1)], kpe], axis=1).astype(BF16)
                v_out[0, h] = kv[:, 512 + VDIM * h:512 + VDIM * (h + 1)].astype(BF16)

        @pl.when(i < n)
        def _():
            project(x_ref[...], c_ref[...], sa_ref[...], sb_ref[...], p_ref, q_ref, k_ref, v_ref)

        @pl.when(i == n)
        def _():
            project(mt_ref[...], cm_ref[...], sam_ref[...], sbm_ref[...], pm_ref, None, km_ref, vm_ref)
            gat_in.run_vmem(gat_in.STAGES, wis_ref, wi_ref, gat_in_scr)

    cl = lambda i: jnp.minimum(i, n - 1)
    full = lambda a: pl.BlockSpec(a.shape, lambda i: (0,) * a.ndim)
    const = lambda shape: pl.BlockSpec(shape, lambda i: (0,) * len(shape))
    tab = pl.BlockSpec((tm, 128), lambda i: (cl(i) % nt, 0))
    hb = lambda w: pl.BlockSpec((1, HEADS, tm, w), lambda i: (cl(i) // nt, 0, cl(i) % nt, 0))
    whole = pl.BlockSpec(memory_space=pl.ANY)
    return pl.pallas_call(
        body, name="fwd_proj", grid=(n_steps,),
        in_specs=[pl.BlockSpec((tm, D_MODEL), lambda i: (cl(i), 0)), tab, tab, tab,
                  full(meta), full(cm_t), full(sam_t), full(sbm_t),
                  full(norm_g), full(w_head), full(q_norm_g), full(wq_p), full(kv_norm_g), full(wkv_p), whole, whole],
        out_specs=[pl.BlockSpec((tm, IN_HEAD), lambda i: (cl(i), 0)), hb(QK_PAD), hb(QK_PAD), hb(VDIM),
                   const((N_META, IN_HEAD)), const((1, HEADS, N_META, QK_PAD)), const((1, HEADS, N_META, VDIM)),
                   whole, whole],
        out_shape=[jax.ShapeDtypeStruct((nb * s, IN_HEAD), F32),
                   jax.ShapeDtypeStruct((nb, HEADS, s, QK_PAD), BF16),
                   jax.ShapeDtypeStruct((nb, HEADS, s, QK_PAD), BF16),
                   jax.ShapeDtypeStruct((nb, HEADS, s, VDIM), BF16),
                   jax.ShapeDtypeStruct((N_META, IN_HEAD), F32),
                   jax.ShapeDtypeStruct((1, HEADS, N_META, QK_PAD), BF16),
                   jax.ShapeDtypeStruct((1, HEADS, N_META, VDIM), BF16),
                   jax.ShapeDtypeStruct((D_MODEL, D_MODEL), BF16),
                   jax.ShapeDtypeStruct((IN_PAD, D_MODEL), BF16)],
        scratch_shapes=gat.scratch() + gat_in.vmem_scratch(w_in_shard.shape, (IN_PAD, D_MODEL)),
        compiler_params=_cparams("arbitrary"),
    )(x2d, c_t, sa_t, sb_t, meta, cm_t, sam_t, sbm_t, norm_g, w_head, q_norm_g, wq_p, kv_norm_g, wkv_p, w_out_shard,
      w_in_shard)


def _attn_fwd(q, k, v, km, vm, w_in_shard, w_in_part, nb, s, tq):
    nq = s // tq
    n_steps = nb * HEADS
    gat = _StagedGather(W_IN_PIECES_2, zero_rows=(HEAD_ROWS, IN_HEAD - HEAD_ROWS))
    assert n_steps >= 3

    def body(q_ref, k_ref, v_ref, km_ref, vm_ref, ws_ref, _, o_ref, lse_ref, w_ref, s_scr, p_scr, *gat_scr):
        step = pl.program_id(0) * HEADS + pl.program_id(1)
        for stage, at in enumerate(_StagedGather.steps(n_steps)):
            @pl.when(step == at)
            def _(stage=stage):
                gat.run_vmem(stage, ws_ref, w_ref, gat_scr)

        row = lax.broadcasted_iota(jnp.int32, (tq, tq), 0)
        col = lax.broadcasted_iota(jnp.int32, (tq, tq), 1)
        def scores(i):
            slot = i % 2
            qi = q_ref[0, 0, i * tq:(i + 1) * tq, :]
            sm = _dot_nt(qi, km_ref[0, 0])
            m128 = None
            for j in range(i + 1):
                sc = _dot_nt(qi, k_ref[0, 0, j * tq:(j + 1) * tq, :])
                if j == i:
                    sc = jnp.where(col <= row, sc, NEG_INF)
                s_scr[slot, :, j * tq:(j + 1) * tq] = sc
                mx = sc[:, 0:128]
                for c0 in range(128, tq, 128):
                    mx = jnp.maximum(mx, sc[:, c0:c0 + 128])
                m128 = mx if m128 is None else jnp.maximum(m128, mx)
            return sm, jnp.maximum(jnp.max(m128, axis=1, keepdims=True), jnp.max(sm, axis=1, keepdims=True))

        def weighted_sum(i, pm, l):
            n = (i + 1) * tq
            acc = _dot(p_scr[i % 2, :, 0:n], v_ref[0, 0, 0:n, :]) + _dot(pm.astype(BF16), vm_ref[0, 0])
            o_ref[0, 0, i * tq:(i + 1) * tq, :] = acc / l

        nxt, pending = scores(0), None
        for i in range(nq):
            slot = i % 2
            sm, m = nxt
            if i + 1 < nq:
                nxt = scores(i + 1)
            pm = jnp.exp(sm - m)
            l128 = None
            for j in range(i + 1):
                p = jnp.exp(s_scr[slot, :, j * tq:(j + 1) * tq] - m)
                p_scr[slot, :, j * tq:(j + 1) * tq] = p.astype(BF16)
                ps = p[:, 0:128]
                for c0 in range(128, tq, 128):
                    ps = ps + p[:, c0:c0 + 128]
                l128 = ps if l128 is None else l128 + ps
            l = jnp.sum(l128, axis=1, keepdims=True) + jnp.sum(pm, axis=1, keepdims=True)
            lse_ref[0, 0, :, i * tq:(i + 1) * tq] = _row_of(m + jnp.log(l), tq)
            if pending is not None:
                weighted_sum(*pending)
            pending = (i, pm, l)
        weighted_sum(*pending)

        @pl.when(step == n_steps - 1)
        def _():
            gat.run_vmem(gat.STAGES, ws_ref, w_ref, gat_scr)

    hblk = lambda w: pl.BlockSpec((1, 1, s, w), lambda b, h: (b, h, 0, 0))
    mblk = lambda w: pl.BlockSpec((1, 1, N_META, w), lambda b, h: (0, h, 0, 0))
    whole = pl.BlockSpec(memory_space=pl.ANY)
    return pl.pallas_call(
        body, name="attn_fwd", grid=(nb, HEADS),
        in_specs=[hblk(QK_PAD), hblk(QK_PAD), hblk(VDIM), mblk(QK_PAD), mblk(VDIM), whole, whole],
        out_specs=[hblk(VDIM), pl.BlockSpec((1, 1, 1, s), lambda b, h: (b, h, 0, 0)), whole],
        out_shape=[jax.ShapeDtypeStruct((nb, HEADS, s, VDIM), F32),
                   jax.ShapeDtypeStruct((nb, HEADS, 1, s), F32),
                   jax.ShapeDtypeStruct(w_in_part.shape, BF16)],
        input_output_aliases={6: 2},
        scratch_shapes=[pltpu.VMEM((2, tq, s), F32), pltpu.VMEM((2, tq, s), BF16)]
        + gat.vmem_scratch(w_in_shard.shape, w_in_part.shape),
        compiler_params=_cparams("arbitrary", "arbitrary"),
    )(q, k, v, km, vm, w_in_shard, w_in_part)


def _shift_rows(a, prev, n_rows):
    rid = lax.broadcasted_iota(jnp.int32, a.shape, 0)
    a1 = jnp.where(rid == 0, prev[7:8, :], pltpu.roll(a, 1, 0))
    a2 = jnp.where(rid == 0, prev[6:7, :], jnp.where(rid == 1, prev[7:8, :], pltpu.roll(a, 2, 0)))
    return a1, a2


def _attn_gate(o, za, ga_h):
    on, r = _rms(o, ga_h)
    return on * (za * _sigmoid(za)), on, r


def _out_fwd_bwd(x2d, tgt2d, o, meta, norm_g, w_in_p, conv_w, ga, gc, gmat, w_out, gf, nb, s, tm):
    nt = s // tm
    r = nb * s

    def body(x_ref, t_ref, o_ref, mt_ref, g_ref, wi_ref, cw_ref, ga_ref, gc_ref, gm_ref, w_ref, gf_ref,
             dh_ref, dy_ref, dw_ref, dgf_ref, loss_ref, p_ref, pm_ref, last_cc):
        i = pl.program_id(0)
        blk = lambda ref, j, rows=slice(None): ref[rows, 512 * j:512 * (j + 1)]

        def tail(xv):
            u, _ = _rms(xv, g_ref[...])
            return _dot_nt(u.astype(BF16), wi_ref[IN_HEAD:IN_PAD, :])

        @pl.when(i == 0)
        def _():
            dw_ref[...] = jnp.zeros_like(dw_ref)
            dgf_ref[...] = jnp.zeros_like(dgf_ref)
            loss_ref[...] = jnp.zeros_like(loss_ref)
            last_cc[...] = jnp.zeros_like(last_cc)
            pm_ref[...] = tail(mt_ref[...])

        u16 = _rms(x_ref[...], g_ref[...])[0].astype(BF16)

        def project(j):
            p_ref[:, 512 * j:512 * (j + 1)] = _dot_nt(u16, wi_ref[IN_HEAD + 512 * j:IN_HEAD + 512 * (j + 1), :])

        project(BLK_ZA)
        project(BLK_CC)
        project(BLK_CH)
        ya = []
        for h in range(HEADS):
            y, _, _ = _attn_gate(o_ref[0, h], p_ref[:, 512 * BLK_ZA + VDIM * h:512 * BLK_ZA + VDIM * (h + 1)],
                                 ga_ref[:, VDIM * h:VDIM * (h + 1)])
            ya.append(y)
        project(BLK_CB)
        project(BLK_ZC)
        cc = blk(p_ref, BLK_CC) * blk(p_ref, BLK_CH)
        meta_cc = blk(pm_ref, BLK_CC, slice(8, 16)) * blk(pm_ref, BLK_CH, slice(8, 16))
        prev = jnp.where(i % nt == 0, meta_cc, last_cc[...])
        last_cc[...] = cc[tm - 8:tm, :]
        cc1, cc2 = _shift_rows(cc, prev, tm)
        yc = blk(p_ref, BLK_CB) * (cw_ref[0:1, :] * cc2 + cw_ref[1:2, :] * cc1 + cw_ref[2:3, :] * cc)
        rg = lax.rsqrt(_group_mean(yc * yc, gm_ref[...]) + EPS)
        zc = blk(p_ref, BLK_ZC)
        yconv = yc * rg * gc_ref[...] * (zc * _sigmoid(zc))
        ycat = jnp.concatenate(ya + [yconv], axis=1).astype(BF16)
        h2 = x_ref[...] + _dot(ycat, w_ref[...])
        gfv = gf_ref[...]
        y, r2 = _rms(h2, gfv)
        e = y - t_ref[...]
        loss_ref[...] += 0.5 * jnp.sum(e * e) / D_MODEL
        dyv = e * (1.0 / D_MODEL)
        dh2, dgf = _rms_bwd(dyv, h2, r2, gfv)
        dgf_ref[...] += jnp.sum(dgf, axis=0, keepdims=True)
        dh_ref[...] = dh2
        dhb = dh2.astype(BF16)
        dy_ref[...] = _dot_nt(dhb, w_ref[...])
        dw_ref[...] += _dot_tn(ycat, dhb)

    row = lambda w: pl.BlockSpec((tm, w), lambda i: (i, 0))
    const = lambda shape: pl.BlockSpec(shape, lambda i: (0,) * len(shape))
    full = lambda a: const(a.shape)
    return pl.pallas_call(
        body, name="out_fwd_bwd", grid=(nb * nt,),
        in_specs=[row(D_MODEL), row(D_MODEL),
                  pl.BlockSpec((1, HEADS, tm, VDIM), lambda i: (i // nt, 0, i % nt, 0)),
                  full(meta), full(norm_g), full(w_in_p),
                  full(conv_w), full(ga), full(gc), full(gmat), full(w_out), full(gf)],
        out_specs=[row(D_MODEL), row(D_MODEL), const((D_MODEL, D_MODEL)), const((1, D_MODEL)), const((1, 128)),
                   row(IN_TAIL), const((N_META, IN_TAIL))],
        out_shape=[jax.ShapeDtypeStruct((r, D_MODEL), F32), jax.ShapeDtypeStruct((r, D_MODEL), F32),
                   jax.ShapeDtypeStruct((D_MODEL, D_MODEL), F32), jax.ShapeDtypeStruct((1, D_MODEL), F32),
                   jax.ShapeDtypeStruct((1, 128), F32),
                   jax.ShapeDtypeStruct((r, IN_TAIL), F32), jax.ShapeDtypeStruct((N_META, IN_TAIL), F32)],
        scratch_shapes=[pltpu.VMEM((8, 512), F32)],
        compiler_params=_cparams("arbitrary"),
    )(x2d, tgt2d, o, meta, norm_g, w_in_p, conv_w, ga, gc, gmat, w_out, gf)


def _gate_bwd(dycat, o, p, pm, conv_w, ga, gc, gmat, nb, s, tm):
    nt = s // tm
    r = nb * s
    ext = tm + 8
    prev_idx = lambda i: jnp.maximum(i * (tm // 8) - 1, 0)
    next_idx = lambda i: jnp.minimum((i + 1) * (tm // 8), r // 8 - 1)

    def body(dya_ref, dyc_ref, dycn_ref, o_ref, za_ref, cb_ref, cbn_ref, cc_ref, ccp_ref, ccn_ref,
             ch_ref, chp_ref, chn_ref, zc_ref, zcn_ref, mc_ref, mh_ref, cw_ref, ga_ref, gc_ref, gm_ref,
             dpb_ref, do_ref, dl_ref, dccm_ref, dga_ref, dgc_ref, dcw_ref):
        i = pl.program_id(0)

        @pl.when(i == 0)
        def _():
            dga_ref[...] = jnp.zeros_like(dga_ref)
            dgc_ref[...] = jnp.zeros_like(dgc_ref)
            dcw_ref[...] = jnp.zeros_like(dcw_ref)

        dga = []
        for h in range(HEADS):
            hs = slice(VDIM * h, VDIM * (h + 1))
            oh, za, gah, dya = o_ref[0, h], za_ref[:, hs], ga_ref[:, hs], dya_ref[:, hs]
            sg = _sigmoid(za)
            on, ro = _rms(oh, gah)
            don = dya * (za * sg)
            dpb_ref[:, hs] = (dya * on * (sg * (1.0 + za * (1.0 - sg)))).astype(BF16)
            do, dg = _rms_bwd(don, oh, ro, gah)
            dga.append(jnp.sum(dg, axis=0, keepdims=True))
            dob = do.astype(BF16)
            do_ref[0, h] = dob
            dl_ref[0, h] = _row_of(jnp.sum(dob.astype(F32) * oh, axis=1, keepdims=True), tm)
        dga_ref[...] += jnp.concatenate(dga, axis=1)

        cat = lambda a, b: jnp.concatenate([a[...], b[...]], axis=0)
        cch = cat(cc_ref, ccn_ref)
        chh = cat(ch_ref, chn_ref)
        cb = cat(cb_ref, cbn_ref)
        zc = cat(zc_ref, zcn_ref)
        dy = cat(dyc_ref, dycn_ref)
        first = i % nt == 0
        last = i % nt == nt - 1
        cc = cch * chh
        prev = jnp.where(first, mc_ref[8:16, :] * mh_ref[8:16, :], ccp_ref[...] * chp_ref[...])
        cc1, cc2 = _shift_rows(cc, prev, ext)
        w0, w1, w2 = cw_ref[0:1, :], cw_ref[1:2, :], cw_ref[2:3, :]
        dw = w0 * cc2 + w1 * cc1 + w2 * cc
        yc = cb * dw
        rg = lax.rsqrt(_group_mean(yc * yc, gm_ref[...]) + EPS)
        ych = yc * rg
        gcv = gc_ref[...]
        sg = _sigmoid(zc)
        dycn = dy * (zc * sg)
        dzc = dy * (ych * gcv) * (sg * (1.0 + zc * (1.0 - sg)))
        dgc_ref[...] += jnp.sum((dycn * ych)[:tm], axis=0, keepdims=True)
        dycg = dycn * gcv
        dyc = rg * (dycg - ych * _group_mean(dycg * ych, gm_ref[...]))
        rid = lax.broadcasted_iota(jnp.int32, (ext, CONV_W), 0)
        ddw = jnp.where(jnp.logical_and(last, rid >= tm), 0.0, dyc * cb)
        dcb = dyc * dw
        dcc = w2 * ddw + w1 * pltpu.roll(ddw, ext - 1, 0) + w0 * pltpu.roll(ddw, ext - 2, 0)
        dpb_ref[:, 512:1024] = dcb[:tm].astype(BF16)
        dpb_ref[:, 1024:1536] = (dcc * chh)[:tm].astype(BF16)
        dpb_ref[:, 1536:2048] = (dcc * cch)[:tm].astype(BF16)
        dpb_ref[:, 2048:2560] = dzc[:tm].astype(BF16)
        rs = lambda a: jnp.sum(a[:tm], axis=0, keepdims=True)
        dcw_ref[0:1, :] += rs(ddw * cc2)
        dcw_ref[1:2, :] += rs(ddw * cc1)
        dcw_ref[2:3, :] += rs(ddw * cc)

        @pl.when(first)
        def _():
            d0, d1 = ddw[0:1, :], ddw[1:2, :]
            r8 = lax.broadcasted_iota(jnp.int32, (8, CONV_W), 0)
            dccm_ref[0] = jnp.where(r8 == 7, w1 * d0 + w0 * d1, jnp.where(r8 == 6, w0 * d0, 0.0))

    row = lambda j: pl.BlockSpec((tm, 512), lambda i: (i, j))
    prv = lambda j: pl.BlockSpec((8, 512), lambda i: (prev_idx(i), j))
    nxt = lambda j: pl.BlockSpec((8, 512), lambda i: (next_idx(i), j))
    mblk = lambda j: pl.BlockSpec((N_META, 512), lambda i: (0, j))
    full = lambda a: pl.BlockSpec(a.shape, lambda i: (0,) * a.ndim)
    hb = lambda w: pl.BlockSpec((1, HEADS, tm, w), lambda i: (i // nt, 0, i % nt, 0))
    acc = lambda rr: pl.BlockSpec((rr, 512), lambda i: (0, 0))
    return pl.pallas_call(
        body, name="gate_bwd", grid=(nb * nt,),
        in_specs=[row(0), row(1), nxt(1), hb(VDIM),
                  row(BLK_ZA), row(BLK_CB), nxt(BLK_CB), row(BLK_CC), prv(BLK_CC), nxt(BLK_CC),
                  row(BLK_CH), prv(BLK_CH), nxt(BLK_CH), row(BLK_ZC), nxt(BLK_ZC),
                  mblk(BLK_CC), mblk(BLK_CH), full(conv_w), full(ga), full(gc), full(gmat)],
        out_specs=[pl.BlockSpec((tm, 2560), lambda i: (i, 0)), hb(VDIM),
                   pl.BlockSpec((1, HEADS, 1, tm), lambda i: (i // nt, 0, 0, i % nt)),
                   pl.BlockSpec((1, 8, 512), lambda i: (i // nt, 0, 0)),
                   acc(1), acc(1), acc(8)],
        out_shape=[jax.ShapeDtypeStruct((r, 2560), BF16), jax.ShapeDtypeStruct((nb, HEADS, s, VDIM), BF16),
                   jax.ShapeDtypeStruct((nb, HEADS, 1, s), F32), jax.ShapeDtypeStruct((nb, 8, 512), F32),
                   jax.ShapeDtypeStruct((1, 512), F32), jax.ShapeDtypeStruct((1, 512), F32),
                   jax.ShapeDtypeStruct((8, 512), F32)],
        compiler_params=_cparams("arbitrary"),
    )(dycat, dycat, dycat, o, p, p, p, p, p, p, p, p, p, p, p, pm, pm, conv_w, ga, gc, gmat)


class _StagedReduce:
    LOC, PRE_S, PRE_R, ICI_S, ICI_R, POST_S, POST_R, OUT, N_SEM = 0, 1, 2, 3, 6, 9, 10, 11, 12

    def __init__(self, shard_shape):
        self.half = (shard_shape[0] // 2, shard_shape[1])

    def scratch(self):
        h = self.half
        return [pltpu.VMEM((4,) + h, F32), pltpu.VMEM((4,) + h, F32), pltpu.VMEM((4,) + h, BF16),
                pltpu.VMEM((3,) + h, BF16), pltpu.VMEM(h, F32), pltpu.SemaphoreType.DMA((self.N_SEM,))]

    def run(self, stage, pin, gout, scr):
        own, sib, wire, rbuf, fin, sems = scr
        r2 = self.half[0]
        x, y, c = lax.axis_index("x"), lax.axis_index("y"), lax.axis_index("c")
        mine = 2 * x + y
        sibling = (x, y, 1 - c)
        chips = [(1 - x, y), (x, 1 - y), (1 - x, 1 - y)]
        rows = lambda half: pl.ds(pl.multiple_of(half * r2, r2), r2)
        mesh = pl.DeviceIdType.MESH

        loc = pltpu.make_async_copy(pin.at[:, rows(c), :], own, sems.at[self.LOC])
        pre = pltpu.make_async_remote_copy(
            src_ref=pin.at[:, rows(1 - c), :], dst_ref=sib, send_sem=sems.at[self.PRE_S],
            recv_sem=sems.at[self.PRE_R], device_id=sibling, device_id_type=mesh)

        def ici(j):
            px, py = chips[j]
            return pltpu.make_async_remote_copy(
                src_ref=wire.at[2 * px + py], dst_ref=rbuf.at[j], send_sem=sems.at[self.ICI_S + j],
                recv_sem=sems.at[self.ICI_R + j], device_id=(px, py, c), device_id_type=mesh)

        def post(half):
            return pltpu.make_async_remote_copy(
                src_ref=fin, dst_ref=gout.at[rows(half), :], send_sem=sems.at[self.POST_S],
                recv_sem=sems.at[self.POST_R], device_id=sibling, device_id_type=mesh)

        keep = pltpu.make_async_copy(fin, gout.at[rows(c), :], sems.at[self.OUT])
        if stage == 0:
            loc.start()
            pre.start()
        elif stage == 1:
            loc.wait()
            pre.wait_recv()
            for blk in range(4):
                tot = own[blk] + sib[blk]
                own[blk] = tot
                wire[blk] = tot.astype(BF16)
            for j in range(3):
                ici(j).start()
        elif stage == 2:
            for j in range(3):
                ici(j).wait_recv()
            tot = own[mine]
            for j in range(3):
                tot = tot + rbuf[j].astype(F32)
            fin[...] = tot
            post(c).start()
            keep.start()
        else:
            post(1 - c).wait_recv()
            pre.wait_send()
            for j in range(3):
                ici(j).wait_send()
            post(c).wait_send()
            keep.wait()


def _attn_bwd(q, k, v, do, lse, delta, km, vm, early, nb, s, t):
    n = s // t
    ne = len(early)
    reds = [_StagedReduce(a.shape[1:]) for a in early]
    n_steps = HEADS * nb
    assert n_steps >= 4

    def body(q_ref, k_ref, v_ref, do_ref, lse_ref, dl_ref, km_ref, vm_ref, *rest):
        pin_refs, rest = rest[:ne], rest[ne:]
        dq_ref, dk_ref, dv_ref, dkm_ref, dvm_ref = rest[:5]
        gout_refs, (p_scr, ds_scr, dq_acc), red_scr = rest[5:5 + ne], rest[5 + ne:8 + ne], rest[8 + ne:]
        b = pl.program_id(1)
        step = pl.program_id(0) * nb + b
        for stage, at in enumerate((0, 1, n_steps - 2, n_steps - 1)):
            @pl.when(step == at)
            def _(stage=stage):
                for a, red in enumerate(reds):
                    red.run(stage, pin_refs[a], gout_refs[a], red_scr[6 * a:6 * a + 6])

        @pl.when(b == 0)
        def _():
            dkm_ref[...] = jnp.zeros_like(dkm_ref)
            dvm_ref[...] = jnp.zeros_like(dvm_ref)

        kr = lax.broadcasted_iota(jnp.int32, (t, t), 0)
        qc = lax.broadcasted_iota(jnp.int32, (t, t), 1)
        km_v, vm_v = km_ref[0, 0], vm_ref[0, 0]
        ptm = jnp.exp(_dot_nt(km_v, q_ref[0, 0]) - lse_ref[0, 0])
        dstm = (ptm * (_dot_nt(vm_v, do_ref[0, 0]) - dl_ref[0, 0])).astype(BF16)
        dkm_ref[0] += _dot(dstm, q_ref[0, 0])
        dvm_ref[0] += _dot(ptm.astype(BF16), do_ref[0, 0])
        dq_acc[...] = _dot_tn(dstm, km_v)
        def tiles(j):
            slot = j % 2
            kj = k_ref[0, 0, j * t:(j + 1) * t, :]
            vj = v_ref[0, 0, j * t:(j + 1) * t, :]
            def products(i):
                cs = slice(i * t, (i + 1) * t)
                return _dot_nt(kj, q_ref[0, 0, cs, :]), _dot_nt(vj, do_ref[0, 0, cs, :])

            nxt, pending = products(j), None
            for i in range(j, n):
                cs = slice(i * t, (i + 1) * t)
                st, dpt = nxt
                if i + 1 < n:
                    nxt = products(i + 1)
                if i == j:
                    st = jnp.where(kr <= qc, st, NEG_INF)
                pt = jnp.exp(st - lse_ref[0, 0, :, cs])
                dst = (pt * (dpt - dl_ref[0, 0, :, cs])).astype(BF16)
                p_scr[slot, :, cs] = pt.astype(BF16)
                ds_scr[slot, :, cs] = dst
                if pending is not None:
                    dq_acc[pending[0], :] += _dot_tn(pending[1], kj)
                pending = (cs, dst)
            dq_acc[pending[0], :] += _dot_tn(pending[1], kj)

        for j in range(n):
            slot = j % 2
            tiles(j)
            dv_ref[0, 0, j * t:(j + 1) * t, :] = _dot(p_scr[slot, :, j * t:s], do_ref[0, 0, j * t:s, :]).astype(BF16)
            dk_ref[0, 0, j * t:(j + 1) * t, :] = _dot(ds_scr[slot, :, j * t:s], q_ref[0, 0, j * t:s, :]).astype(BF16)
        dq_ref[0, 0] = dq_acc[...].astype(BF16)

    big = lambda w: pl.BlockSpec((1, 1, s, w), lambda h, b: (b, h, 0, 0))
    rowv = pl.BlockSpec((1, 1, 1, s), lambda h, b: (b, h, 0, 0))
    mk = lambda w: pl.BlockSpec((1, 1, N_META, w), lambda h, b: (0, h, 0, 0))
    mo = lambda w: pl.BlockSpec((1, N_META, w), lambda h, b: (h, 0, 0))
    return pl.pallas_call(
        body, name="attn_bwd", grid=(HEADS, nb),
        in_specs=[big(QK_PAD), big(QK_PAD), big(VDIM), big(VDIM), rowv, rowv, mk(QK_PAD), mk(VDIM)]
        + [pl.BlockSpec(memory_space=pl.ANY)] * ne,
        out_specs=[big(QK_PAD), big(QK_PAD), big(VDIM), mo(QK_PAD), mo(VDIM)]
        + [pl.BlockSpec(memory_space=pl.ANY)] * ne,
        out_shape=[jax.ShapeDtypeStruct((nb, HEADS, s, QK_PAD), BF16),
                   jax.ShapeDtypeStruct((nb, HEADS, s, QK_PAD), BF16),
                   jax.ShapeDtypeStruct((nb, HEADS, s, VDIM), BF16),
                   jax.ShapeDtypeStruct((HEADS, N_META, QK_PAD), F32),
                   jax.ShapeDtypeStruct((HEADS, N_META, VDIM), F32)]
        + [jax.ShapeDtypeStruct(a.shape[1:], F32) for a in early],
        scratch_shapes=[pltpu.VMEM((2, t, s), BF16), pltpu.VMEM((2, t, s), BF16), pltpu.VMEM((s, QK_PAD), F32)]
        + [sc for red in reds for sc in red.scratch()],
        compiler_params=_cparams("arbitrary", "arbitrary"),
    )(q, k, v, do, lse, delta, km, vm, *early)


def _up_bwd(dq, dk, dv, dkm, dvm, p, pm, tabs, tabs_m, wq_p, wkv_p, gq, gkv, nb, s, tm):
    nt = s // tm
    n = nb * nt
    c_t, sa_t, sb_t = tabs
    cm_t, sam_t, sbm_t = tabs_m

    def kv_path(dkh, dvh, pa, c, sa, sb, wkv, gkvv):
        dkpe = dkh[0][:, NOPE:]
        for h in range(1, HEADS):
            dkpe = dkpe + dkh[h][:, NOPE:]
        dkr = _rope_bwd(dkpe, c, sa, sb)
        dkv = jnp.concatenate([d[:, :NOPE] for d in dkh] + list(dvh), axis=1).astype(BF16)
        ckv = pa[:, Q_RANK:Q_RANK + KV_RANK]
        kvn, rkv = _rms(ckv, gkvv)
        dckv, dg = _rms_bwd(_dot(dkv, wkv), ckv, rkv, gkvv)
        return dckv, dkr, kvn.astype(BF16), dkv, jnp.sum(dg, axis=0, keepdims=True)

    def body(dq_ref, dk_ref, dv_ref, pa_ref, c_ref, sa_ref, sb_ref,
             dkm_ref, dvm_ref, pam_ref, cm_ref, sam_ref, sbm_ref,
             wq_ref, wkv_ref, gq_ref, gkv_ref,
             dpa_ref, dpam_ref, pq_ref, pkv_ref, dgq_ref, dgkv_ref, dwq_ref, dwkv_ref):
        i = pl.program_id(0)

        @pl.when(i == 0)
        def _():
            dwq_ref[...] = jnp.zeros_like(dwq_ref)
            dwkv_ref[...] = jnp.zeros_like(dwkv_ref)
            dgq_ref[...] = jnp.zeros_like(dgq_ref)
            dgkv_ref[...] = jnp.zeros_like(dgkv_ref)

        @pl.when(i < n)
        def _():
            c, sa, sb = c_ref[...], sa_ref[...], sb_ref[...]
            pa = pa_ref[...]
            parts = []
            for h in range(HEADS):
                dqh = dq_ref[0, h].astype(F32) * ATTN_SCALE
                parts += [dqh[:, :NOPE], _rope_bwd(dqh[:, NOPE:], c, sa, sb)]
            dql = jnp.concatenate(parts, axis=1).astype(BF16)
            cq = pa[:, 0:Q_RANK]
            gqv = gq_ref[...]
            qn, rq = _rms(cq, gqv)
            dwq_ref[...] += _dot_tn(dql, qn.astype(BF16))
            dcq, dg = _rms_bwd(_dot(dql, wq_ref[...]), cq, rq, gqv)
            dgq_ref[...] += jnp.sum(dg, axis=0, keepdims=True)
            dckv, dkr, kvn, dkv, dgk = kv_path([dk_ref[0, h].astype(F32) for h in range(HEADS)],
                                               [dv_ref[0, h].astype(F32) for h in range(HEADS)],
                                               pa, c, sa, sb, wkv_ref[...], gkv_ref[...])
            dwkv_ref[...] += _dot_tn(dkv, kvn)
            dgkv_ref[...] += dgk
            dpa_ref[...] = jnp.concatenate([dcq, dckv, dkr], axis=1).astype(BF16)

        @pl.when(i == n)
        def _():
            dckv, dkr, kvn, dkv, dgk = kv_path([dkm_ref[h] for h in range(HEADS)],
                                               [dvm_ref[h] for h in range(HEADS)],
                                               pam_ref[...], cm_ref[...], sam_ref[...], sbm_ref[...],
                                               wkv_ref[...], gkv_ref[...])
            dwkv_ref[...] += _dot_tn(dkv, kvn)
            dgkv_ref[...] += dgk
            dpam_ref[...] = jnp.concatenate([jnp.zeros((N_META, Q_RANK), F32), dckv, dkr], axis=1)
            for h in range(HEADS):
                pq_ref[h] = dwq_ref[QK_PAD * h:QK_PAD * h + NOPE + ROPE, :]
                pkv_ref[h, 0:NOPE, :] = dwkv_ref[NOPE * h:NOPE * (h + 1), :]
                pkv_ref[h, NOPE:NOPE + VDIM, :] = dwkv_ref[512 + VDIM * h:512 + VDIM * (h + 1), :]

    cl = lambda i: jnp.minimum(i, n - 1)
    hb = lambda w: pl.BlockSpec((1, HEADS, tm, w), lambda i: (cl(i) // nt, 0, cl(i) % nt, 0))
    tab = pl.BlockSpec((tm, 128), lambda i: (cl(i) % nt, 0))
    full = lambda a: pl.BlockSpec(a.shape, lambda i: (0,) * a.ndim)
    const = lambda shape: pl.BlockSpec(shape, lambda i: (0,) * len(shape))
    return pl.pallas_call(
        body, name="up_bwd", grid=(n + 1,),
        in_specs=[hb(QK_PAD), hb(QK_PAD), hb(VDIM), pl.BlockSpec((tm, 512), lambda i: (cl(i), 0)), tab, tab, tab,
                  full(dkm), full(dvm), pl.BlockSpec((N_META, 512), lambda i: (0, 0)),
                  full(cm_t), full(sam_t), full(sbm_t), full(wq_p), full(wkv_p), full(gq), full(gkv)],
        out_specs=[pl.BlockSpec((tm, 512), lambda i: (cl(i), 0)), const((N_META, 512)),
                   const((HEADS, NOPE + ROPE, Q_RANK)), const((HEADS, NOPE + VDIM, KV_RANK)),
                   const((1, Q_RANK)), const((1, KV_RANK))],
        out_shape=[jax.ShapeDtypeStruct((nb * s, 512), BF16), jax.ShapeDtypeStruct((N_META, 512), F32),
                   jax.ShapeDtypeStruct((HEADS, NOPE + ROPE, Q_RANK), F32),
                   jax.ShapeDtypeStruct((HEADS, NOPE + VDIM, KV_RANK), F32),
                   jax.ShapeDtypeStruct((1, Q_RANK), F32), jax.ShapeDtypeStruct((1, KV_RANK), F32)],
        scratch_shapes=[pltpu.VMEM((HEADS * QK_PAD, Q_RANK), F32), pltpu.VMEM((1024, KV_RANK), F32)],
        compiler_params=_cparams("arbitrary"),
    )(dq, dk, dv, p, c_t, sa_t, sb_t, dkm, dvm, pm, cm_t, sam_t, sbm_t, wq_p, wkv_p, gq, gkv)


def _in_bwd(x2d, dh2, dpa, dpb, meta, dpam, dccm, pm, w_in_p, norm_g, nb, s, tm):
    nt = s // tm
    n = nb * nt

    def body(x_ref, dh_ref, dpa_ref, dpb_ref, mt_ref, dpam_ref, dccm_ref, mc_ref, mh_ref, w_ref, g_ref,
             gx_ref, gm_ref, dw_hbm, dg_ref, acc_ref, sems):
        i = pl.program_id(0)

        @pl.when(i == 0)
        def _():
            acc_ref[...] = jnp.zeros_like(acc_ref)
            dg_ref[...] = jnp.zeros_like(dg_ref)

        def rows(x, dp, dres):
            g = g_ref[...]
            dpb16 = dp.astype(BF16)
            du = _dot(dpb16, w_ref[...])
            u, r1 = _rms(x, g)
            acc_ref[...] += _dot_tn(dpb16, u.astype(BF16))
            dx, dg = _rms_bwd(du, x, r1, g)
            dg_ref[...] += jnp.sum(dg, axis=0, keepdims=True)
            return dx if dres is None else dx + dres

        @pl.when(i < n)
        def _():
            dp = jnp.concatenate([dpa_ref[...], dpb_ref[...]], axis=1)
            gx_ref[...] = rows(x_ref[...], dp, dh_ref[...])

        @pl.when(i == n)
        def _():
            dcc = dccm_ref[0]
            for b in range(1, nb):
                dcc = dcc + dccm_ref[b]
            z8 = jnp.zeros((8, CONV_W), F32)
            dc = jnp.concatenate([z8, dcc * mh_ref[8:16, :]], axis=0)
            dh = jnp.concatenate([z8, dcc * mc_ref[8:16, :]], axis=0)
            z = jnp.zeros((N_META, CONV_W), F32)
            dp = jnp.concatenate([dpam_ref[...], z, z, dc, dh, z], axis=1)
            gm_ref[...] = rows(mt_ref[...], dp, None)
            per = IN_DIM // 4
            cps = [pltpu.make_async_copy(acc_ref.at[0:448], dw_hbm.at[0, 0:448], sems.at[0]),
                   pltpu.make_async_copy(acc_ref.at[512:per + 64], dw_hbm.at[0, 448:per], sems.at[1])]
            for qq in range(1, 4):
                cps.append(pltpu.make_async_copy(acc_ref.at[per * qq + 64:per * (qq + 1) + 64], dw_hbm.at[qq],
                                                 sems.at[qq + 1]))
            for cp in cps:
                cp.start()
            for cp in cps:
                cp.wait()

    cl = lambda i: jnp.minimum(i, n - 1)
    row = lambda w: pl.BlockSpec((tm, w), lambda i: (cl(i), 0))
    full = lambda a: pl.BlockSpec(a.shape, lambda i: (0,) * a.ndim)
    mblk = lambda j: pl.BlockSpec((N_META, 512), lambda i: (0, j))
    return pl.pallas_call(
        body, name="in_bwd", grid=(n + 1,),
        in_specs=[row(D_MODEL), row(D_MODEL), row(512), row(2560), full(meta), full(dpam), full(dccm),
                  mblk(BLK_CC), mblk(BLK_CH), full(w_in_p), full(norm_g)],
        out_specs=[row(D_MODEL), pl.BlockSpec((N_META, D_MODEL), lambda i: (0, 0)),
                   pl.BlockSpec(memory_space=pl.ANY), pl.BlockSpec((1, D_MODEL), lambda i: (0, 0))],
        out_shape=[jax.ShapeDtypeStruct((nb * s, D_MODEL), F32), jax.ShapeDtypeStruct((N_META, D_MODEL), F32),
                   jax.ShapeDtypeStruct((4, IN_DIM // 4, D_MODEL), F32), jax.ShapeDtypeStruct((1, D_MODEL), F32)],
        scratch_shapes=[pltpu.VMEM((IN_PAD, D_MODEL), F32), pltpu.SemaphoreType.DMA((5,))],
        compiler_params=_cparams("arbitrary"),
    )(x2d, dh2, dpa, dpb, meta, dpam, dccm, pm, pm, w_in_p, norm_g)


def _gather_weights(w_in_shard, split, pieces, out_rows, whole, zero_fills):
    ns, nw, nz = len(split), len(whole), len(zero_fills)
    flat = [(a, pc) for a in range(ns) for pc in pieces[a]]
    nk = len(flat)
    hh = HEAD_ROWS // 2
    hc = hh // HEAD_CHUNKS
    assert hc * HEAD_CHUNKS == hh and hc % 16 == 0

    def body(*refs):
        ins, wins, zins = refs[1:1 + ns], refs[1 + ns:1 + ns + nw], refs[1 + ns + nw:1 + ns + nw + nz]
        n_in = 1 + ns + nw + nz
        head_ref, shard16 = refs[n_in], refs[n_in + 1]
        outs, wouts = refs[n_in + 2:n_in + 2 + ns], refs[n_in + 2 + ns:n_in + 2 + ns + nw]
        scr = refs[n_in + 2 + ns + nw:]
        stage = scr[:ns]
        (send_sems, recv_sems, fwd_send, fwd_recv, loc_sems, w_send, w_recv, w_loc, z_sems,
         h_send, h_recv, h_relay, h_pass) = scr[ns:]
        x, y, c = lax.axis_index("x"), lax.axis_index("y"), lax.axis_index("c")
        mine = 2 * x + y
        chips = [(1 - x, y), (x, 1 - y), (1 - x, 1 - y)]
        chip_of = [2 * px + py for px, py in chips]
        shard16[...] = refs[0][...].astype(BF16)
        for a in range(ns):
            stage[a][...] = ins[a][...].astype(BF16)

        nxt = (jnp.where(mine == 0, 1 - c, jnp.where(mine == 3, c, 1)),
               jnp.where(mine == 0, c, jnp.where(mine == 3, 1 - c, 1)), c)
        is_last = mine == 1 + c

        def head_rows(half, n):
            return pl.ds(pl.multiple_of(half * hh + n * hc, 16), hc)

        def head_copy(n):
            return pltpu.make_async_remote_copy(
                src_ref=shard16.at[head_rows(c, n)], dst_ref=head_ref.at[head_rows(c, n)], send_sem=h_send.at[n],
                recv_sem=h_recv.at[n], device_id=nxt, device_id_type=pl.DeviceIdType.MESH)

        def head_relay(n):
            ref = head_ref.at[head_rows(c, n)]
            return pltpu.make_async_remote_copy(
                src_ref=ref, dst_ref=ref, send_sem=h_relay.at[n], recv_sem=h_recv.at[n],
                device_id=nxt, device_id_type=pl.DeviceIdType.MESH)

        def head_pass(half, n):
            ref = head_ref.at[head_rows(half, n)]
            return pltpu.make_async_remote_copy(
                src_ref=ref, dst_ref=ref, send_sem=h_pass.at[0, n], recv_sem=h_pass.at[1, n],
                device_id=(x, y, 1 - c), device_id_type=pl.DeviceIdType.MESH)

        head_ref[HEAD_ROWS:IN_HEAD, :] = jnp.zeros((IN_HEAD - HEAD_ROWS, D_MODEL), BF16)

        @pl.when(mine == 0)
        def _():
            for n in range(HEAD_CHUNKS):
                head_copy(n).start()
            head_ref[0:HEAD_ROWS, :] = shard16[0:HEAD_ROWS, :]

        def src(k):
            a, (s0, nr, _, _, _, _) = flat[k]
            return stage[a].at[s0:s0 + nr]

        def dst(k, q):
            a, (_, nr, per, first, rest, _) = flat[k]
            row = per * q + first + (rest - first) * jnp.minimum(q, 1)
            return outs[a].at[pl.ds(pl.multiple_of(row, 16), nr)]

        def ici(k, j, q):
            px, py = chips[j]
            return pltpu.make_async_remote_copy(
                src_ref=src(k), dst_ref=dst(k, q), send_sem=send_sems.at[k, j], recv_sem=recv_sems.at[k, j],
                device_id=(px, py, c), device_id_type=pl.DeviceIdType.MESH)

        def fwd(k, j):
            ref = dst(k, chip_of[j])
            return pltpu.make_async_remote_copy(
                src_ref=ref, dst_ref=ref, send_sem=fwd_send.at[k, j], recv_sem=fwd_recv.at[k, j],
                device_id=(x, y, 1 - c), device_id_type=pl.DeviceIdType.MESH)

        def wcopy(b, j, q):
            px, py = chips[j]
            return pltpu.make_async_remote_copy(
                src_ref=wins[b], dst_ref=wouts[b].at[q], send_sem=w_send.at[b, j], recv_sem=w_recv.at[b, j],
                device_id=(px, py, c), device_id_type=pl.DeviceIdType.MESH)

        local = [pltpu.make_async_copy(src(k), dst(k, mine), loc_sems.at[k]) for k in range(nk)]
        local += [pltpu.make_async_copy(wins[b], wouts[b].at[mine], w_loc.at[b]) for b in range(nw)]
        for z, (a, _, row0) in enumerate(zero_fills):
            local.append(pltpu.make_async_copy(zins[z], outs[a].at[row0:row0 + zins[z].shape[0]], z_sems.at[z]))
        wsends = [wcopy(b, j, mine) for b in range(nw) for j in range(3)]
        for cp in local + wsends:
            cp.start()

        for half in (0, 1):
            @pl.when(c == half)
            def _(half=half):
                my_k = [k for k in range(nk) if flat[k][1][5] == half]
                other_k = [k for k in range(nk) if flat[k][1][5] != half]
                sends = [ici(k, j, mine) for k in my_k for j in range(3)]
                for cp in sends:
                    cp.start()
                passed = []
                for k in my_k:
                    for j in range(3):
                        ici(k, j, chip_of[j]).wait_recv()
                        cp = fwd(k, j)
                        cp.start()
                        passed.append(cp)
                for k in other_k:
                    for j in range(3):
                        fwd(k, j).wait_recv()
                for cp in sends + passed:
                    cp.wait_send()

        for b in range(nw):
            for j in range(3):
                wcopy(b, j, chip_of[j]).wait_recv()
        for cp in wsends:
            cp.wait_send()
        for cp in local:
            cp.wait()

        @pl.when(mine == 0)
        def _():
            for n in range(HEAD_CHUNKS):
                head_copy(n).wait_send()

        @pl.when(mine != 0)
        def _():
            hands_on = jnp.logical_not(is_last)
            for n in range(HEAD_CHUNKS):
                head_copy(n).wait_recv()

                @pl.when(hands_on)
                def _(n=n):
                    head_relay(n).start()

                head_pass(c, n).start()
            for n in range(HEAD_CHUNKS):
                head_pass(1 - c, n).wait_recv()
                head_pass(c, n).wait_send()

                @pl.when(hands_on)
                def _(n=n):
                    head_relay(n).wait_send()

    vmem = pl.BlockSpec(memory_space=pltpu.VMEM)
    dma = pltpu.SemaphoreType.DMA
    zeros = [z for _, z, _ in zero_fills]
    return pl.pallas_call(
        body, name="gather_weights",
        in_specs=[vmem] * (1 + ns + nw + nz), out_specs=[vmem] * (2 + ns + nw),
        out_shape=([jax.ShapeDtypeStruct((IN_HEAD, D_MODEL), BF16), jax.ShapeDtypeStruct(w_in_shard.shape, BF16)]
                   + [jax.ShapeDtypeStruct((out_rows[a], split[a].shape[1]), BF16) for a in range(ns)]
                   + [jax.ShapeDtypeStruct((4,) + w.shape, w.dtype) for w in whole]),
        scratch_shapes=[pltpu.VMEM(a.shape, BF16) for a in split]
        + [dma((nk, 3)), dma((nk, 3)), dma((nk, 3)), dma((nk, 3)), dma((nk,)),
           dma((nw, 3)), dma((nw, 3)), dma((nw,)), dma((nz,)),
           dma((HEAD_CHUNKS,)), dma((HEAD_CHUNKS,)), dma((HEAD_CHUNKS,)), dma((2, HEAD_CHUNKS))],
        compiler_params=pltpu.CompilerParams(vmem_limit_bytes=VMEM_LIMIT),
    )(w_in_shard, *split, *whole, *zeros)


def _reduce_grads(parts, small):
    n = len(parts)
    shapes = [a.shape[1:] for a in parts]
    halves = [(sh[0] // 2, sh[1]) for sh in shapes]

    def body(*refs):
        pin, sm_in = refs[:n], refs[n]
        gout, sm_out = refs[n + 1:2 * n + 1], refs[2 * n + 1]
        scr = refs[2 * n + 2:]
        own, sib, wire, rbuf = scr[:n], scr[n:2 * n], scr[2 * n:3 * n], scr[3 * n:4 * n]
        (sbuf, send_sems, recv_sems, loc_sems, pre_send, pre_recv, post_send, post_recv,
         sm_send, sm_recv) = scr[4 * n:]
        x, y, c = lax.axis_index("x"), lax.axis_index("y"), lax.axis_index("c")
        mine = 2 * x + y
        me = 4 * x + 2 * y + c
        sibling = (x, y, 1 - c)

        def rows(a, half):
            r2 = halves[a][0]
            return pl.ds(pl.multiple_of(half * r2, r2), r2)

        near = (jnp.where(c == 0, 1 - x, x), jnp.where(c == 0, y, 1 - y))
        far = (jnp.where(c == 0, x, 1 - x), jnp.where(c == 0, 1 - y, y))
        chip = lambda p: 2 * p[0] + p[1]
        blocks = [3 - mine, chip(near), chip(far), mine]

        def pre(a, k):
            q = blocks[k]
            return pltpu.make_async_remote_copy(
                src_ref=pin[a].at[q, rows(a, 1 - c), :], dst_ref=sib[a].at[q],
                send_sem=pre_send.at[a, q], recv_sem=pre_recv.at[a, q], device_id=sibling,
                device_id_type=pl.DeviceIdType.MESH)

        def ici(a, m):
            px, py = near if m < 2 else far
            return pltpu.make_async_remote_copy(
                src_ref=wire[a].at[blocks[m]], dst_ref=rbuf[a].at[m], send_sem=send_sems.at[a, m],
                recv_sem=recv_sems.at[a, m], device_id=(px, py, c), device_id_type=pl.DeviceIdType.MESH)

        def post(a, half):
            ref = gout[a].at[rows(a, half), :]
            return pltpu.make_async_remote_copy(
                src_ref=ref, dst_ref=ref, send_sem=post_send.at[a], recv_sem=post_recv.at[a],
                device_id=sibling, device_id_type=pl.DeviceIdType.MESH)

        def small_copy(kk):
            peer = (x ^ (kk >> 2), y ^ ((kk >> 1) & 1), c ^ (kk & 1))
            return pltpu.make_async_remote_copy(
                src_ref=sm_in, dst_ref=sbuf.at[kk], send_sem=sm_send.at[kk - 1], recv_sem=sm_recv.at[kk - 1],
                device_id=peer, device_id_type=pl.DeviceIdType.MESH)

        local = [[pltpu.make_async_copy(pin[a].at[blocks[k], rows(a, c), :], own[a].at[blocks[k]], loc_sems.at[a, k])
                  for k in range(4)] for a in range(n)]
        pres = [[pre(a, k) for k in range(4)] for a in range(n)]
        smalls = [small_copy(kk) for kk in range(1, 8)]
        for a in range(n):
            for k in range(4):
                local[a][k].start()
                pres[a][k].start()
        for cp in smalls:
            cp.start()
        sbuf[0] = sm_in[...]
        sends = []
        for a in range(n):
            for k in range(4):
                local[a][k].wait()
                pres[a][k].wait_recv()
                tot = own[a][blocks[k]] + sib[a][blocks[k]]
                if k == 2:
                    ici(a, 0).wait_recv()
                    tot = tot + rbuf[a][0].astype(F32)
                own[a][blocks[k]] = tot
                if k < 3:
                    wire[a][blocks[k]] = tot.astype(BF16)
                    cp = ici(a, k)
                    cp.start()
                    sends.append(cp)
        for cp in smalls:
            cp.wait_recv()
        total = sbuf[me]
        for d in range(1, 8):
            total = total + sbuf[me ^ d]
        sm_out[...] = total
        posts = []
        for a in range(n):
            fin = own[a][mine]
            for m in (1, 2):
                ici(a, m).wait_recv()
                fin = fin + rbuf[a][m].astype(F32)
            gout[a][rows(a, c), :] = fin
            cp = post(a, c)
            cp.start()
            posts.append(cp)
        for a in range(n):
            post(a, 1 - c).wait_recv()
        for cp in [cp for row in pres for cp in row] + sends + smalls + posts:
            cp.wait_send()

    vmem = pl.BlockSpec(memory_space=pltpu.VMEM)
    dma = pltpu.SemaphoreType.DMA
    return pl.pallas_call(
        body, name="reduce_grads",
        in_specs=[pl.BlockSpec(memory_space=pl.ANY)] * n + [vmem], out_specs=[vmem] * (n + 1),
        out_shape=[jax.ShapeDtypeStruct(sh, F32) for sh in shapes] + [jax.ShapeDtypeStruct(small.shape, F32)],
        scratch_shapes=([pltpu.VMEM((4,) + hs, F32) for hs in halves] + [pltpu.VMEM((4,) + hs, F32) for hs in halves]
                        + [pltpu.VMEM((4,) + hs, BF16) for hs in halves]
                        + [pltpu.VMEM((3,) + hs, BF16) for hs in halves]
                        + [pltpu.VMEM((8,) + small.shape, F32), dma((n, 3)), dma((n, 3)), dma((n, 4)),
                           dma((n, 4)), dma((n, 4)), dma((n,)), dma((n,)), dma((7,)), dma((7,))]),
        compiler_params=pltpu.CompilerParams(vmem_limit_bytes=VMEM_LIMIT),
    )(*parts, small)


def _adamw_update(w_ref, g_ref, m_ref, v_ref, d_ref, nm_ref, nv_ref):
    gv = g_ref[...]
    nm = ADAM_B1 * m_ref[...] + (1.0 - ADAM_B1) * gv
    nv = ADAM_B2 * v_ref[...] + (1.0 - ADAM_B2) * (gv * gv)
    m_hat = nm / (1.0 - ADAM_B1 ** ADAM_STEP)
    v_hat = nv / (1.0 - ADAM_B2 ** ADAM_STEP)
    d_ref[...] = -ADAM_LR * (m_hat / (jnp.sqrt(v_hat) + ADAM_EPS) + ADAM_WD * w_ref[...])
    nm_ref[...] = nm
    nv_ref[...] = nv


def _adamw_small(ws, gs, ms, vs):
    k = len(ws)

    def body(*refs):
        ins, outs = refs[:4 * k], refs[4 * k:]
        for a in range(k):
            _adamw_update(ins[a], ins[k + a], ins[2 * k + a], ins[3 * k + a], outs[a], outs[k + a], outs[2 * k + a])

    out = pl.pallas_call(
        body, name="adamw_small",
        out_shape=[jax.ShapeDtypeStruct(w.shape, F32) for w in ws] * 3,
        compiler_params=pltpu.CompilerParams(vmem_limit_bytes=VMEM_LIMIT),
    )(*ws, *gs, *ms, *vs)
    return out[:k], out[k:2 * k], out[2 * k:]


def _adamw(w, g, m, v, name):
    shape = w.shape
    w2, g2, m2, v2 = (a.reshape((-1, shape[-1])) for a in (w, g, m, v))

    def body(w_ref, g_ref, m_ref, v_ref, d_ref, nm_ref, nv_ref):
        _adamw_update(w_ref, g_ref, m_ref, v_ref, d_ref, nm_ref, nv_ref)

    rows, cols = w2.shape
    nblk = cols // 256 if cols % 256 == 0 and rows >= 64 else 1
    blk = pl.BlockSpec((rows, cols // nblk), lambda j: (0, j))
    out = pl.pallas_call(
        body, name=name, grid=(nblk,), in_specs=[blk] * 4, out_specs=[blk] * 3,
        out_shape=[jax.ShapeDtypeStruct(w2.shape, F32)] * 3,
        compiler_params=_cparams("parallel"),
    )(w2, g2, m2, v2)
    return tuple(a.reshape(shape) for a in out)


def kernel(x, meta_tokens, norm_g, w_in, q_norm_g, w_q_up, kv_norm_g, w_kv_up, conv_w, attn_out_g, conv_out_g, w_out, final_norm_g, loss_target, m_meta_tokens, m_norm_g, m_w_in, m_q_norm_g, m_w_q_up, m_kv_norm_g, m_w_kv_up, m_conv_w, m_attn_out_g, m_conv_out_g, m_w_out, m_final_norm_g, v_meta_tokens, v_norm_g, v_w_in, v_q_norm_g, v_w_q_up, v_kv_norm_g, v_w_kv_up, v_conv_w, v_attn_out_g, v_conv_out_g, v_w_out, v_final_norm_g):
    nb, s, _ = x.shape
    tm = min(ROW_TILE, s)
    ta = min(ATTN_TILE, s)
    assert s % tm == 0 and s % ta == 0 and tm % 16 == 0
    r = nb * s

    tr = lambda a: jnp.transpose(a[0])
    w_head, w_in_shard, wq_p, wkv_p, g_cw, g_meta = _gather_weights(
        tr(w_in), [tr(w_q_up), tr(w_kv_up)],
        [W_Q_PIECES, W_KV_PIECES], [HEADS * QK_PAD, 1024],
        [jnp.transpose(conv_w, (1, 0, 2)), meta_tokens],
        [(0, jnp.zeros((64, Q_RANK), BF16), QK_PAD * h + NOPE + ROPE) for h in range(HEADS)])
    conv_f = jnp.transpose(g_cw[:, :, 0, :], (1, 0, 2)).reshape(3, CONV_W)
    meta_f = jnp.transpose(g_meta, (1, 0, 2)).reshape(N_META, D_MODEL)

    c_all, sa_all, sb_all = _rope_tables(N_META + s)
    tabs_m = (c_all[:N_META], sa_all[:N_META], sb_all[:N_META])
    tabs = (c_all[N_META:], sa_all[N_META:], sb_all[N_META:])
    gid = np.arange(CONV_W) // CONV_GROUP
    gmat = jnp.asarray(np.where(gid[:, None] == gid[None, :], 1.0 / CONV_GROUP, 0.0), BF16)
    ga, gc = attn_out_g, conv_out_g
    gf = final_norm_g.reshape(1, D_MODEL)

    x2d = x.reshape(r, D_MODEL)
    tgt2d = loss_target.reshape(r, D_MODEL)

    ph, q, k, v, pmh, km, vm, w_out_f, w_in_part = _fwd_proj(
        x2d, meta_f, tabs, tabs_m, norm_g, w_head, q_norm_g, wq_p, kv_norm_g, wkv_p, w_out[0].astype(BF16),
        w_in_shard, nb, s, tm)
    o, lse, w_in_p = _attn_fwd(q, k, v, km, vm, w_in_shard, w_in_part, nb, s, ta)
    dh2, dycat, dw_out, dgf, loss_acc, pt, pmt = _out_fwd_bwd(x2d, tgt2d, o, meta_f, norm_g, w_in_p, conv_f, ga, gc,
                                                              gmat, w_out_f, gf, nb, s, tm)
    dpb, do, delta, dccm, dga, dgc, dcw = _gate_bwd(dycat, o, pt, pmt, conv_f, ga, gc, gmat, nb, s, tm)
    p_out = dw_out.reshape(4, D_MODEL // 4, D_MODEL)
    dq, dk, dv, dkm, dvm, g_w_out = _attn_bwd(q, k, v, do, lse, delta, km, vm, [p_out], nb, s, ta)
    dpa, dpam, p_q, p_kv, dgq, dgkv = _up_bwd(dq, dk, dv, dkm, dvm, ph, pmh, tabs, tabs_m, wq_p, wkv_p,
                                              q_norm_g, kv_norm_g, nb, s, tm)
    gx, gmeta, p_in, dng = _in_bwd(x2d, dh2, dpa, dpb, meta_f, dpam, dccm, pmt, w_in_p, norm_g, nb, s, tm)

    flat =jnp.concatenate([dng.reshape(-1), dgq.reshape(-1), dgkv.reshape(-1), dga.reshape(-1), dgc.reshape(-1),
                            dgf.reshape(-1), dcw[:3].reshape(-1), gmeta.reshape(-1), loss_acc[0, 0:1]])
    n_small = flat.shape[0]
    rows_small = -(-n_small // 1024) * 8
    small = jnp.pad(flat, (0, rows_small * 128 - n_small)).reshape(rows_small, 128)
    g_w_in_t, g_w_q_t, g_w_kv_t, small_sum = _reduce_grads([p_in, p_q, p_kv], small)
    ssum = small_sum.reshape(-1)

    def take(off, n):
        return ssum[off:off + n], off + n

    off = 0
    g_norm, off = take(off, D_MODEL)
    g_qn, off = take(off, Q_RANK)
    g_kvn, off = take(off, KV_RANK)
    g_ga, off = take(off, CONV_W)
    g_gc, off = take(off, CONV_W)
    g_gf, off = take(off, D_MODEL)
    g_cw_all, off = take(off, 3 * CONV_W)
    g_meta_all, off = take(off, N_META * D_MODEL)
    loss = ssum[off]
    chip = 2 * lax.axis_index("x") + lax.axis_index("y")
    g_conv = lax.dynamic_slice(g_cw_all.reshape(3, CONV_W), (0, chip * 128), (3, 128))
    g_mt = lax.dynamic_slice(g_meta_all.reshape(N_META, D_MODEL), (0, chip * 256), (N_META, 256))

    grads = {
        "meta_tokens": g_mt, "norm_g": g_norm.reshape(1, -1), "w_in": g_w_in_t, "q_norm_g": g_qn.reshape(1, -1),
        "w_q_up": g_w_q_t, "kv_norm_g": g_kvn.reshape(1, -1), "w_kv_up": jnp.transpose(g_w_kv_t)[None],
        "conv_w": g_conv[None], "attn_out_g": g_ga.reshape(1, -1), "conv_out_g": g_gc.reshape(1, -1),
        "w_out": g_w_out[None], "final_norm_g": g_gf,
    }
    transposed = ("w_in", "w_q_up")
    weights = {
        "meta_tokens": (meta_tokens, m_meta_tokens, v_meta_tokens), "norm_g": (norm_g, m_norm_g, v_norm_g),
        "w_in": (w_in, m_w_in, v_w_in), "q_norm_g": (q_norm_g, m_q_norm_g, v_q_norm_g),
        "w_q_up": (w_q_up, m_w_q_up, v_w_q_up), "kv_norm_g": (kv_norm_g, m_kv_norm_g, v_kv_norm_g),
        "w_kv_up": (w_kv_up, m_w_kv_up, v_w_kv_up), "conv_w": (conv_w, m_conv_w, v_conv_w),
        "attn_out_g": (attn_out_g, m_attn_out_g, v_attn_out_g), "conv_out_g": (conv_out_g, m_conv_out_g, v_conv_out_g),
        "w_out": (w_out, m_w_out, v_w_out), "final_norm_g": (final_norm_g, m_final_norm_g, v_final_norm_g),
    }
    names = list(weights)
    small = [nme for nme in names if nme != "w_in"]

    def view(nme, a):
        if nme in transposed:
            return a if a.ndim == 2 else tr(a)
        if nme == "conv_w":
            return jnp.transpose(a.reshape(1, 3, -1), (1, 0, 2))
        if a.ndim == 3:
            return a[0]
        return a.reshape(1, -1) if a.ndim == 1 else a

    def unview(nme, a):
        if nme in transposed:
            return jnp.transpose(a)[None]
        if nme == "conv_w":
            return jnp.transpose(a, (1, 0, 2))
        return a.reshape(weights[nme][0].shape)

    res_small = _adamw_small(*[[view(nme, a) for nme, a in zip(small, col)] for col in (
        [weights[nme][0] for nme in small], [grads[nme] for nme in small],
        [weights[nme][1] for nme in small], [weights[nme][2] for nme in small])])
    w_, m_, v_ = weights["w_in"]
    res = _adamw(tr(w_), grads["w_in"], tr(m_), tr(v_), "adamw_w_in")
    upd = {"w_in": tuple(jnp.transpose(a)[None] for a in (grads["w_in"],) + res)}
    for j, nme in enumerate(small):
        upd[nme] = (unview(nme, view(nme, grads[nme])),) + tuple(unview(nme, r[j]) for r in res_small)
    grads = {nme: upd[nme][0] for nme in names}
    deltas, new_m, new_v = ([upd[nme][j] for nme in names] for j in (1, 2, 3))

    grad_x = gx.reshape(nb, s, D_MODEL)
    return (loss, grad_x, *[grads[nme] for nme in names], *deltas, *new_m, *new_v)
```

```python
import functools

import jax
import jax.numpy as jnp
import numpy as np
from jax import lax
from jax.experimental import pallas as pl
from jax.experimental.pallas import tpu as pltpu

F32 = jnp.float32
BF16 = jnp.bfloat16

D_MODEL = 1024
N_META = 16
HEADS = 4
NOPE = 128
ROPE = 64
VDIM = 128
QK_PAD = 256
Q_RANK = 256
KV_RANK = 128
CONV_W = 512
CONV_GROUP = 64
ROPE_THETA = 10000.0
EPS = 1e-6
ATTN_SCALE = (NOPE + ROPE) ** -0.5
IN_DIM = 3008
IN_PAD = 3072
HEAD_ROWS = Q_RANK + KV_RANK + ROPE
IN_HEAD = 512
IN_TAIL = IN_PAD - IN_HEAD
BLK_ZA, BLK_CB, BLK_CC, BLK_CH, BLK_ZC = 0, 1, 2, 3, 4
NEG_INF = -1e30

ADAM_LR = 0.001
ADAM_B1 = 0.9
ADAM_B2 = 0.999
ADAM_EPS = 1e-08
ADAM_WD = 0.01
ADAM_STEP = 10

ROW_TILE = 512
ATTN_TILE = 256
VMEM_LIMIT = 56 * 1024 * 1024

NT = (((1,), (1,)), ((), ()))
TN = (((0,), (0,)), ((), ()))


def _cparams(*sem):
    return pltpu.CompilerParams(dimension_semantics=sem, vmem_limit_bytes=VMEM_LIMIT)


def _dot(a, b):
    return jnp.dot(a, b, preferred_element_type=F32)


def _dot_nt(a, b):
    return lax.dot_general(a, b, NT, preferred_element_type=F32)


def _dot_tn(a, b):
    return lax.dot_general(a, b, TN, preferred_element_type=F32)


def _rms(x, g):
    r = lax.rsqrt(jnp.mean(x * x, axis=-1, keepdims=True) + EPS)
    return x * r * g, r


def _rms_bwd(dy, x, r, g):
    xh = x * r
    dyg = dy * g
    dx = r * (dyg - xh * jnp.mean(dyg * xh, axis=-1, keepdims=True))
    return dx, dy * xh


def _sigmoid(z):
    return 1.0 / (1.0 + jnp.exp(-z))


def _rope(b, c, sa, sb):
    return b * c + pltpu.roll(b, 96, 1) * sa + pltpu.roll(b, 32, 1) * sb


def _rope_bwd(d, c, sa, sb):
    return d * c + pltpu.roll(d * sa, 32, 1) + pltpu.roll(d * sb, 96, 1)


def _group_mean(x, gmat):
    hi = x.astype(BF16)
    lo = (x - hi.astype(F32)).astype(BF16)
    return _dot(hi, gmat) + _dot(lo, gmat)


def _row_of(col, rows):
    return jnp.transpose(jnp.broadcast_to(col, (rows, 128)))[0:1, :]


def _rope_tables(n_pos):
    half = ROPE // 2
    inv_freq = (np.float32(1.0) / (np.float32(ROPE_THETA) ** (np.arange(half, dtype=np.float32) / np.float32(half))))
    ang = np.arange(n_pos, dtype=np.float32)[:, None] * inv_freq.astype(np.float32)[None, :]
    cos, sin = np.cos(ang).astype(np.float32), np.sin(ang).astype(np.float32)
    z = np.zeros((n_pos, half), np.float32)
    c = np.concatenate([cos, cos, z, z], axis=1)
    sa = np.concatenate([-sin, z, z, z], axis=1)
    sb = np.concatenate([z, sin, z, z], axis=1)
    return jnp.asarray(c), jnp.asarray(sa), jnp.asarray(sb)


W_IN_PIECES_1 = ((0, 80, 752, 0, 64, 0), (384, 64, 752, 384, 448, 1), (448, 16, 752, 512, 512, 1))
W_IN_PIECES_2 = ((80, 304, 752, 80, 144, 0), (464, 288, 752, 528, 528, 1))
W_Q_PIECES = ((0, 96, 256, 0, 0, 0), (96, 96, 256, 96, 96, 1))
W_KV_PIECES = ((0, 128, 128, 0, 0, 0), (128, 128, 128, 512, 512, 1))
W_OUT_PIECES = ((0, 128, 256, 0, 0, 0), (128, 128, 256, 128, 128, 1))


class _StagedGather:
    STAGES = 4

    @staticmethod
    def steps(n_steps):
        return (0, 5 * n_steps // 8, 7 * n_steps // 8, n_steps - 1)

    def __init__(self, pieces, zero_rows=None):
        self.pieces = pieces
        self.zero_rows = zero_rows

    def scratch(self):
        nk, dma = len(self.pieces), pltpu.SemaphoreType.DMA
        return [dma((nk, 3)), dma((nk, 3)), dma((nk, 3)), dma((nk, 3)), dma((nk,))]

    def vmem_scratch(self, shard_shape, out_shape):
        return [pltpu.VMEM(shard_shape, BF16), pltpu.VMEM(out_shape, BF16),
                pltpu.SemaphoreType.DMA((4 * len(self.pieces) + 2,))] + self.scratch()

    def run_vmem(self, stage, shard_ref, out_ref, scr):
        src_scr, land_scr, io_sems = scr[:3]
        spans = []
        for _, nr, per, first, rest, _ in self.pieces:
            spans += [(per * q + (first if q == 0 else rest), nr) for q in range(4)]
        if self.zero_rows is not None:
            spans.append(self.zero_rows)
        flush = [pltpu.make_async_copy(land_scr.at[r0:r0 + nr], out_ref.at[r0:r0 + nr], io_sems.at[n])
                 for n, (r0, nr) in enumerate(spans)]
        if stage == 0:
            load = pltpu.make_async_copy(shard_ref, src_scr, io_sems.at[len(spans)])
            load.start()
            if self.zero_rows is not None:
                r0, nr = self.zero_rows
                land_scr[r0:r0 + nr, :] = jnp.zeros((nr, land_scr.shape[1]), BF16)
            load.wait()
        if stage < self.STAGES:
            self.run(stage, src_scr, land_scr, scr[3:])
        for cp in flush:
            if stage == self.STAGES - 1:
                cp.start()
            if stage == self.STAGES:
                cp.wait()

    def run(self, stage, src_ref, out_ref, scr):
        send_sems, recv_sems, fwd_send, fwd_recv, loc_sems = scr
        pieces = self.pieces
        nk = len(pieces)
        x, y, c = lax.axis_index("x"), lax.axis_index("y"), lax.axis_index("c")
        mine = 2 * x + y
        chips = [(1 - x, y), (x, 1 - y), (1 - x, 1 - y)]
        chip_of = [2 * px + py for px, py in chips]
        mesh = pl.DeviceIdType.MESH

        def src(k):
            s0, nr = pieces[k][0], pieces[k][1]
            return src_ref.at[s0:s0 + nr]

        def dst(k, q):
            _, nr, per, first, rest, _ = pieces[k]
            row = per * q + first + (rest - first) * jnp.minimum(q, 1)
            return out_ref.at[pl.ds(pl.multiple_of(row, 16), nr)]

        def ici(k, j, q):
            px, py = chips[j]
            return pltpu.make_async_remote_copy(
                src_ref=src(k), dst_ref=dst(k, q), send_sem=send_sems.at[k, j], recv_sem=recv_sems.at[k, j],
                device_id=(px, py, c), device_id_type=mesh)

        def fwd(k, j):
            ref = dst(k, chip_of[j])
            return pltpu.make_async_remote_copy(
                src_ref=ref, dst_ref=ref, send_sem=fwd_send.at[k, j], recv_sem=fwd_recv.at[k, j],
                device_id=(x, y, 1 - c), device_id_type=mesh)

        def relay(k, half):
            ref = dst(k, chip_of[half])
            px, py = chips[1 - half]
            return pltpu.make_async_remote_copy(
                src_ref=ref, dst_ref=ref, send_sem=send_sems.at[k, 2], recv_sem=recv_sems.at[k, 2],
                device_id=(px, py, c), device_id_type=mesh)

        local = [pltpu.make_async_copy(src(k), dst(k, mine), loc_sems.at[k]) for k in range(nk)]
        if stage == 0:
            for cp in local:
                cp.start()
        if stage == 3:
            for cp in local:
                cp.wait()
        for half in (0, 1):
            @pl.when(c == half)
            def _(half=half):
                my_k = [k for k in range(nk) if pieces[k][5] == half]
                other_k = [k for k in range(nk) if pieces[k][5] != half]
                for k in my_k:
                    if stage == 0:
                        for j in range(2):
                            ici(k, j, mine).start()
                    elif stage == 1:
                        for j in (half, 1 - half):
                            ici(k, j, chip_of[j]).wait_recv()
                            if j == half:
                                relay(k, half).start()
                            fwd(k, j).start()
                    elif stage == 2:
                        ici(k, 2, chip_of[2]).wait_recv()
                        fwd(k, 2).start()
                    else:
                        for j in range(2):
                            ici(k, j, mine).wait_send()
                        relay(k, half).wait_send()
                        for j in range(3):
                            fwd(k, j).wait_send()
                if stage == 3:
                    for k in other_k:
                        for j in range(3):
                            fwd(k, j).wait_recv()


def _fwd_proj(x2d, meta, tabs, tabs_m, norm_g, w_head, q_norm_g, wq_p, kv_norm_g, wkv_p, w_out_shard, w_in_shard,
              nb, s, tm):
    nt = s // tm
    n = nb * nt
    n_steps = n + 1
    c_t, sa_t, sb_t = tabs
    cm_t, sam_t, sbm_t = tabs_m
    gat = _StagedGather(W_OUT_PIECES)
    gat_in = _StagedGather(W_IN_PIECES_1)
    n_sems = len(gat.scratch())
    assert n_steps >= 3

    def body(x_ref, c_ref, sa_ref, sb_ref, mt_ref, cm_ref, sam_ref, sbm_ref,
             g_ref, w_ref, gq_ref, wq_ref, gkv_ref, wkv_ref, wos_ref, wis_ref,
             p_ref, q_ref, k_ref, v_ref, pm_ref, km_ref, vm_ref, wo_ref, wi_ref, *scr):
        gat_scr, gat_in_scr = scr[:n_sems], scr[n_sems:]
        i = pl.program_id(0)
        for stage, at in enumerate(_StagedGather.steps(n_steps)):
            @pl.when(i == at)
            def _(stage=stage):
                gat_in.run_vmem(stage, wis_ref, wi_ref, gat_in_scr)
                gat.run(stage, wos_ref, wo_ref, gat_scr)

        def project(xv, c, sa, sb, p_out, q_out, k_out, v_out):
            u, _ = _rms(xv, g_ref[...])
            p = _dot_nt(u.astype(BF16), w_ref[...])
            p_out[...] = p
            qn, _ = _rms(p[:, 0:Q_RANK], gq_ref[...])
            q = _dot_nt(qn.astype(BF16), wq_ref[...])
            kvn, _ = _rms(p[:, Q_RANK:Q_RANK + KV_RANK], gkv_ref[...])
            kv = _dot_nt(kvn.astype(BF16), wkv_ref[...])
            kpe = _rope(p[:, 384:512], c, sa, sb)
            for h in range(HEADS):
                if q_out is not None:
                    pe = _rope(q[:, QK_PAD * h + NOPE:QK_PAD * (h + 1)], c, sa, sb)
                    qh = jnp.concatenate([q[:, QK_PAD * h:QK_PAD * h + NOPE], pe], axis=1)
                    q_out[0, h] = (qh * ATTN_SCALE).astype(BF16)
                k_out[0, h] = jnp.concatenate([kv[:, NOPE * h:NOPE * (h + 1)], kpe], axis=1).astype(BF16)
                v_out[0, h] = kv[:, 512 + VDIM * h:512 + VDIM * (h + 1)].astype(BF16)

        @pl.when(i < n)
        def _():
            project(x_ref[...], c_ref[...], sa_ref[...], sb_ref[...], p_ref, q_ref, k_ref, v_ref)

        @pl.when(i == n)
        def _():
            project(mt_ref[...], cm_ref[...], sam_ref[...], sbm_ref[...], pm_ref, None, km_ref, vm_ref)
            gat_in.run_vmem(gat_in.STAGES, wis_ref, wi_ref, gat_in_scr)

    cl = lambda i: jnp.minimum(i, n - 1)
    full = lambda a: pl.BlockSpec(a.shape, lambda i: (0,) * a.ndim)
    const = lambda shape: pl.BlockSpec(shape, lambda i: (0,) * len(shape))
    tab = pl.BlockSpec((tm, 128), lambda i: (cl(i) % nt, 0))
    hb = lambda w: pl.BlockSpec((1, HEADS, tm, w), lambda i: (cl(i) // nt, 0, cl(i) % nt, 0))
    whole = pl.BlockSpec(memory_space=pl.ANY)
    return pl.pallas_call(
        body, name="fwd_proj", grid=(n_steps,),
        in_specs=[pl.BlockSpec((tm, D_MODEL), lambda i: (cl(i), 0)), tab, tab, tab,
                  full(meta), full(cm_t), full(sam_t), full(sbm_t),
                  full(norm_g), full(w_head), full(q_norm_g), full(wq_p), full(kv_norm_g), full(wkv_p), whole, whole],
        out_specs=[pl.BlockSpec((tm, IN_HEAD), lambda i: (cl(i), 0)), hb(QK_PAD), hb(QK_PAD), hb(VDIM),
                   const((N_META, IN_HEAD)), const((1, HEADS, N_META, QK_PAD)), const((1, HEADS, N_META, VDIM)),
                   whole, whole],
        out_shape=[jax.ShapeDtypeStruct((nb * s, IN_HEAD), F32),
                   jax.ShapeDtypeStruct((nb, HEADS, s, QK_PAD), BF16),
                   jax.ShapeDtypeStruct((nb, HEADS, s, QK_PAD), BF16),
                   jax.ShapeDtypeStruct((nb, HEADS, s, VDIM), BF16),
                   jax.ShapeDtypeStruct((N_META, IN_HEAD), F32),
                   jax.ShapeDtypeStruct((1, HEADS, N_META, QK_PAD), BF16),
                   jax.ShapeDtypeStruct((1, HEADS, N_META, VDIM), BF16),
                   jax.ShapeDtypeStruct((D_MODEL, D_MODEL), BF16),
                   jax.ShapeDtypeStruct((IN_PAD, D_MODEL), BF16)],
        scratch_shapes=gat.scratch() + gat_in.vmem_scratch(w_in_shard.shape, (IN_PAD, D_MODEL)),
        compiler_params=_cparams("arbitrary"),
    )(x2d, c_t, sa_t, sb_t, meta, cm_t, sam_t, sbm_t, norm_g, w_head, q_norm_g, wq_p, kv_norm_g, wkv_p, w_out_shard,
      w_in_shard)


def _attn_fwd(q, k, v, km, vm, w_in_shard, w_in_part, nb, s, tq):
    nq = s // tq
    n_steps = nb * HEADS
    gat = _StagedGather(W_IN_PIECES_2, zero_rows=(HEAD_ROWS, IN_HEAD - HEAD_ROWS))
    assert n_steps >= 3

    def body(q_ref, k_ref, v_ref, km_ref, vm_ref, ws_ref, _, o_ref, lse_ref, w_ref, s_scr, p_scr, *gat_scr):
        step = pl.program_id(0) * HEADS + pl.program_id(1)
        for stage, at in enumerate(_StagedGather.steps(n_steps)):
            @pl.when(step == at)
            def _(stage=stage):
                gat.run_vmem(stage, ws_ref, w_ref, gat_scr)

        row = lax.broadcasted_iota(jnp.int32, (tq, tq), 0)
        col = lax.broadcasted_iota(jnp.int32, (tq, tq), 1)
        def scores(i):
            slot = i % 2
            qi = q_ref[0, 0, i * tq:(i + 1) * tq, :]
            sm = _dot_nt(qi, km_ref[0, 0])
            m128 = None
            for j in range(i + 1):
                sc = _dot_nt(qi, k_ref[0, 0, j * tq:(j + 1) * tq, :])
                if j == i:
                    sc = jnp.where(col <= row, sc, NEG_INF)
                s_scr[slot, :, j * tq:(j + 1) * tq] = sc
                mx = sc[:, 0:128]
                for c0 in range(128, tq, 128):
                    mx = jnp.maximum(mx, sc[:, c0:c0 + 128])
                m128 = mx if m128 is None else jnp.maximum(m128, mx)
            return sm, jnp.maximum(jnp.max(m128, axis=1, keepdims=True), jnp.max(sm, axis=1, keepdims=True))

        def weighted_sum(i, pm, l):
            n = (i + 1) * tq
            acc = _dot(p_scr[i % 2, :, 0:n], v_ref[0, 0, 0:n, :]) + _dot(pm.astype(BF16), vm_ref[0, 0])
            o_ref[0, 0, i * tq:(i + 1) * tq, :] = acc / l

        nxt, pending = scores(0), None
        for i in range(nq):
            slot = i % 2
            sm, m = nxt
            if i + 1 < nq:
                nxt = scores(i + 1)
            pm = jnp.exp(sm - m)
            l128 = None
            for j in range(i + 1):
                p = jnp.exp(s_scr[slot, :, j * tq:(j + 1) * tq] - m)
                p_scr[slot, :, j * tq:(j + 1) * tq] = p.astype(BF16)
                ps = p[:, 0:128]
                for c0 in range(128, tq, 128):
                    ps = ps + p[:, c0:c0 + 128]
                l128 = ps if l128 is None else l128 + ps
            l = jnp.sum(l128, axis=1, keepdims=True) + jnp.sum(pm, axis=1, keepdims=True)
            lse_ref[0, 0, :, i * tq:(i + 1) * tq] = _row_of(m + jnp.log(l), tq)
            if pending is not None:
                weighted_sum(*pending)
            pending = (i, pm, l)
        weighted_sum(*pending)

        @pl.when(step == n_steps - 1)
        def _():
            gat.run_vmem(gat.STAGES, ws_ref, w_ref, gat_scr)

    hblk = lambda w: pl.BlockSpec((1, 1, s, w), lambda b, h: (b, h, 0, 0))
    mblk = lambda w: pl.BlockSpec((1, 1, N_META, w), lambda b, h: (0, h, 0, 0))
    whole = pl.BlockSpec(memory_space=pl.ANY)
    return pl.pallas_call(
        body, name="attn_fwd", grid=(nb, HEADS),
        in_specs=[hblk(QK_PAD), hblk(QK_PAD), hblk(VDIM), mblk(QK_PAD), mblk(VDIM), whole, whole],
        out_specs=[hblk(VDIM), pl.BlockSpec((1, 1, 1, s), lambda b, h: (b, h, 0, 0)), whole],
        out_shape=[jax.ShapeDtypeStruct((nb, HEADS, s, VDIM), F32),
                   jax.ShapeDtypeStruct((nb, HEADS, 1, s), F32),
                   jax.ShapeDtypeStruct(w_in_part.shape, BF16)],
        input_output_aliases={6: 2},
        scratch_shapes=[pltpu.VMEM((2, tq, s), F32), pltpu.VMEM((2, tq, s), BF16)]
        + gat.vmem_scratch(w_in_shard.shape, w_in_part.shape),
        compiler_params=_cparams("arbitrary", "arbitrary"),
    )(q, k, v, km, vm, w_in_shard, w_in_part)


def _shift_rows(a, prev, n_rows):
    rid = lax.broadcasted_iota(jnp.int32, a.shape, 0)
    a1 = jnp.where(rid == 0, prev[7:8, :], pltpu.roll(a, 1, 0))
    a2 = jnp.where(rid == 0, prev[6:7, :], jnp.where(rid == 1, prev[7:8, :], pltpu.roll(a, 2, 0)))
    return a1, a2


def _attn_gate(o, za, ga_h):
    on, r = _rms(o, ga_h)
    return on * (za * _sigmoid(za)), on, r


def _out_fwd_bwd(x2d, tgt2d, o, meta, norm_g, w_in_p, conv_w, ga, gc, gmat, w_out, gf, nb, s, tm):
    nt = s // tm
    r = nb * s

    def body(x_ref, t_ref, o_ref, mt_ref, g_ref, wi_ref, cw_ref, ga_ref, gc_ref, gm_ref, w_ref, gf_ref,
             dh_ref, dy_ref, dw_ref, dgf_ref, loss_ref, p_ref, pm_ref, last_cc):
        i = pl.program_id(0)
        blk = lambda ref, j, rows=slice(None): ref[rows, 512 * j:512 * (j + 1)]

        def tail(xv):
            u, _ = _rms(xv, g_ref[...])
            return _dot_nt(u.astype(BF16), wi_ref[IN_HEAD:IN_PAD, :])

        @pl.when(i == 0)
        def _():
            dw_ref[...] = jnp.zeros_like(dw_ref)
            dgf_ref[...] = jnp.zeros_like(dgf_ref)
            loss_ref[...] = jnp.zeros_like(loss_ref)
            last_cc[...] = jnp.zeros_like(last_cc)
            pm_ref[...] = tail(mt_ref[...])

        u16 = _rms(x_ref[...], g_ref[...])[0].astype(BF16)

        def project(j):
            p_ref[:, 512 * j:512 * (j + 1)] = _dot_nt(u16, wi_ref[IN_HEAD + 512 * j:IN_HEAD + 512 * (j + 1), :])

        project(BLK_ZA)
        project(BLK_CC)
        project(BLK_CH)
        ya = []
        for h in range(HEADS):
            y, _, _ = _attn_gate(o_ref[0, h], p_ref[:, 512 * BLK_ZA + VDIM * h:512 * BLK_ZA + VDIM * (h + 1)],
                                 ga_ref[:, VDIM * h:VDIM * (h + 1)])
            ya.append(y)
        project(BLK_CB)
        project(BLK_ZC)
        cc = blk(p_ref, BLK_CC) * blk(p_ref, BLK_CH)
        meta_cc = blk(pm_ref, BLK_CC, slice(8, 16)) * blk(pm_ref, BLK_CH, slice(8, 16))
        prev = jnp.where(i % nt == 0, meta_cc, last_cc[...])
        last_cc[...] = cc[tm - 8:tm, :]
        cc1, cc2 = _shift_rows(cc, prev, tm)
        yc = blk(p_ref, BLK_CB) * (cw_ref[0:1, :] * cc2 + cw_ref[1:2, :] * cc1 + cw_ref[2:3, :] * cc)
        rg = lax.rsqrt(_group_mean(yc * yc, gm_ref[...]) + EPS)
        zc = blk(p_ref, BLK_ZC)
        yconv = yc * rg * gc_ref[...] * (zc * _sigmoid(zc))
        ycat = jnp.concatenate(ya + [yconv], axis=1).astype(BF16)
        h2 = x_ref[...] + _dot(ycat, w_ref[...])
        gfv = gf_ref[...]
        y, r2 = _rms(h2, gfv)
        e = y - t_ref[...]
        loss_ref[...] += 0.5 * jnp.sum(e * e) / D_MODEL
        dyv = e * (1.0 / D_MODEL)
        dh2, dgf = _rms_bwd(dyv, h2, r2, gfv)
        dgf_ref[...] += jnp.sum(dgf, axis=0, keepdims=True)
        dh_ref[...] = dh2
        dhb = dh2.astype(BF16)
        dy_ref[...] = _dot_nt(dhb, w_ref[...])
        dw_ref[...] += _dot_tn(ycat, dhb)

    row = lambda w: pl.BlockSpec((tm, w), lambda i: (i, 0))
    const = lambda shape: pl.BlockSpec(shape, lambda i: (0,) * len(shape))
    full = lambda a: const(a.shape)
    return pl.pallas_call(
        body, name="out_fwd_bwd", grid=(nb * nt,),
        in_specs=[row(D_MODEL), row(D_MODEL),
                  pl.BlockSpec((1, HEADS, tm, VDIM), lambda i: (i // nt, 0, i % nt, 0)),
                  full(meta), full(norm_g), full(w_in_p),
                  full(conv_w), full(ga), full(gc), full(gmat), full(w_out), full(gf)],
        out_specs=[row(D_MODEL), row(D_MODEL), const((D_MODEL, D_MODEL)), const((1, D_MODEL)), const((1, 128)),
                   row(IN_TAIL), const((N_META, IN_TAIL))],
        out_shape=[jax.ShapeDtypeStruct((r, D_MODEL), F32), jax.ShapeDtypeStruct((r, D_MODEL), F32),
                   jax.ShapeDtypeStruct((D_MODEL, D_MODEL), F32), jax.ShapeDtypeStruct((1, D_MODEL), F32),
                   jax.ShapeDtypeStruct((1, 128), F32),
                   jax.ShapeDtypeStruct((r, IN_TAIL), F32), jax.ShapeDtypeStruct((N_META, IN_TAIL), F32)],
        scratch_shapes=[pltpu.VMEM((8, 512), F32)],
        compiler_params=_cparams("arbitrary"),
    )(x2d, tgt2d, o, meta, norm_g, w_in_p, conv_w, ga, gc, gmat, w_out, gf)


def _gate_bwd(dycat, o, p, pm, conv_w, ga, gc, gmat, nb, s, tm):
    nt = s // tm
    r = nb * s
    ext = tm + 8
    prev_idx = lambda i: jnp.maximum(i * (tm // 8) - 1, 0)
    next_idx = lambda i: jnp.minimum((i + 1) * (tm // 8), r // 8 - 1)

    def body(dya_ref, dyc_ref, dycn_ref, o_ref, za_ref, cb_ref, cbn_ref, cc_ref, ccp_ref, ccn_ref,
             ch_ref, chp_ref, chn_ref, zc_ref, zcn_ref, mc_ref, mh_ref, cw_ref, ga_ref, gc_ref, gm_ref,
             dpb_ref, do_ref, dl_ref, dccm_ref, dga_ref, dgc_ref, dcw_ref):
        i = pl.program_id(0)

        @pl.when(i == 0)
        def _():
            dga_ref[...] = jnp.zeros_like(dga_ref)
            dgc_ref[...] = jnp.zeros_like(dgc_ref)
            dcw_ref[...] = jnp.zeros_like(dcw_ref)

        dga = []
        for h in range(HEADS):
            hs = slice(VDIM * h, VDIM * (h + 1))
            oh, za, gah, dya = o_ref[0, h], za_ref[:, hs], ga_ref[:, hs], dya_ref[:, hs]
            sg = _sigmoid(za)
            on, ro = _rms(oh, gah)
            don = dya * (za * sg)
            dpb_ref[:, hs] = (dya * on * (sg * (1.0 + za * (1.0 - sg)))).astype(BF16)
            do, dg = _rms_bwd(don, oh, ro, gah)
            dga.append(jnp.sum(dg, axis=0, keepdims=True))
            dob = do.astype(BF16)
            do_ref[0, h] = dob
            dl_ref[0, h] = _row_of(jnp.sum(dob.astype(F32) * oh, axis=1, keepdims=True), tm)
        dga_ref[...] += jnp.concatenate(dga, axis=1)

        cat = lambda a, b: jnp.concatenate([a[...], b[...]], axis=0)
        cch = cat(cc_ref, ccn_ref)
        chh = cat(ch_ref, chn_ref)
        cb = cat(cb_ref, cbn_ref)
        zc = cat(zc_ref, zcn_ref)
        dy = cat(dyc_ref, dycn_ref)
        first = i % nt == 0
        last = i % nt == nt - 1
        cc = cch * chh
        prev = jnp.where(first, mc_ref[8:16, :] * mh_ref[8:16, :], ccp_ref[...] * chp_ref[...])
        cc1, cc2 = _shift_rows(cc, prev, ext)
        w0, w1, w2 = cw_ref[0:1, :], cw_ref[1:2, :], cw_ref[2:3, :]
        dw = w0 * cc2 + w1 * cc1 + w2 * cc
        yc = cb * dw
        rg = lax.rsqrt(_group_mean(yc * yc, gm_ref[...]) + EPS)
        ych = yc * rg
        gcv = gc_ref[...]
        sg = _sigmoid(zc)
        dycn = dy * (zc * sg)
        dzc = dy * (ych * gcv) * (sg * (1.0 + zc * (1.0 - sg)))
        dgc_ref[...] += jnp.sum((dycn * ych)[:tm], axis=0, keepdims=True)
        dycg = dycn * gcv
        dyc = rg * (dycg - ych * _group_mean(dycg * ych, gm_ref[...]))
        rid = lax.broadcasted_iota(jnp.int32, (ext, CONV_W), 0)
        ddw = jnp.where(jnp.logical_and(last, rid >= tm), 0.0, dyc * cb)
        dcb = dyc * dw
        dcc = w2 * ddw + w1 * pltpu.roll(ddw, ext - 1, 0) + w0 * pltpu.roll(ddw, ext - 2, 0)
        dpb_ref[:, 512:1024] = dcb[:tm].astype(BF16)
        dpb_ref[:, 1024:1536] = (dcc * chh)[:tm].astype(BF16)
        dpb_ref[:, 1536:2048] = (dcc * cch)[:tm].astype(BF16)
        dpb_ref[:, 2048:2560] = dzc[:tm].astype(BF16)
        rs = lambda a: jnp.sum(a[:tm], axis=0, keepdims=True)
        dcw_ref[0:1, :] += rs(ddw * cc2)
        dcw_ref[1:2, :] += rs(ddw * cc1)
        dcw_ref[2:3, :] += rs(ddw * cc)

        @pl.when(first)
        def _():
            d0, d1 = ddw[0:1, :], ddw[1:2, :]
            r8 = lax.broadcasted_iota(jnp.int32, (8, CONV_W), 0)
            dccm_ref[0] = jnp.where(r8 == 7, w1 * d0 + w0 * d1, jnp.where(r8 == 6, w0 * d0, 0.0))

    row = lambda j: pl.BlockSpec((tm, 512), lambda i: (i, j))
    prv = lambda j: pl.BlockSpec((8, 512), lambda i: (prev_idx(i), j))
    nxt = lambda j: pl.BlockSpec((8, 512), lambda i: (next_idx(i), j))
    mblk = lambda j: pl.BlockSpec((N_META, 512), lambda i: (0, j))
    full = lambda a: pl.BlockSpec(a.shape, lambda i: (0,) * a.ndim)
    hb = lambda w: pl.BlockSpec((1, HEADS, tm, w), lambda i: (i // nt, 0, i % nt, 0))
    acc = lambda rr: pl.BlockSpec((rr, 512), lambda i: (0, 0))
    return pl.pallas_call(
        body, name="gate_bwd", grid=(nb * nt,),
        in_specs=[row(0), row(1), nxt(1), hb(VDIM),
                  row(BLK_ZA), row(BLK_CB), nxt(BLK_CB), row(BLK_CC), prv(BLK_CC), nxt(BLK_CC),
                  row(BLK_CH), prv(BLK_CH), nxt(BLK_CH), row(BLK_ZC), nxt(BLK_ZC),
                  mblk(BLK_CC), mblk(BLK_CH), full(conv_w), full(ga), full(gc), full(gmat)],
        out_specs=[pl.BlockSpec((tm, 2560), lambda i: (i, 0)), hb(VDIM),
                   pl.BlockSpec((1, HEADS, 1, tm), lambda i: (i // nt, 0, 0, i % nt)),
                   pl.BlockSpec((1, 8, 512), lambda i: (i // nt, 0, 0)),
                   acc(1), acc(1), acc(8)],
        out_shape=[jax.ShapeDtypeStruct((r, 2560), BF16), jax.ShapeDtypeStruct((nb, HEADS, s, VDIM), BF16),
                   jax.ShapeDtypeStruct((nb, HEADS, 1, s), F32), jax.ShapeDtypeStruct((nb, 8, 512), F32),
                   jax.ShapeDtypeStruct((1, 512), F32), jax.ShapeDtypeStruct((1, 512), F32),
                   jax.ShapeDtypeStruct((8, 512), F32)],
        compiler_params=_cparams("arbitrary"),
    )(dycat, dycat, dycat, o, p, p, p, p, p, p, p, p, p, p, p, pm, pm, conv_w, ga, gc, gmat)


class _StagedReduce:
    LOC, PRE_S, PRE_R, ICI_S, ICI_R, POST_S, POST_R, OUT, N_SEM = 0, 1, 2, 3, 6, 9, 10, 11, 12

    def __init__(self, shard_shape):
        self.half = (shard_shape[0] // 2, shard_shape[1])

    def scratch(self):
        h = self.half
        return [pltpu.VMEM((4,) + h, F32), pltpu.VMEM((4,) + h, F32), pltpu.VMEM((4,) + h, BF16),
                pltpu.VMEM((3,) + h, BF16), pltpu.VMEM(h, F32), pltpu.SemaphoreType.DMA((self.N_SEM,))]

    def run(self, stage, pin, gout, scr):
        own, sib, wire, rbuf, fin, sems = scr
        r2 = self.half[0]
        x, y, c = lax.axis_index("x"), lax.axis_index("y"), lax.axis_index("c")
        mine = 2 * x + y
        sibling = (x, y, 1 - c)
        chips = [(1 - x, y), (x, 1 - y), (1 - x, 1 - y)]
        rows = lambda half: pl.ds(pl.multiple_of(half * r2, r2), r2)
        mesh = pl.DeviceIdType.MESH

        loc = pltpu.make_async_copy(pin.at[:, rows(c), :], own, sems.at[self.LOC])
        pre = pltpu.make_async_remote_copy(
            src_ref=pin.at[:, rows(1 - c), :], dst_ref=sib, send_sem=sems.at[self.PRE_S],
            recv_sem=sems.at[self.PRE_R], device_id=sibling, device_id_type=mesh)

        def ici(j):
            px, py = chips[j]
            return pltpu.make_async_remote_copy(
                src_ref=wire.at[2 * px + py], dst_ref=rbuf.at[j], send_sem=sems.at[self.ICI_S + j],
                recv_sem=sems.at[self.ICI_R + j], device_id=(px, py, c), device_id_type=mesh)

        def post(half):
            return pltpu.make_async_remote_copy(
                src_ref=fin, dst_ref=gout.at[rows(half), :], send_sem=sems.at[self.POST_S],
                recv_sem=sems.at[self.POST_R], device_id=sibling, device_id_type=mesh)

        keep = pltpu.make_async_copy(fin, gout.at[rows(c), :], sems.at[self.OUT])
        if stage == 0:
            loc.start()
            pre.start()
        elif stage == 1:
            loc.wait()
            pre.wait_recv()
            for blk in range(4):
                tot = own[blk] + sib[blk]
                own[blk] = tot
                wire[blk] = tot.astype(BF16)
            for j in range(3):
                ici(j).start()
        elif stage == 2:
            for j in range(3):
                ici(j).wait_recv()
            tot = own[mine]
            for j in range(3):
                tot = tot + rbuf[j].astype(F32)
            fin[...] = tot
            post(c).start()
            keep.start()
        else:
            post(1 - c).wait_recv()
            pre.wait_send()
            for j in range(3):
                ici(j).wait_send()
            post(c).wait_send()
            keep.wait()


def _attn_bwd(q, k, v, do, lse, delta, km, vm, early, nb, s, t):
    n = s // t
    ne = len(early)
    reds = [_StagedReduce(a.shape[1:]) for a in early]
    n_steps = HEADS * nb
    assert n_steps >= 4

    def body(q_ref, k_ref, v_ref, do_ref, lse_ref, dl_ref, km_ref, vm_ref, *rest):
        pin_refs, rest = rest[:ne], rest[ne:]
        dq_ref, dk_ref, dv_ref, dkm_ref, dvm_ref = rest[:5]
        gout_refs, (p_scr, ds_scr, dq_acc), red_scr = rest[5:5 + ne], rest[5 + ne:8 + ne], rest[8 + ne:]
        b = pl.program_id(1)
        step = pl.program_id(0) * nb + b
        for stage, at in enumerate((0, 1, n_steps - 2, n_steps - 1)):
            @pl.when(step == at)
            def _(stage=stage):
                for a, red in enumerate(reds):
                    red.run(stage, pin_refs[a], gout_refs[a], red_scr[6 * a:6 * a + 6])

        @pl.when(b == 0)
        def _():
            dkm_ref[...] = jnp.zeros_like(dkm_ref)
            dvm_ref[...] = jnp.zeros_like(dvm_ref)

        kr = lax.broadcasted_iota(jnp.int32, (t, t), 0)
        qc = lax.broadcasted_iota(jnp.int32, (t, t), 1)
        km_v, vm_v = km_ref[0, 0], vm_ref[0, 0]
        ptm = jnp.exp(_dot_nt(km_v, q_ref[0, 0]) - lse_ref[0, 0])
        dstm = (ptm * (_dot_nt(vm_v, do_ref[0, 0]) - dl_ref[0, 0])).astype(BF16)
        dkm_ref[0] += _dot(dstm, q_ref[0, 0])
        dvm_ref[0] += _dot(ptm.astype(BF16), do_ref[0, 0])
        dq_acc[...] = _dot_tn(dstm, km_v)
        def tiles(j):
            slot = j % 2
            kj = k_ref[0, 0, j * t:(j + 1) * t, :]
            vj = v_ref[0, 0, j * t:(j + 1) * t, :]
            def products(i):
                cs = slice(i * t, (i + 1) * t)
                return _dot_nt(kj, q_ref[0, 0, cs, :]), _dot_nt(vj, do_ref[0, 0, cs, :])

            nxt, pending = products(j), None
            for i in range(j, n):
                cs = slice(i * t, (i + 1) * t)
                st, dpt = nxt
                if i + 1 < n:
                    nxt = products(i + 1)
                if i == j:
                    st = jnp.where(kr <= qc, st, NEG_INF)
                pt = jnp.exp(st - lse_ref[0, 0, :, cs])
                dst = (pt * (dpt - dl_ref[0, 0, :, cs])).astype(BF16)
                p_scr[slot, :, cs] = pt.astype(BF16)
                ds_scr[slot, :, cs] = dst
                if pending is not None:
                    dq_acc[pending[0], :] += _dot_tn(pending[1], kj)
                pending = (cs, dst)
            dq_acc[pending[0], :] += _dot_tn(pending[1], kj)

        for j in range(n):
            slot = j % 2
            tiles(j)
            dv_ref[0, 0, j * t:(j + 1) * t, :] = _dot(p_scr[slot, :, j * t:s], do_ref[0, 0, j * t:s, :]).astype(BF16)
            dk_ref[0, 0, j * t:(j + 1) * t, :] = _dot(ds_scr[slot, :, j * t:s], q_ref[0, 0, j * t:s, :]).astype(BF16)
        dq_ref[0, 0] = dq_acc[...].astype(BF16)

    big = lambda w: pl.BlockSpec((1, 1, s, w), lambda h, b: (b, h, 0, 0))
    rowv = pl.BlockSpec((1, 1, 1, s), lambda h, b: (b, h, 0, 0))
    mk = lambda w: pl.BlockSpec((1, 1, N_META, w), lambda h, b: (0, h, 0, 0))
    mo = lambda w: pl.BlockSpec((1, N_META, w), lambda h, b: (h, 0, 0))
    return pl.pallas_call(
        body, name="attn_bwd", grid=(HEADS, nb),
        in_specs=[big(QK_PAD), big(QK_PAD), big(VDIM), big(VDIM), rowv, rowv, mk(QK_PAD), mk(VDIM)]
        + [pl.BlockSpec(memory_space=pl.ANY)] * ne,
        out_specs=[big(QK_PAD), big(QK_PAD), big(VDIM), mo(QK_PAD), mo(VDIM)]
        + [pl.BlockSpec(memory_space=pl.ANY)] * ne,
        out_shape=[jax.ShapeDtypeStruct((nb, HEADS, s, QK_PAD), BF16),
                   jax.ShapeDtypeStruct((nb, HEADS, s, QK_PAD), BF16),
                   jax.ShapeDtypeStruct((nb, HEADS, s, VDIM), BF16),
                   jax.ShapeDtypeStruct((HEADS, N_META, QK_PAD), F32),
                   jax.ShapeDtypeStruct((HEADS, N_META, VDIM), F32)]
        + [jax.ShapeDtypeStruct(a.shape[1:], F32) for a in early],
        scratch_shapes=[pltpu.VMEM((2, t, s), BF16), pltpu.VMEM((2, t, s), BF16), pltpu.VMEM((s, QK_PAD), F32)]
        + [sc for red in reds for sc in red.scratch()],
        compiler_params=_cparams("arbitrary", "arbitrary"),
    )(q, k, v, do, lse, delta, km, vm, *early)


def _up_bwd(dq, dk, dv, dkm, dvm, p, pm, tabs, tabs_m, wq_p, wkv_p, gq, gkv, nb, s, tm):
    nt = s // tm
    n = nb * nt
    c_t, sa_t, sb_t = tabs
    cm_t, sam_t, sbm_t = tabs_m

    def kv_path(dkh, dvh, pa, c, sa, sb, wkv, gkvv):
        dkpe = dkh[0][:, NOPE:]
        for h in range(1, HEADS):
            dkpe = dkpe + dkh[h][:, NOPE:]
        dkr = _rope_bwd(dkpe, c, sa, sb)
        dkv = jnp.concatenate([d[:, :NOPE] for d in dkh] + list(dvh), axis=1).astype(BF16)
        ckv = pa[:, Q_RANK:Q_RANK + KV_RANK]
        kvn, rkv = _rms(ckv, gkvv)
        dckv, dg = _rms_bwd(_dot(dkv, wkv), ckv, rkv, gkvv)
        return dckv, dkr, kvn.astype(BF16), dkv, jnp.sum(dg, axis=0, keepdims=True)

    def body(dq_ref, dk_ref, dv_ref, pa_ref, c_ref, sa_ref, sb_ref,
             dkm_ref, dvm_ref, pam_ref, cm_ref, sam_ref, sbm_ref,
             wq_ref, wkv_ref, gq_ref, gkv_ref,
             dpa_ref, dpam_ref, pq_ref, pkv_ref, dgq_ref, dgkv_ref, dwq_ref, dwkv_ref):
        i = pl.program_id(0)

        @pl.when(i == 0)
        def _():
            dwq_ref[...] = jnp.zeros_like(dwq_ref)
            dwkv_ref[...] = jnp.zeros_like(dwkv_ref)
            dgq_ref[...] = jnp.zeros_like(dgq_ref)
            dgkv_ref[...] = jnp.zeros_like(dgkv_ref)

        @pl.when(i < n)
        def _():
            c, sa, sb = c_ref[...], sa_ref[...], sb_ref[...]
            pa = pa_ref[...]
            parts = []
            for h in range(HEADS):
                dqh = dq_ref[0, h].astype(F32) * ATTN_SCALE
                parts += [dqh[:, :NOPE], _rope_bwd(dqh[:, NOPE:], c, sa, sb)]
            dql = jnp.concatenate(parts, axis=1).astype(BF16)
            cq = pa[:, 0:Q_RANK]
            gqv = gq_ref[...]
            qn, rq = _rms(cq, gqv)
            dwq_ref[...] += _dot_tn(dql, qn.astype(BF16))
            dcq, dg = _rms_bwd(_dot(dql, wq_ref[...]), cq, rq, gqv)
            dgq_ref[...] += jnp.sum(dg, axis=0, keepdims=True)
            dckv, dkr, kvn, dkv, dgk = kv_path([dk_ref[0, h].astype(F32) for h in range(HEADS)],
                                               [dv_ref[0, h].astype(F32) for h in range(HEADS)],
                                               pa, c, sa, sb, wkv_ref[...], gkv_ref[...])
            dwkv_ref[...] += _dot_tn(dkv, kvn)
            dgkv_ref[...] += dgk
            dpa_ref[...] = jnp.concatenate([dcq, dckv, dkr], axis=1).astype(BF16)

        @pl.when(i == n)
        def _():
            dckv, dkr, kvn, dkv, dgk = kv_path([dkm_ref[h] for h in range(HEADS)],
                                               [dvm_ref[h] for h in range(HEADS)],
                                               pam_ref[...], cm_ref[...], sam_ref[...], sbm_ref[...],
                                               wkv_ref[...], gkv_ref[...])
            dwkv_ref[...] += _dot_tn(dkv, kvn)
            dgkv_ref[...] += dgk
            dpam_ref[...] = jnp.concatenate([jnp.zeros((N_META, Q_RANK), F32), dckv, dkr], axis=1)
            for h in range(HEADS):
                pq_ref[h] = dwq_ref[QK_PAD * h:QK_PAD * h + NOPE + ROPE, :]
                pkv_ref[h, 0:NOPE, :] = dwkv_ref[NOPE * h:NOPE * (h + 1), :]
                pkv_ref[h, NOPE:NOPE + VDIM, :] = dwkv_ref[512 + VDIM * h:512 + VDIM * (h + 1), :]

    cl = lambda i: jnp.minimum(i, n - 1)
    hb = lambda w: pl.BlockSpec((1, HEADS, tm, w), lambda i: (cl(i) // nt, 0, cl(i) % nt, 0))
    tab = pl.BlockSpec((tm, 128), lambda i: (cl(i) % nt, 0))
    full = lambda a: pl.BlockSpec(a.shape, lambda i: (0,) * a.ndim)
    const = lambda shape: pl.BlockSpec(shape, lambda i: (0,) * len(shape))
    return pl.pallas_call(
        body, name="up_bwd", grid=(n + 1,),
        in_specs=[hb(QK_PAD), hb(QK_PAD), hb(VDIM), pl.BlockSpec((tm, 512), lambda i: (cl(i), 0)), tab, tab, tab,
                  full(dkm), full(dvm), pl.BlockSpec((N_META, 512), lambda i: (0, 0)),
                  full(cm_t), full(sam_t), full(sbm_t), full(wq_p), full(wkv_p), full(gq), full(gkv)],
        out_specs=[pl.BlockSpec((tm, 512), lambda i: (cl(i), 0)), const((N_META, 512)),
                   const((HEADS, NOPE + ROPE, Q_RANK)), const((HEADS, NOPE + VDIM, KV_RANK)),
                   const((1, Q_RANK)), const((1, KV_RANK))],
        out_shape=[jax.ShapeDtypeStruct((nb * s, 512), BF16), jax.ShapeDtypeStruct((N_META, 512), F32),
                   jax.ShapeDtypeStruct((HEADS, NOPE + ROPE, Q_RANK), F32),
                   jax.ShapeDtypeStruct((HEADS, NOPE + VDIM, KV_RANK), F32),
                   jax.ShapeDtypeStruct((1, Q_RANK), F32), jax.ShapeDtypeStruct((1, KV_RANK), F32)],
        scratch_shapes=[pltpu.VMEM((HEADS * QK_PAD, Q_RANK), F32), pltpu.VMEM((1024, KV_RANK), F32)],
        compiler_params=_cparams("arbitrary"),
    )(dq, dk, dv, p, c_t, sa_t, sb_t, dkm, dvm, pm, cm_t, sam_t, sbm_t, wq_p, wkv_p, gq, gkv)


def _in_bwd(x2d, dh2, dpa, dpb, meta, dpam, dccm, pm, w_in_p, norm_g, nb, s, tm):
    nt = s // tm
    n = nb * nt

    def body(x_ref, dh_ref, dpa_ref, dpb_ref, mt_ref, dpam_ref, dccm_ref, mc_ref, mh_ref, w_ref, g_ref,
             gx_ref, gm_ref, dw_hbm, dg_ref, acc_ref, sems):
        i = pl.program_id(0)

        @pl.when(i == 0)
        def _():
            acc_ref[...] = jnp.zeros_like(acc_ref)
            dg_ref[...] = jnp.zeros_like(dg_ref)

        def rows(x, dp, dres):
            g = g_ref[...]
            dpb16 = dp.astype(BF16)
            du = _dot(dpb16, w_ref[...])
            u, r1 = _rms(x, g)
            acc_ref[...] += _dot_tn(dpb16, u.astype(BF16))
            dx, dg = _rms_bwd(du, x, r1, g)
            dg_ref[...] += jnp.sum(dg, axis=0, keepdims=True)
            return dx if dres is None else dx + dres

        @pl.when(i < n)
        def _():
            dp = jnp.concatenate([dpa_ref[...], dpb_ref[...]], axis=1)
            gx_ref[...] = rows(x_ref[...], dp, dh_ref[...])

        @pl.when(i == n)
        def _():
            dcc = dccm_ref[0]
            for b in range(1, nb):
                dcc = dcc + dccm_ref[b]
            z8 = jnp.zeros((8, CONV_W), F32)
            dc = jnp.concatenate([z8, dcc * mh_ref[8:16, :]], axis=0)
            dh = jnp.concatenate([z8, dcc * mc_ref[8:16, :]], axis=0)
            z = jnp.zeros((N_META, CONV_W), F32)
            dp = jnp.concatenate([dpam_ref[...], z, z, dc, dh, z], axis=1)
            gm_ref[...] = rows(mt_ref[...], dp, None)
            per = IN_DIM // 4
            cps = [pltpu.make_async_copy(acc_ref.at[0:448], dw_hbm.at[0, 0:448], sems.at[0]),
                   pltpu.make_async_copy(acc_ref.at[512:per + 64], dw_hbm.at[0, 448:per], sems.at[1])]
            for qq in range(1, 4):
                cps.append(pltpu.make_async_copy(acc_ref.at[per * qq + 64:per * (qq + 1) + 64], dw_hbm.at[qq],
                                                 sems.at[qq + 1]))
            for cp in cps:
                cp.start()
            for cp in cps:
                cp.wait()

    cl = lambda i: jnp.minimum(i, n - 1)
    row = lambda w: pl.BlockSpec((tm, w), lambda i: (cl(i), 0))
    full = lambda a: pl.BlockSpec(a.shape, lambda i: (0,) * a.ndim)
    mblk = lambda j: pl.BlockSpec((N_META, 512), lambda i: (0, j))
    return pl.pallas_call(
        body, name="in_bwd", grid=(n + 1,),
        in_specs=[row(D_MODEL), row(D_MODEL), row(512), row(2560), full(meta), full(dpam), full(dccm),
                  mblk(BLK_CC), mblk(BLK_CH), full(w_in_p), full(norm_g)],
        out_specs=[row(D_MODEL), pl.BlockSpec((N_META, D_MODEL), lambda i: (0, 0)),
                   pl.BlockSpec(memory_space=pl.ANY), pl.BlockSpec((1, D_MODEL), lambda i: (0, 0))],
        out_shape=[jax.ShapeDtypeStruct((nb * s, D_MODEL), F32), jax.ShapeDtypeStruct((N_META, D_MODEL), F32),
                   jax.ShapeDtypeStruct((4, IN_DIM // 4, D_MODEL), F32), jax.ShapeDtypeStruct((1, D_MODEL), F32)],
        scratch_shapes=[pltpu.VMEM((IN_PAD, D_MODEL), F32), pltpu.SemaphoreType.DMA((5,))],
        compiler_params=_cparams("arbitrary"),
    )(x2d, dh2, dpa, dpb, meta, dpam, dccm, pm, pm, w_in_p, norm_g)


def _gather_weights(w_in_shard, split, pieces, out_rows, whole, zero_fills):
    ns, nw, nz = len(split), len(whole), len(zero_fills)
    flat = [(a, pc) for a in range(ns) for pc in pieces[a]]
    nk = len(flat)
    hh = HEAD_ROWS // 2
    assert hh % 16 == 0

    def body(*refs):
        ins, wins, zins = refs[1:1 + ns], refs[1 + ns:1 + ns + nw], refs[1 + ns + nw:1 + ns + nw + nz]
        n_in = 1 + ns + nw + nz
        head_ref, shard16 = refs[n_in], refs[n_in + 1]
        outs, wouts = refs[n_in + 2:n_in + 2 + ns], refs[n_in + 2 + ns:n_in + 2 + ns + nw]
        scr = refs[n_in + 2 + ns + nw:]
        stage = scr[:ns]
        (send_sems, recv_sems, fwd_send, fwd_recv, loc_sems, w_send, w_recv, w_loc, z_sems,
         h_send, h_recv, h_relay, h_pass) = scr[ns:]
        x, y, c = lax.axis_index("x"), lax.axis_index("y"), lax.axis_index("c")
        mine = 2 * x + y
        chips = [(1 - x, y), (x, 1 - y), (1 - x, 1 - y)]
        chip_of = [2 * px + py for px, py in chips]
        shard16[...] = refs[0][...].astype(BF16)
        for a in range(ns):
            stage[a][...] = ins[a][...].astype(BF16)

        def head_rows(half):
            return pl.ds(pl.multiple_of(half * hh, 16), hh)

        def head_copy(turn):
            dest = (1 - c, c, c) if turn == 0 else (c, 1 - c, c)
            return pltpu.make_async_remote_copy(
                src_ref=shard16.at[head_rows(c)], dst_ref=head_ref.at[head_rows(c)], send_sem=h_send.at[turn],
                recv_sem=h_recv.at[0], device_id=dest, device_id_type=pl.DeviceIdType.MESH)

        def head_relay():
            ref = head_ref.at[head_rows(c)]
            return pltpu.make_async_remote_copy(
                src_ref=ref, dst_ref=ref, send_sem=h_relay.at[0], recv_sem=h_recv.at[0],
                device_id=(1, 1, c), device_id_type=pl.DeviceIdType.MESH)

        def head_pass(half):
            ref = head_ref.at[head_rows(half)]
            return pltpu.make_async_remote_copy(
                src_ref=ref, dst_ref=ref, send_sem=h_pass.at[0], recv_sem=h_pass.at[1],
                device_id=(x, y, 1 - c), device_id_type=pl.DeviceIdType.MESH)

        head_ref[HEAD_ROWS:IN_HEAD, :] = jnp.zeros((IN_HEAD - HEAD_ROWS, D_MODEL), BF16)

        @pl.when(mine == 0)
        def _():
            head_copy(0).start()
            head_ref[0:HEAD_ROWS, :] = shard16[0:HEAD_ROWS, :]

        def src(k):
            a, (s0, nr, _, _, _, _) = flat[k]
            return stage[a].at[s0:s0 + nr]

        def dst(k, q):
            a, (_, nr, per, first, rest, _) = flat[k]
            row = per * q + first + (rest - first) * jnp.minimum(q, 1)
            return outs[a].at[pl.ds(pl.multiple_of(row, 16), nr)]

        def ici(k, j, q):
            px, py = chips[j]
            return pltpu.make_async_remote_copy(
                src_ref=src(k), dst_ref=dst(k, q), send_sem=send_sems.at[k, j], recv_sem=recv_sems.at[k, j],
                device_id=(px, py, c), device_id_type=pl.DeviceIdType.MESH)

        def fwd(k, j):
            ref = dst(k, chip_of[j])
            return pltpu.make_async_remote_copy(
                src_ref=ref, dst_ref=ref, send_sem=fwd_send.at[k, j], recv_sem=fwd_recv.at[k, j],
                device_id=(x, y, 1 - c), device_id_type=pl.DeviceIdType.MESH)

        def wcopy(b, j, q):
            px, py = chips[j]
            return pltpu.make_async_remote_copy(
                src_ref=wins[b], dst_ref=wouts[b].at[q], send_sem=w_send.at[b, j], recv_sem=w_recv.at[b, j],
                device_id=(px, py, c), device_id_type=pl.DeviceIdType.MESH)

        local = [pltpu.make_async_copy(src(k), dst(k, mine), loc_sems.at[k]) for k in range(nk)]
        local += [pltpu.make_async_copy(wins[b], wouts[b].at[mine], w_loc.at[b]) for b in range(nw)]
        for z, (a, _, row0) in enumerate(zero_fills):
            local.append(pltpu.make_async_copy(zins[z], outs[a].at[row0:row0 + zins[z].shape[0]], z_sems.at[z]))
        wsends = [wcopy(b, j, mine) for b in range(nw) for j in range(3)]
        for cp in local + wsends:
            cp.start()

        for half in (0, 1):
            @pl.when(c == half)
            def _(half=half):
                my_k = [k for k in range(nk) if flat[k][1][5] == half]
                other_k = [k for k in range(nk) if flat[k][1][5] != half]
                sends = [ici(k, j, mine) for k in my_k for j in range(3)]
                for cp in sends:
                    cp.start()
                passed = []
                for k in my_k:
                    for j in range(3):
                        ici(k, j, chip_of[j]).wait_recv()
                        cp = fwd(k, j)
                        cp.start()
                        passed.append(cp)
                for k in other_k:
                    for j in range(3):
                        fwd(k, j).wait_recv()
                for cp in sends + passed:
                    cp.wait_send()

        for b in range(nw):
            for j in range(3):
                wcopy(b, j, chip_of[j]).wait_recv()
        for cp in wsends:
            cp.wait_send()
        for cp in local:
            cp.wait()

        @pl.when(mine == 0)
        def _():
            head_copy(0).wait_send()
            head_copy(1).start()
            head_copy(1).wait_send()

        @pl.when(mine != 0)
        def _():
            hands_on = mine == 2 - c
            head_copy(0).wait_recv()

            @pl.when(hands_on)
            def _():
                head_relay().start()

            head_pass(c).start()
            head_pass(1 - c).wait_recv()
            head_pass(c).wait_send()

            @pl.when(hands_on)
            def _():
                head_relay().wait_send()

    vmem = pl.BlockSpec(memory_space=pltpu.VMEM)
    dma = pltpu.SemaphoreType.DMA
    zeros = [z for _, z, _ in zero_fills]
    return pl.pallas_call(
        body, name="gather_weights",
        in_specs=[vmem] * (1 + ns + nw + nz), out_specs=[vmem] * (2 + ns + nw),
        out_shape=([jax.ShapeDtypeStruct((IN_HEAD, D_MODEL), BF16), jax.ShapeDtypeStruct(w_in_shard.shape, BF16)]
                   + [jax.ShapeDtypeStruct((out_rows[a], split[a].shape[1]), BF16) for a in range(ns)]
                   + [jax.ShapeDtypeStruct((4,) + w.shape, w.dtype) for w in whole]),
        scratch_shapes=[pltpu.VMEM(a.shape, BF16) for a in split]
        + [dma((nk, 3)), dma((nk, 3)), dma((nk, 3)), dma((nk, 3)), dma((nk,)),
           dma((nw, 3)), dma((nw, 3)), dma((nw,)), dma((nz,)),
           dma((2,)), dma((1,)), dma((1,)), dma((2,))],
        compiler_params=pltpu.CompilerParams(vmem_limit_bytes=VMEM_LIMIT),
    )(w_in_shard, *split, *whole, *zeros)


def _reduce_grads(parts, small):
    n = len(parts)
    shapes = [a.shape[1:] for a in parts]
    halves = [(sh[0] // 2, sh[1]) for sh in shapes]

    def body(*refs):
        pin, sm_in = refs[:n], refs[n]
        gout, sm_out = refs[n + 1:2 * n + 1], refs[2 * n + 1]
        scr = refs[2 * n + 2:]
        own, sib, wire, rbuf = scr[:n], scr[n:2 * n], scr[2 * n:3 * n], scr[3 * n:4 * n]
        (sbuf, send_sems, recv_sems, loc_sems, pre_send, pre_recv, post_send, post_recv,
         sm_send, sm_recv) = scr[4 * n:]
        x, y, c = lax.axis_index("x"), lax.axis_index("y"), lax.axis_index("c")
        mine = 2 * x + y
        me = 4 * x + 2 * y + c
        sibling = (x, y, 1 - c)

        def rows(a, half):
            r2 = halves[a][0]
            return pl.ds(pl.multiple_of(half * r2, r2), r2)

        near = (jnp.where(c == 0, 1 - x, x), jnp.where(c == 0, y, 1 - y))
        far = (jnp.where(c == 0, x, 1 - x), jnp.where(c == 0, 1 - y, y))
        chip = lambda p: 2 * p[0] + p[1]
        blocks = [3 - mine, chip(near), chip(far), mine]

        def pre(a, k):
            q = blocks[k]
            return pltpu.make_async_remote_copy(
                src_ref=pin[a].at[q, rows(a, 1 - c), :], dst_ref=sib[a].at[q],
                send_sem=pre_send.at[a, q], recv_sem=pre_recv.at[a, q], device_id=sibling,
                device_id_type=pl.DeviceIdType.MESH)

        def ici(a, m):
            px, py = near if m < 2 else far
            return pltpu.make_async_remote_copy(
                src_ref=wire[a].at[blocks[m]], dst_ref=rbuf[a].at[m], send_sem=send_sems.at[a, m],
                recv_sem=recv_sems.at[a, m], device_id=(px, py, c), device_id_type=pl.DeviceIdType.MESH)

        def post(a, half):
            ref = gout[a].at[rows(a, half), :]
            return pltpu.make_async_remote_copy(
                src_ref=ref, dst_ref=ref, send_sem=post_send.at[a], recv_sem=post_recv.at[a],
                device_id=sibling, device_id_type=pl.DeviceIdType.MESH)

        def small_copy(kk):
            peer = (x ^ (kk >> 2), y ^ ((kk >> 1) & 1), c ^ (kk & 1))
            return pltpu.make_async_remote_copy(
                src_ref=sm_in, dst_ref=sbuf.at[kk], send_sem=sm_send.at[kk - 1], recv_sem=sm_recv.at[kk - 1],
                device_id=peer, device_id_type=pl.DeviceIdType.MESH)

        local = [[pltpu.make_async_copy(pin[a].at[blocks[k], rows(a, c), :], own[a].at[blocks[k]], loc_sems.at[a, k])
                  for k in range(4)] for a in range(n)]
        pres = [[pre(a, k) for k in range(4)] for a in range(n)]
        smalls = [small_copy(kk) for kk in range(1, 8)]
        for a in range(n):
            for k in range(4):
                local[a][k].start()
                pres[a][k].start()
        for cp in smalls:
            cp.start()
        sbuf[0] = sm_in[...]
        sends = []
        for a in range(n):
            for k in range(4):
                local[a][k].wait()
                pres[a][k].wait_recv()
                tot = own[a][blocks[k]] + sib[a][blocks[k]]
                if k == 2:
                    ici(a, 0).wait_recv()
                    tot = tot + rbuf[a][0].astype(F32)
                own[a][blocks[k]] = tot
                if k < 3:
                    wire[a][blocks[k]] = tot.astype(BF16)
                    cp = ici(a, k)
                    cp.start()
                    sends.append(cp)
        for cp in smalls:
            cp.wait_recv()
        total = sbuf[me]
        for d in range(1, 8):
            total = total + sbuf[me ^ d]
        sm_out[...] = total
        posts = []
        for a in range(n):
            fin = own[a][mine]
            for m in (1, 2):
                ici(a, m).wait_recv()
                fin = fin + rbuf[a][m].astype(F32)
            gout[a][rows(a, c), :] = fin
            cp = post(a, c)
            cp.start()
            posts.append(cp)
        for a in range(n):
            post(a, 1 - c).wait_recv()
        for cp in [cp for row in pres for cp in row] + sends + smalls + posts:
            cp.wait_send()

    vmem = pl.BlockSpec(memory_space=pltpu.VMEM)
    dma = pltpu.SemaphoreType.DMA
    return pl.pallas_call(
        body, name="reduce_grads",
        in_specs=[pl.BlockSpec(memory_space=pl.ANY)] * n + [vmem], out_specs=[vmem] * (n + 1),
        out_shape=[jax.ShapeDtypeStruct(sh, F32) for sh in shapes] + [jax.ShapeDtypeStruct(small.shape, F32)],
        scratch_shapes=([pltpu.VMEM((4,) + hs, F32) for hs in halves] + [pltpu.VMEM((4,) + hs, F32) for hs in halves]
                        + [pltpu.VMEM((4,) + hs, BF16) for hs in halves]
                        + [pltpu.VMEM((3,) + hs, BF16) for hs in halves]
                        + [pltpu.VMEM((8,) + small.shape, F32), dma((n, 3)), dma((n, 3)), dma((n, 4)),
                           dma((n, 4)), dma((n, 4)), dma((n,)), dma((n,)), dma((7,)), dma((7,))]),
        compiler_params=pltpu.CompilerParams(vmem_limit_bytes=VMEM_LIMIT),
    )(*parts, small)


def _adamw_update(w_ref, g_ref, m_ref, v_ref, d_ref, nm_ref, nv_ref):
    gv = g_ref[...]
    nm = ADAM_B1 * m_ref[...] + (1.0 - ADAM_B1) * gv
    nv = ADAM_B2 * v_ref[...] + (1.0 - ADAM_B2) * (gv * gv)
    m_hat = nm / (1.0 - ADAM_B1 ** ADAM_STEP)
    v_hat = nv / (1.0 - ADAM_B2 ** ADAM_STEP)
    d_ref[...] = -ADAM_LR * (m_hat / (jnp.sqrt(v_hat) + ADAM_EPS) + ADAM_WD * w_ref[...])
    nm_ref[...] = nm
    nv_ref[...] = nv


def _adamw_small(ws, gs, ms, vs):
    k = len(ws)

    def body(*refs):
        ins, outs = refs[:4 * k], refs[4 * k:]
        for a in range(k):
            _adamw_update(ins[a], ins[k + a], ins[2 * k + a], ins[3 * k + a], outs[a], outs[k + a], outs[2 * k + a])

    out = pl.pallas_call(
        body, name="adamw_small",
        out_shape=[jax.ShapeDtypeStruct(w.shape, F32) for w in ws] * 3,
        compiler_params=pltpu.CompilerParams(vmem_limit_bytes=VMEM_LIMIT),
    )(*ws, *gs, *ms, *vs)
    return out[:k], out[k:2 * k], out[2 * k:]


def _adamw(w, g, m, v, name):
    shape = w.shape
    w2, g2, m2, v2 = (a.reshape((-1, shape[-1])) for a in (w, g, m, v))

    def body(w_ref, g_ref, m_ref, v_ref, d_ref, nm_ref, nv_ref):
        _adamw_update(w_ref, g_ref, m_ref, v_ref, d_ref, nm_ref, nv_ref)

    rows, cols = w2.shape
    nblk = cols // 256 if cols % 256 == 0 and rows >= 64 else 1
    blk = pl.BlockSpec((rows, cols // nblk), lambda j: (0, j))
    out = pl.pallas_call(
        body, name=name, grid=(nblk,), in_specs=[blk] * 4, out_specs=[blk] * 3,
        out_shape=[jax.ShapeDtypeStruct(w2.shape, F32)] * 3,
        compiler_params=_cparams("parallel"),
    )(w2, g2, m2, v2)
    return tuple(a.reshape(shape) for a in out)


def kernel(x, meta_tokens, norm_g, w_in, q_norm_g, w_q_up, kv_norm_g, w_kv_up, conv_w, attn_out_g, conv_out_g, w_out, final_norm_g, loss_target, m_meta_tokens, m_norm_g, m_w_in, m_q_norm_g, m_w_q_up, m_kv_norm_g, m_w_kv_up, m_conv_w, m_attn_out_g, m_conv_out_g, m_w_out, m_final_norm_g, v_meta_tokens, v_norm_g, v_w_in, v_q_norm_g, v_w_q_up, v_kv_norm_g, v_w_kv_up, v_conv_w, v_attn_out_g, v_conv_out_g, v_w_out, v_final_norm_g):
    nb, s, _ = x.shape
    tm = min(ROW_TILE, s)
    ta = min(ATTN_TILE, s)
    assert s % tm == 0 and s % ta == 0 and tm % 16 == 0
    r = nb * s

    tr = lambda a: jnp.transpose(a[0])
    w_head, w_in_shard, wq_p, wkv_p, g_cw, g_meta = _gather_weights(
        tr(w_in), [tr(w_q_up), tr(w_kv_up)],
        [W_Q_PIECES, W_KV_PIECES], [HEADS * QK_PAD, 1024],
        [jnp.transpose(conv_w, (1, 0, 2)), meta_tokens],
        [(0, jnp.zeros((64, Q_RANK), BF16), QK_PAD * h + NOPE + ROPE) for h in range(HEADS)])
    conv_f = jnp.transpose(g_cw[:, :, 0, :], (1, 0, 2)).reshape(3, CONV_W)
    meta_f = jnp.transpose(g_meta, (1, 0, 2)).reshape(N_META, D_MODEL)

    c_all, sa_all, sb_all = _rope_tables(N_META + s)
    tabs_m = (c_all[:N_META], sa_all[:N_META], sb_all[:N_META])
    tabs = (c_all[N_META:], sa_all[N_META:], sb_all[N_META:])
    gid = np.arange(CONV_W) // CONV_GROUP
    gmat = jnp.asarray(np.where(gid[:, None] == gid[None, :], 1.0 / CONV_GROUP, 0.0), BF16)
    ga, gc = attn_out_g, conv_out_g
    gf = final_norm_g.reshape(1, D_MODEL)

    x2d = x.reshape(r, D_MODEL)
    tgt2d = loss_target.reshape(r, D_MODEL)

    ph, q, k, v, pmh, km, vm, w_out_f, w_in_part = _fwd_proj(
        x2d, meta_f, tabs, tabs_m, norm_g, w_head, q_norm_g, wq_p, kv_norm_g, wkv_p, w_out[0].astype(BF16),
        w_in_shard, nb, s, tm)
    o, lse, w_in_p = _attn_fwd(q, k, v, km, vm, w_in_shard, w_in_part, nb, s, ta)
    dh2, dycat, dw_out, dgf, loss_acc, pt, pmt = _out_fwd_bwd(x2d, tgt2d, o, meta_f, norm_g, w_in_p, conv_f, ga, gc,
                                                              gmat, w_out_f, gf, nb, s, tm)
    dpb, do, delta, dccm, dga, dgc, dcw = _gate_bwd(dycat, o, pt, pmt, conv_f, ga, gc, gmat, nb, s, tm)
    p_out = dw_out.reshape(4, D_MODEL // 4, D_MODEL)
    dq, dk, dv, dkm, dvm, g_w_out = _attn_bwd(q, k, v, do, lse, delta, km, vm, [p_out], nb, s, ta)
    dpa, dpam, p_q, p_kv, dgq, dgkv = _up_bwd(dq, dk, dv, dkm, dvm, ph, pmh, tabs, tabs_m, wq_p, wkv_p,
                                              q_norm_g, kv_norm_g, nb, s, tm)
    gx, gmeta, p_in, dng = _in_bwd(x2d, dh2, dpa, dpb, meta_f, dpam, dccm, pmt, w_in_p, norm_g, nb, s, tm)

    flat =jnp.concatenate([dng.reshape(-1), dgq.reshape(-1), dgkv.reshape(-1), dga.reshape(-1), dgc.reshape(-1),
                            dgf.reshape(-1), dcw[:3].reshape(-1), gmeta.reshape(-1), loss_acc[0, 0:1]])
    n_small = flat.shape[0]
    rows_small = -(-n_small // 1024) * 8
    small = jnp.pad(flat, (0, rows_small * 128 - n_small)).reshape(rows_small, 128)
    g_w_in_t, g_w_q_t, g_w_kv_t, small_sum = _reduce_grads([p_in, p_q, p_kv], small)
    ssum = small_sum.reshape(-1)

    def take(off, n):
        return ssum[off:off + n], off + n

    off = 0
    g_norm, off = take(off, D_MODEL)
    g_qn, off = take(off, Q_RANK)
    g_kvn, off = take(off, KV_RANK)
    g_ga, off = take(off, CONV_W)
    g_gc, off = take(off, CONV_W)
    g_gf, off = take(off, D_MODEL)
    g_cw_all, off = take(off, 3 * CONV_W)
    g_meta_all, off = take(off, N_META * D_MODEL)
    loss = ssum[off]
    chip = 2 * lax.axis_index("x") + lax.axis_index("y")
    g_conv = lax.dynamic_slice(g_cw_all.reshape(3, CONV_W), (0, chip * 128), (3, 128))
    g_mt = lax.dynamic_slice(g_meta_all.reshape(N_META, D_MODEL), (0, chip * 256), (N_META, 256))

    grads = {
        "meta_tokens": g_mt, "norm_g": g_norm.reshape(1, -1), "w_in": g_w_in_t, "q_norm_g": g_qn.reshape(1, -1),
        "w_q_up": g_w_q_t, "kv_norm_g": g_kvn.reshape(1, -1), "w_kv_up": jnp.transpose(g_w_kv_t)[None],
        "conv_w": g_conv[None], "attn_out_g": g_ga.reshape(1, -1), "conv_out_g": g_gc.reshape(1, -1),
        "w_out": g_w_out[None], "final_norm_g": g_gf,
    }
    transposed = ("w_in", "w_q_up")
    weights = {
        "meta_tokens": (meta_tokens, m_meta_tokens, v_meta_tokens), "norm_g": (norm_g, m_norm_g, v_norm_g),
        "w_in": (w_in, m_w_in, v_w_in), "q_norm_g": (q_norm_g, m_q_norm_g, v_q_norm_g),
        "w_q_up": (w_q_up, m_w_q_up, v_w_q_up), "kv_norm_g": (kv_norm_g, m_kv_norm_g, v_kv_norm_g),
        "w_kv_up": (w_kv_up, m_w_kv_up, v_w_kv_up), "conv_w": (conv_w, m_conv_w, v_conv_w),
        "attn_out_g": (attn_out_g, m_attn_out_g, v_attn_out_g), "conv_out_g": (conv_out_g, m_conv_out_g, v_conv_out_g),
        "w_out": (w_out, m_w_out, v_w_out), "final_norm_g": (final_norm_g, m_final_norm_g, v_final_norm_g),
    }
    names = list(weights)
    small = [nme for nme in names if nme != "w_in"]

    def view(nme, a):
        if nme in transposed:
            return a if a.ndim == 2 else tr(a)
        if nme == "conv_w":
            return jnp.transpose(a.reshape(1, 3, -1), (1, 0, 2))
        if a.ndim == 3:
            return a[0]
        return a.reshape(1, -1) if a.ndim == 1 else a

    def unview(nme, a):
        if nme in transposed:
            return jnp.transpose(a)[None]
        if nme == "conv_w":
            return jnp.transpose(a, (1, 0, 2))
        return a.reshape(weights[nme][0].shape)

    res_small = _adamw_small(*[[view(nme, a) for nme, a in zip(small, col)] for col in (
        [weights[nme][0] for nme in small], [grads[nme] for nme in small],
        [weights[nme][1] for nme in small], [weights[nme][2] for nme in small])])
    w_, m_, v_ = weights["w_in"]
    res = _adamw(tr(w_), grads["w_in"], tr(m_), tr(v_), "adamw_w_in")
    upd = {"w_in": tuple(jnp.transpose(a)[None] for a in (grads["w_in"],) + res)}
    for j, nme in enumerate(small):
        upd[nme] = (unview(nme, view(nme, grads[nme])),) + tuple(unview(nme, r[j]) for r in res_small)
    grads = {nme: upd[nme][0] for nme in names}
    deltas, new_m, new_v = ([upd[nme][j] for nme in names] for j in (1, 2, 3))

    grad_x = gx.reshape(nb, s, D_MODEL)
    return (loss, grad_x, *[grads[nme] for nme in names], *deltas, *new_m, *new_v)
```

```python
import functools

import jax
import jax.numpy as jnp
import numpy as np
from jax import lax
from jax.experimental import pallas as pl
from jax.experimental.pallas import tpu as pltpu

F32 = jnp.float32
BF16 = jnp.bfloat16

D_MODEL = 1024
N_META = 16
HEADS = 4
NOPE = 128
ROPE = 64
VDIM = 128
QK_PAD = 256
Q_RANK = 256
KV_RANK = 128
CONV_W = 512
CONV_GROUP = 64
ROPE_THETA = 10000.0
EPS = 1e-6
ATTN_SCALE = (NOPE + ROPE) ** -0.5
IN_DIM = 3008
IN_PAD = 3072
HEAD_ROWS = Q_RANK + KV_RANK + ROPE
IN_HEAD = 512
IN_TAIL = IN_PAD - IN_HEAD
BLK_ZA, BLK_CB, BLK_CC, BLK_CH, BLK_ZC = 0, 1, 2, 3, 4
NEG_INF = -1e30

ADAM_LR = 0.001
ADAM_B1 = 0.9
ADAM_B2 = 0.999
ADAM_EPS = 1e-08
ADAM_WD = 0.01
ADAM_STEP = 10

ROW_TILE = 512
ATTN_TILE = 256
VMEM_LIMIT = 56 * 1024 * 1024

NT = (((1,), (1,)), ((), ()))
TN = (((0,), (0,)), ((), ()))


def _cparams(*sem):
    return pltpu.CompilerParams(dimension_semantics=sem, vmem_limit_bytes=VMEM_LIMIT)


def _dot(a, b):
    return jnp.dot(a, b, preferred_element_type=F32)


def _dot_nt(a, b):
    return lax.dot_general(a, b, NT, preferred_element_type=F32)


def _dot_tn(a, b):
    return lax.dot_general(a, b, TN, preferred_element_type=F32)


def _rms(x, g):
    r = lax.rsqrt(jnp.mean(x * x, axis=-1, keepdims=True) + EPS)
    return x * r * g, r


def _rms_bwd(dy, x, r, g):
    xh = x * r
    dyg = dy * g
    dx = r * (dyg - xh * jnp.mean(dyg * xh, axis=-1, keepdims=True))
    return dx, dy * xh


def _sigmoid(z):
    return 1.0 / (1.0 + jnp.exp(-z))


def _rope(b, c, sa, sb):
    return b * c + pltpu.roll(b, 96, 1) * sa + pltpu.roll(b, 32, 1) * sb


def _rope_bwd(d, c, sa, sb):
    return d * c + pltpu.roll(d * sa, 32, 1) + pltpu.roll(d * sb, 96, 1)


def _group_mean(x, gmat):
    hi = x.astype(BF16)
    lo = (x - hi.astype(F32)).astype(BF16)
    return _dot(hi, gmat) + _dot(lo, gmat)


def _row_of(col, rows):
    return jnp.transpose(jnp.broadcast_to(col, (rows, 128)))[0:1, :]


def _rope_tables(n_pos):
    half = ROPE // 2
    inv_freq = (np.float32(1.0) / (np.float32(ROPE_THETA) ** (np.arange(half, dtype=np.float32) / np.float32(half))))
    ang = np.arange(n_pos, dtype=np.float32)[:, None] * inv_freq.astype(np.float32)[None, :]
    cos, sin = np.cos(ang).astype(np.float32), np.sin(ang).astype(np.float32)
    z = np.zeros((n_pos, half), np.float32)
    c = np.concatenate([cos, cos, z, z], axis=1)
    sa = np.concatenate([-sin, z, z, z], axis=1)
    sb = np.concatenate([z, sin, z, z], axis=1)
    return jnp.asarray(c), jnp.asarray(sa), jnp.asarray(sb)


W_IN_PIECES_1 = ((0, 80, 752, 0, 64, 0), (384, 64, 752, 384, 448, 1), (448, 16, 752, 512, 512, 1))
W_IN_PIECES_2 = ((80, 304, 752, 80, 144, 0), (464, 288, 752, 528, 528, 1))
W_Q_PIECES = ((0, 96, 256, 0, 0, 0), (96, 96, 256, 96, 96, 1))
W_KV_PIECES = ((0, 128, 128, 0, 0, 0), (128, 128, 128, 512, 512, 1))
W_OUT_PIECES = ((0, 128, 256, 0, 0, 0), (128, 128, 256, 128, 128, 1))


class _StagedGather:
    STAGES = 4

    @staticmethod
    def steps(n_steps):
        return (0, 5 * n_steps // 8, 7 * n_steps // 8, n_steps - 1)

    def __init__(self, pieces, zero_rows=None):
        self.pieces = pieces
        self.zero_rows = zero_rows

    def scratch(self):
        nk, dma = len(self.pieces), pltpu.SemaphoreType.DMA
        return [dma((nk, 3)), dma((nk, 3)), dma((nk, 3)), dma((nk, 3)), dma((nk,))]

    def vmem_scratch(self, shard_shape, out_shape):
        return [pltpu.VMEM(shard_shape, BF16), pltpu.VMEM(out_shape, BF16),
                pltpu.SemaphoreType.DMA((4 * len(self.pieces) + 2,))] + self.scratch()

    def run_vmem(self, stage, shard_ref, out_ref, scr):
        src_scr, land_scr, io_sems = scr[:3]
        spans = []
        for _, nr, per, first, rest, _ in self.pieces:
            spans += [(per * q + (first if q == 0 else rest), nr) for q in range(4)]
        if self.zero_rows is not None:
            spans.append(self.zero_rows)
        flush = [pltpu.make_async_copy(land_scr.at[r0:r0 + nr], out_ref.at[r0:r0 + nr], io_sems.at[n])
                 for n, (r0, nr) in enumerate(spans)]
        if stage == 0:
            load = pltpu.make_async_copy(shard_ref, src_scr, io_sems.at[len(spans)])
            load.start()
            if self.zero_rows is not None:
                r0, nr = self.zero_rows
                land_scr[r0:r0 + nr, :] = jnp.zeros((nr, land_scr.shape[1]), BF16)
            load.wait()
        if stage < self.STAGES:
            self.run(stage, src_scr, land_scr, scr[3:])
        for cp in flush:
            if stage == self.STAGES - 1:
                cp.start()
            if stage == self.STAGES:
                cp.wait()

    def run(self, stage, src_ref, out_ref, scr):
        send_sems, recv_sems, fwd_send, fwd_recv, loc_sems = scr
        pieces = self.pieces
        nk = len(pieces)
        x, y, c = lax.axis_index("x"), lax.axis_index("y"), lax.axis_index("c")
        mine = 2 * x + y
        chips = [(1 - x, y), (x, 1 - y), (1 - x, 1 - y)]
        chip_of = [2 * px + py for px, py in chips]
        mesh = pl.DeviceIdType.MESH

        def src(k):
            s0, nr = pieces[k][0], pieces[k][1]
            return src_ref.at[s0:s0 + nr]

        def dst(k, q):
            _, nr, per, first, rest, _ = pieces[k]
            row = per * q + first + (rest - first) * jnp.minimum(q, 1)
            return out_ref.at[pl.ds(pl.multiple_of(row, 16), nr)]

        def ici(k, j, q):
            px, py = chips[j]
            return pltpu.make_async_remote_copy(
                src_ref=src(k), dst_ref=dst(k, q), send_sem=send_sems.at[k, j], recv_sem=recv_sems.at[k, j],
                device_id=(px, py, c), device_id_type=mesh)

        def fwd(k, j):
            ref = dst(k, chip_of[j])
            return pltpu.make_async_remote_copy(
                src_ref=ref, dst_ref=ref, send_sem=fwd_send.at[k, j], recv_sem=fwd_recv.at[k, j],
                device_id=(x, y, 1 - c), device_id_type=mesh)

        def relay(k, half):
            ref = dst(k, chip_of[half])
            px, py = chips[1 - half]
            return pltpu.make_async_remote_copy(
                src_ref=ref, dst_ref=ref, send_sem=send_sems.at[k, 2], recv_sem=recv_sems.at[k, 2],
                device_id=(px, py, c), device_id_type=mesh)

        local = [pltpu.make_async_copy(src(k), dst(k, mine), loc_sems.at[k]) for k in range(nk)]
        if stage == 0:
            for cp in local:
                cp.start()
        if stage == 3:
            for cp in local:
                cp.wait()
        for half in (0, 1):
            @pl.when(c == half)
            def _(half=half):
                my_k = [k for k in range(nk) if pieces[k][5] == half]
                other_k = [k for k in range(nk) if pieces[k][5] != half]
                for k in my_k:
                    if stage == 0:
                        for j in range(2):
                            ici(k, j, mine).start()
                    elif stage == 1:
                        for j in (half, 1 - half):
                            ici(k, j, chip_of[j]).wait_recv()
                            if j == half:
                                relay(k, half).start()
                            fwd(k, j).start()
                    elif stage == 2:
                        ici(k, 2, chip_of[2]).wait_recv()
                        fwd(k, 2).start()
                    else:
                        for j in range(2):
                            ici(k, j, mine).wait_send()
                        relay(k, half).wait_send()
                        for j in range(3):
                            fwd(k, j).wait_send()
                if stage == 3:
                    for k in other_k:
                        for j in range(3):
                            fwd(k, j).wait_recv()


def _fwd_proj(x2d, meta, tabs, tabs_m, norm_g, w_head, q_norm_g, wq_p, kv_norm_g, wkv_p, w_out_shard, w_in_shard,
              nb, s, tm):
    nt = s // tm
    n = nb * nt
    n_steps = n + 1
    c_t, sa_t, sb_t = tabs
    cm_t, sam_t, sbm_t = tabs_m
    gat = _StagedGather(W_OUT_PIECES)
    gat_in = _StagedGather(W_IN_PIECES_1)
    n_sems = len(gat.scratch())
    assert n_steps >= 3

    def body(x_ref, c_ref, sa_ref, sb_ref, mt_ref, cm_ref, sam_ref, sbm_ref,
             g_ref, w_ref, gq_ref, wq_ref, gkv_ref, wkv_ref, wos_ref, wis_ref,
             p_ref, q_ref, k_ref, v_ref, pm_ref, km_ref, vm_ref, wo_ref, wi_ref, *scr):
        gat_scr, gat_in_scr = scr[:n_sems], scr[n_sems:]
        i = pl.program_id(0)
        for stage, at in enumerate(_StagedGather.steps(n_steps)):
            @pl.when(i == at)
            def _(stage=stage):
                gat_in.run_vmem(stage, wis_ref, wi_ref, gat_in_scr)
                gat.run(stage, wos_ref, wo_ref, gat_scr)

        def project(xv, c, sa, sb, p_out, q_out, k_out, v_out):
            u, _ = _rms(xv, g_ref[...])
            p = _dot_nt(u.astype(BF16), w_ref[...])
            p_out[...] = p
            qn, _ = _rms(p[:, 0:Q_RANK], gq_ref[...])
            q = _dot_nt(qn.astype(BF16), wq_ref[...])
            kvn, _ = _rms(p[:, Q_RANK:Q_RANK + KV_RANK], gkv_ref[...])
            kv = _dot_nt(kvn.astype(BF16), wkv_ref[...])
            kpe = _rope(p[:, 384:512], c, sa, sb)
            for h in range(HEADS):
                if q_out is not None:
                    pe = _rope(q[:, QK_PAD * h + NOPE:QK_PAD * (h + 1)], c, sa, sb)
                    qh = jnp.concatenate([q[:, QK_PAD * h:QK_PAD * h + NOPE], pe], axis=1)
                    q_out[0, h] = (qh * ATTN_SCALE).astype(BF16)
                k_out[0, h] = jnp.concatenate([kv[:, NOPE * h:NOPE * (h + 1)], kpe], axis=1).astype(BF16)
                v_out[0, h] = kv[:, 512 + VDIM * h:512 + VDIM * (h + 1)].astype(BF16)

        @pl.when(i < n)
        def _():
            project(x_ref[...], c_ref[...], sa_ref[...], sb_ref[...], p_ref, q_ref, k_ref, v_ref)

        @pl.when(i == n)
        def _():
            project(mt_ref[...], cm_ref[...], sam_ref[...], sbm_ref[...], pm_ref, None, km_ref, vm_ref)
            gat_in.run_vmem(gat_in.STAGES, wis_ref, wi_ref, gat_in_scr)

    cl = lambda i: jnp.minimum(i, n - 1)
    full = lambda a: pl.BlockSpec(a.shape, lambda i: (0,) * a.ndim)
    const = lambda shape: pl.BlockSpec(shape, lambda i: (0,) * len(shape))
    tab = pl.BlockSpec((tm, 128), lambda i: (cl(i) % nt, 0))
    hb = lambda w: pl.BlockSpec((1, HEADS, tm, w), lambda i: (cl(i) // nt, 0, cl(i) % nt, 0))
    whole = pl.BlockSpec(memory_space=pl.ANY)
    return pl.pallas_call(
        body, name="fwd_proj", grid=(n_steps,),
        in_specs=[pl.BlockSpec((tm, D_MODEL), lambda i: (cl(i), 0)), tab, tab, tab,
                  full(meta), full(cm_t), full(sam_t), full(sbm_t),
                  full(norm_g), full(w_head), full(q_norm_g), full(wq_p), full(kv_norm_g), full(wkv_p), whole, whole],
        out_specs=[pl.BlockSpec((tm, IN_HEAD), lambda i: (cl(i), 0)), hb(QK_PAD), hb(QK_PAD), hb(VDIM),
                   const((N_META, IN_HEAD)), const((1, HEADS, N_META, QK_PAD)), const((1, HEADS, N_META, VDIM)),
                   whole, whole],
        out_shape=[jax.ShapeDtypeStruct((nb * s, IN_HEAD), F32),
                   jax.ShapeDtypeStruct((nb, HEADS, s, QK_PAD), BF16),
                   jax.ShapeDtypeStruct((nb, HEADS, s, QK_PAD), BF16),
                   jax.ShapeDtypeStruct((nb, HEADS, s, VDIM), BF16),
                   jax.ShapeDtypeStruct((N_META, IN_HEAD), F32),
                   jax.ShapeDtypeStruct((1, HEADS, N_META, QK_PAD), BF16),
                   jax.ShapeDtypeStruct((1, HEADS, N_META, VDIM), BF16),
                   jax.ShapeDtypeStruct((D_MODEL, D_MODEL), BF16),
                   jax.ShapeDtypeStruct((IN_PAD, D_MODEL), BF16)],
        scratch_shapes=gat.scratch() + gat_in.vmem_scratch(w_in_shard.shape, (IN_PAD, D_MODEL)),
        compiler_params=_cparams("arbitrary"),
    )(x2d, c_t, sa_t, sb_t, meta, cm_t, sam_t, sbm_t, norm_g, w_head, q_norm_g, wq_p, kv_norm_g, wkv_p, w_out_shard,
      w_in_shard)


def _attn_fwd(q, k, v, km, vm, w_in_shard, w_in_part, nb, s, tq):
    nq = s // tq
    n_steps = nb * HEADS
    gat = _StagedGather(W_IN_PIECES_2, zero_rows=(HEAD_ROWS, IN_HEAD - HEAD_ROWS))
    assert n_steps >= 3

    def body(q_ref, k_ref, v_ref, km_ref, vm_ref, ws_ref, _, o_ref, lse_ref, w_ref, s_scr, p_scr, *gat_scr):
        step = pl.program_id(0) * HEADS + pl.program_id(1)
        for stage, at in enumerate(_StagedGather.steps(n_steps)):
            @pl.when(step == at)
            def _(stage=stage):
                gat.run_vmem(stage, ws_ref, w_ref, gat_scr)

        row = lax.broadcasted_iota(jnp.int32, (tq, tq), 0)
        col = lax.broadcasted_iota(jnp.int32, (tq, tq), 1)
        def scores(i):
            slot = i % 2
            qi = q_ref[0, 0, i * tq:(i + 1) * tq, :]
            sm = _dot_nt(qi, km_ref[0, 0])
            m128 = None
            for j in range(i + 1):
                sc = _dot_nt(qi, k_ref[0, 0, j * tq:(j + 1) * tq, :])
                if j == i:
                    sc = jnp.where(col <= row, sc, NEG_INF)
                s_scr[slot, :, j * tq:(j + 1) * tq] = sc
                mx = sc[:, 0:128]
                for c0 in range(128, tq, 128):
                    mx = jnp.maximum(mx, sc[:, c0:c0 + 128])
                m128 = mx if m128 is None else jnp.maximum(m128, mx)
            return sm, jnp.maximum(jnp.max(m128, axis=1, keepdims=True), jnp.max(sm, axis=1, keepdims=True))

        def weighted_sum(i, pm, l):
            n = (i + 1) * tq
            acc = _dot(p_scr[i % 2, :, 0:n], v_ref[0, 0, 0:n, :]) + _dot(pm.astype(BF16), vm_ref[0, 0])
            o_ref[0, 0, i * tq:(i + 1) * tq, :] = acc / l

        nxt, pending = scores(0), None
        for i in range(nq):
            slot = i % 2
            sm, m = nxt
            if i + 1 < nq:
                nxt = scores(i + 1)
            pm = jnp.exp(sm - m)
            l128 = None
            for j in range(i + 1):
                p = jnp.exp(s_scr[slot, :, j * tq:(j + 1) * tq] - m)
                p_scr[slot, :, j * tq:(j + 1) * tq] = p.astype(BF16)
                ps = p[:, 0:128]
                for c0 in range(128, tq, 128):
                    ps = ps + p[:, c0:c0 + 128]
                l128 = ps if l128 is None else l128 + ps
            l = jnp.sum(l128, axis=1, keepdims=True) + jnp.sum(pm, axis=1, keepdims=True)
            lse_ref[0, 0, :, i * tq:(i + 1) * tq] = _row_of(m + jnp.log(l), tq)
            if pending is not None:
                weighted_sum(*pending)
            pending = (i, pm, l)
        weighted_sum(*pending)

        @pl.when(step == n_steps - 1)
        def _():
            gat.run_vmem(gat.STAGES, ws_ref, w_ref, gat_scr)

    hblk = lambda w: pl.BlockSpec((1, 1, s, w), lambda b, h: (b, h, 0, 0))
    mblk = lambda w: pl.BlockSpec((1, 1, N_META, w), lambda b, h: (0, h, 0, 0))
    whole = pl.BlockSpec(memory_space=pl.ANY)
    return pl.pallas_call(
        body, name="attn_fwd", grid=(nb, HEADS),
        in_specs=[hblk(QK_PAD), hblk(QK_PAD), hblk(VDIM), mblk(QK_PAD), mblk(VDIM), whole, whole],
        out_specs=[hblk(VDIM), pl.BlockSpec((1, 1, 1, s), lambda b, h: (b, h, 0, 0)), whole],
        out_shape=[jax.ShapeDtypeStruct((nb, HEADS, s, VDIM), F32),
                   jax.ShapeDtypeStruct((nb, HEADS, 1, s), F32),
                   jax.ShapeDtypeStruct(w_in_part.shape, BF16)],
        input_output_aliases={6: 2},
        scratch_shapes=[pltpu.VMEM((2, tq, s), F32), pltpu.VMEM((2, tq, s), BF16)]
        + gat.vmem_scratch(w_in_shard.shape, w_in_part.shape),
        compiler_params=_cparams("arbitrary", "arbitrary"),
    )(q, k, v, km, vm, w_in_shard, w_in_part)


def _shift_rows(a, prev, n_rows):
    rid = lax.broadcasted_iota(jnp.int32, a.shape, 0)
    a1 = jnp.where(rid == 0, prev[7:8, :], pltpu.roll(a, 1, 0))
    a2 = jnp.where(rid == 0, prev[6:7, :], jnp.where(rid == 1, prev[7:8, :], pltpu.roll(a, 2, 0)))
    return a1, a2


def _attn_gate(o, za, ga_h):
    on, r = _rms(o, ga_h)
    return on * (za * _sigmoid(za)), on, r


def _out_fwd_bwd(x2d, tgt2d, o, meta, norm_g, w_in_p, conv_w, ga, gc, gmat, w_out, gf, nb, s, tm):
    nt = s // tm
    r = nb * s

    def body(x_ref, t_ref, o_ref, mt_ref, g_ref, wi_ref, cw_ref, ga_ref, gc_ref, gm_ref, w_ref, gf_ref,
             dh_ref, dy_ref, dw_ref, dgf_ref, loss_ref, p_ref, pm_ref, last_cc):
        i = pl.program_id(0)
        blk = lambda ref, j, rows=slice(None): ref[rows, 512 * j:512 * (j + 1)]

        def tail(xv):
            u, _ = _rms(xv, g_ref[...])
            return _dot_nt(u.astype(BF16), wi_ref[IN_HEAD:IN_PAD, :])

        @pl.when(i == 0)
        def _():
            dw_ref[...] = jnp.zeros_like(dw_ref)
            dgf_ref[...] = jnp.zeros_like(dgf_ref)
            loss_ref[...] = jnp.zeros_like(loss_ref)
            last_cc[...] = jnp.zeros_like(last_cc)
            pm_ref[...] = tail(mt_ref[...])

        u16 = _rms(x_ref[...], g_ref[...])[0].astype(BF16)

        def project(j):
            p_ref[:, 512 * j:512 * (j + 1)] = _dot_nt(u16, wi_ref[IN_HEAD + 512 * j:IN_HEAD + 512 * (j + 1), :])

        project(BLK_ZA)
        project(BLK_CC)
        project(BLK_CH)
        ya = []
        for h in range(HEADS):
            y, _, _ = _attn_gate(o_ref[0, h], p_ref[:, 512 * BLK_ZA + VDIM * h:512 * BLK_ZA + VDIM * (h + 1)],
                                 ga_ref[:, VDIM * h:VDIM * (h + 1)])
            ya.append(y)
        project(BLK_CB)
        project(BLK_ZC)
        cc = blk(p_ref, BLK_CC) * blk(p_ref, BLK_CH)
        meta_cc = blk(pm_ref, BLK_CC, slice(8, 16)) * blk(pm_ref, BLK_CH, slice(8, 16))
        prev = jnp.where(i % nt == 0, meta_cc, last_cc[...])
        last_cc[...] = cc[tm - 8:tm, :]
        cc1, cc2 = _shift_rows(cc, prev, tm)
        yc = blk(p_ref, BLK_CB) * (cw_ref[0:1, :] * cc2 + cw_ref[1:2, :] * cc1 + cw_ref[2:3, :] * cc)
        rg = lax.rsqrt(_group_mean(yc * yc, gm_ref[...]) + EPS)
        zc = blk(p_ref, BLK_ZC)
        yconv = yc * rg * gc_ref[...] * (zc * _sigmoid(zc))
        ycat = jnp.concatenate(ya + [yconv], axis=1).astype(BF16)
        h2 = x_ref[...] + _dot(ycat, w_ref[...])
        gfv = gf_ref[...]
        y, r2 = _rms(h2, gfv)
        e = y - t_ref[...]
        loss_ref[...] += 0.5 * jnp.sum(e * e) / D_MODEL
        dyv = e * (1.0 / D_MODEL)
        dh2, dgf = _rms_bwd(dyv, h2, r2, gfv)
        dgf_ref[...] += jnp.sum(dgf, axis=0, keepdims=True)
        dh_ref[...] = dh2
        dhb = dh2.astype(BF16)
        dy_ref[...] = _dot_nt(dhb, w_ref[...])
        dw_ref[...] += _dot_tn(ycat, dhb)

    row = lambda w: pl.BlockSpec((tm, w), lambda i: (i, 0))
    const = lambda shape: pl.BlockSpec(shape, lambda i: (0,) * len(shape))
    full = lambda a: const(a.shape)
    return pl.pallas_call(
        body, name="out_fwd_bwd", grid=(nb * nt,),
        in_specs=[row(D_MODEL), row(D_MODEL),
                  pl.BlockSpec((1, HEADS, tm, VDIM), lambda i: (i // nt, 0, i % nt, 0)),
                  full(meta), full(norm_g), full(w_in_p),
                  full(conv_w), full(ga), full(gc), full(gmat), full(w_out), full(gf)],
        out_specs=[row(D_MODEL), row(D_MODEL), const((D_MODEL, D_MODEL)), const((1, D_MODEL)), const((1, 128)),
                   row(IN_TAIL), const((N_META, IN_TAIL))],
        out_shape=[jax.ShapeDtypeStruct((r, D_MODEL), F32), jax.ShapeDtypeStruct((r, D_MODEL), F32),
                   jax.ShapeDtypeStruct((D_MODEL, D_MODEL), F32), jax.ShapeDtypeStruct((1, D_MODEL), F32),
                   jax.ShapeDtypeStruct((1, 128), F32),
                   jax.ShapeDtypeStruct((r, IN_TAIL), F32), jax.ShapeDtypeStruct((N_META, IN_TAIL), F32)],
        scratch_shapes=[pltpu.VMEM((8, 512), F32)],
        compiler_params=_cparams("arbitrary"),
    )(x2d, tgt2d, o, meta, norm_g, w_in_p, conv_w, ga, gc, gmat, w_out, gf)


def _gate_bwd(dycat, o, p, pm, conv_w, ga, gc, gmat, nb, s, tm):
    nt = s // tm
    r = nb * s
    ext = tm + 8
    prev_idx = lambda i: jnp.maximum(i * (tm // 8) - 1, 0)
    next_idx = lambda i: jnp.minimum((i + 1) * (tm // 8), r // 8 - 1)

    def body(dya_ref, dyc_ref, dycn_ref, o_ref, za_ref, cb_ref, cbn_ref, cc_ref, ccp_ref, ccn_ref,
             ch_ref, chp_ref, chn_ref, zc_ref, zcn_ref, mc_ref, mh_ref, cw_ref, ga_ref, gc_ref, gm_ref,
             dpb_ref, do_ref, dl_ref, dccm_ref, dga_ref, dgc_ref, dcw_ref):
        i = pl.program_id(0)

        @pl.when(i == 0)
        def _():
            dga_ref[...] = jnp.zeros_like(dga_ref)
            dgc_ref[...] = jnp.zeros_like(dgc_ref)
            dcw_ref[...] = jnp.zeros_like(dcw_ref)

        dga = []
        for h in range(HEADS):
            hs = slice(VDIM * h, VDIM * (h + 1))
            oh, za, gah, dya = o_ref[0, h], za_ref[:, hs], ga_ref[:, hs], dya_ref[:, hs]
            sg = _sigmoid(za)
            on, ro = _rms(oh, gah)
            don = dya * (za * sg)
            dpb_ref[:, hs] = (dya * on * (sg * (1.0 + za * (1.0 - sg)))).astype(BF16)
            do, dg = _rms_bwd(don, oh, ro, gah)
            dga.append(jnp.sum(dg, axis=0, keepdims=True))
            dob = do.astype(BF16)
            do_ref[0, h] = dob
            dl_ref[0, h] = _row_of(jnp.sum(dob.astype(F32) * oh, axis=1, keepdims=True), tm)
        dga_ref[...] += jnp.concatenate(dga, axis=1)

        cat = lambda a, b: jnp.concatenate([a[...], b[...]], axis=0)
        cch = cat(cc_ref, ccn_ref)
        chh = cat(ch_ref, chn_ref)
        cb = cat(cb_ref, cbn_ref)
        zc = cat(zc_ref, zcn_ref)
        dy = cat(dyc_ref, dycn_ref)
        first = i % nt == 0
        last = i % nt == nt - 1
        cc = cch * chh
        prev = jnp.where(first, mc_ref[8:16, :] * mh_ref[8:16, :], ccp_ref[...] * chp_ref[...])
        cc1, cc2 = _shift_rows(cc, prev, ext)
        w0, w1, w2 = cw_ref[0:1, :], cw_ref[1:2, :], cw_ref[2:3, :]
        dw = w0 * cc2 + w1 * cc1 + w2 * cc
        yc = cb * dw
        rg = lax.rsqrt(_group_mean(yc * yc, gm_ref[...]) + EPS)
        ych = yc * rg
        gcv = gc_ref[...]
        sg = _sigmoid(zc)
        dycn = dy * (zc * sg)
        dzc = dy * (ych * gcv) * (sg * (1.0 + zc * (1.0 - sg)))
        dgc_ref[...] += jnp.sum((dycn * ych)[:tm], axis=0, keepdims=True)
        dycg = dycn * gcv
        dyc = rg * (dycg - ych * _group_mean(dycg * ych, gm_ref[...]))
        rid = lax.broadcasted_iota(jnp.int32, (ext, CONV_W), 0)
        ddw = jnp.where(jnp.logical_and(last, rid >= tm), 0.0, dyc * cb)
        dcb = dyc * dw
        dcc = w2 * ddw + w1 * pltpu.roll(ddw, ext - 1, 0) + w0 * pltpu.roll(ddw, ext - 2, 0)
        dpb_ref[:, 512:1024] = dcb[:tm].astype(BF16)
        dpb_ref[:, 1024:1536] = (dcc * chh)[:tm].astype(BF16)
        dpb_ref[:, 1536:2048] = (dcc * cch)[:tm].astype(BF16)
        dpb_ref[:, 2048:2560] = dzc[:tm].astype(BF16)
        rs = lambda a: jnp.sum(a[:tm], axis=0, keepdims=True)
        dcw_ref[0:1, :] += rs(ddw * cc2)
        dcw_ref[1:2, :] += rs(ddw * cc1)
        dcw_ref[2:3, :] += rs(ddw * cc)

        @pl.when(first)
        def _():
            d0, d1 = ddw[0:1, :], ddw[1:2, :]
            r8 = lax.broadcasted_iota(jnp.int32, (8, CONV_W), 0)
            dccm_ref[0] = jnp.where(r8 == 7, w1 * d0 + w0 * d1, jnp.where(r8 == 6, w0 * d0, 0.0))

    row = lambda j: pl.BlockSpec((tm, 512), lambda i: (i, j))
    prv = lambda j: pl.BlockSpec((8, 512), lambda i: (prev_idx(i), j))
    nxt = lambda j: pl.BlockSpec((8, 512), lambda i: (next_idx(i), j))
    mblk = lambda j: pl.BlockSpec((N_META, 512), lambda i: (0, j))
    full = lambda a: pl.BlockSpec(a.shape, lambda i: (0,) * a.ndim)
    hb = lambda w: pl.BlockSpec((1, HEADS, tm, w), lambda i: (i // nt, 0, i % nt, 0))
    acc = lambda rr: pl.BlockSpec((rr, 512), lambda i: (0, 0))
    return pl.pallas_call(
        body, name="gate_bwd", grid=(nb * nt,),
        in_specs=[row(0), row(1), nxt(1), hb(VDIM),
                  row(BLK_ZA), row(BLK_CB), nxt(BLK_CB), row(BLK_CC), prv(BLK_CC), nxt(BLK_CC),
                  row(BLK_CH), prv(BLK_CH), nxt(BLK_CH), row(BLK_ZC), nxt(BLK_ZC),
                  mblk(BLK_CC), mblk(BLK_CH), full(conv_w), full(ga), full(gc), full(gmat)],
        out_specs=[pl.BlockSpec((tm, 2560), lambda i: (i, 0)), hb(VDIM),
                   pl.BlockSpec((1, HEADS, 1, tm), lambda i: (i // nt, 0, 0, i % nt)),
                   pl.BlockSpec((1, 8, 512), lambda i: (i // nt, 0, 0)),
                   acc(1), acc(1), acc(8)],
        out_shape=[jax.ShapeDtypeStruct((r, 2560), BF16), jax.ShapeDtypeStruct((nb, HEADS, s, VDIM), BF16),
                   jax.ShapeDtypeStruct((nb, HEADS, 1, s), F32), jax.ShapeDtypeStruct((nb, 8, 512), F32),
                   jax.ShapeDtypeStruct((1, 512), F32), jax.ShapeDtypeStruct((1, 512), F32),
                   jax.ShapeDtypeStruct((8, 512), F32)],
        compiler_params=_cparams("arbitrary"),
    )(dycat, dycat, dycat, o, p, p, p, p, p, p, p, p, p, p, p, pm, pm, conv_w, ga, gc, gmat)


class _StagedReduce:
    LOC, PRE_S, PRE_R, ICI_S, ICI_R, POST_S, POST_R, OUT, N_SEM = 0, 1, 2, 3, 6, 9, 10, 11, 12

    def __init__(self, shard_shape):
        self.half = (shard_shape[0] // 2, shard_shape[1])

    def scratch(self):
        h = self.half
        return [pltpu.VMEM((4,) + h, F32), pltpu.VMEM((4,) + h, F32), pltpu.VMEM((4,) + h, BF16),
                pltpu.VMEM((3,) + h, BF16), pltpu.VMEM(h, F32), pltpu.SemaphoreType.DMA((self.N_SEM,))]

    def run(self, stage, pin, gout, scr):
        own, sib, wire, rbuf, fin, sems = scr
        r2 = self.half[0]
        x, y, c = lax.axis_index("x"), lax.axis_index("y"), lax.axis_index("c")
        mine = 2 * x + y
        sibling = (x, y, 1 - c)
        chips = [(1 - x, y), (x, 1 - y), (1 - x, 1 - y)]
        rows = lambda half: pl.ds(pl.multiple_of(half * r2, r2), r2)
        mesh = pl.DeviceIdType.MESH

        loc = pltpu.make_async_copy(pin.at[:, rows(c), :], own, sems.at[self.LOC])
        pre = pltpu.make_async_remote_copy(
            src_ref=pin.at[:, rows(1 - c), :], dst_ref=sib, send_sem=sems.at[self.PRE_S],
            recv_sem=sems.at[self.PRE_R], device_id=sibling, device_id_type=mesh)

        def ici(j):
            px, py = chips[j]
            return pltpu.make_async_remote_copy(
                src_ref=wire.at[2 * px + py], dst_ref=rbuf.at[j], send_sem=sems.at[self.ICI_S + j],
                recv_sem=sems.at[self.ICI_R + j], device_id=(px, py, c), device_id_type=mesh)

        def post(half):
            return pltpu.make_async_remote_copy(
                src_ref=fin, dst_ref=gout.at[rows(half), :], send_sem=sems.at[self.POST_S],
                recv_sem=sems.at[self.POST_R], device_id=sibling, device_id_type=mesh)

        keep = pltpu.make_async_copy(fin, gout.at[rows(c), :], sems.at[self.OUT])
        if stage == 0:
            loc.start()
            pre.start()
        elif stage == 1:
            loc.wait()
            pre.wait_recv()
            for blk in range(4):
                tot = own[blk] + sib[blk]
                own[blk] = tot
                wire[blk] = tot.astype(BF16)
            for j in range(3):
                ici(j).start()
        elif stage == 2:
            for j in range(3):
                ici(j).wait_recv()
            tot = own[mine]
            for j in range(3):
                tot = tot + rbuf[j].astype(F32)
            fin[...] = tot
            post(c).start()
            keep.start()
        else:
            post(1 - c).wait_recv()
            pre.wait_send()
            for j in range(3):
                ici(j).wait_send()
            post(c).wait_send()
            keep.wait()


def _attn_bwd(q, k, v, do, lse, delta, km, vm, early, nb, s, t):
    n = s // t
    ne = len(early)
    reds = [_StagedReduce(a.shape[1:]) for a in early]
    n_steps = HEADS * nb
    assert n_steps >= 4

    def body(q_ref, k_ref, v_ref, do_ref, lse_ref, dl_ref, km_ref, vm_ref, *rest):
        pin_refs, rest = rest[:ne], rest[ne:]
        dq_ref, dk_ref, dv_ref, dkm_ref, dvm_ref = rest[:5]
        gout_refs, (p_scr, ds_scr, dq_acc), red_scr = rest[5:5 + ne], rest[5 + ne:8 + ne], rest[8 + ne:]
        b = pl.program_id(1)
        step = pl.program_id(0) * nb + b
        for stage, at in enumerate((0, 1, n_steps - 2, n_steps - 1)):
            @pl.when(step == at)
            def _(stage=stage):
                for a, red in enumerate(reds):
                    red.run(stage, pin_refs[a], gout_refs[a], red_scr[6 * a:6 * a + 6])

        @pl.when(b == 0)
        def _():
            dkm_ref[...] = jnp.zeros_like(dkm_ref)
            dvm_ref[...] = jnp.zeros_like(dvm_ref)

        kr = lax.broadcasted_iota(jnp.int32, (t, t), 0)
        qc = lax.broadcasted_iota(jnp.int32, (t, t), 1)
        km_v, vm_v = km_ref[0, 0], vm_ref[0, 0]
        ptm = jnp.exp(_dot_nt(km_v, q_ref[0, 0]) - lse_ref[0, 0])
        dstm = (ptm * (_dot_nt(vm_v, do_ref[0, 0]) - dl_ref[0, 0])).astype(BF16)
        dkm_ref[0] += _dot(dstm, q_ref[0, 0])
        dvm_ref[0] += _dot(ptm.astype(BF16), do_ref[0, 0])
        dq_acc[...] = _dot_tn(dstm, km_v)
        def tiles(j):
            slot = j % 2
            kj = k_ref[0, 0, j * t:(j + 1) * t, :]
            vj = v_ref[0, 0, j * t:(j + 1) * t, :]
            def products(i):
                cs = slice(i * t, (i + 1) * t)
                return _dot_nt(kj, q_ref[0, 0, cs, :]), _dot_nt(vj, do_ref[0, 0, cs, :])

            nxt, pending = products(j), None
            for i in range(j, n):
                cs = slice(i * t, (i + 1) * t)
                st, dpt = nxt
                if i + 1 < n:
                    nxt = products(i + 1)
                if i == j:
                    st = jnp.where(kr <= qc, st, NEG_INF)
                pt = jnp.exp(st - lse_ref[0, 0, :, cs])
                dst = (pt * (dpt - dl_ref[0, 0, :, cs])).astype(BF16)
                p_scr[slot, :, cs] = pt.astype(BF16)
                ds_scr[slot, :, cs] = dst
                if pending is not None:
                    dq_acc[pending[0], :] += _dot_tn(pending[1], kj)
                pending = (cs, dst)
            dq_acc[pending[0], :] += _dot_tn(pending[1], kj)

        for j in range(n):
            slot = j % 2
            tiles(j)
            dv_ref[0, 0, j * t:(j + 1) * t, :] = _dot(p_scr[slot, :, j * t:s], do_ref[0, 0, j * t:s, :]).astype(BF16)
            dk_ref[0, 0, j * t:(j + 1) * t, :] = _dot(ds_scr[slot, :, j * t:s], q_ref[0, 0, j * t:s, :]).astype(BF16)
        dq_ref[0, 0] = dq_acc[...].astype(BF16)

    big = lambda w: pl.BlockSpec((1, 1, s, w), lambda h, b: (b, h, 0, 0))
    rowv = pl.BlockSpec((1, 1, 1, s), lambda h, b: (b, h, 0, 0))
    mk = lambda w: pl.BlockSpec((1, 1, N_META, w), lambda h, b: (0, h, 0, 0))
    mo = lambda w: pl.BlockSpec((1, N_META, w), lambda h, b: (h, 0, 0))
    return pl.pallas_call(
        body, name="attn_bwd", grid=(HEADS, nb),
        in_specs=[big(QK_PAD), big(QK_PAD), big(VDIM), big(VDIM), rowv, rowv, mk(QK_PAD), mk(VDIM)]
        + [pl.BlockSpec(memory_space=pl.ANY)] * ne,
        out_specs=[big(QK_PAD), big(QK_PAD), big(VDIM), mo(QK_PAD), mo(VDIM)]
        + [pl.BlockSpec(memory_space=pl.ANY)] * ne,
        out_shape=[jax.ShapeDtypeStruct((nb, HEADS, s, QK_PAD), BF16),
                   jax.ShapeDtypeStruct((nb, HEADS, s, QK_PAD), BF16),
                   jax.ShapeDtypeStruct((nb, HEADS, s, VDIM), BF16),
                   jax.ShapeDtypeStruct((HEADS, N_META, QK_PAD), F32),
                   jax.ShapeDtypeStruct((HEADS, N_META, VDIM), F32)]
        + [jax.ShapeDtypeStruct(a.shape[1:], F32) for a in early],
        scratch_shapes=[pltpu.VMEM((2, t, s), BF16), pltpu.VMEM((2, t, s), BF16), pltpu.VMEM((s, QK_PAD), F32)]
        + [sc for red in reds for sc in red.scratch()],
        compiler_params=_cparams("arbitrary", "arbitrary"),
    )(q, k, v, do, lse, delta, km, vm, *early)


def _up_bwd(dq, dk, dv, dkm, dvm, p, pm, tabs, tabs_m, wq_p, wkv_p, gq, gkv, nb, s, tm):
    nt = s // tm
    n = nb * nt
    c_t, sa_t, sb_t = tabs
    cm_t, sam_t, sbm_t = tabs_m

    def kv_path(dkh, dvh, pa, c, sa, sb, wkv, gkvv):
        dkpe = dkh[0][:, NOPE:]
        for h in range(1, HEADS):
            dkpe = dkpe + dkh[h][:, NOPE:]
        dkr = _rope_bwd(dkpe, c, sa, sb)
        dkv = jnp.concatenate([d[:, :NOPE] for d in dkh] + list(dvh), axis=1).astype(BF16)
        ckv = pa[:, Q_RANK:Q_RANK + KV_RANK]
        kvn, rkv = _rms(ckv, gkvv)
        dckv, dg = _rms_bwd(_dot(dkv, wkv), ckv, rkv, gkvv)
        return dckv, dkr, kvn.astype(BF16), dkv, jnp.sum(dg, axis=0, keepdims=True)

    def body(dq_ref, dk_ref, dv_ref, pa_ref, c_ref, sa_ref, sb_ref,
             dkm_ref, dvm_ref, pam_ref, cm_ref, sam_ref, sbm_ref,
             wq_ref, wkv_ref, gq_ref, gkv_ref,
             dpa_ref, dpam_ref, pq_ref, pkv_ref, dgq_ref, dgkv_ref, dwq_ref, dwkv_ref):
        i = pl.program_id(0)

        @pl.when(i == 0)
        def _():
            dwq_ref[...] = jnp.zeros_like(dwq_ref)
            dwkv_ref[...] = jnp.zeros_like(dwkv_ref)
            dgq_ref[...] = jnp.zeros_like(dgq_ref)
            dgkv_ref[...] = jnp.zeros_like(dgkv_ref)

        @pl.when(i < n)
        def _():
            c, sa, sb = c_ref[...], sa_ref[...], sb_ref[...]
            pa = pa_ref[...]
            parts = []
            for h in range(HEADS):
                dqh = dq_ref[0, h].astype(F32) * ATTN_SCALE
                parts += [dqh[:, :NOPE], _rope_bwd(dqh[:, NOPE:], c, sa, sb)]
            dql = jnp.concatenate(parts, axis=1).astype(BF16)
            cq = pa[:, 0:Q_RANK]
            gqv = gq_ref[...]
            qn, rq = _rms(cq, gqv)
            dwq_ref[...] += _dot_tn(dql, qn.astype(BF16))
            dcq, dg = _rms_bwd(_dot(dql, wq_ref[...]), cq, rq, gqv)
            dgq_ref[...] += jnp.sum(dg, axis=0, keepdims=True)
            dckv, dkr, kvn, dkv, dgk = kv_path([dk_ref[0, h].astype(F32) for h in range(HEADS)],
                                               [dv_ref[0, h].astype(F32) for h in range(HEADS)],
                                               pa, c, sa, sb, wkv_ref[...], gkv_ref[...])
            dwkv_ref[...] += _dot_tn(dkv, kvn)
            dgkv_ref[...] += dgk
            dpa_ref[...] = jnp.concatenate([dcq, dckv, dkr], axis=1).astype(BF16)

        @pl.when(i == n)
        def _():
            dckv, dkr, kvn, dkv, dgk = kv_path([dkm_ref[h] for h in range(HEADS)],
                                               [dvm_ref[h] for h in range(HEADS)],
                                               pam_ref[...], cm_ref[...], sam_ref[...], sbm_ref[...],
                                               wkv_ref[...], gkv_ref[...])
            dwkv_ref[...] += _dot_tn(dkv, kvn)
            dgkv_ref[...] += dgk
            dpam_ref[...] = jnp.concatenate([jnp.zeros((N_META, Q_RANK), F32), dckv, dkr], axis=1)
            for h in range(HEADS):
                pq_ref[h] = dwq_ref[QK_PAD * h:QK_PAD * h + NOPE + ROPE, :]
                pkv_ref[h, 0:NOPE, :] = dwkv_ref[NOPE * h:NOPE * (h + 1), :]
                pkv_ref[h, NOPE:NOPE + VDIM, :] = dwkv_ref[512 + VDIM * h:512 + VDIM * (h + 1), :]

    cl = lambda i: jnp.minimum(i, n - 1)
    hb = lambda w: pl.BlockSpec((1, HEADS, tm, w), lambda i: (cl(i) // nt, 0, cl(i) % nt, 0))
    tab = pl.BlockSpec((tm, 128), lambda i: (cl(i) % nt, 0))
    full = lambda a: pl.BlockSpec(a.shape, lambda i: (0,) * a.ndim)
    const = lambda shape: pl.BlockSpec(shape, lambda i: (0,) * len(shape))
    return pl.pallas_call(
        body, name="up_bwd", grid=(n + 1,),
        in_specs=[hb(QK_PAD), hb(QK_PAD), hb(VDIM), pl.BlockSpec((tm, 512), lambda i: (cl(i), 0)), tab, tab, tab,
                  full(dkm), full(dvm), pl.BlockSpec((N_META, 512), lambda i: (0, 0)),
                  full(cm_t), full(sam_t), full(sbm_t), full(wq_p), full(wkv_p), full(gq), full(gkv)],
        out_specs=[pl.BlockSpec((tm, 512), lambda i: (cl(i), 0)), const((N_META, 512)),
                   const((HEADS, NOPE + ROPE, Q_RANK)), const((HEADS, NOPE + VDIM, KV_RANK)),
                   const((1, Q_RANK)), const((1, KV_RANK))],
        out_shape=[jax.ShapeDtypeStruct((nb * s, 512), BF16), jax.ShapeDtypeStruct((N_META, 512), F32),
                   jax.ShapeDtypeStruct((HEADS, NOPE + ROPE, Q_RANK), F32),
                   jax.ShapeDtypeStruct((HEADS, NOPE + VDIM, KV_RANK), F32),
                   jax.ShapeDtypeStruct((1, Q_RANK), F32), jax.ShapeDtypeStruct((1, KV_RANK), F32)],
        scratch_shapes=[pltpu.VMEM((HEADS * QK_PAD, Q_RANK), F32), pltpu.VMEM((1024, KV_RANK), F32)],
        compiler_params=_cparams("arbitrary"),
    )(dq, dk, dv, p, c_t, sa_t, sb_t, dkm, dvm, pm, cm_t, sam_t, sbm_t, wq_p, wkv_p, gq, gkv)


def _in_bwd(x2d, dh2, dpa, dpb, meta, dpam, dccm, pm, w_in_p, norm_g, nb, s, tm):
    nt = s // tm
    n = nb * nt

    def body(x_ref, dh_ref, dpa_ref, dpb_ref, mt_ref, dpam_ref, dccm_ref, mc_ref, mh_ref, w_ref, g_ref,
             gx_ref, gm_ref, dw_hbm, dg_ref, acc_ref, sems):
        i = pl.program_id(0)

        @pl.when(i == 0)
        def _():
            acc_ref[...] = jnp.zeros_like(acc_ref)
            dg_ref[...] = jnp.zeros_like(dg_ref)

        def rows(x, dp, dres):
            g = g_ref[...]
            dpb16 = dp.astype(BF16)
            du = _dot(dpb16, w_ref[...])
            u, r1 = _rms(x, g)
            acc_ref[...] += _dot_tn(dpb16, u.astype(BF16))
            dx, dg = _rms_bwd(du, x, r1, g)
            dg_ref[...] += jnp.sum(dg, axis=0, keepdims=True)
            return dx if dres is None else dx + dres

        @pl.when(i < n)
        def _():
            dp = jnp.concatenate([dpa_ref[...], dpb_ref[...]], axis=1)
            gx_ref[...] = rows(x_ref[...], dp, dh_ref[...])

        @pl.when(i == n)
        def _():
            dcc = dccm_ref[0]
            for b in range(1, nb):
                dcc = dcc + dccm_ref[b]
            z8 = jnp.zeros((8, CONV_W), F32)
            dc = jnp.concatenate([z8, dcc * mh_ref[8:16, :]], axis=0)
            dh = jnp.concatenate([z8, dcc * mc_ref[8:16, :]], axis=0)
            z = jnp.zeros((N_META, CONV_W), F32)
            dp = jnp.concatenate([dpam_ref[...], z, z, dc, dh, z], axis=1)
            gm_ref[...] = rows(mt_ref[...], dp, None)
            per = IN_DIM // 4
            cps = [pltpu.make_async_copy(acc_ref.at[0:448], dw_hbm.at[0, 0:448], sems.at[0]),
                   pltpu.make_async_copy(acc_ref.at[512:per + 64], dw_hbm.at[0, 448:per], sems.at[1])]
            for qq in range(1, 4):
                cps.append(pltpu.make_async_copy(acc_ref.at[per * qq + 64:per * (qq + 1) + 64], dw_hbm.at[qq],
                                                 sems.at[qq + 1]))
            for cp in cps:
                cp.start()
            for cp in cps:
                cp.wait()

    cl = lambda i: jnp.minimum(i, n - 1)
    row = lambda w: pl.BlockSpec((tm, w), lambda i: (cl(i), 0))
    full = lambda a: pl.BlockSpec(a.shape, lambda i: (0,) * a.ndim)
    mblk = lambda j: pl.BlockSpec((N_META, 512), lambda i: (0, j))
    return pl.pallas_call(
        body, name="in_bwd", grid=(n + 1,),
        in_specs=[row(D_MODEL), row(D_MODEL), row(512), row(2560), full(meta), full(dpam), full(dccm),
                  mblk(BLK_CC), mblk(BLK_CH), full(w_in_p), full(norm_g)],
        out_specs=[row(D_MODEL), pl.BlockSpec((N_META, D_MODEL), lambda i: (0, 0)),
                   pl.BlockSpec(memory_space=pl.ANY), pl.BlockSpec((1, D_MODEL), lambda i: (0, 0))],
        out_shape=[jax.ShapeDtypeStruct((nb * s, D_MODEL), F32), jax.ShapeDtypeStruct((N_META, D_MODEL), F32),
                   jax.ShapeDtypeStruct((4, IN_DIM // 4, D_MODEL), F32), jax.ShapeDtypeStruct((1, D_MODEL), F32)],
        scratch_shapes=[pltpu.VMEM((IN_PAD, D_MODEL), F32), pltpu.SemaphoreType.DMA((5,))],
        compiler_params=_cparams("arbitrary"),
    )(x2d, dh2, dpa, dpb, meta, dpam, dccm, pm, pm, w_in_p, norm_g)


def _gather_weights(w_in_shard, w_out_shard, split, pieces, out_rows, whole, zero_fills):
    ns, nw, nz = len(split), len(whole), len(zero_fills)
    flat = [(a, pc) for a in range(ns) for pc in pieces[a]]
    nk = len(flat)
    hh = HEAD_ROWS // 2
    assert hh % 16 == 0

    def body(*refs):
        ins, wins, zins = refs[2:2 + ns], refs[2 + ns:2 + ns + nw], refs[2 + ns + nw:2 + ns + nw + nz]
        n_in = 2 + ns + nw + nz
        head_ref, shard16, w_out16 = refs[n_in:n_in + 3]
        outs, wcat = refs[n_in + 3:n_in + 3 + ns], refs[n_in + 3 + ns:n_in + 3 + ns + nw]
        scr = refs[n_in + 3 + ns + nw:]
        stage, wouts = scr[:ns], scr[ns:ns + nw]
        (send_sems, recv_sems, fwd_send, fwd_recv, loc_sems, w_send, w_recv, w_loc, z_sems,
         h_send, h_recv, h_relay, h_pass) = scr[ns + nw:]
        x, y, c = lax.axis_index("x"), lax.axis_index("y"), lax.axis_index("c")
        mine = 2 * x + y
        chips = [(1 - x, y), (x, 1 - y), (1 - x, 1 - y)]
        chip_of = [2 * px + py for px, py in chips]
        shard16[...] = refs[0][...].astype(BF16)
        w_out16[...] = refs[1][...].astype(BF16)
        for a in range(ns):
            stage[a][...] = ins[a][...].astype(BF16)

        def head_rows(half):
            return pl.ds(pl.multiple_of(half * hh, 16), hh)

        def head_copy(turn):
            dest = (1 - c, c, c) if turn == 0 else (c, 1 - c, c)
            return pltpu.make_async_remote_copy(
                src_ref=shard16.at[head_rows(c)], dst_ref=head_ref.at[head_rows(c)], send_sem=h_send.at[turn],
                recv_sem=h_recv.at[0], device_id=dest, device_id_type=pl.DeviceIdType.MESH)

        def head_relay():
            ref = head_ref.at[head_rows(c)]
            return pltpu.make_async_remote_copy(
                src_ref=ref, dst_ref=ref, send_sem=h_relay.at[0], recv_sem=h_recv.at[0],
                device_id=(1, 1, c), device_id_type=pl.DeviceIdType.MESH)

        def head_pass(half):
            ref = head_ref.at[head_rows(half)]
            return pltpu.make_async_remote_copy(
                src_ref=ref, dst_ref=ref, send_sem=h_pass.at[0], recv_sem=h_pass.at[1],
                device_id=(x, y, 1 - c), device_id_type=pl.DeviceIdType.MESH)

        head_ref[HEAD_ROWS:IN_HEAD, :] = jnp.zeros((IN_HEAD - HEAD_ROWS, D_MODEL), BF16)

        @pl.when(mine == 0)
        def _():
            head_copy(0).start()
            head_ref[0:HEAD_ROWS, :] = shard16[0:HEAD_ROWS, :]

        def src(k):
            a, (s0, nr, _, _, _, _) = flat[k]
            return stage[a].at[s0:s0 + nr]

        def dst(k, q):
            a, (_, nr, per, first, rest, _) = flat[k]
            row = per * q + first + (rest - first) * jnp.minimum(q, 1)
            return outs[a].at[pl.ds(pl.multiple_of(row, 16), nr)]

        def ici(k, j, q):
            px, py = chips[j]
            return pltpu.make_async_remote_copy(
                src_ref=src(k), dst_ref=dst(k, q), send_sem=send_sems.at[k, j], recv_sem=recv_sems.at[k, j],
                device_id=(px, py, c), device_id_type=pl.DeviceIdType.MESH)

        def fwd(k, j):
            ref = dst(k, chip_of[j])
            return pltpu.make_async_remote_copy(
                src_ref=ref, dst_ref=ref, send_sem=fwd_send.at[k, j], recv_sem=fwd_recv.at[k, j],
                device_id=(x, y, 1 - c), device_id_type=pl.DeviceIdType.MESH)

        def wcopy(b, j, q):
            px, py = chips[j]
            return pltpu.make_async_remote_copy(
                src_ref=wins[b], dst_ref=wouts[b].at[q], send_sem=w_send.at[b, j], recv_sem=w_recv.at[b, j],
                device_id=(px, py, c), device_id_type=pl.DeviceIdType.MESH)

        local = [pltpu.make_async_copy(src(k), dst(k, mine), loc_sems.at[k]) for k in range(nk)]
        local += [pltpu.make_async_copy(wins[b], wouts[b].at[mine], w_loc.at[b]) for b in range(nw)]
        for z, (a, _, row0) in enumerate(zero_fills):
            local.append(pltpu.make_async_copy(zins[z], outs[a].at[row0:row0 + zins[z].shape[0]], z_sems.at[z]))
        wsends = [wcopy(b, j, mine) for b in range(nw) for j in range(3)]
        for cp in local + wsends:
            cp.start()

        for half in (0, 1):
            @pl.when(c == half)
            def _(half=half):
                my_k = [k for k in range(nk) if flat[k][1][5] == half]
                other_k = [k for k in range(nk) if flat[k][1][5] != half]
                sends = [ici(k, j, mine) for k in my_k for j in range(3)]
                for cp in sends:
                    cp.start()
                passed = []
                for k in my_k:
                    for j in range(3):
                        ici(k, j, chip_of[j]).wait_recv()
                        cp = fwd(k, j)
                        cp.start()
                        passed.append(cp)
                for k in other_k:
                    for j in range(3):
                        fwd(k, j).wait_recv()
                for cp in sends + passed:
                    cp.wait_send()

        for b in range(nw):
            for j in range(3):
                wcopy(b, j, chip_of[j]).wait_recv()
        for cp in wsends:
            cp.wait_send()
        for cp in local:
            cp.wait()
        for b in range(nw):
            cols = wins[b].shape[-1]
            for q in range(4):
                wcat[b][..., cols * q:cols * (q + 1)] = wouts[b][q]

        @pl.when(mine == 0)
        def _():
            head_copy(0).wait_send()
            head_copy(1).start()
            head_copy(1).wait_send()

        @pl.when(mine != 0)
        def _():
            hands_on = mine == 2 - c
            head_copy(0).wait_recv()

            @pl.when(hands_on)
            def _():
                head_relay().start()

            head_pass(c).start()
            head_pass(1 - c).wait_recv()
            head_pass(c).wait_send()

            @pl.when(hands_on)
            def _():
                head_relay().wait_send()

    vmem = pl.BlockSpec(memory_space=pltpu.VMEM)
    dma = pltpu.SemaphoreType.DMA
    zeros = [z for _, z, _ in zero_fills]
    return pl.pallas_call(
        body, name="gather_weights",
        in_specs=[vmem] * (2 + ns + nw + nz), out_specs=[vmem] * (3 + ns + nw),
        out_shape=([jax.ShapeDtypeStruct((IN_HEAD, D_MODEL), BF16), jax.ShapeDtypeStruct(w_in_shard.shape, BF16),
                    jax.ShapeDtypeStruct(w_out_shard.shape, BF16)]
                   + [jax.ShapeDtypeStruct((out_rows[a], split[a].shape[1]), BF16) for a in range(ns)]
                   + [jax.ShapeDtypeStruct(w.shape[:-1] + (4 * w.shape[-1],), w.dtype) for w in whole]),
        scratch_shapes=[pltpu.VMEM(a.shape, BF16) for a in split] + [pltpu.VMEM((4,) + w.shape, w.dtype) for w in whole]
        + [dma((nk, 3)), dma((nk, 3)), dma((nk, 3)), dma((nk, 3)), dma((nk,)),
           dma((nw, 3)), dma((nw, 3)), dma((nw,)), dma((nz,)),
           dma((2,)), dma((1,)), dma((1,)), dma((2,))],
        compiler_params=pltpu.CompilerParams(vmem_limit_bytes=VMEM_LIMIT),
    )(w_in_shard, w_out_shard, *split, *whole, *zeros)


def _reduce_grads(parts, small):
    n = len(parts)
    shapes = [a.shape[1:] for a in parts]
    halves = [(sh[0] // 2, sh[1]) for sh in shapes]

    def body(*refs):
        pin, sm_in = refs[:n], refs[n]
        gout, sm_out = refs[n + 1:2 * n + 1], refs[2 * n + 1]
        scr = refs[2 * n + 2:]
        own, sib, wire, rbuf = scr[:n], scr[n:2 * n], scr[2 * n:3 * n], scr[3 * n:4 * n]
        (sbuf, send_sems, recv_sems, loc_sems, pre_send, pre_recv, post_send, post_recv,
         sm_send, sm_recv) = scr[4 * n:]
        x, y, c = lax.axis_index("x"), lax.axis_index("y"), lax.axis_index("c")
        mine = 2 * x + y
        me = 4 * x + 2 * y + c
        sibling = (x, y, 1 - c)

        def rows(a, half):
            r2 = halves[a][0]
            return pl.ds(pl.multiple_of(half * r2, r2), r2)

        near = (jnp.where(c == 0, 1 - x, x), jnp.where(c == 0, y, 1 - y))
        far = (jnp.where(c == 0, x, 1 - x), jnp.where(c == 0, 1 - y, y))
        chip = lambda p: 2 * p[0] + p[1]
        blocks = [3 - mine, chip(near), chip(far), mine]

        def pre(a, k):
            q = blocks[k]
            return pltpu.make_async_remote_copy(
                src_ref=pin[a].at[q, rows(a, 1 - c), :], dst_ref=sib[a].at[q],
                send_sem=pre_send.at[a, q], recv_sem=pre_recv.at[a, q], device_id=sibling,
                device_id_type=pl.DeviceIdType.MESH)

        def ici(a, m):
            px, py = near if m < 2 else far
            return pltpu.make_async_remote_copy(
                src_ref=wire[a].at[blocks[m]], dst_ref=rbuf[a].at[m], send_sem=send_sems.at[a, m],
                recv_sem=recv_sems.at[a, m], device_id=(px, py, c), device_id_type=pl.DeviceIdType.MESH)

        def post(a, half):
            ref = gout[a].at[rows(a, half), :]
            return pltpu.make_async_remote_copy(
                src_ref=ref, dst_ref=ref, send_sem=post_send.at[a], recv_sem=post_recv.at[a],
                device_id=sibling, device_id_type=pl.DeviceIdType.MESH)

        def small_copy(kk):
            peer = (x ^ (kk >> 2), y ^ ((kk >> 1) & 1), c ^ (kk & 1))
            return pltpu.make_async_remote_copy(
                src_ref=sm_in, dst_ref=sbuf.at[kk], send_sem=sm_send.at[kk - 1], recv_sem=sm_recv.at[kk - 1],
                device_id=peer, device_id_type=pl.DeviceIdType.MESH)

        local = [[pltpu.make_async_copy(pin[a].at[blocks[k], rows(a, c), :], own[a].at[blocks[k]], loc_sems.at[a, k])
                  for k in range(4)] for a in range(n)]
        pres = [[pre(a, k) for k in range(4)] for a in range(n)]
        smalls = [small_copy(kk) for kk in range(1, 8)]
        for a in range(n):
            for k in range(4):
                local[a][k].start()
                pres[a][k].start()
        for cp in smalls:
            cp.start()
        sbuf[0] = sm_in[...]
        sends = []
        for a in range(n):
            for k in range(4):
                local[a][k].wait()
                pres[a][k].wait_recv()
                tot = own[a][blocks[k]] + sib[a][blocks[k]]
                if k == 2:
                    ici(a, 0).wait_recv()
                    tot = tot + rbuf[a][0].astype(F32)
                own[a][blocks[k]] = tot
                if k < 3:
                    wire[a][blocks[k]] = tot.astype(BF16)
                    cp = ici(a, k)
                    cp.start()
                    sends.append(cp)
        for cp in smalls:
            cp.wait_recv()
        total = sbuf[me]
        for d in range(1, 8):
            total = total + sbuf[me ^ d]
        sm_out[...] = total
        posts = []
        for a in range(n):
            fin = own[a][mine]
            for m in (1, 2):
                ici(a, m).wait_recv()
                fin = fin + rbuf[a][m].astype(F32)
            gout[a][rows(a, c), :] = fin
            cp = post(a, c)
            cp.start()
            posts.append(cp)
        for a in range(n):
            post(a, 1 - c).wait_recv()
        for cp in [cp for row in pres for cp in row] + sends + smalls + posts:
            cp.wait_send()

    vmem = pl.BlockSpec(memory_space=pltpu.VMEM)
    dma = pltpu.SemaphoreType.DMA
    return pl.pallas_call(
        body, name="reduce_grads",
        in_specs=[pl.BlockSpec(memory_space=pl.ANY)] * n + [vmem], out_specs=[vmem] * (n + 1),
        out_shape=[jax.ShapeDtypeStruct(sh, F32) for sh in shapes] + [jax.ShapeDtypeStruct(small.shape, F32)],
        scratch_shapes=([pltpu.VMEM((4,) + hs, F32) for hs in halves] + [pltpu.VMEM((4,) + hs, F32) for hs in halves]
                        + [pltpu.VMEM((4,) + hs, BF16) for hs in halves]
                        + [pltpu.VMEM((3,) + hs, BF16) for hs in halves]
                        + [pltpu.VMEM((8,) + small.shape, F32), dma((n, 3)), dma((n, 3)), dma((n, 4)),
                           dma((n, 4)), dma((n, 4)), dma((n,)), dma((n,)), dma((7,)), dma((7,))]),
        compiler_params=pltpu.CompilerParams(vmem_limit_bytes=VMEM_LIMIT),
    )(*parts, small)


def _adamw_update(w_ref, g_ref, m_ref, v_ref, d_ref, nm_ref, nv_ref):
    gv = g_ref[...]
    nm = ADAM_B1 * m_ref[...] + (1.0 - ADAM_B1) * gv
    nv = ADAM_B2 * v_ref[...] + (1.0 - ADAM_B2) * (gv * gv)
    m_hat = nm / (1.0 - ADAM_B1 ** ADAM_STEP)
    v_hat = nv / (1.0 - ADAM_B2 ** ADAM_STEP)
    d_ref[...] = -ADAM_LR * (m_hat / (jnp.sqrt(v_hat) + ADAM_EPS) + ADAM_WD * w_ref[...])
    nm_ref[...] = nm
    nv_ref[...] = nv


def _adamw_small(ws, gs, ms, vs):
    k = len(ws)

    def body(*refs):
        ins, outs = refs[:4 * k], refs[4 * k:]
        for a in range(k):
            _adamw_update(ins[a], ins[k + a], ins[2 * k + a], ins[3 * k + a], outs[a], outs[k + a], outs[2 * k + a])

    out = pl.pallas_call(
        body, name="adamw_small",
        out_shape=[jax.ShapeDtypeStruct(w.shape, F32) for w in ws] * 3,
        compiler_params=pltpu.CompilerParams(vmem_limit_bytes=VMEM_LIMIT),
    )(*ws, *gs, *ms, *vs)
    return out[:k], out[k:2 * k], out[2 * k:]


def _adamw(w, g, m, v, name):
    shape = w.shape
    w2, g2, m2, v2 = (a.reshape((-1, shape[-1])) for a in (w, g, m, v))

    def body(w_ref, g_ref, m_ref, v_ref, d_ref, nm_ref, nv_ref):
        _adamw_update(w_ref, g_ref, m_ref, v_ref, d_ref, nm_ref, nv_ref)

    rows, cols = w2.shape
    nblk = cols // 256 if cols % 256 == 0 and rows >= 64 else 1
    blk = pl.BlockSpec((rows, cols // nblk), lambda j: (0, j))
    out = pl.pallas_call(
        body, name=name, grid=(nblk,), in_specs=[blk] * 4, out_specs=[blk] * 3,
        out_shape=[jax.ShapeDtypeStruct(w2.shape, F32)] * 3,
        compiler_params=_cparams("parallel"),
    )(w2, g2, m2, v2)
    return tuple(a.reshape(shape) for a in out)


def kernel(x, meta_tokens, norm_g, w_in, q_norm_g, w_q_up, kv_norm_g, w_kv_up, conv_w, attn_out_g, conv_out_g, w_out, final_norm_g, loss_target, m_meta_tokens, m_norm_g, m_w_in, m_q_norm_g, m_w_q_up, m_kv_norm_g, m_w_kv_up, m_conv_w, m_attn_out_g, m_conv_out_g, m_w_out, m_final_norm_g, v_meta_tokens, v_norm_g, v_w_in, v_q_norm_g, v_w_q_up, v_kv_norm_g, v_w_kv_up, v_conv_w, v_attn_out_g, v_conv_out_g, v_w_out, v_final_norm_g):
    nb, s, _ = x.shape
    tm = min(ROW_TILE, s)
    ta = min(ATTN_TILE, s)
    assert s % tm == 0 and s % ta == 0 and tm % 16 == 0
    r = nb * s

    tr = lambda a: jnp.transpose(a[0])
    w_head, w_in_shard, w_out_shard, wq_p, wkv_p, g_cw, meta_f = _gather_weights(
        tr(w_in), w_out[0], [tr(w_q_up), tr(w_kv_up)],
        [W_Q_PIECES, W_KV_PIECES], [HEADS * QK_PAD, 1024],
        [jnp.transpose(conv_w, (1, 0, 2)), meta_tokens],
        [(0, jnp.zeros((64, Q_RANK), BF16), QK_PAD * h + NOPE + ROPE) for h in range(HEADS)])
    conv_f = g_cw.reshape(3, CONV_W)

    c_all, sa_all, sb_all = _rope_tables(N_META + s)
    tabs_m = (c_all[:N_META], sa_all[:N_META], sb_all[:N_META])
    tabs = (c_all[N_META:], sa_all[N_META:], sb_all[N_META:])
    gid = np.arange(CONV_W) // CONV_GROUP
    gmat = jnp.asarray(np.where(gid[:, None] == gid[None, :], 1.0 / CONV_GROUP, 0.0), BF16)
    ga, gc = attn_out_g, conv_out_g
    gf = final_norm_g.reshape(1, D_MODEL)

    x2d = x.reshape(r, D_MODEL)
    tgt2d = loss_target.reshape(r, D_MODEL)

    ph, q, k, v, pmh, km, vm, w_out_f, w_in_part = _fwd_proj(
        x2d, meta_f, tabs, tabs_m, norm_g, w_head, q_norm_g, wq_p, kv_norm_g, wkv_p, w_out_shard, w_in_shard,
        nb, s, tm)
    o, lse, w_in_p = _attn_fwd(q, k, v, km, vm, w_in_shard, w_in_part, nb, s, ta)
    dh2, dycat, dw_out, dgf, loss_acc, pt, pmt = _out_fwd_bwd(x2d, tgt2d, o, meta_f, norm_g, w_in_p, conv_f, ga, gc,
                                                              gmat, w_out_f, gf, nb, s, tm)
    dpb, do, delta, dccm, dga, dgc, dcw = _gate_bwd(dycat, o, pt, pmt, conv_f, ga, gc, gmat, nb, s, tm)
    p_out = dw_out.reshape(4, D_MODEL // 4, D_MODEL)
    dq, dk, dv, dkm, dvm, g_w_out = _attn_bwd(q, k, v, do, lse, delta, km, vm, [p_out], nb, s, ta)
    dpa, dpam, p_q, p_kv, dgq, dgkv = _up_bwd(dq, dk, dv, dkm, dvm, ph, pmh, tabs, tabs_m, wq_p, wkv_p,
                                              q_norm_g, kv_norm_g, nb, s, tm)
    gx, gmeta, p_in, dng = _in_bwd(x2d, dh2, dpa, dpb, meta_f, dpam, dccm, pmt, w_in_p, norm_g, nb, s, tm)

    flat =jnp.concatenate([dng.reshape(-1), dgq.reshape(-1), dgkv.reshape(-1), dga.reshape(-1), dgc.reshape(-1),
                            dgf.reshape(-1), dcw[:3].reshape(-1), gmeta.reshape(-1), loss_acc[0, 0:1]])
    n_small = flat.shape[0]
    rows_small = -(-n_small // 1024) * 8
    small = jnp.pad(flat, (0, rows_small * 128 - n_small)).reshape(rows_small, 128)
    g_w_in_t, g_w_q_t, g_w_kv_t, small_sum = _reduce_grads([p_in, p_q, p_kv], small)
    ssum = small_sum.reshape(-1)

    def take(off, n):
        return ssum[off:off + n], off + n

    off = 0
    g_norm, off = take(off, D_MODEL)
    g_qn, off = take(off, Q_RANK)
    g_kvn, off = take(off, KV_RANK)
    g_ga, off = take(off, CONV_W)
    g_gc, off = take(off, CONV_W)
    g_gf, off = take(off, D_MODEL)
    g_cw_all, off = take(off, 3 * CONV_W)
    g_meta_all, off = take(off, N_META * D_MODEL)
    loss = ssum[off]
    chip = 2 * lax.axis_index("x") + lax.axis_index("y")
    g_conv = lax.dynamic_slice(g_cw_all.reshape(3, CONV_W), (0, chip * 128), (3, 128))
    g_mt = lax.dynamic_slice(g_meta_all.reshape(N_META, D_MODEL), (0, chip * 256), (N_META, 256))

    grads = {
        "meta_tokens": g_mt, "norm_g": g_norm.reshape(1, -1), "w_in": g_w_in_t, "q_norm_g": g_qn.reshape(1, -1),
        "w_q_up": g_w_q_t, "kv_norm_g": g_kvn.reshape(1, -1), "w_kv_up": jnp.transpose(g_w_kv_t)[None],
        "conv_w": g_conv[None], "attn_out_g": g_ga.reshape(1, -1), "conv_out_g": g_gc.reshape(1, -1),
        "w_out": g_w_out[None], "final_norm_g": g_gf,
    }
    transposed = ("w_in", "w_q_up")
    weights = {
        "meta_tokens": (meta_tokens, m_meta_tokens, v_meta_tokens), "norm_g": (norm_g, m_norm_g, v_norm_g),
        "w_in": (w_in, m_w_in, v_w_in), "q_norm_g": (q_norm_g, m_q_norm_g, v_q_norm_g),
        "w_q_up": (w_q_up, m_w_q_up, v_w_q_up), "kv_norm_g": (kv_norm_g, m_kv_norm_g, v_kv_norm_g),
        "w_kv_up": (w_kv_up, m_w_kv_up, v_w_kv_up), "conv_w": (conv_w, m_conv_w, v_conv_w),
        "attn_out_g": (attn_out_g, m_attn_out_g, v_attn_out_g), "conv_out_g": (conv_out_g, m_conv_out_g, v_conv_out_g),
        "w_out": (w_out, m_w_out, v_w_out), "final_norm_g": (final_norm_g, m_final_norm_g, v_final_norm_g),
    }
    names = list(weights)
    small = [nme for nme in names if nme != "w_in"]

    def view(nme, a):
        if nme in transposed:
            return a if a.ndim == 2 else tr(a)
        if nme == "conv_w":
            return jnp.transpose(a.reshape(1, 3, -1), (1, 0, 2))
        if a.ndim == 3:
            return a[0]
        return a.reshape(1, -1) if a.ndim == 1 else a

    def unview(nme, a):
        if nme in transposed:
            return jnp.transpose(a)[None]
        if nme == "conv_w":
            return jnp.transpose(a, (1, 0, 2))
        return a.reshape(weights[nme][0].shape)

    res_small = _adamw_small(*[[view(nme, a) for nme, a in zip(small, col)] for col in (
        [weights[nme][0] for nme in small], [grads[nme] for nme in small],
        [weights[nme][1] for nme in small], [weights[nme][2] for nme in small])])
    w_, m_, v_ = weights["w_in"]
    res = _adamw(tr(w_), grads["w_in"], tr(m_), tr(v_), "adamw_w_in")
    upd = {"w_in": tuple(jnp.transpose(a)[None] for a in (grads["w_in"],) + res)}
    for j, nme in enumerate(small):
        upd[nme] = (unview(nme, view(nme, grads[nme])),) + tuple(unview(nme, r[j]) for r in res_small)
    grads = {nme: upd[nme][0] for nme in names}
    deltas, new_m, new_v = ([upd[nme][j] for nme in names] for j in (1, 2, 3))

    grad_x = gx.reshape(nb, s, D_MODEL)
    return (loss, grad_x, *[grads[nme] for nme in names], *deltas, *new_m, *new_v)
```

```python
import functools

import jax
import jax.numpy as jnp
import numpy as np
from jax import lax
from jax.experimental import pallas as pl
from jax.experimental.pallas import tpu as pltpu

F32 = jnp.float32
BF16 = jnp.bfloat16

D_MODEL = 1024
N_META = 16
HEADS = 4
NOPE = 128
ROPE = 64
VDIM = 128
QK_PAD = 256
Q_RANK = 256
KV_RANK = 128
CONV_W = 512
CONV_GROUP = 64
ROPE_THETA = 10000.0
EPS = 1e-6
ATTN_SCALE = (NOPE + ROPE) ** -0.5
IN_DIM = 3008
IN_PAD = 3072
HEAD_ROWS = Q_RANK + KV_RANK + ROPE
IN_HEAD = 512
IN_TAIL = IN_PAD - IN_HEAD
BLK_ZA, BLK_CB, BLK_CC, BLK_CH, BLK_ZC = 0, 1, 2, 3, 4
NEG_INF = -1e30

ADAM_LR = 0.001
ADAM_B1 = 0.9
ADAM_B2 = 0.999
ADAM_EPS = 1e-08
ADAM_WD = 0.01
ADAM_STEP = 10

ROW_TILE = 512
ATTN_TILE = 256
VMEM_LIMIT = 56 * 1024 * 1024

NT = (((1,), (1,)), ((), ()))
TN = (((0,), (0,)), ((), ()))


def _cparams(*sem):
    return pltpu.CompilerParams(dimension_semantics=sem, vmem_limit_bytes=VMEM_LIMIT)


def _dot(a, b):
    return jnp.dot(a, b, preferred_element_type=F32)


def _dot_nt(a, b):
    return lax.dot_general(a, b, NT, preferred_element_type=F32)


def _dot_tn(a, b):
    return lax.dot_general(a, b, TN, preferred_element_type=F32)


def _rms(x, g):
    r = lax.rsqrt(jnp.mean(x * x, axis=-1, keepdims=True) + EPS)
    return x * r * g, r


def _rms_bwd(dy, x, r, g):
    xh = x * r
    dyg = dy * g
    dx = r * (dyg - xh * jnp.mean(dyg * xh, axis=-1, keepdims=True))
    return dx, dy * xh


def _sigmoid(z):
    return 1.0 / (1.0 + jnp.exp(-z))


def _rope(b, c, sa, sb):
    return b * c + pltpu.roll(b, 96, 1) * sa + pltpu.roll(b, 32, 1) * sb


def _rope_bwd(d, c, sa, sb):
    return d * c + pltpu.roll(d * sa, 32, 1) + pltpu.roll(d * sb, 96, 1)


def _group_mean(x, gmat):
    hi = x.astype(BF16)
    lo = (x - hi.astype(F32)).astype(BF16)
    return _dot(hi, gmat) + _dot(lo, gmat)


def _row_of(col, rows):
    return jnp.transpose(jnp.broadcast_to(col, (rows, 128)))[0:1, :]


def _rope_tables(n_pos):
    half = ROPE // 2
    inv_freq = (np.float32(1.0) / (np.float32(ROPE_THETA) ** (np.arange(half, dtype=np.float32) / np.float32(half))))
    ang = np.arange(n_pos, dtype=np.float32)[:, None] * inv_freq.astype(np.float32)[None, :]
    cos, sin = np.cos(ang).astype(np.float32), np.sin(ang).astype(np.float32)
    z = np.zeros((n_pos, half), np.float32)
    c = np.concatenate([cos, cos, z, z], axis=1)
    sa = np.concatenate([-sin, z, z, z], axis=1)
    sb = np.concatenate([z, sin, z, z], axis=1)
    return jnp.asarray(c), jnp.asarray(sa), jnp.asarray(sb)


W_IN_PIECES_1 = ((0, 80, 752, 0, 64, 0), (384, 64, 752, 384, 448, 1), (448, 16, 752, 512, 512, 1))
W_IN_PIECES_2 = ((80, 304, 752, 80, 144, 0), (464, 288, 752, 528, 528, 1))
W_Q_PIECES = ((0, 96, 256, 0, 0, 0), (96, 96, 256, 96, 96, 1))
W_KV_PIECES = ((0, 128, 128, 0, 0, 0), (128, 128, 128, 512, 512, 1))
W_OUT_PIECES = ((0, 128, 256, 0, 0, 0), (128, 128, 256, 128, 128, 1))


class _StagedGather:
    STAGES = 4

    @staticmethod
    def steps(n_steps):
        return (0, 5 * n_steps // 8, 7 * n_steps // 8, n_steps - 1)

    def __init__(self, pieces, zero_rows=None):
        self.pieces = pieces
        self.zero_rows = zero_rows

    def scratch(self):
        nk, dma = len(self.pieces), pltpu.SemaphoreType.DMA
        return [dma((nk, 3)), dma((nk, 3)), dma((nk, 3)), dma((nk, 3)), dma((nk,))]

    def vmem_scratch(self, shard_shape, out_shape):
        return [pltpu.VMEM(shard_shape, BF16), pltpu.VMEM(out_shape, BF16),
                pltpu.SemaphoreType.DMA((4 * len(self.pieces) + 2,))] + self.scratch()

    def run_vmem(self, stage, shard_ref, out_ref, scr):
        src_scr, land_scr, io_sems = scr[:3]
        spans = []
        for _, nr, per, first, rest, _ in self.pieces:
            spans += [(per * q + (first if q == 0 else rest), nr) for q in range(4)]
        if self.zero_rows is not None:
            spans.append(self.zero_rows)
        flush = [pltpu.make_async_copy(land_scr.at[r0:r0 + nr], out_ref.at[r0:r0 + nr], io_sems.at[n])
                 for n, (r0, nr) in enumerate(spans)]
        if stage == 0:
            load = pltpu.make_async_copy(shard_ref, src_scr, io_sems.at[len(spans)])
            load.start()
            if self.zero_rows is not None:
                r0, nr = self.zero_rows
                land_scr[r0:r0 + nr, :] = jnp.zeros((nr, land_scr.shape[1]), BF16)
            load.wait()
        if stage < self.STAGES:
            self.run(stage, src_scr, land_scr, scr[3:])
        for cp in flush:
            if stage == self.STAGES - 1:
                cp.start()
            if stage == self.STAGES:
                cp.wait()

    def run(self, stage, src_ref, out_ref, scr):
        send_sems, recv_sems, fwd_send, fwd_recv, loc_sems = scr
        pieces = self.pieces
        nk = len(pieces)
        x, y, c = lax.axis_index("x"), lax.axis_index("y"), lax.axis_index("c")
        mine = 2 * x + y
        chips = [(1 - x, y), (x, 1 - y), (1 - x, 1 - y)]
        chip_of = [2 * px + py for px, py in chips]
        mesh = pl.DeviceIdType.MESH

        def src(k):
            s0, nr = pieces[k][0], pieces[k][1]
            return src_ref.at[s0:s0 + nr]

        def dst(k, q):
            _, nr, per, first, rest, _ = pieces[k]
            row = per * q + first + (rest - first) * jnp.minimum(q, 1)
            return out_ref.at[pl.ds(pl.multiple_of(row, 16), nr)]

        def ici(k, j, q):
            px, py = chips[j]
            return pltpu.make_async_remote_copy(
                src_ref=src(k), dst_ref=dst(k, q), send_sem=send_sems.at[k, j], recv_sem=recv_sems.at[k, j],
                device_id=(px, py, c), device_id_type=mesh)

        def fwd(k, j):
            ref = dst(k, chip_of[j])
            return pltpu.make_async_remote_copy(
                src_ref=ref, dst_ref=ref, send_sem=fwd_send.at[k, j], recv_sem=fwd_recv.at[k, j],
                device_id=(x, y, 1 - c), device_id_type=mesh)

        def relay(k, half):
            ref = dst(k, chip_of[half])
            px, py = chips[1 - half]
            return pltpu.make_async_remote_copy(
                src_ref=ref, dst_ref=ref, send_sem=send_sems.at[k, 2], recv_sem=recv_sems.at[k, 2],
                device_id=(px, py, c), device_id_type=mesh)

        local = [pltpu.make_async_copy(src(k), dst(k, mine), loc_sems.at[k]) for k in range(nk)]
        if stage == 0:
            for cp in local:
                cp.start()
        if stage == 3:
            for cp in local:
                cp.wait()
        for half in (0, 1):
            @pl.when(c == half)
            def _(half=half):
                my_k = [k for k in range(nk) if pieces[k][5] == half]
                other_k = [k for k in range(nk) if pieces[k][5] != half]
                for k in my_k:
                    if stage == 0:
                        for j in range(2):
                            ici(k, j, mine).start()
                    elif stage == 1:
                        for j in (half, 1 - half):
                            ici(k, j, chip_of[j]).wait_recv()
                            if j == half:
                                relay(k, half).start()
                            fwd(k, j).start()
                    elif stage == 2:
                        ici(k, 2, chip_of[2]).wait_recv()
                        fwd(k, 2).start()
                    else:
                        for j in range(2):
                            ici(k, j, mine).wait_send()
                        relay(k, half).wait_send()
                        for j in range(3):
                            fwd(k, j).wait_send()
                if stage == 3:
                    for k in other_k:
                        for j in range(3):
                            fwd(k, j).wait_recv()


def _fwd_proj(x2d, meta, tabs, tabs_m, norm_g, w_head, q_norm_g, wq_p, kv_norm_g, wkv_p, w_out_shard, w_in_shard,
              nb, s, tm):
    nt = s // tm
    n = nb * nt
    n_steps = n + 1
    c_t, sa_t, sb_t = tabs
    cm_t, sam_t, sbm_t = tabs_m
    gat = _StagedGather(W_OUT_PIECES)
    gat_in = _StagedGather(W_IN_PIECES_1)
    n_sems = len(gat.scratch())
    assert n_steps >= 3

    def body(x_ref, c_ref, sa_ref, sb_ref, mt_ref, cm_ref, sam_ref, sbm_ref,
             g_ref, w_ref, gq_ref, wq_ref, gkv_ref, wkv_ref, wos_ref, wis_ref,
             p_ref, q_ref, k_ref, v_ref, pm_ref, km_ref, vm_ref, wo_ref, wi_ref, *scr):
        gat_scr, gat_in_scr = scr[:n_sems], scr[n_sems:]
        i = pl.program_id(0)
        for stage, at in enumerate(_StagedGather.steps(n_steps)):
            @pl.when(i == at)
            def _(stage=stage):
                gat_in.run_vmem(stage, wis_ref, wi_ref, gat_in_scr)
                gat.run(stage, wos_ref, wo_ref, gat_scr)

        def project(xv, c, sa, sb, p_out, q_out, k_out, v_out):
            u, _ = _rms(xv, g_ref[...])
            p = _dot_nt(u.astype(BF16), w_ref[...])
            p_out[...] = p
            qn, _ = _rms(p[:, 0:Q_RANK], gq_ref[...])
            q = _dot_nt(qn.astype(BF16), wq_ref[...])
            kvn, _ = _rms(p[:, Q_RANK:Q_RANK + KV_RANK], gkv_ref[...])
            kv = _dot_nt(kvn.astype(BF16), wkv_ref[...])
            kpe = _rope(p[:, 384:512], c, sa, sb)
            for h in range(HEADS):
                if q_out is not None:
                    pe = _rope(q[:, QK_PAD * h + NOPE:QK_PAD * (h + 1)], c, sa, sb)
                    qh = jnp.concatenate([q[:, QK_PAD * h:QK_PAD * h + NOPE], pe], axis=1)
                    q_out[0, h] = (qh * ATTN_SCALE).astype(BF16)
                k_out[0, h] = jnp.concatenate([kv[:, NOPE * h:NOPE * (h + 1)], kpe], axis=1).astype(BF16)
                v_out[0, h] = kv[:, 512 + VDIM * h:512 + VDIM * (h + 1)].astype(BF16)

        @pl.when(i < n)
        def _():
            project(x_ref[...], c_ref[...], sa_ref[...], sb_ref[...], p_ref, q_ref, k_ref, v_ref)

        @pl.when(i == n)
        def _():
            project(mt_ref[...], cm_ref[...], sam_ref[...], sbm_ref[...], pm_ref, None, km_ref, vm_ref)
            gat_in.run_vmem(gat_in.STAGES, wis_ref, wi_ref, gat_in_scr)

    cl = lambda i: jnp.minimum(i, n - 1)
    full = lambda a: pl.BlockSpec(a.shape, lambda i: (0,) * a.ndim)
    const = lambda shape: pl.BlockSpec(shape, lambda i: (0,) * len(shape))
    tab = pl.BlockSpec((tm, 128), lambda i: (cl(i) % nt, 0))
    hb = lambda w: pl.BlockSpec((1, HEADS, tm, w), lambda i: (cl(i) // nt, 0, cl(i) % nt, 0))
    whole = pl.BlockSpec(memory_space=pl.ANY)
    return pl.pallas_call(
        body, name="fwd_proj", grid=(n_steps,),
        in_specs=[pl.BlockSpec((tm, D_MODEL), lambda i: (cl(i), 0)), tab, tab, tab,
                  full(meta), full(cm_t), full(sam_t), full(sbm_t),
                  full(norm_g), full(w_head), full(q_norm_g), full(wq_p), full(kv_norm_g), full(wkv_p), whole, whole],
        out_specs=[pl.BlockSpec((tm, IN_HEAD), lambda i: (cl(i), 0)), hb(QK_PAD), hb(QK_PAD), hb(VDIM),
                   const((N_META, IN_HEAD)), const((1, HEADS, N_META, QK_PAD)), const((1, HEADS, N_META, VDIM)),
                   whole, whole],
        out_shape=[jax.ShapeDtypeStruct((nb * s, IN_HEAD), F32),
                   jax.ShapeDtypeStruct((nb, HEADS, s, QK_PAD), BF16),
                   jax.ShapeDtypeStruct((nb, HEADS, s, QK_PAD), BF16),
                   jax.ShapeDtypeStruct((nb, HEADS, s, VDIM), BF16),
                   jax.ShapeDtypeStruct((N_META, IN_HEAD), F32),
                   jax.ShapeDtypeStruct((1, HEADS, N_META, QK_PAD), BF16),
                   jax.ShapeDtypeStruct((1, HEADS, N_META, VDIM), BF16),
                   jax.ShapeDtypeStruct((D_MODEL, D_MODEL), BF16),
                   jax.ShapeDtypeStruct((IN_PAD, D_MODEL), BF16)],
        scratch_shapes=gat.scratch() + gat_in.vmem_scratch(w_in_shard.shape, (IN_PAD, D_MODEL)),
        compiler_params=_cparams("arbitrary"),
    )(x2d, c_t, sa_t, sb_t, meta, cm_t, sam_t, sbm_t, norm_g, w_head, q_norm_g, wq_p, kv_norm_g, wkv_p, w_out_shard,
      w_in_shard)


def _attn_fwd(q, k, v, km, vm, w_in_shard, w_in_part, nb, s, tq):
    nq = s // tq
    n_steps = nb * HEADS
    gat = _StagedGather(W_IN_PIECES_2, zero_rows=(HEAD_ROWS, IN_HEAD - HEAD_ROWS))
    assert n_steps >= 3

    def body(q_ref, k_ref, v_ref, km_ref, vm_ref, ws_ref, _, o_ref, lse_ref, w_ref, s_scr, p_scr, *gat_scr):
        step = pl.program_id(0) * HEADS + pl.program_id(1)
        for stage, at in enumerate(_StagedGather.steps(n_steps)):
            @pl.when(step == at)
            def _(stage=stage):
                gat.run_vmem(stage, ws_ref, w_ref, gat_scr)

        row = lax.broadcasted_iota(jnp.int32, (tq, tq), 0)
        col = lax.broadcasted_iota(jnp.int32, (tq, tq), 1)
        def scores(i):
            slot = i % 2
            qi = q_ref[0, 0, i * tq:(i + 1) * tq, :]
            sm = _dot_nt(qi, km_ref[0, 0])
            m128 = None
            for j in range(i + 1):
                sc = _dot_nt(qi, k_ref[0, 0, j * tq:(j + 1) * tq, :])
                if j == i:
                    sc = jnp.where(col <= row, sc, NEG_INF)
                s_scr[slot, :, j * tq:(j + 1) * tq] = sc
                mx = sc[:, 0:128]
                for c0 in range(128, tq, 128):
                    mx = jnp.maximum(mx, sc[:, c0:c0 + 128])
                m128 = mx if m128 is None else jnp.maximum(m128, mx)
            return sm, jnp.maximum(jnp.max(m128, axis=1, keepdims=True), jnp.max(sm, axis=1, keepdims=True))

        def weighted_sum(i, pm, l):
            n = (i + 1) * tq
            acc = _dot(p_scr[i % 2, :, 0:n], v_ref[0, 0, 0:n, :]) + _dot(pm.astype(BF16), vm_ref[0, 0])
            o_ref[0, 0, i * tq:(i + 1) * tq, :] = acc / l

        nxt, pending = scores(0), None
        for i in range(nq):
            slot = i % 2
            sm, m = nxt
            if i + 1 < nq:
                nxt = scores(i + 1)
            pm = jnp.exp(sm - m)
            l128 = None
            for j in range(i + 1):
                p = jnp.exp(s_scr[slot, :, j * tq:(j + 1) * tq] - m)
                p_scr[slot, :, j * tq:(j + 1) * tq] = p.astype(BF16)
                ps = p[:, 0:128]
                for c0 in range(128, tq, 128):
                    ps = ps + p[:, c0:c0 + 128]
                l128 = ps if l128 is None else l128 + ps
            l = jnp.sum(l128, axis=1, keepdims=True) + jnp.sum(pm, axis=1, keepdims=True)
            lse_ref[0, 0, :, i * tq:(i + 1) * tq] = _row_of(m + jnp.log(l), tq)
            if pending is not None:
                weighted_sum(*pending)
            pending = (i, pm, l)
        weighted_sum(*pending)

        @pl.when(step == n_steps - 1)
        def _():
            gat.run_vmem(gat.STAGES, ws_ref, w_ref, gat_scr)

    hblk = lambda w: pl.BlockSpec((1, 1, s, w), lambda b, h: (b, h, 0, 0))
    mblk = lambda w: pl.BlockSpec((1, 1, N_META, w), lambda b, h: (0, h, 0, 0))
    whole = pl.BlockSpec(memory_space=pl.ANY)
    return pl.pallas_call(
        body, name="attn_fwd", grid=(nb, HEADS),
        in_specs=[hblk(QK_PAD), hblk(QK_PAD), hblk(VDIM), mblk(QK_PAD), mblk(VDIM), whole, whole],
        out_specs=[hblk(VDIM), pl.BlockSpec((1, 1, 1, s), lambda b, h: (b, h, 0, 0)), whole],
        out_shape=[jax.ShapeDtypeStruct((nb, HEADS, s, VDIM), F32),
                   jax.ShapeDtypeStruct((nb, HEADS, 1, s), F32),
                   jax.ShapeDtypeStruct(w_in_part.shape, BF16)],
        input_output_aliases={6: 2},
        scratch_shapes=[pltpu.VMEM((2, tq, s), F32), pltpu.VMEM((2, tq, s), BF16)]
        + gat.vmem_scratch(w_in_shard.shape, w_in_part.shape),
        compiler_params=_cparams("arbitrary", "arbitrary"),
    )(q, k, v, km, vm, w_in_shard, w_in_part)


def _shift_rows(a, prev, n_rows):
    rid = lax.broadcasted_iota(jnp.int32, a.shape, 0)
    a1 = jnp.where(rid == 0, prev[7:8, :], pltpu.roll(a, 1, 0))
    a2 = jnp.where(rid == 0, prev[6:7, :], jnp.where(rid == 1, prev[7:8, :], pltpu.roll(a, 2, 0)))
    return a1, a2


def _attn_gate(o, za, ga_h):
    on, r = _rms(o, ga_h)
    return on * (za * _sigmoid(za)), on, r


def _out_fwd_bwd(x2d, tgt2d, o, meta, norm_g, w_in_p, conv_w, ga, gc, gmat, w_out, gf, nb, s, tm):
    nt = s // tm
    r = nb * s

    def body(x_ref, t_ref, o_ref, mt_ref, g_ref, wi_ref, cw_ref, ga_ref, gc_ref, gm_ref, w_ref, gf_ref,
             dh_ref, dy_ref, dw_ref, dgf_ref, loss_ref, p_ref, pm_ref, last_cc):
        i = pl.program_id(0)
        blk = lambda ref, j, rows=slice(None): ref[rows, 512 * j:512 * (j + 1)]

        def tail(xv):
            u, _ = _rms(xv, g_ref[...])
            return _dot_nt(u.astype(BF16), wi_ref[IN_HEAD:IN_PAD, :])

        @pl.when(i == 0)
        def _():
            dw_ref[...] = jnp.zeros_like(dw_ref)
            dgf_ref[...] = jnp.zeros_like(dgf_ref)
            loss_ref[...] = jnp.zeros_like(loss_ref)
            last_cc[...] = jnp.zeros_like(last_cc)
            pm_ref[...] = tail(mt_ref[...])

        u16 = _rms(x_ref[...], g_ref[...])[0].astype(BF16)

        def project(j):
            p_ref[:, 512 * j:512 * (j + 1)] = _dot_nt(u16, wi_ref[IN_HEAD + 512 * j:IN_HEAD + 512 * (j + 1), :])

        project(BLK_ZA)
        project(BLK_CC)
        project(BLK_CH)
        ya = []
        for h in range(HEADS):
            y, _, _ = _attn_gate(o_ref[0, h], p_ref[:, 512 * BLK_ZA + VDIM * h:512 * BLK_ZA + VDIM * (h + 1)],
                                 ga_ref[:, VDIM * h:VDIM * (h + 1)])
            ya.append(y)
        project(BLK_CB)
        project(BLK_ZC)
        cc = blk(p_ref, BLK_CC) * blk(p_ref, BLK_CH)
        meta_cc = blk(pm_ref, BLK_CC, slice(8, 16)) * blk(pm_ref, BLK_CH, slice(8, 16))
        prev = jnp.where(i % nt == 0, meta_cc, last_cc[...])
        last_cc[...] = cc[tm - 8:tm, :]
        cc1, cc2 = _shift_rows(cc, prev, tm)
        yc = blk(p_ref, BLK_CB) * (cw_ref[0:1, :] * cc2 + cw_ref[1:2, :] * cc1 + cw_ref[2:3, :] * cc)
        rg = lax.rsqrt(_group_mean(yc * yc, gm_ref[...]) + EPS)
        zc = blk(p_ref, BLK_ZC)
        yconv = yc * rg * gc_ref[...] * (zc * _sigmoid(zc))
        ycat = jnp.concatenate(ya + [yconv], axis=1).astype(BF16)
        h2 = x_ref[...] + _dot(ycat, w_ref[...])
        gfv = gf_ref[...]
        y, r2 = _rms(h2, gfv)
        e = y - t_ref[...]
        loss_ref[...] += 0.5 * jnp.sum(e * e) / D_MODEL
        dyv = e * (1.0 / D_MODEL)
        dh2, dgf = _rms_bwd(dyv, h2, r2, gfv)
        dgf_ref[...] += jnp.sum(dgf, axis=0, keepdims=True)
        dh_ref[...] = dh2
        dhb = dh2.astype(BF16)
        dy_ref[...] = _dot_nt(dhb, w_ref[...])
        dw_ref[...] += _dot_tn(ycat, dhb)

    row = lambda w: pl.BlockSpec((tm, w), lambda i: (i, 0))
    const = lambda shape: pl.BlockSpec(shape, lambda i: (0,) * len(shape))
    full = lambda a: const(a.shape)
    return pl.pallas_call(
        body, name="out_fwd_bwd", grid=(nb * nt,),
        in_specs=[row(D_MODEL), row(D_MODEL),
                  pl.BlockSpec((1, HEADS, tm, VDIM), lambda i: (i // nt, 0, i % nt, 0)),
                  full(meta), full(norm_g), full(w_in_p),
                  full(conv_w), full(ga), full(gc), full(gmat), full(w_out), full(gf)],
        out_specs=[row(D_MODEL), row(D_MODEL), const((D_MODEL, D_MODEL)), const((1, D_MODEL)), const((1, 128)),
                   row(IN_TAIL), const((N_META, IN_TAIL))],
        out_shape=[jax.ShapeDtypeStruct((r, D_MODEL), F32), jax.ShapeDtypeStruct((r, D_MODEL), F32),
                   jax.ShapeDtypeStruct((D_MODEL, D_MODEL), F32), jax.ShapeDtypeStruct((1, D_MODEL), F32),
                   jax.ShapeDtypeStruct((1, 128), F32),
                   jax.ShapeDtypeStruct((r, IN_TAIL), F32), jax.ShapeDtypeStruct((N_META, IN_TAIL), F32)],
        scratch_shapes=[pltpu.VMEM((8, 512), F32)],
        compiler_params=_cparams("arbitrary"),
    )(x2d, tgt2d, o, meta, norm_g, w_in_p, conv_w, ga, gc, gmat, w_out, gf)


def _gate_bwd(dycat, o, p, pm, conv_w, ga, gc, gmat, nb, s, tm):
    nt = s // tm
    r = nb * s
    ext = tm + 8
    prev_idx = lambda i: jnp.maximum(i * (tm // 8) - 1, 0)
    next_idx = lambda i: jnp.minimum((i + 1) * (tm // 8), r // 8 - 1)

    def body(dya_ref, dyc_ref, dycn_ref, o_ref, za_ref, cb_ref, cbn_ref, cc_ref, ccp_ref, ccn_ref,
             ch_ref, chp_ref, chn_ref, zc_ref, zcn_ref, mc_ref, mh_ref, cw_ref, ga_ref, gc_ref, gm_ref,
             dpb_ref, do_ref, dl_ref, dccm_ref, dga_ref, dgc_ref, dcw_ref):
        i = pl.program_id(0)

        @pl.when(i == 0)
        def _():
            dga_ref[...] = jnp.zeros_like(dga_ref)
            dgc_ref[...] = jnp.zeros_like(dgc_ref)
            dcw_ref[...] = jnp.zeros_like(dcw_ref)

        dga = []
        for h in range(HEADS):
            hs = slice(VDIM * h, VDIM * (h + 1))
            oh, za, gah, dya = o_ref[0, h], za_ref[:, hs], ga_ref[:, hs], dya_ref[:, hs]
            sg = _sigmoid(za)
            on, ro = _rms(oh, gah)
            don = dya * (za * sg)
            dpb_ref[:, hs] = (dya * on * (sg * (1.0 + za * (1.0 - sg)))).astype(BF16)
            do, dg = _rms_bwd(don, oh, ro, gah)
            dga.append(jnp.sum(dg, axis=0, keepdims=True))
            dob = do.astype(BF16)
            do_ref[0, h] = dob
            dl_ref[0, h] = _row_of(jnp.sum(dob.astype(F32) * oh, axis=1, keepdims=True), tm)
        dga_ref[...] += jnp.concatenate(dga, axis=1)

        cat = lambda a, b: jnp.concatenate([a[...], b[...]], axis=0)
        cch = cat(cc_ref, ccn_ref)
        chh = cat(ch_ref, chn_ref)
        cb = cat(cb_ref, cbn_ref)
        zc = cat(zc_ref, zcn_ref)
        dy = cat(dyc_ref, dycn_ref)
        first = i % nt == 0
        last = i % nt == nt - 1
        cc = cch * chh
        prev = jnp.where(first, mc_ref[8:16, :] * mh_ref[8:16, :], ccp_ref[...] * chp_ref[...])
        cc1, cc2 = _shift_rows(cc, prev, ext)
        w0, w1, w2 = cw_ref[0:1, :], cw_ref[1:2, :], cw_ref[2:3, :]
        dw = w0 * cc2 + w1 * cc1 + w2 * cc
        yc = cb * dw
        rg = lax.rsqrt(_group_mean(yc * yc, gm_ref[...]) + EPS)
        ych = yc * rg
        gcv = gc_ref[...]
        sg = _sigmoid(zc)
        dycn = dy * (zc * sg)
        dzc = dy * (ych * gcv) * (sg * (1.0 + zc * (1.0 - sg)))
        dgc_ref[...] += jnp.sum((dycn * ych)[:tm], axis=0, keepdims=True)
        dycg = dycn * gcv
        dyc = rg * (dycg - ych * _group_mean(dycg * ych, gm_ref[...]))
        rid = lax.broadcasted_iota(jnp.int32, (ext, CONV_W), 0)
        ddw = jnp.where(jnp.logical_and(last, rid >= tm), 0.0, dyc * cb)
        dcb = dyc * dw
        dcc = w2 * ddw + w1 * pltpu.roll(ddw, ext - 1, 0) + w0 * pltpu.roll(ddw, ext - 2, 0)
        dpb_ref[:, 512:1024] = dcb[:tm].astype(BF16)
        dpb_ref[:, 1024:1536] = (dcc * chh)[:tm].astype(BF16)
        dpb_ref[:, 1536:2048] = (dcc * cch)[:tm].astype(BF16)
        dpb_ref[:, 2048:2560] = dzc[:tm].astype(BF16)
        rs = lambda a: jnp.sum(a[:tm], axis=0, keepdims=True)
        dcw_ref[0:1, :] += rs(ddw * cc2)
        dcw_ref[1:2, :] += rs(ddw * cc1)
        dcw_ref[2:3, :] += rs(ddw * cc)

        @pl.when(first)
        def _():
            d0, d1 = ddw[0:1, :], ddw[1:2, :]
            r8 = lax.broadcasted_iota(jnp.int32, (8, CONV_W), 0)
            dccm_ref[0] = jnp.where(r8 == 7, w1 * d0 + w0 * d1, jnp.where(r8 == 6, w0 * d0, 0.0))

    row = lambda j: pl.BlockSpec((tm, 512), lambda i: (i, j))
    prv = lambda j: pl.BlockSpec((8, 512), lambda i: (prev_idx(i), j))
    nxt = lambda j: pl.BlockSpec((8, 512), lambda i: (next_idx(i), j))
    mblk = lambda j: pl.BlockSpec((N_META, 512), lambda i: (0, j))
    full = lambda a: pl.BlockSpec(a.shape, lambda i: (0,) * a.ndim)
    hb = lambda w: pl.BlockSpec((1, HEADS, tm, w), lambda i: (i // nt, 0, i % nt, 0))
    acc = lambda rr: pl.BlockSpec((rr, 512), lambda i: (0, 0))
    return pl.pallas_call(
        body, name="gate_bwd", grid=(nb * nt,),
        in_specs=[row(0), row(1), nxt(1), hb(VDIM),
                  row(BLK_ZA), row(BLK_CB), nxt(BLK_CB), row(BLK_CC), prv(BLK_CC), nxt(BLK_CC),
                  row(BLK_CH), prv(BLK_CH), nxt(BLK_CH), row(BLK_ZC), nxt(BLK_ZC),
                  mblk(BLK_CC), mblk(BLK_CH), full(conv_w), full(ga), full(gc), full(gmat)],
        out_specs=[pl.BlockSpec((tm, 2560), lambda i: (i, 0)), hb(VDIM),
                   pl.BlockSpec((1, HEADS, 1, tm), lambda i: (i // nt, 0, 0, i % nt)),
                   pl.BlockSpec((1, 8, 512), lambda i: (i // nt, 0, 0)),
                   acc(1), acc(1), acc(8)],
        out_shape=[jax.ShapeDtypeStruct((r, 2560), BF16), jax.ShapeDtypeStruct((nb, HEADS, s, VDIM), BF16),
                   jax.ShapeDtypeStruct((nb, HEADS, 1, s), F32), jax.ShapeDtypeStruct((nb, 8, 512), F32),
                   jax.ShapeDtypeStruct((1, 512), F32), jax.ShapeDtypeStruct((1, 512), F32),
                   jax.ShapeDtypeStruct((8, 512), F32)],
        compiler_params=_cparams("arbitrary"),
    )(dycat, dycat, dycat, o, p, p, p, p, p, p, p, p, p, p, p, pm, pm, conv_w, ga, gc, gmat)


class _StagedReduce:
    LOC, PRE_S, PRE_R, ICI_S, ICI_R, POST_S, POST_R, OUT, N_SEM = 0, 1, 2, 3, 6, 9, 10, 11, 12

    def __init__(self, shard_shape):
        self.half = (shard_shape[0] // 2, shard_shape[1])

    def scratch(self):
        h = self.half
        return [pltpu.VMEM((4,) + h, F32), pltpu.VMEM((4,) + h, F32), pltpu.VMEM((4,) + h, BF16),
                pltpu.VMEM((3,) + h, BF16), pltpu.VMEM(h, F32), pltpu.SemaphoreType.DMA((self.N_SEM,))]

    def run(self, stage, pin, gout, scr):
        own, sib, wire, rbuf, fin, sems = scr
        r2 = self.half[0]
        x, y, c = lax.axis_index("x"), lax.axis_index("y"), lax.axis_index("c")
        mine = 2 * x + y
        sibling = (x, y, 1 - c)
        chips = [(1 - x, y), (x, 1 - y), (1 - x, 1 - y)]
        rows = lambda half: pl.ds(pl.multiple_of(half * r2, r2), r2)
        mesh = pl.DeviceIdType.MESH

        loc = pltpu.make_async_copy(pin.at[:, rows(c), :], own, sems.at[self.LOC])
        pre = pltpu.make_async_remote_copy(
            src_ref=pin.at[:, rows(1 - c), :], dst_ref=sib, send_sem=sems.at[self.PRE_S],
            recv_sem=sems.at[self.PRE_R], device_id=sibling, device_id_type=mesh)

        def ici(j):
            px, py = chips[j]
            return pltpu.make_async_remote_copy(
                src_ref=wire.at[2 * px + py], dst_ref=rbuf.at[j], send_sem=sems.at[self.ICI_S + j],
                recv_sem=sems.at[self.ICI_R + j], device_id=(px, py, c), device_id_type=mesh)

        def post(half):
            return pltpu.make_async_remote_copy(
                src_ref=fin, dst_ref=gout.at[rows(half), :], send_sem=sems.at[self.POST_S],
                recv_sem=sems.at[self.POST_R], device_id=sibling, device_id_type=mesh)

        keep = pltpu.make_async_copy(fin, gout.at[rows(c), :], sems.at[self.OUT])
        if stage == 0:
            loc.start()
            pre.start()
        elif stage == 1:
            loc.wait()
            pre.wait_recv()
            for blk in range(4):
                tot = own[blk] + sib[blk]
                own[blk] = tot
                wire[blk] = tot.astype(BF16)
            for j in range(3):
                ici(j).start()
        elif stage == 2:
            for j in range(3):
                ici(j).wait_recv()
            tot = own[mine]
            for j in range(3):
                tot = tot + rbuf[j].astype(F32)
            fin[...] = tot
            post(c).start()
            keep.start()
        else:
            post(1 - c).wait_recv()
            pre.wait_send()
            for j in range(3):
                ici(j).wait_send()
            post(c).wait_send()
            keep.wait()


def _attn_bwd(q, k, v, do, lse, delta, km, vm, early, nb, s, t):
    n = s // t
    ne = len(early)
    reds = [_StagedReduce(a.shape[1:]) for a in early]
    n_steps = HEADS * nb
    assert n_steps >= 4

    def body(q_ref, k_ref, v_ref, do_ref, lse_ref, dl_ref, km_ref, vm_ref, *rest):
        pin_refs, rest = rest[:ne], rest[ne:]
        dq_ref, dk_ref, dv_ref, dkm_ref, dvm_ref = rest[:5]
        gout_refs, (p_scr, ds_scr, dq_acc), red_scr = rest[5:5 + ne], rest[5 + ne:8 + ne], rest[8 + ne:]
        b = pl.program_id(1)
        step = pl.program_id(0) * nb + b
        for stage, at in enumerate((0, 1, n_steps - 2, n_steps - 1)):
            @pl.when(step == at)
            def _(stage=stage):
                for a, red in enumerate(reds):
                    red.run(stage, pin_refs[a], gout_refs[a], red_scr[6 * a:6 * a + 6])

        @pl.when(b == 0)
        def _():
            dkm_ref[...] = jnp.zeros_like(dkm_ref)
            dvm_ref[...] = jnp.zeros_like(dvm_ref)

        kr = lax.broadcasted_iota(jnp.int32, (t, t), 0)
        qc = lax.broadcasted_iota(jnp.int32, (t, t), 1)
        km_v, vm_v = km_ref[0, 0], vm_ref[0, 0]
        ptm = jnp.exp(_dot_nt(km_v, q_ref[0, 0]) - lse_ref[0, 0])
        dstm = (ptm * (_dot_nt(vm_v, do_ref[0, 0]) - dl_ref[0, 0])).astype(BF16)
        dkm_ref[0] += _dot(dstm, q_ref[0, 0])
        dvm_ref[0] += _dot(ptm.astype(BF16), do_ref[0, 0])
        dq_acc[...] = _dot_tn(dstm, km_v)
        def tiles(j):
            slot = j % 2
            kj = k_ref[0, 0, j * t:(j + 1) * t, :]
            vj = v_ref[0, 0, j * t:(j + 1) * t, :]
            def products(i):
                cs = slice(i * t, (i + 1) * t)
                return _dot_nt(kj, q_ref[0, 0, cs, :]), _dot_nt(vj, do_ref[0, 0, cs, :])

            nxt, pending = products(j), None
            for i in range(j, n):
                cs = slice(i * t, (i + 1) * t)
                st, dpt = nxt
                if i + 1 < n:
                    nxt = products(i + 1)
                if i == j:
                    st = jnp.where(kr <= qc, st, NEG_INF)
                pt = jnp.exp(st - lse_ref[0, 0, :, cs])
                dst = (pt * (dpt - dl_ref[0, 0, :, cs])).astype(BF16)
                p_scr[slot, :, cs] = pt.astype(BF16)
                ds_scr[slot, :, cs] = dst
                if pending is not None:
                    dq_acc[pending[0], :] += _dot_tn(pending[1], kj)
                pending = (cs, dst)
            dq_acc[pending[0], :] += _dot_tn(pending[1], kj)

        for j in range(n):
            slot = j % 2
            tiles(j)
            dv_ref[0, 0, j * t:(j + 1) * t, :] = _dot(p_scr[slot, :, j * t:s], do_ref[0, 0, j * t:s, :]).astype(BF16)
            dk_ref[0, 0, j * t:(j + 1) * t, :] = _dot(ds_scr[slot, :, j * t:s], q_ref[0, 0, j * t:s, :]).astype(BF16)
        dq_ref[0, 0] = dq_acc[...].astype(BF16)

    big = lambda w: pl.BlockSpec((1, 1, s, w), lambda h, b: (b, h, 0, 0))
    rowv = pl.BlockSpec((1, 1, 1, s), lambda h, b: (b, h, 0, 0))
    mk = lambda w: pl.BlockSpec((1, 1, N_META, w), lambda h, b: (0, h, 0, 0))
    mo = lambda w: pl.BlockSpec((1, N_META, w), lambda h, b: (h, 0, 0))
    return pl.pallas_call(
        body, name="attn_bwd", grid=(HEADS, nb),
        in_specs=[big(QK_PAD), big(QK_PAD), big(VDIM), big(VDIM), rowv, rowv, mk(QK_PAD), mk(VDIM)]
        + [pl.BlockSpec(memory_space=pl.ANY)] * ne,
        out_specs=[big(QK_PAD), big(QK_PAD), big(VDIM), mo(QK_PAD), mo(VDIM)]
        + [pl.BlockSpec(memory_space=pl.ANY)] * ne,
        out_shape=[jax.ShapeDtypeStruct((nb, HEADS, s, QK_PAD), BF16),
                   jax.ShapeDtypeStruct((nb, HEADS, s, QK_PAD), BF16),
                   jax.ShapeDtypeStruct((nb, HEADS, s, VDIM), BF16),
                   jax.ShapeDtypeStruct((HEADS, N_META, QK_PAD), F32),
                   jax.ShapeDtypeStruct((HEADS, N_META, VDIM), F32)]
        + [jax.ShapeDtypeStruct(a.shape[1:], F32) for a in early],
        scratch_shapes=[pltpu.VMEM((2, t, s), BF16), pltpu.VMEM((2, t, s), BF16), pltpu.VMEM((s, QK_PAD), F32)]
        + [sc for red in reds for sc in red.scratch()],
        compiler_params=_cparams("arbitrary", "arbitrary"),
    )(q, k, v, do, lse, delta, km, vm, *early)


def _up_bwd(dq, dk, dv, dkm, dvm, p, pm, tabs, tabs_m, wq_p, wkv_p, gq, gkv, nb, s, tm):
    nt = s // tm
    n = nb * nt
    c_t, sa_t, sb_t = tabs
    cm_t, sam_t, sbm_t = tabs_m

    def kv_path(dkh, dvh, pa, c, sa, sb, wkv, gkvv):
        dkpe = dkh[0][:, NOPE:]
        for h in range(1, HEADS):
            dkpe = dkpe + dkh[h][:, NOPE:]
        dkr = _rope_bwd(dkpe, c, sa, sb)
        dkv = jnp.concatenate([d[:, :NOPE] for d in dkh] + list(dvh), axis=1).astype(BF16)
        ckv = pa[:, Q_RANK:Q_RANK + KV_RANK]
        kvn, rkv = _rms(ckv, gkvv)
        dckv, dg = _rms_bwd(_dot(dkv, wkv), ckv, rkv, gkvv)
        return dckv, dkr, kvn.astype(BF16), dkv, jnp.sum(dg, axis=0, keepdims=True)

    def body(dq_ref, dk_ref, dv_ref, pa_ref, c_ref, sa_ref, sb_ref,
             dkm_ref, dvm_ref, pam_ref, cm_ref, sam_ref, sbm_ref,
             wq_ref, wkv_ref, gq_ref, gkv_ref,
             dpa_ref, dpam_ref, pq_ref, pkv_ref, dgq_ref, dgkv_ref, dwq_ref, dwkv_ref):
        i = pl.program_id(0)

        @pl.when(i == 0)
        def _():
            dwq_ref[...] = jnp.zeros_like(dwq_ref)
            dwkv_ref[...] = jnp.zeros_like(dwkv_ref)
            dgq_ref[...] = jnp.zeros_like(dgq_ref)
            dgkv_ref[...] = jnp.zeros_like(dgkv_ref)

        @pl.when(i < n)
        def _():
            c, sa, sb = c_ref[...], sa_ref[...], sb_ref[...]
            pa = pa_ref[...]
            parts = []
            for h in range(HEADS):
                dqh = dq_ref[0, h].astype(F32) * ATTN_SCALE
                parts += [dqh[:, :NOPE], _rope_bwd(dqh[:, NOPE:], c, sa, sb)]
            dql = jnp.concatenate(parts, axis=1).astype(BF16)
            cq = pa[:, 0:Q_RANK]
            gqv = gq_ref[...]
            qn, rq = _rms(cq, gqv)
            dwq_ref[...] += _dot_tn(dql, qn.astype(BF16))
            dcq, dg = _rms_bwd(_dot(dql, wq_ref[...]), cq, rq, gqv)
            dgq_ref[...] += jnp.sum(dg, axis=0, keepdims=True)
            dckv, dkr, kvn, dkv, dgk = kv_path([dk_ref[0, h].astype(F32) for h in range(HEADS)],
                                               [dv_ref[0, h].astype(F32) for h in range(HEADS)],
                                               pa, c, sa, sb, wkv_ref[...], gkv_ref[...])
            dwkv_ref[...] += _dot_tn(dkv, kvn)
            dgkv_ref[...] += dgk
            dpa_ref[...] = jnp.concatenate([dcq, dckv, dkr], axis=1).astype(BF16)

        @pl.when(i == n)
        def _():
            dckv, dkr, kvn, dkv, dgk = kv_path([dkm_ref[h] for h in range(HEADS)],
                                               [dvm_ref[h] for h in range(HEADS)],
                                               pam_ref[...], cm_ref[...], sam_ref[...], sbm_ref[...],
                                               wkv_ref[...], gkv_ref[...])
            dwkv_ref[...] += _dot_tn(dkv, kvn)
            dgkv_ref[...] += dgk
            dpam_ref[...] = jnp.concatenate([jnp.zeros((N_META, Q_RANK), F32), dckv, dkr], axis=1)
            for h in range(HEADS):
                pq_ref[h] = dwq_ref[QK_PAD * h:QK_PAD * h + NOPE + ROPE, :]
                pkv_ref[h, 0:NOPE, :] = dwkv_ref[NOPE * h:NOPE * (h + 1), :]
                pkv_ref[h, NOPE:NOPE + VDIM, :] = dwkv_ref[512 + VDIM * h:512 + VDIM * (h + 1), :]

    cl = lambda i: jnp.minimum(i, n - 1)
    hb = lambda w: pl.BlockSpec((1, HEADS, tm, w), lambda i: (cl(i) // nt, 0, cl(i) % nt, 0))
    tab = pl.BlockSpec((tm, 128), lambda i: (cl(i) % nt, 0))
    full = lambda a: pl.BlockSpec(a.shape, lambda i: (0,) * a.ndim)
    const = lambda shape: pl.BlockSpec(shape, lambda i: (0,) * len(shape))
    return pl.pallas_call(
        body, name="up_bwd", grid=(n + 1,),
        in_specs=[hb(QK_PAD), hb(QK_PAD), hb(VDIM), pl.BlockSpec((tm, 512), lambda i: (cl(i), 0)), tab, tab, tab,
                  full(dkm), full(dvm), pl.BlockSpec((N_META, 512), lambda i: (0, 0)),
                  full(cm_t), full(sam_t), full(sbm_t), full(wq_p), full(wkv_p), full(gq), full(gkv)],
        out_specs=[pl.BlockSpec((tm, 512), lambda i: (cl(i), 0)), const((N_META, 512)),
                   const((HEADS, NOPE + ROPE, Q_RANK)), const((HEADS, NOPE + VDIM, KV_RANK)),
                   const((1, Q_RANK)), const((1, KV_RANK))],
        out_shape=[jax.ShapeDtypeStruct((nb * s, 512), BF16), jax.ShapeDtypeStruct((N_META, 512), F32),
                   jax.ShapeDtypeStruct((HEADS, NOPE + ROPE, Q_RANK), F32),
                   jax.ShapeDtypeStruct((HEADS, NOPE + VDIM, KV_RANK), F32),
                   jax.ShapeDtypeStruct((1, Q_RANK), F32), jax.ShapeDtypeStruct((1, KV_RANK), F32)],
        scratch_shapes=[pltpu.VMEM((HEADS * QK_PAD, Q_RANK), F32), pltpu.VMEM((1024, KV_RANK), F32)],
        compiler_params=_cparams("arbitrary"),
    )(dq, dk, dv, p, c_t, sa_t, sb_t, dkm, dvm, pm, cm_t, sam_t, sbm_t, wq_p, wkv_p, gq, gkv)


def _in_bwd(x2d, dh2, dpa, dpb, meta, dpam, dccm, pm, w_in_p, norm_g, nb, s, tm):
    nt = s // tm
    n = nb * nt

    def body(x_ref, dh_ref, dpa_ref, dpb_ref, mt_ref, dpam_ref, dccm_ref, mc_ref, mh_ref, w_ref, g_ref,
             gx_ref, gm_ref, dw_hbm, dg_ref, acc_ref, sems):
        i = pl.program_id(0)

        @pl.when(i == 0)
        def _():
            acc_ref[...] = jnp.zeros_like(acc_ref)
            dg_ref[...] = jnp.zeros_like(dg_ref)

        def rows(x, dp, dres):
            g = g_ref[...]
            dpb16 = dp.astype(BF16)
            du = _dot(dpb16, w_ref[...])
            u, r1 = _rms(x, g)
            acc_ref[...] += _dot_tn(dpb16, u.astype(BF16))
            dx, dg = _rms_bwd(du, x, r1, g)
            dg_ref[...] += jnp.sum(dg, axis=0, keepdims=True)
            return dx if dres is None else dx + dres

        @pl.when(i < n)
        def _():
            dp = jnp.concatenate([dpa_ref[...], dpb_ref[...]], axis=1)
            gx_ref[...] = rows(x_ref[...], dp, dh_ref[...])

        @pl.when(i == n)
        def _():
            dcc = dccm_ref[0]
            for b in range(1, nb):
                dcc = dcc + dccm_ref[b]
            z8 = jnp.zeros((8, CONV_W), F32)
            dc = jnp.concatenate([z8, dcc * mh_ref[8:16, :]], axis=0)
            dh = jnp.concatenate([z8, dcc * mc_ref[8:16, :]], axis=0)
            z = jnp.zeros((N_META, CONV_W), F32)
            dp = jnp.concatenate([dpam_ref[...], z, z, dc, dh, z], axis=1)
            gm_ref[...] = rows(mt_ref[...], dp, None)
            per = IN_DIM // 4
            cps = [pltpu.make_async_copy(acc_ref.at[0:448], dw_hbm.at[0, 0:448], sems.at[0]),
                   pltpu.make_async_copy(acc_ref.at[512:per + 64], dw_hbm.at[0, 448:per], sems.at[1])]
            for qq in range(1, 4):
                cps.append(pltpu.make_async_copy(acc_ref.at[per * qq + 64:per * (qq + 1) + 64], dw_hbm.at[qq],
                                                 sems.at[qq + 1]))
            for cp in cps:
                cp.start()
            for cp in cps:
                cp.wait()

    cl = lambda i: jnp.minimum(i, n - 1)
    row = lambda w: pl.BlockSpec((tm, w), lambda i: (cl(i), 0))
    full = lambda a: pl.BlockSpec(a.shape, lambda i: (0,) * a.ndim)
    mblk = lambda j: pl.BlockSpec((N_META, 512), lambda i: (0, j))
    return pl.pallas_call(
        body, name="in_bwd", grid=(n + 1,),
        in_specs=[row(D_MODEL), row(D_MODEL), row(512), row(2560), full(meta), full(dpam), full(dccm),
                  mblk(BLK_CC), mblk(BLK_CH), full(w_in_p), full(norm_g)],
        out_specs=[row(D_MODEL), pl.BlockSpec((N_META, D_MODEL), lambda i: (0, 0)),
                   pl.BlockSpec(memory_space=pl.ANY), pl.BlockSpec((1, D_MODEL), lambda i: (0, 0))],
        out_shape=[jax.ShapeDtypeStruct((nb * s, D_MODEL), F32), jax.ShapeDtypeStruct((N_META, D_MODEL), F32),
                   jax.ShapeDtypeStruct((4, IN_DIM // 4, D_MODEL), F32), jax.ShapeDtypeStruct((1, D_MODEL), F32)],
        scratch_shapes=[pltpu.VMEM((IN_PAD, D_MODEL), F32), pltpu.SemaphoreType.DMA((5,))],
        compiler_params=_cparams("arbitrary"),
    )(x2d, dh2, dpa, dpb, meta, dpam, dccm, pm, pm, w_in_p, norm_g)


def _gather_weights(w_in_shard, w_out_shard, split, pieces, out_rows, whole, zero_fills):
    ns, nw, nz = len(split), len(whole), len(zero_fills)
    flat = [(a, pc) for a in range(ns) for pc in pieces[a]]
    nk = len(flat)
    hh = HEAD_ROWS // 2
    assert hh % 16 == 0

    def body(*refs):
        ins, wins, zins = refs[2:2 + ns], refs[2 + ns:2 + ns + nw], refs[2 + ns + nw:2 + ns + nw + nz]
        n_in = 2 + ns + nw + nz
        head_ref, shard16, w_out16 = refs[n_in:n_in + 3]
        outs, wcat = refs[n_in + 3:n_in + 3 + ns], refs[n_in + 3 + ns:n_in + 3 + ns + nw]
        scr = refs[n_in + 3 + ns + nw:]
        stage, wouts = scr[:ns], scr[ns:ns + nw]
        (send_sems, recv_sems, fwd_send, fwd_recv, loc_sems, w_send, w_recv, w_loc, z_sems,
         h_send, h_recv, h_relay, h_pass) = scr[ns + nw:]
        x, y, c = lax.axis_index("x"), lax.axis_index("y"), lax.axis_index("c")
        mine = 2 * x + y
        chips = [(1 - x, y), (x, 1 - y), (1 - x, 1 - y)]
        chip_of = [2 * px + py for px, py in chips]
        shard16[...] = refs[0][...].astype(BF16)
        w_out16[...] = refs[1][...].astype(BF16)
        for a in range(ns):
            stage[a][...] = ins[a][...].astype(BF16)

        def head_rows(half):
            return pl.ds(pl.multiple_of(half * hh, 16), hh)

        def head_copy(turn):
            dest = (1 - c, c, c) if turn == 0 else (c, 1 - c, c)
            return pltpu.make_async_remote_copy(
                src_ref=shard16.at[head_rows(c)], dst_ref=head_ref.at[head_rows(c)], send_sem=h_send.at[turn],
                recv_sem=h_recv.at[0], device_id=dest, device_id_type=pl.DeviceIdType.MESH)

        def head_relay():
            ref = head_ref.at[head_rows(c)]
            return pltpu.make_async_remote_copy(
                src_ref=ref, dst_ref=ref, send_sem=h_relay.at[0], recv_sem=h_recv.at[0],
                device_id=(1, 1, c), device_id_type=pl.DeviceIdType.MESH)

        def head_pass(half):
            ref = head_ref.at[head_rows(half)]
            return pltpu.make_async_remote_copy(
                src_ref=ref, dst_ref=ref, send_sem=h_pass.at[0], recv_sem=h_pass.at[1],
                device_id=(x, y, 1 - c), device_id_type=pl.DeviceIdType.MESH)

        head_ref[HEAD_ROWS:IN_HEAD, :] = jnp.zeros((IN_HEAD - HEAD_ROWS, D_MODEL), BF16)

        @pl.when(mine == 0)
        def _():
            head_copy(0).start()
            head_ref[0:HEAD_ROWS, :] = shard16[0:HEAD_ROWS, :]

        def src(k):
            a, (s0, nr, _, _, _, _) = flat[k]
            return stage[a].at[s0:s0 + nr]

        def dst(k, q):
            a, (_, nr, per, first, rest, _) = flat[k]
            row = per * q + first + (rest - first) * jnp.minimum(q, 1)
            return outs[a].at[pl.ds(pl.multiple_of(row, 16), nr)]

        def ici(k, j, q):
            px, py = chips[j]
            return pltpu.make_async_remote_copy(
                src_ref=src(k), dst_ref=dst(k, q), send_sem=send_sems.at[k, j], recv_sem=recv_sems.at[k, j],
                device_id=(px, py, c), device_id_type=pl.DeviceIdType.MESH)

        def fwd(k, j):
            ref = dst(k, chip_of[j])
            return pltpu.make_async_remote_copy(
                src_ref=ref, dst_ref=ref, send_sem=fwd_send.at[k, j], recv_sem=fwd_recv.at[k, j],
                device_id=(x, y, 1 - c), device_id_type=pl.DeviceIdType.MESH)

        def wcopy(b, j, q):
            px, py = chips[j]
            return pltpu.make_async_remote_copy(
                src_ref=wins[b], dst_ref=wouts[b].at[q], send_sem=w_send.at[b, j], recv_sem=w_recv.at[b, j],
                device_id=(px, py, c), device_id_type=pl.DeviceIdType.MESH)

        local = [pltpu.make_async_copy(src(k), dst(k, mine), loc_sems.at[k]) for k in range(nk)]
        local += [pltpu.make_async_copy(wins[b], wouts[b].at[mine], w_loc.at[b]) for b in range(nw)]
        for z, (a, _, row0) in enumerate(zero_fills):
            local.append(pltpu.make_async_copy(zins[z], outs[a].at[row0:row0 + zins[z].shape[0]], z_sems.at[z]))
        wsends = [wcopy(b, j, mine) for b in range(nw) for j in range(3)]
        for cp in local + wsends:
            cp.start()

        for half in (0, 1):
            @pl.when(c == half)
            def _(half=half):
                my_k = [k for k in range(nk) if flat[k][1][5] == half]
                other_k = [k for k in range(nk) if flat[k][1][5] != half]
                sends = [ici(k, j, mine) for k in my_k for j in range(3)]
                for cp in sends:
                    cp.start()
                passed = []
                for k in my_k:
                    for j in range(3):
                        ici(k, j, chip_of[j]).wait_recv()
                        cp = fwd(k, j)
                        cp.start()
                        passed.append(cp)
                for k in other_k:
                    for j in range(3):
                        fwd(k, j).wait_recv()
                for cp in sends + passed:
                    cp.wait_send()

        for b in range(nw):
            for j in range(3):
                wcopy(b, j, chip_of[j]).wait_recv()
        for cp in wsends:
            cp.wait_send()
        for cp in local:
            cp.wait()
        for b in range(nw):
            cols = wins[b].shape[-1]
            for q in range(4):
                wcat[b][..., cols * q:cols * (q + 1)] = wouts[b][q]

        @pl.when(mine == 0)
        def _():
            head_copy(0).wait_send()
            head_copy(1).start()
            head_copy(1).wait_send()

        @pl.when(mine != 0)
        def _():
            hands_on = mine == 2 - c
            head_copy(0).wait_recv()

            @pl.when(hands_on)
            def _():
                head_relay().start()

            head_pass(c).start()
            head_pass(1 - c).wait_recv()
            head_pass(c).wait_send()

            @pl.when(hands_on)
            def _():
                head_relay().wait_send()

    vmem = pl.BlockSpec(memory_space=pltpu.VMEM)
    dma = pltpu.SemaphoreType.DMA
    zeros = [z for _, z, _ in zero_fills]
    return pl.pallas_call(
        body, name="gather_weights",
        in_specs=[vmem] * (2 + ns + nw + nz), out_specs=[vmem] * (3 + ns + nw),
        out_shape=([jax.ShapeDtypeStruct((IN_HEAD, D_MODEL), BF16), jax.ShapeDtypeStruct(w_in_shard.shape, BF16),
                    jax.ShapeDtypeStruct(w_out_shard.shape, BF16)]
                   + [jax.ShapeDtypeStruct((out_rows[a], split[a].shape[1]), BF16) for a in range(ns)]
                   + [jax.ShapeDtypeStruct(w.shape[:-1] + (4 * w.shape[-1],), w.dtype) for w in whole]),
        scratch_shapes=[pltpu.VMEM(a.shape, BF16) for a in split] + [pltpu.VMEM((4,) + w.shape, w.dtype) for w in whole]
        + [dma((nk, 3)), dma((nk, 3)), dma((nk, 3)), dma((nk, 3)), dma((nk,)),
           dma((nw, 3)), dma((nw, 3)), dma((nw,)), dma((nz,)),
           dma((2,)), dma((1,)), dma((1,)), dma((2,))],
        compiler_params=pltpu.CompilerParams(vmem_limit_bytes=VMEM_LIMIT),
    )(w_in_shard, w_out_shard, *split, *whole, *zeros)


def _reduce_grads(parts, small):
    n = len(parts)
    ns = len(small)
    shapes = [a.shape[1:] for a in parts]
    halves = [(sh[0] // 2, sh[1]) for sh in shapes]
    sm_blocks = [a.shape[1] // 128 for a, _, _ in small]
    sm_first = [sum(nr * nblk for (_, nr, _), nblk in zip(small[:k], sm_blocks[:k])) for k in range(ns)]
    sm_rows = -(-(sm_first[-1] + small[-1][1] * sm_blocks[-1]) // 8) * 8
    sm_shape = (sm_rows, 128)

    def body(*refs):
        pin, sm_in = refs[:n], refs[n:n + ns]
        gout, sm_out = refs[n + ns:2 * n + ns], refs[2 * n + ns:2 * n + 2 * ns]
        scr = refs[2 * n + 2 * ns:]
        own, sib, wire, rbuf = scr[:n], scr[n:2 * n], scr[2 * n:3 * n], scr[3 * n:4 * n]
        (sbuf, send_sems, recv_sems, loc_sems, pre_send, pre_recv, post_send, post_recv,
         sm_send, sm_recv, sm_pack, sm_tot) = scr[4 * n:]
        x, y, c = lax.axis_index("x"), lax.axis_index("y"), lax.axis_index("c")
        mine = 2 * x + y
        sm_pack[...] = jnp.zeros(sm_shape, F32)
        for k, (_, nr, _) in enumerate(small):
            for i in range(nr):
                for j in range(sm_blocks[k]):
                    row = sm_first[k] + i * sm_blocks[k] + j
                    sm_pack[row:row + 1, :] = sm_in[k][i:i + 1, 128 * j:128 * (j + 1)]
        me = 4 * x + 2 * y + c
        sibling = (x, y, 1 - c)

        def rows(a, half):
            r2 = halves[a][0]
            return pl.ds(pl.multiple_of(half * r2, r2), r2)

        near = (jnp.where(c == 0, 1 - x, x), jnp.where(c == 0, y, 1 - y))
        far = (jnp.where(c == 0, x, 1 - x), jnp.where(c == 0, 1 - y, y))
        chip = lambda p: 2 * p[0] + p[1]
        blocks = [3 - mine, chip(near), chip(far), mine]

        def pre(a, k):
            q = blocks[k]
            return pltpu.make_async_remote_copy(
                src_ref=pin[a].at[q, rows(a, 1 - c), :], dst_ref=sib[a].at[q],
                send_sem=pre_send.at[a, q], recv_sem=pre_recv.at[a, q], device_id=sibling,
                device_id_type=pl.DeviceIdType.MESH)

        def ici(a, m):
            px, py = near if m < 2 else far
            return pltpu.make_async_remote_copy(
                src_ref=wire[a].at[blocks[m]], dst_ref=rbuf[a].at[m], send_sem=send_sems.at[a, m],
                recv_sem=recv_sems.at[a, m], device_id=(px, py, c), device_id_type=pl.DeviceIdType.MESH)

        def post(a, half):
            ref = gout[a].at[rows(a, half), :]
            return pltpu.make_async_remote_copy(
                src_ref=ref, dst_ref=ref, send_sem=post_send.at[a], recv_sem=post_recv.at[a],
                device_id=sibling, device_id_type=pl.DeviceIdType.MESH)

        def small_copy(kk):
            peer = (x ^ (kk >> 2), y ^ ((kk >> 1) & 1), c ^ (kk & 1))
            return pltpu.make_async_remote_copy(
                src_ref=sm_pack, dst_ref=sbuf.at[kk], send_sem=sm_send.at[kk - 1], recv_sem=sm_recv.at[kk - 1],
                device_id=peer, device_id_type=pl.DeviceIdType.MESH)

        local = [[pltpu.make_async_copy(pin[a].at[blocks[k], rows(a, c), :], own[a].at[blocks[k]], loc_sems.at[a, k])
                  for k in range(4)] for a in range(n)]
        pres = [[pre(a, k) for k in range(4)] for a in range(n)]
        smalls = [small_copy(kk) for kk in range(1, 8)]
        for a in range(n):
            for k in range(4):
                local[a][k].start()
                pres[a][k].start()
        for cp in smalls:
            cp.start()
        sbuf[0] = sm_pack[...]
        sends = []
        for a in range(n):
            for k in range(4):
                local[a][k].wait()
                pres[a][k].wait_recv()
                tot = own[a][blocks[k]] + sib[a][blocks[k]]
                if k == 2:
                    ici(a, 0).wait_recv()
                    tot = tot + rbuf[a][0].astype(F32)
                own[a][blocks[k]] = tot
                if k < 3:
                    wire[a][blocks[k]] = tot.astype(BF16)
                    cp = ici(a, k)
                    cp.start()
                    sends.append(cp)
        for cp in smalls:
            cp.wait_recv()
        total = sbuf[me]
        for d in range(1, 8):
            total = total + sbuf[me ^ d]
        sm_tot[...] = total
        for k, (_, nr, sliced) in enumerate(small):
            nblk = sm_blocks[k] // 4 if sliced else sm_blocks[k]
            for i in range(nr):
                for j in range(nblk):
                    row = sm_first[k] + i * sm_blocks[k] + j
                    if sliced:
                        row = row + mine * nblk
                    sm_out[k][i:i + 1, 128 * j:128 * (j + 1)] = sm_tot[pl.ds(row, 1), :]
        posts = []
        for a in range(n):
            fin = own[a][mine]
            for m in (1, 2):
                ici(a, m).wait_recv()
                fin = fin + rbuf[a][m].astype(F32)
            gout[a][rows(a, c), :] = fin
            cp = post(a, c)
            cp.start()
            posts.append(cp)
        for a in range(n):
            post(a, 1 - c).wait_recv()
        for cp in [cp for row in pres for cp in row] + sends + smalls + posts:
            cp.wait_send()

    vmem = pl.BlockSpec(memory_space=pltpu.VMEM)
    dma = pltpu.SemaphoreType.DMA
    return pl.pallas_call(
        body, name="reduce_grads",
        in_specs=[pl.BlockSpec(memory_space=pl.ANY)] * n + [vmem] * ns, out_specs=[vmem] * (n + ns),
        out_shape=[jax.ShapeDtypeStruct(sh, F32) for sh in shapes]
        + [jax.ShapeDtypeStruct((nr, a.shape[1] // 4 if sliced else a.shape[1]), F32) for a, nr, sliced in small],
        scratch_shapes=([pltpu.VMEM((4,) + hs, F32) for hs in halves] + [pltpu.VMEM((4,) + hs, F32) for hs in halves]
                        + [pltpu.VMEM((4,) + hs, BF16) for hs in halves]
                        + [pltpu.VMEM((3,) + hs, BF16) for hs in halves]
                        + [pltpu.VMEM((8,) + sm_shape, F32), dma((n, 3)), dma((n, 3)), dma((n, 4)),
                           dma((n, 4)), dma((n, 4)), dma((n,)), dma((n,)), dma((7,)), dma((7,)),
                           pltpu.VMEM(sm_shape, F32), pltpu.VMEM(sm_shape, F32)]),
        compiler_params=pltpu.CompilerParams(vmem_limit_bytes=VMEM_LIMIT),
    )(*parts, *[a for a, _, _ in small])


def _adamw_update(w_ref, g_ref, m_ref, v_ref, d_ref, nm_ref, nv_ref):
    gv = g_ref[...]
    nm = ADAM_B1 * m_ref[...] + (1.0 - ADAM_B1) * gv
    nv = ADAM_B2 * v_ref[...] + (1.0 - ADAM_B2) * (gv * gv)
    m_hat = nm / (1.0 - ADAM_B1 ** ADAM_STEP)
    v_hat = nv / (1.0 - ADAM_B2 ** ADAM_STEP)
    d_ref[...] = -ADAM_LR * (m_hat / (jnp.sqrt(v_hat) + ADAM_EPS) + ADAM_WD * w_ref[...])
    nm_ref[...] = nm
    nv_ref[...] = nv


def _adamw_small(ws, gs, ms, vs):
    k = len(ws)

    def body(*refs):
        ins, outs = refs[:4 * k], refs[4 * k:]
        for a in range(k):
            _adamw_update(ins[a], ins[k + a], ins[2 * k + a], ins[3 * k + a], outs[a], outs[k + a], outs[2 * k + a])

    out = pl.pallas_call(
        body, name="adamw_small",
        out_shape=[jax.ShapeDtypeStruct(w.shape, F32) for w in ws] * 3,
        compiler_params=pltpu.CompilerParams(vmem_limit_bytes=VMEM_LIMIT),
    )(*ws, *gs, *ms, *vs)
    return out[:k], out[k:2 * k], out[2 * k:]


def _adamw(w, g, m, v, name):
    shape = w.shape
    w2, g2, m2, v2 = (a.reshape((-1, shape[-1])) for a in (w, g, m, v))

    def body(w_ref, g_ref, m_ref, v_ref, d_ref, nm_ref, nv_ref):
        _adamw_update(w_ref, g_ref, m_ref, v_ref, d_ref, nm_ref, nv_ref)

    rows, cols = w2.shape
    nblk = cols // 256 if cols % 256 == 0 and rows >= 64 else 1
    blk = pl.BlockSpec((rows, cols // nblk), lambda j: (0, j))
    out = pl.pallas_call(
        body, name=name, grid=(nblk,), in_specs=[blk] * 4, out_specs=[blk] * 3,
        out_shape=[jax.ShapeDtypeStruct(w2.shape, F32)] * 3,
        compiler_params=_cparams("parallel"),
    )(w2, g2, m2, v2)
    return tuple(a.reshape(shape) for a in out)


def kernel(x, meta_tokens, norm_g, w_in, q_norm_g, w_q_up, kv_norm_g, w_kv_up, conv_w, attn_out_g, conv_out_g, w_out, final_norm_g, loss_target, m_meta_tokens, m_norm_g, m_w_in, m_q_norm_g, m_w_q_up, m_kv_norm_g, m_w_kv_up, m_conv_w, m_attn_out_g, m_conv_out_g, m_w_out, m_final_norm_g, v_meta_tokens, v_norm_g, v_w_in, v_q_norm_g, v_w_q_up, v_kv_norm_g, v_w_kv_up, v_conv_w, v_attn_out_g, v_conv_out_g, v_w_out, v_final_norm_g):
    nb, s, _ = x.shape
    tm = min(ROW_TILE, s)
    ta = min(ATTN_TILE, s)
    assert s % tm == 0 and s % ta == 0 and tm % 16 == 0
    r = nb * s

    tr = lambda a: jnp.transpose(a[0])
    w_head, w_in_shard, w_out_shard, wq_p, wkv_p, g_cw, meta_f = _gather_weights(
        tr(w_in), w_out[0], [tr(w_q_up), tr(w_kv_up)],
        [W_Q_PIECES, W_KV_PIECES], [HEADS * QK_PAD, 1024],
        [jnp.transpose(conv_w, (1, 0, 2)), meta_tokens],
        [(0, jnp.zeros((64, Q_RANK), BF16), QK_PAD * h + NOPE + ROPE) for h in range(HEADS)])
    conv_f = g_cw.reshape(3, CONV_W)

    c_all, sa_all, sb_all = _rope_tables(N_META + s)
    tabs_m = (c_all[:N_META], sa_all[:N_META], sb_all[:N_META])
    tabs = (c_all[N_META:], sa_all[N_META:], sb_all[N_META:])
    gid = np.arange(CONV_W) // CONV_GROUP
    gmat = jnp.asarray(np.where(gid[:, None] == gid[None, :], 1.0 / CONV_GROUP, 0.0), BF16)
    ga, gc = attn_out_g, conv_out_g
    gf = final_norm_g.reshape(1, D_MODEL)

    x2d = x.reshape(r, D_MODEL)
    tgt2d = loss_target.reshape(r, D_MODEL)

    ph, q, k, v, pmh, km, vm, w_out_f, w_in_part = _fwd_proj(
        x2d, meta_f, tabs, tabs_m, norm_g, w_head, q_norm_g, wq_p, kv_norm_g, wkv_p, w_out_shard, w_in_shard,
        nb, s, tm)
    o, lse, w_in_p = _attn_fwd(q, k, v, km, vm, w_in_shard, w_in_part, nb, s, ta)
    dh2, dycat, dw_out, dgf, loss_acc, pt, pmt = _out_fwd_bwd(x2d, tgt2d, o, meta_f, norm_g, w_in_p, conv_f, ga, gc,
                                                              gmat, w_out_f, gf, nb, s, tm)
    dpb, do, delta, dccm, dga, dgc, dcw = _gate_bwd(dycat, o, pt, pmt, conv_f, ga, gc, gmat, nb, s, tm)
    p_out = dw_out.reshape(4, D_MODEL // 4, D_MODEL)
    dq, dk, dv, dkm, dvm, g_w_out = _attn_bwd(q, k, v, do, lse, delta, km, vm, [p_out], nb, s, ta)
    dpa, dpam, p_q, p_kv, dgq, dgkv = _up_bwd(dq, dk, dv, dkm, dvm, ph, pmh, tabs, tabs_m, wq_p, wkv_p,
                                              q_norm_g, kv_norm_g, nb, s, tm)
    gx, gmeta, p_in, dng = _in_bwd(x2d, dh2, dpa, dpb, meta_f, dpam, dccm, pmt, w_in_p, norm_g, nb, s, tm)

    (g_w_in_t, g_w_q_t, g_w_kv_t, g_norm, g_qn, g_kvn, g_ga, g_gc, g_gf, g_conv, g_mt, loss_sum) = _reduce_grads(
        [p_in, p_q, p_kv],
        [(dng, 1, False), (dgq, 1, False), (dgkv, 1, False), (dga, 1, False), (dgc, 1, False), (dgf, 1, False),
         (dcw, 3, True), (gmeta, N_META, True), (loss_acc, 1, False)])
    loss = loss_sum[0, 0]
    g_gf = g_gf.reshape(-1)

    grads = {
        "meta_tokens": g_mt, "norm_g": g_norm.reshape(1, -1), "w_in": g_w_in_t, "q_norm_g": g_qn.reshape(1, -1),
        "w_q_up": g_w_q_t, "kv_norm_g": g_kvn.reshape(1, -1), "w_kv_up": jnp.transpose(g_w_kv_t)[None],
        "conv_w": g_conv[None], "attn_out_g": g_ga.reshape(1, -1), "conv_out_g": g_gc.reshape(1, -1),
        "w_out": g_w_out[None], "final_norm_g": g_gf,
    }
    transposed = ("w_in", "w_q_up")
    weights = {
        "meta_tokens": (meta_tokens, m_meta_tokens, v_meta_tokens), "norm_g": (norm_g, m_norm_g, v_norm_g),
        "w_in": (w_in, m_w_in, v_w_in), "q_norm_g": (q_norm_g, m_q_norm_g, v_q_norm_g),
        "w_q_up": (w_q_up, m_w_q_up, v_w_q_up), "kv_norm_g": (kv_norm_g, m_kv_norm_g, v_kv_norm_g),
        "w_kv_up": (w_kv_up, m_w_kv_up, v_w_kv_up), "conv_w": (conv_w, m_conv_w, v_conv_w),
        "attn_out_g": (attn_out_g, m_attn_out_g, v_attn_out_g), "conv_out_g": (conv_out_g, m_conv_out_g, v_conv_out_g),
        "w_out": (w_out, m_w_out, v_w_out), "final_norm_g": (final_norm_g, m_final_norm_g, v_final_norm_g),
    }
    names = list(weights)
    small = [nme for nme in names if nme != "w_in"]

    def view(nme, a):
        if nme in transposed:
            return a if a.ndim == 2 else tr(a)
        if nme == "conv_w":
            return jnp.transpose(a.reshape(1, 3, -1), (1, 0, 2))
        if a.ndim == 3:
            return a[0]
        return a.reshape(1, -1) if a.ndim == 1 else a

    def unview(nme, a):
        if nme in transposed:
            return jnp.transpose(a)[None]
        if nme == "conv_w":
            return jnp.transpose(a, (1, 0, 2))
        return a.reshape(weights[nme][0].shape)

    res_small = _adamw_small(*[[view(nme, a) for nme, a in zip(small, col)] for col in (
        [weights[nme][0] for nme in small], [grads[nme] for nme in small],
        [weights[nme][1] for nme in small], [weights[nme][2] for nme in small])])
    w_, m_, v_ = weights["w_in"]
    res = _adamw(tr(w_), grads["w_in"], tr(m_), tr(v_), "adamw_w_in")
    upd = {"w_in": tuple(jnp.transpose(a)[None] for a in (grads["w_in"],) + res)}
    for j, nme in enumerate(small):
        upd[nme] = (unview(nme, view(nme, grads[nme])),) + tuple(unview(nme, r[j]) for r in res_small)
    grads = {nme: upd[nme][0] for nme in names}
    deltas, new_m, new_v = ([upd[nme][j] for nme in names] for j in (1, 2, 3))

    grad_x = gx.reshape(nb, s, D_MODEL)
    return (loss, grad_x, *[grads[nme] for nme in names], *deltas, *new_m, *new_v)
```

```python
import functools

import jax
import jax.numpy as jnp
import numpy as np
from jax import lax
from jax.experimental import pallas as pl
from jax.experimental.pallas import tpu as pltpu

F32 = jnp.float32
BF16 = jnp.bfloat16

D_MODEL = 1024
N_META = 16
HEADS = 4
NOPE = 128
ROPE = 64
VDIM = 128
QK_PAD = 256
Q_RANK = 256
KV_RANK = 128
CONV_W = 512
CONV_GROUP = 64
ROPE_THETA = 10000.0
EPS = 1e-6
ATTN_SCALE = (NOPE + ROPE) ** -0.5
IN_DIM = 3008
IN_PAD = 3072
HEAD_ROWS = Q_RANK + KV_RANK + ROPE
IN_HEAD = 512
IN_TAIL = IN_PAD - IN_HEAD
BLK_ZA, BLK_CB, BLK_CC, BLK_CH, BLK_ZC = 0, 1, 2, 3, 4
NEG_INF = -1e30

ADAM_LR = 0.001
ADAM_B1 = 0.9
ADAM_B2 = 0.999
ADAM_EPS = 1e-08
ADAM_WD = 0.01
ADAM_STEP = 10

ROW_TILE = 512
ATTN_TILE = 256
VMEM_LIMIT = 56 * 1024 * 1024

NT = (((1,), (1,)), ((), ()))
TN = (((0,), (0,)), ((), ()))


def _cparams(*sem):
    return pltpu.CompilerParams(dimension_semantics=sem, vmem_limit_bytes=VMEM_LIMIT)


def _dot(a, b):
    return jnp.dot(a, b, preferred_element_type=F32)


def _dot_nt(a, b):
    return lax.dot_general(a, b, NT, preferred_element_type=F32)


def _dot_tn(a, b):
    return lax.dot_general(a, b, TN, preferred_element_type=F32)


def _rms(x, g):
    r = lax.rsqrt(jnp.mean(x * x, axis=-1, keepdims=True) + EPS)
    return x * r * g, r


def _rms_bwd(dy, x, r, g):
    xh = x * r
    dyg = dy * g
    dx = r * (dyg - xh * jnp.mean(dyg * xh, axis=-1, keepdims=True))
    return dx, dy * xh


def _sigmoid(z):
    return 1.0 / (1.0 + jnp.exp(-z))


def _rope(b, c, sa, sb):
    return b * c + pltpu.roll(b, 96, 1) * sa + pltpu.roll(b, 32, 1) * sb


def _rope_bwd(d, c, sa, sb):
    return d * c + pltpu.roll(d * sa, 32, 1) + pltpu.roll(d * sb, 96, 1)


def _group_mean(x, gmat):
    hi = x.astype(BF16)
    lo = (x - hi.astype(F32)).astype(BF16)
    return _dot(hi, gmat) + _dot(lo, gmat)


def _row_of(col, rows):
    return jnp.transpose(jnp.broadcast_to(col, (rows, 128)))[0:1, :]


def _rope_tables(n_pos):
    half = ROPE // 2
    inv_freq = (np.float32(1.0) / (np.float32(ROPE_THETA) ** (np.arange(half, dtype=np.float32) / np.float32(half))))
    ang = np.arange(n_pos, dtype=np.float32)[:, None] * inv_freq.astype(np.float32)[None, :]
    cos, sin = np.cos(ang).astype(np.float32), np.sin(ang).astype(np.float32)
    z = np.zeros((n_pos, half), np.float32)
    c = np.concatenate([cos, cos, z, z], axis=1)
    sa = np.concatenate([-sin, z, z, z], axis=1)
    sb = np.concatenate([z, sin, z, z], axis=1)
    return jnp.asarray(c), jnp.asarray(sa), jnp.asarray(sb)


W_IN_PIECES_1 = ((0, 80, 752, 0, 64, 0), (384, 64, 752, 384, 448, 1), (448, 16, 752, 512, 512, 1))
W_IN_PIECES_2 = ((80, 304, 752, 80, 144, 0), (464, 288, 752, 528, 528, 1))
W_Q_PIECES = ((0, 96, 256, 0, 0, 0), (96, 96, 256, 96, 96, 1))
W_KV_PIECES = ((0, 128, 128, 0, 0, 0), (128, 128, 128, 512, 512, 1))
W_OUT_PIECES = ((0, 128, 256, 0, 0, 0), (128, 128, 256, 128, 128, 1))


class _StagedGather:
    STAGES = 4

    @staticmethod
    def steps(n_steps):
        return (0, 5 * n_steps // 8, 7 * n_steps // 8, n_steps - 1)

    def __init__(self, pieces, zero_rows=None):
        self.pieces = pieces
        self.zero_rows = zero_rows

    def scratch(self):
        nk, dma = len(self.pieces), pltpu.SemaphoreType.DMA
        return [dma((nk, 3)), dma((nk, 3)), dma((nk, 3)), dma((nk, 3)), dma((nk,))]

    def vmem_scratch(self, shard_shape, out_shape):
        return [pltpu.VMEM(shard_shape, BF16), pltpu.VMEM(out_shape, BF16),
                pltpu.SemaphoreType.DMA((4 * len(self.pieces) + 2,))] + self.scratch()

    def run_vmem(self, stage, shard_ref, out_ref, scr):
        src_scr, land_scr, io_sems = scr[:3]
        spans = []
        for _, nr, per, first, rest, _ in self.pieces:
            spans += [(per * q + (first if q == 0 else rest), nr) for q in range(4)]
        if self.zero_rows is not None:
            spans.append(self.zero_rows)
        flush = [pltpu.make_async_copy(land_scr.at[r0:r0 + nr], out_ref.at[r0:r0 + nr], io_sems.at[n])
                 for n, (r0, nr) in enumerate(spans)]
        if stage == 0:
            load = pltpu.make_async_copy(shard_ref, src_scr, io_sems.at[len(spans)])
            load.start()
            if self.zero_rows is not None:
                r0, nr = self.zero_rows
                land_scr[r0:r0 + nr, :] = jnp.zeros((nr, land_scr.shape[1]), BF16)
            load.wait()
        if stage < self.STAGES:
            self.run(stage, src_scr, land_scr, scr[3:])
        for cp in flush:
            if stage == self.STAGES - 1:
                cp.start()
            if stage == self.STAGES:
                cp.wait()

    def run(self, stage, src_ref, out_ref, scr):
        send_sems, recv_sems, fwd_send, fwd_recv, loc_sems = scr
        pieces = self.pieces
        nk = len(pieces)
        x, y, c = lax.axis_index("x"), lax.axis_index("y"), lax.axis_index("c")
        mine = 2 * x + y
        chips = [(1 - x, y), (x, 1 - y), (1 - x, 1 - y)]
        chip_of = [2 * px + py for px, py in chips]
        mesh = pl.DeviceIdType.MESH

        def src(k):
            s0, nr = pieces[k][0], pieces[k][1]
            return src_ref.at[s0:s0 + nr]

        def dst(k, q):
            _, nr, per, first, rest, _ = pieces[k]
            row = per * q + first + (rest - first) * jnp.minimum(q, 1)
            return out_ref.at[pl.ds(pl.multiple_of(row, 16), nr)]

        def ici(k, j, q):
            px, py = chips[j]
            return pltpu.make_async_remote_copy(
                src_ref=src(k), dst_ref=dst(k, q), send_sem=send_sems.at[k, j], recv_sem=recv_sems.at[k, j],
                device_id=(px, py, c), device_id_type=mesh)

        def fwd(k, j):
            ref = dst(k, chip_of[j])
            return pltpu.make_async_remote_copy(
                src_ref=ref, dst_ref=ref, send_sem=fwd_send.at[k, j], recv_sem=fwd_recv.at[k, j],
                device_id=(x, y, 1 - c), device_id_type=mesh)

        def relay(k, half):
            ref = dst(k, chip_of[half])
            px, py = chips[1 - half]
            return pltpu.make_async_remote_copy(
                src_ref=ref, dst_ref=ref, send_sem=send_sems.at[k, 2], recv_sem=recv_sems.at[k, 2],
                device_id=(px, py, c), device_id_type=mesh)

        local = [pltpu.make_async_copy(src(k), dst(k, mine), loc_sems.at[k]) for k in range(nk)]
        if stage == 0:
            for cp in local:
                cp.start()
        if stage == 3:
            for cp in local:
                cp.wait()
        for half in (0, 1):
            @pl.when(c == half)
            def _(half=half):
                my_k = [k for k in range(nk) if pieces[k][5] == half]
                other_k = [k for k in range(nk) if pieces[k][5] != half]
                for k in my_k:
                    if stage == 0:
                        for j in range(2):
                            ici(k, j, mine).start()
                    elif stage == 1:
                        for j in (half, 1 - half):
                            ici(k, j, chip_of[j]).wait_recv()
                            if j == half:
                                relay(k, half).start()
                            fwd(k, j).start()
                    elif stage == 2:
                        ici(k, 2, chip_of[2]).wait_recv()
                        fwd(k, 2).start()
                    else:
                        for j in range(2):
                            ici(k, j, mine).wait_send()
                        relay(k, half).wait_send()
                        for j in range(3):
                            fwd(k, j).wait_send()
                if stage == 3:
                    for k in other_k:
                        for j in range(3):
                            fwd(k, j).wait_recv()


def _fwd_proj(x2d, meta, tabs, tabs_m, norm_g, w_head, q_norm_g, wq_p, kv_norm_g, wkv_p, w_out_shard, w_in_shard,
              nb, s, tm):
    nt = s // tm
    n = nb * nt
    n_steps = n + 1
    c_t, sa_t, sb_t = tabs
    cm_t, sam_t, sbm_t = tabs_m
    gat = _StagedGather(W_OUT_PIECES)
    gat_in = _StagedGather(W_IN_PIECES_1)
    n_sems = len(gat.scratch())
    assert n_steps >= 3

    def body(x_ref, c_ref, sa_ref, sb_ref, mt_ref, cm_ref, sam_ref, sbm_ref,
             g_ref, w_ref, gq_ref, wq_ref, gkv_ref, wkv_ref, wos_ref, wis_ref,
             p_ref, q_ref, k_ref, v_ref, pm_ref, km_ref, vm_ref, wo_ref, wi_ref, *scr):
        gat_scr, gat_in_scr = scr[:n_sems], scr[n_sems:]
        i = pl.program_id(0)
        for stage, at in enumerate(_StagedGather.steps(n_steps)):
            @pl.when(i == at)
            def _(stage=stage):
                gat_in.run_vmem(stage, wis_ref, wi_ref, gat_in_scr)
                gat.run(stage, wos_ref, wo_ref, gat_scr)

        def project(xv, c, sa, sb, p_out, q_out, k_out, v_out):
            u, _ = _rms(xv, g_ref[...])
            p = _dot_nt(u.astype(BF16), w_ref[...])
            p_out[...] = p
            qn, _ = _rms(p[:, 0:Q_RANK], gq_ref[...])
            q = _dot_nt(qn.astype(BF16), wq_ref[...])
            kvn, _ = _rms(p[:, Q_RANK:Q_RANK + KV_RANK], gkv_ref[...])
            kv = _dot_nt(kvn.astype(BF16), wkv_ref[...])
            kpe = _rope(p[:, 384:512], c, sa, sb)
            for h in range(HEADS):
                if q_out is not None:
                    pe = _rope(q[:, QK_PAD * h + NOPE:QK_PAD * (h + 1)], c, sa, sb)
                    qh = jnp.concatenate([q[:, QK_PAD * h:QK_PAD * h + NOPE], pe], axis=1)
                    q_out[0, h] = (qh * ATTN_SCALE).astype(BF16)
                k_out[0, h] = jnp.concatenate([kv[:, NOPE * h:NOPE * (h + 1)], kpe], axis=1).astype(BF16)
                v_out[0, h] = kv[:, 512 + VDIM * h:512 + VDIM * (h + 1)].astype(BF16)

        @pl.when(i < n)
        def _():
            project(x_ref[...], c_ref[...], sa_ref[...], sb_ref[...], p_ref, q_ref, k_ref, v_ref)

        @pl.when(i == n)
        def _():
            project(mt_ref[...], cm_ref[...], sam_ref[...], sbm_ref[...], pm_ref, None, km_ref, vm_ref)
            gat_in.run_vmem(gat_in.STAGES, wis_ref, wi_ref, gat_in_scr)

    cl = lambda i: jnp.minimum(i, n - 1)
    full = lambda a: pl.BlockSpec(a.shape, lambda i: (0,) * a.ndim)
    const = lambda shape: pl.BlockSpec(shape, lambda i: (0,) * len(shape))
    tab = pl.BlockSpec((tm, 128), lambda i: (cl(i) % nt, 0))
    hb = lambda w: pl.BlockSpec((1, HEADS, tm, w), lambda i: (cl(i) // nt, 0, cl(i) % nt, 0))
    whole = pl.BlockSpec(memory_space=pl.ANY)
    return pl.pallas_call(
        body, name="fwd_proj", grid=(n_steps,),
        in_specs=[pl.BlockSpec((tm, D_MODEL), lambda i: (cl(i), 0)), tab, tab, tab,
                  full(meta), full(cm_t), full(sam_t), full(sbm_t),
                  full(norm_g), full(w_head), full(q_norm_g), full(wq_p), full(kv_norm_g), full(wkv_p), whole, whole],
        out_specs=[pl.BlockSpec((tm, IN_HEAD), lambda i: (cl(i), 0)), hb(QK_PAD), hb(QK_PAD), hb(VDIM),
                   const((N_META, IN_HEAD)), const((1, HEADS, N_META, QK_PAD)), const((1, HEADS, N_META, VDIM)),
                   whole, whole],
        out_shape=[jax.ShapeDtypeStruct((nb * s, IN_HEAD), F32),
                   jax.ShapeDtypeStruct((nb, HEADS, s, QK_PAD), BF16),
                   jax.ShapeDtypeStruct((nb, HEADS, s, QK_PAD), BF16),
                   jax.ShapeDtypeStruct((nb, HEADS, s, VDIM), BF16),
                   jax.ShapeDtypeStruct((N_META, IN_HEAD), F32),
                   jax.ShapeDtypeStruct((1, HEADS, N_META, QK_PAD), BF16),
                   jax.ShapeDtypeStruct((1, HEADS, N_META, VDIM), BF16),
                   jax.ShapeDtypeStruct((D_MODEL, D_MODEL), BF16),
                   jax.ShapeDtypeStruct((IN_PAD, D_MODEL), BF16)],
        scratch_shapes=gat.scratch() + gat_in.vmem_scratch(w_in_shard.shape, (IN_PAD, D_MODEL)),
        compiler_params=_cparams("arbitrary"),
    )(x2d, c_t, sa_t, sb_t, meta, cm_t, sam_t, sbm_t, norm_g, w_head, q_norm_g, wq_p, kv_norm_g, wkv_p, w_out_shard,
      w_in_shard)


def _attn_fwd(q, k, v, km, vm, w_in_shard, w_in_part, nb, s, tq):
    nq = s // tq
    n_steps = nb * HEADS
    gat = _StagedGather(W_IN_PIECES_2, zero_rows=(HEAD_ROWS, IN_HEAD - HEAD_ROWS))
    assert n_steps >= 3

    def body(q_ref, k_ref, v_ref, km_ref, vm_ref, ws_ref, _, o_ref, lse_ref, w_ref, s_scr, p_scr, *gat_scr):
        step = pl.program_id(0) * HEADS + pl.program_id(1)
        for stage, at in enumerate(_StagedGather.steps(n_steps)):
            @pl.when(step == at)
            def _(stage=stage):
                gat.run_vmem(stage, ws_ref, w_ref, gat_scr)

        row = lax.broadcasted_iota(jnp.int32, (tq, tq), 0)
        col = lax.broadcasted_iota(jnp.int32, (tq, tq), 1)
        def scores(i):
            slot = i % 2
            qi = q_ref[0, 0, i * tq:(i + 1) * tq, :]
            sm = _dot_nt(qi, km_ref[0, 0])
            m128 = None
            for j in range(i + 1):
                sc = _dot_nt(qi, k_ref[0, 0, j * tq:(j + 1) * tq, :])
                if j == i:
                    sc = jnp.where(col <= row, sc, NEG_INF)
                s_scr[slot, :, j * tq:(j + 1) * tq] = sc
                mx = sc[:, 0:128]
                for c0 in range(128, tq, 128):
                    mx = jnp.maximum(mx, sc[:, c0:c0 + 128])
                m128 = mx if m128 is None else jnp.maximum(m128, mx)
            return sm, jnp.maximum(jnp.max(m128, axis=1, keepdims=True), jnp.max(sm, axis=1, keepdims=True))

        def weighted_sum(i, pm, l):
            n = (i + 1) * tq
            acc = _dot(p_scr[i % 2, :, 0:n], v_ref[0, 0, 0:n, :]) + _dot(pm.astype(BF16), vm_ref[0, 0])
            o_ref[0, 0, i * tq:(i + 1) * tq, :] = acc / l

        nxt, pending = scores(0), None
        for i in range(nq):
            slot = i % 2
            sm, m = nxt
            if i + 1 < nq:
                nxt = scores(i + 1)
            pm = jnp.exp(sm - m)
            l128 = None
            for j in range(i + 1):
                p = jnp.exp(s_scr[slot, :, j * tq:(j + 1) * tq] - m)
                p_scr[slot, :, j * tq:(j + 1) * tq] = p.astype(BF16)
                ps = p[:, 0:128]
                for c0 in range(128, tq, 128):
                    ps = ps + p[:, c0:c0 + 128]
                l128 = ps if l128 is None else l128 + ps
            l = jnp.sum(l128, axis=1, keepdims=True) + jnp.sum(pm, axis=1, keepdims=True)
            lse_ref[0, 0, :, i * tq:(i + 1) * tq] = _row_of(m + jnp.log(l), tq)
            if pending is not None:
                weighted_sum(*pending)
            pending = (i, pm, l)
        weighted_sum(*pending)

        @pl.when(step == n_steps - 1)
        def _():
            gat.run_vmem(gat.STAGES, ws_ref, w_ref, gat_scr)

    hblk = lambda w: pl.BlockSpec((1, 1, s, w), lambda b, h: (b, h, 0, 0))
    mblk = lambda w: pl.BlockSpec((1, 1, N_META, w), lambda b, h: (0, h, 0, 0))
    whole = pl.BlockSpec(memory_space=pl.ANY)
    return pl.pallas_call(
        body, name="attn_fwd", grid=(nb, HEADS),
        in_specs=[hblk(QK_PAD), hblk(QK_PAD), hblk(VDIM), mblk(QK_PAD), mblk(VDIM), whole, whole],
        out_specs=[hblk(VDIM), pl.BlockSpec((1, 1, 1, s), lambda b, h: (b, h, 0, 0)), whole],
        out_shape=[jax.ShapeDtypeStruct((nb, HEADS, s, VDIM), F32),
                   jax.ShapeDtypeStruct((nb, HEADS, 1, s), F32),
                   jax.ShapeDtypeStruct(w_in_part.shape, BF16)],
        input_output_aliases={6: 2},
        scratch_shapes=[pltpu.VMEM((2, tq, s), F32), pltpu.VMEM((2, tq, s), BF16)]
        + gat.vmem_scratch(w_in_shard.shape, w_in_part.shape),
        compiler_params=_cparams("arbitrary", "arbitrary"),
    )(q, k, v, km, vm, w_in_shard, w_in_part)


def _shift_rows(a, prev, n_rows):
    rid = lax.broadcasted_iota(jnp.int32, a.shape, 0)
    a1 = jnp.where(rid == 0, prev[7:8, :], pltpu.roll(a, 1, 0))
    a2 = jnp.where(rid == 0, prev[6:7, :], jnp.where(rid == 1, prev[7:8, :], pltpu.roll(a, 2, 0)))
    return a1, a2


def _attn_gate(o, za, ga_h):
    on, r = _rms(o, ga_h)
    return on * (za * _sigmoid(za)), on, r


def _out_fwd_bwd(x2d, tgt2d, o, meta, norm_g, w_in_p, conv_w, ga, gc, gmat, w_out, gf, nb, s, tm):
    nt = s // tm
    r = nb * s

    def body(x_ref, t_ref, o_ref, mt_ref, g_ref, wi_ref, cw_ref, ga_ref, gc_ref, gm_ref, w_ref, gf_ref,
             dh_ref, dy_ref, dw_ref, dgf_ref, loss_ref, p_ref, pm_ref, last_cc):
        i = pl.program_id(0)
        blk = lambda ref, j, rows=slice(None): ref[rows, 512 * j:512 * (j + 1)]

        def tail(xv):
            u, _ = _rms(xv, g_ref[...])
            return _dot_nt(u.astype(BF16), wi_ref[IN_HEAD:IN_PAD, :])

        @pl.when(i == 0)
        def _():
            dw_ref[...] = jnp.zeros_like(dw_ref)
            dgf_ref[...] = jnp.zeros_like(dgf_ref)
            loss_ref[...] = jnp.zeros_like(loss_ref)
            last_cc[...] = jnp.zeros_like(last_cc)
            pm_ref[...] = tail(mt_ref[...])

        u16 = _rms(x_ref[...], g_ref[...])[0].astype(BF16)

        def project(j):
            p_ref[:, 512 * j:512 * (j + 1)] = _dot_nt(u16, wi_ref[IN_HEAD + 512 * j:IN_HEAD + 512 * (j + 1), :])

        project(BLK_ZA)
        project(BLK_CC)
        project(BLK_CH)
        ya = []
        for h in range(HEADS):
            y, _, _ = _attn_gate(o_ref[0, h], p_ref[:, 512 * BLK_ZA + VDIM * h:512 * BLK_ZA + VDIM * (h + 1)],
                                 ga_ref[:, VDIM * h:VDIM * (h + 1)])
            ya.append(y)
        project(BLK_CB)
        project(BLK_ZC)
        cc = blk(p_ref, BLK_CC) * blk(p_ref, BLK_CH)
        meta_cc = blk(pm_ref, BLK_CC, slice(8, 16)) * blk(pm_ref, BLK_CH, slice(8, 16))
        prev = jnp.where(i % nt == 0, meta_cc, last_cc[...])
        last_cc[...] = cc[tm - 8:tm, :]
        cc1, cc2 = _shift_rows(cc, prev, tm)
        yc = blk(p_ref, BLK_CB) * (cw_ref[0:1, :] * cc2 + cw_ref[1:2, :] * cc1 + cw_ref[2:3, :] * cc)
        rg = lax.rsqrt(_group_mean(yc * yc, gm_ref[...]) + EPS)
        zc = blk(p_ref, BLK_ZC)
        yconv = yc * rg * gc_ref[...] * (zc * _sigmoid(zc))
        ycat = jnp.concatenate(ya + [yconv], axis=1).astype(BF16)
        h2 = x_ref[...] + _dot(ycat, w_ref[...])
        gfv = gf_ref[...]
        y, r2 = _rms(h2, gfv)
        e = y - t_ref[...]
        loss_ref[...] += 0.5 * jnp.sum(e * e) / D_MODEL
        dyv = e * (1.0 / D_MODEL)
        dh2, dgf = _rms_bwd(dyv, h2, r2, gfv)
        dgf_ref[...] += jnp.sum(dgf, axis=0, keepdims=True)
        dh_ref[...] = dh2
        dhb = dh2.astype(BF16)
        dy_ref[...] = _dot_nt(dhb, w_ref[...])
        dw_ref[...] += _dot_tn(ycat, dhb)

    row = lambda w: pl.BlockSpec((tm, w), lambda i: (i, 0))
    const = lambda shape: pl.BlockSpec(shape, lambda i: (0,) * len(shape))
    full = lambda a: const(a.shape)
    return pl.pallas_call(
        body, name="out_fwd_bwd", grid=(nb * nt,),
        in_specs=[row(D_MODEL), row(D_MODEL),
                  pl.BlockSpec((1, HEADS, tm, VDIM), lambda i: (i // nt, 0, i % nt, 0)),
                  full(meta), full(norm_g), full(w_in_p),
                  full(conv_w), full(ga), full(gc), full(gmat), full(w_out), full(gf)],
        out_specs=[row(D_MODEL), row(D_MODEL), const((D_MODEL, D_MODEL)), const((1, D_MODEL)), const((1, 128)),
                   row(IN_TAIL), const((N_META, IN_TAIL))],
        out_shape=[jax.ShapeDtypeStruct((r, D_MODEL), F32), jax.ShapeDtypeStruct((r, D_MODEL), F32),
                   jax.ShapeDtypeStruct((D_MODEL, D_MODEL), F32), jax.ShapeDtypeStruct((1, D_MODEL), F32),
                   jax.ShapeDtypeStruct((1, 128), F32),
                   jax.ShapeDtypeStruct((r, IN_TAIL), F32), jax.ShapeDtypeStruct((N_META, IN_TAIL), F32)],
        scratch_shapes=[pltpu.VMEM((8, 512), F32)],
        compiler_params=_cparams("arbitrary"),
    )(x2d, tgt2d, o, meta, norm_g, w_in_p, conv_w, ga, gc, gmat, w_out, gf)


def _gate_bwd(dycat, o, p, pm, conv_w, ga, gc, gmat, nb, s, tm):
    nt = s // tm
    r = nb * s
    ext = tm + 8
    prev_idx = lambda i: jnp.maximum(i * (tm // 8) - 1, 0)
    next_idx = lambda i: jnp.minimum((i + 1) * (tm // 8), r // 8 - 1)

    def body(dya_ref, dyc_ref, dycn_ref, o_ref, za_ref, cb_ref, cbn_ref, cc_ref, ccp_ref, ccn_ref,
             ch_ref, chp_ref, chn_ref, zc_ref, zcn_ref, mc_ref, mh_ref, cw_ref, ga_ref, gc_ref, gm_ref,
             dpb_ref, do_ref, dl_ref, dccm_ref, dga_ref, dgc_ref, dcw_ref):
        i = pl.program_id(0)

        @pl.when(i == 0)
        def _():
            dga_ref[...] = jnp.zeros_like(dga_ref)
            dgc_ref[...] = jnp.zeros_like(dgc_ref)
            dcw_ref[...] = jnp.zeros_like(dcw_ref)

        dga = []
        for h in range(HEADS):
            hs = slice(VDIM * h, VDIM * (h + 1))
            oh, za, gah, dya = o_ref[0, h], za_ref[:, hs], ga_ref[:, hs], dya_ref[:, hs]
            sg = _sigmoid(za)
            on, ro = _rms(oh, gah)
            don = dya * (za * sg)
            dpb_ref[:, hs] = (dya * on * (sg * (1.0 + za * (1.0 - sg)))).astype(BF16)
            do, dg = _rms_bwd(don, oh, ro, gah)
            dga.append(jnp.sum(dg, axis=0, keepdims=True))
            dob = do.astype(BF16)
            do_ref[0, h] = dob
            dl_ref[0, h] = _row_of(jnp.sum(dob.astype(F32) * oh, axis=1, keepdims=True), tm)
        dga_ref[...] += jnp.concatenate(dga, axis=1)

        cat = lambda a, b: jnp.concatenate([a[...], b[...]], axis=0)
        cch = cat(cc_ref, ccn_ref)
        chh = cat(ch_ref, chn_ref)
        cb = cat(cb_ref, cbn_ref)
        zc = cat(zc_ref, zcn_ref)
        dy = cat(dyc_ref, dycn_ref)
        first = i % nt == 0
        last = i % nt == nt - 1
        cc = cch * chh
        prev = jnp.where(first, mc_ref[8:16, :] * mh_ref[8:16, :], ccp_ref[...] * chp_ref[...])
        cc1, cc2 = _shift_rows(cc, prev, ext)
        w0, w1, w2 = cw_ref[0:1, :], cw_ref[1:2, :], cw_ref[2:3, :]
        dw = w0 * cc2 + w1 * cc1 + w2 * cc
        yc = cb * dw
        rg = lax.rsqrt(_group_mean(yc * yc, gm_ref[...]) + EPS)
        ych = yc * rg
        gcv = gc_ref[...]
        sg = _sigmoid(zc)
        dycn = dy * (zc * sg)
        dzc = dy * (ych * gcv) * (sg * (1.0 + zc * (1.0 - sg)))
        dgc_ref[...] += jnp.sum((dycn * ych)[:tm], axis=0, keepdims=True)
        dycg = dycn * gcv
        dyc = rg * (dycg - ych * _group_mean(dycg * ych, gm_ref[...]))
        rid = lax.broadcasted_iota(jnp.int32, (ext, CONV_W), 0)
        ddw = jnp.where(jnp.logical_and(last, rid >= tm), 0.0, dyc * cb)
        dcb = dyc * dw
        dcc = w2 * ddw + w1 * pltpu.roll(ddw, ext - 1, 0) + w0 * pltpu.roll(ddw, ext - 2, 0)
        dpb_ref[:, 512:1024] = dcb[:tm].astype(BF16)
        dpb_ref[:, 1024:1536] = (dcc * chh)[:tm].astype(BF16)
        dpb_ref[:, 1536:2048] = (dcc * cch)[:tm].astype(BF16)
        dpb_ref[:, 2048:2560] = dzc[:tm].astype(BF16)
        rs = lambda a: jnp.sum(a[:tm], axis=0, keepdims=True)
        dcw_ref[0:1, :] += rs(ddw * cc2)
        dcw_ref[1:2, :] += rs(ddw * cc1)
        dcw_ref[2:3, :] += rs(ddw * cc)

        @pl.when(first)
        def _():
            d0, d1 = ddw[0:1, :], ddw[1:2, :]
            r8 = lax.broadcasted_iota(jnp.int32, (8, CONV_W), 0)
            dccm_ref[0] = jnp.where(r8 == 7, w1 * d0 + w0 * d1, jnp.where(r8 == 6, w0 * d0, 0.0))

    row = lambda j: pl.BlockSpec((tm, 512), lambda i: (i, j))
    prv = lambda j: pl.BlockSpec((8, 512), lambda i: (prev_idx(i), j))
    nxt = lambda j: pl.BlockSpec((8, 512), lambda i: (next_idx(i), j))
    mblk = lambda j: pl.BlockSpec((N_META, 512), lambda i: (0, j))
    full = lambda a: pl.BlockSpec(a.shape, lambda i: (0,) * a.ndim)
    hb = lambda w: pl.BlockSpec((1, HEADS, tm, w), lambda i: (i // nt, 0, i % nt, 0))
    acc = lambda rr: pl.BlockSpec((rr, 512), lambda i: (0, 0))
    return pl.pallas_call(
        body, name="gate_bwd", grid=(nb * nt,),
        in_specs=[row(0), row(1), nxt(1), hb(VDIM),
                  row(BLK_ZA), row(BLK_CB), nxt(BLK_CB), row(BLK_CC), prv(BLK_CC), nxt(BLK_CC),
                  row(BLK_CH), prv(BLK_CH), nxt(BLK_CH), row(BLK_ZC), nxt(BLK_ZC),
                  mblk(BLK_CC), mblk(BLK_CH), full(conv_w), full(ga), full(gc), full(gmat)],
        out_specs=[pl.BlockSpec((tm, 2560), lambda i: (i, 0)), hb(VDIM),
                   pl.BlockSpec((1, HEADS, 1, tm), lambda i: (i // nt, 0, 0, i % nt)),
                   pl.BlockSpec((1, 8, 512), lambda i: (i // nt, 0, 0)),
                   acc(1), acc(1), acc(8)],
        out_shape=[jax.ShapeDtypeStruct((r, 2560), BF16), jax.ShapeDtypeStruct((nb, HEADS, s, VDIM), BF16),
                   jax.ShapeDtypeStruct((nb, HEADS, 1, s), F32), jax.ShapeDtypeStruct((nb, 8, 512), F32),
                   jax.ShapeDtypeStruct((1, 512), F32), jax.ShapeDtypeStruct((1, 512), F32),
                   jax.ShapeDtypeStruct((8, 512), F32)],
        compiler_params=_cparams("arbitrary"),
    )(dycat, dycat, dycat, o, p, p, p, p, p, p, p, p, p, p, p, pm, pm, conv_w, ga, gc, gmat)


class _StagedReduce:
    LOC, PRE_S, PRE_R, ICI_S, ICI_R, POST_S, POST_R, OUT, N_SEM = 0, 1, 2, 3, 6, 9, 10, 11, 12

    def __init__(self, shard_shape):
        self.half = (shard_shape[0] // 2, shard_shape[1])

    def scratch(self):
        h = self.half
        return [pltpu.VMEM((4,) + h, F32), pltpu.VMEM((4,) + h, F32), pltpu.VMEM((4,) + h, BF16),
                pltpu.VMEM((3,) + h, BF16), pltpu.VMEM(h, F32), pltpu.SemaphoreType.DMA((self.N_SEM,))]

    def run(self, stage, pin, gout, scr):
        own, sib, wire, rbuf, fin, sems = scr
        r2 = self.half[0]
        x, y, c = lax.axis_index("x"), lax.axis_index("y"), lax.axis_index("c")
        mine = 2 * x + y
        sibling = (x, y, 1 - c)
        chips = [(1 - x, y), (x, 1 - y), (1 - x, 1 - y)]
        rows = lambda half: pl.ds(pl.multiple_of(half * r2, r2), r2)
        mesh = pl.DeviceIdType.MESH

        loc = pltpu.make_async_copy(pin.at[:, rows(c), :], own, sems.at[self.LOC])
        pre = pltpu.make_async_remote_copy(
            src_ref=pin.at[:, rows(1 - c), :], dst_ref=sib, send_sem=sems.at[self.PRE_S],
            recv_sem=sems.at[self.PRE_R], device_id=sibling, device_id_type=mesh)

        def ici(j):
            px, py = chips[j]
            return pltpu.make_async_remote_copy(
                src_ref=wire.at[2 * px + py], dst_ref=rbuf.at[j], send_sem=sems.at[self.ICI_S + j],
                recv_sem=sems.at[self.ICI_R + j], device_id=(px, py, c), device_id_type=mesh)

        def post(half):
            return pltpu.make_async_remote_copy(
                src_ref=fin, dst_ref=gout.at[rows(half), :], send_sem=sems.at[self.POST_S],
                recv_sem=sems.at[self.POST_R], device_id=sibling, device_id_type=mesh)

        keep = pltpu.make_async_copy(fin, gout.at[rows(c), :], sems.at[self.OUT])
        if stage == 0:
            loc.start()
            pre.start()
        elif stage == 1:
            loc.wait()
            pre.wait_recv()
            for blk in range(4):
                tot = own[blk] + sib[blk]
                own[blk] = tot
                wire[blk] = tot.astype(BF16)
            for j in range(3):
                ici(j).start()
        elif stage == 2:
            for j in range(3):
                ici(j).wait_recv()
            tot = own[mine]
            for j in range(3):
                tot = tot + rbuf[j].astype(F32)
            fin[...] = tot
            post(c).start()
            keep.start()
        else:
            post(1 - c).wait_recv()
            pre.wait_send()
            for j in range(3):
                ici(j).wait_send()
            post(c).wait_send()
            keep.wait()


def _attn_bwd(q, k, v, do, lse, delta, km, vm, early, nb, s, t):
    n = s // t
    ne = len(early)
    reds = [_StagedReduce(a.shape[1:]) for a in early]
    n_steps = HEADS * nb
    assert n_steps >= 4

    def body(q_ref, k_ref, v_ref, do_ref, lse_ref, dl_ref, km_ref, vm_ref, *rest):
        pin_refs, rest = rest[:ne], rest[ne:]
        dq_ref, dk_ref, dv_ref, dkm_ref, dvm_ref = rest[:5]
        gout_refs, (p_scr, ds_scr, dq_acc), red_scr = rest[5:5 + ne], rest[5 + ne:8 + ne], rest[8 + ne:]
        b = pl.program_id(1)
        step = pl.program_id(0) * nb + b
        for stage, at in enumerate((0, 1, n_steps - 2, n_steps - 1)):
            @pl.when(step == at)
            def _(stage=stage):
                for a, red in enumerate(reds):
                    red.run(stage, pin_refs[a], gout_refs[a], red_scr[6 * a:6 * a + 6])

        @pl.when(b == 0)
        def _():
            dkm_ref[...] = jnp.zeros_like(dkm_ref)
            dvm_ref[...] = jnp.zeros_like(dvm_ref)

        kr = lax.broadcasted_iota(jnp.int32, (t, t), 0)
        qc = lax.broadcasted_iota(jnp.int32, (t, t), 1)
        km_v, vm_v = km_ref[0, 0], vm_ref[0, 0]
        ptm = jnp.exp(_dot_nt(km_v, q_ref[0, 0]) - lse_ref[0, 0])
        dstm = (ptm * (_dot_nt(vm_v, do_ref[0, 0]) - dl_ref[0, 0])).astype(BF16)
        dkm_ref[0] += _dot(dstm, q_ref[0, 0])
        dvm_ref[0] += _dot(ptm.astype(BF16), do_ref[0, 0])
        dq_acc[...] = _dot_tn(dstm, km_v)
        def tiles(j):
            slot = j % 2
            kj = k_ref[0, 0, j * t:(j + 1) * t, :]
            vj = v_ref[0, 0, j * t:(j + 1) * t, :]
            def products(i):
                cs = slice(i * t, (i + 1) * t)
                return _dot_nt(kj, q_ref[0, 0, cs, :]), _dot_nt(vj, do_ref[0, 0, cs, :])

            nxt, pending = products(j), None
            for i in range(j, n):
                cs = slice(i * t, (i + 1) * t)
                st, dpt = nxt
                if i + 1 < n:
                    nxt = products(i + 1)
                if i == j:
                    st = jnp.where(kr <= qc, st, NEG_INF)
                pt = jnp.exp(st - lse_ref[0, 0, :, cs])
                dst = (pt * (dpt - dl_ref[0, 0, :, cs])).astype(BF16)
                p_scr[slot, :, cs] = pt.astype(BF16)
                ds_scr[slot, :, cs] = dst
                if pending is not None:
                    dq_acc[pending[0], :] += _dot_tn(pending[1], kj)
                pending = (cs, dst)
            dq_acc[pending[0], :] += _dot_tn(pending[1], kj)

        for j in range(n):
            slot = j % 2
            tiles(j)
            dv_ref[0, 0, j * t:(j + 1) * t, :] = _dot(p_scr[slot, :, j * t:s], do_ref[0, 0, j * t:s, :]).astype(BF16)
            dk_ref[0, 0, j * t:(j + 1) * t, :] = _dot(ds_scr[slot, :, j * t:s], q_ref[0, 0, j * t:s, :]).astype(BF16)
        dq_ref[0, 0] = dq_acc[...].astype(BF16)

    big = lambda w: pl.BlockSpec((1, 1, s, w), lambda h, b: (b, h, 0, 0))
    rowv = pl.BlockSpec((1, 1, 1, s), lambda h, b: (b, h, 0, 0))
    mk = lambda w: pl.BlockSpec((1, 1, N_META, w), lambda h, b: (0, h, 0, 0))
    mo = lambda w: pl.BlockSpec((1, N_META, w), lambda h, b: (h, 0, 0))
    return pl.pallas_call(
        body, name="attn_bwd", grid=(HEADS, nb),
        in_specs=[big(QK_PAD), big(QK_PAD), big(VDIM), big(VDIM), rowv, rowv, mk(QK_PAD), mk(VDIM)]
        + [pl.BlockSpec(memory_space=pl.ANY)] * ne,
        out_specs=[big(QK_PAD), big(QK_PAD), big(VDIM), mo(QK_PAD), mo(VDIM)]
        + [pl.BlockSpec(memory_space=pl.ANY)] * ne,
        out_shape=[jax.ShapeDtypeStruct((nb, HEADS, s, QK_PAD), BF16),
                   jax.ShapeDtypeStruct((nb, HEADS, s, QK_PAD), BF16),
                   jax.ShapeDtypeStruct((nb, HEADS, s, VDIM), BF16),
                   jax.ShapeDtypeStruct((HEADS, N_META, QK_PAD), F32),
                   jax.ShapeDtypeStruct((HEADS, N_META, VDIM), F32)]
        + [jax.ShapeDtypeStruct(a.shape[1:], F32) for a in early],
        scratch_shapes=[pltpu.VMEM((2, t, s), BF16), pltpu.VMEM((2, t, s), BF16), pltpu.VMEM((s, QK_PAD), F32)]
        + [sc for red in reds for sc in red.scratch()],
        compiler_params=_cparams("arbitrary", "arbitrary"),
    )(q, k, v, do, lse, delta, km, vm, *early)


def _up_bwd(dq, dk, dv, dkm, dvm, p, pm, tabs, tabs_m, wq_p, wkv_p, gq, gkv, nb, s, tm):
    nt = s // tm
    n = nb * nt
    c_t, sa_t, sb_t = tabs
    cm_t, sam_t, sbm_t = tabs_m

    def kv_path(dkh, dvh, pa, c, sa, sb, wkv, gkvv):
        dkpe = dkh[0][:, NOPE:]
        for h in range(1, HEADS):
            dkpe = dkpe + dkh[h][:, NOPE:]
        dkr = _rope_bwd(dkpe, c, sa, sb)
        dkv = jnp.concatenate([d[:, :NOPE] for d in dkh] + list(dvh), axis=1).astype(BF16)
        ckv = pa[:, Q_RANK:Q_RANK + KV_RANK]
        kvn, rkv = _rms(ckv, gkvv)
        dckv, dg = _rms_bwd(_dot(dkv, wkv), ckv, rkv, gkvv)
        return dckv, dkr, kvn.astype(BF16), dkv, jnp.sum(dg, axis=0, keepdims=True)

    def body(dq_ref, dk_ref, dv_ref, pa_ref, c_ref, sa_ref, sb_ref,
             dkm_ref, dvm_ref, pam_ref, cm_ref, sam_ref, sbm_ref,
             wq_ref, wkv_ref, gq_ref, gkv_ref,
             dpa_ref, dpam_ref, pq_ref, pkv_ref, dgq_ref, dgkv_ref, dwq_ref, dwkv_ref):
        i = pl.program_id(0)

        @pl.when(i == 0)
        def _():
            dwq_ref[...] = jnp.zeros_like(dwq_ref)
            dwkv_ref[...] = jnp.zeros_like(dwkv_ref)
            dgq_ref[...] = jnp.zeros_like(dgq_ref)
            dgkv_ref[...] = jnp.zeros_like(dgkv_ref)

        @pl.when(i < n)
        def _():
            c, sa, sb = c_ref[...], sa_ref[...], sb_ref[...]
            pa = pa_ref[...]
            parts = []
            for h in range(HEADS):
                dqh = dq_ref[0, h].astype(F32) * ATTN_SCALE
                parts += [dqh[:, :NOPE], _rope_bwd(dqh[:, NOPE:], c, sa, sb)]
            dql = jnp.concatenate(parts, axis=1).astype(BF16)
            cq = pa[:, 0:Q_RANK]
            gqv = gq_ref[...]
            qn, rq = _rms(cq, gqv)
            dwq_ref[...] += _dot_tn(dql, qn.astype(BF16))
            dcq, dg = _rms_bwd(_dot(dql, wq_ref[...]), cq, rq, gqv)
            dgq_ref[...] += jnp.sum(dg, axis=0, keepdims=True)
            dckv, dkr, kvn, dkv, dgk = kv_path([dk_ref[0, h].astype(F32) for h in range(HEADS)],
                                               [dv_ref[0, h].astype(F32) for h in range(HEADS)],
                                               pa, c, sa, sb, wkv_ref[...], gkv_ref[...])
            dwkv_ref[...] += _dot_tn(dkv, kvn)
            dgkv_ref[...] += dgk
            dpa_ref[...] = jnp.concatenate([dcq, dckv, dkr], axis=1).astype(BF16)

        @pl.when(i == n)
        def _():
            dckv, dkr, kvn, dkv, dgk = kv_path([dkm_ref[h] for h in range(HEADS)],
                                               [dvm_ref[h] for h in range(HEADS)],
                                               pam_ref[...], cm_ref[...], sam_ref[...], sbm_ref[...],
                                               wkv_ref[...], gkv_ref[...])
            dwkv_ref[...] += _dot_tn(dkv, kvn)
            dgkv_ref[...] += dgk
            dpam_ref[...] = jnp.concatenate([jnp.zeros((N_META, Q_RANK), F32), dckv, dkr], axis=1)
            for h in range(HEADS):
                pq_ref[h] = dwq_ref[QK_PAD * h:QK_PAD * h + NOPE + ROPE, :]
                pkv_ref[h, 0:NOPE, :] = dwkv_ref[NOPE * h:NOPE * (h + 1), :]
                pkv_ref[h, NOPE:NOPE + VDIM, :] = dwkv_ref[512 + VDIM * h:512 + VDIM * (h + 1), :]

    cl = lambda i: jnp.minimum(i, n - 1)
    hb = lambda w: pl.BlockSpec((1, HEADS, tm, w), lambda i: (cl(i) // nt, 0, cl(i) % nt, 0))
    tab = pl.BlockSpec((tm, 128), lambda i: (cl(i) % nt, 0))
    full = lambda a: pl.BlockSpec(a.shape, lambda i: (0,) * a.ndim)
    const = lambda shape: pl.BlockSpec(shape, lambda i: (0,) * len(shape))
    return pl.pallas_call(
        body, name="up_bwd", grid=(n + 1,),
        in_specs=[hb(QK_PAD), hb(QK_PAD), hb(VDIM), pl.BlockSpec((tm, 512), lambda i: (cl(i), 0)), tab, tab, tab,
                  full(dkm), full(dvm), pl.BlockSpec((N_META, 512), lambda i: (0, 0)),
                  full(cm_t), full(sam_t), full(sbm_t), full(wq_p), full(wkv_p), full(gq), full(gkv)],
        out_specs=[pl.BlockSpec((tm, 512), lambda i: (cl(i), 0)), const((N_META, 512)),
                   const((HEADS, NOPE + ROPE, Q_RANK)), const((HEADS, NOPE + VDIM, KV_RANK)),
                   const((1, Q_RANK)), const((1, KV_RANK))],
        out_shape=[jax.ShapeDtypeStruct((nb * s, 512), BF16), jax.ShapeDtypeStruct((N_META, 512), F32),
                   jax.ShapeDtypeStruct((HEADS, NOPE + ROPE, Q_RANK), F32),
                   jax.ShapeDtypeStruct((HEADS, NOPE + VDIM, KV_RANK), F32),
                   jax.ShapeDtypeStruct((1, Q_RANK), F32), jax.ShapeDtypeStruct((1, KV_RANK), F32)],
        scratch_shapes=[pltpu.VMEM((HEADS * QK_PAD, Q_RANK), F32), pltpu.VMEM((1024, KV_RANK), F32)],
        compiler_params=_cparams("arbitrary"),
    )(dq, dk, dv, p, c_t, sa_t, sb_t, dkm, dvm, pm, cm_t, sam_t, sbm_t, wq_p, wkv_p, gq, gkv)


def _in_bwd(x2d, dh2, dpa, dpb, meta, dpam, dccm, pm, w_in_p, norm_g, nb, s, tm):
    nt = s // tm
    n = nb * nt

    def body(x_ref, dh_ref, dpa_ref, dpb_ref, mt_ref, dpam_ref, dccm_ref, mc_ref, mh_ref, w_ref, g_ref,
             gx_ref, gm_ref, dw_hbm, dg_ref, acc_ref, sems):
        i = pl.program_id(0)

        @pl.when(i == 0)
        def _():
            acc_ref[...] = jnp.zeros_like(acc_ref)
            dg_ref[...] = jnp.zeros_like(dg_ref)

        def rows(x, dp, dres):
            g = g_ref[...]
            dpb16 = dp.astype(BF16)
            du = _dot(dpb16, w_ref[...])
            u, r1 = _rms(x, g)
            acc_ref[...] += _dot_tn(dpb16, u.astype(BF16))
            dx, dg = _rms_bwd(du, x, r1, g)
            dg_ref[...] += jnp.sum(dg, axis=0, keepdims=True)
            return dx if dres is None else dx + dres

        @pl.when(i < n)
        def _():
            dp = jnp.concatenate([dpa_ref[...], dpb_ref[...]], axis=1)
            gx_ref[...] = rows(x_ref[...], dp, dh_ref[...])

        @pl.when(i == n)
        def _():
            dcc = dccm_ref[0]
            for b in range(1, nb):
                dcc = dcc + dccm_ref[b]
            z8 = jnp.zeros((8, CONV_W), F32)
            dc = jnp.concatenate([z8, dcc * mh_ref[8:16, :]], axis=0)
            dh = jnp.concatenate([z8, dcc * mc_ref[8:16, :]], axis=0)
            z = jnp.zeros((N_META, CONV_W), F32)
            dp = jnp.concatenate([dpam_ref[...], z, z, dc, dh, z], axis=1)
            gm_ref[...] = rows(mt_ref[...], dp, None)
            per = IN_DIM // 4
            cps = [pltpu.make_async_copy(acc_ref.at[0:448], dw_hbm.at[0, 0:448], sems.at[0]),
                   pltpu.make_async_copy(acc_ref.at[512:per + 64], dw_hbm.at[0, 448:per], sems.at[1])]
            for qq in range(1, 4):
                cps.append(pltpu.make_async_copy(acc_ref.at[per * qq + 64:per * (qq + 1) + 64], dw_hbm.at[qq],
                                                 sems.at[qq + 1]))
            for cp in cps:
                cp.start()
            for cp in cps:
                cp.wait()

    cl = lambda i: jnp.minimum(i, n - 1)
    row = lambda w: pl.BlockSpec((tm, w), lambda i: (cl(i), 0))
    full = lambda a: pl.BlockSpec(a.shape, lambda i: (0,) * a.ndim)
    mblk = lambda j: pl.BlockSpec((N_META, 512), lambda i: (0, j))
    return pl.pallas_call(
        body, name="in_bwd", grid=(n + 1,),
        in_specs=[row(D_MODEL), row(D_MODEL), row(512), row(2560), full(meta), full(dpam), full(dccm),
                  mblk(BLK_CC), mblk(BLK_CH), full(w_in_p), full(norm_g)],
        out_specs=[row(D_MODEL), pl.BlockSpec((N_META, D_MODEL), lambda i: (0, 0)),
                   pl.BlockSpec(memory_space=pl.ANY), pl.BlockSpec((1, D_MODEL), lambda i: (0, 0))],
        out_shape=[jax.ShapeDtypeStruct((nb * s, D_MODEL), F32), jax.ShapeDtypeStruct((N_META, D_MODEL), F32),
                   jax.ShapeDtypeStruct((4, IN_DIM // 4, D_MODEL), F32), jax.ShapeDtypeStruct((1, D_MODEL), F32)],
        scratch_shapes=[pltpu.VMEM((IN_PAD, D_MODEL), F32), pltpu.SemaphoreType.DMA((5,))],
        compiler_params=_cparams("arbitrary"),
    )(x2d, dh2, dpa, dpb, meta, dpam, dccm, pm, pm, w_in_p, norm_g)


def _gather_weights(w_in_shard, w_out_shard, split, pieces, out_rows, whole, zero_fills):
    ns, nw, nz = len(split), len(whole), len(zero_fills)
    flat = [(a, pc) for a in range(ns) for pc in pieces[a]]
    nk = len(flat)
    hh = HEAD_ROWS // 2
    assert hh % 16 == 0

    def body(*refs):
        ins, wins, zins = refs[2:2 + ns], refs[2 + ns:2 + ns + nw], refs[2 + ns + nw:2 + ns + nw + nz]
        n_in = 2 + ns + nw + nz
        head_ref, shard16, w_out16 = refs[n_in:n_in + 3]
        outs, wcat = refs[n_in + 3:n_in + 3 + ns], refs[n_in + 3 + ns:n_in + 3 + ns + nw]
        scr = refs[n_in + 3 + ns + nw:]
        stage, wouts = scr[:ns], scr[ns:ns + nw]
        (send_sems, recv_sems, fwd_send, fwd_recv, loc_sems, w_send, w_recv, w_loc, z_sems,
         h_send, h_recv, h_relay, h_pass) = scr[ns + nw:]
        x, y, c = lax.axis_index("x"), lax.axis_index("y"), lax.axis_index("c")
        mine = 2 * x + y
        chips = [(1 - x, y), (x, 1 - y), (1 - x, 1 - y)]
        chip_of = [2 * px + py for px, py in chips]
        shard16[...] = refs[0][...].astype(BF16)
        w_out16[...] = refs[1][...].astype(BF16)
        for a in range(ns):
            stage[a][...] = ins[a][...].astype(BF16)

        def head_rows(half):
            return pl.ds(pl.multiple_of(half * hh, 16), hh)

        def head_copy(turn):
            dest = (1 - c, c, c) if turn == 0 else (c, 1 - c, c)
            return pltpu.make_async_remote_copy(
                src_ref=shard16.at[head_rows(c)], dst_ref=head_ref.at[head_rows(c)], send_sem=h_send.at[turn],
                recv_sem=h_recv.at[0], device_id=dest, device_id_type=pl.DeviceIdType.MESH)

        def head_relay():
            ref = head_ref.at[head_rows(c)]
            return pltpu.make_async_remote_copy(
                src_ref=ref, dst_ref=ref, send_sem=h_relay.at[0], recv_sem=h_recv.at[0],
                device_id=(1, 1, c), device_id_type=pl.DeviceIdType.MESH)

        def head_pass(half):
            ref = head_ref.at[head_rows(half)]
            return pltpu.make_async_remote_copy(
                src_ref=ref, dst_ref=ref, send_sem=h_pass.at[0], recv_sem=h_pass.at[1],
                device_id=(x, y, 1 - c), device_id_type=pl.DeviceIdType.MESH)

        head_ref[HEAD_ROWS:IN_HEAD, :] = jnp.zeros((IN_HEAD - HEAD_ROWS, D_MODEL), BF16)

        @pl.when(mine == 0)
        def _():
            head_copy(0).start()
            head_ref[0:HEAD_ROWS, :] = shard16[0:HEAD_ROWS, :]

        def src(k):
            a, (s0, nr, _, _, _, _) = flat[k]
            return stage[a].at[s0:s0 + nr]

        def dst(k, q):
            a, (_, nr, per, first, rest, _) = flat[k]
            row = per * q + first + (rest - first) * jnp.minimum(q, 1)
            return outs[a].at[pl.ds(pl.multiple_of(row, 16), nr)]

        def ici(k, j, q):
            px, py = chips[j]
            return pltpu.make_async_remote_copy(
                src_ref=src(k), dst_ref=dst(k, q), send_sem=send_sems.at[k, j], recv_sem=recv_sems.at[k, j],
                device_id=(px, py, c), device_id_type=pl.DeviceIdType.MESH)

        def fwd(k, j):
            ref = dst(k, chip_of[j])
            return pltpu.make_async_remote_copy(
                src_ref=ref, dst_ref=ref, send_sem=fwd_send.at[k, j], recv_sem=fwd_recv.at[k, j],
                device_id=(x, y, 1 - c), device_id_type=pl.DeviceIdType.MESH)

        def wcopy(b, j, q):
            px, py = chips[j]
            return pltpu.make_async_remote_copy(
                src_ref=wins[b], dst_ref=wouts[b].at[q], send_sem=w_send.at[b, j], recv_sem=w_recv.at[b, j],
                device_id=(px, py, c), device_id_type=pl.DeviceIdType.MESH)

        local = [pltpu.make_async_copy(src(k), dst(k, mine), loc_sems.at[k]) for k in range(nk)]
        local += [pltpu.make_async_copy(wins[b], wouts[b].at[mine], w_loc.at[b]) for b in range(nw)]
        for z, (a, _, row0) in enumerate(zero_fills):
            local.append(pltpu.make_async_copy(zins[z], outs[a].at[row0:row0 + zins[z].shape[0]], z_sems.at[z]))
        wsends = [wcopy(b, j, mine) for b in range(nw) for j in range(3)]
        for cp in local + wsends:
            cp.start()

        for half in (0, 1):
            @pl.when(c == half)
            def _(half=half):
                my_k = [k for k in range(nk) if flat[k][1][5] == half]
                other_k = [k for k in range(nk) if flat[k][1][5] != half]
                sends = [ici(k, j, mine) for k in my_k for j in range(3)]
                for cp in sends:
                    cp.start()
                passed = []
                for k in my_k:
                    for j in range(3):
                        ici(k, j, chip_of[j]).wait_recv()
                        cp = fwd(k, j)
                        cp.start()
                        passed.append(cp)
                for k in other_k:
                    for j in range(3):
                        fwd(k, j).wait_recv()
                for cp in sends + passed:
                    cp.wait_send()

        for b in range(nw):
            for j in range(3):
                wcopy(b, j, chip_of[j]).wait_recv()
        for cp in wsends:
            cp.wait_send()
        for cp in local:
            cp.wait()
        for b in range(nw):
            cols = wins[b].shape[-1]
            for q in range(4):
                wcat[b][..., cols * q:cols * (q + 1)] = wouts[b][q]

        @pl.when(mine == 0)
        def _():
            head_copy(0).wait_send()
            head_copy(1).start()
            head_copy(1).wait_send()

        @pl.when(mine != 0)
        def _():
            hands_on = mine == 2 - c
            head_copy(0).wait_recv()

            @pl.when(hands_on)
            def _():
                head_relay().start()

            head_pass(c).start()
            head_pass(1 - c).wait_recv()
            head_pass(c).wait_send()

            @pl.when(hands_on)
            def _():
                head_relay().wait_send()

    vmem = pl.BlockSpec(memory_space=pltpu.VMEM)
    dma = pltpu.SemaphoreType.DMA
    zeros = [z for _, z, _ in zero_fills]
    return pl.pallas_call(
        body, name="gather_weights",
        in_specs=[vmem] * (2 + ns + nw + nz), out_specs=[vmem] * (3 + ns + nw),
        out_shape=([jax.ShapeDtypeStruct((IN_HEAD, D_MODEL), BF16), jax.ShapeDtypeStruct(w_in_shard.shape, BF16),
                    jax.ShapeDtypeStruct(w_out_shard.shape, BF16)]
                   + [jax.ShapeDtypeStruct((out_rows[a], split[a].shape[1]), BF16) for a in range(ns)]
                   + [jax.ShapeDtypeStruct(w.shape[:-1] + (4 * w.shape[-1],), w.dtype) for w in whole]),
        scratch_shapes=[pltpu.VMEM(a.shape, BF16) for a in split] + [pltpu.VMEM((4,) + w.shape, w.dtype) for w in whole]
        + [dma((nk, 3)), dma((nk, 3)), dma((nk, 3)), dma((nk, 3)), dma((nk,)),
           dma((nw, 3)), dma((nw, 3)), dma((nw,)), dma((nz,)),
           dma((2,)), dma((1,)), dma((1,)), dma((2,))],
        compiler_params=pltpu.CompilerParams(vmem_limit_bytes=VMEM_LIMIT),
    )(w_in_shard, w_out_shard, *split, *whole, *zeros)


def _reduce_grads(parts, small):
    n = len(parts)
    ns = len(small)
    shapes = [a.shape[1:] for a in parts]
    halves = [(sh[0] // 2, sh[1]) for sh in shapes]
    sm_blocks = [a.shape[1] // 128 for a, _ in small]
    sm_first = [sum(nr * nblk for (_, nr), nblk in zip(small[:k], sm_blocks[:k])) for k in range(ns)]
    sm_rows = -(-(sm_first[-1] + small[-1][1] * sm_blocks[-1]) // 8) * 8
    sm_shape = (sm_rows, 128)

    def body(*refs):
        pin, sm_in = refs[:n], refs[n:n + ns]
        gout, sm_out = refs[n + ns:2 * n + ns], refs[2 * n + ns]
        scr = refs[2 * n + ns + 1:]
        own, sib, wire, rbuf = scr[:n], scr[n:2 * n], scr[2 * n:3 * n], scr[3 * n:4 * n]
        (sbuf, send_sems, recv_sems, loc_sems, pre_send, pre_recv, post_send, post_recv,
         sm_send, sm_recv, sm_pack) = scr[4 * n:]
        x, y, c = lax.axis_index("x"), lax.axis_index("y"), lax.axis_index("c")
        mine = 2 * x + y
        sm_pack[...] = jnp.zeros(sm_shape, F32)
        for k, (_, nr) in enumerate(small):
            for i in range(nr):
                for j in range(sm_blocks[k]):
                    row = sm_first[k] + i * sm_blocks[k] + j
                    sm_pack[row:row + 1, :] = sm_in[k][i:i + 1, 128 * j:128 * (j + 1)]
        me = 4 * x + 2 * y + c
        sibling = (x, y, 1 - c)

        def rows(a, half):
            r2 = halves[a][0]
            return pl.ds(pl.multiple_of(half * r2, r2), r2)

        near = (jnp.where(c == 0, 1 - x, x), jnp.where(c == 0, y, 1 - y))
        far = (jnp.where(c == 0, x, 1 - x), jnp.where(c == 0, 1 - y, y))
        chip = lambda p: 2 * p[0] + p[1]
        blocks = [3 - mine, chip(near), chip(far), mine]

        def pre(a, k):
            q = blocks[k]
            return pltpu.make_async_remote_copy(
                src_ref=pin[a].at[q, rows(a, 1 - c), :], dst_ref=sib[a].at[q],
                send_sem=pre_send.at[a, q], recv_sem=pre_recv.at[a, q], device_id=sibling,
                device_id_type=pl.DeviceIdType.MESH)

        def ici(a, m):
            px, py = near if m < 2 else far
            return pltpu.make_async_remote_copy(
                src_ref=wire[a].at[blocks[m]], dst_ref=rbuf[a].at[m], send_sem=send_sems.at[a, m],
                recv_sem=recv_sems.at[a, m], device_id=(px, py, c), device_id_type=pl.DeviceIdType.MESH)

        def post(a, half):
            ref = gout[a].at[rows(a, half), :]
            return pltpu.make_async_remote_copy(
                src_ref=ref, dst_ref=ref, send_sem=post_send.at[a], recv_sem=post_recv.at[a],
                device_id=sibling, device_id_type=pl.DeviceIdType.MESH)

        def small_copy(kk):
            peer = (x ^ (kk >> 2), y ^ ((kk >> 1) & 1), c ^ (kk & 1))
            return pltpu.make_async_remote_copy(
                src_ref=sm_pack, dst_ref=sbuf.at[kk], send_sem=sm_send.at[kk - 1], recv_sem=sm_recv.at[kk - 1],
                device_id=peer, device_id_type=pl.DeviceIdType.MESH)

        local = [[pltpu.make_async_copy(pin[a].at[blocks[k], rows(a, c), :], own[a].at[blocks[k]], loc_sems.at[a, k])
                  for k in range(4)] for a in range(n)]
        pres = [[pre(a, k) for k in range(4)] for a in range(n)]
        smalls = [small_copy(kk) for kk in range(1, 8)]
        for a in range(n):
            for k in range(4):
                local[a][k].start()
                pres[a][k].start()
        for cp in smalls:
            cp.start()
        sbuf[0] = sm_pack[...]
        sends = []
        for a in range(n):
            for k in range(4):
                local[a][k].wait()
                pres[a][k].wait_recv()
                tot = own[a][blocks[k]] + sib[a][blocks[k]]
                if k == 2:
                    ici(a, 0).wait_recv()
                    tot = tot + rbuf[a][0].astype(F32)
                own[a][blocks[k]] = tot
                if k < 3:
                    wire[a][blocks[k]] = tot.astype(BF16)
                    cp = ici(a, k)
                    cp.start()
                    sends.append(cp)
        for cp in smalls:
            cp.wait_recv()
        total = sbuf[me]
        for d in range(1, 8):
            total = total + sbuf[me ^ d]
        sm_out[...] = total
        posts = []
        for a in range(n):
            fin = own[a][mine]
            for m in (1, 2):
                ici(a, m).wait_recv()
                fin = fin + rbuf[a][m].astype(F32)
            gout[a][rows(a, c), :] = fin
            cp = post(a, c)
            cp.start()
            posts.append(cp)
        for a in range(n):
            post(a, 1 - c).wait_recv()
        for cp in [cp for row in pres for cp in row] + sends + smalls + posts:
            cp.wait_send()

    vmem = pl.BlockSpec(memory_space=pltpu.VMEM)
    dma = pltpu.SemaphoreType.DMA
    return pl.pallas_call(
        body, name="reduce_grads",
        in_specs=[pl.BlockSpec(memory_space=pl.ANY)] * n + [vmem] * ns, out_specs=[vmem] * (n + 1),
        out_shape=[jax.ShapeDtypeStruct(sh, F32) for sh in shapes] + [jax.ShapeDtypeStruct(sm_shape, F32)],
        scratch_shapes=([pltpu.VMEM((4,) + hs, F32) for hs in halves] + [pltpu.VMEM((4,) + hs, F32) for hs in halves]
                        + [pltpu.VMEM((4,) + hs, BF16) for hs in halves]
                        + [pltpu.VMEM((3,) + hs, BF16) for hs in halves]
                        + [pltpu.VMEM((8,) + sm_shape, F32), dma((n, 3)), dma((n, 3)), dma((n, 4)),
                           dma((n, 4)), dma((n, 4)), dma((n,)), dma((n,)), dma((7,)), dma((7,)),
                           pltpu.VMEM(sm_shape, F32)]),
        compiler_params=pltpu.CompilerParams(vmem_limit_bytes=VMEM_LIMIT),
    )(*parts, *[a for a, _ in small])


def _adamw_update(w_ref, g_ref, m_ref, v_ref, d_ref, nm_ref, nv_ref):
    gv = g_ref[...]
    nm = ADAM_B1 * m_ref[...] + (1.0 - ADAM_B1) * gv
    nv = ADAM_B2 * v_ref[...] + (1.0 - ADAM_B2) * (gv * gv)
    m_hat = nm / (1.0 - ADAM_B1 ** ADAM_STEP)
    v_hat = nv / (1.0 - ADAM_B2 ** ADAM_STEP)
    d_ref[...] = -ADAM_LR * (m_hat / (jnp.sqrt(v_hat) + ADAM_EPS) + ADAM_WD * w_ref[...])
    nm_ref[...] = nm
    nv_ref[...] = nv


def _adamw_small(ws, gs, ms, vs):
    k = len(ws)

    def body(*refs):
        ins, outs = refs[:4 * k], refs[4 * k:]
        for a in range(k):
            _adamw_update(ins[a], ins[k + a], ins[2 * k + a], ins[3 * k + a], outs[a], outs[k + a], outs[2 * k + a])

    out = pl.pallas_call(
        body, name="adamw_small",
        out_shape=[jax.ShapeDtypeStruct(w.shape, F32) for w in ws] * 3,
        compiler_params=pltpu.CompilerParams(vmem_limit_bytes=VMEM_LIMIT),
    )(*ws, *gs, *ms, *vs)
    return out[:k], out[k:2 * k], out[2 * k:]


def _adamw(w, g, m, v, name):
    shape = w.shape
    w2, g2, m2, v2 = (a.reshape((-1, shape[-1])) for a in (w, g, m, v))

    def body(w_ref, g_ref, m_ref, v_ref, d_ref, nm_ref, nv_ref):
        _adamw_update(w_ref, g_ref, m_ref, v_ref, d_ref, nm_ref, nv_ref)

    rows, cols = w2.shape
    nblk = cols // 256 if cols % 256 == 0 and rows >= 64 else 1
    blk = pl.BlockSpec((rows, cols // nblk), lambda j: (0, j))
    out = pl.pallas_call(
        body, name=name, grid=(nblk,), in_specs=[blk] * 4, out_specs=[blk] * 3,
        out_shape=[jax.ShapeDtypeStruct(w2.shape, F32)] * 3,
        compiler_params=_cparams("parallel"),
    )(w2, g2, m2, v2)
    return tuple(a.reshape(shape) for a in out)


def kernel(x, meta_tokens, norm_g, w_in, q_norm_g, w_q_up, kv_norm_g, w_kv_up, conv_w, attn_out_g, conv_out_g, w_out, final_norm_g, loss_target, m_meta_tokens, m_norm_g, m_w_in, m_q_norm_g, m_w_q_up, m_kv_norm_g, m_w_kv_up, m_conv_w, m_attn_out_g, m_conv_out_g, m_w_out, m_final_norm_g, v_meta_tokens, v_norm_g, v_w_in, v_q_norm_g, v_w_q_up, v_kv_norm_g, v_w_kv_up, v_conv_w, v_attn_out_g, v_conv_out_g, v_w_out, v_final_norm_g):
    nb, s, _ = x.shape
    tm = min(ROW_TILE, s)
    ta = min(ATTN_TILE, s)
    assert s % tm == 0 and s % ta == 0 and tm % 16 == 0
    r = nb * s

    tr = lambda a: jnp.transpose(a[0])
    w_head, w_in_shard, w_out_shard, wq_p, wkv_p, g_cw, meta_f = _gather_weights(
        tr(w_in), w_out[0], [tr(w_q_up), tr(w_kv_up)],
        [W_Q_PIECES, W_KV_PIECES], [HEADS * QK_PAD, 1024],
        [jnp.transpose(conv_w, (1, 0, 2)), meta_tokens],
        [(0, jnp.zeros((64, Q_RANK), BF16), QK_PAD * h + NOPE + ROPE) for h in range(HEADS)])
    conv_f = g_cw.reshape(3, CONV_W)

    c_all, sa_all, sb_all = _rope_tables(N_META + s)
    tabs_m = (c_all[:N_META], sa_all[:N_META], sb_all[:N_META])
    tabs = (c_all[N_META:], sa_all[N_META:], sb_all[N_META:])
    gid = np.arange(CONV_W) // CONV_GROUP
    gmat = jnp.asarray(np.where(gid[:, None] == gid[None, :], 1.0 / CONV_GROUP, 0.0), BF16)
    ga, gc = attn_out_g, conv_out_g
    gf = final_norm_g.reshape(1, D_MODEL)

    x2d = x.reshape(r, D_MODEL)
    tgt2d = loss_target.reshape(r, D_MODEL)

    ph, q, k, v, pmh, km, vm, w_out_f, w_in_part = _fwd_proj(
        x2d, meta_f, tabs, tabs_m, norm_g, w_head, q_norm_g, wq_p, kv_norm_g, wkv_p, w_out_shard, w_in_shard,
        nb, s, tm)
    o, lse, w_in_p = _attn_fwd(q, k, v, km, vm, w_in_shard, w_in_part, nb, s, ta)
    dh2, dycat, dw_out, dgf, loss_acc, pt, pmt = _out_fwd_bwd(x2d, tgt2d, o, meta_f, norm_g, w_in_p, conv_f, ga, gc,
                                                              gmat, w_out_f, gf, nb, s, tm)
    dpb, do, delta, dccm, dga, dgc, dcw = _gate_bwd(dycat, o, pt, pmt, conv_f, ga, gc, gmat, nb, s, tm)
    p_out = dw_out.reshape(4, D_MODEL // 4, D_MODEL)
    dq, dk, dv, dkm, dvm, g_w_out = _attn_bwd(q, k, v, do, lse, delta, km, vm, [p_out], nb, s, ta)
    dpa, dpam, p_q, p_kv, dgq, dgkv = _up_bwd(dq, dk, dv, dkm, dvm, ph, pmh, tabs, tabs_m, wq_p, wkv_p,
                                              q_norm_g, kv_norm_g, nb, s, tm)
    gx, gmeta, p_in, dng = _in_bwd(x2d, dh2, dpa, dpb, meta_f, dpam, dccm, pmt, w_in_p, norm_g, nb, s, tm)

    g_w_in_t, g_w_q_t, g_w_kv_t, small_sum = _reduce_grads(
        [p_in, p_q, p_kv],
        [(dng, 1), (dgq, 1), (dgkv, 1), (dga, 1), (dgc, 1), (dgf, 1), (dcw, 3), (gmeta, N_META), (loss_acc, 1)])
    ssum = small_sum.reshape(-1)

    def take(off, n):
        return ssum[off:off + n], off + n

    off = 0
    g_norm, off = take(off, D_MODEL)
    g_qn, off = take(off, Q_RANK)
    g_kvn, off = take(off, KV_RANK)
    g_ga, off = take(off, CONV_W)
    g_gc, off = take(off, CONV_W)
    g_gf, off = take(off, D_MODEL)
    g_cw_all, off = take(off, 3 * CONV_W)
    g_meta_all, off = take(off, N_META * D_MODEL)
    loss = ssum[off]
    chip = 2 * lax.axis_index("x") + lax.axis_index("y")
    g_conv = lax.dynamic_slice(g_cw_all.reshape(3, CONV_W), (0, chip * 128), (3, 128))
    g_mt = lax.dynamic_slice(g_meta_all.reshape(N_META, D_MODEL), (0, chip * 256), (N_META, 256))

    grads = {
        "meta_tokens": g_mt, "norm_g": g_norm.reshape(1, -1), "w_in": g_w_in_t, "q_norm_g": g_qn.reshape(1, -1),
        "w_q_up": g_w_q_t, "kv_norm_g": g_kvn.reshape(1, -1), "w_kv_up": jnp.transpose(g_w_kv_t)[None],
        "conv_w": g_conv[None], "attn_out_g": g_ga.reshape(1, -1), "conv_out_g": g_gc.reshape(1, -1),
        "w_out": g_w_out[None], "final_norm_g": g_gf,
    }
    transposed = ("w_in", "w_q_up")
    weights = {
        "meta_tokens": (meta_tokens, m_meta_tokens, v_meta_tokens), "norm_g": (norm_g, m_norm_g, v_norm_g),
        "w_in": (w_in, m_w_in, v_w_in), "q_norm_g": (q_norm_g, m_q_norm_g, v_q_norm_g),
        "w_q_up": (w_q_up, m_w_q_up, v_w_q_up), "kv_norm_g": (kv_norm_g, m_kv_norm_g, v_kv_norm_g),
        "w_kv_up": (w_kv_up, m_w_kv_up, v_w_kv_up), "conv_w": (conv_w, m_conv_w, v_conv_w),
        "attn_out_g": (attn_out_g, m_attn_out_g, v_attn_out_g), "conv_out_g": (conv_out_g, m_conv_out_g, v_conv_out_g),
        "w_out": (w_out, m_w_out, v_w_out), "final_norm_g": (final_norm_g, m_final_norm_g, v_final_norm_g),
    }
    names = list(weights)
    small = [nme for nme in names if nme != "w_in"]

    def view(nme, a):
        if nme in transposed:
            return a if a.ndim == 2 else tr(a)
        if nme == "conv_w":
            return jnp.transpose(a.reshape(1, 3, -1), (1, 0, 2))
        if a.ndim == 3:
            return a[0]
        return a.reshape(1, -1) if a.ndim == 1 else a

    def unview(nme, a):
        if nme in transposed:
            return jnp.transpose(a)[None]
        if nme == "conv_w":
            return jnp.transpose(a, (1, 0, 2))
        return a.reshape(weights[nme][0].shape)

    res_small = _adamw_small(*[[view(nme, a) for nme, a in zip(small, col)] for col in (
        [weights[nme][0] for nme in small], [grads[nme] for nme in small],
        [weights[nme][1] for nme in small], [weights[nme][2] for nme in small])])
    w_, m_, v_ = weights["w_in"]
    res = _adamw(tr(w_), grads["w_in"], tr(m_), tr(v_), "adamw_w_in")
    upd = {"w_in": tuple(jnp.transpose(a)[None] for a in (grads["w_in"],) + res)}
    for j, nme in enumerate(small):
        upd[nme] = (unview(nme, view(nme, grads[nme])),) + tuple(unview(nme, r[j]) for r in res_small)
    grads = {nme: upd[nme][0] for nme in names}
    deltas, new_m, new_v = ([upd[nme][j] for nme in names] for j in (1, 2, 3))

    grad_x = gx.reshape(nb, s, D_MODEL)
    return (loss, grad_x, *[grads[nme] for nme in names], *deltas, *new_m, *new_v)
```

```python
import functools

import jax
import jax.numpy as jnp
import numpy as np
from jax import lax
from jax.experimental import pallas as pl
from jax.experimental.pallas import tpu as pltpu

F32 = jnp.float32
BF16 = jnp.bfloat16

D_MODEL = 1024
N_META = 16
HEADS = 4
NOPE = 128
ROPE = 64
VDIM = 128
QK_PAD = 256
Q_RANK = 256
KV_RANK = 128
CONV_W = 512
CONV_GROUP = 64
ROPE_THETA = 10000.0
EPS = 1e-6
ATTN_SCALE = (NOPE + ROPE) ** -0.5
IN_DIM = 3008
IN_PAD = 3072
HEAD_ROWS = Q_RANK + KV_RANK + ROPE
IN_HEAD = 512
IN_TAIL = IN_PAD - IN_HEAD
BLK_ZA, BLK_CB, BLK_CC, BLK_CH, BLK_ZC = 0, 1, 2, 3, 4
NEG_INF = -1e30

ADAM_LR = 0.001
ADAM_B1 = 0.9
ADAM_B2 = 0.999
ADAM_EPS = 1e-08
ADAM_WD = 0.01
ADAM_STEP = 10

ROW_TILE = 512
ATTN_TILE = 256
VMEM_LIMIT = 56 * 1024 * 1024

NT = (((1,), (1,)), ((), ()))
TN = (((0,), (0,)), ((), ()))


def _cparams(*sem):
    return pltpu.CompilerParams(dimension_semantics=sem, vmem_limit_bytes=VMEM_LIMIT)


def _dot(a, b):
    return jnp.dot(a, b, preferred_element_type=F32)


def _dot_nt(a, b):
    return lax.dot_general(a, b, NT, preferred_element_type=F32)


def _dot_tn(a, b):
    return lax.dot_general(a, b, TN, preferred_element_type=F32)


def _rms(x, g):
    r = lax.rsqrt(jnp.mean(x * x, axis=-1, keepdims=True) + EPS)
    return x * r * g, r


def _rms_bwd(dy, x, r, g):
    xh = x * r
    dyg = dy * g
    dx = r * (dyg - xh * jnp.mean(dyg * xh, axis=-1, keepdims=True))
    return dx, dy * xh


def _sigmoid(z):
    return 1.0 / (1.0 + jnp.exp(-z))


def _rope(b, c, sa, sb):
    return b * c + pltpu.roll(b, 96, 1) * sa + pltpu.roll(b, 32, 1) * sb


def _rope_bwd(d, c, sa, sb):
    return d * c + pltpu.roll(d * sa, 32, 1) + pltpu.roll(d * sb, 96, 1)


def _group_mean(x, gmat):
    hi = x.astype(BF16)
    lo = (x - hi.astype(F32)).astype(BF16)
    return _dot(hi, gmat) + _dot(lo, gmat)


def _row_of(col, rows):
    return jnp.transpose(jnp.broadcast_to(col, (rows, 128)))[0:1, :]


def _rope_tables(n_pos):
    half = ROPE // 2
    inv_freq = (np.float32(1.0) / (np.float32(ROPE_THETA) ** (np.arange(half, dtype=np.float32) / np.float32(half))))
    ang = np.arange(n_pos, dtype=np.float32)[:, None] * inv_freq.astype(np.float32)[None, :]
    cos, sin = np.cos(ang).astype(np.float32), np.sin(ang).astype(np.float32)
    z = np.zeros((n_pos, half), np.float32)
    c = np.concatenate([cos, cos, z, z], axis=1)
    sa = np.concatenate([-sin, z, z, z], axis=1)
    sb = np.concatenate([z, sin, z, z], axis=1)
    return jnp.asarray(c), jnp.asarray(sa), jnp.asarray(sb)


W_IN_PIECES_1 = ((0, 112, 752, 0, 64, 0), (384, 64, 752, 384, 448, 1), (448, 48, 752, 512, 512, 1))
W_IN_PIECES_2 = ((112, 272, 752, 112, 176, 0), (496, 256, 752, 560, 560, 1))
W_Q_PIECES = ((0, 96, 256, 0, 0, 0), (96, 96, 256, 96, 96, 1))
W_KV_PIECES = ((0, 128, 128, 0, 0, 0), (128, 128, 128, 512, 512, 1))
W_OUT_PIECES = ((0, 128, 256, 0, 0, 0), (128, 128, 256, 128, 128, 1))


class _StagedGather:
    STAGES = 4

    @staticmethod
    def steps(n_steps):
        return (0, 5 * n_steps // 8, 7 * n_steps // 8, n_steps - 1)

    def __init__(self, pieces, zero_rows=None):
        self.pieces = pieces
        self.zero_rows = zero_rows

    def scratch(self):
        nk, dma = len(self.pieces), pltpu.SemaphoreType.DMA
        return [dma((nk, 3)), dma((nk, 3)), dma((nk, 3)), dma((nk, 3)), dma((nk,))]

    def vmem_scratch(self, shard_shape, out_shape):
        return [pltpu.VMEM(shard_shape, BF16), pltpu.VMEM(out_shape, BF16),
                pltpu.SemaphoreType.DMA((4 * len(self.pieces) + 2,))] + self.scratch()

    def run_vmem(self, stage, shard_ref, out_ref, scr):
        src_scr, land_scr, io_sems = scr[:3]
        spans = []
        for _, nr, per, first, rest, _ in self.pieces:
            spans += [(per * q + (first if q == 0 else rest), nr) for q in range(4)]
        if self.zero_rows is not None:
            spans.append(self.zero_rows)
        flush = [pltpu.make_async_copy(land_scr.at[r0:r0 + nr], out_ref.at[r0:r0 + nr], io_sems.at[n])
                 for n, (r0, nr) in enumerate(spans)]
        if stage == 0:
            load = pltpu.make_async_copy(shard_ref, src_scr, io_sems.at[len(spans)])
            load.start()
            if self.zero_rows is not None:
                r0, nr = self.zero_rows
                land_scr[r0:r0 + nr, :] = jnp.zeros((nr, land_scr.shape[1]), BF16)
            load.wait()
        if stage < self.STAGES:
            self.run(stage, src_scr, land_scr, scr[3:])
        for cp in flush:
            if stage == self.STAGES - 1:
                cp.start()
            if stage == self.STAGES:
                cp.wait()

    def run(self, stage, src_ref, out_ref, scr):
        send_sems, recv_sems, fwd_send, fwd_recv, loc_sems = scr
        pieces = self.pieces
        nk = len(pieces)
        x, y, c = lax.axis_index("x"), lax.axis_index("y"), lax.axis_index("c")
        mine = 2 * x + y
        chips = [(1 - x, y), (x, 1 - y), (1 - x, 1 - y)]
        chip_of = [2 * px + py for px, py in chips]
        mesh = pl.DeviceIdType.MESH

        def src(k):
            s0, nr = pieces[k][0], pieces[k][1]
            return src_ref.at[s0:s0 + nr]

        def dst(k, q):
            _, nr, per, first, rest, _ = pieces[k]
            row = per * q + first + (rest - first) * jnp.minimum(q, 1)
            return out_ref.at[pl.ds(pl.multiple_of(row, 16), nr)]

        def ici(k, j, q):
            px, py = chips[j]
            return pltpu.make_async_remote_copy(
                src_ref=src(k), dst_ref=dst(k, q), send_sem=send_sems.at[k, j], recv_sem=recv_sems.at[k, j],
                device_id=(px, py, c), device_id_type=mesh)

        def fwd(k, j):
            ref = dst(k, chip_of[j])
            return pltpu.make_async_remote_copy(
                src_ref=ref, dst_ref=ref, send_sem=fwd_send.at[k, j], recv_sem=fwd_recv.at[k, j],
                device_id=(x, y, 1 - c), device_id_type=mesh)

        def relay(k, half):
            ref = dst(k, chip_of[half])
            px, py = chips[1 - half]
            return pltpu.make_async_remote_copy(
                src_ref=ref, dst_ref=ref, send_sem=send_sems.at[k, 2], recv_sem=recv_sems.at[k, 2],
                device_id=(px, py, c), device_id_type=mesh)

        local = [pltpu.make_async_copy(src(k), dst(k, mine), loc_sems.at[k]) for k in range(nk)]
        if stage == 0:
            for cp in local:
                cp.start()
        if stage == 3:
            for cp in local:
                cp.wait()
        for half in (0, 1):
            @pl.when(c == half)
            def _(half=half):
                my_k = [k for k in range(nk) if pieces[k][5] == half]
                other_k = [k for k in range(nk) if pieces[k][5] != half]
                for k in my_k:
                    if stage == 0:
                        for j in range(2):
                            ici(k, j, mine).start()
                    elif stage == 1:
                        for j in (half, 1 - half):
                            ici(k, j, chip_of[j]).wait_recv()
                            if j == half:
                                relay(k, half).start()
                            fwd(k, j).start()
                    elif stage == 2:
                        ici(k, 2, chip_of[2]).wait_recv()
                        fwd(k, 2).start()
                    else:
                        for j in range(2):
                            ici(k, j, mine).wait_send()
                        relay(k, half).wait_send()
                        for j in range(3):
                            fwd(k, j).wait_send()
                if stage == 3:
                    for k in other_k:
                        for j in range(3):
                            fwd(k, j).wait_recv()


def _fwd_proj(x2d, meta, tabs, tabs_m, norm_g, w_head, q_norm_g, wq_p, kv_norm_g, wkv_p, w_out_shard, w_in_shard,
              nb, s, tm):
    nt = s // tm
    n = nb * nt
    n_steps = n + 1
    c_t, sa_t, sb_t = tabs
    cm_t, sam_t, sbm_t = tabs_m
    gat = _StagedGather(W_OUT_PIECES)
    gat_in = _StagedGather(W_IN_PIECES_1)
    n_sems = len(gat.scratch())
    assert n_steps >= 3

    def body(x_ref, c_ref, sa_ref, sb_ref, mt_ref, cm_ref, sam_ref, sbm_ref,
             g_ref, w_ref, gq_ref, wq_ref, gkv_ref, wkv_ref, wos_ref, wis_ref,
             p_ref, q_ref, k_ref, v_ref, pm_ref, km_ref, vm_ref, wo_ref, wi_ref, *scr):
        gat_scr, gat_in_scr = scr[:n_sems], scr[n_sems:]
        i = pl.program_id(0)
        for stage, at in enumerate(_StagedGather.steps(n_steps)):
            @pl.when(i == at)
            def _(stage=stage):
                gat_in.run_vmem(stage, wis_ref, wi_ref, gat_in_scr)
                gat.run(stage, wos_ref, wo_ref, gat_scr)

        def project(xv, c, sa, sb, p_out, q_out, k_out, v_out):
            u, _ = _rms(xv, g_ref[...])
            p = _dot_nt(u.astype(BF16), w_ref[...])
            p_out[...] = p
            qn, _ = _rms(p[:, 0:Q_RANK], gq_ref[...])
            q = _dot_nt(qn.astype(BF16), wq_ref[...])
            kvn, _ = _rms(p[:, Q_RANK:Q_RANK + KV_RANK], gkv_ref[...])
            kv = _dot_nt(kvn.astype(BF16), wkv_ref[...])
            kpe = _rope(p[:, 384:512], c, sa, sb)
            for h in range(HEADS):
                if q_out is not None:
                    pe = _rope(q[:, QK_PAD * h + NOPE:QK_PAD * (h + 1)], c, sa, sb)
                    qh = jnp.concatenate([q[:, QK_PAD * h:QK_PAD * h + NOPE], pe], axis=1)
                    q_out[0, h] = (qh * ATTN_SCALE).astype(BF16)
                k_out[0, h] = jnp.concatenate([kv[:, NOPE * h:NOPE * (h + 1)], kpe], axis=1).astype(BF16)
                v_out[0, h] = kv[:, 512 + VDIM * h:512 + VDIM * (h + 1)].astype(BF16)

        @pl.when(i < n)
        def _():
            project(x_ref[...], c_ref[...], sa_ref[...], sb_ref[...], p_ref, q_ref, k_ref, v_ref)

        @pl.when(i == n)
        def _():
            project(mt_ref[...], cm_ref[...], sam_ref[...], sbm_ref[...], pm_ref, None, km_ref, vm_ref)
            gat_in.run_vmem(gat_in.STAGES, wis_ref, wi_ref, gat_in_scr)

    cl = lambda i: jnp.minimum(i, n - 1)
    full = lambda a: pl.BlockSpec(a.shape, lambda i: (0,) * a.ndim)
    const = lambda shape: pl.BlockSpec(shape, lambda i: (0,) * len(shape))
    tab = pl.BlockSpec((tm, 128), lambda i: (cl(i) % nt, 0))
    hb = lambda w: pl.BlockSpec((1, HEADS, tm, w), lambda i: (cl(i) // nt, 0, cl(i) % nt, 0))
    whole = pl.BlockSpec(memory_space=pl.ANY)
    return pl.pallas_call(
        body, name="fwd_proj", grid=(n_steps,),
        in_specs=[pl.BlockSpec((tm, D_MODEL), lambda i: (cl(i), 0)), tab, tab, tab,
                  full(meta), full(cm_t), full(sam_t), full(sbm_t),
                  full(norm_g), full(w_head), full(q_norm_g), full(wq_p), full(kv_norm_g), full(wkv_p), whole, whole],
        out_specs=[pl.BlockSpec((tm, IN_HEAD), lambda i: (cl(i), 0)), hb(QK_PAD), hb(QK_PAD), hb(VDIM),
                   const((N_META, IN_HEAD)), const((1, HEADS, N_META, QK_PAD)), const((1, HEADS, N_META, VDIM)),
                   whole, whole],
        out_shape=[jax.ShapeDtypeStruct((nb * s, IN_HEAD), F32),
                   jax.ShapeDtypeStruct((nb, HEADS, s, QK_PAD), BF16),
                   jax.ShapeDtypeStruct((nb, HEADS, s, QK_PAD), BF16),
                   jax.ShapeDtypeStruct((nb, HEADS, s, VDIM), BF16),
                   jax.ShapeDtypeStruct((N_META, IN_HEAD), F32),
                   jax.ShapeDtypeStruct((1, HEADS, N_META, QK_PAD), BF16),
                   jax.ShapeDtypeStruct((1, HEADS, N_META, VDIM), BF16),
                   jax.ShapeDtypeStruct((D_MODEL, D_MODEL), BF16),
                   jax.ShapeDtypeStruct((IN_PAD, D_MODEL), BF16)],
        scratch_shapes=gat.scratch() + gat_in.vmem_scratch(w_in_shard.shape, (IN_PAD, D_MODEL)),
        compiler_params=_cparams("arbitrary"),
    )(x2d, c_t, sa_t, sb_t, meta, cm_t, sam_t, sbm_t, norm_g, w_head, q_norm_g, wq_p, kv_norm_g, wkv_p, w_out_shard,
      w_in_shard)


def _attn_fwd(q, k, v, km, vm, w_in_shard, w_in_part, nb, s, tq):
    nq = s // tq
    n_steps = nb * HEADS
    gat = _StagedGather(W_IN_PIECES_2, zero_rows=(HEAD_ROWS, IN_HEAD - HEAD_ROWS))
    assert n_steps >= 3

    def body(q_ref, k_ref, v_ref, km_ref, vm_ref, ws_ref, _, o_ref, lse_ref, w_ref, s_scr, p_scr, *gat_scr):
        step = pl.program_id(0) * HEADS + pl.program_id(1)
        for stage, at in enumerate(_StagedGather.steps(n_steps)):
            @pl.when(step == at)
            def _(stage=stage):
                gat.run_vmem(stage, ws_ref, w_ref, gat_scr)

        row = lax.broadcasted_iota(jnp.int32, (tq, tq), 0)
        col = lax.broadcasted_iota(jnp.int32, (tq, tq), 1)
        def scores(i):
            slot = i % 2
            qi = q_ref[0, 0, i * tq:(i + 1) * tq, :]
            sm = _dot_nt(qi, km_ref[0, 0])
            m128 = None
            for j in range(i + 1):
                sc = _dot_nt(qi, k_ref[0, 0, j * tq:(j + 1) * tq, :])
                if j == i:
                    sc = jnp.where(col <= row, sc, NEG_INF)
                s_scr[slot, :, j * tq:(j + 1) * tq] = sc
                mx = sc[:, 0:128]
                for c0 in range(128, tq, 128):
                    mx = jnp.maximum(mx, sc[:, c0:c0 + 128])
                m128 = mx if m128 is None else jnp.maximum(m128, mx)
            return sm, jnp.maximum(jnp.max(m128, axis=1, keepdims=True), jnp.max(sm, axis=1, keepdims=True))

        def weighted_sum(i, pm, l):
            n = (i + 1) * tq
            acc = _dot(p_scr[i % 2, :, 0:n], v_ref[0, 0, 0:n, :]) + _dot(pm.astype(BF16), vm_ref[0, 0])
            o_ref[0, 0, i * tq:(i + 1) * tq, :] = acc / l

        nxt, pending = scores(0), None
        for i in range(nq):
            slot = i % 2
            sm, m = nxt
            if i + 1 < nq:
                nxt = scores(i + 1)
            pm = jnp.exp(sm - m)
            l128 = None
            for j in range(i + 1):
                p = jnp.exp(s_scr[slot, :, j * tq:(j + 1) * tq] - m)
                p_scr[slot, :, j * tq:(j + 1) * tq] = p.astype(BF16)
                ps = p[:, 0:128]
                for c0 in range(128, tq, 128):
                    ps = ps + p[:, c0:c0 + 128]
                l128 = ps if l128 is None else l128 + ps
            l = jnp.sum(l128, axis=1, keepdims=True) + jnp.sum(pm, axis=1, keepdims=True)
            lse_ref[0, 0, :, i * tq:(i + 1) * tq] = _row_of(m + jnp.log(l), tq)
            if pending is not None:
                weighted_sum(*pending)
            pending = (i, pm, l)
        weighted_sum(*pending)

        @pl.when(step == n_steps - 1)
        def _():
            gat.run_vmem(gat.STAGES, ws_ref, w_ref, gat_scr)

    hblk = lambda w: pl.BlockSpec((1, 1, s, w), lambda b, h: (b, h, 0, 0))
    mblk = lambda w: pl.BlockSpec((1, 1, N_META, w), lambda b, h: (0, h, 0, 0))
    whole = pl.BlockSpec(memory_space=pl.ANY)
    return pl.pallas_call(
        body, name="attn_fwd", grid=(nb, HEADS),
        in_specs=[hblk(QK_PAD), hblk(QK_PAD), hblk(VDIM), mblk(QK_PAD), mblk(VDIM), whole, whole],
        out_specs=[hblk(VDIM), pl.BlockSpec((1, 1, 1, s), lambda b, h: (b, h, 0, 0)), whole],
        out_shape=[jax.ShapeDtypeStruct((nb, HEADS, s, VDIM), F32),
                   jax.ShapeDtypeStruct((nb, HEADS, 1, s), F32),
                   jax.ShapeDtypeStruct(w_in_part.shape, BF16)],
        input_output_aliases={6: 2},
        scratch_shapes=[pltpu.VMEM((2, tq, s), F32), pltpu.VMEM((2, tq, s), BF16)]
        + gat.vmem_scratch(w_in_shard.shape, w_in_part.shape),
        compiler_params=_cparams("arbitrary", "arbitrary"),
    )(q, k, v, km, vm, w_in_shard, w_in_part)


def _shift_rows(a, prev, n_rows):
    rid = lax.broadcasted_iota(jnp.int32, a.shape, 0)
    a1 = jnp.where(rid == 0, prev[7:8, :], pltpu.roll(a, 1, 0))
    a2 = jnp.where(rid == 0, prev[6:7, :], jnp.where(rid == 1, prev[7:8, :], pltpu.roll(a, 2, 0)))
    return a1, a2


def _attn_gate(o, za, ga_h):
    on, r = _rms(o, ga_h)
    return on * (za * _sigmoid(za)), on, r


def _out_fwd_bwd(x2d, tgt2d, o, meta, norm_g, w_in_p, conv_w, ga, gc, gmat, w_out, gf, nb, s, tm):
    nt = s // tm
    r = nb * s

    def body(x_ref, t_ref, o_ref, mt_ref, g_ref, wi_ref, cw_ref, ga_ref, gc_ref, gm_ref, w_ref, gf_ref,
             dh_ref, dy_ref, dw_ref, dgf_ref, loss_ref, p_ref, pm_ref, last_cc):
        i = pl.program_id(0)
        blk = lambda ref, j, rows=slice(None): ref[rows, 512 * j:512 * (j + 1)]

        def tail(xv):
            u, _ = _rms(xv, g_ref[...])
            return _dot_nt(u.astype(BF16), wi_ref[IN_HEAD:IN_PAD, :])

        @pl.when(i == 0)
        def _():
            dw_ref[...] = jnp.zeros_like(dw_ref)
            dgf_ref[...] = jnp.zeros_like(dgf_ref)
            loss_ref[...] = jnp.zeros_like(loss_ref)
            last_cc[...] = jnp.zeros_like(last_cc)
            pm_ref[...] = tail(mt_ref[...])

        u16 = _rms(x_ref[...], g_ref[...])[0].astype(BF16)

        def project(j):
            p_ref[:, 512 * j:512 * (j + 1)] = _dot_nt(u16, wi_ref[IN_HEAD + 512 * j:IN_HEAD + 512 * (j + 1), :])

        project(BLK_ZA)
        project(BLK_CC)
        project(BLK_CH)
        ya = []
        for h in range(HEADS):
            y, _, _ = _attn_gate(o_ref[0, h], p_ref[:, 512 * BLK_ZA + VDIM * h:512 * BLK_ZA + VDIM * (h + 1)],
                                 ga_ref[:, VDIM * h:VDIM * (h + 1)])
            ya.append(y)
        project(BLK_CB)
        project(BLK_ZC)
        cc = blk(p_ref, BLK_CC) * blk(p_ref, BLK_CH)
        meta_cc = blk(pm_ref, BLK_CC, slice(8, 16)) * blk(pm_ref, BLK_CH, slice(8, 16))
        prev = jnp.where(i % nt == 0, meta_cc, last_cc[...])
        last_cc[...] = cc[tm - 8:tm, :]
        cc1, cc2 = _shift_rows(cc, prev, tm)
        yc = blk(p_ref, BLK_CB) * (cw_ref[0:1, :] * cc2 + cw_ref[1:2, :] * cc1 + cw_ref[2:3, :] * cc)
        rg = lax.rsqrt(_group_mean(yc * yc, gm_ref[...]) + EPS)
        zc = blk(p_ref, BLK_ZC)
        yconv = yc * rg * gc_ref[...] * (zc * _sigmoid(zc))
        ycat = jnp.concatenate(ya + [yconv], axis=1).astype(BF16)
        h2 = x_ref[...] + _dot(ycat, w_ref[...])
        gfv = gf_ref[...]
        y, r2 = _rms(h2, gfv)
        e = y - t_ref[...]
        loss_ref[...] += 0.5 * jnp.sum(e * e) / D_MODEL
        dyv = e * (1.0 / D_MODEL)
        dh2, dgf = _rms_bwd(dyv, h2, r2, gfv)
        dgf_ref[...] += jnp.sum(dgf, axis=0, keepdims=True)
        dh_ref[...] = dh2
        dhb = dh2.astype(BF16)
        dy_ref[...] = _dot_nt(dhb, w_ref[...])
        dw_ref[...] += _dot_tn(ycat, dhb)

    row = lambda w: pl.BlockSpec((tm, w), lambda i: (i, 0))
    const = lambda shape: pl.BlockSpec(shape, lambda i: (0,) * len(shape))
    full = lambda a: const(a.shape)
    return pl.pallas_call(
        body, name="out_fwd_bwd", grid=(nb * nt,),
        in_specs=[row(D_MODEL), row(D_MODEL),
                  pl.BlockSpec((1, HEADS, tm, VDIM), lambda i: (i // nt, 0, i % nt, 0)),
                  full(meta), full(norm_g), full(w_in_p),
                  full(conv_w), full(ga), full(gc), full(gmat), full(w_out), full(gf)],
        out_specs=[row(D_MODEL), row(D_MODEL), const((D_MODEL, D_MODEL)), const((1, D_MODEL)), const((1, 128)),
                   row(IN_TAIL), const((N_META, IN_TAIL))],
        out_shape=[jax.ShapeDtypeStruct((r, D_MODEL), F32), jax.ShapeDtypeStruct((r, D_MODEL), F32),
                   jax.ShapeDtypeStruct((D_MODEL, D_MODEL), F32), jax.ShapeDtypeStruct((1, D_MODEL), F32),
                   jax.ShapeDtypeStruct((1, 128), F32),
                   jax.ShapeDtypeStruct((r, IN_TAIL), F32), jax.ShapeDtypeStruct((N_META, IN_TAIL), F32)],
        scratch_shapes=[pltpu.VMEM((8, 512), F32)],
        compiler_params=_cparams("arbitrary"),
    )(x2d, tgt2d, o, meta, norm_g, w_in_p, conv_w, ga, gc, gmat, w_out, gf)


def _gate_bwd(dycat, o, p, pm, conv_w, ga, gc, gmat, nb, s, tm):
    nt = s // tm
    r = nb * s
    ext = tm + 8
    prev_idx = lambda i: jnp.maximum(i * (tm // 8) - 1, 0)
    next_idx = lambda i: jnp.minimum((i + 1) * (tm // 8), r // 8 - 1)

    def body(dya_ref, dyc_ref, dycn_ref, o_ref, za_ref, cb_ref, cbn_ref, cc_ref, ccp_ref, ccn_ref,
             ch_ref, chp_ref, chn_ref, zc_ref, zcn_ref, mc_ref, mh_ref, cw_ref, ga_ref, gc_ref, gm_ref,
             dpb_ref, do_ref, dl_ref, dccm_ref, dga_ref, dgc_ref, dcw_ref):
        i = pl.program_id(0)

        @pl.when(i == 0)
        def _():
            dga_ref[...] = jnp.zeros_like(dga_ref)
            dgc_ref[...] = jnp.zeros_like(dgc_ref)
            dcw_ref[...] = jnp.zeros_like(dcw_ref)

        dga = []
        for h in range(HEADS):
            hs = slice(VDIM * h, VDIM * (h + 1))
            oh, za, gah, dya = o_ref[0, h], za_ref[:, hs], ga_ref[:, hs], dya_ref[:, hs]
            sg = _sigmoid(za)
            on, ro = _rms(oh, gah)
            don = dya * (za * sg)
            dpb_ref[:, hs] = (dya * on * (sg * (1.0 + za * (1.0 - sg)))).astype(BF16)
            do, dg = _rms_bwd(don, oh, ro, gah)
            dga.append(jnp.sum(dg, axis=0, keepdims=True))
            dob = do.astype(BF16)
            do_ref[0, h] = dob
            dl_ref[0, h] = _row_of(jnp.sum(dob.astype(F32) * oh, axis=1, keepdims=True), tm)
        dga_ref[...] += jnp.concatenate(dga, axis=1)

        cat = lambda a, b: jnp.concatenate([a[...], b[...]], axis=0)
        cch = cat(cc_ref, ccn_ref)
        chh = cat(ch_ref, chn_ref)
        cb = cat(cb_ref, cbn_ref)
        zc = cat(zc_ref, zcn_ref)
        dy = cat(dyc_ref, dycn_ref)
        first = i % nt == 0
        last = i % nt == nt - 1
        cc = cch * chh
        prev = jnp.where(first, mc_ref[8:16, :] * mh_ref[8:16, :], ccp_ref[...] * chp_ref[...])
        cc1, cc2 = _shift_rows(cc, prev, ext)
        w0, w1, w2 = cw_ref[0:1, :], cw_ref[1:2, :], cw_ref[2:3, :]
        dw = w0 * cc2 + w1 * cc1 + w2 * cc
        yc = cb * dw
        rg = lax.rsqrt(_group_mean(yc * yc, gm_ref[...]) + EPS)
        ych = yc * rg
        gcv = gc_ref[...]
        sg = _sigmoid(zc)
        dycn = dy * (zc * sg)
        dzc = dy * (ych * gcv) * (sg * (1.0 + zc * (1.0 - sg)))
        dgc_ref[...] += jnp.sum((dycn * ych)[:tm], axis=0, keepdims=True)
        dycg = dycn * gcv
        dyc = rg * (dycg - ych * _group_mean(dycg * ych, gm_ref[...]))
        rid = lax.broadcasted_iota(jnp.int32, (ext, CONV_W), 0)
        ddw = jnp.where(jnp.logical_and(last, rid >= tm), 0.0, dyc * cb)
        dcb = dyc * dw
        dcc = w2 * ddw + w1 * pltpu.roll(ddw, ext - 1, 0) + w0 * pltpu.roll(ddw, ext - 2, 0)
        dpb_ref[:, 512:1024] = dcb[:tm].astype(BF16)
        dpb_ref[:, 1024:1536] = (dcc * chh)[:tm].astype(BF16)
        dpb_ref[:, 1536:2048] = (dcc * cch)[:tm].astype(BF16)
        dpb_ref[:, 2048:2560] = dzc[:tm].astype(BF16)
        rs = lambda a: jnp.sum(a[:tm], axis=0, keepdims=True)
        dcw_ref[0:1, :] += rs(ddw * cc2)
        dcw_ref[1:2, :] += rs(ddw * cc1)
        dcw_ref[2:3, :] += rs(ddw * cc)

        @pl.when(first)
        def _():
            d0, d1 = ddw[0:1, :], ddw[1:2, :]
            r8 = lax.broadcasted_iota(jnp.int32, (8, CONV_W), 0)
            dccm_ref[0] = jnp.where(r8 == 7, w1 * d0 + w0 * d1, jnp.where(r8 == 6, w0 * d0, 0.0))

    row = lambda j: pl.BlockSpec((tm, 512), lambda i: (i, j))
    prv = lambda j: pl.BlockSpec((8, 512), lambda i: (prev_idx(i), j))
    nxt = lambda j: pl.BlockSpec((8, 512), lambda i: (next_idx(i), j))
    mblk = lambda j: pl.BlockSpec((N_META, 512), lambda i: (0, j))
    full = lambda a: pl.BlockSpec(a.shape, lambda i: (0,) * a.ndim)
    hb = lambda w: pl.BlockSpec((1, HEADS, tm, w), lambda i: (i // nt, 0, i % nt, 0))
    acc = lambda rr: pl.BlockSpec((rr, 512), lambda i: (0, 0))
    return pl.pallas_call(
        body, name="gate_bwd", grid=(nb * nt,),
        in_specs=[row(0), row(1), nxt(1), hb(VDIM),
                  row(BLK_ZA), row(BLK_CB), nxt(BLK_CB), row(BLK_CC), prv(BLK_CC), nxt(BLK_CC),
                  row(BLK_CH), prv(BLK_CH), nxt(BLK_CH), row(BLK_ZC), nxt(BLK_ZC),
                  mblk(BLK_CC), mblk(BLK_CH), full(conv_w), full(ga), full(gc), full(gmat)],
        out_specs=[pl.BlockSpec((tm, 2560), lambda i: (i, 0)), hb(VDIM),
                   pl.BlockSpec((1, HEADS, 1, tm), lambda i: (i // nt, 0, 0, i % nt)),
                   pl.BlockSpec((1, 8, 512), lambda i: (i // nt, 0, 0)),
                   acc(1), acc(1), acc(8)],
        out_shape=[jax.ShapeDtypeStruct((r, 2560), BF16), jax.ShapeDtypeStruct((nb, HEADS, s, VDIM), BF16),
                   jax.ShapeDtypeStruct((nb, HEADS, 1, s), F32), jax.ShapeDtypeStruct((nb, 8, 512), F32),
                   jax.ShapeDtypeStruct((1, 512), F32), jax.ShapeDtypeStruct((1, 512), F32),
                   jax.ShapeDtypeStruct((8, 512), F32)],
        compiler_params=_cparams("arbitrary"),
    )(dycat, dycat, dycat, o, p, p, p, p, p, p, p, p, p, p, p, pm, pm, conv_w, ga, gc, gmat)


class _StagedReduce:
    LOC, PRE_S, PRE_R, ICI_S, ICI_R, POST_S, POST_R, OUT, N_SEM = 0, 1, 2, 3, 6, 9, 10, 11, 12

    def __init__(self, shard_shape):
        self.half = (shard_shape[0] // 2, shard_shape[1])

    def scratch(self):
        h = self.half
        return [pltpu.VMEM((4,) + h, F32), pltpu.VMEM((4,) + h, F32), pltpu.VMEM((4,) + h, BF16),
                pltpu.VMEM((3,) + h, BF16), pltpu.VMEM(h, F32), pltpu.SemaphoreType.DMA((self.N_SEM,))]

    def run(self, stage, pin, gout, scr):
        own, sib, wire, rbuf, fin, sems = scr
        r2 = self.half[0]
        x, y, c = lax.axis_index("x"), lax.axis_index("y"), lax.axis_index("c")
        mine = 2 * x + y
        sibling = (x, y, 1 - c)
        chips = [(1 - x, y), (x, 1 - y), (1 - x, 1 - y)]
        rows = lambda half: pl.ds(pl.multiple_of(half * r2, r2), r2)
        mesh = pl.DeviceIdType.MESH

        loc = pltpu.make_async_copy(pin.at[:, rows(c), :], own, sems.at[self.LOC])
        pre = pltpu.make_async_remote_copy(
            src_ref=pin.at[:, rows(1 - c), :], dst_ref=sib, send_sem=sems.at[self.PRE_S],
            recv_sem=sems.at[self.PRE_R], device_id=sibling, device_id_type=mesh)

        def ici(j):
            px, py = chips[j]
            return pltpu.make_async_remote_copy(
                src_ref=wire.at[2 * px + py], dst_ref=rbuf.at[j], send_sem=sems.at[self.ICI_S + j],
                recv_sem=sems.at[self.ICI_R + j], device_id=(px, py, c), device_id_type=mesh)

        def post(half):
            return pltpu.make_async_remote_copy(
                src_ref=fin, dst_ref=gout.at[rows(half), :], send_sem=sems.at[self.POST_S],
                recv_sem=sems.at[self.POST_R], device_id=sibling, device_id_type=mesh)

        keep = pltpu.make_async_copy(fin, gout.at[rows(c), :], sems.at[self.OUT])
        if stage == 0:
            loc.start()
            pre.start()
        elif stage == 1:
            loc.wait()
            pre.wait_recv()
            for blk in range(4):
                tot = own[blk] + sib[blk]
                own[blk] = tot
                wire[blk] = tot.astype(BF16)
            for j in range(3):
                ici(j).start()
        elif stage == 2:
            for j in range(3):
                ici(j).wait_recv()
            tot = own[mine]
            for j in range(3):
                tot = tot + rbuf[j].astype(F32)
            fin[...] = tot
            post(c).start()
            keep.start()
        else:
            post(1 - c).wait_recv()
            pre.wait_send()
            for j in range(3):
                ici(j).wait_send()
            post(c).wait_send()
            keep.wait()


def _attn_bwd(q, k, v, do, lse, delta, km, vm, early, nb, s, t):
    n = s // t
    ne = len(early)
    reds = [_StagedReduce(a.shape[1:]) for a in early]
    n_steps = HEADS * nb
    assert n_steps >= 4

    def body(q_ref, k_ref, v_ref, do_ref, lse_ref, dl_ref, km_ref, vm_ref, *rest):
        pin_refs, rest = rest[:ne], rest[ne:]
        dq_ref, dk_ref, dv_ref, dkm_ref, dvm_ref = rest[:5]
        gout_refs, (p_scr, ds_scr, dq_acc), red_scr = rest[5:5 + ne], rest[5 + ne:8 + ne], rest[8 + ne:]
        b = pl.program_id(1)
        step = pl.program_id(0) * nb + b
        for stage, at in enumerate((0, 1, n_steps - 2, n_steps - 1)):
            @pl.when(step == at)
            def _(stage=stage):
                for a, red in enumerate(reds):
                    red.run(stage, pin_refs[a], gout_refs[a], red_scr[6 * a:6 * a + 6])

        @pl.when(b == 0)
        def _():
            dkm_ref[...] = jnp.zeros_like(dkm_ref)
            dvm_ref[...] = jnp.zeros_like(dvm_ref)

        kr = lax.broadcasted_iota(jnp.int32, (t, t), 0)
        qc = lax.broadcasted_iota(jnp.int32, (t, t), 1)
        km_v, vm_v = km_ref[0, 0], vm_ref[0, 0]
        ptm = jnp.exp(_dot_nt(km_v, q_ref[0, 0]) - lse_ref[0, 0])
        dstm = (ptm * (_dot_nt(vm_v, do_ref[0, 0]) - dl_ref[0, 0])).astype(BF16)
        dkm_ref[0] += _dot(dstm, q_ref[0, 0])
        dvm_ref[0] += _dot(ptm.astype(BF16), do_ref[0, 0])
        dq_acc[...] = _dot_tn(dstm, km_v)
        def tiles(j):
            slot = j % 2
            kj = k_ref[0, 0, j * t:(j + 1) * t, :]
            vj = v_ref[0, 0, j * t:(j + 1) * t, :]
            def products(i):
                cs = slice(i * t, (i + 1) * t)
                return _dot_nt(kj, q_ref[0, 0, cs, :]), _dot_nt(vj, do_ref[0, 0, cs, :])

            nxt, pending = products(j), None
            for i in range(j, n):
                cs = slice(i * t, (i + 1) * t)
                st, dpt = nxt
                if i + 1 < n:
                    nxt = products(i + 1)
                if i == j:
                    st = jnp.where(kr <= qc, st, NEG_INF)
                pt = jnp.exp(st - lse_ref[0, 0, :, cs])
                dst = (pt * (dpt - dl_ref[0, 0, :, cs])).astype(BF16)
                p_scr[slot, :, cs] = pt.astype(BF16)
                ds_scr[slot, :, cs] = dst
                if pending is not None:
                    dq_acc[pending[0], :] += _dot_tn(pending[1], kj)
                pending = (cs, dst)
            dq_acc[pending[0], :] += _dot_tn(pending[1], kj)

        for j in range(n):
            slot = j % 2
            tiles(j)
            dv_ref[0, 0, j * t:(j + 1) * t, :] = _dot(p_scr[slot, :, j * t:s], do_ref[0, 0, j * t:s, :]).astype(BF16)
            dk_ref[0, 0, j * t:(j + 1) * t, :] = _dot(ds_scr[slot, :, j * t:s], q_ref[0, 0, j * t:s, :]).astype(BF16)
        dq_ref[0, 0] = dq_acc[...].astype(BF16)

    big = lambda w: pl.BlockSpec((1, 1, s, w), lambda h, b: (b, h, 0, 0))
    rowv = pl.BlockSpec((1, 1, 1, s), lambda h, b: (b, h, 0, 0))
    mk = lambda w: pl.BlockSpec((1, 1, N_META, w), lambda h, b: (0, h, 0, 0))
    mo = lambda w: pl.BlockSpec((1, N_META, w), lambda h, b: (h, 0, 0))
    return pl.pallas_call(
        body, name="attn_bwd", grid=(HEADS, nb),
        in_specs=[big(QK_PAD), big(QK_PAD), big(VDIM), big(VDIM), rowv, rowv, mk(QK_PAD), mk(VDIM)]
        + [pl.BlockSpec(memory_space=pl.ANY)] * ne,
        out_specs=[big(QK_PAD), big(QK_PAD), big(VDIM), mo(QK_PAD), mo(VDIM)]
        + [pl.BlockSpec(memory_space=pl.ANY)] * ne,
        out_shape=[jax.ShapeDtypeStruct((nb, HEADS, s, QK_PAD), BF16),
                   jax.ShapeDtypeStruct((nb, HEADS, s, QK_PAD), BF16),
                   jax.ShapeDtypeStruct((nb, HEADS, s, VDIM), BF16),
                   jax.ShapeDtypeStruct((HEADS, N_META, QK_PAD), F32),
                   jax.ShapeDtypeStruct((HEADS, N_META, VDIM), F32)]
        + [jax.ShapeDtypeStruct(a.shape[1:], F32) for a in early],
        scratch_shapes=[pltpu.VMEM((2, t, s), BF16), pltpu.VMEM((2, t, s), BF16), pltpu.VMEM((s, QK_PAD), F32)]
        + [sc for red in reds for sc in red.scratch()],
        compiler_params=_cparams("arbitrary", "arbitrary"),
    )(q, k, v, do, lse, delta, km, vm, *early)


def _up_bwd(dq, dk, dv, dkm, dvm, p, pm, tabs, tabs_m, wq_p, wkv_p, gq, gkv, nb, s, tm):
    nt = s // tm
    n = nb * nt
    c_t, sa_t, sb_t = tabs
    cm_t, sam_t, sbm_t = tabs_m

    def kv_path(dkh, dvh, pa, c, sa, sb, wkv, gkvv):
        dkpe = dkh[0][:, NOPE:]
        for h in range(1, HEADS):
            dkpe = dkpe + dkh[h][:, NOPE:]
        dkr = _rope_bwd(dkpe, c, sa, sb)
        dkv = jnp.concatenate([d[:, :NOPE] for d in dkh] + list(dvh), axis=1).astype(BF16)
        ckv = pa[:, Q_RANK:Q_RANK + KV_RANK]
        kvn, rkv = _rms(ckv, gkvv)
        dckv, dg = _rms_bwd(_dot(dkv, wkv), ckv, rkv, gkvv)
        return dckv, dkr, kvn.astype(BF16), dkv, jnp.sum(dg, axis=0, keepdims=True)

    def body(dq_ref, dk_ref, dv_ref, pa_ref, c_ref, sa_ref, sb_ref,
             dkm_ref, dvm_ref, pam_ref, cm_ref, sam_ref, sbm_ref,
             wq_ref, wkv_ref, gq_ref, gkv_ref,
             dpa_ref, dpam_ref, pq_ref, pkv_ref, dgq_ref, dgkv_ref, dwq_ref, dwkv_ref):
        i = pl.program_id(0)

        @pl.when(i == 0)
        def _():
            dwq_ref[...] = jnp.zeros_like(dwq_ref)
            dwkv_ref[...] = jnp.zeros_like(dwkv_ref)
            dgq_ref[...] = jnp.zeros_like(dgq_ref)
            dgkv_ref[...] = jnp.zeros_like(dgkv_ref)

        @pl.when(i < n)
        def _():
            c, sa, sb = c_ref[...], sa_ref[...], sb_ref[...]
            pa = pa_ref[...]
            parts = []
            for h in range(HEADS):
                dqh = dq_ref[0, h].astype(F32) * ATTN_SCALE
                parts += [dqh[:, :NOPE], _rope_bwd(dqh[:, NOPE:], c, sa, sb)]
            dql = jnp.concatenate(parts, axis=1).astype(BF16)
            cq = pa[:, 0:Q_RANK]
            gqv = gq_ref[...]
            qn, rq = _rms(cq, gqv)
            dwq_ref[...] += _dot_tn(dql, qn.astype(BF16))
            dcq, dg = _rms_bwd(_dot(dql, wq_ref[...]), cq, rq, gqv)
            dgq_ref[...] += jnp.sum(dg, axis=0, keepdims=True)
            dckv, dkr, kvn, dkv, dgk = kv_path([dk_ref[0, h].astype(F32) for h in range(HEADS)],
                                               [dv_ref[0, h].astype(F32) for h in range(HEADS)],
                                               pa, c, sa, sb, wkv_ref[...], gkv_ref[...])
            dwkv_ref[...] += _dot_tn(dkv, kvn)
            dgkv_ref[...] += dgk
            dpa_ref[...] = jnp.concatenate([dcq, dckv, dkr], axis=1).astype(BF16)

        @pl.when(i == n)
        def _():
            dckv, dkr, kvn, dkv, dgk = kv_path([dkm_ref[h] for h in range(HEADS)],
                                               [dvm_ref[h] for h in range(HEADS)],
                                               pam_ref[...], cm_ref[...], sam_ref[...], sbm_ref[...],
                                               wkv_ref[...], gkv_ref[...])
            dwkv_ref[...] += _dot_tn(dkv, kvn)
            dgkv_ref[...] += dgk
            dpam_ref[...] = jnp.concatenate([jnp.zeros((N_META, Q_RANK), F32), dckv, dkr], axis=1)
            for h in range(HEADS):
                pq_ref[h] = dwq_ref[QK_PAD * h:QK_PAD * h + NOPE + ROPE, :]
                pkv_ref[h, 0:NOPE, :] = dwkv_ref[NOPE * h:NOPE * (h + 1), :]
                pkv_ref[h, NOPE:NOPE + VDIM, :] = dwkv_ref[512 + VDIM * h:512 + VDIM * (h + 1), :]

    cl = lambda i: jnp.minimum(i, n - 1)
    hb = lambda w: pl.BlockSpec((1, HEADS, tm, w), lambda i: (cl(i) // nt, 0, cl(i) % nt, 0))
    tab = pl.BlockSpec((tm, 128), lambda i: (cl(i) % nt, 0))
    full = lambda a: pl.BlockSpec(a.shape, lambda i: (0,) * a.ndim)
    const = lambda shape: pl.BlockSpec(shape, lambda i: (0,) * len(shape))
    return pl.pallas_call(
        body, name="up_bwd", grid=(n + 1,),
        in_specs=[hb(QK_PAD), hb(QK_PAD), hb(VDIM), pl.BlockSpec((tm, 512), lambda i: (cl(i), 0)), tab, tab, tab,
                  full(dkm), full(dvm), pl.BlockSpec((N_META, 512), lambda i: (0, 0)),
                  full(cm_t), full(sam_t), full(sbm_t), full(wq_p), full(wkv_p), full(gq), full(gkv)],
        out_specs=[pl.BlockSpec((tm, 512), lambda i: (cl(i), 0)), const((N_META, 512)),
                   const((HEADS, NOPE + ROPE, Q_RANK)), const((HEADS, NOPE + VDIM, KV_RANK)),
                   const((1, Q_RANK)), const((1, KV_RANK))],
        out_shape=[jax.ShapeDtypeStruct((nb * s, 512), BF16), jax.ShapeDtypeStruct((N_META, 512), F32),
                   jax.ShapeDtypeStruct((HEADS, NOPE + ROPE, Q_RANK), F32),
                   jax.ShapeDtypeStruct((HEADS, NOPE + VDIM, KV_RANK), F32),
                   jax.ShapeDtypeStruct((1, Q_RANK), F32), jax.ShapeDtypeStruct((1, KV_RANK), F32)],
        scratch_shapes=[pltpu.VMEM((HEADS * QK_PAD, Q_RANK), F32), pltpu.VMEM((1024, KV_RANK), F32)],
        compiler_params=_cparams("arbitrary"),
    )(dq, dk, dv, p, c_t, sa_t, sb_t, dkm, dvm, pm, cm_t, sam_t, sbm_t, wq_p, wkv_p, gq, gkv)


def _in_bwd(x2d, dh2, dpa, dpb, meta, dpam, dccm, pm, w_in_p, norm_g, nb, s, tm):
    nt = s // tm
    n = nb * nt

    def body(x_ref, dh_ref, dpa_ref, dpb_ref, mt_ref, dpam_ref, dccm_ref, mc_ref, mh_ref, w_ref, g_ref,
             gx_ref, gm_ref, dw_hbm, dg_ref, acc_ref, sems):
        i = pl.program_id(0)

        @pl.when(i == 0)
        def _():
            acc_ref[...] = jnp.zeros_like(acc_ref)
            dg_ref[...] = jnp.zeros_like(dg_ref)

        def rows(x, dp, dres):
            g = g_ref[...]
            dpb16 = dp.astype(BF16)
            du = _dot(dpb16, w_ref[...])
            u, r1 = _rms(x, g)
            acc_ref[...] += _dot_tn(dpb16, u.astype(BF16))
            dx, dg = _rms_bwd(du, x, r1, g)
            dg_ref[...] += jnp.sum(dg, axis=0, keepdims=True)
            return dx if dres is None else dx + dres

        @pl.when(i < n)
        def _():
            dp = jnp.concatenate([dpa_ref[...], dpb_ref[...]], axis=1)
            gx_ref[...] = rows(x_ref[...], dp, dh_ref[...])

        @pl.when(i == n)
        def _():
            dcc = dccm_ref[0]
            for b in range(1, nb):
                dcc = dcc + dccm_ref[b]
            z8 = jnp.zeros((8, CONV_W), F32)
            dc = jnp.concatenate([z8, dcc * mh_ref[8:16, :]], axis=0)
            dh = jnp.concatenate([z8, dcc * mc_ref[8:16, :]], axis=0)
            z = jnp.zeros((N_META, CONV_W), F32)
            dp = jnp.concatenate([dpam_ref[...], z, z, dc, dh, z], axis=1)
            gm_ref[...] = rows(mt_ref[...], dp, None)
            per = IN_DIM // 4
            cps = [pltpu.make_async_copy(acc_ref.at[0:448], dw_hbm.at[0, 0:448], sems.at[0]),
                   pltpu.make_async_copy(acc_ref.at[512:per + 64], dw_hbm.at[0, 448:per], sems.at[1])]
            for qq in range(1, 4):
                cps.append(pltpu.make_async_copy(acc_ref.at[per * qq + 64:per * (qq + 1) + 64], dw_hbm.at[qq],
                                                 sems.at[qq + 1]))
            for cp in cps:
                cp.start()
            for cp in cps:
                cp.wait()

    cl = lambda i: jnp.minimum(i, n - 1)
    row = lambda w: pl.BlockSpec((tm, w), lambda i: (cl(i), 0))
    full = lambda a: pl.BlockSpec(a.shape, lambda i: (0,) * a.ndim)
    mblk = lambda j: pl.BlockSpec((N_META, 512), lambda i: (0, j))
    return pl.pallas_call(
        body, name="in_bwd", grid=(n + 1,),
        in_specs=[row(D_MODEL), row(D_MODEL), row(512), row(2560), full(meta), full(dpam), full(dccm),
                  mblk(BLK_CC), mblk(BLK_CH), full(w_in_p), full(norm_g)],
        out_specs=[row(D_MODEL), pl.BlockSpec((N_META, D_MODEL), lambda i: (0, 0)),
                   pl.BlockSpec(memory_space=pl.ANY), pl.BlockSpec((1, D_MODEL), lambda i: (0, 0))],
        out_shape=[jax.ShapeDtypeStruct((nb * s, D_MODEL), F32), jax.ShapeDtypeStruct((N_META, D_MODEL), F32),
                   jax.ShapeDtypeStruct((4, IN_DIM // 4, D_MODEL), F32), jax.ShapeDtypeStruct((1, D_MODEL), F32)],
        scratch_shapes=[pltpu.VMEM((IN_PAD, D_MODEL), F32), pltpu.SemaphoreType.DMA((5,))],
        compiler_params=_cparams("arbitrary"),
    )(x2d, dh2, dpa, dpb, meta, dpam, dccm, pm, pm, w_in_p, norm_g)


def _gather_weights(w_in_shard, w_out_shard, split, pieces, out_rows, whole, zero_fills):
    ns, nw, nz = len(split), len(whole), len(zero_fills)
    flat = [(a, pc) for a in range(ns) for pc in pieces[a]]
    nk = len(flat)
    hh = HEAD_ROWS // 2
    assert hh % 16 == 0

    def body(*refs):
        ins, wins, zins = refs[2:2 + ns], refs[2 + ns:2 + ns + nw], refs[2 + ns + nw:2 + ns + nw + nz]
        n_in = 2 + ns + nw + nz
        head_ref, shard16, w_out16 = refs[n_in:n_in + 3]
        outs, wcat = refs[n_in + 3:n_in + 3 + ns], refs[n_in + 3 + ns:n_in + 3 + ns + nw]
        scr = refs[n_in + 3 + ns + nw:]
        stage, wouts = scr[:ns], scr[ns:ns + nw]
        (send_sems, recv_sems, fwd_send, fwd_recv, loc_sems, w_send, w_recv, w_loc, z_sems,
         h_send, h_recv, h_relay, h_pass) = scr[ns + nw:]
        x, y, c = lax.axis_index("x"), lax.axis_index("y"), lax.axis_index("c")
        mine = 2 * x + y
        chips = [(1 - x, y), (x, 1 - y), (1 - x, 1 - y)]
        chip_of = [2 * px + py for px, py in chips]
        shard16[...] = refs[0][...].astype(BF16)
        w_out16[...] = refs[1][...].astype(BF16)
        for a in range(ns):
            stage[a][...] = ins[a][...].astype(BF16)

        def head_rows(half):
            return pl.ds(pl.multiple_of(half * hh, 16), hh)

        def head_copy(turn):
            dest = (1 - c, c, c) if turn == 0 else (c, 1 - c, c)
            return pltpu.make_async_remote_copy(
                src_ref=shard16.at[head_rows(c)], dst_ref=head_ref.at[head_rows(c)], send_sem=h_send.at[turn],
                recv_sem=h_recv.at[0], device_id=dest, device_id_type=pl.DeviceIdType.MESH)

        def head_relay():
            ref = head_ref.at[head_rows(c)]
            return pltpu.make_async_remote_copy(
                src_ref=ref, dst_ref=ref, send_sem=h_relay.at[0], recv_sem=h_recv.at[0],
                device_id=(1, 1, c), device_id_type=pl.DeviceIdType.MESH)

        def head_pass(half):
            ref = head_ref.at[head_rows(half)]
            return pltpu.make_async_remote_copy(
                src_ref=ref, dst_ref=ref, send_sem=h_pass.at[0], recv_sem=h_pass.at[1],
                device_id=(x, y, 1 - c), device_id_type=pl.DeviceIdType.MESH)

        head_ref[HEAD_ROWS:IN_HEAD, :] = jnp.zeros((IN_HEAD - HEAD_ROWS, D_MODEL), BF16)

        @pl.when(mine == 0)
        def _():
            head_copy(0).start()
            head_ref[0:HEAD_ROWS, :] = shard16[0:HEAD_ROWS, :]

        def src(k):
            a, (s0, nr, _, _, _, _) = flat[k]
            return stage[a].at[s0:s0 + nr]

        def dst(k, q):
            a, (_, nr, per, first, rest, _) = flat[k]
            row = per * q + first + (rest - first) * jnp.minimum(q, 1)
            return outs[a].at[pl.ds(pl.multiple_of(row, 16), nr)]

        def ici(k, j, q):
            px, py = chips[j]
            return pltpu.make_async_remote_copy(
                src_ref=src(k), dst_ref=dst(k, q), send_sem=send_sems.at[k, j], recv_sem=recv_sems.at[k, j],
                device_id=(px, py, c), device_id_type=pl.DeviceIdType.MESH)

        def fwd(k, j):
            ref = dst(k, chip_of[j])
            return pltpu.make_async_remote_copy(
                src_ref=ref, dst_ref=ref, send_sem=fwd_send.at[k, j], recv_sem=fwd_recv.at[k, j],
                device_id=(x, y, 1 - c), device_id_type=pl.DeviceIdType.MESH)

        def wcopy(b, j, q):
            px, py = chips[j]
            return pltpu.make_async_remote_copy(
                src_ref=wins[b], dst_ref=wouts[b].at[q], send_sem=w_send.at[b, j], recv_sem=w_recv.at[b, j],
                device_id=(px, py, c), device_id_type=pl.DeviceIdType.MESH)

        local = [pltpu.make_async_copy(src(k), dst(k, mine), loc_sems.at[k]) for k in range(nk)]
        local += [pltpu.make_async_copy(wins[b], wouts[b].at[mine], w_loc.at[b]) for b in range(nw)]
        for z, (a, _, row0) in enumerate(zero_fills):
            local.append(pltpu.make_async_copy(zins[z], outs[a].at[row0:row0 + zins[z].shape[0]], z_sems.at[z]))
        wsends = [wcopy(b, j, mine) for b in range(nw) for j in range(3)]
        for cp in local + wsends:
            cp.start()

        for half in (0, 1):
            @pl.when(c == half)
            def _(half=half):
                my_k = [k for k in range(nk) if flat[k][1][5] == half]
                other_k = [k for k in range(nk) if flat[k][1][5] != half]
                sends = [ici(k, j, mine) for k in my_k for j in range(3)]
                for cp in sends:
                    cp.start()
                passed = []
                for k in my_k:
                    for j in range(3):
                        ici(k, j, chip_of[j]).wait_recv()
                        cp = fwd(k, j)
                        cp.start()
                        passed.append(cp)
                for k in other_k:
                    for j in range(3):
                        fwd(k, j).wait_recv()
                for cp in sends + passed:
                    cp.wait_send()

        for b in range(nw):
            for j in range(3):
                wcopy(b, j, chip_of[j]).wait_recv()
        for cp in wsends:
            cp.wait_send()
        for cp in local:
            cp.wait()
        for b in range(nw):
            cols = wins[b].shape[-1]
            for q in range(4):
                wcat[b][..., cols * q:cols * (q + 1)] = wouts[b][q]

        @pl.when(mine == 0)
        def _():
            head_copy(0).wait_send()
            head_copy(1).start()
            head_copy(1).wait_send()

        @pl.when(mine != 0)
        def _():
            hands_on = mine == 2 - c
            head_copy(0).wait_recv()

            @pl.when(hands_on)
            def _():
                head_relay().start()

            head_pass(c).start()
            head_pass(1 - c).wait_recv()
            head_pass(c).wait_send()

            @pl.when(hands_on)
            def _():
                head_relay().wait_send()

    vmem = pl.BlockSpec(memory_space=pltpu.VMEM)
    dma = pltpu.SemaphoreType.DMA
    zeros = [z for _, z, _ in zero_fills]
    return pl.pallas_call(
        body, name="gather_weights",
        in_specs=[vmem] * (2 + ns + nw + nz), out_specs=[vmem] * (3 + ns + nw),
        out_shape=([jax.ShapeDtypeStruct((IN_HEAD, D_MODEL), BF16), jax.ShapeDtypeStruct(w_in_shard.shape, BF16),
                    jax.ShapeDtypeStruct(w_out_shard.shape, BF16)]
                   + [jax.ShapeDtypeStruct((out_rows[a], split[a].shape[1]), BF16) for a in range(ns)]
                   + [jax.ShapeDtypeStruct(w.shape[:-1] + (4 * w.shape[-1],), w.dtype) for w in whole]),
        scratch_shapes=[pltpu.VMEM(a.shape, BF16) for a in split] + [pltpu.VMEM((4,) + w.shape, w.dtype) for w in whole]
        + [dma((nk, 3)), dma((nk, 3)), dma((nk, 3)), dma((nk, 3)), dma((nk,)),
           dma((nw, 3)), dma((nw, 3)), dma((nw,)), dma((nz,)),
           dma((2,)), dma((1,)), dma((1,)), dma((2,))],
        compiler_params=pltpu.CompilerParams(vmem_limit_bytes=VMEM_LIMIT),
    )(w_in_shard, w_out_shard, *split, *whole, *zeros)


def _reduce_grads(parts, small):
    n = len(parts)
    ns = len(small)
    shapes = [a.shape[1:] for a in parts]
    halves = [(sh[0] // 2, sh[1]) for sh in shapes]
    sm_blocks = [a.shape[1] // 128 for a, _ in small]
    sm_first = [sum(nr * nblk for (_, nr), nblk in zip(small[:k], sm_blocks[:k])) for k in range(ns)]
    sm_rows = -(-(sm_first[-1] + small[-1][1] * sm_blocks[-1]) // 8) * 8
    sm_shape = (sm_rows, 128)

    def body(*refs):
        pin, sm_in = refs[:n], refs[n:n + ns]
        gout, sm_out = refs[n + ns:2 * n + ns], refs[2 * n + ns]
        scr = refs[2 * n + ns + 1:]
        own, sib, wire, rbuf = scr[:n], scr[n:2 * n], scr[2 * n:3 * n], scr[3 * n:4 * n]
        (sbuf, send_sems, recv_sems, loc_sems, pre_send, pre_recv, post_send, post_recv,
         sm_send, sm_recv, sm_pack) = scr[4 * n:]
        x, y, c = lax.axis_index("x"), lax.axis_index("y"), lax.axis_index("c")
        mine = 2 * x + y
        sm_pack[...] = jnp.zeros(sm_shape, F32)
        for k, (_, nr) in enumerate(small):
            for i in range(nr):
                for j in range(sm_blocks[k]):
                    row = sm_first[k] + i * sm_blocks[k] + j
                    sm_pack[row:row + 1, :] = sm_in[k][i:i + 1, 128 * j:128 * (j + 1)]
        me = 4 * x + 2 * y + c
        sibling = (x, y, 1 - c)

        def rows(a, half):
            r2 = halves[a][0]
            return pl.ds(pl.multiple_of(half * r2, r2), r2)

        near = (jnp.where(c == 0, 1 - x, x), jnp.where(c == 0, y, 1 - y))
        far = (jnp.where(c == 0, x, 1 - x), jnp.where(c == 0, 1 - y, y))
        chip = lambda p: 2 * p[0] + p[1]
        blocks = [3 - mine, chip(near), chip(far), mine]

        def pre(a, k):
            q = blocks[k]
            return pltpu.make_async_remote_copy(
                src_ref=pin[a].at[q, rows(a, 1 - c), :], dst_ref=sib[a].at[q],
                send_sem=pre_send.at[a, q], recv_sem=pre_recv.at[a, q], device_id=sibling,
                device_id_type=pl.DeviceIdType.MESH)

        def ici(a, m):
            px, py = near if m < 2 else far
            return pltpu.make_async_remote_copy(
                src_ref=wire[a].at[blocks[m]], dst_ref=rbuf[a].at[m], send_sem=send_sems.at[a, m],
                recv_sem=recv_sems.at[a, m], device_id=(px, py, c), device_id_type=pl.DeviceIdType.MESH)

        def post(a, half):
            ref = gout[a].at[rows(a, half), :]
            return pltpu.make_async_remote_copy(
                src_ref=ref, dst_ref=ref, send_sem=post_send.at[a], recv_sem=post_recv.at[a],
                device_id=sibling, device_id_type=pl.DeviceIdType.MESH)

        def small_copy(kk):
            peer = (x ^ (kk >> 2), y ^ ((kk >> 1) & 1), c ^ (kk & 1))
            return pltpu.make_async_remote_copy(
                src_ref=sm_pack, dst_ref=sbuf.at[kk], send_sem=sm_send.at[kk - 1], recv_sem=sm_recv.at[kk - 1],
                device_id=peer, device_id_type=pl.DeviceIdType.MESH)

        local = [[pltpu.make_async_copy(pin[a].at[blocks[k], rows(a, c), :], own[a].at[blocks[k]], loc_sems.at[a, k])
                  for k in range(4)] for a in range(n)]
        pres = [[pre(a, k) for k in range(4)] for a in range(n)]
        smalls = [small_copy(kk) for kk in range(1, 8)]
        for a in range(n):
            for k in range(4):
                local[a][k].start()
                pres[a][k].start()
        for cp in smalls:
            cp.start()
        sbuf[0] = sm_pack[...]
        sends = []
        for a in range(n):
            for k in range(4):
                local[a][k].wait()
                pres[a][k].wait_recv()
                tot = own[a][blocks[k]] + sib[a][blocks[k]]
                if k == 2:
                    ici(a, 0).wait_recv()
                    tot = tot + rbuf[a][0].astype(F32)
                own[a][blocks[k]] = tot
                if k < 3:
                    wire[a][blocks[k]] = tot.astype(BF16)
                    cp = ici(a, k)
                    cp.start()
                    sends.append(cp)
        for cp in smalls:
            cp.wait_recv()
        total = sbuf[me]
        for d in range(1, 8):
            total = total + sbuf[me ^ d]
        sm_out[...] = total
        posts = []
        for a in range(n):
            fin = own[a][mine]
            for m in (1, 2):
                ici(a, m).wait_recv()
                fin = fin + rbuf[a][m].astype(F32)
            gout[a][rows(a, c), :] = fin
            cp = post(a, c)
            cp.start()
            posts.append(cp)
        for a in range(n):
            post(a, 1 - c).wait_recv()
        for cp in [cp for row in pres for cp in row] + sends + smalls + posts:
            cp.wait_send()

    vmem = pl.BlockSpec(memory_space=pltpu.VMEM)
    dma = pltpu.SemaphoreType.DMA
    return pl.pallas_call(
        body, name="reduce_grads",
        in_specs=[pl.BlockSpec(memory_space=pl.ANY)] * n + [vmem] * ns, out_specs=[vmem] * (n + 1),
        out_shape=[jax.ShapeDtypeStruct(sh, F32) for sh in shapes] + [jax.ShapeDtypeStruct(sm_shape, F32)],
        scratch_shapes=([pltpu.VMEM((4,) + hs, F32) for hs in halves] + [pltpu.VMEM((4,) + hs, F32) for hs in halves]
                        + [pltpu.VMEM((4,) + hs, BF16) for hs in halves]
                        + [pltpu.VMEM((3,) + hs, BF16) for hs in halves]
                        + [pltpu.VMEM((8,) + sm_shape, F32), dma((n, 3)), dma((n, 3)), dma((n, 4)),
                           dma((n, 4)), dma((n, 4)), dma((n,)), dma((n,)), dma((7,)), dma((7,)),
                           pltpu.VMEM(sm_shape, F32)]),
        compiler_params=pltpu.CompilerParams(vmem_limit_bytes=VMEM_LIMIT),
    )(*parts, *[a for a, _ in small])


def _adamw_update(w_ref, g_ref, m_ref, v_ref, d_ref, nm_ref, nv_ref):
    gv = g_ref[...]
    nm = ADAM_B1 * m_ref[...] + (1.0 - ADAM_B1) * gv
    nv = ADAM_B2 * v_ref[...] + (1.0 - ADAM_B2) * (gv * gv)
    m_hat = nm / (1.0 - ADAM_B1 ** ADAM_STEP)
    v_hat = nv / (1.0 - ADAM_B2 ** ADAM_STEP)
    d_ref[...] = -ADAM_LR * (m_hat / (jnp.sqrt(v_hat) + ADAM_EPS) + ADAM_WD * w_ref[...])
    nm_ref[...] = nm
    nv_ref[...] = nv


def _adamw_small(ws, gs, ms, vs):
    k = len(ws)

    def body(*refs):
        ins, outs = refs[:4 * k], refs[4 * k:]
        for a in range(k):
            _adamw_update(ins[a], ins[k + a], ins[2 * k + a], ins[3 * k + a], outs[a], outs[k + a], outs[2 * k + a])

    out = pl.pallas_call(
        body, name="adamw_small",
        out_shape=[jax.ShapeDtypeStruct(w.shape, F32) for w in ws] * 3,
        compiler_params=pltpu.CompilerParams(vmem_limit_bytes=VMEM_LIMIT),
    )(*ws, *gs, *ms, *vs)
    return out[:k], out[k:2 * k], out[2 * k:]


def _adamw(w, g, m, v, name):
    shape = w.shape
    w2, g2, m2, v2 = (a.reshape((-1, shape[-1])) for a in (w, g, m, v))

    def body(w_ref, g_ref, m_ref, v_ref, d_ref, nm_ref, nv_ref):
        _adamw_update(w_ref, g_ref, m_ref, v_ref, d_ref, nm_ref, nv_ref)

    rows, cols = w2.shape
    nblk = cols // 256 if cols % 256 == 0 and rows >= 64 else 1
    blk = pl.BlockSpec((rows, cols // nblk), lambda j: (0, j))
    out = pl.pallas_call(
        body, name=name, grid=(nblk,), in_specs=[blk] * 4, out_specs=[blk] * 3,
        out_shape=[jax.ShapeDtypeStruct(w2.shape, F32)] * 3,
        compiler_params=_cparams("parallel"),
    )(w2, g2, m2, v2)
    return tuple(a.reshape(shape) for a in out)


def kernel(x, meta_tokens, norm_g, w_in, q_norm_g, w_q_up, kv_norm_g, w_kv_up, conv_w, attn_out_g, conv_out_g, w_out, final_norm_g, loss_target, m_meta_tokens, m_norm_g, m_w_in, m_q_norm_g, m_w_q_up, m_kv_norm_g, m_w_kv_up, m_conv_w, m_attn_out_g, m_conv_out_g, m_w_out, m_final_norm_g, v_meta_tokens, v_norm_g, v_w_in, v_q_norm_g, v_w_q_up, v_kv_norm_g, v_w_kv_up, v_conv_w, v_attn_out_g, v_conv_out_g, v_w_out, v_final_norm_g):
    nb, s, _ = x.shape
    tm = min(ROW_TILE, s)
    ta = min(ATTN_TILE, s)
    assert s % tm == 0 and s % ta == 0 and tm % 16 == 0
    r = nb * s

    tr = lambda a: jnp.transpose(a[0])
    w_head, w_in_shard, w_out_shard, wq_p, wkv_p, g_cw, meta_f = _gather_weights(
        tr(w_in), w_out[0], [tr(w_q_up), tr(w_kv_up)],
        [W_Q_PIECES, W_KV_PIECES], [HEADS * QK_PAD, 1024],
        [jnp.transpose(conv_w, (1, 0, 2)), meta_tokens],
        [(0, jnp.zeros((64, Q_RANK), BF16), QK_PAD * h + NOPE + ROPE) for h in range(HEADS)])
    conv_f = g_cw.reshape(3, CONV_W)

    c_all, sa_all, sb_all = _rope_tables(N_META + s)
    tabs_m = (c_all[:N_META], sa_all[:N_META], sb_all[:N_META])
    tabs = (c_all[N_META:], sa_all[N_META:], sb_all[N_META:])
    gid = np.arange(CONV_W) // CONV_GROUP
    gmat = jnp.asarray(np.where(gid[:, None] == gid[None, :], 1.0 / CONV_GROUP, 0.0), BF16)
    ga, gc = attn_out_g, conv_out_g
    gf = final_norm_g.reshape(1, D_MODEL)

    x2d = x.reshape(r, D_MODEL)
    tgt2d = loss_target.reshape(r, D_MODEL)

    ph, q, k, v, pmh, km, vm, w_out_f, w_in_part = _fwd_proj(
        x2d, meta_f, tabs, tabs_m, norm_g, w_head, q_norm_g, wq_p, kv_norm_g, wkv_p, w_out_shard, w_in_shard,
        nb, s, tm)
    o, lse, w_in_p = _attn_fwd(q, k, v, km, vm, w_in_shard, w_in_part, nb, s, ta)
    dh2, dycat, dw_out, dgf, loss_acc, pt, pmt = _out_fwd_bwd(x2d, tgt2d, o, meta_f, norm_g, w_in_p, conv_f, ga, gc,
                                                              gmat, w_out_f, gf, nb, s, tm)
    dpb, do, delta, dccm, dga, dgc, dcw = _gate_bwd(dycat, o, pt, pmt, conv_f, ga, gc, gmat, nb, s, tm)
    p_out = dw_out.reshape(4, D_MODEL // 4, D_MODEL)
    dq, dk, dv, dkm, dvm, g_w_out = _attn_bwd(q, k, v, do, lse, delta, km, vm, [p_out], nb, s, ta)
    dpa, dpam, p_q, p_kv, dgq, dgkv = _up_bwd(dq, dk, dv, dkm, dvm, ph, pmh, tabs, tabs_m, wq_p, wkv_p,
                                              q_norm_g, kv_norm_g, nb, s, tm)
    gx, gmeta, p_in, dng = _in_bwd(x2d, dh2, dpa, dpb, meta_f, dpam, dccm, pmt, w_in_p, norm_g, nb, s, tm)

    g_w_in_t, g_w_q_t, g_w_kv_t, small_sum = _reduce_grads(
        [p_in, p_q, p_kv],
        [(dng, 1), (dgq, 1), (dgkv, 1), (dga, 1), (dgc, 1), (dgf, 1), (dcw, 3), (gmeta, N_META), (loss_acc, 1)])
    ssum = small_sum.reshape(-1)

    def take(off, n):
        return ssum[off:off + n], off + n

    off = 0
    g_norm, off = take(off, D_MODEL)
    g_qn, off = take(off, Q_RANK)
    g_kvn, off = take(off, KV_RANK)
    g_ga, off = take(off, CONV_W)
    g_gc, off = take(off, CONV_W)
    g_gf, off = take(off, D_MODEL)
    g_cw_all, off = take(off, 3 * CONV_W)
    g_meta_all, off = take(off, N_META * D_MODEL)
    loss = ssum[off]
    chip = 2 * lax.axis_index("x") + lax.axis_index("y")
    g_conv = lax.dynamic_slice(g_cw_all.reshape(3, CONV_W), (0, chip * 128), (3, 128))
    g_mt = lax.dynamic_slice(g_meta_all.reshape(N_META, D_MODEL), (0, chip * 256), (N_META, 256))

    grads = {
        "meta_tokens": g_mt, "norm_g": g_norm.reshape(1, -1), "w_in": g_w_in_t, "q_norm_g": g_qn.reshape(1, -1),
        "w_q_up": g_w_q_t, "kv_norm_g": g_kvn.reshape(1, -1), "w_kv_up": jnp.transpose(g_w_kv_t)[None],
        "conv_w": g_conv[None], "attn_out_g": g_ga.reshape(1, -1), "conv_out_g": g_gc.reshape(1, -1),
        "w_out": g_w_out[None], "final_norm_g": g_gf,
    }
    transposed = ("w_in", "w_q_up")
    weights = {
        "meta_tokens": (meta_tokens, m_meta_tokens, v_meta_tokens), "norm_g": (norm_g, m_norm_g, v_norm_g),
        "w_in": (w_in, m_w_in, v_w_in), "q_norm_g": (q_norm_g, m_q_norm_g, v_q_norm_g),
        "w_q_up": (w_q_up, m_w_q_up, v_w_q_up), "kv_norm_g": (kv_norm_g, m_kv_norm_g, v_kv_norm_g),
        "w_kv_up": (w_kv_up, m_w_kv_up, v_w_kv_up), "conv_w": (conv_w, m_conv_w, v_conv_w),
        "attn_out_g": (attn_out_g, m_attn_out_g, v_attn_out_g), "conv_out_g": (conv_out_g, m_conv_out_g, v_conv_out_g),
        "w_out": (w_out, m_w_out, v_w_out), "final_norm_g": (final_norm_g, m_final_norm_g, v_final_norm_g),
    }
    names = list(weights)
    small = [nme for nme in names if nme != "w_in"]

    def view(nme, a):
        if nme in transposed:
            return a if a.ndim == 2 else tr(a)
        if nme == "conv_w":
            return jnp.transpose(a.reshape(1, 3, -1), (1, 0, 2))
        if a.ndim == 3:
            return a[0]
        return a.reshape(1, -1) if a.ndim == 1 else a

    def unview(nme, a):
        if nme in transposed:
            return jnp.transpose(a)[None]
        if nme == "conv_w":
            return jnp.transpose(a, (1, 0, 2))
        return a.reshape(weights[nme][0].shape)

    res_small = _adamw_small(*[[view(nme, a) for nme, a in zip(small, col)] for col in (
        [weights[nme][0] for nme in small], [grads[nme] for nme in small],
        [weights[nme][1] for nme in small], [weights[nme][2] for nme in small])])
    w_, m_, v_ = weights["w_in"]
    res = _adamw(tr(w_), grads["w_in"], tr(m_), tr(v_), "adamw_w_in")
    upd = {"w_in": tuple(jnp.transpose(a)[None] for a in (grads["w_in"],) + res)}
    for j, nme in enumerate(small):
        upd[nme] = (unview(nme, view(nme, grads[nme])),) + tuple(unview(nme, r[j]) for r in res_small)
    grads = {nme: upd[nme][0] for nme in names}
    deltas, new_m, new_v = ([upd[nme][j] for nme in names] for j in (1, 2, 3))

    grad_x = gx.reshape(nb, s, D_MODEL)
    return (loss, grad_x, *[grads[nme] for nme in names], *deltas, *new_m, *new_v)
```

```python
import functools

import jax
import jax.numpy as jnp
import numpy as np
from jax import lax
from jax.experimental import pallas as pl
from jax.experimental.pallas import tpu as pltpu

F32 = jnp.float32
BF16 = jnp.bfloat16

D_MODEL = 1024
N_META = 16
HEADS = 4
NOPE = 128
ROPE = 64
VDIM = 128
QK_PAD = 256
Q_RANK = 256
KV_RANK = 128
CONV_W = 512
CONV_GROUP = 64
ROPE_THETA = 10000.0
EPS = 1e-6
ATTN_SCALE = (NOPE + ROPE) ** -0.5
IN_DIM = 3008
IN_PAD = 3072
HEAD_ROWS = Q_RANK + KV_RANK + ROPE
IN_HEAD = 512
IN_TAIL = IN_PAD - IN_HEAD
BLK_ZA, BLK_CB, BLK_CC, BLK_CH, BLK_ZC = 0, 1, 2, 3, 4
NEG_INF = -1e30

ADAM_LR = 0.001
ADAM_B1 = 0.9
ADAM_B2 = 0.999
ADAM_EPS = 1e-08
ADAM_WD = 0.01
ADAM_STEP = 10

ROW_TILE = 512
ATTN_TILE = 256
VMEM_LIMIT = 56 * 1024 * 1024

NT = (((1,), (1,)), ((), ()))
TN = (((0,), (0,)), ((), ()))


def _cparams(*sem):
    return pltpu.CompilerParams(dimension_semantics=sem, vmem_limit_bytes=VMEM_LIMIT)


def _dot(a, b):
    return jnp.dot(a, b, preferred_element_type=F32)


def _dot_nt(a, b):
    return lax.dot_general(a, b, NT, preferred_element_type=F32)


def _dot_tn(a, b):
    return lax.dot_general(a, b, TN, preferred_element_type=F32)


def _rms(x, g):
    r = lax.rsqrt(jnp.mean(x * x, axis=-1, keepdims=True) + EPS)
    return x * r * g, r


def _rms_bwd(dy, x, r, g):
    xh = x * r
    dyg = dy * g
    dx = r * (dyg - xh * jnp.mean(dyg * xh, axis=-1, keepdims=True))
    return dx, dy * xh


def _sigmoid(z):
    return 1.0 / (1.0 + jnp.exp(-z))


def _rope(b, c, sa, sb):
    return b * c + pltpu.roll(b, 96, 1) * sa + pltpu.roll(b, 32, 1) * sb


def _rope_bwd(d, c, sa, sb):
    return d * c + pltpu.roll(d * sa, 32, 1) + pltpu.roll(d * sb, 96, 1)


def _group_mean(x, gmat):
    hi = x.astype(BF16)
    lo = (x - hi.astype(F32)).astype(BF16)
    return _dot(hi, gmat) + _dot(lo, gmat)


def _row_of(col, rows):
    return jnp.transpose(jnp.broadcast_to(col, (rows, 128)))[0:1, :]


def _rope_tables(n_pos):
    half = ROPE // 2
    inv_freq = (np.float32(1.0) / (np.float32(ROPE_THETA) ** (np.arange(half, dtype=np.float32) / np.float32(half))))
    ang = np.arange(n_pos, dtype=np.float32)[:, None] * inv_freq.astype(np.float32)[None, :]
    cos, sin = np.cos(ang).astype(np.float32), np.sin(ang).astype(np.float32)
    z = np.zeros((n_pos, half), np.float32)
    c = np.concatenate([cos, cos, z, z], axis=1)
    sa = np.concatenate([-sin, z, z, z], axis=1)
    sb = np.concatenate([z, sin, z, z], axis=1)
    return jnp.asarray(c), jnp.asarray(sa), jnp.asarray(sb)


W_IN_PIECES_1 = ((0, 80, 752, 0, 64, 0), (384, 64, 752, 384, 448, 1), (448, 16, 752, 512, 512, 1))
W_IN_PIECES_2 = ((80, 304, 752, 80, 144, 0), (464, 288, 752, 528, 528, 1))
W_Q_PIECES = ((0, 96, 256, 0, 0, 0), (96, 96, 256, 96, 96, 1))
W_KV_PIECES = ((0, 128, 128, 0, 0, 0), (128, 128, 128, 512, 512, 1))
W_OUT_PIECES = ((0, 128, 256, 0, 0, 0), (128, 128, 256, 128, 128, 1))


class _StagedGather:
    STAGES = 4

    @staticmethod
    def steps(n_steps):
        return (0, 5 * n_steps // 8, 7 * n_steps // 8, n_steps - 1)

    def __init__(self, pieces, zero_rows=None):
        self.pieces = pieces
        self.zero_rows = zero_rows

    def scratch(self):
        nk, dma = len(self.pieces), pltpu.SemaphoreType.DMA
        return [dma((nk, 3)), dma((nk, 3)), dma((nk, 3)), dma((nk, 3)), dma((nk,))]

    def vmem_scratch(self, shard_shape, out_shape):
        return [pltpu.VMEM(shard_shape, BF16), pltpu.VMEM(out_shape, BF16),
                pltpu.SemaphoreType.DMA((4 * len(self.pieces) + 2,))] + self.scratch()

    def run_vmem(self, stage, shard_ref, out_ref, scr):
        src_scr, land_scr, io_sems = scr[:3]
        spans = []
        for _, nr, per, first, rest, _ in self.pieces:
            spans += [(per * q + (first if q == 0 else rest), nr) for q in range(4)]
        if self.zero_rows is not None:
            spans.append(self.zero_rows)
        flush = [pltpu.make_async_copy(land_scr.at[r0:r0 + nr], out_ref.at[r0:r0 + nr], io_sems.at[n])
                 for n, (r0, nr) in enumerate(spans)]
        if stage == 0:
            load = pltpu.make_async_copy(shard_ref, src_scr, io_sems.at[len(spans)])
            load.start()
            if self.zero_rows is not None:
                r0, nr = self.zero_rows
                land_scr[r0:r0 + nr, :] = jnp.zeros((nr, land_scr.shape[1]), BF16)
            load.wait()
        if stage < self.STAGES:
            self.run(stage, src_scr, land_scr, scr[3:])
        for cp in flush:
            if stage == self.STAGES - 1:
                cp.start()
            if stage == self.STAGES:
                cp.wait()

    def run(self, stage, src_ref, out_ref, scr):
        send_sems, recv_sems, fwd_send, fwd_recv, loc_sems = scr
        pieces = self.pieces
        nk = len(pieces)
        x, y, c = lax.axis_index("x"), lax.axis_index("y"), lax.axis_index("c")
        mine = 2 * x + y
        chips = [(1 - x, y), (x, 1 - y), (1 - x, 1 - y)]
        chip_of = [2 * px + py for px, py in chips]
        mesh = pl.DeviceIdType.MESH

        def src(k):
            s0, nr = pieces[k][0], pieces[k][1]
            return src_ref.at[s0:s0 + nr]

        def dst(k, q):
            _, nr, per, first, rest, _ = pieces[k]
            row = per * q + first + (rest - first) * jnp.minimum(q, 1)
            return out_ref.at[pl.ds(pl.multiple_of(row, 16), nr)]

        def ici(k, j, q):
            px, py = chips[j]
            return pltpu.make_async_remote_copy(
                src_ref=src(k), dst_ref=dst(k, q), send_sem=send_sems.at[k, j], recv_sem=recv_sems.at[k, j],
                device_id=(px, py, c), device_id_type=mesh)

        def fwd(k, j):
            ref = dst(k, chip_of[j])
            return pltpu.make_async_remote_copy(
                src_ref=ref, dst_ref=ref, send_sem=fwd_send.at[k, j], recv_sem=fwd_recv.at[k, j],
                device_id=(x, y, 1 - c), device_id_type=mesh)

        def relay(k, half):
            ref = dst(k, chip_of[half])
            px, py = chips[1 - half]
            return pltpu.make_async_remote_copy(
                src_ref=ref, dst_ref=ref, send_sem=send_sems.at[k, 2], recv_sem=recv_sems.at[k, 2],
                device_id=(px, py, c), device_id_type=mesh)

        local = [pltpu.make_async_copy(src(k), dst(k, mine), loc_sems.at[k]) for k in range(nk)]
        if stage == 0:
            for cp in local:
                cp.start()
        if stage == 3:
            for cp in local:
                cp.wait()
        for half in (0, 1):
            @pl.when(c == half)
            def _(half=half):
                my_k = [k for k in range(nk) if pieces[k][5] == half]
                other_k = [k for k in range(nk) if pieces[k][5] != half]
                for k in my_k:
                    if stage == 0:
                        for j in range(2):
                            ici(k, j, mine).start()
                    elif stage == 1:
                        for j in (half, 1 - half):
                            ici(k, j, chip_of[j]).wait_recv()
                            if j == half:
                                relay(k, half).start()
                            fwd(k, j).start()
                    elif stage == 2:
                        ici(k, 2, chip_of[2]).wait_recv()
                        fwd(k, 2).start()
                    else:
                        for j in range(2):
                            ici(k, j, mine).wait_send()
                        relay(k, half).wait_send()
                        for j in range(3):
                            fwd(k, j).wait_send()
                if stage == 3:
                    for k in other_k:
                        for j in range(3):
                            fwd(k, j).wait_recv()


def _fwd_proj(x2d, meta, tabs, tabs_m, norm_g, w_head, q_norm_g, wq_p, kv_norm_g, wkv_p, w_out_shard, w_in_shard,
              nb, s, tm):
    nt = s // tm
    n = nb * nt
    n_steps = n + 1
    c_t, sa_t, sb_t = tabs
    cm_t, sam_t, sbm_t = tabs_m
    gat = _StagedGather(W_OUT_PIECES)
    gat_in = _StagedGather(W_IN_PIECES_1)
    n_sems = len(gat.scratch())
    assert n_steps >= 3

    def body(x_ref, c_ref, sa_ref, sb_ref, mt_ref, cm_ref, sam_ref, sbm_ref,
             g_ref, w_ref, gq_ref, wq_ref, gkv_ref, wkv_ref, wos_ref, wis_ref,
             p_ref, q_ref, k_ref, v_ref, pm_ref, km_ref, vm_ref, wo_ref, wi_ref, *scr):
        gat_scr, gat_in_scr = scr[:n_sems], scr[n_sems:]
        i = pl.program_id(0)
        for stage, at in enumerate(_StagedGather.steps(n_steps)):
            @pl.when(i == at)
            def _(stage=stage):
                gat_in.run_vmem(stage, wis_ref, wi_ref, gat_in_scr)
                gat.run(stage, wos_ref, wo_ref, gat_scr)

        def project(xv, c, sa, sb, p_out, q_out, k_out, v_out):
            u, _ = _rms(xv, g_ref[...])
            p = _dot_nt(u.astype(BF16), w_ref[...])
            p_out[...] = p
            qn, _ = _rms(p[:, 0:Q_RANK], gq_ref[...])
            q = _dot_nt(qn.astype(BF16), wq_ref[...])
            kvn, _ = _rms(p[:, Q_RANK:Q_RANK + KV_RANK], gkv_ref[...])
            kv = _dot_nt(kvn.astype(BF16), wkv_ref[...])
            kpe = _rope(p[:, 384:512], c, sa, sb)
            for h in range(HEADS):
                if q_out is not None:
                    pe = _rope(q[:, QK_PAD * h + NOPE:QK_PAD * (h + 1)], c, sa, sb)
                    qh = jnp.concatenate([q[:, QK_PAD * h:QK_PAD * h + NOPE], pe], axis=1)
                    q_out[0, h] = (qh * ATTN_SCALE).astype(BF16)
                k_out[0, h] = jnp.concatenate([kv[:, NOPE * h:NOPE * (h + 1)], kpe], axis=1).astype(BF16)
                v_out[0, h] = kv[:, 512 + VDIM * h:512 + VDIM * (h + 1)].astype(BF16)

        @pl.when(i < n)
        def _():
            project(x_ref[...], c_ref[...], sa_ref[...], sb_ref[...], p_ref, q_ref, k_ref, v_ref)

        @pl.when(i == n)
        def _():
            project(mt_ref[...], cm_ref[...], sam_ref[...], sbm_ref[...], pm_ref, None, km_ref, vm_ref)
            gat_in.run_vmem(gat_in.STAGES, wis_ref, wi_ref, gat_in_scr)

    cl = lambda i: jnp.minimum(i, n - 1)
    full = lambda a: pl.BlockSpec(a.shape, lambda i: (0,) * a.ndim)
    const = lambda shape: pl.BlockSpec(shape, lambda i: (0,) * len(shape))
    tab = pl.BlockSpec((tm, 128), lambda i: (cl(i) % nt, 0))
    hb = lambda w: pl.BlockSpec((1, HEADS, tm, w), lambda i: (cl(i) // nt, 0, cl(i) % nt, 0))
    whole = pl.BlockSpec(memory_space=pl.ANY)
    return pl.pallas_call(
        body, name="fwd_proj", grid=(n_steps,),
        in_specs=[pl.BlockSpec((tm, D_MODEL), lambda i: (cl(i), 0)), tab, tab, tab,
                  full(meta), full(cm_t), full(sam_t), full(sbm_t),
                  full(norm_g), full(w_head), full(q_norm_g), full(wq_p), full(kv_norm_g), full(wkv_p), whole, whole],
        out_specs=[pl.BlockSpec((tm, IN_HEAD), lambda i: (cl(i), 0)), hb(QK_PAD), hb(QK_PAD), hb(VDIM),
                   const((N_META, IN_HEAD)), const((1, HEADS, N_META, QK_PAD)), const((1, HEADS, N_META, VDIM)),
                   whole, whole],
        out_shape=[jax.ShapeDtypeStruct((nb * s, IN_HEAD), F32),
                   jax.ShapeDtypeStruct((nb, HEADS, s, QK_PAD), BF16),
                   jax.ShapeDtypeStruct((nb, HEADS, s, QK_PAD), BF16),
                   jax.ShapeDtypeStruct((nb, HEADS, s, VDIM), BF16),
                   jax.ShapeDtypeStruct((N_META, IN_HEAD), F32),
                   jax.ShapeDtypeStruct((1, HEADS, N_META, QK_PAD), BF16),
                   jax.ShapeDtypeStruct((1, HEADS, N_META, VDIM), BF16),
                   jax.ShapeDtypeStruct((D_MODEL, D_MODEL), BF16),
                   jax.ShapeDtypeStruct((IN_PAD, D_MODEL), BF16)],
        scratch_shapes=gat.scratch() + gat_in.vmem_scratch(w_in_shard.shape, (IN_PAD, D_MODEL)),
        compiler_params=_cparams("arbitrary"),
    )(x2d, c_t, sa_t, sb_t, meta, cm_t, sam_t, sbm_t, norm_g, w_head, q_norm_g, wq_p, kv_norm_g, wkv_p, w_out_shard,
      w_in_shard)


def _attn_fwd(q, k, v, km, vm, w_in_shard, w_in_part, nb, s, tq):
    nq = s // tq
    n_steps = nb * HEADS
    gat = _StagedGather(W_IN_PIECES_2, zero_rows=(HEAD_ROWS, IN_HEAD - HEAD_ROWS))
    assert n_steps >= 3

    def body(q_ref, k_ref, v_ref, km_ref, vm_ref, ws_ref, _, o_ref, lse_ref, w_ref, s_scr, p_scr, *gat_scr):
        step = pl.program_id(0) * HEADS + pl.program_id(1)
        for stage, at in enumerate(_StagedGather.steps(n_steps)[:2]):
            @pl.when(step == at)
            def _(stage=stage):
                gat.run_vmem(stage, ws_ref, w_ref, gat_scr)

        row = lax.broadcasted_iota(jnp.int32, (tq, tq), 0)
        col = lax.broadcasted_iota(jnp.int32, (tq, tq), 1)
        def scores(i):
            slot = i % 2
            qi = q_ref[0, 0, i * tq:(i + 1) * tq, :]
            sm = _dot_nt(qi, km_ref[0, 0])
            m128 = None
            for j in range(i + 1):
                sc = _dot_nt(qi, k_ref[0, 0, j * tq:(j + 1) * tq, :])
                if j == i:
                    sc = jnp.where(col <= row, sc, NEG_INF)
                s_scr[slot, :, j * tq:(j + 1) * tq] = sc
                mx = sc[:, 0:128]
                for c0 in range(128, tq, 128):
                    mx = jnp.maximum(mx, sc[:, c0:c0 + 128])
                m128 = mx if m128 is None else jnp.maximum(m128, mx)
            return sm, jnp.maximum(jnp.max(m128, axis=1, keepdims=True), jnp.max(sm, axis=1, keepdims=True))

        def weighted_sum(i, pm, l):
            n = (i + 1) * tq
            acc = _dot(p_scr[i % 2, :, 0:n], v_ref[0, 0, 0:n, :]) + _dot(pm.astype(BF16), vm_ref[0, 0])
            o_ref[0, 0, i * tq:(i + 1) * tq, :] = acc / l

        nxt, pending = scores(0), None
        for i in range(nq):
            slot = i % 2
            sm, m = nxt
            if i + 1 < nq:
                nxt = scores(i + 1)
            pm = jnp.exp(sm - m)
            l128 = None
            for j in range(i + 1):
                p = jnp.exp(s_scr[slot, :, j * tq:(j + 1) * tq] - m)
                p_scr[slot, :, j * tq:(j + 1) * tq] = p.astype(BF16)
                ps = p[:, 0:128]
                for c0 in range(128, tq, 128):
                    ps = ps + p[:, c0:c0 + 128]
                l128 = ps if l128 is None else l128 + ps
            l = jnp.sum(l128, axis=1, keepdims=True) + jnp.sum(pm, axis=1, keepdims=True)
            lse_ref[0, 0, :, i * tq:(i + 1) * tq] = _row_of(m + jnp.log(l), tq)
            if pending is not None:
                weighted_sum(*pending)
            pending = (i, pm, l)
        weighted_sum(*pending)

        @pl.when(step == n_steps - 1)
        def _():
            for stage in range(2, gat.STAGES + 1):
                gat.run_vmem(stage, ws_ref, w_ref, gat_scr)

    hblk = lambda w: pl.BlockSpec((1, 1, s, w), lambda b, h: (b, h, 0, 0))
    mblk = lambda w: pl.BlockSpec((1, 1, N_META, w), lambda b, h: (0, h, 0, 0))
    whole = pl.BlockSpec(memory_space=pl.ANY)
    return pl.pallas_call(
        body, name="attn_fwd", grid=(nb, HEADS),
        in_specs=[hblk(QK_PAD), hblk(QK_PAD), hblk(VDIM), mblk(QK_PAD), mblk(VDIM), whole, whole],
        out_specs=[hblk(VDIM), pl.BlockSpec((1, 1, 1, s), lambda b, h: (b, h, 0, 0)), whole],
        out_shape=[jax.ShapeDtypeStruct((nb, HEADS, s, VDIM), F32),
                   jax.ShapeDtypeStruct((nb, HEADS, 1, s), F32),
                   jax.ShapeDtypeStruct(w_in_part.shape, BF16)],
        input_output_aliases={6: 2},
        scratch_shapes=[pltpu.VMEM((2, tq, s), F32), pltpu.VMEM((2, tq, s), BF16)]
        + gat.vmem_scratch(w_in_shard.shape, w_in_part.shape),
        compiler_params=_cparams("arbitrary", "arbitrary"),
    )(q, k, v, km, vm, w_in_shard, w_in_part)


def _shift_rows(a, prev, n_rows):
    rid = lax.broadcasted_iota(jnp.int32, a.shape, 0)
    a1 = jnp.where(rid == 0, prev[7:8, :], pltpu.roll(a, 1, 0))
    a2 = jnp.where(rid == 0, prev[6:7, :], jnp.where(rid == 1, prev[7:8, :], pltpu.roll(a, 2, 0)))
    return a1, a2


def _attn_gate(o, za, ga_h):
    on, r = _rms(o, ga_h)
    return on * (za * _sigmoid(za)), on, r


def _out_fwd_bwd(x2d, tgt2d, o, meta, norm_g, w_in_p, conv_w, ga, gc, gmat, w_out, gf, nb, s, tm):
    nt = s // tm
    r = nb * s

    def body(x_ref, t_ref, o_ref, mt_ref, g_ref, wi_ref, cw_ref, ga_ref, gc_ref, gm_ref, w_ref, gf_ref,
             dh_ref, dy_ref, dw_ref, dgf_ref, loss_ref, p_ref, pm_ref, last_cc):
        i = pl.program_id(0)
        blk = lambda ref, j, rows=slice(None): ref[rows, 512 * j:512 * (j + 1)]

        def tail(xv):
            u, _ = _rms(xv, g_ref[...])
            return _dot_nt(u.astype(BF16), wi_ref[IN_HEAD:IN_PAD, :])

        @pl.when(i == 0)
        def _():
            dw_ref[...] = jnp.zeros_like(dw_ref)
            dgf_ref[...] = jnp.zeros_like(dgf_ref)
            loss_ref[...] = jnp.zeros_like(loss_ref)
            last_cc[...] = jnp.zeros_like(last_cc)
            pm_ref[...] = tail(mt_ref[...])

        u16 = _rms(x_ref[...], g_ref[...])[0].astype(BF16)

        def project(j):
            p_ref[:, 512 * j:512 * (j + 1)] = _dot_nt(u16, wi_ref[IN_HEAD + 512 * j:IN_HEAD + 512 * (j + 1), :])

        project(BLK_ZA)
        project(BLK_CC)
        project(BLK_CH)
        ya = []
        for h in range(HEADS):
            y, _, _ = _attn_gate(o_ref[0, h], p_ref[:, 512 * BLK_ZA + VDIM * h:512 * BLK_ZA + VDIM * (h + 1)],
                                 ga_ref[:, VDIM * h:VDIM * (h + 1)])
            ya.append(y)
        project(BLK_CB)
        project(BLK_ZC)
        cc = blk(p_ref, BLK_CC) * blk(p_ref, BLK_CH)
        meta_cc = blk(pm_ref, BLK_CC, slice(8, 16)) * blk(pm_ref, BLK_CH, slice(8, 16))
        prev = jnp.where(i % nt == 0, meta_cc, last_cc[...])
        last_cc[...] = cc[tm - 8:tm, :]
        cc1, cc2 = _shift_rows(cc, prev, tm)
        yc = blk(p_ref, BLK_CB) * (cw_ref[0:1, :] * cc2 + cw_ref[1:2, :] * cc1 + cw_ref[2:3, :] * cc)
        rg = lax.rsqrt(_group_mean(yc * yc, gm_ref[...]) + EPS)
        zc = blk(p_ref, BLK_ZC)
        yconv = yc * rg * gc_ref[...] * (zc * _sigmoid(zc))
        ycat = jnp.concatenate(ya + [yconv], axis=1).astype(BF16)
        h2 = x_ref[...] + _dot(ycat, w_ref[...])
        gfv = gf_ref[...]
        y, r2 = _rms(h2, gfv)
        e = y - t_ref[...]
        loss_ref[...] += 0.5 * jnp.sum(e * e) / D_MODEL
        dyv = e * (1.0 / D_MODEL)
        dh2, dgf = _rms_bwd(dyv, h2, r2, gfv)
        dgf_ref[...] += jnp.sum(dgf, axis=0, keepdims=True)
        dh_ref[...] = dh2
        dhb = dh2.astype(BF16)
        dy_ref[...] = _dot_nt(dhb, w_ref[...])
        dw_ref[...] += _dot_tn(ycat, dhb)

    row = lambda w: pl.BlockSpec((tm, w), lambda i: (i, 0))
    const = lambda shape: pl.BlockSpec(shape, lambda i: (0,) * len(shape))
    full = lambda a: const(a.shape)
    return pl.pallas_call(
        body, name="out_fwd_bwd", grid=(nb * nt,),
        in_specs=[row(D_MODEL), row(D_MODEL),
                  pl.BlockSpec((1, HEADS, tm, VDIM), lambda i: (i // nt, 0, i % nt, 0)),
                  full(meta), full(norm_g), full(w_in_p),
                  full(conv_w), full(ga), full(gc), full(gmat), full(w_out), full(gf)],
        out_specs=[row(D_MODEL), row(D_MODEL), const((D_MODEL, D_MODEL)), const((1, D_MODEL)), const((1, 128)),
                   row(IN_TAIL), const((N_META, IN_TAIL))],
        out_shape=[jax.ShapeDtypeStruct((r, D_MODEL), F32), jax.ShapeDtypeStruct((r, D_MODEL), F32),
                   jax.ShapeDtypeStruct((D_MODEL, D_MODEL), F32), jax.ShapeDtypeStruct((1, D_MODEL), F32),
                   jax.ShapeDtypeStruct((1, 128), F32),
                   jax.ShapeDtypeStruct((r, IN_TAIL), F32), jax.ShapeDtypeStruct((N_META, IN_TAIL), F32)],
        scratch_shapes=[pltpu.VMEM((8, 512), F32)],
        compiler_params=_cparams("arbitrary"),
    )(x2d, tgt2d, o, meta, norm_g, w_in_p, conv_w, ga, gc, gmat, w_out, gf)


def _gate_bwd(dycat, o, p, pm, conv_w, ga, gc, gmat, nb, s, tm):
    nt = s // tm
    r = nb * s
    ext = tm + 8
    prev_idx = lambda i: jnp.maximum(i * (tm // 8) - 1, 0)
    next_idx = lambda i: jnp.minimum((i + 1) * (tm // 8), r // 8 - 1)

    def body(dya_ref, dyc_ref, dycn_ref, o_ref, za_ref, cb_ref, cbn_ref, cc_ref, ccp_ref, ccn_ref,
             ch_ref, chp_ref, chn_ref, zc_ref, zcn_ref, mc_ref, mh_ref, cw_ref, ga_ref, gc_ref, gm_ref,
             dpb_ref, do_ref, dl_ref, dccm_ref, dga_ref, dgc_ref, dcw_ref):
        i = pl.program_id(0)

        @pl.when(i == 0)
        def _():
            dga_ref[...] = jnp.zeros_like(dga_ref)
            dgc_ref[...] = jnp.zeros_like(dgc_ref)
            dcw_ref[...] = jnp.zeros_like(dcw_ref)

        dga = []
        for h in range(HEADS):
            hs = slice(VDIM * h, VDIM * (h + 1))
            oh, za, gah, dya = o_ref[0, h], za_ref[:, hs], ga_ref[:, hs], dya_ref[:, hs]
            sg = _sigmoid(za)
            on, ro = _rms(oh, gah)
            don = dya * (za * sg)
            dpb_ref[:, hs] = (dya * on * (sg * (1.0 + za * (1.0 - sg)))).astype(BF16)
            do, dg = _rms_bwd(don, oh, ro, gah)
            dga.append(jnp.sum(dg, axis=0, keepdims=True))
            dob = do.astype(BF16)
            do_ref[0, h] = dob
            dl_ref[0, h] = _row_of(jnp.sum(dob.astype(F32) * oh, axis=1, keepdims=True), tm)
        dga_ref[...] += jnp.concatenate(dga, axis=1)

        cat = lambda a, b: jnp.concatenate([a[...], b[...]], axis=0)
        cch = cat(cc_ref, ccn_ref)
        chh = cat(ch_ref, chn_ref)
        cb = cat(cb_ref, cbn_ref)
        zc = cat(zc_ref, zcn_ref)
        dy = cat(dyc_ref, dycn_ref)
        first = i % nt == 0
        last = i % nt == nt - 1
        cc = cch * chh
        prev = jnp.where(first, mc_ref[8:16, :] * mh_ref[8:16, :], ccp_ref[...] * chp_ref[...])
        cc1, cc2 = _shift_rows(cc, prev, ext)
        w0, w1, w2 = cw_ref[0:1, :], cw_ref[1:2, :], cw_ref[2:3, :]
        dw = w0 * cc2 + w1 * cc1 + w2 * cc
        yc = cb * dw
        rg = lax.rsqrt(_group_mean(yc * yc, gm_ref[...]) + EPS)
        ych = yc * rg
        gcv = gc_ref[...]
        sg = _sigmoid(zc)
        dycn = dy * (zc * sg)
        dzc = dy * (ych * gcv) * (sg * (1.0 + zc * (1.0 - sg)))
        dgc_ref[...] += jnp.sum((dycn * ych)[:tm], axis=0, keepdims=True)
        dycg = dycn * gcv
        dyc = rg * (dycg - ych * _group_mean(dycg * ych, gm_ref[...]))
        rid = lax.broadcasted_iota(jnp.int32, (ext, CONV_W), 0)
        ddw = jnp.where(jnp.logical_and(last, rid >= tm), 0.0, dyc * cb)
        dcb = dyc * dw
        dcc = w2 * ddw + w1 * pltpu.roll(ddw, ext - 1, 0) + w0 * pltpu.roll(ddw, ext - 2, 0)
        dpb_ref[:, 512:1024] = dcb[:tm].astype(BF16)
        dpb_ref[:, 1024:1536] = (dcc * chh)[:tm].astype(BF16)
        dpb_ref[:, 1536:2048] = (dcc * cch)[:tm].astype(BF16)
        dpb_ref[:, 2048:2560] = dzc[:tm].astype(BF16)
        rs = lambda a: jnp.sum(a[:tm], axis=0, keepdims=True)
        dcw_ref[0:1, :] += rs(ddw * cc2)
        dcw_ref[1:2, :] += rs(ddw * cc1)
        dcw_ref[2:3, :] += rs(ddw * cc)

        @pl.when(first)
        def _():
            d0, d1 = ddw[0:1, :], ddw[1:2, :]
            r8 = lax.broadcasted_iota(jnp.int32, (8, CONV_W), 0)
            dccm_ref[0] = jnp.where(r8 == 7, w1 * d0 + w0 * d1, jnp.where(r8 == 6, w0 * d0, 0.0))

    row = lambda j: pl.BlockSpec((tm, 512), lambda i: (i, j))
    prv = lambda j: pl.BlockSpec((8, 512), lambda i: (prev_idx(i), j))
    nxt = lambda j: pl.BlockSpec((8, 512), lambda i: (next_idx(i), j))
    mblk = lambda j: pl.BlockSpec((N_META, 512), lambda i: (0, j))
    full = lambda a: pl.BlockSpec(a.shape, lambda i: (0,) * a.ndim)
    hb = lambda w: pl.BlockSpec((1, HEADS, tm, w), lambda i: (i // nt, 0, i % nt, 0))
    acc = lambda rr: pl.BlockSpec((rr, 512), lambda i: (0, 0))
    return pl.pallas_call(
        body, name="gate_bwd", grid=(nb * nt,),
        in_specs=[row(0), row(1), nxt(1), hb(VDIM),
                  row(BLK_ZA), row(BLK_CB), nxt(BLK_CB), row(BLK_CC), prv(BLK_CC), nxt(BLK_CC),
                  row(BLK_CH), prv(BLK_CH), nxt(BLK_CH), row(BLK_ZC), nxt(BLK_ZC),
                  mblk(BLK_CC), mblk(BLK_CH), full(conv_w), full(ga), full(gc), full(gmat)],
        out_specs=[pl.BlockSpec((tm, 2560), lambda i: (i, 0)), hb(VDIM),
                   pl.BlockSpec((1, HEADS, 1, tm), lambda i: (i // nt, 0, 0, i % nt)),
                   pl.BlockSpec((1, 8, 512), lambda i: (i // nt, 0, 0)),
                   acc(1), acc(1), acc(8)],
        out_shape=[jax.ShapeDtypeStruct((r, 2560), BF16), jax.ShapeDtypeStruct((nb, HEADS, s, VDIM), BF16),
                   jax.ShapeDtypeStruct((nb, HEADS, 1, s), F32), jax.ShapeDtypeStruct((nb, 8, 512), F32),
                   jax.ShapeDtypeStruct((1, 512), F32), jax.ShapeDtypeStruct((1, 512), F32),
                   jax.ShapeDtypeStruct((8, 512), F32)],
        compiler_params=_cparams("arbitrary"),
    )(dycat, dycat, dycat, o, p, p, p, p, p, p, p, p, p, p, p, pm, pm, conv_w, ga, gc, gmat)


class _StagedReduce:
    LOC, PRE_S, PRE_R, ICI_S, ICI_R, POST_S, POST_R, OUT, N_SEM = 0, 1, 2, 3, 6, 9, 10, 11, 12

    def __init__(self, shard_shape):
        self.half = (shard_shape[0] // 2, shard_shape[1])

    def scratch(self):
        h = self.half
        return [pltpu.VMEM((4,) + h, F32), pltpu.VMEM((4,) + h, F32), pltpu.VMEM((4,) + h, BF16),
                pltpu.VMEM((3,) + h, BF16), pltpu.VMEM(h, F32), pltpu.SemaphoreType.DMA((self.N_SEM,))]

    def run(self, stage, pin, gout, scr):
        own, sib, wire, rbuf, fin, sems = scr
        r2 = self.half[0]
        x, y, c = lax.axis_index("x"), lax.axis_index("y"), lax.axis_index("c")
        mine = 2 * x + y
        sibling = (x, y, 1 - c)
        chips = [(1 - x, y), (x, 1 - y), (1 - x, 1 - y)]
        rows = lambda half: pl.ds(pl.multiple_of(half * r2, r2), r2)
        mesh = pl.DeviceIdType.MESH

        loc = pltpu.make_async_copy(pin.at[:, rows(c), :], own, sems.at[self.LOC])
        pre = pltpu.make_async_remote_copy(
            src_ref=pin.at[:, rows(1 - c), :], dst_ref=sib, send_sem=sems.at[self.PRE_S],
            recv_sem=sems.at[self.PRE_R], device_id=sibling, device_id_type=mesh)

        def ici(j):
            px, py = chips[j]
            return pltpu.make_async_remote_copy(
                src_ref=wire.at[2 * px + py], dst_ref=rbuf.at[j], send_sem=sems.at[self.ICI_S + j],
                recv_sem=sems.at[self.ICI_R + j], device_id=(px, py, c), device_id_type=mesh)

        def post(half):
            return pltpu.make_async_remote_copy(
                src_ref=fin, dst_ref=gout.at[rows(half), :], send_sem=sems.at[self.POST_S],
                recv_sem=sems.at[self.POST_R], device_id=sibling, device_id_type=mesh)

        keep = pltpu.make_async_copy(fin, gout.at[rows(c), :], sems.at[self.OUT])
        if stage == 0:
            loc.start()
            pre.start()
        elif stage == 1:
            loc.wait()
            pre.wait_recv()
            for blk in range(4):
                tot = own[blk] + sib[blk]
                own[blk] = tot
                wire[blk] = tot.astype(BF16)
            for j in range(3):
                ici(j).start()
        elif stage == 2:
            for j in range(3):
                ici(j).wait_recv()
            tot = own[mine]
            for j in range(3):
                tot = tot + rbuf[j].astype(F32)
            fin[...] = tot
            post(c).start()
            keep.start()
        else:
            post(1 - c).wait_recv()
            pre.wait_send()
            for j in range(3):
                ici(j).wait_send()
            post(c).wait_send()
            keep.wait()


def _attn_bwd(q, k, v, do, lse, delta, km, vm, early, nb, s, t):
    n = s // t
    ne = len(early)
    reds = [_StagedReduce(a.shape[1:]) for a in early]
    n_steps = HEADS * nb
    assert n_steps >= 4

    def body(q_ref, k_ref, v_ref, do_ref, lse_ref, dl_ref, km_ref, vm_ref, *rest):
        pin_refs, rest = rest[:ne], rest[ne:]
        dq_ref, dk_ref, dv_ref, dkm_ref, dvm_ref = rest[:5]
        gout_refs, (p_scr, ds_scr, dq_acc), red_scr = rest[5:5 + ne], rest[5 + ne:8 + ne], rest[8 + ne:]
        b = pl.program_id(1)
        step = pl.program_id(0) * nb + b
        for stage, at in enumerate((0, 1, n_steps - 2, n_steps - 1)):
            @pl.when(step == at)
            def _(stage=stage):
                for a, red in enumerate(reds):
                    red.run(stage, pin_refs[a], gout_refs[a], red_scr[6 * a:6 * a + 6])

        @pl.when(b == 0)
        def _():
            dkm_ref[...] = jnp.zeros_like(dkm_ref)
            dvm_ref[...] = jnp.zeros_like(dvm_ref)

        kr = lax.broadcasted_iota(jnp.int32, (t, t), 0)
        qc = lax.broadcasted_iota(jnp.int32, (t, t), 1)
        km_v, vm_v = km_ref[0, 0], vm_ref[0, 0]
        ptm = jnp.exp(_dot_nt(km_v, q_ref[0, 0]) - lse_ref[0, 0])
        dstm = (ptm * (_dot_nt(vm_v, do_ref[0, 0]) - dl_ref[0, 0])).astype(BF16)
        dkm_ref[0] += _dot(dstm, q_ref[0, 0])
        dvm_ref[0] += _dot(ptm.astype(BF16), do_ref[0, 0])
        dq_acc[...] = _dot_tn(dstm, km_v)
        def tiles(j):
            slot = j % 2
            kj = k_ref[0, 0, j * t:(j + 1) * t, :]
            vj = v_ref[0, 0, j * t:(j + 1) * t, :]
            def products(i):
                cs = slice(i * t, (i + 1) * t)
                return _dot_nt(kj, q_ref[0, 0, cs, :]), _dot_nt(vj, do_ref[0, 0, cs, :])

            nxt, pending = products(j), None
            for i in range(j, n):
                cs = slice(i * t, (i + 1) * t)
                st, dpt = nxt
                if i + 1 < n:
                    nxt = products(i + 1)
                if i == j:
                    st = jnp.where(kr <= qc, st, NEG_INF)
                pt = jnp.exp(st - lse_ref[0, 0, :, cs])
                dst = (pt * (dpt - dl_ref[0, 0, :, cs])).astype(BF16)
                p_scr[slot, :, cs] = pt.astype(BF16)
                ds_scr[slot, :, cs] = dst
                if pending is not None:
                    dq_acc[pending[0], :] += _dot_tn(pending[1], kj)
                pending = (cs, dst)
            dq_acc[pending[0], :] += _dot_tn(pending[1], kj)

        for j in range(n):
            slot = j % 2
            tiles(j)
            dv_ref[0, 0, j * t:(j + 1) * t, :] = _dot(p_scr[slot, :, j * t:s], do_ref[0, 0, j * t:s, :]).astype(BF16)
            dk_ref[0, 0, j * t:(j + 1) * t, :] = _dot(ds_scr[slot, :, j * t:s], q_ref[0, 0, j * t:s, :]).astype(BF16)
        dq_ref[0, 0] = dq_acc[...].astype(BF16)

    big = lambda w: pl.BlockSpec((1, 1, s, w), lambda h, b: (b, h, 0, 0))
    rowv = pl.BlockSpec((1, 1, 1, s), lambda h, b: (b, h, 0, 0))
    mk = lambda w: pl.BlockSpec((1, 1, N_META, w), lambda h, b: (0, h, 0, 0))
    mo = lambda w: pl.BlockSpec((1, N_META, w), lambda h, b: (h, 0, 0))
    return pl.pallas_call(
        body, name="attn_bwd", grid=(HEADS, nb),
        in_specs=[big(QK_PAD), big(QK_PAD), big(VDIM), big(VDIM), rowv, rowv, mk(QK_PAD), mk(VDIM)]
        + [pl.BlockSpec(memory_space=pl.ANY)] * ne,
        out_specs=[big(QK_PAD), big(QK_PAD), big(VDIM), mo(QK_PAD), mo(VDIM)]
        + [pl.BlockSpec(memory_space=pl.ANY)] * ne,
        out_shape=[jax.ShapeDtypeStruct((nb, HEADS, s, QK_PAD), BF16),
                   jax.ShapeDtypeStruct((nb, HEADS, s, QK_PAD), BF16),
                   jax.ShapeDtypeStruct((nb, HEADS, s, VDIM), BF16),
                   jax.ShapeDtypeStruct((HEADS, N_META, QK_PAD), F32),
                   jax.ShapeDtypeStruct((HEADS, N_META, VDIM), F32)]
        + [jax.ShapeDtypeStruct(a.shape[1:], F32) for a in early],
        scratch_shapes=[pltpu.VMEM((2, t, s), BF16), pltpu.VMEM((2, t, s), BF16), pltpu.VMEM((s, QK_PAD), F32)]
        + [sc for red in reds for sc in red.scratch()],
        compiler_params=_cparams("arbitrary", "arbitrary"),
    )(q, k, v, do, lse, delta, km, vm, *early)


def _up_bwd(dq, dk, dv, dkm, dvm, p, pm, tabs, tabs_m, wq_p, wkv_p, gq, gkv, nb, s, tm):
    nt = s // tm
    n = nb * nt
    c_t, sa_t, sb_t = tabs
    cm_t, sam_t, sbm_t = tabs_m

    def kv_path(dkh, dvh, pa, c, sa, sb, wkv, gkvv):
        dkpe = dkh[0][:, NOPE:]
        for h in range(1, HEADS):
            dkpe = dkpe + dkh[h][:, NOPE:]
        dkr = _rope_bwd(dkpe, c, sa, sb)
        dkv = jnp.concatenate([d[:, :NOPE] for d in dkh] + list(dvh), axis=1).astype(BF16)
        ckv = pa[:, Q_RANK:Q_RANK + KV_RANK]
        kvn, rkv = _rms(ckv, gkvv)
        dckv, dg = _rms_bwd(_dot(dkv, wkv), ckv, rkv, gkvv)
        return dckv, dkr, kvn.astype(BF16), dkv, jnp.sum(dg, axis=0, keepdims=True)

    def body(dq_ref, dk_ref, dv_ref, pa_ref, c_ref, sa_ref, sb_ref,
             dkm_ref, dvm_ref, pam_ref, cm_ref, sam_ref, sbm_ref,
             wq_ref, wkv_ref, gq_ref, gkv_ref,
             dpa_ref, dpam_ref, pq_ref, pkv_ref, dgq_ref, dgkv_ref, dwq_ref, dwkv_ref):
        i = pl.program_id(0)

        @pl.when(i == 0)
        def _():
            dwq_ref[...] = jnp.zeros_like(dwq_ref)
            dwkv_ref[...] = jnp.zeros_like(dwkv_ref)
            dgq_ref[...] = jnp.zeros_like(dgq_ref)
            dgkv_ref[...] = jnp.zeros_like(dgkv_ref)

        @pl.when(i < n)
        def _():
            c, sa, sb = c_ref[...], sa_ref[...], sb_ref[...]
            pa = pa_ref[...]
            parts = []
            for h in range(HEADS):
                dqh = dq_ref[0, h].astype(F32) * ATTN_SCALE
                parts += [dqh[:, :NOPE], _rope_bwd(dqh[:, NOPE:], c, sa, sb)]
            dql = jnp.concatenate(parts, axis=1).astype(BF16)
            cq = pa[:, 0:Q_RANK]
            gqv = gq_ref[...]
            qn, rq = _rms(cq, gqv)
            dwq_ref[...] += _dot_tn(dql, qn.astype(BF16))
            dcq, dg = _rms_bwd(_dot(dql, wq_ref[...]), cq, rq, gqv)
            dgq_ref[...] += jnp.sum(dg, axis=0, keepdims=True)
            dckv, dkr, kvn, dkv, dgk = kv_path([dk_ref[0, h].astype(F32) for h in range(HEADS)],
                                               [dv_ref[0, h].astype(F32) for h in range(HEADS)],
                                               pa, c, sa, sb, wkv_ref[...], gkv_ref[...])
            dwkv_ref[...] += _dot_tn(dkv, kvn)
            dgkv_ref[...] += dgk
            dpa_ref[...] = jnp.concatenate([dcq, dckv, dkr], axis=1).astype(BF16)

        @pl.when(i == n)
        def _():
            dckv, dkr, kvn, dkv, dgk = kv_path([dkm_ref[h] for h in range(HEADS)],
                                               [dvm_ref[h] for h in range(HEADS)],
                                               pam_ref[...], cm_ref[...], sam_ref[...], sbm_ref[...],
                                               wkv_ref[...], gkv_ref[...])
            dwkv_ref[...] += _dot_tn(dkv, kvn)
            dgkv_ref[...] += dgk
            dpam_ref[...] = jnp.concatenate([jnp.zeros((N_META, Q_RANK), F32), dckv, dkr], axis=1)
            for h in range(HEADS):
                pq_ref[h] = dwq_ref[QK_PAD * h:QK_PAD * h + NOPE + ROPE, :]
                pkv_ref[h, 0:NOPE, :] = dwkv_ref[NOPE * h:NOPE * (h + 1), :]
                pkv_ref[h, NOPE:NOPE + VDIM, :] = dwkv_ref[512 + VDIM * h:512 + VDIM * (h + 1), :]

    cl = lambda i: jnp.minimum(i, n - 1)
    hb = lambda w: pl.BlockSpec((1, HEADS, tm, w), lambda i: (cl(i) // nt, 0, cl(i) % nt, 0))
    tab = pl.BlockSpec((tm, 128), lambda i: (cl(i) % nt, 0))
    full = lambda a: pl.BlockSpec(a.shape, lambda i: (0,) * a.ndim)
    const = lambda shape: pl.BlockSpec(shape, lambda i: (0,) * len(shape))
    return pl.pallas_call(
        body, name="up_bwd", grid=(n + 1,),
        in_specs=[hb(QK_PAD), hb(QK_PAD), hb(VDIM), pl.BlockSpec((tm, 512), lambda i: (cl(i), 0)), tab, tab, tab,
                  full(dkm), full(dvm), pl.BlockSpec((N_META, 512), lambda i: (0, 0)),
                  full(cm_t), full(sam_t), full(sbm_t), full(wq_p), full(wkv_p), full(gq), full(gkv)],
        out_specs=[pl.BlockSpec((tm, 512), lambda i: (cl(i), 0)), const((N_META, 512)),
                   const((HEADS, NOPE + ROPE, Q_RANK)), const((HEADS, NOPE + VDIM, KV_RANK)),
                   const((1, Q_RANK)), const((1, KV_RANK))],
        out_shape=[jax.ShapeDtypeStruct((nb * s, 512), BF16), jax.ShapeDtypeStruct((N_META, 512), F32),
                   jax.ShapeDtypeStruct((HEADS, NOPE + ROPE, Q_RANK), F32),
                   jax.ShapeDtypeStruct((HEADS, NOPE + VDIM, KV_RANK), F32),
                   jax.ShapeDtypeStruct((1, Q_RANK), F32), jax.ShapeDtypeStruct((1, KV_RANK), F32)],
        scratch_shapes=[pltpu.VMEM((HEADS * QK_PAD, Q_RANK), F32), pltpu.VMEM((1024, KV_RANK), F32)],
        compiler_params=_cparams("arbitrary"),
    )(dq, dk, dv, p, c_t, sa_t, sb_t, dkm, dvm, pm, cm_t, sam_t, sbm_t, wq_p, wkv_p, gq, gkv)


def _in_bwd(x2d, dh2, dpa, dpb, meta, dpam, dccm, pm, w_in_p, norm_g, nb, s, tm):
    nt = s // tm
    n = nb * nt

    def body(x_ref, dh_ref, dpa_ref, dpb_ref, mt_ref, dpam_ref, dccm_ref, mc_ref, mh_ref, w_ref, g_ref,
             gx_ref, gm_ref, dw_hbm, dg_ref, acc_ref, sems):
        i = pl.program_id(0)

        @pl.when(i == 0)
        def _():
            acc_ref[...] = jnp.zeros_like(acc_ref)
            dg_ref[...] = jnp.zeros_like(dg_ref)

        def rows(x, dp, dres):
            g = g_ref[...]
            dpb16 = dp.astype(BF16)
            du = _dot(dpb16, w_ref[...])
            u, r1 = _rms(x, g)
            acc_ref[...] += _dot_tn(dpb16, u.astype(BF16))
            dx, dg = _rms_bwd(du, x, r1, g)
            dg_ref[...] += jnp.sum(dg, axis=0, keepdims=True)
            return dx if dres is None else dx + dres

        @pl.when(i < n)
        def _():
            dp = jnp.concatenate([dpa_ref[...], dpb_ref[...]], axis=1)
            gx_ref[...] = rows(x_ref[...], dp, dh_ref[...])

        @pl.when(i == n)
        def _():
            dcc = dccm_ref[0]
            for b in range(1, nb):
                dcc = dcc + dccm_ref[b]
            z8 = jnp.zeros((8, CONV_W), F32)
            dc = jnp.concatenate([z8, dcc * mh_ref[8:16, :]], axis=0)
            dh = jnp.concatenate([z8, dcc * mc_ref[8:16, :]], axis=0)
            z = jnp.zeros((N_META, CONV_W), F32)
            dp = jnp.concatenate([dpam_ref[...], z, z, dc, dh, z], axis=1)
            gm_ref[...] = rows(mt_ref[...], dp, None)
            per = IN_DIM // 4
            cps = [pltpu.make_async_copy(acc_ref.at[0:448], dw_hbm.at[0, 0:448], sems.at[0]),
                   pltpu.make_async_copy(acc_ref.at[512:per + 64], dw_hbm.at[0, 448:per], sems.at[1])]
            for qq in range(1, 4):
                cps.append(pltpu.make_async_copy(acc_ref.at[per * qq + 64:per * (qq + 1) + 64], dw_hbm.at[qq],
                                                 sems.at[qq + 1]))
            for cp in cps:
                cp.start()
            for cp in cps:
                cp.wait()

    cl = lambda i: jnp.minimum(i, n - 1)
    row = lambda w: pl.BlockSpec((tm, w), lambda i: (cl(i), 0))
    full = lambda a: pl.BlockSpec(a.shape, lambda i: (0,) * a.ndim)
    mblk = lambda j: pl.BlockSpec((N_META, 512), lambda i: (0, j))
    return pl.pallas_call(
        body, name="in_bwd", grid=(n + 1,),
        in_specs=[row(D_MODEL), row(D_MODEL), row(512), row(2560), full(meta), full(dpam), full(dccm),
                  mblk(BLK_CC), mblk(BLK_CH), full(w_in_p), full(norm_g)],
        out_specs=[row(D_MODEL), pl.BlockSpec((N_META, D_MODEL), lambda i: (0, 0)),
                   pl.BlockSpec(memory_space=pl.ANY), pl.BlockSpec((1, D_MODEL), lambda i: (0, 0))],
        out_shape=[jax.ShapeDtypeStruct((nb * s, D_MODEL), F32), jax.ShapeDtypeStruct((N_META, D_MODEL), F32),
                   jax.ShapeDtypeStruct((4, IN_DIM // 4, D_MODEL), F32), jax.ShapeDtypeStruct((1, D_MODEL), F32)],
        scratch_shapes=[pltpu.VMEM((IN_PAD, D_MODEL), F32), pltpu.SemaphoreType.DMA((5,))],
        compiler_params=_cparams("arbitrary"),
    )(x2d, dh2, dpa, dpb, meta, dpam, dccm, pm, pm, w_in_p, norm_g)


def _gather_weights(w_in_shard, w_out_shard, split, pieces, out_rows, whole, zero_fills):
    ns, nw, nz = len(split), len(whole), len(zero_fills)
    flat = [(a, pc) for a in range(ns) for pc in pieces[a]]
    nk = len(flat)
    hh = HEAD_ROWS // 2
    assert hh % 16 == 0

    def body(*refs):
        ins, wins, zins = refs[2:2 + ns], refs[2 + ns:2 + ns + nw], refs[2 + ns + nw:2 + ns + nw + nz]
        n_in = 2 + ns + nw + nz
        head_ref, shard16, w_out16 = refs[n_in:n_in + 3]
        outs, wcat = refs[n_in + 3:n_in + 3 + ns], refs[n_in + 3 + ns:n_in + 3 + ns + nw]
        scr = refs[n_in + 3 + ns + nw:]
        stage, wouts = scr[:ns], scr[ns:ns + nw]
        (send_sems, recv_sems, fwd_send, fwd_recv, loc_sems, w_send, w_recv, w_loc, z_sems,
         h_send, h_recv, h_relay, h_pass) = scr[ns + nw:]
        x, y, c = lax.axis_index("x"), lax.axis_index("y"), lax.axis_index("c")
        mine = 2 * x + y
        chips = [(1 - x, y), (x, 1 - y), (1 - x, 1 - y)]
        chip_of = [2 * px + py for px, py in chips]
        shard16[...] = refs[0][...].astype(BF16)
        w_out16[...] = refs[1][...].astype(BF16)
        for a in range(ns):
            stage[a][...] = ins[a][...].astype(BF16)

        def head_rows(half):
            return pl.ds(pl.multiple_of(half * hh, 16), hh)

        def head_copy(turn):
            dest = (1 - c, c, c) if turn == 0 else (c, 1 - c, c)
            return pltpu.make_async_remote_copy(
                src_ref=shard16.at[head_rows(c)], dst_ref=head_ref.at[head_rows(c)], send_sem=h_send.at[turn],
                recv_sem=h_recv.at[0], device_id=dest, device_id_type=pl.DeviceIdType.MESH)

        def head_relay():
            ref = head_ref.at[head_rows(c)]
            return pltpu.make_async_remote_copy(
                src_ref=ref, dst_ref=ref, send_sem=h_relay.at[0], recv_sem=h_recv.at[0],
                device_id=(1, 1, c), device_id_type=pl.DeviceIdType.MESH)

        def head_pass(half):
            ref = head_ref.at[head_rows(half)]
            return pltpu.make_async_remote_copy(
                src_ref=ref, dst_ref=ref, send_sem=h_pass.at[0], recv_sem=h_pass.at[1],
                device_id=(x, y, 1 - c), device_id_type=pl.DeviceIdType.MESH)

        head_ref[HEAD_ROWS:IN_HEAD, :] = jnp.zeros((IN_HEAD - HEAD_ROWS, D_MODEL), BF16)

        @pl.when(mine == 0)
        def _():
            head_copy(0).start()
            head_ref[0:HEAD_ROWS, :] = shard16[0:HEAD_ROWS, :]

        def src(k):
            a, (s0, nr, _, _, _, _) = flat[k]
            return stage[a].at[s0:s0 + nr]

        def dst(k, q):
            a, (_, nr, per, first, rest, _) = flat[k]
            row = per * q + first + (rest - first) * jnp.minimum(q, 1)
            return outs[a].at[pl.ds(pl.multiple_of(row, 16), nr)]

        def ici(k, j, q):
            px, py = chips[j]
            return pltpu.make_async_remote_copy(
                src_ref=src(k), dst_ref=dst(k, q), send_sem=send_sems.at[k, j], recv_sem=recv_sems.at[k, j],
                device_id=(px, py, c), device_id_type=pl.DeviceIdType.MESH)

        def fwd(k, j):
            ref = dst(k, chip_of[j])
            return pltpu.make_async_remote_copy(
                src_ref=ref, dst_ref=ref, send_sem=fwd_send.at[k, j], recv_sem=fwd_recv.at[k, j],
                device_id=(x, y, 1 - c), device_id_type=pl.DeviceIdType.MESH)

        def wcopy(b, j, q):
            px, py = chips[j]
            return pltpu.make_async_remote_copy(
                src_ref=wins[b], dst_ref=wouts[b].at[q], send_sem=w_send.at[b, j], recv_sem=w_recv.at[b, j],
                device_id=(px, py, c), device_id_type=pl.DeviceIdType.MESH)

        local = [pltpu.make_async_copy(src(k), dst(k, mine), loc_sems.at[k]) for k in range(nk)]
        local += [pltpu.make_async_copy(wins[b], wouts[b].at[mine], w_loc.at[b]) for b in range(nw)]
        for z, (a, _, row0) in enumerate(zero_fills):
            local.append(pltpu.make_async_copy(zins[z], outs[a].at[row0:row0 + zins[z].shape[0]], z_sems.at[z]))
        wsends = [wcopy(b, j, mine) for b in range(nw) for j in range(3)]
        for cp in local + wsends:
            cp.start()

        for half in (0, 1):
            @pl.when(c == half)
            def _(half=half):
                my_k = [k for k in range(nk) if flat[k][1][5] == half]
                other_k = [k for k in range(nk) if flat[k][1][5] != half]
                sends = [ici(k, j, mine) for k in my_k for j in range(3)]
                for cp in sends:
                    cp.start()
                passed = []
                for k in my_k:
                    for j in range(3):
                        ici(k, j, chip_of[j]).wait_recv()
                        cp = fwd(k, j)
                        cp.start()
                        passed.append(cp)
                for k in other_k:
                    for j in range(3):
                        fwd(k, j).wait_recv()
                for cp in sends + passed:
                    cp.wait_send()

        for b in range(nw):
            for j in range(3):
                wcopy(b, j, chip_of[j]).wait_recv()
        for cp in wsends:
            cp.wait_send()
        for cp in local:
            cp.wait()
        for b in range(nw):
            cols = wins[b].shape[-1]
            for q in range(4):
                wcat[b][..., cols * q:cols * (q + 1)] = wouts[b][q]

        @pl.when(mine == 0)
        def _():
            head_copy(0).wait_send()
            head_copy(1).start()
            head_copy(1).wait_send()

        @pl.when(mine != 0)
        def _():
            hands_on = mine == 2 - c
            head_copy(0).wait_recv()

            @pl.when(hands_on)
            def _():
                head_relay().start()

            head_pass(c).start()
            head_pass(1 - c).wait_recv()
            head_pass(c).wait_send()

            @pl.when(hands_on)
            def _():
                head_relay().wait_send()

    vmem = pl.BlockSpec(memory_space=pltpu.VMEM)
    dma = pltpu.SemaphoreType.DMA
    zeros = [z for _, z, _ in zero_fills]
    return pl.pallas_call(
        body, name="gather_weights",
        in_specs=[vmem] * (2 + ns + nw + nz), out_specs=[vmem] * (3 + ns + nw),
        out_shape=([jax.ShapeDtypeStruct((IN_HEAD, D_MODEL), BF16), jax.ShapeDtypeStruct(w_in_shard.shape, BF16),
                    jax.ShapeDtypeStruct(w_out_shard.shape, BF16)]
                   + [jax.ShapeDtypeStruct((out_rows[a], split[a].shape[1]), BF16) for a in range(ns)]
                   + [jax.ShapeDtypeStruct(w.shape[:-1] + (4 * w.shape[-1],), w.dtype) for w in whole]),
        scratch_shapes=[pltpu.VMEM(a.shape, BF16) for a in split] + [pltpu.VMEM((4,) + w.shape, w.dtype) for w in whole]
        + [dma((nk, 3)), dma((nk, 3)), dma((nk, 3)), dma((nk, 3)), dma((nk,)),
           dma((nw, 3)), dma((nw, 3)), dma((nw,)), dma((nz,)),
           dma((2,)), dma((1,)), dma((1,)), dma((2,))],
        compiler_params=pltpu.CompilerParams(vmem_limit_bytes=VMEM_LIMIT),
    )(w_in_shard, w_out_shard, *split, *whole, *zeros)


def _reduce_grads(parts, small):
    n = len(parts)
    ns = len(small)
    shapes = [a.shape[1:] for a in parts]
    halves = [(sh[0] // 2, sh[1]) for sh in shapes]
    sm_blocks = [a.shape[1] // 128 for a, _ in small]
    sm_first = [sum(nr * nblk for (_, nr), nblk in zip(small[:k], sm_blocks[:k])) for k in range(ns)]
    sm_rows = -(-(sm_first[-1] + small[-1][1] * sm_blocks[-1]) // 8) * 8
    sm_shape = (sm_rows, 128)

    def body(*refs):
        pin, sm_in = refs[:n], refs[n:n + ns]
        gout, sm_out = refs[n + ns:2 * n + ns], refs[2 * n + ns]
        scr = refs[2 * n + ns + 1:]
        own, sib, wire, rbuf = scr[:n], scr[n:2 * n], scr[2 * n:3 * n], scr[3 * n:4 * n]
        (sbuf, send_sems, recv_sems, loc_sems, pre_send, pre_recv, post_send, post_recv,
         sm_send, sm_recv, sm_pack) = scr[4 * n:]
        x, y, c = lax.axis_index("x"), lax.axis_index("y"), lax.axis_index("c")
        mine = 2 * x + y
        sm_pack[...] = jnp.zeros(sm_shape, F32)
        for k, (_, nr) in enumerate(small):
            for i in range(nr):
                for j in range(sm_blocks[k]):
                    row = sm_first[k] + i * sm_blocks[k] + j
                    sm_pack[row:row + 1, :] = sm_in[k][i:i + 1, 128 * j:128 * (j + 1)]
        me = 4 * x + 2 * y + c
        sibling = (x, y, 1 - c)

        def rows(a, half):
            r2 = halves[a][0]
            return pl.ds(pl.multiple_of(half * r2, r2), r2)

        near = (jnp.where(c == 0, 1 - x, x), jnp.where(c == 0, y, 1 - y))
        far = (jnp.where(c == 0, x, 1 - x), jnp.where(c == 0, 1 - y, y))
        chip = lambda p: 2 * p[0] + p[1]
        blocks = [3 - mine, chip(near), chip(far), mine]

        def pre(a, k):
            q = blocks[k]
            return pltpu.make_async_remote_copy(
                src_ref=pin[a].at[q, rows(a, 1 - c), :], dst_ref=sib[a].at[q],
                send_sem=pre_send.at[a, q], recv_sem=pre_recv.at[a, q], device_id=sibling,
                device_id_type=pl.DeviceIdType.MESH)

        def ici(a, m):
            px, py = near if m < 2 else far
            return pltpu.make_async_remote_copy(
                src_ref=wire[a].at[blocks[m]], dst_ref=rbuf[a].at[m], send_sem=send_sems.at[a, m],
                recv_sem=recv_sems.at[a, m], device_id=(px, py, c), device_id_type=pl.DeviceIdType.MESH)

        def post(a, half):
            ref = gout[a].at[rows(a, half), :]
            return pltpu.make_async_remote_copy(
                src_ref=ref, dst_ref=ref, send_sem=post_send.at[a], recv_sem=post_recv.at[a],
                device_id=sibling, device_id_type=pl.DeviceIdType.MESH)

        def small_copy(kk):
            peer = (x ^ (kk >> 2), y ^ ((kk >> 1) & 1), c ^ (kk & 1))
            return pltpu.make_async_remote_copy(
                src_ref=sm_pack, dst_ref=sbuf.at[kk], send_sem=sm_send.at[kk - 1], recv_sem=sm_recv.at[kk - 1],
                device_id=peer, device_id_type=pl.DeviceIdType.MESH)

        local = [[pltpu.make_async_copy(pin[a].at[blocks[k], rows(a, c), :], own[a].at[blocks[k]], loc_sems.at[a, k])
                  for k in range(4)] for a in range(n)]
        pres = [[pre(a, k) for k in range(4)] for a in range(n)]
        smalls = [small_copy(kk) for kk in range(1, 8)]
        for a in range(n):
            for k in range(4):
                local[a][k].start()
                pres[a][k].start()
        for cp in smalls:
            cp.start()
        sbuf[0] = sm_pack[...]
        sends = []
        for a in range(n):
            for k in range(4):
                local[a][k].wait()
                pres[a][k].wait_recv()
                tot = own[a][blocks[k]] + sib[a][blocks[k]]
                if k == 2:
                    ici(a, 0).wait_recv()
                    tot = tot + rbuf[a][0].astype(F32)
                own[a][blocks[k]] = tot
                if k < 3:
                    wire[a][blocks[k]] = tot.astype(BF16)
                    cp = ici(a, k)
                    cp.start()
                    sends.append(cp)
        for cp in smalls:
            cp.wait_recv()
        total = sbuf[me]
        for d in range(1, 8):
            total = total + sbuf[me ^ d]
        sm_out[...] = total
        posts = []
        for a in range(n):
            fin = own[a][mine]
            for m in (1, 2):
                ici(a, m).wait_recv()
                fin = fin + rbuf[a][m].astype(F32)
            gout[a][rows(a, c), :] = fin
            cp = post(a, c)
            cp.start()
            posts.append(cp)
        for a in range(n):
            post(a, 1 - c).wait_recv()
        for cp in [cp for row in pres for cp in row] + sends + smalls + posts:
            cp.wait_send()

    vmem = pl.BlockSpec(memory_space=pltpu.VMEM)
    dma = pltpu.SemaphoreType.DMA
    return pl.pallas_call(
        body, name="reduce_grads",
        in_specs=[pl.BlockSpec(memory_space=pl.ANY)] * n + [vmem] * ns, out_specs=[vmem] * (n + 1),
        out_shape=[jax.ShapeDtypeStruct(sh, F32) for sh in shapes] + [jax.ShapeDtypeStruct(sm_shape, F32)],
        scratch_shapes=([pltpu.VMEM((4,) + hs, F32) for hs in halves] + [pltpu.VMEM((4,) + hs, F32) for hs in halves]
                        + [pltpu.VMEM((4,) + hs, BF16) for hs in halves]
                        + [pltpu.VMEM((3,) + hs, BF16) for hs in halves]
                        + [pltpu.VMEM((8,) + sm_shape, F32), dma((n, 3)), dma((n, 3)), dma((n, 4)),
                           dma((n, 4)), dma((n, 4)), dma((n,)), dma((n,)), dma((7,)), dma((7,)),
                           pltpu.VMEM(sm_shape, F32)]),
        compiler_params=pltpu.CompilerParams(vmem_limit_bytes=VMEM_LIMIT),
    )(*parts, *[a for a, _ in small])


def _adamw_update(w_ref, g_ref, m_ref, v_ref, d_ref, nm_ref, nv_ref):
    gv = g_ref[...]
    nm = ADAM_B1 * m_ref[...] + (1.0 - ADAM_B1) * gv
    nv = ADAM_B2 * v_ref[...] + (1.0 - ADAM_B2) * (gv * gv)
    m_hat = nm / (1.0 - ADAM_B1 ** ADAM_STEP)
    v_hat = nv / (1.0 - ADAM_B2 ** ADAM_STEP)
    d_ref[...] = -ADAM_LR * (m_hat / (jnp.sqrt(v_hat) + ADAM_EPS) + ADAM_WD * w_ref[...])
    nm_ref[...] = nm
    nv_ref[...] = nv


def _adamw_small(ws, gs, ms, vs):
    k = len(ws)

    def body(*refs):
        ins, outs = refs[:4 * k], refs[4 * k:]
        for a in range(k):
            _adamw_update(ins[a], ins[k + a], ins[2 * k + a], ins[3 * k + a], outs[a], outs[k + a], outs[2 * k + a])

    out = pl.pallas_call(
        body, name="adamw_small",
        out_shape=[jax.ShapeDtypeStruct(w.shape, F32) for w in ws] * 3,
        compiler_params=pltpu.CompilerParams(vmem_limit_bytes=VMEM_LIMIT),
    )(*ws, *gs, *ms, *vs)
    return out[:k], out[k:2 * k], out[2 * k:]


def _adamw(w, g, m, v, name):
    shape = w.shape
    w2, g2, m2, v2 = (a.reshape((-1, shape[-1])) for a in (w, g, m, v))

    def body(w_ref, g_ref, m_ref, v_ref, d_ref, nm_ref, nv_ref):
        _adamw_update(w_ref, g_ref, m_ref, v_ref, d_ref, nm_ref, nv_ref)

    rows, cols = w2.shape
    nblk = cols // 256 if cols % 256 == 0 and rows >= 64 else 1
    blk = pl.BlockSpec((rows, cols // nblk), lambda j: (0, j))
    out = pl.pallas_call(
        body, name=name, grid=(nblk,), in_specs=[blk] * 4, out_specs=[blk] * 3,
        out_shape=[jax.ShapeDtypeStruct(w2.shape, F32)] * 3,
        compiler_params=_cparams("parallel"),
    )(w2, g2, m2, v2)
    return tuple(a.reshape(shape) for a in out)


def kernel(x, meta_tokens, norm_g, w_in, q_norm_g, w_q_up, kv_norm_g, w_kv_up, conv_w, attn_out_g, conv_out_g, w_out, final_norm_g, loss_target, m_meta_tokens, m_norm_g, m_w_in, m_q_norm_g, m_w_q_up, m_kv_norm_g, m_w_kv_up, m_conv_w, m_attn_out_g, m_conv_out_g, m_w_out, m_final_norm_g, v_meta_tokens, v_norm_g, v_w_in, v_q_norm_g, v_w_q_up, v_kv_norm_g, v_w_kv_up, v_conv_w, v_attn_out_g, v_conv_out_g, v_w_out, v_final_norm_g):
    nb, s, _ = x.shape
    tm = min(ROW_TILE, s)
    ta = min(ATTN_TILE, s)
    assert s % tm == 0 and s % ta == 0 and tm % 16 == 0
    r = nb * s

    tr = lambda a: jnp.transpose(a[0])
    w_head, w_in_shard, w_out_shard, wq_p, wkv_p, g_cw, meta_f = _gather_weights(
        tr(w_in), w_out[0], [tr(w_q_up), tr(w_kv_up)],
        [W_Q_PIECES, W_KV_PIECES], [HEADS * QK_PAD, 1024],
        [jnp.transpose(conv_w, (1, 0, 2)), meta_tokens],
        [(0, jnp.zeros((64, Q_RANK), BF16), QK_PAD * h + NOPE + ROPE) for h in range(HEADS)])
    conv_f = g_cw.reshape(3, CONV_W)

    c_all, sa_all, sb_all = _rope_tables(N_META + s)
    tabs_m = (c_all[:N_META], sa_all[:N_META], sb_all[:N_META])
    tabs = (c_all[N_META:], sa_all[N_META:], sb_all[N_META:])
    gid = np.arange(CONV_W) // CONV_GROUP
    gmat = jnp.asarray(np.where(gid[:, None] == gid[None, :], 1.0 / CONV_GROUP, 0.0), BF16)
    ga, gc = attn_out_g, conv_out_g
    gf = final_norm_g.reshape(1, D_MODEL)

    x2d = x.reshape(r, D_MODEL)
    tgt2d = loss_target.reshape(r, D_MODEL)

    ph, q, k, v, pmh, km, vm, w_out_f, w_in_part = _fwd_proj(
        x2d, meta_f, tabs, tabs_m, norm_g, w_head, q_norm_g, wq_p, kv_norm_g, wkv_p, w_out_shard, w_in_shard,
        nb, s, tm)
    o, lse, w_in_p = _attn_fwd(q, k, v, km, vm, w_in_shard, w_in_part, nb, s, ta)
    dh2, dycat, dw_out, dgf, loss_acc, pt, pmt = _out_fwd_bwd(x2d, tgt2d, o, meta_f, norm_g, w_in_p, conv_f, ga, gc,
                                                              gmat, w_out_f, gf, nb, s, tm)
    dpb, do, delta, dccm, dga, dgc, dcw = _gate_bwd(dycat, o, pt, pmt, conv_f, ga, gc, gmat, nb, s, tm)
    p_out = dw_out.reshape(4, D_MODEL // 4, D_MODEL)
    dq, dk, dv, dkm, dvm, g_w_out = _attn_bwd(q, k, v, do, lse, delta, km, vm, [p_out], nb, s, ta)
    dpa, dpam, p_q, p_kv, dgq, dgkv = _up_bwd(dq, dk, dv, dkm, dvm, ph, pmh, tabs, tabs_m, wq_p, wkv_p,
                                              q_norm_g, kv_norm_g, nb, s, tm)
    gx, gmeta, p_in, dng = _in_bwd(x2d, dh2, dpa, dpb, meta_f, dpam, dccm, pmt, w_in_p, norm_g, nb, s, tm)

    g_w_in_t, g_w_q_t, g_w_kv_t, small_sum = _reduce_grads(
        [p_in, p_q, p_kv],
        [(dng, 1), (dgq, 1), (dgkv, 1), (dga, 1), (dgc, 1), (dgf, 1), (dcw, 3), (gmeta, N_META), (loss_acc, 1)])
    ssum = small_sum.reshape(-1)

    def take(off, n):
        return ssum[off:off + n], off + n

    off = 0
    g_norm, off = take(off, D_MODEL)
    g_qn, off = take(off, Q_RANK)
    g_kvn, off = take(off, KV_RANK)
    g_ga, off = take(off, CONV_W)
    g_gc, off = take(off, CONV_W)
    g_gf, off = take(off, D_MODEL)
    g_cw_all, off = take(off, 3 * CONV_W)
    g_meta_all, off = take(off, N_META * D_MODEL)
    loss = ssum[off]
    chip = 2 * lax.axis_index("x") + lax.axis_index("y")
    g_conv = lax.dynamic_slice(g_cw_all.reshape(3, CONV_W), (0, chip * 128), (3, 128))
    g_mt = lax.dynamic_slice(g_meta_all.reshape(N_META, D_MODEL), (0, chip * 256), (N_META, 256))

    grads = {
        "meta_tokens": g_mt, "norm_g": g_norm.reshape(1, -1), "w_in": g_w_in_t, "q_norm_g": g_qn.reshape(1, -1),
        "w_q_up": g_w_q_t, "kv_norm_g": g_kvn.reshape(1, -1), "w_kv_up": jnp.transpose(g_w_kv_t)[None],
        "conv_w": g_conv[None], "attn_out_g": g_ga.reshape(1, -1), "conv_out_g": g_gc.reshape(1, -1),
        "w_out": g_w_out[None], "final_norm_g": g_gf,
    }
    transposed = ("w_in", "w_q_up")
    weights = {
        "meta_tokens": (meta_tokens, m_meta_tokens, v_meta_tokens), "norm_g": (norm_g, m_norm_g, v_norm_g),
        "w_in": (w_in, m_w_in, v_w_in), "q_norm_g": (q_norm_g, m_q_norm_g, v_q_norm_g),
        "w_q_up": (w_q_up, m_w_q_up, v_w_q_up), "kv_norm_g": (kv_norm_g, m_kv_norm_g, v_kv_norm_g),
        "w_kv_up": (w_kv_up, m_w_kv_up, v_w_kv_up), "conv_w": (conv_w, m_conv_w, v_conv_w),
        "attn_out_g": (attn_out_g, m_attn_out_g, v_attn_out_g), "conv_out_g": (conv_out_g, m_conv_out_g, v_conv_out_g),
        "w_out": (w_out, m_w_out, v_w_out), "final_norm_g": (final_norm_g, m_final_norm_g, v_final_norm_g),
    }
    names = list(weights)
    small = [nme for nme in names if nme != "w_in"]

    def view(nme, a):
        if nme in transposed:
            return a if a.ndim == 2 else tr(a)
        if nme == "conv_w":
            return jnp.transpose(a.reshape(1, 3, -1), (1, 0, 2))
        if a.ndim == 3:
            return a[0]
        return a.reshape(1, -1) if a.ndim == 1 else a

    def unview(nme, a):
        if nme in transposed:
            return jnp.transpose(a)[None]
        if nme == "conv_w":
            return jnp.transpose(a, (1, 0, 2))
        return a.reshape(weights[nme][0].shape)

    res_small = _adamw_small(*[[view(nme, a) for nme, a in zip(small, col)] for col in (
        [weights[nme][0] for nme in small], [grads[nme] for nme in small],
        [weights[nme][1] for nme in small], [weights[nme][2] for nme in small])])
    w_, m_, v_ = weights["w_in"]
    res = _adamw(tr(w_), grads["w_in"], tr(m_), tr(v_), "adamw_w_in")
    upd = {"w_in": tuple(jnp.transpose(a)[None] for a in (grads["w_in"],) + res)}
    for j, nme in enumerate(small):
        upd[nme] = (unview(nme, view(nme, grads[nme])),) + tuple(unview(nme, r[j]) for r in res_small)
    grads = {nme: upd[nme][0] for nme in names}
    deltas, new_m, new_v = ([upd[nme][j] for nme in names] for j in (1, 2, 3))

    grad_x = gx.reshape(nb, s, D_MODEL)
    return (loss, grad_x, *[grads[nme] for nme in names], *deltas, *new_m, *new_v)
```

```python
import functools

import jax
import jax.numpy as jnp
import numpy as np
from jax import lax
from jax.experimental import pallas as pl
from jax.experimental.pallas import tpu as pltpu

F32 = jnp.float32
BF16 = jnp.bfloat16

D_MODEL = 1024
N_META = 16
HEADS = 4
NOPE = 128
ROPE = 64
VDIM = 128
QK_PAD = 256
Q_RANK = 256
KV_RANK = 128
CONV_W = 512
CONV_GROUP = 64
ROPE_THETA = 10000.0
EPS = 1e-6
ATTN_SCALE = (NOPE + ROPE) ** -0.5
IN_DIM = 3008
IN_PAD = 3072
HEAD_ROWS = Q_RANK + KV_RANK + ROPE
IN_HEAD = 512
IN_TAIL = IN_PAD - IN_HEAD
BLK_ZA, BLK_CB, BLK_CC, BLK_CH, BLK_ZC = 0, 1, 2, 3, 4
NEG_INF = -1e30

ADAM_LR = 0.001
ADAM_B1 = 0.9
ADAM_B2 = 0.999
ADAM_EPS = 1e-08
ADAM_WD = 0.01
ADAM_STEP = 10

ROW_TILE = 512
ATTN_TILE = 256
VMEM_LIMIT = 56 * 1024 * 1024

NT = (((1,), (1,)), ((), ()))
TN = (((0,), (0,)), ((), ()))


def _cparams(*sem):
    return pltpu.CompilerParams(dimension_semantics=sem, vmem_limit_bytes=VMEM_LIMIT)


def _dot(a, b):
    return jnp.dot(a, b, preferred_element_type=F32)


def _dot_nt(a, b):
    return lax.dot_general(a, b, NT, preferred_element_type=F32)


def _dot_tn(a, b):
    return lax.dot_general(a, b, TN, preferred_element_type=F32)


def _rms(x, g):
    r = lax.rsqrt(jnp.mean(x * x, axis=-1, keepdims=True) + EPS)
    return x * r * g, r


def _rms_bwd(dy, x, r, g):
    xh = x * r
    dyg = dy * g
    dx = r * (dyg - xh * jnp.mean(dyg * xh, axis=-1, keepdims=True))
    return dx, dy * xh


def _sigmoid(z):
    return 1.0 / (1.0 + jnp.exp(-z))


def _rope(b, c, sa, sb):
    return b * c + pltpu.roll(b, 96, 1) * sa + pltpu.roll(b, 32, 1) * sb


def _rope_bwd(d, c, sa, sb):
    return d * c + pltpu.roll(d * sa, 32, 1) + pltpu.roll(d * sb, 96, 1)


def _group_mean(x, gmat):
    hi = x.astype(BF16)
    lo = (x - hi.astype(F32)).astype(BF16)
    return _dot(hi, gmat) + _dot(lo, gmat)


def _row_of(col, rows):
    return jnp.transpose(jnp.broadcast_to(col, (rows, 128)))[0:1, :]


def _rope_tables(n_pos):
    half = ROPE // 2
    inv_freq = (np.float32(1.0) / (np.float32(ROPE_THETA) ** (np.arange(half, dtype=np.float32) / np.float32(half))))
    ang = np.arange(n_pos, dtype=np.float32)[:, None] * inv_freq.astype(np.float32)[None, :]
    cos, sin = np.cos(ang).astype(np.float32), np.sin(ang).astype(np.float32)
    z = np.zeros((n_pos, half), np.float32)
    c = np.concatenate([cos, cos, z, z], axis=1)
    sa = np.concatenate([-sin, z, z, z], axis=1)
    sb = np.concatenate([z, sin, z, z], axis=1)
    return jnp.asarray(c), jnp.asarray(sa), jnp.asarray(sb)


W_IN_PIECES_1 = ((0, 80, 752, 0, 64, 0), (384, 64, 752, 384, 448, 1), (448, 16, 752, 512, 512, 1))
W_IN_PIECES_2 = ((80, 304, 752, 80, 144, 0), (464, 288, 752, 528, 528, 1))
W_Q_PIECES = ((0, 96, 256, 0, 0, 0), (96, 96, 256, 96, 96, 1))
W_KV_PIECES = ((0, 128, 128, 0, 0, 0), (128, 128, 128, 512, 512, 1))
W_OUT_PIECES = ((0, 128, 256, 0, 0, 0), (128, 128, 256, 128, 128, 1))


class _StagedGather:
    STAGES = 4

    @staticmethod
    def steps(n_steps):
        return (0, 5 * n_steps // 8, 7 * n_steps // 8, n_steps - 1)

    def __init__(self, pieces, zero_rows=None):
        self.pieces = pieces
        self.zero_rows = zero_rows

    def scratch(self):
        nk, dma = len(self.pieces), pltpu.SemaphoreType.DMA
        return [dma((nk, 3)), dma((nk, 3)), dma((nk, 3)), dma((nk, 3)), dma((nk,))]

    def vmem_scratch(self, shard_shape, out_shape):
        return [pltpu.VMEM(shard_shape, BF16), pltpu.VMEM(out_shape, BF16),
                pltpu.SemaphoreType.DMA((4 * len(self.pieces) + 1,)),
                pltpu.SemaphoreType.DMA((len(self.pieces),))] + self.scratch()

    def run_vmem(self, stage, shard_ref, out_ref, scr):
        src_scr, land_scr, io_sems, ld_sems = scr[:4]
        spans = []
        for _, nr, per, first, rest, _ in self.pieces:
            spans += [(per * q + (first if q == 0 else rest), nr) for q in range(4)]
        if self.zero_rows is not None:
            spans.append(self.zero_rows)
        flush = [pltpu.make_async_copy(land_scr.at[r0:r0 + nr], out_ref.at[r0:r0 + nr], io_sems.at[n])
                 for n, (r0, nr) in enumerate(spans)]
        if stage == 0:
            loads = [pltpu.make_async_copy(shard_ref.at[s0:s0 + nr], src_scr.at[s0:s0 + nr], ld_sems.at[k])
                     for k, (s0, nr, _, _, _, _) in enumerate(self.pieces)]
            for cp in loads:
                cp.start()
            if self.zero_rows is not None:
                r0, nr = self.zero_rows
                land_scr[r0:r0 + nr, :] = jnp.zeros((nr, land_scr.shape[1]), BF16)
            for cp in loads:
                cp.wait()
        if stage < self.STAGES:
            self.run(stage, src_scr, land_scr, scr[4:])
        for cp in flush:
            if stage == self.STAGES - 1:
                cp.start()
            if stage == self.STAGES:
                cp.wait()

    def run(self, stage, src_ref, out_ref, scr):
        send_sems, recv_sems, fwd_send, fwd_recv, loc_sems = scr
        pieces = self.pieces
        nk = len(pieces)
        x, y, c = lax.axis_index("x"), lax.axis_index("y"), lax.axis_index("c")
        mine = 2 * x + y
        chips = [(1 - x, y), (x, 1 - y), (1 - x, 1 - y)]
        chip_of = [2 * px + py for px, py in chips]
        mesh = pl.DeviceIdType.MESH

        def src(k):
            s0, nr = pieces[k][0], pieces[k][1]
            return src_ref.at[s0:s0 + nr]

        def dst(k, q):
            _, nr, per, first, rest, _ = pieces[k]
            row = per * q + first + (rest - first) * jnp.minimum(q, 1)
            return out_ref.at[pl.ds(pl.multiple_of(row, 16), nr)]

        def ici(k, j, q):
            px, py = chips[j]
            return pltpu.make_async_remote_copy(
                src_ref=src(k), dst_ref=dst(k, q), send_sem=send_sems.at[k, j], recv_sem=recv_sems.at[k, j],
                device_id=(px, py, c), device_id_type=mesh)

        def fwd(k, j):
            ref = dst(k, chip_of[j])
            return pltpu.make_async_remote_copy(
                src_ref=ref, dst_ref=ref, send_sem=fwd_send.at[k, j], recv_sem=fwd_recv.at[k, j],
                device_id=(x, y, 1 - c), device_id_type=mesh)

        def relay(k, half):
            ref = dst(k, chip_of[half])
            px, py = chips[1 - half]
            return pltpu.make_async_remote_copy(
                src_ref=ref, dst_ref=ref, send_sem=send_sems.at[k, 2], recv_sem=recv_sems.at[k, 2],
                device_id=(px, py, c), device_id_type=mesh)

        local = [pltpu.make_async_copy(src(k), dst(k, mine), loc_sems.at[k]) for k in range(nk)]
        if stage == 0:
            for cp in local:
                cp.start()
        if stage == 3:
            for cp in local:
                cp.wait()
        for half in (0, 1):
            @pl.when(c == half)
            def _(half=half):
                my_k = [k for k in range(nk) if pieces[k][5] == half]
                other_k = [k for k in range(nk) if pieces[k][5] != half]
                for k in my_k:
                    if stage == 0:
                        for j in range(2):
                            ici(k, j, mine).start()
                    elif stage == 1:
                        for j in (half, 1 - half):
                            ici(k, j, chip_of[j]).wait_recv()
                            if j == half:
                                relay(k, half).start()
                            fwd(k, j).start()
                    elif stage == 2:
                        ici(k, 2, chip_of[2]).wait_recv()
                        fwd(k, 2).start()
                    else:
                        for j in range(2):
                            ici(k, j, mine).wait_send()
                        relay(k, half).wait_send()
                        for j in range(3):
                            fwd(k, j).wait_send()
                if stage == 3:
                    for k in other_k:
                        for j in range(3):
                            fwd(k, j).wait_recv()


def _fwd_proj(x2d, meta, tabs, tabs_m, norm_g, w_head, q_norm_g, wq_p, kv_norm_g, wkv_p, w_out_shard, w_in_shard,
              nb, s, tm):
    nt = s // tm
    n = nb * nt
    n_steps = n + 1
    c_t, sa_t, sb_t = tabs
    cm_t, sam_t, sbm_t = tabs_m
    gat = _StagedGather(W_OUT_PIECES)
    gat_in = _StagedGather(W_IN_PIECES_1)
    n_sems = len(gat.scratch())
    assert n_steps >= 3

    def body(x_ref, c_ref, sa_ref, sb_ref, mt_ref, cm_ref, sam_ref, sbm_ref,
             g_ref, w_ref, gq_ref, wq_ref, gkv_ref, wkv_ref, wos_ref, wis_ref,
             p_ref, q_ref, k_ref, v_ref, pm_ref, km_ref, vm_ref, wo_ref, wi_ref, *scr):
        gat_scr, gat_in_scr = scr[:n_sems], scr[n_sems:]
        i = pl.program_id(0)
        for stage, at in enumerate(_StagedGather.steps(n_steps)):
            @pl.when(i == at)
            def _(stage=stage):
                gat_in.run_vmem(stage, wis_ref, wi_ref, gat_in_scr)
                gat.run(stage, wos_ref, wo_ref, gat_scr)

        def project(xv, c, sa, sb, p_out, q_out, k_out, v_out):
            u, _ = _rms(xv, g_ref[...])
            p = _dot_nt(u.astype(BF16), w_ref[...])
            p_out[...] = p
            qn, _ = _rms(p[:, 0:Q_RANK], gq_ref[...])
            q = _dot_nt(qn.astype(BF16), wq_ref[...])
            kvn, _ = _rms(p[:, Q_RANK:Q_RANK + KV_RANK], gkv_ref[...])
            kv = _dot_nt(kvn.astype(BF16), wkv_ref[...])
            kpe = _rope(p[:, 384:512], c, sa, sb)
            for h in range(HEADS):
                if q_out is not None:
                    pe = _rope(q[:, QK_PAD * h + NOPE:QK_PAD * (h + 1)], c, sa, sb)
                    qh = jnp.concatenate([q[:, QK_PAD * h:QK_PAD * h + NOPE], pe], axis=1)
                    q_out[0, h] = (qh * ATTN_SCALE).astype(BF16)
                k_out[0, h] = jnp.concatenate([kv[:, NOPE * h:NOPE * (h + 1)], kpe], axis=1).astype(BF16)
                v_out[0, h] = kv[:, 512 + VDIM * h:512 + VDIM * (h + 1)].astype(BF16)

        @pl.when(i < n)
        def _():
            project(x_ref[...], c_ref[...], sa_ref[...], sb_ref[...], p_ref, q_ref, k_ref, v_ref)

        @pl.when(i == n)
        def _():
            project(mt_ref[...], cm_ref[...], sam_ref[...], sbm_ref[...], pm_ref, None, km_ref, vm_ref)
            gat_in.run_vmem(gat_in.STAGES, wis_ref, wi_ref, gat_in_scr)

    cl = lambda i: jnp.minimum(i, n - 1)
    full = lambda a: pl.BlockSpec(a.shape, lambda i: (0,) * a.ndim)
    const = lambda shape: pl.BlockSpec(shape, lambda i: (0,) * len(shape))
    tab = pl.BlockSpec((tm, 128), lambda i: (cl(i) % nt, 0))
    hb = lambda w: pl.BlockSpec((1, HEADS, tm, w), lambda i: (cl(i) // nt, 0, cl(i) % nt, 0))
    whole = pl.BlockSpec(memory_space=pl.ANY)
    return pl.pallas_call(
        body, name="fwd_proj", grid=(n_steps,),
        in_specs=[pl.BlockSpec((tm, D_MODEL), lambda i: (cl(i), 0)), tab, tab, tab,
                  full(meta), full(cm_t), full(sam_t), full(sbm_t),
                  full(norm_g), full(w_head), full(q_norm_g), full(wq_p), full(kv_norm_g), full(wkv_p), whole, whole],
        out_specs=[pl.BlockSpec((tm, IN_HEAD), lambda i: (cl(i), 0)), hb(QK_PAD), hb(QK_PAD), hb(VDIM),
                   const((N_META, IN_HEAD)), const((1, HEADS, N_META, QK_PAD)), const((1, HEADS, N_META, VDIM)),
                   whole, whole],
        out_shape=[jax.ShapeDtypeStruct((nb * s, IN_HEAD), F32),
                   jax.ShapeDtypeStruct((nb, HEADS, s, QK_PAD), BF16),
                   jax.ShapeDtypeStruct((nb, HEADS, s, QK_PAD), BF16),
                   jax.ShapeDtypeStruct((nb, HEADS, s, VDIM), BF16),
                   jax.ShapeDtypeStruct((N_META, IN_HEAD), F32),
                   jax.ShapeDtypeStruct((1, HEADS, N_META, QK_PAD), BF16),
                   jax.ShapeDtypeStruct((1, HEADS, N_META, VDIM), BF16),
                   jax.ShapeDtypeStruct((D_MODEL, D_MODEL), BF16),
                   jax.ShapeDtypeStruct((IN_PAD, D_MODEL), BF16)],
        scratch_shapes=gat.scratch() + gat_in.vmem_scratch(w_in_shard.shape, (IN_PAD, D_MODEL)),
        compiler_params=_cparams("arbitrary"),
    )(x2d, c_t, sa_t, sb_t, meta, cm_t, sam_t, sbm_t, norm_g, w_head, q_norm_g, wq_p, kv_norm_g, wkv_p, w_out_shard,
      w_in_shard)


def _attn_fwd(q, k, v, km, vm, w_in_shard, w_in_part, nb, s, tq):
    nq = s // tq
    n_steps = nb * HEADS
    gat = _StagedGather(W_IN_PIECES_2, zero_rows=(HEAD_ROWS, IN_HEAD - HEAD_ROWS))
    assert n_steps >= 3

    def body(q_ref, k_ref, v_ref, km_ref, vm_ref, ws_ref, _, o_ref, lse_ref, w_ref, s_scr, p_scr, *gat_scr):
        step = pl.program_id(0) * HEADS + pl.program_id(1)
        for stage, at in enumerate(_StagedGather.steps(n_steps)):
            @pl.when(step == at)
            def _(stage=stage):
                gat.run_vmem(stage, ws_ref, w_ref, gat_scr)

        row = lax.broadcasted_iota(jnp.int32, (tq, tq), 0)
        col = lax.broadcasted_iota(jnp.int32, (tq, tq), 1)
        def scores(i):
            slot = i % 2
            qi = q_ref[0, 0, i * tq:(i + 1) * tq, :]
            sm = _dot_nt(qi, km_ref[0, 0])
            m128 = None
            for j in range(i + 1):
                sc = _dot_nt(qi, k_ref[0, 0, j * tq:(j + 1) * tq, :])
                if j == i:
                    sc = jnp.where(col <= row, sc, NEG_INF)
                s_scr[slot, :, j * tq:(j + 1) * tq] = sc
                mx = sc[:, 0:128]
                for c0 in range(128, tq, 128):
                    mx = jnp.maximum(mx, sc[:, c0:c0 + 128])
                m128 = mx if m128 is None else jnp.maximum(m128, mx)
            return sm, jnp.maximum(jnp.max(m128, axis=1, keepdims=True), jnp.max(sm, axis=1, keepdims=True))

        def weighted_sum(i, pm, l):
            n = (i + 1) * tq
            acc = _dot(p_scr[i % 2, :, 0:n], v_ref[0, 0, 0:n, :]) + _dot(pm.astype(BF16), vm_ref[0, 0])
            o_ref[0, 0, i * tq:(i + 1) * tq, :] = acc / l

        nxt, pending = scores(0), None
        for i in range(nq):
            slot = i % 2
            sm, m = nxt
            if i + 1 < nq:
                nxt = scores(i + 1)
            pm = jnp.exp(sm - m)
            l128 = None
            for j in range(i + 1):
                p = jnp.exp(s_scr[slot, :, j * tq:(j + 1) * tq] - m)
                p_scr[slot, :, j * tq:(j + 1) * tq] = p.astype(BF16)
                ps = p[:, 0:128]
                for c0 in range(128, tq, 128):
                    ps = ps + p[:, c0:c0 + 128]
                l128 = ps if l128 is None else l128 + ps
            l = jnp.sum(l128, axis=1, keepdims=True) + jnp.sum(pm, axis=1, keepdims=True)
            lse_ref[0, 0, :, i * tq:(i + 1) * tq] = _row_of(m + jnp.log(l), tq)
            if pending is not None:
                weighted_sum(*pending)
            pending = (i, pm, l)
        weighted_sum(*pending)

        @pl.when(step == n_steps - 1)
        def _():
            gat.run_vmem(gat.STAGES, ws_ref, w_ref, gat_scr)

    hblk = lambda w: pl.BlockSpec((1, 1, s, w), lambda b, h: (b, h, 0, 0))
    mblk = lambda w: pl.BlockSpec((1, 1, N_META, w), lambda b, h: (0, h, 0, 0))
    whole = pl.BlockSpec(memory_space=pl.ANY)
    return pl.pallas_call(
        body, name="attn_fwd", grid=(nb, HEADS),
        in_specs=[hblk(QK_PAD), hblk(QK_PAD), hblk(VDIM), mblk(QK_PAD), mblk(VDIM), whole, whole],
        out_specs=[hblk(VDIM), pl.BlockSpec((1, 1, 1, s), lambda b, h: (b, h, 0, 0)), whole],
        out_shape=[jax.ShapeDtypeStruct((nb, HEADS, s, VDIM), F32),
                   jax.ShapeDtypeStruct((nb, HEADS, 1, s), F32),
                   jax.ShapeDtypeStruct(w_in_part.shape, BF16)],
        input_output_aliases={6: 2},
        scratch_shapes=[pltpu.VMEM((2, tq, s), F32), pltpu.VMEM((2, tq, s), BF16)]
        + gat.vmem_scratch(w_in_shard.shape, w_in_part.shape),
        compiler_params=_cparams("arbitrary", "arbitrary"),
    )(q, k, v, km, vm, w_in_shard, w_in_part)


def _shift_rows(a, prev, n_rows):
    rid = lax.broadcasted_iota(jnp.int32, a.shape, 0)
    a1 = jnp.where(rid == 0, prev[7:8, :], pltpu.roll(a, 1, 0))
    a2 = jnp.where(rid == 0, prev[6:7, :], jnp.where(rid == 1, prev[7:8, :], pltpu.roll(a, 2, 0)))
    return a1, a2


def _attn_gate(o, za, ga_h):
    on, r = _rms(o, ga_h)
    return on * (za * _sigmoid(za)), on, r


def _out_fwd_bwd(x2d, tgt2d, o, meta, norm_g, w_in_p, conv_w, ga, gc, gmat, w_out, gf, nb, s, tm):
    nt = s // tm
    r = nb * s

    def body(x_ref, t_ref, o_ref, mt_ref, g_ref, wi_ref, cw_ref, ga_ref, gc_ref, gm_ref, w_ref, gf_ref,
             dh_ref, dy_ref, dw_ref, dgf_ref, loss_ref, p_ref, pm_ref, last_cc):
        i = pl.program_id(0)
        blk = lambda ref, j, rows=slice(None): ref[rows, 512 * j:512 * (j + 1)]

        def tail(xv):
            u, _ = _rms(xv, g_ref[...])
            return _dot_nt(u.astype(BF16), wi_ref[IN_HEAD:IN_PAD, :])

        @pl.when(i == 0)
        def _():
            dw_ref[...] = jnp.zeros_like(dw_ref)
            dgf_ref[...] = jnp.zeros_like(dgf_ref)
            loss_ref[...] = jnp.zeros_like(loss_ref)
            last_cc[...] = jnp.zeros_like(last_cc)
            pm_ref[...] = tail(mt_ref[...])

        u16 = _rms(x_ref[...], g_ref[...])[0].astype(BF16)

        def project(j):
            p_ref[:, 512 * j:512 * (j + 1)] = _dot_nt(u16, wi_ref[IN_HEAD + 512 * j:IN_HEAD + 512 * (j + 1), :])

        project(BLK_ZA)
        project(BLK_CC)
        project(BLK_CH)
        ya = []
        for h in range(HEADS):
            y, _, _ = _attn_gate(o_ref[0, h], p_ref[:, 512 * BLK_ZA + VDIM * h:512 * BLK_ZA + VDIM * (h + 1)],
                                 ga_ref[:, VDIM * h:VDIM * (h + 1)])
            ya.append(y)
        project(BLK_CB)
        project(BLK_ZC)
        cc = blk(p_ref, BLK_CC) * blk(p_ref, BLK_CH)
        meta_cc = blk(pm_ref, BLK_CC, slice(8, 16)) * blk(pm_ref, BLK_CH, slice(8, 16))
        prev = jnp.where(i % nt == 0, meta_cc, last_cc[...])
        last_cc[...] = cc[tm - 8:tm, :]
        cc1, cc2 = _shift_rows(cc, prev, tm)
        yc = blk(p_ref, BLK_CB) * (cw_ref[0:1, :] * cc2 + cw_ref[1:2, :] * cc1 + cw_ref[2:3, :] * cc)
        rg = lax.rsqrt(_group_mean(yc * yc, gm_ref[...]) + EPS)
        zc = blk(p_ref, BLK_ZC)
        yconv = yc * rg * gc_ref[...] * (zc * _sigmoid(zc))
        ycat = jnp.concatenate(ya + [yconv], axis=1).astype(BF16)
        h2 = x_ref[...] + _dot(ycat, w_ref[...])
        gfv = gf_ref[...]
        y, r2 = _rms(h2, gfv)
        e = y - t_ref[...]
        loss_ref[...] += 0.5 * jnp.sum(e * e) / D_MODEL
        dyv = e * (1.0 / D_MODEL)
        dh2, dgf = _rms_bwd(dyv, h2, r2, gfv)
        dgf_ref[...] += jnp.sum(dgf, axis=0, keepdims=True)
        dh_ref[...] = dh2
        dhb = dh2.astype(BF16)
        dy_ref[...] = _dot_nt(dhb, w_ref[...])
        dw_ref[...] += _dot_tn(ycat, dhb)

    row = lambda w: pl.BlockSpec((tm, w), lambda i: (i, 0))
    const = lambda shape: pl.BlockSpec(shape, lambda i: (0,) * len(shape))
    full = lambda a: const(a.shape)
    return pl.pallas_call(
        body, name="out_fwd_bwd", grid=(nb * nt,),
        in_specs=[row(D_MODEL), row(D_MODEL),
                  pl.BlockSpec((1, HEADS, tm, VDIM), lambda i: (i // nt, 0, i % nt, 0)),
                  full(meta), full(norm_g), full(w_in_p),
                  full(conv_w), full(ga), full(gc), full(gmat), full(w_out), full(gf)],
        out_specs=[row(D_MODEL), row(D_MODEL), const((D_MODEL, D_MODEL)), const((1, D_MODEL)), const((1, 128)),
                   row(IN_TAIL), const((N_META, IN_TAIL))],
        out_shape=[jax.ShapeDtypeStruct((r, D_MODEL), F32), jax.ShapeDtypeStruct((r, D_MODEL), F32),
                   jax.ShapeDtypeStruct((D_MODEL, D_MODEL), F32), jax.ShapeDtypeStruct((1, D_MODEL), F32),
                   jax.ShapeDtypeStruct((1, 128), F32),
                   jax.ShapeDtypeStruct((r, IN_TAIL), F32), jax.ShapeDtypeStruct((N_META, IN_TAIL), F32)],
        scratch_shapes=[pltpu.VMEM((8, 512), F32)],
        compiler_params=_cparams("arbitrary"),
    )(x2d, tgt2d, o, meta, norm_g, w_in_p, conv_w, ga, gc, gmat, w_out, gf)


def _gate_bwd(dycat, o, p, pm, conv_w, ga, gc, gmat, nb, s, tm):
    nt = s // tm
    r = nb * s
    ext = tm + 8
    prev_idx = lambda i: jnp.maximum(i * (tm // 8) - 1, 0)
    next_idx = lambda i: jnp.minimum((i + 1) * (tm // 8), r // 8 - 1)

    def body(dya_ref, dyc_ref, dycn_ref, o_ref, za_ref, cb_ref, cbn_ref, cc_ref, ccp_ref, ccn_ref,
             ch_ref, chp_ref, chn_ref, zc_ref, zcn_ref, mc_ref, mh_ref, cw_ref, ga_ref, gc_ref, gm_ref,
             dpb_ref, do_ref, dl_ref, dccm_ref, dga_ref, dgc_ref, dcw_ref):
        i = pl.program_id(0)

        @pl.when(i == 0)
        def _():
            dga_ref[...] = jnp.zeros_like(dga_ref)
            dgc_ref[...] = jnp.zeros_like(dgc_ref)
            dcw_ref[...] = jnp.zeros_like(dcw_ref)

        dga = []
        for h in range(HEADS):
            hs = slice(VDIM * h, VDIM * (h + 1))
            oh, za, gah, dya = o_ref[0, h], za_ref[:, hs], ga_ref[:, hs], dya_ref[:, hs]
            sg = _sigmoid(za)
            on, ro = _rms(oh, gah)
            don = dya * (za * sg)
            dpb_ref[:, hs] = (dya * on * (sg * (1.0 + za * (1.0 - sg)))).astype(BF16)
            do, dg = _rms_bwd(don, oh, ro, gah)
            dga.append(jnp.sum(dg, axis=0, keepdims=True))
            dob = do.astype(BF16)
            do_ref[0, h] = dob
            dl_ref[0, h] = _row_of(jnp.sum(dob.astype(F32) * oh, axis=1, keepdims=True), tm)
        dga_ref[...] += jnp.concatenate(dga, axis=1)

        cat = lambda a, b: jnp.concatenate([a[...], b[...]], axis=0)
        cch = cat(cc_ref, ccn_ref)
        chh = cat(ch_ref, chn_ref)
        cb = cat(cb_ref, cbn_ref)
        zc = cat(zc_ref, zcn_ref)
        dy = cat(dyc_ref, dycn_ref)
        first = i % nt == 0
        last = i % nt == nt - 1
        cc = cch * chh
        prev = jnp.where(first, mc_ref[8:16, :] * mh_ref[8:16, :], ccp_ref[...] * chp_ref[...])
        cc1, cc2 = _shift_rows(cc, prev, ext)
        w0, w1, w2 = cw_ref[0:1, :], cw_ref[1:2, :], cw_ref[2:3, :]
        dw = w0 * cc2 + w1 * cc1 + w2 * cc
        yc = cb * dw
        rg = lax.rsqrt(_group_mean(yc * yc, gm_ref[...]) + EPS)
        ych = yc * rg
        gcv = gc_ref[...]
        sg = _sigmoid(zc)
        dycn = dy * (zc * sg)
        dzc = dy * (ych * gcv) * (sg * (1.0 + zc * (1.0 - sg)))
        dgc_ref[...] += jnp.sum((dycn * ych)[:tm], axis=0, keepdims=True)
        dycg = dycn * gcv
        dyc = rg * (dycg - ych * _group_mean(dycg * ych, gm_ref[...]))
        rid = lax.broadcasted_iota(jnp.int32, (ext, CONV_W), 0)
        ddw = jnp.where(jnp.logical_and(last, rid >= tm), 0.0, dyc * cb)
        dcb = dyc * dw
        dcc = w2 * ddw + w1 * pltpu.roll(ddw, ext - 1, 0) + w0 * pltpu.roll(ddw, ext - 2, 0)
        dpb_ref[:, 512:1024] = dcb[:tm].astype(BF16)
        dpb_ref[:, 1024:1536] = (dcc * chh)[:tm].astype(BF16)
        dpb_ref[:, 1536:2048] = (dcc * cch)[:tm].astype(BF16)
        dpb_ref[:, 2048:2560] = dzc[:tm].astype(BF16)
        rs = lambda a: jnp.sum(a[:tm], axis=0, keepdims=True)
        dcw_ref[0:1, :] += rs(ddw * cc2)
        dcw_ref[1:2, :] += rs(ddw * cc1)
        dcw_ref[2:3, :] += rs(ddw * cc)

        @pl.when(first)
        def _():
            d0, d1 = ddw[0:1, :], ddw[1:2, :]
            r8 = lax.broadcasted_iota(jnp.int32, (8, CONV_W), 0)
            dccm_ref[0] = jnp.where(r8 == 7, w1 * d0 + w0 * d1, jnp.where(r8 == 6, w0 * d0, 0.0))

    row = lambda j: pl.BlockSpec((tm, 512), lambda i: (i, j))
    prv = lambda j: pl.BlockSpec((8, 512), lambda i: (prev_idx(i), j))
    nxt = lambda j: pl.BlockSpec((8, 512), lambda i: (next_idx(i), j))
    mblk = lambda j: pl.BlockSpec((N_META, 512), lambda i: (0, j))
    full = lambda a: pl.BlockSpec(a.shape, lambda i: (0,) * a.ndim)
    hb = lambda w: pl.BlockSpec((1, HEADS, tm, w), lambda i: (i // nt, 0, i % nt, 0))
    acc = lambda rr: pl.BlockSpec((rr, 512), lambda i: (0, 0))
    return pl.pallas_call(
        body, name="gate_bwd", grid=(nb * nt,),
        in_specs=[row(0), row(1), nxt(1), hb(VDIM),
                  row(BLK_ZA), row(BLK_CB), nxt(BLK_CB), row(BLK_CC), prv(BLK_CC), nxt(BLK_CC),
                  row(BLK_CH), prv(BLK_CH), nxt(BLK_CH), row(BLK_ZC), nxt(BLK_ZC),
                  mblk(BLK_CC), mblk(BLK_CH), full(conv_w), full(ga), full(gc), full(gmat)],
        out_specs=[pl.BlockSpec((tm, 2560), lambda i: (i, 0)), hb(VDIM),
                   pl.BlockSpec((1, HEADS, 1, tm), lambda i: (i // nt, 0, 0, i % nt)),
                   pl.BlockSpec((1, 8, 512), lambda i: (i // nt, 0, 0)),
                   acc(1), acc(1), acc(8)],
        out_shape=[jax.ShapeDtypeStruct((r, 2560), BF16), jax.ShapeDtypeStruct((nb, HEADS, s, VDIM), BF16),
                   jax.ShapeDtypeStruct((nb, HEADS, 1, s), F32), jax.ShapeDtypeStruct((nb, 8, 512), F32),
                   jax.ShapeDtypeStruct((1, 512), F32), jax.ShapeDtypeStruct((1, 512), F32),
                   jax.ShapeDtypeStruct((8, 512), F32)],
        compiler_params=_cparams("arbitrary"),
    )(dycat, dycat, dycat, o, p, p, p, p, p, p, p, p, p, p, p, pm, pm, conv_w, ga, gc, gmat)


class _StagedReduce:
    LOC, PRE_S, PRE_R, ICI_S, ICI_R, POST_S, POST_R, OUT, N_SEM = 0, 1, 2, 3, 6, 9, 10, 11, 12

    def __init__(self, shard_shape):
        self.half = (shard_shape[0] // 2, shard_shape[1])

    def scratch(self):
        h = self.half
        return [pltpu.VMEM((4,) + h, F32), pltpu.VMEM((4,) + h, F32), pltpu.VMEM((4,) + h, BF16),
                pltpu.VMEM((3,) + h, BF16), pltpu.VMEM(h, F32), pltpu.SemaphoreType.DMA((self.N_SEM,))]

    def run(self, stage, pin, gout, scr):
        own, sib, wire, rbuf, fin, sems = scr
        r2 = self.half[0]
        x, y, c = lax.axis_index("x"), lax.axis_index("y"), lax.axis_index("c")
        mine = 2 * x + y
        sibling = (x, y, 1 - c)
        chips = [(1 - x, y), (x, 1 - y), (1 - x, 1 - y)]
        rows = lambda half: pl.ds(pl.multiple_of(half * r2, r2), r2)
        mesh = pl.DeviceIdType.MESH

        loc = pltpu.make_async_copy(pin.at[:, rows(c), :], own, sems.at[self.LOC])
        pre = pltpu.make_async_remote_copy(
            src_ref=pin.at[:, rows(1 - c), :], dst_ref=sib, send_sem=sems.at[self.PRE_S],
            recv_sem=sems.at[self.PRE_R], device_id=sibling, device_id_type=mesh)

        def ici(j):
            px, py = chips[j]
            return pltpu.make_async_remote_copy(
                src_ref=wire.at[2 * px + py], dst_ref=rbuf.at[j], send_sem=sems.at[self.ICI_S + j],
                recv_sem=sems.at[self.ICI_R + j], device_id=(px, py, c), device_id_type=mesh)

        def post(half):
            return pltpu.make_async_remote_copy(
                src_ref=fin, dst_ref=gout.at[rows(half), :], send_sem=sems.at[self.POST_S],
                recv_sem=sems.at[self.POST_R], device_id=sibling, device_id_type=mesh)

        keep = pltpu.make_async_copy(fin, gout.at[rows(c), :], sems.at[self.OUT])
        if stage == 0:
            loc.start()
            pre.start()
        elif stage == 1:
            loc.wait()
            pre.wait_recv()
            for blk in range(4):
                tot = own[blk] + sib[blk]
                own[blk] = tot
                wire[blk] = tot.astype(BF16)
            for j in range(3):
                ici(j).start()
        elif stage == 2:
            for j in range(3):
                ici(j).wait_recv()
            tot = own[mine]
            for j in range(3):
                tot = tot + rbuf[j].astype(F32)
            fin[...] = tot
            post(c).start()
            keep.start()
        else:
            post(1 - c).wait_recv()
            pre.wait_send()
            for j in range(3):
                ici(j).wait_send()
            post(c).wait_send()
            keep.wait()


def _attn_bwd(q, k, v, do, lse, delta, km, vm, early, nb, s, t):
    n = s // t
    ne = len(early)
    reds = [_StagedReduce(a.shape[1:]) for a in early]
    n_steps = HEADS * nb
    assert n_steps >= 4

    def body(q_ref, k_ref, v_ref, do_ref, lse_ref, dl_ref, km_ref, vm_ref, *rest):
        pin_refs, rest = rest[:ne], rest[ne:]
        dq_ref, dk_ref, dv_ref, dkm_ref, dvm_ref = rest[:5]
        gout_refs, (p_scr, ds_scr, dq_acc), red_scr = rest[5:5 + ne], rest[5 + ne:8 + ne], rest[8 + ne:]
        b = pl.program_id(1)
        step = pl.program_id(0) * nb + b
        for stage, at in enumerate((0, 1, n_steps - 2, n_steps - 1)):
            @pl.when(step == at)
            def _(stage=stage):
                for a, red in enumerate(reds):
                    red.run(stage, pin_refs[a], gout_refs[a], red_scr[6 * a:6 * a + 6])

        @pl.when(b == 0)
        def _():
            dkm_ref[...] = jnp.zeros_like(dkm_ref)
            dvm_ref[...] = jnp.zeros_like(dvm_ref)

        kr = lax.broadcasted_iota(jnp.int32, (t, t), 0)
        qc = lax.broadcasted_iota(jnp.int32, (t, t), 1)
        km_v, vm_v = km_ref[0, 0], vm_ref[0, 0]
        ptm = jnp.exp(_dot_nt(km_v, q_ref[0, 0]) - lse_ref[0, 0])
        dstm = (ptm * (_dot_nt(vm_v, do_ref[0, 0]) - dl_ref[0, 0])).astype(BF16)
        dkm_ref[0] += _dot(dstm, q_ref[0, 0])
        dvm_ref[0] += _dot(ptm.astype(BF16), do_ref[0, 0])
        dq_acc[...] = _dot_tn(dstm, km_v)
        def tiles(j):
            slot = j % 2
            kj = k_ref[0, 0, j * t:(j + 1) * t, :]
            vj = v_ref[0, 0, j * t:(j + 1) * t, :]
            def products(i):
                cs = slice(i * t, (i + 1) * t)
                return _dot_nt(kj, q_ref[0, 0, cs, :]), _dot_nt(vj, do_ref[0, 0, cs, :])

            nxt, pending = products(j), None
            for i in range(j, n):
                cs = slice(i * t, (i + 1) * t)
                st, dpt = nxt
                if i + 1 < n:
                    nxt = products(i + 1)
                if i == j:
                    st = jnp.where(kr <= qc, st, NEG_INF)
                pt = jnp.exp(st - lse_ref[0, 0, :, cs])
                dst = (pt * (dpt - dl_ref[0, 0, :, cs])).astype(BF16)
                p_scr[slot, :, cs] = pt.astype(BF16)
                ds_scr[slot, :, cs] = dst
                if pending is not None:
                    dq_acc[pending[0], :] += _dot_tn(pending[1], kj)
                pending = (cs, dst)
            dq_acc[pending[0], :] += _dot_tn(pending[1], kj)

        for j in range(n):
            slot = j % 2
            tiles(j)
            dv_ref[0, 0, j * t:(j + 1) * t, :] = _dot(p_scr[slot, :, j * t:s], do_ref[0, 0, j * t:s, :]).astype(BF16)
            dk_ref[0, 0, j * t:(j + 1) * t, :] = _dot(ds_scr[slot, :, j * t:s], q_ref[0, 0, j * t:s, :]).astype(BF16)
        dq_ref[0, 0] = dq_acc[...].astype(BF16)

    big = lambda w: pl.BlockSpec((1, 1, s, w), lambda h, b: (b, h, 0, 0))
    rowv = pl.BlockSpec((1, 1, 1, s), lambda h, b: (b, h, 0, 0))
    mk = lambda w: pl.BlockSpec((1, 1, N_META, w), lambda h, b: (0, h, 0, 0))
    mo = lambda w: pl.BlockSpec((1, N_META, w), lambda h, b: (h, 0, 0))
    return pl.pallas_call(
        body, name="attn_bwd", grid=(HEADS, nb),
        in_specs=[big(QK_PAD), big(QK_PAD), big(VDIM), big(VDIM), rowv, rowv, mk(QK_PAD), mk(VDIM)]
        + [pl.BlockSpec(memory_space=pl.ANY)] * ne,
        out_specs=[big(QK_PAD), big(QK_PAD), big(VDIM), mo(QK_PAD), mo(VDIM)]
        + [pl.BlockSpec(memory_space=pl.ANY)] * ne,
        out_shape=[jax.ShapeDtypeStruct((nb, HEADS, s, QK_PAD), BF16),
                   jax.ShapeDtypeStruct((nb, HEADS, s, QK_PAD), BF16),
                   jax.ShapeDtypeStruct((nb, HEADS, s, VDIM), BF16),
                   jax.ShapeDtypeStruct((HEADS, N_META, QK_PAD), F32),
                   jax.ShapeDtypeStruct((HEADS, N_META, VDIM), F32)]
        + [jax.ShapeDtypeStruct(a.shape[1:], F32) for a in early],
        scratch_shapes=[pltpu.VMEM((2, t, s), BF16), pltpu.VMEM((2, t, s), BF16), pltpu.VMEM((s, QK_PAD), F32)]
        + [sc for red in reds for sc in red.scratch()],
        compiler_params=_cparams("arbitrary", "arbitrary"),
    )(q, k, v, do, lse, delta, km, vm, *early)


def _up_bwd(dq, dk, dv, dkm, dvm, p, pm, tabs, tabs_m, wq_p, wkv_p, gq, gkv, nb, s, tm):
    nt = s // tm
    n = nb * nt
    c_t, sa_t, sb_t = tabs
    cm_t, sam_t, sbm_t = tabs_m

    def kv_path(dkh, dvh, pa, c, sa, sb, wkv, gkvv):
        dkpe = dkh[0][:, NOPE:]
        for h in range(1, HEADS):
            dkpe = dkpe + dkh[h][:, NOPE:]
        dkr = _rope_bwd(dkpe, c, sa, sb)
        dkv = jnp.concatenate([d[:, :NOPE] for d in dkh] + list(dvh), axis=1).astype(BF16)
        ckv = pa[:, Q_RANK:Q_RANK + KV_RANK]
        kvn, rkv = _rms(ckv, gkvv)
        dckv, dg = _rms_bwd(_dot(dkv, wkv), ckv, rkv, gkvv)
        return dckv, dkr, kvn.astype(BF16), dkv, jnp.sum(dg, axis=0, keepdims=True)

    def body(dq_ref, dk_ref, dv_ref, pa_ref, c_ref, sa_ref, sb_ref,
             dkm_ref, dvm_ref, pam_ref, cm_ref, sam_ref, sbm_ref,
             wq_ref, wkv_ref, gq_ref, gkv_ref,
             dpa_ref, dpam_ref, pq_ref, pkv_ref, dgq_ref, dgkv_ref, dwq_ref, dwkv_ref):
        i = pl.program_id(0)

        @pl.when(i == 0)
        def _():
            dwq_ref[...] = jnp.zeros_like(dwq_ref)
            dwkv_ref[...] = jnp.zeros_like(dwkv_ref)
            dgq_ref[...] = jnp.zeros_like(dgq_ref)
            dgkv_ref[...] = jnp.zeros_like(dgkv_ref)

        @pl.when(i < n)
        def _():
            c, sa, sb = c_ref[...], sa_ref[...], sb_ref[...]
            pa = pa_ref[...]
            parts = []
            for h in range(HEADS):
                dqh = dq_ref[0, h].astype(F32) * ATTN_SCALE
                parts += [dqh[:, :NOPE], _rope_bwd(dqh[:, NOPE:], c, sa, sb)]
            dql = jnp.concatenate(parts, axis=1).astype(BF16)
            cq = pa[:, 0:Q_RANK]
            gqv = gq_ref[...]
            qn, rq = _rms(cq, gqv)
            dwq_ref[...] += _dot_tn(dql, qn.astype(BF16))
            dcq, dg = _rms_bwd(_dot(dql, wq_ref[...]), cq, rq, gqv)
            dgq_ref[...] += jnp.sum(dg, axis=0, keepdims=True)
            dckv, dkr, kvn, dkv, dgk = kv_path([dk_ref[0, h].astype(F32) for h in range(HEADS)],
                                               [dv_ref[0, h].astype(F32) for h in range(HEADS)],
                                               pa, c, sa, sb, wkv_ref[...], gkv_ref[...])
            dwkv_ref[...] += _dot_tn(dkv, kvn)
            dgkv_ref[...] += dgk
            dpa_ref[...] = jnp.concatenate([dcq, dckv, dkr], axis=1).astype(BF16)

        @pl.when(i == n)
        def _():
            dckv, dkr, kvn, dkv, dgk = kv_path([dkm_ref[h] for h in range(HEADS)],
                                               [dvm_ref[h] for h in range(HEADS)],
                                               pam_ref[...], cm_ref[...], sam_ref[...], sbm_ref[...],
                                               wkv_ref[...], gkv_ref[...])
            dwkv_ref[...] += _dot_tn(dkv, kvn)
            dgkv_ref[...] += dgk
            dpam_ref[...] = jnp.concatenate([jnp.zeros((N_META, Q_RANK), F32), dckv, dkr], axis=1)
            for h in range(HEADS):
                pq_ref[h] = dwq_ref[QK_PAD * h:QK_PAD * h + NOPE + ROPE, :]
                pkv_ref[h, 0:NOPE, :] = dwkv_ref[NOPE * h:NOPE * (h + 1), :]
                pkv_ref[h, NOPE:NOPE + VDIM, :] = dwkv_ref[512 + VDIM * h:512 + VDIM * (h + 1), :]

    cl = lambda i: jnp.minimum(i, n - 1)
    hb = lambda w: pl.BlockSpec((1, HEADS, tm, w), lambda i: (cl(i) // nt, 0, cl(i) % nt, 0))
    tab = pl.BlockSpec((tm, 128), lambda i: (cl(i) % nt, 0))
    full = lambda a: pl.BlockSpec(a.shape, lambda i: (0,) * a.ndim)
    const = lambda shape: pl.BlockSpec(shape, lambda i: (0,) * len(shape))
    return pl.pallas_call(
        body, name="up_bwd", grid=(n + 1,),
        in_specs=[hb(QK_PAD), hb(QK_PAD), hb(VDIM), pl.BlockSpec((tm, 512), lambda i: (cl(i), 0)), tab, tab, tab,
                  full(dkm), full(dvm), pl.BlockSpec((N_META, 512), lambda i: (0, 0)),
                  full(cm_t), full(sam_t), full(sbm_t), full(wq_p), full(wkv_p), full(gq), full(gkv)],
        out_specs=[pl.BlockSpec((tm, 512), lambda i: (cl(i), 0)), const((N_META, 512)),
                   const((HEADS, NOPE + ROPE, Q_RANK)), const((HEADS, NOPE + VDIM, KV_RANK)),
                   const((1, Q_RANK)), const((1, KV_RANK))],
        out_shape=[jax.ShapeDtypeStruct((nb * s, 512), BF16), jax.ShapeDtypeStruct((N_META, 512), F32),
                   jax.ShapeDtypeStruct((HEADS, NOPE + ROPE, Q_RANK), F32),
                   jax.ShapeDtypeStruct((HEADS, NOPE + VDIM, KV_RANK), F32),
                   jax.ShapeDtypeStruct((1, Q_RANK), F32), jax.ShapeDtypeStruct((1, KV_RANK), F32)],
        scratch_shapes=[pltpu.VMEM((HEADS * QK_PAD, Q_RANK), F32), pltpu.VMEM((1024, KV_RANK), F32)],
        compiler_params=_cparams("arbitrary"),
    )(dq, dk, dv, p, c_t, sa_t, sb_t, dkm, dvm, pm, cm_t, sam_t, sbm_t, wq_p, wkv_p, gq, gkv)


def _in_bwd(x2d, dh2, dpa, dpb, meta, dpam, dccm, pm, w_in_p, norm_g, nb, s, tm):
    nt = s // tm
    n = nb * nt

    def body(x_ref, dh_ref, dpa_ref, dpb_ref, mt_ref, dpam_ref, dccm_ref, mc_ref, mh_ref, w_ref, g_ref,
             gx_ref, gm_ref, dw_hbm, dg_ref, acc_ref, sems):
        i = pl.program_id(0)

        @pl.when(i == 0)
        def _():
            acc_ref[...] = jnp.zeros_like(acc_ref)
            dg_ref[...] = jnp.zeros_like(dg_ref)

        def rows(x, dp, dres):
            g = g_ref[...]
            dpb16 = dp.astype(BF16)
            du = _dot(dpb16, w_ref[...])
            u, r1 = _rms(x, g)
            acc_ref[...] += _dot_tn(dpb16, u.astype(BF16))
            dx, dg = _rms_bwd(du, x, r1, g)
            dg_ref[...] += jnp.sum(dg, axis=0, keepdims=True)
            return dx if dres is None else dx + dres

        @pl.when(i < n)
        def _():
            dp = jnp.concatenate([dpa_ref[...], dpb_ref[...]], axis=1)
            gx_ref[...] = rows(x_ref[...], dp, dh_ref[...])

        @pl.when(i == n)
        def _():
            dcc = dccm_ref[0]
            for b in range(1, nb):
                dcc = dcc + dccm_ref[b]
            z8 = jnp.zeros((8, CONV_W), F32)
            dc = jnp.concatenate([z8, dcc * mh_ref[8:16, :]], axis=0)
            dh = jnp.concatenate([z8, dcc * mc_ref[8:16, :]], axis=0)
            z = jnp.zeros((N_META, CONV_W), F32)
            dp = jnp.concatenate([dpam_ref[...], z, z, dc, dh, z], axis=1)
            gm_ref[...] = rows(mt_ref[...], dp, None)
            per = IN_DIM // 4
            cps = [pltpu.make_async_copy(acc_ref.at[0:448], dw_hbm.at[0, 0:448], sems.at[0]),
                   pltpu.make_async_copy(acc_ref.at[512:per + 64], dw_hbm.at[0, 448:per], sems.at[1])]
            for qq in range(1, 4):
                cps.append(pltpu.make_async_copy(acc_ref.at[per * qq + 64:per * (qq + 1) + 64], dw_hbm.at[qq],
                                                 sems.at[qq + 1]))
            for cp in cps:
                cp.start()
            for cp in cps:
                cp.wait()

    cl = lambda i: jnp.minimum(i, n - 1)
    row = lambda w: pl.BlockSpec((tm, w), lambda i: (cl(i), 0))
    full = lambda a: pl.BlockSpec(a.shape, lambda i: (0,) * a.ndim)
    mblk = lambda j: pl.BlockSpec((N_META, 512), lambda i: (0, j))
    return pl.pallas_call(
        body, name="in_bwd", grid=(n + 1,),
        in_specs=[row(D_MODEL), row(D_MODEL), row(512), row(2560), full(meta), full(dpam), full(dccm),
                  mblk(BLK_CC), mblk(BLK_CH), full(w_in_p), full(norm_g)],
        out_specs=[row(D_MODEL), pl.BlockSpec((N_META, D_MODEL), lambda i: (0, 0)),
                   pl.BlockSpec(memory_space=pl.ANY), pl.BlockSpec((1, D_MODEL), lambda i: (0, 0))],
        out_shape=[jax.ShapeDtypeStruct((nb * s, D_MODEL), F32), jax.ShapeDtypeStruct((N_META, D_MODEL), F32),
                   jax.ShapeDtypeStruct((4, IN_DIM // 4, D_MODEL), F32), jax.ShapeDtypeStruct((1, D_MODEL), F32)],
        scratch_shapes=[pltpu.VMEM((IN_PAD, D_MODEL), F32), pltpu.SemaphoreType.DMA((5,))],
        compiler_params=_cparams("arbitrary"),
    )(x2d, dh2, dpa, dpb, meta, dpam, dccm, pm, pm, w_in_p, norm_g)


def _gather_weights(w_in_shard, w_out_shard, split, pieces, out_rows, whole, zero_fills):
    ns, nw, nz = len(split), len(whole), len(zero_fills)
    flat = [(a, pc) for a in range(ns) for pc in pieces[a]]
    nk = len(flat)
    hh = HEAD_ROWS // 2
    assert hh % 16 == 0

    def body(*refs):
        ins, wins, zins = refs[2:2 + ns], refs[2 + ns:2 + ns + nw], refs[2 + ns + nw:2 + ns + nw + nz]
        n_in = 2 + ns + nw + nz
        head_ref, shard16, w_out16 = refs[n_in:n_in + 3]
        outs, wcat = refs[n_in + 3:n_in + 3 + ns], refs[n_in + 3 + ns:n_in + 3 + ns + nw]
        scr = refs[n_in + 3 + ns + nw:]
        stage, wouts = scr[:ns], scr[ns:ns + nw]
        (send_sems, recv_sems, fwd_send, fwd_recv, loc_sems, w_send, w_recv, w_loc, z_sems,
         h_send, h_recv, h_relay, h_pass) = scr[ns + nw:]
        x, y, c = lax.axis_index("x"), lax.axis_index("y"), lax.axis_index("c")
        mine = 2 * x + y
        chips = [(1 - x, y), (x, 1 - y), (1 - x, 1 - y)]
        chip_of = [2 * px + py for px, py in chips]
        shard16[...] = refs[0][...].astype(BF16)
        w_out16[...] = refs[1][...].astype(BF16)
        for a in range(ns):
            stage[a][...] = ins[a][...].astype(BF16)

        def head_rows(half):
            return pl.ds(pl.multiple_of(half * hh, 16), hh)

        def head_copy(turn):
            dest = (1 - c, c, c) if turn == 0 else (c, 1 - c, c)
            return pltpu.make_async_remote_copy(
                src_ref=shard16.at[head_rows(c)], dst_ref=head_ref.at[head_rows(c)], send_sem=h_send.at[turn],
                recv_sem=h_recv.at[0], device_id=dest, device_id_type=pl.DeviceIdType.MESH)

        def head_relay():
            ref = head_ref.at[head_rows(c)]
            return pltpu.make_async_remote_copy(
                src_ref=ref, dst_ref=ref, send_sem=h_relay.at[0], recv_sem=h_recv.at[0],
                device_id=(1, 1, c), device_id_type=pl.DeviceIdType.MESH)

        def head_pass(half):
            ref = head_ref.at[head_rows(half)]
            return pltpu.make_async_remote_copy(
                src_ref=ref, dst_ref=ref, send_sem=h_pass.at[0], recv_sem=h_pass.at[1],
                device_id=(x, y, 1 - c), device_id_type=pl.DeviceIdType.MESH)

        head_ref[HEAD_ROWS:IN_HEAD, :] = jnp.zeros((IN_HEAD - HEAD_ROWS, D_MODEL), BF16)

        @pl.when(mine == 0)
        def _():
            head_copy(0).start()
            head_ref[0:HEAD_ROWS, :] = shard16[0:HEAD_ROWS, :]

        def src(k):
            a, (s0, nr, _, _, _, _) = flat[k]
            return stage[a].at[s0:s0 + nr]

        def dst(k, q):
            a, (_, nr, per, first, rest, _) = flat[k]
            row = per * q + first + (rest - first) * jnp.minimum(q, 1)
            return outs[a].at[pl.ds(pl.multiple_of(row, 16), nr)]

        def ici(k, j, q):
            px, py = chips[j]
            return pltpu.make_async_remote_copy(
                src_ref=src(k), dst_ref=dst(k, q), send_sem=send_sems.at[k, j], recv_sem=recv_sems.at[k, j],
                device_id=(px, py, c), device_id_type=pl.DeviceIdType.MESH)

        def fwd(k, j):
            ref = dst(k, chip_of[j])
            return pltpu.make_async_remote_copy(
                src_ref=ref, dst_ref=ref, send_sem=fwd_send.at[k, j], recv_sem=fwd_recv.at[k, j],
                device_id=(x, y, 1 - c), device_id_type=pl.DeviceIdType.MESH)

        def wcopy(b, j, q):
            px, py = chips[j]
            return pltpu.make_async_remote_copy(
                src_ref=wins[b], dst_ref=wouts[b].at[q], send_sem=w_send.at[b, j], recv_sem=w_recv.at[b, j],
                device_id=(px, py, c), device_id_type=pl.DeviceIdType.MESH)

        local = [pltpu.make_async_copy(src(k), dst(k, mine), loc_sems.at[k]) for k in range(nk)]
        local += [pltpu.make_async_copy(wins[b], wouts[b].at[mine], w_loc.at[b]) for b in range(nw)]
        for z, (a, _, row0) in enumerate(zero_fills):
            local.append(pltpu.make_async_copy(zins[z], outs[a].at[row0:row0 + zins[z].shape[0]], z_sems.at[z]))
        wsends = [wcopy(b, j, mine) for b in range(nw) for j in range(3)]
        for cp in local + wsends:
            cp.start()

        for half in (0, 1):
            @pl.when(c == half)
            def _(half=half):
                my_k = [k for k in range(nk) if flat[k][1][5] == half]
                other_k = [k for k in range(nk) if flat[k][1][5] != half]
                sends = [ici(k, j, mine) for k in my_k for j in range(3)]
                for cp in sends:
                    cp.start()
                passed = []
                for k in my_k:
                    for j in range(3):
                        ici(k, j, chip_of[j]).wait_recv()
                        cp = fwd(k, j)
                        cp.start()
                        passed.append(cp)
                for k in other_k:
                    for j in range(3):
                        fwd(k, j).wait_recv()
                for cp in sends + passed:
                    cp.wait_send()

        for b in range(nw):
            for j in range(3):
                wcopy(b, j, chip_of[j]).wait_recv()
        for cp in wsends:
            cp.wait_send()
        for cp in local:
            cp.wait()
        for b in range(nw):
            cols = wins[b].shape[-1]
            for q in range(4):
                wcat[b][..., cols * q:cols * (q + 1)] = wouts[b][q]

        @pl.when(mine == 0)
        def _():
            head_copy(0).wait_send()
            head_copy(1).start()
            head_copy(1).wait_send()

        @pl.when(mine != 0)
        def _():
            hands_on = mine == 2 - c
            head_copy(0).wait_recv()

            @pl.when(hands_on)
            def _():
                head_relay().start()

            head_pass(c).start()
            head_pass(1 - c).wait_recv()
            head_pass(c).wait_send()

            @pl.when(hands_on)
            def _():
                head_relay().wait_send()

    vmem = pl.BlockSpec(memory_space=pltpu.VMEM)
    dma = pltpu.SemaphoreType.DMA
    zeros = [z for _, z, _ in zero_fills]
    return pl.pallas_call(
        body, name="gather_weights",
        in_specs=[vmem] * (2 + ns + nw + nz), out_specs=[vmem] * (3 + ns + nw),
        out_shape=([jax.ShapeDtypeStruct((IN_HEAD, D_MODEL), BF16), jax.ShapeDtypeStruct(w_in_shard.shape, BF16),
                    jax.ShapeDtypeStruct(w_out_shard.shape, BF16)]
                   + [jax.ShapeDtypeStruct((out_rows[a], split[a].shape[1]), BF16) for a in range(ns)]
                   + [jax.ShapeDtypeStruct(w.shape[:-1] + (4 * w.shape[-1],), w.dtype) for w in whole]),
        scratch_shapes=[pltpu.VMEM(a.shape, BF16) for a in split] + [pltpu.VMEM((4,) + w.shape, w.dtype) for w in whole]
        + [dma((nk, 3)), dma((nk, 3)), dma((nk, 3)), dma((nk, 3)), dma((nk,)),
           dma((nw, 3)), dma((nw, 3)), dma((nw,)), dma((nz,)),
           dma((2,)), dma((1,)), dma((1,)), dma((2,))],
        compiler_params=pltpu.CompilerParams(vmem_limit_bytes=VMEM_LIMIT),
    )(w_in_shard, w_out_shard, *split, *whole, *zeros)


def _reduce_grads(parts, small):
    n = len(parts)
    ns = len(small)
    shapes = [a.shape[1:] for a in parts]
    halves = [(sh[0] // 2, sh[1]) for sh in shapes]
    sm_blocks = [a.shape[1] // 128 for a, _ in small]
    sm_first = [sum(nr * nblk for (_, nr), nblk in zip(small[:k], sm_blocks[:k])) for k in range(ns)]
    sm_rows = -(-(sm_first[-1] + small[-1][1] * sm_blocks[-1]) // 8) * 8
    sm_shape = (sm_rows, 128)

    def body(*refs):
        pin, sm_in = refs[:n], refs[n:n + ns]
        gout, sm_out = refs[n + ns:2 * n + ns], refs[2 * n + ns]
        scr = refs[2 * n + ns + 1:]
        own, sib, wire, rbuf = scr[:n], scr[n:2 * n], scr[2 * n:3 * n], scr[3 * n:4 * n]
        (sbuf, send_sems, recv_sems, loc_sems, pre_send, pre_recv, post_send, post_recv,
         sm_send, sm_recv, sm_pack) = scr[4 * n:]
        x, y, c = lax.axis_index("x"), lax.axis_index("y"), lax.axis_index("c")
        mine = 2 * x + y
        sm_pack[...] = jnp.zeros(sm_shape, F32)
        for k, (_, nr) in enumerate(small):
            for i in range(nr):
                for j in range(sm_blocks[k]):
                    row = sm_first[k] + i * sm_blocks[k] + j
                    sm_pack[row:row + 1, :] = sm_in[k][i:i + 1, 128 * j:128 * (j + 1)]
        me = 4 * x + 2 * y + c
        sibling = (x, y, 1 - c)

        def rows(a, half):
            r2 = halves[a][0]
            return pl.ds(pl.multiple_of(half * r2, r2), r2)

        near = (jnp.where(c == 0, 1 - x, x), jnp.where(c == 0, y, 1 - y))
        far = (jnp.where(c == 0, x, 1 - x), jnp.where(c == 0, 1 - y, y))
        chip = lambda p: 2 * p[0] + p[1]
        blocks = [3 - mine, chip(near), chip(far), mine]

        def pre(a, k):
            q = blocks[k]
            return pltpu.make_async_remote_copy(
                src_ref=pin[a].at[q, rows(a, 1 - c), :], dst_ref=sib[a].at[q],
                send_sem=pre_send.at[a, q], recv_sem=pre_recv.at[a, q], device_id=sibling,
                device_id_type=pl.DeviceIdType.MESH)

        def ici(a, m):
            px, py = near if m < 2 else far
            return pltpu.make_async_remote_copy(
                src_ref=wire[a].at[blocks[m]], dst_ref=rbuf[a].at[m], send_sem=send_sems.at[a, m],
                recv_sem=recv_sems.at[a, m], device_id=(px, py, c), device_id_type=pl.DeviceIdType.MESH)

        def post(a, half):
            ref = gout[a].at[rows(a, half), :]
            return pltpu.make_async_remote_copy(
                src_ref=ref, dst_ref=ref, send_sem=post_send.at[a], recv_sem=post_recv.at[a],
                device_id=sibling, device_id_type=pl.DeviceIdType.MESH)

        def small_copy(kk):
            peer = (x ^ (kk >> 2), y ^ ((kk >> 1) & 1), c ^ (kk & 1))
            return pltpu.make_async_remote_copy(
                src_ref=sm_pack, dst_ref=sbuf.at[kk], send_sem=sm_send.at[kk - 1], recv_sem=sm_recv.at[kk - 1],
                device_id=peer, device_id_type=pl.DeviceIdType.MESH)

        local = [[pltpu.make_async_copy(pin[a].at[blocks[k], rows(a, c), :], own[a].at[blocks[k]], loc_sems.at[a, k])
                  for k in range(4)] for a in range(n)]
        pres = [[pre(a, k) for k in range(4)] for a in range(n)]
        smalls = [small_copy(kk) for kk in range(1, 8)]
        for a in range(n):
            for k in range(4):
                local[a][k].start()
                pres[a][k].start()
        for cp in smalls:
            cp.start()
        sbuf[0] = sm_pack[...]
        sends = []
        for a in range(n):
            for k in range(4):
                local[a][k].wait()
                pres[a][k].wait_recv()
                tot = own[a][blocks[k]] + sib[a][blocks[k]]
                if k == 2:
                    ici(a, 0).wait_recv()
                    tot = tot + rbuf[a][0].astype(F32)
                own[a][blocks[k]] = tot
                if k < 3:
                    wire[a][blocks[k]] = tot.astype(BF16)
                    cp = ici(a, k)
                    cp.start()
                    sends.append(cp)
        for cp in smalls:
            cp.wait_recv()
        total = sbuf[me]
        for d in range(1, 8):
            total = total + sbuf[me ^ d]
        sm_out[...] = total
        posts = []
        for a in range(n):
            fin = own[a][mine]
            for m in (1, 2):
                ici(a, m).wait_recv()
                fin = fin + rbuf[a][m].astype(F32)
            gout[a][rows(a, c), :] = fin
            cp = post(a, c)
            cp.start()
            posts.append(cp)
        for a in range(n):
            post(a, 1 - c).wait_recv()
        for cp in [cp for row in pres for cp in row] + sends + smalls + posts:
            cp.wait_send()

    vmem = pl.BlockSpec(memory_space=pltpu.VMEM)
    dma = pltpu.SemaphoreType.DMA
    return pl.pallas_call(
        body, name="reduce_grads",
        in_specs=[pl.BlockSpec(memory_space=pl.ANY)] * n + [vmem] * ns, out_specs=[vmem] * (n + 1),
        out_shape=[jax.ShapeDtypeStruct(sh, F32) for sh in shapes] + [jax.ShapeDtypeStruct(sm_shape, F32)],
        scratch_shapes=([pltpu.VMEM((4,) + hs, F32) for hs in halves] + [pltpu.VMEM((4,) + hs, F32) for hs in halves]
                        + [pltpu.VMEM((4,) + hs, BF16) for hs in halves]
                        + [pltpu.VMEM((3,) + hs, BF16) for hs in halves]
                        + [pltpu.VMEM((8,) + sm_shape, F32), dma((n, 3)), dma((n, 3)), dma((n, 4)),
                           dma((n, 4)), dma((n, 4)), dma((n,)), dma((n,)), dma((7,)), dma((7,)),
                           pltpu.VMEM(sm_shape, F32)]),
        compiler_params=pltpu.CompilerParams(vmem_limit_bytes=VMEM_LIMIT),
    )(*parts, *[a for a, _ in small])


def _adamw_update(w_ref, g_ref, m_ref, v_ref, d_ref, nm_ref, nv_ref):
    gv = g_ref[...]
    nm = ADAM_B1 * m_ref[...] + (1.0 - ADAM_B1) * gv
    nv = ADAM_B2 * v_ref[...] + (1.0 - ADAM_B2) * (gv * gv)
    m_hat = nm / (1.0 - ADAM_B1 ** ADAM_STEP)
    v_hat = nv / (1.0 - ADAM_B2 ** ADAM_STEP)
    d_ref[...] = -ADAM_LR * (m_hat / (jnp.sqrt(v_hat) + ADAM_EPS) + ADAM_WD * w_ref[...])
    nm_ref[...] = nm
    nv_ref[...] = nv


def _adamw_small(ws, gs, ms, vs):
    k = len(ws)

    def body(*refs):
        ins, outs = refs[:4 * k], refs[4 * k:]
        for a in range(k):
            _adamw_update(ins[a], ins[k + a], ins[2 * k + a], ins[3 * k + a], outs[a], outs[k + a], outs[2 * k + a])

    out = pl.pallas_call(
        body, name="adamw_small",
        out_shape=[jax.ShapeDtypeStruct(w.shape, F32) for w in ws] * 3,
        compiler_params=pltpu.CompilerParams(vmem_limit_bytes=VMEM_LIMIT),
    )(*ws, *gs, *ms, *vs)
    return out[:k], out[k:2 * k], out[2 * k:]


def _adamw(w, g, m, v, name):
    shape = w.shape
    w2, g2, m2, v2 = (a.reshape((-1, shape[-1])) for a in (w, g, m, v))

    def body(w_ref, g_ref, m_ref, v_ref, d_ref, nm_ref, nv_ref):
        _adamw_update(w_ref, g_ref, m_ref, v_ref, d_ref, nm_ref, nv_ref)

    rows, cols = w2.shape
    nblk = cols // 256 if cols % 256 == 0 and rows >= 64 else 1
    blk = pl.BlockSpec((rows, cols // nblk), lambda j: (0, j))
    out = pl.pallas_call(
        body, name=name, grid=(nblk,), in_specs=[blk] * 4, out_specs=[blk] * 3,
        out_shape=[jax.ShapeDtypeStruct(w2.shape, F32)] * 3,
        compiler_params=_cparams("parallel"),
    )(w2, g2, m2, v2)
    return tuple(a.reshape(shape) for a in out)


def kernel(x, meta_tokens, norm_g, w_in, q_norm_g, w_q_up, kv_norm_g, w_kv_up, conv_w, attn_out_g, conv_out_g, w_out, final_norm_g, loss_target, m_meta_tokens, m_norm_g, m_w_in, m_q_norm_g, m_w_q_up, m_kv_norm_g, m_w_kv_up, m_conv_w, m_attn_out_g, m_conv_out_g, m_w_out, m_final_norm_g, v_meta_tokens, v_norm_g, v_w_in, v_q_norm_g, v_w_q_up, v_kv_norm_g, v_w_kv_up, v_conv_w, v_attn_out_g, v_conv_out_g, v_w_out, v_final_norm_g):
    nb, s, _ = x.shape
    tm = min(ROW_TILE, s)
    ta = min(ATTN_TILE, s)
    assert s % tm == 0 and s % ta == 0 and tm % 16 == 0
    r = nb * s

    tr = lambda a: jnp.transpose(a[0])
    w_head, w_in_shard, w_out_shard, wq_p, wkv_p, g_cw, meta_f = _gather_weights(
        tr(w_in), w_out[0], [tr(w_q_up), tr(w_kv_up)],
        [W_Q_PIECES, W_KV_PIECES], [HEADS * QK_PAD, 1024],
        [jnp.transpose(conv_w, (1, 0, 2)), meta_tokens],
        [(0, jnp.zeros((64, Q_RANK), BF16), QK_PAD * h + NOPE + ROPE) for h in range(HEADS)])
    conv_f = g_cw.reshape(3, CONV_W)

    c_all, sa_all, sb_all = _rope_tables(N_META + s)
    tabs_m = (c_all[:N_META], sa_all[:N_META], sb_all[:N_META])
    tabs = (c_all[N_META:], sa_all[N_META:], sb_all[N_META:])
    gid = np.arange(CONV_W) // CONV_GROUP
    gmat = jnp.asarray(np.where(gid[:, None] == gid[None, :], 1.0 / CONV_GROUP, 0.0), BF16)
    ga, gc = attn_out_g, conv_out_g
    gf = final_norm_g.reshape(1, D_MODEL)

    x2d = x.reshape(r, D_MODEL)
    tgt2d = loss_target.reshape(r, D_MODEL)

    ph, q, k, v, pmh, km, vm, w_out_f, w_in_part = _fwd_proj(
        x2d, meta_f, tabs, tabs_m, norm_g, w_head, q_norm_g, wq_p, kv_norm_g, wkv_p, w_out_shard, w_in_shard,
        nb, s, tm)
    o, lse, w_in_p = _attn_fwd(q, k, v, km, vm, w_in_shard, w_in_part, nb, s, ta)
    dh2, dycat, dw_out, dgf, loss_acc, pt, pmt = _out_fwd_bwd(x2d, tgt2d, o, meta_f, norm_g, w_in_p, conv_f, ga, gc,
                                                              gmat, w_out_f, gf, nb, s, tm)
    dpb, do, delta, dccm, dga, dgc, dcw = _gate_bwd(dycat, o, pt, pmt, conv_f, ga, gc, gmat, nb, s, tm)
    p_out = dw_out.reshape(4, D_MODEL // 4, D_MODEL)
    dq, dk, dv, dkm, dvm, g_w_out = _attn_bwd(q, k, v, do, lse, delta, km, vm, [p_out], nb, s, ta)
    dpa, dpam, p_q, p_kv, dgq, dgkv = _up_bwd(dq, dk, dv, dkm, dvm, ph, pmh, tabs, tabs_m, wq_p, wkv_p,
                                              q_norm_g, kv_norm_g, nb, s, tm)
    gx, gmeta, p_in, dng = _in_bwd(x2d, dh2, dpa, dpb, meta_f, dpam, dccm, pmt, w_in_p, norm_g, nb, s, tm)

    g_w_in_t, g_w_q_t, g_w_kv_t, small_sum = _reduce_grads(
        [p_in, p_q, p_kv],
        [(dng, 1), (dgq, 1), (dgkv, 1), (dga, 1), (dgc, 1), (dgf, 1), (dcw, 3), (gmeta, N_META), (loss_acc, 1)])
    ssum = small_sum.reshape(-1)

    def take(off, n):
        return ssum[off:off + n], off + n

    off = 0
    g_norm, off = take(off, D_MODEL)
    g_qn, off = take(off, Q_RANK)
    g_kvn, off = take(off, KV_RANK)
    g_ga, off = take(off, CONV_W)
    g_gc, off = take(off, CONV_W)
    g_gf, off = take(off, D_MODEL)
    g_cw_all, off = take(off, 3 * CONV_W)
    g_meta_all, off = take(off, N_META * D_MODEL)
    loss = ssum[off]
    chip = 2 * lax.axis_index("x") + lax.axis_index("y")
    g_conv = lax.dynamic_slice(g_cw_all.reshape(3, CONV_W), (0, chip * 128), (3, 128))
    g_mt = lax.dynamic_slice(g_meta_all.reshape(N_META, D_MODEL), (0, chip * 256), (N_META, 256))

    grads = {
        "meta_tokens": g_mt, "norm_g": g_norm.reshape(1, -1), "w_in": g_w_in_t, "q_norm_g": g_qn.reshape(1, -1),
        "w_q_up": g_w_q_t, "kv_norm_g": g_kvn.reshape(1, -1), "w_kv_up": jnp.transpose(g_w_kv_t)[None],
        "conv_w": g_conv[None], "attn_out_g": g_ga.reshape(1, -1), "conv_out_g": g_gc.reshape(1, -1),
        "w_out": g_w_out[None], "final_norm_g": g_gf,
    }
    transposed = ("w_in", "w_q_up")
    weights = {
        "meta_tokens": (meta_tokens, m_meta_tokens, v_meta_tokens), "norm_g": (norm_g, m_norm_g, v_norm_g),
        "w_in": (w_in, m_w_in, v_w_in), "q_norm_g": (q_norm_g, m_q_norm_g, v_q_norm_g),
        "w_q_up": (w_q_up, m_w_q_up, v_w_q_up), "kv_norm_g": (kv_norm_g, m_kv_norm_g, v_kv_norm_g),
        "w_kv_up": (w_kv_up, m_w_kv_up, v_w_kv_up), "conv_w": (conv_w, m_conv_w, v_conv_w),
        "attn_out_g": (attn_out_g, m_attn_out_g, v_attn_out_g), "conv_out_g": (conv_out_g, m_conv_out_g, v_conv_out_g),
        "w_out": (w_out, m_w_out, v_w_out), "final_norm_g": (final_norm_g, m_final_norm_g, v_final_norm_g),
    }
    names = list(weights)
    small = [nme for nme in names if nme != "w_in"]

    def view(nme, a):
        if nme in transposed:
            return a if a.ndim == 2 else tr(a)
        if nme == "conv_w":
            return jnp.transpose(a.reshape(1, 3, -1), (1, 0, 2))
        if a.ndim == 3:
            return a[0]
        return a.reshape(1, -1) if a.ndim == 1 else a

    def unview(nme, a):
        if nme in transposed:
            return jnp.transpose(a)[None]
        if nme == "conv_w":
            return jnp.transpose(a, (1, 0, 2))
        return a.reshape(weights[nme][0].shape)

    res_small = _adamw_small(*[[view(nme, a) for nme, a in zip(small, col)] for col in (
        [weights[nme][0] for nme in small], [grads[nme] for nme in small],
        [weights[nme][1] for nme in small], [weights[nme][2] for nme in small])])
    w_, m_, v_ = weights["w_in"]
    res = _adamw(tr(w_), grads["w_in"], tr(m_), tr(v_), "adamw_w_in")
    upd = {"w_in": tuple(jnp.transpose(a)[None] for a in (grads["w_in"],) + res)}
    for j, nme in enumerate(small):
        upd[nme] = (unview(nme, view(nme, grads[nme])),) + tuple(unview(nme, r[j]) for r in res_small)
    grads = {nme: upd[nme][0] for nme in names}
    deltas, new_m, new_v = ([upd[nme][j] for nme in names] for j in (1, 2, 3))

    grad_x = gx.reshape(nb, s, D_MODEL)
    return (loss, grad_x, *[grads[nme] for nme in names], *deltas, *new_m, *new_v)
```

```python
import functools

import jax
import jax.numpy as jnp
import numpy as np
from jax import lax
from jax.experimental import pallas as pl
from jax.experimental.pallas import tpu as pltpu

F32 = jnp.float32
BF16 = jnp.bfloat16

D_MODEL = 1024
N_META = 16
HEADS = 4
NOPE = 128
ROPE = 64
VDIM = 128
QK_PAD = 256
Q_RANK = 256
KV_RANK = 128
CONV_W = 512
CONV_GROUP = 64
ROPE_THETA = 10000.0
EPS = 1e-6
ATTN_SCALE = (NOPE + ROPE) ** -0.5
IN_DIM = 3008
IN_PAD = 3072
HEAD_ROWS = Q_RANK + KV_RANK + ROPE
IN_HEAD = 512
IN_TAIL = IN_PAD - IN_HEAD
BLK_ZA, BLK_CB, BLK_CC, BLK_CH, BLK_ZC = 0, 1, 2, 3, 4
NEG_INF = -1e30

ADAM_LR = 0.001
ADAM_B1 = 0.9
ADAM_B2 = 0.999
ADAM_EPS = 1e-08
ADAM_WD = 0.01
ADAM_STEP = 10

ROW_TILE = 512
ATTN_TILE = 256
VMEM_LIMIT = 56 * 1024 * 1024

NT = (((1,), (1,)), ((), ()))
TN = (((0,), (0,)), ((), ()))


def _cparams(*sem):
    return pltpu.CompilerParams(dimension_semantics=sem, vmem_limit_bytes=VMEM_LIMIT)


def _dot(a, b):
    return jnp.dot(a, b, preferred_element_type=F32)


def _dot_nt(a, b):
    return lax.dot_general(a, b, NT, preferred_element_type=F32)


def _dot_tn(a, b):
    return lax.dot_general(a, b, TN, preferred_element_type=F32)


def _rms(x, g):
    r = lax.rsqrt(jnp.mean(x * x, axis=-1, keepdims=True) + EPS)
    return x * r * g, r


def _rms_bwd(dy, x, r, g):
    xh = x * r
    dyg = dy * g
    dx = r * (dyg - xh * jnp.mean(dyg * xh, axis=-1, keepdims=True))
    return dx, dy * xh


def _sigmoid(z):
    return 1.0 / (1.0 + jnp.exp(-z))


def _rope(b, c, sa, sb):
    return b * c + pltpu.roll(b, 96, 1) * sa + pltpu.roll(b, 32, 1) * sb


def _rope_bwd(d, c, sa, sb):
    return d * c + pltpu.roll(d * sa, 32, 1) + pltpu.roll(d * sb, 96, 1)


def _group_mean(x, gmat):
    hi = x.astype(BF16)
    lo = (x - hi.astype(F32)).astype(BF16)
    return _dot(hi, gmat) + _dot(lo, gmat)


def _row_of(col, rows):
    return jnp.transpose(jnp.broadcast_to(col, (rows, 128)))[0:1, :]


def _rope_tables(n_pos):
    half = ROPE // 2
    inv_freq = (np.float32(1.0) / (np.float32(ROPE_THETA) ** (np.arange(half, dtype=np.float32) / np.float32(half))))
    ang = np.arange(n_pos, dtype=np.float32)[:, None] * inv_freq.astype(np.float32)[None, :]
    cos, sin = np.cos(ang).astype(np.float32), np.sin(ang).astype(np.float32)
    z = np.zeros((n_pos, half), np.float32)
    c = np.concatenate([cos, cos, z, z], axis=1)
    sa = np.concatenate([-sin, z, z, z], axis=1)
    sb = np.concatenate([z, sin, z, z], axis=1)
    return jnp.asarray(c), jnp.asarray(sa), jnp.asarray(sb)


W_IN_PIECES_1 = ((0, 80, 752, 0, 64, 0), (384, 64, 752, 384, 448, 1), (448, 16, 752, 512, 512, 1))
W_IN_PIECES_2 = ((80, 304, 752, 80, 144, 0), (464, 288, 752, 528, 528, 1))
W_Q_PIECES = ((0, 96, 256, 0, 0, 0), (96, 96, 256, 96, 96, 1))
W_KV_PIECES = ((0, 128, 128, 0, 0, 0), (128, 128, 128, 512, 512, 1))
W_OUT_PIECES = ((0, 128, 256, 0, 0, 0), (128, 128, 256, 128, 128, 1))


class _StagedGather:
    STAGES = 4

    @staticmethod
    def steps(n_steps):
        return (0, 5 * n_steps // 8, 7 * n_steps // 8, n_steps - 1)

    def __init__(self, pieces, zero_rows=None):
        self.pieces = pieces
        self.zero_rows = zero_rows

    def scratch(self):
        nk, dma = len(self.pieces), pltpu.SemaphoreType.DMA
        return [dma((nk, 3)), dma((nk, 3)), dma((nk, 3)), dma((nk, 3)), dma((nk,))]

    def vmem_scratch(self, shard_shape, out_shape):
        return [pltpu.VMEM(shard_shape, BF16), pltpu.VMEM(out_shape, BF16),
                pltpu.SemaphoreType.DMA((4 * len(self.pieces) + 1,)),
                pltpu.SemaphoreType.DMA((len(self.pieces),))] + self.scratch()

    def run_vmem(self, stage, shard_ref, out_ref, scr):
        src_scr, land_scr, io_sems, ld_sems = scr[:4]
        spans = []
        for _, nr, per, first, rest, _ in self.pieces:
            spans += [(per * q + (first if q == 0 else rest), nr) for q in range(4)]
        if self.zero_rows is not None:
            spans.append(self.zero_rows)
        flush = [pltpu.make_async_copy(land_scr.at[r0:r0 + nr], out_ref.at[r0:r0 + nr], io_sems.at[n])
                 for n, (r0, nr) in enumerate(spans)]
        if stage == 0:
            loads = [pltpu.make_async_copy(shard_ref.at[s0:s0 + nr], src_scr.at[s0:s0 + nr], ld_sems.at[k])
                     for k, (s0, nr, _, _, _, _) in enumerate(self.pieces)]
            for cp in loads:
                cp.start()
            if self.zero_rows is not None:
                r0, nr = self.zero_rows
                land_scr[r0:r0 + nr, :] = jnp.zeros((nr, land_scr.shape[1]), BF16)
            for cp in loads:
                cp.wait()
        if stage < self.STAGES:
            self.run(stage, src_scr, land_scr, scr[4:])
        for cp in flush:
            if stage == self.STAGES - 1:
                cp.start()
            if stage == self.STAGES:
                cp.wait()

    def run(self, stage, src_ref, out_ref, scr):
        send_sems, recv_sems, fwd_send, fwd_recv, loc_sems = scr
        pieces = self.pieces
        nk = len(pieces)
        x, y, c = lax.axis_index("x"), lax.axis_index("y"), lax.axis_index("c")
        mine = 2 * x + y
        chips = [(1 - x, y), (x, 1 - y), (1 - x, 1 - y)]
        chip_of = [2 * px + py for px, py in chips]
        mesh = pl.DeviceIdType.MESH

        def src(k):
            s0, nr = pieces[k][0], pieces[k][1]
            return src_ref.at[s0:s0 + nr]

        def dst(k, q):
            _, nr, per, first, rest, _ = pieces[k]
            row = per * q + first + (rest - first) * jnp.minimum(q, 1)
            return out_ref.at[pl.ds(pl.multiple_of(row, 16), nr)]

        def ici(k, j, q):
            px, py = chips[j]
            return pltpu.make_async_remote_copy(
                src_ref=src(k), dst_ref=dst(k, q), send_sem=send_sems.at[k, j], recv_sem=recv_sems.at[k, j],
                device_id=(px, py, c), device_id_type=mesh)

        def fwd(k, j):
            ref = dst(k, chip_of[j])
            return pltpu.make_async_remote_copy(
                src_ref=ref, dst_ref=ref, send_sem=fwd_send.at[k, j], recv_sem=fwd_recv.at[k, j],
                device_id=(x, y, 1 - c), device_id_type=mesh)

        def relay(k, half):
            ref = dst(k, chip_of[half])
            px, py = chips[1 - half]
            return pltpu.make_async_remote_copy(
                src_ref=ref, dst_ref=ref, send_sem=send_sems.at[k, 2], recv_sem=recv_sems.at[k, 2],
                device_id=(px, py, c), device_id_type=mesh)

        local = [pltpu.make_async_copy(src(k), dst(k, mine), loc_sems.at[k]) for k in range(nk)]
        if stage == 0:
            for cp in local:
                cp.start()
        if stage == 3:
            for cp in local:
                cp.wait()
        for half in (0, 1):
            @pl.when(c == half)
            def _(half=half):
                my_k = [k for k in range(nk) if pieces[k][5] == half]
                other_k = [k for k in range(nk) if pieces[k][5] != half]
                for k in my_k:
                    if stage == 0:
                        for j in range(2):
                            ici(k, j, mine).start()
                    elif stage == 1:
                        for j in (half, 1 - half):
                            ici(k, j, chip_of[j]).wait_recv()
                            if j == half:
                                relay(k, half).start()
                            fwd(k, j).start()
                    elif stage == 2:
                        ici(k, 2, chip_of[2]).wait_recv()
                        fwd(k, 2).start()
                    else:
                        for j in range(2):
                            ici(k, j, mine).wait_send()
                        relay(k, half).wait_send()
                        for j in range(3):
                            fwd(k, j).wait_send()
                if stage == 3:
                    for k in other_k:
                        for j in range(3):
                            fwd(k, j).wait_recv()


def _fwd_proj(x2d, meta, tabs, tabs_m, norm_g, w_head, q_norm_g, wq_p, kv_norm_g, wkv_p, w_out_shard, w_in_shard,
              nb, s, tm):
    nt = s // tm
    n = nb * nt
    n_steps = n + 1
    c_t, sa_t, sb_t = tabs
    cm_t, sam_t, sbm_t = tabs_m
    gat = _StagedGather(W_OUT_PIECES)
    gat_in = _StagedGather(W_IN_PIECES_1)
    gat_scratch = gat.vmem_scratch(w_out_shard.shape, (D_MODEL, D_MODEL))
    n_sems = len(gat_scratch)
    assert n_steps >= 3

    def body(x_ref, c_ref, sa_ref, sb_ref, mt_ref, cm_ref, sam_ref, sbm_ref,
             g_ref, w_ref, gq_ref, wq_ref, gkv_ref, wkv_ref, wos_ref, wis_ref,
             p_ref, q_ref, k_ref, v_ref, pm_ref, km_ref, vm_ref, wo_ref, wi_ref, *scr):
        gat_scr, gat_in_scr = scr[:n_sems], scr[n_sems:]
        i = pl.program_id(0)
        for stage, at in enumerate(_StagedGather.steps(n_steps)):
            @pl.when(i == at)
            def _(stage=stage):
                gat_in.run_vmem(stage, wis_ref, wi_ref, gat_in_scr)
                gat.run_vmem(stage, wos_ref, wo_ref, gat_scr)

        def project(xv, c, sa, sb, p_out, q_out, k_out, v_out):
            u, _ = _rms(xv, g_ref[...])
            p = _dot_nt(u.astype(BF16), w_ref[...])
            p_out[...] = p
            qn, _ = _rms(p[:, 0:Q_RANK], gq_ref[...])
            q = _dot_nt(qn.astype(BF16), wq_ref[...])
            kvn, _ = _rms(p[:, Q_RANK:Q_RANK + KV_RANK], gkv_ref[...])
            kv = _dot_nt(kvn.astype(BF16), wkv_ref[...])
            kpe = _rope(p[:, 384:512], c, sa, sb)
            for h in range(HEADS):
                if q_out is not None:
                    pe = _rope(q[:, QK_PAD * h + NOPE:QK_PAD * (h + 1)], c, sa, sb)
                    qh = jnp.concatenate([q[:, QK_PAD * h:QK_PAD * h + NOPE], pe], axis=1)
                    q_out[0, h] = (qh * ATTN_SCALE).astype(BF16)
                k_out[0, h] = jnp.concatenate([kv[:, NOPE * h:NOPE * (h + 1)], kpe], axis=1).astype(BF16)
                v_out[0, h] = kv[:, 512 + VDIM * h:512 + VDIM * (h + 1)].astype(BF16)

        @pl.when(i < n)
        def _():
            project(x_ref[...], c_ref[...], sa_ref[...], sb_ref[...], p_ref, q_ref, k_ref, v_ref)

        @pl.when(i == n)
        def _():
            project(mt_ref[...], cm_ref[...], sam_ref[...], sbm_ref[...], pm_ref, None, km_ref, vm_ref)
            gat_in.run_vmem(gat_in.STAGES, wis_ref, wi_ref, gat_in_scr)
            gat.run_vmem(gat.STAGES, wos_ref, wo_ref, gat_scr)

    cl = lambda i: jnp.minimum(i, n - 1)
    full = lambda a: pl.BlockSpec(a.shape, lambda i: (0,) * a.ndim)
    const = lambda shape: pl.BlockSpec(shape, lambda i: (0,) * len(shape))
    tab = pl.BlockSpec((tm, 128), lambda i: (cl(i) % nt, 0))
    hb = lambda w: pl.BlockSpec((1, HEADS, tm, w), lambda i: (cl(i) // nt, 0, cl(i) % nt, 0))
    whole = pl.BlockSpec(memory_space=pl.ANY)
    return pl.pallas_call(
        body, name="fwd_proj", grid=(n_steps,),
        in_specs=[pl.BlockSpec((tm, D_MODEL), lambda i: (cl(i), 0)), tab, tab, tab,
                  full(meta), full(cm_t), full(sam_t), full(sbm_t),
                  full(norm_g), full(w_head), full(q_norm_g), full(wq_p), full(kv_norm_g), full(wkv_p), whole, whole],
        out_specs=[pl.BlockSpec((tm, IN_HEAD), lambda i: (cl(i), 0)), hb(QK_PAD), hb(QK_PAD), hb(VDIM),
                   const((N_META, IN_HEAD)), const((1, HEADS, N_META, QK_PAD)), const((1, HEADS, N_META, VDIM)),
                   whole, whole],
        out_shape=[jax.ShapeDtypeStruct((nb * s, IN_HEAD), F32),
                   jax.ShapeDtypeStruct((nb, HEADS, s, QK_PAD), BF16),
                   jax.ShapeDtypeStruct((nb, HEADS, s, QK_PAD), BF16),
                   jax.ShapeDtypeStruct((nb, HEADS, s, VDIM), BF16),
                   jax.ShapeDtypeStruct((N_META, IN_HEAD), F32),
                   jax.ShapeDtypeStruct((1, HEADS, N_META, QK_PAD), BF16),
                   jax.ShapeDtypeStruct((1, HEADS, N_META, VDIM), BF16),
                   jax.ShapeDtypeStruct((D_MODEL, D_MODEL), BF16),
                   jax.ShapeDtypeStruct((IN_PAD, D_MODEL), BF16)],
        scratch_shapes=gat_scratch + gat_in.vmem_scratch(w_in_shard.shape, (IN_PAD, D_MODEL)),
        compiler_params=_cparams("arbitrary"),
    )(x2d, c_t, sa_t, sb_t, meta, cm_t, sam_t, sbm_t, norm_g, w_head, q_norm_g, wq_p, kv_norm_g, wkv_p, w_out_shard,
      w_in_shard)


def _attn_fwd(q, k, v, km, vm, w_in_shard, w_in_part, nb, s, tq):
    nq = s // tq
    n_steps = nb * HEADS
    gat = _StagedGather(W_IN_PIECES_2, zero_rows=(HEAD_ROWS, IN_HEAD - HEAD_ROWS))
    assert n_steps >= 3

    def body(q_ref, k_ref, v_ref, km_ref, vm_ref, ws_ref, _, o_ref, lse_ref, w_ref, s_scr, p_scr, *gat_scr):
        step = pl.program_id(0) * HEADS + pl.program_id(1)
        for stage, at in enumerate(_StagedGather.steps(n_steps)):
            @pl.when(step == at)
            def _(stage=stage):
                gat.run_vmem(stage, ws_ref, w_ref, gat_scr)

        row = lax.broadcasted_iota(jnp.int32, (tq, tq), 0)
        col = lax.broadcasted_iota(jnp.int32, (tq, tq), 1)
        def scores(i):
            slot = i % 2
            qi = q_ref[0, 0, i * tq:(i + 1) * tq, :]
            sm = _dot_nt(qi, km_ref[0, 0])
            m128 = None
            for j in range(i + 1):
                sc = _dot_nt(qi, k_ref[0, 0, j * tq:(j + 1) * tq, :])
                if j == i:
                    sc = jnp.where(col <= row, sc, NEG_INF)
                s_scr[slot, :, j * tq:(j + 1) * tq] = sc
                mx = sc[:, 0:128]
                for c0 in range(128, tq, 128):
                    mx = jnp.maximum(mx, sc[:, c0:c0 + 128])
                m128 = mx if m128 is None else jnp.maximum(m128, mx)
            return sm, jnp.maximum(jnp.max(m128, axis=1, keepdims=True), jnp.max(sm, axis=1, keepdims=True))

        def weighted_sum(i, pm, l):
            n = (i + 1) * tq
            acc = _dot(p_scr[i % 2, :, 0:n], v_ref[0, 0, 0:n, :]) + _dot(pm.astype(BF16), vm_ref[0, 0])
            o_ref[0, 0, i * tq:(i + 1) * tq, :] = acc / l

        nxt, pending = scores(0), None
        for i in range(nq):
            slot = i % 2
            sm, m = nxt
            if i + 1 < nq:
                nxt = scores(i + 1)
            pm = jnp.exp(sm - m)
            l128 = None
            for j in range(i + 1):
                p = jnp.exp(s_scr[slot, :, j * tq:(j + 1) * tq] - m)
                p_scr[slot, :, j * tq:(j + 1) * tq] = p.astype(BF16)
                ps = p[:, 0:128]
                for c0 in range(128, tq, 128):
                    ps = ps + p[:, c0:c0 + 128]
                l128 = ps if l128 is None else l128 + ps
            l = jnp.sum(l128, axis=1, keepdims=True) + jnp.sum(pm, axis=1, keepdims=True)
            lse_ref[0, 0, :, i * tq:(i + 1) * tq] = _row_of(m + jnp.log(l), tq)
            if pending is not None:
                weighted_sum(*pending)
            pending = (i, pm, l)
        weighted_sum(*pending)

        @pl.when(step == n_steps - 1)
        def _():
            gat.run_vmem(gat.STAGES, ws_ref, w_ref, gat_scr)

    hblk = lambda w: pl.BlockSpec((1, 1, s, w), lambda b, h: (b, h, 0, 0))
    mblk = lambda w: pl.BlockSpec((1, 1, N_META, w), lambda b, h: (0, h, 0, 0))
    whole = pl.BlockSpec(memory_space=pl.ANY)
    return pl.pallas_call(
        body, name="attn_fwd", grid=(nb, HEADS),
        in_specs=[hblk(QK_PAD), hblk(QK_PAD), hblk(VDIM), mblk(QK_PAD), mblk(VDIM), whole, whole],
        out_specs=[hblk(VDIM), pl.BlockSpec((1, 1, 1, s), lambda b, h: (b, h, 0, 0)), whole],
        out_shape=[jax.ShapeDtypeStruct((nb, HEADS, s, VDIM), F32),
                   jax.ShapeDtypeStruct((nb, HEADS, 1, s), F32),
                   jax.ShapeDtypeStruct(w_in_part.shape, BF16)],
        input_output_aliases={6: 2},
        scratch_shapes=[pltpu.VMEM((2, tq, s), F32), pltpu.VMEM((2, tq, s), BF16)]
        + gat.vmem_scratch(w_in_shard.shape, w_in_part.shape),
        compiler_params=_cparams("arbitrary", "arbitrary"),
    )(q, k, v, km, vm, w_in_shard, w_in_part)


def _shift_rows(a, prev, n_rows):
    rid = lax.broadcasted_iota(jnp.int32, a.shape, 0)
    a1 = jnp.where(rid == 0, prev[7:8, :], pltpu.roll(a, 1, 0))
    a2 = jnp.where(rid == 0, prev[6:7, :], jnp.where(rid == 1, prev[7:8, :], pltpu.roll(a, 2, 0)))
    return a1, a2


def _attn_gate(o, za, ga_h):
    on, r = _rms(o, ga_h)
    return on * (za * _sigmoid(za)), on, r


def _out_fwd_bwd(x2d, tgt2d, o, meta, norm_g, w_in_p, conv_w, ga, gc, gmat, w_out, gf, nb, s, tm):
    nt = s // tm
    r = nb * s

    def body(x_ref, t_ref, o_ref, mt_ref, g_ref, wi_ref, cw_ref, ga_ref, gc_ref, gm_ref, w_ref, gf_ref,
             dh_ref, dy_ref, dw_ref, dgf_ref, loss_ref, p_ref, pm_ref, last_cc):
        i = pl.program_id(0)
        blk = lambda ref, j, rows=slice(None): ref[rows, 512 * j:512 * (j + 1)]

        def tail(xv):
            u, _ = _rms(xv, g_ref[...])
            return _dot_nt(u.astype(BF16), wi_ref[IN_HEAD:IN_PAD, :])

        @pl.when(i == 0)
        def _():
            dw_ref[...] = jnp.zeros_like(dw_ref)
            dgf_ref[...] = jnp.zeros_like(dgf_ref)
            loss_ref[...] = jnp.zeros_like(loss_ref)
            last_cc[...] = jnp.zeros_like(last_cc)
            pm_ref[...] = tail(mt_ref[...])

        u16 = _rms(x_ref[...], g_ref[...])[0].astype(BF16)

        def project(j):
            p_ref[:, 512 * j:512 * (j + 1)] = _dot_nt(u16, wi_ref[IN_HEAD + 512 * j:IN_HEAD + 512 * (j + 1), :])

        project(BLK_ZA)
        project(BLK_CC)
        project(BLK_CH)
        ya = []
        for h in range(HEADS):
            y, _, _ = _attn_gate(o_ref[0, h], p_ref[:, 512 * BLK_ZA + VDIM * h:512 * BLK_ZA + VDIM * (h + 1)],
                                 ga_ref[:, VDIM * h:VDIM * (h + 1)])
            ya.append(y)
        project(BLK_CB)
        project(BLK_ZC)
        cc = blk(p_ref, BLK_CC) * blk(p_ref, BLK_CH)
        meta_cc = blk(pm_ref, BLK_CC, slice(8, 16)) * blk(pm_ref, BLK_CH, slice(8, 16))
        prev = jnp.where(i % nt == 0, meta_cc, last_cc[...])
        last_cc[...] = cc[tm - 8:tm, :]
        cc1, cc2 = _shift_rows(cc, prev, tm)
        yc = blk(p_ref, BLK_CB) * (cw_ref[0:1, :] * cc2 + cw_ref[1:2, :] * cc1 + cw_ref[2:3, :] * cc)
        rg = lax.rsqrt(_group_mean(yc * yc, gm_ref[...]) + EPS)
        zc = blk(p_ref, BLK_ZC)
        yconv = yc * rg * gc_ref[...] * (zc * _sigmoid(zc))
        ycat = jnp.concatenate(ya + [yconv], axis=1).astype(BF16)
        h2 = x_ref[...] + _dot(ycat, w_ref[...])
        gfv = gf_ref[...]
        y, r2 = _rms(h2, gfv)
        e = y - t_ref[...]
        loss_ref[...] += 0.5 * jnp.sum(e * e) / D_MODEL
        dyv = e * (1.0 / D_MODEL)
        dh2, dgf = _rms_bwd(dyv, h2, r2, gfv)
        dgf_ref[...] += jnp.sum(dgf, axis=0, keepdims=True)
        dh_ref[...] = dh2
        dhb = dh2.astype(BF16)
        dy_ref[...] = _dot_nt(dhb, w_ref[...])
        dw_ref[...] += _dot_tn(ycat, dhb)

    row = lambda w: pl.BlockSpec((tm, w), lambda i: (i, 0))
    const = lambda shape: pl.BlockSpec(shape, lambda i: (0,) * len(shape))
    full = lambda a: const(a.shape)
    return pl.pallas_call(
        body, name="out_fwd_bwd", grid=(nb * nt,),
        in_specs=[row(D_MODEL), row(D_MODEL),
                  pl.BlockSpec((1, HEADS, tm, VDIM), lambda i: (i // nt, 0, i % nt, 0)),
                  full(meta), full(norm_g), full(w_in_p),
                  full(conv_w), full(ga), full(gc), full(gmat), full(w_out), full(gf)],
        out_specs=[row(D_MODEL), row(D_MODEL), const((D_MODEL, D_MODEL)), const((1, D_MODEL)), const((1, 128)),
                   row(IN_TAIL), const((N_META, IN_TAIL))],
        out_shape=[jax.ShapeDtypeStruct((r, D_MODEL), F32), jax.ShapeDtypeStruct((r, D_MODEL), F32),
                   jax.ShapeDtypeStruct((D_MODEL, D_MODEL), F32), jax.ShapeDtypeStruct((1, D_MODEL), F32),
                   jax.ShapeDtypeStruct((1, 128), F32),
                   jax.ShapeDtypeStruct((r, IN_TAIL), F32), jax.ShapeDtypeStruct((N_META, IN_TAIL), F32)],
        scratch_shapes=[pltpu.VMEM((8, 512), F32)],
        compiler_params=_cparams("arbitrary"),
    )(x2d, tgt2d, o, meta, norm_g, w_in_p, conv_w, ga, gc, gmat, w_out, gf)


def _gate_bwd(dycat, o, p, pm, conv_w, ga, gc, gmat, nb, s, tm):
    nt = s // tm
    r = nb * s
    ext = tm + 8
    prev_idx = lambda i: jnp.maximum(i * (tm // 8) - 1, 0)
    next_idx = lambda i: jnp.minimum((i + 1) * (tm // 8), r // 8 - 1)

    def body(dya_ref, dyc_ref, dycn_ref, o_ref, za_ref, cb_ref, cbn_ref, cc_ref, ccp_ref, ccn_ref,
             ch_ref, chp_ref, chn_ref, zc_ref, zcn_ref, mc_ref, mh_ref, cw_ref, ga_ref, gc_ref, gm_ref,
             dpb_ref, do_ref, dl_ref, dccm_ref, dga_ref, dgc_ref, dcw_ref):
        i = pl.program_id(0)

        @pl.when(i == 0)
        def _():
            dga_ref[...] = jnp.zeros_like(dga_ref)
            dgc_ref[...] = jnp.zeros_like(dgc_ref)
            dcw_ref[...] = jnp.zeros_like(dcw_ref)

        dga = []
        for h in range(HEADS):
            hs = slice(VDIM * h, VDIM * (h + 1))
            oh, za, gah, dya = o_ref[0, h], za_ref[:, hs], ga_ref[:, hs], dya_ref[:, hs]
            sg = _sigmoid(za)
            on, ro = _rms(oh, gah)
            don = dya * (za * sg)
            dpb_ref[:, hs] = (dya * on * (sg * (1.0 + za * (1.0 - sg)))).astype(BF16)
            do, dg = _rms_bwd(don, oh, ro, gah)
            dga.append(jnp.sum(dg, axis=0, keepdims=True))
            dob = do.astype(BF16)
            do_ref[0, h] = dob
            dl_ref[0, h] = _row_of(jnp.sum(dob.astype(F32) * oh, axis=1, keepdims=True), tm)
        dga_ref[...] += jnp.concatenate(dga, axis=1)

        cat = lambda a, b: jnp.concatenate([a[...], b[...]], axis=0)
        cch = cat(cc_ref, ccn_ref)
        chh = cat(ch_ref, chn_ref)
        cb = cat(cb_ref, cbn_ref)
        zc = cat(zc_ref, zcn_ref)
        dy = cat(dyc_ref, dycn_ref)
        first = i % nt == 0
        last = i % nt == nt - 1
        cc = cch * chh
        prev = jnp.where(first, mc_ref[8:16, :] * mh_ref[8:16, :], ccp_ref[...] * chp_ref[...])
        cc1, cc2 = _shift_rows(cc, prev, ext)
        w0, w1, w2 = cw_ref[0:1, :], cw_ref[1:2, :], cw_ref[2:3, :]
        dw = w0 * cc2 + w1 * cc1 + w2 * cc
        yc = cb * dw
        rg = lax.rsqrt(_group_mean(yc * yc, gm_ref[...]) + EPS)
        ych = yc * rg
        gcv = gc_ref[...]
        sg = _sigmoid(zc)
        dycn = dy * (zc * sg)
        dzc = dy * (ych * gcv) * (sg * (1.0 + zc * (1.0 - sg)))
        dgc_ref[...] += jnp.sum((dycn * ych)[:tm], axis=0, keepdims=True)
        dycg = dycn * gcv
        dyc = rg * (dycg - ych * _group_mean(dycg * ych, gm_ref[...]))
        rid = lax.broadcasted_iota(jnp.int32, (ext, CONV_W), 0)
        ddw = jnp.where(jnp.logical_and(last, rid >= tm), 0.0, dyc * cb)
        dcb = dyc * dw
        dcc = w2 * ddw + w1 * pltpu.roll(ddw, ext - 1, 0) + w0 * pltpu.roll(ddw, ext - 2, 0)
        dpb_ref[:, 512:1024] = dcb[:tm].astype(BF16)
        dpb_ref[:, 1024:1536] = (dcc * chh)[:tm].astype(BF16)
        dpb_ref[:, 1536:2048] = (dcc * cch)[:tm].astype(BF16)
        dpb_ref[:, 2048:2560] = dzc[:tm].astype(BF16)
        rs = lambda a: jnp.sum(a[:tm], axis=0, keepdims=True)
        dcw_ref[0:1, :] += rs(ddw * cc2)
        dcw_ref[1:2, :] += rs(ddw * cc1)
        dcw_ref[2:3, :] += rs(ddw * cc)

        @pl.when(first)
        def _():
            d0, d1 = ddw[0:1, :], ddw[1:2, :]
            r8 = lax.broadcasted_iota(jnp.int32, (8, CONV_W), 0)
            dccm_ref[0] = jnp.where(r8 == 7, w1 * d0 + w0 * d1, jnp.where(r8 == 6, w0 * d0, 0.0))

    row = lambda j: pl.BlockSpec((tm, 512), lambda i: (i, j))
    prv = lambda j: pl.BlockSpec((8, 512), lambda i: (prev_idx(i), j))
    nxt = lambda j: pl.BlockSpec((8, 512), lambda i: (next_idx(i), j))
    mblk = lambda j: pl.BlockSpec((N_META, 512), lambda i: (0, j))
    full = lambda a: pl.BlockSpec(a.shape, lambda i: (0,) * a.ndim)
    hb = lambda w: pl.BlockSpec((1, HEADS, tm, w), lambda i: (i // nt, 0, i % nt, 0))
    acc = lambda rr: pl.BlockSpec((rr, 512), lambda i: (0, 0))
    return pl.pallas_call(
        body, name="gate_bwd", grid=(nb * nt,),
        in_specs=[row(0), row(1), nxt(1), hb(VDIM),
                  row(BLK_ZA), row(BLK_CB), nxt(BLK_CB), row(BLK_CC), prv(BLK_CC), nxt(BLK_CC),
                  row(BLK_CH), prv(BLK_CH), nxt(BLK_CH), row(BLK_ZC), nxt(BLK_ZC),
                  mblk(BLK_CC), mblk(BLK_CH), full(conv_w), full(ga), full(gc), full(gmat)],
        out_specs=[pl.BlockSpec((tm, 2560), lambda i: (i, 0)), hb(VDIM),
                   pl.BlockSpec((1, HEADS, 1, tm), lambda i: (i // nt, 0, 0, i % nt)),
                   pl.BlockSpec((1, 8, 512), lambda i: (i // nt, 0, 0)),
                   acc(1), acc(1), acc(8)],
        out_shape=[jax.ShapeDtypeStruct((r, 2560), BF16), jax.ShapeDtypeStruct((nb, HEADS, s, VDIM), BF16),
                   jax.ShapeDtypeStruct((nb, HEADS, 1, s), F32), jax.ShapeDtypeStruct((nb, 8, 512), F32),
                   jax.ShapeDtypeStruct((1, 512), F32), jax.ShapeDtypeStruct((1, 512), F32),
                   jax.ShapeDtypeStruct((8, 512), F32)],
        compiler_params=_cparams("arbitrary"),
    )(dycat, dycat, dycat, o, p, p, p, p, p, p, p, p, p, p, p, pm, pm, conv_w, ga, gc, gmat)


class _StagedReduce:
    LOC, PRE_S, PRE_R, ICI_S, ICI_R, POST_S, POST_R, OUT, N_SEM = 0, 1, 2, 3, 6, 9, 10, 11, 12

    def __init__(self, shard_shape):
        self.half = (shard_shape[0] // 2, shard_shape[1])

    def scratch(self):
        h = self.half
        return [pltpu.VMEM((4,) + h, F32), pltpu.VMEM((4,) + h, F32), pltpu.VMEM((4,) + h, BF16),
                pltpu.VMEM((3,) + h, BF16), pltpu.VMEM(h, F32), pltpu.SemaphoreType.DMA((self.N_SEM,))]

    def run(self, stage, pin, gout, scr):
        own, sib, wire, rbuf, fin, sems = scr
        r2 = self.half[0]
        x, y, c = lax.axis_index("x"), lax.axis_index("y"), lax.axis_index("c")
        mine = 2 * x + y
        sibling = (x, y, 1 - c)
        chips = [(1 - x, y), (x, 1 - y), (1 - x, 1 - y)]
        rows = lambda half: pl.ds(pl.multiple_of(half * r2, r2), r2)
        mesh = pl.DeviceIdType.MESH

        loc = pltpu.make_async_copy(pin.at[:, rows(c), :], own, sems.at[self.LOC])
        pre = pltpu.make_async_remote_copy(
            src_ref=pin.at[:, rows(1 - c), :], dst_ref=sib, send_sem=sems.at[self.PRE_S],
            recv_sem=sems.at[self.PRE_R], device_id=sibling, device_id_type=mesh)

        def ici(j):
            px, py = chips[j]
            return pltpu.make_async_remote_copy(
                src_ref=wire.at[2 * px + py], dst_ref=rbuf.at[j], send_sem=sems.at[self.ICI_S + j],
                recv_sem=sems.at[self.ICI_R + j], device_id=(px, py, c), device_id_type=mesh)

        def post(half):
            return pltpu.make_async_remote_copy(
                src_ref=fin, dst_ref=gout.at[rows(half), :], send_sem=sems.at[self.POST_S],
                recv_sem=sems.at[self.POST_R], device_id=sibling, device_id_type=mesh)

        keep = pltpu.make_async_copy(fin, gout.at[rows(c), :], sems.at[self.OUT])
        if stage == 0:
            loc.start()
            pre.start()
        elif stage == 1:
            loc.wait()
            pre.wait_recv()
            for blk in range(4):
                tot = own[blk] + sib[blk]
                own[blk] = tot
                wire[blk] = tot.astype(BF16)
            for j in range(3):
                ici(j).start()
        elif stage == 2:
            for j in range(3):
                ici(j).wait_recv()
            tot = own[mine]
            for j in range(3):
                tot = tot + rbuf[j].astype(F32)
            fin[...] = tot
            post(c).start()
            keep.start()
        else:
            post(1 - c).wait_recv()
            pre.wait_send()
            for j in range(3):
                ici(j).wait_send()
            post(c).wait_send()
            keep.wait()


def _attn_bwd(q, k, v, do, lse, delta, km, vm, early, nb, s, t):
    n = s // t
    ne = len(early)
    reds = [_StagedReduce(a.shape[1:]) for a in early]
    n_steps = HEADS * nb
    assert n_steps >= 4

    def body(q_ref, k_ref, v_ref, do_ref, lse_ref, dl_ref, km_ref, vm_ref, *rest):
        pin_refs, rest = rest[:ne], rest[ne:]
        dq_ref, dk_ref, dv_ref, dkm_ref, dvm_ref = rest[:5]
        gout_refs, (p_scr, ds_scr, dq_acc), red_scr = rest[5:5 + ne], rest[5 + ne:8 + ne], rest[8 + ne:]
        b = pl.program_id(1)
        step = pl.program_id(0) * nb + b
        for stage, at in enumerate((0, 1, n_steps - 2, n_steps - 1)):
            @pl.when(step == at)
            def _(stage=stage):
                for a, red in enumerate(reds):
                    red.run(stage, pin_refs[a], gout_refs[a], red_scr[6 * a:6 * a + 6])

        @pl.when(b == 0)
        def _():
            dkm_ref[...] = jnp.zeros_like(dkm_ref)
            dvm_ref[...] = jnp.zeros_like(dvm_ref)

        kr = lax.broadcasted_iota(jnp.int32, (t, t), 0)
        qc = lax.broadcasted_iota(jnp.int32, (t, t), 1)
        km_v, vm_v = km_ref[0, 0], vm_ref[0, 0]
        ptm = jnp.exp(_dot_nt(km_v, q_ref[0, 0]) - lse_ref[0, 0])
        dstm = (ptm * (_dot_nt(vm_v, do_ref[0, 0]) - dl_ref[0, 0])).astype(BF16)
        dkm_ref[0] += _dot(dstm, q_ref[0, 0])
        dvm_ref[0] += _dot(ptm.astype(BF16), do_ref[0, 0])
        dq_acc[...] = _dot_tn(dstm, km_v)
        def tiles(j):
            slot = j % 2
            kj = k_ref[0, 0, j * t:(j + 1) * t, :]
            vj = v_ref[0, 0, j * t:(j + 1) * t, :]
            def products(i):
                cs = slice(i * t, (i + 1) * t)
                return _dot_nt(kj, q_ref[0, 0, cs, :]), _dot_nt(vj, do_ref[0, 0, cs, :])

            nxt, pending = products(j), None
            for i in range(j, n):
                cs = slice(i * t, (i + 1) * t)
                st, dpt = nxt
                if i + 1 < n:
                    nxt = products(i + 1)
                if i == j:
                    st = jnp.where(kr <= qc, st, NEG_INF)
                pt = jnp.exp(st - lse_ref[0, 0, :, cs])
                dst = (pt * (dpt - dl_ref[0, 0, :, cs])).astype(BF16)
                p_scr[slot, :, cs] = pt.astype(BF16)
                ds_scr[slot, :, cs] = dst
                if pending is not None:
                    dq_acc[pending[0], :] += _dot_tn(pending[1], kj)
                pending = (cs, dst)
            dq_acc[pending[0], :] += _dot_tn(pending[1], kj)

        for j in range(n):
            slot = j % 2
            tiles(j)
            dv_ref[0, 0, j * t:(j + 1) * t, :] = _dot(p_scr[slot, :, j * t:s], do_ref[0, 0, j * t:s, :]).astype(BF16)
            dk_ref[0, 0, j * t:(j + 1) * t, :] = _dot(ds_scr[slot, :, j * t:s], q_ref[0, 0, j * t:s, :]).astype(BF16)
        dq_ref[0, 0] = dq_acc[...].astype(BF16)

    big = lambda w: pl.BlockSpec((1, 1, s, w), lambda h, b: (b, h, 0, 0))
    rowv = pl.BlockSpec((1, 1, 1, s), lambda h, b: (b, h, 0, 0))
    mk = lambda w: pl.BlockSpec((1, 1, N_META, w), lambda h, b: (0, h, 0, 0))
    mo = lambda w: pl.BlockSpec((1, N_META, w), lambda h, b: (h, 0, 0))
    return pl.pallas_call(
        body, name="attn_bwd", grid=(HEADS, nb),
        in_specs=[big(QK_PAD), big(QK_PAD), big(VDIM), big(VDIM), rowv, rowv, mk(QK_PAD), mk(VDIM)]
        + [pl.BlockSpec(memory_space=pl.ANY)] * ne,
        out_specs=[big(QK_PAD), big(QK_PAD), big(VDIM), mo(QK_PAD), mo(VDIM)]
        + [pl.BlockSpec(memory_space=pl.ANY)] * ne,
        out_shape=[jax.ShapeDtypeStruct((nb, HEADS, s, QK_PAD), BF16),
                   jax.ShapeDtypeStruct((nb, HEADS, s, QK_PAD), BF16),
                   jax.ShapeDtypeStruct((nb, HEADS, s, VDIM), BF16),
                   jax.ShapeDtypeStruct((HEADS, N_META, QK_PAD), F32),
                   jax.ShapeDtypeStruct((HEADS, N_META, VDIM), F32)]
        + [jax.ShapeDtypeStruct(a.shape[1:], F32) for a in early],
        scratch_shapes=[pltpu.VMEM((2, t, s), BF16), pltpu.VMEM((2, t, s), BF16), pltpu.VMEM((s, QK_PAD), F32)]
        + [sc for red in reds for sc in red.scratch()],
        compiler_params=_cparams("arbitrary", "arbitrary"),
    )(q, k, v, do, lse, delta, km, vm, *early)


def _up_bwd(dq, dk, dv, dkm, dvm, p, pm, tabs, tabs_m, wq_p, wkv_p, gq, gkv, nb, s, tm):
    nt = s // tm
    n = nb * nt
    c_t, sa_t, sb_t = tabs
    cm_t, sam_t, sbm_t = tabs_m

    def kv_path(dkh, dvh, pa, c, sa, sb, wkv, gkvv):
        dkpe = dkh[0][:, NOPE:]
        for h in range(1, HEADS):
            dkpe = dkpe + dkh[h][:, NOPE:]
        dkr = _rope_bwd(dkpe, c, sa, sb)
        dkv = jnp.concatenate([d[:, :NOPE] for d in dkh] + list(dvh), axis=1).astype(BF16)
        ckv = pa[:, Q_RANK:Q_RANK + KV_RANK]
        kvn, rkv = _rms(ckv, gkvv)
        dckv, dg = _rms_bwd(_dot(dkv, wkv), ckv, rkv, gkvv)
        return dckv, dkr, kvn.astype(BF16), dkv, jnp.sum(dg, axis=0, keepdims=True)

    def body(dq_ref, dk_ref, dv_ref, pa_ref, c_ref, sa_ref, sb_ref,
             dkm_ref, dvm_ref, pam_ref, cm_ref, sam_ref, sbm_ref,
             wq_ref, wkv_ref, gq_ref, gkv_ref,
             dpa_ref, dpam_ref, pq_ref, pkv_ref, dgq_ref, dgkv_ref, dwq_ref, dwkv_ref):
        i = pl.program_id(0)

        @pl.when(i == 0)
        def _():
            dwq_ref[...] = jnp.zeros_like(dwq_ref)
            dwkv_ref[...] = jnp.zeros_like(dwkv_ref)
            dgq_ref[...] = jnp.zeros_like(dgq_ref)
            dgkv_ref[...] = jnp.zeros_like(dgkv_ref)

        @pl.when(i < n)
        def _():
            c, sa, sb = c_ref[...], sa_ref[...], sb_ref[...]
            pa = pa_ref[...]
            parts = []
            for h in range(HEADS):
                dqh = dq_ref[0, h].astype(F32) * ATTN_SCALE
                parts += [dqh[:, :NOPE], _rope_bwd(dqh[:, NOPE:], c, sa, sb)]
            dql = jnp.concatenate(parts, axis=1).astype(BF16)
            cq = pa[:, 0:Q_RANK]
            gqv = gq_ref[...]
            qn, rq = _rms(cq, gqv)
            dwq_ref[...] += _dot_tn(dql, qn.astype(BF16))
            dcq, dg = _rms_bwd(_dot(dql, wq_ref[...]), cq, rq, gqv)
            dgq_ref[...] += jnp.sum(dg, axis=0, keepdims=True)
            dckv, dkr, kvn, dkv, dgk = kv_path([dk_ref[0, h].astype(F32) for h in range(HEADS)],
                                               [dv_ref[0, h].astype(F32) for h in range(HEADS)],
                                               pa, c, sa, sb, wkv_ref[...], gkv_ref[...])
            dwkv_ref[...] += _dot_tn(dkv, kvn)
            dgkv_ref[...] += dgk
            dpa_ref[...] = jnp.concatenate([dcq, dckv, dkr], axis=1).astype(BF16)

        @pl.when(i == n)
        def _():
            dckv, dkr, kvn, dkv, dgk = kv_path([dkm_ref[h] for h in range(HEADS)],
                                               [dvm_ref[h] for h in range(HEADS)],
                                               pam_ref[...], cm_ref[...], sam_ref[...], sbm_ref[...],
                                               wkv_ref[...], gkv_ref[...])
            dwkv_ref[...] += _dot_tn(dkv, kvn)
            dgkv_ref[...] += dgk
            dpam_ref[...] = jnp.concatenate([jnp.zeros((N_META, Q_RANK), F32), dckv, dkr], axis=1)
            for h in range(HEADS):
                pq_ref[h] = dwq_ref[QK_PAD * h:QK_PAD * h + NOPE + ROPE, :]
                pkv_ref[h, 0:NOPE, :] = dwkv_ref[NOPE * h:NOPE * (h + 1), :]
                pkv_ref[h, NOPE:NOPE + VDIM, :] = dwkv_ref[512 + VDIM * h:512 + VDIM * (h + 1), :]

    cl = lambda i: jnp.minimum(i, n - 1)
    hb = lambda w: pl.BlockSpec((1, HEADS, tm, w), lambda i: (cl(i) // nt, 0, cl(i) % nt, 0))
    tab = pl.BlockSpec((tm, 128), lambda i: (cl(i) % nt, 0))
    full = lambda a: pl.BlockSpec(a.shape, lambda i: (0,) * a.ndim)
    const = lambda shape: pl.BlockSpec(shape, lambda i: (0,) * len(shape))
    return pl.pallas_call(
        body, name="up_bwd", grid=(n + 1,),
        in_specs=[hb(QK_PAD), hb(QK_PAD), hb(VDIM), pl.BlockSpec((tm, 512), lambda i: (cl(i), 0)), tab, tab, tab,
                  full(dkm), full(dvm), pl.BlockSpec((N_META, 512), lambda i: (0, 0)),
                  full(cm_t), full(sam_t), full(sbm_t), full(wq_p), full(wkv_p), full(gq), full(gkv)],
        out_specs=[pl.BlockSpec((tm, 512), lambda i: (cl(i), 0)), const((N_META, 512)),
                   const((HEADS, NOPE + ROPE, Q_RANK)), const((HEADS, NOPE + VDIM, KV_RANK)),
                   const((1, Q_RANK)), const((1, KV_RANK))],
        out_shape=[jax.ShapeDtypeStruct((nb * s, 512), BF16), jax.ShapeDtypeStruct((N_META, 512), F32),
                   jax.ShapeDtypeStruct((HEADS, NOPE + ROPE, Q_RANK), F32),
                   jax.ShapeDtypeStruct((HEADS, NOPE + VDIM, KV_RANK), F32),
                   jax.ShapeDtypeStruct((1, Q_RANK), F32), jax.ShapeDtypeStruct((1, KV_RANK), F32)],
        scratch_shapes=[pltpu.VMEM((HEADS * QK_PAD, Q_RANK), F32), pltpu.VMEM((1024, KV_RANK), F32)],
        compiler_params=_cparams("arbitrary"),
    )(dq, dk, dv, p, c_t, sa_t, sb_t, dkm, dvm, pm, cm_t, sam_t, sbm_t, wq_p, wkv_p, gq, gkv)


def _in_bwd(x2d, dh2, dpa, dpb, meta, dpam, dccm, pm, w_in_p, norm_g, nb, s, tm):
    nt = s // tm
    n = nb * nt

    def body(x_ref, dh_ref, dpa_ref, dpb_ref, mt_ref, dpam_ref, dccm_ref, mc_ref, mh_ref, w_ref, g_ref,
             gx_ref, gm_ref, dw_hbm, dg_ref, acc_ref, sems):
        i = pl.program_id(0)

        @pl.when(i == 0)
        def _():
            acc_ref[...] = jnp.zeros_like(acc_ref)
            dg_ref[...] = jnp.zeros_like(dg_ref)

        def rows(x, dp, dres):
            g = g_ref[...]
            dpb16 = dp.astype(BF16)
            du = _dot(dpb16, w_ref[...])
            u, r1 = _rms(x, g)
            acc_ref[...] += _dot_tn(dpb16, u.astype(BF16))
            dx, dg = _rms_bwd(du, x, r1, g)
            dg_ref[...] += jnp.sum(dg, axis=0, keepdims=True)
            return dx if dres is None else dx + dres

        @pl.when(i < n)
        def _():
            dp = jnp.concatenate([dpa_ref[...], dpb_ref[...]], axis=1)
            gx_ref[...] = rows(x_ref[...], dp, dh_ref[...])

        @pl.when(i == n)
        def _():
            dcc = dccm_ref[0]
            for b in range(1, nb):
                dcc = dcc + dccm_ref[b]
            z8 = jnp.zeros((8, CONV_W), F32)
            dc = jnp.concatenate([z8, dcc * mh_ref[8:16, :]], axis=0)
            dh = jnp.concatenate([z8, dcc * mc_ref[8:16, :]], axis=0)
            z = jnp.zeros((N_META, CONV_W), F32)
            dp = jnp.concatenate([dpam_ref[...], z, z, dc, dh, z], axis=1)
            gm_ref[...] = rows(mt_ref[...], dp, None)
            per = IN_DIM // 4
            cps = [pltpu.make_async_copy(acc_ref.at[0:448], dw_hbm.at[0, 0:448], sems.at[0]),
                   pltpu.make_async_copy(acc_ref.at[512:per + 64], dw_hbm.at[0, 448:per], sems.at[1])]
            for qq in range(1, 4):
                cps.append(pltpu.make_async_copy(acc_ref.at[per * qq + 64:per * (qq + 1) + 64], dw_hbm.at[qq],
                                                 sems.at[qq + 1]))
            for cp in cps:
                cp.start()
            for cp in cps:
                cp.wait()

    cl = lambda i: jnp.minimum(i, n - 1)
    row = lambda w: pl.BlockSpec((tm, w), lambda i: (cl(i), 0))
    full = lambda a: pl.BlockSpec(a.shape, lambda i: (0,) * a.ndim)
    mblk = lambda j: pl.BlockSpec((N_META, 512), lambda i: (0, j))
    return pl.pallas_call(
        body, name="in_bwd", grid=(n + 1,),
        in_specs=[row(D_MODEL), row(D_MODEL), row(512), row(2560), full(meta), full(dpam), full(dccm),
                  mblk(BLK_CC), mblk(BLK_CH), full(w_in_p), full(norm_g)],
        out_specs=[row(D_MODEL), pl.BlockSpec((N_META, D_MODEL), lambda i: (0, 0)),
                   pl.BlockSpec(memory_space=pl.ANY), pl.BlockSpec((1, D_MODEL), lambda i: (0, 0))],
        out_shape=[jax.ShapeDtypeStruct((nb * s, D_MODEL), F32), jax.ShapeDtypeStruct((N_META, D_MODEL), F32),
                   jax.ShapeDtypeStruct((4, IN_DIM // 4, D_MODEL), F32), jax.ShapeDtypeStruct((1, D_MODEL), F32)],
        scratch_shapes=[pltpu.VMEM((IN_PAD, D_MODEL), F32), pltpu.SemaphoreType.DMA((5,))],
        compiler_params=_cparams("arbitrary"),
    )(x2d, dh2, dpa, dpb, meta, dpam, dccm, pm, pm, w_in_p, norm_g)


def _gather_weights(w_in_shard, w_out_shard, split, pieces, out_rows, whole, zero_fills):
    ns, nw, nz = len(split), len(whole), len(zero_fills)
    flat = [(a, pc) for a in range(ns) for pc in pieces[a]]
    nk = len(flat)
    hh = HEAD_ROWS // 2
    assert hh % 16 == 0

    def body(*refs):
        ins, wins, zins = refs[2:2 + ns], refs[2 + ns:2 + ns + nw], refs[2 + ns + nw:2 + ns + nw + nz]
        n_in = 2 + ns + nw + nz
        head_ref, shard16, w_out16 = refs[n_in:n_in + 3]
        outs, wcat = refs[n_in + 3:n_in + 3 + ns], refs[n_in + 3 + ns:n_in + 3 + ns + nw]
        scr = refs[n_in + 3 + ns + nw:]
        stage, wouts = scr[:ns], scr[ns:ns + nw]
        (send_sems, recv_sems, fwd_send, fwd_recv, loc_sems, w_send, w_recv, w_loc, z_sems,
         h_send, h_recv, h_relay, h_pass) = scr[ns + nw:]
        x, y, c = lax.axis_index("x"), lax.axis_index("y"), lax.axis_index("c")
        mine = 2 * x + y
        chips = [(1 - x, y), (x, 1 - y), (1 - x, 1 - y)]
        chip_of = [2 * px + py for px, py in chips]
        shard16[...] = refs[0][...].astype(BF16)
        w_out16[...] = refs[1][...].astype(BF16)
        for a in range(ns):
            stage[a][...] = ins[a][...].astype(BF16)

        def head_rows(half):
            return pl.ds(pl.multiple_of(half * hh, 16), hh)

        def head_copy(turn):
            dest = (1 - c, c, c) if turn == 0 else (c, 1 - c, c)
            return pltpu.make_async_remote_copy(
                src_ref=shard16.at[head_rows(c)], dst_ref=head_ref.at[head_rows(c)], send_sem=h_send.at[turn],
                recv_sem=h_recv.at[0], device_id=dest, device_id_type=pl.DeviceIdType.MESH)

        def head_relay():
            ref = head_ref.at[head_rows(c)]
            return pltpu.make_async_remote_copy(
                src_ref=ref, dst_ref=ref, send_sem=h_relay.at[0], recv_sem=h_recv.at[0],
                device_id=(1, 1, c), device_id_type=pl.DeviceIdType.MESH)

        def head_pass(half):
            ref = head_ref.at[head_rows(half)]
            return pltpu.make_async_remote_copy(
                src_ref=ref, dst_ref=ref, send_sem=h_pass.at[0], recv_sem=h_pass.at[1],
                device_id=(x, y, 1 - c), device_id_type=pl.DeviceIdType.MESH)

        head_ref[HEAD_ROWS:IN_HEAD, :] = jnp.zeros((IN_HEAD - HEAD_ROWS, D_MODEL), BF16)

        @pl.when(mine == 0)
        def _():
            head_copy(0).start()
            head_ref[0:HEAD_ROWS, :] = shard16[0:HEAD_ROWS, :]

        def src(k):
            a, (s0, nr, _, _, _, _) = flat[k]
            return stage[a].at[s0:s0 + nr]

        def dst(k, q):
            a, (_, nr, per, first, rest, _) = flat[k]
            row = per * q + first + (rest - first) * jnp.minimum(q, 1)
            return outs[a].at[pl.ds(pl.multiple_of(row, 16), nr)]

        def ici(k, j, q):
            px, py = chips[j]
            return pltpu.make_async_remote_copy(
                src_ref=src(k), dst_ref=dst(k, q), send_sem=send_sems.at[k, j], recv_sem=recv_sems.at[k, j],
                device_id=(px, py, c), device_id_type=pl.DeviceIdType.MESH)

        def fwd(k, j):
            ref = dst(k, chip_of[j])
            return pltpu.make_async_remote_copy(
                src_ref=ref, dst_ref=ref, send_sem=fwd_send.at[k, j], recv_sem=fwd_recv.at[k, j],
                device_id=(x, y, 1 - c), device_id_type=pl.DeviceIdType.MESH)

        def wcopy(b, j, q):
            px, py = chips[j]
            return pltpu.make_async_remote_copy(
                src_ref=wins[b], dst_ref=wouts[b].at[q], send_sem=w_send.at[b, j], recv_sem=w_recv.at[b, j],
                device_id=(px, py, c), device_id_type=pl.DeviceIdType.MESH)

        local = [pltpu.make_async_copy(src(k), dst(k, mine), loc_sems.at[k]) for k in range(nk)]
        local += [pltpu.make_async_copy(wins[b], wouts[b].at[mine], w_loc.at[b]) for b in range(nw)]
        for z, (a, _, row0) in enumerate(zero_fills):
            local.append(pltpu.make_async_copy(zins[z], outs[a].at[row0:row0 + zins[z].shape[0]], z_sems.at[z]))
        wsends = [wcopy(b, j, mine) for b in range(nw) for j in range(3)]
        for cp in local + wsends:
            cp.start()

        for half in (0, 1):
            @pl.when(c == half)
            def _(half=half):
                my_k = [k for k in range(nk) if flat[k][1][5] == half]
                other_k = [k for k in range(nk) if flat[k][1][5] != half]
                sends = [ici(k, j, mine) for k in my_k for j in range(3)]
                for cp in sends:
                    cp.start()
                passed = []
                for k in my_k:
                    for j in range(3):
                        ici(k, j, chip_of[j]).wait_recv()
                        cp = fwd(k, j)
                        cp.start()
                        passed.append(cp)
                for k in other_k:
                    for j in range(3):
                        fwd(k, j).wait_recv()
                for cp in sends + passed:
                    cp.wait_send()

        for b in range(nw):
            for j in range(3):
                wcopy(b, j, chip_of[j]).wait_recv()
        for cp in wsends:
            cp.wait_send()
        for cp in local:
            cp.wait()
        for b in range(nw):
            cols = wins[b].shape[-1]
            for q in range(4):
                wcat[b][..., cols * q:cols * (q + 1)] = wouts[b][q]

        @pl.when(mine == 0)
        def _():
            head_copy(0).wait_send()
            head_copy(1).start()
            head_copy(1).wait_send()

        @pl.when(mine != 0)
        def _():
            hands_on = mine == 2 - c
            head_copy(0).wait_recv()

            @pl.when(hands_on)
            def _():
                head_relay().start()

            head_pass(c).start()
            head_pass(1 - c).wait_recv()
            head_pass(c).wait_send()

            @pl.when(hands_on)
            def _():
                head_relay().wait_send()

    vmem = pl.BlockSpec(memory_space=pltpu.VMEM)
    dma = pltpu.SemaphoreType.DMA
    zeros = [z for _, z, _ in zero_fills]
    return pl.pallas_call(
        body, name="gather_weights",
        in_specs=[vmem] * (2 + ns + nw + nz), out_specs=[vmem] * (3 + ns + nw),
        out_shape=([jax.ShapeDtypeStruct((IN_HEAD, D_MODEL), BF16), jax.ShapeDtypeStruct(w_in_shard.shape, BF16),
                    jax.ShapeDtypeStruct(w_out_shard.shape, BF16)]
                   + [jax.ShapeDtypeStruct((out_rows[a], split[a].shape[1]), BF16) for a in range(ns)]
                   + [jax.ShapeDtypeStruct(w.shape[:-1] + (4 * w.shape[-1],), w.dtype) for w in whole]),
        scratch_shapes=[pltpu.VMEM(a.shape, BF16) for a in split] + [pltpu.VMEM((4,) + w.shape, w.dtype) for w in whole]
        + [dma((nk, 3)), dma((nk, 3)), dma((nk, 3)), dma((nk, 3)), dma((nk,)),
           dma((nw, 3)), dma((nw, 3)), dma((nw,)), dma((nz,)),
           dma((2,)), dma((1,)), dma((1,)), dma((2,))],
        compiler_params=pltpu.CompilerParams(vmem_limit_bytes=VMEM_LIMIT),
    )(w_in_shard, w_out_shard, *split, *whole, *zeros)


def _reduce_grads(parts, small):
    n = len(parts)
    ns = len(small)
    shapes = [a.shape[1:] for a in parts]
    halves = [(sh[0] // 2, sh[1]) for sh in shapes]
    sm_blocks = [a.shape[1] // 128 for a, _ in small]
    sm_first = [sum(nr * nblk for (_, nr), nblk in zip(small[:k], sm_blocks[:k])) for k in range(ns)]
    sm_rows = -(-(sm_first[-1] + small[-1][1] * sm_blocks[-1]) // 8) * 8
    sm_shape = (sm_rows, 128)

    def body(*refs):
        pin, sm_in = refs[:n], refs[n:n + ns]
        gout, sm_out = refs[n + ns:2 * n + ns], refs[2 * n + ns]
        scr = refs[2 * n + ns + 1:]
        own, sib, wire, rbuf = scr[:n], scr[n:2 * n], scr[2 * n:3 * n], scr[3 * n:4 * n]
        (sbuf, send_sems, recv_sems, loc_sems, pre_send, pre_recv, post_send, post_recv,
         sm_send, sm_recv, sm_pack) = scr[4 * n:]
        x, y, c = lax.axis_index("x"), lax.axis_index("y"), lax.axis_index("c")
        mine = 2 * x + y
        sm_pack[...] = jnp.zeros(sm_shape, F32)
        for k, (_, nr) in enumerate(small):
            for i in range(nr):
                for j in range(sm_blocks[k]):
                    row = sm_first[k] + i * sm_blocks[k] + j
                    sm_pack[row:row + 1, :] = sm_in[k][i:i + 1, 128 * j:128 * (j + 1)]
        me = 4 * x + 2 * y + c
        sibling = (x, y, 1 - c)

        def rows(a, half):
            r2 = halves[a][0]
            return pl.ds(pl.multiple_of(half * r2, r2), r2)

        near = (jnp.where(c == 0, 1 - x, x), jnp.where(c == 0, y, 1 - y))
        far = (jnp.where(c == 0, x, 1 - x), jnp.where(c == 0, 1 - y, y))
        chip = lambda p: 2 * p[0] + p[1]
        blocks = [3 - mine, chip(near), chip(far), mine]

        def pre(a, k):
            q = blocks[k]
            return pltpu.make_async_remote_copy(
                src_ref=pin[a].at[q, rows(a, 1 - c), :], dst_ref=sib[a].at[q],
                send_sem=pre_send.at[a, q], recv_sem=pre_recv.at[a, q], device_id=sibling,
                device_id_type=pl.DeviceIdType.MESH)

        def ici(a, m):
            px, py = near if m < 2 else far
            return pltpu.make_async_remote_copy(
                src_ref=wire[a].at[blocks[m]], dst_ref=rbuf[a].at[m], send_sem=send_sems.at[a, m],
                recv_sem=recv_sems.at[a, m], device_id=(px, py, c), device_id_type=pl.DeviceIdType.MESH)

        def post(a, half):
            ref = gout[a].at[rows(a, half), :]
            return pltpu.make_async_remote_copy(
                src_ref=ref, dst_ref=ref, send_sem=post_send.at[a], recv_sem=post_recv.at[a],
                device_id=sibling, device_id_type=pl.DeviceIdType.MESH)

        def small_copy(kk):
            peer = (x ^ (kk >> 2), y ^ ((kk >> 1) & 1), c ^ (kk & 1))
            return pltpu.make_async_remote_copy(
                src_ref=sm_pack, dst_ref=sbuf.at[kk], send_sem=sm_send.at[kk - 1], recv_sem=sm_recv.at[kk - 1],
                device_id=peer, device_id_type=pl.DeviceIdType.MESH)

        local = [[pltpu.make_async_copy(pin[a].at[blocks[k], rows(a, c), :], own[a].at[blocks[k]], loc_sems.at[a, k])
                  for k in range(4)] for a in range(n)]
        pres = [[pre(a, k) for k in range(4)] for a in range(n)]
        smalls = [small_copy(kk) for kk in range(1, 8)]
        for a in range(n):
            for k in range(4):
                local[a][k].start()
                pres[a][k].start()
        for cp in smalls:
            cp.start()
        sbuf[0] = sm_pack[...]
        sends = []
        for a in range(n):
            for k in range(4):
                local[a][k].wait()
                pres[a][k].wait_recv()
                tot = own[a][blocks[k]] + sib[a][blocks[k]]
                if k == 2:
                    ici(a, 0).wait_recv()
                    tot = tot + rbuf[a][0].astype(F32)
                own[a][blocks[k]] = tot
                if k < 3:
                    wire[a][blocks[k]] = tot.astype(BF16)
                    cp = ici(a, k)
                    cp.start()
                    sends.append(cp)
        for cp in smalls:
            cp.wait_recv()
        total = sbuf[me]
        for d in range(1, 8):
            total = total + sbuf[me ^ d]
        sm_out[...] = total
        posts = []
        for a in range(n):
            fin = own[a][mine]
            for m in (1, 2):
                ici(a, m).wait_recv()
                fin = fin + rbuf[a][m].astype(F32)
            gout[a][rows(a, c), :] = fin
            cp = post(a, c)
            cp.start()
            posts.append(cp)
        for a in range(n):
            post(a, 1 - c).wait_recv()
        for cp in [cp for row in pres for cp in row] + sends + smalls + posts:
            cp.wait_send()

    vmem = pl.BlockSpec(memory_space=pltpu.VMEM)
    dma = pltpu.SemaphoreType.DMA
    return pl.pallas_call(
        body, name="reduce_grads",
        in_specs=[pl.BlockSpec(memory_space=pl.ANY)] * n + [vmem] * ns, out_specs=[vmem] * (n + 1),
        out_shape=[jax.ShapeDtypeStruct(sh, F32) for sh in shapes] + [jax.ShapeDtypeStruct(sm_shape, F32)],
        scratch_shapes=([pltpu.VMEM((4,) + hs, F32) for hs in halves] + [pltpu.VMEM((4,) + hs, F32) for hs in halves]
                        + [pltpu.VMEM((4,) + hs, BF16) for hs in halves]
                        + [pltpu.VMEM((3,) + hs, BF16) for hs in halves]
                        + [pltpu.VMEM((8,) + sm_shape, F32), dma((n, 3)), dma((n, 3)), dma((n, 4)),
                           dma((n, 4)), dma((n, 4)), dma((n,)), dma((n,)), dma((7,)), dma((7,)),
                           pltpu.VMEM(sm_shape, F32)]),
        compiler_params=pltpu.CompilerParams(vmem_limit_bytes=VMEM_LIMIT),
    )(*parts, *[a for a, _ in small])


def _adamw_update(w_ref, g_ref, m_ref, v_ref, d_ref, nm_ref, nv_ref):
    gv = g_ref[...]
    nm = ADAM_B1 * m_ref[...] + (1.0 - ADAM_B1) * gv
    nv = ADAM_B2 * v_ref[...] + (1.0 - ADAM_B2) * (gv * gv)
    m_hat = nm / (1.0 - ADAM_B1 ** ADAM_STEP)
    v_hat = nv / (1.0 - ADAM_B2 ** ADAM_STEP)
    d_ref[...] = -ADAM_LR * (m_hat / (jnp.sqrt(v_hat) + ADAM_EPS) + ADAM_WD * w_ref[...])
    nm_ref[...] = nm
    nv_ref[...] = nv


def _adamw_small(ws, gs, ms, vs):
    k = len(ws)

    def body(*refs):
        ins, outs = refs[:4 * k], refs[4 * k:]
        for a in range(k):
            _adamw_update(ins[a], ins[k + a], ins[2 * k + a], ins[3 * k + a], outs[a], outs[k + a], outs[2 * k + a])

    out = pl.pallas_call(
        body, name="adamw_small",
        out_shape=[jax.ShapeDtypeStruct(w.shape, F32) for w in ws] * 3,
        compiler_params=pltpu.CompilerParams(vmem_limit_bytes=VMEM_LIMIT),
    )(*ws, *gs, *ms, *vs)
    return out[:k], out[k:2 * k], out[2 * k:]


def _adamw(w, g, m, v, name):
    shape = w.shape
    w2, g2, m2, v2 = (a.reshape((-1, shape[-1])) for a in (w, g, m, v))

    def body(w_ref, g_ref, m_ref, v_ref, d_ref, nm_ref, nv_ref):
        _adamw_update(w_ref, g_ref, m_ref, v_ref, d_ref, nm_ref, nv_ref)

    rows, cols = w2.shape
    nblk = cols // 256 if cols % 256 == 0 and rows >= 64 else 1
    blk = pl.BlockSpec((rows, cols // nblk), lambda j: (0, j))
    out = pl.pallas_call(
        body, name=name, grid=(nblk,), in_specs=[blk] * 4, out_specs=[blk] * 3,
        out_shape=[jax.ShapeDtypeStruct(w2.shape, F32)] * 3,
        compiler_params=_cparams("parallel"),
    )(w2, g2, m2, v2)
    return tuple(a.reshape(shape) for a in out)


def kernel(x, meta_tokens, norm_g, w_in, q_norm_g, w_q_up, kv_norm_g, w_kv_up, conv_w, attn_out_g, conv_out_g, w_out, final_norm_g, loss_target, m_meta_tokens, m_norm_g, m_w_in, m_q_norm_g, m_w_q_up, m_kv_norm_g, m_w_kv_up, m_conv_w, m_attn_out_g, m_conv_out_g, m_w_out, m_final_norm_g, v_meta_tokens, v_norm_g, v_w_in, v_q_norm_g, v_w_q_up, v_kv_norm_g, v_w_kv_up, v_conv_w, v_attn_out_g, v_conv_out_g, v_w_out, v_final_norm_g):
    nb, s, _ = x.shape
    tm = min(ROW_TILE, s)
    ta = min(ATTN_TILE, s)
    assert s % tm == 0 and s % ta == 0 and tm % 16 == 0
    r = nb * s

    tr = lambda a: jnp.transpose(a[0])
    w_head, w_in_shard, w_out_shard, wq_p, wkv_p, g_cw, meta_f = _gather_weights(
        tr(w_in), w_out[0], [tr(w_q_up), tr(w_kv_up)],
        [W_Q_PIECES, W_KV_PIECES], [HEADS * QK_PAD, 1024],
        [jnp.transpose(conv_w, (1, 0, 2)), meta_tokens],
        [(0, jnp.zeros((64, Q_RANK), BF16), QK_PAD * h + NOPE + ROPE) for h in range(HEADS)])
    conv_f = g_cw.reshape(3, CONV_W)

    c_all, sa_all, sb_all = _rope_tables(N_META + s)
    tabs_m = (c_all[:N_META], sa_all[:N_META], sb_all[:N_META])
    tabs = (c_all[N_META:], sa_all[N_META:], sb_all[N_META:])
    gid = np.arange(CONV_W) // CONV_GROUP
    gmat = jnp.asarray(np.where(gid[:, None] == gid[None, :], 1.0 / CONV_GROUP, 0.0), BF16)
    ga, gc = attn_out_g, conv_out_g
    gf = final_norm_g.reshape(1, D_MODEL)

    x2d = x.reshape(r, D_MODEL)
    tgt2d = loss_target.reshape(r, D_MODEL)

    ph, q, k, v, pmh, km, vm, w_out_f, w_in_part = _fwd_proj(
        x2d, meta_f, tabs, tabs_m, norm_g, w_head, q_norm_g, wq_p, kv_norm_g, wkv_p, w_out_shard, w_in_shard,
        nb, s, tm)
    o, lse, w_in_p = _attn_fwd(q, k, v, km, vm, w_in_shard, w_in_part, nb, s, ta)
    dh2, dycat, dw_out, dgf, loss_acc, pt, pmt = _out_fwd_bwd(x2d, tgt2d, o, meta_f, norm_g, w_in_p, conv_f, ga, gc,
                                                              gmat, w_out_f, gf, nb, s, tm)
    dpb, do, delta, dccm, dga, dgc, dcw = _gate_bwd(dycat, o, pt, pmt, conv_f, ga, gc, gmat, nb, s, tm)
    p_out = dw_out.reshape(4, D_MODEL // 4, D_MODEL)
    dq, dk, dv, dkm, dvm, g_w_out = _attn_bwd(q, k, v, do, lse, delta, km, vm, [p_out], nb, s, ta)
    dpa, dpam, p_q, p_kv, dgq, dgkv = _up_bwd(dq, dk, dv, dkm, dvm, ph, pmh, tabs, tabs_m, wq_p, wkv_p,
                                              q_norm_g, kv_norm_g, nb, s, tm)
    gx, gmeta, p_in, dng = _in_bwd(x2d, dh2, dpa, dpb, meta_f, dpam, dccm, pmt, w_in_p, norm_g, nb, s, tm)

    g_w_in_t, g_w_q_t, g_w_kv_t, small_sum = _reduce_grads(
        [p_in, p_q, p_kv],
        [(dng, 1), (dgq, 1), (dgkv, 1), (dga, 1), (dgc, 1), (dgf, 1), (dcw, 3), (gmeta, N_META), (loss_acc, 1)])
    ssum = small_sum.reshape(-1)

    def take(off, n):
        return ssum[off:off + n], off + n

    off = 0
    g_norm, off = take(off, D_MODEL)
    g_qn, off = take(off, Q_RANK)
    g_kvn, off = take(off, KV_RANK)
    g_ga, off = take(off, CONV_W)
    g_gc, off = take(off, CONV_W)
    g_gf, off = take(off, D_MODEL)
    g_cw_all, off = take(off, 3 * CONV_W)
    g_meta_all, off = take(off, N_META * D_MODEL)
    loss = ssum[off]
    chip = 2 * lax.axis_index("x") + lax.axis_index("y")
    g_conv = lax.dynamic_slice(g_cw_all.reshape(3, CONV_W), (0, chip * 128), (3, 128))
    g_mt = lax.dynamic_slice(g_meta_all.reshape(N_META, D_MODEL), (0, chip * 256), (N_META, 256))

    grads = {
        "meta_tokens": g_mt, "norm_g": g_norm.reshape(1, -1), "w_in": g_w_in_t, "q_norm_g": g_qn.reshape(1, -1),
        "w_q_up": g_w_q_t, "kv_norm_g": g_kvn.reshape(1, -1), "w_kv_up": jnp.transpose(g_w_kv_t)[None],
        "conv_w": g_conv[None], "attn_out_g": g_ga.reshape(1, -1), "conv_out_g": g_gc.reshape(1, -1),
        "w_out": g_w_out[None], "final_norm_g": g_gf,
    }
    transposed = ("w_in", "w_q_up")
    weights = {
        "meta_tokens": (meta_tokens, m_meta_tokens, v_meta_tokens), "norm_g": (norm_g, m_norm_g, v_norm_g),
        "w_in": (w_in, m_w_in, v_w_in), "q_norm_g": (q_norm_g, m_q_norm_g, v_q_norm_g),
        "w_q_up": (w_q_up, m_w_q_up, v_w_q_up), "kv_norm_g": (kv_norm_g, m_kv_norm_g, v_kv_norm_g),
        "w_kv_up": (w_kv_up, m_w_kv_up, v_w_kv_up), "conv_w": (conv_w, m_conv_w, v_conv_w),
        "attn_out_g": (attn_out_g, m_attn_out_g, v_attn_out_g), "conv_out_g": (conv_out_g, m_conv_out_g, v_conv_out_g),
        "w_out": (w_out, m_w_out, v_w_out), "final_norm_g": (final_norm_g, m_final_norm_g, v_final_norm_g),
    }
    names = list(weights)
    small = [nme for nme in names if nme != "w_in"]

    def view(nme, a):
        if nme in transposed:
            return a if a.ndim == 2 else tr(a)
        if nme == "conv_w":
            return jnp.transpose(a.reshape(1, 3, -1), (1, 0, 2))
        if a.ndim == 3:
            return a[0]
        return a.reshape(1, -1) if a.ndim == 1 else a

    def unview(nme, a):
        if nme in transposed:
            return jnp.transpose(a)[None]
        if nme == "conv_w":
            return jnp.transpose(a, (1, 0, 2))
        return a.reshape(weights[nme][0].shape)

    res_small = _adamw_small(*[[view(nme, a) for nme, a in zip(small, col)] for col in (
        [weights[nme][0] for nme in small], [grads[nme] for nme in small],
        [weights[nme][1] for nme in small], [weights[nme][2] for nme in small])])
    w_, m_, v_ = weights["w_in"]
    res = _adamw(tr(w_), grads["w_in"], tr(m_), tr(v_), "adamw_w_in")
    upd = {"w_in": tuple(jnp.transpose(a)[None] for a in (grads["w_in"],) + res)}
    for j, nme in enumerate(small):
        upd[nme] = (unview(nme, view(nme, grads[nme])),) + tuple(unview(nme, r[j]) for r in res_small)
    grads = {nme: upd[nme][0] for nme in names}
    deltas, new_m, new_v = ([upd[nme][j] for nme in names] for j in (1, 2, 3))

    grad_x = gx.reshape(nb, s, D_MODEL)
    return (loss, grad_x, *[grads[nme] for nme in names], *deltas, *new_m, *new_v)
```

```python
import functools

import jax
import jax.numpy as jnp
import numpy as np
from jax import lax
from jax.experimental import pallas as pl
from jax.experimental.pallas import tpu as pltpu

F32 = jnp.float32
BF16 = jnp.bfloat16

D_MODEL = 1024
N_META = 16
HEADS = 4
NOPE = 128
ROPE = 64
VDIM = 128
QK_PAD = 256
Q_RANK = 256
KV_RANK = 128
CONV_W = 512
CONV_GROUP = 64
ROPE_THETA = 10000.0
EPS = 1e-6
ATTN_SCALE = (NOPE + ROPE) ** -0.5
IN_DIM = 3008
IN_PAD = 3072
HEAD_ROWS = Q_RANK + KV_RANK + ROPE
IN_HEAD = 512
IN_TAIL = IN_PAD - IN_HEAD
BLK_ZA, BLK_CB, BLK_CC, BLK_CH, BLK_ZC = 0, 1, 2, 3, 4
NEG_INF = -1e30

ADAM_LR = 0.001
ADAM_B1 = 0.9
ADAM_B2 = 0.999
ADAM_EPS = 1e-08
ADAM_WD = 0.01
ADAM_STEP = 10

ROW_TILE = 512
ATTN_TILE = 256
VMEM_LIMIT = 56 * 1024 * 1024

NT = (((1,), (1,)), ((), ()))
TN = (((0,), (0,)), ((), ()))


def _cparams(*sem):
    return pltpu.CompilerParams(dimension_semantics=sem, vmem_limit_bytes=VMEM_LIMIT)


def _dot(a, b):
    return jnp.dot(a, b, preferred_element_type=F32)


def _dot_nt(a, b):
    return lax.dot_general(a, b, NT, preferred_element_type=F32)


def _dot_tn(a, b):
    return lax.dot_general(a, b, TN, preferred_element_type=F32)


def _rms(x, g):
    r = lax.rsqrt(jnp.mean(x * x, axis=-1, keepdims=True) + EPS)
    return x * r * g, r


def _rms_bwd(dy, x, r, g):
    xh = x * r
    dyg = dy * g
    dx = r * (dyg - xh * jnp.mean(dyg * xh, axis=-1, keepdims=True))
    return dx, dy * xh


def _sigmoid(z):
    return 1.0 / (1.0 + jnp.exp(-z))


def _rope(b, c, sa, sb):
    return b * c + pltpu.roll(b, 96, 1) * sa + pltpu.roll(b, 32, 1) * sb


def _rope_bwd(d, c, sa, sb):
    return d * c + pltpu.roll(d * sa, 32, 1) + pltpu.roll(d * sb, 96, 1)


def _group_mean(x, gmat):
    hi = x.astype(BF16)
    lo = (x - hi.astype(F32)).astype(BF16)
    return _dot(hi, gmat) + _dot(lo, gmat)


def _row_of(col, rows):
    return jnp.transpose(jnp.broadcast_to(col, (rows, 128)))[0:1, :]


def _rope_tables(n_pos):
    half = ROPE // 2
    inv_freq = (np.float32(1.0) / (np.float32(ROPE_THETA) ** (np.arange(half, dtype=np.float32) / np.float32(half))))
    ang = np.arange(n_pos, dtype=np.float32)[:, None] * inv_freq.astype(np.float32)[None, :]
    cos, sin = np.cos(ang).astype(np.float32), np.sin(ang).astype(np.float32)
    z = np.zeros((n_pos, half), np.float32)
    c = np.concatenate([cos, cos, z, z], axis=1)
    sa = np.concatenate([-sin, z, z, z], axis=1)
    sb = np.concatenate([z, sin, z, z], axis=1)
    return jnp.asarray(c), jnp.asarray(sa), jnp.asarray(sb)


W_IN_PIECES_1 = ((0, 80, 752, 0, 64, 0), (384, 64, 752, 384, 448, 1), (448, 16, 752, 512, 512, 1))
W_IN_PIECES_2 = ((80, 304, 752, 80, 144, 0), (464, 288, 752, 528, 528, 1))
W_Q_PIECES = ((0, 96, 256, 0, 0, 0), (96, 96, 256, 96, 96, 1))
W_KV_PIECES = ((0, 128, 128, 0, 0, 0), (128, 128, 128, 512, 512, 1))
W_OUT_PIECES = ((0, 128, 256, 0, 0, 0), (128, 128, 256, 128, 128, 1))


class _StagedGather:
    STAGES = 4

    @staticmethod
    def steps(n_steps, early=False):
        if early:
            return (0, 3 * n_steps // 8, 6 * n_steps // 8, n_steps - 1)
        return (0, 5 * n_steps // 8, 7 * n_steps // 8, n_steps - 1)

    def __init__(self, pieces, zero_rows=None):
        self.pieces = pieces
        self.zero_rows = zero_rows

    def scratch(self):
        nk, dma = len(self.pieces), pltpu.SemaphoreType.DMA
        return [dma((nk, 3)), dma((nk, 3)), dma((nk, 3)), dma((nk, 3)), dma((nk,))]

    def vmem_scratch(self, shard_shape, out_shape):
        return [pltpu.VMEM(shard_shape, BF16), pltpu.VMEM(out_shape, BF16),
                pltpu.SemaphoreType.DMA((4 * len(self.pieces) + 1,)),
                pltpu.SemaphoreType.DMA((len(self.pieces),))] + self.scratch()

    def run_vmem(self, stage, shard_ref, out_ref, scr):
        src_scr, land_scr, io_sems, ld_sems = scr[:4]
        spans = []
        for _, nr, per, first, rest, _ in self.pieces:
            spans += [(per * q + (first if q == 0 else rest), nr) for q in range(4)]
        if self.zero_rows is not None:
            spans.append(self.zero_rows)
        flush = [pltpu.make_async_copy(land_scr.at[r0:r0 + nr], out_ref.at[r0:r0 + nr], io_sems.at[n])
                 for n, (r0, nr) in enumerate(spans)]
        if stage == 0:
            loads = [pltpu.make_async_copy(shard_ref.at[s0:s0 + nr], src_scr.at[s0:s0 + nr], ld_sems.at[k])
                     for k, (s0, nr, _, _, _, _) in enumerate(self.pieces)]
            for cp in loads:
                cp.start()
            if self.zero_rows is not None:
                r0, nr = self.zero_rows
                land_scr[r0:r0 + nr, :] = jnp.zeros((nr, land_scr.shape[1]), BF16)
            for cp in loads:
                cp.wait()
        if stage < self.STAGES:
            self.run(stage, src_scr, land_scr, scr[4:])
        for cp in flush:
            if stage == self.STAGES - 1:
                cp.start()
            if stage == self.STAGES:
                cp.wait()

    def run(self, stage, src_ref, out_ref, scr):
        send_sems, recv_sems, fwd_send, fwd_recv, loc_sems = scr
        pieces = self.pieces
        nk = len(pieces)
        x, y, c = lax.axis_index("x"), lax.axis_index("y"), lax.axis_index("c")
        mine = 2 * x + y
        chips = [(1 - x, y), (x, 1 - y), (1 - x, 1 - y)]
        chip_of = [2 * px + py for px, py in chips]
        mesh = pl.DeviceIdType.MESH

        def src(k):
            s0, nr = pieces[k][0], pieces[k][1]
            return src_ref.at[s0:s0 + nr]

        def dst(k, q):
            _, nr, per, first, rest, _ = pieces[k]
            row = per * q + first + (rest - first) * jnp.minimum(q, 1)
            return out_ref.at[pl.ds(pl.multiple_of(row, 16), nr)]

        def ici(k, j, q):
            px, py = chips[j]
            return pltpu.make_async_remote_copy(
                src_ref=src(k), dst_ref=dst(k, q), send_sem=send_sems.at[k, j], recv_sem=recv_sems.at[k, j],
                device_id=(px, py, c), device_id_type=mesh)

        def fwd(k, j):
            ref = dst(k, chip_of[j])
            return pltpu.make_async_remote_copy(
                src_ref=ref, dst_ref=ref, send_sem=fwd_send.at[k, j], recv_sem=fwd_recv.at[k, j],
                device_id=(x, y, 1 - c), device_id_type=mesh)

        def relay(k, half):
            ref = dst(k, chip_of[half])
            px, py = chips[1 - half]
            return pltpu.make_async_remote_copy(
                src_ref=ref, dst_ref=ref, send_sem=send_sems.at[k, 2], recv_sem=recv_sems.at[k, 2],
                device_id=(px, py, c), device_id_type=mesh)

        local = [pltpu.make_async_copy(src(k), dst(k, mine), loc_sems.at[k]) for k in range(nk)]
        if stage == 0:
            for cp in local:
                cp.start()
        if stage == 3:
            for cp in local:
                cp.wait()
        for half in (0, 1):
            @pl.when(c == half)
            def _(half=half):
                my_k = [k for k in range(nk) if pieces[k][5] == half]
                other_k = [k for k in range(nk) if pieces[k][5] != half]
                for k in my_k:
                    if stage == 0:
                        for j in range(2):
                            ici(k, j, mine).start()
                    elif stage == 1:
                        for j in (half, 1 - half):
                            ici(k, j, chip_of[j]).wait_recv()
                            if j == half:
                                relay(k, half).start()
                            fwd(k, j).start()
                    elif stage == 2:
                        ici(k, 2, chip_of[2]).wait_recv()
                        fwd(k, 2).start()
                    else:
                        for j in range(2):
                            ici(k, j, mine).wait_send()
                        relay(k, half).wait_send()
                        for j in range(3):
                            fwd(k, j).wait_send()
                if stage == 3:
                    for k in other_k:
                        for j in range(3):
                            fwd(k, j).wait_recv()


def _fwd_proj(x2d, meta, tabs, tabs_m, norm_g, w_head, q_norm_g, wq_p, kv_norm_g, wkv_p, w_out_shard, w_in_shard,
              nb, s, tm):
    nt = s // tm
    n = nb * nt
    n_steps = n + 1
    c_t, sa_t, sb_t = tabs
    cm_t, sam_t, sbm_t = tabs_m
    gat = _StagedGather(W_OUT_PIECES)
    gat_in = _StagedGather(W_IN_PIECES_1)
    n_sems = len(gat.scratch())
    assert n_steps >= 3

    def body(x_ref, c_ref, sa_ref, sb_ref, mt_ref, cm_ref, sam_ref, sbm_ref,
             g_ref, w_ref, gq_ref, wq_ref, gkv_ref, wkv_ref, wos_ref, wis_ref,
             p_ref, q_ref, k_ref, v_ref, pm_ref, km_ref, vm_ref, wo_ref, wi_ref, *scr):
        gat_scr, gat_in_scr = scr[:n_sems], scr[n_sems:]
        i = pl.program_id(0)
        for stage, at in enumerate(_StagedGather.steps(n_steps, early=True)):
            @pl.when(i == at)
            def _(stage=stage):
                gat_in.run_vmem(stage, wis_ref, wi_ref, gat_in_scr)
                gat.run(stage, wos_ref, wo_ref, gat_scr)

        def project(xv, c, sa, sb, p_out, q_out, k_out, v_out):
            u, _ = _rms(xv, g_ref[...])
            p = _dot_nt(u.astype(BF16), w_ref[...])
            p_out[...] = p
            qn, _ = _rms(p[:, 0:Q_RANK], gq_ref[...])
            q = _dot_nt(qn.astype(BF16), wq_ref[...])
            kvn, _ = _rms(p[:, Q_RANK:Q_RANK + KV_RANK], gkv_ref[...])
            kv = _dot_nt(kvn.astype(BF16), wkv_ref[...])
            kpe = _rope(p[:, 384:512], c, sa, sb)
            for h in range(HEADS):
                if q_out is not None:
                    pe = _rope(q[:, QK_PAD * h + NOPE:QK_PAD * (h + 1)], c, sa, sb)
                    qh = jnp.concatenate([q[:, QK_PAD * h:QK_PAD * h + NOPE], pe], axis=1)
                    q_out[0, h] = (qh * ATTN_SCALE).astype(BF16)
                k_out[0, h] = jnp.concatenate([kv[:, NOPE * h:NOPE * (h + 1)], kpe], axis=1).astype(BF16)
                v_out[0, h] = kv[:, 512 + VDIM * h:512 + VDIM * (h + 1)].astype(BF16)

        @pl.when(i < n)
        def _():
            project(x_ref[...], c_ref[...], sa_ref[...], sb_ref[...], p_ref, q_ref, k_ref, v_ref)

        @pl.when(i == n)
        def _():
            project(mt_ref[...], cm_ref[...], sam_ref[...], sbm_ref[...], pm_ref, None, km_ref, vm_ref)
            gat_in.run_vmem(gat_in.STAGES, wis_ref, wi_ref, gat_in_scr)

    cl = lambda i: jnp.minimum(i, n - 1)
    full = lambda a: pl.BlockSpec(a.shape, lambda i: (0,) * a.ndim)
    const = lambda shape: pl.BlockSpec(shape, lambda i: (0,) * len(shape))
    tab = pl.BlockSpec((tm, 128), lambda i: (cl(i) % nt, 0))
    hb = lambda w: pl.BlockSpec((1, HEADS, tm, w), lambda i: (cl(i) // nt, 0, cl(i) % nt, 0))
    whole = pl.BlockSpec(memory_space=pl.ANY)
    return pl.pallas_call(
        body, name="fwd_proj", grid=(n_steps,),
        in_specs=[pl.BlockSpec((tm, D_MODEL), lambda i: (cl(i), 0)), tab, tab, tab,
                  full(meta), full(cm_t), full(sam_t), full(sbm_t),
                  full(norm_g), full(w_head), full(q_norm_g), full(wq_p), full(kv_norm_g), full(wkv_p), whole, whole],
        out_specs=[pl.BlockSpec((tm, IN_HEAD), lambda i: (cl(i), 0)), hb(QK_PAD), hb(QK_PAD), hb(VDIM),
                   const((N_META, IN_HEAD)), const((1, HEADS, N_META, QK_PAD)), const((1, HEADS, N_META, VDIM)),
                   whole, whole],
        out_shape=[jax.ShapeDtypeStruct((nb * s, IN_HEAD), F32),
                   jax.ShapeDtypeStruct((nb, HEADS, s, QK_PAD), BF16),
                   jax.ShapeDtypeStruct((nb, HEADS, s, QK_PAD), BF16),
                   jax.ShapeDtypeStruct((nb, HEADS, s, VDIM), BF16),
                   jax.ShapeDtypeStruct((N_META, IN_HEAD), F32),
                   jax.ShapeDtypeStruct((1, HEADS, N_META, QK_PAD), BF16),
                   jax.ShapeDtypeStruct((1, HEADS, N_META, VDIM), BF16),
                   jax.ShapeDtypeStruct((D_MODEL, D_MODEL), BF16),
                   jax.ShapeDtypeStruct((IN_PAD, D_MODEL), BF16)],
        scratch_shapes=gat.scratch() + gat_in.vmem_scratch(w_in_shard.shape, (IN_PAD, D_MODEL)),
        compiler_params=_cparams("arbitrary"),
    )(x2d, c_t, sa_t, sb_t, meta, cm_t, sam_t, sbm_t, norm_g, w_head, q_norm_g, wq_p, kv_norm_g, wkv_p, w_out_shard,
      w_in_shard)


def _attn_fwd(q, k, v, km, vm, w_in_shard, w_in_part, nb, s, tq):
    nq = s // tq
    n_steps = nb * HEADS
    gat = _StagedGather(W_IN_PIECES_2, zero_rows=(HEAD_ROWS, IN_HEAD - HEAD_ROWS))
    assert n_steps >= 3

    def body(q_ref, k_ref, v_ref, km_ref, vm_ref, ws_ref, _, o_ref, lse_ref, w_ref, s_scr, p_scr, *gat_scr):
        step = pl.program_id(0) * HEADS + pl.program_id(1)
        for stage, at in enumerate(_StagedGather.steps(n_steps)):
            @pl.when(step == at)
            def _(stage=stage):
                gat.run_vmem(stage, ws_ref, w_ref, gat_scr)

        row = lax.broadcasted_iota(jnp.int32, (tq, tq), 0)
        col = lax.broadcasted_iota(jnp.int32, (tq, tq), 1)
        def scores(i):
            slot = i % 2
            qi = q_ref[0, 0, i * tq:(i + 1) * tq, :]
            sm = _dot_nt(qi, km_ref[0, 0])
            m128 = None
            for j in range(i + 1):
                sc = _dot_nt(qi, k_ref[0, 0, j * tq:(j + 1) * tq, :])
                if j == i:
                    sc = jnp.where(col <= row, sc, NEG_INF)
                s_scr[slot, :, j * tq:(j + 1) * tq] = sc
                mx = sc[:, 0:128]
                for c0 in range(128, tq, 128):
                    mx = jnp.maximum(mx, sc[:, c0:c0 + 128])
                m128 = mx if m128 is None else jnp.maximum(m128, mx)
            return sm, jnp.maximum(jnp.max(m128, axis=1, keepdims=True), jnp.max(sm, axis=1, keepdims=True))

        def weighted_sum(i, pm, l):
            n = (i + 1) * tq
            acc = _dot(p_scr[i % 2, :, 0:n], v_ref[0, 0, 0:n, :]) + _dot(pm.astype(BF16), vm_ref[0, 0])
            o_ref[0, 0, i * tq:(i + 1) * tq, :] = acc / l

        nxt, pending = scores(0), None
        for i in range(nq):
            slot = i % 2
            sm, m = nxt
            if i + 1 < nq:
                nxt = scores(i + 1)
            pm = jnp.exp(sm - m)
            l128 = None
            for j in range(i + 1):
                p = jnp.exp(s_scr[slot, :, j * tq:(j + 1) * tq] - m)
                p_scr[slot, :, j * tq:(j + 1) * tq] = p.astype(BF16)
                ps = p[:, 0:128]
                for c0 in range(128, tq, 128):
                    ps = ps + p[:, c0:c0 + 128]
                l128 = ps if l128 is None else l128 + ps
            l = jnp.sum(l128, axis=1, keepdims=True) + jnp.sum(pm, axis=1, keepdims=True)
            lse_ref[0, 0, :, i * tq:(i + 1) * tq] = _row_of(m + jnp.log(l), tq)
            if pending is not None:
                weighted_sum(*pending)
            pending = (i, pm, l)
        weighted_sum(*pending)

        @pl.when(step == n_steps - 1)
        def _():
            gat.run_vmem(gat.STAGES, ws_ref, w_ref, gat_scr)

    hblk = lambda w: pl.BlockSpec((1, 1, s, w), lambda b, h: (b, h, 0, 0))
    mblk = lambda w: pl.BlockSpec((1, 1, N_META, w), lambda b, h: (0, h, 0, 0))
    whole = pl.BlockSpec(memory_space=pl.ANY)
    return pl.pallas_call(
        body, name="attn_fwd", grid=(nb, HEADS),
        in_specs=[hblk(QK_PAD), hblk(QK_PAD), hblk(VDIM), mblk(QK_PAD), mblk(VDIM), whole, whole],
        out_specs=[hblk(VDIM), pl.BlockSpec((1, 1, 1, s), lambda b, h: (b, h, 0, 0)), whole],
        out_shape=[jax.ShapeDtypeStruct((nb, HEADS, s, VDIM), F32),
                   jax.ShapeDtypeStruct((nb, HEADS, 1, s), F32),
                   jax.ShapeDtypeStruct(w_in_part.shape, BF16)],
        input_output_aliases={6: 2},
        scratch_shapes=[pltpu.VMEM((2, tq, s), F32), pltpu.VMEM((2, tq, s), BF16)]
        + gat.vmem_scratch(w_in_shard.shape, w_in_part.shape),
        compiler_params=_cparams("arbitrary", "arbitrary"),
    )(q, k, v, km, vm, w_in_shard, w_in_part)


def _shift_rows(a, prev, n_rows):
    rid = lax.broadcasted_iota(jnp.int32, a.shape, 0)
    a1 = jnp.where(rid == 0, prev[7:8, :], pltpu.roll(a, 1, 0))
    a2 = jnp.where(rid == 0, prev[6:7, :], jnp.where(rid == 1, prev[7:8, :], pltpu.roll(a, 2, 0)))
    return a1, a2


def _attn_gate(o, za, ga_h):
    on, r = _rms(o, ga_h)
    return on * (za * _sigmoid(za)), on, r


def _out_fwd_bwd(x2d, tgt2d, o, meta, norm_g, w_in_p, conv_w, ga, gc, gmat, w_out, gf, nb, s, tm):
    nt = s // tm
    r = nb * s

    def body(x_ref, t_ref, o_ref, mt_ref, g_ref, wi_ref, cw_ref, ga_ref, gc_ref, gm_ref, w_ref, gf_ref,
             dh_ref, dy_ref, dw_ref, dgf_ref, loss_ref, p_ref, pm_ref, last_cc):
        i = pl.program_id(0)
        blk = lambda ref, j, rows=slice(None): ref[rows, 512 * j:512 * (j + 1)]

        def tail(xv):
            u, _ = _rms(xv, g_ref[...])
            return _dot_nt(u.astype(BF16), wi_ref[IN_HEAD:IN_PAD, :])

        @pl.when(i == 0)
        def _():
            dw_ref[...] = jnp.zeros_like(dw_ref)
            dgf_ref[...] = jnp.zeros_like(dgf_ref)
            loss_ref[...] = jnp.zeros_like(loss_ref)
            last_cc[...] = jnp.zeros_like(last_cc)
            pm_ref[...] = tail(mt_ref[...])

        u16 = _rms(x_ref[...], g_ref[...])[0].astype(BF16)

        def project(j):
            p_ref[:, 512 * j:512 * (j + 1)] = _dot_nt(u16, wi_ref[IN_HEAD + 512 * j:IN_HEAD + 512 * (j + 1), :])

        project(BLK_ZA)
        project(BLK_CC)
        project(BLK_CH)
        ya = []
        for h in range(HEADS):
            y, _, _ = _attn_gate(o_ref[0, h], p_ref[:, 512 * BLK_ZA + VDIM * h:512 * BLK_ZA + VDIM * (h + 1)],
                                 ga_ref[:, VDIM * h:VDIM * (h + 1)])
            ya.append(y)
        project(BLK_CB)
        project(BLK_ZC)
        cc = blk(p_ref, BLK_CC) * blk(p_ref, BLK_CH)
        meta_cc = blk(pm_ref, BLK_CC, slice(8, 16)) * blk(pm_ref, BLK_CH, slice(8, 16))
        prev = jnp.where(i % nt == 0, meta_cc, last_cc[...])
        last_cc[...] = cc[tm - 8:tm, :]
        cc1, cc2 = _shift_rows(cc, prev, tm)
        yc = blk(p_ref, BLK_CB) * (cw_ref[0:1, :] * cc2 + cw_ref[1:2, :] * cc1 + cw_ref[2:3, :] * cc)
        rg = lax.rsqrt(_group_mean(yc * yc, gm_ref[...]) + EPS)
        zc = blk(p_ref, BLK_ZC)
        yconv = yc * rg * gc_ref[...] * (zc * _sigmoid(zc))
        ycat = jnp.concatenate(ya + [yconv], axis=1).astype(BF16)
        h2 = x_ref[...] + _dot(ycat, w_ref[...])
        gfv = gf_ref[...]
        y, r2 = _rms(h2, gfv)
        e = y - t_ref[...]
        loss_ref[...] += 0.5 * jnp.sum(e * e) / D_MODEL
        dyv = e * (1.0 / D_MODEL)
        dh2, dgf = _rms_bwd(dyv, h2, r2, gfv)
        dgf_ref[...] += jnp.sum(dgf, axis=0, keepdims=True)
        dh_ref[...] = dh2
        dhb = dh2.astype(BF16)
        dy_ref[...] = _dot_nt(dhb, w_ref[...])
        dw_ref[...] += _dot_tn(ycat, dhb)

    row = lambda w: pl.BlockSpec((tm, w), lambda i: (i, 0))
    const = lambda shape: pl.BlockSpec(shape, lambda i: (0,) * len(shape))
    full = lambda a: const(a.shape)
    return pl.pallas_call(
        body, name="out_fwd_bwd", grid=(nb * nt,),
        in_specs=[row(D_MODEL), row(D_MODEL),
                  pl.BlockSpec((1, HEADS, tm, VDIM), lambda i: (i // nt, 0, i % nt, 0)),
                  full(meta), full(norm_g), full(w_in_p),
                  full(conv_w), full(ga), full(gc), full(gmat), full(w_out), full(gf)],
        out_specs=[row(D_MODEL), row(D_MODEL), const((D_MODEL, D_MODEL)), const((1, D_MODEL)), const((1, 128)),
                   row(IN_TAIL), const((N_META, IN_TAIL))],
        out_shape=[jax.ShapeDtypeStruct((r, D_MODEL), F32), jax.ShapeDtypeStruct((r, D_MODEL), F32),
                   jax.ShapeDtypeStruct((D_MODEL, D_MODEL), F32), jax.ShapeDtypeStruct((1, D_MODEL), F32),
                   jax.ShapeDtypeStruct((1, 128), F32),
                   jax.ShapeDtypeStruct((r, IN_TAIL), F32), jax.ShapeDtypeStruct((N_META, IN_TAIL), F32)],
        scratch_shapes=[pltpu.VMEM((8, 512), F32)],
        compiler_params=_cparams("arbitrary"),
    )(x2d, tgt2d, o, meta, norm_g, w_in_p, conv_w, ga, gc, gmat, w_out, gf)


def _gate_bwd(dycat, o, p, pm, conv_w, ga, gc, gmat, nb, s, tm):
    nt = s // tm
    r = nb * s
    ext = tm + 8
    prev_idx = lambda i: jnp.maximum(i * (tm // 8) - 1, 0)
    next_idx = lambda i: jnp.minimum((i + 1) * (tm // 8), r // 8 - 1)

    def body(dya_ref, dyc_ref, dycn_ref, o_ref, za_ref, cb_ref, cbn_ref, cc_ref, ccp_ref, ccn_ref,
             ch_ref, chp_ref, chn_ref, zc_ref, zcn_ref, mc_ref, mh_ref, cw_ref, ga_ref, gc_ref, gm_ref,
             dpb_ref, do_ref, dl_ref, dccm_ref, dga_ref, dgc_ref, dcw_ref):
        i = pl.program_id(0)

        @pl.when(i == 0)
        def _():
            dga_ref[...] = jnp.zeros_like(dga_ref)
            dgc_ref[...] = jnp.zeros_like(dgc_ref)
            dcw_ref[...] = jnp.zeros_like(dcw_ref)

        dga = []
        for h in range(HEADS):
            hs = slice(VDIM * h, VDIM * (h + 1))
            oh, za, gah, dya = o_ref[0, h], za_ref[:, hs], ga_ref[:, hs], dya_ref[:, hs]
            sg = _sigmoid(za)
            on, ro = _rms(oh, gah)
            don = dya * (za * sg)
            dpb_ref[:, hs] = (dya * on * (sg * (1.0 + za * (1.0 - sg)))).astype(BF16)
            do, dg = _rms_bwd(don, oh, ro, gah)
            dga.append(jnp.sum(dg, axis=0, keepdims=True))
            dob = do.astype(BF16)
            do_ref[0, h] = dob
            dl_ref[0, h] = _row_of(jnp.sum(dob.astype(F32) * oh, axis=1, keepdims=True), tm)
        dga_ref[...] += jnp.concatenate(dga, axis=1)

        cat = lambda a, b: jnp.concatenate([a[...], b[...]], axis=0)
        cch = cat(cc_ref, ccn_ref)
        chh = cat(ch_ref, chn_ref)
        cb = cat(cb_ref, cbn_ref)
        zc = cat(zc_ref, zcn_ref)
        dy = cat(dyc_ref, dycn_ref)
        first = i % nt == 0
        last = i % nt == nt - 1
        cc = cch * chh
        prev = jnp.where(first, mc_ref[8:16, :] * mh_ref[8:16, :], ccp_ref[...] * chp_ref[...])
        cc1, cc2 = _shift_rows(cc, prev, ext)
        w0, w1, w2 = cw_ref[0:1, :], cw_ref[1:2, :], cw_ref[2:3, :]
        dw = w0 * cc2 + w1 * cc1 + w2 * cc
        yc = cb * dw
        rg = lax.rsqrt(_group_mean(yc * yc, gm_ref[...]) + EPS)
        ych = yc * rg
        gcv = gc_ref[...]
        sg = _sigmoid(zc)
        dycn = dy * (zc * sg)
        dzc = dy * (ych * gcv) * (sg * (1.0 + zc * (1.0 - sg)))
        dgc_ref[...] += jnp.sum((dycn * ych)[:tm], axis=0, keepdims=True)
        dycg = dycn * gcv
        dyc = rg * (dycg - ych * _group_mean(dycg * ych, gm_ref[...]))
        rid = lax.broadcasted_iota(jnp.int32, (ext, CONV_W), 0)
        ddw = jnp.where(jnp.logical_and(last, rid >= tm), 0.0, dyc * cb)
        dcb = dyc * dw
        dcc = w2 * ddw + w1 * pltpu.roll(ddw, ext - 1, 0) + w0 * pltpu.roll(ddw, ext - 2, 0)
        dpb_ref[:, 512:1024] = dcb[:tm].astype(BF16)
        dpb_ref[:, 1024:1536] = (dcc * chh)[:tm].astype(BF16)
        dpb_ref[:, 1536:2048] = (dcc * cch)[:tm].astype(BF16)
        dpb_ref[:, 2048:2560] = dzc[:tm].astype(BF16)
        rs = lambda a: jnp.sum(a[:tm], axis=0, keepdims=True)
        dcw_ref[0:1, :] += rs(ddw * cc2)
        dcw_ref[1:2, :] += rs(ddw * cc1)
        dcw_ref[2:3, :] += rs(ddw * cc)

        @pl.when(first)
        def _():
            d0, d1 = ddw[0:1, :], ddw[1:2, :]
            r8 = lax.broadcasted_iota(jnp.int32, (8, CONV_W), 0)
            dccm_ref[0] = jnp.where(r8 == 7, w1 * d0 + w0 * d1, jnp.where(r8 == 6, w0 * d0, 0.0))

    row = lambda j: pl.BlockSpec((tm, 512), lambda i: (i, j))
    prv = lambda j: pl.BlockSpec((8, 512), lambda i: (prev_idx(i), j))
    nxt = lambda j: pl.BlockSpec((8, 512), lambda i: (next_idx(i), j))
    mblk = lambda j: pl.BlockSpec((N_META, 512), lambda i: (0, j))
    full = lambda a: pl.BlockSpec(a.shape, lambda i: (0,) * a.ndim)
    hb = lambda w: pl.BlockSpec((1, HEADS, tm, w), lambda i: (i // nt, 0, i % nt, 0))
    acc = lambda rr: pl.BlockSpec((rr, 512), lambda i: (0, 0))
    return pl.pallas_call(
        body, name="gate_bwd", grid=(nb * nt,),
        in_specs=[row(0), row(1), nxt(1), hb(VDIM),
                  row(BLK_ZA), row(BLK_CB), nxt(BLK_CB), row(BLK_CC), prv(BLK_CC), nxt(BLK_CC),
                  row(BLK_CH), prv(BLK_CH), nxt(BLK_CH), row(BLK_ZC), nxt(BLK_ZC),
                  mblk(BLK_CC), mblk(BLK_CH), full(conv_w), full(ga), full(gc), full(gmat)],
        out_specs=[pl.BlockSpec((tm, 2560), lambda i: (i, 0)), hb(VDIM),
                   pl.BlockSpec((1, HEADS, 1, tm), lambda i: (i // nt, 0, 0, i % nt)),
                   pl.BlockSpec((1, 8, 512), lambda i: (i // nt, 0, 0)),
                   acc(1), acc(1), acc(8)],
        out_shape=[jax.ShapeDtypeStruct((r, 2560), BF16), jax.ShapeDtypeStruct((nb, HEADS, s, VDIM), BF16),
                   jax.ShapeDtypeStruct((nb, HEADS, 1, s), F32), jax.ShapeDtypeStruct((nb, 8, 512), F32),
                   jax.ShapeDtypeStruct((1, 512), F32), jax.ShapeDtypeStruct((1, 512), F32),
                   jax.ShapeDtypeStruct((8, 512), F32)],
        compiler_params=_cparams("arbitrary"),
    )(dycat, dycat, dycat, o, p, p, p, p, p, p, p, p, p, p, p, pm, pm, conv_w, ga, gc, gmat)


class _StagedReduce:
    LOC, PRE_S, PRE_R, ICI_S, ICI_R, POST_S, POST_R, OUT, N_SEM = 0, 1, 2, 3, 6, 9, 10, 11, 12

    def __init__(self, shard_shape):
        self.half = (shard_shape[0] // 2, shard_shape[1])

    def scratch(self):
        h = self.half
        return [pltpu.VMEM((4,) + h, F32), pltpu.VMEM((4,) + h, F32), pltpu.VMEM((4,) + h, BF16),
                pltpu.VMEM((3,) + h, BF16), pltpu.VMEM(h, F32), pltpu.SemaphoreType.DMA((self.N_SEM,))]

    def run(self, stage, pin, gout, scr):
        own, sib, wire, rbuf, fin, sems = scr
        r2 = self.half[0]
        x, y, c = lax.axis_index("x"), lax.axis_index("y"), lax.axis_index("c")
        mine = 2 * x + y
        sibling = (x, y, 1 - c)
        chips = [(1 - x, y), (x, 1 - y), (1 - x, 1 - y)]
        rows = lambda half: pl.ds(pl.multiple_of(half * r2, r2), r2)
        mesh = pl.DeviceIdType.MESH

        loc = pltpu.make_async_copy(pin.at[:, rows(c), :], own, sems.at[self.LOC])
        pre = pltpu.make_async_remote_copy(
            src_ref=pin.at[:, rows(1 - c), :], dst_ref=sib, send_sem=sems.at[self.PRE_S],
            recv_sem=sems.at[self.PRE_R], device_id=sibling, device_id_type=mesh)

        def ici(j):
            px, py = chips[j]
            return pltpu.make_async_remote_copy(
                src_ref=wire.at[2 * px + py], dst_ref=rbuf.at[j], send_sem=sems.at[self.ICI_S + j],
                recv_sem=sems.at[self.ICI_R + j], device_id=(px, py, c), device_id_type=mesh)

        def post(half):
            return pltpu.make_async_remote_copy(
                src_ref=fin, dst_ref=gout.at[rows(half), :], send_sem=sems.at[self.POST_S],
                recv_sem=sems.at[self.POST_R], device_id=sibling, device_id_type=mesh)

        keep = pltpu.make_async_copy(fin, gout.at[rows(c), :], sems.at[self.OUT])
        if stage == 0:
            loc.start()
            pre.start()
        elif stage == 1:
            loc.wait()
            pre.wait_recv()
            for blk in range(4):
                tot = own[blk] + sib[blk]
                own[blk] = tot
                wire[blk] = tot.astype(BF16)
            for j in range(3):
                ici(j).start()
        elif stage == 2:
            for j in range(3):
                ici(j).wait_recv()
            tot = own[mine]
            for j in range(3):
                tot = tot + rbuf[j].astype(F32)
            fin[...] = tot
            post(c).start()
            keep.start()
        else:
            post(1 - c).wait_recv()
            pre.wait_send()
            for j in range(3):
                ici(j).wait_send()
            post(c).wait_send()
            keep.wait()


def _attn_bwd(q, k, v, do, lse, delta, km, vm, early, nb, s, t):
    n = s // t
    ne = len(early)
    reds = [_StagedReduce(a.shape[1:]) for a in early]
    n_steps = HEADS * nb
    assert n_steps >= 4

    def body(q_ref, k_ref, v_ref, do_ref, lse_ref, dl_ref, km_ref, vm_ref, *rest):
        pin_refs, rest = rest[:ne], rest[ne:]
        dq_ref, dk_ref, dv_ref, dkm_ref, dvm_ref = rest[:5]
        gout_refs, (p_scr, ds_scr, dq_acc), red_scr = rest[5:5 + ne], rest[5 + ne:8 + ne], rest[8 + ne:]
        b = pl.program_id(1)
        step = pl.program_id(0) * nb + b
        for stage, at in enumerate((0, 1, n_steps - 2, n_steps - 1)):
            @pl.when(step == at)
            def _(stage=stage):
                for a, red in enumerate(reds):
                    red.run(stage, pin_refs[a], gout_refs[a], red_scr[6 * a:6 * a + 6])

        @pl.when(b == 0)
        def _():
            dkm_ref[...] = jnp.zeros_like(dkm_ref)
            dvm_ref[...] = jnp.zeros_like(dvm_ref)

        kr = lax.broadcasted_iota(jnp.int32, (t, t), 0)
        qc = lax.broadcasted_iota(jnp.int32, (t, t), 1)
        km_v, vm_v = km_ref[0, 0], vm_ref[0, 0]
        ptm = jnp.exp(_dot_nt(km_v, q_ref[0, 0]) - lse_ref[0, 0])
        dstm = (ptm * (_dot_nt(vm_v, do_ref[0, 0]) - dl_ref[0, 0])).astype(BF16)
        dkm_ref[0] += _dot(dstm, q_ref[0, 0])
        dvm_ref[0] += _dot(ptm.astype(BF16), do_ref[0, 0])
        dq_acc[...] = _dot_tn(dstm, km_v)
        def tiles(j):
            slot = j % 2
            kj = k_ref[0, 0, j * t:(j + 1) * t, :]
            vj = v_ref[0, 0, j * t:(j + 1) * t, :]
            def products(i):
                cs = slice(i * t, (i + 1) * t)
                return _dot_nt(kj, q_ref[0, 0, cs, :]), _dot_nt(vj, do_ref[0, 0, cs, :])

            nxt, pending = products(j), None
            for i in range(j, n):
                cs = slice(i * t, (i + 1) * t)
                st, dpt = nxt
                if i + 1 < n:
                    nxt = products(i + 1)
                if i == j:
                    st = jnp.where(kr <= qc, st, NEG_INF)
                pt = jnp.exp(st - lse_ref[0, 0, :, cs])
                dst = (pt * (dpt - dl_ref[0, 0, :, cs])).astype(BF16)
                p_scr[slot, :, cs] = pt.astype(BF16)
                ds_scr[slot, :, cs] = dst
                if pending is not None:
                    dq_acc[pending[0], :] += _dot_tn(pending[1], kj)
                pending = (cs, dst)
            dq_acc[pending[0], :] += _dot_tn(pending[1], kj)

        for j in range(n):
            slot = j % 2
            tiles(j)
            dv_ref[0, 0, j * t:(j + 1) * t, :] = _dot(p_scr[slot, :, j * t:s], do_ref[0, 0, j * t:s, :]).astype(BF16)
            dk_ref[0, 0, j * t:(j + 1) * t, :] = _dot(ds_scr[slot, :, j * t:s], q_ref[0, 0, j * t:s, :]).astype(BF16)
        dq_ref[0, 0] = dq_acc[...].astype(BF16)

    big = lambda w: pl.BlockSpec((1, 1, s, w), lambda h, b: (b, h, 0, 0))
    rowv = pl.BlockSpec((1, 1, 1, s), lambda h, b: (b, h, 0, 0))
    mk = lambda w: pl.BlockSpec((1, 1, N_META, w), lambda h, b: (0, h, 0, 0))
    mo = lambda w: pl.BlockSpec((1, N_META, w), lambda h, b: (h, 0, 0))
    return pl.pallas_call(
        body, name="attn_bwd", grid=(HEADS, nb),
        in_specs=[big(QK_PAD), big(QK_PAD), big(VDIM), big(VDIM), rowv, rowv, mk(QK_PAD), mk(VDIM)]
        + [pl.BlockSpec(memory_space=pl.ANY)] * ne,
        out_specs=[big(QK_PAD), big(QK_PAD), big(VDIM), mo(QK_PAD), mo(VDIM)]
        + [pl.BlockSpec(memory_space=pl.ANY)] * ne,
        out_shape=[jax.ShapeDtypeStruct((nb, HEADS, s, QK_PAD), BF16),
                   jax.ShapeDtypeStruct((nb, HEADS, s, QK_PAD), BF16),
                   jax.ShapeDtypeStruct((nb, HEADS, s, VDIM), BF16),
                   jax.ShapeDtypeStruct((HEADS, N_META, QK_PAD), F32),
                   jax.ShapeDtypeStruct((HEADS, N_META, VDIM), F32)]
        + [jax.ShapeDtypeStruct(a.shape[1:], F32) for a in early],
        scratch_shapes=[pltpu.VMEM((2, t, s), BF16), pltpu.VMEM((2, t, s), BF16), pltpu.VMEM((s, QK_PAD), F32)]
        + [sc for red in reds for sc in red.scratch()],
        compiler_params=_cparams("arbitrary", "arbitrary"),
    )(q, k, v, do, lse, delta, km, vm, *early)


def _up_bwd(dq, dk, dv, dkm, dvm, p, pm, tabs, tabs_m, wq_p, wkv_p, gq, gkv, nb, s, tm):
    nt = s // tm
    n = nb * nt
    c_t, sa_t, sb_t = tabs
    cm_t, sam_t, sbm_t = tabs_m

    def kv_path(dkh, dvh, pa, c, sa, sb, wkv, gkvv):
        dkpe = dkh[0][:, NOPE:]
        for h in range(1, HEADS):
            dkpe = dkpe + dkh[h][:, NOPE:]
        dkr = _rope_bwd(dkpe, c, sa, sb)
        dkv = jnp.concatenate([d[:, :NOPE] for d in dkh] + list(dvh), axis=1).astype(BF16)
        ckv = pa[:, Q_RANK:Q_RANK + KV_RANK]
        kvn, rkv = _rms(ckv, gkvv)
        dckv, dg = _rms_bwd(_dot(dkv, wkv), ckv, rkv, gkvv)
        return dckv, dkr, kvn.astype(BF16), dkv, jnp.sum(dg, axis=0, keepdims=True)

    def body(dq_ref, dk_ref, dv_ref, pa_ref, c_ref, sa_ref, sb_ref,
             dkm_ref, dvm_ref, pam_ref, cm_ref, sam_ref, sbm_ref,
             wq_ref, wkv_ref, gq_ref, gkv_ref,
             dpa_ref, dpam_ref, pq_ref, pkv_ref, dgq_ref, dgkv_ref, dwq_ref, dwkv_ref):
        i = pl.program_id(0)

        @pl.when(i == 0)
        def _():
            dwq_ref[...] = jnp.zeros_like(dwq_ref)
            dwkv_ref[...] = jnp.zeros_like(dwkv_ref)
            dgq_ref[...] = jnp.zeros_like(dgq_ref)
            dgkv_ref[...] = jnp.zeros_like(dgkv_ref)

        @pl.when(i < n)
        def _():
            c, sa, sb = c_ref[...], sa_ref[...], sb_ref[...]
            pa = pa_ref[...]
            parts = []
            for h in range(HEADS):
                dqh = dq_ref[0, h].astype(F32) * ATTN_SCALE
                parts += [dqh[:, :NOPE], _rope_bwd(dqh[:, NOPE:], c, sa, sb)]
            dql = jnp.concatenate(parts, axis=1).astype(BF16)
            cq = pa[:, 0:Q_RANK]
            gqv = gq_ref[...]
            qn, rq = _rms(cq, gqv)
            dwq_ref[...] += _dot_tn(dql, qn.astype(BF16))
            dcq, dg = _rms_bwd(_dot(dql, wq_ref[...]), cq, rq, gqv)
            dgq_ref[...] += jnp.sum(dg, axis=0, keepdims=True)
            dckv, dkr, kvn, dkv, dgk = kv_path([dk_ref[0, h].astype(F32) for h in range(HEADS)],
                                               [dv_ref[0, h].astype(F32) for h in range(HEADS)],
                                               pa, c, sa, sb, wkv_ref[...], gkv_ref[...])
            dwkv_ref[...] += _dot_tn(dkv, kvn)
            dgkv_ref[...] += dgk
            dpa_ref[...] = jnp.concatenate([dcq, dckv, dkr], axis=1).astype(BF16)

        @pl.when(i == n)
        def _():
            dckv, dkr, kvn, dkv, dgk = kv_path([dkm_ref[h] for h in range(HEADS)],
                                               [dvm_ref[h] for h in range(HEADS)],
                                               pam_ref[...], cm_ref[...], sam_ref[...], sbm_ref[...],
                                               wkv_ref[...], gkv_ref[...])
            dwkv_ref[...] += _dot_tn(dkv, kvn)
            dgkv_ref[...] += dgk
            dpam_ref[...] = jnp.concatenate([jnp.zeros((N_META, Q_RANK), F32), dckv, dkr], axis=1)
            for h in range(HEADS):
                pq_ref[h] = dwq_ref[QK_PAD * h:QK_PAD * h + NOPE + ROPE, :]
                pkv_ref[h, 0:NOPE, :] = dwkv_ref[NOPE * h:NOPE * (h + 1), :]
                pkv_ref[h, NOPE:NOPE + VDIM, :] = dwkv_ref[512 + VDIM * h:512 + VDIM * (h + 1), :]

    cl = lambda i: jnp.minimum(i, n - 1)
    hb = lambda w: pl.BlockSpec((1, HEADS, tm, w), lambda i: (cl(i) // nt, 0, cl(i) % nt, 0))
    tab = pl.BlockSpec((tm, 128), lambda i: (cl(i) % nt, 0))
    full = lambda a: pl.BlockSpec(a.shape, lambda i: (0,) * a.ndim)
    const = lambda shape: pl.BlockSpec(shape, lambda i: (0,) * len(shape))
    return pl.pallas_call(
        body, name="up_bwd", grid=(n + 1,),
        in_specs=[hb(QK_PAD), hb(QK_PAD), hb(VDIM), pl.BlockSpec((tm, 512), lambda i: (cl(i), 0)), tab, tab, tab,
                  full(dkm), full(dvm), pl.BlockSpec((N_META, 512), lambda i: (0, 0)),
                  full(cm_t), full(sam_t), full(sbm_t), full(wq_p), full(wkv_p), full(gq), full(gkv)],
        out_specs=[pl.BlockSpec((tm, 512), lambda i: (cl(i), 0)), const((N_META, 512)),
                   const((HEADS, NOPE + ROPE, Q_RANK)), const((HEADS, NOPE + VDIM, KV_RANK)),
                   const((1, Q_RANK)), const((1, KV_RANK))],
        out_shape=[jax.ShapeDtypeStruct((nb * s, 512), BF16), jax.ShapeDtypeStruct((N_META, 512), F32),
                   jax.ShapeDtypeStruct((HEADS, NOPE + ROPE, Q_RANK), F32),
                   jax.ShapeDtypeStruct((HEADS, NOPE + VDIM, KV_RANK), F32),
                   jax.ShapeDtypeStruct((1, Q_RANK), F32), jax.ShapeDtypeStruct((1, KV_RANK), F32)],
        scratch_shapes=[pltpu.VMEM((HEADS * QK_PAD, Q_RANK), F32), pltpu.VMEM((1024, KV_RANK), F32)],
        compiler_params=_cparams("arbitrary"),
    )(dq, dk, dv, p, c_t, sa_t, sb_t, dkm, dvm, pm, cm_t, sam_t, sbm_t, wq_p, wkv_p, gq, gkv)


def _in_bwd(x2d, dh2, dpa, dpb, meta, dpam, dccm, pm, w_in_p, norm_g, nb, s, tm):
    nt = s // tm
    n = nb * nt

    def body(x_ref, dh_ref, dpa_ref, dpb_ref, mt_ref, dpam_ref, dccm_ref, mc_ref, mh_ref, w_ref, g_ref,
             gx_ref, gm_ref, dw_hbm, dg_ref, acc_ref, sems):
        i = pl.program_id(0)

        @pl.when(i == 0)
        def _():
            acc_ref[...] = jnp.zeros_like(acc_ref)
            dg_ref[...] = jnp.zeros_like(dg_ref)

        def rows(x, dp, dres):
            g = g_ref[...]
            dpb16 = dp.astype(BF16)
            du = _dot(dpb16, w_ref[...])
            u, r1 = _rms(x, g)
            acc_ref[...] += _dot_tn(dpb16, u.astype(BF16))
            dx, dg = _rms_bwd(du, x, r1, g)
            dg_ref[...] += jnp.sum(dg, axis=0, keepdims=True)
            return dx if dres is None else dx + dres

        @pl.when(i < n)
        def _():
            dp = jnp.concatenate([dpa_ref[...], dpb_ref[...]], axis=1)
            gx_ref[...] = rows(x_ref[...], dp, dh_ref[...])

        @pl.when(i == n)
        def _():
            dcc = dccm_ref[0]
            for b in range(1, nb):
                dcc = dcc + dccm_ref[b]
            z8 = jnp.zeros((8, CONV_W), F32)
            dc = jnp.concatenate([z8, dcc * mh_ref[8:16, :]], axis=0)
            dh = jnp.concatenate([z8, dcc * mc_ref[8:16, :]], axis=0)
            z = jnp.zeros((N_META, CONV_W), F32)
            dp = jnp.concatenate([dpam_ref[...], z, z, dc, dh, z], axis=1)
            gm_ref[...] = rows(mt_ref[...], dp, None)
            per = IN_DIM // 4
            cps = [pltpu.make_async_copy(acc_ref.at[0:448], dw_hbm.at[0, 0:448], sems.at[0]),
                   pltpu.make_async_copy(acc_ref.at[512:per + 64], dw_hbm.at[0, 448:per], sems.at[1])]
            for qq in range(1, 4):
                cps.append(pltpu.make_async_copy(acc_ref.at[per * qq + 64:per * (qq + 1) + 64], dw_hbm.at[qq],
                                                 sems.at[qq + 1]))
            for cp in cps:
                cp.start()
            for cp in cps:
                cp.wait()

    cl = lambda i: jnp.minimum(i, n - 1)
    row = lambda w: pl.BlockSpec((tm, w), lambda i: (cl(i), 0))
    full = lambda a: pl.BlockSpec(a.shape, lambda i: (0,) * a.ndim)
    mblk = lambda j: pl.BlockSpec((N_META, 512), lambda i: (0, j))
    return pl.pallas_call(
        body, name="in_bwd", grid=(n + 1,),
        in_specs=[row(D_MODEL), row(D_MODEL), row(512), row(2560), full(meta), full(dpam), full(dccm),
                  mblk(BLK_CC), mblk(BLK_CH), full(w_in_p), full(norm_g)],
        out_specs=[row(D_MODEL), pl.BlockSpec((N_META, D_MODEL), lambda i: (0, 0)),
                   pl.BlockSpec(memory_space=pl.ANY), pl.BlockSpec((1, D_MODEL), lambda i: (0, 0))],
        out_shape=[jax.ShapeDtypeStruct((nb * s, D_MODEL), F32), jax.ShapeDtypeStruct((N_META, D_MODEL), F32),
                   jax.ShapeDtypeStruct((4, IN_DIM // 4, D_MODEL), F32), jax.ShapeDtypeStruct((1, D_MODEL), F32)],
        scratch_shapes=[pltpu.VMEM((IN_PAD, D_MODEL), F32), pltpu.SemaphoreType.DMA((5,))],
        compiler_params=_cparams("arbitrary"),
    )(x2d, dh2, dpa, dpb, meta, dpam, dccm, pm, pm, w_in_p, norm_g)


def _gather_weights(w_in_shard, w_out_shard, split, pieces, out_rows, whole, zero_fills):
    ns, nw, nz = len(split), len(whole), len(zero_fills)
    flat = [(a, pc) for a in range(ns) for pc in pieces[a]]
    nk = len(flat)
    hh = HEAD_ROWS // 2
    assert hh % 16 == 0

    def body(*refs):
        ins, wins, zins = refs[2:2 + ns], refs[2 + ns:2 + ns + nw], refs[2 + ns + nw:2 + ns + nw + nz]
        n_in = 2 + ns + nw + nz
        head_ref, shard16, w_out16 = refs[n_in:n_in + 3]
        outs, wcat = refs[n_in + 3:n_in + 3 + ns], refs[n_in + 3 + ns:n_in + 3 + ns + nw]
        scr = refs[n_in + 3 + ns + nw:]
        stage, wouts = scr[:ns], scr[ns:ns + nw]
        (send_sems, recv_sems, fwd_send, fwd_recv, loc_sems, w_send, w_recv, w_loc, z_sems,
         h_send, h_recv, h_relay, h_pass) = scr[ns + nw:]
        x, y, c = lax.axis_index("x"), lax.axis_index("y"), lax.axis_index("c")
        mine = 2 * x + y
        chips = [(1 - x, y), (x, 1 - y), (1 - x, 1 - y)]
        chip_of = [2 * px + py for px, py in chips]
        shard16[...] = refs[0][...].astype(BF16)
        w_out16[...] = refs[1][...].astype(BF16)
        for a in range(ns):
            stage[a][...] = ins[a][...].astype(BF16)

        def head_rows(half):
            return pl.ds(pl.multiple_of(half * hh, 16), hh)

        def head_copy(turn):
            dest = (1 - c, c, c) if turn == 0 else (c, 1 - c, c)
            return pltpu.make_async_remote_copy(
                src_ref=shard16.at[head_rows(c)], dst_ref=head_ref.at[head_rows(c)], send_sem=h_send.at[turn],
                recv_sem=h_recv.at[0], device_id=dest, device_id_type=pl.DeviceIdType.MESH)

        def head_relay():
            ref = head_ref.at[head_rows(c)]
            return pltpu.make_async_remote_copy(
                src_ref=ref, dst_ref=ref, send_sem=h_relay.at[0], recv_sem=h_recv.at[0],
                device_id=(1, 1, c), device_id_type=pl.DeviceIdType.MESH)

        def head_pass(half):
            ref = head_ref.at[head_rows(half)]
            return pltpu.make_async_remote_copy(
                src_ref=ref, dst_ref=ref, send_sem=h_pass.at[0], recv_sem=h_pass.at[1],
                device_id=(x, y, 1 - c), device_id_type=pl.DeviceIdType.MESH)

        head_ref[HEAD_ROWS:IN_HEAD, :] = jnp.zeros((IN_HEAD - HEAD_ROWS, D_MODEL), BF16)

        @pl.when(mine == 0)
        def _():
            head_copy(0).start()
            head_ref[0:HEAD_ROWS, :] = shard16[0:HEAD_ROWS, :]

        def src(k):
            a, (s0, nr, _, _, _, _) = flat[k]
            return stage[a].at[s0:s0 + nr]

        def dst(k, q):
            a, (_, nr, per, first, rest, _) = flat[k]
            row = per * q + first + (rest - first) * jnp.minimum(q, 1)
            return outs[a].at[pl.ds(pl.multiple_of(row, 16), nr)]

        def ici(k, j, q):
            px, py = chips[j]
            return pltpu.make_async_remote_copy(
                src_ref=src(k), dst_ref=dst(k, q), send_sem=send_sems.at[k, j], recv_sem=recv_sems.at[k, j],
                device_id=(px, py, c), device_id_type=pl.DeviceIdType.MESH)

        def fwd(k, j):
            ref = dst(k, chip_of[j])
            return pltpu.make_async_remote_copy(
                src_ref=ref, dst_ref=ref, send_sem=fwd_send.at[k, j], recv_sem=fwd_recv.at[k, j],
                device_id=(x, y, 1 - c), device_id_type=pl.DeviceIdType.MESH)

        def wcopy(b, j, q):
            px, py = chips[j]
            return pltpu.make_async_remote_copy(
                src_ref=wins[b], dst_ref=wouts[b].at[q], send_sem=w_send.at[b, j], recv_sem=w_recv.at[b, j],
                device_id=(px, py, c), device_id_type=pl.DeviceIdType.MESH)

        local = [pltpu.make_async_copy(src(k), dst(k, mine), loc_sems.at[k]) for k in range(nk)]
        local += [pltpu.make_async_copy(wins[b], wouts[b].at[mine], w_loc.at[b]) for b in range(nw)]
        for z, (a, _, row0) in enumerate(zero_fills):
            local.append(pltpu.make_async_copy(zins[z], outs[a].at[row0:row0 + zins[z].shape[0]], z_sems.at[z]))
        wsends = [wcopy(b, j, mine) for b in range(nw) for j in range(3)]
        for cp in local + wsends:
            cp.start()

        for half in (0, 1):
            @pl.when(c == half)
            def _(half=half):
                my_k = [k for k in range(nk) if flat[k][1][5] == half]
                other_k = [k for k in range(nk) if flat[k][1][5] != half]
                sends = [ici(k, j, mine) for k in my_k for j in range(3)]
                for cp in sends:
                    cp.start()
                passed = []
                for k in my_k:
                    for j in range(3):
                        ici(k, j, chip_of[j]).wait_recv()
                        cp = fwd(k, j)
                        cp.start()
                        passed.append(cp)
                for k in other_k:
                    for j in range(3):
                        fwd(k, j).wait_recv()
                for cp in sends + passed:
                    cp.wait_send()

        for b in range(nw):
            for j in range(3):
                wcopy(b, j, chip_of[j]).wait_recv()
        for cp in wsends:
            cp.wait_send()
        for cp in local:
            cp.wait()
        for b in range(nw):
            cols = wins[b].shape[-1]
            for q in range(4):
                wcat[b][..., cols * q:cols * (q + 1)] = wouts[b][q]

        @pl.when(mine == 0)
        def _():
            head_copy(0).wait_send()
            head_copy(1).start()
            head_copy(1).wait_send()

        @pl.when(mine != 0)
        def _():
            hands_on = mine == 2 - c
            head_copy(0).wait_recv()

            @pl.when(hands_on)
            def _():
                head_relay().start()

            head_pass(c).start()
            head_pass(1 - c).wait_recv()
            head_pass(c).wait_send()

            @pl.when(hands_on)
            def _():
                head_relay().wait_send()

    vmem = pl.BlockSpec(memory_space=pltpu.VMEM)
    dma = pltpu.SemaphoreType.DMA
    zeros = [z for _, z, _ in zero_fills]
    return pl.pallas_call(
        body, name="gather_weights",
        in_specs=[vmem] * (2 + ns + nw + nz), out_specs=[vmem] * (3 + ns + nw),
        out_shape=([jax.ShapeDtypeStruct((IN_HEAD, D_MODEL), BF16), jax.ShapeDtypeStruct(w_in_shard.shape, BF16),
                    jax.ShapeDtypeStruct(w_out_shard.shape, BF16)]
                   + [jax.ShapeDtypeStruct((out_rows[a], split[a].shape[1]), BF16) for a in range(ns)]
                   + [jax.ShapeDtypeStruct(w.shape[:-1] + (4 * w.shape[-1],), w.dtype) for w in whole]),
        scratch_shapes=[pltpu.VMEM(a.shape, BF16) for a in split] + [pltpu.VMEM((4,) + w.shape, w.dtype) for w in whole]
        + [dma((nk, 3)), dma((nk, 3)), dma((nk, 3)), dma((nk, 3)), dma((nk,)),
           dma((nw, 3)), dma((nw, 3)), dma((nw,)), dma((nz,)),
           dma((2,)), dma((1,)), dma((1,)), dma((2,))],
        compiler_params=pltpu.CompilerParams(vmem_limit_bytes=VMEM_LIMIT),
    )(w_in_shard, w_out_shard, *split, *whole, *zeros)


def _reduce_grads(parts, small):
    n = len(parts)
    ns = len(small)
    shapes = [a.shape[1:] for a in parts]
    halves = [(sh[0] // 2, sh[1]) for sh in shapes]
    sm_blocks = [a.shape[1] // 128 for a, _ in small]
    sm_first = [sum(nr * nblk for (_, nr), nblk in zip(small[:k], sm_blocks[:k])) for k in range(ns)]
    sm_rows = -(-(sm_first[-1] + small[-1][1] * sm_blocks[-1]) // 8) * 8
    sm_shape = (sm_rows, 128)

    def body(*refs):
        pin, sm_in = refs[:n], refs[n:n + ns]
        gout, sm_out = refs[n + ns:2 * n + ns], refs[2 * n + ns]
        scr = refs[2 * n + ns + 1:]
        own, sib, wire, rbuf = scr[:n], scr[n:2 * n], scr[2 * n:3 * n], scr[3 * n:4 * n]
        (sbuf, send_sems, recv_sems, loc_sems, pre_send, pre_recv, post_send, post_recv,
         sm_send, sm_recv, sm_pack) = scr[4 * n:]
        x, y, c = lax.axis_index("x"), lax.axis_index("y"), lax.axis_index("c")
        mine = 2 * x + y
        sm_pack[...] = jnp.zeros(sm_shape, F32)
        for k, (_, nr) in enumerate(small):
            for i in range(nr):
                for j in range(sm_blocks[k]):
                    row = sm_first[k] + i * sm_blocks[k] + j
                    sm_pack[row:row + 1, :] = sm_in[k][i:i + 1, 128 * j:128 * (j + 1)]
        me = 4 * x + 2 * y + c
        sibling = (x, y, 1 - c)

        def rows(a, half):
            r2 = halves[a][0]
            return pl.ds(pl.multiple_of(half * r2, r2), r2)

        near = (jnp.where(c == 0, 1 - x, x), jnp.where(c == 0, y, 1 - y))
        far = (jnp.where(c == 0, x, 1 - x), jnp.where(c == 0, 1 - y, y))
        chip = lambda p: 2 * p[0] + p[1]
        blocks = [3 - mine, chip(near), chip(far), mine]

        def pre(a, k):
            q = blocks[k]
            return pltpu.make_async_remote_copy(
                src_ref=pin[a].at[q, rows(a, 1 - c), :], dst_ref=sib[a].at[q],
                send_sem=pre_send.at[a, q], recv_sem=pre_recv.at[a, q], device_id=sibling,
                device_id_type=pl.DeviceIdType.MESH)

        def ici(a, m):
            px, py = near if m < 2 else far
            return pltpu.make_async_remote_copy(
                src_ref=wire[a].at[blocks[m]], dst_ref=rbuf[a].at[m], send_sem=send_sems.at[a, m],
                recv_sem=recv_sems.at[a, m], device_id=(px, py, c), device_id_type=pl.DeviceIdType.MESH)

        def post(a, half):
            ref = gout[a].at[rows(a, half), :]
            return pltpu.make_async_remote_copy(
                src_ref=ref, dst_ref=ref, send_sem=post_send.at[a], recv_sem=post_recv.at[a],
                device_id=sibling, device_id_type=pl.DeviceIdType.MESH)

        def small_copy(kk):
            peer = (x ^ (kk >> 2), y ^ ((kk >> 1) & 1), c ^ (kk & 1))
            return pltpu.make_async_remote_copy(
                src_ref=sm_pack, dst_ref=sbuf.at[kk], send_sem=sm_send.at[kk - 1], recv_sem=sm_recv.at[kk - 1],
                device_id=peer, device_id_type=pl.DeviceIdType.MESH)

        local = [[pltpu.make_async_copy(pin[a].at[blocks[k], rows(a, c), :], own[a].at[blocks[k]], loc_sems.at[a, k])
                  for k in range(4)] for a in range(n)]
        pres = [[pre(a, k) for k in range(4)] for a in range(n)]
        smalls = [small_copy(kk) for kk in range(1, 8)]
        for a in range(n):
            for k in range(4):
                local[a][k].start()
                pres[a][k].start()
        for cp in smalls:
            cp.start()
        sbuf[0] = sm_pack[...]
        sends = []
        for a in range(n):
            for k in range(4):
                local[a][k].wait()
                pres[a][k].wait_recv()
                tot = own[a][blocks[k]] + sib[a][blocks[k]]
                if k == 2:
                    ici(a, 0).wait_recv()
                    tot = tot + rbuf[a][0].astype(F32)
                own[a][blocks[k]] = tot
                if k < 3:
                    wire[a][blocks[k]] = tot.astype(BF16)
                    cp = ici(a, k)
                    cp.start()
                    sends.append(cp)
        for cp in smalls:
            cp.wait_recv()
        total = sbuf[me]
        for d in range(1, 8):
            total = total + sbuf[me ^ d]
        sm_out[...] = total
        posts = []
        for a in range(n):
            fin = own[a][mine]
            for m in (1, 2):
                ici(a, m).wait_recv()
                fin = fin + rbuf[a][m].astype(F32)
            gout[a][rows(a, c), :] = fin
            cp = post(a, c)
            cp.start()
            posts.append(cp)
        for a in range(n):
            post(a, 1 - c).wait_recv()
        for cp in [cp for row in pres for cp in row] + sends + smalls + posts:
            cp.wait_send()

    vmem = pl.BlockSpec(memory_space=pltpu.VMEM)
    dma = pltpu.SemaphoreType.DMA
    return pl.pallas_call(
        body, name="reduce_grads",
        in_specs=[pl.BlockSpec(memory_space=pl.ANY)] * n + [vmem] * ns, out_specs=[vmem] * (n + 1),
        out_shape=[jax.ShapeDtypeStruct(sh, F32) for sh in shapes] + [jax.ShapeDtypeStruct(sm_shape, F32)],
        scratch_shapes=([pltpu.VMEM((4,) + hs, F32) for hs in halves] + [pltpu.VMEM((4,) + hs, F32) for hs in halves]
                        + [pltpu.VMEM((4,) + hs, BF16) for hs in halves]
                        + [pltpu.VMEM((3,) + hs, BF16) for hs in halves]
                        + [pltpu.VMEM((8,) + sm_shape, F32), dma((n, 3)), dma((n, 3)), dma((n, 4)),
                           dma((n, 4)), dma((n, 4)), dma((n,)), dma((n,)), dma((7,)), dma((7,)),
                           pltpu.VMEM(sm_shape, F32)]),
        compiler_params=pltpu.CompilerParams(vmem_limit_bytes=VMEM_LIMIT),
    )(*parts, *[a for a, _ in small])


def _adamw_update(w_ref, g_ref, m_ref, v_ref, d_ref, nm_ref, nv_ref):
    gv = g_ref[...]
    nm = ADAM_B1 * m_ref[...] + (1.0 - ADAM_B1) * gv
    nv = ADAM_B2 * v_ref[...] + (1.0 - ADAM_B2) * (gv * gv)
    m_hat = nm / (1.0 - ADAM_B1 ** ADAM_STEP)
    v_hat = nv / (1.0 - ADAM_B2 ** ADAM_STEP)
    d_ref[...] = -ADAM_LR * (m_hat / (jnp.sqrt(v_hat) + ADAM_EPS) + ADAM_WD * w_ref[...])
    nm_ref[...] = nm
    nv_ref[...] = nv


def _adamw_small(ws, gs, ms, vs):
    k = len(ws)

    def body(*refs):
        ins, outs = refs[:4 * k], refs[4 * k:]
        for a in range(k):
            _adamw_update(ins[a], ins[k + a], ins[2 * k + a], ins[3 * k + a], outs[a], outs[k + a], outs[2 * k + a])

    out = pl.pallas_call(
        body, name="adamw_small",
        out_shape=[jax.ShapeDtypeStruct(w.shape, F32) for w in ws] * 3,
        compiler_params=pltpu.CompilerParams(vmem_limit_bytes=VMEM_LIMIT),
    )(*ws, *gs, *ms, *vs)
    return out[:k], out[k:2 * k], out[2 * k:]


def _adamw(w, g, m, v, name):
    shape = w.shape
    w2, g2, m2, v2 = (a.reshape((-1, shape[-1])) for a in (w, g, m, v))

    def body(w_ref, g_ref, m_ref, v_ref, d_ref, nm_ref, nv_ref):
        _adamw_update(w_ref, g_ref, m_ref, v_ref, d_ref, nm_ref, nv_ref)

    rows, cols = w2.shape
    nblk = cols // 256 if cols % 256 == 0 and rows >= 64 else 1
    blk = pl.BlockSpec((rows, cols // nblk), lambda j: (0, j))
    out = pl.pallas_call(
        body, name=name, grid=(nblk,), in_specs=[blk] * 4, out_specs=[blk] * 3,
        out_shape=[jax.ShapeDtypeStruct(w2.shape, F32)] * 3,
        compiler_params=_cparams("parallel"),
    )(w2, g2, m2, v2)
    return tuple(a.reshape(shape) for a in out)


def kernel(x, meta_tokens, norm_g, w_in, q_norm_g, w_q_up, kv_norm_g, w_kv_up, conv_w, attn_out_g, conv_out_g, w_out, final_norm_g, loss_target, m_meta_tokens, m_norm_g, m_w_in, m_q_norm_g, m_w_q_up, m_kv_norm_g, m_w_kv_up, m_conv_w, m_attn_out_g, m_conv_out_g, m_w_out, m_final_norm_g, v_meta_tokens, v_norm_g, v_w_in, v_q_norm_g, v_w_q_up, v_kv_norm_g, v_w_kv_up, v_conv_w, v_attn_out_g, v_conv_out_g, v_w_out, v_final_norm_g):
    nb, s, _ = x.shape
    tm = min(ROW_TILE, s)
    ta = min(ATTN_TILE, s)
    assert s % tm == 0 and s % ta == 0 and tm % 16 == 0
    r = nb * s

    tr = lambda a: jnp.transpose(a[0])
    w_head, w_in_shard, w_out_shard, wq_p, wkv_p, g_cw, meta_f = _gather_weights(
        tr(w_in), w_out[0], [tr(w_q_up), tr(w_kv_up)],
        [W_Q_PIECES, W_KV_PIECES], [HEADS * QK_PAD, 1024],
        [jnp.transpose(conv_w, (1, 0, 2)), meta_tokens],
        [(0, jnp.zeros((64, Q_RANK), BF16), QK_PAD * h + NOPE + ROPE) for h in range(HEADS)])
    conv_f = g_cw.reshape(3, CONV_W)

    c_all, sa_all, sb_all = _rope_tables(N_META + s)
    tabs_m = (c_all[:N_META], sa_all[:N_META], sb_all[:N_META])
    tabs = (c_all[N_META:], sa_all[N_META:], sb_all[N_META:])
    gid = np.arange(CONV_W) // CONV_GROUP
    gmat = jnp.asarray(np.where(gid[:, None] == gid[None, :], 1.0 / CONV_GROUP, 0.0), BF16)
    ga, gc = attn_out_g, conv_out_g
    gf = final_norm_g.reshape(1, D_MODEL)

    x2d = x.reshape(r, D_MODEL)
    tgt2d = loss_target.reshape(r, D_MODEL)

    ph, q, k, v, pmh, km, vm, w_out_f, w_in_part = _fwd_proj(
        x2d, meta_f, tabs, tabs_m, norm_g, w_head, q_norm_g, wq_p, kv_norm_g, wkv_p, w_out_shard, w_in_shard,
        nb, s, tm)
    o, lse, w_in_p = _attn_fwd(q, k, v, km, vm, w_in_shard, w_in_part, nb, s, ta)
    dh2, dycat, dw_out, dgf, loss_acc, pt, pmt = _out_fwd_bwd(x2d, tgt2d, o, meta_f, norm_g, w_in_p, conv_f, ga, gc,
                                                              gmat, w_out_f, gf, nb, s, tm)
    dpb, do, delta, dccm, dga, dgc, dcw = _gate_bwd(dycat, o, pt, pmt, conv_f, ga, gc, gmat, nb, s, tm)
    p_out = dw_out.reshape(4, D_MODEL // 4, D_MODEL)
    dq, dk, dv, dkm, dvm, g_w_out = _attn_bwd(q, k, v, do, lse, delta, km, vm, [p_out], nb, s, ta)
    dpa, dpam, p_q, p_kv, dgq, dgkv = _up_bwd(dq, dk, dv, dkm, dvm, ph, pmh, tabs, tabs_m, wq_p, wkv_p,
                                              q_norm_g, kv_norm_g, nb, s, tm)
    gx, gmeta, p_in, dng = _in_bwd(x2d, dh2, dpa, dpb, meta_f, dpam, dccm, pmt, w_in_p, norm_g, nb, s, tm)

    g_w_in_t, g_w_q_t, g_w_kv_t, small_sum = _reduce_grads(
        [p_in, p_q, p_kv],
        [(dng, 1), (dgq, 1), (dgkv, 1), (dga, 1), (dgc, 1), (dgf, 1), (dcw, 3), (gmeta, N_META), (loss_acc, 1)])
    ssum = small_sum.reshape(-1)

    def take(off, n):
        return ssum[off:off + n], off + n

    off = 0
    g_norm, off = take(off, D_MODEL)
    g_qn, off = take(off, Q_RANK)
    g_kvn, off = take(off, KV_RANK)
    g_ga, off = take(off, CONV_W)
    g_gc, off = take(off, CONV_W)
    g_gf, off = take(off, D_MODEL)
    g_cw_all, off = take(off, 3 * CONV_W)
    g_meta_all, off = take(off, N_META * D_MODEL)
    loss = ssum[off]
    chip = 2 * lax.axis_index("x") + lax.axis_index("y")
    g_conv = lax.dynamic_slice(g_cw_all.reshape(3, CONV_W), (0, chip * 128), (3, 128))
    g_mt = lax.dynamic_slice(g_meta_all.reshape(N_META, D_MODEL), (0, chip * 256), (N_META, 256))

    grads = {
        "meta_tokens": g_mt, "norm_g": g_norm.reshape(1, -1), "w_in": g_w_in_t, "q_norm_g": g_qn.reshape(1, -1),
        "w_q_up": g_w_q_t, "kv_norm_g": g_kvn.reshape(1, -1), "w_kv_up": jnp.transpose(g_w_kv_t)[None],
        "conv_w": g_conv[None], "attn_out_g": g_ga.reshape(1, -1), "conv_out_g": g_gc.reshape(1, -1),
        "w_out": g_w_out[None], "final_norm_g": g_gf,
    }
    transposed = ("w_in", "w_q_up")
    weights = {
        "meta_tokens": (meta_tokens, m_meta_tokens, v_meta_tokens), "norm_g": (norm_g, m_norm_g, v_norm_g),
        "w_in": (w_in, m_w_in, v_w_in), "q_norm_g": (q_norm_g, m_q_norm_g, v_q_norm_g),
        "w_q_up": (w_q_up, m_w_q_up, v_w_q_up), "kv_norm_g": (kv_norm_g, m_kv_norm_g, v_kv_norm_g),
        "w_kv_up": (w_kv_up, m_w_kv_up, v_w_kv_up), "conv_w": (conv_w, m_conv_w, v_conv_w),
        "attn_out_g": (attn_out_g, m_attn_out_g, v_attn_out_g), "conv_out_g": (conv_out_g, m_conv_out_g, v_conv_out_g),
        "w_out": (w_out, m_w_out, v_w_out), "final_norm_g": (final_norm_g, m_final_norm_g, v_final_norm_g),
    }
    names = list(weights)
    small = [nme for nme in names if nme != "w_in"]

    def view(nme, a):
        if nme in transposed:
            return a if a.ndim == 2 else tr(a)
        if nme == "conv_w":
            return jnp.transpose(a.reshape(1, 3, -1), (1, 0, 2))
        if a.ndim == 3:
            return a[0]
        return a.reshape(1, -1) if a.ndim == 1 else a

    def unview(nme, a):
        if nme in transposed:
            return jnp.transpose(a)[None]
        if nme == "conv_w":
            return jnp.transpose(a, (1, 0, 2))
        return a.reshape(weights[nme][0].shape)

    res_small = _adamw_small(*[[view(nme, a) for nme, a in zip(small, col)] for col in (
        [weights[nme][0] for nme in small], [grads[nme] for nme in small],
        [weights[nme][1] for nme in small], [weights[nme][2] for nme in small])])
    w_, m_, v_ = weights["w_in"]
    res = _adamw(tr(w_), grads["w_in"], tr(m_), tr(v_), "adamw_w_in")
    upd = {"w_in": tuple(jnp.transpose(a)[None] for a in (grads["w_in"],) + res)}
    for j, nme in enumerate(small):
        upd[nme] = (unview(nme, view(nme, grads[nme])),) + tuple(unview(nme, r[j]) for r in res_small)
    grads = {nme: upd[nme][0] for nme in names}
    deltas, new_m, new_v = ([upd[nme][j] for nme in names] for j in (1, 2, 3))

    grad_x = gx.reshape(nb, s, D_MODEL)
    return (loss, grad_x, *[grads[nme] for nme in names], *deltas, *new_m, *new_v)
```

```python
import functools

import jax
import jax.numpy as jnp
import numpy as np
from jax import lax
from jax.experimental import pallas as pl
from jax.experimental.pallas import tpu as pltpu

F32 = jnp.float32
BF16 = jnp.bfloat16

D_MODEL = 1024
N_META = 16
HEADS = 4
NOPE = 128
ROPE = 64
VDIM = 128
QK_PAD = 256
Q_RANK = 256
KV_RANK = 128
CONV_W = 512
CONV_GROUP = 64
ROPE_THETA = 10000.0
EPS = 1e-6
ATTN_SCALE = (NOPE + ROPE) ** -0.5
IN_DIM = 3008
IN_PAD = 3072
HEAD_ROWS = Q_RANK + KV_RANK + ROPE
IN_HEAD = 512
IN_TAIL = IN_PAD - IN_HEAD
BLK_ZA, BLK_CB, BLK_CC, BLK_CH, BLK_ZC = 0, 1, 2, 3, 4
NEG_INF = -1e30

ADAM_LR = 0.001
ADAM_B1 = 0.9
ADAM_B2 = 0.999
ADAM_EPS = 1e-08
ADAM_WD = 0.01
ADAM_STEP = 10

ROW_TILE = 512
ATTN_TILE = 256
VMEM_LIMIT = 56 * 1024 * 1024

NT = (((1,), (1,)), ((), ()))
TN = (((0,), (0,)), ((), ()))


def _cparams(*sem):
    return pltpu.CompilerParams(dimension_semantics=sem, vmem_limit_bytes=VMEM_LIMIT)


def _dot(a, b):
    return jnp.dot(a, b, preferred_element_type=F32)


def _dot_nt(a, b):
    return lax.dot_general(a, b, NT, preferred_element_type=F32)


def _dot_tn(a, b):
    return lax.dot_general(a, b, TN, preferred_element_type=F32)


def _rms(x, g):
    r = lax.rsqrt(jnp.mean(x * x, axis=-1, keepdims=True) + EPS)
    return x * r * g, r


def _rms_bwd(dy, x, r, g):
    xh = x * r
    dyg = dy * g
    dx = r * (dyg - xh * jnp.mean(dyg * xh, axis=-1, keepdims=True))
    return dx, dy * xh


def _sigmoid(z):
    return 1.0 / (1.0 + jnp.exp(-z))


def _rope(b, c, sa, sb):
    return b * c + pltpu.roll(b, 96, 1) * sa + pltpu.roll(b, 32, 1) * sb


def _rope_bwd(d, c, sa, sb):
    return d * c + pltpu.roll(d * sa, 32, 1) + pltpu.roll(d * sb, 96, 1)


def _group_mean(x, gmat):
    hi = x.astype(BF16)
    lo = (x - hi.astype(F32)).astype(BF16)
    return _dot(hi, gmat) + _dot(lo, gmat)


def _row_of(col, rows):
    return jnp.transpose(jnp.broadcast_to(col, (rows, 128)))[0:1, :]


def _rope_tables(n_pos):
    half = ROPE // 2
    inv_freq = (np.float32(1.0) / (np.float32(ROPE_THETA) ** (np.arange(half, dtype=np.float32) / np.float32(half))))
    ang = np.arange(n_pos, dtype=np.float32)[:, None] * inv_freq.astype(np.float32)[None, :]
    cos, sin = np.cos(ang).astype(np.float32), np.sin(ang).astype(np.float32)
    z = np.zeros((n_pos, half), np.float32)
    c = np.concatenate([cos, cos, z, z], axis=1)
    sa = np.concatenate([-sin, z, z, z], axis=1)
    sb = np.concatenate([z, sin, z, z], axis=1)
    return jnp.asarray(c), jnp.asarray(sa), jnp.asarray(sb)


W_IN_PIECES_1 = ((0, 80, 752, 0, 64, 0), (384, 64, 752, 384, 448, 1), (448, 16, 752, 512, 512, 1))
W_IN_PIECES_2 = ((80, 304, 752, 80, 144, 0), (464, 288, 752, 528, 528, 1))
W_Q_PIECES = ((0, 96, 256, 0, 0, 0), (96, 96, 256, 96, 96, 1))
W_KV_PIECES = ((0, 128, 128, 0, 0, 0), (128, 128, 128, 512, 512, 1))
W_OUT_PIECES = ((0, 128, 256, 0, 0, 0), (128, 128, 256, 128, 128, 1))


class _StagedGather:
    STAGES = 4

    @staticmethod
    def steps(n_steps):
        return (0, 5 * n_steps // 8, 7 * n_steps // 8, n_steps - 1)

    def __init__(self, pieces, zero_rows=None):
        self.pieces = pieces
        self.zero_rows = zero_rows

    def scratch(self):
        nk, dma = len(self.pieces), pltpu.SemaphoreType.DMA
        return [dma((nk, 3)), dma((nk, 3)), dma((nk, 3)), dma((nk, 3)), dma((nk,))]

    def vmem_scratch(self, shard_shape, out_shape):
        return [pltpu.VMEM(shard_shape, BF16), pltpu.VMEM(out_shape, BF16),
                pltpu.SemaphoreType.DMA((4 * len(self.pieces) + 1,)),
                pltpu.SemaphoreType.DMA((len(self.pieces),))] + self.scratch()

    def run_vmem(self, stage, shard_ref, out_ref, scr):
        src_scr, land_scr, io_sems, ld_sems = scr[:4]
        spans = []
        for _, nr, per, first, rest, _ in self.pieces:
            spans += [(per * q + (first if q == 0 else rest), nr) for q in range(4)]
        if self.zero_rows is not None:
            spans.append(self.zero_rows)
        flush = [pltpu.make_async_copy(land_scr.at[r0:r0 + nr], out_ref.at[r0:r0 + nr], io_sems.at[n])
                 for n, (r0, nr) in enumerate(spans)]
        if stage == 0:
            loads = [pltpu.make_async_copy(shard_ref.at[s0:s0 + nr], src_scr.at[s0:s0 + nr], ld_sems.at[k])
                     for k, (s0, nr, _, _, _, _) in enumerate(self.pieces)]
            for cp in loads:
                cp.start()
            if self.zero_rows is not None:
                r0, nr = self.zero_rows
                land_scr[r0:r0 + nr, :] = jnp.zeros((nr, land_scr.shape[1]), BF16)
            for cp in loads:
                cp.wait()
        if stage < self.STAGES:
            self.run(stage, src_scr, land_scr, scr[4:])
        for cp in flush:
            if stage == self.STAGES - 1:
                cp.start()
            if stage == self.STAGES:
                cp.wait()

    def run(self, stage, src_ref, out_ref, scr):
        send_sems, recv_sems, fwd_send, fwd_recv, loc_sems = scr
        pieces = self.pieces
        nk = len(pieces)
        x, y, c = lax.axis_index("x"), lax.axis_index("y"), lax.axis_index("c")
        mine = 2 * x + y
        chips = [(1 - x, y), (x, 1 - y), (1 - x, 1 - y)]
        chip_of = [2 * px + py for px, py in chips]
        mesh = pl.DeviceIdType.MESH

        def src(k):
            s0, nr = pieces[k][0], pieces[k][1]
            return src_ref.at[s0:s0 + nr]

        def dst(k, q):
            _, nr, per, first, rest, _ = pieces[k]
            row = per * q + first + (rest - first) * jnp.minimum(q, 1)
            return out_ref.at[pl.ds(pl.multiple_of(row, 16), nr)]

        def ici(k, j, q):
            px, py = chips[j]
            return pltpu.make_async_remote_copy(
                src_ref=src(k), dst_ref=dst(k, q), send_sem=send_sems.at[k, j], recv_sem=recv_sems.at[k, j],
                device_id=(px, py, c), device_id_type=mesh)

        def fwd(k, j):
            ref = dst(k, chip_of[j])
            return pltpu.make_async_remote_copy(
                src_ref=ref, dst_ref=ref, send_sem=fwd_send.at[k, j], recv_sem=fwd_recv.at[k, j],
                device_id=(x, y, 1 - c), device_id_type=mesh)

        def relay(k, half):
            ref = dst(k, chip_of[half])
            px, py = chips[1 - half]
            return pltpu.make_async_remote_copy(
                src_ref=ref, dst_ref=ref, send_sem=send_sems.at[k, 2], recv_sem=recv_sems.at[k, 2],
                device_id=(px, py, c), device_id_type=mesh)

        local = [pltpu.make_async_copy(src(k), dst(k, mine), loc_sems.at[k]) for k in range(nk)]
        if stage == 0:
            for cp in local:
                cp.start()
        if stage == 3:
            for cp in local:
                cp.wait()
        for half in (0, 1):
            @pl.when(c == half)
            def _(half=half):
                my_k = [k for k in range(nk) if pieces[k][5] == half]
                other_k = [k for k in range(nk) if pieces[k][5] != half]
                for k in my_k:
                    if stage == 0:
                        for j in range(2):
                            ici(k, j, mine).start()
                    elif stage == 1:
                        for j in (half, 1 - half):
                            ici(k, j, chip_of[j]).wait_recv()
                            if j == half:
                                relay(k, half).start()
                            fwd(k, j).start()
                    elif stage == 2:
                        ici(k, 2, chip_of[2]).wait_recv()
                        fwd(k, 2).start()
                    else:
                        for j in range(2):
                            ici(k, j, mine).wait_send()
                        relay(k, half).wait_send()
                        for j in range(3):
                            fwd(k, j).wait_send()
                if stage == 3:
                    for k in other_k:
                        for j in range(3):
                            fwd(k, j).wait_recv()


def _fwd_proj(x2d, meta, tabs, tabs_m, norm_g, w_head, q_norm_g, wq_p, kv_norm_g, wkv_p, w_out_shard, w_in_shard,
              nb, s, tm):
    nt = s // tm
    n = nb * nt
    n_steps = n + 1
    c_t, sa_t, sb_t = tabs
    cm_t, sam_t, sbm_t = tabs_m
    gat = _StagedGather(W_OUT_PIECES)
    gat_in = _StagedGather(W_IN_PIECES_1)
    n_sems = len(gat.scratch())
    assert n_steps >= 3

    def body(x_ref, c_ref, sa_ref, sb_ref, mt_ref, cm_ref, sam_ref, sbm_ref,
             g_ref, w_ref, gq_ref, wq_ref, gkv_ref, wkv_ref, wos_ref, wis_ref,
             p_ref, q_ref, k_ref, v_ref, pm_ref, km_ref, vm_ref, wo_ref, wi_ref, x_ring, x_sems, *scr):
        gat_scr, gat_in_scr = scr[:n_sems], scr[n_sems:]
        i = pl.program_id(0)

        def x_copy(t):
            return pltpu.make_async_copy(x_ref.at[pl.ds(pl.multiple_of(t * tm, tm), tm)], x_ring.at[t % 3],
                                         x_sems.at[t % 3])

        @pl.when(i == 0)
        def _():
            for t in range(min(2, n)):
                x_copy(t).start()

        @pl.when(i + 2 < n)
        def _():
            x_copy(i + 2).start()

        for stage, at in enumerate(_StagedGather.steps(n_steps)):
            @pl.when(i == at)
            def _(stage=stage):
                gat_in.run_vmem(stage, wis_ref, wi_ref, gat_in_scr)
                gat.run(stage, wos_ref, wo_ref, gat_scr)

        def project(xv, c, sa, sb, p_out, q_out, k_out, v_out):
            u, _ = _rms(xv, g_ref[...])
            p = _dot_nt(u.astype(BF16), w_ref[...])
            p_out[...] = p
            qn, _ = _rms(p[:, 0:Q_RANK], gq_ref[...])
            q = _dot_nt(qn.astype(BF16), wq_ref[...])
            kvn, _ = _rms(p[:, Q_RANK:Q_RANK + KV_RANK], gkv_ref[...])
            kv = _dot_nt(kvn.astype(BF16), wkv_ref[...])
            kpe = _rope(p[:, 384:512], c, sa, sb)
            for h in range(HEADS):
                if q_out is not None:
                    pe = _rope(q[:, QK_PAD * h + NOPE:QK_PAD * (h + 1)], c, sa, sb)
                    qh = jnp.concatenate([q[:, QK_PAD * h:QK_PAD * h + NOPE], pe], axis=1)
                    q_out[0, h] = (qh * ATTN_SCALE).astype(BF16)
                k_out[0, h] = jnp.concatenate([kv[:, NOPE * h:NOPE * (h + 1)], kpe], axis=1).astype(BF16)
                v_out[0, h] = kv[:, 512 + VDIM * h:512 + VDIM * (h + 1)].astype(BF16)

        @pl.when(i < n)
        def _():
            x_copy(i).wait()
            project(x_ring[i % 3], c_ref[...], sa_ref[...], sb_ref[...], p_ref, q_ref, k_ref, v_ref)

        @pl.when(i == n)
        def _():
            project(mt_ref[...], cm_ref[...], sam_ref[...], sbm_ref[...], pm_ref, None, km_ref, vm_ref)
            gat_in.run_vmem(gat_in.STAGES, wis_ref, wi_ref, gat_in_scr)

    cl = lambda i: jnp.minimum(i, n - 1)
    full = lambda a: pl.BlockSpec(a.shape, lambda i: (0,) * a.ndim)
    const = lambda shape: pl.BlockSpec(shape, lambda i: (0,) * len(shape))
    tab = pl.BlockSpec((tm, 128), lambda i: (cl(i) % nt, 0))
    hb = lambda w: pl.BlockSpec((1, HEADS, tm, w), lambda i: (cl(i) // nt, 0, cl(i) % nt, 0))
    whole = pl.BlockSpec(memory_space=pl.ANY)
    return pl.pallas_call(
        body, name="fwd_proj", grid=(n_steps,),
        in_specs=[whole, tab, tab, tab,
                  full(meta), full(cm_t), full(sam_t), full(sbm_t),
                  full(norm_g), full(w_head), full(q_norm_g), full(wq_p), full(kv_norm_g), full(wkv_p), whole, whole],
        out_specs=[pl.BlockSpec((tm, IN_HEAD), lambda i: (cl(i), 0)), hb(QK_PAD), hb(QK_PAD), hb(VDIM),
                   const((N_META, IN_HEAD)), const((1, HEADS, N_META, QK_PAD)), const((1, HEADS, N_META, VDIM)),
                   whole, whole],
        out_shape=[jax.ShapeDtypeStruct((nb * s, IN_HEAD), F32),
                   jax.ShapeDtypeStruct((nb, HEADS, s, QK_PAD), BF16),
                   jax.ShapeDtypeStruct((nb, HEADS, s, QK_PAD), BF16),
                   jax.ShapeDtypeStruct((nb, HEADS, s, VDIM), BF16),
                   jax.ShapeDtypeStruct((N_META, IN_HEAD), F32),
                   jax.ShapeDtypeStruct((1, HEADS, N_META, QK_PAD), BF16),
                   jax.ShapeDtypeStruct((1, HEADS, N_META, VDIM), BF16),
                   jax.ShapeDtypeStruct((D_MODEL, D_MODEL), BF16),
                   jax.ShapeDtypeStruct((IN_PAD, D_MODEL), BF16)],
        scratch_shapes=[pltpu.VMEM((3, tm, D_MODEL), F32), pltpu.SemaphoreType.DMA((3,))]
        + gat.scratch() + gat_in.vmem_scratch(w_in_shard.shape, (IN_PAD, D_MODEL)),
        compiler_params=_cparams("arbitrary"),
    )(x2d, c_t, sa_t, sb_t, meta, cm_t, sam_t, sbm_t, norm_g, w_head, q_norm_g, wq_p, kv_norm_g, wkv_p, w_out_shard,
      w_in_shard)


def _attn_fwd(q, k, v, km, vm, w_in_shard, w_in_part, nb, s, tq):
    nq = s // tq
    n_steps = nb * HEADS
    gat = _StagedGather(W_IN_PIECES_2, zero_rows=(HEAD_ROWS, IN_HEAD - HEAD_ROWS))
    assert n_steps >= 3

    def body(q_ref, k_ref, v_ref, km_ref, vm_ref, ws_ref, _, o_ref, lse_ref, w_ref, s_scr, p_scr, *gat_scr):
        step = pl.program_id(0) * HEADS + pl.program_id(1)
        for stage, at in enumerate(_StagedGather.steps(n_steps)):
            @pl.when(step == at)
            def _(stage=stage):
                gat.run_vmem(stage, ws_ref, w_ref, gat_scr)

        row = lax.broadcasted_iota(jnp.int32, (tq, tq), 0)
        col = lax.broadcasted_iota(jnp.int32, (tq, tq), 1)
        def scores(i):
            slot = i % 2
            qi = q_ref[0, 0, i * tq:(i + 1) * tq, :]
            sm = _dot_nt(qi, km_ref[0, 0])
            m128 = None
            for j in range(i + 1):
                sc = _dot_nt(qi, k_ref[0, 0, j * tq:(j + 1) * tq, :])
                if j == i:
                    sc = jnp.where(col <= row, sc, NEG_INF)
                s_scr[slot, :, j * tq:(j + 1) * tq] = sc
                mx = sc[:, 0:128]
                for c0 in range(128, tq, 128):
                    mx = jnp.maximum(mx, sc[:, c0:c0 + 128])
                m128 = mx if m128 is None else jnp.maximum(m128, mx)
            return sm, jnp.maximum(jnp.max(m128, axis=1, keepdims=True), jnp.max(sm, axis=1, keepdims=True))

        def weighted_sum(i, pm, l):
            n = (i + 1) * tq
            acc = _dot(p_scr[i % 2, :, 0:n], v_ref[0, 0, 0:n, :]) + _dot(pm.astype(BF16), vm_ref[0, 0])
            o_ref[0, 0, i * tq:(i + 1) * tq, :] = acc / l

        nxt, pending = scores(0), None
        for i in range(nq):
            slot = i % 2
            sm, m = nxt
            if i + 1 < nq:
                nxt = scores(i + 1)
            pm = jnp.exp(sm - m)
            l128 = None
            for j in range(i + 1):
                p = jnp.exp(s_scr[slot, :, j * tq:(j + 1) * tq] - m)
                p_scr[slot, :, j * tq:(j + 1) * tq] = p.astype(BF16)
                ps = p[:, 0:128]
                for c0 in range(128, tq, 128):
                    ps = ps + p[:, c0:c0 + 128]
                l128 = ps if l128 is None else l128 + ps
            l = jnp.sum(l128, axis=1, keepdims=True) + jnp.sum(pm, axis=1, keepdims=True)
            lse_ref[0, 0, :, i * tq:(i + 1) * tq] = _row_of(m + jnp.log(l), tq)
            if pending is not None:
                weighted_sum(*pending)
            pending = (i, pm, l)
        weighted_sum(*pending)

        @pl.when(step == n_steps - 1)
        def _():
            gat.run_vmem(gat.STAGES, ws_ref, w_ref, gat_scr)

    hblk = lambda w: pl.BlockSpec((1, 1, s, w), lambda b, h: (b, h, 0, 0))
    mblk = lambda w: pl.BlockSpec((1, 1, N_META, w), lambda b, h: (0, h, 0, 0))
    whole = pl.BlockSpec(memory_space=pl.ANY)
    return pl.pallas_call(
        body, name="attn_fwd", grid=(nb, HEADS),
        in_specs=[hblk(QK_PAD), hblk(QK_PAD), hblk(VDIM), mblk(QK_PAD), mblk(VDIM), whole, whole],
        out_specs=[hblk(VDIM), pl.BlockSpec((1, 1, 1, s), lambda b, h: (b, h, 0, 0)), whole],
        out_shape=[jax.ShapeDtypeStruct((nb, HEADS, s, VDIM), F32),
                   jax.ShapeDtypeStruct((nb, HEADS, 1, s), F32),
                   jax.ShapeDtypeStruct(w_in_part.shape, BF16)],
        input_output_aliases={6: 2},
        scratch_shapes=[pltpu.VMEM((2, tq, s), F32), pltpu.VMEM((2, tq, s), BF16)]
        + gat.vmem_scratch(w_in_shard.shape, w_in_part.shape),
        compiler_params=_cparams("arbitrary", "arbitrary"),
    )(q, k, v, km, vm, w_in_shard, w_in_part)


def _shift_rows(a, prev, n_rows):
    rid = lax.broadcasted_iota(jnp.int32, a.shape, 0)
    a1 = jnp.where(rid == 0, prev[7:8, :], pltpu.roll(a, 1, 0))
    a2 = jnp.where(rid == 0, prev[6:7, :], jnp.where(rid == 1, prev[7:8, :], pltpu.roll(a, 2, 0)))
    return a1, a2


def _attn_gate(o, za, ga_h):
    on, r = _rms(o, ga_h)
    return on * (za * _sigmoid(za)), on, r


def _out_fwd_bwd(x2d, tgt2d, o, meta, norm_g, w_in_p, conv_w, ga, gc, gmat, w_out, gf, nb, s, tm):
    nt = s // tm
    r = nb * s

    def body(x_ref, t_ref, o_ref, mt_ref, g_ref, wi_ref, cw_ref, ga_ref, gc_ref, gm_ref, w_ref, gf_ref,
             dh_ref, dy_ref, dw_ref, dgf_ref, loss_ref, p_ref, pm_ref, last_cc):
        i = pl.program_id(0)
        blk = lambda ref, j, rows=slice(None): ref[rows, 512 * j:512 * (j + 1)]

        def tail(xv):
            u, _ = _rms(xv, g_ref[...])
            return _dot_nt(u.astype(BF16), wi_ref[IN_HEAD:IN_PAD, :])

        @pl.when(i == 0)
        def _():
            dw_ref[...] = jnp.zeros_like(dw_ref)
            dgf_ref[...] = jnp.zeros_like(dgf_ref)
            loss_ref[...] = jnp.zeros_like(loss_ref)
            last_cc[...] = jnp.zeros_like(last_cc)
            pm_ref[...] = tail(mt_ref[...])

        u16 = _rms(x_ref[...], g_ref[...])[0].astype(BF16)

        def project(j):
            p_ref[:, 512 * j:512 * (j + 1)] = _dot_nt(u16, wi_ref[IN_HEAD + 512 * j:IN_HEAD + 512 * (j + 1), :])

        project(BLK_ZA)
        project(BLK_CC)
        project(BLK_CH)
        ya = []
        for h in range(HEADS):
            y, _, _ = _attn_gate(o_ref[0, h], p_ref[:, 512 * BLK_ZA + VDIM * h:512 * BLK_ZA + VDIM * (h + 1)],
                                 ga_ref[:, VDIM * h:VDIM * (h + 1)])
            ya.append(y)
        project(BLK_CB)
        project(BLK_ZC)
        cc = blk(p_ref, BLK_CC) * blk(p_ref, BLK_CH)
        meta_cc = blk(pm_ref, BLK_CC, slice(8, 16)) * blk(pm_ref, BLK_CH, slice(8, 16))
        prev = jnp.where(i % nt == 0, meta_cc, last_cc[...])
        last_cc[...] = cc[tm - 8:tm, :]
        cc1, cc2 = _shift_rows(cc, prev, tm)
        yc = blk(p_ref, BLK_CB) * (cw_ref[0:1, :] * cc2 + cw_ref[1:2, :] * cc1 + cw_ref[2:3, :] * cc)
        rg = lax.rsqrt(_group_mean(yc * yc, gm_ref[...]) + EPS)
        zc = blk(p_ref, BLK_ZC)
        yconv = yc * rg * gc_ref[...] * (zc * _sigmoid(zc))
        ycat = jnp.concatenate(ya + [yconv], axis=1).astype(BF16)
        h2 = x_ref[...] + _dot(ycat, w_ref[...])
        gfv = gf_ref[...]
        y, r2 = _rms(h2, gfv)
        e = y - t_ref[...]
        loss_ref[...] += 0.5 * jnp.sum(e * e) / D_MODEL
        dyv = e * (1.0 / D_MODEL)
        dh2, dgf = _rms_bwd(dyv, h2, r2, gfv)
        dgf_ref[...] += jnp.sum(dgf, axis=0, keepdims=True)
        dh_ref[...] = dh2
        dhb = dh2.astype(BF16)
        dy_ref[...] = _dot_nt(dhb, w_ref[...])
        dw_ref[...] += _dot_tn(ycat, dhb)

    row = lambda w: pl.BlockSpec((tm, w), lambda i: (i, 0))
    const = lambda shape: pl.BlockSpec(shape, lambda i: (0,) * len(shape))
    full = lambda a: const(a.shape)
    return pl.pallas_call(
        body, name="out_fwd_bwd", grid=(nb * nt,),
        in_specs=[row(D_MODEL), row(D_MODEL),
                  pl.BlockSpec((1, HEADS, tm, VDIM), lambda i: (i // nt, 0, i % nt, 0)),
                  full(meta), full(norm_g), full(w_in_p),
                  full(conv_w), full(ga), full(gc), full(gmat), full(w_out), full(gf)],
        out_specs=[row(D_MODEL), row(D_MODEL), const((D_MODEL, D_MODEL)), const((1, D_MODEL)), const((1, 128)),
                   row(IN_TAIL), const((N_META, IN_TAIL))],
        out_shape=[jax.ShapeDtypeStruct((r, D_MODEL), F32), jax.ShapeDtypeStruct((r, D_MODEL), F32),
                   jax.ShapeDtypeStruct((D_MODEL, D_MODEL), F32), jax.ShapeDtypeStruct((1, D_MODEL), F32),
                   jax.ShapeDtypeStruct((1, 128), F32),
                   jax.ShapeDtypeStruct((r, IN_TAIL), F32), jax.ShapeDtypeStruct((N_META, IN_TAIL), F32)],
        scratch_shapes=[pltpu.VMEM((8, 512), F32)],
        compiler_params=_cparams("arbitrary"),
    )(x2d, tgt2d, o, meta, norm_g, w_in_p, conv_w, ga, gc, gmat, w_out, gf)


def _gate_bwd(dycat, o, p, pm, conv_w, ga, gc, gmat, nb, s, tm):
    nt = s // tm
    r = nb * s
    ext = tm + 8
    prev_idx = lambda i: jnp.maximum(i * (tm // 8) - 1, 0)
    next_idx = lambda i: jnp.minimum((i + 1) * (tm // 8), r // 8 - 1)

    def body(dya_ref, dyc_ref, dycn_ref, o_ref, za_ref, cb_ref, cbn_ref, cc_ref, ccp_ref, ccn_ref,
             ch_ref, chp_ref, chn_ref, zc_ref, zcn_ref, mc_ref, mh_ref, cw_ref, ga_ref, gc_ref, gm_ref,
             dpb_ref, do_ref, dl_ref, dccm_ref, dga_ref, dgc_ref, dcw_ref):
        i = pl.program_id(0)

        @pl.when(i == 0)
        def _():
            dga_ref[...] = jnp.zeros_like(dga_ref)
            dgc_ref[...] = jnp.zeros_like(dgc_ref)
            dcw_ref[...] = jnp.zeros_like(dcw_ref)

        dga = []
        for h in range(HEADS):
            hs = slice(VDIM * h, VDIM * (h + 1))
            oh, za, gah, dya = o_ref[0, h], za_ref[:, hs], ga_ref[:, hs], dya_ref[:, hs]
            sg = _sigmoid(za)
            on, ro = _rms(oh, gah)
            don = dya * (za * sg)
            dpb_ref[:, hs] = (dya * on * (sg * (1.0 + za * (1.0 - sg)))).astype(BF16)
            do, dg = _rms_bwd(don, oh, ro, gah)
            dga.append(jnp.sum(dg, axis=0, keepdims=True))
            dob = do.astype(BF16)
            do_ref[0, h] = dob
            dl_ref[0, h] = _row_of(jnp.sum(dob.astype(F32) * oh, axis=1, keepdims=True), tm)
        dga_ref[...] += jnp.concatenate(dga, axis=1)

        cat = lambda a, b: jnp.concatenate([a[...], b[...]], axis=0)
        cch = cat(cc_ref, ccn_ref)
        chh = cat(ch_ref, chn_ref)
        cb = cat(cb_ref, cbn_ref)
        zc = cat(zc_ref, zcn_ref)
        dy = cat(dyc_ref, dycn_ref)
        first = i % nt == 0
        last = i % nt == nt - 1
        cc = cch * chh
        prev = jnp.where(first, mc_ref[8:16, :] * mh_ref[8:16, :], ccp_ref[...] * chp_ref[...])
        cc1, cc2 = _shift_rows(cc, prev, ext)
        w0, w1, w2 = cw_ref[0:1, :], cw_ref[1:2, :], cw_ref[2:3, :]
        dw = w0 * cc2 + w1 * cc1 + w2 * cc
        yc = cb * dw
        rg = lax.rsqrt(_group_mean(yc * yc, gm_ref[...]) + EPS)
        ych = yc * rg
        gcv = gc_ref[...]
        sg = _sigmoid(zc)
        dycn = dy * (zc * sg)
        dzc = dy * (ych * gcv) * (sg * (1.0 + zc * (1.0 - sg)))
        dgc_ref[...] += jnp.sum((dycn * ych)[:tm], axis=0, keepdims=True)
        dycg = dycn * gcv
        dyc = rg * (dycg - ych * _group_mean(dycg * ych, gm_ref[...]))
        rid = lax.broadcasted_iota(jnp.int32, (ext, CONV_W), 0)
        ddw = jnp.where(jnp.logical_and(last, rid >= tm), 0.0, dyc * cb)
        dcb = dyc * dw
        dcc = w2 * ddw + w1 * pltpu.roll(ddw, ext - 1, 0) + w0 * pltpu.roll(ddw, ext - 2, 0)
        dpb_ref[:, 512:1024] = dcb[:tm].astype(BF16)
        dpb_ref[:, 1024:1536] = (dcc * chh)[:tm].astype(BF16)
        dpb_ref[:, 1536:2048] = (dcc * cch)[:tm].astype(BF16)
        dpb_ref[:, 2048:2560] = dzc[:tm].astype(BF16)
        rs = lambda a: jnp.sum(a[:tm], axis=0, keepdims=True)
        dcw_ref[0:1, :] += rs(ddw * cc2)
        dcw_ref[1:2, :] += rs(ddw * cc1)
        dcw_ref[2:3, :] += rs(ddw * cc)

        @pl.when(first)
        def _():
            d0, d1 = ddw[0:1, :], ddw[1:2, :]
            r8 = lax.broadcasted_iota(jnp.int32, (8, CONV_W), 0)
            dccm_ref[0] = jnp.where(r8 == 7, w1 * d0 + w0 * d1, jnp.where(r8 == 6, w0 * d0, 0.0))

    row = lambda j: pl.BlockSpec((tm, 512), lambda i: (i, j))
    prv = lambda j: pl.BlockSpec((8, 512), lambda i: (prev_idx(i), j))
    nxt = lambda j: pl.BlockSpec((8, 512), lambda i: (next_idx(i), j))
    mblk = lambda j: pl.BlockSpec((N_META, 512), lambda i: (0, j))
    full = lambda a: pl.BlockSpec(a.shape, lambda i: (0,) * a.ndim)
    hb = lambda w: pl.BlockSpec((1, HEADS, tm, w), lambda i: (i // nt, 0, i % nt, 0))
    acc = lambda rr: pl.BlockSpec((rr, 512), lambda i: (0, 0))
    return pl.pallas_call(
        body, name="gate_bwd", grid=(nb * nt,),
        in_specs=[row(0), row(1), nxt(1), hb(VDIM),
                  row(BLK_ZA), row(BLK_CB), nxt(BLK_CB), row(BLK_CC), prv(BLK_CC), nxt(BLK_CC),
                  row(BLK_CH), prv(BLK_CH), nxt(BLK_CH), row(BLK_ZC), nxt(BLK_ZC),
                  mblk(BLK_CC), mblk(BLK_CH), full(conv_w), full(ga), full(gc), full(gmat)],
        out_specs=[pl.BlockSpec((tm, 2560), lambda i: (i, 0)), hb(VDIM),
                   pl.BlockSpec((1, HEADS, 1, tm), lambda i: (i // nt, 0, 0, i % nt)),
                   pl.BlockSpec((1, 8, 512), lambda i: (i // nt, 0, 0)),
                   acc(1), acc(1), acc(8)],
        out_shape=[jax.ShapeDtypeStruct((r, 2560), BF16), jax.ShapeDtypeStruct((nb, HEADS, s, VDIM), BF16),
                   jax.ShapeDtypeStruct((nb, HEADS, 1, s), F32), jax.ShapeDtypeStruct((nb, 8, 512), F32),
                   jax.ShapeDtypeStruct((1, 512), F32), jax.ShapeDtypeStruct((1, 512), F32),
                   jax.ShapeDtypeStruct((8, 512), F32)],
        compiler_params=_cparams("arbitrary"),
    )(dycat, dycat, dycat, o, p, p, p, p, p, p, p, p, p, p, p, pm, pm, conv_w, ga, gc, gmat)


class _StagedReduce:
    LOC, PRE_S, PRE_R, ICI_S, ICI_R, POST_S, POST_R, OUT, N_SEM = 0, 1, 2, 3, 6, 9, 10, 11, 12

    def __init__(self, shard_shape):
        self.half = (shard_shape[0] // 2, shard_shape[1])

    def scratch(self):
        h = self.half
        return [pltpu.VMEM((4,) + h, F32), pltpu.VMEM((4,) + h, F32), pltpu.VMEM((4,) + h, BF16),
                pltpu.VMEM((3,) + h, BF16), pltpu.VMEM(h, F32), pltpu.SemaphoreType.DMA((self.N_SEM,))]

    def run(self, stage, pin, gout, scr):
        own, sib, wire, rbuf, fin, sems = scr
        r2 = self.half[0]
        x, y, c = lax.axis_index("x"), lax.axis_index("y"), lax.axis_index("c")
        mine = 2 * x + y
        sibling = (x, y, 1 - c)
        chips = [(1 - x, y), (x, 1 - y), (1 - x, 1 - y)]
        rows = lambda half: pl.ds(pl.multiple_of(half * r2, r2), r2)
        mesh = pl.DeviceIdType.MESH

        loc = pltpu.make_async_copy(pin.at[:, rows(c), :], own, sems.at[self.LOC])
        pre = pltpu.make_async_remote_copy(
            src_ref=pin.at[:, rows(1 - c), :], dst_ref=sib, send_sem=sems.at[self.PRE_S],
            recv_sem=sems.at[self.PRE_R], device_id=sibling, device_id_type=mesh)

        def ici(j):
            px, py = chips[j]
            return pltpu.make_async_remote_copy(
                src_ref=wire.at[2 * px + py], dst_ref=rbuf.at[j], send_sem=sems.at[self.ICI_S + j],
                recv_sem=sems.at[self.ICI_R + j], device_id=(px, py, c), device_id_type=mesh)

        def post(half):
            return pltpu.make_async_remote_copy(
                src_ref=fin, dst_ref=gout.at[rows(half), :], send_sem=sems.at[self.POST_S],
                recv_sem=sems.at[self.POST_R], device_id=sibling, device_id_type=mesh)

        keep = pltpu.make_async_copy(fin, gout.at[rows(c), :], sems.at[self.OUT])
        if stage == 0:
            loc.start()
            pre.start()
        elif stage == 1:
            loc.wait()
            pre.wait_recv()
            for blk in range(4):
                tot = own[blk] + sib[blk]
                own[blk] = tot
                wire[blk] = tot.astype(BF16)
            for j in range(3):
                ici(j).start()
        elif stage == 2:
            for j in range(3):
                ici(j).wait_recv()
            tot = own[mine]
            for j in range(3):
                tot = tot + rbuf[j].astype(F32)
            fin[...] = tot
            post(c).start()
            keep.start()
        else:
            post(1 - c).wait_recv()
            pre.wait_send()
            for j in range(3):
                ici(j).wait_send()
            post(c).wait_send()
            keep.wait()


def _attn_bwd(q, k, v, do, lse, delta, km, vm, early, nb, s, t):
    n = s // t
    ne = len(early)
    reds = [_StagedReduce(a.shape[1:]) for a in early]
    n_steps = HEADS * nb
    assert n_steps >= 4

    def body(q_ref, k_ref, v_ref, do_ref, lse_ref, dl_ref, km_ref, vm_ref, *rest):
        pin_refs, rest = rest[:ne], rest[ne:]
        dq_ref, dk_ref, dv_ref, dkm_ref, dvm_ref = rest[:5]
        gout_refs, (p_scr, ds_scr, dq_acc), red_scr = rest[5:5 + ne], rest[5 + ne:8 + ne], rest[8 + ne:]
        b = pl.program_id(1)
        step = pl.program_id(0) * nb + b
        for stage, at in enumerate((0, 1, n_steps - 2, n_steps - 1)):
            @pl.when(step == at)
            def _(stage=stage):
                for a, red in enumerate(reds):
                    red.run(stage, pin_refs[a], gout_refs[a], red_scr[6 * a:6 * a + 6])

        @pl.when(b == 0)
        def _():
            dkm_ref[...] = jnp.zeros_like(dkm_ref)
            dvm_ref[...] = jnp.zeros_like(dvm_ref)

        kr = lax.broadcasted_iota(jnp.int32, (t, t), 0)
        qc = lax.broadcasted_iota(jnp.int32, (t, t), 1)
        km_v, vm_v = km_ref[0, 0], vm_ref[0, 0]
        ptm = jnp.exp(_dot_nt(km_v, q_ref[0, 0]) - lse_ref[0, 0])
        dstm = (ptm * (_dot_nt(vm_v, do_ref[0, 0]) - dl_ref[0, 0])).astype(BF16)
        dkm_ref[0] += _dot(dstm, q_ref[0, 0])
        dvm_ref[0] += _dot(ptm.astype(BF16), do_ref[0, 0])
        dq_acc[...] = _dot_tn(dstm, km_v)
        def tiles(j):
            slot = j % 2
            kj = k_ref[0, 0, j * t:(j + 1) * t, :]
            vj = v_ref[0, 0, j * t:(j + 1) * t, :]
            def products(i):
                cs = slice(i * t, (i + 1) * t)
                return _dot_nt(kj, q_ref[0, 0, cs, :]), _dot_nt(vj, do_ref[0, 0, cs, :])

            nxt, pending = products(j), None
            for i in range(j, n):
                cs = slice(i * t, (i + 1) * t)
                st, dpt = nxt
                if i + 1 < n:
                    nxt = products(i + 1)
                if i == j:
                    st = jnp.where(kr <= qc, st, NEG_INF)
                pt = jnp.exp(st - lse_ref[0, 0, :, cs])
                dst = (pt * (dpt - dl_ref[0, 0, :, cs])).astype(BF16)
                p_scr[slot, :, cs] = pt.astype(BF16)
                ds_scr[slot, :, cs] = dst
                if pending is not None:
                    dq_acc[pending[0], :] += _dot_tn(pending[1], kj)
                pending = (cs, dst)
            dq_acc[pending[0], :] += _dot_tn(pending[1], kj)

        for j in range(n):
            slot = j % 2
            tiles(j)
            dv_ref[0, 0, j * t:(j + 1) * t, :] = _dot(p_scr[slot, :, j * t:s], do_ref[0, 0, j * t:s, :]).astype(BF16)
            dk_ref[0, 0, j * t:(j + 1) * t, :] = _dot(ds_scr[slot, :, j * t:s], q_ref[0, 0, j * t:s, :]).astype(BF16)
        dq_ref[0, 0] = dq_acc[...].astype(BF16)

    big = lambda w: pl.BlockSpec((1, 1, s, w), lambda h, b: (b, h, 0, 0))
    rowv = pl.BlockSpec((1, 1, 1, s), lambda h, b: (b, h, 0, 0))
    mk = lambda w: pl.BlockSpec((1, 1, N_META, w), lambda h, b: (0, h, 0, 0))
    mo = lambda w: pl.BlockSpec((1, N_META, w), lambda h, b: (h, 0, 0))
    return pl.pallas_call(
        body, name="attn_bwd", grid=(HEADS, nb),
        in_specs=[big(QK_PAD), big(QK_PAD), big(VDIM), big(VDIM), rowv, rowv, mk(QK_PAD), mk(VDIM)]
        + [pl.BlockSpec(memory_space=pl.ANY)] * ne,
        out_specs=[big(QK_PAD), big(QK_PAD), big(VDIM), mo(QK_PAD), mo(VDIM)]
        + [pl.BlockSpec(memory_space=pl.ANY)] * ne,
        out_shape=[jax.ShapeDtypeStruct((nb, HEADS, s, QK_PAD), BF16),
                   jax.ShapeDtypeStruct((nb, HEADS, s, QK_PAD), BF16),
                   jax.ShapeDtypeStruct((nb, HEADS, s, VDIM), BF16),
                   jax.ShapeDtypeStruct((HEADS, N_META, QK_PAD), F32),
                   jax.ShapeDtypeStruct((HEADS, N_META, VDIM), F32)]
        + [jax.ShapeDtypeStruct(a.shape[1:], F32) for a in early],
        scratch_shapes=[pltpu.VMEM((2, t, s), BF16), pltpu.VMEM((2, t, s), BF16), pltpu.VMEM((s, QK_PAD), F32)]
        + [sc for red in reds for sc in red.scratch()],
        compiler_params=_cparams("arbitrary", "arbitrary"),
    )(q, k, v, do, lse, delta, km, vm, *early)


def _up_bwd(dq, dk, dv, dkm, dvm, p, pm, tabs, tabs_m, wq_p, wkv_p, gq, gkv, nb, s, tm):
    nt = s // tm
    n = nb * nt
    c_t, sa_t, sb_t = tabs
    cm_t, sam_t, sbm_t = tabs_m

    def kv_path(dkh, dvh, pa, c, sa, sb, wkv, gkvv):
        dkpe = dkh[0][:, NOPE:]
        for h in range(1, HEADS):
            dkpe = dkpe + dkh[h][:, NOPE:]
        dkr = _rope_bwd(dkpe, c, sa, sb)
        dkv = jnp.concatenate([d[:, :NOPE] for d in dkh] + list(dvh), axis=1).astype(BF16)
        ckv = pa[:, Q_RANK:Q_RANK + KV_RANK]
        kvn, rkv = _rms(ckv, gkvv)
        dckv, dg = _rms_bwd(_dot(dkv, wkv), ckv, rkv, gkvv)
        return dckv, dkr, kvn.astype(BF16), dkv, jnp.sum(dg, axis=0, keepdims=True)

    def body(dq_ref, dk_ref, dv_ref, pa_ref, c_ref, sa_ref, sb_ref,
             dkm_ref, dvm_ref, pam_ref, cm_ref, sam_ref, sbm_ref,
             wq_ref, wkv_ref, gq_ref, gkv_ref,
             dpa_ref, dpam_ref, pq_ref, pkv_ref, dgq_ref, dgkv_ref, dwq_ref, dwkv_ref):
        i = pl.program_id(0)

        @pl.when(i == 0)
        def _():
            dwq_ref[...] = jnp.zeros_like(dwq_ref)
            dwkv_ref[...] = jnp.zeros_like(dwkv_ref)
            dgq_ref[...] = jnp.zeros_like(dgq_ref)
            dgkv_ref[...] = jnp.zeros_like(dgkv_ref)

        @pl.when(i < n)
        def _():
            c, sa, sb = c_ref[...], sa_ref[...], sb_ref[...]
            pa = pa_ref[...]
            parts = []
            for h in range(HEADS):
                dqh = dq_ref[0, h].astype(F32) * ATTN_SCALE
                parts += [dqh[:, :NOPE], _rope_bwd(dqh[:, NOPE:], c, sa, sb)]
            dql = jnp.concatenate(parts, axis=1).astype(BF16)
            cq = pa[:, 0:Q_RANK]
            gqv = gq_ref[...]
            qn, rq = _rms(cq, gqv)
            dwq_ref[...] += _dot_tn(dql, qn.astype(BF16))
            dcq, dg = _rms_bwd(_dot(dql, wq_ref[...]), cq, rq, gqv)
            dgq_ref[...] += jnp.sum(dg, axis=0, keepdims=True)
            dckv, dkr, kvn, dkv, dgk = kv_path([dk_ref[0, h].astype(F32) for h in range(HEADS)],
                                               [dv_ref[0, h].astype(F32) for h in range(HEADS)],
                                               pa, c, sa, sb, wkv_ref[...], gkv_ref[...])
            dwkv_ref[...] += _dot_tn(dkv, kvn)
            dgkv_ref[...] += dgk
            dpa_ref[...] = jnp.concatenate([dcq, dckv, dkr], axis=1).astype(BF16)

        @pl.when(i == n)
        def _():
            dckv, dkr, kvn, dkv, dgk = kv_path([dkm_ref[h] for h in range(HEADS)],
                                               [dvm_ref[h] for h in range(HEADS)],
                                               pam_ref[...], cm_ref[...], sam_ref[...], sbm_ref[...],
                                               wkv_ref[...], gkv_ref[...])
            dwkv_ref[...] += _dot_tn(dkv, kvn)
            dgkv_ref[...] += dgk
            dpam_ref[...] = jnp.concatenate([jnp.zeros((N_META, Q_RANK), F32), dckv, dkr], axis=1)
            for h in range(HEADS):
                pq_ref[h] = dwq_ref[QK_PAD * h:QK_PAD * h + NOPE + ROPE, :]
                pkv_ref[h, 0:NOPE, :] = dwkv_ref[NOPE * h:NOPE * (h + 1), :]
                pkv_ref[h, NOPE:NOPE + VDIM, :] = dwkv_ref[512 + VDIM * h:512 + VDIM * (h + 1), :]

    cl = lambda i: jnp.minimum(i, n - 1)
    hb = lambda w: pl.BlockSpec((1, HEADS, tm, w), lambda i: (cl(i) // nt, 0, cl(i) % nt, 0))
    tab = pl.BlockSpec((tm, 128), lambda i: (cl(i) % nt, 0))
    full = lambda a: pl.BlockSpec(a.shape, lambda i: (0,) * a.ndim)
    const = lambda shape: pl.BlockSpec(shape, lambda i: (0,) * len(shape))
    return pl.pallas_call(
        body, name="up_bwd", grid=(n + 1,),
        in_specs=[hb(QK_PAD), hb(QK_PAD), hb(VDIM), pl.BlockSpec((tm, 512), lambda i: (cl(i), 0)), tab, tab, tab,
                  full(dkm), full(dvm), pl.BlockSpec((N_META, 512), lambda i: (0, 0)),
                  full(cm_t), full(sam_t), full(sbm_t), full(wq_p), full(wkv_p), full(gq), full(gkv)],
        out_specs=[pl.BlockSpec((tm, 512), lambda i: (cl(i), 0)), const((N_META, 512)),
                   const((HEADS, NOPE + ROPE, Q_RANK)), const((HEADS, NOPE + VDIM, KV_RANK)),
                   const((1, Q_RANK)), const((1, KV_RANK))],
        out_shape=[jax.ShapeDtypeStruct((nb * s, 512), BF16), jax.ShapeDtypeStruct((N_META, 512), F32),
                   jax.ShapeDtypeStruct((HEADS, NOPE + ROPE, Q_RANK), F32),
                   jax.ShapeDtypeStruct((HEADS, NOPE + VDIM, KV_RANK), F32),
                   jax.ShapeDtypeStruct((1, Q_RANK), F32), jax.ShapeDtypeStruct((1, KV_RANK), F32)],
        scratch_shapes=[pltpu.VMEM((HEADS * QK_PAD, Q_RANK), F32), pltpu.VMEM((1024, KV_RANK), F32)],
        compiler_params=_cparams("arbitrary"),
    )(dq, dk, dv, p, c_t, sa_t, sb_t, dkm, dvm, pm, cm_t, sam_t, sbm_t, wq_p, wkv_p, gq, gkv)


def _in_bwd(x2d, dh2, dpa, dpb, meta, dpam, dccm, pm, w_in_p, norm_g, nb, s, tm):
    nt = s // tm
    n = nb * nt

    def body(x_ref, dh_ref, dpa_ref, dpb_ref, mt_ref, dpam_ref, dccm_ref, mc_ref, mh_ref, w_ref, g_ref,
             gx_ref, gm_ref, dw_hbm, dg_ref, acc_ref, sems):
        i = pl.program_id(0)

        @pl.when(i == 0)
        def _():
            acc_ref[...] = jnp.zeros_like(acc_ref)
            dg_ref[...] = jnp.zeros_like(dg_ref)

        def rows(x, dp, dres):
            g = g_ref[...]
            dpb16 = dp.astype(BF16)
            du = _dot(dpb16, w_ref[...])
            u, r1 = _rms(x, g)
            acc_ref[...] += _dot_tn(dpb16, u.astype(BF16))
            dx, dg = _rms_bwd(du, x, r1, g)
            dg_ref[...] += jnp.sum(dg, axis=0, keepdims=True)
            return dx if dres is None else dx + dres

        @pl.when(i < n)
        def _():
            dp = jnp.concatenate([dpa_ref[...], dpb_ref[...]], axis=1)
            gx_ref[...] = rows(x_ref[...], dp, dh_ref[...])

        @pl.when(i == n)
        def _():
            dcc = dccm_ref[0]
            for b in range(1, nb):
                dcc = dcc + dccm_ref[b]
            z8 = jnp.zeros((8, CONV_W), F32)
            dc = jnp.concatenate([z8, dcc * mh_ref[8:16, :]], axis=0)
            dh = jnp.concatenate([z8, dcc * mc_ref[8:16, :]], axis=0)
            z = jnp.zeros((N_META, CONV_W), F32)
            dp = jnp.concatenate([dpam_ref[...], z, z, dc, dh, z], axis=1)
            gm_ref[...] = rows(mt_ref[...], dp, None)
            per = IN_DIM // 4
            cps = [pltpu.make_async_copy(acc_ref.at[0:448], dw_hbm.at[0, 0:448], sems.at[0]),
                   pltpu.make_async_copy(acc_ref.at[512:per + 64], dw_hbm.at[0, 448:per], sems.at[1])]
            for qq in range(1, 4):
                cps.append(pltpu.make_async_copy(acc_ref.at[per * qq + 64:per * (qq + 1) + 64], dw_hbm.at[qq],
                                                 sems.at[qq + 1]))
            for cp in cps:
                cp.start()
            for cp in cps:
                cp.wait()

    cl = lambda i: jnp.minimum(i, n - 1)
    row = lambda w: pl.BlockSpec((tm, w), lambda i: (cl(i), 0))
    full = lambda a: pl.BlockSpec(a.shape, lambda i: (0,) * a.ndim)
    mblk = lambda j: pl.BlockSpec((N_META, 512), lambda i: (0, j))
    return pl.pallas_call(
        body, name="in_bwd", grid=(n + 1,),
        in_specs=[row(D_MODEL), row(D_MODEL), row(512), row(2560), full(meta), full(dpam), full(dccm),
                  mblk(BLK_CC), mblk(BLK_CH), full(w_in_p), full(norm_g)],
        out_specs=[row(D_MODEL), pl.BlockSpec((N_META, D_MODEL), lambda i: (0, 0)),
                   pl.BlockSpec(memory_space=pl.ANY), pl.BlockSpec((1, D_MODEL), lambda i: (0, 0))],
        out_shape=[jax.ShapeDtypeStruct((nb * s, D_MODEL), F32), jax.ShapeDtypeStruct((N_META, D_MODEL), F32),
                   jax.ShapeDtypeStruct((4, IN_DIM // 4, D_MODEL), F32), jax.ShapeDtypeStruct((1, D_MODEL), F32)],
        scratch_shapes=[pltpu.VMEM((IN_PAD, D_MODEL), F32), pltpu.SemaphoreType.DMA((5,))],
        compiler_params=_cparams("arbitrary"),
    )(x2d, dh2, dpa, dpb, meta, dpam, dccm, pm, pm, w_in_p, norm_g)


def _gather_weights(w_in_shard, w_out_shard, split, pieces, out_rows, whole, zero_fills):
    ns, nw, nz = len(split), len(whole), len(zero_fills)
    flat = [(a, pc) for a in range(ns) for pc in pieces[a]]
    nk = len(flat)
    hh = HEAD_ROWS // 2
    assert hh % 16 == 0

    def body(*refs):
        ins, wins, zins = refs[2:2 + ns], refs[2 + ns:2 + ns + nw], refs[2 + ns + nw:2 + ns + nw + nz]
        n_in = 2 + ns + nw + nz
        head_ref, shard16, w_out16 = refs[n_in:n_in + 3]
        outs, wcat = refs[n_in + 3:n_in + 3 + ns], refs[n_in + 3 + ns:n_in + 3 + ns + nw]
        scr = refs[n_in + 3 + ns + nw:]
        stage, wouts = scr[:ns], scr[ns:ns + nw]
        (send_sems, recv_sems, fwd_send, fwd_recv, loc_sems, w_send, w_recv, w_loc, z_sems,
         h_send, h_recv, h_relay, h_pass) = scr[ns + nw:]
        x, y, c = lax.axis_index("x"), lax.axis_index("y"), lax.axis_index("c")
        mine = 2 * x + y
        chips = [(1 - x, y), (x, 1 - y), (1 - x, 1 - y)]
        chip_of = [2 * px + py for px, py in chips]
        shard16[...] = refs[0][...].astype(BF16)
        w_out16[...] = refs[1][...].astype(BF16)
        for a in range(ns):
            stage[a][...] = ins[a][...].astype(BF16)

        def head_rows(half):
            return pl.ds(pl.multiple_of(half * hh, 16), hh)

        def head_copy(turn):
            dest = (1 - c, c, c) if turn == 0 else (c, 1 - c, c)
            return pltpu.make_async_remote_copy(
                src_ref=shard16.at[head_rows(c)], dst_ref=head_ref.at[head_rows(c)], send_sem=h_send.at[turn],
                recv_sem=h_recv.at[0], device_id=dest, device_id_type=pl.DeviceIdType.MESH)

        def head_relay():
            ref = head_ref.at[head_rows(c)]
            return pltpu.make_async_remote_copy(
                src_ref=ref, dst_ref=ref, send_sem=h_relay.at[0], recv_sem=h_recv.at[0],
                device_id=(1, 1, c), device_id_type=pl.DeviceIdType.MESH)

        def head_pass(half):
            ref = head_ref.at[head_rows(half)]
            return pltpu.make_async_remote_copy(
                src_ref=ref, dst_ref=ref, send_sem=h_pass.at[0], recv_sem=h_pass.at[1],
                device_id=(x, y, 1 - c), device_id_type=pl.DeviceIdType.MESH)

        head_ref[HEAD_ROWS:IN_HEAD, :] = jnp.zeros((IN_HEAD - HEAD_ROWS, D_MODEL), BF16)

        @pl.when(mine == 0)
        def _():
            head_copy(0).start()
            head_ref[0:HEAD_ROWS, :] = shard16[0:HEAD_ROWS, :]

        def src(k):
            a, (s0, nr, _, _, _, _) = flat[k]
            return stage[a].at[s0:s0 + nr]

        def dst(k, q):
            a, (_, nr, per, first, rest, _) = flat[k]
            row = per * q + first + (rest - first) * jnp.minimum(q, 1)
            return outs[a].at[pl.ds(pl.multiple_of(row, 16), nr)]

        def ici(k, j, q):
            px, py = chips[j]
            return pltpu.make_async_remote_copy(
                src_ref=src(k), dst_ref=dst(k, q), send_sem=send_sems.at[k, j], recv_sem=recv_sems.at[k, j],
                device_id=(px, py, c), device_id_type=pl.DeviceIdType.MESH)

        def fwd(k, j):
            ref = dst(k, chip_of[j])
            return pltpu.make_async_remote_copy(
                src_ref=ref, dst_ref=ref, send_sem=fwd_send.at[k, j], recv_sem=fwd_recv.at[k, j],
                device_id=(x, y, 1 - c), device_id_type=pl.DeviceIdType.MESH)

        def wcopy(b, j, q):
            px, py = chips[j]
            return pltpu.make_async_remote_copy(
                src_ref=wins[b], dst_ref=wouts[b].at[q], send_sem=w_send.at[b, j], recv_sem=w_recv.at[b, j],
                device_id=(px, py, c), device_id_type=pl.DeviceIdType.MESH)

        local = [pltpu.make_async_copy(src(k), dst(k, mine), loc_sems.at[k]) for k in range(nk)]
        local += [pltpu.make_async_copy(wins[b], wouts[b].at[mine], w_loc.at[b]) for b in range(nw)]
        for z, (a, _, row0) in enumerate(zero_fills):
            local.append(pltpu.make_async_copy(zins[z], outs[a].at[row0:row0 + zins[z].shape[0]], z_sems.at[z]))
        wsends = [wcopy(b, j, mine) for b in range(nw) for j in range(3)]
        for cp in local + wsends:
            cp.start()

        for half in (0, 1):
            @pl.when(c == half)
            def _(half=half):
                my_k = [k for k in range(nk) if flat[k][1][5] == half]
                other_k = [k for k in range(nk) if flat[k][1][5] != half]
                sends = [ici(k, j, mine) for k in my_k for j in range(3)]
                for cp in sends:
                    cp.start()
                passed = []
                for k in my_k:
                    for j in range(3):
                        ici(k, j, chip_of[j]).wait_recv()
                        cp = fwd(k, j)
                        cp.start()
                        passed.append(cp)
                for k in other_k:
                    for j in range(3):
                        fwd(k, j).wait_recv()
                for cp in sends + passed:
                    cp.wait_send()

        for b in range(nw):
            for j in range(3):
                wcopy(b, j, chip_of[j]).wait_recv()
        for cp in wsends:
            cp.wait_send()
        for cp in local:
            cp.wait()
        for b in range(nw):
            cols = wins[b].shape[-1]
            for q in range(4):
                wcat[b][..., cols * q:cols * (q + 1)] = wouts[b][q]

        @pl.when(mine == 0)
        def _():
            head_copy(0).wait_send()
            head_copy(1).start()
            head_copy(1).wait_send()

        @pl.when(mine != 0)
        def _():
            hands_on = mine == 2 - c
            head_copy(0).wait_recv()

            @pl.when(hands_on)
            def _():
                head_relay().start()

            head_pass(c).start()
            head_pass(1 - c).wait_recv()
            head_pass(c).wait_send()

            @pl.when(hands_on)
            def _():
                head_relay().wait_send()

    vmem = pl.BlockSpec(memory_space=pltpu.VMEM)
    dma = pltpu.SemaphoreType.DMA
    zeros = [z for _, z, _ in zero_fills]
    return pl.pallas_call(
        body, name="gather_weights",
        in_specs=[vmem] * (2 + ns + nw + nz), out_specs=[vmem] * (3 + ns + nw),
        out_shape=([jax.ShapeDtypeStruct((IN_HEAD, D_MODEL), BF16), jax.ShapeDtypeStruct(w_in_shard.shape, BF16),
                    jax.ShapeDtypeStruct(w_out_shard.shape, BF16)]
                   + [jax.ShapeDtypeStruct((out_rows[a], split[a].shape[1]), BF16) for a in range(ns)]
                   + [jax.ShapeDtypeStruct(w.shape[:-1] + (4 * w.shape[-1],), w.dtype) for w in whole]),
        scratch_shapes=[pltpu.VMEM(a.shape, BF16) for a in split] + [pltpu.VMEM((4,) + w.shape, w.dtype) for w in whole]
        + [dma((nk, 3)), dma((nk, 3)), dma((nk, 3)), dma((nk, 3)), dma((nk,)),
           dma((nw, 3)), dma((nw, 3)), dma((nw,)), dma((nz,)),
           dma((2,)), dma((1,)), dma((1,)), dma((2,))],
        compiler_params=pltpu.CompilerParams(vmem_limit_bytes=VMEM_LIMIT),
    )(w_in_shard, w_out_shard, *split, *whole, *zeros)


def _reduce_grads(parts, small):
    n = len(parts)
    ns = len(small)
    shapes = [a.shape[1:] for a in parts]
    halves = [(sh[0] // 2, sh[1]) for sh in shapes]
    sm_blocks = [a.shape[1] // 128 for a, _ in small]
    sm_first = [sum(nr * nblk for (_, nr), nblk in zip(small[:k], sm_blocks[:k])) for k in range(ns)]
    sm_rows = -(-(sm_first[-1] + small[-1][1] * sm_blocks[-1]) // 8) * 8
    sm_shape = (sm_rows, 128)

    def body(*refs):
        pin, sm_in = refs[:n], refs[n:n + ns]
        gout, sm_out = refs[n + ns:2 * n + ns], refs[2 * n + ns]
        scr = refs[2 * n + ns + 1:]
        own, sib, wire, rbuf = scr[:n], scr[n:2 * n], scr[2 * n:3 * n], scr[3 * n:4 * n]
        (sbuf, send_sems, recv_sems, loc_sems, pre_send, pre_recv, post_send, post_recv,
         sm_send, sm_recv, sm_pack) = scr[4 * n:]
        x, y, c = lax.axis_index("x"), lax.axis_index("y"), lax.axis_index("c")
        mine = 2 * x + y
        sm_pack[...] = jnp.zeros(sm_shape, F32)
        for k, (_, nr) in enumerate(small):
            for i in range(nr):
                for j in range(sm_blocks[k]):
                    row = sm_first[k] + i * sm_blocks[k] + j
                    sm_pack[row:row + 1, :] = sm_in[k][i:i + 1, 128 * j:128 * (j + 1)]
        me = 4 * x + 2 * y + c
        sibling = (x, y, 1 - c)

        def rows(a, half):
            r2 = halves[a][0]
            return pl.ds(pl.multiple_of(half * r2, r2), r2)

        near = (jnp.where(c == 0, 1 - x, x), jnp.where(c == 0, y, 1 - y))
        far = (jnp.where(c == 0, x, 1 - x), jnp.where(c == 0, 1 - y, y))
        chip = lambda p: 2 * p[0] + p[1]
        blocks = [3 - mine, chip(near), chip(far), mine]

        def pre(a, k):
            q = blocks[k]
            return pltpu.make_async_remote_copy(
                src_ref=pin[a].at[q, rows(a, 1 - c), :], dst_ref=sib[a].at[q],
                send_sem=pre_send.at[a, q], recv_sem=pre_recv.at[a, q], device_id=sibling,
                device_id_type=pl.DeviceIdType.MESH)

        def ici(a, m):
            px, py = near if m < 2 else far
            return pltpu.make_async_remote_copy(
                src_ref=wire[a].at[blocks[m]], dst_ref=rbuf[a].at[m], send_sem=send_sems.at[a, m],
                recv_sem=recv_sems.at[a, m], device_id=(px, py, c), device_id_type=pl.DeviceIdType.MESH)

        def post(a, half):
            ref = gout[a].at[rows(a, half), :]
            return pltpu.make_async_remote_copy(
                src_ref=ref, dst_ref=ref, send_sem=post_send.at[a], recv_sem=post_recv.at[a],
                device_id=sibling, device_id_type=pl.DeviceIdType.MESH)

        def small_copy(kk):
            peer = (x ^ (kk >> 2), y ^ ((kk >> 1) & 1), c ^ (kk & 1))
            return pltpu.make_async_remote_copy(
                src_ref=sm_pack, dst_ref=sbuf.at[kk], send_sem=sm_send.at[kk - 1], recv_sem=sm_recv.at[kk - 1],
                device_id=peer, device_id_type=pl.DeviceIdType.MESH)

        local = [[pltpu.make_async_copy(pin[a].at[blocks[k], rows(a, c), :], own[a].at[blocks[k]], loc_sems.at[a, k])
                  for k in range(4)] for a in range(n)]
        pres = [[pre(a, k) for k in range(4)] for a in range(n)]
        smalls = [small_copy(kk) for kk in range(1, 8)]
        for a in range(n):
            for k in range(4):
                local[a][k].start()
                pres[a][k].start()
        for cp in smalls:
            cp.start()
        sbuf[0] = sm_pack[...]
        sends = []
        for a in range(n):
            for k in range(4):
                local[a][k].wait()
                pres[a][k].wait_recv()
                tot = own[a][blocks[k]] + sib[a][blocks[k]]
                if k == 2:
                    ici(a, 0).wait_recv()
                    tot = tot + rbuf[a][0].astype(F32)
                own[a][blocks[k]] = tot
                if k < 3:
                    wire[a][blocks[k]] = tot.astype(BF16)
                    cp = ici(a, k)
                    cp.start()
                    sends.append(cp)
        for cp in smalls:
            cp.wait_recv()
        total = sbuf[me]
        for d in range(1, 8):
            total = total + sbuf[me ^ d]
        sm_out[...] = total
        posts = []
        for a in range(n):
            fin = own[a][mine]
            for m in (1, 2):
                ici(a, m).wait_recv()
                fin = fin + rbuf[a][m].astype(F32)
            gout[a][rows(a, c), :] = fin
            cp = post(a, c)
            cp.start()
            posts.append(cp)
        for a in range(n):
            post(a, 1 - c).wait_recv()
        for cp in [cp for row in pres for cp in row] + sends + smalls + posts:
            cp.wait_send()

    vmem = pl.BlockSpec(memory_space=pltpu.VMEM)
    dma = pltpu.SemaphoreType.DMA
    return pl.pallas_call(
        body, name="reduce_grads",
        in_specs=[pl.BlockSpec(memory_space=pl.ANY)] * n + [vmem] * ns, out_specs=[vmem] * (n + 1),
        out_shape=[jax.ShapeDtypeStruct(sh, F32) for sh in shapes] + [jax.ShapeDtypeStruct(sm_shape, F32)],
        scratch_shapes=([pltpu.VMEM((4,) + hs, F32) for hs in halves] + [pltpu.VMEM((4,) + hs, F32) for hs in halves]
                        + [pltpu.VMEM((4,) + hs, BF16) for hs in halves]
                        + [pltpu.VMEM((3,) + hs, BF16) for hs in halves]
                        + [pltpu.VMEM((8,) + sm_shape, F32), dma((n, 3)), dma((n, 3)), dma((n, 4)),
                           dma((n, 4)), dma((n, 4)), dma((n,)), dma((n,)), dma((7,)), dma((7,)),
                           pltpu.VMEM(sm_shape, F32)]),
        compiler_params=pltpu.CompilerParams(vmem_limit_bytes=VMEM_LIMIT),
    )(*parts, *[a for a, _ in small])


def _adamw_update(w_ref, g_ref, m_ref, v_ref, d_ref, nm_ref, nv_ref):
    gv = g_ref[...]
    nm = ADAM_B1 * m_ref[...] + (1.0 - ADAM_B1) * gv
    nv = ADAM_B2 * v_ref[...] + (1.0 - ADAM_B2) * (gv * gv)
    m_hat = nm / (1.0 - ADAM_B1 ** ADAM_STEP)
    v_hat = nv / (1.0 - ADAM_B2 ** ADAM_STEP)
    d_ref[...] = -ADAM_LR * (m_hat / (jnp.sqrt(v_hat) + ADAM_EPS) + ADAM_WD * w_ref[...])
    nm_ref[...] = nm
    nv_ref[...] = nv


def _adamw_small(ws, gs, ms, vs):
    k = len(ws)

    def body(*refs):
        ins, outs = refs[:4 * k], refs[4 * k:]
        for a in range(k):
            _adamw_update(ins[a], ins[k + a], ins[2 * k + a], ins[3 * k + a], outs[a], outs[k + a], outs[2 * k + a])

    out = pl.pallas_call(
        body, name="adamw_small",
        out_shape=[jax.ShapeDtypeStruct(w.shape, F32) for w in ws] * 3,
        compiler_params=pltpu.CompilerParams(vmem_limit_bytes=VMEM_LIMIT),
    )(*ws, *gs, *ms, *vs)
    return out[:k], out[k:2 * k], out[2 * k:]


def _adamw(w, g, m, v, name):
    shape = w.shape
    w2, g2, m2, v2 = (a.reshape((-1, shape[-1])) for a in (w, g, m, v))

    def body(w_ref, g_ref, m_ref, v_ref, d_ref, nm_ref, nv_ref):
        _adamw_update(w_ref, g_ref, m_ref, v_ref, d_ref, nm_ref, nv_ref)

    rows, cols = w2.shape
    nblk = cols // 256 if cols % 256 == 0 and rows >= 64 else 1
    blk = pl.BlockSpec((rows, cols // nblk), lambda j: (0, j))
    out = pl.pallas_call(
        body, name=name, grid=(nblk,), in_specs=[blk] * 4, out_specs=[blk] * 3,
        out_shape=[jax.ShapeDtypeStruct(w2.shape, F32)] * 3,
        compiler_params=_cparams("parallel"),
    )(w2, g2, m2, v2)
    return tuple(a.reshape(shape) for a in out)


def kernel(x, meta_tokens, norm_g, w_in, q_norm_g, w_q_up, kv_norm_g, w_kv_up, conv_w, attn_out_g, conv_out_g, w_out, final_norm_g, loss_target, m_meta_tokens, m_norm_g, m_w_in, m_q_norm_g, m_w_q_up, m_kv_norm_g, m_w_kv_up, m_conv_w, m_attn_out_g, m_conv_out_g, m_w_out, m_final_norm_g, v_meta_tokens, v_norm_g, v_w_in, v_q_norm_g, v_w_q_up, v_kv_norm_g, v_w_kv_up, v_conv_w, v_attn_out_g, v_conv_out_g, v_w_out, v_final_norm_g):
    nb, s, _ = x.shape
    tm = min(ROW_TILE, s)
    ta = min(ATTN_TILE, s)
    assert s % tm == 0 and s % ta == 0 and tm % 16 == 0
    r = nb * s

    tr = lambda a: jnp.transpose(a[0])
    w_head, w_in_shard, w_out_shard, wq_p, wkv_p, g_cw, meta_f = _gather_weights(
        tr(w_in), w_out[0], [tr(w_q_up), tr(w_kv_up)],
        [W_Q_PIECES, W_KV_PIECES], [HEADS * QK_PAD, 1024],
        [jnp.transpose(conv_w, (1, 0, 2)), meta_tokens],
        [(0, jnp.zeros((64, Q_RANK), BF16), QK_PAD * h + NOPE + ROPE) for h in range(HEADS)])
    conv_f = g_cw.reshape(3, CONV_W)

    c_all, sa_all, sb_all = _rope_tables(N_META + s)
    tabs_m = (c_all[:N_META], sa_all[:N_META], sb_all[:N_META])
    tabs = (c_all[N_META:], sa_all[N_META:], sb_all[N_META:])
    gid = np.arange(CONV_W) // CONV_GROUP
    gmat = jnp.asarray(np.where(gid[:, None] == gid[None, :], 1.0 / CONV_GROUP, 0.0), BF16)
    ga, gc = attn_out_g, conv_out_g
    gf = final_norm_g.reshape(1, D_MODEL)

    x2d = x.reshape(r, D_MODEL)
    tgt2d = loss_target.reshape(r, D_MODEL)

    ph, q, k, v, pmh, km, vm, w_out_f, w_in_part = _fwd_proj(
        x2d, meta_f, tabs, tabs_m, norm_g, w_head, q_norm_g, wq_p, kv_norm_g, wkv_p, w_out_shard, w_in_shard,
        nb, s, tm)
    o, lse, w_in_p = _attn_fwd(q, k, v, km, vm, w_in_shard, w_in_part, nb, s, ta)
    dh2, dycat, dw_out, dgf, loss_acc, pt, pmt = _out_fwd_bwd(x2d, tgt2d, o, meta_f, norm_g, w_in_p, conv_f, ga, gc,
                                                              gmat, w_out_f, gf, nb, s, tm)
    dpb, do, delta, dccm, dga, dgc, dcw = _gate_bwd(dycat, o, pt, pmt, conv_f, ga, gc, gmat, nb, s, tm)
    p_out = dw_out.reshape(4, D_MODEL // 4, D_MODEL)
    dq, dk, dv, dkm, dvm, g_w_out = _attn_bwd(q, k, v, do, lse, delta, km, vm, [p_out], nb, s, ta)
    dpa, dpam, p_q, p_kv, dgq, dgkv = _up_bwd(dq, dk, dv, dkm, dvm, ph, pmh, tabs, tabs_m, wq_p, wkv_p,
                                              q_norm_g, kv_norm_g, nb, s, tm)
    gx, gmeta, p_in, dng = _in_bwd(x2d, dh2, dpa, dpb, meta_f, dpam, dccm, pmt, w_in_p, norm_g, nb, s, tm)

    g_w_in_t, g_w_q_t, g_w_kv_t, small_sum = _reduce_grads(
        [p_in, p_q, p_kv],
        [(dng, 1), (dgq, 1), (dgkv, 1), (dga, 1), (dgc, 1), (dgf, 1), (dcw, 3), (gmeta, N_META), (loss_acc, 1)])
    ssum = small_sum.reshape(-1)

    def take(off, n):
        return ssum[off:off + n], off + n

    off = 0
    g_norm, off = take(off, D_MODEL)
    g_qn, off = take(off, Q_RANK)
    g_kvn, off = take(off, KV_RANK)
    g_ga, off = take(off, CONV_W)
    g_gc, off = take(off, CONV_W)
    g_gf, off = take(off, D_MODEL)
    g_cw_all, off = take(off, 3 * CONV_W)
    g_meta_all, off = take(off, N_META * D_MODEL)
    loss = ssum[off]
    chip = 2 * lax.axis_index("x") + lax.axis_index("y")
    g_conv = lax.dynamic_slice(g_cw_all.reshape(3, CONV_W), (0, chip * 128), (3, 128))
    g_mt = lax.dynamic_slice(g_meta_all.reshape(N_META, D_MODEL), (0, chip * 256), (N_META, 256))

    grads = {
        "meta_tokens": g_mt, "norm_g": g_norm.reshape(1, -1), "w_in": g_w_in_t, "q_norm_g": g_qn.reshape(1, -1),
        "w_q_up": g_w_q_t, "kv_norm_g": g_kvn.reshape(1, -1), "w_kv_up": jnp.transpose(g_w_kv_t)[None],
        "conv_w": g_conv[None], "attn_out_g": g_ga.reshape(1, -1), "conv_out_g": g_gc.reshape(1, -1),
        "w_out": g_w_out[None], "final_norm_g": g_gf,
    }
    transposed = ("w_in", "w_q_up")
    weights = {
        "meta_tokens": (meta_tokens, m_meta_tokens, v_meta_tokens), "norm_g": (norm_g, m_norm_g, v_norm_g),
        "w_in": (w_in, m_w_in, v_w_in), "q_norm_g": (q_norm_g, m_q_norm_g, v_q_norm_g),
        "w_q_up": (w_q_up, m_w_q_up, v_w_q_up), "kv_norm_g": (kv_norm_g, m_kv_norm_g, v_kv_norm_g),
        "w_kv_up": (w_kv_up, m_w_kv_up, v_w_kv_up), "conv_w": (conv_w, m_conv_w, v_conv_w),
        "attn_out_g": (attn_out_g, m_attn_out_g, v_attn_out_g), "conv_out_g": (conv_out_g, m_conv_out_g, v_conv_out_g),
        "w_out": (w_out, m_w_out, v_w_out), "final_norm_g": (final_norm_g, m_final_norm_g, v_final_norm_g),
    }
    names = list(weights)
    small = [nme for nme in names if nme != "w_in"]

    def view(nme, a):
        if nme in transposed:
            return a if a.ndim == 2 else tr(a)
        if nme == "conv_w":
            return jnp.transpose(a.reshape(1, 3, -1), (1, 0, 2))
        if a.ndim == 3:
            return a[0]
        return a.reshape(1, -1) if a.ndim == 1 else a

    def unview(nme, a):
        if nme in transposed:
            return jnp.transpose(a)[None]
        if nme == "conv_w":
            return jnp.transpose(a, (1, 0, 2))
        return a.reshape(weights[nme][0].shape)

    res_small = _adamw_small(*[[view(nme, a) for nme, a in zip(small, col)] for col in (
        [weights[nme][0] for nme in small], [grads[nme] for nme in small],
        [weights[nme][1] for nme in small], [weights[nme][2] for nme in small])])
    w_, m_, v_ = weights["w_in"]
    res = _adamw(tr(w_), grads["w_in"], tr(m_), tr(v_), "adamw_w_in")
    upd = {"w_in": tuple(jnp.transpose(a)[None] for a in (grads["w_in"],) + res)}
    for j, nme in enumerate(small):
        upd[nme] = (unview(nme, view(nme, grads[nme])),) + tuple(unview(nme, r[j]) for r in res_small)
    grads = {nme: upd[nme][0] for nme in names}
    deltas, new_m, new_v = ([upd[nme][j] for nme in names] for j in (1, 2, 3))

    grad_x = gx.reshape(nb, s, D_MODEL)
    return (loss, grad_x, *[grads[nme] for nme in names], *deltas, *new_m, *new_v)
```

```python
import functools

import jax
import jax.numpy as jnp
import numpy as np
from jax import lax
from jax.experimental import pallas as pl
from jax.experimental.pallas import tpu as pltpu

F32 = jnp.float32
BF16 = jnp.bfloat16

D_MODEL = 1024
N_META = 16
HEADS = 4
NOPE = 128
ROPE = 64
VDIM = 128
QK_PAD = 256
Q_RANK = 256
KV_RANK = 128
CONV_W = 512
CONV_GROUP = 64
ROPE_THETA = 10000.0
EPS = 1e-6
ATTN_SCALE = (NOPE + ROPE) ** -0.5
IN_DIM = 3008
IN_PAD = 3072
HEAD_ROWS = Q_RANK + KV_RANK + ROPE
IN_HEAD = 512
IN_TAIL = IN_PAD - IN_HEAD
BLK_ZA, BLK_CB, BLK_CC, BLK_CH, BLK_ZC = 0, 1, 2, 3, 4
NEG_INF = -1e30

ADAM_LR = 0.001
ADAM_B1 = 0.9
ADAM_B2 = 0.999
ADAM_EPS = 1e-08
ADAM_WD = 0.01
ADAM_STEP = 10

ROW_TILE = 512
ATTN_TILE = 256
VMEM_LIMIT = 56 * 1024 * 1024

NT = (((1,), (1,)), ((), ()))
TN = (((0,), (0,)), ((), ()))


def _cparams(*sem):
    return pltpu.CompilerParams(dimension_semantics=sem, vmem_limit_bytes=VMEM_LIMIT)


def _dot(a, b):
    return jnp.dot(a, b, preferred_element_type=F32)


def _dot_nt(a, b):
    return lax.dot_general(a, b, NT, preferred_element_type=F32)


def _dot_tn(a, b):
    return lax.dot_general(a, b, TN, preferred_element_type=F32)


def _rms(x, g):
    r = lax.rsqrt(jnp.mean(x * x, axis=-1, keepdims=True) + EPS)
    return x * r * g, r


def _rms_bwd(dy, x, r, g):
    xh = x * r
    dyg = dy * g
    dx = r * (dyg - xh * jnp.mean(dyg * xh, axis=-1, keepdims=True))
    return dx, dy * xh


def _sigmoid(z):
    return 1.0 / (1.0 + jnp.exp(-z))


def _rope(b, c, sa, sb):
    return b * c + pltpu.roll(b, 96, 1) * sa + pltpu.roll(b, 32, 1) * sb


def _rope_bwd(d, c, sa, sb):
    return d * c + pltpu.roll(d * sa, 32, 1) + pltpu.roll(d * sb, 96, 1)


def _group_mean(x, gmat):
    hi = x.astype(BF16)
    lo = (x - hi.astype(F32)).astype(BF16)
    return _dot(hi, gmat) + _dot(lo, gmat)


def _row_of(col, rows):
    return jnp.transpose(jnp.broadcast_to(col, (rows, 128)))[0:1, :]


def _rope_tables(n_pos):
    half = ROPE // 2
    inv_freq = (np.float32(1.0) / (np.float32(ROPE_THETA) ** (np.arange(half, dtype=np.float32) / np.float32(half))))
    ang = np.arange(n_pos, dtype=np.float32)[:, None] * inv_freq.astype(np.float32)[None, :]
    cos, sin = np.cos(ang).astype(np.float32), np.sin(ang).astype(np.float32)
    z = np.zeros((n_pos, half), np.float32)
    c = np.concatenate([cos, cos, z, z], axis=1)
    sa = np.concatenate([-sin, z, z, z], axis=1)
    sb = np.concatenate([z, sin, z, z], axis=1)
    return jnp.asarray(c), jnp.asarray(sa), jnp.asarray(sb)


W_IN_PIECES_1 = ((0, 80, 752, 0, 64, 0), (384, 64, 752, 384, 448, 1), (448, 16, 752, 512, 512, 1))
W_IN_PIECES_2 = ((80, 304, 752, 80, 144, 0), (464, 288, 752, 528, 528, 1))
W_Q_PIECES = ((0, 96, 256, 0, 0, 0), (96, 96, 256, 96, 96, 1))
W_KV_PIECES = ((0, 128, 128, 0, 0, 0), (128, 128, 128, 512, 512, 1))
W_OUT_PIECES = ((0, 128, 256, 0, 0, 0), (128, 128, 256, 128, 128, 1))


class _StagedGather:
    STAGES = 4

    @staticmethod
    def steps(n_steps):
        return (0, 5 * n_steps // 8, 7 * n_steps // 8, n_steps - 1)

    def __init__(self, pieces, zero_rows=None):
        self.pieces = pieces
        self.zero_rows = zero_rows

    def scratch(self):
        nk, dma = len(self.pieces), pltpu.SemaphoreType.DMA
        return [dma((nk, 3)), dma((nk, 3)), dma((nk, 3)), dma((nk, 3)), dma((nk,))]

    def vmem_scratch(self, shard_shape, out_shape):
        return [pltpu.VMEM(shard_shape, BF16), pltpu.VMEM(out_shape, BF16),
                pltpu.SemaphoreType.DMA((4 * len(self.pieces) + 1,)),
                pltpu.SemaphoreType.DMA((len(self.pieces),))] + self.scratch()

    def run_vmem(self, stage, shard_ref, out_ref, scr):
        src_scr, land_scr, io_sems, ld_sems = scr[:4]
        spans = []
        for _, nr, per, first, rest, _ in self.pieces:
            spans += [(per * q + (first if q == 0 else rest), nr) for q in range(4)]
        if self.zero_rows is not None:
            spans.append(self.zero_rows)
        flush = [pltpu.make_async_copy(land_scr.at[r0:r0 + nr], out_ref.at[r0:r0 + nr], io_sems.at[n])
                 for n, (r0, nr) in enumerate(spans)]
        if stage == 0:
            loads = [pltpu.make_async_copy(shard_ref.at[s0:s0 + nr], src_scr.at[s0:s0 + nr], ld_sems.at[k])
                     for k, (s0, nr, _, _, _, _) in enumerate(self.pieces)]
            for cp in loads:
                cp.start()
            if self.zero_rows is not None:
                r0, nr = self.zero_rows
                land_scr[r0:r0 + nr, :] = jnp.zeros((nr, land_scr.shape[1]), BF16)
            for cp in loads:
                cp.wait()
        if stage < self.STAGES:
            self.run(stage, src_scr, land_scr, scr[4:])
        for cp in flush:
            if stage == self.STAGES - 1:
                cp.start()
            if stage == self.STAGES:
                cp.wait()

    def run(self, stage, src_ref, out_ref, scr):
        send_sems, recv_sems, fwd_send, fwd_recv, loc_sems = scr
        pieces = self.pieces
        nk = len(pieces)
        x, y, c = lax.axis_index("x"), lax.axis_index("y"), lax.axis_index("c")
        mine = 2 * x + y
        chips = [(1 - x, y), (x, 1 - y), (1 - x, 1 - y)]
        chip_of = [2 * px + py for px, py in chips]
        mesh = pl.DeviceIdType.MESH

        def src(k):
            s0, nr = pieces[k][0], pieces[k][1]
            return src_ref.at[s0:s0 + nr]

        def dst(k, q):
            _, nr, per, first, rest, _ = pieces[k]
            row = per * q + first + (rest - first) * jnp.minimum(q, 1)
            return out_ref.at[pl.ds(pl.multiple_of(row, 16), nr)]

        def ici(k, j, q):
            px, py = chips[j]
            return pltpu.make_async_remote_copy(
                src_ref=src(k), dst_ref=dst(k, q), send_sem=send_sems.at[k, j], recv_sem=recv_sems.at[k, j],
                device_id=(px, py, c), device_id_type=mesh)

        def fwd(k, j):
            ref = dst(k, chip_of[j])
            return pltpu.make_async_remote_copy(
                src_ref=ref, dst_ref=ref, send_sem=fwd_send.at[k, j], recv_sem=fwd_recv.at[k, j],
                device_id=(x, y, 1 - c), device_id_type=mesh)

        def relay(k, half):
            ref = dst(k, chip_of[half])
            px, py = chips[1 - half]
            return pltpu.make_async_remote_copy(
                src_ref=ref, dst_ref=ref, send_sem=send_sems.at[k, 2], recv_sem=recv_sems.at[k, 2],
                device_id=(px, py, c), device_id_type=mesh)

        local = [pltpu.make_async_copy(src(k), dst(k, mine), loc_sems.at[k]) for k in range(nk)]
        if stage == 0:
            for cp in local:
                cp.start()
        if stage == 3:
            for cp in local:
                cp.wait()
        for half in (0, 1):
            @pl.when(c == half)
            def _(half=half):
                my_k = [k for k in range(nk) if pieces[k][5] == half]
                other_k = [k for k in range(nk) if pieces[k][5] != half]
                for k in my_k:
                    if stage == 0:
                        for j in range(2):
                            ici(k, j, mine).start()
                    elif stage == 1:
                        for j in (half, 1 - half):
                            ici(k, j, chip_of[j]).wait_recv()
                            if j == half:
                                relay(k, half).start()
                            fwd(k, j).start()
                    elif stage == 2:
                        ici(k, 2, chip_of[2]).wait_recv()
                        fwd(k, 2).start()
                    else:
                        for j in range(2):
                            ici(k, j, mine).wait_send()
                        relay(k, half).wait_send()
                        for j in range(3):
                            fwd(k, j).wait_send()
                if stage == 3:
                    for k in other_k:
                        for j in range(3):
                            fwd(k, j).wait_recv()


def _fwd_proj(x2d, meta, tabs, tabs_m, norm_g, w_head, q_norm_g, wq_p, kv_norm_g, wkv_p, w_out_shard, w_in_shard,
              nb, s, tm):
    nt = s // tm
    n = nb * nt
    n_steps = n + 1
    c_t, sa_t, sb_t = tabs
    cm_t, sam_t, sbm_t = tabs_m
    gat = _StagedGather(W_OUT_PIECES)
    gat_in = _StagedGather(W_IN_PIECES_1)
    n_sems = len(gat.scratch())
    assert n_steps >= 3

    def body(x_ref, c_ref, sa_ref, sb_ref, mt_ref, cm_ref, sam_ref, sbm_ref,
             g_ref, w_ref, gq_ref, wq_ref, gkv_ref, wkv_ref, wos_ref, wis_ref,
             p_ref, q_ref, k_ref, v_ref, pm_ref, km_ref, vm_ref, wo_ref, wi_ref, *scr):
        gat_scr, gat_in_scr = scr[:n_sems], scr[n_sems:]
        i = pl.program_id(0)
        for stage, at in enumerate(_StagedGather.steps(n_steps)):
            @pl.when(i == at)
            def _(stage=stage):
                gat_in.run_vmem(stage, wis_ref, wi_ref, gat_in_scr)
                gat.run(stage, wos_ref, wo_ref, gat_scr)

        def project(xv, c, sa, sb, p_out, q_out, k_out, v_out):
            u, _ = _rms(xv, g_ref[...])
            p = _dot_nt(u.astype(BF16), w_ref[...])
            p_out[...] = p
            qn, _ = _rms(p[:, 0:Q_RANK], gq_ref[...])
            q = _dot_nt(qn.astype(BF16), wq_ref[...])
            kvn, _ = _rms(p[:, Q_RANK:Q_RANK + KV_RANK], gkv_ref[...])
            kv = _dot_nt(kvn.astype(BF16), wkv_ref[...])
            kpe = _rope(p[:, 384:512], c, sa, sb)
            for h in range(HEADS):
                if q_out is not None:
                    pe = _rope(q[:, QK_PAD * h + NOPE:QK_PAD * (h + 1)], c, sa, sb)
                    qh = jnp.concatenate([q[:, QK_PAD * h:QK_PAD * h + NOPE], pe], axis=1)
                    q_out[0, h] = (qh * ATTN_SCALE).astype(BF16)
                k_out[0, h] = jnp.concatenate([kv[:, NOPE * h:NOPE * (h + 1)], kpe], axis=1).astype(BF16)
                v_out[0, h] = kv[:, 512 + VDIM * h:512 + VDIM * (h + 1)].astype(BF16)

        @pl.when(i < n)
        def _():
            project(x_ref[...], c_ref[...], sa_ref[...], sb_ref[...], p_ref, q_ref, k_ref, v_ref)

        @pl.when(i == n)
        def _():
            project(mt_ref[...], cm_ref[...], sam_ref[...], sbm_ref[...], pm_ref, None, km_ref, vm_ref)
            gat_in.run_vmem(gat_in.STAGES, wis_ref, wi_ref, gat_in_scr)

    cl = lambda i: jnp.minimum(i, n - 1)
    full = lambda a: pl.BlockSpec(a.shape, lambda i: (0,) * a.ndim)
    const = lambda shape: pl.BlockSpec(shape, lambda i: (0,) * len(shape))
    tab = pl.BlockSpec((tm, 128), lambda i: (cl(i) % nt, 0))
    hb = lambda w: pl.BlockSpec((1, HEADS, tm, w), lambda i: (cl(i) // nt, 0, cl(i) % nt, 0))
    whole = pl.BlockSpec(memory_space=pl.ANY)
    return pl.pallas_call(
        body, name="fwd_proj", grid=(n_steps,),
        in_specs=[pl.BlockSpec((tm, D_MODEL), lambda i: (cl(i), 0)), tab, tab, tab,
                  full(meta), full(cm_t), full(sam_t), full(sbm_t),
                  full(norm_g), full(w_head), full(q_norm_g), full(wq_p), full(kv_norm_g), full(wkv_p), whole, whole],
        out_specs=[pl.BlockSpec((tm, IN_HEAD), lambda i: (cl(i), 0)), hb(QK_PAD), hb(QK_PAD), hb(VDIM),
                   const((N_META, IN_HEAD)), const((1, HEADS, N_META, QK_PAD)), const((1, HEADS, N_META, VDIM)),
                   whole, whole],
        out_shape=[jax.ShapeDtypeStruct((nb * s, IN_HEAD), F32),
                   jax.ShapeDtypeStruct((nb, HEADS, s, QK_PAD), BF16),
                   jax.ShapeDtypeStruct((nb, HEADS, s, QK_PAD), BF16),
                   jax.ShapeDtypeStruct((nb, HEADS, s, VDIM), BF16),
                   jax.ShapeDtypeStruct((N_META, IN_HEAD), F32),
                   jax.ShapeDtypeStruct((1, HEADS, N_META, QK_PAD), BF16),
                   jax.ShapeDtypeStruct((1, HEADS, N_META, VDIM), BF16),
                   jax.ShapeDtypeStruct((D_MODEL, D_MODEL), BF16),
                   jax.ShapeDtypeStruct((IN_PAD, D_MODEL), BF16)],
        scratch_shapes=gat.scratch() + gat_in.vmem_scratch(w_in_shard.shape, (IN_PAD, D_MODEL)),
        compiler_params=_cparams("arbitrary"),
    )(x2d, c_t, sa_t, sb_t, meta, cm_t, sam_t, sbm_t, norm_g, w_head, q_norm_g, wq_p, kv_norm_g, wkv_p, w_out_shard,
      w_in_shard)


def _attn_fwd(q, k, v, km, vm, w_in_shard, w_in_part, nb, s, tq):
    nq = s // tq
    n_steps = nb * HEADS
    gat = _StagedGather(W_IN_PIECES_2, zero_rows=(HEAD_ROWS, IN_HEAD - HEAD_ROWS))
    assert n_steps >= 3

    def body(q_ref, k_ref, v_ref, km_ref, vm_ref, ws_ref, _, o_ref, lse_ref, w_ref, s_scr, p_scr, *gat_scr):
        step = pl.program_id(0) * HEADS + pl.program_id(1)
        for stage, at in enumerate(_StagedGather.steps(n_steps)):
            @pl.when(step == at)
            def _(stage=stage):
                gat.run_vmem(stage, ws_ref, w_ref, gat_scr)

        row = lax.broadcasted_iota(jnp.int32, (tq, tq), 0)
        col = lax.broadcasted_iota(jnp.int32, (tq, tq), 1)
        def scores(i):
            slot = i % 2
            qi = q_ref[0, 0, i * tq:(i + 1) * tq, :]
            sm = _dot_nt(qi, km_ref[0, 0])
            m128 = None
            for j in range(i + 1):
                sc = _dot_nt(qi, k_ref[0, 0, j * tq:(j + 1) * tq, :])
                if j == i:
                    sc = jnp.where(col <= row, sc, NEG_INF)
                s_scr[slot, :, j * tq:(j + 1) * tq] = sc
                mx = sc[:, 0:128]
                for c0 in range(128, tq, 128):
                    mx = jnp.maximum(mx, sc[:, c0:c0 + 128])
                m128 = mx if m128 is None else jnp.maximum(m128, mx)
            return sm, jnp.maximum(jnp.max(m128, axis=1, keepdims=True), jnp.max(sm, axis=1, keepdims=True))

        def weighted_sum(i, pm, l):
            n = (i + 1) * tq
            acc = _dot(p_scr[i % 2, :, 0:n], v_ref[0, 0, 0:n, :]) + _dot(pm.astype(BF16), vm_ref[0, 0])
            o_ref[0, 0, i * tq:(i + 1) * tq, :] = acc / l

        nxt, pending = scores(0), None
        for i in range(nq):
            slot = i % 2
            sm, m = nxt
            if i + 1 < nq:
                nxt = scores(i + 1)
            pm = jnp.exp(sm - m)
            l128 = None
            for j in range(i + 1):
                p = jnp.exp(s_scr[slot, :, j * tq:(j + 1) * tq] - m)
                p_scr[slot, :, j * tq:(j + 1) * tq] = p.astype(BF16)
                ps = p[:, 0:128]
                for c0 in range(128, tq, 128):
                    ps = ps + p[:, c0:c0 + 128]
                l128 = ps if l128 is None else l128 + ps
            l = jnp.sum(l128, axis=1, keepdims=True) + jnp.sum(pm, axis=1, keepdims=True)
            lse_ref[0, 0, :, i * tq:(i + 1) * tq] = _row_of(m + jnp.log(l), tq)
            if pending is not None:
                weighted_sum(*pending)
            pending = (i, pm, l)
        weighted_sum(*pending)

        @pl.when(step == n_steps - 1)
        def _():
            gat.run_vmem(gat.STAGES, ws_ref, w_ref, gat_scr)

    hblk = lambda w: pl.BlockSpec((1, 1, s, w), lambda b, h: (b, h, 0, 0))
    mblk = lambda w: pl.BlockSpec((1, 1, N_META, w), lambda b, h: (0, h, 0, 0))
    whole = pl.BlockSpec(memory_space=pl.ANY)
    return pl.pallas_call(
        body, name="attn_fwd", grid=(nb, HEADS),
        in_specs=[hblk(QK_PAD), hblk(QK_PAD), hblk(VDIM), mblk(QK_PAD), mblk(VDIM), whole, whole],
        out_specs=[hblk(VDIM), pl.BlockSpec((1, 1, 1, s), lambda b, h: (b, h, 0, 0)), whole],
        out_shape=[jax.ShapeDtypeStruct((nb, HEADS, s, VDIM), F32),
                   jax.ShapeDtypeStruct((nb, HEADS, 1, s), F32),
                   jax.ShapeDtypeStruct(w_in_part.shape, BF16)],
        input_output_aliases={6: 2},
        scratch_shapes=[pltpu.VMEM((2, tq, s), F32), pltpu.VMEM((2, tq, s), BF16)]
        + gat.vmem_scratch(w_in_shard.shape, w_in_part.shape),
        compiler_params=_cparams("arbitrary", "arbitrary"),
    )(q, k, v, km, vm, w_in_shard, w_in_part)


def _shift_rows(a, prev, n_rows):
    rid = lax.broadcasted_iota(jnp.int32, a.shape, 0)
    a1 = jnp.where(rid == 0, prev[7:8, :], pltpu.roll(a, 1, 0))
    a2 = jnp.where(rid == 0, prev[6:7, :], jnp.where(rid == 1, prev[7:8, :], pltpu.roll(a, 2, 0)))
    return a1, a2


def _attn_gate(o, za, ga_h):
    on, r = _rms(o, ga_h)
    return on * (za * _sigmoid(za)), on, r


def _out_fwd_bwd(x2d, tgt2d, o, meta, norm_g, w_in_p, conv_w, ga, gc, gmat, w_out, gf, nb, s, tm):
    nt = s // tm
    r = nb * s

    def body(x_ref, t_ref, o_ref, mt_ref, g_ref, wi_ref, cw_ref, ga_ref, gc_ref, gm_ref, w_ref, gf_ref,
             dh_ref, dy_ref, dw_ref, dgf_ref, loss_ref, p_ref, pm_ref, last_cc):
        i = pl.program_id(0)
        blk = lambda ref, j, rows=slice(None): ref[rows, 512 * j:512 * (j + 1)]

        def tail(xv):
            u, _ = _rms(xv, g_ref[...])
            return _dot_nt(u.astype(BF16), wi_ref[IN_HEAD:IN_PAD, :])

        @pl.when(i == 0)
        def _():
            dw_ref[...] = jnp.zeros_like(dw_ref)
            dgf_ref[...] = jnp.zeros_like(dgf_ref)
            loss_ref[...] = jnp.zeros_like(loss_ref)
            last_cc[...] = jnp.zeros_like(last_cc)
            pm_ref[...] = tail(mt_ref[...])

        u16 = _rms(x_ref[...], g_ref[...])[0].astype(BF16)

        def project(j):
            p_ref[:, 512 * j:512 * (j + 1)] = _dot_nt(u16, wi_ref[IN_HEAD + 512 * j:IN_HEAD + 512 * (j + 1), :])

        project(BLK_ZA)
        project(BLK_CC)
        project(BLK_CH)
        ya = []
        for h in range(HEADS):
            y, _, _ = _attn_gate(o_ref[0, h], p_ref[:, 512 * BLK_ZA + VDIM * h:512 * BLK_ZA + VDIM * (h + 1)],
                                 ga_ref[:, VDIM * h:VDIM * (h + 1)])
            ya.append(y)
        project(BLK_CB)
        project(BLK_ZC)
        cc = blk(p_ref, BLK_CC) * blk(p_ref, BLK_CH)
        meta_cc = blk(pm_ref, BLK_CC, slice(8, 16)) * blk(pm_ref, BLK_CH, slice(8, 16))
        prev = jnp.where(i % nt == 0, meta_cc, last_cc[...])
        last_cc[...] = cc[tm - 8:tm, :]
        cc1, cc2 = _shift_rows(cc, prev, tm)
        yc = blk(p_ref, BLK_CB) * (cw_ref[0:1, :] * cc2 + cw_ref[1:2, :] * cc1 + cw_ref[2:3, :] * cc)
        rg = lax.rsqrt(_group_mean(yc * yc, gm_ref[...]) + EPS)
        zc = blk(p_ref, BLK_ZC)
        yconv = yc * rg * gc_ref[...] * (zc * _sigmoid(zc))
        ycat = jnp.concatenate(ya + [yconv], axis=1).astype(BF16)
        h2 = x_ref[...] + _dot(ycat, w_ref[...])
        gfv = gf_ref[...]
        y, r2 = _rms(h2, gfv)
        e = y - t_ref[...]
        loss_ref[...] += 0.5 * jnp.sum(e * e) / D_MODEL
        dyv = e * (1.0 / D_MODEL)
        dh2, dgf = _rms_bwd(dyv, h2, r2, gfv)
        dgf_ref[...] += jnp.sum(dgf, axis=0, keepdims=True)
        dh_ref[...] = dh2
        dhb = dh2.astype(BF16)
        dy_ref[...] = _dot_nt(dhb, w_ref[...])
        dw_ref[...] += _dot_tn(ycat, dhb)

    row = lambda w: pl.BlockSpec((tm, w), lambda i: (i, 0))
    const = lambda shape: pl.BlockSpec(shape, lambda i: (0,) * len(shape))
    full = lambda a: const(a.shape)
    return pl.pallas_call(
        body, name="out_fwd_bwd", grid=(nb * nt,),
        in_specs=[row(D_MODEL), row(D_MODEL),
                  pl.BlockSpec((1, HEADS, tm, VDIM), lambda i: (i // nt, 0, i % nt, 0)),
                  full(meta), full(norm_g), full(w_in_p),
                  full(conv_w), full(ga), full(gc), full(gmat), full(w_out), full(gf)],
        out_specs=[row(D_MODEL), row(D_MODEL), const((D_MODEL, D_MODEL)), const((1, D_MODEL)), const((1, 128)),
                   row(IN_TAIL), const((N_META, IN_TAIL))],
        out_shape=[jax.ShapeDtypeStruct((r, D_MODEL), F32), jax.ShapeDtypeStruct((r, D_MODEL), F32),
                   jax.ShapeDtypeStruct((D_MODEL, D_MODEL), F32), jax.ShapeDtypeStruct((1, D_MODEL), F32),
                   jax.ShapeDtypeStruct((1, 128), F32),
                   jax.ShapeDtypeStruct((r, IN_TAIL), F32), jax.ShapeDtypeStruct((N_META, IN_TAIL), F32)],
        scratch_shapes=[pltpu.VMEM((8, 512), F32)],
        compiler_params=_cparams("arbitrary"),
    )(x2d, tgt2d, o, meta, norm_g, w_in_p, conv_w, ga, gc, gmat, w_out, gf)


def _gate_bwd(dycat, o, p, pm, conv_w, ga, gc, gmat, nb, s, tm):
    nt = s // tm
    r = nb * s
    ext = tm + 8
    prev_idx = lambda i: jnp.maximum(i * (tm // 8) - 1, 0)
    next_idx = lambda i: jnp.minimum((i + 1) * (tm // 8), r // 8 - 1)

    def body(dya_ref, dyc_ref, dycn_ref, o_ref, za_ref, cb_ref, cbn_ref, cc_ref, ccp_ref, ccn_ref,
             ch_ref, chp_ref, chn_ref, zc_ref, zcn_ref, mc_ref, mh_ref, cw_ref, ga_ref, gc_ref, gm_ref,
             dpb_ref, do_ref, dl_ref, dccm_ref, dga_ref, dgc_ref, dcw_ref):
        i = pl.program_id(0)

        @pl.when(i == 0)
        def _():
            dga_ref[...] = jnp.zeros_like(dga_ref)
            dgc_ref[...] = jnp.zeros_like(dgc_ref)
            dcw_ref[...] = jnp.zeros_like(dcw_ref)

        dga = []
        for h in range(HEADS):
            hs = slice(VDIM * h, VDIM * (h + 1))
            oh, za, gah, dya = o_ref[0, h], za_ref[:, hs], ga_ref[:, hs], dya_ref[:, hs]
            sg = _sigmoid(za)
            on, ro = _rms(oh, gah)
            don = dya * (za * sg)
            dpb_ref[:, hs] = (dya * on * (sg * (1.0 + za * (1.0 - sg)))).astype(BF16)
            do, dg = _rms_bwd(don, oh, ro, gah)
            dga.append(jnp.sum(dg, axis=0, keepdims=True))
            dob = do.astype(BF16)
            do_ref[0, h] = dob
            dl_ref[0, h] = _row_of(jnp.sum(dob.astype(F32) * oh, axis=1, keepdims=True), tm)
        dga_ref[...] += jnp.concatenate(dga, axis=1)

        cat = lambda a, b: jnp.concatenate([a[...], b[...]], axis=0)
        cch = cat(cc_ref, ccn_ref)
        chh = cat(ch_ref, chn_ref)
        cb = cat(cb_ref, cbn_ref)
        zc = cat(zc_ref, zcn_ref)
        dy = cat(dyc_ref, dycn_ref)
        first = i % nt == 0
        last = i % nt == nt - 1
        cc = cch * chh
        prev = jnp.where(first, mc_ref[8:16, :] * mh_ref[8:16, :], ccp_ref[...] * chp_ref[...])
        cc1, cc2 = _shift_rows(cc, prev, ext)
        w0, w1, w2 = cw_ref[0:1, :], cw_ref[1:2, :], cw_ref[2:3, :]
        dw = w0 * cc2 + w1 * cc1 + w2 * cc
        yc = cb * dw
        rg = lax.rsqrt(_group_mean(yc * yc, gm_ref[...]) + EPS)
        ych = yc * rg
        gcv = gc_ref[...]
        sg = _sigmoid(zc)
        dycn = dy * (zc * sg)
        dzc = dy * (ych * gcv) * (sg * (1.0 + zc * (1.0 - sg)))
        dgc_ref[...] += jnp.sum((dycn * ych)[:tm], axis=0, keepdims=True)
        dycg = dycn * gcv
        dyc = rg * (dycg - ych * _group_mean(dycg * ych, gm_ref[...]))
        rid = lax.broadcasted_iota(jnp.int32, (ext, CONV_W), 0)
        ddw = jnp.where(jnp.logical_and(last, rid >= tm), 0.0, dyc * cb)
        dcb = dyc * dw
        dcc = w2 * ddw + w1 * pltpu.roll(ddw, ext - 1, 0) + w0 * pltpu.roll(ddw, ext - 2, 0)
        dpb_ref[:, 512:1024] = dcb[:tm].astype(BF16)
        dpb_ref[:, 1024:1536] = (dcc * chh)[:tm].astype(BF16)
        dpb_ref[:, 1536:2048] = (dcc * cch)[:tm].astype(BF16)
        dpb_ref[:, 2048:2560] = dzc[:tm].astype(BF16)
        rs = lambda a: jnp.sum(a[:tm], axis=0, keepdims=True)
        dcw_ref[0:1, :] += rs(ddw * cc2)
        dcw_ref[1:2, :] += rs(ddw * cc1)
        dcw_ref[2:3, :] += rs(ddw * cc)

        @pl.when(first)
        def _():
            d0, d1 = ddw[0:1, :], ddw[1:2, :]
            r8 = lax.broadcasted_iota(jnp.int32, (8, CONV_W), 0)
            dccm_ref[0] = jnp.where(r8 == 7, w1 * d0 + w0 * d1, jnp.where(r8 == 6, w0 * d0, 0.0))

    row = lambda j: pl.BlockSpec((tm, 512), lambda i: (i, j))
    prv = lambda j: pl.BlockSpec((8, 512), lambda i: (prev_idx(i), j))
    nxt = lambda j: pl.BlockSpec((8, 512), lambda i: (next_idx(i), j))
    mblk = lambda j: pl.BlockSpec((N_META, 512), lambda i: (0, j))
    full = lambda a: pl.BlockSpec(a.shape, lambda i: (0,) * a.ndim)
    hb = lambda w: pl.BlockSpec((1, HEADS, tm, w), lambda i: (i // nt, 0, i % nt, 0))
    acc = lambda rr: pl.BlockSpec((rr, 512), lambda i: (0, 0))
    return pl.pallas_call(
        body, name="gate_bwd", grid=(nb * nt,),
        in_specs=[row(0), row(1), nxt(1), hb(VDIM),
                  row(BLK_ZA), row(BLK_CB), nxt(BLK_CB), row(BLK_CC), prv(BLK_CC), nxt(BLK_CC),
                  row(BLK_CH), prv(BLK_CH), nxt(BLK_CH), row(BLK_ZC), nxt(BLK_ZC),
                  mblk(BLK_CC), mblk(BLK_CH), full(conv_w), full(ga), full(gc), full(gmat)],
        out_specs=[pl.BlockSpec((tm, 2560), lambda i: (i, 0)), hb(VDIM),
                   pl.BlockSpec((1, HEADS, 1, tm), lambda i: (i // nt, 0, 0, i % nt)),
                   pl.BlockSpec((1, 8, 512), lambda i: (i // nt, 0, 0)),
                   acc(1), acc(1), acc(8)],
        out_shape=[jax.ShapeDtypeStruct((r, 2560), BF16), jax.ShapeDtypeStruct((nb, HEADS, s, VDIM), BF16),
                   jax.ShapeDtypeStruct((nb, HEADS, 1, s), F32), jax.ShapeDtypeStruct((nb, 8, 512), F32),
                   jax.ShapeDtypeStruct((1, 512), F32), jax.ShapeDtypeStruct((1, 512), F32),
                   jax.ShapeDtypeStruct((8, 512), F32)],
        compiler_params=_cparams("arbitrary"),
    )(dycat, dycat, dycat, o, p, p, p, p, p, p, p, p, p, p, p, pm, pm, conv_w, ga, gc, gmat)


class _StagedReduce:
    LOC, PRE_S, PRE_R, ICI_S, ICI_R, POST_S, POST_R, OUT, N_SEM = 0, 1, 2, 3, 6, 9, 10, 11, 12

    def __init__(self, shard_shape):
        self.half = (shard_shape[0] // 2, shard_shape[1])

    def scratch(self):
        h = self.half
        return [pltpu.VMEM((4,) + h, F32), pltpu.VMEM((4,) + h, F32), pltpu.VMEM((4,) + h, BF16),
                pltpu.VMEM((3,) + h, BF16), pltpu.VMEM(h, F32), pltpu.SemaphoreType.DMA((self.N_SEM,))]

    def run(self, stage, pin, gout, scr):
        own, sib, wire, rbuf, fin, sems = scr
        r2 = self.half[0]
        x, y, c = lax.axis_index("x"), lax.axis_index("y"), lax.axis_index("c")
        mine = 2 * x + y
        sibling = (x, y, 1 - c)
        chips = [(1 - x, y), (x, 1 - y), (1 - x, 1 - y)]
        rows = lambda half: pl.ds(pl.multiple_of(half * r2, r2), r2)
        mesh = pl.DeviceIdType.MESH

        loc = pltpu.make_async_copy(pin.at[:, rows(c), :], own, sems.at[self.LOC])
        pre = pltpu.make_async_remote_copy(
            src_ref=pin.at[:, rows(1 - c), :], dst_ref=sib, send_sem=sems.at[self.PRE_S],
            recv_sem=sems.at[self.PRE_R], device_id=sibling, device_id_type=mesh)

        def ici(j):
            px, py = chips[j]
            return pltpu.make_async_remote_copy(
                src_ref=wire.at[2 * px + py], dst_ref=rbuf.at[j], send_sem=sems.at[self.ICI_S + j],
                recv_sem=sems.at[self.ICI_R + j], device_id=(px, py, c), device_id_type=mesh)

        def post(half):
            return pltpu.make_async_remote_copy(
                src_ref=fin, dst_ref=gout.at[rows(half), :], send_sem=sems.at[self.POST_S],
                recv_sem=sems.at[self.POST_R], device_id=sibling, device_id_type=mesh)

        keep = pltpu.make_async_copy(fin, gout.at[rows(c), :], sems.at[self.OUT])
        if stage == 0:
            loc.start()
            pre.start()
        elif stage == 1:
            loc.wait()
            pre.wait_recv()
            for blk in range(4):
                tot = own[blk] + sib[blk]
                own[blk] = tot
                wire[blk] = tot.astype(BF16)
            for j in range(3):
                ici(j).start()
        elif stage == 2:
            for j in range(3):
                ici(j).wait_recv()
            tot = own[mine]
            for j in range(3):
                tot = tot + rbuf[j].astype(F32)
            fin[...] = tot
            post(c).start()
            keep.start()
        else:
            post(1 - c).wait_recv()
            pre.wait_send()
            for j in range(3):
                ici(j).wait_send()
            post(c).wait_send()
            keep.wait()


def _attn_bwd(q, k, v, do, lse, delta, km, vm, early, nb, s, t):
    n = s // t
    ne = len(early)
    reds = [_StagedReduce(a.shape[1:]) for a in early]
    n_steps = HEADS * nb
    assert n_steps >= 4

    def body(q_ref, k_ref, v_ref, do_ref, lse_ref, dl_ref, km_ref, vm_ref, *rest):
        pin_refs, rest = rest[:ne], rest[ne:]
        dq_ref, dk_ref, dv_ref, dkm_ref, dvm_ref = rest[:5]
        gout_refs, (p_scr, ds_scr, dq_acc), red_scr = rest[5:5 + ne], rest[5 + ne:8 + ne], rest[8 + ne:]
        b = pl.program_id(1)
        step = pl.program_id(0) * nb + b
        for stage, at in enumerate((0, 1, n_steps - 2, n_steps - 1)):
            @pl.when(step == at)
            def _(stage=stage):
                for a, red in enumerate(reds):
                    red.run(stage, pin_refs[a], gout_refs[a], red_scr[6 * a:6 * a + 6])

        @pl.when(b == 0)
        def _():
            dkm_ref[...] = jnp.zeros_like(dkm_ref)
            dvm_ref[...] = jnp.zeros_like(dvm_ref)

        kr = lax.broadcasted_iota(jnp.int32, (t, t), 0)
        qc = lax.broadcasted_iota(jnp.int32, (t, t), 1)
        km_v, vm_v = km_ref[0, 0], vm_ref[0, 0]
        ptm = jnp.exp(_dot_nt(km_v, q_ref[0, 0]) - lse_ref[0, 0])
        dstm = (ptm * (_dot_nt(vm_v, do_ref[0, 0]) - dl_ref[0, 0])).astype(BF16)
        dkm_ref[0] += _dot(dstm, q_ref[0, 0])
        dvm_ref[0] += _dot(ptm.astype(BF16), do_ref[0, 0])
        dq_acc[...] = _dot_tn(dstm, km_v)
        def tiles(j):
            slot = j % 2
            kj = k_ref[0, 0, j * t:(j + 1) * t, :]
            vj = v_ref[0, 0, j * t:(j + 1) * t, :]
            def products(i):
                cs = slice(i * t, (i + 1) * t)
                return _dot_nt(kj, q_ref[0, 0, cs, :]), _dot_nt(vj, do_ref[0, 0, cs, :])

            nxt, pending = products(j), None
            for i in range(j, n):
                cs = slice(i * t, (i + 1) * t)
                st, dpt = nxt
                if i + 1 < n:
                    nxt = products(i + 1)
                if i == j:
                    st = jnp.where(kr <= qc, st, NEG_INF)
                pt = jnp.exp(st - lse_ref[0, 0, :, cs])
                dst = (pt * (dpt - dl_ref[0, 0, :, cs])).astype(BF16)
                p_scr[slot, :, cs] = pt.astype(BF16)
                ds_scr[slot, :, cs] = dst
                if pending is not None:
                    dq_acc[pending[0], :] += _dot_tn(pending[1], kj)
                pending = (cs, dst)
            dq_acc[pending[0], :] += _dot_tn(pending[1], kj)

        for j in range(n):
            slot = j % 2
            tiles(j)
            dv_ref[0, 0, j * t:(j + 1) * t, :] = _dot(p_scr[slot, :, j * t:s], do_ref[0, 0, j * t:s, :]).astype(BF16)
            dk_ref[0, 0, j * t:(j + 1) * t, :] = _dot(ds_scr[slot, :, j * t:s], q_ref[0, 0, j * t:s, :]).astype(BF16)
        dq_ref[0, 0] = dq_acc[...].astype(BF16)

    big = lambda w: pl.BlockSpec((1, 1, s, w), lambda h, b: (b, h, 0, 0))
    rowv = pl.BlockSpec((1, 1, 1, s), lambda h, b: (b, h, 0, 0))
    mk = lambda w: pl.BlockSpec((1, 1, N_META, w), lambda h, b: (0, h, 0, 0))
    mo = lambda w: pl.BlockSpec((1, N_META, w), lambda h, b: (h, 0, 0))
    return pl.pallas_call(
        body, name="attn_bwd", grid=(HEADS, nb),
        in_specs=[big(QK_PAD), big(QK_PAD), big(VDIM), big(VDIM), rowv, rowv, mk(QK_PAD), mk(VDIM)]
        + [pl.BlockSpec(memory_space=pl.ANY)] * ne,
        out_specs=[big(QK_PAD), big(QK_PAD), big(VDIM), mo(QK_PAD), mo(VDIM)]
        + [pl.BlockSpec(memory_space=pl.ANY)] * ne,
        out_shape=[jax.ShapeDtypeStruct((nb, HEADS, s, QK_PAD), BF16),
                   jax.ShapeDtypeStruct((nb, HEADS, s, QK_PAD), BF16),
                   jax.ShapeDtypeStruct((nb, HEADS, s, VDIM), BF16),
                   jax.ShapeDtypeStruct((HEADS, N_META, QK_PAD), F32),
                   jax.ShapeDtypeStruct((HEADS, N_META, VDIM), F32)]
        + [jax.ShapeDtypeStruct(a.shape[1:], F32) for a in early],
        scratch_shapes=[pltpu.VMEM((2, t, s), BF16), pltpu.VMEM((2, t, s), BF16), pltpu.VMEM((s, QK_PAD), F32)]
        + [sc for red in reds for sc in red.scratch()],
        compiler_params=_cparams("arbitrary", "arbitrary"),
    )(q, k, v, do, lse, delta, km, vm, *early)


def _up_bwd(dq, dk, dv, dkm, dvm, p, pm, tabs, tabs_m, wq_p, wkv_p, gq, gkv, nb, s, tm):
    nt = s // tm
    n = nb * nt
    c_t, sa_t, sb_t = tabs
    cm_t, sam_t, sbm_t = tabs_m

    def kv_path(dkh, dvh, pa, c, sa, sb, wkv, gkvv):
        dkpe = dkh[0][:, NOPE:]
        for h in range(1, HEADS):
            dkpe = dkpe + dkh[h][:, NOPE:]
        dkr = _rope_bwd(dkpe, c, sa, sb)
        dkv = jnp.concatenate([d[:, :NOPE] for d in dkh] + list(dvh), axis=1).astype(BF16)
        ckv = pa[:, Q_RANK:Q_RANK + KV_RANK]
        kvn, rkv = _rms(ckv, gkvv)
        dckv, dg = _rms_bwd(_dot(dkv, wkv), ckv, rkv, gkvv)
        return dckv, dkr, kvn.astype(BF16), dkv, jnp.sum(dg, axis=0, keepdims=True)

    def body(dq_ref, dk_ref, dv_ref, pa_ref, c_ref, sa_ref, sb_ref,
             dkm_ref, dvm_ref, pam_ref, cm_ref, sam_ref, sbm_ref,
             wq_ref, wkv_ref, gq_ref, gkv_ref,
             dpa_ref, dpam_ref, pq_ref, pkv_ref, dgq_ref, dgkv_ref, dwq_ref, dwkv_ref):
        i = pl.program_id(0)

        @pl.when(i == 0)
        def _():
            dwq_ref[...] = jnp.zeros_like(dwq_ref)
            dwkv_ref[...] = jnp.zeros_like(dwkv_ref)
            dgq_ref[...] = jnp.zeros_like(dgq_ref)
            dgkv_ref[...] = jnp.zeros_like(dgkv_ref)

        @pl.when(i < n)
        def _():
            c, sa, sb = c_ref[...], sa_ref[...], sb_ref[...]
            pa = pa_ref[...]
            parts = []
            for h in range(HEADS):
                dqh = dq_ref[0, h].astype(F32) * ATTN_SCALE
                parts += [dqh[:, :NOPE], _rope_bwd(dqh[:, NOPE:], c, sa, sb)]
            dql = jnp.concatenate(parts, axis=1).astype(BF16)
            cq = pa[:, 0:Q_RANK]
            gqv = gq_ref[...]
            qn, rq = _rms(cq, gqv)
            dwq_ref[...] += _dot_tn(dql, qn.astype(BF16))
            dcq, dg = _rms_bwd(_dot(dql, wq_ref[...]), cq, rq, gqv)
            dgq_ref[...] += jnp.sum(dg, axis=0, keepdims=True)
            dckv, dkr, kvn, dkv, dgk = kv_path([dk_ref[0, h].astype(F32) for h in range(HEADS)],
                                               [dv_ref[0, h].astype(F32) for h in range(HEADS)],
                                               pa, c, sa, sb, wkv_ref[...], gkv_ref[...])
            dwkv_ref[...] += _dot_tn(dkv, kvn)
            dgkv_ref[...] += dgk
            dpa_ref[...] = jnp.concatenate([dcq, dckv, dkr], axis=1).astype(BF16)

        @pl.when(i == n)
        def _():
            dckv, dkr, kvn, dkv, dgk = kv_path([dkm_ref[h] for h in range(HEADS)],
                                               [dvm_ref[h] for h in range(HEADS)],
                                               pam_ref[...], cm_ref[...], sam_ref[...], sbm_ref[...],
                                               wkv_ref[...], gkv_ref[...])
            dwkv_ref[...] += _dot_tn(dkv, kvn)
            dgkv_ref[...] += dgk
            dpam_ref[...] = jnp.concatenate([jnp.zeros((N_META, Q_RANK), F32), dckv, dkr], axis=1)
            for h in range(HEADS):
                pq_ref[h] = dwq_ref[QK_PAD * h:QK_PAD * h + NOPE + ROPE, :]
                pkv_ref[h, 0:NOPE, :] = dwkv_ref[NOPE * h:NOPE * (h + 1), :]
                pkv_ref[h, NOPE:NOPE + VDIM, :] = dwkv_ref[512 + VDIM * h:512 + VDIM * (h + 1), :]

    cl = lambda i: jnp.minimum(i, n - 1)
    hb = lambda w: pl.BlockSpec((1, HEADS, tm, w), lambda i: (cl(i) // nt, 0, cl(i) % nt, 0))
    tab = pl.BlockSpec((tm, 128), lambda i: (cl(i) % nt, 0))
    full = lambda a: pl.BlockSpec(a.shape, lambda i: (0,) * a.ndim)
    const = lambda shape: pl.BlockSpec(shape, lambda i: (0,) * len(shape))
    return pl.pallas_call(
        body, name="up_bwd", grid=(n + 1,),
        in_specs=[hb(QK_PAD), hb(QK_PAD), hb(VDIM), pl.BlockSpec((tm, 512), lambda i: (cl(i), 0)), tab, tab, tab,
                  full(dkm), full(dvm), pl.BlockSpec((N_META, 512), lambda i: (0, 0)),
                  full(cm_t), full(sam_t), full(sbm_t), full(wq_p), full(wkv_p), full(gq), full(gkv)],
        out_specs=[pl.BlockSpec((tm, 512), lambda i: (cl(i), 0)), const((N_META, 512)),
                   const((HEADS, NOPE + ROPE, Q_RANK)), const((HEADS, NOPE + VDIM, KV_RANK)),
                   const((1, Q_RANK)), const((1, KV_RANK))],
        out_shape=[jax.ShapeDtypeStruct((nb * s, 512), BF16), jax.ShapeDtypeStruct((N_META, 512), F32),
                   jax.ShapeDtypeStruct((HEADS, NOPE + ROPE, Q_RANK), F32),
                   jax.ShapeDtypeStruct((HEADS, NOPE + VDIM, KV_RANK), F32),
                   jax.ShapeDtypeStruct((1, Q_RANK), F32), jax.ShapeDtypeStruct((1, KV_RANK), F32)],
        scratch_shapes=[pltpu.VMEM((HEADS * QK_PAD, Q_RANK), F32), pltpu.VMEM((1024, KV_RANK), F32)],
        compiler_params=_cparams("arbitrary"),
    )(dq, dk, dv, p, c_t, sa_t, sb_t, dkm, dvm, pm, cm_t, sam_t, sbm_t, wq_p, wkv_p, gq, gkv)


def _in_bwd(x2d, dh2, dpa, dpb, meta, dpam, dccm, pm, w_in_p, norm_g, nb, s, tm):
    nt = s // tm
    n = nb * nt

    def body(x_ref, dh_ref, dpa_ref, dpb_ref, mt_ref, dpam_ref, dccm_ref, mc_ref, mh_ref, w_ref, g_ref,
             gx_ref, gm_ref, dw_hbm, dg_ref, acc_ref, sems):
        i = pl.program_id(0)

        @pl.when(i == 0)
        def _():
            acc_ref[...] = jnp.zeros_like(acc_ref)
            dg_ref[...] = jnp.zeros_like(dg_ref)

        def rows(x, dp, dres):
            g = g_ref[...]
            dpb16 = dp.astype(BF16)
            du = _dot(dpb16, w_ref[...])
            u, r1 = _rms(x, g)
            acc_ref[...] += _dot_tn(dpb16, u.astype(BF16))
            dx, dg = _rms_bwd(du, x, r1, g)
            dg_ref[...] += jnp.sum(dg, axis=0, keepdims=True)
            return dx if dres is None else dx + dres

        @pl.when(i < n)
        def _():
            dp = jnp.concatenate([dpa_ref[...], dpb_ref[...]], axis=1)
            gx_ref[...] = rows(x_ref[...], dp, dh_ref[...])

        @pl.when(i == n)
        def _():
            dcc = dccm_ref[0]
            for b in range(1, nb):
                dcc = dcc + dccm_ref[b]
            z8 = jnp.zeros((8, CONV_W), F32)
            dc = jnp.concatenate([z8, dcc * mh_ref[8:16, :]], axis=0)
            dh = jnp.concatenate([z8, dcc * mc_ref[8:16, :]], axis=0)
            z = jnp.zeros((N_META, CONV_W), F32)
            dp = jnp.concatenate([dpam_ref[...], z, z, dc, dh, z], axis=1)
            gm_ref[...] = rows(mt_ref[...], dp, None)
            per = IN_DIM // 4
            cps = [pltpu.make_async_copy(acc_ref.at[0:448], dw_hbm.at[0, 0:448], sems.at[0]),
                   pltpu.make_async_copy(acc_ref.at[512:per + 64], dw_hbm.at[0, 448:per], sems.at[1])]
            for qq in range(1, 4):
                cps.append(pltpu.make_async_copy(acc_ref.at[per * qq + 64:per * (qq + 1) + 64], dw_hbm.at[qq],
                                                 sems.at[qq + 1]))
            for cp in cps:
                cp.start()
            for cp in cps:
                cp.wait()

    cl = lambda i: jnp.minimum(i, n - 1)
    row = lambda w: pl.BlockSpec((tm, w), lambda i: (cl(i), 0))
    full = lambda a: pl.BlockSpec(a.shape, lambda i: (0,) * a.ndim)
    mblk = lambda j: pl.BlockSpec((N_META, 512), lambda i: (0, j))
    return pl.pallas_call(
        body, name="in_bwd", grid=(n + 1,),
        in_specs=[row(D_MODEL), row(D_MODEL), row(512), row(2560), full(meta), full(dpam), full(dccm),
                  mblk(BLK_CC), mblk(BLK_CH), full(w_in_p), full(norm_g)],
        out_specs=[row(D_MODEL), pl.BlockSpec((N_META, D_MODEL), lambda i: (0, 0)),
                   pl.BlockSpec(memory_space=pl.ANY), pl.BlockSpec((1, D_MODEL), lambda i: (0, 0))],
        out_shape=[jax.ShapeDtypeStruct((nb * s, D_MODEL), F32), jax.ShapeDtypeStruct((N_META, D_MODEL), F32),
                   jax.ShapeDtypeStruct((4, IN_DIM // 4, D_MODEL), F32), jax.ShapeDtypeStruct((1, D_MODEL), F32)],
        scratch_shapes=[pltpu.VMEM((IN_PAD, D_MODEL), F32), pltpu.SemaphoreType.DMA((5,))],
        compiler_params=_cparams("arbitrary"),
    )(x2d, dh2, dpa, dpb, meta, dpam, dccm, pm, pm, w_in_p, norm_g)


def _gather_weights(w_in_shard, w_out_shard, split, pieces, out_rows, whole, zero_fills):
    ns, nw, nz = len(split), len(whole), len(zero_fills)
    flat = [(a, pc) for a in range(ns) for pc in pieces[a]]
    nk = len(flat)
    hh = HEAD_ROWS // 2
    assert hh % 16 == 0

    def body(*refs):
        ins, wins, zins = refs[2:2 + ns], refs[2 + ns:2 + ns + nw], refs[2 + ns + nw:2 + ns + nw + nz]
        n_in = 2 + ns + nw + nz
        head_ref, shard16, w_out16 = refs[n_in:n_in + 3]
        outs, wcat = refs[n_in + 3:n_in + 3 + ns], refs[n_in + 3 + ns:n_in + 3 + ns + nw]
        scr = refs[n_in + 3 + ns + nw:]
        stage, wouts = scr[:ns], scr[ns:ns + nw]
        (send_sems, recv_sems, fwd_send, fwd_recv, loc_sems, w_send, w_recv, w_loc, z_sems,
         h_send, h_recv, h_relay, h_pass) = scr[ns + nw:]
        x, y, c = lax.axis_index("x"), lax.axis_index("y"), lax.axis_index("c")
        mine = 2 * x + y
        chips = [(1 - x, y), (x, 1 - y), (1 - x, 1 - y)]
        chip_of = [2 * px + py for px, py in chips]
        shard16[...] = refs[0][...].astype(BF16)
        w_out16[...] = refs[1][...].astype(BF16)
        for a in range(ns):
            stage[a][...] = ins[a][...].astype(BF16)

        def head_rows(half):
            return pl.ds(pl.multiple_of(half * hh, 16), hh)

        def head_copy(turn):
            dest = (1 - c, c, c) if turn == 0 else (c, 1 - c, c)
            return pltpu.make_async_remote_copy(
                src_ref=shard16.at[head_rows(c)], dst_ref=head_ref.at[head_rows(c)], send_sem=h_send.at[turn],
                recv_sem=h_recv.at[0], device_id=dest, device_id_type=pl.DeviceIdType.MESH)

        def head_relay():
            ref = head_ref.at[head_rows(c)]
            return pltpu.make_async_remote_copy(
                src_ref=ref, dst_ref=ref, send_sem=h_relay.at[0], recv_sem=h_recv.at[0],
                device_id=(1, 1, c), device_id_type=pl.DeviceIdType.MESH)

        def head_pass(half):
            ref = head_ref.at[head_rows(half)]
            return pltpu.make_async_remote_copy(
                src_ref=ref, dst_ref=ref, send_sem=h_pass.at[0], recv_sem=h_pass.at[1],
                device_id=(x, y, 1 - c), device_id_type=pl.DeviceIdType.MESH)

        head_ref[HEAD_ROWS:IN_HEAD, :] = jnp.zeros((IN_HEAD - HEAD_ROWS, D_MODEL), BF16)

        @pl.when(mine == 0)
        def _():
            head_copy(0).start()
            head_ref[0:HEAD_ROWS, :] = shard16[0:HEAD_ROWS, :]

        def src(k):
            a, (s0, nr, _, _, _, _) = flat[k]
            return stage[a].at[s0:s0 + nr]

        def dst(k, q):
            a, (_, nr, per, first, rest, _) = flat[k]
            row = per * q + first + (rest - first) * jnp.minimum(q, 1)
            return outs[a].at[pl.ds(pl.multiple_of(row, 16), nr)]

        def ici(k, j, q):
            px, py = chips[j]
            return pltpu.make_async_remote_copy(
                src_ref=src(k), dst_ref=dst(k, q), send_sem=send_sems.at[k, j], recv_sem=recv_sems.at[k, j],
                device_id=(px, py, c), device_id_type=pl.DeviceIdType.MESH)

        def fwd(k, j):
            ref = dst(k, chip_of[j])
            return pltpu.make_async_remote_copy(
                src_ref=ref, dst_ref=ref, send_sem=fwd_send.at[k, j], recv_sem=fwd_recv.at[k, j],
                device_id=(x, y, 1 - c), device_id_type=pl.DeviceIdType.MESH)

        def wcopy(b, j, q):
            px, py = chips[j]
            return pltpu.make_async_remote_copy(
                src_ref=wins[b], dst_ref=wouts[b].at[q], send_sem=w_send.at[b, j], recv_sem=w_recv.at[b, j],
                device_id=(px, py, c), device_id_type=pl.DeviceIdType.MESH)

        local = [pltpu.make_async_copy(src(k), dst(k, mine), loc_sems.at[k]) for k in range(nk)]
        local += [pltpu.make_async_copy(wins[b], wouts[b].at[mine], w_loc.at[b]) for b in range(nw)]
        for z, (a, _, row0) in enumerate(zero_fills):
            local.append(pltpu.make_async_copy(zins[z], outs[a].at[row0:row0 + zins[z].shape[0]], z_sems.at[z]))
        wsends = [wcopy(b, j, mine) for b in range(nw) for j in range(3)]
        for cp in local + wsends:
            cp.start()

        for half in (0, 1):
            @pl.when(c == half)
            def _(half=half):
                my_k = [k for k in range(nk) if flat[k][1][5] == half]
                other_k = [k for k in range(nk) if flat[k][1][5] != half]
                sends = [ici(k, j, mine) for k in my_k for j in range(3)]
                for cp in sends:
                    cp.start()
                passed = []
                for k in my_k:
                    for j in range(3):
                        ici(k, j, chip_of[j]).wait_recv()
                        cp = fwd(k, j)
                        cp.start()
                        passed.append(cp)
                for k in other_k:
                    for j in range(3):
                        fwd(k, j).wait_recv()
                for cp in sends + passed:
                    cp.wait_send()

        for b in range(nw):
            for j in range(3):
                wcopy(b, j, chip_of[j]).wait_recv()
        for cp in wsends:
            cp.wait_send()
        for cp in local:
            cp.wait()
        for b in range(nw):
            cols = wins[b].shape[-1]
            for q in range(4):
                wcat[b][..., cols * q:cols * (q + 1)] = wouts[b][q]

        @pl.when(mine == 0)
        def _():
            head_copy(0).wait_send()
            head_copy(1).start()
            head_copy(1).wait_send()

        @pl.when(mine != 0)
        def _():
            hands_on = mine == 2 - c
            head_copy(0).wait_recv()

            @pl.when(hands_on)
            def _():
                head_relay().start()

            head_pass(c).start()
            head_pass(1 - c).wait_recv()
            head_pass(c).wait_send()

            @pl.when(hands_on)
            def _():
                head_relay().wait_send()

    vmem = pl.BlockSpec(memory_space=pltpu.VMEM)
    dma = pltpu.SemaphoreType.DMA
    zeros = [z for _, z, _ in zero_fills]
    return pl.pallas_call(
        body, name="gather_weights",
        in_specs=[vmem] * (2 + ns + nw + nz), out_specs=[vmem] * (3 + ns + nw),
        out_shape=([jax.ShapeDtypeStruct((IN_HEAD, D_MODEL), BF16), jax.ShapeDtypeStruct(w_in_shard.shape, BF16),
                    jax.ShapeDtypeStruct(w_out_shard.shape, BF16)]
                   + [jax.ShapeDtypeStruct((out_rows[a], split[a].shape[1]), BF16) for a in range(ns)]
                   + [jax.ShapeDtypeStruct(w.shape[:-1] + (4 * w.shape[-1],), w.dtype) for w in whole]),
        scratch_shapes=[pltpu.VMEM(a.shape, BF16) for a in split] + [pltpu.VMEM((4,) + w.shape, w.dtype) for w in whole]
        + [dma((nk, 3)), dma((nk, 3)), dma((nk, 3)), dma((nk, 3)), dma((nk,)),
           dma((nw, 3)), dma((nw, 3)), dma((nw,)), dma((nz,)),
           dma((2,)), dma((1,)), dma((1,)), dma((2,))],
        compiler_params=pltpu.CompilerParams(vmem_limit_bytes=VMEM_LIMIT),
    )(w_in_shard, w_out_shard, *split, *whole, *zeros)


def _reduce_grads(parts, small):
    n = len(parts)
    ns = len(small)
    shapes = [a.shape[1:] for a in parts]
    halves = [(sh[0] // 2, sh[1]) for sh in shapes]
    sm_blocks = [a.shape[1] // 128 for a, _ in small]
    sm_first = [sum(nr * nblk for (_, nr), nblk in zip(small[:k], sm_blocks[:k])) for k in range(ns)]
    sm_rows = -(-(sm_first[-1] + small[-1][1] * sm_blocks[-1]) // 8) * 8
    sm_shape = (sm_rows, 128)

    def body(*refs):
        pin, sm_in = refs[:n], refs[n:n + ns]
        gout, sm_out = refs[n + ns:2 * n + ns], refs[2 * n + ns]
        scr = refs[2 * n + ns + 1:]
        own, sib, wire, rbuf = scr[:n], scr[n:2 * n], scr[2 * n:3 * n], scr[3 * n:4 * n]
        (sbuf, send_sems, recv_sems, loc_sems, pre_send, pre_recv, post_send, post_recv,
         sm_send, sm_recv, sm_pack) = scr[4 * n:]
        x, y, c = lax.axis_index("x"), lax.axis_index("y"), lax.axis_index("c")
        mine = 2 * x + y
        sm_pack[...] = jnp.zeros(sm_shape, F32)
        for k, (_, nr) in enumerate(small):
            for i in range(nr):
                for j in range(sm_blocks[k]):
                    row = sm_first[k] + i * sm_blocks[k] + j
                    sm_pack[row:row + 1, :] = sm_in[k][i:i + 1, 128 * j:128 * (j + 1)]
        me = 4 * x + 2 * y + c
        sibling = (x, y, 1 - c)

        def rows(a, half):
            r2 = halves[a][0]
            return pl.ds(pl.multiple_of(half * r2, r2), r2)

        near = (jnp.where(c == 0, 1 - x, x), jnp.where(c == 0, y, 1 - y))
        far = (jnp.where(c == 0, x, 1 - x), jnp.where(c == 0, 1 - y, y))
        chip = lambda p: 2 * p[0] + p[1]
        blocks = [3 - mine, chip(near), chip(far), mine]

        def pre(a, k):
            q = blocks[k]
            return pltpu.make_async_remote_copy(
                src_ref=pin[a].at[q, rows(a, 1 - c), :], dst_ref=sib[a].at[q],
                send_sem=pre_send.at[a, q], recv_sem=pre_recv.at[a, q], device_id=sibling,
                device_id_type=pl.DeviceIdType.MESH)

        def ici(a, m):
            px, py = near if m < 2 else far
            return pltpu.make_async_remote_copy(
                src_ref=wire[a].at[blocks[m]], dst_ref=rbuf[a].at[m], send_sem=send_sems.at[a, m],
                recv_sem=recv_sems.at[a, m], device_id=(px, py, c), device_id_type=pl.DeviceIdType.MESH)

        def post(a, half):
            ref = gout[a].at[rows(a, half), :]
            return pltpu.make_async_remote_copy(
                src_ref=ref, dst_ref=ref, send_sem=post_send.at[a], recv_sem=post_recv.at[a],
                device_id=sibling, device_id_type=pl.DeviceIdType.MESH)

        def small_copy(kk):
            peer = (x ^ (kk >> 2), y ^ ((kk >> 1) & 1), c ^ (kk & 1))
            return pltpu.make_async_remote_copy(
                src_ref=sm_pack, dst_ref=sbuf.at[kk], send_sem=sm_send.at[kk - 1], recv_sem=sm_recv.at[kk - 1],
                device_id=peer, device_id_type=pl.DeviceIdType.MESH)

        local = [[pltpu.make_async_copy(pin[a].at[blocks[k], rows(a, c), :], own[a].at[blocks[k]], loc_sems.at[a, k])
                  for k in range(4)] for a in range(n)]
        pres = [[pre(a, k) for k in range(4)] for a in range(n)]
        smalls = [small_copy(kk) for kk in range(1, 8)]
        for a in range(n):
            for k in range(4):
                local[a][k].start()
                pres[a][k].start()
        for cp in smalls:
            cp.start()
        sbuf[0] = sm_pack[...]
        sends = []
        for a in range(n):
            for k in range(4):
                local[a][k].wait()
                pres[a][k].wait_recv()
                tot = own[a][blocks[k]] + sib[a][blocks[k]]
                if k == 2:
                    ici(a, 0).wait_recv()
                    tot = tot + rbuf[a][0].astype(F32)
                own[a][blocks[k]] = tot
                if k < 3:
                    wire[a][blocks[k]] = tot.astype(BF16)
                    cp = ici(a, k)
                    cp.start()
                    sends.append(cp)
        for cp in smalls:
            cp.wait_recv()
        total = sbuf[me]
        for d in range(1, 8):
            total = total + sbuf[me ^ d]
        sm_out[...] = total
        posts = []
        for a in range(n):
            fin = own[a][mine]
            for m in (1, 2):
                ici(a, m).wait_recv()
                fin = fin + rbuf[a][m].astype(F32)
            gout[a][rows(a, c), :] = fin
            cp = post(a, c)
            cp.start()
            posts.append(cp)
        for a in range(n):
            post(a, 1 - c).wait_recv()
        for cp in [cp for row in pres for cp in row] + sends + smalls + posts:
            cp.wait_send()

    vmem = pl.BlockSpec(memory_space=pltpu.VMEM)
    dma = pltpu.SemaphoreType.DMA
    return pl.pallas_call(
        body, name="reduce_grads",
        in_specs=[pl.BlockSpec(memory_space=pl.ANY)] * n + [vmem] * ns, out_specs=[vmem] * (n + 1),
        out_shape=[jax.ShapeDtypeStruct(sh, F32) for sh in shapes] + [jax.ShapeDtypeStruct(sm_shape, F32)],
        scratch_shapes=([pltpu.VMEM((4,) + hs, F32) for hs in halves] + [pltpu.VMEM((4,) + hs, F32) for hs in halves]
                        + [pltpu.VMEM((4,) + hs, BF16) for hs in halves]
                        + [pltpu.VMEM((3,) + hs, BF16) for hs in halves]
                        + [pltpu.VMEM((8,) + sm_shape, F32), dma((n, 3)), dma((n, 3)), dma((n, 4)),
                           dma((n, 4)), dma((n, 4)), dma((n,)), dma((n,)), dma((7,)), dma((7,)),
                           pltpu.VMEM(sm_shape, F32)]),
        compiler_params=pltpu.CompilerParams(vmem_limit_bytes=VMEM_LIMIT),
    )(*parts, *[a for a, _ in small])


def _adamw_update(w_ref, g_ref, m_ref, v_ref, d_ref, nm_ref, nv_ref):
    gv = g_ref[...]
    nm = ADAM_B1 * m_ref[...] + (1.0 - ADAM_B1) * gv
    nv = ADAM_B2 * v_ref[...] + (1.0 - ADAM_B2) * (gv * gv)
    m_hat = nm / (1.0 - ADAM_B1 ** ADAM_STEP)
    v_hat = nv / (1.0 - ADAM_B2 ** ADAM_STEP)
    d_ref[...] = -ADAM_LR * (m_hat / (jnp.sqrt(v_hat) + ADAM_EPS) + ADAM_WD * w_ref[...])
    nm_ref[...] = nm
    nv_ref[...] = nv


def _adamw_small(ws, gs, ms, vs):
    k = len(ws)

    def body(*refs):
        ins, outs = refs[:4 * k], refs[4 * k:]
        for a in range(k):
            _adamw_update(ins[a], ins[k + a], ins[2 * k + a], ins[3 * k + a], outs[a], outs[k + a], outs[2 * k + a])

    out = pl.pallas_call(
        body, name="adamw_small",
        out_shape=[jax.ShapeDtypeStruct(w.shape, F32) for w in ws] * 3,
        compiler_params=pltpu.CompilerParams(vmem_limit_bytes=VMEM_LIMIT),
    )(*ws, *gs, *ms, *vs)
    return out[:k], out[k:2 * k], out[2 * k:]


def _adamw(w, g, m, v, name):
    shape = w.shape
    w2, g2, m2, v2 = (a.reshape((-1, shape[-1])) for a in (w, g, m, v))

    def body(w_ref, g_ref, m_ref, v_ref, d_ref, nm_ref, nv_ref):
        _adamw_update(w_ref, g_ref, m_ref, v_ref, d_ref, nm_ref, nv_ref)

    rows, cols = w2.shape
    nblk = cols // 256 if cols % 256 == 0 and rows >= 64 else 1
    blk = pl.BlockSpec((rows, cols // nblk), lambda j: (0, j))
    out = pl.pallas_call(
        body, name=name, grid=(nblk,), in_specs=[blk] * 4, out_specs=[blk] * 3,
        out_shape=[jax.ShapeDtypeStruct(w2.shape, F32)] * 3,
        compiler_params=_cparams("parallel"),
    )(w2, g2, m2, v2)
    return tuple(a.reshape(shape) for a in out)


def kernel(x, meta_tokens, norm_g, w_in, q_norm_g, w_q_up, kv_norm_g, w_kv_up, conv_w, attn_out_g, conv_out_g, w_out, final_norm_g, loss_target, m_meta_tokens, m_norm_g, m_w_in, m_q_norm_g, m_w_q_up, m_kv_norm_g, m_w_kv_up, m_conv_w, m_attn_out_g, m_conv_out_g, m_w_out, m_final_norm_g, v_meta_tokens, v_norm_g, v_w_in, v_q_norm_g, v_w_q_up, v_kv_norm_g, v_w_kv_up, v_conv_w, v_attn_out_g, v_conv_out_g, v_w_out, v_final_norm_g):
    nb, s, _ = x.shape
    tm = min(ROW_TILE, s)
    ta = min(ATTN_TILE, s)
    assert s % tm == 0 and s % ta == 0 and tm % 16 == 0
    r = nb * s

    tr = lambda a: jnp.transpose(a[0])
    w_head, w_in_shard, w_out_shard, wq_p, wkv_p, g_cw, meta_f = _gather_weights(
        tr(w_in), w_out[0], [tr(w_q_up), tr(w_kv_up)],
        [W_Q_PIECES, W_KV_PIECES], [HEADS * QK_PAD, 1024],
        [jnp.transpose(conv_w, (1, 0, 2)), meta_tokens],
        [(0, jnp.zeros((64, Q_RANK), BF16), QK_PAD * h + NOPE + ROPE) for h in range(HEADS)])
    conv_f = g_cw.reshape(3, CONV_W)

    c_all, sa_all, sb_all = _rope_tables(N_META + s)
    tabs_m = (c_all[:N_META], sa_all[:N_META], sb_all[:N_META])
    tabs = (c_all[N_META:], sa_all[N_META:], sb_all[N_META:])
    gid = np.arange(CONV_W) // CONV_GROUP
    gmat = jnp.asarray(np.where(gid[:, None] == gid[None, :], 1.0 / CONV_GROUP, 0.0), BF16)
    ga, gc = attn_out_g, conv_out_g
    gf = final_norm_g.reshape(1, D_MODEL)

    x2d = x.reshape(r, D_MODEL)
    tgt2d = loss_target.reshape(r, D_MODEL)

    ph, q, k, v, pmh, km, vm, w_out_f, w_in_part = _fwd_proj(
        x2d, meta_f, tabs, tabs_m, norm_g, w_head, q_norm_g, wq_p, kv_norm_g, wkv_p, w_out_shard, w_in_shard,
        nb, s, tm // 2)
    o, lse, w_in_p = _attn_fwd(q, k, v, km, vm, w_in_shard, w_in_part, nb, s, ta)
    dh2, dycat, dw_out, dgf, loss_acc, pt, pmt = _out_fwd_bwd(x2d, tgt2d, o, meta_f, norm_g, w_in_p, conv_f, ga, gc,
                                                              gmat, w_out_f, gf, nb, s, tm)
    dpb, do, delta, dccm, dga, dgc, dcw = _gate_bwd(dycat, o, pt, pmt, conv_f, ga, gc, gmat, nb, s, tm)
    p_out = dw_out.reshape(4, D_MODEL // 4, D_MODEL)
    dq, dk, dv, dkm, dvm, g_w_out = _attn_bwd(q, k, v, do, lse, delta, km, vm, [p_out], nb, s, ta)
    dpa, dpam, p_q, p_kv, dgq, dgkv = _up_bwd(dq, dk, dv, dkm, dvm, ph, pmh, tabs, tabs_m, wq_p, wkv_p,
                                              q_norm_g, kv_norm_g, nb, s, tm)
    gx, gmeta, p_in, dng = _in_bwd(x2d, dh2, dpa, dpb, meta_f, dpam, dccm, pmt, w_in_p, norm_g, nb, s, tm)

    g_w_in_t, g_w_q_t, g_w_kv_t, small_sum = _reduce_grads(
        [p_in, p_q, p_kv],
        [(dng, 1), (dgq, 1), (dgkv, 1), (dga, 1), (dgc, 1), (dgf, 1), (dcw, 3), (gmeta, N_META), (loss_acc, 1)])
    ssum = small_sum.reshape(-1)

    def take(off, n):
        return ssum[off:off + n], off + n

    off = 0
    g_norm, off = take(off, D_MODEL)
    g_qn, off = take(off, Q_RANK)
    g_kvn, off = take(off, KV_RANK)
    g_ga, off = take(off, CONV_W)
    g_gc, off = take(off, CONV_W)
    g_gf, off = take(off, D_MODEL)
    g_cw_all, off = take(off, 3 * CONV_W)
    g_meta_all, off = take(off, N_META * D_MODEL)
    loss = ssum[off]
    chip = 2 * lax.axis_index("x") + lax.axis_index("y")
    g_conv = lax.dynamic_slice(g_cw_all.reshape(3, CONV_W), (0, chip * 128), (3, 128))
    g_mt = lax.dynamic_slice(g_meta_all.reshape(N_META, D_MODEL), (0, chip * 256), (N_META, 256))

    grads = {
        "meta_tokens": g_mt, "norm_g": g_norm.reshape(1, -1), "w_in": g_w_in_t, "q_norm_g": g_qn.reshape(1, -1),
        "w_q_up": g_w_q_t, "kv_norm_g": g_kvn.reshape(1, -1), "w_kv_up": jnp.transpose(g_w_kv_t)[None],
        "conv_w": g_conv[None], "attn_out_g": g_ga.reshape(1, -1), "conv_out_g": g_gc.reshape(1, -1),
        "w_out": g_w_out[None], "final_norm_g": g_gf,
    }
    transposed = ("w_in", "w_q_up")
    weights = {
        "meta_tokens": (meta_tokens, m_meta_tokens, v_meta_tokens), "norm_g": (norm_g, m_norm_g, v_norm_g),
        "w_in": (w_in, m_w_in, v_w_in), "q_norm_g": (q_norm_g, m_q_norm_g, v_q_norm_g),
        "w_q_up": (w_q_up, m_w_q_up, v_w_q_up), "kv_norm_g": (kv_norm_g, m_kv_norm_g, v_kv_norm_g),
        "w_kv_up": (w_kv_up, m_w_kv_up, v_w_kv_up), "conv_w": (conv_w, m_conv_w, v_conv_w),
        "attn_out_g": (attn_out_g, m_attn_out_g, v_attn_out_g), "conv_out_g": (conv_out_g, m_conv_out_g, v_conv_out_g),
        "w_out": (w_out, m_w_out, v_w_out), "final_norm_g": (final_norm_g, m_final_norm_g, v_final_norm_g),
    }
    names = list(weights)
    small = [nme for nme in names if nme != "w_in"]

    def view(nme, a):
        if nme in transposed:
            return a if a.ndim == 2 else tr(a)
        if nme == "conv_w":
            return jnp.transpose(a.reshape(1, 3, -1), (1, 0, 2))
        if a.ndim == 3:
            return a[0]
        return a.reshape(1, -1) if a.ndim == 1 else a

    def unview(nme, a):
        if nme in transposed:
            return jnp.transpose(a)[None]
        if nme == "conv_w":
            return jnp.transpose(a, (1, 0, 2))
        return a.reshape(weights[nme][0].shape)

    res_small = _adamw_small(*[[view(nme, a) for nme, a in zip(small, col)] for col in (
        [weights[nme][0] for nme in small], [grads[nme] for nme in small],
        [weights[nme][1] for nme in small], [weights[nme][2] for nme in small])])
    w_, m_, v_ = weights["w_in"]
    res = _adamw(tr(w_), grads["w_in"], tr(m_), tr(v_), "adamw_w_in")
    upd = {"w_in": tuple(jnp.transpose(a)[None] for a in (grads["w_in"],) + res)}
    for j, nme in enumerate(small):
        upd[nme] = (unview(nme, view(nme, grads[nme])),) + tuple(unview(nme, r[j]) for r in res_small)
    grads = {nme: upd[nme][0] for nme in names}
    deltas, new_m, new_v = ([upd[nme][j] for nme in names] for j in (1, 2, 3))

    grad_x = gx.reshape(nb, s, D_MODEL)
    return (loss, grad_x, *[grads[nme] for nme in names], *deltas, *new_m, *new_v)
```
